```python
import jax, jax.numpy as jnp
from jax import lax
import numpy as np

D_MODEL = 1024
BATCH = 32
SEQ = 2048
DEPTH = 1

CHUNK = 64
Q_BLOCK = 128
ATT_HEADS = 16
HEAD_DIM = 64
D_ATT = ATT_HEADS * HEAD_DIM
D_RNN = D_MODEL
RNN_BLOCKS = 16
RNN_BLOCK_W = D_RNN // RNN_BLOCKS
CONV_W = 4
RG_C = 8.0
NORM_EPS = 1e-6
MASK_VALUE = -1e30
IN_WIDTHS = (D_ATT, D_ATT, D_ATT, ATT_HEADS, D_ATT, D_RNN, D_RNN, D_MODEL, D_MODEL)
IN_TOTAL = 5 * D_ATT + ATT_HEADS + 2 * D_RNN + 2 * D_MODEL

kernel_name = "hybrid_fox_rglru_gated_block"


def rms_norm(x, w):
    xf = x.astype(jnp.float32)
    y = xf * lax.rsqrt(jnp.mean(xf * xf, axis=-1, keepdims=True) + NORM_EPS)
    return (y * w.astype(jnp.float32)).astype(x.dtype)


def split_columns(z):
    parts = []
    start = 0
    for width in IN_WIDTHS:
        parts.append(z[..., start:start + width])
        start += width
    return parts


def forgetting_attention(q, k, v, log_f):
    seq = q.shape[1]
    scale = HEAD_DIM ** -0.5
    c = jnp.transpose(jnp.cumsum(log_f, axis=1), (0, 2, 1))
    outs = []
    for blk in range(seq // Q_BLOCK):
        q0 = blk * Q_BLOCK
        q1 = q0 + Q_BLOCK
        qb = q[:, q0:q1]
        kb = k[:, :q1]
        vb = v[:, :q1]
        s = jnp.einsum('bqhd,bkhd->bhqk', qb, kb).astype(jnp.float32) * scale
        s = s + c[:, :, q0:q1][:, :, :, None] - c[:, :, :q1][:, :, None, :]
        mask = (q0 + jnp.arange(Q_BLOCK))[:, None] >= jnp.arange(q1)[None, :]
        s = jnp.where(mask[None, None], s, MASK_VALUE)
        p = jax.nn.softmax(s, axis=-1)
        outs.append(jnp.einsum('bhqk,bkhd->bqhd', p.astype(vb.dtype), vb))
    return jnp.concatenate(outs, axis=1)


def causal_depthwise_conv(x, w, b):
    seq = x.shape[1]
    xp = jnp.pad(x, ((0, 0), (CONV_W - 1, 0), (0, 0)))
    y = b + w[0] * xp[:, 0:seq]
    for j in range(1, CONV_W):
        y = y + w[j] * xp[:, j:j + seq]
    return y


def rg_lru(x, r, i, lam):
    log_a = -RG_C * r.astype(jnp.float32) * jax.nn.softplus(-lam.astype(jnp.float32))
    a = jnp.exp(log_a)
    u = jnp.sqrt(jnp.maximum(-jnp.expm1(2.0 * log_a), 0.0)) * (
        i.astype(jnp.float32) * x.astype(jnp.float32))
    a_t = jnp.swapaxes(a, 0, 1)
    u_t = jnp.swapaxes(u, 0, 1)

    def step(h, inp):
        a_s, u_s = inp
        h = a_s * h + u_s
        return h, h

    h0 = jnp.zeros(a_t.shape[1:], jnp.float32)
    _, hs = lax.scan(step, h0, (a_t, u_t))
    return jnp.swapaxes(hs, 0, 1).astype(x.dtype)


def _fwd_setup_inputs(seed: int = 0) -> dict:
    key = jax.random.key(seed)
    ks = jax.random.split(key, 18)
    f32 = jnp.float32
    L = DEPTH

    def nrm(k, shape, fan_in):
        return jax.random.normal(k, shape, f32) * (fan_in ** -0.5)

    x = jax.random.normal(ks[0], (BATCH, SEQ, D_MODEL), f32)
    pre_norm_w = 1.0 + 0.05 * jax.random.normal(ks[1], (L, D_MODEL), f32)
    w_in = nrm(ks[2], (L, D_MODEL, IN_TOTAL), D_MODEL)
    b_in = 0.02 * jax.random.normal(ks[3], (L, IN_TOTAL), f32)
    conv_w = nrm(ks[4], (L, CONV_W, D_RNN), CONV_W)
    conv_b = 0.02 * jax.random.normal(ks[5], (L, D_RNN), f32)
    rg_wa = nrm(ks[6], (L, RNN_BLOCKS, RNN_BLOCK_W, RNN_BLOCK_W), RNN_BLOCK_W)
    rg_ba = 0.02 * jax.random.normal(ks[7], (L, D_RNN), f32)
    rg_wx = nrm(ks[8], (L, RNN_BLOCKS, RNN_BLOCK_W, RNN_BLOCK_W), RNN_BLOCK_W)
    rg_bx = 0.02 * jax.random.normal(ks[9], (L, D_RNN), f32)
    u = jax.random.uniform(ks[10], (L, D_RNN), f32, minval=0.9, maxval=0.999)
    a0 = u ** (1.0 / RG_C)
    rg_lambda = jnp.log(a0) - jnp.log1p(-a0)
    w_branch_a = nrm(ks[11], (L, D_ATT, D_MODEL), D_ATT)
    w_branch_r = nrm(ks[12], (L, D_RNN, D_MODEL), D_RNN)
    w_out = nrm(ks[13], (L, D_MODEL, D_MODEL), D_MODEL)
    post_norm_w = 1.0 + 0.05 * jax.random.normal(ks[14], (L, D_MODEL), f32)
    return {"x": x, "pre_norm_w": pre_norm_w, "w_in": w_in, "b_in": b_in,
            "conv_w": conv_w, "conv_b": conv_b, "rg_wa": rg_wa, "rg_ba": rg_ba,
            "rg_wx": rg_wx, "rg_bx": rg_bx, "rg_lambda": rg_lambda,
            "w_branch_a": w_branch_a, "w_branch_r": w_branch_r, "w_out": w_out,
            "post_norm_w": post_norm_w}


def _fwd_reference(x, pre_norm_w, w_in, b_in, conv_w, conv_b, rg_wa, rg_ba, rg_wx, rg_bx,
              rg_lambda, w_branch_a, w_branch_r, w_out, post_norm_w):
    bsz, seq, _ = x.shape
    for l in range(DEPTH):
        h = rms_norm(x, pre_norm_w[l])
        z = jnp.einsum('bsd,de->bse', h, w_in[l]) + b_in[l]
        q, k, v, f_logit, gate_a, x_r, gate_r, mg_a, mg_r = split_columns(z)

        q = q.reshape(bsz, seq, ATT_HEADS, HEAD_DIM)
        k = k.reshape(bsz, seq, ATT_HEADS, HEAD_DIM)
        v = v.reshape(bsz, seq, ATT_HEADS, HEAD_DIM)
        log_f = jax.nn.log_sigmoid(f_logit.astype(jnp.float32))
        y_a = forgetting_attention(q, k, v, log_f).reshape(bsz, seq, D_ATT)
        y_a = jnp.einsum('bsc,cd->bsd', y_a * jax.nn.silu(gate_a), w_branch_a[l])

        xc = causal_depthwise_conv(x_r, conv_w[l], conv_b[l])
        xb = xc.reshape(bsz, seq, RNN_BLOCKS, RNN_BLOCK_W)
        r = jax.nn.sigmoid(jnp.einsum('bsgi,gij->bsgj', xb, rg_wa[l]).reshape(bsz, seq, D_RNN) + rg_ba[l])
        i = jax.nn.sigmoid(jnp.einsum('bsgi,gij->bsgj', xb, rg_wx[l]).reshape(bsz, seq, D_RNN) + rg_bx[l])
        y_r = rg_lru(xc, r, i, rg_lambda[l])
        y_r = jnp.einsum('bsc,cd->bsd', y_r * jax.nn.silu(gate_r), w_branch_r[l])

        m = jax.nn.sigmoid(mg_a) * y_a + jax.nn.sigmoid(mg_r) * y_r
        o = jnp.einsum('bsd,de->bse', m, w_out[l])
        x = x + rms_norm(o, post_norm_w[l])
    return x


import jax as _jax
import jax.numpy as _jnp

TWIN_FORMAT = 'train_step'
FWD_PARAMS = ['x', 'pre_norm_w', 'w_in', 'b_in', 'conv_w', 'conv_b', 'rg_wa', 'rg_ba', 'rg_wx', 'rg_bx', 'rg_lambda', 'w_branch_a', 'w_branch_r', 'w_out', 'post_norm_w']
TWIN_WEIGHTS = ['pre_norm_w', 'w_in', 'b_in', 'conv_w', 'conv_b', 'rg_wa', 'rg_ba', 'rg_wx', 'rg_bx', 'rg_lambda', 'w_branch_a', 'w_branch_r', 'w_out', 'post_norm_w']
TWIN_DIFF_INPUT = 'x'
TWIN_INPUTS = ['x', 'pre_norm_w', 'w_in', 'b_in', 'conv_w', 'conv_b', 'rg_wa', 'rg_ba', 'rg_wx', 'rg_bx', 'rg_lambda', 'w_branch_a', 'w_branch_r', 'w_out', 'post_norm_w', 'loss_target', 'm_pre_norm_w', 'm_w_in', 'm_b_in', 'm_conv_w', 'm_conv_b', 'm_rg_wa', 'm_rg_ba', 'm_rg_wx', 'm_rg_bx', 'm_rg_lambda', 'm_w_branch_a', 'm_w_branch_r', 'm_w_out', 'm_post_norm_w', 'v_pre_norm_w', 'v_w_in', 'v_b_in', 'v_conv_w', 'v_conv_b', 'v_rg_wa', 'v_rg_ba', 'v_rg_wx', 'v_rg_bx', 'v_rg_lambda', 'v_w_branch_a', 'v_w_branch_r', 'v_w_out', 'v_post_norm_w']
TWIN_OUTPUTS = ['loss', 'grad_x', 'grad_pre_norm_w', 'grad_w_in', 'grad_b_in', 'grad_conv_w', 'grad_conv_b', 'grad_rg_wa', 'grad_rg_ba', 'grad_rg_wx', 'grad_rg_bx', 'grad_rg_lambda', 'grad_w_branch_a', 'grad_w_branch_r', 'grad_w_out', 'grad_post_norm_w', 'delta_pre_norm_w', 'delta_w_in', 'delta_b_in', 'delta_conv_w', 'delta_conv_b', 'delta_rg_wa', 'delta_rg_ba', 'delta_rg_wx', 'delta_rg_bx', 'delta_rg_lambda', 'delta_w_branch_a', 'delta_w_branch_r', 'delta_w_out', 'delta_post_norm_w', 'new_m_pre_norm_w', 'new_m_w_in', 'new_m_b_in', 'new_m_conv_w', 'new_m_conv_b', 'new_m_rg_wa', 'new_m_rg_ba', 'new_m_rg_wx', 'new_m_rg_bx', 'new_m_rg_lambda', 'new_m_w_branch_a', 'new_m_w_branch_r', 'new_m_w_out', 'new_m_post_norm_w', 'new_v_pre_norm_w', 'new_v_w_in', 'new_v_b_in', 'new_v_conv_w', 'new_v_conv_b', 'new_v_rg_wa', 'new_v_rg_ba', 'new_v_rg_wx', 'new_v_rg_bx', 'new_v_rg_lambda', 'new_v_w_branch_a', 'new_v_w_branch_r', 'new_v_w_out', 'new_v_post_norm_w']
TWIN_LEAF_KINDS = {'loss': 'loss', 'grad_x': 'grad_x', 'grad_pre_norm_w': 'grad_w', 'grad_w_in': 'grad_w', 'grad_b_in': 'grad_w', 'grad_conv_w': 'grad_w', 'grad_conv_b': 'grad_w', 'grad_rg_wa': 'grad_w', 'grad_rg_ba': 'grad_w', 'grad_rg_wx': 'grad_w', 'grad_rg_bx': 'grad_w', 'grad_rg_lambda': 'grad_w', 'grad_w_branch_a': 'grad_w', 'grad_w_branch_r': 'grad_w', 'grad_w_out': 'grad_w', 'grad_post_norm_w': 'grad_w', 'delta_pre_norm_w': 'delta_w', 'delta_w_in': 'delta_w', 'delta_b_in': 'delta_w', 'delta_conv_w': 'delta_w', 'delta_conv_b': 'delta_w', 'delta_rg_wa': 'delta_w', 'delta_rg_ba': 'delta_w', 'delta_rg_wx': 'delta_w', 'delta_rg_bx': 'delta_w', 'delta_rg_lambda': 'delta_w', 'delta_w_branch_a': 'delta_w', 'delta_w_branch_r': 'delta_w', 'delta_w_out': 'delta_w', 'delta_post_norm_w': 'delta_w', 'new_m_pre_norm_w': 'new_m', 'new_m_w_in': 'new_m', 'new_m_b_in': 'new_m', 'new_m_conv_w': 'new_m', 'new_m_conv_b': 'new_m', 'new_m_rg_wa': 'new_m', 'new_m_rg_ba': 'new_m', 'new_m_rg_wx': 'new_m', 'new_m_rg_bx': 'new_m', 'new_m_rg_lambda': 'new_m', 'new_m_w_branch_a': 'new_m', 'new_m_w_branch_r': 'new_m', 'new_m_w_out': 'new_m', 'new_m_post_norm_w': 'new_m', 'new_v_pre_norm_w': 'new_v', 'new_v_w_in': 'new_v', 'new_v_b_in': 'new_v', 'new_v_conv_w': 'new_v', 'new_v_conv_b': 'new_v', 'new_v_rg_wa': 'new_v', 'new_v_rg_ba': 'new_v', 'new_v_rg_wx': 'new_v', 'new_v_rg_bx': 'new_v', 'new_v_rg_lambda': 'new_v', 'new_v_w_branch_a': 'new_v', 'new_v_w_branch_r': 'new_v', 'new_v_w_out': 'new_v', 'new_v_post_norm_w': 'new_v'}


def _forward(args):
    return _fwd_reference(*[args[k] for k in FWD_PARAMS])


def _output_shape():
    out = _jax.eval_shape(lambda: _forward(_fwd_setup_inputs(0)))
    return out.shape, out.dtype

N_MICROBATCH = 1
ADAM_LR = 0.001
ADAM_B1 = 0.9
ADAM_B2 = 0.999
ADAM_EPS = 1e-08
ADAM_WD = 0.01
ADAM_STEP = 10
PER_EXAMPLE_BATCH_AXIS = {'x': 0, 'loss_target': 0}
SHARED_INPUTS = []
_WEIGHT_DTYPES = {'pre_norm_w': _jnp.float32, 'w_in': _jnp.float32, 'b_in': _jnp.float32, 'conv_w': _jnp.float32, 'conv_b': _jnp.float32, 'rg_wa': _jnp.float32, 'rg_ba': _jnp.float32, 'rg_wx': _jnp.float32, 'rg_bx': _jnp.float32, 'rg_lambda': _jnp.float32, 'w_branch_a': _jnp.float32, 'w_branch_r': _jnp.float32, 'w_out': _jnp.float32, 'post_norm_w': _jnp.float32}
MOMENT_SCALE = {'pre_norm_w': 6.455216e-01, 'w_in': 2.036283e-01, 'b_in': 2.182810e+00, 'conv_w': 3.801754e-01, 'conv_b': 6.859149e+00, 'rg_wa': 2.123741e-01, 'rg_ba': 1.660408e-01, 'rg_wx': 3.985717e-01, 'rg_bx': 9.487701e-02, 'rg_lambda': 2.534209e-01, 'w_branch_a': 2.726500e-01, 'w_branch_r': 5.020197e-01, 'w_out': 5.125339e-01, 'post_norm_w': 6.423831e+01}


def _to_microbatches(a, axis):
    t = _jnp.moveaxis(a, axis, 0)
    t = t.reshape((N_MICROBATCH, t.shape[0] // N_MICROBATCH) + t.shape[1:])
    return _jnp.moveaxis(t, 1, axis + 1)


def setup_inputs(seed: int = 0) -> dict:
    inp = _fwd_setup_inputs(seed)
    key = _jax.random.fold_in(_jax.random.key(seed), 7919)
    shape, _ = _output_shape()
    out = dict(inp)
    out["loss_target"] = _jax.random.normal(_jax.random.fold_in(key, 0), shape, _jnp.float32)
    for i, name in enumerate(TWIN_WEIGHTS):
        w = inp[name].astype(_jnp.float32)
        if MOMENT_SCALE is None:
            s = _jnp.sqrt(_jnp.mean(_jnp.square(w)) + 1e-30)
        else:
            s = MOMENT_SCALE[name]
        km, kv = _jax.random.split(_jax.random.fold_in(key, i + 1))
        out[name] = w
        out["m_" + name] = s * _jax.random.normal(km, w.shape, _jnp.float32)
        out["v_" + name] = (s * s) * _jax.random.uniform(kv, w.shape, _jnp.float32, 0.5, 1.5)
    if N_MICROBATCH > 1:
        for name, axis in PER_EXAMPLE_BATCH_AXIS.items():
            out[name] = _to_microbatches(out[name], axis)
    return {'x': out['x'], 'pre_norm_w': out['pre_norm_w'], 'w_in': out['w_in'], 'b_in': out['b_in'], 'conv_w': out['conv_w'], 'conv_b': out['conv_b'], 'rg_wa': out['rg_wa'], 'rg_ba': out['rg_ba'], 'rg_wx': out['rg_wx'], 'rg_bx': out['rg_bx'], 'rg_lambda': out['rg_lambda'], 'w_branch_a': out['w_branch_a'], 'w_branch_r': out['w_branch_r'], 'w_out': out['w_out'], 'post_norm_w': out['post_norm_w'], 'loss_target': out['loss_target'], 'm_pre_norm_w': out['m_pre_norm_w'], 'm_w_in': out['m_w_in'], 'm_b_in': out['m_b_in'], 'm_conv_w': out['m_conv_w'], 'm_conv_b': out['m_conv_b'], 'm_rg_wa': out['m_rg_wa'], 'm_rg_ba': out['m_rg_ba'], 'm_rg_wx': out['m_rg_wx'], 'm_rg_bx': out['m_rg_bx'], 'm_rg_lambda': out['m_rg_lambda'], 'm_w_branch_a': out['m_w_branch_a'], 'm_w_branch_r': out['m_w_branch_r'], 'm_w_out': out['m_w_out'], 'm_post_norm_w': out['m_post_norm_w'], 'v_pre_norm_w': out['v_pre_norm_w'], 'v_w_in': out['v_w_in'], 'v_b_in': out['v_b_in'], 'v_conv_w': out['v_conv_w'], 'v_conv_b': out['v_conv_b'], 'v_rg_wa': out['v_rg_wa'], 'v_rg_ba': out['v_rg_ba'], 'v_rg_wx': out['v_rg_wx'], 'v_rg_bx': out['v_rg_bx'], 'v_rg_lambda': out['v_rg_lambda'], 'v_w_branch_a': out['v_w_branch_a'], 'v_w_branch_r': out['v_w_branch_r'], 'v_w_out': out['v_w_out'], 'v_post_norm_w': out['v_post_norm_w']}


def _loss(weights, diff, rest, loss_target):
    with _jax.named_scope("forward"):
        args = {**rest, TWIN_DIFF_INPUT: diff, **{k: w.astype(_WEIGHT_DTYPES[k]) for k, w in weights.items()}}
        y = _forward(args)
    with _jax.named_scope("loss_head"):
        err = _jnp.square(y.astype(_jnp.float32) - loss_target)
        return 0.5 * _jnp.sum(_jnp.mean(err, axis=-1)) if err.ndim else 0.5 * err


def _adamw(w, g, m, v):
    m = ADAM_B1 * m + (1.0 - ADAM_B1) * g
    v = ADAM_B2 * v + (1.0 - ADAM_B2) * _jnp.square(g)
    m_hat = m / (1.0 - ADAM_B1 ** ADAM_STEP)
    v_hat = v / (1.0 - ADAM_B2 ** ADAM_STEP)
    delta = -ADAM_LR * (m_hat / (_jnp.sqrt(v_hat) + ADAM_EPS) + ADAM_WD * w)
    return delta, m, v


def reference(x, pre_norm_w, w_in, b_in, conv_w, conv_b, rg_wa, rg_ba, rg_wx, rg_bx, rg_lambda, w_branch_a, w_branch_r, w_out, post_norm_w, loss_target, m_pre_norm_w, m_w_in, m_b_in, m_conv_w, m_conv_b, m_rg_wa, m_rg_ba, m_rg_wx, m_rg_bx, m_rg_lambda, m_w_branch_a, m_w_branch_r, m_w_out, m_post_norm_w, v_pre_norm_w, v_w_in, v_b_in, v_conv_w, v_conv_b, v_rg_wa, v_rg_ba, v_rg_wx, v_rg_bx, v_rg_lambda, v_w_branch_a, v_w_branch_r, v_w_out, v_post_norm_w):
    given = dict(x=x, pre_norm_w=pre_norm_w, w_in=w_in, b_in=b_in, conv_w=conv_w, conv_b=conv_b, rg_wa=rg_wa, rg_ba=rg_ba, rg_wx=rg_wx, rg_bx=rg_bx, rg_lambda=rg_lambda, w_branch_a=w_branch_a, w_branch_r=w_branch_r, w_out=w_out, post_norm_w=post_norm_w, loss_target=loss_target, m_pre_norm_w=m_pre_norm_w, m_w_in=m_w_in, m_b_in=m_b_in, m_conv_w=m_conv_w, m_conv_b=m_conv_b, m_rg_wa=m_rg_wa, m_rg_ba=m_rg_ba, m_rg_wx=m_rg_wx, m_rg_bx=m_rg_bx, m_rg_lambda=m_rg_lambda, m_w_branch_a=m_w_branch_a, m_w_branch_r=m_w_branch_r, m_w_out=m_w_out, m_post_norm_w=m_post_norm_w, v_pre_norm_w=v_pre_norm_w, v_w_in=v_w_in, v_b_in=v_b_in, v_conv_w=v_conv_w, v_conv_b=v_conv_b, v_rg_wa=v_rg_wa, v_rg_ba=v_rg_ba, v_rg_wx=v_rg_wx, v_rg_bx=v_rg_bx, v_rg_lambda=v_rg_lambda, v_w_branch_a=v_w_branch_a, v_w_branch_r=v_w_branch_r, v_w_out=v_w_out, v_post_norm_w=v_post_norm_w)
    weights = {n: given[n] for n in TWIN_WEIGHTS}
    shared = {n: given[n] for n in SHARED_INPUTS}
    per_example = {n: given[n] for n in ['x']}
    grad_fn = _jax.value_and_grad(_loss, argnums=(0, 1))

    def one_microbatch(ex, loss_target):
        ex = dict(ex)
        diff = ex.pop(TWIN_DIFF_INPUT)
        return grad_fn(weights, diff, {**shared, **ex}, loss_target)

    if N_MICROBATCH == 1:
        loss, (grad_w, grad_x) = one_microbatch(per_example, given["loss_target"])
    else:
        def body(carry, xs):
            loss_sum, grad_sum = carry
            l_k, (gw_k, gx_k) = one_microbatch(xs[0], xs[1])
            with _jax.named_scope("update"):
                return (loss_sum + l_k, _jax.tree.map(_jnp.add, grad_sum, gw_k)), gx_k

        init = (_jnp.zeros((), _jnp.float32), _jax.tree.map(_jnp.zeros_like, weights))
        (loss, grad_w), grad_x = _jax.lax.scan(body, init, (per_example, given["loss_target"]))
    with _jax.named_scope("update"):
        delta_w, new_m, new_v = {}, {}, {}
        for n in TWIN_WEIGHTS:
            delta_w[n], new_m[n], new_v[n] = _adamw(weights[n], grad_w[n], given["m_" + n], given["v_" + n])
    return (loss, grad_x, *[grad_w[n] for n in TWIN_WEIGHTS], *[delta_w[n] for n in TWIN_WEIGHTS],
            *[new_m[n] for n in TWIN_WEIGHTS], *[new_v[n] for n in TWIN_WEIGHTS])
```

```python
import jax
import jax.numpy as jnp
from jax import lax
from jax.experimental import pallas as pl
from jax.experimental.pallas import tpu as pltpu

f32 = jnp.float32
bf16 = jnp.bfloat16

D = 1024
HEADS = 16
HEAD_PAIRS = 8
LANES = 128
NORM_EPS = 1e-6
MASK_VALUE = -1e30
RG_C = 8.0
QK_SCALE = 0.125
TQ = 256
TL = 256
TM = 256
IN_USED = 8 * D + HEADS
IN_TOTAL = 9 * D + HEADS
N_CHIPS = 4
N_DEV = 8
ADAM_LR, ADAM_B1, ADAM_B2, ADAM_EPS, ADAM_WD, ADAM_STEP = 0.001, 0.9, 0.999, 1e-08, 0.01, 10
VMEM_LIMIT = 56 * 1024 * 1024
MESH = pl.DeviceIdType.MESH


def _dot(a, b):
    return jnp.dot(a, b, preferred_element_type=f32)


def _dot_nt(a, b):
    return lax.dot_general(a, b, (((1,), (1,)), ((), ())), preferred_element_type=f32)


def _dot_tn(a, b):
    return lax.dot_general(a, b, (((0,), (0,)), ((), ())), preferred_element_type=f32)


def _sig(x):
    return 1.0 / (1.0 + jnp.exp(-x))


def _softplus(x):
    return jnp.maximum(x, 0.0) + jnp.log(1.0 + jnp.exp(-jnp.abs(x)))


def _params(sem, vmem=None):
    return pltpu.CompilerParams(dimension_semantics=sem, vmem_limit_bytes=vmem)


def _tile(tm, width, cb=0):
    return pl.BlockSpec((tm, width), lambda i, cb=cb: (i, cb))


def _whole(shape):
    nd = len(shape)
    return pl.BlockSpec(shape, lambda *_: (0,) * nd)


def _prenorm(x, w_pre):
    t = x.shape[0]

    def body(x_ref, w_ref, h_ref):
        xv = x_ref[...]
        r = lax.rsqrt(jnp.mean(xv * xv, axis=-1, keepdims=True) + NORM_EPS)
        h_ref[...] = (xv * r * w_ref[...]).astype(bf16)

    return pl.pallas_call(
        body, name="prenorm", grid=(t // TM,),
        in_specs=[_tile(TM, D), _whole((1, D))], out_specs=_tile(TM, D),
        out_shape=jax.ShapeDtypeStruct((t, D), bf16),
        compiler_params=_params(("parallel",)),
    )(x, w_pre)


def _mm(name, ins, prologue, w, bias, out_dtype, tm, tn):
    t = ins[0][0].shape[0]
    tm = min(tm, t)
    k, n = w.shape
    n_in = len(ins)

    def body(*refs):
        a = prologue(*[r[...] for r in refs[:n_in]])
        acc = _dot(a, refs[n_in][...])
        if bias is not None:
            acc = acc + refs[n_in + 1][...]
        refs[-1][...] = acc.astype(out_dtype)

    in_specs = [pl.BlockSpec((tm, k), lambda i, j, cb=cb: (i, cb)) for _, cb in ins]
    in_specs.append(pl.BlockSpec((k, tn), lambda i, j: (0, j)))
    args = [a for a, _ in ins] + [w]
    if bias is not None:
        in_specs.append(pl.BlockSpec((1, tn), lambda i, j: (0, j)))
        args.append(bias)
    return pl.pallas_call(
        body, name=name, grid=(t // tm, n // tn), in_specs=in_specs,
        out_specs=pl.BlockSpec((tm, tn), lambda i, j: (i, j)),
        out_shape=jax.ShapeDtypeStruct((t, n), out_dtype),
        compiler_params=_params(("parallel", "parallel"), VMEM_LIMIT),
    )(*args)


def _forget_prep(f128, seq):
    t = f128.shape[0]
    nb = seq // LANES

    def body(f_ref, c_ref, ct_ref):
        r = lax.broadcasted_iota(jnp.int32, (LANES, LANES), 0)
        cidx = lax.broadcasted_iota(jnp.int32, (LANES, LANES), 1)
        tri = (r >= cidx).astype(f32)
        carry = jnp.zeros((1, LANES), f32)
        for blk in range(nb):
            fv = f_ref[pl.ds(blk * LANES, LANES), :]
            lf = -_softplus(-fv)
            cb = jnp.dot(tri, lf, preferred_element_type=f32, precision=lax.Precision.HIGHEST) + carry
            c_ref[pl.ds(blk * LANES, LANES), :] = cb
            ct_ref[blk // 2, :, pl.ds((blk % 2) * LANES, LANES)] = cb.T
            carry = carry + jnp.sum(lf, axis=0, keepdims=True)

    return pl.pallas_call(
        body, name="forget_prep", grid=(t // seq,),
        in_specs=[pl.BlockSpec((seq, LANES), lambda b: (b, 0))],
        out_specs=[pl.BlockSpec((seq, LANES), lambda b: (b, 0)),
                   pl.BlockSpec((seq // TQ, LANES, TQ), lambda b: (b, 0, 0))],
        out_shape=[jax.ShapeDtypeStruct((t, LANES), f32), jax.ShapeDtypeStruct((t // TQ, LANES, TQ), f32)],
        compiler_params=_params(("parallel",)),
    )(f128)


def _attn_fwd(qkv, rest, c, ct, seq):
    t = qkv.shape[0]
    nb, nq = t // seq, seq // TQ

    def body(q_ref, k_ref, v_ref, c_ref, ct_ref, ga_ref, o_ref, pa_ref, lse_ref):
        qi, hp = pl.program_id(1), pl.program_id(2)
        lane = lax.broadcasted_iota(jnp.int32, (1, LANES), 1)
        rows = lax.broadcasted_iota(jnp.int32, (TQ, TQ), 0)
        cols = lax.broadcasted_iota(jnp.int32, (TQ, TQ), 1)
        q2 = q_ref[...]
        cq_all = c_ref[...]
        outs, lses = [], []
        for hh in range(2):
            head = 2 * hp + hh
            lmask = (lane // 64) == hh
            qh = jnp.where(lmask, q2, jnp.zeros_like(q2)) * jnp.asarray(QK_SCALE, bf16)
            cq = jnp.sum(jnp.where(lane == head, cq_all, 0.0), axis=1, keepdims=True)

            def kv_step(kt, carry, masked, qh=qh, cq=cq, head=head):
                m, l, acc = carry
                ks = pl.multiple_of(kt * TQ, TQ)
                k2 = k_ref[pl.ds(ks, TQ), :]
                v2 = v_ref[pl.ds(ks, TQ), :]
                ck = ct_ref[kt, pl.ds(head, 1), :]
                s = _dot_nt(qh, k2) + cq - ck
                if masked:
                    s = jnp.where(rows >= cols, s, MASK_VALUE)
                m_new = jnp.maximum(m, jnp.max(s, axis=1, keepdims=True))
                p = jnp.exp(s - m_new)
                alpha = jnp.exp(m - m_new)
                l = alpha * l + jnp.sum(p, axis=1, keepdims=True)
                acc = alpha * acc + _dot(p.astype(bf16), v2)
                return m_new, l, acc

            init = (jnp.full((TQ, 1), MASK_VALUE, f32), jnp.zeros((TQ, 1), f32), jnp.zeros((TQ, LANES), f32))
            carry = lax.fori_loop(0, qi, lambda kt, cr: kv_step(kt, cr, False), init)
            m, l, acc = kv_step(qi, carry, True)
            outs.append(acc / l)
            lses.append(m + jnp.log(l))
        o2 = jnp.where((lane // 64) == 0, outs[0], outs[1])
        o_ref[...] = o2
        ga = ga_ref[...]
        pa_ref[...] = (o2 * (ga * _sig(ga))).astype(bf16)

        @pl.when(hp == 0)
        def _():
            lse_ref[...] = jnp.zeros_like(lse_ref)

        lse_ref[...] = jnp.where(lane == 2 * hp, lses[0], jnp.where(lane == 2 * hp + 1, lses[1], lse_ref[...]))

    qspec = lambda off: pl.BlockSpec((TQ, LANES), lambda b, qi, hp: (b * nq + qi, off + hp))
    kvspec = lambda off: pl.BlockSpec((seq, LANES), lambda b, qi, hp: (b, off + hp))
    rowspec = pl.BlockSpec((TQ, LANES), lambda b, qi, hp: (b * nq + qi, 0))
    return pl.pallas_call(
        body, name="attn_fwd", grid=(nb, nq, HEAD_PAIRS),
        in_specs=[qspec(0), kvspec(HEAD_PAIRS), kvspec(2 * HEAD_PAIRS), rowspec,
                  pl.BlockSpec((nq, LANES, TQ), lambda b, qi, hp: (b, 0, 0)), qspec(0)],
        out_specs=[qspec(0), qspec(0), rowspec],
        out_shape=[jax.ShapeDtypeStruct((t, D), f32), jax.ShapeDtypeStruct((t, D), bf16),
                   jax.ShapeDtypeStruct((t, LANES), f32)],
        compiler_params=_params(("parallel", "parallel", "arbitrary")),
    )(qkv, qkv, qkv, c, ct, rest)


def _shifted_rows(x, top8, prev8, shift, row, row8):
    body = pltpu.roll(x, shift, 0)
    head = jnp.where(row8 < shift, pltpu.roll(prev8, shift, 0), pltpu.roll(top8, shift, 0))
    return body, head


def _rnn_gates(xc, wa_ref, wx_ref, ba_ref, bx_ref, lam_ref):
    xcb = xc.astype(bf16)
    r = _sig(_dot(xcb, wa_ref[...]) + ba_ref[...])
    i = _sig(_dot(xcb, wx_ref[...]) + bx_ref[...])
    sp = _softplus(-lam_ref[...])
    log_a = (-RG_C) * r * sp
    th = jnp.tanh(log_a)
    w1 = (-2.0) * th / (1.0 - th)
    sq = jnp.sqrt(jnp.maximum(w1, 0.0))
    return r, i, sp, log_a, w1, sq


def _conv_tile(x_ref, xprev_ref, has_prev, cw_ref, cb_ref, xc_ref):
    row = lax.broadcasted_iota(jnp.int32, (TL, D), 0)
    row8 = lax.broadcasted_iota(jnp.int32, (8, D), 0)
    x = x_ref[...]
    top8 = x_ref[pl.ds(0, 8), :]
    prev8 = jnp.where(has_prev, xprev_ref[...], 0.0)
    xc = cb_ref[...] + cw_ref[pl.ds(3, 1), :] * x
    xc8 = cb_ref[...] + cw_ref[pl.ds(3, 1), :] * top8
    for sh in range(1, 4):
        w = cw_ref[pl.ds(3 - sh, 1), :]
        xs, xs8 = _shifted_rows(x, top8, prev8, sh, row, row8)
        xc = xc + w * xs
        xc8 = xc8 + w * xs8
    xc_ref[...] = xc
    xc_ref[pl.ds(0, 8), :] = xc8


def _rnn_fwd(rest, conv_w, conv_b, wa_d, wx_d, ba, bx, lam, seq):
    t = rest.shape[0]
    nb, nt = t // seq, seq // TL

    def body(x_ref, xprev_ref, gr_ref, cw_ref, cb_ref, wa_ref, wx_ref, ba_ref, bx_ref, lam_ref,
             xc_ref, a_ref, h_ref, pr_ref, u_scr, carry):
        tt = pl.program_id(1)
        _conv_tile(x_ref, xprev_ref, tt > 0, cw_ref, cb_ref, xc_ref)
        xc = xc_ref[...]
        r, i, sp, log_a, w1, sq = _rnn_gates(xc, wa_ref, wx_ref, ba_ref, bx_ref, lam_ref)
        a_ref[...] = jnp.exp(log_a)
        u_scr[...] = sq * (i * xc)

        @pl.when(tt == 0)
        def _():
            carry[...] = jnp.zeros_like(carry)

        def step(s, h):
            h = a_ref[pl.ds(s, 1), :] * h + u_scr[pl.ds(s, 1), :]
            h_ref[pl.ds(s, 1), :] = h
            return h

        carry[...] = lax.fori_loop(0, TL, step, carry[...], unroll=8)
        gr = gr_ref[...]
        pr_ref[...] = (h_ref[...] * (gr * _sig(gr))).astype(bf16)

    tile = lambda cb: pl.BlockSpec((TL, D), lambda b, tt, cb=cb: (b * nt + tt, cb))
    prev = lambda cb: pl.BlockSpec((8, D), lambda b, tt, cb=cb: (jnp.maximum((b * nt + tt) * (TL // 8) - 1, 0), cb))
    vec = _whole((1, D))
    return pl.pallas_call(
        body, name="rnn_fwd", grid=(nb, nt),
        in_specs=[tile(1), prev(1), tile(2), _whole((4, D)), vec, _whole((D, D)), _whole((D, D)), vec, vec, vec],
        out_specs=[tile(0)] * 4,
        out_shape=[jax.ShapeDtypeStruct((t, D), f32)] * 3 + [jax.ShapeDtypeStruct((t, D), bf16)],
        scratch_shapes=[pltpu.VMEM((TL, D), f32), pltpu.VMEM((1, D), f32)],
        compiler_params=_params(("parallel", "arbitrary"), VMEM_LIMIT),
    )(rest, rest, rest, conv_w, conv_b, wa_d, wx_d, ba, bx, lam)


def _merge(mga, mgr, ya, yr):
    return (_sig(mga) * ya + _sig(mgr) * yr).astype(bf16)


def _post_loss(o, x, tgt, w_post):
    t = o.shape[0]

    def body(o_ref, x_ref, t_ref, w_ref, do_ref, dy_ref, loss_ref, dwp_ref):
        @pl.when(pl.program_id(0) == 0)
        def _():
            loss_ref[...] = jnp.zeros_like(loss_ref)
            dwp_ref[...] = jnp.zeros_like(dwp_ref)

        ov = o_ref[...]
        w = w_ref[...]
        r2 = lax.rsqrt(jnp.mean(ov * ov, axis=-1, keepdims=True) + NORM_EPS)
        oh = ov * r2
        e = x_ref[...] + oh * w - t_ref[...]
        loss_ref[...] += 0.5 * jnp.sum(jnp.mean(e * e, axis=-1, keepdims=True))
        dy = e * (1.0 / D)
        dy_ref[...] = dy
        dwp_ref[...] += jnp.sum(dy * oh, axis=0, keepdims=True)
        doh = dy * w
        do_ref[...] = (r2 * (doh - oh * jnp.mean(doh * oh, axis=-1, keepdims=True))).astype(bf16)

    return pl.pallas_call(
        body, name="post_loss", grid=(t // TM,),
        in_specs=[_tile(TM, D)] * 3 + [_whole((1, D))],
        out_specs=[_tile(TM, D), _tile(TM, D), _whole((8, LANES)), _whole((1, D))],
        out_shape=[jax.ShapeDtypeStruct((t, D), bf16), jax.ShapeDtypeStruct((t, D), f32),
                   jax.ShapeDtypeStruct((8, LANES), f32), jax.ShapeDtypeStruct((1, D), f32)],
        compiler_params=_params(("arbitrary",)),
    )(o, x, tgt, w_post)


def _out_bwd(do, rest, ya, yr, w_out):
    t = do.shape[0]

    def body(do_ref, mga_ref, mgr_ref, ya_ref, yr_ref, w_ref, dya_ref, dyr_ref, dmga_ref, dmgr_ref, dw_ref):
        @pl.when(pl.program_id(0) == 0)
        def _():
            dw_ref[...] = jnp.zeros_like(dw_ref)

        dov = do_ref[...]
        sa, sr = _sig(mga_ref[...]), _sig(mgr_ref[...])
        ya, yr = ya_ref[...], yr_ref[...]
        dw_ref[...] += _dot_tn((sa * ya + sr * yr).astype(bf16), dov)
        dm = _dot_nt(dov, w_ref[...])
        dya_ref[...] = (dm * sa).astype(bf16)
        dyr_ref[...] = (dm * sr).astype(bf16)
        dmga_ref[...] = (dm * ya * sa * (1.0 - sa)).astype(bf16)
        dmgr_ref[...] = (dm * yr * sr * (1.0 - sr)).astype(bf16)

    return pl.pallas_call(
        body, name="out_bwd", grid=(t // TM,),
        in_specs=[_tile(TM, D), _tile(TM, D, 3), _tile(TM, D, 4), _tile(TM, D), _tile(TM, D), _whole((D, D))],
        out_specs=[_tile(TM, D)] * 4 + [_whole((D, D))],
        out_shape=[jax.ShapeDtypeStruct((t, D), bf16)] * 4 + [jax.ShapeDtypeStruct((D, D), f32)],
        compiler_params=_params(("arbitrary",), VMEM_LIMIT),
    )(do, rest, rest, ya, yr, w_out)


def _branch_bwd(name, dyb, rest, gate_cb, act, pact, w, act_grad_dtype):
    t = dyb.shape[0]

    def body(dy_ref, g_ref, act_ref, pact_ref, w_ref, dact_ref, dg_ref, dw_ref):
        @pl.when(pl.program_id(0) == 0)
        def _():
            dw_ref[...] = jnp.zeros_like(dw_ref)

        dyv = dy_ref[...]
        dw_ref[...] += _dot_tn(pact_ref[...], dyv)
        dp = _dot_nt(dyv, w_ref[...])
        g = g_ref[...]
        sg = _sig(g)
        dact_ref[...] = (dp * (g * sg)).astype(act_grad_dtype)
        dg_ref[...] = (dp * act_ref[...] * (sg * (1.0 + g * (1.0 - sg)))).astype(bf16)

    return pl.pallas_call(
        body, name=name, grid=(t // TM,),
        in_specs=[_tile(TM, D), _tile(TM, D, gate_cb), _tile(TM, D), _tile(TM, D), _whole((D, D))],
        out_specs=[_tile(TM, D), _tile(TM, D), _whole((D, D))],
        out_shape=[jax.ShapeDtypeStruct((t, D), act_grad_dtype), jax.ShapeDtypeStruct((t, D), bf16),
                   jax.ShapeDtypeStruct((D, D), f32)],
        compiler_params=_params(("arbitrary",), VMEM_LIMIT),
    )(dyb, rest, act, pact, w)


def _rnn_bwd(dh, a, h, xc, rest, conv_w, conv_b, wa_d, wx_d, ba, bx, lam, seq):
    t = dh.shape[0]
    nb, nt = t // seq, seq // TL

    def body(dh_ref, a_ref, h_ref, hprev_ref, xc_ref, x_ref, xprev_ref, cw_ref, cb_ref, wa_ref, wx_ref,
             ba_ref, bx_ref, lam_ref, dxr_ref, dwa_ref, dwx_ref, vec_ref, g_scr, dxc_scr, dxr_scr, qcarry, dxc_next):
        b, tt = pl.program_id(0), pl.program_id(1)
        rt = nt - 1 - tt

        @pl.when((b == 0) & (tt == 0))
        def _():
            dwa_ref[...] = jnp.zeros_like(dwa_ref)
            dwx_ref[...] = jnp.zeros_like(dwx_ref)
            vec_ref[...] = jnp.zeros_like(vec_ref)

        @pl.when(tt == 0)
        def _():
            qcarry[...] = jnp.zeros_like(qcarry)
            dxc_next[...] = jnp.zeros_like(dxc_next)

        def step(k, q):
            s = TL - 1 - k
            g = dh_ref[pl.ds(s, 1), :] + q
            g_scr[pl.ds(s, 1), :] = g
            return a_ref[pl.ds(s, 1), :] * g

        qcarry[...] = lax.fori_loop(0, TL, step, qcarry[...], unroll=8)

        row = lax.broadcasted_iota(jnp.int32, (TL, D), 0)
        row8 = lax.broadcasted_iota(jnp.int32, (8, D), 0)
        g = g_scr[...]
        av = a_ref[...]
        xc = xc_ref[...]
        hlast = jnp.where(rt > 0, hprev_ref[pl.ds(7, 1), :], 0.0)
        hp = jnp.where(row == 0, hlast, pltpu.roll(h_ref[...], 1, 0))
        r, i, sp, log_a, w1, sq = _rnn_gates(xc, wa_ref, wx_ref, ba_ref, bx_ref, lam_ref)
        dix = g * sq
        di = dix * xc
        dxc = dix * i
        dsq = g * (i * xc)
        dlog_a = g * hp * av - dsq * jnp.where(sq > 0.0, (1.0 - w1) / sq, 0.0)
        dpr = (dlog_a * ((-RG_C) * sp)) * r * (1.0 - r)
        dpi = di * i * (1.0 - i)
        dprb, dpib, xcb = dpr.astype(bf16), dpi.astype(bf16), xc.astype(bf16)
        dxc = dxc + _dot_nt(dprb, wa_ref[...]) + _dot_nt(dpib, wx_ref[...])
        dwa_ref[...] += _dot_tn(xcb, dprb)
        dwx_ref[...] += _dot_tn(xcb, dpib)
        vec_ref[pl.ds(0, 1), :] += jnp.sum(dpr, axis=0, keepdims=True)
        vec_ref[pl.ds(1, 1), :] += jnp.sum(dpi, axis=0, keepdims=True)
        dsp = jnp.sum(dlog_a * ((-RG_C) * r), axis=0, keepdims=True)
        vec_ref[pl.ds(2, 1), :] += dsp * (-_sig(-lam_ref[...]))
        vec_ref[pl.ds(3, 1), :] += jnp.sum(dxc, axis=0, keepdims=True)

        dxc_scr[...] = dxc
        bot8 = dxc_scr[pl.ds(TL - 8, 8), :]
        nxt8 = dxc_next[...]
        dxr = cw_ref[pl.ds(3, 1), :] * dxc
        dxr8 = cw_ref[pl.ds(3, 1), :] * bot8
        for sh in range(1, 4):
            w = cw_ref[pl.ds(3 - sh, 1), :]
            dxr = dxr + w * pltpu.roll(dxc, TL - sh, 0)
            dxr8 = dxr8 + w * jnp.where(row8 < 8 - sh, pltpu.roll(bot8, 8 - sh, 0), pltpu.roll(nxt8, 8 - sh, 0))
        dxr_scr[...] = dxr
        dxr_scr[pl.ds(TL - 8, 8), :] = dxr8
        dxr_ref[...] = dxr_scr[...].astype(bf16)
        dxc_next[...] = dxc_scr[pl.ds(0, 8), :]

        x = x_ref[...]
        prev8 = jnp.where(rt > 0, xprev_ref[...], 0.0)
        dxc_top8 = dxc_scr[pl.ds(0, 8), :]
        vec_ref[pl.ds(7, 1), :] += jnp.sum(dxc * x, axis=0, keepdims=True)
        for sh in range(1, 4):
            inside = jnp.sum(dxc * jnp.where(row >= sh, pltpu.roll(x, sh, 0), 0.0), axis=0, keepdims=True)
            above = jnp.sum(dxc_top8 * jnp.where(row8 < sh, pltpu.roll(prev8, sh, 0), 0.0), axis=0, keepdims=True)
            vec_ref[pl.ds(7 - sh, 1), :] += inside + above

    tile = lambda cb: pl.BlockSpec((TL, D), lambda b, tt, cb=cb: (b * nt + nt - 1 - tt, cb))
    prev = lambda cb: pl.BlockSpec(
        (8, D), lambda b, tt, cb=cb: (jnp.maximum((b * nt + nt - 1 - tt) * (TL // 8) - 1, 0), cb))
    vec = _whole((1, D))
    return pl.pallas_call(
        body, name="rnn_bwd", grid=(nb, nt),
        in_specs=[tile(0), tile(0), tile(0), prev(0), tile(0), tile(1), prev(1),
                  _whole((4, D)), vec, _whole((D, D)), _whole((D, D)), vec, vec, vec],
        out_specs=[tile(0), _whole((D, D)), _whole((D, D)), _whole((8, D))],
        out_shape=[jax.ShapeDtypeStruct((t, D), bf16), jax.ShapeDtypeStruct((D, D), f32),
                   jax.ShapeDtypeStruct((D, D), f32), jax.ShapeDtypeStruct((8, D), f32)],
        scratch_shapes=[pltpu.VMEM((TL, D), f32), pltpu.VMEM((TL, D), f32), pltpu.VMEM((TL, D), f32),
                        pltpu.VMEM((1, D), f32), pltpu.VMEM((8, D), f32)],
        compiler_params=_params(("arbitrary", "arbitrary"), VMEM_LIMIT),
    )(dh, a, h, h, xc, rest, rest, conv_w, conv_b, wa_d, wx_d, ba, bx, lam)


def _attn_bwd(qkv, doa, o, lse, c, ct, seq):
    t = qkv.shape[0]
    nb, nq = t // seq, seq // TQ

    def body(q_ref, k_ref, v_ref, do_ref, o_ref, lse_ref, c_ref, ct_ref, dq_ref, dk_ref, dv_ref, dck_ref,
             dq_scr, dk_scr, dv_scr, dck_scr):
        hp, kt = pl.program_id(1), pl.program_id(2)
        lane = lax.broadcasted_iota(jnp.int32, (1, LANES), 1)
        rows = lax.broadcasted_iota(jnp.int32, (TQ, TQ), 0)
        cols = lax.broadcasted_iota(jnp.int32, (TQ, TQ), 1)

        @pl.when(kt == 0)
        def _():
            dq_scr[...] = jnp.zeros_like(dq_scr)

        dk_scr[...] = jnp.zeros_like(dk_scr)
        dv_scr[...] = jnp.zeros_like(dv_scr)
        dck_scr[...] = jnp.zeros_like(dck_scr)
        k2 = k_ref[...]
        v2 = v_ref[...]
        for hh in range(2):
            head = 2 * hp + hh
            lmask = (lane // 64) == hh
            kh = jnp.where(lmask, k2, jnp.zeros_like(k2))
            ck = ct_ref[0, pl.ds(head, 1), :]

            def q_step(qt, masked, lmask=lmask, kh=kh, ck=ck, head=head, hh=hh):
                qs = pl.multiple_of(qt * TQ, TQ)
                q2 = q_ref[pl.ds(qs, TQ), :]
                do2 = do_ref[pl.ds(qs, TQ), :]
                qh = jnp.where(lmask, q2, jnp.zeros_like(q2)) * jnp.asarray(QK_SCALE, bf16)
                doh = jnp.where(lmask, do2, jnp.zeros_like(do2))
                cq = jnp.sum(jnp.where(lane == head, c_ref[pl.ds(qs, TQ), :], 0.0), axis=1, keepdims=True)
                ls = jnp.sum(jnp.where(lane == head, lse_ref[pl.ds(qs, TQ), :], 0.0), axis=1, keepdims=True)
                delta = jnp.sum(doh.astype(f32) * o_ref[pl.ds(qs, TQ), :], axis=1, keepdims=True)
                s = _dot_nt(qh, k2) + cq - ck
                if masked:
                    s = jnp.where(rows >= cols, s, MASK_VALUE)
                p = jnp.exp(s - ls)
                ds = p * (_dot_nt(doh, v2) - delta)
                pb, dsb = p.astype(bf16), ds.astype(bf16)
                dv_scr[...] += _dot_tn(pb, doh)
                dk_scr[...] += _dot_tn(dsb, qh)
                dq_scr[pl.ds(qs, TQ), :] += _dot(dsb, kh) * QK_SCALE
                dck_scr[pl.ds(hh, 1), :] -= jnp.sum(ds, axis=0, keepdims=True)

            q_step(kt, True)

            def loop_body(qt, carry):
                q_step(qt, False)
                return carry

            lax.fori_loop(kt + 1, nq, loop_body, 0)

        dk_ref[...] = dk_scr[...].astype(bf16)
        dv_ref[...] = dv_scr[...].astype(bf16)
        dck_ref[0] = dck_scr[...]

        @pl.when(kt == nq - 1)
        def _():
            dq_ref[...] = dq_scr[...].astype(bf16)

    seqspec = lambda off: pl.BlockSpec((seq, LANES), lambda b, hp, kt: (b, off + hp))
    kspec = lambda off: pl.BlockSpec((TQ, LANES), lambda b, hp, kt: (b * nq + kt, off + hp))
    headspec = pl.BlockSpec((seq, LANES), lambda b, hp, kt: (b, 0))
    return pl.pallas_call(
        body, name="attn_bwd", grid=(nb, HEAD_PAIRS, nq),
        in_specs=[seqspec(0), kspec(HEAD_PAIRS), kspec(2 * HEAD_PAIRS), seqspec(0), seqspec(0), headspec, headspec,
                  pl.BlockSpec((1, LANES, TQ), lambda b, hp, kt: (b * nq + kt, 0, 0))],
        out_specs=[seqspec(0), kspec(0), kspec(0),
                   pl.BlockSpec((1, 8, TQ), lambda b, hp, kt: (b * HEAD_PAIRS + hp, 0, kt))],
        out_shape=[jax.ShapeDtypeStruct((t, D), bf16)] * 3 + [jax.ShapeDtypeStruct((nb * HEAD_PAIRS, 8, seq), f32)],
        scratch_shapes=[pltpu.VMEM((seq, LANES), f32), pltpu.VMEM((TQ, LANES), f32), pltpu.VMEM((TQ, LANES), f32),
                        pltpu.VMEM((8, TQ), f32)],
        compiler_params=_params(("parallel", "parallel", "arbitrary"), VMEM_LIMIT),
    )(qkv, qkv, qkv, doa, o, lse, c, ct)


def _forget_bwd(dct, f128, seq):
    t = f128.shape[0]
    nb = seq // LANES

    def body(dct_ref, f_ref, df_ref, dbf_ref):
        @pl.when(pl.program_id(0) == 0)
        def _():
            dbf_ref[...] = jnp.zeros_like(dbf_ref)

        r = lax.broadcasted_iota(jnp.int32, (LANES, LANES), 0)
        cidx = lax.broadcasted_iota(jnp.int32, (LANES, LANES), 1)
        tri = (r <= cidx).astype(f32)
        carry = jnp.zeros((1, LANES), f32)
        total = jnp.zeros((1, LANES), f32)
        for blk in reversed(range(nb)):
            dcb = dct_ref[0, :, pl.ds(blk * LANES, LANES)].T
            dlf = jnp.dot(tri, dcb, preferred_element_type=f32, precision=lax.Precision.HIGHEST) + carry
            df = dlf * _sig(-f_ref[pl.ds(blk * LANES, LANES), :])
            df_ref[pl.ds(blk * LANES, LANES), :] = df.astype(bf16)
            total = total + jnp.sum(df, axis=0, keepdims=True)
            carry = carry + jnp.sum(dcb, axis=0, keepdims=True)
        dbf_ref[...] += total

    return pl.pallas_call(
        body, name="forget_bwd", grid=(t // seq,),
        in_specs=[pl.BlockSpec((1, LANES, seq), lambda b: (b, 0, 0)), pl.BlockSpec((seq, LANES), lambda b: (b, 0))],
        out_specs=[pl.BlockSpec((seq, LANES), lambda b: (b, 0)), _whole((1, LANES))],
        out_shape=[jax.ShapeDtypeStruct((t, LANES), bf16), jax.ShapeDtypeStruct((1, LANES), f32)],
        compiler_params=_params(("arbitrary",)),
    )(dct, f128)


def _in_bwd(dz, df, x, dy, w_qkv, w_rest, w_f, w_pre):
    t = x.shape[0]
    n_qkv = w_qkv.shape[1] // D
    n_rest = w_rest.shape[1] // D

    def body(*refs):
        dz_refs = refs[:n_qkv + n_rest]
        df_ref, x_ref, dy_ref, wq_ref, wr_ref, wf_ref, wp_ref, gx_ref, dwp_ref = refs[n_qkv + n_rest:]

        @pl.when(pl.program_id(0) == 0)
        def _():
            dwp_ref[...] = jnp.zeros_like(dwp_ref)

        dh = _dot_nt(df_ref[...], wf_ref[...])
        for p in range(n_qkv):
            dh = dh + _dot_nt(dz_refs[p][...], wq_ref[:, pl.ds(p * D, D)])
        for p in range(n_rest):
            dh = dh + _dot_nt(dz_refs[n_qkv + p][...], wr_ref[:, pl.ds(p * D, D)])
        xv = x_ref[...]
        r1 = lax.rsqrt(jnp.mean(xv * xv, axis=-1, keepdims=True) + NORM_EPS)
        xh = xv * r1
        dwp_ref[...] += jnp.sum(dh * xh, axis=0, keepdims=True)
        dxh = dh * wp_ref[...]
        gx_ref[...] = dy_ref[...] + r1 * (dxh - xh * jnp.mean(dxh * xh, axis=-1, keepdims=True))

    once = lambda shape: pl.BlockSpec(shape, lambda i: (0, 0), pipeline_mode=pl.Buffered(1))
    return pl.pallas_call(
        body, name="in_bwd", grid=(t // TM,),
        in_specs=[_tile(TM, D)] * (n_qkv + n_rest) + [_tile(TM, LANES), _tile(TM, D), _tile(TM, D),
                  once(w_qkv.shape), once(w_rest.shape), once(w_f.shape), _whole((1, D))],
        out_specs=[_tile(TM, D), _whole((1, D))],
        out_shape=[jax.ShapeDtypeStruct((t, D), f32), jax.ShapeDtypeStruct((1, D), f32)],
        compiler_params=_params(("arbitrary",), VMEM_LIMIT),
    )(*dz, df, x, dy, w_qkv, w_rest, w_f, w_pre)


def _tn_mm(name, a, b, tn, tk=512):
    t, k = a.shape
    tk = min(tk, t)
    n = b.shape[1]

    def body(a_ref, b_ref, o_ref, s_ref):
        @pl.when(pl.program_id(1) == 0)
        def _():
            o_ref[...] = jnp.zeros_like(o_ref)
            s_ref[...] = jnp.zeros_like(s_ref)

        bv = b_ref[...]
        o_ref[...] += _dot_tn(a_ref[...], bv)
        s_ref[...] += jnp.sum(bv.astype(f32), axis=0, keepdims=True)

    return pl.pallas_call(
        body, name=name, grid=(n // tn, t // tk),
        in_specs=[pl.BlockSpec((tk, k), lambda j, kk: (kk, 0)), pl.BlockSpec((tk, tn), lambda j, kk: (kk, j))],
        out_specs=[pl.BlockSpec((k, tn), lambda j, kk: (0, j)), pl.BlockSpec((1, tn), lambda j, kk: (0, j))],
        out_shape=[jax.ShapeDtypeStruct((k, n), f32), jax.ShapeDtypeStruct((1, n), f32)],
        compiler_params=_params(("parallel", "arbitrary"), VMEM_LIMIT),
    )(a, b)


def _position():
    return lax.axis_index("x"), lax.axis_index("y"), lax.axis_index("c")


def _gather_shards(parts):
    n = len(parts)

    def body(*refs):
        srcs, dsts = refs[:n], refs[n:2 * n]
        send, recv, local = refs[2 * n:]
        x, y, c = _position()
        me = 2 * x + y
        chips = [(1 - x, y), (x, 1 - y), (1 - x, 1 - y)]

        def remote(a, j, landing):
            px, py = chips[j]
            return pltpu.make_async_remote_copy(
                src_ref=srcs[a], dst_ref=dsts[a].at[landing], send_sem=send.at[a * 3 + j], recv_sem=recv.at[a * 3 + j],
                device_id=(px, py, c), device_id_type=MESH)

        own = [pltpu.make_async_copy(srcs[a], dsts[a].at[me], local.at[a]) for a in range(n)]
        for cp in own:
            cp.start()
        sends = [remote(a, j, me) for j in range(3) for a in range(n)]
        for cp in sends:
            cp.start()
        for j, (px, py) in enumerate(chips):
            for a in range(n):
                remote(a, j, 2 * px + py).wait_recv()
        for cp in sends:
            cp.wait_send()
        for cp in own:
            cp.wait()

    anyspec = pl.BlockSpec(memory_space=pl.ANY)
    return pl.pallas_call(
        body, name="gather_shards",
        in_specs=[anyspec] * n, out_specs=[anyspec] * n,
        out_shape=[jax.ShapeDtypeStruct((N_CHIPS,) + p.shape, p.dtype) for p in parts],
        scratch_shapes=[pltpu.SemaphoreType.DMA((3 * n,)), pltpu.SemaphoreType.DMA((3 * n,)),
                        pltpu.SemaphoreType.DMA((n,))],
    )(*parts)


def _allsum_rows(part):
    rows_n = part.shape[0]

    def body(x_ref, gath_ref, sum_ref, send_sems, recv_sems, local_sem):
        x, y, c = _position()
        me, sibling = (x, y, c), (x, y, 1 - c)
        chips = [(1 - x, y), (x, 1 - y), (1 - x, 1 - y)]

        def rows(px, py, pc):
            return gath_ref.at[pl.ds((4 * px + 2 * py + pc) * rows_n, rows_n), :]

        def copy(k, block, to, src=None):
            return pltpu.make_async_remote_copy(
                src_ref=rows(*block) if src is None else src, dst_ref=rows(*block),
                send_sem=send_sems.at[k], recv_sem=recv_sems.at[k], device_id=to, device_id_type=MESH)

        mine = pltpu.make_async_copy(x_ref, rows(*me), local_sem)
        mine.start()
        first = [copy(0, me, sibling, src=x_ref)]
        first += [copy(1 + j, me, (*chip, c), src=x_ref) for j, chip in enumerate(chips)]
        for cp in first:
            cp.start()
        passed = [copy(4 + j, (*chip, c), sibling) for j, chip in enumerate(chips)]
        for j, chip in enumerate(chips):
            copy(1 + j, (*chip, c), me).wait_recv()
            passed[j].start()
        copy(0, sibling, me).wait_recv()
        for j, chip in enumerate(chips):
            copy(4 + j, (*chip, 1 - c), me).wait_recv()
        for cp in first + passed:
            cp.wait_send()
        mine.wait()
        total = gath_ref[pl.ds(0, rows_n), :]
        for d in range(1, N_DEV):
            total = total + gath_ref[pl.ds(d * rows_n, rows_n), :]
        sum_ref[...] = total

    vm = pl.BlockSpec(memory_space=pltpu.VMEM)
    return pl.pallas_call(
        body, name="allsum_rows", in_specs=[vm], out_specs=[vm, vm],
        out_shape=[jax.ShapeDtypeStruct((N_DEV * rows_n, D), f32), jax.ShapeDtypeStruct((rows_n, D), f32)],
        scratch_shapes=[pltpu.SemaphoreType.DMA((7,)), pltpu.SemaphoreType.DMA((7,)), pltpu.SemaphoreType.DMA],
    )(part)[1]


def _exchange_pieces(arrs):
    n = len(arrs)

    def body(*refs):
        srcs, dsts = refs[:n], refs[n:2 * n]
        send, recv, local = refs[2 * n:]
        x, y, c = _position()
        me = 4 * x + 2 * y + c
        peers = []
        for mask in range(1, N_DEV):
            px = 1 - x if mask & 4 else x
            py = 1 - y if mask & 2 else y
            pc = 1 - c if mask & 1 else c
            peers.append((px, py, pc))

        def remote(a, k, piece, landing):
            return pltpu.make_async_remote_copy(
                src_ref=srcs[a].at[piece], dst_ref=dsts[a].at[landing], send_sem=send.at[a * 7 + k],
                recv_sem=recv.at[a * 7 + k], device_id=peers[k], device_id_type=MESH)

        own = [pltpu.make_async_copy(srcs[a].at[me], dsts[a].at[me], local.at[a]) for a in range(n)]
        for cp in own:
            cp.start()
        sends = []
        for k, (px, py, pc) in enumerate(peers):
            for a in range(n):
                sends.append(remote(a, k, 4 * px + 2 * py + pc, me))
        for cp in sends:
            cp.start()
        for k, (px, py, pc) in enumerate(peers):
            for a in range(n):
                remote(a, k, me, 4 * px + 2 * py + pc).wait_recv()
        for cp in sends:
            cp.wait_send()
        for cp in own:
            cp.wait()

    anyspec = pl.BlockSpec(memory_space=pl.ANY)
    return pl.pallas_call(
        body, name="exchange_pieces", in_specs=[anyspec] * n, out_specs=[anyspec] * n,
        out_shape=[jax.ShapeDtypeStruct(a.shape, a.dtype) for a in arrs],
        scratch_shapes=[pltpu.SemaphoreType.DMA((7 * n,)), pltpu.SemaphoreType.DMA((7 * n,)),
                        pltpu.SemaphoreType.DMA((n,))],
    )(*arrs)


def _swap_halves(arrs):
    n = len(arrs)

    def body(*refs):
        srcs, dsts = refs[:n], refs[n:2 * n]
        send, recv, local = refs[2 * n:]
        x, y, c = _position()

        def remote(a, landing):
            return pltpu.make_async_remote_copy(
                src_ref=srcs[a], dst_ref=dsts[a].at[landing], send_sem=send.at[a], recv_sem=recv.at[a],
                device_id=(x, y, 1 - c), device_id_type=MESH)

        own = [pltpu.make_async_copy(srcs[a], dsts[a].at[c], local.at[a]) for a in range(n)]
        sends = [remote(a, c) for a in range(n)]
        for cp in own + sends:
            cp.start()
        for a in range(n):
            remote(a, 1 - c).wait_recv()
        for cp in sends:
            cp.wait_send()
        for cp in own:
            cp.wait()

    anyspec = pl.BlockSpec(memory_space=pl.ANY)
    return pl.pallas_call(
        body, name="swap_halves", in_specs=[anyspec] * n, out_specs=[anyspec] * n,
        out_shape=[jax.ShapeDtypeStruct((2,) + a.shape, a.dtype) for a in arrs],
        scratch_shapes=[pltpu.SemaphoreType.DMA((n,)), pltpu.SemaphoreType.DMA((n,)), pltpu.SemaphoreType.DMA((n,))],
    )(*arrs)


def _row_block(r):
    return 128 if r % 128 == 0 else r


def _sum_slots(name, slots):
    _, r, n = slots.shape
    rb = _row_block(r)

    def body(s_ref, o_ref):
        total = s_ref[0].astype(f32)
        for d in range(1, N_DEV):
            total = total + s_ref[d].astype(f32)
        o_ref[...] = total

    return pl.pallas_call(
        body, name=name, grid=(r // rb,),
        in_specs=[pl.BlockSpec((N_DEV, rb, n), lambda i: (0, i, 0))],
        out_specs=pl.BlockSpec((rb, n), lambda i: (i, 0)),
        out_shape=jax.ShapeDtypeStruct((r, n), f32),
        compiler_params=_params(("parallel",), VMEM_LIMIT),
    )(slots)


def _adamw(name, w, g, m, v):
    r, n = w.shape
    rb = _row_block(r)

    def body(w_ref, g_ref, m_ref, v_ref, d_ref, nm_ref, nv_ref):
        gv = g_ref[...]
        m2 = ADAM_B1 * m_ref[...] + (1.0 - ADAM_B1) * gv
        v2 = ADAM_B2 * v_ref[...] + (1.0 - ADAM_B2) * (gv * gv)
        m_hat = m2 / (1.0 - ADAM_B1 ** ADAM_STEP)
        v_hat = v2 / (1.0 - ADAM_B2 ** ADAM_STEP)
        d_ref[...] = (-ADAM_LR) * (m_hat / (jnp.sqrt(v_hat) + ADAM_EPS) + ADAM_WD * w_ref[...])
        nm_ref[...] = m2
        nv_ref[...] = v2

    spec = pl.BlockSpec((rb, n), lambda i: (i, 0))
    return pl.pallas_call(
        body, name=name, grid=(r // rb,), in_specs=[spec] * 4, out_specs=[spec] * 3,
        out_shape=[jax.ShapeDtypeStruct((r, n), f32)] * 3,
        compiler_params=_params(("parallel",), VMEM_LIMIT),
    )(w, g, m, v)


def _identity(a):
    return a


def _local_step(x2, tgt2, seq, wt):
    nb = x2.shape[0] // seq
    h = _prenorm(x2, wt["pre_w"])
    qkv = _mm("in_qkv", [(h, 0)], _identity, wt["w_qkv"], wt["b_qkv"], bf16, 512, 1024)
    rest = _mm("in_rest", [(h, 0)], _identity, wt["w_rest"], wt["b_rest"], f32, 512, 1024)
    f128 = _mm("in_f", [(h, 0)], _identity, wt["w_f"], wt["b_f"], f32, 512, LANES)
    c, ct = _forget_prep(f128, seq)
    o_att, pa, lse = _attn_fwd(qkv, rest, c, ct, seq)
    ya = _mm("proj_a", [(pa, 0)], _identity, wt["w_a"], None, f32, 512, D)
    rnn_w = (wt["conv_w"], wt["conv_b"], wt["wa_d"], wt["wx_d"], wt["ba"], wt["bx"], wt["lam"])
    xc, a, hrec, pr = _rnn_fwd(rest, *rnn_w, seq)
    yr = _mm("proj_r", [(pr, 0)], _identity, wt["w_r"], None, f32, 512, D)
    o = _mm("proj_out", [(rest, 3), (rest, 4), (ya, 0), (yr, 0)], _merge, wt["w_o"], None, f32, TM, D)

    do, dy, loss8, d_post = _post_loss(o, x2, tgt2, wt["post_w"])
    dya, dyr, dmga, dmgr, d_wo = _out_bwd(do, rest, ya, yr, wt["w_o"])
    doa, dga, d_wa = _branch_bwd("branch_a_bwd", dya, rest, 0, o_att, pa, wt["w_a"], bf16)
    dhrec, dgr, d_wr = _branch_bwd("branch_r_bwd", dyr, rest, 2, hrec, pr, wt["w_r"], f32)
    dxr, d_wad, d_wxd, vec = _rnn_bwd(dhrec, a, hrec, xc, rest, *rnn_w, seq)
    dq, dk, dv, dck = _attn_bwd(qkv, doa, o_att, lse, c, ct, seq)
    dct = dck.reshape(nb, HEAD_PAIRS, 8, seq)[:, :, :2, :].reshape(nb, HEADS, seq)
    dct = jnp.pad(dct, ((0, 0), (0, LANES - HEADS), (0, 0)))
    df, db_f = _forget_bwd(dct, f128, seq)
    pieces = [dq, dk, dv, dga, dxr, dgr, dmga, dmgr]
    gx, d_pre = _in_bwd(pieces, df, x2, dy, wt["w_qkv"], wt["w_rest"], wt["w_f"], wt["pre_w"])
    names = ["q", "k", "v", "ga", "xr", "gr", "mga", "mgr"]
    dws, dbs = [], []
    for nm, piece in zip(names, pieces):
        dw_p, db_p = _tn_mm("dw_in_" + nm, h, piece, 512)
        dws.append(dw_p)
        dbs.append(db_p)
    dw_f, _ = _tn_mm("dw_in_f", h, df, LANES)
    zeros_w = jnp.zeros((D, IN_TOTAL - IN_USED), f32)
    d_w_in = jnp.concatenate(dws[:3] + [dw_f[:, :HEADS]] + dws[3:] + [zeros_w], axis=1)
    d_b_in = jnp.concatenate(dbs[:3] + [db_f[:, :HEADS]] + dbs[3:] + [zeros_w[:1]], axis=1)
    return dict(loss=loss8[0, 0], grad_x=gx, pre_w=d_pre, w_in=d_w_in, b_in=d_b_in, conv_w=vec[4:8], conv_b=vec[3:4],
                wa_d=d_wad, ba=vec[0:1], wx_d=d_wxd, bx=vec[1:2], lam=vec[2:3], w_a=d_wa, w_r=d_wr, w_o=d_wo,
                post_w=d_post)


def _block_diag(w):
    g, bw, _ = w.shape
    eye = jnp.eye(g, dtype=w.dtype)
    return (w[:, :, None, :] * eye[:, None, :, None]).reshape(g * bw, g * bw)


def _block_diag_of(dense, g):
    bw = dense.shape[0] // g
    eye = jnp.eye(g, dtype=dense.dtype)
    return jnp.sum(dense.reshape(g, bw, g, bw) * eye[:, None, :, None], axis=2)


def _pad_cols(a, n):
    return jnp.pad(a, ((0, 0), (0, n - a.shape[1])))


def _pad_rows(a, n):
    return jnp.pad(a, ((0, n - a.shape[0]), (0, 0)))


def kernel(x, pre_norm_w, w_in, b_in, conv_w, conv_b, rg_wa, rg_ba, rg_wx, rg_bx, rg_lambda, w_branch_a, w_branch_r, w_out, post_norm_w, loss_target, m_pre_norm_w, m_w_in, m_b_in, m_conv_w, m_conv_b, m_rg_wa, m_rg_ba, m_rg_wx, m_rg_bx, m_rg_lambda, m_w_branch_a, m_w_branch_r, m_w_out, m_post_norm_w, v_pre_norm_w, v_w_in, v_b_in, v_conv_w, v_conv_b, v_rg_wa, v_rg_ba, v_rg_wx, v_rg_bx, v_rg_lambda, v_w_branch_a, v_w_branch_r, v_w_out, v_post_norm_w):
    nb, seq, _ = x.shape
    chip = 2 * lax.axis_index("x") + lax.axis_index("y")
    n_groups = rg_wa.shape[1]

    g_in, g_a, g_r, g_o, g_cw = _gather_shards(
        [w_in[0].astype(bf16), w_branch_a[0].astype(bf16), w_branch_r[0].astype(bf16), w_out[0].astype(bf16), conv_w[0]])
    w_full = jnp.transpose(g_in, (1, 0, 2)).reshape(D, IN_TOTAL)
    q_end, f_end = 3 * D, 3 * D + HEADS
    wt = dict(
        pre_w=pre_norm_w, post_w=post_norm_w,
        w_qkv=w_full[:, :q_end], b_qkv=b_in[:, :q_end],
        w_f=_pad_cols(w_full[:, q_end:f_end], LANES), b_f=_pad_cols(b_in[:, q_end:f_end], LANES),
        w_rest=w_full[:, f_end:IN_USED], b_rest=b_in[:, f_end:IN_USED],
        w_a=g_a.reshape(D, D), w_r=g_r.reshape(D, D), w_o=g_o.reshape(D, D),
        conv_w=jnp.transpose(g_cw, (1, 0, 2)).reshape(4, D), conv_b=conv_b,
        wa_d=_block_diag(rg_wa[0]).astype(bf16), wx_d=_block_diag(rg_wx[0]).astype(bf16),
        ba=rg_ba, bx=rg_bx, lam=rg_lambda)

    part = _local_step(x.reshape(nb * seq, D), loss_target.reshape(nb * seq, D), seq, wt)
    loss = lax.psum(part["loss"], ("x", "y", "c"))
    grad_x = part["grad_x"].reshape(nb, seq, D)

    small = jnp.concatenate([
        part["pre_w"], _pad_cols(part["b_in"], 10 * D).reshape(10, D), part["conv_b"],
        _block_diag_of(part["wa_d"], n_groups).reshape(-1, D), part["ba"],
        _block_diag_of(part["wx_d"], n_groups).reshape(-1, D), part["bx"], part["lam"], part["post_w"],
        part["conv_w"]], axis=0)
    n_small = small.shape[0]
    n_rep = n_small - 4
    tot = _allsum_rows(_pad_rows(small, -(-n_small // 8) * 8))
    g_rep = tot[:n_rep]
    g_conv_w = lax.dynamic_slice_in_dim(tot[n_rep:n_small], chip * (D // N_CHIPS), D // N_CHIPS, axis=1)

    def pack(pre, b, cb, wa, ba, wx, bx, lam, post):
        return jnp.concatenate([pre, _pad_cols(b, 10 * D).reshape(10, D), cb, wa.reshape(-1, D), ba,
                                wx.reshape(-1, D), bx, lam, post], axis=0)

    def unpack(p):
        o = [0]

        def take(k):
            o[0] += k
            return p[o[0] - k:o[0]]

        pre = take(1)
        b = take(10).reshape(1, 10 * D)[:, :IN_TOTAL]
        cb = take(1)
        wa = take(64).reshape(rg_wa.shape)
        ba = take(1)
        wx = take(64).reshape(rg_wx.shape)
        bx = take(1)
        lam = take(1)
        post = take(1)
        return dict(pre_norm_w=pre, b_in=b, conv_b=cb, rg_wa=wa, rg_ba=ba, rg_wx=wx, rg_bx=bx, rg_lambda=lam,
                    post_norm_w=post)

    w_rep = pack(pre_norm_w, b_in, conv_b, rg_wa, rg_ba, rg_wx, rg_bx, rg_lambda, post_norm_w)
    m_rep = pack(m_pre_norm_w, m_b_in, m_conv_b, m_rg_wa, m_rg_ba, m_rg_wx, m_rg_bx, m_rg_lambda, m_post_norm_w)
    v_rep = pack(v_pre_norm_w, v_b_in, v_conv_b, v_rg_wa, v_rg_ba, v_rg_wx, v_rg_bx, v_rg_lambda, v_post_norm_w)
    d_rep, nm_rep, nv_rep = _adamw("adamw_rep", w_rep, g_rep, m_rep, v_rep)
    grads, deltas, new_m, new_v = unpack(g_rep), unpack(d_rep), unpack(nm_rep), unpack(nv_rep)

    shard_cols = IN_TOTAL // N_CHIPS
    p_in = jnp.transpose(part["w_in"].reshape(D, N_CHIPS, shard_cols), (1, 0, 2)).reshape(N_DEV, D // 2, shard_cols)
    p_aro = jnp.concatenate([part[k].reshape(N_DEV, D // N_DEV, D) for k in ("w_a", "w_r", "w_o")], axis=1)
    s_in, s_aro = _exchange_pieces([p_in.astype(bf16), p_aro.astype(bf16)])
    f_in, f_aro = _swap_halves([_sum_slots("sum_w_in", s_in), _sum_slots("sum_w_aro", s_aro)])
    g_w_in = f_in.reshape(D, shard_cols)
    rows = D // N_DEV
    g_aro = jnp.concatenate([f_aro[:, i * rows:(i + 1) * rows, :].reshape(2 * rows, D) for i in range(3)], axis=0)

    d_w_in, nm_w_in, nv_w_in = _adamw("adamw_w_in", w_in[0], g_w_in, m_w_in[0], v_w_in[0])
    stack = lambda a, b, c: jnp.concatenate([a[0], b[0], c[0]], axis=0)
    d_aro, nm_aro, nv_aro = _adamw("adamw_w_aro", stack(w_branch_a, w_branch_r, w_out), g_aro,
                                   stack(m_w_branch_a, m_w_branch_r, m_w_out),
                                   stack(v_w_branch_a, v_w_branch_r, v_w_out))
    d_cw, nm_cw, nv_cw = _adamw("adamw_conv_w", conv_w[0], g_conv_w, m_conv_w[0], v_conv_w[0])

    def sharded(t_in, t_aro, t_cw):
        r2 = 2 * rows
        return dict(w_in=t_in[None], conv_w=t_cw[None], w_branch_a=t_aro[None, :r2], w_branch_r=t_aro[None, r2:2 * r2],
                    w_out=t_aro[None, 2 * r2:])

    order = ["pre_norm_w", "w_in", "b_in", "conv_w", "conv_b", "rg_wa", "rg_ba", "rg_wx", "rg_bx", "rg_lambda",
             "w_branch_a", "w_branch_r", "w_out", "post_norm_w"]
    outs = [loss, grad_x]
    for rep, shd in ((grads, sharded(g_w_in, g_aro, g_conv_w)), (deltas, sharded(d_w_in, d_aro, d_cw)),
                     (new_m, sharded(nm_w_in, nm_aro, nm_cw)), (new_v, sharded(nv_w_in, nv_aro, nv_cw))):
        both = {**rep, **shd}
        outs.extend(both[k] for k in order)
    return tuple(outs)
```

```python
import jax
import jax.numpy as jnp
from jax import lax
from jax.experimental import pallas as pl
from jax.experimental.pallas import tpu as pltpu

f32 = jnp.float32
bf16 = jnp.bfloat16

D = 1024
HEADS = 16
HEAD_PAIRS = 8
LANES = 128
NORM_EPS = 1e-6
MASK_VALUE = -1e30
RG_C = 8.0
QK_SCALE = 0.125
TQ = 256
ATT_GROUP = 8
TL = 256
TM = 256
IN_USED = 8 * D + HEADS
IN_TOTAL = 9 * D + HEADS
N_CHIPS = 4
N_DEV = 8
ADAM_LR, ADAM_B1, ADAM_B2, ADAM_EPS, ADAM_WD, ADAM_STEP = 0.001, 0.9, 0.999, 1e-08, 0.01, 10
VMEM_LIMIT = 56 * 1024 * 1024
MESH = pl.DeviceIdType.MESH


def _dot(a, b):
    return jnp.dot(a, b, preferred_element_type=f32)


def _dot_nt(a, b):
    return lax.dot_general(a, b, (((1,), (1,)), ((), ())), preferred_element_type=f32)


def _dot_tn(a, b):
    return lax.dot_general(a, b, (((0,), (0,)), ((), ())), preferred_element_type=f32)


def _sig(x):
    return 1.0 / (1.0 + jnp.exp(-x))


def _softplus(x):
    return jnp.maximum(x, 0.0) + jnp.log(1.0 + jnp.exp(-jnp.abs(x)))


def _params(sem, vmem=None):
    return pltpu.CompilerParams(dimension_semantics=sem, vmem_limit_bytes=vmem)


def _tile(tm, width, cb=0):
    return pl.BlockSpec((tm, width), lambda i, cb=cb: (i, cb))


def _whole(shape):
    nd = len(shape)
    return pl.BlockSpec(shape, lambda *_: (0,) * nd)


def _prenorm(x, w_pre):
    t = x.shape[0]

    def body(x_ref, w_ref, h_ref):
        xv = x_ref[...]
        r = lax.rsqrt(jnp.mean(xv * xv, axis=-1, keepdims=True) + NORM_EPS)
        h_ref[...] = (xv * r * w_ref[...]).astype(bf16)

    return pl.pallas_call(
        body, name="prenorm", grid=(t // TM,),
        in_specs=[_tile(TM, D), _whole((1, D))], out_specs=_tile(TM, D),
        out_shape=jax.ShapeDtypeStruct((t, D), bf16),
        compiler_params=_params(("parallel",)),
    )(x, w_pre)


def _mm(name, ins, prologue, w, bias, out_dtype, tm, tn):
    t = ins[0][0].shape[0]
    tm = min(tm, t)
    k, n = w.shape
    n_in = len(ins)

    def body(*refs):
        a = prologue(*[r[...] for r in refs[:n_in]])
        acc = _dot(a, refs[n_in][...])
        if bias is not None:
            acc = acc + refs[n_in + 1][...]
        refs[-1][...] = acc.astype(out_dtype)

    in_specs = [pl.BlockSpec((tm, k), lambda i, j, cb=cb: (i, cb)) for _, cb in ins]
    in_specs.append(pl.BlockSpec((k, tn), lambda i, j: (0, j)))
    args = [a for a, _ in ins] + [w]
    if bias is not None:
        in_specs.append(pl.BlockSpec((1, tn), lambda i, j: (0, j)))
        args.append(bias)
    return pl.pallas_call(
        body, name=name, grid=(t // tm, n // tn), in_specs=in_specs,
        out_specs=pl.BlockSpec((tm, tn), lambda i, j: (i, j)),
        out_shape=jax.ShapeDtypeStruct((t, n), out_dtype),
        compiler_params=_params(("parallel", "parallel"), VMEM_LIMIT),
    )(*args)


def _forget_prep(f128, seq):
    t = f128.shape[0]
    nb = seq // LANES

    def body(f_ref, c_ref):
        r = lax.broadcasted_iota(jnp.int32, (LANES, LANES), 0)
        cidx = lax.broadcasted_iota(jnp.int32, (LANES, LANES), 1)
        tri = (r >= cidx).astype(f32)
        carry = jnp.zeros((1, LANES), f32)
        for blk in range(nb):
            fv = f_ref[pl.ds(blk * LANES, LANES), :]
            lf = -_softplus(-fv)
            c_ref[pl.ds(blk * LANES, LANES), :] = (
                jnp.dot(tri, lf, preferred_element_type=f32, precision=lax.Precision.HIGHEST) + carry)
            carry = carry + jnp.sum(lf, axis=0, keepdims=True)

    return pl.pallas_call(
        body, name="forget_prep", grid=(t // seq,),
        in_specs=[pl.BlockSpec((seq, LANES), lambda b: (b, 0))],
        out_specs=pl.BlockSpec((seq, LANES), lambda b: (b, 0)),
        out_shape=jax.ShapeDtypeStruct((t, LANES), f32),
        compiler_params=_params(("parallel",)),
    )(f128)


def _split3(cv):
    hi = cv.astype(bf16)
    r1 = cv - hi.astype(f32)
    mid = r1.astype(bf16)
    lo = (r1 - mid.astype(f32)).astype(bf16)
    return hi, mid, lo


def _attn_prep(qkv, c):
    t = qkv.shape[0]

    def body(q_ref, k_ref, c_ref, qa_ref, ka_ref):
        hp = pl.program_id(1)
        lane = lax.broadcasted_iota(jnp.int32, (1, LANES), 1)
        cv = c_ref[...]
        one = jnp.ones((), bf16)
        zero = jnp.zeros((), bf16)
        for hh in range(2):
            ch = jnp.sum(jnp.where(lane == 2 * hp + hh, cv, 0.0), axis=1, keepdims=True)
            hi, mid, lo = _split3(ch)
            q2, k2 = q_ref[...], k_ref[...]
            if hh == 1:
                q2, k2 = pltpu.roll(q2, 64, 1), pltpu.roll(k2, 64, 1)
            ones = jnp.where((lane >= 67) & (lane < 70), one, zero)
            qa = jnp.where(lane < 64, q2 * jnp.asarray(QK_SCALE, bf16),
                           jnp.where(lane == 64, hi, jnp.where(lane == 65, mid, jnp.where(lane == 66, lo, ones))))
            ones = jnp.where((lane >= 64) & (lane < 67), one, zero)
            ka = jnp.where(lane < 64, k2,
                           jnp.where(lane == 67, -hi, jnp.where(lane == 68, -mid, jnp.where(lane == 69, -lo, ones))))
            qa_ref[:, pl.ds(hh * LANES, LANES)] = qa
            ka_ref[:, pl.ds(hh * LANES, LANES)] = ka

    tm = min(512, t)
    spec = lambda off: pl.BlockSpec((tm, LANES), lambda i, hp: (i, off + hp))
    out = pl.BlockSpec((tm, 2 * LANES), lambda i, hp: (i, hp))
    return pl.pallas_call(
        body, name="attn_prep", grid=(t // tm, HEAD_PAIRS),
        in_specs=[spec(0), spec(HEAD_PAIRS), pl.BlockSpec((tm, LANES), lambda i, hp: (i, 0))],
        out_specs=[out, out],
        out_shape=[jax.ShapeDtypeStruct((t, 2 * D), bf16)] * 2,
        compiler_params=_params(("parallel", "parallel")),
    )(qkv, qkv, c)


def _attn_fwd(qa, ka, qkv, rest, seq):
    t = qkv.shape[0]
    nb, nq = t // seq, seq // TQ

    hg = ATT_GROUP
    ng = HEADS // hg

    def body(q_ref, k_ref, v_ref, ga_ref, o_ref, pa_ref, lse_ref, acc_scr):
        qi, gi = pl.program_id(1), pl.program_id(2)
        krow = lax.broadcasted_iota(jnp.int32, (TQ, TQ), 0)
        qcol = lax.broadcasted_iota(jnp.int32, (TQ, TQ), 1)
        acc_scr[...] = jnp.zeros_like(acc_scr)

        def kv_step(kt, carry, masked):
            ks = pl.multiple_of(kt * TQ, TQ)
            sts = [_dot_nt(k_ref[pl.ds(ks, TQ), pl.ds(g * LANES, LANES)], q_ref[:, pl.ds(g * LANES, LANES)])
                   for g in range(hg)]
            if masked:
                sts = [jnp.where(krow <= qcol, st, MASK_VALUE) for st in sts]
            m_new = [jnp.maximum(carry[g][0], jnp.max(sts[g], axis=0, keepdims=True)) for g in range(hg)]
            ps = [jnp.exp(sts[g] - m_new[g]) for g in range(hg)]
            alphas = [jnp.exp(carry[g][0] - m_new[g]) for g in range(hg)]
            pvs = [_dot_tn(v_ref[pl.ds(ks, TQ), pl.ds((g // 2) * LANES, LANES)], ps[g].astype(bf16))
                   for g in range(hg)]
            olds = [acc_scr[g] for g in range(hg)]
            for g in range(hg):
                acc_scr[g] = alphas[g] * olds[g] + pvs[g]
            return tuple((m_new[g], alphas[g] * carry[g][1] + jnp.sum(ps[g], axis=0, keepdims=True))
                         for g in range(hg))

        init = tuple((jnp.full((1, TQ), MASK_VALUE, f32), jnp.zeros((1, TQ), f32)) for _ in range(hg))
        carry = lax.fori_loop(0, qi, lambda kt, cr: kv_step(kt, cr, False), init)
        stats = kv_step(qi, carry, True)
        drow = lax.broadcasted_iota(jnp.int32, (LANES, TQ), 0)
        for g in range(hg):
            m, l = stats[g]
            lse_ref[0, pl.ds(hg * gi + g, 1), :] = m + jnp.log(l)
        for j in range(hg // 2):
            o2 = jnp.where(drow < 64, acc_scr[2 * j] / stats[2 * j][1], acc_scr[2 * j + 1] / stats[2 * j + 1][1]).T
            o_ref[:, pl.ds(j * LANES, LANES)] = o2
            ga = ga_ref[:, pl.ds(j * LANES, LANES)]
            pa_ref[:, pl.ds(j * LANES, LANES)] = (o2 * (ga * _sig(ga))).astype(bf16)

    vw = hg * 64
    tile = pl.BlockSpec((TQ, vw), lambda b, qi, gi: (b * nq + qi, gi))
    return pl.pallas_call(
        body, name="attn_fwd", grid=(nb, nq, ng),
        in_specs=[pl.BlockSpec((TQ, hg * LANES), lambda b, qi, gi: (b * nq + qi, gi)),
                  pl.BlockSpec((seq, hg * LANES), lambda b, qi, gi: (b, gi)),
                  pl.BlockSpec((seq, vw), lambda b, qi, gi: (b, 2 * ng + gi)), tile],
        out_specs=[tile, tile, pl.BlockSpec((1, HEADS, TQ), lambda b, qi, gi: (b * nq + qi, 0, 0))],
        out_shape=[jax.ShapeDtypeStruct((t, D), f32), jax.ShapeDtypeStruct((t, D), bf16),
                   jax.ShapeDtypeStruct((t // TQ, HEADS, TQ), f32)],
        scratch_shapes=[pltpu.VMEM((hg, LANES, TQ), f32)],
        compiler_params=_params(("parallel", "parallel", "arbitrary"), VMEM_LIMIT),
    )(qa, ka, qkv, rest)


def _shifted_rows(x, top8, prev8, shift, row, row8):
    body = pltpu.roll(x, shift, 0)
    head = jnp.where(row8 < shift, pltpu.roll(prev8, shift, 0), pltpu.roll(top8, shift, 0))
    return body, head


def _rnn_gates(xc, wa_ref, wx_ref, ba_ref, bx_ref, lam_ref):
    xcb = xc.astype(bf16)
    r = _sig(_dot(xcb, wa_ref[...]) + ba_ref[...])
    i = _sig(_dot(xcb, wx_ref[...]) + bx_ref[...])
    sp = _softplus(-lam_ref[...])
    log_a = (-RG_C) * r * sp
    th = jnp.tanh(log_a)
    w1 = (-2.0) * th / (1.0 - th)
    sq = jnp.sqrt(jnp.maximum(w1, 0.0))
    return r, i, sp, log_a, w1, sq


def _conv_tile(x_ref, xprev_ref, has_prev, cw_ref, cb_ref, xc_ref):
    row = lax.broadcasted_iota(jnp.int32, (TL, D), 0)
    row8 = lax.broadcasted_iota(jnp.int32, (8, D), 0)
    x = x_ref[...]
    top8 = x_ref[pl.ds(0, 8), :]
    prev8 = jnp.where(has_prev, xprev_ref[...], 0.0)
    xc = cb_ref[...] + cw_ref[pl.ds(3, 1), :] * x
    xc8 = cb_ref[...] + cw_ref[pl.ds(3, 1), :] * top8
    for sh in range(1, 4):
        w = cw_ref[pl.ds(3 - sh, 1), :]
        xs, xs8 = _shifted_rows(x, top8, prev8, sh, row, row8)
        xc = xc + w * xs
        xc8 = xc8 + w * xs8
    xc_ref[...] = xc
    xc_ref[pl.ds(0, 8), :] = xc8


def _rnn_fwd(rest, conv_w, conv_b, wa_d, wx_d, ba, bx, lam, seq):
    t = rest.shape[0]
    nb, nt = t // seq, seq // TL

    def body(x_ref, xprev_ref, gr_ref, cw_ref, cb_ref, wa_ref, wx_ref, ba_ref, bx_ref, lam_ref,
             xc_ref, a_ref, h_ref, pr_ref, u_scr, carry):
        tt = pl.program_id(1)
        _conv_tile(x_ref, xprev_ref, tt > 0, cw_ref, cb_ref, xc_ref)
        xc = xc_ref[...]
        r, i, sp, log_a, w1, sq = _rnn_gates(xc, wa_ref, wx_ref, ba_ref, bx_ref, lam_ref)
        a_ref[...] = jnp.exp(log_a)
        u_scr[...] = sq * (i * xc)

        @pl.when(tt == 0)
        def _():
            carry[...] = jnp.zeros_like(carry)

        def step(s, h):
            h = a_ref[pl.ds(s, 1), :] * h + u_scr[pl.ds(s, 1), :]
            h_ref[pl.ds(s, 1), :] = h
            return h

        carry[...] = lax.fori_loop(0, TL, step, carry[...], unroll=8)
        gr = gr_ref[...]
        pr_ref[...] = (h_ref[...] * (gr * _sig(gr))).astype(bf16)

    tile = lambda cb: pl.BlockSpec((TL, D), lambda b, tt, cb=cb: (b * nt + tt, cb))
    prev = lambda cb: pl.BlockSpec((8, D), lambda b, tt, cb=cb: (jnp.maximum((b * nt + tt) * (TL // 8) - 1, 0), cb))
    vec = _whole((1, D))
    return pl.pallas_call(
        body, name="rnn_fwd", grid=(nb, nt),
        in_specs=[tile(1), prev(1), tile(2), _whole((4, D)), vec, _whole((D, D)), _whole((D, D)), vec, vec, vec],
        out_specs=[tile(0)] * 4,
        out_shape=[jax.ShapeDtypeStruct((t, D), f32)] * 3 + [jax.ShapeDtypeStruct((t, D), bf16)],
        scratch_shapes=[pltpu.VMEM((TL, D), f32), pltpu.VMEM((1, D), f32)],
        compiler_params=_params(("parallel", "arbitrary"), VMEM_LIMIT),
    )(rest, rest, rest, conv_w, conv_b, wa_d, wx_d, ba, bx, lam)


def _merge(mga, mgr, ya, yr):
    return (_sig(mga) * ya + _sig(mgr) * yr).astype(bf16)


def _post_loss(o, x, tgt, w_post):
    t = o.shape[0]

    def body(o_ref, x_ref, t_ref, w_ref, do_ref, dy_ref, loss_ref, dwp_ref):
        @pl.when(pl.program_id(0) == 0)
        def _():
            loss_ref[...] = jnp.zeros_like(loss_ref)
            dwp_ref[...] = jnp.zeros_like(dwp_ref)

        ov = o_ref[...]
        w = w_ref[...]
        r2 = lax.rsqrt(jnp.mean(ov * ov, axis=-1, keepdims=True) + NORM_EPS)
        oh = ov * r2
        e = x_ref[...] + oh * w - t_ref[...]
        loss_ref[...] += 0.5 * jnp.sum(jnp.mean(e * e, axis=-1, keepdims=True))
        dy = e * (1.0 / D)
        dy_ref[...] = dy
        dwp_ref[...] += jnp.sum(dy * oh, axis=0, keepdims=True)
        doh = dy * w
        do_ref[...] = (r2 * (doh - oh * jnp.mean(doh * oh, axis=-1, keepdims=True))).astype(bf16)

    return pl.pallas_call(
        body, name="post_loss", grid=(t // TM,),
        in_specs=[_tile(TM, D)] * 3 + [_whole((1, D))],
        out_specs=[_tile(TM, D), _tile(TM, D), _whole((8, LANES)), _whole((1, D))],
        out_shape=[jax.ShapeDtypeStruct((t, D), bf16), jax.ShapeDtypeStruct((t, D), f32),
                   jax.ShapeDtypeStruct((8, LANES), f32), jax.ShapeDtypeStruct((1, D), f32)],
        compiler_params=_params(("arbitrary",)),
    )(o, x, tgt, w_post)


def _out_bwd(do, rest, ya, yr, w_out):
    t = do.shape[0]

    def body(do_ref, mga_ref, mgr_ref, ya_ref, yr_ref, w_ref, dya_ref, dyr_ref, dmga_ref, dmgr_ref, dw_ref):
        @pl.when(pl.program_id(0) == 0)
        def _():
            dw_ref[...] = jnp.zeros_like(dw_ref)

        dov = do_ref[...]
        sa, sr = _sig(mga_ref[...]), _sig(mgr_ref[...])
        ya, yr = ya_ref[...], yr_ref[...]
        dw_ref[...] += _dot_tn((sa * ya + sr * yr).astype(bf16), dov)
        dm = _dot_nt(dov, w_ref[...])
        dya_ref[...] = (dm * sa).astype(bf16)
        dyr_ref[...] = (dm * sr).astype(bf16)
        dmga_ref[...] = (dm * ya * sa * (1.0 - sa)).astype(bf16)
        dmgr_ref[...] = (dm * yr * sr * (1.0 - sr)).astype(bf16)

    return pl.pallas_call(
        body, name="out_bwd", grid=(t // TM,),
        in_specs=[_tile(TM, D), _tile(TM, D, 3), _tile(TM, D, 4), _tile(TM, D), _tile(TM, D), _whole((D, D))],
        out_specs=[_tile(TM, D)] * 4 + [_whole((D, D))],
        out_shape=[jax.ShapeDtypeStruct((t, D), bf16)] * 4 + [jax.ShapeDtypeStruct((D, D), f32)],
        compiler_params=_params(("arbitrary",), VMEM_LIMIT),
    )(do, rest, rest, ya, yr, w_out)


def _branch_bwd(name, dyb, rest, gate_cb, act, pact, w, act_grad_dtype):
    t = dyb.shape[0]

    def body(dy_ref, g_ref, act_ref, pact_ref, w_ref, dact_ref, dg_ref, dw_ref):
        @pl.when(pl.program_id(0) == 0)
        def _():
            dw_ref[...] = jnp.zeros_like(dw_ref)

        dyv = dy_ref[...]
        dw_ref[...] += _dot_tn(pact_ref[...], dyv)
        dp = _dot_nt(dyv, w_ref[...])
        g = g_ref[...]
        sg = _sig(g)
        dact_ref[...] = (dp * (g * sg)).astype(act_grad_dtype)
        dg_ref[...] = (dp * act_ref[...] * (sg * (1.0 + g * (1.0 - sg)))).astype(bf16)

    return pl.pallas_call(
        body, name=name, grid=(t // TM,),
        in_specs=[_tile(TM, D), _tile(TM, D, gate_cb), _tile(TM, D), _tile(TM, D), _whole((D, D))],
        out_specs=[_tile(TM, D), _tile(TM, D), _whole((D, D))],
        out_shape=[jax.ShapeDtypeStruct((t, D), act_grad_dtype), jax.ShapeDtypeStruct((t, D), bf16),
                   jax.ShapeDtypeStruct((D, D), f32)],
        compiler_params=_params(("arbitrary",), VMEM_LIMIT),
    )(dyb, rest, act, pact, w)


def _rnn_bwd(dh, a, h, xc, rest, conv_w, conv_b, wa_d, wx_d, ba, bx, lam, seq):
    t = dh.shape[0]
    nb, nt = t // seq, seq // TL

    def body(dh_ref, a_ref, h_ref, hprev_ref, xc_ref, x_ref, xprev_ref, cw_ref, cb_ref, wa_ref, wx_ref,
             ba_ref, bx_ref, lam_ref, dxr_ref, dwa_ref, dwx_ref, vec_ref, g_scr, dxc_scr, dxr_scr, qcarry, dxc_next):
        b, tt = pl.program_id(0), pl.program_id(1)
        rt = nt - 1 - tt

        @pl.when((b == 0) & (tt == 0))
        def _():
            dwa_ref[...] = jnp.zeros_like(dwa_ref)
            dwx_ref[...] = jnp.zeros_like(dwx_ref)
            vec_ref[...] = jnp.zeros_like(vec_ref)

        @pl.when(tt == 0)
        def _():
            qcarry[...] = jnp.zeros_like(qcarry)
            dxc_next[...] = jnp.zeros_like(dxc_next)

        def step(k, q):
            s = TL - 1 - k
            g = dh_ref[pl.ds(s, 1), :] + q
            g_scr[pl.ds(s, 1), :] = g
            return a_ref[pl.ds(s, 1), :] * g

        qcarry[...] = lax.fori_loop(0, TL, step, qcarry[...], unroll=8)

        row = lax.broadcasted_iota(jnp.int32, (TL, D), 0)
        row8 = lax.broadcasted_iota(jnp.int32, (8, D), 0)
        g = g_scr[...]
        av = a_ref[...]
        xc = xc_ref[...]
        hlast = jnp.where(rt > 0, hprev_ref[pl.ds(7, 1), :], 0.0)
        hp = jnp.where(row == 0, hlast, pltpu.roll(h_ref[...], 1, 0))
        r, i, sp, log_a, w1, sq = _rnn_gates(xc, wa_ref, wx_ref, ba_ref, bx_ref, lam_ref)
        dix = g * sq
        di = dix * xc
        dxc = dix * i
        dsq = g * (i * xc)
        dlog_a = g * hp * av - dsq * jnp.where(sq > 0.0, (1.0 - w1) / sq, 0.0)
        dpr = (dlog_a * ((-RG_C) * sp)) * r * (1.0 - r)
        dpi = di * i * (1.0 - i)
        dprb, dpib, xcb = dpr.astype(bf16), dpi.astype(bf16), xc.astype(bf16)
        dxc = dxc + _dot_nt(dprb, wa_ref[...]) + _dot_nt(dpib, wx_ref[...])
        dwa_ref[...] += _dot_tn(xcb, dprb)
        dwx_ref[...] += _dot_tn(xcb, dpib)
        vec_ref[pl.ds(0, 1), :] += jnp.sum(dpr, axis=0, keepdims=True)
        vec_ref[pl.ds(1, 1), :] += jnp.sum(dpi, axis=0, keepdims=True)
        dsp = jnp.sum(dlog_a * ((-RG_C) * r), axis=0, keepdims=True)
        vec_ref[pl.ds(2, 1), :] += dsp * (-_sig(-lam_ref[...]))
        vec_ref[pl.ds(3, 1), :] += jnp.sum(dxc, axis=0, keepdims=True)

        dxc_scr[...] = dxc
        bot8 = dxc_scr[pl.ds(TL - 8, 8), :]
        nxt8 = dxc_next[...]
        dxr = cw_ref[pl.ds(3, 1), :] * dxc
        dxr8 = cw_ref[pl.ds(3, 1), :] * bot8
        for sh in range(1, 4):
            w = cw_ref[pl.ds(3 - sh, 1), :]
            dxr = dxr + w * pltpu.roll(dxc, TL - sh, 0)
            dxr8 = dxr8 + w * jnp.where(row8 < 8 - sh, pltpu.roll(bot8, 8 - sh, 0), pltpu.roll(nxt8, 8 - sh, 0))
        dxr_scr[...] = dxr
        dxr_scr[pl.ds(TL - 8, 8), :] = dxr8
        dxr_ref[...] = dxr_scr[...].astype(bf16)
        dxc_next[...] = dxc_scr[pl.ds(0, 8), :]

        x = x_ref[...]
        prev8 = jnp.where(rt > 0, xprev_ref[...], 0.0)
        dxc_top8 = dxc_scr[pl.ds(0, 8), :]
        vec_ref[pl.ds(7, 1), :] += jnp.sum(dxc * x, axis=0, keepdims=True)
        for sh in range(1, 4):
            inside = jnp.sum(dxc * jnp.where(row >= sh, pltpu.roll(x, sh, 0), 0.0), axis=0, keepdims=True)
            above = jnp.sum(dxc_top8 * jnp.where(row8 < sh, pltpu.roll(prev8, sh, 0), 0.0), axis=0, keepdims=True)
            vec_ref[pl.ds(7 - sh, 1), :] += inside + above

    tile = lambda cb: pl.BlockSpec((TL, D), lambda b, tt, cb=cb: (b * nt + nt - 1 - tt, cb))
    prev = lambda cb: pl.BlockSpec(
        (8, D), lambda b, tt, cb=cb: (jnp.maximum((b * nt + nt - 1 - tt) * (TL // 8) - 1, 0), cb))
    vec = _whole((1, D))
    return pl.pallas_call(
        body, name="rnn_bwd", grid=(nb, nt),
        in_specs=[tile(0), tile(0), tile(0), prev(0), tile(0), tile(1), prev(1),
                  _whole((4, D)), vec, _whole((D, D)), _whole((D, D)), vec, vec, vec],
        out_specs=[tile(0), _whole((D, D)), _whole((D, D)), _whole((8, D))],
        out_shape=[jax.ShapeDtypeStruct((t, D), bf16), jax.ShapeDtypeStruct((D, D), f32),
                   jax.ShapeDtypeStruct((D, D), f32), jax.ShapeDtypeStruct((8, D), f32)],
        scratch_shapes=[pltpu.VMEM((TL, D), f32), pltpu.VMEM((TL, D), f32), pltpu.VMEM((TL, D), f32),
                        pltpu.VMEM((1, D), f32), pltpu.VMEM((8, D), f32)],
        compiler_params=_params(("arbitrary", "arbitrary"), VMEM_LIMIT),
    )(dh, a, h, h, xc, rest, rest, conv_w, conv_b, wa_d, wx_d, ba, bx, lam)


def _attn_delta(doa, o):
    t = doa.shape[0]

    def body(do_ref, o_ref, d_ref):
        prod = do_ref[...].astype(f32) * o_ref[...]
        ch = lax.broadcasted_iota(jnp.int32, (D, LANES), 0)
        hd = lax.broadcasted_iota(jnp.int32, (D, LANES), 1)
        pick = (ch // 64 == hd).astype(f32)
        per_head = jnp.dot(prod, pick, preferred_element_type=f32, precision=lax.Precision.HIGHEST)
        d_ref[0] = per_head.T[:HEADS, :]

    return pl.pallas_call(
        body, name="attn_delta", grid=(t // TQ,),
        in_specs=[_tile(TQ, D), _tile(TQ, D)],
        out_specs=pl.BlockSpec((1, HEADS, TQ), lambda i: (i, 0, 0)),
        out_shape=jax.ShapeDtypeStruct((t // TQ, HEADS, TQ), f32),
        compiler_params=_params(("parallel",)),
    )(doa, o)


def _attn_bwd(qa, ka, qkv, doa, lse, delta, seq):
    t = qkv.shape[0]
    nb, nq = t // seq, seq // TQ
    hg = ATT_GROUP
    ng, npair = HEADS // hg, hg // 2

    def body(qa_ref, ka_ref, q_ref, k_ref, v_ref, do_ref, lse_ref, dl_ref, dq_ref, dk_ref, dv_ref, dc_ref,
             dqt_scr, dk_scr, dv_scr, ds_scr, kht_scr):
        gi, kt = pl.program_id(1), pl.program_id(2)
        lane = lax.broadcasted_iota(jnp.int32, (1, LANES), 1)
        krow = lax.broadcasted_iota(jnp.int32, (TQ, TQ), 0)
        qcol = lax.broadcasted_iota(jnp.int32, (TQ, TQ), 1)
        lmask = [(lane // 64) == hh for hh in range(2)]
        scale = jnp.asarray(QK_SCALE, bf16)

        @pl.when(kt == 0)
        def _():
            dqt_scr[...] = jnp.zeros_like(dqt_scr)

        dk_scr[...] = jnp.zeros_like(dk_scr)
        dv_scr[...] = jnp.zeros_like(dv_scr)
        ds_scr[...] = jnp.zeros_like(ds_scr)
        for g in range(hg):
            k2 = k_ref[:, pl.ds((g // 2) * LANES, LANES)]
            kht_scr[g] = jnp.where(lmask[g % 2], k2, jnp.zeros_like(k2)).T

        def q_step(qt, masked):
            qs = pl.multiple_of(qt * TQ, TQ)
            heads = range(hg)
            do2 = [do_ref[pl.ds(qs, TQ), pl.ds(j * LANES, LANES)] for j in range(npair)]
            q2 = [q_ref[pl.ds(qs, TQ), pl.ds(j * LANES, LANES)] for j in range(npair)]
            doh = [jnp.where(lmask[g % 2], do2[g // 2], jnp.zeros_like(do2[0])) for g in heads]
            qh = [jnp.where(lmask[g % 2], q2[g // 2], jnp.zeros_like(q2[0])) * scale for g in heads]
            st = [_dot_nt(ka_ref[:, pl.ds(g * LANES, LANES)], qa_ref[pl.ds(qs, TQ), pl.ds(g * LANES, LANES)])
                  for g in heads]
            if masked:
                st = [jnp.where(krow <= qcol, s, MASK_VALUE) for s in st]
            dp = [_dot_nt(v_ref[:, pl.ds((g // 2) * LANES, LANES)], doh[g]) for g in heads]
            p = [jnp.exp(st[g] - lse_ref[qt, pl.ds(hg * gi + g, 1), :]) for g in heads]
            ds = [p[g] * (dp[g] - dl_ref[qt, pl.ds(hg * gi + g, 1), :]) for g in heads]
            pb = [x.astype(bf16) for x in p]
            dsb = [x.astype(bf16) for x in ds]
            for j in range(npair):
                a, b = 2 * j, 2 * j + 1
                dv_scr[j] += _dot(pb[a], doh[a]) + _dot(pb[b], doh[b])
                dk_scr[j] += _dot(dsb[a], qh[a]) + _dot(dsb[b], qh[b])
                dqt_scr[qt, j] += (_dot(kht_scr[a], dsb[a]) + _dot(kht_scr[b], dsb[b])) * QK_SCALE
            for g in heads:
                ds_scr[g] += ds[g][:, :LANES] + ds[g][:, LANES:]

        q_step(kt, True)

        def loop_body(qt, carry):
            q_step(qt, False)
            return carry

        lax.fori_loop(kt + 1, nq, loop_body, 0)

        dc = jnp.zeros((TQ, LANES), f32)
        for g in range(hg):
            dc = jnp.where(lane == g, -jnp.sum(ds_scr[g], axis=1, keepdims=True), dc)
        dc_ref[...] = dc
        for j in range(npair):
            dk_ref[:, pl.ds(j * LANES, LANES)] = dk_scr[j].astype(bf16)
            dv_ref[:, pl.ds(j * LANES, LANES)] = dv_scr[j].astype(bf16)

        @pl.when(kt == nq - 1)
        def _():
            for qt in range(nq):
                for j in range(npair):
                    dq_ref[pl.ds(qt * TQ, TQ), pl.ds(j * LANES, LANES)] = dqt_scr[qt, j].T.astype(bf16)

    vw = hg * 64
    seqspec = pl.BlockSpec((seq, vw), lambda b, gi, kt: (b, gi))
    kspec = lambda off: pl.BlockSpec((TQ, vw), lambda b, gi, kt: (b * nq + kt, off + gi))
    rowspec = pl.BlockSpec((nq, HEADS, TQ), lambda b, gi, kt: (b, 0, 0))
    return pl.pallas_call(
        body, name="attn_bwd", grid=(nb, ng, nq),
        in_specs=[pl.BlockSpec((seq, hg * LANES), lambda b, gi, kt: (b, gi)),
                  pl.BlockSpec((TQ, hg * LANES), lambda b, gi, kt: (b * nq + kt, gi)),
                  seqspec, kspec(ng), kspec(2 * ng), seqspec, rowspec, rowspec],
        out_specs=[seqspec, kspec(0), kspec(0), pl.BlockSpec((TQ, LANES), lambda b, gi, kt: (b * nq + kt, gi))],
        out_shape=[jax.ShapeDtypeStruct((t, D), bf16)] * 3 + [jax.ShapeDtypeStruct((t, ng * LANES), f32)],
        scratch_shapes=[pltpu.VMEM((nq, npair, LANES, TQ), f32), pltpu.VMEM((npair, TQ, LANES), f32),
                        pltpu.VMEM((npair, TQ, LANES), f32), pltpu.VMEM((hg, TQ, LANES), f32),
                        pltpu.VMEM((hg, LANES, TQ), bf16)],
        compiler_params=_params(("parallel", "parallel", "arbitrary"), VMEM_LIMIT),
    )(qa, ka, qkv, qkv, qkv, doa, lse, delta)


def _forget_bwd(dc, f128, seq):
    t = f128.shape[0]
    nb = seq // LANES

    def body(dc_ref, f_ref, df_ref, dbf_ref):
        @pl.when(pl.program_id(0) == 0)
        def _():
            dbf_ref[...] = jnp.zeros_like(dbf_ref)

        r = lax.broadcasted_iota(jnp.int32, (LANES, LANES), 0)
        cidx = lax.broadcasted_iota(jnp.int32, (LANES, LANES), 1)
        tri = (r <= cidx).astype(f32)
        carry = jnp.zeros((1, LANES), f32)
        total = jnp.zeros((1, LANES), f32)
        for blk in reversed(range(nb)):
            dcb = dc_ref[pl.ds(blk * LANES, LANES), :]
            dlf = jnp.dot(tri, dcb, preferred_element_type=f32, precision=lax.Precision.HIGHEST) + carry
            df = dlf * _sig(-f_ref[pl.ds(blk * LANES, LANES), :])
            df_ref[pl.ds(blk * LANES, LANES), :] = df.astype(bf16)
            total = total + jnp.sum(df, axis=0, keepdims=True)
            carry = carry + jnp.sum(dcb, axis=0, keepdims=True)
        dbf_ref[...] += total

    return pl.pallas_call(
        body, name="forget_bwd", grid=(t // seq,),
        in_specs=[pl.BlockSpec((seq, LANES), lambda b: (b, 0)), pl.BlockSpec((seq, LANES), lambda b: (b, 0))],
        out_specs=[pl.BlockSpec((seq, LANES), lambda b: (b, 0)), _whole((1, LANES))],
        out_shape=[jax.ShapeDtypeStruct((t, LANES), bf16), jax.ShapeDtypeStruct((1, LANES), f32)],
        compiler_params=_params(("arbitrary",)),
    )(dc, f128)


def _in_bwd(dz, df, x, dy, w_qkv, w_rest, w_f, w_pre):
    t = x.shape[0]
    n_qkv = w_qkv.shape[1] // D
    n_rest = w_rest.shape[1] // D

    def body(*refs):
        dz_refs = refs[:n_qkv + n_rest]
        df_ref, x_ref, dy_ref, wq_ref, wr_ref, wf_ref, wp_ref, gx_ref, dwp_ref = refs[n_qkv + n_rest:]

        @pl.when(pl.program_id(0) == 0)
        def _():
            dwp_ref[...] = jnp.zeros_like(dwp_ref)

        dh = _dot_nt(df_ref[...], wf_ref[...])
        for p in range(n_qkv):
            dh = dh + _dot_nt(dz_refs[p][...], wq_ref[:, pl.ds(p * D, D)])
        for p in range(n_rest):
            dh = dh + _dot_nt(dz_refs[n_qkv + p][...], wr_ref[:, pl.ds(p * D, D)])
        xv = x_ref[...]
        r1 = lax.rsqrt(jnp.mean(xv * xv, axis=-1, keepdims=True) + NORM_EPS)
        xh = xv * r1
        dwp_ref[...] += jnp.sum(dh * xh, axis=0, keepdims=True)
        dxh = dh * wp_ref[...]
        gx_ref[...] = dy_ref[...] + r1 * (dxh - xh * jnp.mean(dxh * xh, axis=-1, keepdims=True))

    once = lambda shape: pl.BlockSpec(shape, lambda i: (0, 0), pipeline_mode=pl.Buffered(1))
    return pl.pallas_call(
        body, name="in_bwd", grid=(t // TM,),
        in_specs=[_tile(TM, D)] * (n_qkv + n_rest) + [_tile(TM, LANES), _tile(TM, D), _tile(TM, D),
                  once(w_qkv.shape), once(w_rest.shape), once(w_f.shape), _whole((1, D))],
        out_specs=[_tile(TM, D), _whole((1, D))],
        out_shape=[jax.ShapeDtypeStruct((t, D), f32), jax.ShapeDtypeStruct((1, D), f32)],
        compiler_params=_params(("arbitrary",), VMEM_LIMIT),
    )(*dz, df, x, dy, w_qkv, w_rest, w_f, w_pre)


def _tn_mm(name, a, b, tn, tk=512):
    t, k = a.shape
    tk = min(tk, t)
    n = b.shape[1]

    def body(a_ref, b_ref, o_ref, s_ref):
        @pl.when(pl.program_id(1) == 0)
        def _():
            o_ref[...] = jnp.zeros_like(o_ref)
            s_ref[...] = jnp.zeros_like(s_ref)

        bv = b_ref[...]
        o_ref[...] += _dot_tn(a_ref[...], bv)
        s_ref[...] += jnp.sum(bv.astype(f32), axis=0, keepdims=True)

    return pl.pallas_call(
        body, name=name, grid=(n // tn, t // tk),
        in_specs=[pl.BlockSpec((tk, k), lambda j, kk: (kk, 0)), pl.BlockSpec((tk, tn), lambda j, kk: (kk, j))],
        out_specs=[pl.BlockSpec((k, tn), lambda j, kk: (0, j)), pl.BlockSpec((1, tn), lambda j, kk: (0, j))],
        out_shape=[jax.ShapeDtypeStruct((k, n), f32), jax.ShapeDtypeStruct((1, n), f32)],
        compiler_params=_params(("parallel", "arbitrary"), VMEM_LIMIT),
    )(a, b)


def _position():
    return lax.axis_index("x"), lax.axis_index("y"), lax.axis_index("c")


def _gather_shards(parts):
    n = len(parts)

    def body(*refs):
        srcs, dsts = refs[:n], refs[n:2 * n]
        send, recv, local = refs[2 * n:]
        x, y, c = _position()
        me = 2 * x + y
        chips = [(1 - x, y), (x, 1 - y), (1 - x, 1 - y)]

        def remote(a, j, landing):
            px, py = chips[j]
            return pltpu.make_async_remote_copy(
                src_ref=srcs[a], dst_ref=dsts[a].at[landing], send_sem=send.at[a * 3 + j], recv_sem=recv.at[a * 3 + j],
                device_id=(px, py, c), device_id_type=MESH)

        own = [pltpu.make_async_copy(srcs[a], dsts[a].at[me], local.at[a]) for a in range(n)]
        for cp in own:
            cp.start()
        sends = [remote(a, j, me) for j in range(3) for a in range(n)]
        for cp in sends:
            cp.start()
        for j, (px, py) in enumerate(chips):
            for a in range(n):
                remote(a, j, 2 * px + py).wait_recv()
        for cp in sends:
            cp.wait_send()
        for cp in own:
            cp.wait()

    anyspec = pl.BlockSpec(memory_space=pl.ANY)
    return pl.pallas_call(
        body, name="gather_shards",
        in_specs=[anyspec] * n, out_specs=[anyspec] * n,
        out_shape=[jax.ShapeDtypeStruct((N_CHIPS,) + p.shape, p.dtype) for p in parts],
        scratch_shapes=[pltpu.SemaphoreType.DMA((3 * n,)), pltpu.SemaphoreType.DMA((3 * n,)),
                        pltpu.SemaphoreType.DMA((n,))],
    )(*parts)


def _allsum_rows(part):
    rows_n = part.shape[0]

    def body(x_ref, gath_ref, sum_ref, send_sems, recv_sems, local_sem):
        x, y, c = _position()
        me, sibling = (x, y, c), (x, y, 1 - c)
        chips = [(1 - x, y), (x, 1 - y), (1 - x, 1 - y)]

        def rows(px, py, pc):
            return gath_ref.at[pl.ds((4 * px + 2 * py + pc) * rows_n, rows_n), :]

        def copy(k, block, to, src=None):
            return pltpu.make_async_remote_copy(
                src_ref=rows(*block) if src is None else src, dst_ref=rows(*block),
                send_sem=send_sems.at[k], recv_sem=recv_sems.at[k], device_id=to, device_id_type=MESH)

        mine = pltpu.make_async_copy(x_ref, rows(*me), local_sem)
        mine.start()
        first = [copy(0, me, sibling, src=x_ref)]
        first += [copy(1 + j, me, (*chip, c), src=x_ref) for j, chip in enumerate(chips)]
        for cp in first:
            cp.start()
        passed = [copy(4 + j, (*chip, c), sibling) for j, chip in enumerate(chips)]
        for j, chip in enumerate(chips):
            copy(1 + j, (*chip, c), me).wait_recv()
            passed[j].start()
        copy(0, sibling, me).wait_recv()
        for j, chip in enumerate(chips):
            copy(4 + j, (*chip, 1 - c), me).wait_recv()
        for cp in first + passed:
            cp.wait_send()
        mine.wait()
        total = gath_ref[pl.ds(0, rows_n), :]
        for d in range(1, N_DEV):
            total = total + gath_ref[pl.ds(d * rows_n, rows_n), :]
        sum_ref[...] = total

    vm = pl.BlockSpec(memory_space=pltpu.VMEM)
    return pl.pallas_call(
        body, name="allsum_rows", in_specs=[vm], out_specs=[vm, vm],
        out_shape=[jax.ShapeDtypeStruct((N_DEV * rows_n, D), f32), jax.ShapeDtypeStruct((rows_n, D), f32)],
        scratch_shapes=[pltpu.SemaphoreType.DMA((7,)), pltpu.SemaphoreType.DMA((7,)), pltpu.SemaphoreType.DMA],
    )(part)[1]


def _exchange_pieces(arrs):
    n = len(arrs)

    def body(*refs):
        srcs, dsts = refs[:n], refs[n:2 * n]
        send, recv, local = refs[2 * n:]
        x, y, c = _position()
        me = 4 * x + 2 * y + c
        peers = []
        for mask in range(1, N_DEV):
            px = 1 - x if mask & 4 else x
            py = 1 - y if mask & 2 else y
            pc = 1 - c if mask & 1 else c
            peers.append((px, py, pc))

        def remote(a, k, piece, landing):
            return pltpu.make_async_remote_copy(
                src_ref=srcs[a].at[piece], dst_ref=dsts[a].at[landing], send_sem=send.at[a * 7 + k],
                recv_sem=recv.at[a * 7 + k], device_id=peers[k], device_id_type=MESH)

        own = [pltpu.make_async_copy(srcs[a].at[me], dsts[a].at[me], local.at[a]) for a in range(n)]
        for cp in own:
            cp.start()
        sends = []
        for k, (px, py, pc) in enumerate(peers):
            for a in range(n):
                sends.append(remote(a, k, 4 * px + 2 * py + pc, me))
        for cp in sends:
            cp.start()
        for k, (px, py, pc) in enumerate(peers):
            for a in range(n):
                remote(a, k, me, 4 * px + 2 * py + pc).wait_recv()
        for cp in sends:
            cp.wait_send()
        for cp in own:
            cp.wait()

    anyspec = pl.BlockSpec(memory_space=pl.ANY)
    return pl.pallas_call(
        body, name="exchange_pieces", in_specs=[anyspec] * n, out_specs=[anyspec] * n,
        out_shape=[jax.ShapeDtypeStruct(a.shape, a.dtype) for a in arrs],
        scratch_shapes=[pltpu.SemaphoreType.DMA((7 * n,)), pltpu.SemaphoreType.DMA((7 * n,)),
                        pltpu.SemaphoreType.DMA((n,))],
    )(*arrs)


def _swap_halves(arrs):
    n = len(arrs)

    def body(*refs):
        srcs, dsts = refs[:n], refs[n:2 * n]
        send, recv, local = refs[2 * n:]
        x, y, c = _position()

        def remote(a, landing):
            return pltpu.make_async_remote_copy(
                src_ref=srcs[a], dst_ref=dsts[a].at[landing], send_sem=send.at[a], recv_sem=recv.at[a],
                device_id=(x, y, 1 - c), device_id_type=MESH)

        own = [pltpu.make_async_copy(srcs[a], dsts[a].at[c], local.at[a]) for a in range(n)]
        sends = [remote(a, c) for a in range(n)]
        for cp in own + sends:
            cp.start()
        for a in range(n):
            remote(a, 1 - c).wait_recv()
        for cp in sends:
            cp.wait_send()
        for cp in own:
            cp.wait()

    anyspec = pl.BlockSpec(memory_space=pl.ANY)
    return pl.pallas_call(
        body, name="swap_halves", in_specs=[anyspec] * n, out_specs=[anyspec] * n,
        out_shape=[jax.ShapeDtypeStruct((2,) + a.shape, a.dtype) for a in arrs],
        scratch_shapes=[pltpu.SemaphoreType.DMA((n,)), pltpu.SemaphoreType.DMA((n,)), pltpu.SemaphoreType.DMA((n,))],
    )(*arrs)


def _row_block(r):
    return 128 if r % 128 == 0 else r


def _sum_slots(name, slots):
    _, r, n = slots.shape
    rb = _row_block(r)

    def body(s_ref, o_ref):
        total = s_ref[0].astype(f32)
        for d in range(1, N_DEV):
            total = total + s_ref[d].astype(f32)
        o_ref[...] = total

    return pl.pallas_call(
        body, name=name, grid=(r // rb,),
        in_specs=[pl.BlockSpec((N_DEV, rb, n), lambda i: (0, i, 0))],
        out_specs=pl.BlockSpec((rb, n), lambda i: (i, 0)),
        out_shape=jax.ShapeDtypeStruct((r, n), f32),
        compiler_params=_params(("parallel",), VMEM_LIMIT),
    )(slots)


def _adamw(name, w, g, m, v):
    r, n = w.shape
    rb = _row_block(r)

    def body(w_ref, g_ref, m_ref, v_ref, d_ref, nm_ref, nv_ref):
        gv = g_ref[...]
        m2 = ADAM_B1 * m_ref[...] + (1.0 - ADAM_B1) * gv
        v2 = ADAM_B2 * v_ref[...] + (1.0 - ADAM_B2) * (gv * gv)
        m_hat = m2 / (1.0 - ADAM_B1 ** ADAM_STEP)
        v_hat = v2 / (1.0 - ADAM_B2 ** ADAM_STEP)
        d_ref[...] = (-ADAM_LR) * (m_hat / (jnp.sqrt(v_hat) + ADAM_EPS) + ADAM_WD * w_ref[...])
        nm_ref[...] = m2
        nv_ref[...] = v2

    spec = pl.BlockSpec((rb, n), lambda i: (i, 0))
    return pl.pallas_call(
        body, name=name, grid=(r // rb,), in_specs=[spec] * 4, out_specs=[spec] * 3,
        out_shape=[jax.ShapeDtypeStruct((r, n), f32)] * 3,
        compiler_params=_params(("parallel",), VMEM_LIMIT),
    )(w, g, m, v)


def _identity(a):
    return a


def _local_step(x2, tgt2, seq, wt):
    nb = x2.shape[0] // seq
    h = _prenorm(x2, wt["pre_w"])
    qkv = _mm("in_qkv", [(h, 0)], _identity, wt["w_qkv"], wt["b_qkv"], bf16, 512, 1024)
    rest = _mm("in_rest", [(h, 0)], _identity, wt["w_rest"], wt["b_rest"], f32, 512, 1024)
    f128 = _mm("in_f", [(h, 0)], _identity, wt["w_f"], wt["b_f"], f32, 512, LANES)
    c = _forget_prep(f128, seq)
    qa, ka = _attn_prep(qkv, c)
    o_att, pa, lse = _attn_fwd(qa, ka, qkv, rest, seq)
    ya = _mm("proj_a", [(pa, 0)], _identity, wt["w_a"], None, f32, 512, D)
    rnn_w = (wt["conv_w"], wt["conv_b"], wt["wa_d"], wt["wx_d"], wt["ba"], wt["bx"], wt["lam"])
    xc, a, hrec, pr = _rnn_fwd(rest, *rnn_w, seq)
    yr = _mm("proj_r", [(pr, 0)], _identity, wt["w_r"], None, f32, 512, D)
    o = _mm("proj_out", [(rest, 3), (rest, 4), (ya, 0), (yr, 0)], _merge, wt["w_o"], None, f32, TM, D)

    do, dy, loss8, d_post = _post_loss(o, x2, tgt2, wt["post_w"])
    dya, dyr, dmga, dmgr, d_wo = _out_bwd(do, rest, ya, yr, wt["w_o"])
    doa, dga, d_wa = _branch_bwd("branch_a_bwd", dya, rest, 0, o_att, pa, wt["w_a"], bf16)
    dhrec, dgr, d_wr = _branch_bwd("branch_r_bwd", dyr, rest, 2, hrec, pr, wt["w_r"], f32)
    dxr, d_wad, d_wxd, vec = _rnn_bwd(dhrec, a, hrec, xc, rest, *rnn_w, seq)
    dq, dk, dv, dc_pairs = _attn_bwd(qa, ka, qkv, doa, lse, _attn_delta(doa, o_att), seq)
    dc = dc_pairs.reshape(-1, HEADS // ATT_GROUP, LANES)[:, :, :ATT_GROUP].reshape(-1, HEADS)
    df, db_f = _forget_bwd(_pad_cols(dc, LANES), f128, seq)
    pieces = [dq, dk, dv, dga, dxr, dgr, dmga, dmgr]
    gx, d_pre = _in_bwd(pieces, df, x2, dy, wt["w_qkv"], wt["w_rest"], wt["w_f"], wt["pre_w"])
    names = ["q", "k", "v", "ga", "xr", "gr", "mga", "mgr"]
    dws, dbs = [], []
    for nm, piece in zip(names, pieces):
        dw_p, db_p = _tn_mm("dw_in_" + nm, h, piece, 512)
        dws.append(dw_p)
        dbs.append(db_p)
    dw_f, _ = _tn_mm("dw_in_f", h, df, LANES)
    zeros_w = jnp.zeros((D, IN_TOTAL - IN_USED), f32)
    d_w_in = jnp.concatenate(dws[:3] + [dw_f[:, :HEADS]] + dws[3:] + [zeros_w], axis=1)
    d_b_in = jnp.concatenate(dbs[:3] + [db_f[:, :HEADS]] + dbs[3:] + [zeros_w[:1]], axis=1)
    return dict(loss=loss8[0, 0], grad_x=gx, pre_w=d_pre, w_in=d_w_in, b_in=d_b_in, conv_w=vec[4:8], conv_b=vec[3:4],
                wa_d=d_wad, ba=vec[0:1], wx_d=d_wxd, bx=vec[1:2], lam=vec[2:3], w_a=d_wa, w_r=d_wr, w_o=d_wo,
                post_w=d_post)


def _block_diag(w):
    g, bw, _ = w.shape
    eye = jnp.eye(g, dtype=w.dtype)
    return (w[:, :, None, :] * eye[:, None, :, None]).reshape(g * bw, g * bw)


def _block_diag_of(dense, g):
    bw = dense.shape[0] // g
    eye = jnp.eye(g, dtype=dense.dtype)
    return jnp.sum(dense.reshape(g, bw, g, bw) * eye[:, None, :, None], axis=2)


def _pad_cols(a, n):
    return jnp.pad(a, ((0, 0), (0, n - a.shape[1])))


def _pad_rows(a, n):
    return jnp.pad(a, ((0, n - a.shape[0]), (0, 0)))


def kernel(x, pre_norm_w, w_in, b_in, conv_w, conv_b, rg_wa, rg_ba, rg_wx, rg_bx, rg_lambda, w_branch_a, w_branch_r, w_out, post_norm_w, loss_target, m_pre_norm_w, m_w_in, m_b_in, m_conv_w, m_conv_b, m_rg_wa, m_rg_ba, m_rg_wx, m_rg_bx, m_rg_lambda, m_w_branch_a, m_w_branch_r, m_w_out, m_post_norm_w, v_pre_norm_w, v_w_in, v_b_in, v_conv_w, v_conv_b, v_rg_wa, v_rg_ba, v_rg_wx, v_rg_bx, v_rg_lambda, v_w_branch_a, v_w_branch_r, v_w_out, v_post_norm_w):
    nb, seq, _ = x.shape
    chip = 2 * lax.axis_index("x") + lax.axis_index("y")
    n_groups = rg_wa.shape[1]

    g_in, g_a, g_r, g_o, g_cw = _gather_shards(
        [w_in[0].astype(bf16), w_branch_a[0].astype(bf16), w_branch_r[0].astype(bf16), w_out[0].astype(bf16), conv_w[0]])
    w_full = jnp.transpose(g_in, (1, 0, 2)).reshape(D, IN_TOTAL)
    q_end, f_end = 3 * D, 3 * D + HEADS
    wt = dict(
        pre_w=pre_norm_w, post_w=post_norm_w,
        w_qkv=w_full[:, :q_end], b_qkv=b_in[:, :q_end],
        w_f=_pad_cols(w_full[:, q_end:f_end], LANES), b_f=_pad_cols(b_in[:, q_end:f_end], LANES),
        w_rest=w_full[:, f_end:IN_USED], b_rest=b_in[:, f_end:IN_USED],
        w_a=g_a.reshape(D, D), w_r=g_r.reshape(D, D), w_o=g_o.reshape(D, D),
        conv_w=jnp.transpose(g_cw, (1, 0, 2)).reshape(4, D), conv_b=conv_b,
        wa_d=_block_diag(rg_wa[0]).astype(bf16), wx_d=_block_diag(rg_wx[0]).astype(bf16),
        ba=rg_ba, bx=rg_bx, lam=rg_lambda)

    part = _local_step(x.reshape(nb * seq, D), loss_target.reshape(nb * seq, D), seq, wt)
    loss = lax.psum(part["loss"], ("x", "y", "c"))
    grad_x = part["grad_x"].reshape(nb, seq, D)

    small = jnp.concatenate([
        part["pre_w"], _pad_cols(part["b_in"], 10 * D).reshape(10, D), part["conv_b"],
        _block_diag_of(part["wa_d"], n_groups).reshape(-1, D), part["ba"],
        _block_diag_of(part["wx_d"], n_groups).reshape(-1, D), part["bx"], part["lam"], part["post_w"],
        part["conv_w"]], axis=0)
    n_small = small.shape[0]
    n_rep = n_small - 4
    tot = _allsum_rows(_pad_rows(small, -(-n_small // 8) * 8))
    g_rep = tot[:n_rep]
    g_conv_w = lax.dynamic_slice_in_dim(tot[n_rep:n_small], chip * (D // N_CHIPS), D // N_CHIPS, axis=1)

    def pack(pre, b, cb, wa, ba, wx, bx, lam, post):
        return jnp.concatenate([pre, _pad_cols(b, 10 * D).reshape(10, D), cb, wa.reshape(-1, D), ba,
                                wx.reshape(-1, D), bx, lam, post], axis=0)

    def unpack(p):
        o = [0]

        def take(k):
            o[0] += k
            return p[o[0] - k:o[0]]

        pre = take(1)
        b = take(10).reshape(1, 10 * D)[:, :IN_TOTAL]
        cb = take(1)
        wa = take(64).reshape(rg_wa.shape)
        ba = take(1)
        wx = take(64).reshape(rg_wx.shape)
        bx = take(1)
        lam = take(1)
        post = take(1)
        return dict(pre_norm_w=pre, b_in=b, conv_b=cb, rg_wa=wa, rg_ba=ba, rg_wx=wx, rg_bx=bx, rg_lambda=lam,
                    post_norm_w=post)

    w_rep = pack(pre_norm_w, b_in, conv_b, rg_wa, rg_ba, rg_wx, rg_bx, rg_lambda, post_norm_w)
    m_rep = pack(m_pre_norm_w, m_b_in, m_conv_b, m_rg_wa, m_rg_ba, m_rg_wx, m_rg_bx, m_rg_lambda, m_post_norm_w)
    v_rep = pack(v_pre_norm_w, v_b_in, v_conv_b, v_rg_wa, v_rg_ba, v_rg_wx, v_rg_bx, v_rg_lambda, v_post_norm_w)
    d_rep, nm_rep, nv_rep = _adamw("adamw_rep", w_rep, g_rep, m_rep, v_rep)
    grads, deltas, new_m, new_v = unpack(g_rep), unpack(d_rep), unpack(nm_rep), unpack(nv_rep)

    shard_cols = IN_TOTAL // N_CHIPS
    p_in = jnp.transpose(part["w_in"].reshape(D, N_CHIPS, shard_cols), (1, 0, 2)).reshape(N_DEV, D // 2, shard_cols)
    p_aro = jnp.concatenate([part[k].reshape(N_DEV, D // N_DEV, D) for k in ("w_a", "w_r", "w_o")], axis=1)
    s_in, s_aro = _exchange_pieces([p_in.astype(bf16), p_aro.astype(bf16)])
    f_in, f_aro = _swap_halves([_sum_slots("sum_w_in", s_in), _sum_slots("sum_w_aro", s_aro)])
    g_w_in = f_in.reshape(D, shard_cols)
    rows = D // N_DEV
    g_aro = jnp.concatenate([f_aro[:, i * rows:(i + 1) * rows, :].reshape(2 * rows, D) for i in range(3)], axis=0)

    d_w_in, nm_w_in, nv_w_in = _adamw("adamw_w_in", w_in[0], g_w_in, m_w_in[0], v_w_in[0])
    stack = lambda a, b, c: jnp.concatenate([a[0], b[0], c[0]], axis=0)
    d_aro, nm_aro, nv_aro = _adamw("adamw_w_aro", stack(w_branch_a, w_branch_r, w_out), g_aro,
                                   stack(m_w_branch_a, m_w_branch_r, m_w_out),
                                   stack(v_w_branch_a, v_w_branch_r, v_w_out))
    d_cw, nm_cw, nv_cw = _adamw("adamw_conv_w", conv_w[0], g_conv_w, m_conv_w[0], v_conv_w[0])

    def sharded(t_in, t_aro, t_cw):
        r2 = 2 * rows
        return dict(w_in=t_in[None], conv_w=t_cw[None], w_branch_a=t_aro[None, :r2], w_branch_r=t_aro[None, r2:2 * r2],
                    w_out=t_aro[None, 2 * r2:])

    order = ["pre_norm_w", "w_in", "b_in", "conv_w", "conv_b", "rg_wa", "rg_ba", "rg_wx", "rg_bx", "rg_lambda",
             "w_branch_a", "w_branch_r", "w_out", "post_norm_w"]
    outs = [loss, grad_x]
    for rep, shd in ((grads, sharded(g_w_in, g_aro, g_conv_w)), (deltas, sharded(d_w_in, d_aro, d_cw)),
                     (new_m, sharded(nm_w_in, nm_aro, nm_cw)), (new_v, sharded(nv_w_in, nv_aro, nv_cw))):
        both = {**rep, **shd}
        outs.extend(both[k] for k in order)
    return tuple(outs)
```

```python
import jax
import jax.numpy as jnp
from jax import lax
from jax.experimental import pallas as pl
from jax.experimental.pallas import tpu as pltpu

f32 = jnp.float32
bf16 = jnp.bfloat16

D = 1024
HEADS = 16
HEAD_PAIRS = 8
LANES = 128
NORM_EPS = 1e-6
MASK_VALUE = -1e30
RG_C = 8.0
QK_SCALE = 0.125
TQ = 256
ATT_GROUP = 8
TL = 256
TM = 256
IN_USED = 8 * D + HEADS
IN_TOTAL = 9 * D + HEADS
N_CHIPS = 4
N_DEV = 8
ADAM_LR, ADAM_B1, ADAM_B2, ADAM_EPS, ADAM_WD, ADAM_STEP = 0.001, 0.9, 0.999, 1e-08, 0.01, 10
VMEM_LIMIT = 56 * 1024 * 1024
MESH = pl.DeviceIdType.MESH


def _dot(a, b):
    return jnp.dot(a, b, preferred_element_type=f32)


def _dot_nt(a, b):
    return lax.dot_general(a, b, (((1,), (1,)), ((), ())), preferred_element_type=f32)


def _dot_tn(a, b):
    return lax.dot_general(a, b, (((0,), (0,)), ((), ())), preferred_element_type=f32)


def _sig(x):
    return 1.0 / (1.0 + jnp.exp(-x))


def _softplus(x):
    return jnp.maximum(x, 0.0) + jnp.log(1.0 + jnp.exp(-jnp.abs(x)))


def _params(sem, vmem=None):
    return pltpu.CompilerParams(dimension_semantics=sem, vmem_limit_bytes=vmem)


def _tile(tm, width, cb=0):
    return pl.BlockSpec((tm, width), lambda i, cb=cb: (i, cb))


def _whole(shape):
    nd = len(shape)
    return pl.BlockSpec(shape, lambda *_: (0,) * nd)


def _prenorm(x, w_pre):
    t = x.shape[0]

    def body(x_ref, w_ref, h_ref):
        xv = x_ref[...]
        r = lax.rsqrt(jnp.mean(xv * xv, axis=-1, keepdims=True) + NORM_EPS)
        h_ref[...] = (xv * r * w_ref[...]).astype(bf16)

    return pl.pallas_call(
        body, name="prenorm", grid=(t // TM,),
        in_specs=[_tile(TM, D), _whole((1, D))], out_specs=_tile(TM, D),
        out_shape=jax.ShapeDtypeStruct((t, D), bf16),
        compiler_params=_params(("parallel",)),
    )(x, w_pre)


def _mm(name, ins, prologue, w, bias, out_dtype, tm, tn, keep_lhs=False):
    t = ins[0][0].shape[0]
    tm = min(tm, t)
    k, n = w.shape
    n_in = len(ins)
    assert not keep_lhs or tn == n

    def body(*refs):
        a = prologue(*[r[...] for r in refs[:n_in]])
        acc = _dot(a, refs[n_in][...])
        if bias is not None:
            acc = acc + refs[n_in + 1][...]
        if keep_lhs:
            refs[-1][...] = a
            refs[-2][...] = acc.astype(out_dtype)
        else:
            refs[-1][...] = acc.astype(out_dtype)

    in_specs = [pl.BlockSpec((tm, k), lambda i, j, cb=cb: (i, cb)) for _, cb in ins]
    in_specs.append(pl.BlockSpec((k, tn), lambda i, j: (0, j)))
    args = [a for a, _ in ins] + [w]
    if bias is not None:
        in_specs.append(pl.BlockSpec((1, tn), lambda i, j: (0, j)))
        args.append(bias)
    out_specs = pl.BlockSpec((tm, tn), lambda i, j: (i, j))
    out_shape = jax.ShapeDtypeStruct((t, n), out_dtype)
    if keep_lhs:
        out_specs = [out_specs, pl.BlockSpec((tm, k), lambda i, j: (i, 0))]
        out_shape = [out_shape, jax.ShapeDtypeStruct((t, k), bf16)]
    return pl.pallas_call(
        body, name=name, grid=(t // tm, n // tn), in_specs=in_specs, out_specs=out_specs, out_shape=out_shape,
        compiler_params=_params(("parallel", "parallel"), VMEM_LIMIT),
    )(*args)


def _forget_prep(f128, seq):
    t = f128.shape[0]
    nb = seq // LANES

    def body(f_ref, c_ref):
        r = lax.broadcasted_iota(jnp.int32, (LANES, LANES), 0)
        cidx = lax.broadcasted_iota(jnp.int32, (LANES, LANES), 1)
        tri = (r >= cidx).astype(f32)
        carry = jnp.zeros((1, LANES), f32)
        for blk in range(nb):
            fv = f_ref[pl.ds(blk * LANES, LANES), :]
            lf = -_softplus(-fv)
            c_ref[pl.ds(blk * LANES, LANES), :] = (
                jnp.dot(tri, lf, preferred_element_type=f32, precision=lax.Precision.HIGHEST) + carry)
            carry = carry + jnp.sum(lf, axis=0, keepdims=True)

    return pl.pallas_call(
        body, name="forget_prep", grid=(t // seq,),
        in_specs=[pl.BlockSpec((seq, LANES), lambda b: (b, 0))],
        out_specs=pl.BlockSpec((seq, LANES), lambda b: (b, 0)),
        out_shape=jax.ShapeDtypeStruct((t, LANES), f32),
        compiler_params=_params(("parallel",)),
    )(f128)


def _split3(cv):
    hi = cv.astype(bf16)
    r1 = cv - hi.astype(f32)
    mid = r1.astype(bf16)
    lo = (r1 - mid.astype(f32)).astype(bf16)
    return hi, mid, lo


def _attn_prep(qkv, c):
    t = qkv.shape[0]

    def body(q_ref, k_ref, c_ref, qa_ref, ka_ref):
        hp = pl.program_id(1)
        lane = lax.broadcasted_iota(jnp.int32, (1, LANES), 1)
        cv = c_ref[...]
        one = jnp.ones((), bf16)
        zero = jnp.zeros((), bf16)
        for hh in range(2):
            ch = jnp.sum(jnp.where(lane == 2 * hp + hh, cv, 0.0), axis=1, keepdims=True)
            hi, mid, lo = _split3(ch)
            q2, k2 = q_ref[...], k_ref[...]
            if hh == 1:
                q2, k2 = pltpu.roll(q2, 64, 1), pltpu.roll(k2, 64, 1)
            ones = jnp.where((lane >= 67) & (lane < 70), one, zero)
            qa = jnp.where(lane < 64, q2 * jnp.asarray(QK_SCALE, bf16),
                           jnp.where(lane == 64, hi, jnp.where(lane == 65, mid, jnp.where(lane == 66, lo, ones))))
            ones = jnp.where((lane >= 64) & (lane < 67), one, zero)
            ka = jnp.where(lane < 64, k2,
                           jnp.where(lane == 67, -hi, jnp.where(lane == 68, -mid, jnp.where(lane == 69, -lo, ones))))
            qa_ref[:, pl.ds(hh * LANES, LANES)] = qa
            ka_ref[:, pl.ds(hh * LANES, LANES)] = ka

    tm = min(512, t)
    spec = lambda off: pl.BlockSpec((tm, LANES), lambda i, hp: (i, off + hp))
    out = pl.BlockSpec((tm, 2 * LANES), lambda i, hp: (i, hp))
    return pl.pallas_call(
        body, name="attn_prep", grid=(t // tm, HEAD_PAIRS),
        in_specs=[spec(0), spec(HEAD_PAIRS), pl.BlockSpec((tm, LANES), lambda i, hp: (i, 0))],
        out_specs=[out, out],
        out_shape=[jax.ShapeDtypeStruct((t, 2 * D), bf16)] * 2,
        compiler_params=_params(("parallel", "parallel")),
    )(qkv, qkv, c)


def _attn_fwd(qa, ka, qkv, rest, seq):
    t = qkv.shape[0]
    nb, nq = t // seq, seq // TQ

    hg = ATT_GROUP
    ng = HEADS // hg

    def body(q_ref, k_ref, v_ref, ga_ref, o_ref, pa_ref, lse_ref, acc_scr):
        qi, gi = pl.program_id(1), pl.program_id(2)
        krow = lax.broadcasted_iota(jnp.int32, (TQ, TQ), 0)
        qcol = lax.broadcasted_iota(jnp.int32, (TQ, TQ), 1)
        acc_scr[...] = jnp.zeros_like(acc_scr)

        def kv_step(kt, carry, masked):
            ks = pl.multiple_of(kt * TQ, TQ)
            sts = [_dot_nt(k_ref[pl.ds(ks, TQ), pl.ds(g * LANES, LANES)], q_ref[:, pl.ds(g * LANES, LANES)])
                   for g in range(hg)]
            if masked:
                sts = [jnp.where(krow <= qcol, st, MASK_VALUE) for st in sts]
            m_new = [jnp.maximum(carry[g][0], jnp.max(sts[g], axis=0, keepdims=True)) for g in range(hg)]
            ps = [jnp.exp(sts[g] - m_new[g]) for g in range(hg)]
            alphas = [jnp.exp(carry[g][0] - m_new[g]) for g in range(hg)]
            phi = [ps[g].astype(bf16) for g in range(hg)]
            plo = [(ps[g] - phi[g].astype(f32)).astype(bf16) for g in range(hg)]
            vs = [v_ref[pl.ds(ks, TQ), pl.ds(j * LANES, LANES)] for j in range(hg // 2)]
            pvs = [_dot_tn(vs[g // 2], phi[g]) + _dot_tn(vs[g // 2], plo[g]) for g in range(hg)]
            olds = [acc_scr[g] for g in range(hg)]
            for g in range(hg):
                acc_scr[g] = alphas[g] * olds[g] + pvs[g]
            return tuple((m_new[g], alphas[g] * carry[g][1] + jnp.sum(ps[g], axis=0, keepdims=True))
                         for g in range(hg))

        init = tuple((jnp.full((1, TQ), MASK_VALUE, f32), jnp.zeros((1, TQ), f32)) for _ in range(hg))
        carry = lax.fori_loop(0, qi, lambda kt, cr: kv_step(kt, cr, False), init)
        stats = kv_step(qi, carry, True)
        drow = lax.broadcasted_iota(jnp.int32, (LANES, TQ), 0)
        for g in range(hg):
            m, l = stats[g]
            lse_ref[0, pl.ds(hg * gi + g, 1), :] = m + jnp.log(l)
        for j in range(hg // 2):
            o2 = jnp.where(drow < 64, acc_scr[2 * j] / stats[2 * j][1], acc_scr[2 * j + 1] / stats[2 * j + 1][1]).T
            o_ref[:, pl.ds(j * LANES, LANES)] = o2
            ga = ga_ref[:, pl.ds(j * LANES, LANES)]
            pa_ref[:, pl.ds(j * LANES, LANES)] = (o2 * (ga * _sig(ga))).astype(bf16)

    vw = hg * 64
    tile = pl.BlockSpec((TQ, vw), lambda b, qi, gi: (b * nq + qi, gi))
    return pl.pallas_call(
        body, name="attn_fwd", grid=(nb, nq, ng),
        in_specs=[pl.BlockSpec((TQ, hg * LANES), lambda b, qi, gi: (b * nq + qi, gi)),
                  pl.BlockSpec((seq, hg * LANES), lambda b, qi, gi: (b, gi)),
                  pl.BlockSpec((seq, vw), lambda b, qi, gi: (b, 2 * ng + gi)), tile],
        out_specs=[tile, tile, pl.BlockSpec((1, HEADS, TQ), lambda b, qi, gi: (b * nq + qi, 0, 0))],
        out_shape=[jax.ShapeDtypeStruct((t, D), f32), jax.ShapeDtypeStruct((t, D), bf16),
                   jax.ShapeDtypeStruct((t // TQ, HEADS, TQ), f32)],
        scratch_shapes=[pltpu.VMEM((hg, LANES, TQ), f32)],
        compiler_params=_params(("parallel", "parallel", "arbitrary"), VMEM_LIMIT),
    )(qa, ka, qkv, rest)


def _shifted_rows(x, top8, prev8, shift, row, row8):
    body = pltpu.roll(x, shift, 0)
    head = jnp.where(row8 < shift, pltpu.roll(prev8, shift, 0), pltpu.roll(top8, shift, 0))
    return body, head


def _rnn_gates(xc, wa_ref, wx_ref, ba_ref, bx_ref, lam_ref):
    xcb = xc.astype(bf16)
    r = _sig(_dot(xcb, wa_ref[...]) + ba_ref[...])
    i = _sig(_dot(xcb, wx_ref[...]) + bx_ref[...])
    sp = _softplus(-lam_ref[...])
    log_a = (-RG_C) * r * sp
    th = jnp.tanh(log_a)
    w1 = (-2.0) * th / (1.0 - th)
    sq = jnp.sqrt(jnp.maximum(w1, 0.0))
    return r, i, sp, log_a, w1, sq


def _conv_tile(x_ref, xprev_ref, has_prev, cw_ref, cb_ref, xc_ref):
    row = lax.broadcasted_iota(jnp.int32, (TL, D), 0)
    row8 = lax.broadcasted_iota(jnp.int32, (8, D), 0)
    x = x_ref[...]
    top8 = x_ref[pl.ds(0, 8), :]
    prev8 = jnp.where(has_prev, xprev_ref[...], 0.0)
    xc = cb_ref[...] + cw_ref[pl.ds(3, 1), :] * x
    xc8 = cb_ref[...] + cw_ref[pl.ds(3, 1), :] * top8
    for sh in range(1, 4):
        w = cw_ref[pl.ds(3 - sh, 1), :]
        xs, xs8 = _shifted_rows(x, top8, prev8, sh, row, row8)
        xc = xc + w * xs
        xc8 = xc8 + w * xs8
    xc_ref[...] = xc
    xc_ref[pl.ds(0, 8), :] = xc8


def _rnn_fwd(rest, conv_w, conv_b, wa_d, wx_d, ba, bx, lam, seq):
    t = rest.shape[0]
    nb, nt = t // seq, seq // TL

    def body(x_ref, xprev_ref, gr_ref, cw_ref, cb_ref, wa_ref, wx_ref, ba_ref, bx_ref, lam_ref,
             xc_ref, a_ref, h_ref, pr_ref, u_scr, carry):
        tt = pl.program_id(1)
        _conv_tile(x_ref, xprev_ref, tt > 0, cw_ref, cb_ref, xc_ref)
        xc = xc_ref[...]
        r, i, sp, log_a, w1, sq = _rnn_gates(xc, wa_ref, wx_ref, ba_ref, bx_ref, lam_ref)
        a_ref[...] = jnp.exp(log_a)
        u_scr[...] = sq * (i * xc)

        @pl.when(tt == 0)
        def _():
            carry[...] = jnp.zeros_like(carry)

        def step(s, h):
            h = a_ref[pl.ds(s, 1), :] * h + u_scr[pl.ds(s, 1), :]
            h_ref[pl.ds(s, 1), :] = h
            return h

        carry[...] = lax.fori_loop(0, TL, step, carry[...], unroll=8)
        gr = gr_ref[...]
        pr_ref[...] = (h_ref[...] * (gr * _sig(gr))).astype(bf16)

    tile = lambda cb: pl.BlockSpec((TL, D), lambda b, tt, cb=cb: (b * nt + tt, cb))
    prev = lambda cb: pl.BlockSpec((8, D), lambda b, tt, cb=cb: (jnp.maximum((b * nt + tt) * (TL // 8) - 1, 0), cb))
    vec = _whole((1, D))
    return pl.pallas_call(
        body, name="rnn_fwd", grid=(nb, nt),
        in_specs=[tile(1), prev(1), tile(2), _whole((4, D)), vec, _whole((D, D)), _whole((D, D)), vec, vec, vec],
        out_specs=[tile(0)] * 4,
        out_shape=[jax.ShapeDtypeStruct((t, D), f32)] * 3 + [jax.ShapeDtypeStruct((t, D), bf16)],
        scratch_shapes=[pltpu.VMEM((TL, D), f32), pltpu.VMEM((1, D), f32)],
        compiler_params=_params(("parallel", "arbitrary"), VMEM_LIMIT),
    )(rest, rest, rest, conv_w, conv_b, wa_d, wx_d, ba, bx, lam)


def _merge(mga, mgr, ya, yr):
    return (_sig(mga) * ya + _sig(mgr) * yr).astype(bf16)


def _post_loss(o, x, tgt, w_post):
    t = o.shape[0]

    def body(o_ref, x_ref, t_ref, w_ref, do_ref, dy_ref, loss_ref, dwp_ref):
        @pl.when(pl.program_id(0) == 0)
        def _():
            loss_ref[...] = jnp.zeros_like(loss_ref)
            dwp_ref[...] = jnp.zeros_like(dwp_ref)

        ov = o_ref[...]
        w = w_ref[...]
        r2 = lax.rsqrt(jnp.mean(ov * ov, axis=-1, keepdims=True) + NORM_EPS)
        oh = ov * r2
        e = x_ref[...] + oh * w - t_ref[...]
        loss_ref[...] += 0.5 * jnp.sum(jnp.mean(e * e, axis=-1, keepdims=True))
        dy = e * (1.0 / D)
        dy_ref[...] = dy
        dwp_ref[...] += jnp.sum(dy * oh, axis=0, keepdims=True)
        doh = dy * w
        do_ref[...] = (r2 * (doh - oh * jnp.mean(doh * oh, axis=-1, keepdims=True))).astype(bf16)

    return pl.pallas_call(
        body, name="post_loss", grid=(t // TM,),
        in_specs=[_tile(TM, D)] * 3 + [_whole((1, D))],
        out_specs=[_tile(TM, D), _tile(TM, D), _whole((8, LANES)), _whole((1, D))],
        out_shape=[jax.ShapeDtypeStruct((t, D), bf16), jax.ShapeDtypeStruct((t, D), f32),
                   jax.ShapeDtypeStruct((8, LANES), f32), jax.ShapeDtypeStruct((1, D), f32)],
        compiler_params=_params(("arbitrary",)),
    )(o, x, tgt, w_post)


def _out_bwd(do, rest, ya, yr, w_out):
    t = do.shape[0]

    def body(do_ref, mga_ref, mgr_ref, ya_ref, yr_ref, w_ref, dya_ref, dyr_ref, dmga_ref, dmgr_ref):
        sa, sr = _sig(mga_ref[...]), _sig(mgr_ref[...])
        ya, yr = ya_ref[...], yr_ref[...]
        dm = _dot_nt(do_ref[...], w_ref[...])
        dya_ref[...] = (dm * sa).astype(bf16)
        dyr_ref[...] = (dm * sr).astype(bf16)
        dmga_ref[...] = (dm * ya * sa * (1.0 - sa)).astype(bf16)
        dmgr_ref[...] = (dm * yr * sr * (1.0 - sr)).astype(bf16)

    return pl.pallas_call(
        body, name="out_bwd", grid=(t // TM,),
        in_specs=[_tile(TM, D), _tile(TM, D, 3), _tile(TM, D, 4), _tile(TM, D), _tile(TM, D), _whole((D, D))],
        out_specs=[_tile(TM, D)] * 4,
        out_shape=[jax.ShapeDtypeStruct((t, D), bf16)] * 4,
        compiler_params=_params(("parallel",), VMEM_LIMIT),
    )(do, rest, rest, ya, yr, w_out)


def _branch_bwd(name, dyb, rest, gate_cb, act, w, act_grad_dtype):
    t = dyb.shape[0]

    def body(dy_ref, g_ref, act_ref, w_ref, dact_ref, dg_ref):
        dp = _dot_nt(dy_ref[...], w_ref[...])
        g = g_ref[...]
        sg = _sig(g)
        dact_ref[...] = (dp * (g * sg)).astype(act_grad_dtype)
        dg_ref[...] = (dp * act_ref[...] * (sg * (1.0 + g * (1.0 - sg)))).astype(bf16)

    return pl.pallas_call(
        body, name=name, grid=(t // TM,),
        in_specs=[_tile(TM, D), _tile(TM, D, gate_cb), _tile(TM, D), _whole((D, D))],
        out_specs=[_tile(TM, D), _tile(TM, D)],
        out_shape=[jax.ShapeDtypeStruct((t, D), act_grad_dtype), jax.ShapeDtypeStruct((t, D), bf16)],
        compiler_params=_params(("parallel",), VMEM_LIMIT),
    )(dyb, rest, act, w)


def _rnn_bwd(dh, a, h, xc, rest, conv_w, conv_b, wa_d, wx_d, ba, bx, lam, seq):
    t = dh.shape[0]
    nb, nt = t // seq, seq // TL
    diag = (D // LANES, LANES, LANES)

    def body(dh_ref, a_ref, h_ref, hprev_ref, xc_ref, x_ref, xprev_ref, cw_ref, cb_ref, wa_ref, wx_ref,
             ba_ref, bx_ref, lam_ref, dxr_ref, dwa_ref, dwx_ref, vec_ref, g_scr, dxc_scr, dxr_scr, qcarry, dxc_next):
        b, tt = pl.program_id(0), pl.program_id(1)
        rt = nt - 1 - tt

        @pl.when((b == 0) & (tt == 0))
        def _():
            dwa_ref[...] = jnp.zeros_like(dwa_ref)
            dwx_ref[...] = jnp.zeros_like(dwx_ref)
            vec_ref[...] = jnp.zeros_like(vec_ref)

        @pl.when(tt == 0)
        def _():
            qcarry[...] = jnp.zeros_like(qcarry)
            dxc_next[...] = jnp.zeros_like(dxc_next)

        def step(k, q):
            s = TL - 1 - k
            g = dh_ref[pl.ds(s, 1), :] + q
            g_scr[pl.ds(s, 1), :] = g
            return a_ref[pl.ds(s, 1), :] * g

        qcarry[...] = lax.fori_loop(0, TL, step, qcarry[...], unroll=8)

        row = lax.broadcasted_iota(jnp.int32, (TL, D), 0)
        row8 = lax.broadcasted_iota(jnp.int32, (8, D), 0)
        g = g_scr[...]
        av = a_ref[...]
        xc = xc_ref[...]
        hlast = jnp.where(rt > 0, hprev_ref[pl.ds(7, 1), :], 0.0)
        hp = jnp.where(row == 0, hlast, pltpu.roll(h_ref[...], 1, 0))
        r, i, sp, log_a, w1, sq = _rnn_gates(xc, wa_ref, wx_ref, ba_ref, bx_ref, lam_ref)
        dix = g * sq
        di = dix * xc
        dxc = dix * i
        dsq = g * (i * xc)
        dlog_a = g * hp * av - dsq * jnp.where(sq > 0.0, (1.0 - w1) / sq, 0.0)
        dpr = (dlog_a * ((-RG_C) * sp)) * r * (1.0 - r)
        dpi = di * i * (1.0 - i)
        dprb, dpib, xcb = dpr.astype(bf16), dpi.astype(bf16), xc.astype(bf16)
        dxc = dxc + _dot_nt(dprb, wa_ref[...]) + _dot_nt(dpib, wx_ref[...])
        for j in range(D // LANES):
            cols = slice(j * LANES, (j + 1) * LANES)
            dwa_ref[j] += _dot_tn(xcb[:, cols], dprb[:, cols])
            dwx_ref[j] += _dot_tn(xcb[:, cols], dpib[:, cols])
        vec_ref[pl.ds(0, 1), :] += jnp.sum(dpr, axis=0, keepdims=True)
        vec_ref[pl.ds(1, 1), :] += jnp.sum(dpi, axis=0, keepdims=True)
        dsp = jnp.sum(dlog_a * ((-RG_C) * r), axis=0, keepdims=True)
        vec_ref[pl.ds(2, 1), :] += dsp * (-_sig(-lam_ref[...]))
        vec_ref[pl.ds(3, 1), :] += jnp.sum(dxc, axis=0, keepdims=True)

        dxc_scr[...] = dxc
        bot8 = dxc_scr[pl.ds(TL - 8, 8), :]
        nxt8 = dxc_next[...]
        dxr = cw_ref[pl.ds(3, 1), :] * dxc
        dxr8 = cw_ref[pl.ds(3, 1), :] * bot8
        for sh in range(1, 4):
            w = cw_ref[pl.ds(3 - sh, 1), :]
            dxr = dxr + w * pltpu.roll(dxc, TL - sh, 0)
            dxr8 = dxr8 + w * jnp.where(row8 < 8 - sh, pltpu.roll(bot8, 8 - sh, 0), pltpu.roll(nxt8, 8 - sh, 0))
        dxr_scr[...] = dxr
        dxr_scr[pl.ds(TL - 8, 8), :] = dxr8
        dxr_ref[...] = dxr_scr[...].astype(bf16)
        dxc_next[...] = dxc_scr[pl.ds(0, 8), :]

        x = x_ref[...]
        prev8 = jnp.where(rt > 0, xprev_ref[...], 0.0)
        dxc_top8 = dxc_scr[pl.ds(0, 8), :]
        vec_ref[pl.ds(7, 1), :] += jnp.sum(dxc * x, axis=0, keepdims=True)
        for sh in range(1, 4):
            inside = jnp.sum(dxc * jnp.where(row >= sh, pltpu.roll(x, sh, 0), 0.0), axis=0, keepdims=True)
            above = jnp.sum(dxc_top8 * jnp.where(row8 < sh, pltpu.roll(prev8, sh, 0), 0.0), axis=0, keepdims=True)
            vec_ref[pl.ds(7 - sh, 1), :] += inside + above

    tile = lambda cb: pl.BlockSpec((TL, D), lambda b, tt, cb=cb: (b * nt + nt - 1 - tt, cb))
    prev = lambda cb: pl.BlockSpec(
        (8, D), lambda b, tt, cb=cb: (jnp.maximum((b * nt + nt - 1 - tt) * (TL // 8) - 1, 0), cb))
    vec = _whole((1, D))
    return pl.pallas_call(
        body, name="rnn_bwd", grid=(nb, nt),
        in_specs=[tile(0), tile(0), tile(0), prev(0), tile(0), tile(1), prev(1),
                  _whole((4, D)), vec, _whole((D, D)), _whole((D, D)), vec, vec, vec],
        out_specs=[tile(0), _whole(diag), _whole(diag), _whole((8, D))],
        out_shape=[jax.ShapeDtypeStruct((t, D), bf16), jax.ShapeDtypeStruct(diag, f32),
                   jax.ShapeDtypeStruct(diag, f32), jax.ShapeDtypeStruct((8, D), f32)],
        scratch_shapes=[pltpu.VMEM((TL, D), f32), pltpu.VMEM((TL, D), f32), pltpu.VMEM((TL, D), f32),
                        pltpu.VMEM((1, D), f32), pltpu.VMEM((8, D), f32)],
        compiler_params=_params(("arbitrary", "arbitrary"), VMEM_LIMIT),
    )(dh, a, h, h, xc, rest, rest, conv_w, conv_b, wa_d, wx_d, ba, bx, lam)


def _attn_delta(doa, o):
    t = doa.shape[0]

    def body(do_ref, o_ref, d_ref):
        prod = do_ref[...].astype(f32) * o_ref[...]
        ch = lax.broadcasted_iota(jnp.int32, (D, LANES), 0)
        hd = lax.broadcasted_iota(jnp.int32, (D, LANES), 1)
        pick = (ch // 64 == hd).astype(f32)
        per_head = jnp.dot(prod, pick, preferred_element_type=f32, precision=lax.Precision.HIGHEST)
        d_ref[0] = per_head.T[:HEADS, :]

    return pl.pallas_call(
        body, name="attn_delta", grid=(t // TQ,),
        in_specs=[_tile(TQ, D), _tile(TQ, D)],
        out_specs=pl.BlockSpec((1, HEADS, TQ), lambda i: (i, 0, 0)),
        out_shape=jax.ShapeDtypeStruct((t // TQ, HEADS, TQ), f32),
        compiler_params=_params(("parallel",)),
    )(doa, o)


def _attn_bwd(qa, ka, qkv, doa, lse, delta, seq):
    t = qkv.shape[0]
    nb, nq = t // seq, seq // TQ
    hg = ATT_GROUP
    ng, npair = HEADS // hg, hg // 2

    def body(qa_ref, ka_ref, q_ref, k_ref, v_ref, do_ref, lse_ref, dl_ref, dq_ref, dk_ref, dv_ref, dc_ref,
             dqt_scr, dk_scr, dv_scr, ds_scr, kht_scr):
        gi, kt = pl.program_id(1), pl.program_id(2)
        lane = lax.broadcasted_iota(jnp.int32, (1, LANES), 1)
        krow = lax.broadcasted_iota(jnp.int32, (TQ, TQ), 0)
        qcol = lax.broadcasted_iota(jnp.int32, (TQ, TQ), 1)
        lmask = [(lane // 64) == hh for hh in range(2)]
        scale = jnp.asarray(QK_SCALE, bf16)

        @pl.when(kt == 0)
        def _():
            dqt_scr[...] = jnp.zeros_like(dqt_scr)

        dk_scr[...] = jnp.zeros_like(dk_scr)
        dv_scr[...] = jnp.zeros_like(dv_scr)
        ds_scr[...] = jnp.zeros_like(ds_scr)
        for g in range(hg):
            k2 = k_ref[:, pl.ds((g // 2) * LANES, LANES)]
            kht_scr[g] = jnp.where(lmask[g % 2], k2, jnp.zeros_like(k2)).T

        def q_step(qt, masked):
            qs = pl.multiple_of(qt * TQ, TQ)
            heads = range(hg)
            do2 = [do_ref[pl.ds(qs, TQ), pl.ds(j * LANES, LANES)] for j in range(npair)]
            q2 = [q_ref[pl.ds(qs, TQ), pl.ds(j * LANES, LANES)] for j in range(npair)]
            doh = [jnp.where(lmask[g % 2], do2[g // 2], jnp.zeros_like(do2[0])) for g in heads]
            qh = [jnp.where(lmask[g % 2], q2[g // 2], jnp.zeros_like(q2[0])) * scale for g in heads]
            st = [_dot_nt(ka_ref[:, pl.ds(g * LANES, LANES)], qa_ref[pl.ds(qs, TQ), pl.ds(g * LANES, LANES)])
                  for g in heads]
            if masked:
                st = [jnp.where(krow <= qcol, s, MASK_VALUE) for s in st]
            dp = [_dot_nt(v_ref[:, pl.ds((g // 2) * LANES, LANES)], doh[g]) for g in heads]
            p = [jnp.exp(st[g] - lse_ref[qt, pl.ds(hg * gi + g, 1), :]) for g in heads]
            ds = [p[g] * (dp[g] - dl_ref[qt, pl.ds(hg * gi + g, 1), :]) for g in heads]
            pb = [x.astype(bf16) for x in p]
            dsb = [x.astype(bf16) for x in ds]
            for j in range(npair):
                a, b = 2 * j, 2 * j + 1
                dv_scr[j] += _dot(pb[a], doh[a]) + _dot(pb[b], doh[b])
                dk_scr[j] += _dot(dsb[a], qh[a]) + _dot(dsb[b], qh[b])
                dqt_scr[qt, j] += (_dot(kht_scr[a], dsb[a]) + _dot(kht_scr[b], dsb[b])) * QK_SCALE
            for g in heads:
                ds_scr[g] += ds[g][:, :LANES] + ds[g][:, LANES:]

        q_step(kt, True)

        def loop_body(qt, carry):
            q_step(qt, False)
            return carry

        lax.fori_loop(kt + 1, nq, loop_body, 0)

        dc = jnp.zeros((TQ, LANES), f32)
        for g in range(hg):
            dc = jnp.where(lane == g, -jnp.sum(ds_scr[g], axis=1, keepdims=True), dc)
        dc_ref[...] = dc
        for j in range(npair):
            dk_ref[:, pl.ds(j * LANES, LANES)] = dk_scr[j].astype(bf16)
            dv_ref[:, pl.ds(j * LANES, LANES)] = dv_scr[j].astype(bf16)

        @pl.when(kt == nq - 1)
        def _():
            for qt in range(nq):
                for j in range(npair):
                    dq_ref[pl.ds(qt * TQ, TQ), pl.ds(j * LANES, LANES)] = dqt_scr[qt, j].T.astype(bf16)

    vw = hg * 64
    seqspec = pl.BlockSpec((seq, vw), lambda b, gi, kt: (b, gi))
    kspec = lambda off: pl.BlockSpec((TQ, vw), lambda b, gi, kt: (b * nq + kt, off + gi))
    rowspec = pl.BlockSpec((nq, HEADS, TQ), lambda b, gi, kt: (b, 0, 0))
    return pl.pallas_call(
        body, name="attn_bwd", grid=(nb, ng, nq),
        in_specs=[pl.BlockSpec((seq, hg * LANES), lambda b, gi, kt: (b, gi)),
                  pl.BlockSpec((TQ, hg * LANES), lambda b, gi, kt: (b * nq + kt, gi)),
                  seqspec, kspec(ng), kspec(2 * ng), seqspec, rowspec, rowspec],
        out_specs=[seqspec, kspec(0), kspec(0), pl.BlockSpec((TQ, LANES), lambda b, gi, kt: (b * nq + kt, gi))],
        out_shape=[jax.ShapeDtypeStruct((t, D), bf16)] * 3 + [jax.ShapeDtypeStruct((t, ng * LANES), f32)],
        scratch_shapes=[pltpu.VMEM((nq, npair, LANES, TQ), f32), pltpu.VMEM((npair, TQ, LANES), f32),
                        pltpu.VMEM((npair, TQ, LANES), f32), pltpu.VMEM((hg, TQ, LANES), f32),
                        pltpu.VMEM((hg, LANES, TQ), bf16)],
        compiler_params=_params(("parallel", "parallel", "arbitrary"), VMEM_LIMIT),
    )(qa, ka, qkv, qkv, qkv, doa, lse, delta)


def _forget_bwd(dc, f128, seq):
    t = f128.shape[0]
    nb = seq // LANES

    def body(dc_ref, f_ref, df_ref, dbf_ref):
        @pl.when(pl.program_id(0) == 0)
        def _():
            dbf_ref[...] = jnp.zeros_like(dbf_ref)

        r = lax.broadcasted_iota(jnp.int32, (LANES, LANES), 0)
        cidx = lax.broadcasted_iota(jnp.int32, (LANES, LANES), 1)
        tri = (r <= cidx).astype(f32)
        carry = jnp.zeros((1, LANES), f32)
        total = jnp.zeros((1, LANES), f32)
        for blk in reversed(range(nb)):
            dcb = dc_ref[pl.ds(blk * LANES, LANES), :]
            dlf = jnp.dot(tri, dcb, preferred_element_type=f32, precision=lax.Precision.HIGHEST) + carry
            df = dlf * _sig(-f_ref[pl.ds(blk * LANES, LANES), :])
            df_ref[pl.ds(blk * LANES, LANES), :] = df.astype(bf16)
            total = total + jnp.sum(df, axis=0, keepdims=True)
            carry = carry + jnp.sum(dcb, axis=0, keepdims=True)
        dbf_ref[...] += total

    return pl.pallas_call(
        body, name="forget_bwd", grid=(t // seq,),
        in_specs=[pl.BlockSpec((seq, LANES), lambda b: (b, 0)), pl.BlockSpec((seq, LANES), lambda b: (b, 0))],
        out_specs=[pl.BlockSpec((seq, LANES), lambda b: (b, 0)), _whole((1, LANES))],
        out_shape=[jax.ShapeDtypeStruct((t, LANES), bf16), jax.ShapeDtypeStruct((1, LANES), f32)],
        compiler_params=_params(("arbitrary",)),
    )(dc, f128)


def _in_bwd(dz, df, x, dy, w_qkv, w_rest, w_f, w_pre):
    t = x.shape[0]
    n_qkv = w_qkv.shape[1] // D
    n_rest = w_rest.shape[1] // D

    def body(*refs):
        dz_refs = refs[:n_qkv + n_rest]
        df_ref, x_ref, dy_ref, wq_ref, wr_ref, wf_ref, wp_ref, gx_ref, dwp_ref = refs[n_qkv + n_rest:]

        @pl.when(pl.program_id(0) == 0)
        def _():
            dwp_ref[...] = jnp.zeros_like(dwp_ref)

        dh = _dot_nt(df_ref[...], wf_ref[...])
        for p in range(n_qkv):
            dh = dh + _dot_nt(dz_refs[p][...], wq_ref[:, pl.ds(p * D, D)])
        for p in range(n_rest):
            dh = dh + _dot_nt(dz_refs[n_qkv + p][...], wr_ref[:, pl.ds(p * D, D)])
        xv = x_ref[...]
        r1 = lax.rsqrt(jnp.mean(xv * xv, axis=-1, keepdims=True) + NORM_EPS)
        xh = xv * r1
        dwp_ref[...] += jnp.sum(dh * xh, axis=0, keepdims=True)
        dxh = dh * wp_ref[...]
        gx_ref[...] = dy_ref[...] + r1 * (dxh - xh * jnp.mean(dxh * xh, axis=-1, keepdims=True))

    once = lambda shape: pl.BlockSpec(shape, lambda i: (0, 0), pipeline_mode=pl.Buffered(1))
    return pl.pallas_call(
        body, name="in_bwd", grid=(t // TM,),
        in_specs=[_tile(TM, D)] * (n_qkv + n_rest) + [_tile(TM, LANES), _tile(TM, D), _tile(TM, D),
                  once(w_qkv.shape), once(w_rest.shape), once(w_f.shape), _whole((1, D))],
        out_specs=[_tile(TM, D), _whole((1, D))],
        out_shape=[jax.ShapeDtypeStruct((t, D), f32), jax.ShapeDtypeStruct((1, D), f32)],
        compiler_params=_params(("arbitrary",), VMEM_LIMIT),
    )(*dz, df, x, dy, w_qkv, w_rest, w_f, w_pre)


def _tn_mm(name, a, b, tn, tk=2048):
    t, k = a.shape
    tk = min(tk, t)
    n = b.shape[1]

    def body(a_ref, b_ref, o_ref, s_ref):
        @pl.when(pl.program_id(1) == 0)
        def _():
            o_ref[...] = jnp.zeros_like(o_ref)
            s_ref[...] = jnp.zeros_like(s_ref)

        bv = b_ref[...]
        o_ref[...] += _dot_tn(a_ref[...], bv)
        s_ref[...] += jnp.sum(bv.astype(f32), axis=0, keepdims=True)

    return pl.pallas_call(
        body, name=name, grid=(n // tn, t // tk),
        in_specs=[pl.BlockSpec((tk, k), lambda j, kk: (kk, 0)), pl.BlockSpec((tk, tn), lambda j, kk: (kk, j))],
        out_specs=[pl.BlockSpec((k, tn), lambda j, kk: (0, j)), pl.BlockSpec((1, tn), lambda j, kk: (0, j))],
        out_shape=[jax.ShapeDtypeStruct((k, n), f32), jax.ShapeDtypeStruct((1, n), f32)],
        compiler_params=_params(("parallel", "arbitrary"), VMEM_LIMIT),
    )(a, b)


def _position():
    return lax.axis_index("x"), lax.axis_index("y"), lax.axis_index("c")


def _gather_shards(parts, small):
    n = len(parts)
    halves = [p.shape[0] // 2 for p in parts]

    def body(*refs):
        srcs, small_src = refs[:n], refs[n]
        dsts, small_dst = refs[n + 1:2 * n + 1], refs[2 * n + 1]
        send, recv, local = refs[2 * n + 2:]
        x, y, c = _position()
        me = 2 * x + y
        chips = [(1 - x, y), (x, 1 - y), (1 - x, 1 - y)]
        ids = [2 * px + py for px, py in chips]

        def half(a, shard, which):
            return dsts[a].at[shard, pl.ds(which * halves[a], halves[a]), :]

        def over_ici(a, j, shard):
            px, py = chips[j]
            return pltpu.make_async_remote_copy(
                src_ref=srcs[a].at[pl.ds(c * halves[a], halves[a]), :], dst_ref=half(a, shard, c),
                send_sem=send.at[a * 3 + j], recv_sem=recv.at[a * 3 + j], device_id=(px, py, c), device_id_type=MESH)

        def to_sibling(a, j, which):
            k = 3 * n + a * 3 + j
            return pltpu.make_async_remote_copy(
                src_ref=half(a, ids[j], which), dst_ref=half(a, ids[j], which), send_sem=send.at[k],
                recv_sem=recv.at[k], device_id=(x, y, 1 - c), device_id_type=MESH)

        def small_copy(j, shard):
            px, py = chips[j]
            return pltpu.make_async_remote_copy(
                src_ref=small_src, dst_ref=small_dst.at[shard], send_sem=send.at[6 * n + j], recv_sem=recv.at[6 * n + j],
                device_id=(px, py, c), device_id_type=MESH)

        own = [pltpu.make_async_copy(srcs[a], dsts[a].at[me], local.at[a]) for a in range(n)]
        own.append(pltpu.make_async_copy(small_src, small_dst.at[me], local.at[n]))
        for cp in own:
            cp.start()
        first = [over_ici(a, j, me) for j in range(3) for a in range(n)] + [small_copy(j, me) for j in range(3)]
        for cp in first:
            cp.start()
        passed = []
        for j in range(3):
            for a in range(n):
                over_ici(a, j, ids[j]).wait_recv()
                passed.append(to_sibling(a, j, c))
                passed[-1].start()
        for j in range(3):
            small_copy(j, ids[j]).wait_recv()
            for a in range(n):
                to_sibling(a, j, 1 - c).wait_recv()
        for cp in first + passed:
            cp.wait_send()
        for cp in own:
            cp.wait()

    anyspec = pl.BlockSpec(memory_space=pl.ANY)
    return pl.pallas_call(
        body, name="gather_shards",
        in_specs=[anyspec] * (n + 1), out_specs=[anyspec] * (n + 1),
        out_shape=[jax.ShapeDtypeStruct((N_CHIPS,) + p.shape, p.dtype) for p in parts + [small]],
        scratch_shapes=[pltpu.SemaphoreType.DMA((6 * n + 3,)), pltpu.SemaphoreType.DMA((6 * n + 3,)),
                        pltpu.SemaphoreType.DMA((n + 1,))],
    )(*parts, small)


def _allsum_rows(part):
    rows_n = part.shape[0]

    def body(x_ref, gath_ref, sum_ref, send_sems, recv_sems, local_sem):
        x, y, c = _position()
        me, sibling = (x, y, c), (x, y, 1 - c)
        chips = [(1 - x, y), (x, 1 - y), (1 - x, 1 - y)]

        def rows(px, py, pc):
            return gath_ref.at[pl.ds((4 * px + 2 * py + pc) * rows_n, rows_n), :]

        def copy(k, block, to, src=None):
            return pltpu.make_async_remote_copy(
                src_ref=rows(*block) if src is None else src, dst_ref=rows(*block),
                send_sem=send_sems.at[k], recv_sem=recv_sems.at[k], device_id=to, device_id_type=MESH)

        mine = pltpu.make_async_copy(x_ref, rows(*me), local_sem)
        mine.start()
        first = [copy(0, me, sibling, src=x_ref)]
        first += [copy(1 + j, me, (*chip, c), src=x_ref) for j, chip in enumerate(chips)]
        for cp in first:
            cp.start()
        passed = [copy(4 + j, (*chip, c), sibling) for j, chip in enumerate(chips)]
        for j, chip in enumerate(chips):
            copy(1 + j, (*chip, c), me).wait_recv()
            passed[j].start()
        copy(0, sibling, me).wait_recv()
        for j, chip in enumerate(chips):
            copy(4 + j, (*chip, 1 - c), me).wait_recv()
        for cp in first + passed:
            cp.wait_send()
        mine.wait()
        total = gath_ref[pl.ds(0, rows_n), :]
        for d in range(1, N_DEV):
            total = total + gath_ref[pl.ds(d * rows_n, rows_n), :]
        sum_ref[...] = total

    vm = pl.BlockSpec(memory_space=pltpu.VMEM)
    return pl.pallas_call(
        body, name="allsum_rows", in_specs=[vm], out_specs=[vm, vm],
        out_shape=[jax.ShapeDtypeStruct((N_DEV * rows_n, D), f32), jax.ShapeDtypeStruct((rows_n, D), f32)],
        scratch_shapes=[pltpu.SemaphoreType.DMA((7,)), pltpu.SemaphoreType.DMA((7,)), pltpu.SemaphoreType.DMA],
    )(part)[1]


COPY_CHUNKS = 4


def _row_chunks(ref, rows):
    size = rows // COPY_CHUNKS
    return [ref.at[pl.ds(i * size, size), :] for i in range(COPY_CHUNKS)]


def _sibling_exchange(arrs):
    n = len(arrs)
    rows = [a.shape[1] for a in arrs]

    def body(*refs):
        srcs, mine, theirs = refs[:n], refs[n:2 * n], refs[2 * n:3 * n]
        send, recv, local = refs[3 * n:]
        x, y, c = _position()

        def remote(a, j, half):
            k0 = (a * N_CHIPS + j) * COPY_CHUNKS
            return [pltpu.make_async_remote_copy(
                src_ref=s, dst_ref=d, send_sem=send.at[k0 + i], recv_sem=recv.at[k0 + i], device_id=(x, y, 1 - c),
                device_id_type=MESH) for i, (s, d) in enumerate(zip(_row_chunks(srcs[a].at[2 * j + half], rows[a]),
                                                                    _row_chunks(theirs[a].at[j], rows[a])))]

        own = [pltpu.make_async_copy(srcs[a].at[2 * j + c], mine[a].at[j], local.at[a * N_CHIPS + j])
               for a in range(n) for j in range(N_CHIPS)]
        sends = [cp for a in range(n) for j in range(N_CHIPS) for cp in remote(a, j, 1 - c)]
        for cp in sends + own:
            cp.start()
        for a in range(n):
            for j in range(N_CHIPS):
                for cp in remote(a, j, c):
                    cp.wait_recv()
        for cp in sends:
            cp.wait_send()
        for cp in own:
            cp.wait()

    anyspec = pl.BlockSpec(memory_space=pl.ANY)
    shapes = [jax.ShapeDtypeStruct((N_CHIPS,) + a.shape[1:], a.dtype) for a in arrs]
    k = n * N_CHIPS * COPY_CHUNKS
    out = pl.pallas_call(
        body, name="sibling_exchange", in_specs=[anyspec] * n, out_specs=[anyspec] * (2 * n), out_shape=shapes + shapes,
        scratch_shapes=[pltpu.SemaphoreType.DMA((k,)), pltpu.SemaphoreType.DMA((k,)),
                        pltpu.SemaphoreType.DMA((n * N_CHIPS,))],
    )(*arrs)
    return list(zip(out[:n], out[n:]))


def _chip_exchange(arrs):
    n = len(arrs)

    def body(*refs):
        srcs, dsts = refs[:n], refs[n:2 * n]
        send, recv, local = refs[2 * n:]
        x, y, c = _position()
        me = 2 * x + y
        chips = [(1 - x, y), (x, 1 - y), (1 - x, 1 - y)]

        def remote(a, j, piece, landing):
            px, py = chips[j]
            return pltpu.make_async_remote_copy(
                src_ref=srcs[a].at[piece], dst_ref=dsts[a].at[landing], send_sem=send.at[a * 3 + j],
                recv_sem=recv.at[a * 3 + j], device_id=(px, py, c), device_id_type=MESH)

        own = [pltpu.make_async_copy(srcs[a].at[me], dsts[a].at[me], local.at[a]) for a in range(n)]
        sends = [remote(a, j, 2 * px + py, me) for j, (px, py) in enumerate(chips) for a in range(n)]
        for cp in sends + own:
            cp.start()
        for j, (px, py) in enumerate(chips):
            for a in range(n):
                remote(a, j, me, 2 * px + py).wait_recv()
        for cp in sends:
            cp.wait_send()
        for cp in own:
            cp.wait()

    anyspec = pl.BlockSpec(memory_space=pl.ANY)
    return pl.pallas_call(
        body, name="chip_exchange", in_specs=[anyspec] * n, out_specs=[anyspec] * n,
        out_shape=[jax.ShapeDtypeStruct(a.shape, a.dtype) for a in arrs],
        scratch_shapes=[pltpu.SemaphoreType.DMA((3 * n,)), pltpu.SemaphoreType.DMA((3 * n,)),
                        pltpu.SemaphoreType.DMA((n,))],
    )(*arrs)


def _swap_halves(arrs):
    n = len(arrs)
    rows = [a.shape[0] for a in arrs]

    def body(*refs):
        srcs, dsts = refs[:n], refs[n:2 * n]
        send, recv, local = refs[2 * n:]
        x, y, c = _position()

        def remote(a, landing):
            return [pltpu.make_async_remote_copy(
                src_ref=s, dst_ref=d, send_sem=send.at[a * COPY_CHUNKS + i], recv_sem=recv.at[a * COPY_CHUNKS + i],
                device_id=(x, y, 1 - c), device_id_type=MESH)
                for i, (s, d) in enumerate(zip(_row_chunks(srcs[a], rows[a]), _row_chunks(dsts[a].at[landing], rows[a])))]

        own = [pltpu.make_async_copy(s, d, local.at[a * COPY_CHUNKS + i]) for a in range(n)
               for i, (s, d) in enumerate(zip(_row_chunks(srcs[a], rows[a]), _row_chunks(dsts[a].at[c], rows[a])))]
        sends = [cp for a in range(n) for cp in remote(a, c)]
        for cp in sends + own:
            cp.start()
        for a in range(n):
            for cp in remote(a, 1 - c):
                cp.wait_recv()
        for cp in sends:
            cp.wait_send()
        for cp in own:
            cp.wait()

    anyspec = pl.BlockSpec(memory_space=pl.ANY)
    k = n * COPY_CHUNKS
    return pl.pallas_call(
        body, name="swap_halves", in_specs=[anyspec] * n, out_specs=[anyspec] * n,
        out_shape=[jax.ShapeDtypeStruct((2,) + a.shape, a.dtype) for a in arrs],
        scratch_shapes=[pltpu.SemaphoreType.DMA((k,)), pltpu.SemaphoreType.DMA((k,)), pltpu.SemaphoreType.DMA((k,))],
    )(*arrs)


def _row_block(r):
    return 128 if r % 128 == 0 else r


def _pair_sum(name, mine, theirs):
    s, r, n = mine.shape
    rb = _row_block(r)

    def body(a_ref, b_ref, o_ref):
        o_ref[...] = (a_ref[...] + b_ref[...]).astype(bf16)

    spec = pl.BlockSpec((1, rb, n), lambda j, i: (j, i, 0))
    return pl.pallas_call(
        body, name=name, grid=(s, r // rb), in_specs=[spec, spec], out_specs=spec,
        out_shape=jax.ShapeDtypeStruct((s, r, n), bf16),
        compiler_params=_params(("parallel", "parallel"), VMEM_LIMIT),
    )(mine, theirs)


def _sum_slots(name, slots):
    s, r, n = slots.shape
    rb = _row_block(r)

    def body(s_ref, o_ref):
        total = s_ref[0].astype(f32)
        for d in range(1, s):
            total = total + s_ref[d].astype(f32)
        o_ref[...] = total

    return pl.pallas_call(
        body, name=name, grid=(r // rb,),
        in_specs=[pl.BlockSpec((s, rb, n), lambda i: (0, i, 0))],
        out_specs=pl.BlockSpec((rb, n), lambda i: (i, 0)),
        out_shape=jax.ShapeDtypeStruct((r, n), f32),
        compiler_params=_params(("parallel",), VMEM_LIMIT),
    )(slots)


def _adamw(name, w, g, m, v):
    r, n = w.shape
    rb = _row_block(r)

    def body(w_ref, g_ref, m_ref, v_ref, d_ref, nm_ref, nv_ref):
        gv = g_ref[...]
        m2 = ADAM_B1 * m_ref[...] + (1.0 - ADAM_B1) * gv
        v2 = ADAM_B2 * v_ref[...] + (1.0 - ADAM_B2) * (gv * gv)
        m_hat = m2 / (1.0 - ADAM_B1 ** ADAM_STEP)
        v_hat = v2 / (1.0 - ADAM_B2 ** ADAM_STEP)
        d_ref[...] = (-ADAM_LR) * (m_hat / (jnp.sqrt(v_hat) + ADAM_EPS) + ADAM_WD * w_ref[...])
        nm_ref[...] = m2
        nv_ref[...] = v2

    spec = pl.BlockSpec((rb, n), lambda i: (i, 0))
    return pl.pallas_call(
        body, name=name, grid=(r // rb,), in_specs=[spec] * 4, out_specs=[spec] * 3,
        out_shape=[jax.ShapeDtypeStruct((r, n), f32)] * 3,
        compiler_params=_params(("parallel",), VMEM_LIMIT),
    )(w, g, m, v)


def _identity(a):
    return a


def _local_step(x2, tgt2, seq, wt):
    nb = x2.shape[0] // seq
    h = _prenorm(x2, wt["pre_w"])
    qkv = _mm("in_qkv", [(h, 0)], _identity, wt["w_qkv"], wt["b_qkv"], bf16, 512, 1024)
    rest = _mm("in_rest", [(h, 0)], _identity, wt["w_rest"], wt["b_rest"], f32, 512, 1024)
    f128 = _mm("in_f", [(h, 0)], _identity, wt["w_f"], wt["b_f"], f32, 512, LANES)
    c = _forget_prep(f128, seq)
    qa, ka = _attn_prep(qkv, c)
    o_att, pa, lse = _attn_fwd(qa, ka, qkv, rest, seq)
    ya = _mm("proj_a", [(pa, 0)], _identity, wt["w_a"], None, f32, 512, D)
    rnn_w = (wt["conv_w"], wt["conv_b"], wt["wa_d"], wt["wx_d"], wt["ba"], wt["bx"], wt["lam"])
    xc, a, hrec, pr = _rnn_fwd(rest, *rnn_w, seq)
    yr = _mm("proj_r", [(pr, 0)], _identity, wt["w_r"], None, f32, 512, D)
    o, mrg = _mm("proj_out", [(rest, 3), (rest, 4), (ya, 0), (yr, 0)], _merge, wt["w_o"], None, f32, TM, D,
                 keep_lhs=True)

    do, dy, loss8, d_post = _post_loss(o, x2, tgt2, wt["post_w"])
    dya, dyr, dmga, dmgr = _out_bwd(do, rest, ya, yr, wt["w_o"])
    doa, dga = _branch_bwd("branch_a_bwd", dya, rest, 0, o_att, wt["w_a"], bf16)
    dhrec, dgr = _branch_bwd("branch_r_bwd", dyr, rest, 2, hrec, wt["w_r"], f32)
    d_wo, _ = _tn_mm("dw_out", mrg, do, 512)
    d_wa, _ = _tn_mm("dw_branch_a", pa, dya, 512)
    d_wr, _ = _tn_mm("dw_branch_r", pr, dyr, 512)
    dxr, d_wad, d_wxd, vec = _rnn_bwd(dhrec, a, hrec, xc, rest, *rnn_w, seq)
    dq, dk, dv, dc_pairs = _attn_bwd(qa, ka, qkv, doa, lse, _attn_delta(doa, o_att), seq)
    dc = dc_pairs.reshape(-1, HEADS // ATT_GROUP, LANES)[:, :, :ATT_GROUP].reshape(-1, HEADS)
    df, db_f = _forget_bwd(_pad_cols(dc, LANES), f128, seq)
    pieces = [dq, dk, dv, dga, dxr, dgr, dmga, dmgr]
    gx, d_pre = _in_bwd(pieces, df, x2, dy, wt["w_qkv"], wt["w_rest"], wt["w_f"], wt["pre_w"])
    names = ["q", "k", "v", "ga", "xr", "gr", "mga", "mgr"]
    dws, dbs = [], []
    for nm, piece in zip(names, pieces):
        dw_p, db_p = _tn_mm("dw_in_" + nm, h, piece, 512)
        dws.append(dw_p)
        dbs.append(db_p)
    dw_f, _ = _tn_mm("dw_in_f", h, df, LANES)
    zeros_w = jnp.zeros((D, IN_TOTAL - IN_USED), f32)
    d_w_in = jnp.concatenate(dws[:3] + [dw_f[:, :HEADS]] + dws[3:] + [zeros_w], axis=1)
    d_b_in = jnp.concatenate(dbs[:3] + [db_f[:, :HEADS]] + dbs[3:] + [zeros_w[:1]], axis=1)
    return dict(loss=loss8[0, 0], grad_x=gx, pre_w=d_pre, w_in=d_w_in, b_in=d_b_in, conv_w=vec[4:8], conv_b=vec[3:4],
                wa_d=d_wad, ba=vec[0:1], wx_d=d_wxd, bx=vec[1:2], lam=vec[2:3], w_a=d_wa, w_r=d_wr, w_o=d_wo,
                post_w=d_post)


def _block_diag(w):
    g, bw, _ = w.shape
    eye = jnp.eye(g, dtype=w.dtype)
    return (w[:, :, None, :] * eye[:, None, :, None]).reshape(g * bw, g * bw)


def _gate_blocks(diag):
    half = diag.shape[1] // 2
    return jnp.stack([diag[:, :half, :half], diag[:, half:, half:]], axis=1).reshape(-1, half, half)


def _pad_cols(a, n):
    return jnp.pad(a, ((0, 0), (0, n - a.shape[1])))


def _pad_rows(a, n):
    return jnp.pad(a, ((0, n - a.shape[0]), (0, 0)))


def kernel(x, pre_norm_w, w_in, b_in, conv_w, conv_b, rg_wa, rg_ba, rg_wx, rg_bx, rg_lambda, w_branch_a, w_branch_r, w_out, post_norm_w, loss_target, m_pre_norm_w, m_w_in, m_b_in, m_conv_w, m_conv_b, m_rg_wa, m_rg_ba, m_rg_wx, m_rg_bx, m_rg_lambda, m_w_branch_a, m_w_branch_r, m_w_out, m_post_norm_w, v_pre_norm_w, v_w_in, v_b_in, v_conv_w, v_conv_b, v_rg_wa, v_rg_ba, v_rg_wx, v_rg_bx, v_rg_lambda, v_w_branch_a, v_w_branch_r, v_w_out, v_post_norm_w):
    nb, seq, _ = x.shape
    chip = 2 * lax.axis_index("x") + lax.axis_index("y")
    n_groups = rg_wa.shape[1]

    g_in, g_a, g_r, g_o, g_cw = _gather_shards(
        [w_in[0].astype(bf16), w_branch_a[0].astype(bf16), w_branch_r[0].astype(bf16), w_out[0].astype(bf16)],
        conv_w[0])
    w_full = jnp.transpose(g_in, (1, 0, 2)).reshape(D, IN_TOTAL)
    q_end, f_end = 3 * D, 3 * D + HEADS
    wt = dict(
        pre_w=pre_norm_w, post_w=post_norm_w,
        w_qkv=w_full[:, :q_end], b_qkv=b_in[:, :q_end],
        w_f=_pad_cols(w_full[:, q_end:f_end], LANES), b_f=_pad_cols(b_in[:, q_end:f_end], LANES),
        w_rest=w_full[:, f_end:IN_USED], b_rest=b_in[:, f_end:IN_USED],
        w_a=g_a.reshape(D, D), w_r=g_r.reshape(D, D), w_o=g_o.reshape(D, D),
        conv_w=jnp.transpose(g_cw, (1, 0, 2)).reshape(4, D), conv_b=conv_b,
        wa_d=_block_diag(rg_wa[0]).astype(bf16), wx_d=_block_diag(rg_wx[0]).astype(bf16),
        ba=rg_ba, bx=rg_bx, lam=rg_lambda)

    part = _local_step(x.reshape(nb * seq, D), loss_target.reshape(nb * seq, D), seq, wt)
    loss = lax.psum(part["loss"], ("x", "y", "c"))
    grad_x = part["grad_x"].reshape(nb, seq, D)

    small = jnp.concatenate([
        part["pre_w"], _pad_cols(part["b_in"], 10 * D).reshape(10, D), part["conv_b"],
        _gate_blocks(part["wa_d"]).reshape(-1, D), part["ba"],
        _gate_blocks(part["wx_d"]).reshape(-1, D), part["bx"], part["lam"], part["post_w"],
        part["conv_w"]], axis=0)
    n_small = small.shape[0]
    n_rep = n_small - 4
    tot = _allsum_rows(_pad_rows(small, -(-n_small // 8) * 8))
    g_rep = tot[:n_rep]
    g_conv_w = lax.dynamic_slice_in_dim(tot[n_rep:n_small], chip * (D // N_CHIPS), D // N_CHIPS, axis=1)

    def pack(pre, b, cb, wa, ba, wx, bx, lam, post):
        return jnp.concatenate([pre, _pad_cols(b, 10 * D).reshape(10, D), cb, wa.reshape(-1, D), ba,
                                wx.reshape(-1, D), bx, lam, post], axis=0)

    def unpack(p):
        o = [0]

        def take(k):
            o[0] += k
            return p[o[0] - k:o[0]]

        pre = take(1)
        b = take(10).reshape(1, 10 * D)[:, :IN_TOTAL]
        cb = take(1)
        wa = take(64).reshape(rg_wa.shape)
        ba = take(1)
        wx = take(64).reshape(rg_wx.shape)
        bx = take(1)
        lam = take(1)
        post = take(1)
        return dict(pre_norm_w=pre, b_in=b, conv_b=cb, rg_wa=wa, rg_ba=ba, rg_wx=wx, rg_bx=bx, rg_lambda=lam,
                    post_norm_w=post)

    w_rep = pack(pre_norm_w, b_in, conv_b, rg_wa, rg_ba, rg_wx, rg_bx, rg_lambda, post_norm_w)
    m_rep = pack(m_pre_norm_w, m_b_in, m_conv_b, m_rg_wa, m_rg_ba, m_rg_wx, m_rg_bx, m_rg_lambda, m_post_norm_w)
    v_rep = pack(v_pre_norm_w, v_b_in, v_conv_b, v_rg_wa, v_rg_ba, v_rg_wx, v_rg_bx, v_rg_lambda, v_post_norm_w)
    d_rep, nm_rep, nv_rep = _adamw("adamw_rep", w_rep, g_rep, m_rep, v_rep)
    grads, deltas, new_m, new_v = unpack(g_rep), unpack(d_rep), unpack(nm_rep), unpack(nv_rep)

    shard_cols = IN_TOTAL // N_CHIPS
    p_in = jnp.transpose(part["w_in"].reshape(D, N_CHIPS, shard_cols), (1, 0, 2)).reshape(N_DEV, D // 2, shard_cols)
    p_aro = jnp.concatenate([part[k].reshape(N_DEV, D // N_DEV, D) for k in ("w_a", "w_r", "w_o")], axis=1)
    pair_in, pair_aro = _sibling_exchange([p_in, p_aro])
    s_in, s_aro = _chip_exchange([_pair_sum("pair_w_in", *pair_in), _pair_sum("pair_w_aro", *pair_aro)])
    f_in, f_aro = _swap_halves([_sum_slots("sum_w_in", s_in), _sum_slots("sum_w_aro", s_aro)])
    g_w_in = f_in.reshape(D, shard_cols)
    rows = D // N_DEV
    g_aro = jnp.concatenate([f_aro[:, i * rows:(i + 1) * rows, :].reshape(2 * rows, D) for i in range(3)], axis=0)

    d_w_in, nm_w_in, nv_w_in = _adamw("adamw_w_in", w_in[0], g_w_in, m_w_in[0], v_w_in[0])
    stack = lambda a, b, c: jnp.concatenate([a[0], b[0], c[0]], axis=0)
    d_aro, nm_aro, nv_aro = _adamw("adamw_w_aro", stack(w_branch_a, w_branch_r, w_out), g_aro,
                                   stack(m_w_branch_a, m_w_branch_r, m_w_out),
                                   stack(v_w_branch_a, v_w_branch_r, v_w_out))
    d_cw, nm_cw, nv_cw = _adamw("adamw_conv_w", conv_w[0], g_conv_w, m_conv_w[0], v_conv_w[0])

    def sharded(t_in, t_aro, t_cw):
        r2 = 2 * rows
        return dict(w_in=t_in[None], conv_w=t_cw[None], w_branch_a=t_aro[None, :r2], w_branch_r=t_aro[None, r2:2 * r2],
                    w_out=t_aro[None, 2 * r2:])

    order = ["pre_norm_w", "w_in", "b_in", "conv_w", "conv_b", "rg_wa", "rg_ba", "rg_wx", "rg_bx", "rg_lambda",
             "w_branch_a", "w_branch_r", "w_out", "post_norm_w"]
    outs = [loss, grad_x]
    for rep, shd in ((grads, sharded(g_w_in, g_aro, g_conv_w)), (deltas, sharded(d_w_in, d_aro, d_cw)),
                     (new_m, sharded(nm_w_in, nm_aro, nm_cw)), (new_v, sharded(nv_w_in, nv_aro, nv_cw))):
        both = {**rep, **shd}
        outs.extend(both[k] for k in order)
    return tuple(outs)
```

```python
import jax
import jax.numpy as jnp
from jax import lax
from jax.experimental import pallas as pl
from jax.experimental.pallas import tpu as pltpu

f32 = jnp.float32
bf16 = jnp.bfloat16

D = 1024
HEADS = 16
HEAD_PAIRS = 8
LANES = 128
NORM_EPS = 1e-6
MASK_VALUE = -1e30
RG_C = 8.0
QK_SCALE = 0.125
TQ = 256
ATT_GROUP = 8
TL = 256
TM = 256
IN_USED = 8 * D + HEADS
IN_TOTAL = 9 * D + HEADS
N_CHIPS = 4
N_DEV = 8
ADAM_LR, ADAM_B1, ADAM_B2, ADAM_EPS, ADAM_WD, ADAM_STEP = 0.001, 0.9, 0.999, 1e-08, 0.01, 10
VMEM_LIMIT = 56 * 1024 * 1024
MESH = pl.DeviceIdType.MESH


def _dot(a, b):
    return jnp.dot(a, b, preferred_element_type=f32)


def _dot_nt(a, b):
    return lax.dot_general(a, b, (((1,), (1,)), ((), ())), preferred_element_type=f32)


def _dot_tn(a, b):
    return lax.dot_general(a, b, (((0,), (0,)), ((), ())), preferred_element_type=f32)


def _sig(x):
    return 1.0 / (1.0 + jnp.exp(-x))


def _softplus(x):
    return jnp.maximum(x, 0.0) + jnp.log(1.0 + jnp.exp(-jnp.abs(x)))


def _params(sem, vmem=None):
    return pltpu.CompilerParams(dimension_semantics=sem, vmem_limit_bytes=vmem)


def _tile(tm, width, cb=0):
    return pl.BlockSpec((tm, width), lambda i, cb=cb: (i, cb))


def _whole(shape):
    nd = len(shape)
    return pl.BlockSpec(shape, lambda *_: (0,) * nd)


def _prenorm(x, w_pre):
    t = x.shape[0]

    def body(x_ref, w_ref, h_ref):
        xv = x_ref[...]
        r = lax.rsqrt(jnp.mean(xv * xv, axis=-1, keepdims=True) + NORM_EPS)
        h_ref[...] = (xv * r * w_ref[...]).astype(bf16)

    return pl.pallas_call(
        body, name="prenorm", grid=(t // TM,),
        in_specs=[_tile(TM, D), _whole((1, D))], out_specs=_tile(TM, D),
        out_shape=jax.ShapeDtypeStruct((t, D), bf16),
        compiler_params=_params(("parallel",)),
    )(x, w_pre)


def _mm(name, ins, prologue, w, bias, out_dtype, tm, tn, keep_lhs=False):
    t = ins[0][0].shape[0]
    tm = min(tm, t)
    k, n = w.shape
    n_in = len(ins)
    assert not keep_lhs or tn == n

    def body(*refs):
        a = prologue(*[r[...] for r in refs[:n_in]])
        acc = _dot(a, refs[n_in][...])
        if bias is not None:
            acc = acc + refs[n_in + 1][...]
        if keep_lhs:
            refs[-1][...] = a
            refs[-2][...] = acc.astype(out_dtype)
        else:
            refs[-1][...] = acc.astype(out_dtype)

    in_specs = [pl.BlockSpec((tm, k), lambda i, j, cb=cb: (i, cb)) for _, cb in ins]
    in_specs.append(pl.BlockSpec((k, tn), lambda i, j: (0, j)))
    args = [a for a, _ in ins] + [w]
    if bias is not None:
        in_specs.append(pl.BlockSpec((1, tn), lambda i, j: (0, j)))
        args.append(bias)
    out_specs = pl.BlockSpec((tm, tn), lambda i, j: (i, j))
    out_shape = jax.ShapeDtypeStruct((t, n), out_dtype)
    if keep_lhs:
        out_specs = [out_specs, pl.BlockSpec((tm, k), lambda i, j: (i, 0))]
        out_shape = [out_shape, jax.ShapeDtypeStruct((t, k), bf16)]
    return pl.pallas_call(
        body, name=name, grid=(t // tm, n // tn), in_specs=in_specs, out_specs=out_specs, out_shape=out_shape,
        compiler_params=_params(("parallel", "parallel"), VMEM_LIMIT),
    )(*args)


def _forget_prep(f128, seq):
    t = f128.shape[0]
    nb = seq // LANES

    def body(f_ref, c_ref):
        r = lax.broadcasted_iota(jnp.int32, (LANES, LANES), 0)
        cidx = lax.broadcasted_iota(jnp.int32, (LANES, LANES), 1)
        tri = (r >= cidx).astype(f32)
        carry = jnp.zeros((1, LANES), f32)
        for blk in range(nb):
            fv = f_ref[pl.ds(blk * LANES, LANES), :]
            lf = -_softplus(-fv)
            c_ref[pl.ds(blk * LANES, LANES), :] = (
                jnp.dot(tri, lf, preferred_element_type=f32, precision=lax.Precision.HIGHEST) + carry)
            carry = carry + jnp.sum(lf, axis=0, keepdims=True)

    return pl.pallas_call(
        body, name="forget_prep", grid=(t // seq,),
        in_specs=[pl.BlockSpec((seq, LANES), lambda b: (b, 0))],
        out_specs=pl.BlockSpec((seq, LANES), lambda b: (b, 0)),
        out_shape=jax.ShapeDtypeStruct((t, LANES), f32),
        compiler_params=_params(("parallel",)),
    )(f128)


def _split3(cv):
    hi = cv.astype(bf16)
    r1 = cv - hi.astype(f32)
    mid = r1.astype(bf16)
    lo = (r1 - mid.astype(f32)).astype(bf16)
    return hi, mid, lo


def _attn_prep(qkv, c):
    t = qkv.shape[0]

    def body(q_ref, k_ref, c_ref, qa_ref, ka_ref):
        hp = pl.program_id(1)
        lane = lax.broadcasted_iota(jnp.int32, (1, LANES), 1)
        cv = c_ref[...]
        one = jnp.ones((), bf16)
        zero = jnp.zeros((), bf16)
        for hh in range(2):
            ch = jnp.sum(jnp.where(lane == 2 * hp + hh, cv, 0.0), axis=1, keepdims=True)
            hi, mid, lo = _split3(ch)
            q2, k2 = q_ref[...], k_ref[...]
            if hh == 1:
                q2, k2 = pltpu.roll(q2, 64, 1), pltpu.roll(k2, 64, 1)
            ones = jnp.where((lane >= 67) & (lane < 70), one, zero)
            qa = jnp.where(lane < 64, q2 * jnp.asarray(QK_SCALE, bf16),
                           jnp.where(lane == 64, hi, jnp.where(lane == 65, mid, jnp.where(lane == 66, lo, ones))))
            ones = jnp.where((lane >= 64) & (lane < 67), one, zero)
            ka = jnp.where(lane < 64, k2,
                           jnp.where(lane == 67, -hi, jnp.where(lane == 68, -mid, jnp.where(lane == 69, -lo, ones))))
            qa_ref[:, pl.ds(hh * LANES, LANES)] = qa
            ka_ref[:, pl.ds(hh * LANES, LANES)] = ka

    tm = min(512, t)
    spec = lambda off: pl.BlockSpec((tm, LANES), lambda i, hp: (i, off + hp))
    out = pl.BlockSpec((tm, 2 * LANES), lambda i, hp: (i, hp))
    return pl.pallas_call(
        body, name="attn_prep", grid=(t // tm, HEAD_PAIRS),
        in_specs=[spec(0), spec(HEAD_PAIRS), pl.BlockSpec((tm, LANES), lambda i, hp: (i, 0))],
        out_specs=[out, out],
        out_shape=[jax.ShapeDtypeStruct((t, 2 * D), bf16)] * 2,
        compiler_params=_params(("parallel", "parallel")),
    )(qkv, qkv, c)


def _attn_fwd(qa, ka, qkv, rest, seq):
    t = qkv.shape[0]
    nb, nq = t // seq, seq // TQ

    hg = ATT_GROUP
    ng = HEADS // hg

    def body(q_ref, k_ref, v_ref, ga_ref, o_ref, pa_ref, lse_ref, acc_scr):
        qi, gi = pl.program_id(1), pl.program_id(2)
        krow = lax.broadcasted_iota(jnp.int32, (TQ, TQ), 0)
        qcol = lax.broadcasted_iota(jnp.int32, (TQ, TQ), 1)
        acc_scr[...] = jnp.zeros_like(acc_scr)

        def kv_step(kt, carry, masked):
            ks = pl.multiple_of(kt * TQ, TQ)
            sts = [_dot_nt(k_ref[pl.ds(ks, TQ), pl.ds(g * LANES, LANES)], q_ref[:, pl.ds(g * LANES, LANES)])
                   for g in range(hg)]
            if masked:
                sts = [jnp.where(krow <= qcol, st, MASK_VALUE) for st in sts]
            m_new = [jnp.maximum(carry[g][0], jnp.max(sts[g], axis=0, keepdims=True)) for g in range(hg)]
            ps = [jnp.exp(sts[g] - m_new[g]) for g in range(hg)]
            alphas = [jnp.exp(carry[g][0] - m_new[g]) for g in range(hg)]
            phi = [ps[g].astype(bf16) for g in range(hg)]
            plo = [(ps[g] - phi[g].astype(f32)).astype(bf16) for g in range(hg)]
            vs = [v_ref[pl.ds(ks, TQ), pl.ds(j * LANES, LANES)] for j in range(hg // 2)]
            pvs = [_dot_tn(vs[g // 2], phi[g]) + _dot_tn(vs[g // 2], plo[g]) for g in range(hg)]
            olds = [acc_scr[g] for g in range(hg)]
            for g in range(hg):
                acc_scr[g] = alphas[g] * olds[g] + pvs[g]
            return tuple((m_new[g], alphas[g] * carry[g][1] + jnp.sum(ps[g], axis=0, keepdims=True))
                         for g in range(hg))

        init = tuple((jnp.full((1, TQ), MASK_VALUE, f32), jnp.zeros((1, TQ), f32)) for _ in range(hg))
        carry = lax.fori_loop(0, qi, lambda kt, cr: kv_step(kt, cr, False), init)
        stats = kv_step(qi, carry, True)
        drow = lax.broadcasted_iota(jnp.int32, (LANES, TQ), 0)
        for g in range(hg):
            m, l = stats[g]
            lse_ref[0, pl.ds(hg * gi + g, 1), :] = m + jnp.log(l)
        for j in range(hg // 2):
            o2 = jnp.where(drow < 64, acc_scr[2 * j] / stats[2 * j][1], acc_scr[2 * j + 1] / stats[2 * j + 1][1]).T
            o_ref[:, pl.ds(j * LANES, LANES)] = o2
            ga = ga_ref[:, pl.ds(j * LANES, LANES)]
            pa_ref[:, pl.ds(j * LANES, LANES)] = (o2 * (ga * _sig(ga))).astype(bf16)

    vw = hg * 64
    tile = pl.BlockSpec((TQ, vw), lambda b, qi, gi: (b * nq + qi, gi))
    return pl.pallas_call(
        body, name="attn_fwd", grid=(nb, nq, ng),
        in_specs=[pl.BlockSpec((TQ, hg * LANES), lambda b, qi, gi: (b * nq + qi, gi)),
                  pl.BlockSpec((seq, hg * LANES), lambda b, qi, gi: (b, gi)),
                  pl.BlockSpec((seq, vw), lambda b, qi, gi: (b, 2 * ng + gi)), tile],
        out_specs=[tile, tile, pl.BlockSpec((1, HEADS, TQ), lambda b, qi, gi: (b * nq + qi, 0, 0))],
        out_shape=[jax.ShapeDtypeStruct((t, D), f32), jax.ShapeDtypeStruct((t, D), bf16),
                   jax.ShapeDtypeStruct((t // TQ, HEADS, TQ), f32)],
        scratch_shapes=[pltpu.VMEM((hg, LANES, TQ), f32)],
        compiler_params=_params(("parallel", "parallel", "arbitrary"), VMEM_LIMIT),
    )(qa, ka, qkv, rest)


def _shifted_rows(x, top8, prev8, shift, row, row8):
    body = pltpu.roll(x, shift, 0)
    head = jnp.where(row8 < shift, pltpu.roll(prev8, shift, 0), pltpu.roll(top8, shift, 0))
    return body, head


def _rnn_gates(xc, wa_ref, wx_ref, ba_ref, bx_ref, lam_ref):
    xcb = xc.astype(bf16)
    r = _sig(_dot(xcb, wa_ref[...]) + ba_ref[...])
    i = _sig(_dot(xcb, wx_ref[...]) + bx_ref[...])
    sp = _softplus(-lam_ref[...])
    log_a = (-RG_C) * r * sp
    th = jnp.tanh(log_a)
    w1 = (-2.0) * th / (1.0 - th)
    sq = jnp.sqrt(jnp.maximum(w1, 0.0))
    return r, i, sp, log_a, w1, sq


def _conv_tile(x_ref, xprev_ref, has_prev, cw_ref, cb_ref, xc_ref):
    row = lax.broadcasted_iota(jnp.int32, (TL, D), 0)
    row8 = lax.broadcasted_iota(jnp.int32, (8, D), 0)
    x = x_ref[...]
    top8 = x_ref[pl.ds(0, 8), :]
    prev8 = jnp.where(has_prev, xprev_ref[...], 0.0)
    xc = cb_ref[...] + cw_ref[pl.ds(3, 1), :] * x
    xc8 = cb_ref[...] + cw_ref[pl.ds(3, 1), :] * top8
    for sh in range(1, 4):
        w = cw_ref[pl.ds(3 - sh, 1), :]
        xs, xs8 = _shifted_rows(x, top8, prev8, sh, row, row8)
        xc = xc + w * xs
        xc8 = xc8 + w * xs8
    xc_ref[...] = xc
    xc_ref[pl.ds(0, 8), :] = xc8


def _rnn_fwd(rest, conv_w, conv_b, wa_d, wx_d, ba, bx, lam, seq):
    t = rest.shape[0]
    nb, nt = t // seq, seq // TL

    def body(x_ref, xprev_ref, gr_ref, cw_ref, cb_ref, wa_ref, wx_ref, ba_ref, bx_ref, lam_ref,
             xc_ref, a_ref, h_ref, pr_ref, u_scr, carry):
        tt = pl.program_id(1)
        _conv_tile(x_ref, xprev_ref, tt > 0, cw_ref, cb_ref, xc_ref)
        xc = xc_ref[...]
        r, i, sp, log_a, w1, sq = _rnn_gates(xc, wa_ref, wx_ref, ba_ref, bx_ref, lam_ref)
        a_ref[...] = jnp.exp(log_a)
        u_scr[...] = sq * (i * xc)

        @pl.when(tt == 0)
        def _():
            carry[...] = jnp.zeros_like(carry)

        def step(s, h):
            h = a_ref[pl.ds(s, 1), :] * h + u_scr[pl.ds(s, 1), :]
            h_ref[pl.ds(s, 1), :] = h
            return h

        carry[...] = lax.fori_loop(0, TL, step, carry[...], unroll=8)
        gr = gr_ref[...]
        pr_ref[...] = (h_ref[...] * (gr * _sig(gr))).astype(bf16)

    tile = lambda cb: pl.BlockSpec((TL, D), lambda b, tt, cb=cb: (b * nt + tt, cb))
    prev = lambda cb: pl.BlockSpec((8, D), lambda b, tt, cb=cb: (jnp.maximum((b * nt + tt) * (TL // 8) - 1, 0), cb))
    vec = _whole((1, D))
    return pl.pallas_call(
        body, name="rnn_fwd", grid=(nb, nt),
        in_specs=[tile(1), prev(1), tile(2), _whole((4, D)), vec, _whole((D, D)), _whole((D, D)), vec, vec, vec],
        out_specs=[tile(0)] * 4,
        out_shape=[jax.ShapeDtypeStruct((t, D), f32)] * 3 + [jax.ShapeDtypeStruct((t, D), bf16)],
        scratch_shapes=[pltpu.VMEM((TL, D), f32), pltpu.VMEM((1, D), f32)],
        compiler_params=_params(("parallel", "arbitrary"), VMEM_LIMIT),
    )(rest, rest, rest, conv_w, conv_b, wa_d, wx_d, ba, bx, lam)


def _merge(mga, mgr, ya, yr):
    return (_sig(mga) * ya + _sig(mgr) * yr).astype(bf16)


def _post_loss(o, x, tgt, w_post):
    t = o.shape[0]

    def body(o_ref, x_ref, t_ref, w_ref, do_ref, dy_ref, loss_ref, dwp_ref):
        @pl.when(pl.program_id(0) == 0)
        def _():
            loss_ref[...] = jnp.zeros_like(loss_ref)
            dwp_ref[...] = jnp.zeros_like(dwp_ref)

        ov = o_ref[...]
        w = w_ref[...]
        r2 = lax.rsqrt(jnp.mean(ov * ov, axis=-1, keepdims=True) + NORM_EPS)
        oh = ov * r2
        e = x_ref[...] + oh * w - t_ref[...]
        loss_ref[...] += 0.5 * jnp.sum(jnp.mean(e * e, axis=-1, keepdims=True))
        dy = e * (1.0 / D)
        dy_ref[...] = dy
        dwp_ref[...] += jnp.sum(dy * oh, axis=0, keepdims=True)
        doh = dy * w
        do_ref[...] = (r2 * (doh - oh * jnp.mean(doh * oh, axis=-1, keepdims=True))).astype(bf16)

    return pl.pallas_call(
        body, name="post_loss", grid=(t // TM,),
        in_specs=[_tile(TM, D)] * 3 + [_whole((1, D))],
        out_specs=[_tile(TM, D), _tile(TM, D), _whole((8, LANES)), _whole((1, D))],
        out_shape=[jax.ShapeDtypeStruct((t, D), bf16), jax.ShapeDtypeStruct((t, D), f32),
                   jax.ShapeDtypeStruct((8, LANES), f32), jax.ShapeDtypeStruct((1, D), f32)],
        compiler_params=_params(("arbitrary",)),
    )(o, x, tgt, w_post)


def _out_bwd(do, rest, ya, yr, w_out):
    t = do.shape[0]

    def body(do_ref, mga_ref, mgr_ref, ya_ref, yr_ref, w_ref, dya_ref, dyr_ref, dmga_ref, dmgr_ref):
        sa, sr = _sig(mga_ref[...]), _sig(mgr_ref[...])
        ya, yr = ya_ref[...], yr_ref[...]
        dm = _dot_nt(do_ref[...], w_ref[...])
        dya_ref[...] = (dm * sa).astype(bf16)
        dyr_ref[...] = (dm * sr).astype(bf16)
        dmga_ref[...] = (dm * ya * sa * (1.0 - sa)).astype(bf16)
        dmgr_ref[...] = (dm * yr * sr * (1.0 - sr)).astype(bf16)

    return pl.pallas_call(
        body, name="out_bwd", grid=(t // TM,),
        in_specs=[_tile(TM, D), _tile(TM, D, 3), _tile(TM, D, 4), _tile(TM, D), _tile(TM, D), _whole((D, D))],
        out_specs=[_tile(TM, D)] * 4,
        out_shape=[jax.ShapeDtypeStruct((t, D), bf16)] * 4,
        compiler_params=_params(("parallel",), VMEM_LIMIT),
    )(do, rest, rest, ya, yr, w_out)


def _branch_bwd(name, dyb, rest, gate_cb, act, w, act_grad_dtype):
    t = dyb.shape[0]

    def body(dy_ref, g_ref, act_ref, w_ref, dact_ref, dg_ref):
        dp = _dot_nt(dy_ref[...], w_ref[...])
        g = g_ref[...]
        sg = _sig(g)
        dact_ref[...] = (dp * (g * sg)).astype(act_grad_dtype)
        dg_ref[...] = (dp * act_ref[...] * (sg * (1.0 + g * (1.0 - sg)))).astype(bf16)

    return pl.pallas_call(
        body, name=name, grid=(t // TM,),
        in_specs=[_tile(TM, D), _tile(TM, D, gate_cb), _tile(TM, D), _whole((D, D))],
        out_specs=[_tile(TM, D), _tile(TM, D)],
        out_shape=[jax.ShapeDtypeStruct((t, D), act_grad_dtype), jax.ShapeDtypeStruct((t, D), bf16)],
        compiler_params=_params(("parallel",), VMEM_LIMIT),
    )(dyb, rest, act, w)


def _rnn_bwd(dh, a, h, xc, rest, conv_w, conv_b, wa_d, wx_d, ba, bx, lam, seq):
    t = dh.shape[0]
    nb, nt = t // seq, seq // TL
    diag = (D // LANES, LANES, LANES)

    def body(dh_ref, a_ref, h_ref, hprev_ref, xc_ref, x_ref, xprev_ref, cw_ref, cb_ref, wa_ref, wx_ref,
             ba_ref, bx_ref, lam_ref, dxr_ref, dwa_ref, dwx_ref, vec_ref, g_scr, dxc_scr, dxr_scr, qcarry, dxc_next):
        b, tt = pl.program_id(0), pl.program_id(1)
        rt = nt - 1 - tt

        @pl.when((b == 0) & (tt == 0))
        def _():
            dwa_ref[...] = jnp.zeros_like(dwa_ref)
            dwx_ref[...] = jnp.zeros_like(dwx_ref)
            vec_ref[...] = jnp.zeros_like(vec_ref)

        @pl.when(tt == 0)
        def _():
            qcarry[...] = jnp.zeros_like(qcarry)
            dxc_next[...] = jnp.zeros_like(dxc_next)

        def step(k, q):
            s = TL - 1 - k
            g = dh_ref[pl.ds(s, 1), :] + q
            g_scr[pl.ds(s, 1), :] = g
            return a_ref[pl.ds(s, 1), :] * g

        qcarry[...] = lax.fori_loop(0, TL, step, qcarry[...], unroll=8)

        row = lax.broadcasted_iota(jnp.int32, (TL, D), 0)
        row8 = lax.broadcasted_iota(jnp.int32, (8, D), 0)
        g = g_scr[...]
        av = a_ref[...]
        xc = xc_ref[...]
        hlast = jnp.where(rt > 0, hprev_ref[pl.ds(7, 1), :], 0.0)
        hp = jnp.where(row == 0, hlast, pltpu.roll(h_ref[...], 1, 0))
        r, i, sp, log_a, w1, sq = _rnn_gates(xc, wa_ref, wx_ref, ba_ref, bx_ref, lam_ref)
        dix = g * sq
        di = dix * xc
        dxc = dix * i
        dsq = g * (i * xc)
        dlog_a = g * hp * av - dsq * jnp.where(sq > 0.0, (1.0 - w1) / sq, 0.0)
        dpr = (dlog_a * ((-RG_C) * sp)) * r * (1.0 - r)
        dpi = di * i * (1.0 - i)
        dprb, dpib, xcb = dpr.astype(bf16), dpi.astype(bf16), xc.astype(bf16)
        dxc = dxc + _dot_nt(dprb, wa_ref[...]) + _dot_nt(dpib, wx_ref[...])
        for j in range(D // LANES):
            cols = slice(j * LANES, (j + 1) * LANES)
            dwa_ref[j] += _dot_tn(xcb[:, cols], dprb[:, cols])
            dwx_ref[j] += _dot_tn(xcb[:, cols], dpib[:, cols])
        vec_ref[pl.ds(0, 1), :] += jnp.sum(dpr, axis=0, keepdims=True)
        vec_ref[pl.ds(1, 1), :] += jnp.sum(dpi, axis=0, keepdims=True)
        dsp = jnp.sum(dlog_a * ((-RG_C) * r), axis=0, keepdims=True)
        vec_ref[pl.ds(2, 1), :] += dsp * (-_sig(-lam_ref[...]))
        vec_ref[pl.ds(3, 1), :] += jnp.sum(dxc, axis=0, keepdims=True)

        dxc_scr[...] = dxc
        bot8 = dxc_scr[pl.ds(TL - 8, 8), :]
        nxt8 = dxc_next[...]
        dxr = cw_ref[pl.ds(3, 1), :] * dxc
        dxr8 = cw_ref[pl.ds(3, 1), :] * bot8
        for sh in range(1, 4):
            w = cw_ref[pl.ds(3 - sh, 1), :]
            dxr = dxr + w * pltpu.roll(dxc, TL - sh, 0)
            dxr8 = dxr8 + w * jnp.where(row8 < 8 - sh, pltpu.roll(bot8, 8 - sh, 0), pltpu.roll(nxt8, 8 - sh, 0))
        dxr_scr[...] = dxr
        dxr_scr[pl.ds(TL - 8, 8), :] = dxr8
        dxr_ref[...] = dxr_scr[...].astype(bf16)
        dxc_next[...] = dxc_scr[pl.ds(0, 8), :]

        x = x_ref[...]
        prev8 = jnp.where(rt > 0, xprev_ref[...], 0.0)
        dxc_top8 = dxc_scr[pl.ds(0, 8), :]
        vec_ref[pl.ds(7, 1), :] += jnp.sum(dxc * x, axis=0, keepdims=True)
        for sh in range(1, 4):
            inside = jnp.sum(dxc * jnp.where(row >= sh, pltpu.roll(x, sh, 0), 0.0), axis=0, keepdims=True)
            above = jnp.sum(dxc_top8 * jnp.where(row8 < sh, pltpu.roll(prev8, sh, 0), 0.0), axis=0, keepdims=True)
            vec_ref[pl.ds(7 - sh, 1), :] += inside + above

    tile = lambda cb: pl.BlockSpec((TL, D), lambda b, tt, cb=cb: (b * nt + nt - 1 - tt, cb))
    prev = lambda cb: pl.BlockSpec(
        (8, D), lambda b, tt, cb=cb: (jnp.maximum((b * nt + nt - 1 - tt) * (TL // 8) - 1, 0), cb))
    vec = _whole((1, D))
    return pl.pallas_call(
        body, name="rnn_bwd", grid=(nb, nt),
        in_specs=[tile(0), tile(0), tile(0), prev(0), tile(0), tile(1), prev(1),
                  _whole((4, D)), vec, _whole((D, D)), _whole((D, D)), vec, vec, vec],
        out_specs=[tile(0), _whole(diag), _whole(diag), _whole((8, D))],
        out_shape=[jax.ShapeDtypeStruct((t, D), bf16), jax.ShapeDtypeStruct(diag, f32),
                   jax.ShapeDtypeStruct(diag, f32), jax.ShapeDtypeStruct((8, D), f32)],
        scratch_shapes=[pltpu.VMEM((TL, D), f32), pltpu.VMEM((TL, D), f32), pltpu.VMEM((TL, D), f32),
                        pltpu.VMEM((1, D), f32), pltpu.VMEM((8, D), f32)],
        compiler_params=_params(("arbitrary", "arbitrary"), VMEM_LIMIT),
    )(dh, a, h, h, xc, rest, rest, conv_w, conv_b, wa_d, wx_d, ba, bx, lam)


def _attn_delta(doa, o):
    t = doa.shape[0]

    def body(do_ref, o_ref, d_ref):
        prod = do_ref[...].astype(f32) * o_ref[...]
        ch = lax.broadcasted_iota(jnp.int32, (D, LANES), 0)
        hd = lax.broadcasted_iota(jnp.int32, (D, LANES), 1)
        pick = (ch // 64 == hd).astype(f32)
        per_head = jnp.dot(prod, pick, preferred_element_type=f32, precision=lax.Precision.HIGHEST)
        d_ref[0] = per_head.T[:HEADS, :]

    return pl.pallas_call(
        body, name="attn_delta", grid=(t // TQ,),
        in_specs=[_tile(TQ, D), _tile(TQ, D)],
        out_specs=pl.BlockSpec((1, HEADS, TQ), lambda i: (i, 0, 0)),
        out_shape=jax.ShapeDtypeStruct((t // TQ, HEADS, TQ), f32),
        compiler_params=_params(("parallel",)),
    )(doa, o)


def _attn_bwd(qa, ka, qkv, doa, lse, delta, seq):
    t = qkv.shape[0]
    nb, nq = t // seq, seq // TQ
    hg = ATT_GROUP
    ng, npair = HEADS // hg, hg // 2

    def body(qa_ref, ka_ref, q_ref, k_ref, v_ref, do_ref, lse_ref, dl_ref, dq_ref, dk_ref, dv_ref, dc_ref,
             dqt_scr, dk_scr, dv_scr, ds_scr, kht_scr):
        gi, kt = pl.program_id(1), pl.program_id(2)
        lane = lax.broadcasted_iota(jnp.int32, (1, LANES), 1)
        krow = lax.broadcasted_iota(jnp.int32, (TQ, TQ), 0)
        qcol = lax.broadcasted_iota(jnp.int32, (TQ, TQ), 1)
        lmask = [(lane // 64) == hh for hh in range(2)]
        scale = jnp.asarray(QK_SCALE, bf16)

        @pl.when(kt == 0)
        def _():
            dqt_scr[...] = jnp.zeros_like(dqt_scr)

        dk_scr[...] = jnp.zeros_like(dk_scr)
        dv_scr[...] = jnp.zeros_like(dv_scr)
        ds_scr[...] = jnp.zeros_like(ds_scr)
        for g in range(hg):
            k2 = k_ref[:, pl.ds((g // 2) * LANES, LANES)]
            kht_scr[g] = jnp.where(lmask[g % 2], k2, jnp.zeros_like(k2)).T

        def q_step(qt, masked):
            qs = pl.multiple_of(qt * TQ, TQ)
            heads = range(hg)
            do2 = [do_ref[pl.ds(qs, TQ), pl.ds(j * LANES, LANES)] for j in range(npair)]
            q2 = [q_ref[pl.ds(qs, TQ), pl.ds(j * LANES, LANES)] for j in range(npair)]
            doh = [jnp.where(lmask[g % 2], do2[g // 2], jnp.zeros_like(do2[0])) for g in heads]
            qh = [jnp.where(lmask[g % 2], q2[g // 2], jnp.zeros_like(q2[0])) * scale for g in heads]
            st = [_dot_nt(ka_ref[:, pl.ds(g * LANES, LANES)], qa_ref[pl.ds(qs, TQ), pl.ds(g * LANES, LANES)])
                  for g in heads]
            if masked:
                st = [jnp.where(krow <= qcol, s, MASK_VALUE) for s in st]
            dp = [_dot_nt(v_ref[:, pl.ds((g // 2) * LANES, LANES)], doh[g]) for g in heads]
            p = [jnp.exp(st[g] - lse_ref[qt, pl.ds(hg * gi + g, 1), :]) for g in heads]
            ds = [p[g] * (dp[g] - dl_ref[qt, pl.ds(hg * gi + g, 1), :]) for g in heads]
            pb = [x.astype(bf16) for x in p]
            dsb = [x.astype(bf16) for x in ds]
            for j in range(npair):
                a, b = 2 * j, 2 * j + 1
                dv_scr[j] += _dot(pb[a], doh[a]) + _dot(pb[b], doh[b])
                dk_scr[j] += _dot(dsb[a], qh[a]) + _dot(dsb[b], qh[b])
                dqt_scr[qt, j] += (_dot(kht_scr[a], dsb[a]) + _dot(kht_scr[b], dsb[b])) * QK_SCALE
            for g in heads:
                ds_scr[g] += ds[g][:, :LANES] + ds[g][:, LANES:]

        q_step(kt, True)

        def loop_body(qt, carry):
            q_step(qt, False)
            return carry

        lax.fori_loop(kt + 1, nq, loop_body, 0)

        dc = jnp.zeros((TQ, LANES), f32)
        for g in range(hg):
            dc = jnp.where(lane == g, -jnp.sum(ds_scr[g], axis=1, keepdims=True), dc)
        dc_ref[...] = dc
        for j in range(npair):
            dk_ref[:, pl.ds(j * LANES, LANES)] = dk_scr[j].astype(bf16)
            dv_ref[:, pl.ds(j * LANES, LANES)] = dv_scr[j].astype(bf16)

        @pl.when(kt == nq - 1)
        def _():
            for qt in range(nq):
                for j in range(npair):
                    dq_ref[pl.ds(qt * TQ, TQ), pl.ds(j * LANES, LANES)] = dqt_scr[qt, j].T.astype(bf16)

    vw = hg * 64
    seqspec = pl.BlockSpec((seq, vw), lambda b, gi, kt: (b, gi))
    kspec = lambda off: pl.BlockSpec((TQ, vw), lambda b, gi, kt: (b * nq + kt, off + gi))
    rowspec = pl.BlockSpec((nq, HEADS, TQ), lambda b, gi, kt: (b, 0, 0))
    return pl.pallas_call(
        body, name="attn_bwd", grid=(nb, ng, nq),
        in_specs=[pl.BlockSpec((seq, hg * LANES), lambda b, gi, kt: (b, gi)),
                  pl.BlockSpec((TQ, hg * LANES), lambda b, gi, kt: (b * nq + kt, gi)),
                  seqspec, kspec(ng), kspec(2 * ng), seqspec, rowspec, rowspec],
        out_specs=[seqspec, kspec(0), kspec(0), pl.BlockSpec((TQ, LANES), lambda b, gi, kt: (b * nq + kt, gi))],
        out_shape=[jax.ShapeDtypeStruct((t, D), bf16)] * 3 + [jax.ShapeDtypeStruct((t, ng * LANES), f32)],
        scratch_shapes=[pltpu.VMEM((nq, npair, LANES, TQ), f32), pltpu.VMEM((npair, TQ, LANES), f32),
                        pltpu.VMEM((npair, TQ, LANES), f32), pltpu.VMEM((hg, TQ, LANES), f32),
                        pltpu.VMEM((hg, LANES, TQ), bf16)],
        compiler_params=_params(("parallel", "parallel", "arbitrary"), VMEM_LIMIT),
    )(qa, ka, qkv, qkv, qkv, doa, lse, delta)


def _forget_bwd(dc, f128, seq):
    t = f128.shape[0]
    nb = seq // LANES

    def body(dc_ref, f_ref, df_ref, dbf_ref):
        @pl.when(pl.program_id(0) == 0)
        def _():
            dbf_ref[...] = jnp.zeros_like(dbf_ref)

        r = lax.broadcasted_iota(jnp.int32, (LANES, LANES), 0)
        cidx = lax.broadcasted_iota(jnp.int32, (LANES, LANES), 1)
        tri = (r <= cidx).astype(f32)
        carry = jnp.zeros((1, LANES), f32)
        total = jnp.zeros((1, LANES), f32)
        for blk in reversed(range(nb)):
            dcb = dc_ref[pl.ds(blk * LANES, LANES), :]
            dlf = jnp.dot(tri, dcb, preferred_element_type=f32, precision=lax.Precision.HIGHEST) + carry
            df = dlf * _sig(-f_ref[pl.ds(blk * LANES, LANES), :])
            df_ref[pl.ds(blk * LANES, LANES), :] = df.astype(bf16)
            total = total + jnp.sum(df, axis=0, keepdims=True)
            carry = carry + jnp.sum(dcb, axis=0, keepdims=True)
        dbf_ref[...] += total

    return pl.pallas_call(
        body, name="forget_bwd", grid=(t // seq,),
        in_specs=[pl.BlockSpec((seq, LANES), lambda b: (b, 0)), pl.BlockSpec((seq, LANES), lambda b: (b, 0))],
        out_specs=[pl.BlockSpec((seq, LANES), lambda b: (b, 0)), _whole((1, LANES))],
        out_shape=[jax.ShapeDtypeStruct((t, LANES), bf16), jax.ShapeDtypeStruct((1, LANES), f32)],
        compiler_params=_params(("arbitrary",)),
    )(dc, f128)


def _in_bwd(dz, df, x, dy, w_qkv, w_rest, w_f, w_pre):
    t = x.shape[0]
    n_qkv = w_qkv.shape[1] // D
    n_rest = w_rest.shape[1] // D

    def body(*refs):
        dz_refs = refs[:n_qkv + n_rest]
        df_ref, x_ref, dy_ref, wq_ref, wr_ref, wf_ref, wp_ref, gx_ref, dwp_ref = refs[n_qkv + n_rest:]

        @pl.when(pl.program_id(0) == 0)
        def _():
            dwp_ref[...] = jnp.zeros_like(dwp_ref)

        dh = _dot_nt(df_ref[...], wf_ref[...])
        for p in range(n_qkv):
            dh = dh + _dot_nt(dz_refs[p][...], wq_ref[:, pl.ds(p * D, D)])
        for p in range(n_rest):
            dh = dh + _dot_nt(dz_refs[n_qkv + p][...], wr_ref[:, pl.ds(p * D, D)])
        xv = x_ref[...]
        r1 = lax.rsqrt(jnp.mean(xv * xv, axis=-1, keepdims=True) + NORM_EPS)
        xh = xv * r1
        dwp_ref[...] += jnp.sum(dh * xh, axis=0, keepdims=True)
        dxh = dh * wp_ref[...]
        gx_ref[...] = dy_ref[...] + r1 * (dxh - xh * jnp.mean(dxh * xh, axis=-1, keepdims=True))

    once = lambda shape: pl.BlockSpec(shape, lambda i: (0, 0), pipeline_mode=pl.Buffered(1))
    return pl.pallas_call(
        body, name="in_bwd", grid=(t // TM,),
        in_specs=[_tile(TM, D)] * (n_qkv + n_rest) + [_tile(TM, LANES), _tile(TM, D), _tile(TM, D),
                  once(w_qkv.shape), once(w_rest.shape), once(w_f.shape), _whole((1, D))],
        out_specs=[_tile(TM, D), _whole((1, D))],
        out_shape=[jax.ShapeDtypeStruct((t, D), f32), jax.ShapeDtypeStruct((1, D), f32)],
        compiler_params=_params(("arbitrary",), VMEM_LIMIT),
    )(*dz, df, x, dy, w_qkv, w_rest, w_f, w_pre)


def _tn_mm(name, a, b, tn, tk=2048):
    t, k = a.shape
    tk = min(tk, t)
    n = b.shape[1]

    def body(a_ref, b_ref, o_ref, s_ref):
        @pl.when(pl.program_id(1) == 0)
        def _():
            o_ref[...] = jnp.zeros_like(o_ref)
            s_ref[...] = jnp.zeros_like(s_ref)

        bv = b_ref[...]
        o_ref[...] += _dot_tn(a_ref[...], bv)
        s_ref[...] += jnp.sum(bv.astype(f32), axis=0, keepdims=True)

    return pl.pallas_call(
        body, name=name, grid=(n // tn, t // tk),
        in_specs=[pl.BlockSpec((tk, k), lambda j, kk: (kk, 0)), pl.BlockSpec((tk, tn), lambda j, kk: (kk, j))],
        out_specs=[pl.BlockSpec((k, tn), lambda j, kk: (0, j)), pl.BlockSpec((1, tn), lambda j, kk: (0, j))],
        out_shape=[jax.ShapeDtypeStruct((k, n), f32), jax.ShapeDtypeStruct((1, n), f32)],
        compiler_params=_params(("parallel", "arbitrary"), VMEM_LIMIT),
    )(a, b)


def _position():
    return lax.axis_index("x"), lax.axis_index("y"), lax.axis_index("c")


def _gather_shards(parts, small):
    n = len(parts)
    halves = [p.shape[0] // 2 for p in parts]

    def body(*refs):
        srcs, small_src = refs[:n], refs[n]
        dsts, small_dst = refs[n + 1:2 * n + 1], refs[2 * n + 1]
        send, recv, local = refs[2 * n + 2:]
        x, y, c = _position()
        me = 2 * x + y
        chips = [(1 - x, y), (x, 1 - y), (1 - x, 1 - y)]
        ids = [2 * px + py for px, py in chips]

        def half(a, shard, which):
            return dsts[a].at[shard, pl.ds(which * halves[a], halves[a]), :]

        def over_ici(a, j, shard):
            px, py = chips[j]
            return pltpu.make_async_remote_copy(
                src_ref=srcs[a].at[pl.ds(c * halves[a], halves[a]), :], dst_ref=half(a, shard, c),
                send_sem=send.at[a * 3 + j], recv_sem=recv.at[a * 3 + j], device_id=(px, py, c), device_id_type=MESH)

        def to_sibling(a, j, which):
            k = 3 * n + a * 3 + j
            return pltpu.make_async_remote_copy(
                src_ref=half(a, ids[j], which), dst_ref=half(a, ids[j], which), send_sem=send.at[k],
                recv_sem=recv.at[k], device_id=(x, y, 1 - c), device_id_type=MESH)

        def small_copy(j, shard):
            px, py = chips[j]
            return pltpu.make_async_remote_copy(
                src_ref=small_src, dst_ref=small_dst.at[shard], send_sem=send.at[6 * n + j], recv_sem=recv.at[6 * n + j],
                device_id=(px, py, c), device_id_type=MESH)

        own = [pltpu.make_async_copy(srcs[a], dsts[a].at[me], local.at[a]) for a in range(n)]
        own.append(pltpu.make_async_copy(small_src, small_dst.at[me], local.at[n]))
        for cp in own:
            cp.start()
        first = [over_ici(a, j, me) for j in range(3) for a in range(n)] + [small_copy(j, me) for j in range(3)]
        for cp in first:
            cp.start()
        passed = []
        for j in range(3):
            for a in range(n):
                over_ici(a, j, ids[j]).wait_recv()
                passed.append(to_sibling(a, j, c))
                passed[-1].start()
        for j in range(3):
            small_copy(j, ids[j]).wait_recv()
            for a in range(n):
                to_sibling(a, j, 1 - c).wait_recv()
        for cp in first + passed:
            cp.wait_send()
        for cp in own:
            cp.wait()

    vm = pl.BlockSpec(memory_space=pltpu.VMEM)
    return pl.pallas_call(
        body, name="gather_shards",
        in_specs=[vm] * (n + 1), out_specs=[vm] * (n + 1),
        out_shape=[jax.ShapeDtypeStruct((N_CHIPS,) + p.shape, p.dtype) for p in parts + [small]],
        scratch_shapes=[pltpu.SemaphoreType.DMA((6 * n + 3,)), pltpu.SemaphoreType.DMA((6 * n + 3,)),
                        pltpu.SemaphoreType.DMA((n + 1,))],
        compiler_params=pltpu.CompilerParams(vmem_limit_bytes=VMEM_LIMIT),
    )(*parts, small)


def _allsum_rows(part):
    rows_n = part.shape[0]

    def body(x_ref, gath_ref, sum_ref, send_sems, recv_sems, local_sem):
        x, y, c = _position()
        me, sibling = (x, y, c), (x, y, 1 - c)
        chips = [(1 - x, y), (x, 1 - y), (1 - x, 1 - y)]

        def rows(px, py, pc):
            return gath_ref.at[pl.ds((4 * px + 2 * py + pc) * rows_n, rows_n), :]

        def copy(k, block, to, src=None):
            return pltpu.make_async_remote_copy(
                src_ref=rows(*block) if src is None else src, dst_ref=rows(*block),
                send_sem=send_sems.at[k], recv_sem=recv_sems.at[k], device_id=to, device_id_type=MESH)

        mine = pltpu.make_async_copy(x_ref, rows(*me), local_sem)
        mine.start()
        first = [copy(0, me, sibling, src=x_ref)]
        first += [copy(1 + j, me, (*chip, c), src=x_ref) for j, chip in enumerate(chips)]
        for cp in first:
            cp.start()
        passed = [copy(4 + j, (*chip, c), sibling) for j, chip in enumerate(chips)]
        for j, chip in enumerate(chips):
            copy(1 + j, (*chip, c), me).wait_recv()
            passed[j].start()
        copy(0, sibling, me).wait_recv()
        for j, chip in enumerate(chips):
            copy(4 + j, (*chip, 1 - c), me).wait_recv()
        for cp in first + passed:
            cp.wait_send()
        mine.wait()
        total = gath_ref[pl.ds(0, rows_n), :]
        for d in range(1, N_DEV):
            total = total + gath_ref[pl.ds(d * rows_n, rows_n), :]
        sum_ref[...] = total

    vm = pl.BlockSpec(memory_space=pltpu.VMEM)
    return pl.pallas_call(
        body, name="allsum_rows", in_specs=[vm], out_specs=[vm, vm],
        out_shape=[jax.ShapeDtypeStruct((N_DEV * rows_n, D), f32), jax.ShapeDtypeStruct((rows_n, D), f32)],
        scratch_shapes=[pltpu.SemaphoreType.DMA((7,)), pltpu.SemaphoreType.DMA((7,)), pltpu.SemaphoreType.DMA],
    )(part)[1]


PAIR_ROWS = 64


def _pair_reduce(name, pieces):
    _, r, n = pieces.shape

    def body(p_ref, o_ref, land, send, recv):
        x, y, c = _position()

        def remote(j, half):
            return pltpu.make_async_remote_copy(
                src_ref=p_ref.at[2 * j + half], dst_ref=land.at[j], send_sem=send.at[j], recv_sem=recv.at[j],
                device_id=(x, y, 1 - c), device_id_type=MESH)

        sends = [remote(j, 1 - c) for j in range(N_CHIPS)]
        for cp in sends:
            cp.start()
        for j in range(N_CHIPS):
            remote(j, c).wait_recv()

            def add_rows(i, carry, j=j):
                rows = pl.ds(pl.multiple_of(i * PAIR_ROWS, PAIR_ROWS), PAIR_ROWS)
                o_ref[j, rows, :] = (p_ref[2 * j + c, rows, :].astype(f32) + land[j, rows, :].astype(f32)).astype(bf16)
                return carry

            lax.fori_loop(0, r // PAIR_ROWS, add_rows, 0)
        for cp in sends:
            cp.wait_send()

    vm = pl.BlockSpec(memory_space=pltpu.VMEM)
    return pl.pallas_call(
        body, name=name, in_specs=[vm], out_specs=vm,
        out_shape=jax.ShapeDtypeStruct((N_CHIPS, r, n), bf16),
        scratch_shapes=[pltpu.VMEM((N_CHIPS, r, n), bf16), pltpu.SemaphoreType.DMA((N_CHIPS,)),
                        pltpu.SemaphoreType.DMA((N_CHIPS,))],
        compiler_params=pltpu.CompilerParams(vmem_limit_bytes=VMEM_LIMIT),
    )(pieces)


def _chip_exchange(arrs):
    n = len(arrs)

    def body(*refs):
        srcs, dsts = refs[:n], refs[n:2 * n]
        send, recv, local = refs[2 * n:]
        x, y, c = _position()
        me = 2 * x + y
        chips = [(1 - x, y), (x, 1 - y), (1 - x, 1 - y)]

        def remote(a, j, piece, landing):
            px, py = chips[j]
            return pltpu.make_async_remote_copy(
                src_ref=srcs[a].at[piece], dst_ref=dsts[a].at[landing], send_sem=send.at[a * 3 + j],
                recv_sem=recv.at[a * 3 + j], device_id=(px, py, c), device_id_type=MESH)

        own = [pltpu.make_async_copy(srcs[a].at[me], dsts[a].at[me], local.at[a]) for a in range(n)]
        sends = [remote(a, j, 2 * px + py, me) for j, (px, py) in enumerate(chips) for a in range(n)]
        for cp in sends + own:
            cp.start()
        for j, (px, py) in enumerate(chips):
            for a in range(n):
                remote(a, j, me, 2 * px + py).wait_recv()
        for cp in sends:
            cp.wait_send()
        for cp in own:
            cp.wait()

    anyspec = pl.BlockSpec(memory_space=pl.ANY)
    return pl.pallas_call(
        body, name="chip_exchange", in_specs=[anyspec] * n, out_specs=[anyspec] * n,
        out_shape=[jax.ShapeDtypeStruct(a.shape, a.dtype) for a in arrs],
        scratch_shapes=[pltpu.SemaphoreType.DMA((3 * n,)), pltpu.SemaphoreType.DMA((3 * n,)),
                        pltpu.SemaphoreType.DMA((n,))],
    )(*arrs)


def _swap_halves(arrs):
    n = len(arrs)

    def body(*refs):
        srcs, dsts = refs[:n], refs[n:2 * n]
        send, recv, local = refs[2 * n:]
        x, y, c = _position()

        def remote(a, landing):
            return pltpu.make_async_remote_copy(
                src_ref=srcs[a], dst_ref=dsts[a].at[landing], send_sem=send.at[a], recv_sem=recv.at[a],
                device_id=(x, y, 1 - c), device_id_type=MESH)

        own = [pltpu.make_async_copy(srcs[a], dsts[a].at[c], local.at[a]) for a in range(n)]
        sends = [remote(a, c) for a in range(n)]
        for cp in sends + own:
            cp.start()
        for a in range(n):
            remote(a, 1 - c).wait_recv()
        for cp in sends:
            cp.wait_send()
        for cp in own:
            cp.wait()

    vm = pl.BlockSpec(memory_space=pltpu.VMEM)
    return pl.pallas_call(
        body, name="swap_halves", in_specs=[vm] * n, out_specs=[vm] * n,
        out_shape=[jax.ShapeDtypeStruct((2,) + a.shape, a.dtype) for a in arrs],
        scratch_shapes=[pltpu.SemaphoreType.DMA((n,)), pltpu.SemaphoreType.DMA((n,)), pltpu.SemaphoreType.DMA((n,))],
        compiler_params=pltpu.CompilerParams(vmem_limit_bytes=VMEM_LIMIT),
    )(*arrs)


def _row_block(r):
    return 128 if r % 128 == 0 else r


def _sum_slots(name, slots):
    s, r, n = slots.shape
    rb = _row_block(r)

    def body(s_ref, o_ref):
        total = s_ref[0].astype(f32)
        for d in range(1, s):
            total = total + s_ref[d].astype(f32)
        o_ref[...] = total

    return pl.pallas_call(
        body, name=name, grid=(r // rb,),
        in_specs=[pl.BlockSpec((s, rb, n), lambda i: (0, i, 0))],
        out_specs=pl.BlockSpec((rb, n), lambda i: (i, 0)),
        out_shape=jax.ShapeDtypeStruct((r, n), f32),
        compiler_params=_params(("parallel",), VMEM_LIMIT),
    )(slots)


def _adamw(name, w, g, m, v):
    r, n = w.shape
    rb = _row_block(r)

    def body(w_ref, g_ref, m_ref, v_ref, d_ref, nm_ref, nv_ref):
        gv = g_ref[...]
        m2 = ADAM_B1 * m_ref[...] + (1.0 - ADAM_B1) * gv
        v2 = ADAM_B2 * v_ref[...] + (1.0 - ADAM_B2) * (gv * gv)
        m_hat = m2 / (1.0 - ADAM_B1 ** ADAM_STEP)
        v_hat = v2 / (1.0 - ADAM_B2 ** ADAM_STEP)
        d_ref[...] = (-ADAM_LR) * (m_hat / (jnp.sqrt(v_hat) + ADAM_EPS) + ADAM_WD * w_ref[...])
        nm_ref[...] = m2
        nv_ref[...] = v2

    spec = pl.BlockSpec((rb, n), lambda i: (i, 0))
    return pl.pallas_call(
        body, name=name, grid=(r // rb,), in_specs=[spec] * 4, out_specs=[spec] * 3,
        out_shape=[jax.ShapeDtypeStruct((r, n), f32)] * 3,
        compiler_params=_params(("parallel",), VMEM_LIMIT),
    )(w, g, m, v)


def _identity(a):
    return a


def _local_step(x2, tgt2, seq, wt):
    nb = x2.shape[0] // seq
    h = _prenorm(x2, wt["pre_w"])
    qkv = _mm("in_qkv", [(h, 0)], _identity, wt["w_qkv"], wt["b_qkv"], bf16, 512, 1024)
    rest = _mm("in_rest", [(h, 0)], _identity, wt["w_rest"], wt["b_rest"], f32, 512, 1024)
    f128 = _mm("in_f", [(h, 0)], _identity, wt["w_f"], wt["b_f"], f32, 512, LANES)
    c = _forget_prep(f128, seq)
    qa, ka = _attn_prep(qkv, c)
    o_att, pa, lse = _attn_fwd(qa, ka, qkv, rest, seq)
    ya = _mm("proj_a", [(pa, 0)], _identity, wt["w_a"], None, f32, 512, D)
    rnn_w = (wt["conv_w"], wt["conv_b"], wt["wa_d"], wt["wx_d"], wt["ba"], wt["bx"], wt["lam"])
    xc, a, hrec, pr = _rnn_fwd(rest, *rnn_w, seq)
    yr = _mm("proj_r", [(pr, 0)], _identity, wt["w_r"], None, f32, 512, D)
    o, mrg = _mm("proj_out", [(rest, 3), (rest, 4), (ya, 0), (yr, 0)], _merge, wt["w_o"], None, f32, TM, D,
                 keep_lhs=True)

    do, dy, loss8, d_post = _post_loss(o, x2, tgt2, wt["post_w"])
    dya, dyr, dmga, dmgr = _out_bwd(do, rest, ya, yr, wt["w_o"])
    doa, dga = _branch_bwd("branch_a_bwd", dya, rest, 0, o_att, wt["w_a"], bf16)
    dhrec, dgr = _branch_bwd("branch_r_bwd", dyr, rest, 2, hrec, wt["w_r"], f32)
    d_wo, _ = _tn_mm("dw_out", mrg, do, 512)
    d_wa, _ = _tn_mm("dw_branch_a", pa, dya, 512)
    d_wr, _ = _tn_mm("dw_branch_r", pr, dyr, 512)
    dxr, d_wad, d_wxd, vec = _rnn_bwd(dhrec, a, hrec, xc, rest, *rnn_w, seq)
    dq, dk, dv, dc_pairs = _attn_bwd(qa, ka, qkv, doa, lse, _attn_delta(doa, o_att), seq)
    dc = dc_pairs.reshape(-1, HEADS // ATT_GROUP, LANES)[:, :, :ATT_GROUP].reshape(-1, HEADS)
    df, db_f = _forget_bwd(_pad_cols(dc, LANES), f128, seq)
    pieces = [dq, dk, dv, dga, dxr, dgr, dmga, dmgr]
    gx, d_pre = _in_bwd(pieces, df, x2, dy, wt["w_qkv"], wt["w_rest"], wt["w_f"], wt["pre_w"])
    names = ["q", "k", "v", "ga", "xr", "gr", "mga", "mgr"]
    dws, dbs = [], []
    for nm, piece in zip(names, pieces):
        dw_p, db_p = _tn_mm("dw_in_" + nm, h, piece, 512)
        dws.append(dw_p)
        dbs.append(db_p)
    dw_f, _ = _tn_mm("dw_in_f", h, df, LANES)
    zeros_w = jnp.zeros((D, IN_TOTAL - IN_USED), f32)
    d_w_in = jnp.concatenate(dws[:3] + [dw_f[:, :HEADS]] + dws[3:] + [zeros_w], axis=1)
    d_b_in = jnp.concatenate(dbs[:3] + [db_f[:, :HEADS]] + dbs[3:] + [zeros_w[:1]], axis=1)
    return dict(loss=loss8[0, 0], grad_x=gx, pre_w=d_pre, w_in=d_w_in, b_in=d_b_in, conv_w=vec[4:8], conv_b=vec[3:4],
                wa_d=d_wad, ba=vec[0:1], wx_d=d_wxd, bx=vec[1:2], lam=vec[2:3], w_a=d_wa, w_r=d_wr, w_o=d_wo,
                post_w=d_post)


def _block_diag(w):
    g, bw, _ = w.shape
    eye = jnp.eye(g, dtype=w.dtype)
    return (w[:, :, None, :] * eye[:, None, :, None]).reshape(g * bw, g * bw)


def _gate_blocks(diag):
    half = diag.shape[1] // 2
    return jnp.stack([diag[:, :half, :half], diag[:, half:, half:]], axis=1).reshape(-1, half, half)


def _pad_cols(a, n):
    return jnp.pad(a, ((0, 0), (0, n - a.shape[1])))


def _pad_rows(a, n):
    return jnp.pad(a, ((0, n - a.shape[0]), (0, 0)))


def kernel(x, pre_norm_w, w_in, b_in, conv_w, conv_b, rg_wa, rg_ba, rg_wx, rg_bx, rg_lambda, w_branch_a, w_branch_r, w_out, post_norm_w, loss_target, m_pre_norm_w, m_w_in, m_b_in, m_conv_w, m_conv_b, m_rg_wa, m_rg_ba, m_rg_wx, m_rg_bx, m_rg_lambda, m_w_branch_a, m_w_branch_r, m_w_out, m_post_norm_w, v_pre_norm_w, v_w_in, v_b_in, v_conv_w, v_conv_b, v_rg_wa, v_rg_ba, v_rg_wx, v_rg_bx, v_rg_lambda, v_w_branch_a, v_w_branch_r, v_w_out, v_post_norm_w):
    nb, seq, _ = x.shape
    chip = 2 * lax.axis_index("x") + lax.axis_index("y")
    n_groups = rg_wa.shape[1]

    g_in, g_a, g_r, g_o, g_cw = _gather_shards(
        [w_in[0].astype(bf16), w_branch_a[0].astype(bf16), w_branch_r[0].astype(bf16), w_out[0].astype(bf16)],
        conv_w[0])
    w_full = jnp.transpose(g_in, (1, 0, 2)).reshape(D, IN_TOTAL)
    q_end, f_end = 3 * D, 3 * D + HEADS
    wt = dict(
        pre_w=pre_norm_w, post_w=post_norm_w,
        w_qkv=w_full[:, :q_end], b_qkv=b_in[:, :q_end],
        w_f=_pad_cols(w_full[:, q_end:f_end], LANES), b_f=_pad_cols(b_in[:, q_end:f_end], LANES),
        w_rest=w_full[:, f_end:IN_USED], b_rest=b_in[:, f_end:IN_USED],
        w_a=g_a.reshape(D, D), w_r=g_r.reshape(D, D), w_o=g_o.reshape(D, D),
        conv_w=jnp.transpose(g_cw, (1, 0, 2)).reshape(4, D), conv_b=conv_b,
        wa_d=_block_diag(rg_wa[0]).astype(bf16), wx_d=_block_diag(rg_wx[0]).astype(bf16),
        ba=rg_ba, bx=rg_bx, lam=rg_lambda)

    part = _local_step(x.reshape(nb * seq, D), loss_target.reshape(nb * seq, D), seq, wt)
    loss = lax.psum(part["loss"], ("x", "y", "c"))
    grad_x = part["grad_x"].reshape(nb, seq, D)

    small = jnp.concatenate([
        part["pre_w"], _pad_cols(part["b_in"], 10 * D).reshape(10, D), part["conv_b"],
        _gate_blocks(part["wa_d"]).reshape(-1, D), part["ba"],
        _gate_blocks(part["wx_d"]).reshape(-1, D), part["bx"], part["lam"], part["post_w"],
        part["conv_w"]], axis=0)
    n_small = small.shape[0]
    n_rep = n_small - 4
    tot = _allsum_rows(_pad_rows(small, -(-n_small // 8) * 8))
    g_rep = tot[:n_rep]
    g_conv_w = lax.dynamic_slice_in_dim(tot[n_rep:n_small], chip * (D // N_CHIPS), D // N_CHIPS, axis=1)

    def pack(pre, b, cb, wa, ba, wx, bx, lam, post):
        return jnp.concatenate([pre, _pad_cols(b, 10 * D).reshape(10, D), cb, wa.reshape(-1, D), ba,
                                wx.reshape(-1, D), bx, lam, post], axis=0)

    def unpack(p):
        o = [0]

        def take(k):
            o[0] += k
            return p[o[0] - k:o[0]]

        pre = take(1)
        b = take(10).reshape(1, 10 * D)[:, :IN_TOTAL]
        cb = take(1)
        wa = take(64).reshape(rg_wa.shape)
        ba = take(1)
        wx = take(64).reshape(rg_wx.shape)
        bx = take(1)
        lam = take(1)
        post = take(1)
        return dict(pre_norm_w=pre, b_in=b, conv_b=cb, rg_wa=wa, rg_ba=ba, rg_wx=wx, rg_bx=bx, rg_lambda=lam,
                    post_norm_w=post)

    w_rep = pack(pre_norm_w, b_in, conv_b, rg_wa, rg_ba, rg_wx, rg_bx, rg_lambda, post_norm_w)
    m_rep = pack(m_pre_norm_w, m_b_in, m_conv_b, m_rg_wa, m_rg_ba, m_rg_wx, m_rg_bx, m_rg_lambda, m_post_norm_w)
    v_rep = pack(v_pre_norm_w, v_b_in, v_conv_b, v_rg_wa, v_rg_ba, v_rg_wx, v_rg_bx, v_rg_lambda, v_post_norm_w)
    d_rep, nm_rep, nv_rep = _adamw("adamw_rep", w_rep, g_rep, m_rep, v_rep)
    grads, deltas, new_m, new_v = unpack(g_rep), unpack(d_rep), unpack(nm_rep), unpack(nv_rep)

    shard_cols = IN_TOTAL // N_CHIPS
    p_in = jnp.transpose(part["w_in"].reshape(D, N_CHIPS, shard_cols), (1, 0, 2)).reshape(N_DEV, D // 2, shard_cols)
    p_aro = jnp.concatenate([part[k].reshape(N_DEV, D // N_DEV, D) for k in ("w_a", "w_r", "w_o")], axis=1)
    s_in, s_aro = _chip_exchange([_pair_reduce("pair_w_in", p_in.astype(bf16)),
                                  _pair_reduce("pair_w_aro", p_aro.astype(bf16))])
    f_in, f_aro = _swap_halves([_sum_slots("sum_w_in", s_in), _sum_slots("sum_w_aro", s_aro)])
    g_w_in = f_in.reshape(D, shard_cols)
    rows = D // N_DEV
    g_aro = jnp.concatenate([f_aro[:, i * rows:(i + 1) * rows, :].reshape(2 * rows, D) for i in range(3)], axis=0)

    d_w_in, nm_w_in, nv_w_in = _adamw("adamw_w_in", w_in[0], g_w_in, m_w_in[0], v_w_in[0])
    stack = lambda a, b, c: jnp.concatenate([a[0], b[0], c[0]], axis=0)
    d_aro, nm_aro, nv_aro = _adamw("adamw_w_aro", stack(w_branch_a, w_branch_r, w_out), g_aro,
                                   stack(m_w_branch_a, m_w_branch_r, m_w_out),
                                   stack(v_w_branch_a, v_w_branch_r, v_w_out))
    d_cw, nm_cw, nv_cw = _adamw("adamw_conv_w", conv_w[0], g_conv_w, m_conv_w[0], v_conv_w[0])

    def sharded(t_in, t_aro, t_cw):
        r2 = 2 * rows
        return dict(w_in=t_in[None], conv_w=t_cw[None], w_branch_a=t_aro[None, :r2], w_branch_r=t_aro[None, r2:2 * r2],
                    w_out=t_aro[None, 2 * r2:])

    order = ["pre_norm_w", "w_in", "b_in", "conv_w", "conv_b", "rg_wa", "rg_ba", "rg_wx", "rg_bx", "rg_lambda",
             "w_branch_a", "w_branch_r", "w_out", "post_norm_w"]
    outs = [loss, grad_x]
    for rep, shd in ((grads, sharded(g_w_in, g_aro, g_conv_w)), (deltas, sharded(d_w_in, d_aro, d_cw)),
                     (new_m, sharded(nm_w_in, nm_aro, nm_cw)), (new_v, sharded(nv_w_in, nv_aro, nv_cw))):
        both = {**rep, **shd}
        outs.extend(both[k] for k in order)
    return tuple(outs)
```

```python
import jax
import jax.numpy as jnp
from jax import lax
from jax.experimental import pallas as pl
from jax.experimental.pallas import tpu as pltpu

f32 = jnp.float32
bf16 = jnp.bfloat16

D = 1024
HEADS = 16
HEAD_PAIRS = 8
LANES = 128
NORM_EPS = 1e-6
MASK_VALUE = -1e30
RG_C = 8.0
QK_SCALE = 0.125
TQ = 256
ATT_GROUP = 8
TL = 256
TM = 256
PREV_ROWS = 16
IN_USED = 8 * D + HEADS
IN_TOTAL = 9 * D + HEADS
N_CHIPS = 4
N_DEV = 8
ADAM_LR, ADAM_B1, ADAM_B2, ADAM_EPS, ADAM_WD, ADAM_STEP = 0.001, 0.9, 0.999, 1e-08, 0.01, 10
VMEM_LIMIT = 56 * 1024 * 1024
MESH = pl.DeviceIdType.MESH


def _dot(a, b):
    return jnp.dot(a, b, preferred_element_type=f32)


def _dot_nt(a, b):
    return lax.dot_general(a, b, (((1,), (1,)), ((), ())), preferred_element_type=f32)


def _dot_tn(a, b):
    return lax.dot_general(a, b, (((0,), (0,)), ((), ())), preferred_element_type=f32)


def _sig(x):
    return 0.5 * jnp.tanh(0.5 * x) + 0.5


def _softplus(x):
    return jnp.maximum(x, 0.0) + jnp.log(1.0 + jnp.exp(-jnp.abs(x)))


def _params(sem, vmem=None):
    return pltpu.CompilerParams(dimension_semantics=sem, vmem_limit_bytes=vmem)


def _tile(tm, width, cb=0):
    return pl.BlockSpec((tm, width), lambda i, cb=cb: (i, cb))


def _whole(shape):
    nd = len(shape)
    return pl.BlockSpec(shape, lambda *_: (0,) * nd)


def _prenorm(x, w_pre):
    t = x.shape[0]

    def body(x_ref, w_ref, h_ref):
        xv = x_ref[...]
        r = lax.rsqrt(jnp.mean(xv * xv, axis=-1, keepdims=True) + NORM_EPS)
        h_ref[...] = (xv * r * w_ref[...]).astype(bf16)

    return pl.pallas_call(
        body, name="prenorm", grid=(t // TM,),
        in_specs=[_tile(TM, D), _whole((1, D))], out_specs=_tile(TM, D),
        out_shape=jax.ShapeDtypeStruct((t, D), bf16),
        compiler_params=_params(("parallel",)),
    )(x, w_pre)


def _mm(name, ins, prologue, w, bias, out_dtype, tm, tn, keep_lhs=False):
    t = ins[0][0].shape[0]
    tm = min(tm, t)
    k, n = w.shape
    n_in = len(ins)
    assert not keep_lhs or tn == n

    def body(*refs):
        a = prologue(*[r[...] for r in refs[:n_in]])
        acc = _dot(a, refs[n_in][...])
        if bias is not None:
            acc = acc + refs[n_in + 1][...]
        if keep_lhs:
            refs[-1][...] = a
            refs[-2][...] = acc.astype(out_dtype)
        else:
            refs[-1][...] = acc.astype(out_dtype)

    in_specs = [pl.BlockSpec((tm, k), lambda i, j, cb=cb: (i, cb)) for _, cb in ins]
    in_specs.append(pl.BlockSpec((k, tn), lambda i, j: (0, j)))
    args = [a for a, _ in ins] + [w]
    if bias is not None:
        in_specs.append(pl.BlockSpec((1, tn), lambda i, j: (0, j)))
        args.append(bias)
    out_specs = pl.BlockSpec((tm, tn), lambda i, j: (i, j))
    out_shape = jax.ShapeDtypeStruct((t, n), out_dtype)
    if keep_lhs:
        out_specs = [out_specs, pl.BlockSpec((tm, k), lambda i, j: (i, 0))]
        out_shape = [out_shape, jax.ShapeDtypeStruct((t, k), bf16)]
    return pl.pallas_call(
        body, name=name, grid=(t // tm, n // tn), in_specs=in_specs, out_specs=out_specs, out_shape=out_shape,
        compiler_params=_params(("parallel", "parallel"), VMEM_LIMIT),
    )(*args)


def _forget_prep(f128, seq):
    t = f128.shape[0]
    nb = seq // LANES

    def body(f_ref, c_ref):
        r = lax.broadcasted_iota(jnp.int32, (LANES, LANES), 0)
        cidx = lax.broadcasted_iota(jnp.int32, (LANES, LANES), 1)
        tri = (r >= cidx).astype(f32)
        carry = jnp.zeros((1, LANES), f32)
        for blk in range(nb):
            fv = f_ref[pl.ds(blk * LANES, LANES), :]
            lf = -_softplus(-fv)
            c_ref[pl.ds(blk * LANES, LANES), :] = (
                jnp.dot(tri, lf, preferred_element_type=f32, precision=lax.Precision.HIGHEST) + carry)
            carry = carry + jnp.sum(lf, axis=0, keepdims=True)

    return pl.pallas_call(
        body, name="forget_prep", grid=(t // seq,),
        in_specs=[pl.BlockSpec((seq, LANES), lambda b: (b, 0))],
        out_specs=pl.BlockSpec((seq, LANES), lambda b: (b, 0)),
        out_shape=jax.ShapeDtypeStruct((t, LANES), f32),
        compiler_params=_params(("parallel",)),
    )(f128)


def _split3(cv):
    hi = cv.astype(bf16)
    r1 = cv - hi.astype(f32)
    mid = r1.astype(bf16)
    lo = (r1 - mid.astype(f32)).astype(bf16)
    return hi, mid, lo


def _attn_prep(qkv, c):
    t = qkv.shape[0]

    def body(q_ref, k_ref, c_ref, qa_ref, ka_ref):
        hp = pl.program_id(1)
        lane = lax.broadcasted_iota(jnp.int32, (1, LANES), 1)
        cv = c_ref[...]
        one = jnp.ones((), bf16)
        zero = jnp.zeros((), bf16)
        for hh in range(2):
            ch = jnp.sum(jnp.where(lane == 2 * hp + hh, cv, 0.0), axis=1, keepdims=True)
            hi, mid, lo = _split3(ch)
            q2, k2 = q_ref[...], k_ref[...]
            if hh == 1:
                q2, k2 = pltpu.roll(q2, 64, 1), pltpu.roll(k2, 64, 1)
            ones = jnp.where((lane >= 67) & (lane < 70), one, zero)
            qa = jnp.where(lane < 64, q2 * jnp.asarray(QK_SCALE, bf16),
                           jnp.where(lane == 64, hi, jnp.where(lane == 65, mid, jnp.where(lane == 66, lo, ones))))
            ones = jnp.where((lane >= 64) & (lane < 67), one, zero)
            ka = jnp.where(lane < 64, k2,
                           jnp.where(lane == 67, -hi, jnp.where(lane == 68, -mid, jnp.where(lane == 69, -lo, ones))))
            qa_ref[:, pl.ds(hh * LANES, LANES)] = qa
            ka_ref[:, pl.ds(hh * LANES, LANES)] = ka

    tm = min(512, t)
    spec = lambda off: pl.BlockSpec((tm, LANES), lambda i, hp: (i, off + hp))
    out = pl.BlockSpec((tm, 2 * LANES), lambda i, hp: (i, hp))
    return pl.pallas_call(
        body, name="attn_prep", grid=(t // tm, HEAD_PAIRS),
        in_specs=[spec(0), spec(HEAD_PAIRS), pl.BlockSpec((tm, LANES), lambda i, hp: (i, 0))],
        out_specs=[out, out],
        out_shape=[jax.ShapeDtypeStruct((t, 2 * D), bf16)] * 2,
        compiler_params=_params(("parallel", "parallel")),
    )(qkv, qkv, c)


def _attn_fwd(qa, ka, qkv, rest, seq):
    t = qkv.shape[0]
    nb, nq = t // seq, seq // TQ

    hg = ATT_GROUP
    ng = HEADS // hg

    def body(q_ref, k_ref, v_ref, ga_ref, o_ref, pa_ref, lse_ref, acc_scr):
        qi, gi = pl.program_id(1), pl.program_id(2)
        krow = lax.broadcasted_iota(jnp.int32, (TQ, TQ), 0)
        qcol = lax.broadcasted_iota(jnp.int32, (TQ, TQ), 1)
        acc_scr[...] = jnp.zeros_like(acc_scr)

        def kv_step(kt, carry, masked):
            ks = pl.multiple_of(kt * TQ, TQ)
            sts = [_dot_nt(k_ref[pl.ds(ks, TQ), pl.ds(g * LANES, LANES)], q_ref[:, pl.ds(g * LANES, LANES)])
                   for g in range(hg)]
            if masked:
                sts = [jnp.where(krow <= qcol, st, MASK_VALUE) for st in sts]
            m_new = [jnp.maximum(carry[g][0], jnp.max(sts[g], axis=0, keepdims=True)) for g in range(hg)]
            ps = [jnp.exp(sts[g] - m_new[g]) for g in range(hg)]
            alphas = [jnp.exp(carry[g][0] - m_new[g]) for g in range(hg)]
            phi = [ps[g].astype(bf16) for g in range(hg)]
            plo = [(ps[g] - phi[g].astype(f32)).astype(bf16) for g in range(hg)]
            vs = [v_ref[pl.ds(ks, TQ), pl.ds(j * LANES, LANES)] for j in range(hg // 2)]
            pvs = [_dot_tn(vs[g // 2], phi[g]) + _dot_tn(vs[g // 2], plo[g]) for g in range(hg)]
            olds = [acc_scr[g] for g in range(hg)]
            for g in range(hg):
                acc_scr[g] = alphas[g] * olds[g] + pvs[g]
            return tuple((m_new[g], alphas[g] * carry[g][1] + jnp.sum(ps[g], axis=0, keepdims=True))
                         for g in range(hg))

        init = tuple((jnp.full((1, TQ), MASK_VALUE, f32), jnp.zeros((1, TQ), f32)) for _ in range(hg))
        carry = lax.fori_loop(0, qi, lambda kt, cr: kv_step(kt, cr, False), init)
        stats = kv_step(qi, carry, True)
        drow = lax.broadcasted_iota(jnp.int32, (LANES, TQ), 0)
        for g in range(hg):
            m, l = stats[g]
            lse_ref[0, pl.ds(hg * gi + g, 1), :] = m + jnp.log(l)
        for j in range(hg // 2):
            o2 = jnp.where(drow < 64, acc_scr[2 * j] / stats[2 * j][1], acc_scr[2 * j + 1] / stats[2 * j + 1][1]).T
            o_ref[:, pl.ds(j * LANES, LANES)] = o2
            ga = ga_ref[:, pl.ds(j * LANES, LANES)].astype(f32)
            pa_ref[:, pl.ds(j * LANES, LANES)] = (o2 * (ga * _sig(ga))).astype(bf16)

    vw = hg * 64
    tile = pl.BlockSpec((TQ, vw), lambda b, qi, gi: (b * nq + qi, gi))
    return pl.pallas_call(
        body, name="attn_fwd", grid=(nb, nq, ng),
        in_specs=[pl.BlockSpec((TQ, hg * LANES), lambda b, qi, gi: (b * nq + qi, gi)),
                  pl.BlockSpec((seq, hg * LANES), lambda b, qi, gi: (b, gi)),
                  pl.BlockSpec((seq, vw), lambda b, qi, gi: (b, 2 * ng + gi)), tile],
        out_specs=[tile, tile, pl.BlockSpec((1, HEADS, TQ), lambda b, qi, gi: (b * nq + qi, 0, 0))],
        out_shape=[jax.ShapeDtypeStruct((t, D), f32), jax.ShapeDtypeStruct((t, D), bf16),
                   jax.ShapeDtypeStruct((t // TQ, HEADS, TQ), f32)],
        scratch_shapes=[pltpu.VMEM((hg, LANES, TQ), f32)],
        compiler_params=_params(("parallel", "parallel", "arbitrary"), VMEM_LIMIT),
    )(qa, ka, qkv, rest)


def _shifted_rows(x, top8, prev8, shift, row, row8):
    body = pltpu.roll(x, shift, 0)
    head = jnp.where(row8 < shift, pltpu.roll(prev8, shift, 0), pltpu.roll(top8, shift, 0))
    return body, head


def _rnn_gates(xc, wa_ref, wx_ref, ba_ref, bx_ref, lam_ref):
    xcb = xc.astype(bf16)
    r = _sig(_dot(xcb, wa_ref[...]) + ba_ref[...])
    i = _sig(_dot(xcb, wx_ref[...]) + bx_ref[...])
    sp = _softplus(-lam_ref[...])
    log_a = (-RG_C) * r * sp
    th = jnp.tanh(log_a)
    w1 = (-2.0) * th / (1.0 - th)
    sq = jnp.sqrt(jnp.maximum(w1, 0.0))
    return r, i, sp, log_a, w1, sq


def _conv_tile(x_ref, xprev_ref, has_prev, cw_ref, cb_ref, xc_ref):
    row = lax.broadcasted_iota(jnp.int32, (TL, D), 0)
    row8 = lax.broadcasted_iota(jnp.int32, (8, D), 0)
    x = x_ref[...].astype(f32)
    top8 = x[:8]
    prev8 = jnp.where(has_prev, xprev_ref[...].astype(f32)[PREV_ROWS - 8:], 0.0)
    xc = cb_ref[...] + cw_ref[pl.ds(3, 1), :] * x
    xc8 = cb_ref[...] + cw_ref[pl.ds(3, 1), :] * top8
    for sh in range(1, 4):
        w = cw_ref[pl.ds(3 - sh, 1), :]
        xs, xs8 = _shifted_rows(x, top8, prev8, sh, row, row8)
        xc = xc + w * xs
        xc8 = xc8 + w * xs8
    xc_ref[...] = xc
    xc_ref[pl.ds(0, 8), :] = xc8


def _rnn_fwd(rest, conv_w, conv_b, wa_d, wx_d, ba, bx, lam, seq):
    t = rest.shape[0]
    nb, nt = t // seq, seq // TL

    def body(x_ref, xprev_ref, gr_ref, cw_ref, cb_ref, wa_ref, wx_ref, ba_ref, bx_ref, lam_ref,
             xc_ref, a_ref, h_ref, pr_ref, u_scr, carry):
        tt = pl.program_id(1)
        _conv_tile(x_ref, xprev_ref, tt > 0, cw_ref, cb_ref, xc_ref)
        xc = xc_ref[...]
        r, i, sp, log_a, w1, sq = _rnn_gates(xc, wa_ref, wx_ref, ba_ref, bx_ref, lam_ref)
        a_ref[...] = jnp.exp(log_a)
        u_scr[...] = sq * (i * xc)

        @pl.when(tt == 0)
        def _():
            carry[...] = jnp.zeros_like(carry)

        def step(s, h):
            h = a_ref[pl.ds(s, 1), :] * h + u_scr[pl.ds(s, 1), :]
            h_ref[pl.ds(s, 1), :] = h
            return h

        carry[...] = lax.fori_loop(0, TL, step, carry[...], unroll=8)
        gr = gr_ref[...].astype(f32)
        pr_ref[...] = (h_ref[...] * (gr * _sig(gr))).astype(bf16)

    tile = lambda cb: pl.BlockSpec((TL, D), lambda b, tt, cb=cb: (b * nt + tt, cb))
    prev = lambda cb: pl.BlockSpec(
        (PREV_ROWS, D), lambda b, tt, cb=cb: (jnp.maximum((b * nt + tt) * (TL // PREV_ROWS) - 1, 0), cb))
    vec = _whole((1, D))
    return pl.pallas_call(
        body, name="rnn_fwd", grid=(nb, nt),
        in_specs=[tile(1), prev(1), tile(2), _whole((4, D)), vec, _whole((D, D)), _whole((D, D)), vec, vec, vec],
        out_specs=[tile(0)] * 4,
        out_shape=[jax.ShapeDtypeStruct((t, D), f32)] * 3 + [jax.ShapeDtypeStruct((t, D), bf16)],
        scratch_shapes=[pltpu.VMEM((TL, D), f32), pltpu.VMEM((1, D), f32)],
        compiler_params=_params(("parallel", "arbitrary"), VMEM_LIMIT),
    )(rest, rest, rest, conv_w, conv_b, wa_d, wx_d, ba, bx, lam)


def _merge(mga, mgr, ya, yr):
    return (_sig(mga.astype(f32)) * ya + _sig(mgr.astype(f32)) * yr).astype(bf16)


def _post_loss(o, x, tgt, w_post):
    t = o.shape[0]

    def body(o_ref, x_ref, t_ref, w_ref, do_ref, dy_ref, loss_ref, dwp_ref):
        @pl.when(pl.program_id(0) == 0)
        def _():
            loss_ref[...] = jnp.zeros_like(loss_ref)
            dwp_ref[...] = jnp.zeros_like(dwp_ref)

        ov = o_ref[...]
        w = w_ref[...]
        r2 = lax.rsqrt(jnp.mean(ov * ov, axis=-1, keepdims=True) + NORM_EPS)
        oh = ov * r2
        e = x_ref[...] + oh * w - t_ref[...]
        loss_ref[...] += 0.5 * jnp.sum(jnp.mean(e * e, axis=-1, keepdims=True))
        dy = e * (1.0 / D)
        dy_ref[...] = dy
        dwp_ref[...] += jnp.sum(dy * oh, axis=0, keepdims=True)
        doh = dy * w
        do_ref[...] = (r2 * (doh - oh * jnp.mean(doh * oh, axis=-1, keepdims=True))).astype(bf16)

    return pl.pallas_call(
        body, name="post_loss", grid=(t // TM,),
        in_specs=[_tile(TM, D)] * 3 + [_whole((1, D))],
        out_specs=[_tile(TM, D), _tile(TM, D), _whole((8, LANES)), _whole((1, D))],
        out_shape=[jax.ShapeDtypeStruct((t, D), bf16), jax.ShapeDtypeStruct((t, D), f32),
                   jax.ShapeDtypeStruct((8, LANES), f32), jax.ShapeDtypeStruct((1, D), f32)],
        compiler_params=_params(("arbitrary",)),
    )(o, x, tgt, w_post)


def _out_bwd(do, rest, ya, yr, w_out):
    t = do.shape[0]

    def body(do_ref, mga_ref, mgr_ref, ya_ref, yr_ref, w_ref, dya_ref, dyr_ref, dmga_ref, dmgr_ref):
        sa, sr = _sig(mga_ref[...].astype(f32)), _sig(mgr_ref[...].astype(f32))
        ya, yr = ya_ref[...], yr_ref[...]
        dm = _dot_nt(do_ref[...], w_ref[...])
        dya_ref[...] = (dm * sa).astype(bf16)
        dyr_ref[...] = (dm * sr).astype(bf16)
        dmga_ref[...] = (dm * ya * sa * (1.0 - sa)).astype(bf16)
        dmgr_ref[...] = (dm * yr * sr * (1.0 - sr)).astype(bf16)

    return pl.pallas_call(
        body, name="out_bwd", grid=(t // TM,),
        in_specs=[_tile(TM, D), _tile(TM, D, 3), _tile(TM, D, 4), _tile(TM, D), _tile(TM, D), _whole((D, D))],
        out_specs=[_tile(TM, D)] * 4,
        out_shape=[jax.ShapeDtypeStruct((t, D), bf16)] * 4,
        compiler_params=_params(("parallel",), VMEM_LIMIT),
    )(do, rest, rest, ya, yr, w_out)


def _branch_bwd(name, dyb, rest, gate_cb, act, w, act_grad_dtype):
    t = dyb.shape[0]

    def body(dy_ref, g_ref, act_ref, w_ref, dact_ref, dg_ref):
        dp = _dot_nt(dy_ref[...], w_ref[...])
        g = g_ref[...].astype(f32)
        sg = _sig(g)
        dact_ref[...] = (dp * (g * sg)).astype(act_grad_dtype)
        dg_ref[...] = (dp * act_ref[...] * (sg * (1.0 + g * (1.0 - sg)))).astype(bf16)

    return pl.pallas_call(
        body, name=name, grid=(t // TM,),
        in_specs=[_tile(TM, D), _tile(TM, D, gate_cb), _tile(TM, D), _whole((D, D))],
        out_specs=[_tile(TM, D), _tile(TM, D)],
        out_shape=[jax.ShapeDtypeStruct((t, D), act_grad_dtype), jax.ShapeDtypeStruct((t, D), bf16)],
        compiler_params=_params(("parallel",), VMEM_LIMIT),
    )(dyb, rest, act, w)


def _rnn_bwd(dh, a, h, xc, rest, conv_w, conv_b, wa_d, wx_d, ba, bx, lam, seq):
    t = dh.shape[0]
    nb, nt = t // seq, seq // TL
    diag = (D // LANES, LANES, LANES)

    def body(dh_ref, a_ref, h_ref, hprev_ref, xc_ref, x_ref, xprev_ref, cw_ref, cb_ref, wa_ref, wx_ref,
             ba_ref, bx_ref, lam_ref, dxr_ref, dwa_ref, dwx_ref, vec_ref, g_scr, dxc_scr, dxr_scr, qcarry, dxc_next):
        b, tt = pl.program_id(0), pl.program_id(1)
        rt = nt - 1 - tt

        @pl.when((b == 0) & (tt == 0))
        def _():
            dwa_ref[...] = jnp.zeros_like(dwa_ref)
            dwx_ref[...] = jnp.zeros_like(dwx_ref)
            vec_ref[...] = jnp.zeros_like(vec_ref)

        @pl.when(tt == 0)
        def _():
            qcarry[...] = jnp.zeros_like(qcarry)
            dxc_next[...] = jnp.zeros_like(dxc_next)

        def step(k, q):
            s = TL - 1 - k
            g = dh_ref[pl.ds(s, 1), :] + q
            g_scr[pl.ds(s, 1), :] = g
            return a_ref[pl.ds(s, 1), :] * g

        qcarry[...] = lax.fori_loop(0, TL, step, qcarry[...], unroll=8)

        row = lax.broadcasted_iota(jnp.int32, (TL, D), 0)
        row8 = lax.broadcasted_iota(jnp.int32, (8, D), 0)
        g = g_scr[...]
        av = a_ref[...]
        xc = xc_ref[...]
        hlast = jnp.where(rt > 0, hprev_ref[pl.ds(PREV_ROWS - 1, 1), :], 0.0)
        hp = jnp.where(row == 0, hlast, pltpu.roll(h_ref[...], 1, 0))
        r, i, sp, log_a, w1, sq = _rnn_gates(xc, wa_ref, wx_ref, ba_ref, bx_ref, lam_ref)
        dix = g * sq
        di = dix * xc
        dxc = dix * i
        dsq = g * (i * xc)
        dlog_a = g * hp * av - dsq * jnp.where(sq > 0.0, (1.0 - w1) / sq, 0.0)
        dpr = (dlog_a * ((-RG_C) * sp)) * r * (1.0 - r)
        dpi = di * i * (1.0 - i)
        dprb, dpib, xcb = dpr.astype(bf16), dpi.astype(bf16), xc.astype(bf16)
        dxc = dxc + _dot_nt(dprb, wa_ref[...]) + _dot_nt(dpib, wx_ref[...])
        for j in range(D // LANES):
            cols = slice(j * LANES, (j + 1) * LANES)
            dwa_ref[j] += _dot_tn(xcb[:, cols], dprb[:, cols])
            dwx_ref[j] += _dot_tn(xcb[:, cols], dpib[:, cols])
        vec_ref[pl.ds(0, 1), :] += jnp.sum(dpr, axis=0, keepdims=True)
        vec_ref[pl.ds(1, 1), :] += jnp.sum(dpi, axis=0, keepdims=True)
        dsp = jnp.sum(dlog_a * ((-RG_C) * r), axis=0, keepdims=True)
        vec_ref[pl.ds(2, 1), :] += dsp * (-_sig(-lam_ref[...]))
        vec_ref[pl.ds(3, 1), :] += jnp.sum(dxc, axis=0, keepdims=True)

        dxc_scr[...] = dxc
        bot8 = dxc_scr[pl.ds(TL - 8, 8), :]
        nxt8 = dxc_next[...]
        dxr = cw_ref[pl.ds(3, 1), :] * dxc
        dxr8 = cw_ref[pl.ds(3, 1), :] * bot8
        for sh in range(1, 4):
            w = cw_ref[pl.ds(3 - sh, 1), :]
            dxr = dxr + w * pltpu.roll(dxc, TL - sh, 0)
            dxr8 = dxr8 + w * jnp.where(row8 < 8 - sh, pltpu.roll(bot8, 8 - sh, 0), pltpu.roll(nxt8, 8 - sh, 0))
        dxr_scr[...] = dxr
        dxr_scr[pl.ds(TL - 8, 8), :] = dxr8
        dxr_ref[...] = dxr_scr[...].astype(bf16)
        dxc_next[...] = dxc_scr[pl.ds(0, 8), :]

        x = x_ref[...].astype(f32)
        prev8 = jnp.where(rt > 0, xprev_ref[...].astype(f32)[PREV_ROWS - 8:], 0.0)
        dxc_top8 = dxc_scr[pl.ds(0, 8), :]
        vec_ref[pl.ds(7, 1), :] += jnp.sum(dxc * x, axis=0, keepdims=True)
        for sh in range(1, 4):
            inside = jnp.sum(dxc * jnp.where(row >= sh, pltpu.roll(x, sh, 0), 0.0), axis=0, keepdims=True)
            above = jnp.sum(dxc_top8 * jnp.where(row8 < sh, pltpu.roll(prev8, sh, 0), 0.0), axis=0, keepdims=True)
            vec_ref[pl.ds(7 - sh, 1), :] += inside + above

    tile = lambda cb: pl.BlockSpec((TL, D), lambda b, tt, cb=cb: (b * nt + nt - 1 - tt, cb))
    prev = lambda cb: pl.BlockSpec(
        (PREV_ROWS, D), lambda b, tt, cb=cb: (jnp.maximum((b * nt + nt - 1 - tt) * (TL // PREV_ROWS) - 1, 0), cb))
    vec = _whole((1, D))
    return pl.pallas_call(
        body, name="rnn_bwd", grid=(nb, nt),
        in_specs=[tile(0), tile(0), tile(0), prev(0), tile(0), tile(1), prev(1),
                  _whole((4, D)), vec, _whole((D, D)), _whole((D, D)), vec, vec, vec],
        out_specs=[tile(0), _whole(diag), _whole(diag), _whole((8, D))],
        out_shape=[jax.ShapeDtypeStruct((t, D), bf16), jax.ShapeDtypeStruct(diag, f32),
                   jax.ShapeDtypeStruct(diag, f32), jax.ShapeDtypeStruct((8, D), f32)],
        scratch_shapes=[pltpu.VMEM((TL, D), f32), pltpu.VMEM((TL, D), f32), pltpu.VMEM((TL, D), f32),
                        pltpu.VMEM((1, D), f32), pltpu.VMEM((8, D), f32)],
        compiler_params=_params(("arbitrary", "arbitrary"), VMEM_LIMIT),
    )(dh, a, h, h, xc, rest, rest, conv_w, conv_b, wa_d, wx_d, ba, bx, lam)


def _attn_delta(doa, o):
    t = doa.shape[0]

    def body(do_ref, o_ref, d_ref):
        prod = do_ref[...].astype(f32) * o_ref[...]
        ch = lax.broadcasted_iota(jnp.int32, (D, LANES), 0)
        hd = lax.broadcasted_iota(jnp.int32, (D, LANES), 1)
        pick = (ch // 64 == hd).astype(f32)
        per_head = jnp.dot(prod, pick, preferred_element_type=f32, precision=lax.Precision.HIGHEST)
        d_ref[0] = per_head.T[:HEADS, :]

    return pl.pallas_call(
        body, name="attn_delta", grid=(t // TQ,),
        in_specs=[_tile(TQ, D), _tile(TQ, D)],
        out_specs=pl.BlockSpec((1, HEADS, TQ), lambda i: (i, 0, 0)),
        out_shape=jax.ShapeDtypeStruct((t // TQ, HEADS, TQ), f32),
        compiler_params=_params(("parallel",)),
    )(doa, o)


def _attn_bwd(qa, ka, qkv, doa, lse, delta, seq):
    t = qkv.shape[0]
    nb, nq = t // seq, seq // TQ
    hg = ATT_GROUP
    ng, npair = HEADS // hg, hg // 2

    def body(qa_ref, ka_ref, q_ref, k_ref, v_ref, do_ref, lse_ref, dl_ref, dq_ref, dk_ref, dv_ref, dc_ref,
             dqt_scr, dk_scr, dv_scr, ds_scr, kht_scr):
        gi, kt = pl.program_id(1), pl.program_id(2)
        lane = lax.broadcasted_iota(jnp.int32, (1, LANES), 1)
        krow = lax.broadcasted_iota(jnp.int32, (TQ, TQ), 0)
        qcol = lax.broadcasted_iota(jnp.int32, (TQ, TQ), 1)
        lmask = [(lane // 64) == hh for hh in range(2)]
        scale = jnp.asarray(QK_SCALE, bf16)

        @pl.when(kt == 0)
        def _():
            dqt_scr[...] = jnp.zeros_like(dqt_scr)

        dk_scr[...] = jnp.zeros_like(dk_scr)
        dv_scr[...] = jnp.zeros_like(dv_scr)
        ds_scr[...] = jnp.zeros_like(ds_scr)
        for g in range(hg):
            k2 = k_ref[:, pl.ds((g // 2) * LANES, LANES)]
            kht_scr[g] = jnp.where(lmask[g % 2], k2, jnp.zeros_like(k2)).T

        def q_step(qt, masked):
            qs = pl.multiple_of(qt * TQ, TQ)
            heads = range(hg)
            do2 = [do_ref[pl.ds(qs, TQ), pl.ds(j * LANES, LANES)] for j in range(npair)]
            q2 = [q_ref[pl.ds(qs, TQ), pl.ds(j * LANES, LANES)] for j in range(npair)]
            doh = [jnp.where(lmask[g % 2], do2[g // 2], jnp.zeros_like(do2[0])) for g in heads]
            qh = [jnp.where(lmask[g % 2], q2[g // 2], jnp.zeros_like(q2[0])) * scale for g in heads]
            st = [_dot_nt(ka_ref[:, pl.ds(g * LANES, LANES)], qa_ref[pl.ds(qs, TQ), pl.ds(g * LANES, LANES)])
                  for g in heads]
            if masked:
                st = [jnp.where(krow <= qcol, s, MASK_VALUE) for s in st]
            dp = [_dot_nt(v_ref[:, pl.ds((g // 2) * LANES, LANES)], doh[g]) for g in heads]
            p = [jnp.exp(st[g] - lse_ref[qt, pl.ds(hg * gi + g, 1), :]) for g in heads]
            ds = [p[g] * (dp[g] - dl_ref[qt, pl.ds(hg * gi + g, 1), :]) for g in heads]
            pb = [x.astype(bf16) for x in p]
            dsb = [x.astype(bf16) for x in ds]
            for j in range(npair):
                a, b = 2 * j, 2 * j + 1
                dv_scr[j] += _dot(pb[a], doh[a]) + _dot(pb[b], doh[b])
                dk_scr[j] += _dot(dsb[a], qh[a]) + _dot(dsb[b], qh[b])
                dqt_scr[qt, j] += (_dot(kht_scr[a], dsb[a]) + _dot(kht_scr[b], dsb[b])) * QK_SCALE
            for g in heads:
                ds_scr[g] += ds[g][:, :LANES] + ds[g][:, LANES:]

        q_step(kt, True)

        def loop_body(qt, carry):
            q_step(qt, False)
            return carry

        lax.fori_loop(kt + 1, nq, loop_body, 0)

        dc = jnp.zeros((TQ, LANES), f32)
        for g in range(hg):
            dc = jnp.where(lane == g, -jnp.sum(ds_scr[g], axis=1, keepdims=True), dc)
        dc_ref[...] = dc
        for j in range(npair):
            dk_ref[:, pl.ds(j * LANES, LANES)] = dk_scr[j].astype(bf16)
            dv_ref[:, pl.ds(j * LANES, LANES)] = dv_scr[j].astype(bf16)

        @pl.when(kt == nq - 1)
        def _():
            for qt in range(nq):
                for j in range(npair):
                    dq_ref[pl.ds(qt * TQ, TQ), pl.ds(j * LANES, LANES)] = dqt_scr[qt, j].T.astype(bf16)

    vw = hg * 64
    seqspec = pl.BlockSpec((seq, vw), lambda b, gi, kt: (b, gi))
    kspec = lambda off: pl.BlockSpec((TQ, vw), lambda b, gi, kt: (b * nq + kt, off + gi))
    rowspec = pl.BlockSpec((nq, HEADS, TQ), lambda b, gi, kt: (b, 0, 0))
    return pl.pallas_call(
        body, name="attn_bwd", grid=(nb, ng, nq),
        in_specs=[pl.BlockSpec((seq, hg * LANES), lambda b, gi, kt: (b, gi)),
                  pl.BlockSpec((TQ, hg * LANES), lambda b, gi, kt: (b * nq + kt, gi)),
                  seqspec, kspec(ng), kspec(2 * ng), seqspec, rowspec, rowspec],
        out_specs=[seqspec, kspec(0), kspec(0), pl.BlockSpec((TQ, LANES), lambda b, gi, kt: (b * nq + kt, gi))],
        out_shape=[jax.ShapeDtypeStruct((t, D), bf16)] * 3 + [jax.ShapeDtypeStruct((t, ng * LANES), f32)],
        scratch_shapes=[pltpu.VMEM((nq, npair, LANES, TQ), f32), pltpu.VMEM((npair, TQ, LANES), f32),
                        pltpu.VMEM((npair, TQ, LANES), f32), pltpu.VMEM((hg, TQ, LANES), f32),
                        pltpu.VMEM((hg, LANES, TQ), bf16)],
        compiler_params=_params(("parallel", "parallel", "arbitrary"), VMEM_LIMIT),
    )(qa, ka, qkv, qkv, qkv, doa, lse, delta)


def _forget_bwd(dc, f128, seq):
    t = f128.shape[0]
    nb = seq // LANES

    def body(dc_ref, f_ref, df_ref, dbf_ref):
        @pl.when(pl.program_id(0) == 0)
        def _():
            dbf_ref[...] = jnp.zeros_like(dbf_ref)

        r = lax.broadcasted_iota(jnp.int32, (LANES, LANES), 0)
        cidx = lax.broadcasted_iota(jnp.int32, (LANES, LANES), 1)
        tri = (r <= cidx).astype(f32)
        carry = jnp.zeros((1, LANES), f32)
        total = jnp.zeros((1, LANES), f32)
        for blk in reversed(range(nb)):
            dcb = dc_ref[pl.ds(blk * LANES, LANES), :]
            dlf = jnp.dot(tri, dcb, preferred_element_type=f32, precision=lax.Precision.HIGHEST) + carry
            df = dlf * _sig(-f_ref[pl.ds(blk * LANES, LANES), :])
            df_ref[pl.ds(blk * LANES, LANES), :] = df.astype(bf16)
            total = total + jnp.sum(df, axis=0, keepdims=True)
            carry = carry + jnp.sum(dcb, axis=0, keepdims=True)
        dbf_ref[...] += total

    return pl.pallas_call(
        body, name="forget_bwd", grid=(t // seq,),
        in_specs=[pl.BlockSpec((seq, LANES), lambda b: (b, 0)), pl.BlockSpec((seq, LANES), lambda b: (b, 0))],
        out_specs=[pl.BlockSpec((seq, LANES), lambda b: (b, 0)), _whole((1, LANES))],
        out_shape=[jax.ShapeDtypeStruct((t, LANES), bf16), jax.ShapeDtypeStruct((1, LANES), f32)],
        compiler_params=_params(("arbitrary",)),
    )(dc, f128)


def _in_bwd(dz, df, x, dy, w_qkv, w_rest, w_f, w_pre):
    t = x.shape[0]
    n_qkv = w_qkv.shape[1] // D
    n_rest = w_rest.shape[1] // D

    def body(*refs):
        dz_refs = refs[:n_qkv + n_rest]
        df_ref, x_ref, dy_ref, wq_ref, wr_ref, wf_ref, wp_ref, gx_ref, dwp_ref = refs[n_qkv + n_rest:]

        @pl.when(pl.program_id(0) == 0)
        def _():
            dwp_ref[...] = jnp.zeros_like(dwp_ref)

        dh = _dot_nt(df_ref[...], wf_ref[...])
        for p in range(n_qkv):
            dh = dh + _dot_nt(dz_refs[p][...], wq_ref[:, pl.ds(p * D, D)])
        for p in range(n_rest):
            dh = dh + _dot_nt(dz_refs[n_qkv + p][...], wr_ref[:, pl.ds(p * D, D)])
        xv = x_ref[...]
        r1 = lax.rsqrt(jnp.mean(xv * xv, axis=-1, keepdims=True) + NORM_EPS)
        xh = xv * r1
        dwp_ref[...] += jnp.sum(dh * xh, axis=0, keepdims=True)
        dxh = dh * wp_ref[...]
        gx_ref[...] = dy_ref[...] + r1 * (dxh - xh * jnp.mean(dxh * xh, axis=-1, keepdims=True))

    once = lambda shape: pl.BlockSpec(shape, lambda i: (0, 0), pipeline_mode=pl.Buffered(1))
    return pl.pallas_call(
        body, name="in_bwd", grid=(t // TM,),
        in_specs=[_tile(TM, D)] * (n_qkv + n_rest) + [_tile(TM, LANES), _tile(TM, D), _tile(TM, D),
                  once(w_qkv.shape), once(w_rest.shape), once(w_f.shape), _whole((1, D))],
        out_specs=[_tile(TM, D), _whole((1, D))],
        out_shape=[jax.ShapeDtypeStruct((t, D), f32), jax.ShapeDtypeStruct((1, D), f32)],
        compiler_params=_params(("arbitrary",), VMEM_LIMIT),
    )(*dz, df, x, dy, w_qkv, w_rest, w_f, w_pre)


def _tn_mm(name, a, b, tn, tk=2048):
    t, k = a.shape
    tk = min(tk, t)
    n = b.shape[1]

    def body(a_ref, b_ref, o_ref, s_ref):
        @pl.when(pl.program_id(1) == 0)
        def _():
            o_ref[...] = jnp.zeros_like(o_ref)
            s_ref[...] = jnp.zeros_like(s_ref)

        bv = b_ref[...]
        o_ref[...] += _dot_tn(a_ref[...], bv)
        s_ref[...] += jnp.sum(bv.astype(f32), axis=0, keepdims=True)

    return pl.pallas_call(
        body, name=name, grid=(n // tn, t // tk),
        in_specs=[pl.BlockSpec((tk, k), lambda j, kk: (kk, 0)), pl.BlockSpec((tk, tn), lambda j, kk: (kk, j))],
        out_specs=[pl.BlockSpec((k, tn), lambda j, kk: (0, j)), pl.BlockSpec((1, tn), lambda j, kk: (0, j))],
        out_shape=[jax.ShapeDtypeStruct((k, n), f32), jax.ShapeDtypeStruct((1, n), f32)],
        compiler_params=_params(("parallel", "arbitrary"), VMEM_LIMIT),
    )(a, b)


def _position():
    return lax.axis_index("x"), lax.axis_index("y"), lax.axis_index("c")


def _gather_shards(parts, small):
    n = len(parts)
    halves = [p.shape[0] // 2 for p in parts]

    def body(*refs):
        srcs, small_src = refs[:n], refs[n]
        dsts, small_dst = refs[n + 1:2 * n + 1], refs[2 * n + 1]
        send, recv, local = refs[2 * n + 2:]
        x, y, c = _position()
        me = 2 * x + y
        chips = [(1 - x, y), (x, 1 - y), (1 - x, 1 - y)]
        ids = [2 * px + py for px, py in chips]

        def half(a, shard, which):
            return dsts[a].at[shard, pl.ds(which * halves[a], halves[a]), :]

        def over_ici(a, j, shard):
            px, py = chips[j]
            return pltpu.make_async_remote_copy(
                src_ref=srcs[a].at[pl.ds(c * halves[a], halves[a]), :], dst_ref=half(a, shard, c),
                send_sem=send.at[a * 3 + j], recv_sem=recv.at[a * 3 + j], device_id=(px, py, c), device_id_type=MESH)

        def to_sibling(a, j, which):
            k = 3 * n + a * 3 + j
            return pltpu.make_async_remote_copy(
                src_ref=half(a, ids[j], which), dst_ref=half(a, ids[j], which), send_sem=send.at[k],
                recv_sem=recv.at[k], device_id=(x, y, 1 - c), device_id_type=MESH)

        def small_copy(j, shard):
            px, py = chips[j]
            return pltpu.make_async_remote_copy(
                src_ref=small_src, dst_ref=small_dst.at[shard], send_sem=send.at[6 * n + j], recv_sem=recv.at[6 * n + j],
                device_id=(px, py, c), device_id_type=MESH)

        own = [pltpu.make_async_copy(srcs[a], dsts[a].at[me], local.at[a]) for a in range(n)]
        own.append(pltpu.make_async_copy(small_src, small_dst.at[me], local.at[n]))
        for cp in own:
            cp.start()
        first = [over_ici(a, j, me) for j in range(3) for a in range(n)] + [small_copy(j, me) for j in range(3)]
        for cp in first:
            cp.start()
        passed = []
        for j in range(3):
            for a in range(n):
                over_ici(a, j, ids[j]).wait_recv()
                passed.append(to_sibling(a, j, c))
                passed[-1].start()
        for j in range(3):
            small_copy(j, ids[j]).wait_recv()
            for a in range(n):
                to_sibling(a, j, 1 - c).wait_recv()
        for cp in first + passed:
            cp.wait_send()
        for cp in own:
            cp.wait()

    vm = pl.BlockSpec(memory_space=pltpu.VMEM)
    return pl.pallas_call(
        body, name="gather_shards",
        in_specs=[vm] * (n + 1), out_specs=[vm] * (n + 1),
        out_shape=[jax.ShapeDtypeStruct((N_CHIPS,) + p.shape, p.dtype) for p in parts + [small]],
        scratch_shapes=[pltpu.SemaphoreType.DMA((6 * n + 3,)), pltpu.SemaphoreType.DMA((6 * n + 3,)),
                        pltpu.SemaphoreType.DMA((n + 1,))],
        compiler_params=pltpu.CompilerParams(vmem_limit_bytes=VMEM_LIMIT),
    )(*parts, small)


def _allsum_rows(part):
    rows_n = part.shape[0]

    def body(x_ref, gath_ref, sum_ref, send_sems, recv_sems, local_sem):
        x, y, c = _position()
        me, sibling = (x, y, c), (x, y, 1 - c)
        chips = [(1 - x, y), (x, 1 - y), (1 - x, 1 - y)]

        def rows(px, py, pc):
            return gath_ref.at[pl.ds((4 * px + 2 * py + pc) * rows_n, rows_n), :]

        def copy(k, block, to, src=None):
            return pltpu.make_async_remote_copy(
                src_ref=rows(*block) if src is None else src, dst_ref=rows(*block),
                send_sem=send_sems.at[k], recv_sem=recv_sems.at[k], device_id=to, device_id_type=MESH)

        mine = pltpu.make_async_copy(x_ref, rows(*me), local_sem)
        mine.start()
        first = [copy(0, me, sibling, src=x_ref)]
        first += [copy(1 + j, me, (*chip, c), src=x_ref) for j, chip in enumerate(chips)]
        for cp in first:
            cp.start()
        passed = [copy(4 + j, (*chip, c), sibling) for j, chip in enumerate(chips)]
        for j, chip in enumerate(chips):
            copy(1 + j, (*chip, c), me).wait_recv()
            passed[j].start()
        copy(0, sibling, me).wait_recv()
        for j, chip in enumerate(chips):
            copy(4 + j, (*chip, 1 - c), me).wait_recv()
        for cp in first + passed:
            cp.wait_send()
        mine.wait()
        total = gath_ref[pl.ds(0, rows_n), :]
        for d in range(1, N_DEV):
            total = total + gath_ref[pl.ds(d * rows_n, rows_n), :]
        sum_ref[...] = total

    vm = pl.BlockSpec(memory_space=pltpu.VMEM)
    return pl.pallas_call(
        body, name="allsum_rows", in_specs=[vm], out_specs=[vm, vm],
        out_shape=[jax.ShapeDtypeStruct((N_DEV * rows_n, D), f32), jax.ShapeDtypeStruct((rows_n, D), f32)],
        scratch_shapes=[pltpu.SemaphoreType.DMA((7,)), pltpu.SemaphoreType.DMA((7,)), pltpu.SemaphoreType.DMA],
    )(part)[1]


PAIR_ROWS = 64


def _pair_reduce(name, pieces):
    _, r, n = pieces.shape

    def body(p_ref, o_ref, land, send, recv):
        x, y, c = _position()

        def remote(j, half):
            return pltpu.make_async_remote_copy(
                src_ref=p_ref.at[2 * j + half], dst_ref=land.at[j], send_sem=send.at[j], recv_sem=recv.at[j],
                device_id=(x, y, 1 - c), device_id_type=MESH)

        sends = [remote(j, 1 - c) for j in range(N_CHIPS)]
        for cp in sends:
            cp.start()
        for j in range(N_CHIPS):
            remote(j, c).wait_recv()

            def add_rows(i, carry, j=j):
                rows = pl.ds(pl.multiple_of(i * PAIR_ROWS, PAIR_ROWS), PAIR_ROWS)
                o_ref[j, rows, :] = (p_ref[2 * j + c, rows, :].astype(f32) + land[j, rows, :].astype(f32)).astype(bf16)
                return carry

            lax.fori_loop(0, r // PAIR_ROWS, add_rows, 0)
        for cp in sends:
            cp.wait_send()

    vm = pl.BlockSpec(memory_space=pltpu.VMEM)
    return pl.pallas_call(
        body, name=name, in_specs=[vm], out_specs=vm,
        out_shape=jax.ShapeDtypeStruct((N_CHIPS, r, n), bf16),
        scratch_shapes=[pltpu.VMEM((N_CHIPS, r, n), bf16), pltpu.SemaphoreType.DMA((N_CHIPS,)),
                        pltpu.SemaphoreType.DMA((N_CHIPS,))],
        compiler_params=pltpu.CompilerParams(vmem_limit_bytes=VMEM_LIMIT),
    )(pieces)


def _chip_exchange(arrs):
    n = len(arrs)

    def body(*refs):
        srcs, dsts = refs[:n], refs[n:2 * n]
        send, recv, local = refs[2 * n:]
        x, y, c = _position()
        me = 2 * x + y
        chips = [(1 - x, y), (x, 1 - y), (1 - x, 1 - y)]

        def remote(a, j, piece, landing):
            px, py = chips[j]
            return pltpu.make_async_remote_copy(
                src_ref=srcs[a].at[piece], dst_ref=dsts[a].at[landing], send_sem=send.at[a * 3 + j],
                recv_sem=recv.at[a * 3 + j], device_id=(px, py, c), device_id_type=MESH)

        own = [pltpu.make_async_copy(srcs[a].at[me], dsts[a].at[me], local.at[a]) for a in range(n)]
        sends = [remote(a, j, 2 * px + py, me) for j, (px, py) in enumerate(chips) for a in range(n)]
        for cp in sends + own:
            cp.start()
        for j, (px, py) in enumerate(chips):
            for a in range(n):
                remote(a, j, me, 2 * px + py).wait_recv()
        for cp in sends:
            cp.wait_send()
        for cp in own:
            cp.wait()

    anyspec = pl.BlockSpec(memory_space=pl.ANY)
    return pl.pallas_call(
        body, name="chip_exchange", in_specs=[anyspec] * n, out_specs=[anyspec] * n,
        out_shape=[jax.ShapeDtypeStruct(a.shape, a.dtype) for a in arrs],
        scratch_shapes=[pltpu.SemaphoreType.DMA((3 * n,)), pltpu.SemaphoreType.DMA((3 * n,)),
                        pltpu.SemaphoreType.DMA((n,))],
    )(*arrs)


def _swap_halves(arrs):
    n = len(arrs)

    def body(*refs):
        srcs, dsts = refs[:n], refs[n:2 * n]
        send, recv, local = refs[2 * n:]
        x, y, c = _position()

        def remote(a, landing):
            return pltpu.make_async_remote_copy(
                src_ref=srcs[a], dst_ref=dsts[a].at[landing], send_sem=send.at[a], recv_sem=recv.at[a],
                device_id=(x, y, 1 - c), device_id_type=MESH)

        own = [pltpu.make_async_copy(srcs[a], dsts[a].at[c], local.at[a]) for a in range(n)]
        sends = [remote(a, c) for a in range(n)]
        for cp in sends + own:
            cp.start()
        for a in range(n):
            remote(a, 1 - c).wait_recv()
        for cp in sends:
            cp.wait_send()
        for cp in own:
            cp.wait()

    vm = pl.BlockSpec(memory_space=pltpu.VMEM)
    return pl.pallas_call(
        body, name="swap_halves", in_specs=[vm] * n, out_specs=[vm] * n,
        out_shape=[jax.ShapeDtypeStruct((2,) + a.shape, a.dtype) for a in arrs],
        scratch_shapes=[pltpu.SemaphoreType.DMA((n,)), pltpu.SemaphoreType.DMA((n,)), pltpu.SemaphoreType.DMA((n,))],
        compiler_params=pltpu.CompilerParams(vmem_limit_bytes=VMEM_LIMIT),
    )(*arrs)


def _row_block(r):
    return 128 if r % 128 == 0 else r


def _sum_slots(name, slots):
    s, r, n = slots.shape
    rb = _row_block(r)

    def body(s_ref, o_ref):
        total = s_ref[0].astype(f32)
        for d in range(1, s):
            total = total + s_ref[d].astype(f32)
        o_ref[...] = total

    return pl.pallas_call(
        body, name=name, grid=(r // rb,),
        in_specs=[pl.BlockSpec((s, rb, n), lambda i: (0, i, 0))],
        out_specs=pl.BlockSpec((rb, n), lambda i: (i, 0)),
        out_shape=jax.ShapeDtypeStruct((r, n), f32),
        compiler_params=_params(("parallel",), VMEM_LIMIT),
    )(slots)


def _adamw(name, w, g, m, v):
    r, n = w.shape
    rb = _row_block(r)

    def body(w_ref, g_ref, m_ref, v_ref, d_ref, nm_ref, nv_ref):
        gv = g_ref[...]
        m2 = ADAM_B1 * m_ref[...] + (1.0 - ADAM_B1) * gv
        v2 = ADAM_B2 * v_ref[...] + (1.0 - ADAM_B2) * (gv * gv)
        m_hat = m2 / (1.0 - ADAM_B1 ** ADAM_STEP)
        v_hat = v2 / (1.0 - ADAM_B2 ** ADAM_STEP)
        d_ref[...] = (-ADAM_LR) * (m_hat / (jnp.sqrt(v_hat) + ADAM_EPS) + ADAM_WD * w_ref[...])
        nm_ref[...] = m2
        nv_ref[...] = v2

    spec = pl.BlockSpec((rb, n), lambda i: (i, 0))
    return pl.pallas_call(
        body, name=name, grid=(r // rb,), in_specs=[spec] * 4, out_specs=[spec] * 3,
        out_shape=[jax.ShapeDtypeStruct((r, n), f32)] * 3,
        compiler_params=_params(("parallel",), VMEM_LIMIT),
    )(w, g, m, v)


def _identity(a):
    return a


def _local_step(x2, tgt2, seq, wt):
    nb = x2.shape[0] // seq
    h = _prenorm(x2, wt["pre_w"])
    qkv = _mm("in_qkv", [(h, 0)], _identity, wt["w_qkv"], wt["b_qkv"], bf16, 512, 1024)
    rest = _mm("in_rest", [(h, 0)], _identity, wt["w_rest"], wt["b_rest"], bf16, 512, 1024)
    f128 = _mm("in_f", [(h, 0)], _identity, wt["w_f"], wt["b_f"], f32, 512, LANES)
    c = _forget_prep(f128, seq)
    qa, ka = _attn_prep(qkv, c)
    o_att, pa, lse = _attn_fwd(qa, ka, qkv, rest, seq)
    ya = _mm("proj_a", [(pa, 0)], _identity, wt["w_a"], None, f32, 512, D)
    rnn_w = (wt["conv_w"], wt["conv_b"], wt["wa_d"], wt["wx_d"], wt["ba"], wt["bx"], wt["lam"])
    xc, a, hrec, pr = _rnn_fwd(rest, *rnn_w, seq)
    yr = _mm("proj_r", [(pr, 0)], _identity, wt["w_r"], None, f32, 512, D)
    o, mrg = _mm("proj_out", [(rest, 3), (rest, 4), (ya, 0), (yr, 0)], _merge, wt["w_o"], None, f32, TM, D,
                 keep_lhs=True)

    do, dy, loss8, d_post = _post_loss(o, x2, tgt2, wt["post_w"])
    dya, dyr, dmga, dmgr = _out_bwd(do, rest, ya, yr, wt["w_o"])
    doa, dga = _branch_bwd("branch_a_bwd", dya, rest, 0, o_att, wt["w_a"], bf16)
    dhrec, dgr = _branch_bwd("branch_r_bwd", dyr, rest, 2, hrec, wt["w_r"], f32)
    d_wo, _ = _tn_mm("dw_out", mrg, do, 512)
    d_wa, _ = _tn_mm("dw_branch_a", pa, dya, 512)
    d_wr, _ = _tn_mm("dw_branch_r", pr, dyr, 512)
    dxr, d_wad, d_wxd, vec = _rnn_bwd(dhrec, a, hrec, xc, rest, *rnn_w, seq)
    dq, dk, dv, dc_pairs = _attn_bwd(qa, ka, qkv, doa, lse, _attn_delta(doa, o_att), seq)
    dc = dc_pairs.reshape(-1, HEADS // ATT_GROUP, LANES)[:, :, :ATT_GROUP].reshape(-1, HEADS)
    df, db_f = _forget_bwd(_pad_cols(dc, LANES), f128, seq)
    pieces = [dq, dk, dv, dga, dxr, dgr, dmga, dmgr]
    gx, d_pre = _in_bwd(pieces, df, x2, dy, wt["w_qkv"], wt["w_rest"], wt["w_f"], wt["pre_w"])
    names = ["q", "k", "v", "ga", "xr", "gr", "mga", "mgr"]
    dws, dbs = [], []
    for nm, piece in zip(names, pieces):
        dw_p, db_p = _tn_mm("dw_in_" + nm, h, piece, 512)
        dws.append(dw_p)
        dbs.append(db_p)
    dw_f, _ = _tn_mm("dw_in_f", h, df, LANES)
    zeros_w = jnp.zeros((D, IN_TOTAL - IN_USED), f32)
    d_w_in = jnp.concatenate(dws[:3] + [dw_f[:, :HEADS]] + dws[3:] + [zeros_w], axis=1)
    d_b_in = jnp.concatenate(dbs[:3] + [db_f[:, :HEADS]] + dbs[3:] + [zeros_w[:1]], axis=1)
    return dict(loss=loss8[0, 0], grad_x=gx, pre_w=d_pre, w_in=d_w_in, b_in=d_b_in, conv_w=vec[4:8], conv_b=vec[3:4],
                wa_d=d_wad, ba=vec[0:1], wx_d=d_wxd, bx=vec[1:2], lam=vec[2:3], w_a=d_wa, w_r=d_wr, w_o=d_wo,
                post_w=d_post)


def _block_diag(w):
    g, bw, _ = w.shape
    eye = jnp.eye(g, dtype=w.dtype)
    return (w[:, :, None, :] * eye[:, None, :, None]).reshape(g * bw, g * bw)


def _gate_blocks(diag):
    half = diag.shape[1] // 2
    return jnp.stack([diag[:, :half, :half], diag[:, half:, half:]], axis=1).reshape(-1, half, half)


def _pad_cols(a, n):
    return jnp.pad(a, ((0, 0), (0, n - a.shape[1])))


def _pad_rows(a, n):
    return jnp.pad(a, ((0, n - a.shape[0]), (0, 0)))


def kernel(x, pre_norm_w, w_in, b_in, conv_w, conv_b, rg_wa, rg_ba, rg_wx, rg_bx, rg_lambda, w_branch_a, w_branch_r, w_out, post_norm_w, loss_target, m_pre_norm_w, m_w_in, m_b_in, m_conv_w, m_conv_b, m_rg_wa, m_rg_ba, m_rg_wx, m_rg_bx, m_rg_lambda, m_w_branch_a, m_w_branch_r, m_w_out, m_post_norm_w, v_pre_norm_w, v_w_in, v_b_in, v_conv_w, v_conv_b, v_rg_wa, v_rg_ba, v_rg_wx, v_rg_bx, v_rg_lambda, v_w_branch_a, v_w_branch_r, v_w_out, v_post_norm_w):
    nb, seq, _ = x.shape
    chip = 2 * lax.axis_index("x") + lax.axis_index("y")
    n_groups = rg_wa.shape[1]

    g_in, g_a, g_r, g_o, g_cw = _gather_shards(
        [w_in[0].astype(bf16), w_branch_a[0].astype(bf16), w_branch_r[0].astype(bf16), w_out[0].astype(bf16)],
        conv_w[0])
    w_full = jnp.transpose(g_in, (1, 0, 2)).reshape(D, IN_TOTAL)
    q_end, f_end = 3 * D, 3 * D + HEADS
    wt = dict(
        pre_w=pre_norm_w, post_w=post_norm_w,
        w_qkv=w_full[:, :q_end], b_qkv=b_in[:, :q_end],
        w_f=_pad_cols(w_full[:, q_end:f_end], LANES), b_f=_pad_cols(b_in[:, q_end:f_end], LANES),
        w_rest=w_full[:, f_end:IN_USED], b_rest=b_in[:, f_end:IN_USED],
        w_a=g_a.reshape(D, D), w_r=g_r.reshape(D, D), w_o=g_o.reshape(D, D),
        conv_w=jnp.transpose(g_cw, (1, 0, 2)).reshape(4, D), conv_b=conv_b,
        wa_d=_block_diag(rg_wa[0]).astype(bf16), wx_d=_block_diag(rg_wx[0]).astype(bf16),
        ba=rg_ba, bx=rg_bx, lam=rg_lambda)

    part = _local_step(x.reshape(nb * seq, D), loss_target.reshape(nb * seq, D), seq, wt)
    loss = lax.psum(part["loss"], ("x", "y", "c"))
    grad_x = part["grad_x"].reshape(nb, seq, D)

    small = jnp.concatenate([
        part["pre_w"], _pad_cols(part["b_in"], 10 * D).reshape(10, D), part["conv_b"],
        _gate_blocks(part["wa_d"]).reshape(-1, D), part["ba"],
        _gate_blocks(part["wx_d"]).reshape(-1, D), part["bx"], part["lam"], part["post_w"],
        part["conv_w"]], axis=0)
    n_small = small.shape[0]
    n_rep = n_small - 4
    tot = _allsum_rows(_pad_rows(small, -(-n_small // 8) * 8))
    g_rep = tot[:n_rep]
    g_conv_w = lax.dynamic_slice_in_dim(tot[n_rep:n_small], chip * (D // N_CHIPS), D // N_CHIPS, axis=1)

    def pack(pre, b, cb, wa, ba, wx, bx, lam, post):
        return jnp.concatenate([pre, _pad_cols(b, 10 * D).reshape(10, D), cb, wa.reshape(-1, D), ba,
                                wx.reshape(-1, D), bx, lam, post], axis=0)

    def unpack(p):
        o = [0]

        def take(k):
            o[0] += k
            return p[o[0] - k:o[0]]

        pre = take(1)
        b = take(10).reshape(1, 10 * D)[:, :IN_TOTAL]
        cb = take(1)
        wa = take(64).reshape(rg_wa.shape)
        ba = take(1)
        wx = take(64).reshape(rg_wx.shape)
        bx = take(1)
        lam = take(1)
        post = take(1)
        return dict(pre_norm_w=pre, b_in=b, conv_b=cb, rg_wa=wa, rg_ba=ba, rg_wx=wx, rg_bx=bx, rg_lambda=lam,
                    post_norm_w=post)

    w_rep = pack(pre_norm_w, b_in, conv_b, rg_wa, rg_ba, rg_wx, rg_bx, rg_lambda, post_norm_w)
    m_rep = pack(m_pre_norm_w, m_b_in, m_conv_b, m_rg_wa, m_rg_ba, m_rg_wx, m_rg_bx, m_rg_lambda, m_post_norm_w)
    v_rep = pack(v_pre_norm_w, v_b_in, v_conv_b, v_rg_wa, v_rg_ba, v_rg_wx, v_rg_bx, v_rg_lambda, v_post_norm_w)
    d_rep, nm_rep, nv_rep = _adamw("adamw_rep", w_rep, g_rep, m_rep, v_rep)
    grads, deltas, new_m, new_v = unpack(g_rep), unpack(d_rep), unpack(nm_rep), unpack(nv_rep)

    shard_cols = IN_TOTAL // N_CHIPS
    p_in = jnp.transpose(part["w_in"].reshape(D, N_CHIPS, shard_cols), (1, 0, 2)).reshape(N_DEV, D // 2, shard_cols)
    p_aro = jnp.concatenate([part[k].reshape(N_DEV, D // N_DEV, D) for k in ("w_a", "w_r", "w_o")], axis=1)
    s_in, s_aro = _chip_exchange([_pair_reduce("pair_w_in", p_in.astype(bf16)),
                                  _pair_reduce("pair_w_aro", p_aro.astype(bf16))])
    f_in, f_aro = _swap_halves([_sum_slots("sum_w_in", s_in), _sum_slots("sum_w_aro", s_aro)])
    g_w_in = f_in.reshape(D, shard_cols)
    rows = D // N_DEV
    g_aro = jnp.concatenate([f_aro[:, i * rows:(i + 1) * rows, :].reshape(2 * rows, D) for i in range(3)], axis=0)

    d_w_in, nm_w_in, nv_w_in = _adamw("adamw_w_in", w_in[0], g_w_in, m_w_in[0], v_w_in[0])
    stack = lambda a, b, c: jnp.concatenate([a[0], b[0], c[0]], axis=0)
    d_aro, nm_aro, nv_aro = _adamw("adamw_w_aro", stack(w_branch_a, w_branch_r, w_out), g_aro,
                                   stack(m_w_branch_a, m_w_branch_r, m_w_out),
                                   stack(v_w_branch_a, v_w_branch_r, v_w_out))
    d_cw, nm_cw, nv_cw = _adamw("adamw_conv_w", conv_w[0], g_conv_w, m_conv_w[0], v_conv_w[0])

    def sharded(t_in, t_aro, t_cw):
        r2 = 2 * rows
        return dict(w_in=t_in[None], conv_w=t_cw[None], w_branch_a=t_aro[None, :r2], w_branch_r=t_aro[None, r2:2 * r2],
                    w_out=t_aro[None, 2 * r2:])

    order = ["pre_norm_w", "w_in", "b_in", "conv_w", "conv_b", "rg_wa", "rg_ba", "rg_wx", "rg_bx", "rg_lambda",
             "w_branch_a", "w_branch_r", "w_out", "post_norm_w"]
    outs = [loss, grad_x]
    for rep, shd in ((grads, sharded(g_w_in, g_aro, g_conv_w)), (deltas, sharded(d_w_in, d_aro, d_cw)),
                     (new_m, sharded(nm_w_in, nm_aro, nm_cw)), (new_v, sharded(nv_w_in, nv_aro, nv_cw))):
        both = {**rep, **shd}
        outs.extend(both[k] for k in order)
    return tuple(outs)
```

```python
import jax
import jax.numpy as jnp
from jax import lax
from jax.experimental import pallas as pl
from jax.experimental.pallas import tpu as pltpu

f32 = jnp.float32
bf16 = jnp.bfloat16

D = 1024
HEADS = 16
HEAD_PAIRS = 8
LANES = 128
NORM_EPS = 1e-6
MASK_VALUE = -1e30
RG_C = 8.0
QK_SCALE = 0.125
TQ = 256
ATT_GROUP = 8
TL = 256
TM = 256
PREV_ROWS = 16
IN_USED = 8 * D + HEADS
IN_TOTAL = 9 * D + HEADS
N_CHIPS = 4
N_DEV = 8
ADAM_LR, ADAM_B1, ADAM_B2, ADAM_EPS, ADAM_WD, ADAM_STEP = 0.001, 0.9, 0.999, 1e-08, 0.01, 10
VMEM_LIMIT = 56 * 1024 * 1024
MESH = pl.DeviceIdType.MESH


def _dot(a, b):
    return jnp.dot(a, b, preferred_element_type=f32)


def _dot_nt(a, b):
    return lax.dot_general(a, b, (((1,), (1,)), ((), ())), preferred_element_type=f32)


def _dot_tn(a, b):
    return lax.dot_general(a, b, (((0,), (0,)), ((), ())), preferred_element_type=f32)


def _sig(x):
    return 0.5 * jnp.tanh(0.5 * x) + 0.5


def _softplus(x):
    return jnp.maximum(x, 0.0) + jnp.log(1.0 + jnp.exp(-jnp.abs(x)))


def _params(sem, vmem=None):
    return pltpu.CompilerParams(dimension_semantics=sem, vmem_limit_bytes=vmem)


def _tile(tm, width, cb=0):
    return pl.BlockSpec((tm, width), lambda i, cb=cb: (i, cb))


def _whole(shape):
    nd = len(shape)
    return pl.BlockSpec(shape, lambda *_: (0,) * nd)


def _prenorm(x, w_pre):
    t = x.shape[0]

    def body(x_ref, w_ref, h_ref):
        xv = x_ref[...]
        r = lax.rsqrt(jnp.mean(xv * xv, axis=-1, keepdims=True) + NORM_EPS)
        h_ref[...] = (xv * r * w_ref[...]).astype(bf16)

    return pl.pallas_call(
        body, name="prenorm", grid=(t // TM,),
        in_specs=[_tile(TM, D), _whole((1, D))], out_specs=_tile(TM, D),
        out_shape=jax.ShapeDtypeStruct((t, D), bf16),
        compiler_params=_params(("parallel",)),
    )(x, w_pre)


def _mm(name, ins, prologue, w, bias, out_dtype, tm, tn, keep_lhs=False, w_is_nk=False):
    t = ins[0][0].shape[0]
    tm = min(tm, t)
    n, k = w.shape if w_is_nk else w.shape[::-1]
    n_in = len(ins)
    assert not keep_lhs or tn == n

    def body(*refs):
        a = prologue(*[r[...] for r in refs[:n_in]])
        acc = _dot_nt(a, refs[n_in][...]) if w_is_nk else _dot(a, refs[n_in][...])
        if bias is not None:
            acc = acc + refs[n_in + 1][...]
        if keep_lhs:
            refs[-1][...] = a
            refs[-2][...] = acc.astype(out_dtype)
        else:
            refs[-1][...] = acc.astype(out_dtype)

    in_specs = [pl.BlockSpec((tm, k), lambda i, j, cb=cb: (i, cb)) for _, cb in ins]
    in_specs.append(pl.BlockSpec((tn, k), lambda i, j: (j, 0)) if w_is_nk else pl.BlockSpec((k, tn), lambda i, j: (0, j)))
    args = [a for a, _ in ins] + [w]
    if bias is not None:
        in_specs.append(pl.BlockSpec((1, tn), lambda i, j: (0, j)))
        args.append(bias)
    out_specs = pl.BlockSpec((tm, tn), lambda i, j: (i, j))
    out_shape = jax.ShapeDtypeStruct((t, n), out_dtype)
    if keep_lhs:
        out_specs = [out_specs, pl.BlockSpec((tm, k), lambda i, j: (i, 0))]
        out_shape = [out_shape, jax.ShapeDtypeStruct((t, k), bf16)]
    return pl.pallas_call(
        body, name=name, grid=(t // tm, n // tn), in_specs=in_specs, out_specs=out_specs, out_shape=out_shape,
        compiler_params=_params(("parallel", "parallel"), VMEM_LIMIT),
    )(*args)


def _forget_prep(f128, seq):
    t = f128.shape[0]
    nb = seq // LANES

    def body(f_ref, c_ref):
        r = lax.broadcasted_iota(jnp.int32, (LANES, LANES), 0)
        cidx = lax.broadcasted_iota(jnp.int32, (LANES, LANES), 1)
        tri = (r >= cidx).astype(f32)
        carry = jnp.zeros((1, LANES), f32)
        for blk in range(nb):
            fv = f_ref[pl.ds(blk * LANES, LANES), :]
            lf = -_softplus(-fv)
            c_ref[pl.ds(blk * LANES, LANES), :] = (
                jnp.dot(tri, lf, preferred_element_type=f32, precision=lax.Precision.HIGHEST) + carry)
            carry = carry + jnp.sum(lf, axis=0, keepdims=True)

    return pl.pallas_call(
        body, name="forget_prep", grid=(t // seq,),
        in_specs=[pl.BlockSpec((seq, LANES), lambda b: (b, 0))],
        out_specs=pl.BlockSpec((seq, LANES), lambda b: (b, 0)),
        out_shape=jax.ShapeDtypeStruct((t, LANES), f32),
        compiler_params=_params(("parallel",)),
    )(f128)


def _split3(cv):
    hi = cv.astype(bf16)
    r1 = cv - hi.astype(f32)
    mid = r1.astype(bf16)
    lo = (r1 - mid.astype(f32)).astype(bf16)
    return hi, mid, lo


def _attn_prep(qkv, c):
    t = qkv.shape[0]

    def body(q_ref, k_ref, c_ref, qa_ref, ka_ref):
        hp = pl.program_id(1)
        lane = lax.broadcasted_iota(jnp.int32, (1, LANES), 1)
        cv = c_ref[...]
        one = jnp.ones((), bf16)
        zero = jnp.zeros((), bf16)
        for hh in range(2):
            ch = jnp.sum(jnp.where(lane == 2 * hp + hh, cv, 0.0), axis=1, keepdims=True)
            hi, mid, lo = _split3(ch)
            q2, k2 = q_ref[...], k_ref[...]
            if hh == 1:
                q2, k2 = pltpu.roll(q2, 64, 1), pltpu.roll(k2, 64, 1)
            ones = jnp.where((lane >= 67) & (lane < 70), one, zero)
            qa = jnp.where(lane < 64, q2 * jnp.asarray(QK_SCALE, bf16),
                           jnp.where(lane == 64, hi, jnp.where(lane == 65, mid, jnp.where(lane == 66, lo, ones))))
            ones = jnp.where((lane >= 64) & (lane < 67), one, zero)
            ka = jnp.where(lane < 64, k2,
                           jnp.where(lane == 67, -hi, jnp.where(lane == 68, -mid, jnp.where(lane == 69, -lo, ones))))
            qa_ref[:, pl.ds(hh * LANES, LANES)] = qa
            ka_ref[:, pl.ds(hh * LANES, LANES)] = ka

    tm = min(512, t)
    spec = lambda off: pl.BlockSpec((tm, LANES), lambda i, hp: (i, off + hp))
    out = pl.BlockSpec((tm, 2 * LANES), lambda i, hp: (i, hp))
    return pl.pallas_call(
        body, name="attn_prep", grid=(t // tm, HEAD_PAIRS),
        in_specs=[spec(0), spec(HEAD_PAIRS), pl.BlockSpec((tm, LANES), lambda i, hp: (i, 0))],
        out_specs=[out, out],
        out_shape=[jax.ShapeDtypeStruct((t, 2 * D), bf16)] * 2,
        compiler_params=_params(("parallel", "parallel")),
    )(qkv, qkv, c)


def _attn_fwd(qa, ka, qkv, rest, seq):
    t = qkv.shape[0]
    nb, nq = t // seq, seq // TQ

    hg = ATT_GROUP
    ng = HEADS // hg

    def body(q_ref, k_ref, v_ref, ga_ref, o_ref, pa_ref, lse_ref, acc_scr):
        qi, gi = pl.program_id(1), pl.program_id(2)
        krow = lax.broadcasted_iota(jnp.int32, (TQ, TQ), 0)
        qcol = lax.broadcasted_iota(jnp.int32, (TQ, TQ), 1)
        acc_scr[...] = jnp.zeros_like(acc_scr)

        def kv_step(kt, carry, masked):
            ks = pl.multiple_of(kt * TQ, TQ)
            sts = [_dot_nt(k_ref[pl.ds(ks, TQ), pl.ds(g * LANES, LANES)], q_ref[:, pl.ds(g * LANES, LANES)])
                   for g in range(hg)]
            if masked:
                sts = [jnp.where(krow <= qcol, st, MASK_VALUE) for st in sts]
            m_new = [jnp.maximum(carry[g][0], jnp.max(sts[g], axis=0, keepdims=True)) for g in range(hg)]
            ps = [jnp.exp(sts[g] - m_new[g]) for g in range(hg)]
            alphas = [jnp.exp(carry[g][0] - m_new[g]) for g in range(hg)]
            phi = [ps[g].astype(bf16) for g in range(hg)]
            plo = [(ps[g] - phi[g].astype(f32)).astype(bf16) for g in range(hg)]
            vs = [v_ref[pl.ds(ks, TQ), pl.ds(j * LANES, LANES)] for j in range(hg // 2)]
            pvs = [_dot_tn(vs[g // 2], phi[g]) + _dot_tn(vs[g // 2], plo[g]) for g in range(hg)]
            olds = [acc_scr[g] for g in range(hg)]
            for g in range(hg):
                acc_scr[g] = alphas[g] * olds[g] + pvs[g]
            return tuple((m_new[g], alphas[g] * carry[g][1] + jnp.sum(ps[g], axis=0, keepdims=True))
                         for g in range(hg))

        init = tuple((jnp.full((1, TQ), MASK_VALUE, f32), jnp.zeros((1, TQ), f32)) for _ in range(hg))
        carry = lax.fori_loop(0, qi, lambda kt, cr: kv_step(kt, cr, False), init)
        stats = kv_step(qi, carry, True)
        drow = lax.broadcasted_iota(jnp.int32, (LANES, TQ), 0)
        for g in range(hg):
            m, l = stats[g]
            lse_ref[0, pl.ds(hg * gi + g, 1), :] = m + jnp.log(l)
        for j in range(hg // 2):
            o2 = jnp.where(drow < 64, acc_scr[2 * j] / stats[2 * j][1], acc_scr[2 * j + 1] / stats[2 * j + 1][1]).T
            o_ref[:, pl.ds(j * LANES, LANES)] = o2
            ga = ga_ref[:, pl.ds(j * LANES, LANES)].astype(f32)
            pa_ref[:, pl.ds(j * LANES, LANES)] = (o2 * (ga * _sig(ga))).astype(bf16)

    vw = hg * 64
    tile = pl.BlockSpec((TQ, vw), lambda b, qi, gi: (b * nq + qi, gi))
    return pl.pallas_call(
        body, name="attn_fwd", grid=(nb, nq, ng),
        in_specs=[pl.BlockSpec((TQ, hg * LANES), lambda b, qi, gi: (b * nq + qi, gi)),
                  pl.BlockSpec((seq, hg * LANES), lambda b, qi, gi: (b, gi)),
                  pl.BlockSpec((seq, vw), lambda b, qi, gi: (b, 2 * ng + gi)), tile],
        out_specs=[tile, tile, pl.BlockSpec((1, HEADS, TQ), lambda b, qi, gi: (b * nq + qi, 0, 0))],
        out_shape=[jax.ShapeDtypeStruct((t, D), f32), jax.ShapeDtypeStruct((t, D), bf16),
                   jax.ShapeDtypeStruct((t // TQ, HEADS, TQ), f32)],
        scratch_shapes=[pltpu.VMEM((hg, LANES, TQ), f32)],
        compiler_params=_params(("parallel", "parallel", "arbitrary"), VMEM_LIMIT),
    )(qa, ka, qkv, rest)


def _shifted_rows(x, top8, prev8, shift, row, row8):
    body = pltpu.roll(x, shift, 0)
    head = jnp.where(row8 < shift, pltpu.roll(prev8, shift, 0), pltpu.roll(top8, shift, 0))
    return body, head


def _rnn_gates(xc, wa_ref, wx_ref, ba_ref, bx_ref, lam_ref):
    xcb = xc.astype(bf16)
    r = _sig(_dot(xcb, wa_ref[...]) + ba_ref[...])
    i = _sig(_dot(xcb, wx_ref[...]) + bx_ref[...])
    sp = _softplus(-lam_ref[...])
    log_a = (-RG_C) * r * sp
    th = jnp.tanh(log_a)
    w1 = (-2.0) * th / (1.0 - th)
    sq = jnp.sqrt(jnp.maximum(w1, 0.0))
    return r, i, sp, log_a, w1, sq


def _conv_tile(x_ref, xprev_ref, has_prev, cw_ref, cb_ref, xc_ref):
    row = lax.broadcasted_iota(jnp.int32, (TL, D), 0)
    row8 = lax.broadcasted_iota(jnp.int32, (8, D), 0)
    x = x_ref[...].astype(f32)
    top8 = x[:8]
    prev8 = jnp.where(has_prev, xprev_ref[...].astype(f32)[PREV_ROWS - 8:], 0.0)
    xc = cb_ref[...] + cw_ref[pl.ds(3, 1), :] * x
    xc8 = cb_ref[...] + cw_ref[pl.ds(3, 1), :] * top8
    for sh in range(1, 4):
        w = cw_ref[pl.ds(3 - sh, 1), :]
        xs, xs8 = _shifted_rows(x, top8, prev8, sh, row, row8)
        xc = xc + w * xs
        xc8 = xc8 + w * xs8
    xc_ref[...] = xc
    xc_ref[pl.ds(0, 8), :] = xc8


def _rnn_fwd(rest, conv_w, conv_b, wa_d, wx_d, ba, bx, lam, seq):
    t = rest.shape[0]
    nb, nt = t // seq, seq // TL

    def body(x_ref, xprev_ref, gr_ref, cw_ref, cb_ref, wa_ref, wx_ref, ba_ref, bx_ref, lam_ref,
             xc_ref, a_ref, h_ref, pr_ref, u_scr, carry):
        tt = pl.program_id(1)
        _conv_tile(x_ref, xprev_ref, tt > 0, cw_ref, cb_ref, xc_ref)
        xc = xc_ref[...]
        r, i, sp, log_a, w1, sq = _rnn_gates(xc, wa_ref, wx_ref, ba_ref, bx_ref, lam_ref)
        a_ref[...] = jnp.exp(log_a)
        u_scr[...] = sq * (i * xc)

        @pl.when(tt == 0)
        def _():
            carry[...] = jnp.zeros_like(carry)

        def step(s, h):
            h = a_ref[pl.ds(s, 1), :] * h + u_scr[pl.ds(s, 1), :]
            h_ref[pl.ds(s, 1), :] = h
            return h

        carry[...] = lax.fori_loop(0, TL, step, carry[...], unroll=8)
        gr = gr_ref[...].astype(f32)
        pr_ref[...] = (h_ref[...] * (gr * _sig(gr))).astype(bf16)

    tile = lambda cb: pl.BlockSpec((TL, D), lambda b, tt, cb=cb: (b * nt + tt, cb))
    prev = lambda cb: pl.BlockSpec(
        (PREV_ROWS, D), lambda b, tt, cb=cb: (jnp.maximum((b * nt + tt) * (TL // PREV_ROWS) - 1, 0), cb))
    vec = _whole((1, D))
    return pl.pallas_call(
        body, name="rnn_fwd", grid=(nb, nt),
        in_specs=[tile(1), prev(1), tile(2), _whole((4, D)), vec, _whole((D, D)), _whole((D, D)), vec, vec, vec],
        out_specs=[tile(0)] * 4,
        out_shape=[jax.ShapeDtypeStruct((t, D), f32)] * 3 + [jax.ShapeDtypeStruct((t, D), bf16)],
        scratch_shapes=[pltpu.VMEM((TL, D), f32), pltpu.VMEM((1, D), f32)],
        compiler_params=_params(("parallel", "arbitrary"), VMEM_LIMIT),
    )(rest, rest, rest, conv_w, conv_b, wa_d, wx_d, ba, bx, lam)


def _merge(mga, mgr, ya, yr):
    return (_sig(mga.astype(f32)) * ya + _sig(mgr.astype(f32)) * yr).astype(bf16)


def _post_loss(o, x, tgt, w_post):
    t = o.shape[0]

    def body(o_ref, x_ref, t_ref, w_ref, do_ref, dy_ref, loss_ref, dwp_ref):
        @pl.when(pl.program_id(0) == 0)
        def _():
            loss_ref[...] = jnp.zeros_like(loss_ref)
            dwp_ref[...] = jnp.zeros_like(dwp_ref)

        ov = o_ref[...]
        w = w_ref[...]
        r2 = lax.rsqrt(jnp.mean(ov * ov, axis=-1, keepdims=True) + NORM_EPS)
        oh = ov * r2
        e = x_ref[...] + oh * w - t_ref[...]
        loss_ref[...] += 0.5 * jnp.sum(jnp.mean(e * e, axis=-1, keepdims=True))
        dy = e * (1.0 / D)
        dy_ref[...] = dy
        dwp_ref[...] += jnp.sum(dy * oh, axis=0, keepdims=True)
        doh = dy * w
        do_ref[...] = (r2 * (doh - oh * jnp.mean(doh * oh, axis=-1, keepdims=True))).astype(bf16)

    return pl.pallas_call(
        body, name="post_loss", grid=(t // TM,),
        in_specs=[_tile(TM, D)] * 3 + [_whole((1, D))],
        out_specs=[_tile(TM, D), _tile(TM, D), _whole((8, LANES)), _whole((1, D))],
        out_shape=[jax.ShapeDtypeStruct((t, D), bf16), jax.ShapeDtypeStruct((t, D), f32),
                   jax.ShapeDtypeStruct((8, LANES), f32), jax.ShapeDtypeStruct((1, D), f32)],
        compiler_params=_params(("arbitrary",)),
    )(o, x, tgt, w_post)


def _out_bwd(do, rest, ya, yr, w_out):
    t = do.shape[0]

    def body(do_ref, mga_ref, mgr_ref, ya_ref, yr_ref, w_ref, dya_ref, dyr_ref, dmga_ref, dmgr_ref):
        sa, sr = _sig(mga_ref[...].astype(f32)), _sig(mgr_ref[...].astype(f32))
        ya, yr = ya_ref[...], yr_ref[...]
        dm = _dot_nt(do_ref[...], w_ref[...])
        dya_ref[...] = (dm * sa).astype(bf16)
        dyr_ref[...] = (dm * sr).astype(bf16)
        dmga_ref[...] = (dm * ya * sa * (1.0 - sa)).astype(bf16)
        dmgr_ref[...] = (dm * yr * sr * (1.0 - sr)).astype(bf16)

    return pl.pallas_call(
        body, name="out_bwd", grid=(t // TM,),
        in_specs=[_tile(TM, D), _tile(TM, D, 3), _tile(TM, D, 4), _tile(TM, D), _tile(TM, D), _whole((D, D))],
        out_specs=[_tile(TM, D)] * 4,
        out_shape=[jax.ShapeDtypeStruct((t, D), bf16)] * 4,
        compiler_params=_params(("parallel",), VMEM_LIMIT),
    )(do, rest, rest, ya, yr, w_out)


def _branch_bwd(name, dyb, rest, gate_cb, act, w, act_grad_dtype):
    t = dyb.shape[0]

    def body(dy_ref, g_ref, act_ref, w_ref, dact_ref, dg_ref):
        dp = _dot_nt(dy_ref[...], w_ref[...])
        g = g_ref[...].astype(f32)
        sg = _sig(g)
        dact_ref[...] = (dp * (g * sg)).astype(act_grad_dtype)
        dg_ref[...] = (dp * act_ref[...] * (sg * (1.0 + g * (1.0 - sg)))).astype(bf16)

    return pl.pallas_call(
        body, name=name, grid=(t // TM,),
        in_specs=[_tile(TM, D), _tile(TM, D, gate_cb), _tile(TM, D), _whole((D, D))],
        out_specs=[_tile(TM, D), _tile(TM, D)],
        out_shape=[jax.ShapeDtypeStruct((t, D), act_grad_dtype), jax.ShapeDtypeStruct((t, D), bf16)],
        compiler_params=_params(("parallel",), VMEM_LIMIT),
    )(dyb, rest, act, w)


def _rnn_bwd(dh, a, h, xc, rest, conv_w, conv_b, wa_d, wx_d, ba, bx, lam, seq):
    t = dh.shape[0]
    nb, nt = t // seq, seq // TL
    diag = (D // LANES, LANES, LANES)

    def body(dh_ref, a_ref, h_ref, hprev_ref, xc_ref, x_ref, xprev_ref, cw_ref, cb_ref, wa_ref, wx_ref,
             ba_ref, bx_ref, lam_ref, dxr_ref, dwa_ref, dwx_ref, vec_ref, g_scr, dxc_scr, dxr_scr, qcarry, dxc_next):
        b, tt = pl.program_id(0), pl.program_id(1)
        rt = nt - 1 - tt

        @pl.when((b == 0) & (tt == 0))
        def _():
            dwa_ref[...] = jnp.zeros_like(dwa_ref)
            dwx_ref[...] = jnp.zeros_like(dwx_ref)
            vec_ref[...] = jnp.zeros_like(vec_ref)

        @pl.when(tt == 0)
        def _():
            qcarry[...] = jnp.zeros_like(qcarry)
            dxc_next[...] = jnp.zeros_like(dxc_next)

        def step(k, q):
            s = TL - 1 - k
            g = dh_ref[pl.ds(s, 1), :] + q
            g_scr[pl.ds(s, 1), :] = g
            return a_ref[pl.ds(s, 1), :] * g

        qcarry[...] = lax.fori_loop(0, TL, step, qcarry[...], unroll=8)

        row = lax.broadcasted_iota(jnp.int32, (TL, D), 0)
        row8 = lax.broadcasted_iota(jnp.int32, (8, D), 0)
        g = g_scr[...]
        av = a_ref[...]
        xc = xc_ref[...]
        hlast = jnp.where(rt > 0, hprev_ref[pl.ds(PREV_ROWS - 1, 1), :], 0.0)
        hp = jnp.where(row == 0, hlast, pltpu.roll(h_ref[...], 1, 0))
        r, i, sp, log_a, w1, sq = _rnn_gates(xc, wa_ref, wx_ref, ba_ref, bx_ref, lam_ref)
        dix = g * sq
        di = dix * xc
        dxc = dix * i
        dsq = g * (i * xc)
        dlog_a = g * hp * av - dsq * jnp.where(sq > 0.0, (1.0 - w1) / sq, 0.0)
        dpr = (dlog_a * ((-RG_C) * sp)) * r * (1.0 - r)
        dpi = di * i * (1.0 - i)
        dprb, dpib, xcb = dpr.astype(bf16), dpi.astype(bf16), xc.astype(bf16)
        dxc = dxc + _dot_nt(dprb, wa_ref[...]) + _dot_nt(dpib, wx_ref[...])
        for j in range(D // LANES):
            cols = slice(j * LANES, (j + 1) * LANES)
            dwa_ref[j] += _dot_tn(xcb[:, cols], dprb[:, cols])
            dwx_ref[j] += _dot_tn(xcb[:, cols], dpib[:, cols])
        vec_ref[pl.ds(0, 1), :] += jnp.sum(dpr, axis=0, keepdims=True)
        vec_ref[pl.ds(1, 1), :] += jnp.sum(dpi, axis=0, keepdims=True)
        dsp = jnp.sum(dlog_a * ((-RG_C) * r), axis=0, keepdims=True)
        vec_ref[pl.ds(2, 1), :] += dsp * (-_sig(-lam_ref[...]))
        vec_ref[pl.ds(3, 1), :] += jnp.sum(dxc, axis=0, keepdims=True)

        dxc_scr[...] = dxc
        bot8 = dxc_scr[pl.ds(TL - 8, 8), :]
        nxt8 = dxc_next[...]
        dxr = cw_ref[pl.ds(3, 1), :] * dxc
        dxr8 = cw_ref[pl.ds(3, 1), :] * bot8
        for sh in range(1, 4):
            w = cw_ref[pl.ds(3 - sh, 1), :]
            dxr = dxr + w * pltpu.roll(dxc, TL - sh, 0)
            dxr8 = dxr8 + w * jnp.where(row8 < 8 - sh, pltpu.roll(bot8, 8 - sh, 0), pltpu.roll(nxt8, 8 - sh, 0))
        dxr_scr[...] = dxr
        dxr_scr[pl.ds(TL - 8, 8), :] = dxr8
        dxr_ref[...] = dxr_scr[...].astype(bf16)
        dxc_next[...] = dxc_scr[pl.ds(0, 8), :]

        x = x_ref[...].astype(f32)
        prev8 = jnp.where(rt > 0, xprev_ref[...].astype(f32)[PREV_ROWS - 8:], 0.0)
        dxc_top8 = dxc_scr[pl.ds(0, 8), :]
        vec_ref[pl.ds(7, 1), :] += jnp.sum(dxc * x, axis=0, keepdims=True)
        for sh in range(1, 4):
            inside = jnp.sum(dxc * jnp.where(row >= sh, pltpu.roll(x, sh, 0), 0.0), axis=0, keepdims=True)
            above = jnp.sum(dxc_top8 * jnp.where(row8 < sh, pltpu.roll(prev8, sh, 0), 0.0), axis=0, keepdims=True)
            vec_ref[pl.ds(7 - sh, 1), :] += inside + above

    tile = lambda cb: pl.BlockSpec((TL, D), lambda b, tt, cb=cb: (b * nt + nt - 1 - tt, cb))
    prev = lambda cb: pl.BlockSpec(
        (PREV_ROWS, D), lambda b, tt, cb=cb: (jnp.maximum((b * nt + nt - 1 - tt) * (TL // PREV_ROWS) - 1, 0), cb))
    vec = _whole((1, D))
    return pl.pallas_call(
        body, name="rnn_bwd", grid=(nb, nt),
        in_specs=[tile(0), tile(0), tile(0), prev(0), tile(0), tile(1), prev(1),
                  _whole((4, D)), vec, _whole((D, D)), _whole((D, D)), vec, vec, vec],
        out_specs=[tile(0), _whole(diag), _whole(diag), _whole((8, D))],
        out_shape=[jax.ShapeDtypeStruct((t, D), bf16), jax.ShapeDtypeStruct(diag, f32),
                   jax.ShapeDtypeStruct(diag, f32), jax.ShapeDtypeStruct((8, D), f32)],
        scratch_shapes=[pltpu.VMEM((TL, D), f32), pltpu.VMEM((TL, D), f32), pltpu.VMEM((TL, D), f32),
                        pltpu.VMEM((1, D), f32), pltpu.VMEM((8, D), f32)],
        compiler_params=_params(("arbitrary", "arbitrary"), VMEM_LIMIT),
    )(dh, a, h, h, xc, rest, rest, conv_w, conv_b, wa_d, wx_d, ba, bx, lam)


def _attn_delta(doa, o):
    t = doa.shape[0]

    def body(do_ref, o_ref, d_ref):
        prod = do_ref[...].astype(f32) * o_ref[...]
        ch = lax.broadcasted_iota(jnp.int32, (D, LANES), 0)
        hd = lax.broadcasted_iota(jnp.int32, (D, LANES), 1)
        pick = (ch // 64 == hd).astype(f32)
        per_head = jnp.dot(prod, pick, preferred_element_type=f32, precision=lax.Precision.HIGHEST)
        d_ref[0] = per_head.T[:HEADS, :]

    return pl.pallas_call(
        body, name="attn_delta", grid=(t // TQ,),
        in_specs=[_tile(TQ, D), _tile(TQ, D)],
        out_specs=pl.BlockSpec((1, HEADS, TQ), lambda i: (i, 0, 0)),
        out_shape=jax.ShapeDtypeStruct((t // TQ, HEADS, TQ), f32),
        compiler_params=_params(("parallel",)),
    )(doa, o)


def _attn_bwd(qa, ka, qkv, doa, lse, delta, seq):
    t = qkv.shape[0]
    nb, nq = t // seq, seq // TQ
    hg = ATT_GROUP
    ng, npair = HEADS // hg, hg // 2

    def body(qa_ref, ka_ref, q_ref, k_ref, v_ref, do_ref, lse_ref, dl_ref, dq_ref, dk_ref, dv_ref, dc_ref,
             dqt_scr, dk_scr, dv_scr, ds_scr, kht_scr):
        gi, kt = pl.program_id(1), pl.program_id(2)
        lane = lax.broadcasted_iota(jnp.int32, (1, LANES), 1)
        krow = lax.broadcasted_iota(jnp.int32, (TQ, TQ), 0)
        qcol = lax.broadcasted_iota(jnp.int32, (TQ, TQ), 1)
        lmask = [(lane // 64) == hh for hh in range(2)]
        scale = jnp.asarray(QK_SCALE, bf16)

        @pl.when(kt == 0)
        def _():
            dqt_scr[...] = jnp.zeros_like(dqt_scr)

        dk_scr[...] = jnp.zeros_like(dk_scr)
        dv_scr[...] = jnp.zeros_like(dv_scr)
        ds_scr[...] = jnp.zeros_like(ds_scr)
        for g in range(hg):
            k2 = k_ref[:, pl.ds((g // 2) * LANES, LANES)]
            kht_scr[g] = jnp.where(lmask[g % 2], k2, jnp.zeros_like(k2)).T

        def q_step(qt, masked):
            qs = pl.multiple_of(qt * TQ, TQ)
            heads = range(hg)
            do2 = [do_ref[pl.ds(qs, TQ), pl.ds(j * LANES, LANES)] for j in range(npair)]
            q2 = [q_ref[pl.ds(qs, TQ), pl.ds(j * LANES, LANES)] for j in range(npair)]
            doh = [jnp.where(lmask[g % 2], do2[g // 2], jnp.zeros_like(do2[0])) for g in heads]
            qh = [jnp.where(lmask[g % 2], q2[g // 2], jnp.zeros_like(q2[0])) * scale for g in heads]
            st = [_dot_nt(ka_ref[:, pl.ds(g * LANES, LANES)], qa_ref[pl.ds(qs, TQ), pl.ds(g * LANES, LANES)])
                  for g in heads]
            if masked:
                st = [jnp.where(krow <= qcol, s, MASK_VALUE) for s in st]
            dp = [_dot_nt(v_ref[:, pl.ds((g // 2) * LANES, LANES)], doh[g]) for g in heads]
            p = [jnp.exp(st[g] - lse_ref[qt, pl.ds(hg * gi + g, 1), :]) for g in heads]
            ds = [p[g] * (dp[g] - dl_ref[qt, pl.ds(hg * gi + g, 1), :]) for g in heads]
            pb = [x.astype(bf16) for x in p]
            dsb = [x.astype(bf16) for x in ds]
            for j in range(npair):
                a, b = 2 * j, 2 * j + 1
                dv_scr[j] += _dot(pb[a], doh[a]) + _dot(pb[b], doh[b])
                dk_scr[j] += _dot(dsb[a], qh[a]) + _dot(dsb[b], qh[b])
                dqt_scr[qt, j] += (_dot(kht_scr[a], dsb[a]) + _dot(kht_scr[b], dsb[b])) * QK_SCALE
            for g in heads:
                ds_scr[g] += ds[g][:, :LANES] + ds[g][:, LANES:]

        q_step(kt, True)

        def loop_body(qt, carry):
            q_step(qt, False)
            return carry

        lax.fori_loop(kt + 1, nq, loop_body, 0)

        dc = jnp.zeros((TQ, LANES), f32)
        for g in range(hg):
            dc = jnp.where(lane == g, -jnp.sum(ds_scr[g], axis=1, keepdims=True), dc)
        dc_ref[...] = dc
        for j in range(npair):
            dk_ref[:, pl.ds(j * LANES, LANES)] = dk_scr[j].astype(bf16)
            dv_ref[:, pl.ds(j * LANES, LANES)] = dv_scr[j].astype(bf16)

        @pl.when(kt == nq - 1)
        def _():
            for qt in range(nq):
                for j in range(npair):
                    dq_ref[pl.ds(qt * TQ, TQ), pl.ds(j * LANES, LANES)] = dqt_scr[qt, j].T.astype(bf16)

    vw = hg * 64
    seqspec = pl.BlockSpec((seq, vw), lambda b, gi, kt: (b, gi))
    kspec = lambda off: pl.BlockSpec((TQ, vw), lambda b, gi, kt: (b * nq + kt, off + gi))
    rowspec = pl.BlockSpec((nq, HEADS, TQ), lambda b, gi, kt: (b, 0, 0))
    return pl.pallas_call(
        body, name="attn_bwd", grid=(nb, ng, nq),
        in_specs=[pl.BlockSpec((seq, hg * LANES), lambda b, gi, kt: (b, gi)),
                  pl.BlockSpec((TQ, hg * LANES), lambda b, gi, kt: (b * nq + kt, gi)),
                  seqspec, kspec(ng), kspec(2 * ng), seqspec, rowspec, rowspec],
        out_specs=[seqspec, kspec(0), kspec(0), pl.BlockSpec((TQ, LANES), lambda b, gi, kt: (b * nq + kt, gi))],
        out_shape=[jax.ShapeDtypeStruct((t, D), bf16)] * 3 + [jax.ShapeDtypeStruct((t, ng * LANES), f32)],
        scratch_shapes=[pltpu.VMEM((nq, npair, LANES, TQ), f32), pltpu.VMEM((npair, TQ, LANES), f32),
                        pltpu.VMEM((npair, TQ, LANES), f32), pltpu.VMEM((hg, TQ, LANES), f32),
                        pltpu.VMEM((hg, LANES, TQ), bf16)],
        compiler_params=_params(("parallel", "parallel", "arbitrary"), VMEM_LIMIT),
    )(qa, ka, qkv, qkv, qkv, doa, lse, delta)


def _forget_bwd(dc, f128, seq):
    t = f128.shape[0]
    nb = seq // LANES

    def body(dc_ref, f_ref, df_ref, dbf_ref):
        @pl.when(pl.program_id(0) == 0)
        def _():
            dbf_ref[...] = jnp.zeros_like(dbf_ref)

        r = lax.broadcasted_iota(jnp.int32, (LANES, LANES), 0)
        cidx = lax.broadcasted_iota(jnp.int32, (LANES, LANES), 1)
        tri = (r <= cidx).astype(f32)
        carry = jnp.zeros((1, LANES), f32)
        total = jnp.zeros((1, LANES), f32)
        for blk in reversed(range(nb)):
            dcb = dc_ref[pl.ds(blk * LANES, LANES), :]
            dlf = jnp.dot(tri, dcb, preferred_element_type=f32, precision=lax.Precision.HIGHEST) + carry
            df = dlf * _sig(-f_ref[pl.ds(blk * LANES, LANES), :])
            df_ref[pl.ds(blk * LANES, LANES), :] = df.astype(bf16)
            total = total + jnp.sum(df, axis=0, keepdims=True)
            carry = carry + jnp.sum(dcb, axis=0, keepdims=True)
        dbf_ref[...] += total

    return pl.pallas_call(
        body, name="forget_bwd", grid=(t // seq,),
        in_specs=[pl.BlockSpec((seq, LANES), lambda b: (b, 0)), pl.BlockSpec((seq, LANES), lambda b: (b, 0))],
        out_specs=[pl.BlockSpec((seq, LANES), lambda b: (b, 0)), _whole((1, LANES))],
        out_shape=[jax.ShapeDtypeStruct((t, LANES), bf16), jax.ShapeDtypeStruct((1, LANES), f32)],
        compiler_params=_params(("arbitrary",)),
    )(dc, f128)


def _in_bwd(dz, df, x, dy, w_qkv, w_rest, w_f, w_pre):
    t = x.shape[0]
    n_qkv = w_qkv.shape[0] // D
    n_rest = w_rest.shape[0] // D

    def body(*refs):
        dz_refs = refs[:n_qkv + n_rest]
        df_ref, x_ref, dy_ref, wq_ref, wr_ref, wf_ref, wp_ref, gx_ref, dwp_ref = refs[n_qkv + n_rest:]

        @pl.when(pl.program_id(0) == 0)
        def _():
            dwp_ref[...] = jnp.zeros_like(dwp_ref)

        dh = _dot(df_ref[...], wf_ref[...])
        for p in range(n_qkv):
            dh = dh + _dot(dz_refs[p][...], wq_ref[pl.ds(p * D, D), :])
        for p in range(n_rest):
            dh = dh + _dot(dz_refs[n_qkv + p][...], wr_ref[pl.ds(p * D, D), :])
        xv = x_ref[...]
        r1 = lax.rsqrt(jnp.mean(xv * xv, axis=-1, keepdims=True) + NORM_EPS)
        xh = xv * r1
        dwp_ref[...] += jnp.sum(dh * xh, axis=0, keepdims=True)
        dxh = dh * wp_ref[...]
        gx_ref[...] = dy_ref[...] + r1 * (dxh - xh * jnp.mean(dxh * xh, axis=-1, keepdims=True))

    once = lambda shape: pl.BlockSpec(shape, lambda i: (0, 0), pipeline_mode=pl.Buffered(1))
    return pl.pallas_call(
        body, name="in_bwd", grid=(t // TM,),
        in_specs=[_tile(TM, D)] * (n_qkv + n_rest) + [_tile(TM, LANES), _tile(TM, D), _tile(TM, D),
                  once(w_qkv.shape), once(w_rest.shape), once(w_f.shape), _whole((1, D))],
        out_specs=[_tile(TM, D), _whole((1, D))],
        out_shape=[jax.ShapeDtypeStruct((t, D), f32), jax.ShapeDtypeStruct((1, D), f32)],
        compiler_params=_params(("arbitrary",), VMEM_LIMIT),
    )(*dz, df, x, dy, w_qkv, w_rest, w_f, w_pre)


def _tn_mm(name, a, b, tn, tk=2048):
    t, k = a.shape
    tk = min(tk, t)
    n = b.shape[1]

    def body(a_ref, b_ref, o_ref, s_ref):
        j, kk = pl.program_id(0), pl.program_id(1)

        @pl.when(kk == 0)
        def _():
            o_ref[...] = jnp.zeros_like(o_ref)

        @pl.when((j == 0) & (kk == 0))
        def _():
            s_ref[...] = jnp.zeros_like(s_ref)

        av = a_ref[...]
        o_ref[...] += _dot_tn(av, b_ref[...])

        @pl.when(j == 0)
        def _():
            s_ref[...] += jnp.sum(av.astype(f32), axis=0, keepdims=True)

    return pl.pallas_call(
        body, name=name, grid=(n // tn, t // tk),
        in_specs=[pl.BlockSpec((tk, k), lambda j, kk: (kk, 0)), pl.BlockSpec((tk, tn), lambda j, kk: (kk, j))],
        out_specs=[pl.BlockSpec((k, tn), lambda j, kk: (0, j)), _whole((1, k))],
        out_shape=[jax.ShapeDtypeStruct((k, n), f32), jax.ShapeDtypeStruct((1, k), f32)],
        compiler_params=_params(("arbitrary", "arbitrary"), VMEM_LIMIT),
    )(a, b)


def _position():
    return lax.axis_index("x"), lax.axis_index("y"), lax.axis_index("c")


def _gather_shards(parts, small):
    n = len(parts)
    halves = [p.shape[0] // 2 for p in parts]

    def body(*refs):
        srcs, small_src = refs[:n], refs[n]
        dsts, small_dst = refs[n + 1:2 * n + 1], refs[2 * n + 1]
        send, recv, local = refs[2 * n + 2:]
        x, y, c = _position()
        me = 2 * x + y
        chips = [(1 - x, y), (x, 1 - y), (1 - x, 1 - y)]
        ids = [2 * px + py for px, py in chips]

        def half(a, shard, which):
            return dsts[a].at[shard, pl.ds(which * halves[a], halves[a]), :]

        def over_ici(a, j, shard):
            px, py = chips[j]
            return pltpu.make_async_remote_copy(
                src_ref=srcs[a].at[pl.ds(c * halves[a], halves[a]), :], dst_ref=half(a, shard, c),
                send_sem=send.at[a * 3 + j], recv_sem=recv.at[a * 3 + j], device_id=(px, py, c), device_id_type=MESH)

        def to_sibling(a, j, which):
            k = 3 * n + a * 3 + j
            return pltpu.make_async_remote_copy(
                src_ref=half(a, ids[j], which), dst_ref=half(a, ids[j], which), send_sem=send.at[k],
                recv_sem=recv.at[k], device_id=(x, y, 1 - c), device_id_type=MESH)

        def small_copy(j, shard):
            px, py = chips[j]
            return pltpu.make_async_remote_copy(
                src_ref=small_src, dst_ref=small_dst.at[shard], send_sem=send.at[6 * n + j], recv_sem=recv.at[6 * n + j],
                device_id=(px, py, c), device_id_type=MESH)

        own = [pltpu.make_async_copy(srcs[a], dsts[a].at[me], local.at[a]) for a in range(n)]
        own.append(pltpu.make_async_copy(small_src, small_dst.at[me], local.at[n]))
        for cp in own:
            cp.start()
        first = [over_ici(a, j, me) for j in range(3) for a in range(n)] + [small_copy(j, me) for j in range(3)]
        for cp in first:
            cp.start()
        passed = []
        for j in range(3):
            for a in range(n):
                over_ici(a, j, ids[j]).wait_recv()
                passed.append(to_sibling(a, j, c))
                passed[-1].start()
        for j in range(3):
            small_copy(j, ids[j]).wait_recv()
            for a in range(n):
                to_sibling(a, j, 1 - c).wait_recv()
        for cp in first + passed:
            cp.wait_send()
        for cp in own:
            cp.wait()

    vm = pl.BlockSpec(memory_space=pltpu.VMEM)
    return pl.pallas_call(
        body, name="gather_shards",
        in_specs=[vm] * (n + 1), out_specs=[vm] * (n + 1),
        out_shape=[jax.ShapeDtypeStruct((N_CHIPS,) + p.shape, p.dtype) for p in parts + [small]],
        scratch_shapes=[pltpu.SemaphoreType.DMA((6 * n + 3,)), pltpu.SemaphoreType.DMA((6 * n + 3,)),
                        pltpu.SemaphoreType.DMA((n + 1,))],
        compiler_params=pltpu.CompilerParams(vmem_limit_bytes=VMEM_LIMIT),
    )(*parts, small)


def _allsum_rows(part):
    rows_n = part.shape[0]

    def body(x_ref, gath_ref, sum_ref, send_sems, recv_sems, local_sem):
        x, y, c = _position()
        me, sibling = (x, y, c), (x, y, 1 - c)
        chips = [(1 - x, y), (x, 1 - y), (1 - x, 1 - y)]

        def rows(px, py, pc):
            return gath_ref.at[pl.ds((4 * px + 2 * py + pc) * rows_n, rows_n), :]

        def copy(k, block, to, src=None):
            return pltpu.make_async_remote_copy(
                src_ref=rows(*block) if src is None else src, dst_ref=rows(*block),
                send_sem=send_sems.at[k], recv_sem=recv_sems.at[k], device_id=to, device_id_type=MESH)

        mine = pltpu.make_async_copy(x_ref, rows(*me), local_sem)
        mine.start()
        first = [copy(0, me, sibling, src=x_ref)]
        first += [copy(1 + j, me, (*chip, c), src=x_ref) for j, chip in enumerate(chips)]
        for cp in first:
            cp.start()
        passed = [copy(4 + j, (*chip, c), sibling) for j, chip in enumerate(chips)]
        for j, chip in enumerate(chips):
            copy(1 + j, (*chip, c), me).wait_recv()
            passed[j].start()
        copy(0, sibling, me).wait_recv()
        for j, chip in enumerate(chips):
            copy(4 + j, (*chip, 1 - c), me).wait_recv()
        for cp in first + passed:
            cp.wait_send()
        mine.wait()
        total = gath_ref[pl.ds(0, rows_n), :]
        for d in range(1, N_DEV):
            total = total + gath_ref[pl.ds(d * rows_n, rows_n), :]
        sum_ref[...] = total

    vm = pl.BlockSpec(memory_space=pltpu.VMEM)
    return pl.pallas_call(
        body, name="allsum_rows", in_specs=[vm], out_specs=[vm, vm],
        out_shape=[jax.ShapeDtypeStruct((N_DEV * rows_n, D), f32), jax.ShapeDtypeStruct((rows_n, D), f32)],
        scratch_shapes=[pltpu.SemaphoreType.DMA((7,)), pltpu.SemaphoreType.DMA((7,)), pltpu.SemaphoreType.DMA],
    )(part)[1]


PAIR_ROWS = 16


def _pair_reduce(name, pieces):
    _, r, n = pieces.shape

    def body(p_ref, o_ref, land, send, recv):
        x, y, c = _position()

        def remote(j, half):
            return pltpu.make_async_remote_copy(
                src_ref=p_ref.at[2 * j + half], dst_ref=land.at[j], send_sem=send.at[j], recv_sem=recv.at[j],
                device_id=(x, y, 1 - c), device_id_type=MESH)

        sends = [remote(j, 1 - c) for j in range(N_CHIPS)]
        for cp in sends:
            cp.start()
        for j in range(N_CHIPS):
            remote(j, c).wait_recv()

            def add_rows(i, carry, j=j):
                rows = pl.ds(pl.multiple_of(i * PAIR_ROWS, PAIR_ROWS), PAIR_ROWS)
                o_ref[j, rows, :] = (p_ref[2 * j + c, rows, :].astype(f32) + land[j, rows, :].astype(f32)).astype(bf16)
                return carry

            lax.fori_loop(0, r // PAIR_ROWS, add_rows, 0)
        for cp in sends:
            cp.wait_send()

    vm = pl.BlockSpec(memory_space=pltpu.VMEM)
    return pl.pallas_call(
        body, name=name, in_specs=[vm], out_specs=vm,
        out_shape=jax.ShapeDtypeStruct((N_CHIPS, r, n), bf16),
        scratch_shapes=[pltpu.VMEM((N_CHIPS, r, n), bf16), pltpu.SemaphoreType.DMA((N_CHIPS,)),
                        pltpu.SemaphoreType.DMA((N_CHIPS,))],
        compiler_params=pltpu.CompilerParams(vmem_limit_bytes=VMEM_LIMIT),
    )(pieces)


def _chip_exchange(arrs):
    n = len(arrs)

    def body(*refs):
        srcs, dsts = refs[:n], refs[n:2 * n]
        send, recv, local = refs[2 * n:]
        x, y, c = _position()
        me = 2 * x + y
        chips = [(1 - x, y), (x, 1 - y), (1 - x, 1 - y)]

        def remote(a, j, piece, landing):
            px, py = chips[j]
            return pltpu.make_async_remote_copy(
                src_ref=srcs[a].at[piece], dst_ref=dsts[a].at[landing], send_sem=send.at[a * 3 + j],
                recv_sem=recv.at[a * 3 + j], device_id=(px, py, c), device_id_type=MESH)

        own = [pltpu.make_async_copy(srcs[a].at[me], dsts[a].at[me], local.at[a]) for a in range(n)]
        sends = [remote(a, j, 2 * px + py, me) for j, (px, py) in enumerate(chips) for a in range(n)]
        for cp in sends + own:
            cp.start()
        for j, (px, py) in enumerate(chips):
            for a in range(n):
                remote(a, j, me, 2 * px + py).wait_recv()
        for cp in sends:
            cp.wait_send()
        for cp in own:
            cp.wait()

    anyspec = pl.BlockSpec(memory_space=pl.ANY)
    return pl.pallas_call(
        body, name="chip_exchange", in_specs=[anyspec] * n, out_specs=[anyspec] * n,
        out_shape=[jax.ShapeDtypeStruct(a.shape, a.dtype) for a in arrs],
        scratch_shapes=[pltpu.SemaphoreType.DMA((3 * n,)), pltpu.SemaphoreType.DMA((3 * n,)),
                        pltpu.SemaphoreType.DMA((n,))],
    )(*arrs)


def _swap_halves(arrs):
    n = len(arrs)

    def body(*refs):
        srcs, dsts = refs[:n], refs[n:2 * n]
        send, recv, local = refs[2 * n:]
        x, y, c = _position()

        def remote(a, landing):
            return pltpu.make_async_remote_copy(
                src_ref=srcs[a], dst_ref=dsts[a].at[landing], send_sem=send.at[a], recv_sem=recv.at[a],
                device_id=(x, y, 1 - c), device_id_type=MESH)

        own = [pltpu.make_async_copy(srcs[a], dsts[a].at[c], local.at[a]) for a in range(n)]
        sends = [remote(a, c) for a in range(n)]
        for cp in sends + own:
            cp.start()
        for a in range(n):
            remote(a, 1 - c).wait_recv()
        for cp in sends:
            cp.wait_send()
        for cp in own:
            cp.wait()

    vm = pl.BlockSpec(memory_space=pltpu.VMEM)
    return pl.pallas_call(
        body, name="swap_halves", in_specs=[vm] * n, out_specs=[vm] * n,
        out_shape=[jax.ShapeDtypeStruct((2,) + a.shape, a.dtype) for a in arrs],
        scratch_shapes=[pltpu.SemaphoreType.DMA((n,)), pltpu.SemaphoreType.DMA((n,)), pltpu.SemaphoreType.DMA((n,))],
        compiler_params=pltpu.CompilerParams(vmem_limit_bytes=VMEM_LIMIT),
    )(*arrs)


def _row_block(r):
    return 128 if r % 128 == 0 else r


def _sum_slots(name, slots):
    s, r, n = slots.shape
    rb = _row_block(r)

    def body(s_ref, o_ref):
        total = s_ref[0].astype(f32)
        for d in range(1, s):
            total = total + s_ref[d].astype(f32)
        o_ref[...] = total

    return pl.pallas_call(
        body, name=name, grid=(r // rb,),
        in_specs=[pl.BlockSpec((s, rb, n), lambda i: (0, i, 0))],
        out_specs=pl.BlockSpec((rb, n), lambda i: (i, 0)),
        out_shape=jax.ShapeDtypeStruct((r, n), f32),
        compiler_params=_params(("parallel",), VMEM_LIMIT),
    )(slots)


def _adamw(name, w, g, m, v):
    r, n = w.shape
    if r % 128 == 0 or r * n <= 128 * 1024:
        rb, nb = _row_block(r), n
    else:
        rb, nb = r, LANES

    def body(w_ref, g_ref, m_ref, v_ref, d_ref, nm_ref, nv_ref):
        gv = g_ref[...]
        m2 = ADAM_B1 * m_ref[...] + (1.0 - ADAM_B1) * gv
        v2 = ADAM_B2 * v_ref[...] + (1.0 - ADAM_B2) * (gv * gv)
        m_hat = m2 / (1.0 - ADAM_B1 ** ADAM_STEP)
        v_hat = v2 / (1.0 - ADAM_B2 ** ADAM_STEP)
        d_ref[...] = (-ADAM_LR) * (m_hat / (jnp.sqrt(v_hat) + ADAM_EPS) + ADAM_WD * w_ref[...])
        nm_ref[...] = m2
        nv_ref[...] = v2

    spec = pl.BlockSpec((rb, nb), lambda i, j: (i, j))
    return pl.pallas_call(
        body, name=name, grid=(r // rb, n // nb), in_specs=[spec] * 4, out_specs=[spec] * 3,
        out_shape=[jax.ShapeDtypeStruct((r, n), f32)] * 3,
        compiler_params=_params(("parallel", "parallel"), VMEM_LIMIT),
    )(w, g, m, v)


def _identity(a):
    return a


def _local_step(x2, tgt2, seq, wt):
    nb = x2.shape[0] // seq
    h = _prenorm(x2, wt["pre_w"])
    qkv = _mm("in_qkv", [(h, 0)], _identity, wt["w_qkv"], wt["b_qkv"], bf16, 1024, 1024, w_is_nk=True)
    rest = _mm("in_rest", [(h, 0)], _identity, wt["w_rest"], wt["b_rest"], bf16, 1024, 1024, w_is_nk=True)
    f128 = _mm("in_f", [(h, 0)], _identity, wt["w_f"], wt["b_f"], f32, 1024, LANES, w_is_nk=True)
    c = _forget_prep(f128, seq)
    qa, ka = _attn_prep(qkv, c)
    o_att, pa, lse = _attn_fwd(qa, ka, qkv, rest, seq)
    ya = _mm("proj_a", [(pa, 0)], _identity, wt["w_a"], None, f32, 512, D)
    rnn_w = (wt["conv_w"], wt["conv_b"], wt["wa_d"], wt["wx_d"], wt["ba"], wt["bx"], wt["lam"])
    xc, a, hrec, pr = _rnn_fwd(rest, *rnn_w, seq)
    yr = _mm("proj_r", [(pr, 0)], _identity, wt["w_r"], None, f32, 512, D)
    o, mrg = _mm("proj_out", [(rest, 3), (rest, 4), (ya, 0), (yr, 0)], _merge, wt["w_o"], None, f32, TM, D,
                 keep_lhs=True)

    do, dy, loss8, d_post = _post_loss(o, x2, tgt2, wt["post_w"])
    dya, dyr, dmga, dmgr = _out_bwd(do, rest, ya, yr, wt["w_o"])
    doa, dga = _branch_bwd("branch_a_bwd", dya, rest, 0, o_att, wt["w_a"], bf16)
    dhrec, dgr = _branch_bwd("branch_r_bwd", dyr, rest, 2, hrec, wt["w_r"], f32)
    d_wo, _ = _tn_mm("dw_out", mrg, do, 512)
    d_wa, _ = _tn_mm("dw_branch_a", pa, dya, 512)
    d_wr, _ = _tn_mm("dw_branch_r", pr, dyr, 512)
    dxr, d_wad, d_wxd, vec = _rnn_bwd(dhrec, a, hrec, xc, rest, *rnn_w, seq)
    dq, dk, dv, dc_pairs = _attn_bwd(qa, ka, qkv, doa, lse, _attn_delta(doa, o_att), seq)
    dc = dc_pairs.reshape(-1, HEADS // ATT_GROUP, LANES)[:, :, :ATT_GROUP].reshape(-1, HEADS)
    df, db_f = _forget_bwd(_pad_cols(dc, LANES), f128, seq)
    pieces = [dq, dk, dv, dga, dxr, dgr, dmga, dmgr]
    gx, d_pre = _in_bwd(pieces, df, x2, dy, wt["w_qkv"], wt["w_rest"], wt["w_f"], wt["pre_w"])
    names = ["q", "k", "v", "ga", "xr", "gr", "mga", "mgr"]
    dws, dbs = [], []
    for nm, piece in zip(names, pieces):
        dw_p, db_p = _tn_mm("dw_in_" + nm, piece, h, 512)
        dws.append(dw_p)
        dbs.append(db_p)
    dw_f, _ = _tn_mm("dw_in_f", df, h, 512)
    zeros_w = jnp.zeros((IN_TOTAL - IN_USED, D), f32)
    d_w_in = jnp.concatenate(dws[:3] + [dw_f[:HEADS]] + dws[3:] + [zeros_w], axis=0)
    d_b_in = jnp.concatenate(dbs[:3] + [db_f[:, :HEADS]] + dbs[3:] + [zeros_w[:, :1].T], axis=1)
    return dict(loss=loss8[0, 0], grad_x=gx, pre_w=d_pre, w_in=d_w_in, b_in=d_b_in, conv_w=vec[4:8], conv_b=vec[3:4],
                wa_d=d_wad, ba=vec[0:1], wx_d=d_wxd, bx=vec[1:2], lam=vec[2:3], w_a=d_wa, w_r=d_wr, w_o=d_wo,
                post_w=d_post)


def _block_diag(w):
    g, bw, _ = w.shape
    eye = jnp.eye(g, dtype=w.dtype)
    return (w[:, :, None, :] * eye[:, None, :, None]).reshape(g * bw, g * bw)


def _gate_blocks(diag):
    half = diag.shape[1] // 2
    return jnp.stack([diag[:, :half, :half], diag[:, half:, half:]], axis=1).reshape(-1, half, half)


def _pad_cols(a, n):
    return jnp.pad(a, ((0, 0), (0, n - a.shape[1])))


def _pad_rows(a, n):
    return jnp.pad(a, ((0, n - a.shape[0]), (0, 0)))


def kernel(x, pre_norm_w, w_in, b_in, conv_w, conv_b, rg_wa, rg_ba, rg_wx, rg_bx, rg_lambda, w_branch_a, w_branch_r, w_out, post_norm_w, loss_target, m_pre_norm_w, m_w_in, m_b_in, m_conv_w, m_conv_b, m_rg_wa, m_rg_ba, m_rg_wx, m_rg_bx, m_rg_lambda, m_w_branch_a, m_w_branch_r, m_w_out, m_post_norm_w, v_pre_norm_w, v_w_in, v_b_in, v_conv_w, v_conv_b, v_rg_wa, v_rg_ba, v_rg_wx, v_rg_bx, v_rg_lambda, v_w_branch_a, v_w_branch_r, v_w_out, v_post_norm_w):
    nb, seq, _ = x.shape
    chip = 2 * lax.axis_index("x") + lax.axis_index("y")
    n_groups = rg_wa.shape[1]

    w_in_t = jnp.transpose(w_in[0])
    shard_cols = w_in_t.shape[0]
    padded = -(-shard_cols // 32) * 32
    g_in, g_a, g_r, g_o, g_cw = _gather_shards(
        [_pad_rows(w_in_t.astype(bf16), padded), w_branch_a[0].astype(bf16), w_branch_r[0].astype(bf16),
         w_out[0].astype(bf16)], conv_w[0])
    w_full = jnp.concatenate([g_in[j, :shard_cols] for j in range(N_CHIPS)], axis=0)
    q_end, f_end = 3 * D, 3 * D + HEADS
    wt = dict(
        pre_w=pre_norm_w, post_w=post_norm_w,
        w_qkv=w_full[:q_end], b_qkv=b_in[:, :q_end],
        w_f=_pad_rows(w_full[q_end:f_end], LANES), b_f=_pad_cols(b_in[:, q_end:f_end], LANES),
        w_rest=w_full[f_end:IN_USED], b_rest=b_in[:, f_end:IN_USED],
        w_a=g_a.reshape(D, D), w_r=g_r.reshape(D, D), w_o=g_o.reshape(D, D),
        conv_w=jnp.transpose(g_cw, (1, 0, 2)).reshape(4, D), conv_b=conv_b,
        wa_d=_block_diag(rg_wa[0]).astype(bf16), wx_d=_block_diag(rg_wx[0]).astype(bf16),
        ba=rg_ba, bx=rg_bx, lam=rg_lambda)

    part = _local_step(x.reshape(nb * seq, D), loss_target.reshape(nb * seq, D), seq, wt)
    loss = lax.psum(part["loss"], ("x", "y", "c"))
    grad_x = part["grad_x"].reshape(nb, seq, D)

    small = jnp.concatenate([
        part["pre_w"], _pad_cols(part["b_in"], 10 * D).reshape(10, D), part["conv_b"],
        _gate_blocks(part["wa_d"]).reshape(-1, D), part["ba"],
        _gate_blocks(part["wx_d"]).reshape(-1, D), part["bx"], part["lam"], part["post_w"],
        part["conv_w"]], axis=0)
    n_small = small.shape[0]
    n_rep = n_small - 4
    tot = _allsum_rows(_pad_rows(small, -(-n_small // 8) * 8))
    g_rep = tot[:n_rep]
    g_conv_w = lax.dynamic_slice_in_dim(tot[n_rep:n_small], chip * (D // N_CHIPS), D // N_CHIPS, axis=1)

    def pack(pre, b, cb, wa, ba, wx, bx, lam, post):
        return jnp.concatenate([pre, _pad_cols(b, 10 * D).reshape(10, D), cb, wa.reshape(-1, D), ba,
                                wx.reshape(-1, D), bx, lam, post], axis=0)

    def unpack(p):
        o = [0]

        def take(k):
            o[0] += k
            return p[o[0] - k:o[0]]

        pre = take(1)
        b = take(10).reshape(1, 10 * D)[:, :IN_TOTAL]
        cb = take(1)
        wa = take(64).reshape(rg_wa.shape)
        ba = take(1)
        wx = take(64).reshape(rg_wx.shape)
        bx = take(1)
        lam = take(1)
        post = take(1)
        return dict(pre_norm_w=pre, b_in=b, conv_b=cb, rg_wa=wa, rg_ba=ba, rg_wx=wx, rg_bx=bx, rg_lambda=lam,
                    post_norm_w=post)

    w_rep = pack(pre_norm_w, b_in, conv_b, rg_wa, rg_ba, rg_wx, rg_bx, rg_lambda, post_norm_w)
    m_rep = pack(m_pre_norm_w, m_b_in, m_conv_b, m_rg_wa, m_rg_ba, m_rg_wx, m_rg_bx, m_rg_lambda, m_post_norm_w)
    v_rep = pack(v_pre_norm_w, v_b_in, v_conv_b, v_rg_wa, v_rg_ba, v_rg_wx, v_rg_bx, v_rg_lambda, v_post_norm_w)
    d_rep, nm_rep, nv_rep = _adamw("adamw_rep", w_rep, g_rep, m_rep, v_rep)
    grads, deltas, new_m, new_v = unpack(g_rep), unpack(d_rep), unpack(nm_rep), unpack(nv_rep)

    p_in = jnp.pad(part["w_in"].reshape(N_CHIPS, shard_cols, D), ((0, 0), (0, padded - shard_cols), (0, 0)))
    p_in = p_in.reshape(N_DEV, padded // 2, D)
    p_aro = jnp.concatenate([part[k].reshape(N_DEV, D // N_DEV, D) for k in ("w_a", "w_r", "w_o")], axis=1)
    s_in, s_aro = _chip_exchange([_pair_reduce("pair_w_in", p_in.astype(bf16)),
                                  _pair_reduce("pair_w_aro", p_aro.astype(bf16))])
    f_in, f_aro = _swap_halves([_sum_slots("sum_w_in", s_in), _sum_slots("sum_w_aro", s_aro)])
    g_w_in_t = f_in.reshape(padded, D)[:shard_cols]
    rows = D // N_DEV
    g_aro = jnp.concatenate([f_aro[:, i * rows:(i + 1) * rows, :].reshape(2 * rows, D) for i in range(3)], axis=0)

    w_in_upd = _adamw("adamw_w_in", w_in_t, g_w_in_t, jnp.transpose(m_w_in[0]), jnp.transpose(v_w_in[0]))
    g_w_in, d_w_in, nm_w_in, nv_w_in = [jnp.transpose(a) for a in (g_w_in_t, *w_in_upd)]
    stack = lambda a, b, c: jnp.concatenate([a[0], b[0], c[0]], axis=0)
    d_aro, nm_aro, nv_aro = _adamw("adamw_w_aro", stack(w_branch_a, w_branch_r, w_out), g_aro,
                                   stack(m_w_branch_a, m_w_branch_r, m_w_out),
                                   stack(v_w_branch_a, v_w_branch_r, v_w_out))
    d_cw, nm_cw, nv_cw = _adamw("adamw_conv_w", conv_w[0], g_conv_w, m_conv_w[0], v_conv_w[0])

    def sharded(t_in, t_aro, t_cw):
        r2 = 2 * rows
        return dict(w_in=t_in[None], conv_w=t_cw[None], w_branch_a=t_aro[None, :r2], w_branch_r=t_aro[None, r2:2 * r2],
                    w_out=t_aro[None, 2 * r2:])

    order = ["pre_norm_w", "w_in", "b_in", "conv_w", "conv_b", "rg_wa", "rg_ba", "rg_wx", "rg_bx", "rg_lambda",
             "w_branch_a", "w_branch_r", "w_out", "post_norm_w"]
    outs = [loss, grad_x]
    for rep, shd in ((grads, sharded(g_w_in, g_aro, g_conv_w)), (deltas, sharded(d_w_in, d_aro, d_cw)),
                     (new_m, sharded(nm_w_in, nm_aro, nm_cw)), (new_v, sharded(nv_w_in, nv_aro, nv_cw))):
        both = {**rep, **shd}
        outs.extend(both[k] for k in order)
    return tuple(outs)
```

```python
import jax
import jax.numpy as jnp
from jax import lax
from jax.experimental import pallas as pl
from jax.experimental.pallas import tpu as pltpu

f32 = jnp.float32
bf16 = jnp.bfloat16

D = 1024
HEADS = 16
HEAD_PAIRS = 8
LANES = 128
NORM_EPS = 1e-6
MASK_VALUE = -1e30
RG_C = 8.0
QK_SCALE = 0.125
TQ = 256
ATT_GROUP = 8
ATT_GROUP_FWD = 16
TL = 256
TM = 256
PREV_ROWS = 16
IN_USED = 8 * D + HEADS
IN_TOTAL = 9 * D + HEADS
N_CHIPS = 4
N_DEV = 8
ADAM_LR, ADAM_B1, ADAM_B2, ADAM_EPS, ADAM_WD, ADAM_STEP = 0.001, 0.9, 0.999, 1e-08, 0.01, 10
VMEM_LIMIT = 56 * 1024 * 1024
MESH = pl.DeviceIdType.MESH


def _dot(a, b):
    return jnp.dot(a, b, preferred_element_type=f32)


def _dot_nt(a, b):
    return lax.dot_general(a, b, (((1,), (1,)), ((), ())), preferred_element_type=f32)


def _dot_tn(a, b):
    return lax.dot_general(a, b, (((0,), (0,)), ((), ())), preferred_element_type=f32)


def _sig(x):
    return 0.5 * jnp.tanh(0.5 * x) + 0.5


def _softplus(x):
    return jnp.maximum(x, 0.0) + jnp.log(1.0 + jnp.exp(-jnp.abs(x)))


def _params(sem, vmem=None):
    return pltpu.CompilerParams(dimension_semantics=sem, vmem_limit_bytes=vmem)


def _tile(tm, width, cb=0):
    return pl.BlockSpec((tm, width), lambda i, cb=cb: (i, cb))


def _whole(shape):
    nd = len(shape)
    return pl.BlockSpec(shape, lambda *_: (0,) * nd)


def _prenorm(x, w_pre):
    t = x.shape[0]

    def body(x_ref, w_ref, h_ref):
        xv = x_ref[...]
        r = lax.rsqrt(jnp.mean(xv * xv, axis=-1, keepdims=True) + NORM_EPS)
        h_ref[...] = (xv * r * w_ref[...]).astype(bf16)

    return pl.pallas_call(
        body, name="prenorm", grid=(t // TM,),
        in_specs=[_tile(TM, D), _whole((1, D))], out_specs=_tile(TM, D),
        out_shape=jax.ShapeDtypeStruct((t, D), bf16),
        compiler_params=_params(("parallel",)),
    )(x, w_pre)


def _mm(name, ins, prologue, w, bias, out_dtype, tm, tn, keep_lhs=False, w_is_nk=False):
    t = ins[0][0].shape[0]
    tm = min(tm, t)
    n, k = w.shape if w_is_nk else w.shape[::-1]
    n_in = len(ins)
    assert not keep_lhs or tn == n

    def body(*refs):
        a = prologue(*[r[...] for r in refs[:n_in]])
        acc = _dot_nt(a, refs[n_in][...]) if w_is_nk else _dot(a, refs[n_in][...])
        if bias is not None:
            acc = acc + refs[n_in + 1][...]
        if keep_lhs:
            refs[-1][...] = a
            refs[-2][...] = acc.astype(out_dtype)
        else:
            refs[-1][...] = acc.astype(out_dtype)

    in_specs = [pl.BlockSpec((tm, k), lambda i, j, cb=cb: (i, cb)) for _, cb in ins]
    in_specs.append(pl.BlockSpec((tn, k), lambda i, j: (j, 0)) if w_is_nk else pl.BlockSpec((k, tn), lambda i, j: (0, j)))
    args = [a for a, _ in ins] + [w]
    if bias is not None:
        in_specs.append(pl.BlockSpec((1, tn), lambda i, j: (0, j)))
        args.append(bias)
    out_specs = pl.BlockSpec((tm, tn), lambda i, j: (i, j))
    out_shape = jax.ShapeDtypeStruct((t, n), out_dtype)
    if keep_lhs:
        out_specs = [out_specs, pl.BlockSpec((tm, k), lambda i, j: (i, 0))]
        out_shape = [out_shape, jax.ShapeDtypeStruct((t, k), bf16)]
    return pl.pallas_call(
        body, name=name, grid=(t // tm, n // tn), in_specs=in_specs, out_specs=out_specs, out_shape=out_shape,
        compiler_params=_params(("parallel", "parallel"), VMEM_LIMIT),
    )(*args)


def _forget_prep(f128, seq):
    t = f128.shape[0]
    nb = seq // LANES

    def body(f_ref, c_ref):
        r = lax.broadcasted_iota(jnp.int32, (LANES, LANES), 0)
        cidx = lax.broadcasted_iota(jnp.int32, (LANES, LANES), 1)
        tri = (r >= cidx).astype(f32)
        carry = jnp.zeros((1, LANES), f32)
        for blk in range(nb):
            fv = f_ref[pl.ds(blk * LANES, LANES), :]
            lf = -_softplus(-fv)
            c_ref[pl.ds(blk * LANES, LANES), :] = (
                jnp.dot(tri, lf, preferred_element_type=f32, precision=lax.Precision.HIGHEST) + carry)
            carry = carry + jnp.sum(lf, axis=0, keepdims=True)

    return pl.pallas_call(
        body, name="forget_prep", grid=(t // seq,),
        in_specs=[pl.BlockSpec((seq, LANES), lambda b: (b, 0))],
        out_specs=pl.BlockSpec((seq, LANES), lambda b: (b, 0)),
        out_shape=jax.ShapeDtypeStruct((t, LANES), f32),
        compiler_params=_params(("parallel",)),
    )(f128)


def _split3(cv):
    hi = cv.astype(bf16)
    r1 = cv - hi.astype(f32)
    mid = r1.astype(bf16)
    lo = (r1 - mid.astype(f32)).astype(bf16)
    return hi, mid, lo


def _attn_prep(qkv, c):
    t = qkv.shape[0]

    def body(q_ref, k_ref, c_ref, qa_ref, ka_ref):
        lane = lax.broadcasted_iota(jnp.int32, (1, LANES), 1)
        cv = c_ref[...]
        one = jnp.ones((), bf16)
        zero = jnp.zeros((), bf16)
        q_ones = jnp.where((lane >= 67) & (lane < 70), one, zero)
        k_ones = jnp.where((lane >= 64) & (lane < 67), one, zero)
        for head in range(HEADS):
            pair = pl.ds((head // 2) * LANES, LANES)
            ch = jnp.sum(jnp.where(lane == head, cv, 0.0), axis=1, keepdims=True)
            hi, mid, lo = _split3(ch)
            q2, k2 = q_ref[:, pair], k_ref[:, pair]
            if head % 2 == 1:
                q2, k2 = pltpu.roll(q2, 64, 1), pltpu.roll(k2, 64, 1)
            qa = jnp.where(lane < 64, q2 * jnp.asarray(QK_SCALE, bf16),
                           jnp.where(lane == 64, hi, jnp.where(lane == 65, mid, jnp.where(lane == 66, lo, q_ones))))
            ka = jnp.where(lane < 64, k2,
                           jnp.where(lane == 67, -hi, jnp.where(lane == 68, -mid, jnp.where(lane == 69, -lo, k_ones))))
            qa_ref[:, pl.ds(head * LANES, LANES)] = qa
            ka_ref[:, pl.ds(head * LANES, LANES)] = ka

    tm = min(TM, t)
    out = pl.BlockSpec((tm, 2 * D), lambda i: (i, 0))
    return pl.pallas_call(
        body, name="attn_prep", grid=(t // tm,),
        in_specs=[_tile(tm, D, 0), _tile(tm, D, 1), _tile(tm, LANES)],
        out_specs=[out, out],
        out_shape=[jax.ShapeDtypeStruct((t, 2 * D), bf16)] * 2,
        compiler_params=_params(("parallel",)),
    )(qkv, qkv, c)


def _attn_fwd(qa, ka, qkv, rest, seq):
    t = qkv.shape[0]
    nb, nq = t // seq, seq // TQ

    hg = ATT_GROUP_FWD
    ng = HEADS // hg

    def body(q_ref, k_ref, v_ref, ga_ref, o_ref, pa_ref, lse_ref, acc_scr):
        qi, gi = pl.program_id(1), pl.program_id(2)
        krow = lax.broadcasted_iota(jnp.int32, (TQ, TQ), 0)
        qcol = lax.broadcasted_iota(jnp.int32, (TQ, TQ), 1)
        acc_scr[...] = jnp.zeros_like(acc_scr)

        def kv_step(kt, carry, masked):
            ks = pl.multiple_of(kt * TQ, TQ)
            sts = [_dot_nt(k_ref[pl.ds(ks, TQ), pl.ds(g * LANES, LANES)], q_ref[:, pl.ds(g * LANES, LANES)])
                   for g in range(hg)]
            if masked:
                sts = [jnp.where(krow <= qcol, st, MASK_VALUE) for st in sts]
            m_new = [jnp.maximum(carry[g][0], jnp.max(sts[g], axis=0, keepdims=True)) for g in range(hg)]
            ps = [jnp.exp(sts[g] - m_new[g]) for g in range(hg)]
            alphas = [jnp.exp(carry[g][0] - m_new[g]) for g in range(hg)]
            phi = [ps[g].astype(bf16) for g in range(hg)]
            plo = [(ps[g] - phi[g].astype(f32)).astype(bf16) for g in range(hg)]
            vs = [v_ref[pl.ds(ks, TQ), pl.ds(j * LANES, LANES)] for j in range(hg // 2)]
            pvs = [_dot_tn(vs[g // 2], phi[g]) + _dot_tn(vs[g // 2], plo[g]) for g in range(hg)]
            olds = [acc_scr[g] for g in range(hg)]
            for g in range(hg):
                acc_scr[g] = alphas[g] * olds[g] + pvs[g]
            return tuple((m_new[g], alphas[g] * carry[g][1] + jnp.sum(ps[g], axis=0, keepdims=True))
                         for g in range(hg))

        init = tuple((jnp.full((1, TQ), MASK_VALUE, f32), jnp.zeros((1, TQ), f32)) for _ in range(hg))
        carry = lax.fori_loop(0, qi, lambda kt, cr: kv_step(kt, cr, False), init)
        stats = kv_step(qi, carry, True)
        drow = lax.broadcasted_iota(jnp.int32, (LANES, TQ), 0)
        for g in range(hg):
            m, l = stats[g]
            lse_ref[0, pl.ds(hg * gi + g, 1), :] = m + jnp.log(l)
        for j in range(hg // 2):
            o2 = jnp.where(drow < 64, acc_scr[2 * j] / stats[2 * j][1], acc_scr[2 * j + 1] / stats[2 * j + 1][1]).T
            o_ref[:, pl.ds(j * LANES, LANES)] = o2
            ga = ga_ref[:, pl.ds(j * LANES, LANES)].astype(f32)
            pa_ref[:, pl.ds(j * LANES, LANES)] = (o2 * (ga * _sig(ga))).astype(bf16)

    vw = hg * 64
    tile = pl.BlockSpec((TQ, vw), lambda b, qi, gi: (b * nq + qi, gi))
    return pl.pallas_call(
        body, name="attn_fwd", grid=(nb, nq, ng),
        in_specs=[pl.BlockSpec((TQ, hg * LANES), lambda b, qi, gi: (b * nq + qi, gi)),
                  pl.BlockSpec((seq, hg * LANES), lambda b, qi, gi: (b, gi)),
                  pl.BlockSpec((seq, vw), lambda b, qi, gi: (b, 2 * ng + gi)), tile],
        out_specs=[tile, tile, pl.BlockSpec((1, HEADS, TQ), lambda b, qi, gi: (b * nq + qi, 0, 0))],
        out_shape=[jax.ShapeDtypeStruct((t, D), f32), jax.ShapeDtypeStruct((t, D), bf16),
                   jax.ShapeDtypeStruct((t // TQ, HEADS, TQ), f32)],
        scratch_shapes=[pltpu.VMEM((hg, LANES, TQ), f32)],
        compiler_params=_params(("parallel", "parallel", "arbitrary"), VMEM_LIMIT),
    )(qa, ka, qkv, rest)


def _shifted_rows(x, top8, prev8, shift, row, row8):
    body = pltpu.roll(x, shift, 0)
    head = jnp.where(row8 < shift, pltpu.roll(prev8, shift, 0), pltpu.roll(top8, shift, 0))
    return body, head


def _rnn_gates(xc, wa_ref, wx_ref, ba_ref, bx_ref, lam_ref):
    xcb = xc.astype(bf16)
    r = _sig(_dot(xcb, wa_ref[...]) + ba_ref[...])
    i = _sig(_dot(xcb, wx_ref[...]) + bx_ref[...])
    sp = _softplus(-lam_ref[...])
    log_a = (-RG_C) * r * sp
    th = jnp.tanh(log_a)
    w1 = (-2.0) * th / (1.0 - th)
    sq = jnp.sqrt(jnp.maximum(w1, 0.0))
    return r, i, sp, log_a, w1, sq


def _conv_tile(x_ref, xprev_ref, has_prev, cw_ref, cb_ref, xc_ref):
    row = lax.broadcasted_iota(jnp.int32, (TL, D), 0)
    row8 = lax.broadcasted_iota(jnp.int32, (8, D), 0)
    x = x_ref[...].astype(f32)
    top8 = x[:8]
    prev8 = jnp.where(has_prev, xprev_ref[...].astype(f32)[PREV_ROWS - 8:], 0.0)
    xc = cb_ref[...] + cw_ref[pl.ds(3, 1), :] * x
    xc8 = cb_ref[...] + cw_ref[pl.ds(3, 1), :] * top8
    for sh in range(1, 4):
        w = cw_ref[pl.ds(3 - sh, 1), :]
        xs, xs8 = _shifted_rows(x, top8, prev8, sh, row, row8)
        xc = xc + w * xs
        xc8 = xc8 + w * xs8
    xc_ref[...] = xc
    xc_ref[pl.ds(0, 8), :] = xc8


def _rnn_fwd(rest, conv_w, conv_b, wa_d, wx_d, ba, bx, lam, seq):
    t = rest.shape[0]
    nb, nt = t // seq, seq // TL

    def body(x_ref, xprev_ref, gr_ref, cw_ref, cb_ref, wa_ref, wx_ref, ba_ref, bx_ref, lam_ref,
             xc_ref, a_ref, h_ref, pr_ref, u_scr, carry):
        tt = pl.program_id(1)
        _conv_tile(x_ref, xprev_ref, tt > 0, cw_ref, cb_ref, xc_ref)
        xc = xc_ref[...]
        r, i, sp, log_a, w1, sq = _rnn_gates(xc, wa_ref, wx_ref, ba_ref, bx_ref, lam_ref)
        a_ref[...] = jnp.exp(log_a)
        u_scr[...] = sq * (i * xc)

        @pl.when(tt == 0)
        def _():
            carry[...] = jnp.zeros_like(carry)

        def step(s, h):
            h = a_ref[pl.ds(s, 1), :] * h + u_scr[pl.ds(s, 1), :]
            h_ref[pl.ds(s, 1), :] = h
            return h

        carry[...] = lax.fori_loop(0, TL, step, carry[...], unroll=8)
        gr = gr_ref[...].astype(f32)
        pr_ref[...] = (h_ref[...] * (gr * _sig(gr))).astype(bf16)

    tile = lambda cb: pl.BlockSpec((TL, D), lambda b, tt, cb=cb: (b * nt + tt, cb))
    prev = lambda cb: pl.BlockSpec(
        (PREV_ROWS, D), lambda b, tt, cb=cb: (jnp.maximum((b * nt + tt) * (TL // PREV_ROWS) - 1, 0), cb))
    vec = _whole((1, D))
    return pl.pallas_call(
        body, name="rnn_fwd", grid=(nb, nt),
        in_specs=[tile(1), prev(1), tile(2), _whole((4, D)), vec, _whole((D, D)), _whole((D, D)), vec, vec, vec],
        out_specs=[tile(0)] * 4,
        out_shape=[jax.ShapeDtypeStruct((t, D), f32)] * 3 + [jax.ShapeDtypeStruct((t, D), bf16)],
        scratch_shapes=[pltpu.VMEM((TL, D), f32), pltpu.VMEM((1, D), f32)],
        compiler_params=_params(("parallel", "arbitrary"), VMEM_LIMIT),
    )(rest, rest, rest, conv_w, conv_b, wa_d, wx_d, ba, bx, lam)


def _merge(mga, mgr, ya, yr):
    return (_sig(mga.astype(f32)) * ya.astype(f32) + _sig(mgr.astype(f32)) * yr.astype(f32)).astype(bf16)


def _post_loss(o, x, tgt, w_post):
    t = o.shape[0]

    def body(o_ref, x_ref, t_ref, w_ref, do_ref, dy_ref, loss_ref, dwp_ref):
        @pl.when(pl.program_id(0) == 0)
        def _():
            loss_ref[...] = jnp.zeros_like(loss_ref)
            dwp_ref[...] = jnp.zeros_like(dwp_ref)

        ov = o_ref[...]
        w = w_ref[...]
        r2 = lax.rsqrt(jnp.mean(ov * ov, axis=-1, keepdims=True) + NORM_EPS)
        oh = ov * r2
        e = x_ref[...] + oh * w - t_ref[...]
        loss_ref[...] += 0.5 * jnp.sum(jnp.mean(e * e, axis=-1, keepdims=True))
        dy = e * (1.0 / D)
        dy_ref[...] = dy
        dwp_ref[...] += jnp.sum(dy * oh, axis=0, keepdims=True)
        doh = dy * w
        do_ref[...] = (r2 * (doh - oh * jnp.mean(doh * oh, axis=-1, keepdims=True))).astype(bf16)

    return pl.pallas_call(
        body, name="post_loss", grid=(t // TM,),
        in_specs=[_tile(TM, D)] * 3 + [_whole((1, D))],
        out_specs=[_tile(TM, D), _tile(TM, D), _whole((8, LANES)), _whole((1, D))],
        out_shape=[jax.ShapeDtypeStruct((t, D), bf16), jax.ShapeDtypeStruct((t, D), f32),
                   jax.ShapeDtypeStruct((8, LANES), f32), jax.ShapeDtypeStruct((1, D), f32)],
        compiler_params=_params(("arbitrary",)),
    )(o, x, tgt, w_post)


def _out_bwd(do, rest, ya, yr, w_out):
    t = do.shape[0]

    def body(do_ref, mga_ref, mgr_ref, ya_ref, yr_ref, w_ref, dya_ref, dyr_ref, dmga_ref, dmgr_ref):
        sa, sr = _sig(mga_ref[...].astype(f32)), _sig(mgr_ref[...].astype(f32))
        ya, yr = ya_ref[...].astype(f32), yr_ref[...].astype(f32)
        dm = _dot_nt(do_ref[...], w_ref[...])
        dya_ref[...] = (dm * sa).astype(bf16)
        dyr_ref[...] = (dm * sr).astype(bf16)
        dmga_ref[...] = (dm * ya * sa * (1.0 - sa)).astype(bf16)
        dmgr_ref[...] = (dm * yr * sr * (1.0 - sr)).astype(bf16)

    return pl.pallas_call(
        body, name="out_bwd", grid=(t // TM,),
        in_specs=[_tile(TM, D), _tile(TM, D, 3), _tile(TM, D, 4), _tile(TM, D), _tile(TM, D), _whole((D, D))],
        out_specs=[_tile(TM, D)] * 4,
        out_shape=[jax.ShapeDtypeStruct((t, D), bf16)] * 4,
        compiler_params=_params(("parallel",), VMEM_LIMIT),
    )(do, rest, rest, ya, yr, w_out)


def _branch_bwd(name, dyb, rest, gate_cb, act, w, act_grad_dtype):
    t = dyb.shape[0]

    def body(dy_ref, g_ref, act_ref, w_ref, dact_ref, dg_ref):
        dp = _dot_nt(dy_ref[...], w_ref[...])
        g = g_ref[...].astype(f32)
        sg = _sig(g)
        dact_ref[...] = (dp * (g * sg)).astype(act_grad_dtype)
        dg_ref[...] = (dp * act_ref[...] * (sg * (1.0 + g * (1.0 - sg)))).astype(bf16)

    return pl.pallas_call(
        body, name=name, grid=(t // TM,),
        in_specs=[_tile(TM, D), _tile(TM, D, gate_cb), _tile(TM, D), _whole((D, D))],
        out_specs=[_tile(TM, D), _tile(TM, D)],
        out_shape=[jax.ShapeDtypeStruct((t, D), act_grad_dtype), jax.ShapeDtypeStruct((t, D), bf16)],
        compiler_params=_params(("parallel",), VMEM_LIMIT),
    )(dyb, rest, act, w)


def _rnn_bwd(dh, a, h, xc, rest, conv_w, conv_b, wa_d, wx_d, ba, bx, lam, seq):
    t = dh.shape[0]
    nb, nt = t // seq, seq // TL
    diag = (D // LANES, LANES, LANES)

    def body(dh_ref, a_ref, h_ref, hprev_ref, xc_ref, x_ref, xprev_ref, cw_ref, cb_ref, wa_ref, wx_ref,
             ba_ref, bx_ref, lam_ref, dxr_ref, dwa_ref, dwx_ref, vec_ref, g_scr, dxc_scr, dxr_scr, qcarry, dxc_next):
        b, tt = pl.program_id(0), pl.program_id(1)
        rt = nt - 1 - tt

        @pl.when((b == 0) & (tt == 0))
        def _():
            dwa_ref[...] = jnp.zeros_like(dwa_ref)
            dwx_ref[...] = jnp.zeros_like(dwx_ref)
            vec_ref[...] = jnp.zeros_like(vec_ref)

        @pl.when(tt == 0)
        def _():
            qcarry[...] = jnp.zeros_like(qcarry)
            dxc_next[...] = jnp.zeros_like(dxc_next)

        def step(k, q):
            s = TL - 1 - k
            g = dh_ref[pl.ds(s, 1), :] + q
            g_scr[pl.ds(s, 1), :] = g
            return a_ref[pl.ds(s, 1), :] * g

        qcarry[...] = lax.fori_loop(0, TL, step, qcarry[...], unroll=8)

        row = lax.broadcasted_iota(jnp.int32, (TL, D), 0)
        row8 = lax.broadcasted_iota(jnp.int32, (8, D), 0)
        g = g_scr[...]
        av = a_ref[...]
        xc = xc_ref[...]
        hlast = jnp.where(rt > 0, hprev_ref[pl.ds(PREV_ROWS - 1, 1), :], 0.0)
        hp = jnp.where(row == 0, hlast, pltpu.roll(h_ref[...], 1, 0))
        r, i, sp, log_a, w1, sq = _rnn_gates(xc, wa_ref, wx_ref, ba_ref, bx_ref, lam_ref)
        dix = g * sq
        di = dix * xc
        dxc = dix * i
        dsq = g * (i * xc)
        dlog_a = g * hp * av - dsq * jnp.where(sq > 0.0, (1.0 - w1) / sq, 0.0)
        dpr = (dlog_a * ((-RG_C) * sp)) * r * (1.0 - r)
        dpi = di * i * (1.0 - i)
        dprb, dpib, xcb = dpr.astype(bf16), dpi.astype(bf16), xc.astype(bf16)
        dxc = dxc + _dot_nt(dprb, wa_ref[...]) + _dot_nt(dpib, wx_ref[...])
        for j in range(D // LANES):
            cols = slice(j * LANES, (j + 1) * LANES)
            dwa_ref[j] += _dot_tn(xcb[:, cols], dprb[:, cols])
            dwx_ref[j] += _dot_tn(xcb[:, cols], dpib[:, cols])
        vec_ref[pl.ds(0, 1), :] += jnp.sum(dpr, axis=0, keepdims=True)
        vec_ref[pl.ds(1, 1), :] += jnp.sum(dpi, axis=0, keepdims=True)
        dsp = jnp.sum(dlog_a * ((-RG_C) * r), axis=0, keepdims=True)
        vec_ref[pl.ds(2, 1), :] += dsp * (-_sig(-lam_ref[...]))
        vec_ref[pl.ds(3, 1), :] += jnp.sum(dxc, axis=0, keepdims=True)

        dxc_scr[...] = dxc
        bot8 = dxc_scr[pl.ds(TL - 8, 8), :]
        nxt8 = dxc_next[...]
        dxr = cw_ref[pl.ds(3, 1), :] * dxc
        dxr8 = cw_ref[pl.ds(3, 1), :] * bot8
        for sh in range(1, 4):
            w = cw_ref[pl.ds(3 - sh, 1), :]
            dxr = dxr + w * pltpu.roll(dxc, TL - sh, 0)
            dxr8 = dxr8 + w * jnp.where(row8 < 8 - sh, pltpu.roll(bot8, 8 - sh, 0), pltpu.roll(nxt8, 8 - sh, 0))
        dxr_scr[...] = dxr
        dxr_scr[pl.ds(TL - 8, 8), :] = dxr8
        dxr_ref[...] = dxr_scr[...].astype(bf16)
        dxc_next[...] = dxc_scr[pl.ds(0, 8), :]

        x = x_ref[...].astype(f32)
        prev8 = jnp.where(rt > 0, xprev_ref[...].astype(f32)[PREV_ROWS - 8:], 0.0)
        dxc_top8 = dxc_scr[pl.ds(0, 8), :]
        vec_ref[pl.ds(7, 1), :] += jnp.sum(dxc * x, axis=0, keepdims=True)
        for sh in range(1, 4):
            inside = jnp.sum(dxc * jnp.where(row >= sh, pltpu.roll(x, sh, 0), 0.0), axis=0, keepdims=True)
            above = jnp.sum(dxc_top8 * jnp.where(row8 < sh, pltpu.roll(prev8, sh, 0), 0.0), axis=0, keepdims=True)
            vec_ref[pl.ds(7 - sh, 1), :] += inside + above

    tile = lambda cb: pl.BlockSpec((TL, D), lambda b, tt, cb=cb: (b * nt + nt - 1 - tt, cb))
    prev = lambda cb: pl.BlockSpec(
        (PREV_ROWS, D), lambda b, tt, cb=cb: (jnp.maximum((b * nt + nt - 1 - tt) * (TL // PREV_ROWS) - 1, 0), cb))
    vec = _whole((1, D))
    return pl.pallas_call(
        body, name="rnn_bwd", grid=(nb, nt),
        in_specs=[tile(0), tile(0), tile(0), prev(0), tile(0), tile(1), prev(1),
                  _whole((4, D)), vec, _whole((D, D)), _whole((D, D)), vec, vec, vec],
        out_specs=[tile(0), _whole(diag), _whole(diag), _whole((8, D))],
        out_shape=[jax.ShapeDtypeStruct((t, D), bf16), jax.ShapeDtypeStruct(diag, f32),
                   jax.ShapeDtypeStruct(diag, f32), jax.ShapeDtypeStruct((8, D), f32)],
        scratch_shapes=[pltpu.VMEM((TL, D), f32), pltpu.VMEM((TL, D), f32), pltpu.VMEM((TL, D), f32),
                        pltpu.VMEM((1, D), f32), pltpu.VMEM((8, D), f32)],
        compiler_params=_params(("arbitrary", "arbitrary"), VMEM_LIMIT),
    )(dh, a, h, h, xc, rest, rest, conv_w, conv_b, wa_d, wx_d, ba, bx, lam)


def _attn_delta(doa, o):
    t = doa.shape[0]

    def body(do_ref, o_ref, d_ref):
        prod = do_ref[...].astype(f32) * o_ref[...]
        ch = lax.broadcasted_iota(jnp.int32, (D, LANES), 0)
        hd = lax.broadcasted_iota(jnp.int32, (D, LANES), 1)
        pick = (ch // 64 == hd).astype(bf16)
        per_head = sum(_dot(piece, pick) for piece in _split3(prod))
        d_ref[0] = per_head.T[:HEADS, :]

    return pl.pallas_call(
        body, name="attn_delta", grid=(t // TQ,),
        in_specs=[_tile(TQ, D), _tile(TQ, D)],
        out_specs=pl.BlockSpec((1, HEADS, TQ), lambda i: (i, 0, 0)),
        out_shape=jax.ShapeDtypeStruct((t // TQ, HEADS, TQ), f32),
        compiler_params=_params(("parallel",)),
    )(doa, o)


def _attn_bwd(qa, ka, qkv, doa, lse, delta, seq):
    t = qkv.shape[0]
    nb, nq = t // seq, seq // TQ
    hg = ATT_GROUP
    ng, npair = HEADS // hg, hg // 2

    def body(qa_ref, ka_ref, q_ref, k_ref, v_ref, do_ref, lse_ref, dl_ref, dq_ref, dk_ref, dv_ref, dc_ref,
             dqt_scr, dk_scr, dv_scr, ds_scr, kht_scr):
        gi, kt = pl.program_id(1), pl.program_id(2)
        lane = lax.broadcasted_iota(jnp.int32, (1, LANES), 1)
        krow = lax.broadcasted_iota(jnp.int32, (TQ, TQ), 0)
        qcol = lax.broadcasted_iota(jnp.int32, (TQ, TQ), 1)
        lmask = [(lane // 64) == hh for hh in range(2)]
        scale = jnp.asarray(QK_SCALE, bf16)

        @pl.when(kt == 0)
        def _():
            dqt_scr[...] = jnp.zeros_like(dqt_scr)

        dk_scr[...] = jnp.zeros_like(dk_scr)
        dv_scr[...] = jnp.zeros_like(dv_scr)
        ds_scr[...] = jnp.zeros_like(ds_scr)
        for g in range(hg):
            k2 = k_ref[:, pl.ds((g // 2) * LANES, LANES)]
            kht_scr[g] = jnp.where(lmask[g % 2], k2, jnp.zeros_like(k2)).T

        def q_step(qt, masked):
            qs = pl.multiple_of(qt * TQ, TQ)
            heads = range(hg)
            do2 = [do_ref[pl.ds(qs, TQ), pl.ds(j * LANES, LANES)] for j in range(npair)]
            q2 = [q_ref[pl.ds(qs, TQ), pl.ds(j * LANES, LANES)] for j in range(npair)]
            doh = [jnp.where(lmask[g % 2], do2[g // 2], jnp.zeros_like(do2[0])) for g in heads]
            qh = [jnp.where(lmask[g % 2], q2[g // 2], jnp.zeros_like(q2[0])) * scale for g in heads]
            st = [_dot_nt(ka_ref[:, pl.ds(g * LANES, LANES)], qa_ref[pl.ds(qs, TQ), pl.ds(g * LANES, LANES)])
                  for g in heads]
            if masked:
                st = [jnp.where(krow <= qcol, s, MASK_VALUE) for s in st]
            dp = [_dot_nt(v_ref[:, pl.ds((g // 2) * LANES, LANES)], doh[g]) for g in heads]
            p = [jnp.exp(st[g] - lse_ref[qt, pl.ds(hg * gi + g, 1), :]) for g in heads]
            ds = [p[g] * (dp[g] - dl_ref[qt, pl.ds(hg * gi + g, 1), :]) for g in heads]
            pb = [x.astype(bf16) for x in p]
            dsb = [x.astype(bf16) for x in ds]
            for j in range(npair):
                a, b = 2 * j, 2 * j + 1
                dv_scr[j] += _dot(pb[a], doh[a]) + _dot(pb[b], doh[b])
                dk_scr[j] += _dot(dsb[a], qh[a]) + _dot(dsb[b], qh[b])
                dqt_scr[qt, j] += (_dot(kht_scr[a], dsb[a]) + _dot(kht_scr[b], dsb[b])) * QK_SCALE
            for g in heads:
                ds_scr[g] += ds[g][:, :LANES] + ds[g][:, LANES:]

        q_step(kt, True)

        def loop_body(qt, carry):
            q_step(qt, False)
            return carry

        lax.fori_loop(kt + 1, nq, loop_body, 0)

        dc = jnp.zeros((TQ, LANES), f32)
        for g in range(hg):
            dc = jnp.where(lane == g, -jnp.sum(ds_scr[g], axis=1, keepdims=True), dc)
        dc_ref[...] = dc
        for j in range(npair):
            dk_ref[:, pl.ds(j * LANES, LANES)] = dk_scr[j].astype(bf16)
            dv_ref[:, pl.ds(j * LANES, LANES)] = dv_scr[j].astype(bf16)

        @pl.when(kt == nq - 1)
        def _():
            for qt in range(nq):
                for j in range(npair):
                    dq_ref[pl.ds(qt * TQ, TQ), pl.ds(j * LANES, LANES)] = dqt_scr[qt, j].T.astype(bf16)

    vw = hg * 64
    seqspec = pl.BlockSpec((seq, vw), lambda b, gi, kt: (b, gi))
    kspec = lambda off: pl.BlockSpec((TQ, vw), lambda b, gi, kt: (b * nq + kt, off + gi))
    rowspec = pl.BlockSpec((nq, HEADS, TQ), lambda b, gi, kt: (b, 0, 0))
    return pl.pallas_call(
        body, name="attn_bwd", grid=(nb, ng, nq),
        in_specs=[pl.BlockSpec((seq, hg * LANES), lambda b, gi, kt: (b, gi)),
                  pl.BlockSpec((TQ, hg * LANES), lambda b, gi, kt: (b * nq + kt, gi)),
                  seqspec, kspec(ng), kspec(2 * ng), seqspec, rowspec, rowspec],
        out_specs=[seqspec, kspec(0), kspec(0), pl.BlockSpec((TQ, LANES), lambda b, gi, kt: (b * nq + kt, gi))],
        out_shape=[jax.ShapeDtypeStruct((t, D), bf16)] * 3 + [jax.ShapeDtypeStruct((t, ng * LANES), f32)],
        scratch_shapes=[pltpu.VMEM((nq, npair, LANES, TQ), f32), pltpu.VMEM((npair, TQ, LANES), f32),
                        pltpu.VMEM((npair, TQ, LANES), f32), pltpu.VMEM((hg, TQ, LANES), f32),
                        pltpu.VMEM((hg, LANES, TQ), bf16)],
        compiler_params=_params(("parallel", "parallel", "arbitrary"), VMEM_LIMIT),
    )(qa, ka, qkv, qkv, qkv, doa, lse, delta)


def _forget_bwd(dc, f128, seq):
    t = f128.shape[0]
    nb = seq // LANES

    def body(dc_ref, f_ref, df_ref, dbf_ref):
        @pl.when(pl.program_id(0) == 0)
        def _():
            dbf_ref[...] = jnp.zeros_like(dbf_ref)

        r = lax.broadcasted_iota(jnp.int32, (LANES, LANES), 0)
        cidx = lax.broadcasted_iota(jnp.int32, (LANES, LANES), 1)
        tri = (r <= cidx).astype(f32)
        carry = jnp.zeros((1, LANES), f32)
        total = jnp.zeros((1, LANES), f32)
        for blk in reversed(range(nb)):
            dcb = dc_ref[pl.ds(blk * LANES, LANES), :]
            dlf = jnp.dot(tri, dcb, preferred_element_type=f32, precision=lax.Precision.HIGHEST) + carry
            df = dlf * _sig(-f_ref[pl.ds(blk * LANES, LANES), :])
            df_ref[pl.ds(blk * LANES, LANES), :] = df.astype(bf16)
            total = total + jnp.sum(df, axis=0, keepdims=True)
            carry = carry + jnp.sum(dcb, axis=0, keepdims=True)
        dbf_ref[...] += total

    return pl.pallas_call(
        body, name="forget_bwd", grid=(t // seq,),
        in_specs=[pl.BlockSpec((seq, LANES), lambda b: (b, 0)), pl.BlockSpec((seq, LANES), lambda b: (b, 0))],
        out_specs=[pl.BlockSpec((seq, LANES), lambda b: (b, 0)), _whole((1, LANES))],
        out_shape=[jax.ShapeDtypeStruct((t, LANES), bf16), jax.ShapeDtypeStruct((1, LANES), f32)],
        compiler_params=_params(("arbitrary",)),
    )(dc, f128)


def _in_bwd(dz, df, x, dy, w_qkv, w_rest, w_f, w_pre):
    t = x.shape[0]
    n_qkv = w_qkv.shape[0] // D
    n_rest = w_rest.shape[0] // D

    def body(*refs):
        dz_refs = refs[:n_qkv + n_rest]
        df_ref, x_ref, dy_ref, wq_ref, wr_ref, wf_ref, wp_ref, gx_ref, dwp_ref = refs[n_qkv + n_rest:]

        @pl.when(pl.program_id(0) == 0)
        def _():
            dwp_ref[...] = jnp.zeros_like(dwp_ref)

        dh = _dot(df_ref[...], wf_ref[...])
        for p in range(n_qkv):
            dh = dh + _dot(dz_refs[p][...], wq_ref[pl.ds(p * D, D), :])
        for p in range(n_rest):
            dh = dh + _dot(dz_refs[n_qkv + p][...], wr_ref[pl.ds(p * D, D), :])
        xv = x_ref[...]
        r1 = lax.rsqrt(jnp.mean(xv * xv, axis=-1, keepdims=True) + NORM_EPS)
        xh = xv * r1
        dwp_ref[...] += jnp.sum(dh * xh, axis=0, keepdims=True)
        dxh = dh * wp_ref[...]
        gx_ref[...] = dy_ref[...] + r1 * (dxh - xh * jnp.mean(dxh * xh, axis=-1, keepdims=True))

    once = lambda shape: pl.BlockSpec(shape, lambda i: (0, 0), pipeline_mode=pl.Buffered(1))
    return pl.pallas_call(
        body, name="in_bwd", grid=(t // TM,),
        in_specs=[_tile(TM, D)] * (n_qkv + n_rest) + [_tile(TM, LANES), _tile(TM, D), _tile(TM, D),
                  once(w_qkv.shape), once(w_rest.shape), once(w_f.shape), _whole((1, D))],
        out_specs=[_tile(TM, D), _whole((1, D))],
        out_shape=[jax.ShapeDtypeStruct((t, D), f32), jax.ShapeDtypeStruct((1, D), f32)],
        compiler_params=_params(("arbitrary",), VMEM_LIMIT),
    )(*dz, df, x, dy, w_qkv, w_rest, w_f, w_pre)


def _tn_mm(name, a, b, tn, tk=2048):
    t, k = a.shape
    tk = min(tk, t)
    n = b.shape[1]

    def body(a_ref, b_ref, o_ref, s_ref):
        j, kk = pl.program_id(0), pl.program_id(1)

        @pl.when(kk == 0)
        def _():
            o_ref[...] = jnp.zeros_like(o_ref)

        @pl.when((j == 0) & (kk == 0))
        def _():
            s_ref[...] = jnp.zeros_like(s_ref)

        av = a_ref[...]
        o_ref[...] += _dot_tn(av, b_ref[...])

        @pl.when(j == 0)
        def _():
            s_ref[...] += jnp.sum(av.astype(f32), axis=0, keepdims=True)

    return pl.pallas_call(
        body, name=name, grid=(n // tn, t // tk),
        in_specs=[pl.BlockSpec((tk, k), lambda j, kk: (kk, 0)), pl.BlockSpec((tk, tn), lambda j, kk: (kk, j))],
        out_specs=[pl.BlockSpec((k, tn), lambda j, kk: (0, j)), _whole((1, k))],
        out_shape=[jax.ShapeDtypeStruct((k, n), f32), jax.ShapeDtypeStruct((1, k), f32)],
        compiler_params=_params(("arbitrary", "arbitrary"), VMEM_LIMIT),
    )(a, b)


def _position():
    return lax.axis_index("x"), lax.axis_index("y"), lax.axis_index("c")


def _gather_shards(parts, small):
    n = len(parts)
    halves = [p.shape[0] // 2 for p in parts]

    def body(*refs):
        srcs, small_src = refs[:n], refs[n]
        dsts, small_dst = refs[n + 1:2 * n + 1], refs[2 * n + 1]
        send, recv, local = refs[2 * n + 2:]
        x, y, c = _position()
        me = 2 * x + y
        chips = [(1 - x, y), (x, 1 - y), (1 - x, 1 - y)]
        ids = [2 * px + py for px, py in chips]

        def half(a, shard, which):
            return dsts[a].at[shard, pl.ds(which * halves[a], halves[a]), :]

        def over_ici(a, j, shard):
            px, py = chips[j]
            return pltpu.make_async_remote_copy(
                src_ref=srcs[a].at[pl.ds(c * halves[a], halves[a]), :], dst_ref=half(a, shard, c),
                send_sem=send.at[a * 3 + j], recv_sem=recv.at[a * 3 + j], device_id=(px, py, c), device_id_type=MESH)

        def to_sibling(a, j, which):
            k = 3 * n + a * 3 + j
            return pltpu.make_async_remote_copy(
                src_ref=half(a, ids[j], which), dst_ref=half(a, ids[j], which), send_sem=send.at[k],
                recv_sem=recv.at[k], device_id=(x, y, 1 - c), device_id_type=MESH)

        def small_copy(j, shard):
            px, py = chips[j]
            return pltpu.make_async_remote_copy(
                src_ref=small_src, dst_ref=small_dst.at[shard], send_sem=send.at[6 * n + j], recv_sem=recv.at[6 * n + j],
                device_id=(px, py, c), device_id_type=MESH)

        own = [pltpu.make_async_copy(srcs[a], dsts[a].at[me], local.at[a]) for a in range(n)]
        own.append(pltpu.make_async_copy(small_src, small_dst.at[me], local.at[n]))
        for cp in own:
            cp.start()
        first = [over_ici(a, j, me) for j in range(3) for a in range(n)] + [small_copy(j, me) for j in range(3)]
        for cp in first:
            cp.start()
        passed = []
        for j in range(3):
            for a in range(n):
                over_ici(a, j, ids[j]).wait_recv()
                passed.append(to_sibling(a, j, c))
                passed[-1].start()
        for j in range(3):
            small_copy(j, ids[j]).wait_recv()
            for a in range(n):
                to_sibling(a, j, 1 - c).wait_recv()
        for cp in first + passed:
            cp.wait_send()
        for cp in own:
            cp.wait()

    vm = pl.BlockSpec(memory_space=pltpu.VMEM)
    return pl.pallas_call(
        body, name="gather_shards",
        in_specs=[vm] * (n + 1), out_specs=[vm] * (n + 1),
        out_shape=[jax.ShapeDtypeStruct((N_CHIPS,) + p.shape, p.dtype) for p in parts + [small]],
        scratch_shapes=[pltpu.SemaphoreType.DMA((6 * n + 3,)), pltpu.SemaphoreType.DMA((6 * n + 3,)),
                        pltpu.SemaphoreType.DMA((n + 1,))],
        compiler_params=pltpu.CompilerParams(vmem_limit_bytes=VMEM_LIMIT),
    )(*parts, small)


def _allsum_rows(part):
    rows_n = part.shape[0]

    def body(x_ref, gath_ref, sum_ref, send_sems, recv_sems, local_sem):
        x, y, c = _position()
        me, sibling = (x, y, c), (x, y, 1 - c)
        chips = [(1 - x, y), (x, 1 - y), (1 - x, 1 - y)]

        def rows(px, py, pc):
            return gath_ref.at[pl.ds((4 * px + 2 * py + pc) * rows_n, rows_n), :]

        def copy(k, block, to, src=None):
            return pltpu.make_async_remote_copy(
                src_ref=rows(*block) if src is None else src, dst_ref=rows(*block),
                send_sem=send_sems.at[k], recv_sem=recv_sems.at[k], device_id=to, device_id_type=MESH)

        mine = pltpu.make_async_copy(x_ref, rows(*me), local_sem)
        mine.start()
        first = [copy(0, me, sibling, src=x_ref)]
        first += [copy(1 + j, me, (*chip, c), src=x_ref) for j, chip in enumerate(chips)]
        for cp in first:
            cp.start()
        passed = [copy(4 + j, (*chip, c), sibling) for j, chip in enumerate(chips)]
        for j, chip in enumerate(chips):
            copy(1 + j, (*chip, c), me).wait_recv()
            passed[j].start()
        copy(0, sibling, me).wait_recv()
        for j, chip in enumerate(chips):
            copy(4 + j, (*chip, 1 - c), me).wait_recv()
        for cp in first + passed:
            cp.wait_send()
        mine.wait()
        total = gath_ref[pl.ds(0, rows_n), :]
        for d in range(1, N_DEV):
            total = total + gath_ref[pl.ds(d * rows_n, rows_n), :]
        sum_ref[...] = total

    vm = pl.BlockSpec(memory_space=pltpu.VMEM)
    return pl.pallas_call(
        body, name="allsum_rows", in_specs=[vm], out_specs=[vm, vm],
        out_shape=[jax.ShapeDtypeStruct((N_DEV * rows_n, D), f32), jax.ShapeDtypeStruct((rows_n, D), f32)],
        scratch_shapes=[pltpu.SemaphoreType.DMA((7,)), pltpu.SemaphoreType.DMA((7,)), pltpu.SemaphoreType.DMA],
    )(part)[1]


PAIR_ROWS = 16


def _pair_reduce(name, pieces):
    _, r, n = pieces.shape

    def body(p_ref, o_ref, land, send, recv):
        x, y, c = _position()

        def remote(j, half):
            return pltpu.make_async_remote_copy(
                src_ref=p_ref.at[2 * j + half], dst_ref=land.at[j], send_sem=send.at[j], recv_sem=recv.at[j],
                device_id=(x, y, 1 - c), device_id_type=MESH)

        sends = [remote(j, 1 - c) for j in range(N_CHIPS)]
        for cp in sends:
            cp.start()
        for j in range(N_CHIPS):
            remote(j, c).wait_recv()

            def add_rows(i, carry, j=j):
                rows = pl.ds(pl.multiple_of(i * PAIR_ROWS, PAIR_ROWS), PAIR_ROWS)
                o_ref[j, rows, :] = (p_ref[2 * j + c, rows, :].astype(f32) + land[j, rows, :].astype(f32)).astype(bf16)
                return carry

            lax.fori_loop(0, r // PAIR_ROWS, add_rows, 0)
        for cp in sends:
            cp.wait_send()

    vm = pl.BlockSpec(memory_space=pltpu.VMEM)
    return pl.pallas_call(
        body, name=name, in_specs=[vm], out_specs=vm,
        out_shape=jax.ShapeDtypeStruct((N_CHIPS, r, n), bf16),
        scratch_shapes=[pltpu.VMEM((N_CHIPS, r, n), bf16), pltpu.SemaphoreType.DMA((N_CHIPS,)),
                        pltpu.SemaphoreType.DMA((N_CHIPS,))],
        compiler_params=pltpu.CompilerParams(vmem_limit_bytes=VMEM_LIMIT),
    )(pieces)


def _chip_exchange(arrs):
    n = len(arrs)

    def body(*refs):
        srcs, dsts = refs[:n], refs[n:2 * n]
        send, recv, local = refs[2 * n:]
        x, y, c = _position()
        me = 2 * x + y
        chips = [(1 - x, y), (x, 1 - y), (1 - x, 1 - y)]

        def remote(a, j, piece, landing):
            px, py = chips[j]
            return pltpu.make_async_remote_copy(
                src_ref=srcs[a].at[piece], dst_ref=dsts[a].at[landing], send_sem=send.at[a * 3 + j],
                recv_sem=recv.at[a * 3 + j], device_id=(px, py, c), device_id_type=MESH)

        own = [pltpu.make_async_copy(srcs[a].at[me], dsts[a].at[me], local.at[a]) for a in range(n)]
        sends = [remote(a, j, 2 * px + py, me) for j, (px, py) in enumerate(chips) for a in range(n)]
        for cp in sends + own:
            cp.start()
        for j, (px, py) in enumerate(chips):
            for a in range(n):
                remote(a, j, me, 2 * px + py).wait_recv()
        for cp in sends:
            cp.wait_send()
        for cp in own:
            cp.wait()

    anyspec = pl.BlockSpec(memory_space=pl.ANY)
    return pl.pallas_call(
        body, name="chip_exchange", in_specs=[anyspec] * n, out_specs=[anyspec] * n,
        out_shape=[jax.ShapeDtypeStruct(a.shape, a.dtype) for a in arrs],
        scratch_shapes=[pltpu.SemaphoreType.DMA((3 * n,)), pltpu.SemaphoreType.DMA((3 * n,)),
                        pltpu.SemaphoreType.DMA((n,))],
    )(*arrs)


def _swap_halves(arrs):
    n = len(arrs)

    def body(*refs):
        srcs, dsts = refs[:n], refs[n:2 * n]
        send, recv, local = refs[2 * n:]
        x, y, c = _position()

        def remote(a, landing):
            return pltpu.make_async_remote_copy(
                src_ref=srcs[a], dst_ref=dsts[a].at[landing], send_sem=send.at[a], recv_sem=recv.at[a],
                device_id=(x, y, 1 - c), device_id_type=MESH)

        own = [pltpu.make_async_copy(srcs[a], dsts[a].at[c], local.at[a]) for a in range(n)]
        sends = [remote(a, c) for a in range(n)]
        for cp in sends + own:
            cp.start()
        for a in range(n):
            remote(a, 1 - c).wait_recv()
        for cp in sends:
            cp.wait_send()
        for cp in own:
            cp.wait()

    vm = pl.BlockSpec(memory_space=pltpu.VMEM)
    return pl.pallas_call(
        body, name="swap_halves", in_specs=[vm] * n, out_specs=[vm] * n,
        out_shape=[jax.ShapeDtypeStruct((2,) + a.shape, a.dtype) for a in arrs],
        scratch_shapes=[pltpu.SemaphoreType.DMA((n,)), pltpu.SemaphoreType.DMA((n,)), pltpu.SemaphoreType.DMA((n,))],
        compiler_params=pltpu.CompilerParams(vmem_limit_bytes=VMEM_LIMIT),
    )(*arrs)


def _row_block(r):
    return 128 if r % 128 == 0 else r


def _sum_slots(name, slots):
    s, r, n = slots.shape
    rb = _row_block(r)

    def body(s_ref, o_ref):
        total = s_ref[0].astype(f32)
        for d in range(1, s):
            total = total + s_ref[d].astype(f32)
        o_ref[...] = total

    return pl.pallas_call(
        body, name=name, grid=(r // rb,),
        in_specs=[pl.BlockSpec((s, rb, n), lambda i: (0, i, 0))],
        out_specs=pl.BlockSpec((rb, n), lambda i: (i, 0)),
        out_shape=jax.ShapeDtypeStruct((r, n), f32),
        compiler_params=_params(("parallel",), VMEM_LIMIT),
    )(slots)


def _adamw(name, w, g, m, v):
    r, n = w.shape
    if r % 128 == 0 or r * n <= 128 * 1024:
        rb, nb = _row_block(r), n
    else:
        rb, nb = r, LANES

    def body(w_ref, g_ref, m_ref, v_ref, d_ref, nm_ref, nv_ref):
        gv = g_ref[...]
        m2 = ADAM_B1 * m_ref[...] + (1.0 - ADAM_B1) * gv
        v2 = ADAM_B2 * v_ref[...] + (1.0 - ADAM_B2) * (gv * gv)
        m_hat = m2 / (1.0 - ADAM_B1 ** ADAM_STEP)
        v_hat = v2 / (1.0 - ADAM_B2 ** ADAM_STEP)
        d_ref[...] = (-ADAM_LR) * (m_hat / (jnp.sqrt(v_hat) + ADAM_EPS) + ADAM_WD * w_ref[...])
        nm_ref[...] = m2
        nv_ref[...] = v2

    spec = pl.BlockSpec((rb, nb), lambda i, j: (i, j))
    return pl.pallas_call(
        body, name=name, grid=(r // rb, n // nb), in_specs=[spec] * 4, out_specs=[spec] * 3,
        out_shape=[jax.ShapeDtypeStruct((r, n), f32)] * 3,
        compiler_params=_params(("parallel", "parallel"), VMEM_LIMIT),
    )(w, g, m, v)


def _identity(a):
    return a


def _local_step(x2, tgt2, seq, wt):
    nb = x2.shape[0] // seq
    h = _prenorm(x2, wt["pre_w"])
    qkv = _mm("in_qkv", [(h, 0)], _identity, wt["w_qkv"], wt["b_qkv"], bf16, 1024, 1024, w_is_nk=True)
    rest = _mm("in_rest", [(h, 0)], _identity, wt["w_rest"], wt["b_rest"], bf16, 1024, 1024, w_is_nk=True)
    f128 = _mm("in_f", [(h, 0)], _identity, wt["w_f"], wt["b_f"], f32, 1024, LANES, w_is_nk=True)
    c = _forget_prep(f128, seq)
    qa, ka = _attn_prep(qkv, c)
    o_att, pa, lse = _attn_fwd(qa, ka, qkv, rest, seq)
    ya = _mm("proj_a", [(pa, 0)], _identity, wt["w_a"], None, bf16, 1024, D)
    rnn_w = (wt["conv_w"], wt["conv_b"], wt["wa_d"], wt["wx_d"], wt["ba"], wt["bx"], wt["lam"])
    xc, a, hrec, pr = _rnn_fwd(rest, *rnn_w, seq)
    yr = _mm("proj_r", [(pr, 0)], _identity, wt["w_r"], None, bf16, 1024, D)
    o, mrg = _mm("proj_out", [(rest, 3), (rest, 4), (ya, 0), (yr, 0)], _merge, wt["w_o"], None, f32, TM, D,
                 keep_lhs=True)

    do, dy, loss8, d_post = _post_loss(o, x2, tgt2, wt["post_w"])
    dya, dyr, dmga, dmgr = _out_bwd(do, rest, ya, yr, wt["w_o"])
    doa, dga = _branch_bwd("branch_a_bwd", dya, rest, 0, o_att, wt["w_a"], bf16)
    dhrec, dgr = _branch_bwd("branch_r_bwd", dyr, rest, 2, hrec, wt["w_r"], f32)
    d_wo, _ = _tn_mm("dw_out", mrg, do, 512)
    d_wa, _ = _tn_mm("dw_branch_a", pa, dya, 512)
    d_wr, _ = _tn_mm("dw_branch_r", pr, dyr, 512)
    dxr, d_wad, d_wxd, vec = _rnn_bwd(dhrec, a, hrec, xc, rest, *rnn_w, seq)
    dq, dk, dv, dc_pairs = _attn_bwd(qa, ka, qkv, doa, lse, _attn_delta(doa, o_att), seq)
    dc = dc_pairs.reshape(-1, HEADS // ATT_GROUP, LANES)[:, :, :ATT_GROUP].reshape(-1, HEADS)
    df, db_f = _forget_bwd(_pad_cols(dc, LANES), f128, seq)
    pieces = [dq, dk, dv, dga, dxr, dgr, dmga, dmgr]
    gx, d_pre = _in_bwd(pieces, df, x2, dy, wt["w_qkv"], wt["w_rest"], wt["w_f"], wt["pre_w"])
    names = ["q", "k", "v", "ga", "xr", "gr", "mga", "mgr"]
    dws, dbs = [], []
    for nm, piece in zip(names, pieces):
        dw_p, db_p = _tn_mm("dw_in_" + nm, piece, h, 512)
        dws.append(dw_p)
        dbs.append(db_p)
    dw_f, _ = _tn_mm("dw_in_f", df, h, 512)
    zeros_w = jnp.zeros((IN_TOTAL - IN_USED, D), f32)
    d_w_in = jnp.concatenate(dws[:3] + [dw_f[:HEADS]] + dws[3:] + [zeros_w], axis=0)
    d_b_in = jnp.concatenate(dbs[:3] + [db_f[:, :HEADS]] + dbs[3:] + [zeros_w[:, :1].T], axis=1)
    return dict(loss=loss8[0, 0], grad_x=gx, pre_w=d_pre, w_in=d_w_in, b_in=d_b_in, conv_w=vec[4:8], conv_b=vec[3:4],
                wa_d=d_wad, ba=vec[0:1], wx_d=d_wxd, bx=vec[1:2], lam=vec[2:3], w_a=d_wa, w_r=d_wr, w_o=d_wo,
                post_w=d_post)


def _block_diag(w):
    g, bw, _ = w.shape
    eye = jnp.eye(g, dtype=w.dtype)
    return (w[:, :, None, :] * eye[:, None, :, None]).reshape(g * bw, g * bw)


def _gate_blocks(diag):
    half = diag.shape[1] // 2
    return jnp.stack([diag[:, :half, :half], diag[:, half:, half:]], axis=1).reshape(-1, half, half)


def _pad_cols(a, n):
    return jnp.pad(a, ((0, 0), (0, n - a.shape[1])))


def _pad_rows(a, n):
    return jnp.pad(a, ((0, n - a.shape[0]), (0, 0)))


def kernel(x, pre_norm_w, w_in, b_in, conv_w, conv_b, rg_wa, rg_ba, rg_wx, rg_bx, rg_lambda, w_branch_a, w_branch_r, w_out, post_norm_w, loss_target, m_pre_norm_w, m_w_in, m_b_in, m_conv_w, m_conv_b, m_rg_wa, m_rg_ba, m_rg_wx, m_rg_bx, m_rg_lambda, m_w_branch_a, m_w_branch_r, m_w_out, m_post_norm_w, v_pre_norm_w, v_w_in, v_b_in, v_conv_w, v_conv_b, v_rg_wa, v_rg_ba, v_rg_wx, v_rg_bx, v_rg_lambda, v_w_branch_a, v_w_branch_r, v_w_out, v_post_norm_w):
    nb, seq, _ = x.shape
    chip = 2 * lax.axis_index("x") + lax.axis_index("y")
    n_groups = rg_wa.shape[1]

    w_in_t = jnp.transpose(w_in[0])
    shard_cols = w_in_t.shape[0]
    padded = -(-shard_cols // 32) * 32
    g_in, g_a, g_r, g_o, g_cw = _gather_shards(
        [_pad_rows(w_in_t.astype(bf16), padded), w_branch_a[0].astype(bf16), w_branch_r[0].astype(bf16),
         w_out[0].astype(bf16)], conv_w[0])
    w_full = jnp.concatenate([g_in[j, :shard_cols] for j in range(N_CHIPS)], axis=0)
    q_end, f_end = 3 * D, 3 * D + HEADS
    wt = dict(
        pre_w=pre_norm_w, post_w=post_norm_w,
        w_qkv=w_full[:q_end], b_qkv=b_in[:, :q_end],
        w_f=_pad_rows(w_full[q_end:f_end], LANES), b_f=_pad_cols(b_in[:, q_end:f_end], LANES),
        w_rest=w_full[f_end:IN_USED], b_rest=b_in[:, f_end:IN_USED],
        w_a=g_a.reshape(D, D), w_r=g_r.reshape(D, D), w_o=g_o.reshape(D, D),
        conv_w=jnp.transpose(g_cw, (1, 0, 2)).reshape(4, D), conv_b=conv_b,
        wa_d=_block_diag(rg_wa[0]).astype(bf16), wx_d=_block_diag(rg_wx[0]).astype(bf16),
        ba=rg_ba, bx=rg_bx, lam=rg_lambda)

    part = _local_step(x.reshape(nb * seq, D), loss_target.reshape(nb * seq, D), seq, wt)
    loss = lax.psum(part["loss"], ("x", "y", "c"))
    grad_x = part["grad_x"].reshape(nb, seq, D)

    small = jnp.concatenate([
        part["pre_w"], _pad_cols(part["b_in"], 10 * D).reshape(10, D), part["conv_b"],
        _gate_blocks(part["wa_d"]).reshape(-1, D), part["ba"],
        _gate_blocks(part["wx_d"]).reshape(-1, D), part["bx"], part["lam"], part["post_w"],
        part["conv_w"]], axis=0)
    n_small = small.shape[0]
    n_rep = n_small - 4
    tot = _allsum_rows(_pad_rows(small, -(-n_small // 8) * 8))
    g_rep = tot[:n_rep]
    g_conv_w = lax.dynamic_slice_in_dim(tot[n_rep:n_small], chip * (D // N_CHIPS), D // N_CHIPS, axis=1)

    def pack(pre, b, cb, wa, ba, wx, bx, lam, post):
        return jnp.concatenate([pre, _pad_cols(b, 10 * D).reshape(10, D), cb, wa.reshape(-1, D), ba,
                                wx.reshape(-1, D), bx, lam, post], axis=0)

    def unpack(p):
        o = [0]

        def take(k):
            o[0] += k
            return p[o[0] - k:o[0]]

        pre = take(1)
        b = take(10).reshape(1, 10 * D)[:, :IN_TOTAL]
        cb = take(1)
        wa = take(64).reshape(rg_wa.shape)
        ba = take(1)
        wx = take(64).reshape(rg_wx.shape)
        bx = take(1)
        lam = take(1)
        post = take(1)
        return dict(pre_norm_w=pre, b_in=b, conv_b=cb, rg_wa=wa, rg_ba=ba, rg_wx=wx, rg_bx=bx, rg_lambda=lam,
                    post_norm_w=post)

    w_rep = pack(pre_norm_w, b_in, conv_b, rg_wa, rg_ba, rg_wx, rg_bx, rg_lambda, post_norm_w)
    m_rep = pack(m_pre_norm_w, m_b_in, m_conv_b, m_rg_wa, m_rg_ba, m_rg_wx, m_rg_bx, m_rg_lambda, m_post_norm_w)
    v_rep = pack(v_pre_norm_w, v_b_in, v_conv_b, v_rg_wa, v_rg_ba, v_rg_wx, v_rg_bx, v_rg_lambda, v_post_norm_w)
    d_rep, nm_rep, nv_rep = _adamw("adamw_rep", w_rep, g_rep, m_rep, v_rep)
    grads, deltas, new_m, new_v = unpack(g_rep), unpack(d_rep), unpack(nm_rep), unpack(nv_rep)

    p_in = jnp.pad(part["w_in"].reshape(N_CHIPS, shard_cols, D), ((0, 0), (0, padded - shard_cols), (0, 0)))
    p_in = p_in.reshape(N_DEV, padded // 2, D)
    p_aro = jnp.concatenate([part[k].reshape(N_DEV, D // N_DEV, D) for k in ("w_a", "w_r", "w_o")], axis=1)
    s_in, s_aro = _chip_exchange([_pair_reduce("pair_w_in", p_in.astype(bf16)),
                                  _pair_reduce("pair_w_aro", p_aro.astype(bf16))])
    f_in, f_aro = _swap_halves([_sum_slots("sum_w_in", s_in), _sum_slots("sum_w_aro", s_aro)])
    g_w_in_t = f_in.reshape(padded, D)[:shard_cols]
    rows = D // N_DEV
    g_aro = [f_aro[:, i * rows:(i + 1) * rows, :].reshape(2 * rows, D) for i in range(3)]

    w_in_upd = _adamw("adamw_w_in", w_in_t, g_w_in_t, jnp.transpose(m_w_in[0]), jnp.transpose(v_w_in[0]))
    g_w_in, d_w_in, nm_w_in, nv_w_in = [jnp.transpose(a) for a in (g_w_in_t, *w_in_upd)]
    upd_a = _adamw("adamw_w_branch_a", w_branch_a[0], g_aro[0], m_w_branch_a[0], v_w_branch_a[0])
    upd_r = _adamw("adamw_w_branch_r", w_branch_r[0], g_aro[1], m_w_branch_r[0], v_w_branch_r[0])
    upd_o = _adamw("adamw_w_out", w_out[0], g_aro[2], m_w_out[0], v_w_out[0])
    d_aro, nm_aro, nv_aro = zip(upd_a, upd_r, upd_o)
    d_cw, nm_cw, nv_cw = _adamw("adamw_conv_w", conv_w[0], g_conv_w, m_conv_w[0], v_conv_w[0])

    def sharded(t_in, t_aro, t_cw):
        return dict(w_in=t_in[None], conv_w=t_cw[None], w_branch_a=t_aro[0][None], w_branch_r=t_aro[1][None],
                    w_out=t_aro[2][None])

    order = ["pre_norm_w", "w_in", "b_in", "conv_w", "conv_b", "rg_wa", "rg_ba", "rg_wx", "rg_bx", "rg_lambda",
             "w_branch_a", "w_branch_r", "w_out", "post_norm_w"]
    outs = [loss, grad_x]
    for rep, shd in ((grads, sharded(g_w_in, g_aro, g_conv_w)), (deltas, sharded(d_w_in, d_aro, d_cw)),
                     (new_m, sharded(nm_w_in, nm_aro, nm_cw)), (new_v, sharded(nv_w_in, nv_aro, nv_cw))):
        both = {**rep, **shd}
        outs.extend(both[k] for k in order)
    return tuple(outs)
```

```python
import jax
import jax.numpy as jnp
from jax import lax
from jax.experimental import pallas as pl
from jax.experimental.pallas import tpu as pltpu

f32 = jnp.float32
bf16 = jnp.bfloat16

D = 1024
HEADS = 16
HEAD_PAIRS = 8
LANES = 128
NORM_EPS = 1e-6
MASK_VALUE = -1e30
RG_C = 8.0
QK_SCALE = 0.125
TQ = 256
ATT_GROUP = 8
ATT_GROUP_FWD = 16
TL = 256
TM = 256
PREV_ROWS = 16
IN_USED = 8 * D + HEADS
IN_TOTAL = 9 * D + HEADS
N_CHIPS = 4
N_DEV = 8
ADAM_LR, ADAM_B1, ADAM_B2, ADAM_EPS, ADAM_WD, ADAM_STEP = 0.001, 0.9, 0.999, 1e-08, 0.01, 10
VMEM_LIMIT = 56 * 1024 * 1024
MESH = pl.DeviceIdType.MESH


def _dot(a, b):
    return jnp.dot(a, b, preferred_element_type=f32)


def _dot_nt(a, b):
    return lax.dot_general(a, b, (((1,), (1,)), ((), ())), preferred_element_type=f32)


def _dot_tn(a, b):
    return lax.dot_general(a, b, (((0,), (0,)), ((), ())), preferred_element_type=f32)


def _sig(x):
    return 0.5 * jnp.tanh(0.5 * x) + 0.5


def _softplus(x):
    return jnp.maximum(x, 0.0) + jnp.log(1.0 + jnp.exp(-jnp.abs(x)))


def _params(sem, vmem=None):
    return pltpu.CompilerParams(dimension_semantics=sem, vmem_limit_bytes=vmem)


def _tile(tm, width, cb=0):
    return pl.BlockSpec((tm, width), lambda i, cb=cb: (i, cb))


def _whole(shape):
    nd = len(shape)
    return pl.BlockSpec(shape, lambda *_: (0,) * nd)


def _prenorm(x, w_pre):
    t = x.shape[0]

    def body(x_ref, w_ref, h_ref):
        xv = x_ref[...]
        r = lax.rsqrt(jnp.mean(xv * xv, axis=-1, keepdims=True) + NORM_EPS)
        h_ref[...] = (xv * r * w_ref[...]).astype(bf16)

    return pl.pallas_call(
        body, name="prenorm", grid=(t // TM,),
        in_specs=[_tile(TM, D), _whole((1, D))], out_specs=_tile(TM, D),
        out_shape=jax.ShapeDtypeStruct((t, D), bf16),
        compiler_params=_params(("parallel",)),
    )(x, w_pre)


def _mm(name, a, w, bias, out_dtype, tm, tn, w_is_nk=False):
    t, k = a.shape
    tm = min(tm, t)
    n = w.shape[0] if w_is_nk else w.shape[1]

    def body(a_ref, w_ref, *refs):
        acc = _dot_nt(a_ref[...], w_ref[...]) if w_is_nk else _dot(a_ref[...], w_ref[...])
        if bias is not None:
            acc = acc + refs[0][...]
        refs[-1][...] = acc.astype(out_dtype)

    in_specs = [pl.BlockSpec((tm, k), lambda i, j: (i, 0)),
                pl.BlockSpec((tn, k), lambda i, j: (j, 0)) if w_is_nk else pl.BlockSpec((k, tn), lambda i, j: (0, j))]
    args = [a, w]
    if bias is not None:
        in_specs.append(pl.BlockSpec((1, tn), lambda i, j: (0, j)))
        args.append(bias)
    return pl.pallas_call(
        body, name=name, grid=(t // tm, n // tn), in_specs=in_specs,
        out_specs=pl.BlockSpec((tm, tn), lambda i, j: (i, j)), out_shape=jax.ShapeDtypeStruct((t, n), out_dtype),
        compiler_params=_params(("parallel", "parallel"), VMEM_LIMIT),
    )(*args)


def _forget_prep(f128, seq):
    t = f128.shape[0]
    nb = seq // LANES

    def body(f_ref, c_ref):
        r = lax.broadcasted_iota(jnp.int32, (LANES, LANES), 0)
        cidx = lax.broadcasted_iota(jnp.int32, (LANES, LANES), 1)
        tri = (r >= cidx).astype(f32)
        carry = jnp.zeros((1, LANES), f32)
        for blk in range(nb):
            fv = f_ref[pl.ds(blk * LANES, LANES), :]
            lf = -_softplus(-fv)
            c_ref[pl.ds(blk * LANES, LANES), :] = (
                jnp.dot(tri, lf, preferred_element_type=f32, precision=lax.Precision.HIGHEST) + carry)
            carry = carry + jnp.sum(lf, axis=0, keepdims=True)

    return pl.pallas_call(
        body, name="forget_prep", grid=(t // seq,),
        in_specs=[pl.BlockSpec((seq, LANES), lambda b: (b, 0))],
        out_specs=pl.BlockSpec((seq, LANES), lambda b: (b, 0)),
        out_shape=jax.ShapeDtypeStruct((t, LANES), f32),
        compiler_params=_params(("parallel",)),
    )(f128)


def _split3(cv):
    hi = cv.astype(bf16)
    r1 = cv - hi.astype(f32)
    mid = r1.astype(bf16)
    lo = (r1 - mid.astype(f32)).astype(bf16)
    return hi, mid, lo


def _attn_prep(qkv, c):
    t = qkv.shape[0]

    def body(q_ref, k_ref, c_ref, qa_ref, ka_ref):
        lane = lax.broadcasted_iota(jnp.int32, (1, LANES), 1)
        cv = c_ref[...]
        one = jnp.ones((), bf16)
        zero = jnp.zeros((), bf16)
        q_ones = jnp.where((lane >= 67) & (lane < 70), one, zero)
        k_ones = jnp.where((lane >= 64) & (lane < 67), one, zero)
        for head in range(HEADS):
            pair = pl.ds((head // 2) * LANES, LANES)
            ch = jnp.sum(jnp.where(lane == head, cv, 0.0), axis=1, keepdims=True)
            hi, mid, lo = _split3(ch)
            q2, k2 = q_ref[:, pair], k_ref[:, pair]
            if head % 2 == 1:
                q2, k2 = pltpu.roll(q2, 64, 1), pltpu.roll(k2, 64, 1)
            qa = jnp.where(lane < 64, q2 * jnp.asarray(QK_SCALE, bf16),
                           jnp.where(lane == 64, hi, jnp.where(lane == 65, mid, jnp.where(lane == 66, lo, q_ones))))
            ka = jnp.where(lane < 64, k2,
                           jnp.where(lane == 67, -hi, jnp.where(lane == 68, -mid, jnp.where(lane == 69, -lo, k_ones))))
            qa_ref[:, pl.ds(head * LANES, LANES)] = qa
            ka_ref[:, pl.ds(head * LANES, LANES)] = ka

    tm = min(TM, t)
    out = pl.BlockSpec((tm, 2 * D), lambda i: (i, 0))
    return pl.pallas_call(
        body, name="attn_prep", grid=(t // tm,),
        in_specs=[_tile(tm, D, 0), _tile(tm, D, 1), _tile(tm, LANES)],
        out_specs=[out, out],
        out_shape=[jax.ShapeDtypeStruct((t, 2 * D), bf16)] * 2,
        compiler_params=_params(("parallel",)),
    )(qkv, qkv, c)


def _attn_fwd(qa, ka, qkv, rest, seq):
    t = qkv.shape[0]
    nb, nq = t // seq, seq // TQ

    hg = ATT_GROUP_FWD
    ng = HEADS // hg

    def body(q_ref, k_ref, v_ref, ga_ref, o_ref, pa_ref, lse_ref, acc_scr):
        qi, gi = pl.program_id(1), pl.program_id(2)
        krow = lax.broadcasted_iota(jnp.int32, (TQ, TQ), 0)
        qcol = lax.broadcasted_iota(jnp.int32, (TQ, TQ), 1)
        acc_scr[...] = jnp.zeros_like(acc_scr)

        def kv_step(kt, carry, masked):
            ks = pl.multiple_of(kt * TQ, TQ)
            sts = [_dot_nt(k_ref[pl.ds(ks, TQ), pl.ds(g * LANES, LANES)], q_ref[:, pl.ds(g * LANES, LANES)])
                   for g in range(hg)]
            if masked:
                sts = [jnp.where(krow <= qcol, st, MASK_VALUE) for st in sts]
            m_new = [jnp.maximum(carry[g][0], jnp.max(sts[g], axis=0, keepdims=True)) for g in range(hg)]
            ps = [jnp.exp(sts[g] - m_new[g]) for g in range(hg)]
            alphas = [jnp.exp(carry[g][0] - m_new[g]) for g in range(hg)]
            phi = [ps[g].astype(bf16) for g in range(hg)]
            plo = [(ps[g] - phi[g].astype(f32)).astype(bf16) for g in range(hg)]
            vs = [v_ref[pl.ds(ks, TQ), pl.ds(j * LANES, LANES)] for j in range(hg // 2)]
            pvs = [_dot_tn(vs[g // 2], phi[g]) + _dot_tn(vs[g // 2], plo[g]) for g in range(hg)]
            olds = [acc_scr[g] for g in range(hg)]
            for g in range(hg):
                acc_scr[g] = alphas[g] * olds[g] + pvs[g]
            return tuple((m_new[g], alphas[g] * carry[g][1] + jnp.sum(ps[g], axis=0, keepdims=True))
                         for g in range(hg))

        init = tuple((jnp.full((1, TQ), MASK_VALUE, f32), jnp.zeros((1, TQ), f32)) for _ in range(hg))
        carry = lax.fori_loop(0, qi, lambda kt, cr: kv_step(kt, cr, False), init)
        stats = kv_step(qi, carry, True)
        drow = lax.broadcasted_iota(jnp.int32, (LANES, TQ), 0)
        for g in range(hg):
            m, l = stats[g]
            lse_ref[0, pl.ds(hg * gi + g, 1), :] = m + jnp.log(l)
        for j in range(hg // 2):
            o2 = jnp.where(drow < 64, acc_scr[2 * j] / stats[2 * j][1], acc_scr[2 * j + 1] / stats[2 * j + 1][1]).T
            o_ref[:, pl.ds(j * LANES, LANES)] = o2
            ga = ga_ref[:, pl.ds(j * LANES, LANES)].astype(f32)
            pa_ref[:, pl.ds(j * LANES, LANES)] = (o2 * (ga * _sig(ga))).astype(bf16)

    vw = hg * 64
    tile = pl.BlockSpec((TQ, vw), lambda b, qi, gi: (b * nq + qi, gi))
    return pl.pallas_call(
        body, name="attn_fwd", grid=(nb, nq, ng),
        in_specs=[pl.BlockSpec((TQ, hg * LANES), lambda b, qi, gi: (b * nq + qi, gi)),
                  pl.BlockSpec((seq, hg * LANES), lambda b, qi, gi: (b, gi)),
                  pl.BlockSpec((seq, vw), lambda b, qi, gi: (b, 2 * ng + gi)), tile],
        out_specs=[tile, tile, pl.BlockSpec((1, HEADS, TQ), lambda b, qi, gi: (b * nq + qi, 0, 0))],
        out_shape=[jax.ShapeDtypeStruct((t, D), f32), jax.ShapeDtypeStruct((t, D), bf16),
                   jax.ShapeDtypeStruct((t // TQ, HEADS, TQ), f32)],
        scratch_shapes=[pltpu.VMEM((hg, LANES, TQ), f32)],
        compiler_params=_params(("parallel", "parallel", "arbitrary"), VMEM_LIMIT),
    )(qa, ka, qkv, rest)


def _shifted_rows(x, top8, prev8, shift, row, row8):
    body = pltpu.roll(x, shift, 0)
    head = jnp.where(row8 < shift, pltpu.roll(prev8, shift, 0), pltpu.roll(top8, shift, 0))
    return body, head


def _rnn_gates(xc, wa_ref, wx_ref, ba_ref, bx_ref, lam_ref):
    xcb = xc.astype(bf16)
    r = _sig(_dot(xcb, wa_ref[...]) + ba_ref[...])
    i = _sig(_dot(xcb, wx_ref[...]) + bx_ref[...])
    sp = _softplus(-lam_ref[...])
    log_a = (-RG_C) * r * sp
    th = jnp.tanh(log_a)
    w1 = (-2.0) * th / (1.0 - th)
    sq = jnp.sqrt(jnp.maximum(w1, 0.0))
    return r, i, sp, log_a, w1, sq


def _conv_tile(x_ref, xprev_ref, has_prev, cw_ref, cb_ref, xc_ref):
    row = lax.broadcasted_iota(jnp.int32, (TL, D), 0)
    row8 = lax.broadcasted_iota(jnp.int32, (8, D), 0)
    x = x_ref[...].astype(f32)
    top8 = x[:8]
    prev8 = jnp.where(has_prev, xprev_ref[...].astype(f32)[PREV_ROWS - 8:], 0.0)
    xc = cb_ref[...] + cw_ref[pl.ds(3, 1), :] * x
    xc8 = cb_ref[...] + cw_ref[pl.ds(3, 1), :] * top8
    for sh in range(1, 4):
        w = cw_ref[pl.ds(3 - sh, 1), :]
        xs, xs8 = _shifted_rows(x, top8, prev8, sh, row, row8)
        xc = xc + w * xs
        xc8 = xc8 + w * xs8
    xc_ref[...] = xc
    xc_ref[pl.ds(0, 8), :] = xc8


def _rnn_fwd(rest, conv_w, conv_b, wa_d, wx_d, ba, bx, lam, seq):
    t = rest.shape[0]
    nb, nt = t // seq, seq // TL

    def body(x_ref, xprev_ref, gr_ref, cw_ref, cb_ref, wa_ref, wx_ref, ba_ref, bx_ref, lam_ref,
             xc_ref, a_ref, h_ref, pr_ref, u_scr, carry):
        tt = pl.program_id(1)
        _conv_tile(x_ref, xprev_ref, tt > 0, cw_ref, cb_ref, xc_ref)
        xc = xc_ref[...]
        r, i, sp, log_a, w1, sq = _rnn_gates(xc, wa_ref, wx_ref, ba_ref, bx_ref, lam_ref)
        a_ref[...] = jnp.exp(log_a)
        u_scr[...] = sq * (i * xc)

        @pl.when(tt == 0)
        def _():
            carry[...] = jnp.zeros_like(carry)

        def step(s, h):
            h = a_ref[pl.ds(s, 1), :] * h + u_scr[pl.ds(s, 1), :]
            h_ref[pl.ds(s, 1), :] = h
            return h

        carry[...] = lax.fori_loop(0, TL, step, carry[...], unroll=8)
        gr = gr_ref[...].astype(f32)
        pr_ref[...] = (h_ref[...] * (gr * _sig(gr))).astype(bf16)

    tile = lambda cb: pl.BlockSpec((TL, D), lambda b, tt, cb=cb: (b * nt + tt, cb))
    prev = lambda cb: pl.BlockSpec(
        (PREV_ROWS, D), lambda b, tt, cb=cb: (jnp.maximum((b * nt + tt) * (TL // PREV_ROWS) - 1, 0), cb))
    vec = _whole((1, D))
    return pl.pallas_call(
        body, name="rnn_fwd", grid=(nb, nt),
        in_specs=[tile(1), prev(1), tile(2), _whole((4, D)), vec, _whole((D, D)), _whole((D, D)), vec, vec, vec],
        out_specs=[tile(0)] * 4,
        out_shape=[jax.ShapeDtypeStruct((t, D), f32)] * 3 + [jax.ShapeDtypeStruct((t, D), bf16)],
        scratch_shapes=[pltpu.VMEM((TL, D), f32), pltpu.VMEM((1, D), f32)],
        compiler_params=_params(("parallel", "arbitrary"), VMEM_LIMIT),
    )(rest, rest, rest, conv_w, conv_b, wa_d, wx_d, ba, bx, lam)


def _merge(mga, mgr, ya, yr):
    return (_sig(mga.astype(f32)) * ya.astype(f32) + _sig(mgr.astype(f32)) * yr.astype(f32)).astype(bf16)


def _out_proj_loss(rest, ya, yr, w_out, x, tgt, w_post):
    t = x.shape[0]

    def body(mga_ref, mgr_ref, ya_ref, yr_ref, wo_ref, x_ref, t_ref, w_ref, do_ref, dy_ref, mrg_ref, loss_ref, dwp_ref):
        @pl.when(pl.program_id(0) == 0)
        def _():
            loss_ref[...] = jnp.zeros_like(loss_ref)
            dwp_ref[...] = jnp.zeros_like(dwp_ref)

        mrg = _merge(mga_ref[...], mgr_ref[...], ya_ref[...], yr_ref[...])
        mrg_ref[...] = mrg
        ov = _dot(mrg, wo_ref[...])
        w = w_ref[...]
        r2 = lax.rsqrt(jnp.mean(ov * ov, axis=-1, keepdims=True) + NORM_EPS)
        oh = ov * r2
        e = x_ref[...] + oh * w - t_ref[...]
        loss_ref[...] += 0.5 * jnp.sum(jnp.mean(e * e, axis=-1, keepdims=True))
        dy = e * (1.0 / D)
        dy_ref[...] = dy
        dwp_ref[...] += jnp.sum(dy * oh, axis=0, keepdims=True)
        doh = dy * w
        do_ref[...] = (r2 * (doh - oh * jnp.mean(doh * oh, axis=-1, keepdims=True))).astype(bf16)

    return pl.pallas_call(
        body, name="out_proj_loss", grid=(t // TM,),
        in_specs=[_tile(TM, D, 3), _tile(TM, D, 4), _tile(TM, D), _tile(TM, D), _whole((D, D)), _tile(TM, D),
                  _tile(TM, D), _whole((1, D))],
        out_specs=[_tile(TM, D), _tile(TM, D), _tile(TM, D), _whole((8, LANES)), _whole((1, D))],
        out_shape=[jax.ShapeDtypeStruct((t, D), bf16), jax.ShapeDtypeStruct((t, D), f32),
                   jax.ShapeDtypeStruct((t, D), bf16), jax.ShapeDtypeStruct((8, LANES), f32),
                   jax.ShapeDtypeStruct((1, D), f32)],
        compiler_params=_params(("arbitrary",), VMEM_LIMIT),
    )(rest, rest, ya, yr, w_out, x, tgt, w_post)


def _out_bwd(do, rest, ya, yr, w_out):
    t = do.shape[0]

    def body(do_ref, mga_ref, mgr_ref, ya_ref, yr_ref, w_ref, dya_ref, dyr_ref, dmga_ref, dmgr_ref):
        sa, sr = _sig(mga_ref[...].astype(f32)), _sig(mgr_ref[...].astype(f32))
        ya, yr = ya_ref[...].astype(f32), yr_ref[...].astype(f32)
        dm = _dot_nt(do_ref[...], w_ref[...])
        dya_ref[...] = (dm * sa).astype(bf16)
        dyr_ref[...] = (dm * sr).astype(bf16)
        dmga_ref[...] = (dm * ya * sa * (1.0 - sa)).astype(bf16)
        dmgr_ref[...] = (dm * yr * sr * (1.0 - sr)).astype(bf16)

    return pl.pallas_call(
        body, name="out_bwd", grid=(t // TM,),
        in_specs=[_tile(TM, D), _tile(TM, D, 3), _tile(TM, D, 4), _tile(TM, D), _tile(TM, D), _whole((D, D))],
        out_specs=[_tile(TM, D)] * 4,
        out_shape=[jax.ShapeDtypeStruct((t, D), bf16)] * 4,
        compiler_params=_params(("parallel",), VMEM_LIMIT),
    )(do, rest, rest, ya, yr, w_out)


def _branch_bwd(name, dyb, rest, gate_cb, act, w, act_grad_dtype, head_sums=False):
    t = dyb.shape[0]

    def body(dy_ref, g_ref, act_ref, w_ref, dact_ref, dg_ref, *delta_ref):
        dp = _dot_nt(dy_ref[...], w_ref[...])
        g = g_ref[...].astype(f32)
        sg = _sig(g)
        act = act_ref[...]
        dact = (dp * (g * sg)).astype(act_grad_dtype)
        dact_ref[...] = dact
        dg_ref[...] = (dp * act * (sg * (1.0 + g * (1.0 - sg)))).astype(bf16)
        if head_sums:
            ch = lax.broadcasted_iota(jnp.int32, (D, LANES), 0)
            hd = lax.broadcasted_iota(jnp.int32, (D, LANES), 1)
            pick = (ch // 64 == hd).astype(bf16)
            per_head = sum(_dot(piece, pick) for piece in _split3(dact.astype(f32) * act))
            delta_ref[0][0] = per_head.T[:HEADS, :]

    out_specs = [_tile(TM, D), _tile(TM, D)]
    out_shape = [jax.ShapeDtypeStruct((t, D), act_grad_dtype), jax.ShapeDtypeStruct((t, D), bf16)]
    if head_sums:
        out_specs.append(pl.BlockSpec((1, HEADS, TM), lambda i: (i, 0, 0)))
        out_shape.append(jax.ShapeDtypeStruct((t // TM, HEADS, TM), f32))
    return pl.pallas_call(
        body, name=name, grid=(t // TM,),
        in_specs=[_tile(TM, D), _tile(TM, D, gate_cb), _tile(TM, D), _whole((D, D))],
        out_specs=out_specs, out_shape=out_shape,
        compiler_params=_params(("parallel",), VMEM_LIMIT),
    )(dyb, rest, act, w)


def _rnn_bwd(dh, a, h, xc, rest, conv_w, conv_b, wa_d, wx_d, ba, bx, lam, seq):
    t = dh.shape[0]
    nb, nt = t // seq, seq // TL
    diag = (D // LANES, LANES, LANES)

    def body(dh_ref, a_ref, h_ref, hprev_ref, xc_ref, x_ref, xprev_ref, cw_ref, cb_ref, wa_ref, wx_ref,
             ba_ref, bx_ref, lam_ref, dxr_ref, dwa_ref, dwx_ref, vec_ref, g_scr, dxc_scr, dxr_scr, qcarry, dxc_next):
        b, tt = pl.program_id(0), pl.program_id(1)
        rt = nt - 1 - tt

        @pl.when((b == 0) & (tt == 0))
        def _():
            dwa_ref[...] = jnp.zeros_like(dwa_ref)
            dwx_ref[...] = jnp.zeros_like(dwx_ref)
            vec_ref[...] = jnp.zeros_like(vec_ref)

        @pl.when(tt == 0)
        def _():
            qcarry[...] = jnp.zeros_like(qcarry)
            dxc_next[...] = jnp.zeros_like(dxc_next)

        def step(k, q):
            s = TL - 1 - k
            g = dh_ref[pl.ds(s, 1), :] + q
            g_scr[pl.ds(s, 1), :] = g
            return a_ref[pl.ds(s, 1), :] * g

        qcarry[...] = lax.fori_loop(0, TL, step, qcarry[...], unroll=8)

        row = lax.broadcasted_iota(jnp.int32, (TL, D), 0)
        row8 = lax.broadcasted_iota(jnp.int32, (8, D), 0)
        g = g_scr[...]
        av = a_ref[...]
        xc = xc_ref[...]
        hlast = jnp.where(rt > 0, hprev_ref[pl.ds(PREV_ROWS - 1, 1), :], 0.0)
        hp = jnp.where(row == 0, hlast, pltpu.roll(h_ref[...], 1, 0))
        r, i, sp, log_a, w1, sq = _rnn_gates(xc, wa_ref, wx_ref, ba_ref, bx_ref, lam_ref)
        dix = g * sq
        di = dix * xc
        dxc = dix * i
        dsq = g * (i * xc)
        dlog_a = g * hp * av - dsq * jnp.where(sq > 0.0, (1.0 - w1) / sq, 0.0)
        dpr = (dlog_a * ((-RG_C) * sp)) * r * (1.0 - r)
        dpi = di * i * (1.0 - i)
        dprb, dpib, xcb = dpr.astype(bf16), dpi.astype(bf16), xc.astype(bf16)
        dxc = dxc + _dot_nt(dprb, wa_ref[...]) + _dot_nt(dpib, wx_ref[...])
        for j in range(D // LANES):
            cols = slice(j * LANES, (j + 1) * LANES)
            dwa_ref[j] += _dot_tn(xcb[:, cols], dprb[:, cols])
            dwx_ref[j] += _dot_tn(xcb[:, cols], dpib[:, cols])
        vec_ref[pl.ds(0, 1), :] += jnp.sum(dpr, axis=0, keepdims=True)
        vec_ref[pl.ds(1, 1), :] += jnp.sum(dpi, axis=0, keepdims=True)
        dsp = jnp.sum(dlog_a * ((-RG_C) * r), axis=0, keepdims=True)
        vec_ref[pl.ds(2, 1), :] += dsp * (-_sig(-lam_ref[...]))
        vec_ref[pl.ds(3, 1), :] += jnp.sum(dxc, axis=0, keepdims=True)

        dxc_scr[...] = dxc
        bot8 = dxc_scr[pl.ds(TL - 8, 8), :]
        nxt8 = dxc_next[...]
        dxr = cw_ref[pl.ds(3, 1), :] * dxc
        dxr8 = cw_ref[pl.ds(3, 1), :] * bot8
        for sh in range(1, 4):
            w = cw_ref[pl.ds(3 - sh, 1), :]
            dxr = dxr + w * pltpu.roll(dxc, TL - sh, 0)
            dxr8 = dxr8 + w * jnp.where(row8 < 8 - sh, pltpu.roll(bot8, 8 - sh, 0), pltpu.roll(nxt8, 8 - sh, 0))
        dxr_scr[...] = dxr
        dxr_scr[pl.ds(TL - 8, 8), :] = dxr8
        dxr_ref[...] = dxr_scr[...].astype(bf16)
        dxc_next[...] = dxc_scr[pl.ds(0, 8), :]

        x = x_ref[...].astype(f32)
        prev8 = jnp.where(rt > 0, xprev_ref[...].astype(f32)[PREV_ROWS - 8:], 0.0)
        dxc_top8 = dxc_scr[pl.ds(0, 8), :]
        vec_ref[pl.ds(7, 1), :] += jnp.sum(dxc * x, axis=0, keepdims=True)
        for sh in range(1, 4):
            inside = jnp.sum(dxc * jnp.where(row >= sh, pltpu.roll(x, sh, 0), 0.0), axis=0, keepdims=True)
            above = jnp.sum(dxc_top8 * jnp.where(row8 < sh, pltpu.roll(prev8, sh, 0), 0.0), axis=0, keepdims=True)
            vec_ref[pl.ds(7 - sh, 1), :] += inside + above

    tile = lambda cb: pl.BlockSpec((TL, D), lambda b, tt, cb=cb: (b * nt + nt - 1 - tt, cb))
    prev = lambda cb: pl.BlockSpec(
        (PREV_ROWS, D), lambda b, tt, cb=cb: (jnp.maximum((b * nt + nt - 1 - tt) * (TL // PREV_ROWS) - 1, 0), cb))
    vec = _whole((1, D))
    return pl.pallas_call(
        body, name="rnn_bwd", grid=(nb, nt),
        in_specs=[tile(0), tile(0), tile(0), prev(0), tile(0), tile(1), prev(1),
                  _whole((4, D)), vec, _whole((D, D)), _whole((D, D)), vec, vec, vec],
        out_specs=[tile(0), _whole(diag), _whole(diag), _whole((8, D))],
        out_shape=[jax.ShapeDtypeStruct((t, D), bf16), jax.ShapeDtypeStruct(diag, f32),
                   jax.ShapeDtypeStruct(diag, f32), jax.ShapeDtypeStruct((8, D), f32)],
        scratch_shapes=[pltpu.VMEM((TL, D), f32), pltpu.VMEM((TL, D), f32), pltpu.VMEM((TL, D), f32),
                        pltpu.VMEM((1, D), f32), pltpu.VMEM((8, D), f32)],
        compiler_params=_params(("arbitrary", "arbitrary"), VMEM_LIMIT),
    )(dh, a, h, h, xc, rest, rest, conv_w, conv_b, wa_d, wx_d, ba, bx, lam)


def _attn_bwd(qa, ka, qkv, doa, lse, delta, seq):
    t = qkv.shape[0]
    nb, nq = t // seq, seq // TQ
    hg = ATT_GROUP
    ng, npair = HEADS // hg, hg // 2

    def body(qa_ref, ka_ref, q_ref, k_ref, v_ref, do_ref, lse_ref, dl_ref, dq_ref, dk_ref, dv_ref, dc_ref,
             dqt_scr, dk_scr, dv_scr, ds_scr, kht_scr):
        gi, kt = pl.program_id(1), pl.program_id(2)
        lane = lax.broadcasted_iota(jnp.int32, (1, LANES), 1)
        krow = lax.broadcasted_iota(jnp.int32, (TQ, TQ), 0)
        qcol = lax.broadcasted_iota(jnp.int32, (TQ, TQ), 1)
        lmask = [(lane // 64) == hh for hh in range(2)]
        chan = lax.broadcasted_iota(jnp.int32, (LANES, 1), 0)
        cmask = [(chan // 64) == hh for hh in range(2)]
        scale = jnp.asarray(QK_SCALE, bf16)

        @pl.when(kt == 0)
        def _():
            dqt_scr[...] = jnp.zeros_like(dqt_scr)

        dk_scr[...] = jnp.zeros_like(dk_scr)
        dv_scr[...] = jnp.zeros_like(dv_scr)
        ds_scr[...] = jnp.zeros_like(ds_scr)
        for g in range(hg):
            k2 = k_ref[:, pl.ds((g // 2) * LANES, LANES)]
            kht_scr[g] = jnp.where(lmask[g % 2], k2, jnp.zeros_like(k2)).T

        def q_step(qt, masked):
            qs = pl.multiple_of(qt * TQ, TQ)
            heads = range(hg)
            st = [_dot_nt(ka_ref[:, pl.ds(g * LANES, LANES)], qa_ref[pl.ds(qs, TQ), pl.ds(g * LANES, LANES)])
                  for g in heads]
            if masked:
                st = [jnp.where(krow <= qcol, s, MASK_VALUE) for s in st]
            do2t = [do_ref[pl.ds(qs, TQ), pl.ds(j * LANES, LANES)].T for j in range(npair)]
            doht = [jnp.where(cmask[g % 2], do2t[g // 2], jnp.zeros_like(do2t[0])) for g in heads]
            dp = [_dot(v_ref[:, pl.ds((g // 2) * LANES, LANES)], doht[g]) for g in heads]
            q2t = [q_ref[pl.ds(qs, TQ), pl.ds(j * LANES, LANES)].T * scale for j in range(npair)]
            qht = [jnp.where(cmask[g % 2], q2t[g // 2], jnp.zeros_like(q2t[0])) for g in heads]
            p = [jnp.exp(st[g] - lse_ref[qt, pl.ds(hg * gi + g, 1), :]) for g in heads]
            ds = [p[g] * (dp[g] - dl_ref[qt, pl.ds(hg * gi + g, 1), :]) for g in heads]
            pb = [x.astype(bf16) for x in p]
            dsb = [x.astype(bf16) for x in ds]
            for j in range(npair):
                a, b = 2 * j, 2 * j + 1
                dv_scr[j] += _dot_nt(doht[a], pb[a]) + _dot_nt(doht[b], pb[b])
                dk_scr[j] += _dot_nt(qht[a], dsb[a]) + _dot_nt(qht[b], dsb[b])
                dqt_scr[qt, j] += (_dot(kht_scr[a], dsb[a]) + _dot(kht_scr[b], dsb[b])) * QK_SCALE
            for g in heads:
                ds_scr[g] += ds[g][:, :LANES] + ds[g][:, LANES:]

        q_step(kt, True)

        def loop_body(qt, carry):
            q_step(qt, False)
            return carry

        lax.fori_loop(kt + 1, nq, loop_body, 0)

        dc = jnp.zeros((TQ, LANES), f32)
        for g in range(hg):
            dc = jnp.where(lane == g, -jnp.sum(ds_scr[g], axis=1, keepdims=True), dc)
        dc_ref[...] = dc
        for j in range(npair):
            dk_ref[:, pl.ds(j * LANES, LANES)] = dk_scr[j].T.astype(bf16)
            dv_ref[:, pl.ds(j * LANES, LANES)] = dv_scr[j].T.astype(bf16)

        @pl.when(kt == nq - 1)
        def _():
            for qt in range(nq):
                for j in range(npair):
                    dq_ref[pl.ds(qt * TQ, TQ), pl.ds(j * LANES, LANES)] = dqt_scr[qt, j].T.astype(bf16)

    vw = hg * 64
    seqspec = pl.BlockSpec((seq, vw), lambda b, gi, kt: (b, gi))
    kspec = lambda off: pl.BlockSpec((TQ, vw), lambda b, gi, kt: (b * nq + kt, off + gi))
    rowspec = pl.BlockSpec((nq, HEADS, TQ), lambda b, gi, kt: (b, 0, 0))
    return pl.pallas_call(
        body, name="attn_bwd", grid=(nb, ng, nq),
        in_specs=[pl.BlockSpec((seq, hg * LANES), lambda b, gi, kt: (b, gi)),
                  pl.BlockSpec((TQ, hg * LANES), lambda b, gi, kt: (b * nq + kt, gi)),
                  seqspec, kspec(ng), kspec(2 * ng), seqspec, rowspec, rowspec],
        out_specs=[seqspec, kspec(0), kspec(0), pl.BlockSpec((TQ, LANES), lambda b, gi, kt: (b * nq + kt, gi))],
        out_shape=[jax.ShapeDtypeStruct((t, D), bf16)] * 3 + [jax.ShapeDtypeStruct((t, ng * LANES), f32)],
        scratch_shapes=[pltpu.VMEM((nq, npair, LANES, TQ), f32), pltpu.VMEM((npair, LANES, TQ), f32),
                        pltpu.VMEM((npair, LANES, TQ), f32), pltpu.VMEM((hg, TQ, LANES), f32),
                        pltpu.VMEM((hg, LANES, TQ), bf16)],
        compiler_params=_params(("parallel", "parallel", "arbitrary"), VMEM_LIMIT),
    )(qa, ka, qkv, qkv, qkv, doa, lse, delta)


def _forget_bwd(dc, f128, seq):
    t = f128.shape[0]
    nb = seq // LANES

    def body(dc_ref, f_ref, df_ref, dbf_ref):
        @pl.when(pl.program_id(0) == 0)
        def _():
            dbf_ref[...] = jnp.zeros_like(dbf_ref)

        r = lax.broadcasted_iota(jnp.int32, (LANES, LANES), 0)
        cidx = lax.broadcasted_iota(jnp.int32, (LANES, LANES), 1)
        tri = (r <= cidx).astype(f32)
        carry = jnp.zeros((1, LANES), f32)
        total = jnp.zeros((1, LANES), f32)
        for blk in reversed(range(nb)):
            dcb = dc_ref[pl.ds(blk * LANES, LANES), :]
            dlf = jnp.dot(tri, dcb, preferred_element_type=f32, precision=lax.Precision.HIGHEST) + carry
            df = dlf * _sig(-f_ref[pl.ds(blk * LANES, LANES), :])
            df_ref[pl.ds(blk * LANES, LANES), :] = df.astype(bf16)
            total = total + jnp.sum(df, axis=0, keepdims=True)
            carry = carry + jnp.sum(dcb, axis=0, keepdims=True)
        dbf_ref[...] += total

    return pl.pallas_call(
        body, name="forget_bwd", grid=(t // seq,),
        in_specs=[pl.BlockSpec((seq, LANES), lambda b: (b, 0)), pl.BlockSpec((seq, LANES), lambda b: (b, 0))],
        out_specs=[pl.BlockSpec((seq, LANES), lambda b: (b, 0)), _whole((1, LANES))],
        out_shape=[jax.ShapeDtypeStruct((t, LANES), bf16), jax.ShapeDtypeStruct((1, LANES), f32)],
        compiler_params=_params(("arbitrary",)),
    )(dc, f128)


def _in_bwd(dz, df, x, dy, w_qkv, w_rest, w_f, w_pre):
    t = x.shape[0]
    n_qkv = w_qkv.shape[0] // D
    n_rest = w_rest.shape[0] // D

    def body(*refs):
        dz_refs = refs[:n_qkv + n_rest]
        df_ref, x_ref, dy_ref, wq_ref, wr_ref, wf_ref, wp_ref, gx_ref, dwp_ref = refs[n_qkv + n_rest:]

        @pl.when(pl.program_id(0) == 0)
        def _():
            dwp_ref[...] = jnp.zeros_like(dwp_ref)

        dh = _dot(df_ref[...], wf_ref[...])
        for p in range(n_qkv):
            dh = dh + _dot(dz_refs[p][...], wq_ref[pl.ds(p * D, D), :])
        for p in range(n_rest):
            dh = dh + _dot(dz_refs[n_qkv + p][...], wr_ref[pl.ds(p * D, D), :])
        xv = x_ref[...]
        r1 = lax.rsqrt(jnp.mean(xv * xv, axis=-1, keepdims=True) + NORM_EPS)
        xh = xv * r1
        dwp_ref[...] += jnp.sum(dh * xh, axis=0, keepdims=True)
        dxh = dh * wp_ref[...]
        gx_ref[...] = dy_ref[...] + r1 * (dxh - xh * jnp.mean(dxh * xh, axis=-1, keepdims=True))

    once = lambda shape: pl.BlockSpec(shape, lambda i: (0, 0), pipeline_mode=pl.Buffered(1))
    return pl.pallas_call(
        body, name="in_bwd", grid=(t // TM,),
        in_specs=[_tile(TM, D)] * (n_qkv + n_rest) + [_tile(TM, LANES), _tile(TM, D), _tile(TM, D),
                  once(w_qkv.shape), once(w_rest.shape), once(w_f.shape), _whole((1, D))],
        out_specs=[_tile(TM, D), _whole((1, D))],
        out_shape=[jax.ShapeDtypeStruct((t, D), f32), jax.ShapeDtypeStruct((1, D), f32)],
        compiler_params=_params(("arbitrary",), VMEM_LIMIT),
    )(*dz, df, x, dy, w_qkv, w_rest, w_f, w_pre)


def _tn_mm(name, a, b, tn, tk=4096):
    t, k = a.shape
    tk = min(tk, t)
    n = b.shape[1]

    def body(a_ref, b_ref, o_ref, s_ref):
        j, kk = pl.program_id(0), pl.program_id(1)

        @pl.when(kk == 0)
        def _():
            o_ref[...] = jnp.zeros_like(o_ref)

        @pl.when((j == 0) & (kk == 0))
        def _():
            s_ref[...] = jnp.zeros_like(s_ref)

        av = a_ref[...]
        o_ref[...] += _dot_tn(av, b_ref[...])

        @pl.when(j == 0)
        def _():
            s_ref[...] += jnp.sum(av.astype(f32), axis=0, keepdims=True)

    return pl.pallas_call(
        body, name=name, grid=(n // tn, t // tk),
        in_specs=[pl.BlockSpec((tk, k), lambda j, kk: (kk, 0)), pl.BlockSpec((tk, tn), lambda j, kk: (kk, j))],
        out_specs=[pl.BlockSpec((k, tn), lambda j, kk: (0, j)), _whole((1, k))],
        out_shape=[jax.ShapeDtypeStruct((k, n), f32), jax.ShapeDtypeStruct((1, k), f32)],
        compiler_params=_params(("arbitrary", "arbitrary"), VMEM_LIMIT),
    )(a, b)


def _position():
    return lax.axis_index("x"), lax.axis_index("y"), lax.axis_index("c")


def _gather_shards(parts, small):
    n = len(parts)
    halves = [p.shape[0] // 2 for p in parts]

    def body(*refs):
        srcs, small_src = refs[:n], refs[n]
        dsts, small_dst = refs[n + 1:2 * n + 1], refs[2 * n + 1]
        send, recv, local = refs[2 * n + 2:]
        x, y, c = _position()
        me = 2 * x + y
        chips = [(1 - x, y), (x, 1 - y), (1 - x, 1 - y)]
        ids = [2 * px + py for px, py in chips]

        def half(a, shard, which):
            return dsts[a].at[shard, pl.ds(which * halves[a], halves[a]), :]

        def over_ici(a, j, shard):
            px, py = chips[j]
            return pltpu.make_async_remote_copy(
                src_ref=srcs[a].at[pl.ds(c * halves[a], halves[a]), :], dst_ref=half(a, shard, c),
                send_sem=send.at[a * 3 + j], recv_sem=recv.at[a * 3 + j], device_id=(px, py, c), device_id_type=MESH)

        def to_sibling(a, j, which):
            k = 3 * n + a * 3 + j
            return pltpu.make_async_remote_copy(
                src_ref=half(a, ids[j], which), dst_ref=half(a, ids[j], which), send_sem=send.at[k],
                recv_sem=recv.at[k], device_id=(x, y, 1 - c), device_id_type=MESH)

        def small_copy(j, shard):
            px, py = chips[j]
            return pltpu.make_async_remote_copy(
                src_ref=small_src, dst_ref=small_dst.at[shard], send_sem=send.at[6 * n + j], recv_sem=recv.at[6 * n + j],
                device_id=(px, py, c), device_id_type=MESH)

        own = [pltpu.make_async_copy(srcs[a], dsts[a].at[me], local.at[a]) for a in range(n)]
        own.append(pltpu.make_async_copy(small_src, small_dst.at[me], local.at[n]))
        for cp in own:
            cp.start()
        first = [over_ici(a, j, me) for j in range(3) for a in range(n)] + [small_copy(j, me) for j in range(3)]
        for cp in first:
            cp.start()
        passed = []
        for j in range(3):
            for a in range(n):
                over_ici(a, j, ids[j]).wait_recv()
                passed.append(to_sibling(a, j, c))
                passed[-1].start()
        for j in range(3):
            small_copy(j, ids[j]).wait_recv()
            for a in range(n):
                to_sibling(a, j, 1 - c).wait_recv()
        for cp in first + passed:
            cp.wait_send()
        for cp in own:
            cp.wait()

    vm = pl.BlockSpec(memory_space=pltpu.VMEM)
    return pl.pallas_call(
        body, name="gather_shards",
        in_specs=[vm] * (n + 1), out_specs=[vm] * (n + 1),
        out_shape=[jax.ShapeDtypeStruct((N_CHIPS,) + p.shape, p.dtype) for p in parts + [small]],
        scratch_shapes=[pltpu.SemaphoreType.DMA((6 * n + 3,)), pltpu.SemaphoreType.DMA((6 * n + 3,)),
                        pltpu.SemaphoreType.DMA((n + 1,))],
        compiler_params=pltpu.CompilerParams(vmem_limit_bytes=VMEM_LIMIT),
    )(*parts, small)


def _allsum_rows(part):
    rows_n = part.shape[0]

    def body(x_ref, gath_ref, sum_ref, send_sems, recv_sems, local_sem):
        x, y, c = _position()
        me, sibling = (x, y, c), (x, y, 1 - c)
        chips = [(1 - x, y), (x, 1 - y), (1 - x, 1 - y)]

        def rows(px, py, pc):
            return gath_ref.at[pl.ds((4 * px + 2 * py + pc) * rows_n, rows_n), :]

        def copy(k, block, to, src=None):
            return pltpu.make_async_remote_copy(
                src_ref=rows(*block) if src is None else src, dst_ref=rows(*block),
                send_sem=send_sems.at[k], recv_sem=recv_sems.at[k], device_id=to, device_id_type=MESH)

        mine = pltpu.make_async_copy(x_ref, rows(*me), local_sem)
        mine.start()
        first = [copy(0, me, sibling, src=x_ref)]
        first += [copy(1 + j, me, (*chip, c), src=x_ref) for j, chip in enumerate(chips)]
        for cp in first:
            cp.start()
        passed = [copy(4 + j, (*chip, c), sibling) for j, chip in enumerate(chips)]
        for j, chip in enumerate(chips):
            copy(1 + j, (*chip, c), me).wait_recv()
            passed[j].start()
        copy(0, sibling, me).wait_recv()
        for j, chip in enumerate(chips):
            copy(4 + j, (*chip, 1 - c), me).wait_recv()
        for cp in first + passed:
            cp.wait_send()
        mine.wait()
        total = gath_ref[pl.ds(0, rows_n), :]
        for d in range(1, N_DEV):
            total = total + gath_ref[pl.ds(d * rows_n, rows_n), :]
        sum_ref[...] = total

    vm = pl.BlockSpec(memory_space=pltpu.VMEM)
    return pl.pallas_call(
        body, name="allsum_rows", in_specs=[vm], out_specs=[vm, vm],
        out_shape=[jax.ShapeDtypeStruct((N_DEV * rows_n, D), f32), jax.ShapeDtypeStruct((rows_n, D), f32)],
        scratch_shapes=[pltpu.SemaphoreType.DMA((7,)), pltpu.SemaphoreType.DMA((7,)), pltpu.SemaphoreType.DMA],
    )(part)[1]


PAIR_ROWS = 16


def _pair_reduce(name, pieces):
    _, r, n = pieces.shape

    def body(p_ref, o_ref, land, send, recv):
        x, y, c = _position()

        def remote(j, half):
            return pltpu.make_async_remote_copy(
                src_ref=p_ref.at[2 * j + half], dst_ref=land.at[j], send_sem=send.at[j], recv_sem=recv.at[j],
                device_id=(x, y, 1 - c), device_id_type=MESH)

        sends = [remote(j, 1 - c) for j in range(N_CHIPS)]
        for cp in sends:
            cp.start()
        for j in range(N_CHIPS):
            remote(j, c).wait_recv()

            def add_rows(i, carry, j=j):
                rows = pl.ds(pl.multiple_of(i * PAIR_ROWS, PAIR_ROWS), PAIR_ROWS)
                o_ref[j, rows, :] = (p_ref[2 * j + c, rows, :].astype(f32) + land[j, rows, :].astype(f32)).astype(bf16)
                return carry

            lax.fori_loop(0, r // PAIR_ROWS, add_rows, 0)
        for cp in sends:
            cp.wait_send()

    vm = pl.BlockSpec(memory_space=pltpu.VMEM)
    return pl.pallas_call(
        body, name=name, in_specs=[vm], out_specs=vm,
        out_shape=jax.ShapeDtypeStruct((N_CHIPS, r, n), bf16),
        scratch_shapes=[pltpu.VMEM((N_CHIPS, r, n), bf16), pltpu.SemaphoreType.DMA((N_CHIPS,)),
                        pltpu.SemaphoreType.DMA((N_CHIPS,))],
        compiler_params=pltpu.CompilerParams(vmem_limit_bytes=VMEM_LIMIT),
    )(pieces)


def _chip_exchange(arrs):
    n = len(arrs)

    def body(*refs):
        srcs, dsts = refs[:n], refs[n:2 * n]
        send, recv, local = refs[2 * n:]
        x, y, c = _position()
        me = 2 * x + y
        chips = [(1 - x, y), (x, 1 - y), (1 - x, 1 - y)]

        def remote(a, j, piece, landing):
            px, py = chips[j]
            return pltpu.make_async_remote_copy(
                src_ref=srcs[a].at[piece], dst_ref=dsts[a].at[landing], send_sem=send.at[a * 3 + j],
                recv_sem=recv.at[a * 3 + j], device_id=(px, py, c), device_id_type=MESH)

        own = [pltpu.make_async_copy(srcs[a].at[me], dsts[a].at[me], local.at[a]) for a in range(n)]
        sends = [remote(a, j, 2 * px + py, me) for j, (px, py) in enumerate(chips) for a in range(n)]
        for cp in sends + own:
            cp.start()
        for j, (px, py) in enumerate(chips):
            for a in range(n):
                remote(a, j, me, 2 * px + py).wait_recv()
        for cp in sends:
            cp.wait_send()
        for cp in own:
            cp.wait()

    anyspec = pl.BlockSpec(memory_space=pl.ANY)
    return pl.pallas_call(
        body, name="chip_exchange", in_specs=[anyspec] * n, out_specs=[anyspec] * n,
        out_shape=[jax.ShapeDtypeStruct(a.shape, a.dtype) for a in arrs],
        scratch_shapes=[pltpu.SemaphoreType.DMA((3 * n,)), pltpu.SemaphoreType.DMA((3 * n,)),
                        pltpu.SemaphoreType.DMA((n,))],
    )(*arrs)


def _swap_halves(arrs):
    n = len(arrs)

    def body(*refs):
        srcs, dsts = refs[:n], refs[n:2 * n]
        send, recv, local = refs[2 * n:]
        x, y, c = _position()

        def remote(a, landing):
            return pltpu.make_async_remote_copy(
                src_ref=srcs[a], dst_ref=dsts[a].at[landing], send_sem=send.at[a], recv_sem=recv.at[a],
                device_id=(x, y, 1 - c), device_id_type=MESH)

        own = [pltpu.make_async_copy(srcs[a], dsts[a].at[c], local.at[a]) for a in range(n)]
        sends = [remote(a, c) for a in range(n)]
        for cp in sends + own:
            cp.start()
        for a in range(n):
            remote(a, 1 - c).wait_recv()
        for cp in sends:
            cp.wait_send()
        for cp in own:
            cp.wait()

    vm = pl.BlockSpec(memory_space=pltpu.VMEM)
    return pl.pallas_call(
        body, name="swap_halves", in_specs=[vm] * n, out_specs=[vm] * n,
        out_shape=[jax.ShapeDtypeStruct((2,) + a.shape, a.dtype) for a in arrs],
        scratch_shapes=[pltpu.SemaphoreType.DMA((n,)), pltpu.SemaphoreType.DMA((n,)), pltpu.SemaphoreType.DMA((n,))],
        compiler_params=pltpu.CompilerParams(vmem_limit_bytes=VMEM_LIMIT),
    )(*arrs)


def _row_block(r):
    return 128 if r % 128 == 0 else r


def _sum_slots(name, slots):
    s, r, n = slots.shape
    rb = _row_block(r)

    def body(s_ref, o_ref):
        total = s_ref[0].astype(f32)
        for d in range(1, s):
            total = total + s_ref[d].astype(f32)
        o_ref[...] = total

    return pl.pallas_call(
        body, name=name, grid=(r // rb,),
        in_specs=[pl.BlockSpec((s, rb, n), lambda i: (0, i, 0))],
        out_specs=pl.BlockSpec((rb, n), lambda i: (i, 0)),
        out_shape=jax.ShapeDtypeStruct((r, n), f32),
        compiler_params=_params(("parallel",), VMEM_LIMIT),
    )(slots)


def _adamw(name, w, g, m, v):
    r, n = w.shape
    if r % 128 == 0 or r * n <= 128 * 1024:
        rb, nb = _row_block(r), n
    else:
        rb, nb = r, LANES

    def body(w_ref, g_ref, m_ref, v_ref, d_ref, nm_ref, nv_ref):
        gv = g_ref[...]
        m2 = ADAM_B1 * m_ref[...] + (1.0 - ADAM_B1) * gv
        v2 = ADAM_B2 * v_ref[...] + (1.0 - ADAM_B2) * (gv * gv)
        m_hat = m2 / (1.0 - ADAM_B1 ** ADAM_STEP)
        v_hat = v2 / (1.0 - ADAM_B2 ** ADAM_STEP)
        d_ref[...] = (-ADAM_LR) * (m_hat / (jnp.sqrt(v_hat) + ADAM_EPS) + ADAM_WD * w_ref[...])
        nm_ref[...] = m2
        nv_ref[...] = v2

    spec = pl.BlockSpec((rb, nb), lambda i, j: (i, j))
    return pl.pallas_call(
        body, name=name, grid=(r // rb, n // nb), in_specs=[spec] * 4, out_specs=[spec] * 3,
        out_shape=[jax.ShapeDtypeStruct((r, n), f32)] * 3,
        compiler_params=_params(("parallel", "parallel"), VMEM_LIMIT),
    )(w, g, m, v)


def _local_step(x2, tgt2, seq, wt):
    nb = x2.shape[0] // seq
    h = _prenorm(x2, wt["pre_w"])
    qkv = _mm("in_qkv", h, wt["w_qkv"], wt["b_qkv"], bf16, 1024, 1024, w_is_nk=True)
    rest = _mm("in_rest", h, wt["w_rest"], wt["b_rest"], bf16, 1024, 1024, w_is_nk=True)
    f128 = _mm("in_f", h, wt["w_f"], wt["b_f"], f32, 1024, LANES, w_is_nk=True)
    c = _forget_prep(f128, seq)
    qa, ka = _attn_prep(qkv, c)
    o_att, pa, lse = _attn_fwd(qa, ka, qkv, rest, seq)
    ya = _mm("proj_a", pa, wt["w_a"], None, bf16, 1024, D)
    rnn_w = (wt["conv_w"], wt["conv_b"], wt["wa_d"], wt["wx_d"], wt["ba"], wt["bx"], wt["lam"])
    xc, a, hrec, pr = _rnn_fwd(rest, *rnn_w, seq)
    yr = _mm("proj_r", pr, wt["w_r"], None, bf16, 1024, D)
    do, dy, mrg, loss8, d_post = _out_proj_loss(rest, ya, yr, wt["w_o"], x2, tgt2, wt["post_w"])
    dya, dyr, dmga, dmgr = _out_bwd(do, rest, ya, yr, wt["w_o"])
    doa, dga, delta = _branch_bwd("branch_a_bwd", dya, rest, 0, o_att, wt["w_a"], bf16, head_sums=True)
    dhrec, dgr = _branch_bwd("branch_r_bwd", dyr, rest, 2, hrec, wt["w_r"], f32)
    d_wo, _ = _tn_mm("dw_out", mrg, do, 512)
    d_wa, _ = _tn_mm("dw_branch_a", pa, dya, 512)
    d_wr, _ = _tn_mm("dw_branch_r", pr, dyr, 512)
    dxr, d_wad, d_wxd, vec = _rnn_bwd(dhrec, a, hrec, xc, rest, *rnn_w, seq)
    dq, dk, dv, dc_pairs = _attn_bwd(qa, ka, qkv, doa, lse, delta, seq)
    dc = dc_pairs.reshape(-1, HEADS // ATT_GROUP, LANES)[:, :, :ATT_GROUP].reshape(-1, HEADS)
    df, db_f = _forget_bwd(_pad_cols(dc, LANES), f128, seq)
    pieces = [dq, dk, dv, dga, dxr, dgr, dmga, dmgr]
    gx, d_pre = _in_bwd(pieces, df, x2, dy, wt["w_qkv"], wt["w_rest"], wt["w_f"], wt["pre_w"])
    names = ["q", "k", "v", "ga", "xr", "gr", "mga", "mgr"]
    dws, dbs = [], []
    for nm, piece in zip(names, pieces):
        dw_p, db_p = _tn_mm("dw_in_" + nm, piece, h, 512)
        dws.append(dw_p)
        dbs.append(db_p)
    dw_f, _ = _tn_mm("dw_in_f", df, h, 512)
    zeros_w = jnp.zeros((IN_TOTAL - IN_USED, D), f32)
    d_w_in = jnp.concatenate(dws[:3] + [dw_f[:HEADS]] + dws[3:] + [zeros_w], axis=0)
    d_b_in = jnp.concatenate(dbs[:3] + [db_f[:, :HEADS]] + dbs[3:] + [zeros_w[:, :1].T], axis=1)
    return dict(loss=loss8[0, 0], grad_x=gx, pre_w=d_pre, w_in=d_w_in, b_in=d_b_in, conv_w=vec[4:8], conv_b=vec[3:4],
                wa_d=d_wad, ba=vec[0:1], wx_d=d_wxd, bx=vec[1:2], lam=vec[2:3], w_a=d_wa, w_r=d_wr, w_o=d_wo,
                post_w=d_post)


def _block_diag(w):
    g, bw, _ = w.shape
    eye = jnp.eye(g, dtype=w.dtype)
    return (w[:, :, None, :] * eye[:, None, :, None]).reshape(g * bw, g * bw)


def _gate_blocks(diag):
    half = diag.shape[1] // 2
    return jnp.stack([diag[:, :half, :half], diag[:, half:, half:]], axis=1).reshape(-1, half, half)


def _pad_cols(a, n):
    return jnp.pad(a, ((0, 0), (0, n - a.shape[1])))


def _pad_rows(a, n):
    return jnp.pad(a, ((0, n - a.shape[0]), (0, 0)))


def kernel(x, pre_norm_w, w_in, b_in, conv_w, conv_b, rg_wa, rg_ba, rg_wx, rg_bx, rg_lambda, w_branch_a, w_branch_r, w_out, post_norm_w, loss_target, m_pre_norm_w, m_w_in, m_b_in, m_conv_w, m_conv_b, m_rg_wa, m_rg_ba, m_rg_wx, m_rg_bx, m_rg_lambda, m_w_branch_a, m_w_branch_r, m_w_out, m_post_norm_w, v_pre_norm_w, v_w_in, v_b_in, v_conv_w, v_conv_b, v_rg_wa, v_rg_ba, v_rg_wx, v_rg_bx, v_rg_lambda, v_w_branch_a, v_w_branch_r, v_w_out, v_post_norm_w):
    nb, seq, _ = x.shape
    chip = 2 * lax.axis_index("x") + lax.axis_index("y")
    n_groups = rg_wa.shape[1]

    w_in_t = jnp.transpose(w_in[0])
    shard_cols = w_in_t.shape[0]
    padded = -(-shard_cols // 32) * 32
    g_in, g_a, g_r, g_o, g_cw = _gather_shards(
        [_pad_rows(w_in_t.astype(bf16), padded), w_branch_a[0].astype(bf16), w_branch_r[0].astype(bf16),
         w_out[0].astype(bf16)], conv_w[0])
    w_full = jnp.concatenate([g_in[j, :shard_cols] for j in range(N_CHIPS)], axis=0)
    q_end, f_end = 3 * D, 3 * D + HEADS
    wt = dict(
        pre_w=pre_norm_w, post_w=post_norm_w,
        w_qkv=w_full[:q_end], b_qkv=b_in[:, :q_end],
        w_f=_pad_rows(w_full[q_end:f_end], LANES), b_f=_pad_cols(b_in[:, q_end:f_end], LANES),
        w_rest=w_full[f_end:IN_USED], b_rest=b_in[:, f_end:IN_USED],
        w_a=g_a.reshape(D, D), w_r=g_r.reshape(D, D), w_o=g_o.reshape(D, D),
        conv_w=jnp.transpose(g_cw, (1, 0, 2)).reshape(4, D), conv_b=conv_b,
        wa_d=_block_diag(rg_wa[0]).astype(bf16), wx_d=_block_diag(rg_wx[0]).astype(bf16),
        ba=rg_ba, bx=rg_bx, lam=rg_lambda)

    part = _local_step(x.reshape(nb * seq, D), loss_target.reshape(nb * seq, D), seq, wt)
    loss = lax.psum(part["loss"], ("x", "y", "c"))
    grad_x = part["grad_x"].reshape(nb, seq, D)

    small = jnp.concatenate([
        part["pre_w"], _pad_cols(part["b_in"], 10 * D).reshape(10, D), part["conv_b"],
        _gate_blocks(part["wa_d"]).reshape(-1, D), part["ba"],
        _gate_blocks(part["wx_d"]).reshape(-1, D), part["bx"], part["lam"], part["post_w"],
        part["conv_w"]], axis=0)
    n_small = small.shape[0]
    n_rep = n_small - 4
    tot = _allsum_rows(_pad_rows(small, -(-n_small // 8) * 8))
    g_rep = tot[:n_rep]
    g_conv_w = lax.dynamic_slice_in_dim(tot[n_rep:n_small], chip * (D // N_CHIPS), D // N_CHIPS, axis=1)

    def pack(pre, b, cb, wa, ba, wx, bx, lam, post):
        return jnp.concatenate([pre, _pad_cols(b, 10 * D).reshape(10, D), cb, wa.reshape(-1, D), ba,
                                wx.reshape(-1, D), bx, lam, post], axis=0)

    def unpack(p):
        o = [0]

        def take(k):
            o[0] += k
            return p[o[0] - k:o[0]]

        pre = take(1)
        b = take(10).reshape(1, 10 * D)[:, :IN_TOTAL]
        cb = take(1)
        wa = take(64).reshape(rg_wa.shape)
        ba = take(1)
        wx = take(64).reshape(rg_wx.shape)
        bx = take(1)
        lam = take(1)
        post = take(1)
        return dict(pre_norm_w=pre, b_in=b, conv_b=cb, rg_wa=wa, rg_ba=ba, rg_wx=wx, rg_bx=bx, rg_lambda=lam,
                    post_norm_w=post)

    w_rep = pack(pre_norm_w, b_in, conv_b, rg_wa, rg_ba, rg_wx, rg_bx, rg_lambda, post_norm_w)
    m_rep = pack(m_pre_norm_w, m_b_in, m_conv_b, m_rg_wa, m_rg_ba, m_rg_wx, m_rg_bx, m_rg_lambda, m_post_norm_w)
    v_rep = pack(v_pre_norm_w, v_b_in, v_conv_b, v_rg_wa, v_rg_ba, v_rg_wx, v_rg_bx, v_rg_lambda, v_post_norm_w)
    d_rep, nm_rep, nv_rep = _adamw("adamw_rep", w_rep, g_rep, m_rep, v_rep)
    grads, deltas, new_m, new_v = unpack(g_rep), unpack(d_rep), unpack(nm_rep), unpack(nv_rep)

    p_in = jnp.pad(part["w_in"].reshape(N_CHIPS, shard_cols, D), ((0, 0), (0, padded - shard_cols), (0, 0)))
    p_in = p_in.reshape(N_DEV, padded // 2, D)
    p_aro = jnp.concatenate([part[k].reshape(N_DEV, D // N_DEV, D) for k in ("w_a", "w_r", "w_o")], axis=1)
    s_in, s_aro = _chip_exchange([_pair_reduce("pair_w_in", p_in.astype(bf16)),
                                  _pair_reduce("pair_w_aro", p_aro.astype(bf16))])
    f_in, f_aro = _swap_halves([_sum_slots("sum_w_in", s_in), _sum_slots("sum_w_aro", s_aro)])
    g_w_in_t = f_in.reshape(padded, D)[:shard_cols]
    rows = D // N_DEV
    g_aro = [f_aro[:, i * rows:(i + 1) * rows, :].reshape(2 * rows, D) for i in range(3)]

    w_in_upd = _adamw("adamw_w_in", w_in_t, g_w_in_t, jnp.transpose(m_w_in[0]), jnp.transpose(v_w_in[0]))
    g_w_in, d_w_in, nm_w_in, nv_w_in = [jnp.transpose(a) for a in (g_w_in_t, *w_in_upd)]
    upd_a = _adamw("adamw_w_branch_a", w_branch_a[0], g_aro[0], m_w_branch_a[0], v_w_branch_a[0])
    upd_r = _adamw("adamw_w_branch_r", w_branch_r[0], g_aro[1], m_w_branch_r[0], v_w_branch_r[0])
    upd_o = _adamw("adamw_w_out", w_out[0], g_aro[2], m_w_out[0], v_w_out[0])
    d_aro, nm_aro, nv_aro = zip(upd_a, upd_r, upd_o)
    d_cw, nm_cw, nv_cw = _adamw("adamw_conv_w", conv_w[0], g_conv_w, m_conv_w[0], v_conv_w[0])

    def sharded(t_in, t_aro, t_cw):
        return dict(w_in=t_in[None], conv_w=t_cw[None], w_branch_a=t_aro[0][None], w_branch_r=t_aro[1][None],
                    w_out=t_aro[2][None])

    order = ["pre_norm_w", "w_in", "b_in", "conv_w", "conv_b", "rg_wa", "rg_ba", "rg_wx", "rg_bx", "rg_lambda",
             "w_branch_a", "w_branch_r", "w_out", "post_norm_w"]
    outs = [loss, grad_x]
    for rep, shd in ((grads, sharded(g_w_in, g_aro, g_conv_w)), (deltas, sharded(d_w_in, d_aro, d_cw)),
                     (new_m, sharded(nm_w_in, nm_aro, nm_cw)), (new_v, sharded(nv_w_in, nv_aro, nv_cw))):
        both = {**rep, **shd}
        outs.extend(both[k] for k in order)
    return tuple(outs)
```

```python
import jax
import jax.numpy as jnp
from jax import lax
from jax.experimental import pallas as pl
from jax.experimental.pallas import tpu as pltpu

f32 = jnp.float32
bf16 = jnp.bfloat16

D = 1024
HEADS = 16
HEAD_PAIRS = 8
LANES = 128
NORM_EPS = 1e-6
MASK_VALUE = -1e30
RG_C = 8.0
QK_SCALE = 0.125
TQ = 256
ATT_GROUP = 8
ATT_GROUP_FWD = 16
TL = 256
TM = 256
PREV_ROWS = 16
IN_USED = 8 * D + HEADS
IN_TOTAL = 9 * D + HEADS
N_CHIPS = 4
N_DEV = 8
ADAM_LR, ADAM_B1, ADAM_B2, ADAM_EPS, ADAM_WD, ADAM_STEP = 0.001, 0.9, 0.999, 1e-08, 0.01, 10
VMEM_LIMIT = 56 * 1024 * 1024
MESH = pl.DeviceIdType.MESH


def _dot(a, b):
    return jnp.dot(a, b, preferred_element_type=f32)


def _dot_nt(a, b):
    return lax.dot_general(a, b, (((1,), (1,)), ((), ())), preferred_element_type=f32)


def _dot_tn(a, b):
    return lax.dot_general(a, b, (((0,), (0,)), ((), ())), preferred_element_type=f32)


def _sig(x):
    return 0.5 * jnp.tanh(0.5 * x) + 0.5


def _softplus(x):
    return jnp.maximum(x, 0.0) + jnp.log(1.0 + jnp.exp(-jnp.abs(x)))


def _params(sem, vmem=None):
    return pltpu.CompilerParams(dimension_semantics=sem, vmem_limit_bytes=vmem)


def _tile(tm, width, cb=0):
    return pl.BlockSpec((tm, width), lambda i, cb=cb: (i, cb))


def _whole(shape):
    nd = len(shape)
    return pl.BlockSpec(shape, lambda *_: (0,) * nd)


def _prenorm(x, w_pre):
    t = x.shape[0]

    def body(x_ref, w_ref, h_ref):
        xv = x_ref[...]
        r = lax.rsqrt(jnp.mean(xv * xv, axis=-1, keepdims=True) + NORM_EPS)
        h_ref[...] = (xv * r * w_ref[...]).astype(bf16)

    return pl.pallas_call(
        body, name="prenorm", grid=(t // TM,),
        in_specs=[_tile(TM, D), _whole((1, D))], out_specs=_tile(TM, D),
        out_shape=jax.ShapeDtypeStruct((t, D), bf16),
        compiler_params=_params(("parallel",)),
    )(x, w_pre)


def _mm(name, a, w, bias, out_dtype, tm, tn, w_is_nk=False):
    t, k = a.shape
    tm = min(tm, t)
    n = w.shape[0] if w_is_nk else w.shape[1]

    def body(a_ref, w_ref, *refs):
        acc = _dot_nt(a_ref[...], w_ref[...]) if w_is_nk else _dot(a_ref[...], w_ref[...])
        if bias is not None:
            acc = acc + refs[0][...]
        refs[-1][...] = acc.astype(out_dtype)

    in_specs = [pl.BlockSpec((tm, k), lambda i, j: (i, 0)),
                pl.BlockSpec((tn, k), lambda i, j: (j, 0)) if w_is_nk else pl.BlockSpec((k, tn), lambda i, j: (0, j))]
    args = [a, w]
    if bias is not None:
        in_specs.append(pl.BlockSpec((1, tn), lambda i, j: (0, j)))
        args.append(bias)
    return pl.pallas_call(
        body, name=name, grid=(t // tm, n // tn), in_specs=in_specs,
        out_specs=pl.BlockSpec((tm, tn), lambda i, j: (i, j)), out_shape=jax.ShapeDtypeStruct((t, n), out_dtype),
        compiler_params=_params(("parallel", "parallel"), VMEM_LIMIT),
    )(*args)


def _forget_prep(f128, seq):
    t = f128.shape[0]
    nb = seq // LANES

    def body(f_ref, c_ref):
        r = lax.broadcasted_iota(jnp.int32, (LANES, LANES), 0)
        cidx = lax.broadcasted_iota(jnp.int32, (LANES, LANES), 1)
        tri = (r >= cidx).astype(f32)
        carry = jnp.zeros((1, LANES), f32)
        for blk in range(nb):
            fv = f_ref[pl.ds(blk * LANES, LANES), :]
            lf = -_softplus(-fv)
            c_ref[pl.ds(blk * LANES, LANES), :] = (
                jnp.dot(tri, lf, preferred_element_type=f32, precision=lax.Precision.HIGHEST) + carry)
            carry = carry + jnp.sum(lf, axis=0, keepdims=True)

    return pl.pallas_call(
        body, name="forget_prep", grid=(t // seq,),
        in_specs=[pl.BlockSpec((seq, LANES), lambda b: (b, 0))],
        out_specs=pl.BlockSpec((seq, LANES), lambda b: (b, 0)),
        out_shape=jax.ShapeDtypeStruct((t, LANES), f32),
        compiler_params=_params(("parallel",)),
    )(f128)


def _split3(cv):
    hi = cv.astype(bf16)
    r1 = cv - hi.astype(f32)
    mid = r1.astype(bf16)
    lo = (r1 - mid.astype(f32)).astype(bf16)
    return hi, mid, lo


def _attn_prep(qkv, c):
    t = qkv.shape[0]

    def body(q_ref, k_ref, c_ref, qa_ref, ka_ref):
        lane = lax.broadcasted_iota(jnp.int32, (1, LANES), 1)
        cv = c_ref[...]
        one = jnp.ones((), bf16)
        zero = jnp.zeros((), bf16)
        q_ones = jnp.where((lane >= 67) & (lane < 70), one, zero)
        k_ones = jnp.where((lane >= 64) & (lane < 67), one, zero)
        for head in range(HEADS):
            pair = pl.ds((head // 2) * LANES, LANES)
            ch = jnp.sum(jnp.where(lane == head, cv, 0.0), axis=1, keepdims=True)
            hi, mid, lo = _split3(ch)
            q2, k2 = q_ref[:, pair], k_ref[:, pair]
            if head % 2 == 1:
                q2, k2 = pltpu.roll(q2, 64, 1), pltpu.roll(k2, 64, 1)
            qa = jnp.where(lane < 64, q2 * jnp.asarray(QK_SCALE, bf16),
                           jnp.where(lane == 64, hi, jnp.where(lane == 65, mid, jnp.where(lane == 66, lo, q_ones))))
            ka = jnp.where(lane < 64, k2,
                           jnp.where(lane == 67, -hi, jnp.where(lane == 68, -mid, jnp.where(lane == 69, -lo, k_ones))))
            qa_ref[:, pl.ds(head * LANES, LANES)] = qa
            ka_ref[:, pl.ds(head * LANES, LANES)] = ka

    tm = min(TM, t)
    out = pl.BlockSpec((tm, 2 * D), lambda i: (i, 0))
    return pl.pallas_call(
        body, name="attn_prep", grid=(t // tm,),
        in_specs=[_tile(tm, D, 0), _tile(tm, D, 1), _tile(tm, LANES)],
        out_specs=[out, out],
        out_shape=[jax.ShapeDtypeStruct((t, 2 * D), bf16)] * 2,
        compiler_params=_params(("parallel",)),
    )(qkv, qkv, c)


def _attn_fwd(qa, ka, qkv, rest, seq):
    t = qkv.shape[0]
    nb, nq = t // seq, seq // TQ

    hg = ATT_GROUP_FWD
    ng = HEADS // hg

    def body(q_ref, k_ref, v_ref, ga_ref, o_ref, pa_ref, lse_ref, acc_scr):
        qi, gi = pl.program_id(1), pl.program_id(2)
        krow = lax.broadcasted_iota(jnp.int32, (TQ, TQ), 0)
        qcol = lax.broadcasted_iota(jnp.int32, (TQ, TQ), 1)
        acc_scr[...] = jnp.zeros_like(acc_scr)

        def kv_step(kt, carry, masked):
            ks = pl.multiple_of(kt * TQ, TQ)
            sts = [_dot_nt(k_ref[pl.ds(ks, TQ), pl.ds(g * LANES, LANES)], q_ref[:, pl.ds(g * LANES, LANES)])
                   for g in range(hg)]
            if masked:
                sts = [jnp.where(krow <= qcol, st, MASK_VALUE) for st in sts]
            m_new = [jnp.maximum(carry[g][0], jnp.max(sts[g], axis=0, keepdims=True)) for g in range(hg)]
            ps = [jnp.exp(sts[g] - m_new[g]) for g in range(hg)]
            alphas = [jnp.exp(carry[g][0] - m_new[g]) for g in range(hg)]
            phi = [ps[g].astype(bf16) for g in range(hg)]
            plo = [(ps[g] - phi[g].astype(f32)).astype(bf16) for g in range(hg)]
            vs = [v_ref[pl.ds(ks, TQ), pl.ds(j * LANES, LANES)] for j in range(hg // 2)]
            pvs = [_dot_tn(vs[g // 2], phi[g]) + _dot_tn(vs[g // 2], plo[g]) for g in range(hg)]
            olds = [acc_scr[g] for g in range(hg)]
            for g in range(hg):
                acc_scr[g] = alphas[g] * olds[g] + pvs[g]
            return tuple((m_new[g], alphas[g] * carry[g][1] + jnp.sum(ps[g], axis=0, keepdims=True))
                         for g in range(hg))

        init = tuple((jnp.full((1, TQ), MASK_VALUE, f32), jnp.zeros((1, TQ), f32)) for _ in range(hg))
        carry = lax.fori_loop(0, qi, lambda kt, cr: kv_step(kt, cr, False), init)
        stats = kv_step(qi, carry, True)
        drow = lax.broadcasted_iota(jnp.int32, (LANES, TQ), 0)
        for g in range(hg):
            m, l = stats[g]
            lse_ref[0, pl.ds(hg * gi + g, 1), :] = m + jnp.log(l)
        for j in range(hg // 2):
            o2 = jnp.where(drow < 64, acc_scr[2 * j] / stats[2 * j][1], acc_scr[2 * j + 1] / stats[2 * j + 1][1]).T
            o_ref[:, pl.ds(j * LANES, LANES)] = o2
            ga = ga_ref[:, pl.ds(j * LANES, LANES)].astype(f32)
            pa_ref[:, pl.ds(j * LANES, LANES)] = (o2 * (ga * _sig(ga))).astype(bf16)

    vw = hg * 64
    tile = pl.BlockSpec((TQ, vw), lambda b, qi, gi: (b * nq + qi, gi))
    return pl.pallas_call(
        body, name="attn_fwd", grid=(nb, nq, ng),
        in_specs=[pl.BlockSpec((TQ, hg * LANES), lambda b, qi, gi: (b * nq + qi, gi)),
                  pl.BlockSpec((seq, hg * LANES), lambda b, qi, gi: (b, gi)),
                  pl.BlockSpec((seq, vw), lambda b, qi, gi: (b, 2 * ng + gi)), tile],
        out_specs=[tile, tile, pl.BlockSpec((1, HEADS, TQ), lambda b, qi, gi: (b * nq + qi, 0, 0))],
        out_shape=[jax.ShapeDtypeStruct((t, D), f32), jax.ShapeDtypeStruct((t, D), bf16),
                   jax.ShapeDtypeStruct((t // TQ, HEADS, TQ), f32)],
        scratch_shapes=[pltpu.VMEM((hg, LANES, TQ), f32)],
        compiler_params=_params(("parallel", "parallel", "arbitrary"), VMEM_LIMIT),
    )(qa, ka, qkv, rest)


def _shifted_rows(x, top8, prev8, shift, row, row8):
    body = pltpu.roll(x, shift, 0)
    head = jnp.where(row8 < shift, pltpu.roll(prev8, shift, 0), pltpu.roll(top8, shift, 0))
    return body, head


def _rnn_gates(xc, wa_ref, wx_ref, ba_ref, bx_ref, lam_ref):
    xcb = xc.astype(bf16)
    r = _sig(_dot(xcb, wa_ref[...]) + ba_ref[...])
    i = _sig(_dot(xcb, wx_ref[...]) + bx_ref[...])
    sp = _softplus(-lam_ref[...])
    log_a = (-RG_C) * r * sp
    th = jnp.tanh(log_a)
    w1 = (-2.0) * th / (1.0 - th)
    sq = jnp.sqrt(jnp.maximum(w1, 0.0))
    return r, i, sp, log_a, w1, sq


def _conv_tile(x_ref, xprev_ref, has_prev, cw_ref, cb_ref, xc_ref):
    row = lax.broadcasted_iota(jnp.int32, (TL, D), 0)
    row8 = lax.broadcasted_iota(jnp.int32, (8, D), 0)
    x = x_ref[...].astype(f32)
    top8 = x[:8]
    prev8 = jnp.where(has_prev, xprev_ref[...].astype(f32)[PREV_ROWS - 8:], 0.0)
    xc = cb_ref[...] + cw_ref[pl.ds(3, 1), :] * x
    xc8 = cb_ref[...] + cw_ref[pl.ds(3, 1), :] * top8
    for sh in range(1, 4):
        w = cw_ref[pl.ds(3 - sh, 1), :]
        xs, xs8 = _shifted_rows(x, top8, prev8, sh, row, row8)
        xc = xc + w * xs
        xc8 = xc8 + w * xs8
    xc_ref[...] = xc
    xc_ref[pl.ds(0, 8), :] = xc8


def _rnn_fwd(rest, conv_w, conv_b, wa_d, wx_d, ba, bx, lam, seq):
    t = rest.shape[0]
    nb, nt = t // seq, seq // TL

    def body(x_ref, xprev_ref, gr_ref, cw_ref, cb_ref, wa_ref, wx_ref, ba_ref, bx_ref, lam_ref,
             xc_ref, a_ref, h_ref, pr_ref, u_scr, carry):
        tt = pl.program_id(1)
        _conv_tile(x_ref, xprev_ref, tt > 0, cw_ref, cb_ref, xc_ref)
        xc = xc_ref[...]
        r, i, sp, log_a, w1, sq = _rnn_gates(xc, wa_ref, wx_ref, ba_ref, bx_ref, lam_ref)
        a_ref[...] = jnp.exp(log_a)
        u_scr[...] = sq * (i * xc)

        @pl.when(tt == 0)
        def _():
            carry[...] = jnp.zeros_like(carry)

        def step(s, h):
            h = a_ref[pl.ds(s, 1), :] * h + u_scr[pl.ds(s, 1), :]
            h_ref[pl.ds(s, 1), :] = h
            return h

        carry[...] = lax.fori_loop(0, TL, step, carry[...], unroll=8)
        gr = gr_ref[...].astype(f32)
        pr_ref[...] = (h_ref[...] * (gr * _sig(gr))).astype(bf16)

    tile = lambda cb: pl.BlockSpec((TL, D), lambda b, tt, cb=cb: (b * nt + tt, cb))
    prev = lambda cb: pl.BlockSpec(
        (PREV_ROWS, D), lambda b, tt, cb=cb: (jnp.maximum((b * nt + tt) * (TL // PREV_ROWS) - 1, 0), cb))
    vec = _whole((1, D))
    return pl.pallas_call(
        body, name="rnn_fwd", grid=(nb, nt),
        in_specs=[tile(1), prev(1), tile(2), _whole((4, D)), vec, _whole((D, D)), _whole((D, D)), vec, vec, vec],
        out_specs=[tile(0)] * 4,
        out_shape=[jax.ShapeDtypeStruct((t, D), f32)] * 3 + [jax.ShapeDtypeStruct((t, D), bf16)],
        scratch_shapes=[pltpu.VMEM((TL, D), f32), pltpu.VMEM((1, D), f32)],
        compiler_params=_params(("parallel", "arbitrary"), VMEM_LIMIT),
    )(rest, rest, rest, conv_w, conv_b, wa_d, wx_d, ba, bx, lam)


def _merge(mga, mgr, ya, yr):
    return (_sig(mga.astype(f32)) * ya.astype(f32) + _sig(mgr.astype(f32)) * yr.astype(f32)).astype(bf16)


def _out_proj_loss(rest, ya, yr, w_out, x, tgt, w_post):
    t = x.shape[0]

    def body(mga_ref, mgr_ref, ya_ref, yr_ref, wo_ref, x_ref, t_ref, w_ref, do_ref, dy_ref, mrg_ref, loss_ref, dwp_ref):
        @pl.when(pl.program_id(0) == 0)
        def _():
            loss_ref[...] = jnp.zeros_like(loss_ref)
            dwp_ref[...] = jnp.zeros_like(dwp_ref)

        mrg = _merge(mga_ref[...], mgr_ref[...], ya_ref[...], yr_ref[...])
        mrg_ref[...] = mrg
        ov = _dot(mrg, wo_ref[...])
        w = w_ref[...]
        r2 = lax.rsqrt(jnp.mean(ov * ov, axis=-1, keepdims=True) + NORM_EPS)
        oh = ov * r2
        e = x_ref[...] + oh * w - t_ref[...]
        loss_ref[...] += 0.5 * jnp.sum(jnp.mean(e * e, axis=-1, keepdims=True))
        dy = e * (1.0 / D)
        dy_ref[...] = dy
        dwp_ref[...] += jnp.sum(dy * oh, axis=0, keepdims=True)
        doh = dy * w
        do_ref[...] = (r2 * (doh - oh * jnp.mean(doh * oh, axis=-1, keepdims=True))).astype(bf16)

    return pl.pallas_call(
        body, name="out_proj_loss", grid=(t // TM,),
        in_specs=[_tile(TM, D, 3), _tile(TM, D, 4), _tile(TM, D), _tile(TM, D), _whole((D, D)), _tile(TM, D),
                  _tile(TM, D), _whole((1, D))],
        out_specs=[_tile(TM, D), _tile(TM, D), _tile(TM, D), _whole((8, LANES)), _whole((1, D))],
        out_shape=[jax.ShapeDtypeStruct((t, D), bf16), jax.ShapeDtypeStruct((t, D), f32),
                   jax.ShapeDtypeStruct((t, D), bf16), jax.ShapeDtypeStruct((8, LANES), f32),
                   jax.ShapeDtypeStruct((1, D), f32)],
        compiler_params=_params(("arbitrary",), VMEM_LIMIT),
    )(rest, rest, ya, yr, w_out, x, tgt, w_post)


def _out_bwd(do, rest, ya, yr, w_out):
    t = do.shape[0]

    def body(do_ref, mga_ref, mgr_ref, ya_ref, yr_ref, w_ref, dya_ref, dyr_ref, dmga_ref, dmgr_ref):
        sa, sr = _sig(mga_ref[...].astype(f32)), _sig(mgr_ref[...].astype(f32))
        ya, yr = ya_ref[...].astype(f32), yr_ref[...].astype(f32)
        dm = _dot_nt(do_ref[...], w_ref[...])
        dya_ref[...] = (dm * sa).astype(bf16)
        dyr_ref[...] = (dm * sr).astype(bf16)
        dmga_ref[...] = (dm * ya * sa * (1.0 - sa)).astype(bf16)
        dmgr_ref[...] = (dm * yr * sr * (1.0 - sr)).astype(bf16)

    return pl.pallas_call(
        body, name="out_bwd", grid=(t // TM,),
        in_specs=[_tile(TM, D), _tile(TM, D, 3), _tile(TM, D, 4), _tile(TM, D), _tile(TM, D), _whole((D, D))],
        out_specs=[_tile(TM, D)] * 4,
        out_shape=[jax.ShapeDtypeStruct((t, D), bf16)] * 4,
        compiler_params=_params(("parallel",), VMEM_LIMIT),
    )(do, rest, rest, ya, yr, w_out)


def _branch_bwd(name, dyb, rest, gate_cb, act, w, act_grad_dtype, head_sums=False):
    t = dyb.shape[0]

    def body(dy_ref, g_ref, act_ref, w_ref, dact_ref, dg_ref, *delta_ref):
        dp = _dot_nt(dy_ref[...], w_ref[...])
        g = g_ref[...].astype(f32)
        sg = _sig(g)
        act = act_ref[...]
        dact = (dp * (g * sg)).astype(act_grad_dtype)
        dact_ref[...] = dact
        dg_ref[...] = (dp * act * (sg * (1.0 + g * (1.0 - sg)))).astype(bf16)
        if head_sums:
            ch = lax.broadcasted_iota(jnp.int32, (D, LANES), 0)
            hd = lax.broadcasted_iota(jnp.int32, (D, LANES), 1)
            pick = (ch // 64 == hd).astype(bf16)
            per_head = sum(_dot(piece, pick) for piece in _split3(dact.astype(f32) * act))
            delta_ref[0][0] = per_head.T[:HEADS, :]

    out_specs = [_tile(TM, D), _tile(TM, D)]
    out_shape = [jax.ShapeDtypeStruct((t, D), act_grad_dtype), jax.ShapeDtypeStruct((t, D), bf16)]
    if head_sums:
        out_specs.append(pl.BlockSpec((1, HEADS, TM), lambda i: (i, 0, 0)))
        out_shape.append(jax.ShapeDtypeStruct((t // TM, HEADS, TM), f32))
    return pl.pallas_call(
        body, name=name, grid=(t // TM,),
        in_specs=[_tile(TM, D), _tile(TM, D, gate_cb), _tile(TM, D), _whole((D, D))],
        out_specs=out_specs, out_shape=out_shape,
        compiler_params=_params(("parallel",), VMEM_LIMIT),
    )(dyb, rest, act, w)


def _rnn_bwd(dh, a, h, xc, rest, conv_w, conv_b, wa_d, wx_d, ba, bx, lam, seq):
    t = dh.shape[0]
    nb, nt = t // seq, seq // TL
    diag = (D // LANES, LANES, LANES)

    def body(dh_ref, a_ref, h_ref, hprev_ref, xc_ref, x_ref, xprev_ref, cw_ref, cb_ref, wa_ref, wx_ref,
             ba_ref, bx_ref, lam_ref, dxr_ref, dwa_ref, dwx_ref, vec_ref, g_scr, dxc_scr, dxr_scr, qcarry, dxc_next):
        b, tt = pl.program_id(0), pl.program_id(1)
        rt = nt - 1 - tt

        @pl.when((b == 0) & (tt == 0))
        def _():
            dwa_ref[...] = jnp.zeros_like(dwa_ref)
            dwx_ref[...] = jnp.zeros_like(dwx_ref)
            vec_ref[...] = jnp.zeros_like(vec_ref)

        @pl.when(tt == 0)
        def _():
            qcarry[...] = jnp.zeros_like(qcarry)
            dxc_next[...] = jnp.zeros_like(dxc_next)

        def step(k, q):
            s = TL - 1 - k
            g = dh_ref[pl.ds(s, 1), :] + q
            g_scr[pl.ds(s, 1), :] = g
            return a_ref[pl.ds(s, 1), :] * g

        qcarry[...] = lax.fori_loop(0, TL, step, qcarry[...], unroll=8)

        row = lax.broadcasted_iota(jnp.int32, (TL, D), 0)
        row8 = lax.broadcasted_iota(jnp.int32, (8, D), 0)
        g = g_scr[...]
        av = a_ref[...]
        xc = xc_ref[...]
        hlast = jnp.where(rt > 0, hprev_ref[pl.ds(PREV_ROWS - 1, 1), :], 0.0)
        hp = jnp.where(row == 0, hlast, pltpu.roll(h_ref[...], 1, 0))
        r, i, sp, log_a, w1, sq = _rnn_gates(xc, wa_ref, wx_ref, ba_ref, bx_ref, lam_ref)
        dix = g * sq
        di = dix * xc
        dxc = dix * i
        dsq = g * (i * xc)
        dlog_a = g * hp * av - dsq * jnp.where(sq > 0.0, (1.0 - w1) / sq, 0.0)
        dpr = (dlog_a * ((-RG_C) * sp)) * r * (1.0 - r)
        dpi = di * i * (1.0 - i)
        dprb, dpib, xcb = dpr.astype(bf16), dpi.astype(bf16), xc.astype(bf16)
        dxc = dxc + _dot_nt(dprb, wa_ref[...]) + _dot_nt(dpib, wx_ref[...])
        for j in range(D // LANES):
            cols = slice(j * LANES, (j + 1) * LANES)
            dwa_ref[j] += _dot_tn(xcb[:, cols], dprb[:, cols])
            dwx_ref[j] += _dot_tn(xcb[:, cols], dpib[:, cols])
        vec_ref[pl.ds(0, 1), :] += jnp.sum(dpr, axis=0, keepdims=True)
        vec_ref[pl.ds(1, 1), :] += jnp.sum(dpi, axis=0, keepdims=True)
        dsp = jnp.sum(dlog_a * ((-RG_C) * r), axis=0, keepdims=True)
        vec_ref[pl.ds(2, 1), :] += dsp * (-_sig(-lam_ref[...]))
        vec_ref[pl.ds(3, 1), :] += jnp.sum(dxc, axis=0, keepdims=True)

        dxc_scr[...] = dxc
        bot8 = dxc_scr[pl.ds(TL - 8, 8), :]
        nxt8 = dxc_next[...]
        dxr = cw_ref[pl.ds(3, 1), :] * dxc
        dxr8 = cw_ref[pl.ds(3, 1), :] * bot8
        for sh in range(1, 4):
            w = cw_ref[pl.ds(3 - sh, 1), :]
            dxr = dxr + w * pltpu.roll(dxc, TL - sh, 0)
            dxr8 = dxr8 + w * jnp.where(row8 < 8 - sh, pltpu.roll(bot8, 8 - sh, 0), pltpu.roll(nxt8, 8 - sh, 0))
        dxr_scr[...] = dxr
        dxr_scr[pl.ds(TL - 8, 8), :] = dxr8
        dxr_ref[...] = dxr_scr[...].astype(bf16)
        dxc_next[...] = dxc_scr[pl.ds(0, 8), :]

        x = x_ref[...].astype(f32)
        prev8 = jnp.where(rt > 0, xprev_ref[...].astype(f32)[PREV_ROWS - 8:], 0.0)
        dxc_top8 = dxc_scr[pl.ds(0, 8), :]
        vec_ref[pl.ds(7, 1), :] += jnp.sum(dxc * x, axis=0, keepdims=True)
        for sh in range(1, 4):
            inside = jnp.sum(dxc * jnp.where(row >= sh, pltpu.roll(x, sh, 0), 0.0), axis=0, keepdims=True)
            above = jnp.sum(dxc_top8 * jnp.where(row8 < sh, pltpu.roll(prev8, sh, 0), 0.0), axis=0, keepdims=True)
            vec_ref[pl.ds(7 - sh, 1), :] += inside + above

    tile = lambda cb: pl.BlockSpec((TL, D), lambda b, tt, cb=cb: (b * nt + nt - 1 - tt, cb))
    prev = lambda cb: pl.BlockSpec(
        (PREV_ROWS, D), lambda b, tt, cb=cb: (jnp.maximum((b * nt + nt - 1 - tt) * (TL // PREV_ROWS) - 1, 0), cb))
    vec = _whole((1, D))
    return pl.pallas_call(
        body, name="rnn_bwd", grid=(nb, nt),
        in_specs=[tile(0), tile(0), tile(0), prev(0), tile(0), tile(1), prev(1),
                  _whole((4, D)), vec, _whole((D, D)), _whole((D, D)), vec, vec, vec],
        out_specs=[tile(0), _whole(diag), _whole(diag), _whole((8, D))],
        out_shape=[jax.ShapeDtypeStruct((t, D), bf16), jax.ShapeDtypeStruct(diag, f32),
                   jax.ShapeDtypeStruct(diag, f32), jax.ShapeDtypeStruct((8, D), f32)],
        scratch_shapes=[pltpu.VMEM((TL, D), f32), pltpu.VMEM((TL, D), f32), pltpu.VMEM((TL, D), f32),
                        pltpu.VMEM((1, D), f32), pltpu.VMEM((8, D), f32)],
        compiler_params=_params(("arbitrary", "arbitrary"), VMEM_LIMIT),
    )(dh, a, h, h, xc, rest, rest, conv_w, conv_b, wa_d, wx_d, ba, bx, lam)


def _attn_bwd(qa, ka, qkv, doa, lse, delta, seq):
    t = qkv.shape[0]
    nb, nq = t // seq, seq // TQ
    hg = ATT_GROUP
    ng, npair = HEADS // hg, hg // 2

    def body(qa_ref, ka_ref, q_ref, k_ref, v_ref, do_ref, lse_ref, dl_ref, dq_ref, dk_ref, dv_ref, dc_ref,
             dqt_scr, dk_scr, dv_scr, ds_scr, kht_scr):
        gi, kt = pl.program_id(1), pl.program_id(2)
        lane = lax.broadcasted_iota(jnp.int32, (1, LANES), 1)
        krow = lax.broadcasted_iota(jnp.int32, (TQ, TQ), 0)
        qcol = lax.broadcasted_iota(jnp.int32, (TQ, TQ), 1)
        lmask = [(lane // 64) == hh for hh in range(2)]
        scale = jnp.asarray(QK_SCALE, bf16)

        @pl.when(kt == 0)
        def _():
            dqt_scr[...] = jnp.zeros_like(dqt_scr)

        dk_scr[...] = jnp.zeros_like(dk_scr)
        dv_scr[...] = jnp.zeros_like(dv_scr)
        ds_scr[...] = jnp.zeros_like(ds_scr)
        for g in range(hg):
            k2 = k_ref[:, pl.ds((g // 2) * LANES, LANES)]
            kht_scr[g] = jnp.where(lmask[g % 2], k2, jnp.zeros_like(k2)).T

        def q_step(qt, masked):
            qs = pl.multiple_of(qt * TQ, TQ)
            heads = range(hg)
            do2 = [do_ref[pl.ds(qs, TQ), pl.ds(j * LANES, LANES)] for j in range(npair)]
            q2 = [q_ref[pl.ds(qs, TQ), pl.ds(j * LANES, LANES)] for j in range(npair)]
            doh = [jnp.where(lmask[g % 2], do2[g // 2], jnp.zeros_like(do2[0])) for g in heads]
            qh = [jnp.where(lmask[g % 2], q2[g // 2], jnp.zeros_like(q2[0])) * scale for g in heads]
            st = [_dot_nt(ka_ref[:, pl.ds(g * LANES, LANES)], qa_ref[pl.ds(qs, TQ), pl.ds(g * LANES, LANES)])
                  for g in heads]
            if masked:
                st = [jnp.where(krow <= qcol, s, MASK_VALUE) for s in st]
            dp = [_dot_nt(v_ref[:, pl.ds((g // 2) * LANES, LANES)], doh[g]) for g in heads]
            p = [jnp.exp(st[g] - lse_ref[qt, pl.ds(hg * gi + g, 1), :]) for g in heads]
            ds = [p[g] * (dp[g] - dl_ref[qt, pl.ds(hg * gi + g, 1), :]) for g in heads]
            pb = [x.astype(bf16) for x in p]
            dsb = [x.astype(bf16) for x in ds]
            for j in range(npair):
                a, b = 2 * j, 2 * j + 1
                dv_scr[j] += _dot(pb[a], doh[a]) + _dot(pb[b], doh[b])
                dk_scr[j] += _dot(dsb[a], qh[a]) + _dot(dsb[b], qh[b])
                dqt_scr[qt, j] += (_dot(kht_scr[a], dsb[a]) + _dot(kht_scr[b], dsb[b])) * QK_SCALE
            for g in heads:
                ds_scr[g] += ds[g][:, :LANES] + ds[g][:, LANES:]

        q_step(kt, True)

        def loop_body(qt, carry):
            q_step(qt, False)
            return carry

        lax.fori_loop(kt + 1, nq, loop_body, 0)

        dc = jnp.zeros((TQ, LANES), f32)
        for g in range(hg):
            dc = jnp.where(lane == g, -jnp.sum(ds_scr[g], axis=1, keepdims=True), dc)
        dc_ref[...] = dc
        for j in range(npair):
            dk_ref[:, pl.ds(j * LANES, LANES)] = dk_scr[j].astype(bf16)
            dv_ref[:, pl.ds(j * LANES, LANES)] = dv_scr[j].astype(bf16)

        @pl.when(kt == nq - 1)
        def _():
            for qt in range(nq):
                for j in range(npair):
                    dq_ref[pl.ds(qt * TQ, TQ), pl.ds(j * LANES, LANES)] = dqt_scr[qt, j].T.astype(bf16)

    vw = hg * 64
    seqspec = pl.BlockSpec((seq, vw), lambda b, gi, kt: (b, gi))
    kspec = lambda off: pl.BlockSpec((TQ, vw), lambda b, gi, kt: (b * nq + kt, off + gi))
    rowspec = pl.BlockSpec((nq, HEADS, TQ), lambda b, gi, kt: (b, 0, 0))
    return pl.pallas_call(
        body, name="attn_bwd", grid=(nb, ng, nq),
        in_specs=[pl.BlockSpec((seq, hg * LANES), lambda b, gi, kt: (b, gi)),
                  pl.BlockSpec((TQ, hg * LANES), lambda b, gi, kt: (b * nq + kt, gi)),
                  seqspec, kspec(ng), kspec(2 * ng), seqspec, rowspec, rowspec],
        out_specs=[seqspec, kspec(0), kspec(0), pl.BlockSpec((TQ, LANES), lambda b, gi, kt: (b * nq + kt, gi))],
        out_shape=[jax.ShapeDtypeStruct((t, D), bf16)] * 3 + [jax.ShapeDtypeStruct((t, ng * LANES), f32)],
        scratch_shapes=[pltpu.VMEM((nq, npair, LANES, TQ), f32), pltpu.VMEM((npair, TQ, LANES), f32),
                        pltpu.VMEM((npair, TQ, LANES), f32), pltpu.VMEM((hg, TQ, LANES), f32),
                        pltpu.VMEM((hg, LANES, TQ), bf16)],
        compiler_params=_params(("parallel", "parallel", "arbitrary"), VMEM_LIMIT),
    )(qa, ka, qkv, qkv, qkv, doa, lse, delta)


def _forget_bwd(dc, f128, seq):
    t = f128.shape[0]
    nb = seq // LANES

    def body(dc_ref, f_ref, df_ref, dbf_ref):
        @pl.when(pl.program_id(0) == 0)
        def _():
            dbf_ref[...] = jnp.zeros_like(dbf_ref)

        r = lax.broadcasted_iota(jnp.int32, (LANES, LANES), 0)
        cidx = lax.broadcasted_iota(jnp.int32, (LANES, LANES), 1)
        tri = (r <= cidx).astype(f32)
        carry = jnp.zeros((1, LANES), f32)
        total = jnp.zeros((1, LANES), f32)
        for blk in reversed(range(nb)):
            dcb = dc_ref[pl.ds(blk * LANES, LANES), :]
            dlf = jnp.dot(tri, dcb, preferred_element_type=f32, precision=lax.Precision.HIGHEST) + carry
            df = dlf * _sig(-f_ref[pl.ds(blk * LANES, LANES), :])
            df_ref[pl.ds(blk * LANES, LANES), :] = df.astype(bf16)
            total = total + jnp.sum(df, axis=0, keepdims=True)
            carry = carry + jnp.sum(dcb, axis=0, keepdims=True)
        dbf_ref[...] += total

    return pl.pallas_call(
        body, name="forget_bwd", grid=(t // seq,),
        in_specs=[pl.BlockSpec((seq, LANES), lambda b: (b, 0)), pl.BlockSpec((seq, LANES), lambda b: (b, 0))],
        out_specs=[pl.BlockSpec((seq, LANES), lambda b: (b, 0)), _whole((1, LANES))],
        out_shape=[jax.ShapeDtypeStruct((t, LANES), bf16), jax.ShapeDtypeStruct((1, LANES), f32)],
        compiler_params=_params(("arbitrary",)),
    )(dc, f128)


def _in_bwd(dz, df, x, dy, w_qkv, w_rest, w_f, w_pre):
    t = x.shape[0]
    n_qkv = w_qkv.shape[0] // D
    n_rest = w_rest.shape[0] // D

    def body(*refs):
        dz_refs = refs[:n_qkv + n_rest]
        df_ref, x_ref, dy_ref, wq_ref, wr_ref, wf_ref, wp_ref, gx_ref, dwp_ref = refs[n_qkv + n_rest:]

        @pl.when(pl.program_id(0) == 0)
        def _():
            dwp_ref[...] = jnp.zeros_like(dwp_ref)

        dh = _dot(df_ref[...], wf_ref[...])
        for p in range(n_qkv):
            dh = dh + _dot(dz_refs[p][...], wq_ref[pl.ds(p * D, D), :])
        for p in range(n_rest):
            dh = dh + _dot(dz_refs[n_qkv + p][...], wr_ref[pl.ds(p * D, D), :])
        xv = x_ref[...]
        r1 = lax.rsqrt(jnp.mean(xv * xv, axis=-1, keepdims=True) + NORM_EPS)
        xh = xv * r1
        dwp_ref[...] += jnp.sum(dh * xh, axis=0, keepdims=True)
        dxh = dh * wp_ref[...]
        gx_ref[...] = dy_ref[...] + r1 * (dxh - xh * jnp.mean(dxh * xh, axis=-1, keepdims=True))

    once = lambda shape: pl.BlockSpec(shape, lambda i: (0, 0), pipeline_mode=pl.Buffered(1))
    return pl.pallas_call(
        body, name="in_bwd", grid=(t // TM,),
        in_specs=[_tile(TM, D)] * (n_qkv + n_rest) + [_tile(TM, LANES), _tile(TM, D), _tile(TM, D),
                  once(w_qkv.shape), once(w_rest.shape), once(w_f.shape), _whole((1, D))],
        out_specs=[_tile(TM, D), _whole((1, D))],
        out_shape=[jax.ShapeDtypeStruct((t, D), f32), jax.ShapeDtypeStruct((1, D), f32)],
        compiler_params=_params(("arbitrary",), VMEM_LIMIT),
    )(*dz, df, x, dy, w_qkv, w_rest, w_f, w_pre)


def _tn_mm(name, a, b, tn, tk=2048):
    t, k = a.shape
    tk = min(tk, t)
    n = b.shape[1]

    def body(a_ref, b_ref, o_ref, s_ref):
        j, kk = pl.program_id(0), pl.program_id(1)

        @pl.when(kk == 0)
        def _():
            o_ref[...] = jnp.zeros_like(o_ref)

        @pl.when((j == 0) & (kk == 0))
        def _():
            s_ref[...] = jnp.zeros_like(s_ref)

        av = a_ref[...]
        o_ref[...] += _dot_tn(av, b_ref[...])

        @pl.when(j == 0)
        def _():
            s_ref[...] += jnp.sum(av.astype(f32), axis=0, keepdims=True)

    return pl.pallas_call(
        body, name=name, grid=(n // tn, t // tk),
        in_specs=[pl.BlockSpec((tk, k), lambda j, kk: (kk, 0)), pl.BlockSpec((tk, tn), lambda j, kk: (kk, j))],
        out_specs=[pl.BlockSpec((k, tn), lambda j, kk: (0, j)), _whole((1, k))],
        out_shape=[jax.ShapeDtypeStruct((k, n), f32), jax.ShapeDtypeStruct((1, k), f32)],
        compiler_params=_params(("arbitrary", "arbitrary"), VMEM_LIMIT),
    )(a, b)


def _position():
    return lax.axis_index("x"), lax.axis_index("y"), lax.axis_index("c")


def _gather_shards(parts, small):
    n = len(parts)
    halves = [p.shape[0] // 2 for p in parts]

    def body(*refs):
        srcs, small_src = refs[:n], refs[n]
        dsts, small_dst = refs[n + 1:2 * n + 1], refs[2 * n + 1]
        send, recv, local = refs[2 * n + 2:]
        x, y, c = _position()
        me = 2 * x + y
        chips = [(1 - x, y), (x, 1 - y), (1 - x, 1 - y)]
        ids = [2 * px + py for px, py in chips]

        def half(a, shard, which):
            return dsts[a].at[shard, pl.ds(which * halves[a], halves[a]), :]

        def over_ici(a, j, shard):
            px, py = chips[j]
            return pltpu.make_async_remote_copy(
                src_ref=srcs[a].at[pl.ds(c * halves[a], halves[a]), :], dst_ref=half(a, shard, c),
                send_sem=send.at[a * 3 + j], recv_sem=recv.at[a * 3 + j], device_id=(px, py, c), device_id_type=MESH)

        def to_sibling(a, j, which):
            k = 3 * n + a * 3 + j
            return pltpu.make_async_remote_copy(
                src_ref=half(a, ids[j], which), dst_ref=half(a, ids[j], which), send_sem=send.at[k],
                recv_sem=recv.at[k], device_id=(x, y, 1 - c), device_id_type=MESH)

        def small_copy(j, shard):
            px, py = chips[j]
            return pltpu.make_async_remote_copy(
                src_ref=small_src, dst_ref=small_dst.at[shard], send_sem=send.at[6 * n + j], recv_sem=recv.at[6 * n + j],
                device_id=(px, py, c), device_id_type=MESH)

        own = [pltpu.make_async_copy(srcs[a], dsts[a].at[me], local.at[a]) for a in range(n)]
        own.append(pltpu.make_async_copy(small_src, small_dst.at[me], local.at[n]))
        for cp in own:
            cp.start()
        first = [over_ici(a, j, me) for j in range(3) for a in range(n)] + [small_copy(j, me) for j in range(3)]
        for cp in first:
            cp.start()
        passed = []
        for j in range(3):
            for a in range(n):
                over_ici(a, j, ids[j]).wait_recv()
                passed.append(to_sibling(a, j, c))
                passed[-1].start()
        for j in range(3):
            small_copy(j, ids[j]).wait_recv()
            for a in range(n):
                to_sibling(a, j, 1 - c).wait_recv()
        for cp in first + passed:
            cp.wait_send()
        for cp in own:
            cp.wait()

    vm = pl.BlockSpec(memory_space=pltpu.VMEM)
    return pl.pallas_call(
        body, name="gather_shards",
        in_specs=[vm] * (n + 1), out_specs=[vm] * (n + 1),
        out_shape=[jax.ShapeDtypeStruct((N_CHIPS,) + p.shape, p.dtype) for p in parts + [small]],
        scratch_shapes=[pltpu.SemaphoreType.DMA((6 * n + 3,)), pltpu.SemaphoreType.DMA((6 * n + 3,)),
                        pltpu.SemaphoreType.DMA((n + 1,))],
        compiler_params=pltpu.CompilerParams(vmem_limit_bytes=VMEM_LIMIT),
    )(*parts, small)


def _allsum_rows(part):
    rows_n = part.shape[0]

    def body(x_ref, gath_ref, sum_ref, send_sems, recv_sems, local_sem):
        x, y, c = _position()
        me, sibling = (x, y, c), (x, y, 1 - c)
        chips = [(1 - x, y), (x, 1 - y), (1 - x, 1 - y)]

        def rows(px, py, pc):
            return gath_ref.at[pl.ds((4 * px + 2 * py + pc) * rows_n, rows_n), :]

        def copy(k, block, to, src=None):
            return pltpu.make_async_remote_copy(
                src_ref=rows(*block) if src is None else src, dst_ref=rows(*block),
                send_sem=send_sems.at[k], recv_sem=recv_sems.at[k], device_id=to, device_id_type=MESH)

        mine = pltpu.make_async_copy(x_ref, rows(*me), local_sem)
        mine.start()
        first = [copy(0, me, sibling, src=x_ref)]
        first += [copy(1 + j, me, (*chip, c), src=x_ref) for j, chip in enumerate(chips)]
        for cp in first:
            cp.start()
        passed = [copy(4 + j, (*chip, c), sibling) for j, chip in enumerate(chips)]
        for j, chip in enumerate(chips):
            copy(1 + j, (*chip, c), me).wait_recv()
            passed[j].start()
        copy(0, sibling, me).wait_recv()
        for j, chip in enumerate(chips):
            copy(4 + j, (*chip, 1 - c), me).wait_recv()
        for cp in first + passed:
            cp.wait_send()
        mine.wait()
        total = gath_ref[pl.ds(0, rows_n), :]
        for d in range(1, N_DEV):
            total = total + gath_ref[pl.ds(d * rows_n, rows_n), :]
        sum_ref[...] = total

    vm = pl.BlockSpec(memory_space=pltpu.VMEM)
    return pl.pallas_call(
        body, name="allsum_rows", in_specs=[vm], out_specs=[vm, vm],
        out_shape=[jax.ShapeDtypeStruct((N_DEV * rows_n, D), f32), jax.ShapeDtypeStruct((rows_n, D), f32)],
        scratch_shapes=[pltpu.SemaphoreType.DMA((7,)), pltpu.SemaphoreType.DMA((7,)), pltpu.SemaphoreType.DMA],
    )(part)[1]


PAIR_ROWS = 16


def _pair_reduce(name, pieces):
    _, r, n = pieces.shape

    def body(p_ref, o_ref, land, send, recv):
        x, y, c = _position()

        def remote(j, half):
            return pltpu.make_async_remote_copy(
                src_ref=p_ref.at[2 * j + half], dst_ref=land.at[j], send_sem=send.at[j], recv_sem=recv.at[j],
                device_id=(x, y, 1 - c), device_id_type=MESH)

        sends = [remote(j, 1 - c) for j in range(N_CHIPS)]
        for cp in sends:
            cp.start()
        for j in range(N_CHIPS):
            remote(j, c).wait_recv()

            def add_rows(i, carry, j=j):
                rows = pl.ds(pl.multiple_of(i * PAIR_ROWS, PAIR_ROWS), PAIR_ROWS)
                o_ref[j, rows, :] = (p_ref[2 * j + c, rows, :].astype(f32) + land[j, rows, :].astype(f32)).astype(bf16)
                return carry

            lax.fori_loop(0, r // PAIR_ROWS, add_rows, 0)
        for cp in sends:
            cp.wait_send()

    vm = pl.BlockSpec(memory_space=pltpu.VMEM)
    return pl.pallas_call(
        body, name=name, in_specs=[vm], out_specs=vm,
        out_shape=jax.ShapeDtypeStruct((N_CHIPS, r, n), bf16),
        scratch_shapes=[pltpu.VMEM((N_CHIPS, r, n), bf16), pltpu.SemaphoreType.DMA((N_CHIPS,)),
                        pltpu.SemaphoreType.DMA((N_CHIPS,))],
        compiler_params=pltpu.CompilerParams(vmem_limit_bytes=VMEM_LIMIT),
    )(pieces)


def _chip_exchange(arrs):
    n = len(arrs)

    def body(*refs):
        srcs, dsts = refs[:n], refs[n:2 * n]
        send, recv, local = refs[2 * n:]
        x, y, c = _position()
        me = 2 * x + y
        chips = [(1 - x, y), (x, 1 - y), (1 - x, 1 - y)]

        def remote(a, j, piece, landing):
            px, py = chips[j]
            return pltpu.make_async_remote_copy(
                src_ref=srcs[a].at[piece], dst_ref=dsts[a].at[landing], send_sem=send.at[a * 3 + j],
                recv_sem=recv.at[a * 3 + j], device_id=(px, py, c), device_id_type=MESH)

        own = [pltpu.make_async_copy(srcs[a].at[me], dsts[a].at[me], local.at[a]) for a in range(n)]
        sends = [remote(a, j, 2 * px + py, me) for j, (px, py) in enumerate(chips) for a in range(n)]
        for cp in sends + own:
            cp.start()
        for j, (px, py) in enumerate(chips):
            for a in range(n):
                remote(a, j, me, 2 * px + py).wait_recv()
        for cp in sends:
            cp.wait_send()
        for cp in own:
            cp.wait()

    anyspec = pl.BlockSpec(memory_space=pl.ANY)
    return pl.pallas_call(
        body, name="chip_exchange", in_specs=[anyspec] * n, out_specs=[anyspec] * n,
        out_shape=[jax.ShapeDtypeStruct(a.shape, a.dtype) for a in arrs],
        scratch_shapes=[pltpu.SemaphoreType.DMA((3 * n,)), pltpu.SemaphoreType.DMA((3 * n,)),
                        pltpu.SemaphoreType.DMA((n,))],
    )(*arrs)


def _swap_halves(arrs):
    n = len(arrs)

    def body(*refs):
        srcs, dsts = refs[:n], refs[n:2 * n]
        send, recv, local = refs[2 * n:]
        x, y, c = _position()

        def remote(a, landing):
            return pltpu.make_async_remote_copy(
                src_ref=srcs[a], dst_ref=dsts[a].at[landing], send_sem=send.at[a], recv_sem=recv.at[a],
                device_id=(x, y, 1 - c), device_id_type=MESH)

        own = [pltpu.make_async_copy(srcs[a], dsts[a].at[c], local.at[a]) for a in range(n)]
        sends = [remote(a, c) for a in range(n)]
        for cp in sends + own:
            cp.start()
        for a in range(n):
            remote(a, 1 - c).wait_recv()
        for cp in sends:
            cp.wait_send()
        for cp in own:
            cp.wait()

    vm = pl.BlockSpec(memory_space=pltpu.VMEM)
    return pl.pallas_call(
        body, name="swap_halves", in_specs=[vm] * n, out_specs=[vm] * n,
        out_shape=[jax.ShapeDtypeStruct((2,) + a.shape, a.dtype) for a in arrs],
        scratch_shapes=[pltpu.SemaphoreType.DMA((n,)), pltpu.SemaphoreType.DMA((n,)), pltpu.SemaphoreType.DMA((n,))],
        compiler_params=pltpu.CompilerParams(vmem_limit_bytes=VMEM_LIMIT),
    )(*arrs)


def _row_block(r):
    return 128 if r % 128 == 0 else r


def _sum_slots(name, slots):
    s, r, n = slots.shape
    rb = _row_block(r)

    def body(s_ref, o_ref):
        total = s_ref[0].astype(f32)
        for d in range(1, s):
            total = total + s_ref[d].astype(f32)
        o_ref[...] = total

    return pl.pallas_call(
        body, name=name, grid=(r // rb,),
        in_specs=[pl.BlockSpec((s, rb, n), lambda i: (0, i, 0))],
        out_specs=pl.BlockSpec((rb, n), lambda i: (i, 0)),
        out_shape=jax.ShapeDtypeStruct((r, n), f32),
        compiler_params=_params(("parallel",), VMEM_LIMIT),
    )(slots)


def _adamw(name, w, g, m, v):
    r, n = w.shape
    if r % 128 == 0 or r * n <= 128 * 1024:
        rb, nb = _row_block(r), n
    else:
        rb, nb = r, LANES

    def body(w_ref, g_ref, m_ref, v_ref, d_ref, nm_ref, nv_ref):
        gv = g_ref[...]
        m2 = ADAM_B1 * m_ref[...] + (1.0 - ADAM_B1) * gv
        v2 = ADAM_B2 * v_ref[...] + (1.0 - ADAM_B2) * (gv * gv)
        m_hat = m2 / (1.0 - ADAM_B1 ** ADAM_STEP)
        v_hat = v2 / (1.0 - ADAM_B2 ** ADAM_STEP)
        d_ref[...] = (-ADAM_LR) * (m_hat / (jnp.sqrt(v_hat) + ADAM_EPS) + ADAM_WD * w_ref[...])
        nm_ref[...] = m2
        nv_ref[...] = v2

    spec = pl.BlockSpec((rb, nb), lambda i, j: (i, j))
    return pl.pallas_call(
        body, name=name, grid=(r // rb, n // nb), in_specs=[spec] * 4, out_specs=[spec] * 3,
        out_shape=[jax.ShapeDtypeStruct((r, n), f32)] * 3,
        compiler_params=_params(("parallel", "parallel"), VMEM_LIMIT),
    )(w, g, m, v)


def _local_step(x2, tgt2, seq, wt):
    nb = x2.shape[0] // seq
    h = _prenorm(x2, wt["pre_w"])
    qkv = _mm("in_qkv", h, wt["w_qkv"], wt["b_qkv"], bf16, 1024, 1024, w_is_nk=True)
    rest = _mm("in_rest", h, wt["w_rest"], wt["b_rest"], bf16, 1024, 1024, w_is_nk=True)
    f128 = _mm("in_f", h, wt["w_f"], wt["b_f"], f32, 1024, LANES, w_is_nk=True)
    c = _forget_prep(f128, seq)
    qa, ka = _attn_prep(qkv, c)
    o_att, pa, lse = _attn_fwd(qa, ka, qkv, rest, seq)
    ya = _mm("proj_a", pa, wt["w_a"], None, bf16, 1024, D)
    rnn_w = (wt["conv_w"], wt["conv_b"], wt["wa_d"], wt["wx_d"], wt["ba"], wt["bx"], wt["lam"])
    xc, a, hrec, pr = _rnn_fwd(rest, *rnn_w, seq)
    yr = _mm("proj_r", pr, wt["w_r"], None, bf16, 1024, D)
    do, dy, mrg, loss8, d_post = _out_proj_loss(rest, ya, yr, wt["w_o"], x2, tgt2, wt["post_w"])
    dya, dyr, dmga, dmgr = _out_bwd(do, rest, ya, yr, wt["w_o"])
    doa, dga, delta = _branch_bwd("branch_a_bwd", dya, rest, 0, o_att, wt["w_a"], bf16, head_sums=True)
    dhrec, dgr = _branch_bwd("branch_r_bwd", dyr, rest, 2, hrec, wt["w_r"], f32)
    d_wo, _ = _tn_mm("dw_out", mrg, do, D)
    d_wa, _ = _tn_mm("dw_branch_a", pa, dya, D)
    d_wr, _ = _tn_mm("dw_branch_r", pr, dyr, D)
    dxr, d_wad, d_wxd, vec = _rnn_bwd(dhrec, a, hrec, xc, rest, *rnn_w, seq)
    dq, dk, dv, dc_pairs = _attn_bwd(qa, ka, qkv, doa, lse, delta, seq)
    dc = dc_pairs.reshape(-1, HEADS // ATT_GROUP, LANES)[:, :, :ATT_GROUP].reshape(-1, HEADS)
    df, db_f = _forget_bwd(_pad_cols(dc, LANES), f128, seq)
    pieces = [dq, dk, dv, dga, dxr, dgr, dmga, dmgr]
    gx, d_pre = _in_bwd(pieces, df, x2, dy, wt["w_qkv"], wt["w_rest"], wt["w_f"], wt["pre_w"])
    names = ["q", "k", "v", "ga", "xr", "gr", "mga", "mgr"]
    dws, dbs = [], []
    for nm, piece in zip(names, pieces):
        dw_p, db_p = _tn_mm("dw_in_" + nm, piece, h, D)
        dws.append(dw_p)
        dbs.append(db_p)
    dw_f, _ = _tn_mm("dw_in_f", df, h, D)
    zeros_w = jnp.zeros((IN_TOTAL - IN_USED, D), f32)
    d_w_in = jnp.concatenate(dws[:3] + [dw_f[:HEADS]] + dws[3:] + [zeros_w], axis=0)
    d_b_in = jnp.concatenate(dbs[:3] + [db_f[:, :HEADS]] + dbs[3:] + [zeros_w[:, :1].T], axis=1)
    return dict(loss=loss8[0, 0], grad_x=gx, pre_w=d_pre, w_in=d_w_in, b_in=d_b_in, conv_w=vec[4:8], conv_b=vec[3:4],
                wa_d=d_wad, ba=vec[0:1], wx_d=d_wxd, bx=vec[1:2], lam=vec[2:3], w_a=d_wa, w_r=d_wr, w_o=d_wo,
                post_w=d_post)


def _block_diag(w):
    g, bw, _ = w.shape
    eye = jnp.eye(g, dtype=w.dtype)
    return (w[:, :, None, :] * eye[:, None, :, None]).reshape(g * bw, g * bw)


def _gate_blocks(diag):
    half = diag.shape[1] // 2
    return jnp.stack([diag[:, :half, :half], diag[:, half:, half:]], axis=1).reshape(-1, half, half)


def _pad_cols(a, n):
    return jnp.pad(a, ((0, 0), (0, n - a.shape[1])))


def _pad_rows(a, n):
    return jnp.pad(a, ((0, n - a.shape[0]), (0, 0)))


def kernel(x, pre_norm_w, w_in, b_in, conv_w, conv_b, rg_wa, rg_ba, rg_wx, rg_bx, rg_lambda, w_branch_a, w_branch_r, w_out, post_norm_w, loss_target, m_pre_norm_w, m_w_in, m_b_in, m_conv_w, m_conv_b, m_rg_wa, m_rg_ba, m_rg_wx, m_rg_bx, m_rg_lambda, m_w_branch_a, m_w_branch_r, m_w_out, m_post_norm_w, v_pre_norm_w, v_w_in, v_b_in, v_conv_w, v_conv_b, v_rg_wa, v_rg_ba, v_rg_wx, v_rg_bx, v_rg_lambda, v_w_branch_a, v_w_branch_r, v_w_out, v_post_norm_w):
    nb, seq, _ = x.shape
    chip = 2 * lax.axis_index("x") + lax.axis_index("y")
    n_groups = rg_wa.shape[1]

    w_in_t = jnp.transpose(w_in[0])
    shard_cols = w_in_t.shape[0]
    padded = -(-shard_cols // 32) * 32
    g_in, g_a, g_r, g_o, g_cw = _gather_shards(
        [_pad_rows(w_in_t.astype(bf16), padded), w_branch_a[0].astype(bf16), w_branch_r[0].astype(bf16),
         w_out[0].astype(bf16)], conv_w[0])
    w_full = jnp.concatenate([g_in[j, :shard_cols] for j in range(N_CHIPS)], axis=0)
    q_end, f_end = 3 * D, 3 * D + HEADS
    wt = dict(
        pre_w=pre_norm_w, post_w=post_norm_w,
        w_qkv=w_full[:q_end], b_qkv=b_in[:, :q_end],
        w_f=_pad_rows(w_full[q_end:f_end], LANES), b_f=_pad_cols(b_in[:, q_end:f_end], LANES),
        w_rest=w_full[f_end:IN_USED], b_rest=b_in[:, f_end:IN_USED],
        w_a=g_a.reshape(D, D), w_r=g_r.reshape(D, D), w_o=g_o.reshape(D, D),
        conv_w=jnp.transpose(g_cw, (1, 0, 2)).reshape(4, D), conv_b=conv_b,
        wa_d=_block_diag(rg_wa[0]).astype(bf16), wx_d=_block_diag(rg_wx[0]).astype(bf16),
        ba=rg_ba, bx=rg_bx, lam=rg_lambda)

    part = _local_step(x.reshape(nb * seq, D), loss_target.reshape(nb * seq, D), seq, wt)
    loss = lax.psum(part["loss"], ("x", "y", "c"))
    grad_x = part["grad_x"].reshape(nb, seq, D)

    small = jnp.concatenate([
        part["pre_w"], _pad_cols(part["b_in"], 10 * D).reshape(10, D), part["conv_b"],
        _gate_blocks(part["wa_d"]).reshape(-1, D), part["ba"],
        _gate_blocks(part["wx_d"]).reshape(-1, D), part["bx"], part["lam"], part["post_w"],
        part["conv_w"]], axis=0)
    n_small = small.shape[0]
    n_rep = n_small - 4
    tot = _allsum_rows(_pad_rows(small, -(-n_small // 8) * 8))
    g_rep = tot[:n_rep]
    g_conv_w = lax.dynamic_slice_in_dim(tot[n_rep:n_small], chip * (D // N_CHIPS), D // N_CHIPS, axis=1)

    def pack(pre, b, cb, wa, ba, wx, bx, lam, post):
        return jnp.concatenate([pre, _pad_cols(b, 10 * D).reshape(10, D), cb, wa.reshape(-1, D), ba,
                                wx.reshape(-1, D), bx, lam, post], axis=0)

    def unpack(p):
        o = [0]

        def take(k):
            o[0] += k
            return p[o[0] - k:o[0]]

        pre = take(1)
        b = take(10).reshape(1, 10 * D)[:, :IN_TOTAL]
        cb = take(1)
        wa = take(64).reshape(rg_wa.shape)
        ba = take(1)
        wx = take(64).reshape(rg_wx.shape)
        bx = take(1)
        lam = take(1)
        post = take(1)
        return dict(pre_norm_w=pre, b_in=b, conv_b=cb, rg_wa=wa, rg_ba=ba, rg_wx=wx, rg_bx=bx, rg_lambda=lam,
                    post_norm_w=post)

    w_rep = pack(pre_norm_w, b_in, conv_b, rg_wa, rg_ba, rg_wx, rg_bx, rg_lambda, post_norm_w)
    m_rep = pack(m_pre_norm_w, m_b_in, m_conv_b, m_rg_wa, m_rg_ba, m_rg_wx, m_rg_bx, m_rg_lambda, m_post_norm_w)
    v_rep = pack(v_pre_norm_w, v_b_in, v_conv_b, v_rg_wa, v_rg_ba, v_rg_wx, v_rg_bx, v_rg_lambda, v_post_norm_w)
    d_rep, nm_rep, nv_rep = _adamw("adamw_rep", w_rep, g_rep, m_rep, v_rep)
    grads, deltas, new_m, new_v = unpack(g_rep), unpack(d_rep), unpack(nm_rep), unpack(nv_rep)

    p_in = jnp.pad(part["w_in"].reshape(N_CHIPS, shard_cols, D), ((0, 0), (0, padded - shard_cols), (0, 0)))
    p_in = p_in.reshape(N_DEV, padded // 2, D)
    p_aro = jnp.concatenate([part[k].reshape(N_DEV, D // N_DEV, D) for k in ("w_a", "w_r", "w_o")], axis=1)
    s_in, s_aro = _chip_exchange([_pair_reduce("pair_w_in", p_in.astype(bf16)),
                                  _pair_reduce("pair_w_aro", p_aro.astype(bf16))])
    f_in, f_aro = _swap_halves([_sum_slots("sum_w_in", s_in), _sum_slots("sum_w_aro", s_aro)])
    g_w_in_t = f_in.reshape(padded, D)[:shard_cols]
    rows = D // N_DEV
    g_aro = [f_aro[:, i * rows:(i + 1) * rows, :].reshape(2 * rows, D) for i in range(3)]

    w_in_upd = _adamw("adamw_w_in", w_in_t, g_w_in_t, jnp.transpose(m_w_in[0]), jnp.transpose(v_w_in[0]))
    g_w_in, d_w_in, nm_w_in, nv_w_in = [jnp.transpose(a) for a in (g_w_in_t, *w_in_upd)]
    upd_a = _adamw("adamw_w_branch_a", w_branch_a[0], g_aro[0], m_w_branch_a[0], v_w_branch_a[0])
    upd_r = _adamw("adamw_w_branch_r", w_branch_r[0], g_aro[1], m_w_branch_r[0], v_w_branch_r[0])
    upd_o = _adamw("adamw_w_out", w_out[0], g_aro[2], m_w_out[0], v_w_out[0])
    d_aro, nm_aro, nv_aro = zip(upd_a, upd_r, upd_o)
    d_cw, nm_cw, nv_cw = _adamw("adamw_conv_w", conv_w[0], g_conv_w, m_conv_w[0], v_conv_w[0])

    def sharded(t_in, t_aro, t_cw):
        return dict(w_in=t_in[None], conv_w=t_cw[None], w_branch_a=t_aro[0][None], w_branch_r=t_aro[1][None],
                    w_out=t_aro[2][None])

    order = ["pre_norm_w", "w_in", "b_in", "conv_w", "conv_b", "rg_wa", "rg_ba", "rg_wx", "rg_bx", "rg_lambda",
             "w_branch_a", "w_branch_r", "w_out", "post_norm_w"]
    outs = [loss, grad_x]
    for rep, shd in ((grads, sharded(g_w_in, g_aro, g_conv_w)), (deltas, sharded(d_w_in, d_aro, d_cw)),
                     (new_m, sharded(nm_w_in, nm_aro, nm_cw)), (new_v, sharded(nv_w_in, nv_aro, nv_cw))):
        both = {**rep, **shd}
        outs.extend(both[k] for k in order)
    return tuple(outs)
```

```python
import jax
import jax.numpy as jnp
from jax import lax
from jax.experimental import pallas as pl
from jax.experimental.pallas import tpu as pltpu

f32 = jnp.float32
bf16 = jnp.bfloat16

D = 1024
HEADS = 16
HEAD_PAIRS = 8
LANES = 128
NORM_EPS = 1e-6
MASK_VALUE = -1e30
RG_C = 8.0
QK_SCALE = 0.125
TQ = 256
ATT_GROUP = 8
ATT_GROUP_FWD = 16
TL = 256
TM = 256
PREV_ROWS = 16
IN_USED = 8 * D + HEADS
IN_TOTAL = 9 * D + HEADS
N_CHIPS = 4
N_DEV = 8
ADAM_LR, ADAM_B1, ADAM_B2, ADAM_EPS, ADAM_WD, ADAM_STEP = 0.001, 0.9, 0.999, 1e-08, 0.01, 10
VMEM_LIMIT = 56 * 1024 * 1024
MESH = pl.DeviceIdType.MESH


def _dot(a, b):
    return jnp.dot(a, b, preferred_element_type=f32)


def _dot_nt(a, b):
    return lax.dot_general(a, b, (((1,), (1,)), ((), ())), preferred_element_type=f32)


def _dot_tn(a, b):
    return lax.dot_general(a, b, (((0,), (0,)), ((), ())), preferred_element_type=f32)


def _sig(x):
    return 0.5 * jnp.tanh(0.5 * x) + 0.5


def _softplus(x):
    return jnp.maximum(x, 0.0) + jnp.log(1.0 + jnp.exp(-jnp.abs(x)))


def _params(sem, vmem=None):
    return pltpu.CompilerParams(dimension_semantics=sem, vmem_limit_bytes=vmem)


def _tile(tm, width, cb=0):
    return pl.BlockSpec((tm, width), lambda i, cb=cb: (i, cb))


def _whole(shape):
    nd = len(shape)
    return pl.BlockSpec(shape, lambda *_: (0,) * nd)


def _prenorm(x, w_pre):
    t = x.shape[0]

    def body(x_ref, w_ref, h_ref):
        xv = x_ref[...]
        r = lax.rsqrt(jnp.mean(xv * xv, axis=-1, keepdims=True) + NORM_EPS)
        h_ref[...] = (xv * r * w_ref[...]).astype(bf16)

    return pl.pallas_call(
        body, name="prenorm", grid=(t // TM,),
        in_specs=[_tile(TM, D), _whole((1, D))], out_specs=_tile(TM, D),
        out_shape=jax.ShapeDtypeStruct((t, D), bf16),
        compiler_params=_params(("parallel",)),
    )(x, w_pre)


def _mm(name, a, w, bias, out_dtype, tm, tn, w_is_nk=False):
    t, k = a.shape
    tm = min(tm, t)
    n = w.shape[0] if w_is_nk else w.shape[1]

    def body(a_ref, w_ref, *refs):
        acc = _dot_nt(a_ref[...], w_ref[...]) if w_is_nk else _dot(a_ref[...], w_ref[...])
        if bias is not None:
            acc = acc + refs[0][...]
        refs[-1][...] = acc.astype(out_dtype)

    in_specs = [pl.BlockSpec((tm, k), lambda i, j: (i, 0)),
                pl.BlockSpec((tn, k), lambda i, j: (j, 0)) if w_is_nk else pl.BlockSpec((k, tn), lambda i, j: (0, j))]
    args = [a, w]
    if bias is not None:
        in_specs.append(pl.BlockSpec((1, tn), lambda i, j: (0, j)))
        args.append(bias)
    return pl.pallas_call(
        body, name=name, grid=(t // tm, n // tn), in_specs=in_specs,
        out_specs=pl.BlockSpec((tm, tn), lambda i, j: (i, j)), out_shape=jax.ShapeDtypeStruct((t, n), out_dtype),
        compiler_params=_params(("parallel", "parallel"), VMEM_LIMIT),
    )(*args)


def _forget_prep(f128, seq):
    t = f128.shape[0]
    nb = seq // LANES

    def body(f_ref, c_ref):
        r = lax.broadcasted_iota(jnp.int32, (LANES, LANES), 0)
        cidx = lax.broadcasted_iota(jnp.int32, (LANES, LANES), 1)
        tri = (r >= cidx).astype(f32)
        carry = jnp.zeros((1, LANES), f32)
        for blk in range(nb):
            fv = f_ref[pl.ds(blk * LANES, LANES), :]
            lf = -_softplus(-fv)
            c_ref[pl.ds(blk * LANES, LANES), :] = (
                jnp.dot(tri, lf, preferred_element_type=f32, precision=lax.Precision.HIGHEST) + carry)
            carry = carry + jnp.sum(lf, axis=0, keepdims=True)

    return pl.pallas_call(
        body, name="forget_prep", grid=(t // seq,),
        in_specs=[pl.BlockSpec((seq, LANES), lambda b: (b, 0))],
        out_specs=pl.BlockSpec((seq, LANES), lambda b: (b, 0)),
        out_shape=jax.ShapeDtypeStruct((t, LANES), f32),
        compiler_params=_params(("parallel",)),
    )(f128)


def _split3(cv):
    hi = cv.astype(bf16)
    r1 = cv - hi.astype(f32)
    mid = r1.astype(bf16)
    lo = (r1 - mid.astype(f32)).astype(bf16)
    return hi, mid, lo


def _attn_prep(qkv, c):
    t = qkv.shape[0]

    def body(q_ref, k_ref, c_ref, qa_ref, ka_ref):
        lane = lax.broadcasted_iota(jnp.int32, (1, LANES), 1)
        cv = c_ref[...]
        one = jnp.ones((), bf16)
        zero = jnp.zeros((), bf16)
        q_ones = jnp.where((lane >= 67) & (lane < 70), one, zero)
        k_ones = jnp.where((lane >= 64) & (lane < 67), one, zero)
        for head in range(HEADS):
            pair = pl.ds((head // 2) * LANES, LANES)
            ch = jnp.sum(jnp.where(lane == head, cv, 0.0), axis=1, keepdims=True)
            hi, mid, lo = _split3(ch)
            q2, k2 = q_ref[:, pair], k_ref[:, pair]
            if head % 2 == 1:
                q2, k2 = pltpu.roll(q2, 64, 1), pltpu.roll(k2, 64, 1)
            qa = jnp.where(lane < 64, q2 * jnp.asarray(QK_SCALE, bf16),
                           jnp.where(lane == 64, hi, jnp.where(lane == 65, mid, jnp.where(lane == 66, lo, q_ones))))
            ka = jnp.where(lane < 64, k2,
                           jnp.where(lane == 67, -hi, jnp.where(lane == 68, -mid, jnp.where(lane == 69, -lo, k_ones))))
            qa_ref[:, pl.ds(head * LANES, LANES)] = qa
            ka_ref[:, pl.ds(head * LANES, LANES)] = ka

    tm = min(TM, t)
    out = pl.BlockSpec((tm, 2 * D), lambda i: (i, 0))
    return pl.pallas_call(
        body, name="attn_prep", grid=(t // tm,),
        in_specs=[_tile(tm, D, 0), _tile(tm, D, 1), _tile(tm, LANES)],
        out_specs=[out, out],
        out_shape=[jax.ShapeDtypeStruct((t, 2 * D), bf16)] * 2,
        compiler_params=_params(("parallel",)),
    )(qkv, qkv, c)


def _attn_fwd(qa, ka, qkv, rest, seq):
    t = qkv.shape[0]
    nb, nq = t // seq, seq // TQ

    hg = ATT_GROUP_FWD
    ng = HEADS // hg

    def body(q_ref, k_ref, v_ref, ga_ref, o_ref, pa_ref, lse_ref, acc_scr):
        qi, gi = pl.program_id(1), pl.program_id(2)
        krow = lax.broadcasted_iota(jnp.int32, (TQ, TQ), 0)
        qcol = lax.broadcasted_iota(jnp.int32, (TQ, TQ), 1)
        acc_scr[...] = jnp.zeros_like(acc_scr)

        def kv_step(kt, carry, masked):
            ks = pl.multiple_of(kt * TQ, TQ)
            sts = [_dot_nt(k_ref[pl.ds(ks, TQ), pl.ds(g * LANES, LANES)], q_ref[:, pl.ds(g * LANES, LANES)])
                   for g in range(hg)]
            if masked:
                sts = [jnp.where(krow <= qcol, st, MASK_VALUE) for st in sts]
            m_new = [jnp.maximum(carry[g][0], jnp.max(sts[g], axis=0, keepdims=True)) for g in range(hg)]
            ps = [jnp.exp(sts[g] - m_new[g]) for g in range(hg)]
            alphas = [jnp.exp(carry[g][0] - m_new[g]) for g in range(hg)]
            phi = [ps[g].astype(bf16) for g in range(hg)]
            plo = [(ps[g] - phi[g].astype(f32)).astype(bf16) for g in range(hg)]
            vs = [v_ref[pl.ds(ks, TQ), pl.ds(j * LANES, LANES)] for j in range(hg // 2)]
            pvs = [_dot_tn(vs[g // 2], phi[g]) + _dot_tn(vs[g // 2], plo[g]) for g in range(hg)]
            olds = [acc_scr[g] for g in range(hg)]
            for g in range(hg):
                acc_scr[g] = alphas[g] * olds[g] + pvs[g]
            return tuple((m_new[g], alphas[g] * carry[g][1] + jnp.sum(ps[g], axis=0, keepdims=True))
                         for g in range(hg))

        init = tuple((jnp.full((1, TQ), MASK_VALUE, f32), jnp.zeros((1, TQ), f32)) for _ in range(hg))
        carry = lax.fori_loop(0, qi, lambda kt, cr: kv_step(kt, cr, False), init)
        stats = kv_step(qi, carry, True)
        drow = lax.broadcasted_iota(jnp.int32, (LANES, TQ), 0)
        for g in range(hg):
            m, l = stats[g]
            lse_ref[0, pl.ds(hg * gi + g, 1), :] = m + jnp.log(l)
        for j in range(hg // 2):
            o2 = jnp.where(drow < 64, acc_scr[2 * j] / stats[2 * j][1], acc_scr[2 * j + 1] / stats[2 * j + 1][1]).T
            o_ref[:, pl.ds(j * LANES, LANES)] = o2
            ga = ga_ref[:, pl.ds(j * LANES, LANES)].astype(f32)
            pa_ref[:, pl.ds(j * LANES, LANES)] = (o2 * (ga * _sig(ga))).astype(bf16)

    vw = hg * 64
    tile = pl.BlockSpec((TQ, vw), lambda b, qi, gi: (b * nq + qi, gi))
    return pl.pallas_call(
        body, name="attn_fwd", grid=(nb, nq, ng),
        in_specs=[pl.BlockSpec((TQ, hg * LANES), lambda b, qi, gi: (b * nq + qi, gi)),
                  pl.BlockSpec((seq, hg * LANES), lambda b, qi, gi: (b, gi)),
                  pl.BlockSpec((seq, vw), lambda b, qi, gi: (b, 2 * ng + gi)), tile],
        out_specs=[tile, tile, pl.BlockSpec((1, HEADS, TQ), lambda b, qi, gi: (b * nq + qi, 0, 0))],
        out_shape=[jax.ShapeDtypeStruct((t, D), f32), jax.ShapeDtypeStruct((t, D), bf16),
                   jax.ShapeDtypeStruct((t // TQ, HEADS, TQ), f32)],
        scratch_shapes=[pltpu.VMEM((hg, LANES, TQ), f32)],
        compiler_params=_params(("parallel", "parallel", "arbitrary"), VMEM_LIMIT),
    )(qa, ka, qkv, rest)


def _shifted_rows(x, top8, prev8, shift, row, row8):
    body = pltpu.roll(x, shift, 0)
    head = jnp.where(row8 < shift, pltpu.roll(prev8, shift, 0), pltpu.roll(top8, shift, 0))
    return body, head


def _rnn_gates(xc, wa_ref, wx_ref, ba_ref, bx_ref, lam_ref):
    xcb = xc.astype(bf16)
    r = _sig(_dot(xcb, wa_ref[...]) + ba_ref[...])
    i = _sig(_dot(xcb, wx_ref[...]) + bx_ref[...])
    sp = _softplus(-lam_ref[...])
    log_a = (-RG_C) * r * sp
    th = jnp.tanh(log_a)
    w1 = (-2.0) * th / (1.0 - th)
    sq = jnp.sqrt(jnp.maximum(w1, 0.0))
    return r, i, sp, log_a, w1, sq


def _conv_tile(x_ref, xprev_ref, has_prev, cw_ref, cb_ref, xc_ref):
    row = lax.broadcasted_iota(jnp.int32, (TL, D), 0)
    row8 = lax.broadcasted_iota(jnp.int32, (8, D), 0)
    x = x_ref[...].astype(f32)
    top8 = x[:8]
    prev8 = jnp.where(has_prev, xprev_ref[...].astype(f32)[PREV_ROWS - 8:], 0.0)
    xc = cb_ref[...] + cw_ref[pl.ds(3, 1), :] * x
    xc8 = cb_ref[...] + cw_ref[pl.ds(3, 1), :] * top8
    for sh in range(1, 4):
        w = cw_ref[pl.ds(3 - sh, 1), :]
        xs, xs8 = _shifted_rows(x, top8, prev8, sh, row, row8)
        xc = xc + w * xs
        xc8 = xc8 + w * xs8
    xc_ref[...] = xc
    xc_ref[pl.ds(0, 8), :] = xc8


def _rnn_fwd(rest, conv_w, conv_b, wa_d, wx_d, ba, bx, lam, seq):
    t = rest.shape[0]
    nb, nt = t // seq, seq // TL

    def body(x_ref, xprev_ref, gr_ref, cw_ref, cb_ref, wa_ref, wx_ref, ba_ref, bx_ref, lam_ref,
             xc_ref, a_ref, h_ref, pr_ref, xc_scr, u_scr, h_scr, carry):
        tt = pl.program_id(1)
        _conv_tile(x_ref, xprev_ref, tt > 0, cw_ref, cb_ref, xc_scr)
        xc = xc_scr[...]
        xc_ref[...] = xc.astype(bf16)
        r, i, sp, log_a, w1, sq = _rnn_gates(xc, wa_ref, wx_ref, ba_ref, bx_ref, lam_ref)
        a_ref[...] = jnp.exp(log_a)
        u_scr[...] = sq * (i * xc)

        @pl.when(tt == 0)
        def _():
            carry[...] = jnp.zeros_like(carry)

        def step(s, h):
            h = a_ref[pl.ds(s, 1), :] * h + u_scr[pl.ds(s, 1), :]
            h_scr[pl.ds(s, 1), :] = h
            return h

        carry[...] = lax.fori_loop(0, TL, step, carry[...], unroll=8)
        gr = gr_ref[...].astype(f32)
        h = h_scr[...]
        h_ref[...] = h.astype(bf16)
        pr_ref[...] = (h * (gr * _sig(gr))).astype(bf16)

    tile = lambda cb: pl.BlockSpec((TL, D), lambda b, tt, cb=cb: (b * nt + tt, cb))
    prev = lambda cb: pl.BlockSpec(
        (PREV_ROWS, D), lambda b, tt, cb=cb: (jnp.maximum((b * nt + tt) * (TL // PREV_ROWS) - 1, 0), cb))
    vec = _whole((1, D))
    return pl.pallas_call(
        body, name="rnn_fwd", grid=(nb, nt),
        in_specs=[tile(1), prev(1), tile(2), _whole((4, D)), vec, _whole((D, D)), _whole((D, D)), vec, vec, vec],
        out_specs=[tile(0)] * 4,
        out_shape=[jax.ShapeDtypeStruct((t, D), dt) for dt in (bf16, f32, bf16, bf16)],
        scratch_shapes=[pltpu.VMEM((TL, D), f32)] * 3 + [pltpu.VMEM((1, D), f32)],
        compiler_params=_params(("parallel", "arbitrary"), VMEM_LIMIT),
    )(rest, rest, rest, conv_w, conv_b, wa_d, wx_d, ba, bx, lam)


def _merge(mga, mgr, ya, yr):
    return (_sig(mga.astype(f32)) * ya.astype(f32) + _sig(mgr.astype(f32)) * yr.astype(f32)).astype(bf16)


def _out_proj_loss(rest, ya, yr, w_out, x, tgt, w_post):
    t = x.shape[0]

    def body(mga_ref, mgr_ref, ya_ref, yr_ref, wo_ref, x_ref, t_ref, w_ref, do_ref, dy_ref, mrg_ref, loss_ref, dwp_ref):
        @pl.when(pl.program_id(0) == 0)
        def _():
            loss_ref[...] = jnp.zeros_like(loss_ref)
            dwp_ref[...] = jnp.zeros_like(dwp_ref)

        mrg = _merge(mga_ref[...], mgr_ref[...], ya_ref[...], yr_ref[...])
        mrg_ref[...] = mrg
        ov = _dot(mrg, wo_ref[...])
        w = w_ref[...]
        r2 = lax.rsqrt(jnp.mean(ov * ov, axis=-1, keepdims=True) + NORM_EPS)
        oh = ov * r2
        e = x_ref[...] + oh * w - t_ref[...]
        loss_ref[...] += 0.5 * jnp.sum(jnp.mean(e * e, axis=-1, keepdims=True))
        dy = e * (1.0 / D)
        dy_ref[...] = dy
        dwp_ref[...] += jnp.sum(dy * oh, axis=0, keepdims=True)
        doh = dy * w
        do_ref[...] = (r2 * (doh - oh * jnp.mean(doh * oh, axis=-1, keepdims=True))).astype(bf16)

    return pl.pallas_call(
        body, name="out_proj_loss", grid=(t // TM,),
        in_specs=[_tile(TM, D, 3), _tile(TM, D, 4), _tile(TM, D), _tile(TM, D), _whole((D, D)), _tile(TM, D),
                  _tile(TM, D), _whole((1, D))],
        out_specs=[_tile(TM, D), _tile(TM, D), _tile(TM, D), _whole((8, LANES)), _whole((1, D))],
        out_shape=[jax.ShapeDtypeStruct((t, D), bf16), jax.ShapeDtypeStruct((t, D), f32),
                   jax.ShapeDtypeStruct((t, D), bf16), jax.ShapeDtypeStruct((8, LANES), f32),
                   jax.ShapeDtypeStruct((1, D), f32)],
        compiler_params=_params(("arbitrary",), VMEM_LIMIT),
    )(rest, rest, ya, yr, w_out, x, tgt, w_post)


def _out_bwd(do, rest, ya, yr, w_out):
    t = do.shape[0]

    def body(do_ref, mga_ref, mgr_ref, ya_ref, yr_ref, w_ref, dya_ref, dyr_ref, dmga_ref, dmgr_ref):
        sa, sr = _sig(mga_ref[...].astype(f32)), _sig(mgr_ref[...].astype(f32))
        ya, yr = ya_ref[...].astype(f32), yr_ref[...].astype(f32)
        dm = _dot_nt(do_ref[...], w_ref[...])
        dya_ref[...] = (dm * sa).astype(bf16)
        dyr_ref[...] = (dm * sr).astype(bf16)
        dmga_ref[...] = (dm * ya * sa * (1.0 - sa)).astype(bf16)
        dmgr_ref[...] = (dm * yr * sr * (1.0 - sr)).astype(bf16)

    return pl.pallas_call(
        body, name="out_bwd", grid=(t // TM,),
        in_specs=[_tile(TM, D), _tile(TM, D, 3), _tile(TM, D, 4), _tile(TM, D), _tile(TM, D), _whole((D, D))],
        out_specs=[_tile(TM, D)] * 4,
        out_shape=[jax.ShapeDtypeStruct((t, D), bf16)] * 4,
        compiler_params=_params(("parallel",), VMEM_LIMIT),
    )(do, rest, rest, ya, yr, w_out)


def _branch_bwd(name, dyb, rest, gate_cb, act, w, act_grad_dtype, head_sums=False):
    t = dyb.shape[0]

    def body(dy_ref, g_ref, act_ref, w_ref, dact_ref, dg_ref, *delta_ref):
        dp = _dot_nt(dy_ref[...], w_ref[...])
        g = g_ref[...].astype(f32)
        sg = _sig(g)
        act = act_ref[...].astype(f32)
        dact = (dp * (g * sg)).astype(act_grad_dtype)
        dact_ref[...] = dact
        dg_ref[...] = (dp * act * (sg * (1.0 + g * (1.0 - sg)))).astype(bf16)
        if head_sums:
            ch = lax.broadcasted_iota(jnp.int32, (D, LANES), 0)
            hd = lax.broadcasted_iota(jnp.int32, (D, LANES), 1)
            pick = (ch // 64 == hd).astype(bf16)
            per_head = sum(_dot(piece, pick) for piece in _split3(dact.astype(f32) * act))
            delta_ref[0][0] = per_head.T[:HEADS, :]

    out_specs = [_tile(TM, D), _tile(TM, D)]
    out_shape = [jax.ShapeDtypeStruct((t, D), act_grad_dtype), jax.ShapeDtypeStruct((t, D), bf16)]
    if head_sums:
        out_specs.append(pl.BlockSpec((1, HEADS, TM), lambda i: (i, 0, 0)))
        out_shape.append(jax.ShapeDtypeStruct((t // TM, HEADS, TM), f32))
    return pl.pallas_call(
        body, name=name, grid=(t // TM,),
        in_specs=[_tile(TM, D), _tile(TM, D, gate_cb), _tile(TM, D), _whole((D, D))],
        out_specs=out_specs, out_shape=out_shape,
        compiler_params=_params(("parallel",), VMEM_LIMIT),
    )(dyb, rest, act, w)


def _rnn_bwd(dh, a, h, xc, rest, conv_w, conv_b, wa_d, wx_d, ba, bx, lam, seq):
    t = dh.shape[0]
    nb, nt = t // seq, seq // TL
    diag = (D // LANES, LANES, LANES)

    def body(dh_ref, a_ref, h_ref, hprev_ref, xc_ref, x_ref, xprev_ref, cw_ref, cb_ref, wa_ref, wx_ref,
             ba_ref, bx_ref, lam_ref, dxr_ref, dwa_ref, dwx_ref, vec_ref, g_scr, dxc_scr, dxr_scr, qcarry, dxc_next):
        b, tt = pl.program_id(0), pl.program_id(1)
        rt = nt - 1 - tt

        @pl.when((b == 0) & (tt == 0))
        def _():
            dwa_ref[...] = jnp.zeros_like(dwa_ref)
            dwx_ref[...] = jnp.zeros_like(dwx_ref)
            vec_ref[...] = jnp.zeros_like(vec_ref)

        @pl.when(tt == 0)
        def _():
            qcarry[...] = jnp.zeros_like(qcarry)
            dxc_next[...] = jnp.zeros_like(dxc_next)

        g_scr[...] = dh_ref[...].astype(f32)

        def step(k, q):
            s = TL - 1 - k
            g = g_scr[pl.ds(s, 1), :] + q
            g_scr[pl.ds(s, 1), :] = g
            return a_ref[pl.ds(s, 1), :] * g

        qcarry[...] = lax.fori_loop(0, TL, step, qcarry[...], unroll=8)

        row = lax.broadcasted_iota(jnp.int32, (TL, D), 0)
        row8 = lax.broadcasted_iota(jnp.int32, (8, D), 0)
        g = g_scr[...]
        av = a_ref[...]
        xc = xc_ref[...].astype(f32)
        hlast = jnp.where(rt > 0, hprev_ref[...].astype(f32)[PREV_ROWS - 1:], 0.0)
        hp = jnp.where(row == 0, hlast, pltpu.roll(h_ref[...].astype(f32), 1, 0))
        r, i, sp, log_a, w1, sq = _rnn_gates(xc, wa_ref, wx_ref, ba_ref, bx_ref, lam_ref)
        dix = g * sq
        di = dix * xc
        dxc = dix * i
        dsq = g * (i * xc)
        dlog_a = g * hp * av - dsq * jnp.where(sq > 0.0, (1.0 - w1) / sq, 0.0)
        dpr = (dlog_a * ((-RG_C) * sp)) * r * (1.0 - r)
        dpi = di * i * (1.0 - i)
        dprb, dpib, xcb = dpr.astype(bf16), dpi.astype(bf16), xc.astype(bf16)
        dxc = dxc + _dot_nt(dprb, wa_ref[...]) + _dot_nt(dpib, wx_ref[...])
        for j in range(D // LANES):
            cols = slice(j * LANES, (j + 1) * LANES)
            dwa_ref[j] += _dot_tn(xcb[:, cols], dprb[:, cols])
            dwx_ref[j] += _dot_tn(xcb[:, cols], dpib[:, cols])
        vec_ref[pl.ds(0, 1), :] += jnp.sum(dpr, axis=0, keepdims=True)
        vec_ref[pl.ds(1, 1), :] += jnp.sum(dpi, axis=0, keepdims=True)
        dsp = jnp.sum(dlog_a * ((-RG_C) * r), axis=0, keepdims=True)
        vec_ref[pl.ds(2, 1), :] += dsp * (-_sig(-lam_ref[...]))
        vec_ref[pl.ds(3, 1), :] += jnp.sum(dxc, axis=0, keepdims=True)

        dxc_scr[...] = dxc
        bot8 = dxc_scr[pl.ds(TL - 8, 8), :]
        nxt8 = dxc_next[...]
        dxr = cw_ref[pl.ds(3, 1), :] * dxc
        dxr8 = cw_ref[pl.ds(3, 1), :] * bot8
        for sh in range(1, 4):
            w = cw_ref[pl.ds(3 - sh, 1), :]
            dxr = dxr + w * pltpu.roll(dxc, TL - sh, 0)
            dxr8 = dxr8 + w * jnp.where(row8 < 8 - sh, pltpu.roll(bot8, 8 - sh, 0), pltpu.roll(nxt8, 8 - sh, 0))
        dxr_scr[...] = dxr
        dxr_scr[pl.ds(TL - 8, 8), :] = dxr8
        dxr_ref[...] = dxr_scr[...].astype(bf16)
        dxc_next[...] = dxc_scr[pl.ds(0, 8), :]

        x = x_ref[...].astype(f32)
        prev8 = jnp.where(rt > 0, xprev_ref[...].astype(f32)[PREV_ROWS - 8:], 0.0)
        dxc_top8 = dxc_scr[pl.ds(0, 8), :]
        vec_ref[pl.ds(7, 1), :] += jnp.sum(dxc * x, axis=0, keepdims=True)
        for sh in range(1, 4):
            inside = jnp.sum(dxc * jnp.where(row >= sh, pltpu.roll(x, sh, 0), 0.0), axis=0, keepdims=True)
            above = jnp.sum(dxc_top8 * jnp.where(row8 < sh, pltpu.roll(prev8, sh, 0), 0.0), axis=0, keepdims=True)
            vec_ref[pl.ds(7 - sh, 1), :] += inside + above

    tile = lambda cb: pl.BlockSpec((TL, D), lambda b, tt, cb=cb: (b * nt + nt - 1 - tt, cb))
    prev = lambda cb: pl.BlockSpec(
        (PREV_ROWS, D), lambda b, tt, cb=cb: (jnp.maximum((b * nt + nt - 1 - tt) * (TL // PREV_ROWS) - 1, 0), cb))
    vec = _whole((1, D))
    return pl.pallas_call(
        body, name="rnn_bwd", grid=(nb, nt),
        in_specs=[tile(0), tile(0), tile(0), prev(0), tile(0), tile(1), prev(1),
                  _whole((4, D)), vec, _whole((D, D)), _whole((D, D)), vec, vec, vec],
        out_specs=[tile(0), _whole(diag), _whole(diag), _whole((8, D))],
        out_shape=[jax.ShapeDtypeStruct((t, D), bf16), jax.ShapeDtypeStruct(diag, f32),
                   jax.ShapeDtypeStruct(diag, f32), jax.ShapeDtypeStruct((8, D), f32)],
        scratch_shapes=[pltpu.VMEM((TL, D), f32), pltpu.VMEM((TL, D), f32), pltpu.VMEM((TL, D), f32),
                        pltpu.VMEM((1, D), f32), pltpu.VMEM((8, D), f32)],
        compiler_params=_params(("arbitrary", "arbitrary"), VMEM_LIMIT),
    )(dh, a, h, h, xc, rest, rest, conv_w, conv_b, wa_d, wx_d, ba, bx, lam)


def _attn_bwd(qa, ka, qkv, doa, lse, delta, seq):
    t = qkv.shape[0]
    nb, nq = t // seq, seq // TQ
    hg = ATT_GROUP
    ng, npair = HEADS // hg, hg // 2

    def body(qa_ref, ka_ref, q_ref, k_ref, v_ref, do_ref, lse_ref, dl_ref, dq_ref, dk_ref, dv_ref, dc_ref,
             dqt_scr, dk_scr, dv_scr, ds_scr, kht_scr):
        gi, kt = pl.program_id(1), pl.program_id(2)
        lane = lax.broadcasted_iota(jnp.int32, (1, LANES), 1)
        krow = lax.broadcasted_iota(jnp.int32, (TQ, TQ), 0)
        qcol = lax.broadcasted_iota(jnp.int32, (TQ, TQ), 1)
        lmask = [(lane // 64) == hh for hh in range(2)]
        scale = jnp.asarray(QK_SCALE, bf16)

        @pl.when(kt == 0)
        def _():
            dqt_scr[...] = jnp.zeros_like(dqt_scr)

        dk_scr[...] = jnp.zeros_like(dk_scr)
        dv_scr[...] = jnp.zeros_like(dv_scr)
        ds_scr[...] = jnp.zeros_like(ds_scr)
        for g in range(hg):
            k2 = k_ref[:, pl.ds((g // 2) * LANES, LANES)]
            kht_scr[g] = jnp.where(lmask[g % 2], k2, jnp.zeros_like(k2)).T

        def q_step(qt, masked):
            qs = pl.multiple_of(qt * TQ, TQ)
            heads = range(hg)
            do2 = [do_ref[pl.ds(qs, TQ), pl.ds(j * LANES, LANES)] for j in range(npair)]
            q2 = [q_ref[pl.ds(qs, TQ), pl.ds(j * LANES, LANES)] for j in range(npair)]
            doh = [jnp.where(lmask[g % 2], do2[g // 2], jnp.zeros_like(do2[0])) for g in heads]
            qh = [jnp.where(lmask[g % 2], q2[g // 2], jnp.zeros_like(q2[0])) * scale for g in heads]
            st = [_dot_nt(ka_ref[:, pl.ds(g * LANES, LANES)], qa_ref[pl.ds(qs, TQ), pl.ds(g * LANES, LANES)])
                  for g in heads]
            if masked:
                st = [jnp.where(krow <= qcol, s, MASK_VALUE) for s in st]
            dp = [_dot_nt(v_ref[:, pl.ds((g // 2) * LANES, LANES)], doh[g]) for g in heads]
            p = [jnp.exp(st[g] - lse_ref[qt, pl.ds(hg * gi + g, 1), :]) for g in heads]
            ds = [p[g] * (dp[g] - dl_ref[qt, pl.ds(hg * gi + g, 1), :]) for g in heads]
            pb = [x.astype(bf16) for x in p]
            dsb = [x.astype(bf16) for x in ds]
            for j in range(npair):
                a, b = 2 * j, 2 * j + 1
                dv_scr[j] += _dot(pb[a], doh[a]) + _dot(pb[b], doh[b])
                dk_scr[j] += _dot(dsb[a], qh[a]) + _dot(dsb[b], qh[b])
                dqt_scr[qt, j] += (_dot(kht_scr[a], dsb[a]) + _dot(kht_scr[b], dsb[b])) * QK_SCALE
            for g in heads:
                ds_scr[g] += ds[g][:, :LANES] + ds[g][:, LANES:]

        q_step(kt, True)

        def loop_body(qt, carry):
            q_step(qt, False)
            return carry

        lax.fori_loop(kt + 1, nq, loop_body, 0)

        dc = jnp.zeros((TQ, LANES), f32)
        for g in range(hg):
            dc = jnp.where(lane == g, -jnp.sum(ds_scr[g], axis=1, keepdims=True), dc)
        dc_ref[...] = dc
        for j in range(npair):
            dk_ref[:, pl.ds(j * LANES, LANES)] = dk_scr[j].astype(bf16)
            dv_ref[:, pl.ds(j * LANES, LANES)] = dv_scr[j].astype(bf16)

        @pl.when(kt == nq - 1)
        def _():
            for qt in range(nq):
                for j in range(npair):
                    dq_ref[pl.ds(qt * TQ, TQ), pl.ds(j * LANES, LANES)] = dqt_scr[qt, j].T.astype(bf16)

    vw = hg * 64
    seqspec = pl.BlockSpec((seq, vw), lambda b, gi, kt: (b, gi))
    kspec = lambda off: pl.BlockSpec((TQ, vw), lambda b, gi, kt: (b * nq + kt, off + gi))
    rowspec = pl.BlockSpec((nq, HEADS, TQ), lambda b, gi, kt: (b, 0, 0))
    return pl.pallas_call(
        body, name="attn_bwd", grid=(nb, ng, nq),
        in_specs=[pl.BlockSpec((seq, hg * LANES), lambda b, gi, kt: (b, gi)),
                  pl.BlockSpec((TQ, hg * LANES), lambda b, gi, kt: (b * nq + kt, gi)),
                  seqspec, kspec(ng), kspec(2 * ng), seqspec, rowspec, rowspec],
        out_specs=[seqspec, kspec(0), kspec(0), pl.BlockSpec((TQ, LANES), lambda b, gi, kt: (b * nq + kt, gi))],
        out_shape=[jax.ShapeDtypeStruct((t, D), bf16)] * 3 + [jax.ShapeDtypeStruct((t, ng * LANES), f32)],
        scratch_shapes=[pltpu.VMEM((nq, npair, LANES, TQ), f32), pltpu.VMEM((npair, TQ, LANES), f32),
                        pltpu.VMEM((npair, TQ, LANES), f32), pltpu.VMEM((hg, TQ, LANES), f32),
                        pltpu.VMEM((hg, LANES, TQ), bf16)],
        compiler_params=_params(("parallel", "parallel", "arbitrary"), VMEM_LIMIT),
    )(qa, ka, qkv, qkv, qkv, doa, lse, delta)


def _forget_bwd(dc, f128, seq):
    t = f128.shape[0]
    nb = seq // LANES

    def body(dc_ref, f_ref, df_ref, dbf_ref):
        @pl.when(pl.program_id(0) == 0)
        def _():
            dbf_ref[...] = jnp.zeros_like(dbf_ref)

        r = lax.broadcasted_iota(jnp.int32, (LANES, LANES), 0)
        cidx = lax.broadcasted_iota(jnp.int32, (LANES, LANES), 1)
        tri = (r <= cidx).astype(f32)
        carry = jnp.zeros((1, LANES), f32)
        total = jnp.zeros((1, LANES), f32)
        for blk in reversed(range(nb)):
            dcb = dc_ref[pl.ds(blk * LANES, LANES), :]
            dlf = jnp.dot(tri, dcb, preferred_element_type=f32, precision=lax.Precision.HIGHEST) + carry
            df = dlf * _sig(-f_ref[pl.ds(blk * LANES, LANES), :])
            df_ref[pl.ds(blk * LANES, LANES), :] = df.astype(bf16)
            total = total + jnp.sum(df, axis=0, keepdims=True)
            carry = carry + jnp.sum(dcb, axis=0, keepdims=True)
        dbf_ref[...] += total

    return pl.pallas_call(
        body, name="forget_bwd", grid=(t // seq,),
        in_specs=[pl.BlockSpec((seq, LANES), lambda b: (b, 0)), pl.BlockSpec((seq, LANES), lambda b: (b, 0))],
        out_specs=[pl.BlockSpec((seq, LANES), lambda b: (b, 0)), _whole((1, LANES))],
        out_shape=[jax.ShapeDtypeStruct((t, LANES), bf16), jax.ShapeDtypeStruct((1, LANES), f32)],
        compiler_params=_params(("arbitrary",)),
    )(dc, f128)


def _in_bwd(dz, df, x, dy, w_qkv, w_rest, w_f, w_pre):
    t = x.shape[0]
    n_qkv = w_qkv.shape[0] // D
    n_rest = w_rest.shape[0] // D

    def body(*refs):
        dz_refs = refs[:n_qkv + n_rest]
        df_ref, x_ref, dy_ref, wq_ref, wr_ref, wf_ref, wp_ref, gx_ref, dwp_ref = refs[n_qkv + n_rest:]

        @pl.when(pl.program_id(0) == 0)
        def _():
            dwp_ref[...] = jnp.zeros_like(dwp_ref)

        dh = _dot(df_ref[...], wf_ref[...])
        for p in range(n_qkv):
            dh = dh + _dot(dz_refs[p][...], wq_ref[pl.ds(p * D, D), :])
        for p in range(n_rest):
            dh = dh + _dot(dz_refs[n_qkv + p][...], wr_ref[pl.ds(p * D, D), :])
        xv = x_ref[...]
        r1 = lax.rsqrt(jnp.mean(xv * xv, axis=-1, keepdims=True) + NORM_EPS)
        xh = xv * r1
        dwp_ref[...] += jnp.sum(dh * xh, axis=0, keepdims=True)
        dxh = dh * wp_ref[...]
        gx_ref[...] = dy_ref[...] + r1 * (dxh - xh * jnp.mean(dxh * xh, axis=-1, keepdims=True))

    once = lambda shape: pl.BlockSpec(shape, lambda i: (0, 0), pipeline_mode=pl.Buffered(1))
    return pl.pallas_call(
        body, name="in_bwd", grid=(t // TM,),
        in_specs=[_tile(TM, D)] * (n_qkv + n_rest) + [_tile(TM, LANES), _tile(TM, D), _tile(TM, D),
                  once(w_qkv.shape), once(w_rest.shape), once(w_f.shape), _whole((1, D))],
        out_specs=[_tile(TM, D), _whole((1, D))],
        out_shape=[jax.ShapeDtypeStruct((t, D), f32), jax.ShapeDtypeStruct((1, D), f32)],
        compiler_params=_params(("arbitrary",), VMEM_LIMIT),
    )(*dz, df, x, dy, w_qkv, w_rest, w_f, w_pre)


def _tn_mm(name, a, b, tn, tk=2048):
    t, k = a.shape
    tk = min(tk, t)
    n = b.shape[1]

    def body(a_ref, b_ref, o_ref, s_ref):
        j, kk = pl.program_id(0), pl.program_id(1)

        @pl.when(kk == 0)
        def _():
            o_ref[...] = jnp.zeros_like(o_ref)

        @pl.when((j == 0) & (kk == 0))
        def _():
            s_ref[...] = jnp.zeros_like(s_ref)

        av = a_ref[...]
        o_ref[...] += _dot_tn(av, b_ref[...])

        @pl.when(j == 0)
        def _():
            s_ref[...] += jnp.sum(av.astype(f32), axis=0, keepdims=True)

    return pl.pallas_call(
        body, name=name, grid=(n // tn, t // tk),
        in_specs=[pl.BlockSpec((tk, k), lambda j, kk: (kk, 0)), pl.BlockSpec((tk, tn), lambda j, kk: (kk, j))],
        out_specs=[pl.BlockSpec((k, tn), lambda j, kk: (0, j)), _whole((1, k))],
        out_shape=[jax.ShapeDtypeStruct((k, n), f32), jax.ShapeDtypeStruct((1, k), f32)],
        compiler_params=_params(("arbitrary", "arbitrary"), VMEM_LIMIT),
    )(a, b)


def _position():
    return lax.axis_index("x"), lax.axis_index("y"), lax.axis_index("c")


def _gather_shards(parts, small):
    n = len(parts)
    halves = [p.shape[0] // 2 for p in parts]

    def body(*refs):
        srcs, small_src = refs[:n], refs[n]
        dsts, small_dst = refs[n + 1:2 * n + 1], refs[2 * n + 1]
        send, recv, local = refs[2 * n + 2:]
        x, y, c = _position()
        me = 2 * x + y
        chips = [(1 - x, y), (x, 1 - y), (1 - x, 1 - y)]
        ids = [2 * px + py for px, py in chips]

        def half(a, shard, which):
            return dsts[a].at[shard, pl.ds(which * halves[a], halves[a]), :]

        def over_ici(a, j, shard):
            px, py = chips[j]
            return pltpu.make_async_remote_copy(
                src_ref=srcs[a].at[pl.ds(c * halves[a], halves[a]), :], dst_ref=half(a, shard, c),
                send_sem=send.at[a * 3 + j], recv_sem=recv.at[a * 3 + j], device_id=(px, py, c), device_id_type=MESH)

        def to_sibling(a, j, which):
            k = 3 * n + a * 3 + j
            return pltpu.make_async_remote_copy(
                src_ref=half(a, ids[j], which), dst_ref=half(a, ids[j], which), send_sem=send.at[k],
                recv_sem=recv.at[k], device_id=(x, y, 1 - c), device_id_type=MESH)

        def small_copy(j, shard):
            px, py = chips[j]
            return pltpu.make_async_remote_copy(
                src_ref=small_src, dst_ref=small_dst.at[shard], send_sem=send.at[6 * n + j], recv_sem=recv.at[6 * n + j],
                device_id=(px, py, c), device_id_type=MESH)

        own = [pltpu.make_async_copy(srcs[a], dsts[a].at[me], local.at[a]) for a in range(n)]
        own.append(pltpu.make_async_copy(small_src, small_dst.at[me], local.at[n]))
        for cp in own:
            cp.start()
        first = [over_ici(a, j, me) for j in range(3) for a in range(n)] + [small_copy(j, me) for j in range(3)]
        for cp in first:
            cp.start()
        passed = []
        for j in range(3):
            for a in range(n):
                over_ici(a, j, ids[j]).wait_recv()
                passed.append(to_sibling(a, j, c))
                passed[-1].start()
        for j in range(3):
            small_copy(j, ids[j]).wait_recv()
            for a in range(n):
                to_sibling(a, j, 1 - c).wait_recv()
        for cp in first + passed:
            cp.wait_send()
        for cp in own:
            cp.wait()

    vm = pl.BlockSpec(memory_space=pltpu.VMEM)
    return pl.pallas_call(
        body, name="gather_shards",
        in_specs=[vm] * (n + 1), out_specs=[vm] * (n + 1),
        out_shape=[jax.ShapeDtypeStruct((N_CHIPS,) + p.shape, p.dtype) for p in parts + [small]],
        scratch_shapes=[pltpu.SemaphoreType.DMA((6 * n + 3,)), pltpu.SemaphoreType.DMA((6 * n + 3,)),
                        pltpu.SemaphoreType.DMA((n + 1,))],
        compiler_params=pltpu.CompilerParams(vmem_limit_bytes=VMEM_LIMIT),
    )(*parts, small)


def _allsum_rows(part):
    rows_n = part.shape[0]

    def body(x_ref, gath_ref, sum_ref, send_sems, recv_sems, local_sem):
        x, y, c = _position()
        me, sibling = (x, y, c), (x, y, 1 - c)
        chips = [(1 - x, y), (x, 1 - y), (1 - x, 1 - y)]

        def rows(px, py, pc):
            return gath_ref.at[pl.ds((4 * px + 2 * py + pc) * rows_n, rows_n), :]

        def copy(k, block, to, src=None):
            return pltpu.make_async_remote_copy(
                src_ref=rows(*block) if src is None else src, dst_ref=rows(*block),
                send_sem=send_sems.at[k], recv_sem=recv_sems.at[k], device_id=to, device_id_type=MESH)

        mine = pltpu.make_async_copy(x_ref, rows(*me), local_sem)
        mine.start()
        first = [copy(0, me, sibling, src=x_ref)]
        first += [copy(1 + j, me, (*chip, c), src=x_ref) for j, chip in enumerate(chips)]
        for cp in first:
            cp.start()
        passed = [copy(4 + j, (*chip, c), sibling) for j, chip in enumerate(chips)]
        for j, chip in enumerate(chips):
            copy(1 + j, (*chip, c), me).wait_recv()
            passed[j].start()
        copy(0, sibling, me).wait_recv()
        for j, chip in enumerate(chips):
            copy(4 + j, (*chip, 1 - c), me).wait_recv()
        for cp in first + passed:
            cp.wait_send()
        mine.wait()
        total = gath_ref[pl.ds(0, rows_n), :]
        for d in range(1, N_DEV):
            total = total + gath_ref[pl.ds(d * rows_n, rows_n), :]
        sum_ref[...] = total

    vm = pl.BlockSpec(memory_space=pltpu.VMEM)
    return pl.pallas_call(
        body, name="allsum_rows", in_specs=[vm], out_specs=[vm, vm],
        out_shape=[jax.ShapeDtypeStruct((N_DEV * rows_n, D), f32), jax.ShapeDtypeStruct((rows_n, D), f32)],
        scratch_shapes=[pltpu.SemaphoreType.DMA((7,)), pltpu.SemaphoreType.DMA((7,)), pltpu.SemaphoreType.DMA],
    )(part)[1]


PAIR_ROWS = 16


def _pair_reduce(name, pieces):
    _, r, n = pieces.shape

    def body(p_ref, o_ref, land, send, recv):
        x, y, c = _position()

        def remote(j, half):
            return pltpu.make_async_remote_copy(
                src_ref=p_ref.at[2 * j + half], dst_ref=land.at[j], send_sem=send.at[j], recv_sem=recv.at[j],
                device_id=(x, y, 1 - c), device_id_type=MESH)

        sends = [remote(j, 1 - c) for j in range(N_CHIPS)]
        for cp in sends:
            cp.start()
        for j in range(N_CHIPS):
            remote(j, c).wait_recv()

            def add_rows(i, carry, j=j):
                rows = pl.ds(pl.multiple_of(i * PAIR_ROWS, PAIR_ROWS), PAIR_ROWS)
                o_ref[j, rows, :] = (p_ref[2 * j + c, rows, :].astype(f32) + land[j, rows, :].astype(f32)).astype(bf16)
                return carry

            lax.fori_loop(0, r // PAIR_ROWS, add_rows, 0)
        for cp in sends:
            cp.wait_send()

    vm = pl.BlockSpec(memory_space=pltpu.VMEM)
    return pl.pallas_call(
        body, name=name, in_specs=[vm], out_specs=vm,
        out_shape=jax.ShapeDtypeStruct((N_CHIPS, r, n), bf16),
        scratch_shapes=[pltpu.VMEM((N_CHIPS, r, n), bf16), pltpu.SemaphoreType.DMA((N_CHIPS,)),
                        pltpu.SemaphoreType.DMA((N_CHIPS,))],
        compiler_params=pltpu.CompilerParams(vmem_limit_bytes=VMEM_LIMIT),
    )(pieces)


def _chip_exchange(arrs):
    n = len(arrs)

    def body(*refs):
        srcs, dsts = refs[:n], refs[n:2 * n]
        send, recv, local = refs[2 * n:]
        x, y, c = _position()
        me = 2 * x + y
        chips = [(1 - x, y), (x, 1 - y), (1 - x, 1 - y)]

        def remote(a, j, piece, landing):
            px, py = chips[j]
            return pltpu.make_async_remote_copy(
                src_ref=srcs[a].at[piece], dst_ref=dsts[a].at[landing], send_sem=send.at[a * 3 + j],
                recv_sem=recv.at[a * 3 + j], device_id=(px, py, c), device_id_type=MESH)

        own = [pltpu.make_async_copy(srcs[a].at[me], dsts[a].at[me], local.at[a]) for a in range(n)]
        sends = [remote(a, j, 2 * px + py, me) for j, (px, py) in enumerate(chips) for a in range(n)]
        for cp in sends + own:
            cp.start()
        for j, (px, py) in enumerate(chips):
            for a in range(n):
                remote(a, j, me, 2 * px + py).wait_recv()
        for cp in sends:
            cp.wait_send()
        for cp in own:
            cp.wait()

    anyspec = pl.BlockSpec(memory_space=pl.ANY)
    return pl.pallas_call(
        body, name="chip_exchange", in_specs=[anyspec] * n, out_specs=[anyspec] * n,
        out_shape=[jax.ShapeDtypeStruct(a.shape, a.dtype) for a in arrs],
        scratch_shapes=[pltpu.SemaphoreType.DMA((3 * n,)), pltpu.SemaphoreType.DMA((3 * n,)),
                        pltpu.SemaphoreType.DMA((n,))],
    )(*arrs)


def _swap_halves(arrs):
    n = len(arrs)

    def body(*refs):
        srcs, dsts = refs[:n], refs[n:2 * n]
        send, recv, local = refs[2 * n:]
        x, y, c = _position()

        def remote(a, landing):
            return pltpu.make_async_remote_copy(
                src_ref=srcs[a], dst_ref=dsts[a].at[landing], send_sem=send.at[a], recv_sem=recv.at[a],
                device_id=(x, y, 1 - c), device_id_type=MESH)

        own = [pltpu.make_async_copy(srcs[a], dsts[a].at[c], local.at[a]) for a in range(n)]
        sends = [remote(a, c) for a in range(n)]
        for cp in sends + own:
            cp.start()
        for a in range(n):
            remote(a, 1 - c).wait_recv()
        for cp in sends:
            cp.wait_send()
        for cp in own:
            cp.wait()

    vm = pl.BlockSpec(memory_space=pltpu.VMEM)
    return pl.pallas_call(
        body, name="swap_halves", in_specs=[vm] * n, out_specs=[vm] * n,
        out_shape=[jax.ShapeDtypeStruct((2,) + a.shape, a.dtype) for a in arrs],
        scratch_shapes=[pltpu.SemaphoreType.DMA((n,)), pltpu.SemaphoreType.DMA((n,)), pltpu.SemaphoreType.DMA((n,))],
        compiler_params=pltpu.CompilerParams(vmem_limit_bytes=VMEM_LIMIT),
    )(*arrs)


def _row_block(r):
    return 128 if r % 128 == 0 else r


def _sum_slots(name, slots):
    s, r, n = slots.shape
    rb = _row_block(r)

    def body(s_ref, o_ref):
        total = s_ref[0].astype(f32)
        for d in range(1, s):
            total = total + s_ref[d].astype(f32)
        o_ref[...] = total

    return pl.pallas_call(
        body, name=name, grid=(r // rb,),
        in_specs=[pl.BlockSpec((s, rb, n), lambda i: (0, i, 0))],
        out_specs=pl.BlockSpec((rb, n), lambda i: (i, 0)),
        out_shape=jax.ShapeDtypeStruct((r, n), f32),
        compiler_params=_params(("parallel",), VMEM_LIMIT),
    )(slots)


def _adamw(name, w, g, m, v):
    r, n = w.shape
    if r % 128 == 0 or r * n <= 128 * 1024:
        rb, nb = _row_block(r), n
    else:
        rb, nb = r, LANES

    def body(w_ref, g_ref, m_ref, v_ref, d_ref, nm_ref, nv_ref):
        gv = g_ref[...]
        m2 = ADAM_B1 * m_ref[...] + (1.0 - ADAM_B1) * gv
        v2 = ADAM_B2 * v_ref[...] + (1.0 - ADAM_B2) * (gv * gv)
        m_hat = m2 / (1.0 - ADAM_B1 ** ADAM_STEP)
        v_hat = v2 / (1.0 - ADAM_B2 ** ADAM_STEP)
        d_ref[...] = (-ADAM_LR) * (m_hat / (jnp.sqrt(v_hat) + ADAM_EPS) + ADAM_WD * w_ref[...])
        nm_ref[...] = m2
        nv_ref[...] = v2

    spec = pl.BlockSpec((rb, nb), lambda i, j: (i, j))
    return pl.pallas_call(
        body, name=name, grid=(r // rb, n // nb), in_specs=[spec] * 4, out_specs=[spec] * 3,
        out_shape=[jax.ShapeDtypeStruct((r, n), f32)] * 3,
        compiler_params=_params(("parallel", "parallel"), VMEM_LIMIT),
    )(w, g, m, v)


def _local_step(x2, tgt2, seq, wt):
    nb = x2.shape[0] // seq
    h = _prenorm(x2, wt["pre_w"])
    qkv = _mm("in_qkv", h, wt["w_qkv"], wt["b_qkv"], bf16, 1024, 1024, w_is_nk=True)
    rest = _mm("in_rest", h, wt["w_rest"], wt["b_rest"], bf16, 1024, 1024, w_is_nk=True)
    f128 = _mm("in_f", h, wt["w_f"], wt["b_f"], f32, 1024, LANES, w_is_nk=True)
    c = _forget_prep(f128, seq)
    qa, ka = _attn_prep(qkv, c)
    o_att, pa, lse = _attn_fwd(qa, ka, qkv, rest, seq)
    ya = _mm("proj_a", pa, wt["w_a"], None, bf16, 1024, D)
    rnn_w = (wt["conv_w"], wt["conv_b"], wt["wa_d"], wt["wx_d"], wt["ba"], wt["bx"], wt["lam"])
    xc, a, hrec, pr = _rnn_fwd(rest, *rnn_w, seq)
    yr = _mm("proj_r", pr, wt["w_r"], None, bf16, 1024, D)
    do, dy, mrg, loss8, d_post = _out_proj_loss(rest, ya, yr, wt["w_o"], x2, tgt2, wt["post_w"])
    dya, dyr, dmga, dmgr = _out_bwd(do, rest, ya, yr, wt["w_o"])
    doa, dga, delta = _branch_bwd("branch_a_bwd", dya, rest, 0, o_att, wt["w_a"], bf16, head_sums=True)
    dhrec, dgr = _branch_bwd("branch_r_bwd", dyr, rest, 2, hrec, wt["w_r"], bf16)
    d_wo, _ = _tn_mm("dw_out", mrg, do, D)
    d_wa, _ = _tn_mm("dw_branch_a", pa, dya, D)
    d_wr, _ = _tn_mm("dw_branch_r", pr, dyr, D)
    dxr, d_wad, d_wxd, vec = _rnn_bwd(dhrec, a, hrec, xc, rest, *rnn_w, seq)
    dq, dk, dv, dc_pairs = _attn_bwd(qa, ka, qkv, doa, lse, delta, seq)
    dc = dc_pairs.reshape(-1, HEADS // ATT_GROUP, LANES)[:, :, :ATT_GROUP].reshape(-1, HEADS)
    df, db_f = _forget_bwd(_pad_cols(dc, LANES), f128, seq)
    pieces = [dq, dk, dv, dga, dxr, dgr, dmga, dmgr]
    gx, d_pre = _in_bwd(pieces, df, x2, dy, wt["w_qkv"], wt["w_rest"], wt["w_f"], wt["pre_w"])
    names = ["q", "k", "v", "ga", "xr", "gr", "mga", "mgr"]
    dws, dbs = [], []
    for nm, piece in zip(names, pieces):
        dw_p, db_p = _tn_mm("dw_in_" + nm, piece, h, D)
        dws.append(dw_p)
        dbs.append(db_p)
    dw_f, _ = _tn_mm("dw_in_f", df, h, D)
    zeros_w = jnp.zeros((IN_TOTAL - IN_USED, D), f32)
    d_w_in = jnp.concatenate(dws[:3] + [dw_f[:HEADS]] + dws[3:] + [zeros_w], axis=0)
    d_b_in = jnp.concatenate(dbs[:3] + [db_f[:, :HEADS]] + dbs[3:] + [zeros_w[:, :1].T], axis=1)
    return dict(loss=loss8[0, 0], grad_x=gx, pre_w=d_pre, w_in=d_w_in, b_in=d_b_in, conv_w=vec[4:8], conv_b=vec[3:4],
                wa_d=d_wad, ba=vec[0:1], wx_d=d_wxd, bx=vec[1:2], lam=vec[2:3], w_a=d_wa, w_r=d_wr, w_o=d_wo,
                post_w=d_post)


def _block_diag(w):
    g, bw, _ = w.shape
    eye = jnp.eye(g, dtype=w.dtype)
    return (w[:, :, None, :] * eye[:, None, :, None]).reshape(g * bw, g * bw)


def _gate_blocks(diag):
    half = diag.shape[1] // 2
    return jnp.stack([diag[:, :half, :half], diag[:, half:, half:]], axis=1).reshape(-1, half, half)


def _pad_cols(a, n):
    return jnp.pad(a, ((0, 0), (0, n - a.shape[1])))


def _pad_rows(a, n):
    return jnp.pad(a, ((0, n - a.shape[0]), (0, 0)))


def kernel(x, pre_norm_w, w_in, b_in, conv_w, conv_b, rg_wa, rg_ba, rg_wx, rg_bx, rg_lambda, w_branch_a, w_branch_r, w_out, post_norm_w, loss_target, m_pre_norm_w, m_w_in, m_b_in, m_conv_w, m_conv_b, m_rg_wa, m_rg_ba, m_rg_wx, m_rg_bx, m_rg_lambda, m_w_branch_a, m_w_branch_r, m_w_out, m_post_norm_w, v_pre_norm_w, v_w_in, v_b_in, v_conv_w, v_conv_b, v_rg_wa, v_rg_ba, v_rg_wx, v_rg_bx, v_rg_lambda, v_w_branch_a, v_w_branch_r, v_w_out, v_post_norm_w):
    nb, seq, _ = x.shape
    chip = 2 * lax.axis_index("x") + lax.axis_index("y")
    n_groups = rg_wa.shape[1]

    w_in_t = jnp.transpose(w_in[0])
    shard_cols = w_in_t.shape[0]
    padded = -(-shard_cols // 32) * 32
    g_in, g_a, g_r, g_o, g_cw = _gather_shards(
        [_pad_rows(w_in_t.astype(bf16), padded), w_branch_a[0].astype(bf16), w_branch_r[0].astype(bf16),
         w_out[0].astype(bf16)], conv_w[0])
    w_full = jnp.concatenate([g_in[j, :shard_cols] for j in range(N_CHIPS)], axis=0)
    q_end, f_end = 3 * D, 3 * D + HEADS
    wt = dict(
        pre_w=pre_norm_w, post_w=post_norm_w,
        w_qkv=w_full[:q_end], b_qkv=b_in[:, :q_end],
        w_f=_pad_rows(w_full[q_end:f_end], LANES), b_f=_pad_cols(b_in[:, q_end:f_end], LANES),
        w_rest=w_full[f_end:IN_USED], b_rest=b_in[:, f_end:IN_USED],
        w_a=g_a.reshape(D, D), w_r=g_r.reshape(D, D), w_o=g_o.reshape(D, D),
        conv_w=jnp.transpose(g_cw, (1, 0, 2)).reshape(4, D), conv_b=conv_b,
        wa_d=_block_diag(rg_wa[0]).astype(bf16), wx_d=_block_diag(rg_wx[0]).astype(bf16),
        ba=rg_ba, bx=rg_bx, lam=rg_lambda)

    part = _local_step(x.reshape(nb * seq, D), loss_target.reshape(nb * seq, D), seq, wt)
    loss = lax.psum(part["loss"], ("x", "y", "c"))
    grad_x = part["grad_x"].reshape(nb, seq, D)

    small = jnp.concatenate([
        part["pre_w"], _pad_cols(part["b_in"], 10 * D).reshape(10, D), part["conv_b"],
        _gate_blocks(part["wa_d"]).reshape(-1, D), part["ba"],
        _gate_blocks(part["wx_d"]).reshape(-1, D), part["bx"], part["lam"], part["post_w"],
        part["conv_w"]], axis=0)
    n_small = small.shape[0]
    n_rep = n_small - 4
    tot = _allsum_rows(_pad_rows(small, -(-n_small // 8) * 8))
    g_rep = tot[:n_rep]
    g_conv_w = lax.dynamic_slice_in_dim(tot[n_rep:n_small], chip * (D // N_CHIPS), D // N_CHIPS, axis=1)

    def unpack(p):
        o = [0]

        def take(k):
            o[0] += k
            return p[o[0] - k:o[0]]

        pre = take(1)
        b = take(10).reshape(1, 10 * D)[:, :IN_TOTAL]
        cb = take(1)
        wa = take(64).reshape(rg_wa.shape)
        ba = take(1)
        wx = take(64).reshape(rg_wx.shape)
        bx = take(1)
        lam = take(1)
        post = take(1)
        return dict(pre_norm_w=pre, b_in=b, conv_b=cb, rg_wa=wa, rg_ba=ba, rg_wx=wx, rg_bx=bx, rg_lambda=lam,
                    post_norm_w=post)

    grads = unpack(g_rep)
    replicated = dict(
        pre_norm_w=(pre_norm_w, m_pre_norm_w, v_pre_norm_w), b_in=(b_in, m_b_in, v_b_in),
        conv_b=(conv_b, m_conv_b, v_conv_b), rg_wa=(rg_wa, m_rg_wa, v_rg_wa), rg_ba=(rg_ba, m_rg_ba, v_rg_ba),
        rg_wx=(rg_wx, m_rg_wx, v_rg_wx), rg_bx=(rg_bx, m_rg_bx, v_rg_bx),
        rg_lambda=(rg_lambda, m_rg_lambda, v_rg_lambda), post_norm_w=(post_norm_w, m_post_norm_w, v_post_norm_w))
    deltas, new_m, new_v = {}, {}, {}
    for name, (w, m, v) in replicated.items():
        as2d = lambda a: a.reshape(-1, D) if a.ndim > 2 else a
        upd = _adamw("adamw_" + name, as2d(w), as2d(grads[name]), as2d(m), as2d(v))
        deltas[name], new_m[name], new_v[name] = [a.reshape(w.shape) for a in upd]

    p_in = jnp.pad(part["w_in"].reshape(N_CHIPS, shard_cols, D), ((0, 0), (0, padded - shard_cols), (0, 0)))
    p_in = p_in.reshape(N_DEV, padded // 2, D)
    p_aro = jnp.concatenate([part[k].reshape(N_DEV, D // N_DEV, D) for k in ("w_a", "w_r", "w_o")], axis=1)
    s_in, s_aro = _chip_exchange([_pair_reduce("pair_w_in", p_in.astype(bf16)),
                                  _pair_reduce("pair_w_aro", p_aro.astype(bf16))])
    f_in, f_aro = _swap_halves([_sum_slots("sum_w_in", s_in), _sum_slots("sum_w_aro", s_aro)])
    g_w_in_t = f_in.reshape(padded, D)[:shard_cols]
    rows = D // N_DEV
    g_aro = [f_aro[:, i * rows:(i + 1) * rows, :].reshape(2 * rows, D) for i in range(3)]

    w_in_upd = _adamw("adamw_w_in", w_in_t, g_w_in_t, jnp.transpose(m_w_in[0]), jnp.transpose(v_w_in[0]))
    g_w_in, d_w_in, nm_w_in, nv_w_in = [jnp.transpose(a) for a in (g_w_in_t, *w_in_upd)]
    upd_a = _adamw("adamw_w_branch_a", w_branch_a[0], g_aro[0], m_w_branch_a[0], v_w_branch_a[0])
    upd_r = _adamw("adamw_w_branch_r", w_branch_r[0], g_aro[1], m_w_branch_r[0], v_w_branch_r[0])
    upd_o = _adamw("adamw_w_out", w_out[0], g_aro[2], m_w_out[0], v_w_out[0])
    d_aro, nm_aro, nv_aro = zip(upd_a, upd_r, upd_o)
    d_cw, nm_cw, nv_cw = _adamw("adamw_conv_w", conv_w[0], g_conv_w, m_conv_w[0], v_conv_w[0])

    def sharded(t_in, t_aro, t_cw):
        return dict(w_in=t_in[None], conv_w=t_cw[None], w_branch_a=t_aro[0][None], w_branch_r=t_aro[1][None],
                    w_out=t_aro[2][None])

    order = ["pre_norm_w", "w_in", "b_in", "conv_w", "conv_b", "rg_wa", "rg_ba", "rg_wx", "rg_bx", "rg_lambda",
             "w_branch_a", "w_branch_r", "w_out", "post_norm_w"]
    outs = [loss, grad_x]
    for rep, shd in ((grads, sharded(g_w_in, g_aro, g_conv_w)), (deltas, sharded(d_w_in, d_aro, d_cw)),
                     (new_m, sharded(nm_w_in, nm_aro, nm_cw)), (new_v, sharded(nv_w_in, nv_aro, nv_cw))):
        both = {**rep, **shd}
        outs.extend(both[k] for k in order)
    return tuple(outs)
```

```python
import jax
import jax.numpy as jnp
from jax import lax
from jax.experimental import pallas as pl
from jax.experimental.pallas import tpu as pltpu

f32 = jnp.float32
bf16 = jnp.bfloat16

D = 1024
HEADS = 16
HEAD_PAIRS = 8
LANES = 128
NORM_EPS = 1e-6
MASK_VALUE = -1e30
RG_C = 8.0
QK_SCALE = 0.125
TQ = 256
ATT_GROUP = 8
ATT_GROUP_FWD = 16
TL = 256
TM = 256
PREV_ROWS = 16
IN_USED = 8 * D + HEADS
IN_TOTAL = 9 * D + HEADS
N_CHIPS = 4
N_DEV = 8
ADAM_LR, ADAM_B1, ADAM_B2, ADAM_EPS, ADAM_WD, ADAM_STEP = 0.001, 0.9, 0.999, 1e-08, 0.01, 10
VMEM_LIMIT = 56 * 1024 * 1024
MESH = pl.DeviceIdType.MESH


def _dot(a, b):
    return jnp.dot(a, b, preferred_element_type=f32)


def _dot_nt(a, b):
    return lax.dot_general(a, b, (((1,), (1,)), ((), ())), preferred_element_type=f32)


def _dot_tn(a, b):
    return lax.dot_general(a, b, (((0,), (0,)), ((), ())), preferred_element_type=f32)


def _sig(x):
    return 0.5 * jnp.tanh(0.5 * x) + 0.5


def _softplus(x):
    return jnp.maximum(x, 0.0) + jnp.log(1.0 + jnp.exp(-jnp.abs(x)))


def _params(sem, vmem=None):
    return pltpu.CompilerParams(dimension_semantics=sem, vmem_limit_bytes=vmem)


def _tile(tm, width, cb=0):
    return pl.BlockSpec((tm, width), lambda i, cb=cb: (i, cb))


def _whole(shape):
    nd = len(shape)
    return pl.BlockSpec(shape, lambda *_: (0,) * nd)


def _prenorm(x, w_pre):
    t = x.shape[0]

    def body(x_ref, w_ref, h_ref):
        xv = x_ref[...]
        r = lax.rsqrt(jnp.mean(xv * xv, axis=-1, keepdims=True) + NORM_EPS)
        h_ref[...] = (xv * r * w_ref[...]).astype(bf16)

    return pl.pallas_call(
        body, name="prenorm", grid=(t // TM,),
        in_specs=[_tile(TM, D), _whole((1, D))], out_specs=_tile(TM, D),
        out_shape=jax.ShapeDtypeStruct((t, D), bf16),
        compiler_params=_params(("parallel",)),
    )(x, w_pre)


def _mm(name, a, w, bias, out_dtype, tm, tn, w_is_nk=False):
    t, k = a.shape
    tm = min(tm, t)
    n = w.shape[0] if w_is_nk else w.shape[1]

    def body(a_ref, w_ref, *refs):
        acc = _dot_nt(a_ref[...], w_ref[...]) if w_is_nk else _dot(a_ref[...], w_ref[...])
        if bias is not None:
            acc = acc + refs[0][...]
        refs[-1][...] = acc.astype(out_dtype)

    in_specs = [pl.BlockSpec((tm, k), lambda i, j: (i, 0)),
                pl.BlockSpec((tn, k), lambda i, j: (j, 0)) if w_is_nk else pl.BlockSpec((k, tn), lambda i, j: (0, j))]
    args = [a, w]
    if bias is not None:
        in_specs.append(pl.BlockSpec((1, tn), lambda i, j: (0, j)))
        args.append(bias)
    return pl.pallas_call(
        body, name=name, grid=(t // tm, n // tn), in_specs=in_specs,
        out_specs=pl.BlockSpec((tm, tn), lambda i, j: (i, j)), out_shape=jax.ShapeDtypeStruct((t, n), out_dtype),
        compiler_params=_params(("parallel", "parallel"), VMEM_LIMIT),
    )(*args)


def _forget_prep(f128, seq):
    t = f128.shape[0]
    nb = seq // LANES

    def body(f_ref, c_ref):
        r = lax.broadcasted_iota(jnp.int32, (LANES, LANES), 0)
        cidx = lax.broadcasted_iota(jnp.int32, (LANES, LANES), 1)
        tri = (r >= cidx).astype(f32)
        carry = jnp.zeros((1, LANES), f32)
        for blk in range(nb):
            fv = f_ref[pl.ds(blk * LANES, LANES), :]
            lf = -_softplus(-fv)
            c_ref[pl.ds(blk * LANES, LANES), :] = (
                jnp.dot(tri, lf, preferred_element_type=f32, precision=lax.Precision.HIGHEST) + carry)
            carry = carry + jnp.sum(lf, axis=0, keepdims=True)

    return pl.pallas_call(
        body, name="forget_prep", grid=(t // seq,),
        in_specs=[pl.BlockSpec((seq, LANES), lambda b: (b, 0))],
        out_specs=pl.BlockSpec((seq, LANES), lambda b: (b, 0)),
        out_shape=jax.ShapeDtypeStruct((t, LANES), f32),
        compiler_params=_params(("parallel",)),
    )(f128)


def _split3(cv):
    hi = cv.astype(bf16)
    r1 = cv - hi.astype(f32)
    mid = r1.astype(bf16)
    lo = (r1 - mid.astype(f32)).astype(bf16)
    return hi, mid, lo


def _attn_prep(qkv, c):
    t = qkv.shape[0]

    def body(q_ref, k_ref, c_ref, qa_ref, ka_ref):
        lane = lax.broadcasted_iota(jnp.int32, (1, LANES), 1)
        cv = c_ref[...]
        one = jnp.ones((), bf16)
        zero = jnp.zeros((), bf16)
        q_ones = jnp.where((lane >= 67) & (lane < 70), one, zero)
        k_ones = jnp.where((lane >= 64) & (lane < 67), one, zero)
        for head in range(HEADS):
            pair = pl.ds((head // 2) * LANES, LANES)
            ch = jnp.sum(jnp.where(lane == head, cv, 0.0), axis=1, keepdims=True)
            hi, mid, lo = _split3(ch)
            q2, k2 = q_ref[:, pair], k_ref[:, pair]
            if head % 2 == 1:
                q2, k2 = pltpu.roll(q2, 64, 1), pltpu.roll(k2, 64, 1)
            qa = jnp.where(lane < 64, q2 * jnp.asarray(QK_SCALE, bf16),
                           jnp.where(lane == 64, hi, jnp.where(lane == 65, mid, jnp.where(lane == 66, lo, q_ones))))
            ka = jnp.where(lane < 64, k2,
                           jnp.where(lane == 67, -hi, jnp.where(lane == 68, -mid, jnp.where(lane == 69, -lo, k_ones))))
            qa_ref[:, pl.ds(head * LANES, LANES)] = qa
            ka_ref[:, pl.ds(head * LANES, LANES)] = ka

    tm = min(TM, t)
    out = pl.BlockSpec((tm, 2 * D), lambda i: (i, 0))
    return pl.pallas_call(
        body, name="attn_prep", grid=(t // tm,),
        in_specs=[_tile(tm, D, 0), _tile(tm, D, 1), _tile(tm, LANES)],
        out_specs=[out, out],
        out_shape=[jax.ShapeDtypeStruct((t, 2 * D), bf16)] * 2,
        compiler_params=_params(("parallel",)),
    )(qkv, qkv, c)


def _attn_fwd(qa, ka, qkv, rest, seq):
    t = qkv.shape[0]
    nb, nq = t // seq, seq // TQ

    hg = ATT_GROUP_FWD
    ng = HEADS // hg

    def body(q_ref, k_ref, v_ref, ga_ref, o_ref, pa_ref, lse_ref, acc_scr):
        qi, gi = pl.program_id(1), pl.program_id(2)
        krow = lax.broadcasted_iota(jnp.int32, (TQ, TQ), 0)
        qcol = lax.broadcasted_iota(jnp.int32, (TQ, TQ), 1)
        acc_scr[...] = jnp.zeros_like(acc_scr)

        def kv_step(kt, carry, masked):
            ks = pl.multiple_of(kt * TQ, TQ)
            sts = [_dot_nt(k_ref[pl.ds(ks, TQ), pl.ds(g * LANES, LANES)], q_ref[:, pl.ds(g * LANES, LANES)])
                   for g in range(hg)]
            if masked:
                sts = [jnp.where(krow <= qcol, st, MASK_VALUE) for st in sts]
            m_new = [jnp.maximum(carry[g][0], jnp.max(sts[g], axis=0, keepdims=True)) for g in range(hg)]
            ps = [jnp.exp(sts[g] - m_new[g]) for g in range(hg)]
            alphas = [jnp.exp(carry[g][0] - m_new[g]) for g in range(hg)]
            phi = [ps[g].astype(bf16) for g in range(hg)]
            plo = [(ps[g] - phi[g].astype(f32)).astype(bf16) for g in range(hg)]
            vs = [v_ref[pl.ds(ks, TQ), pl.ds(j * LANES, LANES)] for j in range(hg // 2)]
            pvs = [_dot_tn(vs[g // 2], phi[g]) + _dot_tn(vs[g // 2], plo[g]) for g in range(hg)]
            olds = [acc_scr[g] for g in range(hg)]
            for g in range(hg):
                acc_scr[g] = alphas[g] * olds[g] + pvs[g]
            return tuple((m_new[g], alphas[g] * carry[g][1] + jnp.sum(ps[g], axis=0, keepdims=True))
                         for g in range(hg))

        init = tuple((jnp.full((1, TQ), MASK_VALUE, f32), jnp.zeros((1, TQ), f32)) for _ in range(hg))
        carry = lax.fori_loop(0, qi, lambda kt, cr: kv_step(kt, cr, False), init)
        stats = kv_step(qi, carry, True)
        drow = lax.broadcasted_iota(jnp.int32, (LANES, TQ), 0)
        for g in range(hg):
            m, l = stats[g]
            lse_ref[0, pl.ds(hg * gi + g, 1), :] = m + jnp.log(l)
        for j in range(hg // 2):
            o2 = jnp.where(drow < 64, acc_scr[2 * j] / stats[2 * j][1], acc_scr[2 * j + 1] / stats[2 * j + 1][1]).T
            o_ref[:, pl.ds(j * LANES, LANES)] = o2
            ga = ga_ref[:, pl.ds(j * LANES, LANES)].astype(f32)
            pa_ref[:, pl.ds(j * LANES, LANES)] = (o2 * (ga * _sig(ga))).astype(bf16)

    vw = hg * 64
    tile = pl.BlockSpec((TQ, vw), lambda b, qi, gi: (b * nq + qi, gi))
    return pl.pallas_call(
        body, name="attn_fwd", grid=(nb, nq, ng),
        in_specs=[pl.BlockSpec((TQ, hg * LANES), lambda b, qi, gi: (b * nq + qi, gi)),
                  pl.BlockSpec((seq, hg * LANES), lambda b, qi, gi: (b, gi)),
                  pl.BlockSpec((seq, vw), lambda b, qi, gi: (b, 2 * ng + gi)), tile],
        out_specs=[tile, tile, pl.BlockSpec((1, HEADS, TQ), lambda b, qi, gi: (b * nq + qi, 0, 0))],
        out_shape=[jax.ShapeDtypeStruct((t, D), f32), jax.ShapeDtypeStruct((t, D), bf16),
                   jax.ShapeDtypeStruct((t // TQ, HEADS, TQ), f32)],
        scratch_shapes=[pltpu.VMEM((hg, LANES, TQ), f32)],
        compiler_params=_params(("parallel", "parallel", "arbitrary"), VMEM_LIMIT),
    )(qa, ka, qkv, rest)


def _shifted_rows(x, top8, prev8, shift, row, row8):
    body = pltpu.roll(x, shift, 0)
    head = jnp.where(row8 < shift, pltpu.roll(prev8, shift, 0), pltpu.roll(top8, shift, 0))
    return body, head


def _rnn_gates(xc, wa_ref, wx_ref, ba_ref, bx_ref, lam_ref):
    xcb = xc.astype(bf16)
    r = _sig(_dot(xcb, wa_ref[...]) + ba_ref[...])
    i = _sig(_dot(xcb, wx_ref[...]) + bx_ref[...])
    sp = _softplus(-lam_ref[...])
    log_a = (-RG_C) * r * sp
    th = jnp.tanh(log_a)
    w1 = (-2.0) * th / (1.0 - th)
    sq = jnp.sqrt(jnp.maximum(w1, 0.0))
    return r, i, sp, log_a, w1, sq


def _conv_tile(x_ref, xprev_ref, has_prev, cw_ref, cb_ref, xc_ref):
    row = lax.broadcasted_iota(jnp.int32, (TL, D), 0)
    row8 = lax.broadcasted_iota(jnp.int32, (8, D), 0)
    x = x_ref[...].astype(f32)
    top8 = x[:8]
    prev8 = jnp.where(has_prev, xprev_ref[...].astype(f32)[PREV_ROWS - 8:], 0.0)
    xc = cb_ref[...] + cw_ref[pl.ds(3, 1), :] * x
    xc8 = cb_ref[...] + cw_ref[pl.ds(3, 1), :] * top8
    for sh in range(1, 4):
        w = cw_ref[pl.ds(3 - sh, 1), :]
        xs, xs8 = _shifted_rows(x, top8, prev8, sh, row, row8)
        xc = xc + w * xs
        xc8 = xc8 + w * xs8
    xc_ref[...] = xc
    xc_ref[pl.ds(0, 8), :] = xc8


def _rnn_fwd(rest, conv_w, conv_b, wa_d, wx_d, ba, bx, lam, seq):
    t = rest.shape[0]
    nb, nt = t // seq, seq // TL

    def body(x_ref, xprev_ref, gr_ref, cw_ref, cb_ref, wa_ref, wx_ref, ba_ref, bx_ref, lam_ref,
             xc_ref, a_ref, h_ref, pr_ref, xc_scr, u_scr, h_scr, carry):
        tt = pl.program_id(1)
        _conv_tile(x_ref, xprev_ref, tt > 0, cw_ref, cb_ref, xc_scr)
        xc = xc_scr[...]
        xc_ref[...] = xc.astype(bf16)
        r, i, sp, log_a, w1, sq = _rnn_gates(xc, wa_ref, wx_ref, ba_ref, bx_ref, lam_ref)
        a_ref[...] = jnp.exp(log_a)
        u_scr[...] = sq * (i * xc)

        @pl.when(tt == 0)
        def _():
            carry[...] = jnp.zeros_like(carry)

        def step(s, h):
            h = a_ref[pl.ds(s, 1), :] * h + u_scr[pl.ds(s, 1), :]
            h_scr[pl.ds(s, 1), :] = h
            return h

        carry[...] = lax.fori_loop(0, TL, step, carry[...], unroll=8)
        gr = gr_ref[...].astype(f32)
        h = h_scr[...]
        h_ref[...] = h.astype(bf16)
        pr_ref[...] = (h * (gr * _sig(gr))).astype(bf16)

    tile = lambda cb: pl.BlockSpec((TL, D), lambda b, tt, cb=cb: (b * nt + tt, cb))
    prev = lambda cb: pl.BlockSpec(
        (PREV_ROWS, D), lambda b, tt, cb=cb: (jnp.maximum((b * nt + tt) * (TL // PREV_ROWS) - 1, 0), cb))
    vec = _whole((1, D))
    return pl.pallas_call(
        body, name="rnn_fwd", grid=(nb, nt),
        in_specs=[tile(1), prev(1), tile(2), _whole((4, D)), vec, _whole((D, D)), _whole((D, D)), vec, vec, vec],
        out_specs=[tile(0)] * 4,
        out_shape=[jax.ShapeDtypeStruct((t, D), dt) for dt in (bf16, f32, bf16, bf16)],
        scratch_shapes=[pltpu.VMEM((TL, D), f32)] * 3 + [pltpu.VMEM((1, D), f32)],
        compiler_params=_params(("parallel", "arbitrary"), VMEM_LIMIT),
    )(rest, rest, rest, conv_w, conv_b, wa_d, wx_d, ba, bx, lam)


def _merge(mga, mgr, ya, yr):
    return (_sig(mga.astype(f32)) * ya.astype(f32) + _sig(mgr.astype(f32)) * yr.astype(f32)).astype(bf16)


def _out_proj_loss(rest, ya, yr, w_out, x, tgt, w_post):
    t = x.shape[0]

    def body(mga_ref, mgr_ref, ya_ref, yr_ref, wo_ref, x_ref, t_ref, w_ref, do_ref, dy_ref, mrg_ref, loss_ref, dwp_ref):
        @pl.when(pl.program_id(0) == 0)
        def _():
            loss_ref[...] = jnp.zeros_like(loss_ref)
            dwp_ref[...] = jnp.zeros_like(dwp_ref)

        mrg = _merge(mga_ref[...], mgr_ref[...], ya_ref[...], yr_ref[...])
        mrg_ref[...] = mrg
        ov = _dot(mrg, wo_ref[...])
        w = w_ref[...]
        r2 = lax.rsqrt(jnp.mean(ov * ov, axis=-1, keepdims=True) + NORM_EPS)
        oh = ov * r2
        e = x_ref[...] + oh * w - t_ref[...]
        loss_ref[...] += 0.5 * jnp.sum(jnp.mean(e * e, axis=-1, keepdims=True))
        dy = e * (1.0 / D)
        dy_ref[...] = dy
        dwp_ref[...] += jnp.sum(dy * oh, axis=0, keepdims=True)
        doh = dy * w
        do_ref[...] = (r2 * (doh - oh * jnp.mean(doh * oh, axis=-1, keepdims=True))).astype(bf16)

    return pl.pallas_call(
        body, name="out_proj_loss", grid=(t // TM,),
        in_specs=[_tile(TM, D, 3), _tile(TM, D, 4), _tile(TM, D), _tile(TM, D), _whole((D, D)), _tile(TM, D),
                  _tile(TM, D), _whole((1, D))],
        out_specs=[_tile(TM, D), _tile(TM, D), _tile(TM, D), _whole((8, LANES)), _whole((1, D))],
        out_shape=[jax.ShapeDtypeStruct((t, D), bf16), jax.ShapeDtypeStruct((t, D), f32),
                   jax.ShapeDtypeStruct((t, D), bf16), jax.ShapeDtypeStruct((8, LANES), f32),
                   jax.ShapeDtypeStruct((1, D), f32)],
        compiler_params=_params(("arbitrary",), VMEM_LIMIT),
    )(rest, rest, ya, yr, w_out, x, tgt, w_post)


def _out_bwd(do, rest, ya, yr, w_out):
    t = do.shape[0]

    def body(do_ref, mga_ref, mgr_ref, ya_ref, yr_ref, w_ref, dya_ref, dyr_ref, dmga_ref, dmgr_ref):
        sa, sr = _sig(mga_ref[...].astype(f32)), _sig(mgr_ref[...].astype(f32))
        ya, yr = ya_ref[...].astype(f32), yr_ref[...].astype(f32)
        dm = _dot_nt(do_ref[...], w_ref[...])
        dya_ref[...] = (dm * sa).astype(bf16)
        dyr_ref[...] = (dm * sr).astype(bf16)
        dmga_ref[...] = (dm * ya * sa * (1.0 - sa)).astype(bf16)
        dmgr_ref[...] = (dm * yr * sr * (1.0 - sr)).astype(bf16)

    return pl.pallas_call(
        body, name="out_bwd", grid=(t // TM,),
        in_specs=[_tile(TM, D), _tile(TM, D, 3), _tile(TM, D, 4), _tile(TM, D), _tile(TM, D), _whole((D, D))],
        out_specs=[_tile(TM, D)] * 4,
        out_shape=[jax.ShapeDtypeStruct((t, D), bf16)] * 4,
        compiler_params=_params(("parallel",), VMEM_LIMIT),
    )(do, rest, rest, ya, yr, w_out)


def _branch_bwd(name, dyb, rest, gate_cb, act, w, act_grad_dtype, head_sums=False):
    t = dyb.shape[0]

    def body(dy_ref, g_ref, act_ref, w_ref, dact_ref, dg_ref, *delta_ref):
        dp = _dot_nt(dy_ref[...], w_ref[...])
        g = g_ref[...].astype(f32)
        sg = _sig(g)
        act = act_ref[...].astype(f32)
        dact = (dp * (g * sg)).astype(act_grad_dtype)
        dact_ref[...] = dact
        dg_ref[...] = (dp * act * (sg * (1.0 + g * (1.0 - sg)))).astype(bf16)
        if head_sums:
            ch = lax.broadcasted_iota(jnp.int32, (D, LANES), 0)
            hd = lax.broadcasted_iota(jnp.int32, (D, LANES), 1)
            pick = (ch // 64 == hd).astype(bf16)
            per_head = sum(_dot(piece, pick) for piece in _split3(dact.astype(f32) * act))
            delta_ref[0][0] = per_head.T[:HEADS, :]

    out_specs = [_tile(TM, D), _tile(TM, D)]
    out_shape = [jax.ShapeDtypeStruct((t, D), act_grad_dtype), jax.ShapeDtypeStruct((t, D), bf16)]
    if head_sums:
        out_specs.append(pl.BlockSpec((1, HEADS, TM), lambda i: (i, 0, 0)))
        out_shape.append(jax.ShapeDtypeStruct((t // TM, HEADS, TM), f32))
    return pl.pallas_call(
        body, name=name, grid=(t // TM,),
        in_specs=[_tile(TM, D), _tile(TM, D, gate_cb), _tile(TM, D), _whole((D, D))],
        out_specs=out_specs, out_shape=out_shape,
        compiler_params=_params(("parallel",), VMEM_LIMIT),
    )(dyb, rest, act, w)


def _rnn_bwd(dh, a, h, xc, rest, conv_w, conv_b, wa_d, wx_d, ba, bx, lam, seq):
    t = dh.shape[0]
    nb, nt = t // seq, seq // TL
    diag = (D // LANES, LANES, LANES)

    def body(dh_ref, a_ref, h_ref, hprev_ref, xc_ref, x_ref, xprev_ref, cw_ref, cb_ref, wa_ref, wx_ref,
             ba_ref, bx_ref, lam_ref, dxr_ref, dwa_ref, dwx_ref, vec_ref, g_scr, dxc_scr, dxr_scr, qcarry, dxc_next):
        b, tt = pl.program_id(0), pl.program_id(1)
        rt = nt - 1 - tt

        @pl.when((b == 0) & (tt == 0))
        def _():
            dwa_ref[...] = jnp.zeros_like(dwa_ref)
            dwx_ref[...] = jnp.zeros_like(dwx_ref)
            vec_ref[...] = jnp.zeros_like(vec_ref)

        @pl.when(tt == 0)
        def _():
            qcarry[...] = jnp.zeros_like(qcarry)
            dxc_next[...] = jnp.zeros_like(dxc_next)

        g_scr[...] = dh_ref[...].astype(f32)

        def step(k, q):
            s = TL - 1 - k
            g = g_scr[pl.ds(s, 1), :] + q
            g_scr[pl.ds(s, 1), :] = g
            return a_ref[pl.ds(s, 1), :] * g

        qcarry[...] = lax.fori_loop(0, TL, step, qcarry[...], unroll=8)

        row = lax.broadcasted_iota(jnp.int32, (TL, D), 0)
        row8 = lax.broadcasted_iota(jnp.int32, (8, D), 0)
        g = g_scr[...]
        av = a_ref[...]
        xc = xc_ref[...].astype(f32)
        hlast = jnp.where(rt > 0, hprev_ref[...].astype(f32)[PREV_ROWS - 1:], 0.0)
        hp = jnp.where(row == 0, hlast, pltpu.roll(h_ref[...].astype(f32), 1, 0))
        r, i, sp, log_a, w1, sq = _rnn_gates(xc, wa_ref, wx_ref, ba_ref, bx_ref, lam_ref)
        dix = g * sq
        di = dix * xc
        dxc = dix * i
        dsq = g * (i * xc)
        dlog_a = g * hp * av - dsq * jnp.where(sq > 0.0, (1.0 - w1) / sq, 0.0)
        dpr = (dlog_a * ((-RG_C) * sp)) * r * (1.0 - r)
        dpi = di * i * (1.0 - i)
        dprb, dpib, xcb = dpr.astype(bf16), dpi.astype(bf16), xc.astype(bf16)
        dxc = dxc + _dot_nt(dprb, wa_ref[...]) + _dot_nt(dpib, wx_ref[...])
        for j in range(D // LANES):
            cols = slice(j * LANES, (j + 1) * LANES)
            dwa_ref[j] += _dot_tn(xcb[:, cols], dprb[:, cols])
            dwx_ref[j] += _dot_tn(xcb[:, cols], dpib[:, cols])
        vec_ref[pl.ds(0, 1), :] += jnp.sum(dpr, axis=0, keepdims=True)
        vec_ref[pl.ds(1, 1), :] += jnp.sum(dpi, axis=0, keepdims=True)
        dsp = jnp.sum(dlog_a * ((-RG_C) * r), axis=0, keepdims=True)
        vec_ref[pl.ds(2, 1), :] += dsp * (-_sig(-lam_ref[...]))
        vec_ref[pl.ds(3, 1), :] += jnp.sum(dxc, axis=0, keepdims=True)

        dxc_scr[...] = dxc
        bot8 = dxc_scr[pl.ds(TL - 8, 8), :]
        nxt8 = dxc_next[...]
        dxr = cw_ref[pl.ds(3, 1), :] * dxc
        dxr8 = cw_ref[pl.ds(3, 1), :] * bot8
        for sh in range(1, 4):
            w = cw_ref[pl.ds(3 - sh, 1), :]
            dxr = dxr + w * pltpu.roll(dxc, TL - sh, 0)
            dxr8 = dxr8 + w * jnp.where(row8 < 8 - sh, pltpu.roll(bot8, 8 - sh, 0), pltpu.roll(nxt8, 8 - sh, 0))
        dxr_scr[...] = dxr
        dxr_scr[pl.ds(TL - 8, 8), :] = dxr8
        dxr_ref[...] = dxr_scr[...].astype(bf16)
        dxc_next[...] = dxc_scr[pl.ds(0, 8), :]

        x = x_ref[...].astype(f32)
        prev8 = jnp.where(rt > 0, xprev_ref[...].astype(f32)[PREV_ROWS - 8:], 0.0)
        dxc_top8 = dxc_scr[pl.ds(0, 8), :]
        vec_ref[pl.ds(7, 1), :] += jnp.sum(dxc * x, axis=0, keepdims=True)
        for sh in range(1, 4):
            inside = jnp.sum(dxc * jnp.where(row >= sh, pltpu.roll(x, sh, 0), 0.0), axis=0, keepdims=True)
            above = jnp.sum(dxc_top8 * jnp.where(row8 < sh, pltpu.roll(prev8, sh, 0), 0.0), axis=0, keepdims=True)
            vec_ref[pl.ds(7 - sh, 1), :] += inside + above

    tile = lambda cb: pl.BlockSpec((TL, D), lambda b, tt, cb=cb: (b * nt + nt - 1 - tt, cb))
    prev = lambda cb: pl.BlockSpec(
        (PREV_ROWS, D), lambda b, tt, cb=cb: (jnp.maximum((b * nt + nt - 1 - tt) * (TL // PREV_ROWS) - 1, 0), cb))
    vec = _whole((1, D))
    return pl.pallas_call(
        body, name="rnn_bwd", grid=(nb, nt),
        in_specs=[tile(0), tile(0), tile(0), prev(0), tile(0), tile(1), prev(1),
                  _whole((4, D)), vec, _whole((D, D)), _whole((D, D)), vec, vec, vec],
        out_specs=[tile(0), _whole(diag), _whole(diag), _whole((8, D))],
        out_shape=[jax.ShapeDtypeStruct((t, D), bf16), jax.ShapeDtypeStruct(diag, f32),
                   jax.ShapeDtypeStruct(diag, f32), jax.ShapeDtypeStruct((8, D), f32)],
        scratch_shapes=[pltpu.VMEM((TL, D), f32), pltpu.VMEM((TL, D), f32), pltpu.VMEM((TL, D), f32),
                        pltpu.VMEM((1, D), f32), pltpu.VMEM((8, D), f32)],
        compiler_params=_params(("arbitrary", "arbitrary"), VMEM_LIMIT),
    )(dh, a, h, h, xc, rest, rest, conv_w, conv_b, wa_d, wx_d, ba, bx, lam)


def _attn_bwd(qa, ka, qkv, doa, lse, delta, seq):
    t = qkv.shape[0]
    nb, nq = t // seq, seq // TQ
    hg = ATT_GROUP
    ng, npair = HEADS // hg, hg // 2

    def body(qa_ref, ka_ref, q_ref, k_ref, v_ref, do_ref, lse_ref, dl_ref, dq_ref, dk_ref, dv_ref, dc_ref,
             dqt_scr, dk_scr, dv_scr, ds_scr, kht_scr):
        gi, kt = pl.program_id(1), pl.program_id(2)
        lane = lax.broadcasted_iota(jnp.int32, (1, LANES), 1)
        krow = lax.broadcasted_iota(jnp.int32, (TQ, TQ), 0)
        qcol = lax.broadcasted_iota(jnp.int32, (TQ, TQ), 1)
        lmask = [(lane // 64) == hh for hh in range(2)]
        scale = jnp.asarray(QK_SCALE, bf16)

        @pl.when(kt == 0)
        def _():
            dqt_scr[...] = jnp.zeros_like(dqt_scr)

        dk_scr[...] = jnp.zeros_like(dk_scr)
        dv_scr[...] = jnp.zeros_like(dv_scr)
        ds_scr[...] = jnp.zeros_like(ds_scr)
        for g in range(hg):
            k2 = k_ref[:, pl.ds((g // 2) * LANES, LANES)]
            kht_scr[g] = jnp.where(lmask[g % 2], k2, jnp.zeros_like(k2)).T

        def q_step(qt, masked):
            qs = pl.multiple_of(qt * TQ, TQ)
            heads = range(hg)
            do2 = [do_ref[pl.ds(qs, TQ), pl.ds(j * LANES, LANES)] for j in range(npair)]
            q2 = [q_ref[pl.ds(qs, TQ), pl.ds(j * LANES, LANES)] for j in range(npair)]
            doh = [jnp.where(lmask[g % 2], do2[g // 2], jnp.zeros_like(do2[0])) for g in heads]
            qh = [jnp.where(lmask[g % 2], q2[g // 2], jnp.zeros_like(q2[0])) * scale for g in heads]
            st = [_dot_nt(ka_ref[:, pl.ds(g * LANES, LANES)], qa_ref[pl.ds(qs, TQ), pl.ds(g * LANES, LANES)])
                  for g in heads]
            if masked:
                st = [jnp.where(krow <= qcol, s, MASK_VALUE) for s in st]
            dp = [_dot_nt(v_ref[:, pl.ds((g // 2) * LANES, LANES)], doh[g]) for g in heads]
            p = [jnp.exp(st[g] - lse_ref[qt, pl.ds(hg * gi + g, 1), :]) for g in heads]
            ds = [p[g] * (dp[g] - dl_ref[qt, pl.ds(hg * gi + g, 1), :]) for g in heads]
            pb = [x.astype(bf16) for x in p]
            dsb = [x.astype(bf16) for x in ds]
            for j in range(npair):
                a, b = 2 * j, 2 * j + 1
                dv_scr[j] += _dot(pb[a], doh[a]) + _dot(pb[b], doh[b])
                dk_scr[j] += _dot(dsb[a], qh[a]) + _dot(dsb[b], qh[b])
                dqt_scr[qt, j] += (_dot(kht_scr[a], dsb[a]) + _dot(kht_scr[b], dsb[b])) * QK_SCALE
            for g in heads:
                ds_scr[g] += ds[g][:, :LANES] + ds[g][:, LANES:]

        q_step(kt, True)

        def loop_body(qt, carry):
            q_step(qt, False)
            return carry

        lax.fori_loop(kt + 1, nq, loop_body, 0)

        dc = jnp.zeros((TQ, LANES), f32)
        for g in range(hg):
            dc = jnp.where(lane == g, -jnp.sum(ds_scr[g], axis=1, keepdims=True), dc)
        dc_ref[...] = dc
        for j in range(npair):
            dk_ref[:, pl.ds(j * LANES, LANES)] = dk_scr[j].astype(bf16)
            dv_ref[:, pl.ds(j * LANES, LANES)] = dv_scr[j].astype(bf16)

        @pl.when(kt == nq - 1)
        def _():
            for qt in range(nq):
                for j in range(npair):
                    dq_ref[pl.ds(qt * TQ, TQ), pl.ds(j * LANES, LANES)] = dqt_scr[qt, j].T.astype(bf16)

    vw = hg * 64
    seqspec = pl.BlockSpec((seq, vw), lambda b, gi, kt: (b, gi))
    kspec = lambda off: pl.BlockSpec((TQ, vw), lambda b, gi, kt: (b * nq + kt, off + gi))
    rowspec = pl.BlockSpec((nq, HEADS, TQ), lambda b, gi, kt: (b, 0, 0))
    return pl.pallas_call(
        body, name="attn_bwd", grid=(nb, ng, nq),
        in_specs=[pl.BlockSpec((seq, hg * LANES), lambda b, gi, kt: (b, gi)),
                  pl.BlockSpec((TQ, hg * LANES), lambda b, gi, kt: (b * nq + kt, gi)),
                  seqspec, kspec(ng), kspec(2 * ng), seqspec, rowspec, rowspec],
        out_specs=[seqspec, kspec(0), kspec(0), pl.BlockSpec((TQ, LANES), lambda b, gi, kt: (b * nq + kt, gi))],
        out_shape=[jax.ShapeDtypeStruct((t, D), bf16)] * 3 + [jax.ShapeDtypeStruct((t, ng * LANES), f32)],
        scratch_shapes=[pltpu.VMEM((nq, npair, LANES, TQ), f32), pltpu.VMEM((npair, TQ, LANES), f32),
                        pltpu.VMEM((npair, TQ, LANES), f32), pltpu.VMEM((hg, TQ, LANES), f32),
                        pltpu.VMEM((hg, LANES, TQ), bf16)],
        compiler_params=_params(("parallel", "parallel", "arbitrary"), VMEM_LIMIT),
    )(qa, ka, qkv, qkv, qkv, doa, lse, delta)


def _forget_bwd(dc, f128, seq):
    t = f128.shape[0]
    nb = seq // LANES

    def body(dc_ref, f_ref, df_ref, dbf_ref):
        @pl.when(pl.program_id(0) == 0)
        def _():
            dbf_ref[...] = jnp.zeros_like(dbf_ref)

        r = lax.broadcasted_iota(jnp.int32, (LANES, LANES), 0)
        cidx = lax.broadcasted_iota(jnp.int32, (LANES, LANES), 1)
        tri = (r <= cidx).astype(f32)
        carry = jnp.zeros((1, LANES), f32)
        total = jnp.zeros((1, LANES), f32)
        for blk in reversed(range(nb)):
            dcb = dc_ref[pl.ds(blk * LANES, LANES), :]
            dlf = jnp.dot(tri, dcb, preferred_element_type=f32, precision=lax.Precision.HIGHEST) + carry
            df = dlf * _sig(-f_ref[pl.ds(blk * LANES, LANES), :])
            df_ref[pl.ds(blk * LANES, LANES), :] = df.astype(bf16)
            total = total + jnp.sum(df, axis=0, keepdims=True)
            carry = carry + jnp.sum(dcb, axis=0, keepdims=True)
        dbf_ref[...] += total

    return pl.pallas_call(
        body, name="forget_bwd", grid=(t // seq,),
        in_specs=[pl.BlockSpec((seq, LANES), lambda b: (b, 0)), pl.BlockSpec((seq, LANES), lambda b: (b, 0))],
        out_specs=[pl.BlockSpec((seq, LANES), lambda b: (b, 0)), _whole((1, LANES))],
        out_shape=[jax.ShapeDtypeStruct((t, LANES), bf16), jax.ShapeDtypeStruct((1, LANES), f32)],
        compiler_params=_params(("arbitrary",)),
    )(dc, f128)


def _in_bwd(dz, df, x, dy, w_qkv, w_rest, w_f, w_pre):
    t = x.shape[0]
    n_qkv = w_qkv.shape[0] // D
    n_rest = w_rest.shape[0] // D

    def body(*refs):
        dz_refs = refs[:n_qkv + n_rest]
        df_ref, x_ref, dy_ref, wq_ref, wr_ref, wf_ref, wp_ref, gx_ref, dwp_ref = refs[n_qkv + n_rest:]

        @pl.when(pl.program_id(0) == 0)
        def _():
            dwp_ref[...] = jnp.zeros_like(dwp_ref)

        dh = _dot(df_ref[...], wf_ref[...])
        for p in range(n_qkv):
            dh = dh + _dot(dz_refs[p][...], wq_ref[pl.ds(p * D, D), :])
        for p in range(n_rest):
            dh = dh + _dot(dz_refs[n_qkv + p][...], wr_ref[pl.ds(p * D, D), :])
        xv = x_ref[...]
        r1 = lax.rsqrt(jnp.mean(xv * xv, axis=-1, keepdims=True) + NORM_EPS)
        xh = xv * r1
        dwp_ref[...] += jnp.sum(dh * xh, axis=0, keepdims=True)
        dxh = dh * wp_ref[...]
        gx_ref[...] = dy_ref[...] + r1 * (dxh - xh * jnp.mean(dxh * xh, axis=-1, keepdims=True))

    once = lambda shape: pl.BlockSpec(shape, lambda i: (0, 0), pipeline_mode=pl.Buffered(1))
    return pl.pallas_call(
        body, name="in_bwd", grid=(t // TM,),
        in_specs=[_tile(TM, D)] * (n_qkv + n_rest) + [_tile(TM, LANES), _tile(TM, D), _tile(TM, D),
                  once(w_qkv.shape), once(w_rest.shape), once(w_f.shape), _whole((1, D))],
        out_specs=[_tile(TM, D), _whole((1, D))],
        out_shape=[jax.ShapeDtypeStruct((t, D), f32), jax.ShapeDtypeStruct((1, D), f32)],
        compiler_params=_params(("arbitrary",), VMEM_LIMIT),
    )(*dz, df, x, dy, w_qkv, w_rest, w_f, w_pre)


def _tn_mm(name, a, b, tn, tk=2048):
    t, k = a.shape
    tk = min(tk, t)
    n = b.shape[1]

    def body(a_ref, b_ref, o_ref, s_ref):
        j, kk = pl.program_id(0), pl.program_id(1)

        @pl.when(kk == 0)
        def _():
            o_ref[...] = jnp.zeros_like(o_ref)

        @pl.when((j == 0) & (kk == 0))
        def _():
            s_ref[...] = jnp.zeros_like(s_ref)

        av = a_ref[...]
        o_ref[...] += _dot_tn(av, b_ref[...])

        @pl.when(j == 0)
        def _():
            s_ref[...] += jnp.sum(av.astype(f32), axis=0, keepdims=True)

    return pl.pallas_call(
        body, name=name, grid=(n // tn, t // tk),
        in_specs=[pl.BlockSpec((tk, k), lambda j, kk: (kk, 0)), pl.BlockSpec((tk, tn), lambda j, kk: (kk, j))],
        out_specs=[pl.BlockSpec((k, tn), lambda j, kk: (0, j)), _whole((1, k))],
        out_shape=[jax.ShapeDtypeStruct((k, n), f32), jax.ShapeDtypeStruct((1, k), f32)],
        compiler_params=_params(("arbitrary", "arbitrary"), VMEM_LIMIT),
    )(a, b)


def _position():
    return lax.axis_index("x"), lax.axis_index("y"), lax.axis_index("c")


def _gather_shards(parts, small):
    n = len(parts)
    halves = [p.shape[0] // 2 for p in parts]
    cuts = [-(-h // 32) * 16 for h in halves]
    n_direct, n_relay, n_sib = 4 * n, 2 * n, 6 * n

    def body(*refs):
        srcs, small_src = refs[:n], refs[n]
        dsts, small_dst = refs[n + 1:2 * n + 1], refs[2 * n + 1]
        send, recv, local = refs[2 * n + 2:]
        x, y, c = _position()
        me = 2 * x + y
        chips = [(1 - x, y), (x, 1 - y), (1 - x, 1 - y)]
        ids = [2 * px + py for px, py in chips]

        def rows(a, half, quarter):
            start = half * halves[a] + (cuts[a] if quarter else 0)
            return pl.ds(start, halves[a] - cuts[a] if quarter else cuts[a])

        def landing(a, shard, half, quarter):
            return dsts[a].at[shard, rows(a, half, quarter), :]

        def direct(a, nb, quarter, shard):
            k = (a * 2 + nb) * 2 + quarter
            px, py = chips[nb]
            return pltpu.make_async_remote_copy(
                src_ref=srcs[a].at[rows(a, c, quarter), :], dst_ref=landing(a, shard, c, quarter),
                send_sem=send.at[k], recv_sem=recv.at[k], device_id=(px, py, c), device_id_type=MESH)

        def relay(a, quarter, shard):
            k = n_direct + a * 2 + quarter
            px, py = chips[1 - quarter]
            return pltpu.make_async_remote_copy(
                src_ref=landing(a, shard, c, quarter), dst_ref=landing(a, shard, c, quarter),
                send_sem=send.at[k], recv_sem=recv.at[k], device_id=(px, py, c), device_id_type=MESH)

        def to_sibling(a, origin, quarter, half):
            k = n_direct + n_relay + (a * 3 + origin) * 2 + quarter
            return pltpu.make_async_remote_copy(
                src_ref=landing(a, ids[origin], half, quarter), dst_ref=landing(a, ids[origin], half, quarter),
                send_sem=send.at[k], recv_sem=recv.at[k], device_id=(x, y, 1 - c), device_id_type=MESH)

        def small_copy(j, shard):
            k = n_direct + n_relay + n_sib + j
            px, py = chips[j]
            return pltpu.make_async_remote_copy(
                src_ref=small_src, dst_ref=small_dst.at[shard], send_sem=send.at[k], recv_sem=recv.at[k],
                device_id=(px, py, c), device_id_type=MESH)

        own = [pltpu.make_async_copy(srcs[a], dsts[a].at[me], local.at[a]) for a in range(n)]
        own.append(pltpu.make_async_copy(small_src, small_dst.at[me], local.at[n]))
        for cp in own:
            cp.start()
        sent = [direct(a, nb, q, me) for q in range(2) for a in range(n) for nb in range(2)]
        sent += [small_copy(j, me) for j in range(3)]
        for cp in sent:
            cp.start()

        def passed_on(cp):
            cp.start()
            sent.append(cp)

        for q in range(2):
            for a in range(n):
                for nb in range(2):
                    direct(a, nb, q, ids[nb]).wait_recv()
                    passed_on(to_sibling(a, nb, q, c))
                    if nb == q:
                        passed_on(relay(a, q, ids[nb]))
        for a in range(n):
            for q in range(2):
                relay(a, q, ids[2]).wait_recv()
                passed_on(to_sibling(a, 2, q, c))
        for j in range(3):
            small_copy(j, ids[j]).wait_recv()
            for a in range(n):
                for q in range(2):
                    to_sibling(a, j, q, 1 - c).wait_recv()
        for cp in sent:
            cp.wait_send()
        for cp in own:
            cp.wait()

    vm = pl.BlockSpec(memory_space=pltpu.VMEM)
    n_sems = n_direct + n_relay + n_sib + 3
    return pl.pallas_call(
        body, name="gather_shards",
        in_specs=[vm] * (n + 1), out_specs=[vm] * (n + 1),
        out_shape=[jax.ShapeDtypeStruct((N_CHIPS,) + p.shape, p.dtype) for p in parts + [small]],
        scratch_shapes=[pltpu.SemaphoreType.DMA((n_sems,)), pltpu.SemaphoreType.DMA((n_sems,)),
                        pltpu.SemaphoreType.DMA((n + 1,))],
        compiler_params=pltpu.CompilerParams(vmem_limit_bytes=VMEM_LIMIT),
    )(*parts, small)


def _allsum_rows(part):
    rows_n = part.shape[0]

    def body(x_ref, gath_ref, sum_ref, send_sems, recv_sems, local_sem):
        x, y, c = _position()
        me, sibling = (x, y, c), (x, y, 1 - c)
        chips = [(1 - x, y), (x, 1 - y), (1 - x, 1 - y)]

        def rows(px, py, pc):
            return gath_ref.at[pl.ds((4 * px + 2 * py + pc) * rows_n, rows_n), :]

        def copy(k, block, to, src=None):
            return pltpu.make_async_remote_copy(
                src_ref=rows(*block) if src is None else src, dst_ref=rows(*block),
                send_sem=send_sems.at[k], recv_sem=recv_sems.at[k], device_id=to, device_id_type=MESH)

        mine = pltpu.make_async_copy(x_ref, rows(*me), local_sem)
        mine.start()
        first = [copy(0, me, sibling, src=x_ref)]
        first += [copy(1 + j, me, (*chip, c), src=x_ref) for j, chip in enumerate(chips)]
        for cp in first:
            cp.start()
        passed = [copy(4 + j, (*chip, c), sibling) for j, chip in enumerate(chips)]
        for j, chip in enumerate(chips):
            copy(1 + j, (*chip, c), me).wait_recv()
            passed[j].start()
        copy(0, sibling, me).wait_recv()
        for j, chip in enumerate(chips):
            copy(4 + j, (*chip, 1 - c), me).wait_recv()
        for cp in first + passed:
            cp.wait_send()
        mine.wait()
        total = gath_ref[pl.ds(0, rows_n), :]
        for d in range(1, N_DEV):
            total = total + gath_ref[pl.ds(d * rows_n, rows_n), :]
        sum_ref[...] = total

    vm = pl.BlockSpec(memory_space=pltpu.VMEM)
    return pl.pallas_call(
        body, name="allsum_rows", in_specs=[vm], out_specs=[vm, vm],
        out_shape=[jax.ShapeDtypeStruct((N_DEV * rows_n, D), f32), jax.ShapeDtypeStruct((rows_n, D), f32)],
        scratch_shapes=[pltpu.SemaphoreType.DMA((7,)), pltpu.SemaphoreType.DMA((7,)), pltpu.SemaphoreType.DMA],
    )(part)[1]


PAIR_ROWS = 16


def _pair_reduce(name, pieces):
    _, r, n = pieces.shape

    def body(p_ref, o_ref, land, send, recv):
        x, y, c = _position()

        def remote(j, half):
            return pltpu.make_async_remote_copy(
                src_ref=p_ref.at[2 * j + half], dst_ref=land.at[j], send_sem=send.at[j], recv_sem=recv.at[j],
                device_id=(x, y, 1 - c), device_id_type=MESH)

        sends = [remote(j, 1 - c) for j in range(N_CHIPS)]
        for cp in sends:
            cp.start()
        for j in range(N_CHIPS):
            remote(j, c).wait_recv()

            def add_rows(i, carry, j=j):
                rows = pl.ds(pl.multiple_of(i * PAIR_ROWS, PAIR_ROWS), PAIR_ROWS)
                o_ref[j, rows, :] = (p_ref[2 * j + c, rows, :].astype(f32) + land[j, rows, :].astype(f32)).astype(bf16)
                return carry

            lax.fori_loop(0, r // PAIR_ROWS, add_rows, 0)
        for cp in sends:
            cp.wait_send()

    vm = pl.BlockSpec(memory_space=pltpu.VMEM)
    return pl.pallas_call(
        body, name=name, in_specs=[vm], out_specs=vm,
        out_shape=jax.ShapeDtypeStruct((N_CHIPS, r, n), bf16),
        scratch_shapes=[pltpu.VMEM((N_CHIPS, r, n), bf16), pltpu.SemaphoreType.DMA((N_CHIPS,)),
                        pltpu.SemaphoreType.DMA((N_CHIPS,))],
        compiler_params=pltpu.CompilerParams(vmem_limit_bytes=VMEM_LIMIT),
    )(pieces)


def _chip_exchange(arrs):
    n = len(arrs)
    heights = [a.shape[1] for a in arrs]
    cuts = [-(-r // 32) * 16 for r in heights]

    def body(*refs):
        srcs, dsts, relays = refs[:n], refs[n:2 * n], refs[2 * n:3 * n]
        send, recv, local = refs[3 * n:]
        x, y, c = _position()
        me = 2 * x + y
        chips = [(1 - x, y), (x, 1 - y), (1 - x, 1 - y)]
        ids = [2 * px + py for px, py in chips]

        def rows(a, quarter):
            return pl.ds(cuts[a], heights[a] - cuts[a]) if quarter else pl.ds(0, cuts[a])

        def held(a, quarter):
            size = heights[a] - cuts[a] if quarter else cuts[a]
            return relays[a].at[quarter, pl.ds(0, size), :]

        def direct(a, nb, piece, landing):
            px, py = chips[nb]
            return pltpu.make_async_remote_copy(
                src_ref=srcs[a].at[piece], dst_ref=dsts[a].at[landing], send_sem=send.at[a * 2 + nb],
                recv_sem=recv.at[a * 2 + nb], device_id=(px, py, c), device_id_type=MESH)

        def first_hop(a, quarter):
            k = 2 * n + a * 2 + quarter
            px, py = chips[quarter]
            return pltpu.make_async_remote_copy(
                src_ref=srcs[a].at[ids[2], rows(a, quarter), :], dst_ref=held(a, quarter), send_sem=send.at[k],
                recv_sem=recv.at[k], device_id=(px, py, c), device_id_type=MESH)

        def second_hop(a, quarter, origin):
            k = 4 * n + a * 2 + quarter
            px, py = chips[1 - quarter]
            return pltpu.make_async_remote_copy(
                src_ref=held(a, quarter), dst_ref=dsts[a].at[origin, rows(a, quarter), :], send_sem=send.at[k],
                recv_sem=recv.at[k], device_id=(px, py, c), device_id_type=MESH)

        own = [pltpu.make_async_copy(srcs[a].at[me], dsts[a].at[me], local.at[a]) for a in range(n)]
        sent = [first_hop(a, q) for a in range(n) for q in range(2)]
        sent += [direct(a, nb, ids[nb], me) for a in range(n) for nb in range(2)]
        for cp in sent + own:
            cp.start()
        for a in range(n):
            for q in range(2):
                first_hop(a, q).wait_recv()
                sent.append(second_hop(a, q, ids[q]))
                sent[-1].start()
        for a in range(n):
            for nb in range(2):
                direct(a, nb, me, ids[nb]).wait_recv()
            for q in range(2):
                second_hop(a, q, ids[2]).wait_recv()
        for cp in sent:
            cp.wait_send()
        for cp in own:
            cp.wait()

    anyspec = pl.BlockSpec(memory_space=pl.ANY)
    out = pl.pallas_call(
        body, name="chip_exchange", in_specs=[anyspec] * n, out_specs=[anyspec] * (2 * n),
        out_shape=[jax.ShapeDtypeStruct(a.shape, a.dtype) for a in arrs]
        + [jax.ShapeDtypeStruct((2, cut, a.shape[2]), a.dtype) for a, cut in zip(arrs, cuts)],
        scratch_shapes=[pltpu.SemaphoreType.DMA((6 * n,)), pltpu.SemaphoreType.DMA((6 * n,)),
                        pltpu.SemaphoreType.DMA((n,))],
    )(*arrs)
    return out[:n]


def _swap_halves(arrs):
    n = len(arrs)

    def body(*refs):
        srcs, dsts = refs[:n], refs[n:2 * n]
        send, recv, local = refs[2 * n:]
        x, y, c = _position()

        def remote(a, landing):
            return pltpu.make_async_remote_copy(
                src_ref=srcs[a], dst_ref=dsts[a].at[landing], send_sem=send.at[a], recv_sem=recv.at[a],
                device_id=(x, y, 1 - c), device_id_type=MESH)

        own = [pltpu.make_async_copy(srcs[a], dsts[a].at[c], local.at[a]) for a in range(n)]
        sends = [remote(a, c) for a in range(n)]
        for cp in sends + own:
            cp.start()
        for a in range(n):
            remote(a, 1 - c).wait_recv()
        for cp in sends:
            cp.wait_send()
        for cp in own:
            cp.wait()

    vm = pl.BlockSpec(memory_space=pltpu.VMEM)
    return pl.pallas_call(
        body, name="swap_halves", in_specs=[vm] * n, out_specs=[vm] * n,
        out_shape=[jax.ShapeDtypeStruct((2,) + a.shape, a.dtype) for a in arrs],
        scratch_shapes=[pltpu.SemaphoreType.DMA((n,)), pltpu.SemaphoreType.DMA((n,)), pltpu.SemaphoreType.DMA((n,))],
        compiler_params=pltpu.CompilerParams(vmem_limit_bytes=VMEM_LIMIT),
    )(*arrs)


def _row_block(r):
    return 128 if r % 128 == 0 else r


def _sum_slots(name, slots):
    s, r, n = slots.shape
    rb = _row_block(r)

    def body(s_ref, o_ref):
        total = s_ref[0].astype(f32)
        for d in range(1, s):
            total = total + s_ref[d].astype(f32)
        o_ref[...] = total

    return pl.pallas_call(
        body, name=name, grid=(r // rb,),
        in_specs=[pl.BlockSpec((s, rb, n), lambda i: (0, i, 0))],
        out_specs=pl.BlockSpec((rb, n), lambda i: (i, 0)),
        out_shape=jax.ShapeDtypeStruct((r, n), f32),
        compiler_params=_params(("parallel",), VMEM_LIMIT),
    )(slots)


def _adamw(name, w, g, m, v):
    r, n = w.shape
    if r % 128 == 0 or r * n <= 128 * 1024:
        rb, nb = _row_block(r), n
    else:
        rb, nb = r, LANES

    def body(w_ref, g_ref, m_ref, v_ref, d_ref, nm_ref, nv_ref):
        gv = g_ref[...]
        m2 = ADAM_B1 * m_ref[...] + (1.0 - ADAM_B1) * gv
        v2 = ADAM_B2 * v_ref[...] + (1.0 - ADAM_B2) * (gv * gv)
        m_hat = m2 / (1.0 - ADAM_B1 ** ADAM_STEP)
        v_hat = v2 / (1.0 - ADAM_B2 ** ADAM_STEP)
        d_ref[...] = (-ADAM_LR) * (m_hat / (jnp.sqrt(v_hat) + ADAM_EPS) + ADAM_WD * w_ref[...])
        nm_ref[...] = m2
        nv_ref[...] = v2

    spec = pl.BlockSpec((rb, nb), lambda i, j: (i, j))
    return pl.pallas_call(
        body, name=name, grid=(r // rb, n // nb), in_specs=[spec] * 4, out_specs=[spec] * 3,
        out_shape=[jax.ShapeDtypeStruct((r, n), f32)] * 3,
        compiler_params=_params(("parallel", "parallel"), VMEM_LIMIT),
    )(w, g, m, v)


def _local_step(x2, tgt2, seq, wt):
    nb = x2.shape[0] // seq
    h = _prenorm(x2, wt["pre_w"])
    qkv = _mm("in_qkv", h, wt["w_qkv"], wt["b_qkv"], bf16, 1024, 1024, w_is_nk=True)
    rest = _mm("in_rest", h, wt["w_rest"], wt["b_rest"], bf16, 1024, 1024, w_is_nk=True)
    f128 = _mm("in_f", h, wt["w_f"], wt["b_f"], f32, 1024, LANES, w_is_nk=True)
    c = _forget_prep(f128, seq)
    qa, ka = _attn_prep(qkv, c)
    o_att, pa, lse = _attn_fwd(qa, ka, qkv, rest, seq)
    ya = _mm("proj_a", pa, wt["w_a"], None, bf16, 1024, D)
    rnn_w = (wt["conv_w"], wt["conv_b"], wt["wa_d"], wt["wx_d"], wt["ba"], wt["bx"], wt["lam"])
    xc, a, hrec, pr = _rnn_fwd(rest, *rnn_w, seq)
    yr = _mm("proj_r", pr, wt["w_r"], None, bf16, 1024, D)
    do, dy, mrg, loss8, d_post = _out_proj_loss(rest, ya, yr, wt["w_o"], x2, tgt2, wt["post_w"])
    dya, dyr, dmga, dmgr = _out_bwd(do, rest, ya, yr, wt["w_o"])
    doa, dga, delta = _branch_bwd("branch_a_bwd", dya, rest, 0, o_att, wt["w_a"], bf16, head_sums=True)
    dhrec, dgr = _branch_bwd("branch_r_bwd", dyr, rest, 2, hrec, wt["w_r"], bf16)
    d_wo, _ = _tn_mm("dw_out", mrg, do, D)
    d_wa, _ = _tn_mm("dw_branch_a", pa, dya, D)
    d_wr, _ = _tn_mm("dw_branch_r", pr, dyr, D)
    dxr, d_wad, d_wxd, vec = _rnn_bwd(dhrec, a, hrec, xc, rest, *rnn_w, seq)
    dq, dk, dv, dc_pairs = _attn_bwd(qa, ka, qkv, doa, lse, delta, seq)
    dc = dc_pairs.reshape(-1, HEADS // ATT_GROUP, LANES)[:, :, :ATT_GROUP].reshape(-1, HEADS)
    df, db_f = _forget_bwd(_pad_cols(dc, LANES), f128, seq)
    pieces = [dq, dk, dv, dga, dxr, dgr, dmga, dmgr]
    gx, d_pre = _in_bwd(pieces, df, x2, dy, wt["w_qkv"], wt["w_rest"], wt["w_f"], wt["pre_w"])
    names = ["q", "k", "v", "ga", "xr", "gr", "mga", "mgr"]
    dws, dbs = [], []
    for nm, piece in zip(names, pieces):
        dw_p, db_p = _tn_mm("dw_in_" + nm, piece, h, D)
        dws.append(dw_p)
        dbs.append(db_p)
    dw_f, _ = _tn_mm("dw_in_f", df, h, D)
    zeros_w = jnp.zeros((IN_TOTAL - IN_USED, D), f32)
    d_w_in = jnp.concatenate(dws[:3] + [dw_f[:HEADS]] + dws[3:] + [zeros_w], axis=0)
    d_b_in = jnp.concatenate(dbs[:3] + [db_f[:, :HEADS]] + dbs[3:] + [zeros_w[:, :1].T], axis=1)
    return dict(loss=loss8[0, 0], grad_x=gx, pre_w=d_pre, w_in=d_w_in, b_in=d_b_in, conv_w=vec[4:8], conv_b=vec[3:4],
                wa_d=d_wad, ba=vec[0:1], wx_d=d_wxd, bx=vec[1:2], lam=vec[2:3], w_a=d_wa, w_r=d_wr, w_o=d_wo,
                post_w=d_post)


def _block_diag(w):
    g, bw, _ = w.shape
    eye = jnp.eye(g, dtype=w.dtype)
    return (w[:, :, None, :] * eye[:, None, :, None]).reshape(g * bw, g * bw)


def _gate_blocks(diag):
    half = diag.shape[1] // 2
    return jnp.stack([diag[:, :half, :half], diag[:, half:, half:]], axis=1).reshape(-1, half, half)


def _pad_cols(a, n):
    return jnp.pad(a, ((0, 0), (0, n - a.shape[1])))


def _pad_rows(a, n):
    return jnp.pad(a, ((0, n - a.shape[0]), (0, 0)))


def kernel(x, pre_norm_w, w_in, b_in, conv_w, conv_b, rg_wa, rg_ba, rg_wx, rg_bx, rg_lambda, w_branch_a, w_branch_r, w_out, post_norm_w, loss_target, m_pre_norm_w, m_w_in, m_b_in, m_conv_w, m_conv_b, m_rg_wa, m_rg_ba, m_rg_wx, m_rg_bx, m_rg_lambda, m_w_branch_a, m_w_branch_r, m_w_out, m_post_norm_w, v_pre_norm_w, v_w_in, v_b_in, v_conv_w, v_conv_b, v_rg_wa, v_rg_ba, v_rg_wx, v_rg_bx, v_rg_lambda, v_w_branch_a, v_w_branch_r, v_w_out, v_post_norm_w):
    nb, seq, _ = x.shape
    chip = 2 * lax.axis_index("x") + lax.axis_index("y")
    n_groups = rg_wa.shape[1]

    w_in_t = jnp.transpose(w_in[0])
    shard_cols = w_in_t.shape[0]
    padded = -(-shard_cols // 32) * 32
    g_in, g_a, g_r, g_o, g_cw = _gather_shards(
        [_pad_rows(w_in_t.astype(bf16), padded), w_branch_a[0].astype(bf16), w_branch_r[0].astype(bf16),
         w_out[0].astype(bf16)], conv_w[0])
    w_full = jnp.concatenate([g_in[j, :shard_cols] for j in range(N_CHIPS)], axis=0)
    q_end, f_end = 3 * D, 3 * D + HEADS
    wt = dict(
        pre_w=pre_norm_w, post_w=post_norm_w,
        w_qkv=w_full[:q_end], b_qkv=b_in[:, :q_end],
        w_f=_pad_rows(w_full[q_end:f_end], LANES), b_f=_pad_cols(b_in[:, q_end:f_end], LANES),
        w_rest=w_full[f_end:IN_USED], b_rest=b_in[:, f_end:IN_USED],
        w_a=g_a.reshape(D, D), w_r=g_r.reshape(D, D), w_o=g_o.reshape(D, D),
        conv_w=jnp.transpose(g_cw, (1, 0, 2)).reshape(4, D), conv_b=conv_b,
        wa_d=_block_diag(rg_wa[0]).astype(bf16), wx_d=_block_diag(rg_wx[0]).astype(bf16),
        ba=rg_ba, bx=rg_bx, lam=rg_lambda)

    part = _local_step(x.reshape(nb * seq, D), loss_target.reshape(nb * seq, D), seq, wt)
    loss = lax.psum(part["loss"], ("x", "y", "c"))
    grad_x = part["grad_x"].reshape(nb, seq, D)

    small = jnp.concatenate([
        part["pre_w"], _pad_cols(part["b_in"], 10 * D).reshape(10, D), part["conv_b"],
        _gate_blocks(part["wa_d"]).reshape(-1, D), part["ba"],
        _gate_blocks(part["wx_d"]).reshape(-1, D), part["bx"], part["lam"], part["post_w"],
        part["conv_w"]], axis=0)
    n_small = small.shape[0]
    n_rep = n_small - 4
    tot = _allsum_rows(_pad_rows(small, -(-n_small // 8) * 8))
    g_rep = tot[:n_rep]
    g_conv_w = lax.dynamic_slice_in_dim(tot[n_rep:n_small], chip * (D // N_CHIPS), D // N_CHIPS, axis=1)

    def unpack(p):
        o = [0]

        def take(k):
            o[0] += k
            return p[o[0] - k:o[0]]

        pre = take(1)
        b = take(10).reshape(1, 10 * D)[:, :IN_TOTAL]
        cb = take(1)
        wa = take(64).reshape(rg_wa.shape)
        ba = take(1)
        wx = take(64).reshape(rg_wx.shape)
        bx = take(1)
        lam = take(1)
        post = take(1)
        return dict(pre_norm_w=pre, b_in=b, conv_b=cb, rg_wa=wa, rg_ba=ba, rg_wx=wx, rg_bx=bx, rg_lambda=lam,
                    post_norm_w=post)

    grads = unpack(g_rep)
    replicated = dict(
        pre_norm_w=(pre_norm_w, m_pre_norm_w, v_pre_norm_w), b_in=(b_in, m_b_in, v_b_in),
        conv_b=(conv_b, m_conv_b, v_conv_b), rg_wa=(rg_wa, m_rg_wa, v_rg_wa), rg_ba=(rg_ba, m_rg_ba, v_rg_ba),
        rg_wx=(rg_wx, m_rg_wx, v_rg_wx), rg_bx=(rg_bx, m_rg_bx, v_rg_bx),
        rg_lambda=(rg_lambda, m_rg_lambda, v_rg_lambda), post_norm_w=(post_norm_w, m_post_norm_w, v_post_norm_w))
    deltas, new_m, new_v = {}, {}, {}
    for name, (w, m, v) in replicated.items():
        as2d = lambda a: a.reshape(-1, D) if a.ndim > 2 else a
        upd = _adamw("adamw_" + name, as2d(w), as2d(grads[name]), as2d(m), as2d(v))
        deltas[name], new_m[name], new_v[name] = [a.reshape(w.shape) for a in upd]

    p_in = jnp.pad(part["w_in"].reshape(N_CHIPS, shard_cols, D), ((0, 0), (0, padded - shard_cols), (0, 0)))
    p_in = p_in.reshape(N_DEV, padded // 2, D)
    p_aro = jnp.concatenate([part[k].reshape(N_DEV, D // N_DEV, D) for k in ("w_a", "w_r", "w_o")], axis=1)
    s_in, s_aro = _chip_exchange([_pair_reduce("pair_w_in", p_in.astype(bf16)),
                                  _pair_reduce("pair_w_aro", p_aro.astype(bf16))])
    f_in, f_aro = _swap_halves([_sum_slots("sum_w_in", s_in), _sum_slots("sum_w_aro", s_aro)])
    g_w_in_t = f_in.reshape(padded, D)[:shard_cols]
    rows = D // N_DEV
    g_aro = [f_aro[:, i * rows:(i + 1) * rows, :].reshape(2 * rows, D) for i in range(3)]

    w_in_upd = _adamw("adamw_w_in", w_in_t, g_w_in_t, jnp.transpose(m_w_in[0]), jnp.transpose(v_w_in[0]))
    g_w_in, d_w_in, nm_w_in, nv_w_in = [jnp.transpose(a) for a in (g_w_in_t, *w_in_upd)]
    upd_a = _adamw("adamw_w_branch_a", w_branch_a[0], g_aro[0], m_w_branch_a[0], v_w_branch_a[0])
    upd_r = _adamw("adamw_w_branch_r", w_branch_r[0], g_aro[1], m_w_branch_r[0], v_w_branch_r[0])
    upd_o = _adamw("adamw_w_out", w_out[0], g_aro[2], m_w_out[0], v_w_out[0])
    d_aro, nm_aro, nv_aro = zip(upd_a, upd_r, upd_o)
    d_cw, nm_cw, nv_cw = _adamw("adamw_conv_w", conv_w[0], g_conv_w, m_conv_w[0], v_conv_w[0])

    def sharded(t_in, t_aro, t_cw):
        return dict(w_in=t_in[None], conv_w=t_cw[None], w_branch_a=t_aro[0][None], w_branch_r=t_aro[1][None],
                    w_out=t_aro[2][None])

    order = ["pre_norm_w", "w_in", "b_in", "conv_w", "conv_b", "rg_wa", "rg_ba", "rg_wx", "rg_bx", "rg_lambda",
             "w_branch_a", "w_branch_r", "w_out", "post_norm_w"]
    outs = [loss, grad_x]
    for rep, shd in ((grads, sharded(g_w_in, g_aro, g_conv_w)), (deltas, sharded(d_w_in, d_aro, d_cw)),
                     (new_m, sharded(nm_w_in, nm_aro, nm_cw)), (new_v, sharded(nv_w_in, nv_aro, nv_cw))):
        both = {**rep, **shd}
        outs.extend(both[k] for k in order)
    return tuple(outs)
```

```python
import jax
import jax.numpy as jnp
from jax import lax
from jax.experimental import pallas as pl
from jax.experimental.pallas import tpu as pltpu

f32 = jnp.float32
bf16 = jnp.bfloat16

D = 1024
HEADS = 16
HEAD_PAIRS = 8
LANES = 128
NORM_EPS = 1e-6
MASK_VALUE = -1e30
RG_C = 8.0
QK_SCALE = 0.125
TQ = 256
ATT_GROUP = 8
ATT_GROUP_FWD = 16
TL = 256
TM = 256
PREV_ROWS = 16
IN_USED = 8 * D + HEADS
IN_TOTAL = 9 * D + HEADS
N_CHIPS = 4
N_DEV = 8
ADAM_LR, ADAM_B1, ADAM_B2, ADAM_EPS, ADAM_WD, ADAM_STEP = 0.001, 0.9, 0.999, 1e-08, 0.01, 10
VMEM_LIMIT = 56 * 1024 * 1024
MESH = pl.DeviceIdType.MESH


def _dot(a, b):
    return jnp.dot(a, b, preferred_element_type=f32)


def _dot_nt(a, b):
    return lax.dot_general(a, b, (((1,), (1,)), ((), ())), preferred_element_type=f32)


def _dot_tn(a, b):
    return lax.dot_general(a, b, (((0,), (0,)), ((), ())), preferred_element_type=f32)


def _sig(x):
    return 0.5 * jnp.tanh(0.5 * x) + 0.5


def _softplus(x):
    return jnp.maximum(x, 0.0) + jnp.log(1.0 + jnp.exp(-jnp.abs(x)))


def _params(sem, vmem=None):
    return pltpu.CompilerParams(dimension_semantics=sem, vmem_limit_bytes=vmem)


def _tile(tm, width, cb=0):
    return pl.BlockSpec((tm, width), lambda i, cb=cb: (i, cb))


def _whole(shape):
    nd = len(shape)
    return pl.BlockSpec(shape, lambda *_: (0,) * nd)


def _prenorm(x, w_pre):
    t = x.shape[0]

    def body(x_ref, w_ref, h_ref):
        xv = x_ref[...]
        r = lax.rsqrt(jnp.mean(xv * xv, axis=-1, keepdims=True) + NORM_EPS)
        h_ref[...] = (xv * r * w_ref[...]).astype(bf16)

    return pl.pallas_call(
        body, name="prenorm", grid=(t // TM,),
        in_specs=[_tile(TM, D), _whole((1, D))], out_specs=_tile(TM, D),
        out_shape=jax.ShapeDtypeStruct((t, D), bf16),
        compiler_params=_params(("parallel",)),
    )(x, w_pre)


def _mm(name, a, w, bias, out_dtype, tm, tn, w_is_nk=False):
    t, k = a.shape
    tm = min(tm, t)
    n = w.shape[0] if w_is_nk else w.shape[1]

    def body(a_ref, w_ref, *refs):
        acc = _dot_nt(a_ref[...], w_ref[...]) if w_is_nk else _dot(a_ref[...], w_ref[...])
        if bias is not None:
            acc = acc + refs[0][...]
        refs[-1][...] = acc.astype(out_dtype)

    in_specs = [pl.BlockSpec((tm, k), lambda i, j: (i, 0)),
                pl.BlockSpec((tn, k), lambda i, j: (j, 0)) if w_is_nk else pl.BlockSpec((k, tn), lambda i, j: (0, j))]
    args = [a, w]
    if bias is not None:
        in_specs.append(pl.BlockSpec((1, tn), lambda i, j: (0, j)))
        args.append(bias)
    return pl.pallas_call(
        body, name=name, grid=(t // tm, n // tn), in_specs=in_specs,
        out_specs=pl.BlockSpec((tm, tn), lambda i, j: (i, j)), out_shape=jax.ShapeDtypeStruct((t, n), out_dtype),
        compiler_params=_params(("parallel", "parallel"), VMEM_LIMIT),
    )(*args)


def _forget_prep(f128, seq):
    t = f128.shape[0]
    nb = seq // LANES

    def body(f_ref, c_ref):
        r = lax.broadcasted_iota(jnp.int32, (LANES, LANES), 0)
        cidx = lax.broadcasted_iota(jnp.int32, (LANES, LANES), 1)
        tri = (r >= cidx).astype(f32)
        carry = jnp.zeros((1, LANES), f32)
        for blk in range(nb):
            fv = f_ref[pl.ds(blk * LANES, LANES), :]
            lf = -_softplus(-fv)
            c_ref[pl.ds(blk * LANES, LANES), :] = (
                jnp.dot(tri, lf, preferred_element_type=f32, precision=lax.Precision.HIGHEST) + carry)
            carry = carry + jnp.sum(lf, axis=0, keepdims=True)

    return pl.pallas_call(
        body, name="forget_prep", grid=(t // seq,),
        in_specs=[pl.BlockSpec((seq, LANES), lambda b: (b, 0))],
        out_specs=pl.BlockSpec((seq, LANES), lambda b: (b, 0)),
        out_shape=jax.ShapeDtypeStruct((t, LANES), f32),
        compiler_params=_params(("parallel",)),
    )(f128)


def _split3(cv):
    hi = cv.astype(bf16)
    r1 = cv - hi.astype(f32)
    mid = r1.astype(bf16)
    lo = (r1 - mid.astype(f32)).astype(bf16)
    return hi, mid, lo


def _attn_prep(qkv, c):
    t = qkv.shape[0]

    def body(q_ref, k_ref, c_ref, qa_ref, ka_ref):
        lane = lax.broadcasted_iota(jnp.int32, (1, LANES), 1)
        cv = c_ref[...]
        one = jnp.ones((), bf16)
        zero = jnp.zeros((), bf16)
        q_ones = jnp.where((lane >= 67) & (lane < 70), one, zero)
        k_ones = jnp.where((lane >= 64) & (lane < 67), one, zero)
        for head in range(HEADS):
            pair = pl.ds((head // 2) * LANES, LANES)
            ch = jnp.sum(jnp.where(lane == head, cv, 0.0), axis=1, keepdims=True)
            hi, mid, lo = _split3(ch)
            q2, k2 = q_ref[:, pair], k_ref[:, pair]
            if head % 2 == 1:
                q2, k2 = pltpu.roll(q2, 64, 1), pltpu.roll(k2, 64, 1)
            qa = jnp.where(lane < 64, q2 * jnp.asarray(QK_SCALE, bf16),
                           jnp.where(lane == 64, hi, jnp.where(lane == 65, mid, jnp.where(lane == 66, lo, q_ones))))
            ka = jnp.where(lane < 64, k2,
                           jnp.where(lane == 67, -hi, jnp.where(lane == 68, -mid, jnp.where(lane == 69, -lo, k_ones))))
            qa_ref[:, pl.ds(head * LANES, LANES)] = qa
            ka_ref[:, pl.ds(head * LANES, LANES)] = ka

    tm = min(TM, t)
    out = pl.BlockSpec((tm, 2 * D), lambda i: (i, 0))
    return pl.pallas_call(
        body, name="attn_prep", grid=(t // tm,),
        in_specs=[_tile(tm, D, 0), _tile(tm, D, 1), _tile(tm, LANES)],
        out_specs=[out, out],
        out_shape=[jax.ShapeDtypeStruct((t, 2 * D), bf16)] * 2,
        compiler_params=_params(("parallel",)),
    )(qkv, qkv, c)


def _attn_fwd(qa, ka, qkv, rest, seq):
    t = qkv.shape[0]
    nb, nq = t // seq, seq // TQ

    hg = ATT_GROUP_FWD
    ng = HEADS // hg

    def body(q_ref, k_ref, v_ref, ga_ref, o_ref, pa_ref, lse_ref, acc_scr):
        qi, gi = pl.program_id(1), pl.program_id(2)
        krow = lax.broadcasted_iota(jnp.int32, (TQ, TQ), 0)
        qcol = lax.broadcasted_iota(jnp.int32, (TQ, TQ), 1)
        acc_scr[...] = jnp.zeros_like(acc_scr)

        def kv_step(kt, carry, masked):
            ks = pl.multiple_of(kt * TQ, TQ)
            sts = [_dot_nt(k_ref[pl.ds(ks, TQ), pl.ds(g * LANES, LANES)], q_ref[:, pl.ds(g * LANES, LANES)])
                   for g in range(hg)]
            if masked:
                sts = [jnp.where(krow <= qcol, st, MASK_VALUE) for st in sts]
            m_new = [jnp.maximum(carry[g][0], jnp.max(sts[g], axis=0, keepdims=True)) for g in range(hg)]
            ps = [jnp.exp(sts[g] - m_new[g]) for g in range(hg)]
            alphas = [jnp.exp(carry[g][0] - m_new[g]) for g in range(hg)]
            phi = [ps[g].astype(bf16) for g in range(hg)]
            plo = [(ps[g] - phi[g].astype(f32)).astype(bf16) for g in range(hg)]
            vs = [v_ref[pl.ds(ks, TQ), pl.ds(j * LANES, LANES)] for j in range(hg // 2)]
            pvs = [_dot_tn(vs[g // 2], phi[g]) + _dot_tn(vs[g // 2], plo[g]) for g in range(hg)]
            olds = [acc_scr[g] for g in range(hg)]
            for g in range(hg):
                acc_scr[g] = alphas[g] * olds[g] + pvs[g]
            return tuple((m_new[g], alphas[g] * carry[g][1] + jnp.sum(ps[g], axis=0, keepdims=True))
                         for g in range(hg))

        init = tuple((jnp.full((1, TQ), MASK_VALUE, f32), jnp.zeros((1, TQ), f32)) for _ in range(hg))
        carry = lax.fori_loop(0, qi, lambda kt, cr: kv_step(kt, cr, False), init)
        stats = kv_step(qi, carry, True)
        drow = lax.broadcasted_iota(jnp.int32, (LANES, TQ), 0)
        for g in range(hg):
            m, l = stats[g]
            lse_ref[0, pl.ds(hg * gi + g, 1), :] = m + jnp.log(l)
        for j in range(hg // 2):
            o2 = jnp.where(drow < 64, acc_scr[2 * j] / stats[2 * j][1], acc_scr[2 * j + 1] / stats[2 * j + 1][1]).T
            o_ref[:, pl.ds(j * LANES, LANES)] = o2
            ga = ga_ref[:, pl.ds(j * LANES, LANES)].astype(f32)
            pa_ref[:, pl.ds(j * LANES, LANES)] = (o2 * (ga * _sig(ga))).astype(bf16)

    vw = hg * 64
    tile = pl.BlockSpec((TQ, vw), lambda b, qi, gi: (b * nq + qi, gi))
    return pl.pallas_call(
        body, name="attn_fwd", grid=(nb, nq, ng),
        in_specs=[pl.BlockSpec((TQ, hg * LANES), lambda b, qi, gi: (b * nq + qi, gi)),
                  pl.BlockSpec((seq, hg * LANES), lambda b, qi, gi: (b, gi)),
                  pl.BlockSpec((seq, vw), lambda b, qi, gi: (b, 2 * ng + gi)), tile],
        out_specs=[tile, tile, pl.BlockSpec((1, HEADS, TQ), lambda b, qi, gi: (b * nq + qi, 0, 0))],
        out_shape=[jax.ShapeDtypeStruct((t, D), f32), jax.ShapeDtypeStruct((t, D), bf16),
                   jax.ShapeDtypeStruct((t // TQ, HEADS, TQ), f32)],
        scratch_shapes=[pltpu.VMEM((hg, LANES, TQ), f32)],
        compiler_params=_params(("parallel", "parallel", "arbitrary"), VMEM_LIMIT),
    )(qa, ka, qkv, rest)


def _shifted_rows(x, top8, prev8, shift, row, row8):
    body = pltpu.roll(x, shift, 0)
    head = jnp.where(row8 < shift, pltpu.roll(prev8, shift, 0), pltpu.roll(top8, shift, 0))
    return body, head


def _rnn_gates(xc, wa_ref, wx_ref, ba_ref, bx_ref, lam_ref):
    xcb = xc.astype(bf16)
    r = _sig(_dot(xcb, wa_ref[...]) + ba_ref[...])
    i = _sig(_dot(xcb, wx_ref[...]) + bx_ref[...])
    sp = _softplus(-lam_ref[...])
    log_a = (-RG_C) * r * sp
    th = jnp.tanh(log_a)
    w1 = (-2.0) * th / (1.0 - th)
    sq = jnp.sqrt(jnp.maximum(w1, 0.0))
    return r, i, sp, log_a, w1, sq


def _conv_tile(x_ref, xprev_ref, has_prev, cw_ref, cb_ref, xc_ref):
    row = lax.broadcasted_iota(jnp.int32, (TL, D), 0)
    row8 = lax.broadcasted_iota(jnp.int32, (8, D), 0)
    x = x_ref[...].astype(f32)
    top8 = x[:8]
    prev8 = jnp.where(has_prev, xprev_ref[...].astype(f32)[PREV_ROWS - 8:], 0.0)
    xc = cb_ref[...] + cw_ref[pl.ds(3, 1), :] * x
    xc8 = cb_ref[...] + cw_ref[pl.ds(3, 1), :] * top8
    for sh in range(1, 4):
        w = cw_ref[pl.ds(3 - sh, 1), :]
        xs, xs8 = _shifted_rows(x, top8, prev8, sh, row, row8)
        xc = xc + w * xs
        xc8 = xc8 + w * xs8
    xc_ref[...] = xc
    xc_ref[pl.ds(0, 8), :] = xc8


def _rnn_fwd(rest, conv_w, conv_b, wa_d, wx_d, ba, bx, lam, seq):
    t = rest.shape[0]
    nb, nt = t // seq, seq // TL

    def body(x_ref, xprev_ref, gr_ref, cw_ref, cb_ref, wa_ref, wx_ref, ba_ref, bx_ref, lam_ref,
             xc_ref, a_ref, h_ref, pr_ref, xc_scr, u_scr, h_scr, carry):
        tt = pl.program_id(1)
        _conv_tile(x_ref, xprev_ref, tt > 0, cw_ref, cb_ref, xc_scr)
        xc = xc_scr[...]
        xc_ref[...] = xc.astype(bf16)
        r, i, sp, log_a, w1, sq = _rnn_gates(xc, wa_ref, wx_ref, ba_ref, bx_ref, lam_ref)
        a_ref[...] = jnp.exp(log_a)
        u_scr[...] = sq * (i * xc)

        @pl.when(tt == 0)
        def _():
            carry[...] = jnp.zeros_like(carry)

        def step(s, h):
            h = a_ref[pl.ds(s, 1), :] * h + u_scr[pl.ds(s, 1), :]
            h_scr[pl.ds(s, 1), :] = h
            return h

        carry[...] = lax.fori_loop(0, TL, step, carry[...], unroll=8)
        gr = gr_ref[...].astype(f32)
        h = h_scr[...]
        h_ref[...] = h.astype(bf16)
        pr_ref[...] = (h * (gr * _sig(gr))).astype(bf16)

    tile = lambda cb: pl.BlockSpec((TL, D), lambda b, tt, cb=cb: (b * nt + tt, cb))
    prev = lambda cb: pl.BlockSpec(
        (PREV_ROWS, D), lambda b, tt, cb=cb: (jnp.maximum((b * nt + tt) * (TL // PREV_ROWS) - 1, 0), cb))
    vec = _whole((1, D))
    return pl.pallas_call(
        body, name="rnn_fwd", grid=(nb, nt),
        in_specs=[tile(1), prev(1), tile(2), _whole((4, D)), vec, _whole((D, D)), _whole((D, D)), vec, vec, vec],
        out_specs=[tile(0)] * 4,
        out_shape=[jax.ShapeDtypeStruct((t, D), dt) for dt in (bf16, f32, bf16, bf16)],
        scratch_shapes=[pltpu.VMEM((TL, D), f32)] * 3 + [pltpu.VMEM((1, D), f32)],
        compiler_params=_params(("parallel", "arbitrary"), VMEM_LIMIT),
    )(rest, rest, rest, conv_w, conv_b, wa_d, wx_d, ba, bx, lam)


def _merge(mga, mgr, ya, yr):
    return (_sig(mga.astype(f32)) * ya.astype(f32) + _sig(mgr.astype(f32)) * yr.astype(f32)).astype(bf16)


def _out_proj_loss(rest, ya, yr, w_out, x, tgt, w_post):
    t = x.shape[0]

    def body(mga_ref, mgr_ref, ya_ref, yr_ref, wo_ref, x_ref, t_ref, w_ref, do_ref, dy_ref, mrg_ref, loss_ref, dwp_ref):
        @pl.when(pl.program_id(0) == 0)
        def _():
            loss_ref[...] = jnp.zeros_like(loss_ref)
            dwp_ref[...] = jnp.zeros_like(dwp_ref)

        mrg = _merge(mga_ref[...], mgr_ref[...], ya_ref[...], yr_ref[...])
        mrg_ref[...] = mrg
        ov = _dot(mrg, wo_ref[...])
        w = w_ref[...]
        r2 = lax.rsqrt(jnp.mean(ov * ov, axis=-1, keepdims=True) + NORM_EPS)
        oh = ov * r2
        e = x_ref[...] + oh * w - t_ref[...]
        loss_ref[...] += 0.5 * jnp.sum(jnp.mean(e * e, axis=-1, keepdims=True))
        dy = e * (1.0 / D)
        dy_ref[...] = dy
        dwp_ref[...] += jnp.sum(dy * oh, axis=0, keepdims=True)
        doh = dy * w
        do_ref[...] = (r2 * (doh - oh * jnp.mean(doh * oh, axis=-1, keepdims=True))).astype(bf16)

    return pl.pallas_call(
        body, name="out_proj_loss", grid=(t // TM,),
        in_specs=[_tile(TM, D, 3), _tile(TM, D, 4), _tile(TM, D), _tile(TM, D), _whole((D, D)), _tile(TM, D),
                  _tile(TM, D), _whole((1, D))],
        out_specs=[_tile(TM, D), _tile(TM, D), _tile(TM, D), _whole((8, LANES)), _whole((1, D))],
        out_shape=[jax.ShapeDtypeStruct((t, D), bf16), jax.ShapeDtypeStruct((t, D), f32),
                   jax.ShapeDtypeStruct((t, D), bf16), jax.ShapeDtypeStruct((8, LANES), f32),
                   jax.ShapeDtypeStruct((1, D), f32)],
        compiler_params=_params(("arbitrary",), VMEM_LIMIT),
    )(rest, rest, ya, yr, w_out, x, tgt, w_post)


def _out_bwd(do, rest, ya, yr, w_out):
    t = do.shape[0]

    def body(do_ref, mga_ref, mgr_ref, ya_ref, yr_ref, w_ref, dya_ref, dyr_ref, dmga_ref, dmgr_ref):
        sa, sr = _sig(mga_ref[...].astype(f32)), _sig(mgr_ref[...].astype(f32))
        ya, yr = ya_ref[...].astype(f32), yr_ref[...].astype(f32)
        dm = _dot_nt(do_ref[...], w_ref[...])
        dya_ref[...] = (dm * sa).astype(bf16)
        dyr_ref[...] = (dm * sr).astype(bf16)
        dmga_ref[...] = (dm * ya * sa * (1.0 - sa)).astype(bf16)
        dmgr_ref[...] = (dm * yr * sr * (1.0 - sr)).astype(bf16)

    return pl.pallas_call(
        body, name="out_bwd", grid=(t // TM,),
        in_specs=[_tile(TM, D), _tile(TM, D, 3), _tile(TM, D, 4), _tile(TM, D), _tile(TM, D), _whole((D, D))],
        out_specs=[_tile(TM, D)] * 4,
        out_shape=[jax.ShapeDtypeStruct((t, D), bf16)] * 4,
        compiler_params=_params(("parallel",), VMEM_LIMIT),
    )(do, rest, rest, ya, yr, w_out)


def _branch_bwd(name, dyb, rest, gate_cb, act, w, act_grad_dtype, head_sums=False):
    t = dyb.shape[0]

    def body(dy_ref, g_ref, act_ref, w_ref, dact_ref, dg_ref, *delta_ref):
        dp = _dot_nt(dy_ref[...], w_ref[...])
        g = g_ref[...].astype(f32)
        sg = _sig(g)
        act = act_ref[...].astype(f32)
        dact = (dp * (g * sg)).astype(act_grad_dtype)
        dact_ref[...] = dact
        dg_ref[...] = (dp * act * (sg * (1.0 + g * (1.0 - sg)))).astype(bf16)
        if head_sums:
            ch = lax.broadcasted_iota(jnp.int32, (D, LANES), 0)
            hd = lax.broadcasted_iota(jnp.int32, (D, LANES), 1)
            pick = (ch // 64 == hd).astype(bf16)
            per_head = sum(_dot(piece, pick) for piece in _split3(dact.astype(f32) * act))
            delta_ref[0][0] = per_head.T[:HEADS, :]

    out_specs = [_tile(TM, D), _tile(TM, D)]
    out_shape = [jax.ShapeDtypeStruct((t, D), act_grad_dtype), jax.ShapeDtypeStruct((t, D), bf16)]
    if head_sums:
        out_specs.append(pl.BlockSpec((1, HEADS, TM), lambda i: (i, 0, 0)))
        out_shape.append(jax.ShapeDtypeStruct((t // TM, HEADS, TM), f32))
    return pl.pallas_call(
        body, name=name, grid=(t // TM,),
        in_specs=[_tile(TM, D), _tile(TM, D, gate_cb), _tile(TM, D), _whole((D, D))],
        out_specs=out_specs, out_shape=out_shape,
        compiler_params=_params(("parallel",), VMEM_LIMIT),
    )(dyb, rest, act, w)


def _rnn_bwd(dh, a, h, xc, rest, conv_w, conv_b, wa_d, wx_d, ba, bx, lam, seq):
    t = dh.shape[0]
    nb, nt = t // seq, seq // TL
    diag = (D // LANES, LANES, LANES)

    def body(dh_ref, a_ref, h_ref, hprev_ref, xc_ref, x_ref, xprev_ref, cw_ref, cb_ref, wa_ref, wx_ref,
             ba_ref, bx_ref, lam_ref, dxr_ref, dwa_ref, dwx_ref, vec_ref, g_scr, dxc_scr, dxr_scr, qcarry, dxc_next):
        b, tt = pl.program_id(0), pl.program_id(1)
        rt = nt - 1 - tt

        @pl.when((b == 0) & (tt == 0))
        def _():
            dwa_ref[...] = jnp.zeros_like(dwa_ref)
            dwx_ref[...] = jnp.zeros_like(dwx_ref)
            vec_ref[...] = jnp.zeros_like(vec_ref)

        @pl.when(tt == 0)
        def _():
            qcarry[...] = jnp.zeros_like(qcarry)
            dxc_next[...] = jnp.zeros_like(dxc_next)

        g_scr[...] = dh_ref[...].astype(f32)

        def step(k, q):
            s = TL - 1 - k
            g = g_scr[pl.ds(s, 1), :] + q
            g_scr[pl.ds(s, 1), :] = g
            return a_ref[pl.ds(s, 1), :] * g

        qcarry[...] = lax.fori_loop(0, TL, step, qcarry[...], unroll=8)

        row = lax.broadcasted_iota(jnp.int32, (TL, D), 0)
        row8 = lax.broadcasted_iota(jnp.int32, (8, D), 0)
        g = g_scr[...]
        av = a_ref[...]
        xc = xc_ref[...].astype(f32)
        hlast = jnp.where(rt > 0, hprev_ref[...].astype(f32)[PREV_ROWS - 1:], 0.0)
        hp = jnp.where(row == 0, hlast, pltpu.roll(h_ref[...].astype(f32), 1, 0))
        r, i, sp, log_a, w1, sq = _rnn_gates(xc, wa_ref, wx_ref, ba_ref, bx_ref, lam_ref)
        dix = g * sq
        di = dix * xc
        dxc = dix * i
        dsq = g * (i * xc)
        dlog_a = g * hp * av - dsq * jnp.where(sq > 0.0, (1.0 - w1) / sq, 0.0)
        dpr = (dlog_a * ((-RG_C) * sp)) * r * (1.0 - r)
        dpi = di * i * (1.0 - i)
        dprb, dpib, xcb = dpr.astype(bf16), dpi.astype(bf16), xc.astype(bf16)
        dxc = dxc + _dot_nt(dprb, wa_ref[...]) + _dot_nt(dpib, wx_ref[...])
        for j in range(D // LANES):
            cols = slice(j * LANES, (j + 1) * LANES)
            dwa_ref[j] += _dot_tn(xcb[:, cols], dprb[:, cols])
            dwx_ref[j] += _dot_tn(xcb[:, cols], dpib[:, cols])
        vec_ref[pl.ds(0, 1), :] += jnp.sum(dpr, axis=0, keepdims=True)
        vec_ref[pl.ds(1, 1), :] += jnp.sum(dpi, axis=0, keepdims=True)
        dsp = jnp.sum(dlog_a * ((-RG_C) * r), axis=0, keepdims=True)
        vec_ref[pl.ds(2, 1), :] += dsp * (-_sig(-lam_ref[...]))
        vec_ref[pl.ds(3, 1), :] += jnp.sum(dxc, axis=0, keepdims=True)

        dxc_scr[...] = dxc
        bot8 = dxc_scr[pl.ds(TL - 8, 8), :]
        nxt8 = dxc_next[...]
        dxr = cw_ref[pl.ds(3, 1), :] * dxc
        dxr8 = cw_ref[pl.ds(3, 1), :] * bot8
        for sh in range(1, 4):
            w = cw_ref[pl.ds(3 - sh, 1), :]
            dxr = dxr + w * pltpu.roll(dxc, TL - sh, 0)
            dxr8 = dxr8 + w * jnp.where(row8 < 8 - sh, pltpu.roll(bot8, 8 - sh, 0), pltpu.roll(nxt8, 8 - sh, 0))
        dxr_scr[...] = dxr
        dxr_scr[pl.ds(TL - 8, 8), :] = dxr8
        dxr_ref[...] = dxr_scr[...].astype(bf16)
        dxc_next[...] = dxc_scr[pl.ds(0, 8), :]

        x = x_ref[...].astype(f32)
        prev8 = jnp.where(rt > 0, xprev_ref[...].astype(f32)[PREV_ROWS - 8:], 0.0)
        dxc_top8 = dxc_scr[pl.ds(0, 8), :]
        vec_ref[pl.ds(7, 1), :] += jnp.sum(dxc * x, axis=0, keepdims=True)
        for sh in range(1, 4):
            inside = jnp.sum(dxc * jnp.where(row >= sh, pltpu.roll(x, sh, 0), 0.0), axis=0, keepdims=True)
            above = jnp.sum(dxc_top8 * jnp.where(row8 < sh, pltpu.roll(prev8, sh, 0), 0.0), axis=0, keepdims=True)
            vec_ref[pl.ds(7 - sh, 1), :] += inside + above

    tile = lambda cb: pl.BlockSpec((TL, D), lambda b, tt, cb=cb: (b * nt + nt - 1 - tt, cb))
    prev = lambda cb: pl.BlockSpec(
        (PREV_ROWS, D), lambda b, tt, cb=cb: (jnp.maximum((b * nt + nt - 1 - tt) * (TL // PREV_ROWS) - 1, 0), cb))
    vec = _whole((1, D))
    return pl.pallas_call(
        body, name="rnn_bwd", grid=(nb, nt),
        in_specs=[tile(0), tile(0), tile(0), prev(0), tile(0), tile(1), prev(1),
                  _whole((4, D)), vec, _whole((D, D)), _whole((D, D)), vec, vec, vec],
        out_specs=[tile(0), _whole(diag), _whole(diag), _whole((8, D))],
        out_shape=[jax.ShapeDtypeStruct((t, D), bf16), jax.ShapeDtypeStruct(diag, f32),
                   jax.ShapeDtypeStruct(diag, f32), jax.ShapeDtypeStruct((8, D), f32)],
        scratch_shapes=[pltpu.VMEM((TL, D), f32), pltpu.VMEM((TL, D), f32), pltpu.VMEM((TL, D), f32),
                        pltpu.VMEM((1, D), f32), pltpu.VMEM((8, D), f32)],
        compiler_params=_params(("arbitrary", "arbitrary"), VMEM_LIMIT),
    )(dh, a, h, h, xc, rest, rest, conv_w, conv_b, wa_d, wx_d, ba, bx, lam)


def _attn_bwd(qa, ka, qkv, doa, lse, delta, seq):
    t = qkv.shape[0]
    nb, nq = t // seq, seq // TQ
    hg = ATT_GROUP
    ng, npair = HEADS // hg, hg // 2

    def body(qa_ref, ka_ref, q_ref, k_ref, v_ref, do_ref, lse_ref, dl_ref, dq_ref, dk_ref, dv_ref, dc_ref,
             dqt_scr, dk_scr, dv_scr, ds_scr, kht_scr):
        gi, kt = pl.program_id(1), pl.program_id(2)
        lane = lax.broadcasted_iota(jnp.int32, (1, LANES), 1)
        krow = lax.broadcasted_iota(jnp.int32, (TQ, TQ), 0)
        qcol = lax.broadcasted_iota(jnp.int32, (TQ, TQ), 1)
        lmask = [(lane // 64) == hh for hh in range(2)]
        scale = jnp.asarray(QK_SCALE, bf16)

        @pl.when(kt == 0)
        def _():
            dqt_scr[...] = jnp.zeros_like(dqt_scr)

        dk_scr[...] = jnp.zeros_like(dk_scr)
        dv_scr[...] = jnp.zeros_like(dv_scr)
        ds_scr[...] = jnp.zeros_like(ds_scr)
        for g in range(hg):
            k2 = k_ref[:, pl.ds((g // 2) * LANES, LANES)]
            kht_scr[g] = jnp.where(lmask[g % 2], k2, jnp.zeros_like(k2)).T

        def q_step(qt, masked):
            qs = pl.multiple_of(qt * TQ, TQ)
            heads = range(hg)
            do2 = [do_ref[pl.ds(qs, TQ), pl.ds(j * LANES, LANES)] for j in range(npair)]
            q2 = [q_ref[pl.ds(qs, TQ), pl.ds(j * LANES, LANES)] for j in range(npair)]
            doh = [jnp.where(lmask[g % 2], do2[g // 2], jnp.zeros_like(do2[0])) for g in heads]
            qh = [jnp.where(lmask[g % 2], q2[g // 2], jnp.zeros_like(q2[0])) * scale for g in heads]
            st = [_dot_nt(ka_ref[:, pl.ds(g * LANES, LANES)], qa_ref[pl.ds(qs, TQ), pl.ds(g * LANES, LANES)])
                  for g in heads]
            if masked:
                st = [jnp.where(krow <= qcol, s, MASK_VALUE) for s in st]
            dp = [_dot_nt(v_ref[:, pl.ds((g // 2) * LANES, LANES)], doh[g]) for g in heads]
            p = [jnp.exp(st[g] - lse_ref[qt, pl.ds(hg * gi + g, 1), :]) for g in heads]
            ds = [p[g] * (dp[g] - dl_ref[qt, pl.ds(hg * gi + g, 1), :]) for g in heads]
            pb = [x.astype(bf16) for x in p]
            dsb = [x.astype(bf16) for x in ds]
            for j in range(npair):
                a, b = 2 * j, 2 * j + 1
                dv_scr[j] += _dot(pb[a], doh[a]) + _dot(pb[b], doh[b])
                dk_scr[j] += _dot(dsb[a], qh[a]) + _dot(dsb[b], qh[b])
                dqt_scr[qt, j] += (_dot(kht_scr[a], dsb[a]) + _dot(kht_scr[b], dsb[b])) * QK_SCALE
            for g in heads:
                ds_scr[g] += ds[g][:, :LANES] + ds[g][:, LANES:]

        q_step(kt, True)

        def loop_body(qt, carry):
            q_step(qt, False)
            return carry

        lax.fori_loop(kt + 1, nq, loop_body, 0)

        dc = jnp.zeros((TQ, LANES), f32)
        for g in range(hg):
            dc = jnp.where(lane == g, -jnp.sum(ds_scr[g], axis=1, keepdims=True), dc)
        dc_ref[...] = dc
        for j in range(npair):
            dk_ref[:, pl.ds(j * LANES, LANES)] = dk_scr[j].astype(bf16)
            dv_ref[:, pl.ds(j * LANES, LANES)] = dv_scr[j].astype(bf16)

        @pl.when(kt == nq - 1)
        def _():
            for qt in range(nq):
                for j in range(npair):
                    dq_ref[pl.ds(qt * TQ, TQ), pl.ds(j * LANES, LANES)] = dqt_scr[qt, j].T.astype(bf16)

    vw = hg * 64
    seqspec = pl.BlockSpec((seq, vw), lambda b, gi, kt: (b, gi))
    kspec = lambda off: pl.BlockSpec((TQ, vw), lambda b, gi, kt: (b * nq + kt, off + gi))
    rowspec = pl.BlockSpec((nq, HEADS, TQ), lambda b, gi, kt: (b, 0, 0))
    return pl.pallas_call(
        body, name="attn_bwd", grid=(nb, ng, nq),
        in_specs=[pl.BlockSpec((seq, hg * LANES), lambda b, gi, kt: (b, gi)),
                  pl.BlockSpec((TQ, hg * LANES), lambda b, gi, kt: (b * nq + kt, gi)),
                  seqspec, kspec(ng), kspec(2 * ng), seqspec, rowspec, rowspec],
        out_specs=[seqspec, kspec(0), kspec(0), pl.BlockSpec((TQ, LANES), lambda b, gi, kt: (b * nq + kt, gi))],
        out_shape=[jax.ShapeDtypeStruct((t, D), bf16)] * 3 + [jax.ShapeDtypeStruct((t, ng * LANES), f32)],
        scratch_shapes=[pltpu.VMEM((nq, npair, LANES, TQ), f32), pltpu.VMEM((npair, TQ, LANES), f32),
                        pltpu.VMEM((npair, TQ, LANES), f32), pltpu.VMEM((hg, TQ, LANES), f32),
                        pltpu.VMEM((hg, LANES, TQ), bf16)],
        compiler_params=_params(("parallel", "parallel", "arbitrary"), VMEM_LIMIT),
    )(qa, ka, qkv, qkv, qkv, doa, lse, delta)


def _forget_bwd(dc, f128, seq):
    t = f128.shape[0]
    nb = seq // LANES

    def body(dc_ref, f_ref, df_ref, dbf_ref):
        @pl.when(pl.program_id(0) == 0)
        def _():
            dbf_ref[...] = jnp.zeros_like(dbf_ref)

        r = lax.broadcasted_iota(jnp.int32, (LANES, LANES), 0)
        cidx = lax.broadcasted_iota(jnp.int32, (LANES, LANES), 1)
        tri = (r <= cidx).astype(f32)
        carry = jnp.zeros((1, LANES), f32)
        total = jnp.zeros((1, LANES), f32)
        for blk in reversed(range(nb)):
            dcb = dc_ref[pl.ds(blk * LANES, LANES), :]
            dlf = jnp.dot(tri, dcb, preferred_element_type=f32, precision=lax.Precision.HIGHEST) + carry
            df = dlf * _sig(-f_ref[pl.ds(blk * LANES, LANES), :])
            df_ref[pl.ds(blk * LANES, LANES), :] = df.astype(bf16)
            total = total + jnp.sum(df, axis=0, keepdims=True)
            carry = carry + jnp.sum(dcb, axis=0, keepdims=True)
        dbf_ref[...] += total

    return pl.pallas_call(
        body, name="forget_bwd", grid=(t // seq,),
        in_specs=[pl.BlockSpec((seq, LANES), lambda b: (b, 0)), pl.BlockSpec((seq, LANES), lambda b: (b, 0))],
        out_specs=[pl.BlockSpec((seq, LANES), lambda b: (b, 0)), _whole((1, LANES))],
        out_shape=[jax.ShapeDtypeStruct((t, LANES), bf16), jax.ShapeDtypeStruct((1, LANES), f32)],
        compiler_params=_params(("arbitrary",)),
    )(dc, f128)


def _in_bwd(dz, df, x, dy, w_qkv, w_rest, w_f, w_pre):
    t = x.shape[0]
    n_qkv = w_qkv.shape[0] // D
    n_rest = w_rest.shape[0] // D

    def body(*refs):
        dz_refs = refs[:n_qkv + n_rest]
        df_ref, x_ref, dy_ref, wq_ref, wr_ref, wf_ref, wp_ref, gx_ref, dwp_ref = refs[n_qkv + n_rest:]

        @pl.when(pl.program_id(0) == 0)
        def _():
            dwp_ref[...] = jnp.zeros_like(dwp_ref)

        dh = _dot(df_ref[...], wf_ref[...])
        for p in range(n_qkv):
            dh = dh + _dot(dz_refs[p][...], wq_ref[pl.ds(p * D, D), :])
        for p in range(n_rest):
            dh = dh + _dot(dz_refs[n_qkv + p][...], wr_ref[pl.ds(p * D, D), :])
        xv = x_ref[...]
        r1 = lax.rsqrt(jnp.mean(xv * xv, axis=-1, keepdims=True) + NORM_EPS)
        xh = xv * r1
        dwp_ref[...] += jnp.sum(dh * xh, axis=0, keepdims=True)
        dxh = dh * wp_ref[...]
        gx_ref[...] = dy_ref[...] + r1 * (dxh - xh * jnp.mean(dxh * xh, axis=-1, keepdims=True))

    once = lambda shape: pl.BlockSpec(shape, lambda i: (0, 0), pipeline_mode=pl.Buffered(1))
    return pl.pallas_call(
        body, name="in_bwd", grid=(t // TM,),
        in_specs=[_tile(TM, D)] * (n_qkv + n_rest) + [_tile(TM, LANES), _tile(TM, D), _tile(TM, D),
                  once(w_qkv.shape), once(w_rest.shape), once(w_f.shape), _whole((1, D))],
        out_specs=[_tile(TM, D), _whole((1, D))],
        out_shape=[jax.ShapeDtypeStruct((t, D), f32), jax.ShapeDtypeStruct((1, D), f32)],
        compiler_params=_params(("arbitrary",), VMEM_LIMIT),
    )(*dz, df, x, dy, w_qkv, w_rest, w_f, w_pre)


def _tn_mm(name, a, b, tn, tk=2048):
    t, k = a.shape
    tk = min(tk, t)
    n = b.shape[1]

    def body(a_ref, b_ref, o_ref, s_ref):
        j, kk = pl.program_id(0), pl.program_id(1)

        @pl.when(kk == 0)
        def _():
            o_ref[...] = jnp.zeros_like(o_ref)

        @pl.when((j == 0) & (kk == 0))
        def _():
            s_ref[...] = jnp.zeros_like(s_ref)

        av = a_ref[...]
        o_ref[...] += _dot_tn(av, b_ref[...])

        @pl.when(j == 0)
        def _():
            s_ref[...] += jnp.sum(av.astype(f32), axis=0, keepdims=True)

    return pl.pallas_call(
        body, name=name, grid=(n // tn, t // tk),
        in_specs=[pl.BlockSpec((tk, k), lambda j, kk: (kk, 0)), pl.BlockSpec((tk, tn), lambda j, kk: (kk, j))],
        out_specs=[pl.BlockSpec((k, tn), lambda j, kk: (0, j)), _whole((1, k))],
        out_shape=[jax.ShapeDtypeStruct((k, n), f32), jax.ShapeDtypeStruct((1, k), f32)],
        compiler_params=_params(("arbitrary", "arbitrary"), VMEM_LIMIT),
    )(a, b)


def _position():
    return lax.axis_index("x"), lax.axis_index("y"), lax.axis_index("c")


def _gather_shards(parts, small):
    n = len(parts)
    halves = [p.shape[0] // 2 for p in parts]
    cuts = [-(-h // 32) * 16 for h in halves]
    n_direct, n_relay, n_sib = 4 * n, 2 * n, 6 * n

    def body(*refs):
        srcs, small_src = refs[:n], refs[n]
        dsts, small_dst = refs[n + 1:2 * n + 1], refs[2 * n + 1]
        send, recv, local = refs[2 * n + 2:]
        x, y, c = _position()
        me = 2 * x + y
        chips = [(1 - x, y), (x, 1 - y), (1 - x, 1 - y)]
        ids = [2 * px + py for px, py in chips]

        def rows(a, half, quarter):
            start = half * halves[a] + (cuts[a] if quarter else 0)
            return pl.ds(start, halves[a] - cuts[a] if quarter else cuts[a])

        def landing(a, shard, half, quarter):
            return dsts[a].at[shard, rows(a, half, quarter), :]

        def direct(a, nb, quarter, shard):
            k = (a * 2 + nb) * 2 + quarter
            px, py = chips[nb]
            return pltpu.make_async_remote_copy(
                src_ref=srcs[a].at[rows(a, c, quarter), :], dst_ref=landing(a, shard, c, quarter),
                send_sem=send.at[k], recv_sem=recv.at[k], device_id=(px, py, c), device_id_type=MESH)

        def relay(a, quarter, shard):
            k = n_direct + a * 2 + quarter
            px, py = chips[1 - quarter]
            return pltpu.make_async_remote_copy(
                src_ref=landing(a, shard, c, quarter), dst_ref=landing(a, shard, c, quarter),
                send_sem=send.at[k], recv_sem=recv.at[k], device_id=(px, py, c), device_id_type=MESH)

        def to_sibling(a, origin, quarter, half):
            k = n_direct + n_relay + (a * 3 + origin) * 2 + quarter
            return pltpu.make_async_remote_copy(
                src_ref=landing(a, ids[origin], half, quarter), dst_ref=landing(a, ids[origin], half, quarter),
                send_sem=send.at[k], recv_sem=recv.at[k], device_id=(x, y, 1 - c), device_id_type=MESH)

        def small_copy(j, shard):
            k = n_direct + n_relay + n_sib + j
            px, py = chips[j]
            return pltpu.make_async_remote_copy(
                src_ref=small_src, dst_ref=small_dst.at[shard], send_sem=send.at[k], recv_sem=recv.at[k],
                device_id=(px, py, c), device_id_type=MESH)

        own = [pltpu.make_async_copy(srcs[a], dsts[a].at[me], local.at[a]) for a in range(n)]
        own.append(pltpu.make_async_copy(small_src, small_dst.at[me], local.at[n]))
        for cp in own:
            cp.start()
        sent = [direct(a, nb, q, me) for q in range(2) for a in range(n) for nb in range(2)]
        sent += [small_copy(j, me) for j in range(3)]
        for cp in sent:
            cp.start()

        def passed_on(cp):
            cp.start()
            sent.append(cp)

        for q in range(2):
            for a in range(n):
                for nb in range(2):
                    direct(a, nb, q, ids[nb]).wait_recv()
                    passed_on(to_sibling(a, nb, q, c))
                    if nb == q:
                        passed_on(relay(a, q, ids[nb]))
        for a in range(n):
            for q in range(2):
                relay(a, q, ids[2]).wait_recv()
                passed_on(to_sibling(a, 2, q, c))
        for j in range(3):
            small_copy(j, ids[j]).wait_recv()
            for a in range(n):
                for q in range(2):
                    to_sibling(a, j, q, 1 - c).wait_recv()
        for cp in sent:
            cp.wait_send()
        for cp in own:
            cp.wait()

    vm = pl.BlockSpec(memory_space=pltpu.VMEM)
    n_sems = n_direct + n_relay + n_sib + 3
    return pl.pallas_call(
        body, name="gather_shards",
        in_specs=[vm] * (n + 1), out_specs=[vm] * (n + 1),
        out_shape=[jax.ShapeDtypeStruct((N_CHIPS,) + p.shape, p.dtype) for p in parts + [small]],
        scratch_shapes=[pltpu.SemaphoreType.DMA((n_sems,)), pltpu.SemaphoreType.DMA((n_sems,)),
                        pltpu.SemaphoreType.DMA((n + 1,))],
        compiler_params=pltpu.CompilerParams(vmem_limit_bytes=VMEM_LIMIT),
    )(*parts, small)


def _allsum_rows(part):
    rows_n = part.shape[0]

    def body(x_ref, gath_ref, sum_ref, send_sems, recv_sems, local_sem):
        x, y, c = _position()
        me, sibling = (x, y, c), (x, y, 1 - c)
        chips = [(1 - x, y), (x, 1 - y), (1 - x, 1 - y)]

        def rows(px, py, pc):
            return gath_ref.at[pl.ds((4 * px + 2 * py + pc) * rows_n, rows_n), :]

        def copy(k, block, to, src=None):
            return pltpu.make_async_remote_copy(
                src_ref=rows(*block) if src is None else src, dst_ref=rows(*block),
                send_sem=send_sems.at[k], recv_sem=recv_sems.at[k], device_id=to, device_id_type=MESH)

        mine = pltpu.make_async_copy(x_ref, rows(*me), local_sem)
        mine.start()
        first = [copy(0, me, sibling, src=x_ref)]
        first += [copy(1 + j, me, (*chip, c), src=x_ref) for j, chip in enumerate(chips)]
        for cp in first:
            cp.start()
        passed = [copy(4 + j, (*chip, c), sibling) for j, chip in enumerate(chips)]
        for j, chip in enumerate(chips):
            copy(1 + j, (*chip, c), me).wait_recv()
            passed[j].start()
        copy(0, sibling, me).wait_recv()
        for j, chip in enumerate(chips):
            copy(4 + j, (*chip, 1 - c), me).wait_recv()
        for cp in first + passed:
            cp.wait_send()
        mine.wait()
        total = gath_ref[pl.ds(0, rows_n), :]
        for d in range(1, N_DEV):
            total = total + gath_ref[pl.ds(d * rows_n, rows_n), :]
        sum_ref[...] = total

    vm = pl.BlockSpec(memory_space=pltpu.VMEM)
    return pl.pallas_call(
        body, name="allsum_rows", in_specs=[vm], out_specs=[vm, vm],
        out_shape=[jax.ShapeDtypeStruct((N_DEV * rows_n, D), f32), jax.ShapeDtypeStruct((rows_n, D), f32)],
        scratch_shapes=[pltpu.SemaphoreType.DMA((7,)), pltpu.SemaphoreType.DMA((7,)), pltpu.SemaphoreType.DMA],
    )(part)[1]


PAIR_ROWS = 16


def _pair_reduce(name, pieces):
    _, r, n = pieces.shape

    def body(p_ref, o_ref, land, send, recv):
        x, y, c = _position()

        def remote(j, half):
            return pltpu.make_async_remote_copy(
                src_ref=p_ref.at[2 * j + half], dst_ref=land.at[j], send_sem=send.at[j], recv_sem=recv.at[j],
                device_id=(x, y, 1 - c), device_id_type=MESH)

        sends = [remote(j, 1 - c) for j in range(N_CHIPS)]
        for cp in sends:
            cp.start()
        for j in range(N_CHIPS):
            remote(j, c).wait_recv()

            def add_rows(i, carry, j=j):
                rows = pl.ds(pl.multiple_of(i * PAIR_ROWS, PAIR_ROWS), PAIR_ROWS)
                o_ref[j, rows, :] = (p_ref[2 * j + c, rows, :].astype(f32) + land[j, rows, :].astype(f32)).astype(bf16)
                return carry

            lax.fori_loop(0, r // PAIR_ROWS, add_rows, 0)
        for cp in sends:
            cp.wait_send()

    vm = pl.BlockSpec(memory_space=pltpu.VMEM)
    return pl.pallas_call(
        body, name=name, in_specs=[vm], out_specs=vm,
        out_shape=jax.ShapeDtypeStruct((N_CHIPS, r, n), bf16),
        scratch_shapes=[pltpu.VMEM((N_CHIPS, r, n), bf16), pltpu.SemaphoreType.DMA((N_CHIPS,)),
                        pltpu.SemaphoreType.DMA((N_CHIPS,))],
        compiler_params=pltpu.CompilerParams(vmem_limit_bytes=VMEM_LIMIT),
    )(pieces)


def _chip_exchange(arrs):
    n = len(arrs)
    heights = [a.shape[1] for a in arrs]
    cuts = [-(-r // 32) * 16 for r in heights]

    def body(*refs):
        srcs, dsts, relays = refs[:n], refs[n:2 * n], refs[2 * n:3 * n]
        send, recv, local = refs[3 * n:]
        x, y, c = _position()
        me = 2 * x + y
        chips = [(1 - x, y), (x, 1 - y), (1 - x, 1 - y)]
        ids = [2 * px + py for px, py in chips]

        def rows(a, quarter):
            return pl.ds(cuts[a], heights[a] - cuts[a]) if quarter else pl.ds(0, cuts[a])

        def held(a, quarter):
            size = heights[a] - cuts[a] if quarter else cuts[a]
            return relays[a].at[quarter, pl.ds(0, size), :]

        def direct(a, nb, piece, landing):
            px, py = chips[nb]
            return pltpu.make_async_remote_copy(
                src_ref=srcs[a].at[piece], dst_ref=dsts[a].at[landing], send_sem=send.at[a * 2 + nb],
                recv_sem=recv.at[a * 2 + nb], device_id=(px, py, c), device_id_type=MESH)

        def first_hop(a, quarter):
            k = 2 * n + a * 2 + quarter
            px, py = chips[quarter]
            return pltpu.make_async_remote_copy(
                src_ref=srcs[a].at[ids[2], rows(a, quarter), :], dst_ref=held(a, quarter), send_sem=send.at[k],
                recv_sem=recv.at[k], device_id=(px, py, c), device_id_type=MESH)

        def second_hop(a, quarter, origin):
            k = 4 * n + a * 2 + quarter
            px, py = chips[1 - quarter]
            return pltpu.make_async_remote_copy(
                src_ref=held(a, quarter), dst_ref=dsts[a].at[origin, rows(a, quarter), :], send_sem=send.at[k],
                recv_sem=recv.at[k], device_id=(px, py, c), device_id_type=MESH)

        own = [pltpu.make_async_copy(srcs[a].at[me], dsts[a].at[me], local.at[a]) for a in range(n)]
        sent = [first_hop(a, q) for a in range(n) for q in range(2)]
        sent += [direct(a, nb, ids[nb], me) for a in range(n) for nb in range(2)]
        for cp in sent + own:
            cp.start()
        for a in range(n):
            for q in range(2):
                first_hop(a, q).wait_recv()
                sent.append(second_hop(a, q, ids[q]))
                sent[-1].start()
        for a in range(n):
            for nb in range(2):
                direct(a, nb, me, ids[nb]).wait_recv()
            for q in range(2):
                second_hop(a, q, ids[2]).wait_recv()
        for cp in sent:
            cp.wait_send()
        for cp in own:
            cp.wait()

    vm = pl.BlockSpec(memory_space=pltpu.VMEM)
    out = pl.pallas_call(
        body, name="chip_exchange", in_specs=[vm] * n, out_specs=[vm] * (2 * n),
        out_shape=[jax.ShapeDtypeStruct(a.shape, a.dtype) for a in arrs]
        + [jax.ShapeDtypeStruct((2, cut, a.shape[2]), a.dtype) for a, cut in zip(arrs, cuts)],
        scratch_shapes=[pltpu.SemaphoreType.DMA((6 * n,)), pltpu.SemaphoreType.DMA((6 * n,)),
                        pltpu.SemaphoreType.DMA((n,))],
        compiler_params=pltpu.CompilerParams(vmem_limit_bytes=VMEM_LIMIT),
    )(*arrs)
    return out[:n]


def _swap_halves(arrs):
    n = len(arrs)

    def body(*refs):
        srcs, dsts = refs[:n], refs[n:2 * n]
        send, recv, local = refs[2 * n:]
        x, y, c = _position()

        def remote(a, landing):
            return pltpu.make_async_remote_copy(
                src_ref=srcs[a], dst_ref=dsts[a].at[landing], send_sem=send.at[a], recv_sem=recv.at[a],
                device_id=(x, y, 1 - c), device_id_type=MESH)

        own = [pltpu.make_async_copy(srcs[a], dsts[a].at[c], local.at[a]) for a in range(n)]
        sends = [remote(a, c) for a in range(n)]
        for cp in sends + own:
            cp.start()
        for a in range(n):
            remote(a, 1 - c).wait_recv()
        for cp in sends:
            cp.wait_send()
        for cp in own:
            cp.wait()

    vm = pl.BlockSpec(memory_space=pltpu.VMEM)
    return pl.pallas_call(
        body, name="swap_halves", in_specs=[vm] * n, out_specs=[vm] * n,
        out_shape=[jax.ShapeDtypeStruct((2,) + a.shape, a.dtype) for a in arrs],
        scratch_shapes=[pltpu.SemaphoreType.DMA((n,)), pltpu.SemaphoreType.DMA((n,)), pltpu.SemaphoreType.DMA((n,))],
        compiler_params=pltpu.CompilerParams(vmem_limit_bytes=VMEM_LIMIT),
    )(*arrs)


def _row_block(r):
    return 128 if r % 128 == 0 else r


def _sum_slots(name, slots):
    s, r, n = slots.shape
    rb = _row_block(r)

    def body(s_ref, o_ref):
        total = s_ref[0].astype(f32)
        for d in range(1, s):
            total = total + s_ref[d].astype(f32)
        o_ref[...] = total

    return pl.pallas_call(
        body, name=name, grid=(r // rb,),
        in_specs=[pl.BlockSpec((s, rb, n), lambda i: (0, i, 0))],
        out_specs=pl.BlockSpec((rb, n), lambda i: (i, 0)),
        out_shape=jax.ShapeDtypeStruct((r, n), f32),
        compiler_params=_params(("parallel",), VMEM_LIMIT),
    )(slots)


def _adamw(name, w, g, m, v):
    r, n = w.shape
    if r % 128 == 0 or r * n <= 128 * 1024:
        rb, nb = _row_block(r), n
    else:
        rb, nb = r, LANES

    def body(w_ref, g_ref, m_ref, v_ref, d_ref, nm_ref, nv_ref):
        gv = g_ref[...]
        m2 = ADAM_B1 * m_ref[...] + (1.0 - ADAM_B1) * gv
        v2 = ADAM_B2 * v_ref[...] + (1.0 - ADAM_B2) * (gv * gv)
        m_hat = m2 / (1.0 - ADAM_B1 ** ADAM_STEP)
        v_hat = v2 / (1.0 - ADAM_B2 ** ADAM_STEP)
        d_ref[...] = (-ADAM_LR) * (m_hat / (jnp.sqrt(v_hat) + ADAM_EPS) + ADAM_WD * w_ref[...])
        nm_ref[...] = m2
        nv_ref[...] = v2

    spec = pl.BlockSpec((rb, nb), lambda i, j: (i, j))
    return pl.pallas_call(
        body, name=name, grid=(r // rb, n // nb), in_specs=[spec] * 4, out_specs=[spec] * 3,
        out_shape=[jax.ShapeDtypeStruct((r, n), f32)] * 3,
        compiler_params=_params(("parallel", "parallel"), VMEM_LIMIT),
    )(w, g, m, v)


def _local_step(x2, tgt2, seq, wt):
    nb = x2.shape[0] // seq
    h = _prenorm(x2, wt["pre_w"])
    qkv = _mm("in_qkv", h, wt["w_qkv"], wt["b_qkv"], bf16, 1024, 1024, w_is_nk=True)
    rest = _mm("in_rest", h, wt["w_rest"], wt["b_rest"], bf16, 1024, 1024, w_is_nk=True)
    f128 = _mm("in_f", h, wt["w_f"], wt["b_f"], f32, 1024, LANES, w_is_nk=True)
    c = _forget_prep(f128, seq)
    qa, ka = _attn_prep(qkv, c)
    o_att, pa, lse = _attn_fwd(qa, ka, qkv, rest, seq)
    ya = _mm("proj_a", pa, wt["w_a"], None, bf16, 1024, D)
    rnn_w = (wt["conv_w"], wt["conv_b"], wt["wa_d"], wt["wx_d"], wt["ba"], wt["bx"], wt["lam"])
    xc, a, hrec, pr = _rnn_fwd(rest, *rnn_w, seq)
    yr = _mm("proj_r", pr, wt["w_r"], None, bf16, 1024, D)
    do, dy, mrg, loss8, d_post = _out_proj_loss(rest, ya, yr, wt["w_o"], x2, tgt2, wt["post_w"])
    dya, dyr, dmga, dmgr = _out_bwd(do, rest, ya, yr, wt["w_o"])
    doa, dga, delta = _branch_bwd("branch_a_bwd", dya, rest, 0, o_att, wt["w_a"], bf16, head_sums=True)
    dhrec, dgr = _branch_bwd("branch_r_bwd", dyr, rest, 2, hrec, wt["w_r"], bf16)
    d_wo, _ = _tn_mm("dw_out", mrg, do, D)
    d_wa, _ = _tn_mm("dw_branch_a", pa, dya, D)
    d_wr, _ = _tn_mm("dw_branch_r", pr, dyr, D)
    dxr, d_wad, d_wxd, vec = _rnn_bwd(dhrec, a, hrec, xc, rest, *rnn_w, seq)
    dq, dk, dv, dc_pairs = _attn_bwd(qa, ka, qkv, doa, lse, delta, seq)
    dc = dc_pairs.reshape(-1, HEADS // ATT_GROUP, LANES)[:, :, :ATT_GROUP].reshape(-1, HEADS)
    df, db_f = _forget_bwd(_pad_cols(dc, LANES), f128, seq)
    pieces = [dq, dk, dv, dga, dxr, dgr, dmga, dmgr]
    gx, d_pre = _in_bwd(pieces, df, x2, dy, wt["w_qkv"], wt["w_rest"], wt["w_f"], wt["pre_w"])
    names = ["q", "k", "v", "ga", "xr", "gr", "mga", "mgr"]
    dws, dbs = [], []
    for nm, piece in zip(names, pieces):
        dw_p, db_p = _tn_mm("dw_in_" + nm, piece, h, D)
        dws.append(dw_p)
        dbs.append(db_p)
    dw_f, _ = _tn_mm("dw_in_f", df, h, D)
    zeros_w = jnp.zeros((IN_TOTAL - IN_USED, D), f32)
    d_w_in = jnp.concatenate(dws[:3] + [dw_f[:HEADS]] + dws[3:] + [zeros_w], axis=0)
    d_b_in = jnp.concatenate(dbs[:3] + [db_f[:, :HEADS]] + dbs[3:] + [zeros_w[:, :1].T], axis=1)
    return dict(loss=loss8[0, 0], grad_x=gx, pre_w=d_pre, w_in=d_w_in, b_in=d_b_in, conv_w=vec[4:8], conv_b=vec[3:4],
                wa_d=d_wad, ba=vec[0:1], wx_d=d_wxd, bx=vec[1:2], lam=vec[2:3], w_a=d_wa, w_r=d_wr, w_o=d_wo,
                post_w=d_post)


def _block_diag(w):
    g, bw, _ = w.shape
    eye = jnp.eye(g, dtype=w.dtype)
    return (w[:, :, None, :] * eye[:, None, :, None]).reshape(g * bw, g * bw)


def _gate_blocks(diag):
    half = diag.shape[1] // 2
    return jnp.stack([diag[:, :half, :half], diag[:, half:, half:]], axis=1).reshape(-1, half, half)


def _pad_cols(a, n):
    return jnp.pad(a, ((0, 0), (0, n - a.shape[1])))


def _pad_rows(a, n):
    return jnp.pad(a, ((0, n - a.shape[0]), (0, 0)))


def kernel(x, pre_norm_w, w_in, b_in, conv_w, conv_b, rg_wa, rg_ba, rg_wx, rg_bx, rg_lambda, w_branch_a, w_branch_r, w_out, post_norm_w, loss_target, m_pre_norm_w, m_w_in, m_b_in, m_conv_w, m_conv_b, m_rg_wa, m_rg_ba, m_rg_wx, m_rg_bx, m_rg_lambda, m_w_branch_a, m_w_branch_r, m_w_out, m_post_norm_w, v_pre_norm_w, v_w_in, v_b_in, v_conv_w, v_conv_b, v_rg_wa, v_rg_ba, v_rg_wx, v_rg_bx, v_rg_lambda, v_w_branch_a, v_w_branch_r, v_w_out, v_post_norm_w):
    nb, seq, _ = x.shape
    chip = 2 * lax.axis_index("x") + lax.axis_index("y")
    n_groups = rg_wa.shape[1]

    w_in_t = jnp.transpose(w_in[0])
    shard_cols = w_in_t.shape[0]
    padded = -(-shard_cols // 32) * 32
    g_in, g_a, g_r, g_o, g_cw = _gather_shards(
        [_pad_rows(w_in_t.astype(bf16), padded), w_branch_a[0].astype(bf16), w_branch_r[0].astype(bf16),
         w_out[0].astype(bf16)], conv_w[0])
    w_full = jnp.concatenate([g_in[j, :shard_cols] for j in range(N_CHIPS)], axis=0)
    q_end, f_end = 3 * D, 3 * D + HEADS
    wt = dict(
        pre_w=pre_norm_w, post_w=post_norm_w,
        w_qkv=w_full[:q_end], b_qkv=b_in[:, :q_end],
        w_f=_pad_rows(w_full[q_end:f_end], LANES), b_f=_pad_cols(b_in[:, q_end:f_end], LANES),
        w_rest=w_full[f_end:IN_USED], b_rest=b_in[:, f_end:IN_USED],
        w_a=g_a.reshape(D, D), w_r=g_r.reshape(D, D), w_o=g_o.reshape(D, D),
        conv_w=jnp.transpose(g_cw, (1, 0, 2)).reshape(4, D), conv_b=conv_b,
        wa_d=_block_diag(rg_wa[0]).astype(bf16), wx_d=_block_diag(rg_wx[0]).astype(bf16),
        ba=rg_ba, bx=rg_bx, lam=rg_lambda)

    part = _local_step(x.reshape(nb * seq, D), loss_target.reshape(nb * seq, D), seq, wt)
    loss = lax.psum(part["loss"], ("x", "y", "c"))
    grad_x = part["grad_x"].reshape(nb, seq, D)

    small = jnp.concatenate([
        part["pre_w"], _pad_cols(part["b_in"], 10 * D).reshape(10, D), part["conv_b"],
        _gate_blocks(part["wa_d"]).reshape(-1, D), part["ba"],
        _gate_blocks(part["wx_d"]).reshape(-1, D), part["bx"], part["lam"], part["post_w"],
        part["conv_w"]], axis=0)
    n_small = small.shape[0]
    n_rep = n_small - 4
    tot = _allsum_rows(_pad_rows(small, -(-n_small // 8) * 8))
    g_rep = tot[:n_rep]
    g_conv_w = lax.dynamic_slice_in_dim(tot[n_rep:n_small], chip * (D // N_CHIPS), D // N_CHIPS, axis=1)

    def unpack(p):
        o = [0]

        def take(k):
            o[0] += k
            return p[o[0] - k:o[0]]

        pre = take(1)
        b = take(10).reshape(1, 10 * D)[:, :IN_TOTAL]
        cb = take(1)
        wa = take(64).reshape(rg_wa.shape)
        ba = take(1)
        wx = take(64).reshape(rg_wx.shape)
        bx = take(1)
        lam = take(1)
        post = take(1)
        return dict(pre_norm_w=pre, b_in=b, conv_b=cb, rg_wa=wa, rg_ba=ba, rg_wx=wx, rg_bx=bx, rg_lambda=lam,
                    post_norm_w=post)

    grads = unpack(g_rep)
    replicated = dict(
        pre_norm_w=(pre_norm_w, m_pre_norm_w, v_pre_norm_w), b_in=(b_in, m_b_in, v_b_in),
        conv_b=(conv_b, m_conv_b, v_conv_b), rg_wa=(rg_wa, m_rg_wa, v_rg_wa), rg_ba=(rg_ba, m_rg_ba, v_rg_ba),
        rg_wx=(rg_wx, m_rg_wx, v_rg_wx), rg_bx=(rg_bx, m_rg_bx, v_rg_bx),
        rg_lambda=(rg_lambda, m_rg_lambda, v_rg_lambda), post_norm_w=(post_norm_w, m_post_norm_w, v_post_norm_w))
    deltas, new_m, new_v = {}, {}, {}
    for name, (w, m, v) in replicated.items():
        as2d = lambda a: a.reshape(-1, D) if a.ndim > 2 else a
        upd = _adamw("adamw_" + name, as2d(w), as2d(grads[name]), as2d(m), as2d(v))
        deltas[name], new_m[name], new_v[name] = [a.reshape(w.shape) for a in upd]

    p_in = jnp.pad(part["w_in"].reshape(N_CHIPS, shard_cols, D), ((0, 0), (0, padded - shard_cols), (0, 0)))
    p_in = p_in.reshape(N_DEV, padded // 2, D)
    p_aro = jnp.concatenate([part[k].reshape(N_DEV, D // N_DEV, D) for k in ("w_a", "w_r", "w_o")], axis=1)
    s_in, s_aro = _chip_exchange([_pair_reduce("pair_w_in", p_in.astype(bf16)),
                                  _pair_reduce("pair_w_aro", p_aro.astype(bf16))])
    f_in, f_aro = _swap_halves([_sum_slots("sum_w_in", s_in), _sum_slots("sum_w_aro", s_aro)])
    g_w_in_t = f_in.reshape(padded, D)[:shard_cols]
    rows = D // N_DEV
    g_aro = [f_aro[:, i * rows:(i + 1) * rows, :].reshape(2 * rows, D) for i in range(3)]

    w_in_upd = _adamw("adamw_w_in", w_in_t, g_w_in_t, jnp.transpose(m_w_in[0]), jnp.transpose(v_w_in[0]))
    g_w_in, d_w_in, nm_w_in, nv_w_in = [jnp.transpose(a) for a in (g_w_in_t, *w_in_upd)]
    upd_a = _adamw("adamw_w_branch_a", w_branch_a[0], g_aro[0], m_w_branch_a[0], v_w_branch_a[0])
    upd_r = _adamw("adamw_w_branch_r", w_branch_r[0], g_aro[1], m_w_branch_r[0], v_w_branch_r[0])
    upd_o = _adamw("adamw_w_out", w_out[0], g_aro[2], m_w_out[0], v_w_out[0])
    d_aro, nm_aro, nv_aro = zip(upd_a, upd_r, upd_o)
    d_cw, nm_cw, nv_cw = _adamw("adamw_conv_w", conv_w[0], g_conv_w, m_conv_w[0], v_conv_w[0])

    def sharded(t_in, t_aro, t_cw):
        return dict(w_in=t_in[None], conv_w=t_cw[None], w_branch_a=t_aro[0][None], w_branch_r=t_aro[1][None],
                    w_out=t_aro[2][None])

    order = ["pre_norm_w", "w_in", "b_in", "conv_w", "conv_b", "rg_wa", "rg_ba", "rg_wx", "rg_bx", "rg_lambda",
             "w_branch_a", "w_branch_r", "w_out", "post_norm_w"]
    outs = [loss, grad_x]
    for rep, shd in ((grads, sharded(g_w_in, g_aro, g_conv_w)), (deltas, sharded(d_w_in, d_aro, d_cw)),
                     (new_m, sharded(nm_w_in, nm_aro, nm_cw)), (new_v, sharded(nv_w_in, nv_aro, nv_cw))):
        both = {**rep, **shd}
        outs.extend(both[k] for k in order)
    return tuple(outs)
```

```python
import jax
import jax.numpy as jnp
from jax import lax
from jax.experimental import pallas as pl
from jax.experimental.pallas import tpu as pltpu

f32 = jnp.float32
bf16 = jnp.bfloat16

D = 1024
HEADS = 16
HEAD_PAIRS = 8
LANES = 128
NORM_EPS = 1e-6
MASK_VALUE = -1e30
RG_C = 8.0
QK_SCALE = 0.125
TQ = 256
ATT_GROUP = 8
ATT_GROUP_FWD = 16
TL = 256
TM = 512
PREV_ROWS = 16
IN_USED = 8 * D + HEADS
IN_TOTAL = 9 * D + HEADS
N_CHIPS = 4
N_DEV = 8
ADAM_LR, ADAM_B1, ADAM_B2, ADAM_EPS, ADAM_WD, ADAM_STEP = 0.001, 0.9, 0.999, 1e-08, 0.01, 10
VMEM_LIMIT = 56 * 1024 * 1024
MESH = pl.DeviceIdType.MESH


def _dot(a, b):
    return jnp.dot(a, b, preferred_element_type=f32)


def _dot_nt(a, b):
    return lax.dot_general(a, b, (((1,), (1,)), ((), ())), preferred_element_type=f32)


def _dot_tn(a, b):
    return lax.dot_general(a, b, (((0,), (0,)), ((), ())), preferred_element_type=f32)


def _sig(x):
    return 0.5 * jnp.tanh(0.5 * x) + 0.5


def _softplus(x):
    return jnp.maximum(x, 0.0) + jnp.log(1.0 + jnp.exp(-jnp.abs(x)))


def _params(sem, vmem=None):
    return pltpu.CompilerParams(dimension_semantics=sem, vmem_limit_bytes=vmem)


def _tile(tm, width, cb=0):
    return pl.BlockSpec((tm, width), lambda i, cb=cb: (i, cb))


def _whole(shape):
    nd = len(shape)
    return pl.BlockSpec(shape, lambda *_: (0,) * nd)


def _prenorm(x, w_pre):
    t = x.shape[0]

    def body(x_ref, w_ref, h_ref):
        xv = x_ref[...]
        r = lax.rsqrt(jnp.mean(xv * xv, axis=-1, keepdims=True) + NORM_EPS)
        h_ref[...] = (xv * r * w_ref[...]).astype(bf16)

    return pl.pallas_call(
        body, name="prenorm", grid=(t // TM,),
        in_specs=[_tile(TM, D), _whole((1, D))], out_specs=_tile(TM, D),
        out_shape=jax.ShapeDtypeStruct((t, D), bf16),
        compiler_params=_params(("parallel",)),
    )(x, w_pre)


def _mm(name, a, w, bias, out_dtype, tm, tn, w_is_nk=False):
    t, k = a.shape
    tm = min(tm, t)
    n = w.shape[0] if w_is_nk else w.shape[1]

    def body(a_ref, w_ref, *refs):
        acc = _dot_nt(a_ref[...], w_ref[...]) if w_is_nk else _dot(a_ref[...], w_ref[...])
        if bias is not None:
            acc = acc + refs[0][...]
        refs[-1][...] = acc.astype(out_dtype)

    in_specs = [pl.BlockSpec((tm, k), lambda i, j: (i, 0)),
                pl.BlockSpec((tn, k), lambda i, j: (j, 0)) if w_is_nk else pl.BlockSpec((k, tn), lambda i, j: (0, j))]
    args = [a, w]
    if bias is not None:
        in_specs.append(pl.BlockSpec((1, tn), lambda i, j: (0, j)))
        args.append(bias)
    return pl.pallas_call(
        body, name=name, grid=(t // tm, n // tn), in_specs=in_specs,
        out_specs=pl.BlockSpec((tm, tn), lambda i, j: (i, j)), out_shape=jax.ShapeDtypeStruct((t, n), out_dtype),
        compiler_params=_params(("parallel", "parallel"), VMEM_LIMIT),
    )(*args)


def _forget_prep(f128, seq):
    t = f128.shape[0]
    nb = seq // LANES

    def body(f_ref, c_ref):
        r = lax.broadcasted_iota(jnp.int32, (LANES, LANES), 0)
        cidx = lax.broadcasted_iota(jnp.int32, (LANES, LANES), 1)
        tri = (r >= cidx).astype(f32)
        carry = jnp.zeros((1, LANES), f32)
        for blk in range(nb):
            fv = f_ref[pl.ds(blk * LANES, LANES), :]
            lf = -_softplus(-fv)
            c_ref[pl.ds(blk * LANES, LANES), :] = (
                jnp.dot(tri, lf, preferred_element_type=f32, precision=lax.Precision.HIGHEST) + carry)
            carry = carry + jnp.sum(lf, axis=0, keepdims=True)

    return pl.pallas_call(
        body, name="forget_prep", grid=(t // seq,),
        in_specs=[pl.BlockSpec((seq, LANES), lambda b: (b, 0))],
        out_specs=pl.BlockSpec((seq, LANES), lambda b: (b, 0)),
        out_shape=jax.ShapeDtypeStruct((t, LANES), f32),
        compiler_params=_params(("parallel",)),
    )(f128)


def _split3(cv):
    hi = cv.astype(bf16)
    r1 = cv - hi.astype(f32)
    mid = r1.astype(bf16)
    lo = (r1 - mid.astype(f32)).astype(bf16)
    return hi, mid, lo


def _attn_prep(qkv, c):
    t = qkv.shape[0]

    def body(q_ref, k_ref, c_ref, qa_ref, ka_ref):
        lane = lax.broadcasted_iota(jnp.int32, (1, LANES), 1)
        cv = c_ref[...]
        one = jnp.ones((), bf16)
        zero = jnp.zeros((), bf16)
        q_ones = jnp.where((lane >= 67) & (lane < 70), one, zero)
        k_ones = jnp.where((lane >= 64) & (lane < 67), one, zero)
        for head in range(HEADS):
            pair = pl.ds((head // 2) * LANES, LANES)
            ch = jnp.sum(jnp.where(lane == head, cv, 0.0), axis=1, keepdims=True)
            hi, mid, lo = _split3(ch)
            q2, k2 = q_ref[:, pair], k_ref[:, pair]
            if head % 2 == 1:
                q2, k2 = pltpu.roll(q2, 64, 1), pltpu.roll(k2, 64, 1)
            qa = jnp.where(lane < 64, q2 * jnp.asarray(QK_SCALE, bf16),
                           jnp.where(lane == 64, hi, jnp.where(lane == 65, mid, jnp.where(lane == 66, lo, q_ones))))
            ka = jnp.where(lane < 64, k2,
                           jnp.where(lane == 67, -hi, jnp.where(lane == 68, -mid, jnp.where(lane == 69, -lo, k_ones))))
            qa_ref[:, pl.ds(head * LANES, LANES)] = qa
            ka_ref[:, pl.ds(head * LANES, LANES)] = ka

    tm = min(TM, t)
    out = pl.BlockSpec((tm, 2 * D), lambda i: (i, 0))
    return pl.pallas_call(
        body, name="attn_prep", grid=(t // tm,),
        in_specs=[_tile(tm, D, 0), _tile(tm, D, 1), _tile(tm, LANES)],
        out_specs=[out, out],
        out_shape=[jax.ShapeDtypeStruct((t, 2 * D), bf16)] * 2,
        compiler_params=_params(("parallel",)),
    )(qkv, qkv, c)


def _attn_fwd(qa, ka, qkv, rest, seq):
    t = qkv.shape[0]
    nb, nq = t // seq, seq // TQ

    hg = ATT_GROUP_FWD
    ng = HEADS // hg

    def body(q_ref, k_ref, v_ref, ga_ref, o_ref, pa_ref, lse_ref, acc_scr):
        qi, gi = pl.program_id(1), pl.program_id(2)
        krow = lax.broadcasted_iota(jnp.int32, (TQ, TQ), 0)
        qcol = lax.broadcasted_iota(jnp.int32, (TQ, TQ), 1)
        acc_scr[...] = jnp.zeros_like(acc_scr)

        def kv_step(kt, carry, masked):
            ks = pl.multiple_of(kt * TQ, TQ)
            sts = [_dot_nt(k_ref[pl.ds(ks, TQ), pl.ds(g * LANES, LANES)], q_ref[:, pl.ds(g * LANES, LANES)])
                   for g in range(hg)]
            if masked:
                sts = [jnp.where(krow <= qcol, st, MASK_VALUE) for st in sts]
            m_new = [jnp.maximum(carry[g][0], jnp.max(sts[g], axis=0, keepdims=True)) for g in range(hg)]
            ps = [jnp.exp(sts[g] - m_new[g]) for g in range(hg)]
            alphas = [jnp.exp(carry[g][0] - m_new[g]) for g in range(hg)]
            phi = [ps[g].astype(bf16) for g in range(hg)]
            plo = [(ps[g] - phi[g].astype(f32)).astype(bf16) for g in range(hg)]
            vs = [v_ref[pl.ds(ks, TQ), pl.ds(j * LANES, LANES)] for j in range(hg // 2)]
            pvs = [_dot_tn(vs[g // 2], phi[g]) + _dot_tn(vs[g // 2], plo[g]) for g in range(hg)]
            olds = [acc_scr[g] for g in range(hg)]
            for g in range(hg):
                acc_scr[g] = alphas[g] * olds[g] + pvs[g]
            return tuple((m_new[g], alphas[g] * carry[g][1] + jnp.sum(ps[g], axis=0, keepdims=True))
                         for g in range(hg))

        init = tuple((jnp.full((1, TQ), MASK_VALUE, f32), jnp.zeros((1, TQ), f32)) for _ in range(hg))
        carry = lax.fori_loop(0, qi, lambda kt, cr: kv_step(kt, cr, False), init)
        stats = kv_step(qi, carry, True)
        drow = lax.broadcasted_iota(jnp.int32, (LANES, TQ), 0)
        for g in range(hg):
            m, l = stats[g]
            lse_ref[0, pl.ds(hg * gi + g, 1), :] = m + jnp.log(l)
        for j in range(hg // 2):
            o2 = jnp.where(drow < 64, acc_scr[2 * j] / stats[2 * j][1], acc_scr[2 * j + 1] / stats[2 * j + 1][1]).T
            o_ref[:, pl.ds(j * LANES, LANES)] = o2
            ga = ga_ref[:, pl.ds(j * LANES, LANES)].astype(f32)
            pa_ref[:, pl.ds(j * LANES, LANES)] = (o2 * (ga * _sig(ga))).astype(bf16)

    vw = hg * 64
    tile = pl.BlockSpec((TQ, vw), lambda b, qi, gi: (b * nq + qi, gi))
    return pl.pallas_call(
        body, name="attn_fwd", grid=(nb, nq, ng),
        in_specs=[pl.BlockSpec((TQ, hg * LANES), lambda b, qi, gi: (b * nq + qi, gi)),
                  pl.BlockSpec((seq, hg * LANES), lambda b, qi, gi: (b, gi)),
                  pl.BlockSpec((seq, vw), lambda b, qi, gi: (b, 2 * ng + gi)), tile],
        out_specs=[tile, tile, pl.BlockSpec((1, HEADS, TQ), lambda b, qi, gi: (b * nq + qi, 0, 0))],
        out_shape=[jax.ShapeDtypeStruct((t, D), f32), jax.ShapeDtypeStruct((t, D), bf16),
                   jax.ShapeDtypeStruct((t // TQ, HEADS, TQ), f32)],
        scratch_shapes=[pltpu.VMEM((hg, LANES, TQ), f32)],
        compiler_params=_params(("parallel", "parallel", "arbitrary"), VMEM_LIMIT),
    )(qa, ka, qkv, rest)


def _shifted_rows(x, top8, prev8, shift, row, row8):
    body = pltpu.roll(x, shift, 0)
    head = jnp.where(row8 < shift, pltpu.roll(prev8, shift, 0), pltpu.roll(top8, shift, 0))
    return body, head


def _rnn_gates(xc, wa_ref, wx_ref, ba_ref, bx_ref, lam_ref):
    xcb = xc.astype(bf16)
    r = _sig(_dot(xcb, wa_ref[...]) + ba_ref[...])
    i = _sig(_dot(xcb, wx_ref[...]) + bx_ref[...])
    sp = _softplus(-lam_ref[...])
    log_a = (-RG_C) * r * sp
    th = jnp.tanh(log_a)
    w1 = (-2.0) * th / (1.0 - th)
    sq = jnp.sqrt(jnp.maximum(w1, 0.0))
    return r, i, sp, log_a, w1, sq


def _conv_tile(x_ref, xprev_ref, has_prev, cw_ref, cb_ref, xc_ref):
    row = lax.broadcasted_iota(jnp.int32, (TL, D), 0)
    row8 = lax.broadcasted_iota(jnp.int32, (8, D), 0)
    x = x_ref[...].astype(f32)
    top8 = x[:8]
    prev8 = jnp.where(has_prev, xprev_ref[...].astype(f32)[PREV_ROWS - 8:], 0.0)
    xc = cb_ref[...] + cw_ref[pl.ds(3, 1), :] * x
    xc8 = cb_ref[...] + cw_ref[pl.ds(3, 1), :] * top8
    for sh in range(1, 4):
        w = cw_ref[pl.ds(3 - sh, 1), :]
        xs, xs8 = _shifted_rows(x, top8, prev8, sh, row, row8)
        xc = xc + w * xs
        xc8 = xc8 + w * xs8
    xc_ref[...] = xc
    xc_ref[pl.ds(0, 8), :] = xc8


def _rnn_fwd(rest, conv_w, conv_b, wa_d, wx_d, ba, bx, lam, seq):
    t = rest.shape[0]
    nb, nt = t // seq, seq // TL

    def body(x_ref, xprev_ref, gr_ref, cw_ref, cb_ref, wa_ref, wx_ref, ba_ref, bx_ref, lam_ref,
             xc_ref, a_ref, h_ref, pr_ref, xc_scr, u_scr, h_scr, carry):
        tt = pl.program_id(1)
        _conv_tile(x_ref, xprev_ref, tt > 0, cw_ref, cb_ref, xc_scr)
        xc = xc_scr[...]
        xc_ref[...] = xc.astype(bf16)
        r, i, sp, log_a, w1, sq = _rnn_gates(xc, wa_ref, wx_ref, ba_ref, bx_ref, lam_ref)
        a_ref[...] = jnp.exp(log_a)
        u_scr[...] = sq * (i * xc)

        @pl.when(tt == 0)
        def _():
            carry[...] = jnp.zeros_like(carry)

        def step(s, h):
            h = a_ref[pl.ds(s, 1), :] * h + u_scr[pl.ds(s, 1), :]
            h_scr[pl.ds(s, 1), :] = h
            return h

        carry[...] = lax.fori_loop(0, TL, step, carry[...], unroll=8)
        gr = gr_ref[...].astype(f32)
        h = h_scr[...]
        h_ref[...] = h.astype(bf16)
        pr_ref[...] = (h * (gr * _sig(gr))).astype(bf16)

    tile = lambda cb: pl.BlockSpec((TL, D), lambda b, tt, cb=cb: (b * nt + tt, cb))
    prev = lambda cb: pl.BlockSpec(
        (PREV_ROWS, D), lambda b, tt, cb=cb: (jnp.maximum((b * nt + tt) * (TL // PREV_ROWS) - 1, 0), cb))
    vec = _whole((1, D))
    return pl.pallas_call(
        body, name="rnn_fwd", grid=(nb, nt),
        in_specs=[tile(1), prev(1), tile(2), _whole((4, D)), vec, _whole((D, D)), _whole((D, D)), vec, vec, vec],
        out_specs=[tile(0)] * 4,
        out_shape=[jax.ShapeDtypeStruct((t, D), dt) for dt in (bf16, f32, bf16, bf16)],
        scratch_shapes=[pltpu.VMEM((TL, D), f32)] * 3 + [pltpu.VMEM((1, D), f32)],
        compiler_params=_params(("parallel", "arbitrary"), VMEM_LIMIT),
    )(rest, rest, rest, conv_w, conv_b, wa_d, wx_d, ba, bx, lam)


def _merge(mga, mgr, ya, yr):
    return (_sig(mga.astype(f32)) * ya.astype(f32) + _sig(mgr.astype(f32)) * yr.astype(f32)).astype(bf16)


def _out_proj_loss(rest, ya, yr, w_out, x, tgt, w_post):
    t = x.shape[0]

    def body(mga_ref, mgr_ref, ya_ref, yr_ref, wo_ref, x_ref, t_ref, w_ref, do_ref, dy_ref, mrg_ref, loss_ref, dwp_ref):
        @pl.when(pl.program_id(0) == 0)
        def _():
            loss_ref[...] = jnp.zeros_like(loss_ref)
            dwp_ref[...] = jnp.zeros_like(dwp_ref)

        mrg = _merge(mga_ref[...], mgr_ref[...], ya_ref[...], yr_ref[...])
        mrg_ref[...] = mrg
        ov = _dot(mrg, wo_ref[...])
        w = w_ref[...]
        r2 = lax.rsqrt(jnp.mean(ov * ov, axis=-1, keepdims=True) + NORM_EPS)
        oh = ov * r2
        e = x_ref[...] + oh * w - t_ref[...]
        loss_ref[...] += 0.5 * jnp.sum(jnp.mean(e * e, axis=-1, keepdims=True))
        dy = e * (1.0 / D)
        dy_ref[...] = dy
        dwp_ref[...] += jnp.sum(dy * oh, axis=0, keepdims=True)
        doh = dy * w
        do_ref[...] = (r2 * (doh - oh * jnp.mean(doh * oh, axis=-1, keepdims=True))).astype(bf16)

    return pl.pallas_call(
        body, name="out_proj_loss", grid=(t // TM,),
        in_specs=[_tile(TM, D, 3), _tile(TM, D, 4), _tile(TM, D), _tile(TM, D), _whole((D, D)), _tile(TM, D),
                  _tile(TM, D), _whole((1, D))],
        out_specs=[_tile(TM, D), _tile(TM, D), _tile(TM, D), _whole((8, LANES)), _whole((1, D))],
        out_shape=[jax.ShapeDtypeStruct((t, D), bf16), jax.ShapeDtypeStruct((t, D), f32),
                   jax.ShapeDtypeStruct((t, D), bf16), jax.ShapeDtypeStruct((8, LANES), f32),
                   jax.ShapeDtypeStruct((1, D), f32)],
        compiler_params=_params(("arbitrary",), VMEM_LIMIT),
    )(rest, rest, ya, yr, w_out, x, tgt, w_post)


def _out_bwd(do, rest, ya, yr, w_out):
    t = do.shape[0]

    def body(do_ref, mga_ref, mgr_ref, ya_ref, yr_ref, w_ref, dya_ref, dyr_ref, dmga_ref, dmgr_ref):
        sa, sr = _sig(mga_ref[...].astype(f32)), _sig(mgr_ref[...].astype(f32))
        ya, yr = ya_ref[...].astype(f32), yr_ref[...].astype(f32)
        dm = _dot_nt(do_ref[...], w_ref[...])
        dya_ref[...] = (dm * sa).astype(bf16)
        dyr_ref[...] = (dm * sr).astype(bf16)
        dmga_ref[...] = (dm * ya * sa * (1.0 - sa)).astype(bf16)
        dmgr_ref[...] = (dm * yr * sr * (1.0 - sr)).astype(bf16)

    return pl.pallas_call(
        body, name="out_bwd", grid=(t // TM,),
        in_specs=[_tile(TM, D), _tile(TM, D, 3), _tile(TM, D, 4), _tile(TM, D), _tile(TM, D), _whole((D, D))],
        out_specs=[_tile(TM, D)] * 4,
        out_shape=[jax.ShapeDtypeStruct((t, D), bf16)] * 4,
        compiler_params=_params(("parallel",), VMEM_LIMIT),
    )(do, rest, rest, ya, yr, w_out)


def _branch_bwd(name, dyb, rest, gate_cb, act, w, act_grad_dtype, head_sums=False):
    t = dyb.shape[0]

    def body(dy_ref, g_ref, act_ref, w_ref, dact_ref, dg_ref, *delta_ref):
        dp = _dot_nt(dy_ref[...], w_ref[...])
        g = g_ref[...].astype(f32)
        sg = _sig(g)
        act = act_ref[...].astype(f32)
        dact = (dp * (g * sg)).astype(act_grad_dtype)
        dact_ref[...] = dact
        dg_ref[...] = (dp * act * (sg * (1.0 + g * (1.0 - sg)))).astype(bf16)
        if head_sums:
            ch = lax.broadcasted_iota(jnp.int32, (D, LANES), 0)
            hd = lax.broadcasted_iota(jnp.int32, (D, LANES), 1)
            pick = (ch // 64 == hd).astype(bf16)
            per_head = sum(_dot(piece, pick) for piece in _split3(dact.astype(f32) * act))
            for s in range(TM // TQ):
                delta_ref[0][s] = per_head[s * TQ:(s + 1) * TQ].T[:HEADS, :]

    out_specs = [_tile(TM, D), _tile(TM, D)]
    out_shape = [jax.ShapeDtypeStruct((t, D), act_grad_dtype), jax.ShapeDtypeStruct((t, D), bf16)]
    if head_sums:
        out_specs.append(pl.BlockSpec((TM // TQ, HEADS, TQ), lambda i: (i, 0, 0)))
        out_shape.append(jax.ShapeDtypeStruct((t // TQ, HEADS, TQ), f32))
    return pl.pallas_call(
        body, name=name, grid=(t // TM,),
        in_specs=[_tile(TM, D), _tile(TM, D, gate_cb), _tile(TM, D), _whole((D, D))],
        out_specs=out_specs, out_shape=out_shape,
        compiler_params=_params(("parallel",), VMEM_LIMIT),
    )(dyb, rest, act, w)


def _rnn_bwd(dh, a, h, xc, rest, conv_w, conv_b, wa_d, wx_d, ba, bx, lam, seq):
    t = dh.shape[0]
    nb, nt = t // seq, seq // TL
    diag = (D // LANES, LANES, LANES)

    def body(dh_ref, a_ref, h_ref, hprev_ref, xc_ref, x_ref, xprev_ref, cw_ref, cb_ref, wa_ref, wx_ref,
             ba_ref, bx_ref, lam_ref, dxr_ref, dwa_ref, dwx_ref, vec_ref, g_scr, dxc_scr, dxr_scr, qcarry, dxc_next):
        b, tt = pl.program_id(0), pl.program_id(1)
        rt = nt - 1 - tt

        @pl.when((b == 0) & (tt == 0))
        def _():
            dwa_ref[...] = jnp.zeros_like(dwa_ref)
            dwx_ref[...] = jnp.zeros_like(dwx_ref)
            vec_ref[...] = jnp.zeros_like(vec_ref)

        @pl.when(tt == 0)
        def _():
            qcarry[...] = jnp.zeros_like(qcarry)
            dxc_next[...] = jnp.zeros_like(dxc_next)

        g_scr[...] = dh_ref[...].astype(f32)

        def step(k, q):
            s = TL - 1 - k
            g = g_scr[pl.ds(s, 1), :] + q
            g_scr[pl.ds(s, 1), :] = g
            return a_ref[pl.ds(s, 1), :] * g

        qcarry[...] = lax.fori_loop(0, TL, step, qcarry[...], unroll=8)

        row = lax.broadcasted_iota(jnp.int32, (TL, D), 0)
        row8 = lax.broadcasted_iota(jnp.int32, (8, D), 0)
        g = g_scr[...]
        av = a_ref[...]
        xc = xc_ref[...].astype(f32)
        hlast = jnp.where(rt > 0, hprev_ref[...].astype(f32)[PREV_ROWS - 1:], 0.0)
        hp = jnp.where(row == 0, hlast, pltpu.roll(h_ref[...].astype(f32), 1, 0))
        r, i, sp, log_a, w1, sq = _rnn_gates(xc, wa_ref, wx_ref, ba_ref, bx_ref, lam_ref)
        dix = g * sq
        di = dix * xc
        dxc = dix * i
        dsq = g * (i * xc)
        dlog_a = g * hp * av - dsq * jnp.where(sq > 0.0, (1.0 - w1) / sq, 0.0)
        dpr = (dlog_a * ((-RG_C) * sp)) * r * (1.0 - r)
        dpi = di * i * (1.0 - i)
        dprb, dpib, xcb = dpr.astype(bf16), dpi.astype(bf16), xc.astype(bf16)
        dxc = dxc + _dot_nt(dprb, wa_ref[...]) + _dot_nt(dpib, wx_ref[...])
        for j in range(D // LANES):
            cols = slice(j * LANES, (j + 1) * LANES)
            dwa_ref[j] += _dot_tn(xcb[:, cols], dprb[:, cols])
            dwx_ref[j] += _dot_tn(xcb[:, cols], dpib[:, cols])
        vec_ref[pl.ds(0, 1), :] += jnp.sum(dpr, axis=0, keepdims=True)
        vec_ref[pl.ds(1, 1), :] += jnp.sum(dpi, axis=0, keepdims=True)
        dsp = jnp.sum(dlog_a * ((-RG_C) * r), axis=0, keepdims=True)
        vec_ref[pl.ds(2, 1), :] += dsp * (-_sig(-lam_ref[...]))
        vec_ref[pl.ds(3, 1), :] += jnp.sum(dxc, axis=0, keepdims=True)

        dxc_scr[...] = dxc
        bot8 = dxc_scr[pl.ds(TL - 8, 8), :]
        nxt8 = dxc_next[...]
        dxr = cw_ref[pl.ds(3, 1), :] * dxc
        dxr8 = cw_ref[pl.ds(3, 1), :] * bot8
        for sh in range(1, 4):
            w = cw_ref[pl.ds(3 - sh, 1), :]
            dxr = dxr + w * pltpu.roll(dxc, TL - sh, 0)
            dxr8 = dxr8 + w * jnp.where(row8 < 8 - sh, pltpu.roll(bot8, 8 - sh, 0), pltpu.roll(nxt8, 8 - sh, 0))
        dxr_scr[...] = dxr
        dxr_scr[pl.ds(TL - 8, 8), :] = dxr8
        dxr_ref[...] = dxr_scr[...].astype(bf16)
        dxc_next[...] = dxc_scr[pl.ds(0, 8), :]

        x = x_ref[...].astype(f32)
        prev8 = jnp.where(rt > 0, xprev_ref[...].astype(f32)[PREV_ROWS - 8:], 0.0)
        dxc_top8 = dxc_scr[pl.ds(0, 8), :]
        vec_ref[pl.ds(7, 1), :] += jnp.sum(dxc * x, axis=0, keepdims=True)
        for sh in range(1, 4):
            inside = jnp.sum(dxc * jnp.where(row >= sh, pltpu.roll(x, sh, 0), 0.0), axis=0, keepdims=True)
            above = jnp.sum(dxc_top8 * jnp.where(row8 < sh, pltpu.roll(prev8, sh, 0), 0.0), axis=0, keepdims=True)
            vec_ref[pl.ds(7 - sh, 1), :] += inside + above

    tile = lambda cb: pl.BlockSpec((TL, D), lambda b, tt, cb=cb: (b * nt + nt - 1 - tt, cb))
    prev = lambda cb: pl.BlockSpec(
        (PREV_ROWS, D), lambda b, tt, cb=cb: (jnp.maximum((b * nt + nt - 1 - tt) * (TL // PREV_ROWS) - 1, 0), cb))
    vec = _whole((1, D))
    return pl.pallas_call(
        body, name="rnn_bwd", grid=(nb, nt),
        in_specs=[tile(0), tile(0), tile(0), prev(0), tile(0), tile(1), prev(1),
                  _whole((4, D)), vec, _whole((D, D)), _whole((D, D)), vec, vec, vec],
        out_specs=[tile(0), _whole(diag), _whole(diag), _whole((8, D))],
        out_shape=[jax.ShapeDtypeStruct((t, D), bf16), jax.ShapeDtypeStruct(diag, f32),
                   jax.ShapeDtypeStruct(diag, f32), jax.ShapeDtypeStruct((8, D), f32)],
        scratch_shapes=[pltpu.VMEM((TL, D), f32), pltpu.VMEM((TL, D), f32), pltpu.VMEM((TL, D), f32),
                        pltpu.VMEM((1, D), f32), pltpu.VMEM((8, D), f32)],
        compiler_params=_params(("arbitrary", "arbitrary"), VMEM_LIMIT),
    )(dh, a, h, h, xc, rest, rest, conv_w, conv_b, wa_d, wx_d, ba, bx, lam)


def _attn_bwd(qa, ka, qkv, doa, lse, delta, seq):
    t = qkv.shape[0]
    nb, nq = t // seq, seq // TQ
    hg = ATT_GROUP
    ng, npair = HEADS // hg, hg // 2

    def body(qa_ref, ka_ref, q_ref, k_ref, v_ref, do_ref, lse_ref, dl_ref, dq_ref, dk_ref, dv_ref, dc_ref,
             dqt_scr, dk_scr, dv_scr, ds_scr, kht_scr):
        gi, kt = pl.program_id(1), pl.program_id(2)
        lane = lax.broadcasted_iota(jnp.int32, (1, LANES), 1)
        krow = lax.broadcasted_iota(jnp.int32, (TQ, TQ), 0)
        qcol = lax.broadcasted_iota(jnp.int32, (TQ, TQ), 1)
        lmask = [(lane // 64) == hh for hh in range(2)]
        scale = jnp.asarray(QK_SCALE, bf16)

        @pl.when(kt == 0)
        def _():
            dqt_scr[...] = jnp.zeros_like(dqt_scr)

        dk_scr[...] = jnp.zeros_like(dk_scr)
        dv_scr[...] = jnp.zeros_like(dv_scr)
        ds_scr[...] = jnp.zeros_like(ds_scr)
        for g in range(hg):
            k2 = k_ref[:, pl.ds((g // 2) * LANES, LANES)]
            kht_scr[g] = jnp.where(lmask[g % 2], k2, jnp.zeros_like(k2)).T

        def q_step(qt, masked):
            qs = pl.multiple_of(qt * TQ, TQ)
            heads = range(hg)
            do2 = [do_ref[pl.ds(qs, TQ), pl.ds(j * LANES, LANES)] for j in range(npair)]
            q2 = [q_ref[pl.ds(qs, TQ), pl.ds(j * LANES, LANES)] for j in range(npair)]
            doh = [jnp.where(lmask[g % 2], do2[g // 2], jnp.zeros_like(do2[0])) for g in heads]
            qh = [jnp.where(lmask[g % 2], q2[g // 2], jnp.zeros_like(q2[0])) * scale for g in heads]
            st = [_dot_nt(ka_ref[:, pl.ds(g * LANES, LANES)], qa_ref[pl.ds(qs, TQ), pl.ds(g * LANES, LANES)])
                  for g in heads]
            if masked:
                st = [jnp.where(krow <= qcol, s, MASK_VALUE) for s in st]
            dp = [_dot_nt(v_ref[:, pl.ds((g // 2) * LANES, LANES)], doh[g]) for g in heads]
            p = [jnp.exp(st[g] - lse_ref[qt, pl.ds(hg * gi + g, 1), :]) for g in heads]
            ds = [p[g] * (dp[g] - dl_ref[qt, pl.ds(hg * gi + g, 1), :]) for g in heads]
            pb = [x.astype(bf16) for x in p]
            dsb = [x.astype(bf16) for x in ds]
            for j in range(npair):
                a, b = 2 * j, 2 * j + 1
                dv_scr[j] += _dot(pb[a], doh[a]) + _dot(pb[b], doh[b])
                dk_scr[j] += _dot(dsb[a], qh[a]) + _dot(dsb[b], qh[b])
                dqt_scr[qt, j] += (_dot(kht_scr[a], dsb[a]) + _dot(kht_scr[b], dsb[b])) * QK_SCALE
            for g in heads:
                ds_scr[g] += ds[g][:, :LANES] + ds[g][:, LANES:]

        q_step(kt, True)

        def loop_body(qt, carry):
            q_step(qt, False)
            return carry

        lax.fori_loop(kt + 1, nq, loop_body, 0)

        dc = jnp.zeros((TQ, LANES), f32)
        for g in range(hg):
            dc = jnp.where(lane == g, -jnp.sum(ds_scr[g], axis=1, keepdims=True), dc)
        dc_ref[...] = dc
        for j in range(npair):
            dk_ref[:, pl.ds(j * LANES, LANES)] = dk_scr[j].astype(bf16)
            dv_ref[:, pl.ds(j * LANES, LANES)] = dv_scr[j].astype(bf16)

        @pl.when(kt == nq - 1)
        def _():
            for qt in range(nq):
                for j in range(npair):
                    dq_ref[pl.ds(qt * TQ, TQ), pl.ds(j * LANES, LANES)] = dqt_scr[qt, j].T.astype(bf16)

    vw = hg * 64
    seqspec = pl.BlockSpec((seq, vw), lambda b, gi, kt: (b, gi))
    kspec = lambda off: pl.BlockSpec((TQ, vw), lambda b, gi, kt: (b * nq + kt, off + gi))
    rowspec = pl.BlockSpec((nq, HEADS, TQ), lambda b, gi, kt: (b, 0, 0))
    return pl.pallas_call(
        body, name="attn_bwd", grid=(nb, ng, nq),
        in_specs=[pl.BlockSpec((seq, hg * LANES), lambda b, gi, kt: (b, gi)),
                  pl.BlockSpec((TQ, hg * LANES), lambda b, gi, kt: (b * nq + kt, gi)),
                  seqspec, kspec(ng), kspec(2 * ng), seqspec, rowspec, rowspec],
        out_specs=[seqspec, kspec(0), kspec(0), pl.BlockSpec((TQ, LANES), lambda b, gi, kt: (b * nq + kt, gi))],
        out_shape=[jax.ShapeDtypeStruct((t, D), bf16)] * 3 + [jax.ShapeDtypeStruct((t, ng * LANES), f32)],
        scratch_shapes=[pltpu.VMEM((nq, npair, LANES, TQ), f32), pltpu.VMEM((npair, TQ, LANES), f32),
                        pltpu.VMEM((npair, TQ, LANES), f32), pltpu.VMEM((hg, TQ, LANES), f32),
                        pltpu.VMEM((hg, LANES, TQ), bf16)],
        compiler_params=_params(("parallel", "parallel", "arbitrary"), VMEM_LIMIT),
    )(qa, ka, qkv, qkv, qkv, doa, lse, delta)


def _forget_bwd(dc, f128, seq):
    t = f128.shape[0]
    nb = seq // LANES

    def body(dc_ref, f_ref, df_ref, dbf_ref):
        @pl.when(pl.program_id(0) == 0)
        def _():
            dbf_ref[...] = jnp.zeros_like(dbf_ref)

        r = lax.broadcasted_iota(jnp.int32, (LANES, LANES), 0)
        cidx = lax.broadcasted_iota(jnp.int32, (LANES, LANES), 1)
        tri = (r <= cidx).astype(f32)
        carry = jnp.zeros((1, LANES), f32)
        total = jnp.zeros((1, LANES), f32)
        for blk in reversed(range(nb)):
            dcb = dc_ref[pl.ds(blk * LANES, LANES), :]
            dlf = jnp.dot(tri, dcb, preferred_element_type=f32, precision=lax.Precision.HIGHEST) + carry
            df = dlf * _sig(-f_ref[pl.ds(blk * LANES, LANES), :])
            df_ref[pl.ds(blk * LANES, LANES), :] = df.astype(bf16)
            total = total + jnp.sum(df, axis=0, keepdims=True)
            carry = carry + jnp.sum(dcb, axis=0, keepdims=True)
        dbf_ref[...] += total

    return pl.pallas_call(
        body, name="forget_bwd", grid=(t // seq,),
        in_specs=[pl.BlockSpec((seq, LANES), lambda b: (b, 0)), pl.BlockSpec((seq, LANES), lambda b: (b, 0))],
        out_specs=[pl.BlockSpec((seq, LANES), lambda b: (b, 0)), _whole((1, LANES))],
        out_shape=[jax.ShapeDtypeStruct((t, LANES), bf16), jax.ShapeDtypeStruct((1, LANES), f32)],
        compiler_params=_params(("arbitrary",)),
    )(dc, f128)


def _in_bwd(dz, df, x, dy, w_qkv, w_rest, w_f, w_pre):
    t = x.shape[0]
    n_qkv = w_qkv.shape[0] // D
    n_rest = w_rest.shape[0] // D

    def body(*refs):
        dz_refs = refs[:n_qkv + n_rest]
        df_ref, x_ref, dy_ref, wq_ref, wr_ref, wf_ref, wp_ref, gx_ref, dwp_ref = refs[n_qkv + n_rest:]

        @pl.when(pl.program_id(0) == 0)
        def _():
            dwp_ref[...] = jnp.zeros_like(dwp_ref)

        dh = _dot(df_ref[...], wf_ref[...])
        for p in range(n_qkv):
            dh = dh + _dot(dz_refs[p][...], wq_ref[pl.ds(p * D, D), :])
        for p in range(n_rest):
            dh = dh + _dot(dz_refs[n_qkv + p][...], wr_ref[pl.ds(p * D, D), :])
        xv = x_ref[...]
        r1 = lax.rsqrt(jnp.mean(xv * xv, axis=-1, keepdims=True) + NORM_EPS)
        xh = xv * r1
        dwp_ref[...] += jnp.sum(dh * xh, axis=0, keepdims=True)
        dxh = dh * wp_ref[...]
        gx_ref[...] = dy_ref[...] + r1 * (dxh - xh * jnp.mean(dxh * xh, axis=-1, keepdims=True))

    once = lambda shape: pl.BlockSpec(shape, lambda i: (0, 0), pipeline_mode=pl.Buffered(1))
    return pl.pallas_call(
        body, name="in_bwd", grid=(t // TM,),
        in_specs=[_tile(TM, D)] * (n_qkv + n_rest) + [_tile(TM, LANES), _tile(TM, D), _tile(TM, D),
                  once(w_qkv.shape), once(w_rest.shape), once(w_f.shape), _whole((1, D))],
        out_specs=[_tile(TM, D), _whole((1, D))],
        out_shape=[jax.ShapeDtypeStruct((t, D), f32), jax.ShapeDtypeStruct((1, D), f32)],
        compiler_params=_params(("arbitrary",), VMEM_LIMIT),
    )(*dz, df, x, dy, w_qkv, w_rest, w_f, w_pre)


def _tn_mm(name, a, b, tn, tk=2048):
    t, k = a.shape
    tk = min(tk, t)
    n = b.shape[1]

    def body(a_ref, b_ref, o_ref, s_ref):
        j, kk = pl.program_id(0), pl.program_id(1)

        @pl.when(kk == 0)
        def _():
            o_ref[...] = jnp.zeros_like(o_ref)

        @pl.when((j == 0) & (kk == 0))
        def _():
            s_ref[...] = jnp.zeros_like(s_ref)

        av = a_ref[...]
        o_ref[...] += _dot_tn(av, b_ref[...])

        @pl.when(j == 0)
        def _():
            s_ref[...] += jnp.sum(av.astype(f32), axis=0, keepdims=True)

    return pl.pallas_call(
        body, name=name, grid=(n // tn, t // tk),
        in_specs=[pl.BlockSpec((tk, k), lambda j, kk: (kk, 0)), pl.BlockSpec((tk, tn), lambda j, kk: (kk, j))],
        out_specs=[pl.BlockSpec((k, tn), lambda j, kk: (0, j)), _whole((1, k))],
        out_shape=[jax.ShapeDtypeStruct((k, n), f32), jax.ShapeDtypeStruct((1, k), f32)],
        compiler_params=_params(("arbitrary", "arbitrary"), VMEM_LIMIT),
    )(a, b)


def _position():
    return lax.axis_index("x"), lax.axis_index("y"), lax.axis_index("c")


def _gather_shards(parts, small):
    n = len(parts)
    halves = [p.shape[0] // 2 for p in parts]
    cuts = [-(-h // 32) * 16 for h in halves]
    n_direct, n_relay, n_sib = 4 * n, 2 * n, 6 * n

    def body(*refs):
        srcs, small_src = refs[:n], refs[n]
        dsts, small_dst = refs[n + 1:2 * n + 1], refs[2 * n + 1]
        send, recv, local = refs[2 * n + 2:]
        x, y, c = _position()
        me = 2 * x + y
        chips = [(1 - x, y), (x, 1 - y), (1 - x, 1 - y)]
        ids = [2 * px + py for px, py in chips]

        def rows(a, half, quarter):
            start = half * halves[a] + (cuts[a] if quarter else 0)
            return pl.ds(start, halves[a] - cuts[a] if quarter else cuts[a])

        def landing(a, shard, half, quarter):
            return dsts[a].at[shard, rows(a, half, quarter), :]

        def direct(a, nb, quarter, shard):
            k = (a * 2 + nb) * 2 + quarter
            px, py = chips[nb]
            return pltpu.make_async_remote_copy(
                src_ref=srcs[a].at[rows(a, c, quarter), :], dst_ref=landing(a, shard, c, quarter),
                send_sem=send.at[k], recv_sem=recv.at[k], device_id=(px, py, c), device_id_type=MESH)

        def relay(a, quarter, shard):
            k = n_direct + a * 2 + quarter
            px, py = chips[1 - quarter]
            return pltpu.make_async_remote_copy(
                src_ref=landing(a, shard, c, quarter), dst_ref=landing(a, shard, c, quarter),
                send_sem=send.at[k], recv_sem=recv.at[k], device_id=(px, py, c), device_id_type=MESH)

        def to_sibling(a, origin, quarter, half):
            k = n_direct + n_relay + (a * 3 + origin) * 2 + quarter
            return pltpu.make_async_remote_copy(
                src_ref=landing(a, ids[origin], half, quarter), dst_ref=landing(a, ids[origin], half, quarter),
                send_sem=send.at[k], recv_sem=recv.at[k], device_id=(x, y, 1 - c), device_id_type=MESH)

        def small_copy(j, shard):
            k = n_direct + n_relay + n_sib + j
            px, py = chips[j]
            return pltpu.make_async_remote_copy(
                src_ref=small_src, dst_ref=small_dst.at[shard], send_sem=send.at[k], recv_sem=recv.at[k],
                device_id=(px, py, c), device_id_type=MESH)

        own = [pltpu.make_async_copy(srcs[a], dsts[a].at[me], local.at[a]) for a in range(n)]
        own.append(pltpu.make_async_copy(small_src, small_dst.at[me], local.at[n]))
        for cp in own:
            cp.start()
        sent = [direct(a, nb, q, me) for q in range(2) for a in range(n) for nb in range(2)]
        sent += [small_copy(j, me) for j in range(3)]
        for cp in sent:
            cp.start()

        def passed_on(cp):
            cp.start()
            sent.append(cp)

        for q in range(2):
            for a in range(n):
                for nb in range(2):
                    direct(a, nb, q, ids[nb]).wait_recv()
                    passed_on(to_sibling(a, nb, q, c))
                    if nb == q:
                        passed_on(relay(a, q, ids[nb]))
        for a in range(n):
            for q in range(2):
                relay(a, q, ids[2]).wait_recv()
                passed_on(to_sibling(a, 2, q, c))
        for j in range(3):
            small_copy(j, ids[j]).wait_recv()
            for a in range(n):
                for q in range(2):
                    to_sibling(a, j, q, 1 - c).wait_recv()
        for cp in sent:
            cp.wait_send()
        for cp in own:
            cp.wait()

    vm = pl.BlockSpec(memory_space=pltpu.VMEM)
    n_sems = n_direct + n_relay + n_sib + 3
    return pl.pallas_call(
        body, name="gather_shards",
        in_specs=[vm] * (n + 1), out_specs=[vm] * (n + 1),
        out_shape=[jax.ShapeDtypeStruct((N_CHIPS,) + p.shape, p.dtype) for p in parts + [small]],
        scratch_shapes=[pltpu.SemaphoreType.DMA((n_sems,)), pltpu.SemaphoreType.DMA((n_sems,)),
                        pltpu.SemaphoreType.DMA((n + 1,))],
        compiler_params=pltpu.CompilerParams(vmem_limit_bytes=VMEM_LIMIT),
    )(*parts, small)


def _allsum_rows(part):
    rows_n = part.shape[0]

    def body(x_ref, gath_ref, sum_ref, send_sems, recv_sems, local_sem):
        x, y, c = _position()
        me, sibling = (x, y, c), (x, y, 1 - c)
        chips = [(1 - x, y), (x, 1 - y), (1 - x, 1 - y)]

        def rows(px, py, pc):
            return gath_ref.at[pl.ds((4 * px + 2 * py + pc) * rows_n, rows_n), :]

        def copy(k, block, to, src=None):
            return pltpu.make_async_remote_copy(
                src_ref=rows(*block) if src is None else src, dst_ref=rows(*block),
                send_sem=send_sems.at[k], recv_sem=recv_sems.at[k], device_id=to, device_id_type=MESH)

        mine = pltpu.make_async_copy(x_ref, rows(*me), local_sem)
        mine.start()
        first = [copy(0, me, sibling, src=x_ref)]
        first += [copy(1 + j, me, (*chip, c), src=x_ref) for j, chip in enumerate(chips)]
        for cp in first:
            cp.start()
        passed = [copy(4 + j, (*chip, c), sibling) for j, chip in enumerate(chips)]
        for j, chip in enumerate(chips):
            copy(1 + j, (*chip, c), me).wait_recv()
            passed[j].start()
        copy(0, sibling, me).wait_recv()
        for j, chip in enumerate(chips):
            copy(4 + j, (*chip, 1 - c), me).wait_recv()
        for cp in first + passed:
            cp.wait_send()
        mine.wait()
        total = gath_ref[pl.ds(0, rows_n), :]
        for d in range(1, N_DEV):
            total = total + gath_ref[pl.ds(d * rows_n, rows_n), :]
        sum_ref[...] = total

    vm = pl.BlockSpec(memory_space=pltpu.VMEM)
    return pl.pallas_call(
        body, name="allsum_rows", in_specs=[vm], out_specs=[vm, vm],
        out_shape=[jax.ShapeDtypeStruct((N_DEV * rows_n, D), f32), jax.ShapeDtypeStruct((rows_n, D), f32)],
        scratch_shapes=[pltpu.SemaphoreType.DMA((7,)), pltpu.SemaphoreType.DMA((7,)), pltpu.SemaphoreType.DMA],
    )(part)[1]


PAIR_ROWS = 16


def _pair_reduce(name, pieces):
    _, r, n = pieces.shape

    def body(p_ref, o_ref, land, send, recv):
        x, y, c = _position()

        def remote(j, half):
            return pltpu.make_async_remote_copy(
                src_ref=p_ref.at[2 * j + half], dst_ref=land.at[j], send_sem=send.at[j], recv_sem=recv.at[j],
                device_id=(x, y, 1 - c), device_id_type=MESH)

        sends = [remote(j, 1 - c) for j in range(N_CHIPS)]
        for cp in sends:
            cp.start()
        for j in range(N_CHIPS):
            remote(j, c).wait_recv()

            def add_rows(i, carry, j=j):
                rows = pl.ds(pl.multiple_of(i * PAIR_ROWS, PAIR_ROWS), PAIR_ROWS)
                o_ref[j, rows, :] = (p_ref[2 * j + c, rows, :].astype(f32) + land[j, rows, :].astype(f32)).astype(bf16)
                return carry

            lax.fori_loop(0, r // PAIR_ROWS, add_rows, 0)
        for cp in sends:
            cp.wait_send()

    vm = pl.BlockSpec(memory_space=pltpu.VMEM)
    return pl.pallas_call(
        body, name=name, in_specs=[vm], out_specs=vm,
        out_shape=jax.ShapeDtypeStruct((N_CHIPS, r, n), bf16),
        scratch_shapes=[pltpu.VMEM((N_CHIPS, r, n), bf16), pltpu.SemaphoreType.DMA((N_CHIPS,)),
                        pltpu.SemaphoreType.DMA((N_CHIPS,))],
        compiler_params=pltpu.CompilerParams(vmem_limit_bytes=VMEM_LIMIT),
    )(pieces)


def _chip_exchange(arrs):
    n = len(arrs)
    heights = [a.shape[1] for a in arrs]
    cuts = [-(-r // 32) * 16 for r in heights]

    def body(*refs):
        srcs, dsts, relays = refs[:n], refs[n:2 * n], refs[2 * n:3 * n]
        send, recv, local = refs[3 * n:]
        x, y, c = _position()
        me = 2 * x + y
        chips = [(1 - x, y), (x, 1 - y), (1 - x, 1 - y)]
        ids = [2 * px + py for px, py in chips]

        def rows(a, quarter):
            return pl.ds(cuts[a], heights[a] - cuts[a]) if quarter else pl.ds(0, cuts[a])

        def held(a, quarter):
            size = heights[a] - cuts[a] if quarter else cuts[a]
            return relays[a].at[quarter, pl.ds(0, size), :]

        def direct(a, nb, piece, landing):
            px, py = chips[nb]
            return pltpu.make_async_remote_copy(
                src_ref=srcs[a].at[piece], dst_ref=dsts[a].at[landing], send_sem=send.at[a * 2 + nb],
                recv_sem=recv.at[a * 2 + nb], device_id=(px, py, c), device_id_type=MESH)

        def first_hop(a, quarter):
            k = 2 * n + a * 2 + quarter
            px, py = chips[quarter]
            return pltpu.make_async_remote_copy(
                src_ref=srcs[a].at[ids[2], rows(a, quarter), :], dst_ref=held(a, quarter), send_sem=send.at[k],
                recv_sem=recv.at[k], device_id=(px, py, c), device_id_type=MESH)

        def second_hop(a, quarter, origin):
            k = 4 * n + a * 2 + quarter
            px, py = chips[1 - quarter]
            return pltpu.make_async_remote_copy(
                src_ref=held(a, quarter), dst_ref=dsts[a].at[origin, rows(a, quarter), :], send_sem=send.at[k],
                recv_sem=recv.at[k], device_id=(px, py, c), device_id_type=MESH)

        own = [pltpu.make_async_copy(srcs[a].at[me], dsts[a].at[me], local.at[a]) for a in range(n)]
        sent = [first_hop(a, q) for a in range(n) for q in range(2)]
        sent += [direct(a, nb, ids[nb], me) for a in range(n) for nb in range(2)]
        for cp in sent + own:
            cp.start()
        for a in range(n):
            for q in range(2):
                first_hop(a, q).wait_recv()
                sent.append(second_hop(a, q, ids[q]))
                sent[-1].start()
        for a in range(n):
            for nb in range(2):
                direct(a, nb, me, ids[nb]).wait_recv()
            for q in range(2):
                second_hop(a, q, ids[2]).wait_recv()
        for cp in sent:
            cp.wait_send()
        for cp in own:
            cp.wait()

    anyspec = pl.BlockSpec(memory_space=pl.ANY)
    out = pl.pallas_call(
        body, name="chip_exchange", in_specs=[anyspec] * n, out_specs=[anyspec] * (2 * n),
        out_shape=[jax.ShapeDtypeStruct(a.shape, a.dtype) for a in arrs]
        + [jax.ShapeDtypeStruct((2, cut, a.shape[2]), a.dtype) for a, cut in zip(arrs, cuts)],
        scratch_shapes=[pltpu.SemaphoreType.DMA((6 * n,)), pltpu.SemaphoreType.DMA((6 * n,)),
                        pltpu.SemaphoreType.DMA((n,))],
    )(*arrs)
    return out[:n]


def _swap_halves(arrs):
    n = len(arrs)

    def body(*refs):
        srcs, dsts = refs[:n], refs[n:2 * n]
        send, recv, local = refs[2 * n:]
        x, y, c = _position()

        def remote(a, landing):
            return pltpu.make_async_remote_copy(
                src_ref=srcs[a], dst_ref=dsts[a].at[landing], send_sem=send.at[a], recv_sem=recv.at[a],
                device_id=(x, y, 1 - c), device_id_type=MESH)

        own = [pltpu.make_async_copy(srcs[a], dsts[a].at[c], local.at[a]) for a in range(n)]
        sends = [remote(a, c) for a in range(n)]
        for cp in sends + own:
            cp.start()
        for a in range(n):
            remote(a, 1 - c).wait_recv()
        for cp in sends:
            cp.wait_send()
        for cp in own:
            cp.wait()

    vm = pl.BlockSpec(memory_space=pltpu.VMEM)
    return pl.pallas_call(
        body, name="swap_halves", in_specs=[vm] * n, out_specs=[vm] * n,
        out_shape=[jax.ShapeDtypeStruct((2,) + a.shape, a.dtype) for a in arrs],
        scratch_shapes=[pltpu.SemaphoreType.DMA((n,)), pltpu.SemaphoreType.DMA((n,)), pltpu.SemaphoreType.DMA((n,))],
        compiler_params=pltpu.CompilerParams(vmem_limit_bytes=VMEM_LIMIT),
    )(*arrs)


def _row_block(r):
    return 128 if r % 128 == 0 else r


def _sum_slots(name, slots):
    s, r, n = slots.shape
    rb = _row_block(r)

    def body(s_ref, o_ref):
        total = s_ref[0].astype(f32)
        for d in range(1, s):
            total = total + s_ref[d].astype(f32)
        o_ref[...] = total

    return pl.pallas_call(
        body, name=name, grid=(r // rb,),
        in_specs=[pl.BlockSpec((s, rb, n), lambda i: (0, i, 0))],
        out_specs=pl.BlockSpec((rb, n), lambda i: (i, 0)),
        out_shape=jax.ShapeDtypeStruct((r, n), f32),
        compiler_params=_params(("parallel",), VMEM_LIMIT),
    )(slots)


def _adamw(name, w, g, m, v):
    r, n = w.shape
    if r % 128 == 0 or r * n <= 128 * 1024:
        rb, nb = _row_block(r), n
    else:
        rb, nb = r, LANES

    def body(w_ref, g_ref, m_ref, v_ref, d_ref, nm_ref, nv_ref):
        gv = g_ref[...]
        m2 = ADAM_B1 * m_ref[...] + (1.0 - ADAM_B1) * gv
        v2 = ADAM_B2 * v_ref[...] + (1.0 - ADAM_B2) * (gv * gv)
        m_hat = m2 / (1.0 - ADAM_B1 ** ADAM_STEP)
        v_hat = v2 / (1.0 - ADAM_B2 ** ADAM_STEP)
        d_ref[...] = (-ADAM_LR) * (m_hat / (jnp.sqrt(v_hat) + ADAM_EPS) + ADAM_WD * w_ref[...])
        nm_ref[...] = m2
        nv_ref[...] = v2

    spec = pl.BlockSpec((rb, nb), lambda i, j: (i, j))
    return pl.pallas_call(
        body, name=name, grid=(r // rb, n // nb), in_specs=[spec] * 4, out_specs=[spec] * 3,
        out_shape=[jax.ShapeDtypeStruct((r, n), f32)] * 3,
        compiler_params=_params(("parallel", "parallel"), VMEM_LIMIT),
    )(w, g, m, v)


def _local_step(x2, tgt2, seq, wt):
    nb = x2.shape[0] // seq
    h = _prenorm(x2, wt["pre_w"])
    qkv = _mm("in_qkv", h, wt["w_qkv"], wt["b_qkv"], bf16, 1024, 1024, w_is_nk=True)
    rest = _mm("in_rest", h, wt["w_rest"], wt["b_rest"], bf16, 1024, 1024, w_is_nk=True)
    f128 = _mm("in_f", h, wt["w_f"], wt["b_f"], f32, 1024, LANES, w_is_nk=True)
    c = _forget_prep(f128, seq)
    qa, ka = _attn_prep(qkv, c)
    o_att, pa, lse = _attn_fwd(qa, ka, qkv, rest, seq)
    ya = _mm("proj_a", pa, wt["w_a"], None, bf16, 1024, D)
    rnn_w = (wt["conv_w"], wt["conv_b"], wt["wa_d"], wt["wx_d"], wt["ba"], wt["bx"], wt["lam"])
    xc, a, hrec, pr = _rnn_fwd(rest, *rnn_w, seq)
    yr = _mm("proj_r", pr, wt["w_r"], None, bf16, 1024, D)
    do, dy, mrg, loss8, d_post = _out_proj_loss(rest, ya, yr, wt["w_o"], x2, tgt2, wt["post_w"])
    dya, dyr, dmga, dmgr = _out_bwd(do, rest, ya, yr, wt["w_o"])
    doa, dga, delta = _branch_bwd("branch_a_bwd", dya, rest, 0, o_att, wt["w_a"], bf16, head_sums=True)
    dhrec, dgr = _branch_bwd("branch_r_bwd", dyr, rest, 2, hrec, wt["w_r"], bf16)
    d_wo, _ = _tn_mm("dw_out", mrg, do, D)
    d_wa, _ = _tn_mm("dw_branch_a", pa, dya, D)
    d_wr, _ = _tn_mm("dw_branch_r", pr, dyr, D)
    dxr, d_wad, d_wxd, vec = _rnn_bwd(dhrec, a, hrec, xc, rest, *rnn_w, seq)
    dq, dk, dv, dc_pairs = _attn_bwd(qa, ka, qkv, doa, lse, delta, seq)
    dc = dc_pairs.reshape(-1, HEADS // ATT_GROUP, LANES)[:, :, :ATT_GROUP].reshape(-1, HEADS)
    df, db_f = _forget_bwd(_pad_cols(dc, LANES), f128, seq)
    pieces = [dq, dk, dv, dga, dxr, dgr, dmga, dmgr]
    gx, d_pre = _in_bwd(pieces, df, x2, dy, wt["w_qkv"], wt["w_rest"], wt["w_f"], wt["pre_w"])
    names = ["q", "k", "v", "ga", "xr", "gr", "mga", "mgr"]
    dws, dbs = [], []
    for nm, piece in zip(names, pieces):
        dw_p, db_p = _tn_mm("dw_in_" + nm, piece, h, D)
        dws.append(dw_p)
        dbs.append(db_p)
    dw_f, _ = _tn_mm("dw_in_f", df, h, D)
    zeros_w = jnp.zeros((IN_TOTAL - IN_USED, D), f32)
    d_w_in = jnp.concatenate(dws[:3] + [dw_f[:HEADS]] + dws[3:] + [zeros_w], axis=0)
    d_b_in = jnp.concatenate(dbs[:3] + [db_f[:, :HEADS]] + dbs[3:] + [zeros_w[:, :1].T], axis=1)
    return dict(loss=loss8[0, 0], grad_x=gx, pre_w=d_pre, w_in=d_w_in, b_in=d_b_in, conv_w=vec[4:8], conv_b=vec[3:4],
                wa_d=d_wad, ba=vec[0:1], wx_d=d_wxd, bx=vec[1:2], lam=vec[2:3], w_a=d_wa, w_r=d_wr, w_o=d_wo,
                post_w=d_post)


def _block_diag(w):
    g, bw, _ = w.shape
    eye = jnp.eye(g, dtype=w.dtype)
    return (w[:, :, None, :] * eye[:, None, :, None]).reshape(g * bw, g * bw)


def _gate_blocks(diag):
    half = diag.shape[1] // 2
    return jnp.stack([diag[:, :half, :half], diag[:, half:, half:]], axis=1).reshape(-1, half, half)


def _pad_cols(a, n):
    return jnp.pad(a, ((0, 0), (0, n - a.shape[1])))


def _pad_rows(a, n):
    return jnp.pad(a, ((0, n - a.shape[0]), (0, 0)))


def kernel(x, pre_norm_w, w_in, b_in, conv_w, conv_b, rg_wa, rg_ba, rg_wx, rg_bx, rg_lambda, w_branch_a, w_branch_r, w_out, post_norm_w, loss_target, m_pre_norm_w, m_w_in, m_b_in, m_conv_w, m_conv_b, m_rg_wa, m_rg_ba, m_rg_wx, m_rg_bx, m_rg_lambda, m_w_branch_a, m_w_branch_r, m_w_out, m_post_norm_w, v_pre_norm_w, v_w_in, v_b_in, v_conv_w, v_conv_b, v_rg_wa, v_rg_ba, v_rg_wx, v_rg_bx, v_rg_lambda, v_w_branch_a, v_w_branch_r, v_w_out, v_post_norm_w):
    nb, seq, _ = x.shape
    chip = 2 * lax.axis_index("x") + lax.axis_index("y")
    n_groups = rg_wa.shape[1]

    w_in_t = jnp.transpose(w_in[0])
    shard_cols = w_in_t.shape[0]
    padded = -(-shard_cols // 32) * 32
    g_in, g_a, g_r, g_o, g_cw = _gather_shards(
        [_pad_rows(w_in_t.astype(bf16), padded), w_branch_a[0].astype(bf16), w_branch_r[0].astype(bf16),
         w_out[0].astype(bf16)], conv_w[0])
    w_full = jnp.concatenate([g_in[j, :shard_cols] for j in range(N_CHIPS)], axis=0)
    q_end, f_end = 3 * D, 3 * D + HEADS
    wt = dict(
        pre_w=pre_norm_w, post_w=post_norm_w,
        w_qkv=w_full[:q_end], b_qkv=b_in[:, :q_end],
        w_f=_pad_rows(w_full[q_end:f_end], LANES), b_f=_pad_cols(b_in[:, q_end:f_end], LANES),
        w_rest=w_full[f_end:IN_USED], b_rest=b_in[:, f_end:IN_USED],
        w_a=g_a.reshape(D, D), w_r=g_r.reshape(D, D), w_o=g_o.reshape(D, D),
        conv_w=jnp.transpose(g_cw, (1, 0, 2)).reshape(4, D), conv_b=conv_b,
        wa_d=_block_diag(rg_wa[0]).astype(bf16), wx_d=_block_diag(rg_wx[0]).astype(bf16),
        ba=rg_ba, bx=rg_bx, lam=rg_lambda)

    part = _local_step(x.reshape(nb * seq, D), loss_target.reshape(nb * seq, D), seq, wt)
    loss = lax.psum(part["loss"], ("x", "y", "c"))
    grad_x = part["grad_x"].reshape(nb, seq, D)

    small = jnp.concatenate([
        part["pre_w"], _pad_cols(part["b_in"], 10 * D).reshape(10, D), part["conv_b"],
        _gate_blocks(part["wa_d"]).reshape(-1, D), part["ba"],
        _gate_blocks(part["wx_d"]).reshape(-1, D), part["bx"], part["lam"], part["post_w"],
        part["conv_w"]], axis=0)
    n_small = small.shape[0]
    n_rep = n_small - 4
    tot = _allsum_rows(_pad_rows(small, -(-n_small // 8) * 8))
    g_rep = tot[:n_rep]
    g_conv_w = lax.dynamic_slice_in_dim(tot[n_rep:n_small], chip * (D // N_CHIPS), D // N_CHIPS, axis=1)

    def unpack(p):
        o = [0]

        def take(k):
            o[0] += k
            return p[o[0] - k:o[0]]

        pre = take(1)
        b = take(10).reshape(1, 10 * D)[:, :IN_TOTAL]
        cb = take(1)
        wa = take(64).reshape(rg_wa.shape)
        ba = take(1)
        wx = take(64).reshape(rg_wx.shape)
        bx = take(1)
        lam = take(1)
        post = take(1)
        return dict(pre_norm_w=pre, b_in=b, conv_b=cb, rg_wa=wa, rg_ba=ba, rg_wx=wx, rg_bx=bx, rg_lambda=lam,
                    post_norm_w=post)

    grads = unpack(g_rep)
    replicated = dict(
        pre_norm_w=(pre_norm_w, m_pre_norm_w, v_pre_norm_w), b_in=(b_in, m_b_in, v_b_in),
        conv_b=(conv_b, m_conv_b, v_conv_b), rg_wa=(rg_wa, m_rg_wa, v_rg_wa), rg_ba=(rg_ba, m_rg_ba, v_rg_ba),
        rg_wx=(rg_wx, m_rg_wx, v_rg_wx), rg_bx=(rg_bx, m_rg_bx, v_rg_bx),
        rg_lambda=(rg_lambda, m_rg_lambda, v_rg_lambda), post_norm_w=(post_norm_w, m_post_norm_w, v_post_norm_w))
    deltas, new_m, new_v = {}, {}, {}
    for name, (w, m, v) in replicated.items():
        as2d = lambda a: a.reshape(-1, D) if a.ndim > 2 else a
        upd = _adamw("adamw_" + name, as2d(w), as2d(grads[name]), as2d(m), as2d(v))
        deltas[name], new_m[name], new_v[name] = [a.reshape(w.shape) for a in upd]

    p_in = jnp.pad(part["w_in"].reshape(N_CHIPS, shard_cols, D), ((0, 0), (0, padded - shard_cols), (0, 0)))
    p_in = p_in.reshape(N_DEV, padded // 2, D)
    p_aro = jnp.concatenate([part[k].reshape(N_DEV, D // N_DEV, D) for k in ("w_a", "w_r", "w_o")], axis=1)
    s_in, s_aro = _chip_exchange([_pair_reduce("pair_w_in", p_in.astype(bf16)),
                                  _pair_reduce("pair_w_aro", p_aro.astype(bf16))])
    f_in, f_aro = _swap_halves([_sum_slots("sum_w_in", s_in), _sum_slots("sum_w_aro", s_aro)])
    g_w_in_t = f_in.reshape(padded, D)[:shard_cols]
    rows = D // N_DEV
    g_aro = [f_aro[:, i * rows:(i + 1) * rows, :].reshape(2 * rows, D) for i in range(3)]

    w_in_upd = _adamw("adamw_w_in", w_in_t, g_w_in_t, jnp.transpose(m_w_in[0]), jnp.transpose(v_w_in[0]))
    g_w_in, d_w_in, nm_w_in, nv_w_in = [jnp.transpose(a) for a in (g_w_in_t, *w_in_upd)]
    upd_a = _adamw("adamw_w_branch_a", w_branch_a[0], g_aro[0], m_w_branch_a[0], v_w_branch_a[0])
    upd_r = _adamw("adamw_w_branch_r", w_branch_r[0], g_aro[1], m_w_branch_r[0], v_w_branch_r[0])
    upd_o = _adamw("adamw_w_out", w_out[0], g_aro[2], m_w_out[0], v_w_out[0])
    d_aro, nm_aro, nv_aro = zip(upd_a, upd_r, upd_o)
    d_cw, nm_cw, nv_cw = _adamw("adamw_conv_w", conv_w[0], g_conv_w, m_conv_w[0], v_conv_w[0])

    def sharded(t_in, t_aro, t_cw):
        return dict(w_in=t_in[None], conv_w=t_cw[None], w_branch_a=t_aro[0][None], w_branch_r=t_aro[1][None],
                    w_out=t_aro[2][None])

    order = ["pre_norm_w", "w_in", "b_in", "conv_w", "conv_b", "rg_wa", "rg_ba", "rg_wx", "rg_bx", "rg_lambda",
             "w_branch_a", "w_branch_r", "w_out", "post_norm_w"]
    outs = [loss, grad_x]
    for rep, shd in ((grads, sharded(g_w_in, g_aro, g_conv_w)), (deltas, sharded(d_w_in, d_aro, d_cw)),
                     (new_m, sharded(nm_w_in, nm_aro, nm_cw)), (new_v, sharded(nv_w_in, nv_aro, nv_cw))):
        both = {**rep, **shd}
        outs.extend(both[k] for k in order)
    return tuple(outs)
```

```python
import jax
import jax.numpy as jnp
from jax import lax
from jax.experimental import pallas as pl
from jax.experimental.pallas import tpu as pltpu

f32 = jnp.float32
bf16 = jnp.bfloat16

D = 1024
HEADS = 16
HEAD_PAIRS = 8
LANES = 128
NORM_EPS = 1e-6
MASK_VALUE = -1e30
RG_C = 8.0
QK_SCALE = 0.125
TQ = 256
ATT_GROUP = 8
ATT_GROUP_FWD = 16
TL = 256
TM = 512
PREV_ROWS = 16
IN_USED = 8 * D + HEADS
IN_TOTAL = 9 * D + HEADS
N_CHIPS = 4
N_DEV = 8
ADAM_LR, ADAM_B1, ADAM_B2, ADAM_EPS, ADAM_WD, ADAM_STEP = 0.001, 0.9, 0.999, 1e-08, 0.01, 10
VMEM_LIMIT = 56 * 1024 * 1024
MESH = pl.DeviceIdType.MESH


def _dot(a, b):
    return jnp.dot(a, b, preferred_element_type=f32)


def _dot_nt(a, b):
    return lax.dot_general(a, b, (((1,), (1,)), ((), ())), preferred_element_type=f32)


def _dot_tn(a, b):
    return lax.dot_general(a, b, (((0,), (0,)), ((), ())), preferred_element_type=f32)


def _sig(x):
    return 0.5 * jnp.tanh(0.5 * x) + 0.5


def _softplus(x):
    return jnp.maximum(x, 0.0) + jnp.log(1.0 + jnp.exp(-jnp.abs(x)))


def _params(sem, vmem=None):
    return pltpu.CompilerParams(dimension_semantics=sem, vmem_limit_bytes=vmem)


def _tile(tm, width, cb=0):
    return pl.BlockSpec((tm, width), lambda i, cb=cb: (i, cb))


def _whole(shape):
    nd = len(shape)
    return pl.BlockSpec(shape, lambda *_: (0,) * nd)


def _norm_qkv(x, w_pre, w_qkv, b_qkv, tm=1024):
    t = x.shape[0]
    tm = min(tm, t)
    n = w_qkv.shape[0]

    def body(x_ref, wp_ref, w_ref, b_ref, h_ref, o_ref):
        @pl.when(pl.program_id(1) == 0)
        def _():
            xv = x_ref[...]
            r = lax.rsqrt(jnp.mean(xv * xv, axis=-1, keepdims=True) + NORM_EPS)
            h_ref[...] = (xv * r * wp_ref[...]).astype(bf16)

        o_ref[...] = (_dot_nt(h_ref[...], w_ref[...]) + b_ref[...]).astype(bf16)

    return pl.pallas_call(
        body, name="norm_qkv", grid=(t // tm, n // D),
        in_specs=[pl.BlockSpec((tm, D), lambda i, j: (i, 0)), _whole((1, D)), pl.BlockSpec((D, D), lambda i, j: (j, 0)),
                  pl.BlockSpec((1, D), lambda i, j: (0, j))],
        out_specs=[pl.BlockSpec((tm, D), lambda i, j: (i, 0)), pl.BlockSpec((tm, D), lambda i, j: (i, j))],
        out_shape=[jax.ShapeDtypeStruct((t, D), bf16), jax.ShapeDtypeStruct((t, n), bf16)],
        compiler_params=_params(("parallel", "arbitrary"), VMEM_LIMIT),
    )(x, w_pre, w_qkv, b_qkv)


def _mm(name, a, w, bias, out_dtype, tm, tn, w_is_nk=False):
    t, k = a.shape
    tm = min(tm, t)
    n = w.shape[0] if w_is_nk else w.shape[1]

    def body(a_ref, w_ref, *refs):
        acc = _dot_nt(a_ref[...], w_ref[...]) if w_is_nk else _dot(a_ref[...], w_ref[...])
        if bias is not None:
            acc = acc + refs[0][...]
        refs[-1][...] = acc.astype(out_dtype)

    in_specs = [pl.BlockSpec((tm, k), lambda i, j: (i, 0)),
                pl.BlockSpec((tn, k), lambda i, j: (j, 0)) if w_is_nk else pl.BlockSpec((k, tn), lambda i, j: (0, j))]
    args = [a, w]
    if bias is not None:
        in_specs.append(pl.BlockSpec((1, tn), lambda i, j: (0, j)))
        args.append(bias)
    return pl.pallas_call(
        body, name=name, grid=(t // tm, n // tn), in_specs=in_specs,
        out_specs=pl.BlockSpec((tm, tn), lambda i, j: (i, j)), out_shape=jax.ShapeDtypeStruct((t, n), out_dtype),
        compiler_params=_params(("parallel", "parallel"), VMEM_LIMIT),
    )(*args)


def _forget_prep(f128, seq):
    t = f128.shape[0]
    nb = seq // LANES

    def body(f_ref, c_ref):
        r = lax.broadcasted_iota(jnp.int32, (LANES, LANES), 0)
        cidx = lax.broadcasted_iota(jnp.int32, (LANES, LANES), 1)
        tri = (r >= cidx).astype(f32)
        carry = jnp.zeros((1, LANES), f32)
        for blk in range(nb):
            fv = f_ref[pl.ds(blk * LANES, LANES), :]
            lf = -_softplus(-fv)
            c_ref[pl.ds(blk * LANES, LANES), :] = (
                jnp.dot(tri, lf, preferred_element_type=f32, precision=lax.Precision.HIGHEST) + carry)
            carry = carry + jnp.sum(lf, axis=0, keepdims=True)

    return pl.pallas_call(
        body, name="forget_prep", grid=(t // seq,),
        in_specs=[pl.BlockSpec((seq, LANES), lambda b: (b, 0))],
        out_specs=pl.BlockSpec((seq, LANES), lambda b: (b, 0)),
        out_shape=jax.ShapeDtypeStruct((t, LANES), f32),
        compiler_params=_params(("parallel",)),
    )(f128)


def _split3(cv):
    hi = cv.astype(bf16)
    r1 = cv - hi.astype(f32)
    mid = r1.astype(bf16)
    lo = (r1 - mid.astype(f32)).astype(bf16)
    return hi, mid, lo


def _attn_prep(qkv, c):
    t = qkv.shape[0]

    def body(q_ref, k_ref, c_ref, qa_ref, ka_ref):
        lane = lax.broadcasted_iota(jnp.int32, (1, LANES), 1)
        cv = c_ref[...]
        one = jnp.ones((), bf16)
        zero = jnp.zeros((), bf16)
        q_ones = jnp.where((lane >= 67) & (lane < 70), one, zero)
        k_ones = jnp.where((lane >= 64) & (lane < 67), one, zero)
        for head in range(HEADS):
            pair = pl.ds((head // 2) * LANES, LANES)
            ch = jnp.sum(jnp.where(lane == head, cv, 0.0), axis=1, keepdims=True)
            hi, mid, lo = _split3(ch)
            q2, k2 = q_ref[:, pair], k_ref[:, pair]
            if head % 2 == 1:
                q2, k2 = pltpu.roll(q2, 64, 1), pltpu.roll(k2, 64, 1)
            qa = jnp.where(lane < 64, q2 * jnp.asarray(QK_SCALE, bf16),
                           jnp.where(lane == 64, hi, jnp.where(lane == 65, mid, jnp.where(lane == 66, lo, q_ones))))
            ka = jnp.where(lane < 64, k2,
                           jnp.where(lane == 67, -hi, jnp.where(lane == 68, -mid, jnp.where(lane == 69, -lo, k_ones))))
            qa_ref[:, pl.ds(head * LANES, LANES)] = qa
            ka_ref[:, pl.ds(head * LANES, LANES)] = ka

    tm = min(TM, t)
    out = pl.BlockSpec((tm, 2 * D), lambda i: (i, 0))
    return pl.pallas_call(
        body, name="attn_prep", grid=(t // tm,),
        in_specs=[_tile(tm, D, 0), _tile(tm, D, 1), _tile(tm, LANES)],
        out_specs=[out, out],
        out_shape=[jax.ShapeDtypeStruct((t, 2 * D), bf16)] * 2,
        compiler_params=_params(("parallel",)),
    )(qkv, qkv, c)


def _attn_fwd(qa, ka, qkv, rest, seq):
    t = qkv.shape[0]
    nb, nq = t // seq, seq // TQ

    hg = ATT_GROUP_FWD
    ng = HEADS // hg

    def body(q_ref, k_ref, v_ref, ga_ref, o_ref, pa_ref, lse_ref, acc_scr):
        qi, gi = pl.program_id(1), pl.program_id(2)
        krow = lax.broadcasted_iota(jnp.int32, (TQ, TQ), 0)
        qcol = lax.broadcasted_iota(jnp.int32, (TQ, TQ), 1)
        acc_scr[...] = jnp.zeros_like(acc_scr)

        def kv_step(kt, carry, masked):
            ks = pl.multiple_of(kt * TQ, TQ)
            sts = [_dot_nt(k_ref[pl.ds(ks, TQ), pl.ds(g * LANES, LANES)], q_ref[:, pl.ds(g * LANES, LANES)])
                   for g in range(hg)]
            if masked:
                sts = [jnp.where(krow <= qcol, st, MASK_VALUE) for st in sts]
            m_new = [jnp.maximum(carry[g][0], jnp.max(sts[g], axis=0, keepdims=True)) for g in range(hg)]
            ps = [jnp.exp(sts[g] - m_new[g]) for g in range(hg)]
            alphas = [jnp.exp(carry[g][0] - m_new[g]) for g in range(hg)]
            phi = [ps[g].astype(bf16) for g in range(hg)]
            plo = [(ps[g] - phi[g].astype(f32)).astype(bf16) for g in range(hg)]
            vs = [v_ref[pl.ds(ks, TQ), pl.ds(j * LANES, LANES)] for j in range(hg // 2)]
            pvs = [_dot_tn(vs[g // 2], phi[g]) + _dot_tn(vs[g // 2], plo[g]) for g in range(hg)]
            olds = [acc_scr[g] for g in range(hg)]
            for g in range(hg):
                acc_scr[g] = alphas[g] * olds[g] + pvs[g]
            return tuple((m_new[g], alphas[g] * carry[g][1] + jnp.sum(ps[g], axis=0, keepdims=True))
                         for g in range(hg))

        init = tuple((jnp.full((1, TQ), MASK_VALUE, f32), jnp.zeros((1, TQ), f32)) for _ in range(hg))
        carry = lax.fori_loop(0, qi, lambda kt, cr: kv_step(kt, cr, False), init)
        stats = kv_step(qi, carry, True)
        drow = lax.broadcasted_iota(jnp.int32, (LANES, TQ), 0)
        for g in range(hg):
            m, l = stats[g]
            lse_ref[0, pl.ds(hg * gi + g, 1), :] = m + jnp.log(l)
        for j in range(hg // 2):
            o2 = jnp.where(drow < 64, acc_scr[2 * j] / stats[2 * j][1], acc_scr[2 * j + 1] / stats[2 * j + 1][1]).T
            o_ref[:, pl.ds(j * LANES, LANES)] = o2
            ga = ga_ref[:, pl.ds(j * LANES, LANES)].astype(f32)
            pa_ref[:, pl.ds(j * LANES, LANES)] = (o2 * (ga * _sig(ga))).astype(bf16)

    vw = hg * 64
    tile = pl.BlockSpec((TQ, vw), lambda b, qi, gi: (b * nq + qi, gi))
    return pl.pallas_call(
        body, name="attn_fwd", grid=(nb, nq, ng),
        in_specs=[pl.BlockSpec((TQ, hg * LANES), lambda b, qi, gi: (b * nq + qi, gi)),
                  pl.BlockSpec((seq, hg * LANES), lambda b, qi, gi: (b, gi)),
                  pl.BlockSpec((seq, vw), lambda b, qi, gi: (b, 2 * ng + gi)), tile],
        out_specs=[tile, tile, pl.BlockSpec((1, HEADS, TQ), lambda b, qi, gi: (b * nq + qi, 0, 0))],
        out_shape=[jax.ShapeDtypeStruct((t, D), f32), jax.ShapeDtypeStruct((t, D), bf16),
                   jax.ShapeDtypeStruct((t // TQ, HEADS, TQ), f32)],
        scratch_shapes=[pltpu.VMEM((hg, LANES, TQ), f32)],
        compiler_params=_params(("parallel", "parallel", "arbitrary"), VMEM_LIMIT),
    )(qa, ka, qkv, rest)


def _shifted_rows(x, top8, prev8, shift, row, row8):
    body = pltpu.roll(x, shift, 0)
    head = jnp.where(row8 < shift, pltpu.roll(prev8, shift, 0), pltpu.roll(top8, shift, 0))
    return body, head


def _rnn_gates(xc, wa_ref, wx_ref, ba_ref, bx_ref, lam_ref):
    xcb = xc.astype(bf16)
    r = _sig(_dot(xcb, wa_ref[...]) + ba_ref[...])
    i = _sig(_dot(xcb, wx_ref[...]) + bx_ref[...])
    sp = _softplus(-lam_ref[...])
    log_a = (-RG_C) * r * sp
    th = jnp.tanh(log_a)
    w1 = (-2.0) * th / (1.0 - th)
    sq = jnp.sqrt(jnp.maximum(w1, 0.0))
    return r, i, sp, log_a, w1, sq


def _conv_tile(x_ref, xprev_ref, has_prev, cw_ref, cb_ref, xc_ref):
    row = lax.broadcasted_iota(jnp.int32, (TL, D), 0)
    row8 = lax.broadcasted_iota(jnp.int32, (8, D), 0)
    x = x_ref[...].astype(f32)
    top8 = x[:8]
    prev8 = jnp.where(has_prev, xprev_ref[...].astype(f32)[PREV_ROWS - 8:], 0.0)
    xc = cb_ref[...] + cw_ref[pl.ds(3, 1), :] * x
    xc8 = cb_ref[...] + cw_ref[pl.ds(3, 1), :] * top8
    for sh in range(1, 4):
        w = cw_ref[pl.ds(3 - sh, 1), :]
        xs, xs8 = _shifted_rows(x, top8, prev8, sh, row, row8)
        xc = xc + w * xs
        xc8 = xc8 + w * xs8
    xc_ref[...] = xc
    xc_ref[pl.ds(0, 8), :] = xc8


def _rnn_fwd(rest, conv_w, conv_b, wa_d, wx_d, ba, bx, lam, seq):
    t = rest.shape[0]
    nb, nt = t // seq, seq // TL

    def body(x_ref, xprev_ref, gr_ref, cw_ref, cb_ref, wa_ref, wx_ref, ba_ref, bx_ref, lam_ref,
             xc_ref, a_ref, h_ref, pr_ref, xc_scr, u_scr, h_scr, carry):
        tt = pl.program_id(1)
        _conv_tile(x_ref, xprev_ref, tt > 0, cw_ref, cb_ref, xc_scr)
        xc = xc_scr[...]
        xc_ref[...] = xc.astype(bf16)
        r, i, sp, log_a, w1, sq = _rnn_gates(xc, wa_ref, wx_ref, ba_ref, bx_ref, lam_ref)
        a_ref[...] = jnp.exp(log_a)
        u_scr[...] = sq * (i * xc)

        @pl.when(tt == 0)
        def _():
            carry[...] = jnp.zeros_like(carry)

        def step(s, h):
            h = a_ref[pl.ds(s, 1), :] * h + u_scr[pl.ds(s, 1), :]
            h_scr[pl.ds(s, 1), :] = h
            return h

        carry[...] = lax.fori_loop(0, TL, step, carry[...], unroll=8)
        gr = gr_ref[...].astype(f32)
        h = h_scr[...]
        h_ref[...] = h.astype(bf16)
        pr_ref[...] = (h * (gr * _sig(gr))).astype(bf16)

    tile = lambda cb: pl.BlockSpec((TL, D), lambda b, tt, cb=cb: (b * nt + tt, cb))
    prev = lambda cb: pl.BlockSpec(
        (PREV_ROWS, D), lambda b, tt, cb=cb: (jnp.maximum((b * nt + tt) * (TL // PREV_ROWS) - 1, 0), cb))
    vec = _whole((1, D))
    return pl.pallas_call(
        body, name="rnn_fwd", grid=(nb, nt),
        in_specs=[tile(1), prev(1), tile(2), _whole((4, D)), vec, _whole((D, D)), _whole((D, D)), vec, vec, vec],
        out_specs=[tile(0)] * 4,
        out_shape=[jax.ShapeDtypeStruct((t, D), dt) for dt in (bf16, f32, bf16, bf16)],
        scratch_shapes=[pltpu.VMEM((TL, D), f32)] * 3 + [pltpu.VMEM((1, D), f32)],
        compiler_params=_params(("parallel", "arbitrary"), VMEM_LIMIT),
    )(rest, rest, rest, conv_w, conv_b, wa_d, wx_d, ba, bx, lam)


def _merge(mga, mgr, ya, yr):
    return (_sig(mga.astype(f32)) * ya.astype(f32) + _sig(mgr.astype(f32)) * yr.astype(f32)).astype(bf16)


def _out_proj_loss(rest, ya, yr, w_out, x, tgt, w_post):
    t = x.shape[0]

    def body(mga_ref, mgr_ref, ya_ref, yr_ref, wo_ref, x_ref, t_ref, w_ref, do_ref, dy_ref, mrg_ref, loss_ref, dwp_ref):
        @pl.when(pl.program_id(0) == 0)
        def _():
            loss_ref[...] = jnp.zeros_like(loss_ref)
            dwp_ref[...] = jnp.zeros_like(dwp_ref)

        mrg = _merge(mga_ref[...], mgr_ref[...], ya_ref[...], yr_ref[...])
        mrg_ref[...] = mrg
        ov = _dot(mrg, wo_ref[...])
        w = w_ref[...]
        r2 = lax.rsqrt(jnp.mean(ov * ov, axis=-1, keepdims=True) + NORM_EPS)
        oh = ov * r2
        e = x_ref[...] + oh * w - t_ref[...]
        loss_ref[...] += 0.5 * jnp.sum(jnp.mean(e * e, axis=-1, keepdims=True))
        dy = e * (1.0 / D)
        dy_ref[...] = dy
        dwp_ref[...] += jnp.sum(dy * oh, axis=0, keepdims=True)
        doh = dy * w
        do_ref[...] = (r2 * (doh - oh * jnp.mean(doh * oh, axis=-1, keepdims=True))).astype(bf16)

    return pl.pallas_call(
        body, name="out_proj_loss", grid=(t // TM,),
        in_specs=[_tile(TM, D, 3), _tile(TM, D, 4), _tile(TM, D), _tile(TM, D), _whole((D, D)), _tile(TM, D),
                  _tile(TM, D), _whole((1, D))],
        out_specs=[_tile(TM, D), _tile(TM, D), _tile(TM, D), _whole((8, LANES)), _whole((1, D))],
        out_shape=[jax.ShapeDtypeStruct((t, D), bf16), jax.ShapeDtypeStruct((t, D), f32),
                   jax.ShapeDtypeStruct((t, D), bf16), jax.ShapeDtypeStruct((8, LANES), f32),
                   jax.ShapeDtypeStruct((1, D), f32)],
        compiler_params=_params(("arbitrary",), VMEM_LIMIT),
    )(rest, rest, ya, yr, w_out, x, tgt, w_post)


def _out_bwd(do, rest, ya, yr, w_out):
    t = do.shape[0]

    def body(do_ref, mga_ref, mgr_ref, ya_ref, yr_ref, w_ref, dya_ref, dyr_ref, dmga_ref, dmgr_ref):
        sa, sr = _sig(mga_ref[...].astype(f32)), _sig(mgr_ref[...].astype(f32))
        ya, yr = ya_ref[...].astype(f32), yr_ref[...].astype(f32)
        dm = _dot_nt(do_ref[...], w_ref[...])
        dya_ref[...] = (dm * sa).astype(bf16)
        dyr_ref[...] = (dm * sr).astype(bf16)
        dmga_ref[...] = (dm * ya * sa * (1.0 - sa)).astype(bf16)
        dmgr_ref[...] = (dm * yr * sr * (1.0 - sr)).astype(bf16)

    return pl.pallas_call(
        body, name="out_bwd", grid=(t // TM,),
        in_specs=[_tile(TM, D), _tile(TM, D, 3), _tile(TM, D, 4), _tile(TM, D), _tile(TM, D), _whole((D, D))],
        out_specs=[_tile(TM, D)] * 4,
        out_shape=[jax.ShapeDtypeStruct((t, D), bf16)] * 4,
        compiler_params=_params(("parallel",), VMEM_LIMIT),
    )(do, rest, rest, ya, yr, w_out)


def _branch_bwd(name, dyb, rest, gate_cb, act, w, act_grad_dtype, head_sums=False):
    t = dyb.shape[0]

    def body(dy_ref, g_ref, act_ref, w_ref, dact_ref, dg_ref, *delta_ref):
        dp = _dot_nt(dy_ref[...], w_ref[...])
        g = g_ref[...].astype(f32)
        sg = _sig(g)
        act = act_ref[...].astype(f32)
        dact = (dp * (g * sg)).astype(act_grad_dtype)
        dact_ref[...] = dact
        dg_ref[...] = (dp * act * (sg * (1.0 + g * (1.0 - sg)))).astype(bf16)
        if head_sums:
            ch = lax.broadcasted_iota(jnp.int32, (D, LANES), 0)
            hd = lax.broadcasted_iota(jnp.int32, (D, LANES), 1)
            pick = (ch // 64 == hd).astype(bf16)
            per_head = sum(_dot(piece, pick) for piece in _split3(dact.astype(f32) * act))
            for s in range(TM // TQ):
                delta_ref[0][s] = per_head[s * TQ:(s + 1) * TQ].T[:HEADS, :]

    out_specs = [_tile(TM, D), _tile(TM, D)]
    out_shape = [jax.ShapeDtypeStruct((t, D), act_grad_dtype), jax.ShapeDtypeStruct((t, D), bf16)]
    if head_sums:
        out_specs.append(pl.BlockSpec((TM // TQ, HEADS, TQ), lambda i: (i, 0, 0)))
        out_shape.append(jax.ShapeDtypeStruct((t // TQ, HEADS, TQ), f32))
    return pl.pallas_call(
        body, name=name, grid=(t // TM,),
        in_specs=[_tile(TM, D), _tile(TM, D, gate_cb), _tile(TM, D), _whole((D, D))],
        out_specs=out_specs, out_shape=out_shape,
        compiler_params=_params(("parallel",), VMEM_LIMIT),
    )(dyb, rest, act, w)


def _rnn_bwd(dh, a, h, xc, rest, conv_w, conv_b, wa_d, wx_d, ba, bx, lam, seq):
    t = dh.shape[0]
    nb, nt = t // seq, seq // TL
    diag = (D // LANES, LANES, LANES)

    def body(dh_ref, a_ref, h_ref, hprev_ref, xc_ref, x_ref, xprev_ref, cw_ref, cb_ref, wa_ref, wx_ref,
             ba_ref, bx_ref, lam_ref, dxr_ref, dwa_ref, dwx_ref, vec_ref, g_scr, dxc_scr, dxr_scr, qcarry, dxc_next):
        b, tt = pl.program_id(0), pl.program_id(1)
        rt = nt - 1 - tt

        @pl.when((b == 0) & (tt == 0))
        def _():
            dwa_ref[...] = jnp.zeros_like(dwa_ref)
            dwx_ref[...] = jnp.zeros_like(dwx_ref)
            vec_ref[...] = jnp.zeros_like(vec_ref)

        @pl.when(tt == 0)
        def _():
            qcarry[...] = jnp.zeros_like(qcarry)
            dxc_next[...] = jnp.zeros_like(dxc_next)

        g_scr[...] = dh_ref[...].astype(f32)

        def step(k, q):
            s = TL - 1 - k
            g = g_scr[pl.ds(s, 1), :] + q
            g_scr[pl.ds(s, 1), :] = g
            return a_ref[pl.ds(s, 1), :] * g

        qcarry[...] = lax.fori_loop(0, TL, step, qcarry[...], unroll=8)

        row = lax.broadcasted_iota(jnp.int32, (TL, D), 0)
        row8 = lax.broadcasted_iota(jnp.int32, (8, D), 0)
        g = g_scr[...]
        av = a_ref[...]
        xc = xc_ref[...].astype(f32)
        hlast = jnp.where(rt > 0, hprev_ref[...].astype(f32)[PREV_ROWS - 1:], 0.0)
        hp = jnp.where(row == 0, hlast, pltpu.roll(h_ref[...].astype(f32), 1, 0))
        r, i, sp, log_a, w1, sq = _rnn_gates(xc, wa_ref, wx_ref, ba_ref, bx_ref, lam_ref)
        dix = g * sq
        di = dix * xc
        dxc = dix * i
        dsq = g * (i * xc)
        dlog_a = g * hp * av - dsq * jnp.where(sq > 0.0, (1.0 - w1) / sq, 0.0)
        dpr = (dlog_a * ((-RG_C) * sp)) * r * (1.0 - r)
        dpi = di * i * (1.0 - i)
        dprb, dpib, xcb = dpr.astype(bf16), dpi.astype(bf16), xc.astype(bf16)
        dxc = dxc + _dot_nt(dprb, wa_ref[...]) + _dot_nt(dpib, wx_ref[...])
        for j in range(D // LANES):
            cols = slice(j * LANES, (j + 1) * LANES)
            dwa_ref[j] += _dot_tn(xcb[:, cols], dprb[:, cols])
            dwx_ref[j] += _dot_tn(xcb[:, cols], dpib[:, cols])
        vec_ref[pl.ds(0, 1), :] += jnp.sum(dpr, axis=0, keepdims=True)
        vec_ref[pl.ds(1, 1), :] += jnp.sum(dpi, axis=0, keepdims=True)
        dsp = jnp.sum(dlog_a * ((-RG_C) * r), axis=0, keepdims=True)
        vec_ref[pl.ds(2, 1), :] += dsp * (-_sig(-lam_ref[...]))
        vec_ref[pl.ds(3, 1), :] += jnp.sum(dxc, axis=0, keepdims=True)

        dxc_scr[...] = dxc
        bot8 = dxc_scr[pl.ds(TL - 8, 8), :]
        nxt8 = dxc_next[...]
        dxr = cw_ref[pl.ds(3, 1), :] * dxc
        dxr8 = cw_ref[pl.ds(3, 1), :] * bot8
        for sh in range(1, 4):
            w = cw_ref[pl.ds(3 - sh, 1), :]
            dxr = dxr + w * pltpu.roll(dxc, TL - sh, 0)
            dxr8 = dxr8 + w * jnp.where(row8 < 8 - sh, pltpu.roll(bot8, 8 - sh, 0), pltpu.roll(nxt8, 8 - sh, 0))
        dxr_scr[...] = dxr
        dxr_scr[pl.ds(TL - 8, 8), :] = dxr8
        dxr_ref[...] = dxr_scr[...].astype(bf16)
        dxc_next[...] = dxc_scr[pl.ds(0, 8), :]

        x = x_ref[...].astype(f32)
        prev8 = jnp.where(rt > 0, xprev_ref[...].astype(f32)[PREV_ROWS - 8:], 0.0)
        dxc_top8 = dxc_scr[pl.ds(0, 8), :]
        vec_ref[pl.ds(7, 1), :] += jnp.sum(dxc * x, axis=0, keepdims=True)
        for sh in range(1, 4):
            inside = jnp.sum(dxc * jnp.where(row >= sh, pltpu.roll(x, sh, 0), 0.0), axis=0, keepdims=True)
            above = jnp.sum(dxc_top8 * jnp.where(row8 < sh, pltpu.roll(prev8, sh, 0), 0.0), axis=0, keepdims=True)
            vec_ref[pl.ds(7 - sh, 1), :] += inside + above

    tile = lambda cb: pl.BlockSpec((TL, D), lambda b, tt, cb=cb: (b * nt + nt - 1 - tt, cb))
    prev = lambda cb: pl.BlockSpec(
        (PREV_ROWS, D), lambda b, tt, cb=cb: (jnp.maximum((b * nt + nt - 1 - tt) * (TL // PREV_ROWS) - 1, 0), cb))
    vec = _whole((1, D))
    return pl.pallas_call(
        body, name="rnn_bwd", grid=(nb, nt),
        in_specs=[tile(0), tile(0), tile(0), prev(0), tile(0), tile(1), prev(1),
                  _whole((4, D)), vec, _whole((D, D)), _whole((D, D)), vec, vec, vec],
        out_specs=[tile(0), _whole(diag), _whole(diag), _whole((8, D))],
        out_shape=[jax.ShapeDtypeStruct((t, D), bf16), jax.ShapeDtypeStruct(diag, f32),
                   jax.ShapeDtypeStruct(diag, f32), jax.ShapeDtypeStruct((8, D), f32)],
        scratch_shapes=[pltpu.VMEM((TL, D), f32), pltpu.VMEM((TL, D), f32), pltpu.VMEM((TL, D), f32),
                        pltpu.VMEM((1, D), f32), pltpu.VMEM((8, D), f32)],
        compiler_params=_params(("arbitrary", "arbitrary"), VMEM_LIMIT),
    )(dh, a, h, h, xc, rest, rest, conv_w, conv_b, wa_d, wx_d, ba, bx, lam)


def _attn_bwd(qa, ka, qkv, doa, lse, delta, seq):
    t = qkv.shape[0]
    nb, nq = t // seq, seq // TQ
    hg = ATT_GROUP
    ng, npair = HEADS // hg, hg // 2

    def body(qa_ref, ka_ref, q_ref, k_ref, v_ref, do_ref, lse_ref, dl_ref, dq_ref, dk_ref, dv_ref, dc_ref,
             dqt_scr, dk_scr, dv_scr, ds_scr, kht_scr):
        gi, kt = pl.program_id(1), pl.program_id(2)
        lane = lax.broadcasted_iota(jnp.int32, (1, LANES), 1)
        krow = lax.broadcasted_iota(jnp.int32, (TQ, TQ), 0)
        qcol = lax.broadcasted_iota(jnp.int32, (TQ, TQ), 1)
        lmask = [(lane // 64) == hh for hh in range(2)]
        scale = jnp.asarray(QK_SCALE, bf16)

        @pl.when(kt == 0)
        def _():
            dqt_scr[...] = jnp.zeros_like(dqt_scr)

        dk_scr[...] = jnp.zeros_like(dk_scr)
        dv_scr[...] = jnp.zeros_like(dv_scr)
        ds_scr[...] = jnp.zeros_like(ds_scr)
        for g in range(hg):
            k2 = k_ref[:, pl.ds((g // 2) * LANES, LANES)]
            kht_scr[g] = jnp.where(lmask[g % 2], k2, jnp.zeros_like(k2)).T

        def q_step(qt, masked):
            qs = pl.multiple_of(qt * TQ, TQ)
            heads = range(hg)
            do2 = [do_ref[pl.ds(qs, TQ), pl.ds(j * LANES, LANES)] for j in range(npair)]
            q2 = [q_ref[pl.ds(qs, TQ), pl.ds(j * LANES, LANES)] for j in range(npair)]
            doh = [jnp.where(lmask[g % 2], do2[g // 2], jnp.zeros_like(do2[0])) for g in heads]
            qh = [jnp.where(lmask[g % 2], q2[g // 2], jnp.zeros_like(q2[0])) * scale for g in heads]
            st = [_dot_nt(ka_ref[:, pl.ds(g * LANES, LANES)], qa_ref[pl.ds(qs, TQ), pl.ds(g * LANES, LANES)])
                  for g in heads]
            if masked:
                st = [jnp.where(krow <= qcol, s, MASK_VALUE) for s in st]
            dp = [_dot_nt(v_ref[:, pl.ds((g // 2) * LANES, LANES)], doh[g]) for g in heads]
            p = [jnp.exp(st[g] - lse_ref[qt, pl.ds(hg * gi + g, 1), :]) for g in heads]
            ds = [p[g] * (dp[g] - dl_ref[qt, pl.ds(hg * gi + g, 1), :]) for g in heads]
            pb = [x.astype(bf16) for x in p]
            dsb = [x.astype(bf16) for x in ds]
            for j in range(npair):
                a, b = 2 * j, 2 * j + 1
                dv_scr[j] += _dot(pb[a], doh[a]) + _dot(pb[b], doh[b])
                dk_scr[j] += _dot(dsb[a], qh[a]) + _dot(dsb[b], qh[b])
                dqt_scr[qt, j] += (_dot(kht_scr[a], dsb[a]) + _dot(kht_scr[b], dsb[b])) * QK_SCALE
            for g in heads:
                ds_scr[g] += ds[g][:, :LANES] + ds[g][:, LANES:]

        q_step(kt, True)

        def loop_body(qt, carry):
            q_step(qt, False)
            return carry

        lax.fori_loop(kt + 1, nq, loop_body, 0)

        dc = jnp.zeros((TQ, LANES), f32)
        for g in range(hg):
            dc = jnp.where(lane == g, -jnp.sum(ds_scr[g], axis=1, keepdims=True), dc)
        dc_ref[...] = dc
        for j in range(npair):
            dk_ref[:, pl.ds(j * LANES, LANES)] = dk_scr[j].astype(bf16)
            dv_ref[:, pl.ds(j * LANES, LANES)] = dv_scr[j].astype(bf16)

        @pl.when(kt == nq - 1)
        def _():
            for qt in range(nq):
                for j in range(npair):
                    dq_ref[pl.ds(qt * TQ, TQ), pl.ds(j * LANES, LANES)] = dqt_scr[qt, j].T.astype(bf16)

    vw = hg * 64
    seqspec = pl.BlockSpec((seq, vw), lambda b, gi, kt: (b, gi))
    kspec = lambda off: pl.BlockSpec((TQ, vw), lambda b, gi, kt: (b * nq + kt, off + gi))
    rowspec = pl.BlockSpec((nq, HEADS, TQ), lambda b, gi, kt: (b, 0, 0))
    return pl.pallas_call(
        body, name="attn_bwd", grid=(nb, ng, nq),
        in_specs=[pl.BlockSpec((seq, hg * LANES), lambda b, gi, kt: (b, gi)),
                  pl.BlockSpec((TQ, hg * LANES), lambda b, gi, kt: (b * nq + kt, gi)),
                  seqspec, kspec(ng), kspec(2 * ng), seqspec, rowspec, rowspec],
        out_specs=[seqspec, kspec(0), kspec(0), pl.BlockSpec((TQ, LANES), lambda b, gi, kt: (b * nq + kt, gi))],
        out_shape=[jax.ShapeDtypeStruct((t, D), bf16)] * 3 + [jax.ShapeDtypeStruct((t, ng * LANES), f32)],
        scratch_shapes=[pltpu.VMEM((nq, npair, LANES, TQ), f32), pltpu.VMEM((npair, TQ, LANES), f32),
                        pltpu.VMEM((npair, TQ, LANES), f32), pltpu.VMEM((hg, TQ, LANES), f32),
                        pltpu.VMEM((hg, LANES, TQ), bf16)],
        compiler_params=_params(("parallel", "parallel", "arbitrary"), VMEM_LIMIT),
    )(qa, ka, qkv, qkv, qkv, doa, lse, delta)


def _forget_bwd(dc, f128, seq):
    t = f128.shape[0]
    nb = seq // LANES

    def body(dc_ref, f_ref, df_ref, dbf_ref):
        @pl.when(pl.program_id(0) == 0)
        def _():
            dbf_ref[...] = jnp.zeros_like(dbf_ref)

        r = lax.broadcasted_iota(jnp.int32, (LANES, LANES), 0)
        cidx = lax.broadcasted_iota(jnp.int32, (LANES, LANES), 1)
        tri = (r <= cidx).astype(f32)
        carry = jnp.zeros((1, LANES), f32)
        total = jnp.zeros((1, LANES), f32)
        for blk in reversed(range(nb)):
            dcb = dc_ref[pl.ds(blk * LANES, LANES), :]
            dlf = jnp.dot(tri, dcb, preferred_element_type=f32, precision=lax.Precision.HIGHEST) + carry
            df = dlf * _sig(-f_ref[pl.ds(blk * LANES, LANES), :])
            df_ref[pl.ds(blk * LANES, LANES), :] = df.astype(bf16)
            total = total + jnp.sum(df, axis=0, keepdims=True)
            carry = carry + jnp.sum(dcb, axis=0, keepdims=True)
        dbf_ref[...] += total

    return pl.pallas_call(
        body, name="forget_bwd", grid=(t // seq,),
        in_specs=[pl.BlockSpec((seq, LANES), lambda b: (b, 0)), pl.BlockSpec((seq, LANES), lambda b: (b, 0))],
        out_specs=[pl.BlockSpec((seq, LANES), lambda b: (b, 0)), _whole((1, LANES))],
        out_shape=[jax.ShapeDtypeStruct((t, LANES), bf16), jax.ShapeDtypeStruct((1, LANES), f32)],
        compiler_params=_params(("arbitrary",)),
    )(dc, f128)


def _in_bwd(dz, df, x, dy, w_qkv, w_rest, w_f, w_pre):
    t = x.shape[0]
    n_qkv = w_qkv.shape[0] // D
    n_rest = w_rest.shape[0] // D

    def body(*refs):
        dz_refs = refs[:n_qkv + n_rest]
        df_ref, x_ref, dy_ref, wq_ref, wr_ref, wf_ref, wp_ref, gx_ref, dwp_ref = refs[n_qkv + n_rest:]

        @pl.when(pl.program_id(0) == 0)
        def _():
            dwp_ref[...] = jnp.zeros_like(dwp_ref)

        dh = _dot(df_ref[...], wf_ref[...])
        for p in range(n_qkv):
            dh = dh + _dot(dz_refs[p][...], wq_ref[pl.ds(p * D, D), :])
        for p in range(n_rest):
            dh = dh + _dot(dz_refs[n_qkv + p][...], wr_ref[pl.ds(p * D, D), :])
        xv = x_ref[...]
        r1 = lax.rsqrt(jnp.mean(xv * xv, axis=-1, keepdims=True) + NORM_EPS)
        xh = xv * r1
        dwp_ref[...] += jnp.sum(dh * xh, axis=0, keepdims=True)
        dxh = dh * wp_ref[...]
        gx_ref[...] = dy_ref[...] + r1 * (dxh - xh * jnp.mean(dxh * xh, axis=-1, keepdims=True))

    once = lambda shape: pl.BlockSpec(shape, lambda i: (0, 0), pipeline_mode=pl.Buffered(1))
    return pl.pallas_call(
        body, name="in_bwd", grid=(t // TM,),
        in_specs=[_tile(TM, D)] * (n_qkv + n_rest) + [_tile(TM, LANES), _tile(TM, D), _tile(TM, D),
                  once(w_qkv.shape), once(w_rest.shape), once(w_f.shape), _whole((1, D))],
        out_specs=[_tile(TM, D), _whole((1, D))],
        out_shape=[jax.ShapeDtypeStruct((t, D), f32), jax.ShapeDtypeStruct((1, D), f32)],
        compiler_params=_params(("arbitrary",), VMEM_LIMIT),
    )(*dz, df, x, dy, w_qkv, w_rest, w_f, w_pre)


def _tn_mm(name, a, b, tn, tk=2048):
    t, k = a.shape
    tk = min(tk, t)
    n = b.shape[1]

    def body(a_ref, b_ref, o_ref, s_ref):
        j, kk = pl.program_id(0), pl.program_id(1)

        @pl.when(kk == 0)
        def _():
            o_ref[...] = jnp.zeros_like(o_ref)

        @pl.when((j == 0) & (kk == 0))
        def _():
            s_ref[...] = jnp.zeros_like(s_ref)

        av = a_ref[...]
        o_ref[...] += _dot_tn(av, b_ref[...])

        @pl.when(j == 0)
        def _():
            s_ref[...] += jnp.sum(av.astype(f32), axis=0, keepdims=True)

    return pl.pallas_call(
        body, name=name, grid=(n // tn, t // tk),
        in_specs=[pl.BlockSpec((tk, k), lambda j, kk: (kk, 0)), pl.BlockSpec((tk, tn), lambda j, kk: (kk, j))],
        out_specs=[pl.BlockSpec((k, tn), lambda j, kk: (0, j)), _whole((1, k))],
        out_shape=[jax.ShapeDtypeStruct((k, n), f32), jax.ShapeDtypeStruct((1, k), f32)],
        compiler_params=_params(("arbitrary", "arbitrary"), VMEM_LIMIT),
    )(a, b)


def _position():
    return lax.axis_index("x"), lax.axis_index("y"), lax.axis_index("c")


def _gather_shards(parts, small):
    n = len(parts)
    halves = [p.shape[0] // 2 for p in parts]
    cuts = [-(-h // 32) * 16 for h in halves]
    n_direct, n_relay, n_sib = 4 * n, 2 * n, 6 * n

    def body(*refs):
        srcs, small_src = refs[:n], refs[n]
        dsts, small_dst = refs[n + 1:2 * n + 1], refs[2 * n + 1]
        send, recv, local = refs[2 * n + 2:]
        x, y, c = _position()
        me = 2 * x + y
        chips = [(1 - x, y), (x, 1 - y), (1 - x, 1 - y)]
        ids = [2 * px + py for px, py in chips]

        def rows(a, half, quarter):
            start = half * halves[a] + (cuts[a] if quarter else 0)
            return pl.ds(start, halves[a] - cuts[a] if quarter else cuts[a])

        def landing(a, shard, half, quarter):
            return dsts[a].at[shard, rows(a, half, quarter), :]

        def direct(a, nb, quarter, shard):
            k = (a * 2 + nb) * 2 + quarter
            px, py = chips[nb]
            return pltpu.make_async_remote_copy(
                src_ref=srcs[a].at[rows(a, c, quarter), :], dst_ref=landing(a, shard, c, quarter),
                send_sem=send.at[k], recv_sem=recv.at[k], device_id=(px, py, c), device_id_type=MESH)

        def relay(a, quarter, shard):
            k = n_direct + a * 2 + quarter
            px, py = chips[1 - quarter]
            return pltpu.make_async_remote_copy(
                src_ref=landing(a, shard, c, quarter), dst_ref=landing(a, shard, c, quarter),
                send_sem=send.at[k], recv_sem=recv.at[k], device_id=(px, py, c), device_id_type=MESH)

        def to_sibling(a, origin, quarter, half):
            k = n_direct + n_relay + (a * 3 + origin) * 2 + quarter
            return pltpu.make_async_remote_copy(
                src_ref=landing(a, ids[origin], half, quarter), dst_ref=landing(a, ids[origin], half, quarter),
                send_sem=send.at[k], recv_sem=recv.at[k], device_id=(x, y, 1 - c), device_id_type=MESH)

        def small_copy(j, shard):
            k = n_direct + n_relay + n_sib + j
            px, py = chips[j]
            return pltpu.make_async_remote_copy(
                src_ref=small_src, dst_ref=small_dst.at[shard], send_sem=send.at[k], recv_sem=recv.at[k],
                device_id=(px, py, c), device_id_type=MESH)

        own = [pltpu.make_async_copy(srcs[a], dsts[a].at[me], local.at[a]) for a in range(n)]
        own.append(pltpu.make_async_copy(small_src, small_dst.at[me], local.at[n]))
        for cp in own:
            cp.start()
        sent = [direct(a, nb, q, me) for q in range(2) for a in range(n) for nb in range(2)]
        sent += [small_copy(j, me) for j in range(3)]
        for cp in sent:
            cp.start()

        def passed_on(cp):
            cp.start()
            sent.append(cp)

        for q in range(2):
            for a in range(n):
                for nb in range(2):
                    direct(a, nb, q, ids[nb]).wait_recv()
                    passed_on(to_sibling(a, nb, q, c))
                    if nb == q:
                        passed_on(relay(a, q, ids[nb]))
        for a in range(n):
            for q in range(2):
                relay(a, q, ids[2]).wait_recv()
                passed_on(to_sibling(a, 2, q, c))
        for j in range(3):
            small_copy(j, ids[j]).wait_recv()
            for a in range(n):
                for q in range(2):
                    to_sibling(a, j, q, 1 - c).wait_recv()
        for cp in sent:
            cp.wait_send()
        for cp in own:
            cp.wait()

    vm = pl.BlockSpec(memory_space=pltpu.VMEM)
    n_sems = n_direct + n_relay + n_sib + 3
    return pl.pallas_call(
        body, name="gather_shards",
        in_specs=[vm] * (n + 1), out_specs=[vm] * (n + 1),
        out_shape=[jax.ShapeDtypeStruct((N_CHIPS,) + p.shape, p.dtype) for p in parts + [small]],
        scratch_shapes=[pltpu.SemaphoreType.DMA((n_sems,)), pltpu.SemaphoreType.DMA((n_sems,)),
                        pltpu.SemaphoreType.DMA((n + 1,))],
        compiler_params=pltpu.CompilerParams(vmem_limit_bytes=VMEM_LIMIT),
    )(*parts, small)


def _allsum_rows(part):
    rows_n = part.shape[0]

    def body(x_ref, gath_ref, sum_ref, send_sems, recv_sems, local_sem):
        x, y, c = _position()
        me, sibling = (x, y, c), (x, y, 1 - c)
        chips = [(1 - x, y), (x, 1 - y), (1 - x, 1 - y)]

        def rows(px, py, pc):
            return gath_ref.at[pl.ds((4 * px + 2 * py + pc) * rows_n, rows_n), :]

        def copy(k, block, to, src=None):
            return pltpu.make_async_remote_copy(
                src_ref=rows(*block) if src is None else src, dst_ref=rows(*block),
                send_sem=send_sems.at[k], recv_sem=recv_sems.at[k], device_id=to, device_id_type=MESH)

        mine = pltpu.make_async_copy(x_ref, rows(*me), local_sem)
        mine.start()
        first = [copy(0, me, sibling, src=x_ref)]
        first += [copy(1 + j, me, (*chip, c), src=x_ref) for j, chip in enumerate(chips)]
        for cp in first:
            cp.start()
        passed = [copy(4 + j, (*chip, c), sibling) for j, chip in enumerate(chips)]
        for j, chip in enumerate(chips):
            copy(1 + j, (*chip, c), me).wait_recv()
            passed[j].start()
        copy(0, sibling, me).wait_recv()
        for j, chip in enumerate(chips):
            copy(4 + j, (*chip, 1 - c), me).wait_recv()
        for cp in first + passed:
            cp.wait_send()
        mine.wait()
        total = gath_ref[pl.ds(0, rows_n), :]
        for d in range(1, N_DEV):
            total = total + gath_ref[pl.ds(d * rows_n, rows_n), :]
        sum_ref[...] = total

    vm = pl.BlockSpec(memory_space=pltpu.VMEM)
    return pl.pallas_call(
        body, name="allsum_rows", in_specs=[vm], out_specs=[vm, vm],
        out_shape=[jax.ShapeDtypeStruct((N_DEV * rows_n, D), f32), jax.ShapeDtypeStruct((rows_n, D), f32)],
        scratch_shapes=[pltpu.SemaphoreType.DMA((7,)), pltpu.SemaphoreType.DMA((7,)), pltpu.SemaphoreType.DMA],
    )(part)[1]


PAIR_ROWS = 16


def _pair_reduce(name, pieces):
    _, r, n = pieces.shape

    def body(p_ref, o_ref, land, send, recv):
        x, y, c = _position()

        def remote(j, half):
            return pltpu.make_async_remote_copy(
                src_ref=p_ref.at[2 * j + half], dst_ref=land.at[j], send_sem=send.at[j], recv_sem=recv.at[j],
                device_id=(x, y, 1 - c), device_id_type=MESH)

        sends = [remote(j, 1 - c) for j in range(N_CHIPS)]
        for cp in sends:
            cp.start()
        for j in range(N_CHIPS):
            remote(j, c).wait_recv()

            def add_rows(i, carry, j=j):
                rows = pl.ds(pl.multiple_of(i * PAIR_ROWS, PAIR_ROWS), PAIR_ROWS)
                o_ref[j, rows, :] = (p_ref[2 * j + c, rows, :].astype(f32) + land[j, rows, :].astype(f32)).astype(bf16)
                return carry

            lax.fori_loop(0, r // PAIR_ROWS, add_rows, 0)
        for cp in sends:
            cp.wait_send()

    vm = pl.BlockSpec(memory_space=pltpu.VMEM)
    return pl.pallas_call(
        body, name=name, in_specs=[vm], out_specs=vm,
        out_shape=jax.ShapeDtypeStruct((N_CHIPS, r, n), bf16),
        scratch_shapes=[pltpu.VMEM((N_CHIPS, r, n), bf16), pltpu.SemaphoreType.DMA((N_CHIPS,)),
                        pltpu.SemaphoreType.DMA((N_CHIPS,))],
        compiler_params=pltpu.CompilerParams(vmem_limit_bytes=VMEM_LIMIT),
    )(pieces)


def _chip_exchange(arrs):
    n = len(arrs)
    heights = [a.shape[1] for a in arrs]
    cuts = [-(-r // 32) * 16 for r in heights]

    def body(*refs):
        srcs, dsts, relays = refs[:n], refs[n:2 * n], refs[2 * n:3 * n]
        send, recv, local = refs[3 * n:]
        x, y, c = _position()
        me = 2 * x + y
        chips = [(1 - x, y), (x, 1 - y), (1 - x, 1 - y)]
        ids = [2 * px + py for px, py in chips]

        def rows(a, quarter):
            return pl.ds(cuts[a], heights[a] - cuts[a]) if quarter else pl.ds(0, cuts[a])

        def held(a, quarter):
            size = heights[a] - cuts[a] if quarter else cuts[a]
            return relays[a].at[quarter, pl.ds(0, size), :]

        def direct(a, nb, piece, landing):
            px, py = chips[nb]
            return pltpu.make_async_remote_copy(
                src_ref=srcs[a].at[piece], dst_ref=dsts[a].at[landing], send_sem=send.at[a * 2 + nb],
                recv_sem=recv.at[a * 2 + nb], device_id=(px, py, c), device_id_type=MESH)

        def first_hop(a, quarter):
            k = 2 * n + a * 2 + quarter
            px, py = chips[quarter]
            return pltpu.make_async_remote_copy(
                src_ref=srcs[a].at[ids[2], rows(a, quarter), :], dst_ref=held(a, quarter), send_sem=send.at[k],
                recv_sem=recv.at[k], device_id=(px, py, c), device_id_type=MESH)

        def second_hop(a, quarter, origin):
            k = 4 * n + a * 2 + quarter
            px, py = chips[1 - quarter]
            return pltpu.make_async_remote_copy(
                src_ref=held(a, quarter), dst_ref=dsts[a].at[origin, rows(a, quarter), :], send_sem=send.at[k],
                recv_sem=recv.at[k], device_id=(px, py, c), device_id_type=MESH)

        own = [pltpu.make_async_copy(srcs[a].at[me], dsts[a].at[me], local.at[a]) for a in range(n)]
        sent = [first_hop(a, q) for a in range(n) for q in range(2)]
        sent += [direct(a, nb, ids[nb], me) for a in range(n) for nb in range(2)]
        for cp in sent + own:
            cp.start()
        for a in range(n):
            for q in range(2):
                first_hop(a, q).wait_recv()
                sent.append(second_hop(a, q, ids[q]))
                sent[-1].start()
        for a in range(n):
            for nb in range(2):
                direct(a, nb, me, ids[nb]).wait_recv()
            for q in range(2):
                second_hop(a, q, ids[2]).wait_recv()
        for cp in sent:
            cp.wait_send()
        for cp in own:
            cp.wait()

    anyspec = pl.BlockSpec(memory_space=pl.ANY)
    out = pl.pallas_call(
        body, name="chip_exchange", in_specs=[anyspec] * n, out_specs=[anyspec] * (2 * n),
        out_shape=[jax.ShapeDtypeStruct(a.shape, a.dtype) for a in arrs]
        + [jax.ShapeDtypeStruct((2, cut, a.shape[2]), a.dtype) for a, cut in zip(arrs, cuts)],
        scratch_shapes=[pltpu.SemaphoreType.DMA((6 * n,)), pltpu.SemaphoreType.DMA((6 * n,)),
                        pltpu.SemaphoreType.DMA((n,))],
    )(*arrs)
    return out[:n]


def _swap_halves(arrs):
    n = len(arrs)

    def body(*refs):
        srcs, dsts = refs[:n], refs[n:2 * n]
        send, recv, local = refs[2 * n:]
        x, y, c = _position()

        def remote(a, landing):
            return pltpu.make_async_remote_copy(
                src_ref=srcs[a], dst_ref=dsts[a].at[landing], send_sem=send.at[a], recv_sem=recv.at[a],
                device_id=(x, y, 1 - c), device_id_type=MESH)

        own = [pltpu.make_async_copy(srcs[a], dsts[a].at[c], local.at[a]) for a in range(n)]
        sends = [remote(a, c) for a in range(n)]
        for cp in sends + own:
            cp.start()
        for a in range(n):
            remote(a, 1 - c).wait_recv()
        for cp in sends:
            cp.wait_send()
        for cp in own:
            cp.wait()

    vm = pl.BlockSpec(memory_space=pltpu.VMEM)
    return pl.pallas_call(
        body, name="swap_halves", in_specs=[vm] * n, out_specs=[vm] * n,
        out_shape=[jax.ShapeDtypeStruct((2,) + a.shape, a.dtype) for a in arrs],
        scratch_shapes=[pltpu.SemaphoreType.DMA((n,)), pltpu.SemaphoreType.DMA((n,)), pltpu.SemaphoreType.DMA((n,))],
        compiler_params=pltpu.CompilerParams(vmem_limit_bytes=VMEM_LIMIT),
    )(*arrs)


def _row_block(r):
    return 128 if r % 128 == 0 else r


def _sum_slots(name, slots):
    s, r, n = slots.shape
    rb = _row_block(r)

    def body(s_ref, o_ref):
        total = s_ref[0].astype(f32)
        for d in range(1, s):
            total = total + s_ref[d].astype(f32)
        o_ref[...] = total

    return pl.pallas_call(
        body, name=name, grid=(r // rb,),
        in_specs=[pl.BlockSpec((s, rb, n), lambda i: (0, i, 0))],
        out_specs=pl.BlockSpec((rb, n), lambda i: (i, 0)),
        out_shape=jax.ShapeDtypeStruct((r, n), f32),
        compiler_params=_params(("parallel",), VMEM_LIMIT),
    )(slots)


def _adamw(name, w, g, m, v):
    r, n = w.shape
    if r % 128 == 0 or r * n <= 128 * 1024:
        rb, nb = _row_block(r), n
    else:
        rb, nb = r, LANES

    def body(w_ref, g_ref, m_ref, v_ref, d_ref, nm_ref, nv_ref):
        gv = g_ref[...]
        m2 = ADAM_B1 * m_ref[...] + (1.0 - ADAM_B1) * gv
        v2 = ADAM_B2 * v_ref[...] + (1.0 - ADAM_B2) * (gv * gv)
        m_hat = m2 / (1.0 - ADAM_B1 ** ADAM_STEP)
        v_hat = v2 / (1.0 - ADAM_B2 ** ADAM_STEP)
        d_ref[...] = (-ADAM_LR) * (m_hat / (jnp.sqrt(v_hat) + ADAM_EPS) + ADAM_WD * w_ref[...])
        nm_ref[...] = m2
        nv_ref[...] = v2

    spec = pl.BlockSpec((rb, nb), lambda i, j: (i, j))
    return pl.pallas_call(
        body, name=name, grid=(r // rb, n // nb), in_specs=[spec] * 4, out_specs=[spec] * 3,
        out_shape=[jax.ShapeDtypeStruct((r, n), f32)] * 3,
        compiler_params=_params(("parallel", "parallel"), VMEM_LIMIT),
    )(w, g, m, v)


def _local_step(x2, tgt2, seq, wt):
    nb = x2.shape[0] // seq
    h, qkv = _norm_qkv(x2, wt["pre_w"], wt["w_qkv"], wt["b_qkv"])
    rest = _mm("in_rest", h, wt["w_rest"], wt["b_rest"], bf16, 1024, 1024, w_is_nk=True)
    f128 = _mm("in_f", h, wt["w_f"], wt["b_f"], f32, 1024, LANES, w_is_nk=True)
    c = _forget_prep(f128, seq)
    qa, ka = _attn_prep(qkv, c)
    o_att, pa, lse = _attn_fwd(qa, ka, qkv, rest, seq)
    ya = _mm("proj_a", pa, wt["w_a"], None, bf16, 1024, D)
    rnn_w = (wt["conv_w"], wt["conv_b"], wt["wa_d"], wt["wx_d"], wt["ba"], wt["bx"], wt["lam"])
    xc, a, hrec, pr = _rnn_fwd(rest, *rnn_w, seq)
    yr = _mm("proj_r", pr, wt["w_r"], None, bf16, 1024, D)
    do, dy, mrg, loss8, d_post = _out_proj_loss(rest, ya, yr, wt["w_o"], x2, tgt2, wt["post_w"])
    dya, dyr, dmga, dmgr = _out_bwd(do, rest, ya, yr, wt["w_o"])
    doa, dga, delta = _branch_bwd("branch_a_bwd", dya, rest, 0, o_att, wt["w_a"], bf16, head_sums=True)
    dhrec, dgr = _branch_bwd("branch_r_bwd", dyr, rest, 2, hrec, wt["w_r"], bf16)
    d_wo, _ = _tn_mm("dw_out", mrg, do, D)
    d_wa, _ = _tn_mm("dw_branch_a", pa, dya, D)
    d_wr, _ = _tn_mm("dw_branch_r", pr, dyr, D)
    dxr, d_wad, d_wxd, vec = _rnn_bwd(dhrec, a, hrec, xc, rest, *rnn_w, seq)
    dq, dk, dv, dc_pairs = _attn_bwd(qa, ka, qkv, doa, lse, delta, seq)
    dc = dc_pairs.reshape(-1, HEADS // ATT_GROUP, LANES)[:, :, :ATT_GROUP].reshape(-1, HEADS)
    df, db_f = _forget_bwd(_pad_cols(dc, LANES), f128, seq)
    pieces = [dq, dk, dv, dga, dxr, dgr, dmga, dmgr]
    gx, d_pre = _in_bwd(pieces, df, x2, dy, wt["w_qkv"], wt["w_rest"], wt["w_f"], wt["pre_w"])
    names = ["q", "k", "v", "ga", "xr", "gr", "mga", "mgr"]
    dws, dbs = [], []
    for nm, piece in zip(names, pieces):
        dw_p, db_p = _tn_mm("dw_in_" + nm, piece, h, D)
        dws.append(dw_p)
        dbs.append(db_p)
    dw_f, _ = _tn_mm("dw_in_f", df, h, D)
    zeros_w = jnp.zeros((IN_TOTAL - IN_USED, D), f32)
    d_w_in = jnp.concatenate(dws[:3] + [dw_f[:HEADS]] + dws[3:] + [zeros_w], axis=0)
    d_b_in = jnp.concatenate(dbs[:3] + [db_f[:, :HEADS]] + dbs[3:] + [zeros_w[:, :1].T], axis=1)
    return dict(loss=loss8[0, 0], grad_x=gx, pre_w=d_pre, w_in=d_w_in, b_in=d_b_in, conv_w=vec[4:8], conv_b=vec[3:4],
                wa_d=d_wad, ba=vec[0:1], wx_d=d_wxd, bx=vec[1:2], lam=vec[2:3], w_a=d_wa, w_r=d_wr, w_o=d_wo,
                post_w=d_post)


def _block_diag(w):
    g, bw, _ = w.shape
    eye = jnp.eye(g, dtype=w.dtype)
    return (w[:, :, None, :] * eye[:, None, :, None]).reshape(g * bw, g * bw)


def _gate_blocks(diag):
    half = diag.shape[1] // 2
    return jnp.stack([diag[:, :half, :half], diag[:, half:, half:]], axis=1).reshape(-1, half, half)


def _pad_cols(a, n):
    return jnp.pad(a, ((0, 0), (0, n - a.shape[1])))


def _pad_rows(a, n):
    return jnp.pad(a, ((0, n - a.shape[0]), (0, 0)))


def kernel(x, pre_norm_w, w_in, b_in, conv_w, conv_b, rg_wa, rg_ba, rg_wx, rg_bx, rg_lambda, w_branch_a, w_branch_r, w_out, post_norm_w, loss_target, m_pre_norm_w, m_w_in, m_b_in, m_conv_w, m_conv_b, m_rg_wa, m_rg_ba, m_rg_wx, m_rg_bx, m_rg_lambda, m_w_branch_a, m_w_branch_r, m_w_out, m_post_norm_w, v_pre_norm_w, v_w_in, v_b_in, v_conv_w, v_conv_b, v_rg_wa, v_rg_ba, v_rg_wx, v_rg_bx, v_rg_lambda, v_w_branch_a, v_w_branch_r, v_w_out, v_post_norm_w):
    nb, seq, _ = x.shape
    chip = 2 * lax.axis_index("x") + lax.axis_index("y")
    n_groups = rg_wa.shape[1]

    w_in_t = jnp.transpose(w_in[0])
    shard_cols = w_in_t.shape[0]
    padded = -(-shard_cols // 32) * 32
    g_in, g_a, g_r, g_o, g_cw = _gather_shards(
        [_pad_rows(w_in_t.astype(bf16), padded), w_branch_a[0].astype(bf16), w_branch_r[0].astype(bf16),
         w_out[0].astype(bf16)], conv_w[0])
    w_full = jnp.concatenate([g_in[j, :shard_cols] for j in range(N_CHIPS)], axis=0)
    q_end, f_end = 3 * D, 3 * D + HEADS
    wt = dict(
        pre_w=pre_norm_w, post_w=post_norm_w,
        w_qkv=w_full[:q_end], b_qkv=b_in[:, :q_end],
        w_f=_pad_rows(w_full[q_end:f_end], LANES), b_f=_pad_cols(b_in[:, q_end:f_end], LANES),
        w_rest=w_full[f_end:IN_USED], b_rest=b_in[:, f_end:IN_USED],
        w_a=g_a.reshape(D, D), w_r=g_r.reshape(D, D), w_o=g_o.reshape(D, D),
        conv_w=jnp.transpose(g_cw, (1, 0, 2)).reshape(4, D), conv_b=conv_b,
        wa_d=_block_diag(rg_wa[0]).astype(bf16), wx_d=_block_diag(rg_wx[0]).astype(bf16),
        ba=rg_ba, bx=rg_bx, lam=rg_lambda)

    part = _local_step(x.reshape(nb * seq, D), loss_target.reshape(nb * seq, D), seq, wt)
    loss = lax.psum(part["loss"], ("x", "y", "c"))
    grad_x = part["grad_x"].reshape(nb, seq, D)

    small = jnp.concatenate([
        part["pre_w"], _pad_cols(part["b_in"], 10 * D).reshape(10, D), part["conv_b"],
        _gate_blocks(part["wa_d"]).reshape(-1, D), part["ba"],
        _gate_blocks(part["wx_d"]).reshape(-1, D), part["bx"], part["lam"], part["post_w"],
        part["conv_w"]], axis=0)
    n_small = small.shape[0]
    n_rep = n_small - 4
    tot = _allsum_rows(_pad_rows(small, -(-n_small // 8) * 8))
    g_rep = tot[:n_rep]
    g_conv_w = lax.dynamic_slice_in_dim(tot[n_rep:n_small], chip * (D // N_CHIPS), D // N_CHIPS, axis=1)

    def unpack(p):
        o = [0]

        def take(k):
            o[0] += k
            return p[o[0] - k:o[0]]

        pre = take(1)
        b = take(10).reshape(1, 10 * D)[:, :IN_TOTAL]
        cb = take(1)
        wa = take(64).reshape(rg_wa.shape)
        ba = take(1)
        wx = take(64).reshape(rg_wx.shape)
        bx = take(1)
        lam = take(1)
        post = take(1)
        return dict(pre_norm_w=pre, b_in=b, conv_b=cb, rg_wa=wa, rg_ba=ba, rg_wx=wx, rg_bx=bx, rg_lambda=lam,
                    post_norm_w=post)

    grads = unpack(g_rep)
    replicated = dict(
        pre_norm_w=(pre_norm_w, m_pre_norm_w, v_pre_norm_w), b_in=(b_in, m_b_in, v_b_in),
        conv_b=(conv_b, m_conv_b, v_conv_b), rg_wa=(rg_wa, m_rg_wa, v_rg_wa), rg_ba=(rg_ba, m_rg_ba, v_rg_ba),
        rg_wx=(rg_wx, m_rg_wx, v_rg_wx), rg_bx=(rg_bx, m_rg_bx, v_rg_bx),
        rg_lambda=(rg_lambda, m_rg_lambda, v_rg_lambda), post_norm_w=(post_norm_w, m_post_norm_w, v_post_norm_w))
    deltas, new_m, new_v = {}, {}, {}
    for name, (w, m, v) in replicated.items():
        as2d = lambda a: a.reshape(-1, D) if a.ndim > 2 else a
        upd = _adamw("adamw_" + name, as2d(w), as2d(grads[name]), as2d(m), as2d(v))
        deltas[name], new_m[name], new_v[name] = [a.reshape(w.shape) for a in upd]

    p_in = jnp.pad(part["w_in"].reshape(N_CHIPS, shard_cols, D), ((0, 0), (0, padded - shard_cols), (0, 0)))
    p_in = p_in.reshape(N_DEV, padded // 2, D)
    p_aro = jnp.concatenate([part[k].reshape(N_DEV, D // N_DEV, D) for k in ("w_a", "w_r", "w_o")], axis=1)
    s_in, s_aro = _chip_exchange([_pair_reduce("pair_w_in", p_in.astype(bf16)),
                                  _pair_reduce("pair_w_aro", p_aro.astype(bf16))])
    f_in, f_aro = _swap_halves([_sum_slots("sum_w_in", s_in), _sum_slots("sum_w_aro", s_aro)])
    g_w_in_t = f_in.reshape(padded, D)[:shard_cols]
    rows = D // N_DEV
    g_aro = [f_aro[:, i * rows:(i + 1) * rows, :].reshape(2 * rows, D) for i in range(3)]

    w_in_upd = _adamw("adamw_w_in", w_in_t, g_w_in_t, jnp.transpose(m_w_in[0]), jnp.transpose(v_w_in[0]))
    g_w_in, d_w_in, nm_w_in, nv_w_in = [jnp.transpose(a) for a in (g_w_in_t, *w_in_upd)]
    upd_a = _adamw("adamw_w_branch_a", w_branch_a[0], g_aro[0], m_w_branch_a[0], v_w_branch_a[0])
    upd_r = _adamw("adamw_w_branch_r", w_branch_r[0], g_aro[1], m_w_branch_r[0], v_w_branch_r[0])
    upd_o = _adamw("adamw_w_out", w_out[0], g_aro[2], m_w_out[0], v_w_out[0])
    d_aro, nm_aro, nv_aro = zip(upd_a, upd_r, upd_o)
    d_cw, nm_cw, nv_cw = _adamw("adamw_conv_w", conv_w[0], g_conv_w, m_conv_w[0], v_conv_w[0])

    def sharded(t_in, t_aro, t_cw):
        return dict(w_in=t_in[None], conv_w=t_cw[None], w_branch_a=t_aro[0][None], w_branch_r=t_aro[1][None],
                    w_out=t_aro[2][None])

    order = ["pre_norm_w", "w_in", "b_in", "conv_w", "conv_b", "rg_wa", "rg_ba", "rg_wx", "rg_bx", "rg_lambda",
             "w_branch_a", "w_branch_r", "w_out", "post_norm_w"]
    outs = [loss, grad_x]
    for rep, shd in ((grads, sharded(g_w_in, g_aro, g_conv_w)), (deltas, sharded(d_w_in, d_aro, d_cw)),
                     (new_m, sharded(nm_w_in, nm_aro, nm_cw)), (new_v, sharded(nv_w_in, nv_aro, nv_cw))):
        both = {**rep, **shd}
        outs.extend(both[k] for k in order)
    return tuple(outs)
```

```python
import jax
import jax.numpy as jnp
from jax import lax
from jax.experimental import pallas as pl
from jax.experimental.pallas import tpu as pltpu

f32 = jnp.float32
bf16 = jnp.bfloat16

D = 1024
HEADS = 16
HEAD_PAIRS = 8
LANES = 128
NORM_EPS = 1e-6
MASK_VALUE = -1e30
RG_C = 8.0
QK_SCALE = 0.125
TQ = 256
ATT_GROUP = 8
ATT_GROUP_FWD = 16
TL = 256
TM = 512
PREV_ROWS = 16
IN_USED = 8 * D + HEADS
IN_TOTAL = 9 * D + HEADS
N_CHIPS = 4
N_DEV = 8
ADAM_LR, ADAM_B1, ADAM_B2, ADAM_EPS, ADAM_WD, ADAM_STEP = 0.001, 0.9, 0.999, 1e-08, 0.01, 10
VMEM_LIMIT = 56 * 1024 * 1024
MESH = pl.DeviceIdType.MESH


def _dot(a, b):
    return jnp.dot(a, b, preferred_element_type=f32)


def _dot_nt(a, b):
    return lax.dot_general(a, b, (((1,), (1,)), ((), ())), preferred_element_type=f32)


def _dot_tn(a, b):
    return lax.dot_general(a, b, (((0,), (0,)), ((), ())), preferred_element_type=f32)


def _sig(x):
    return 0.5 * jnp.tanh(0.5 * x) + 0.5


def _softplus(x):
    return jnp.maximum(x, 0.0) + jnp.log(1.0 + jnp.exp(-jnp.abs(x)))


def _params(sem, vmem=None):
    return pltpu.CompilerParams(dimension_semantics=sem, vmem_limit_bytes=vmem)


def _tile(tm, width, cb=0):
    return pl.BlockSpec((tm, width), lambda i, cb=cb: (i, cb))


def _whole(shape):
    nd = len(shape)
    return pl.BlockSpec(shape, lambda *_: (0,) * nd)


def _norm_qkv(x, w_pre, w_all, b_qkv, tm=1024):
    t = x.shape[0]
    tm = min(tm, t)
    n = b_qkv.shape[1]

    def body(x_ref, wp_ref, w_ref, b_ref, h_ref, o_ref):
        @pl.when(pl.program_id(1) == 0)
        def _():
            xv = x_ref[...]
            r = lax.rsqrt(jnp.mean(xv * xv, axis=-1, keepdims=True) + NORM_EPS)
            h_ref[...] = (xv * r * wp_ref[...]).astype(bf16)

        o_ref[...] = (_dot_nt(h_ref[...], w_ref[...]) + b_ref[...]).astype(bf16)

    return pl.pallas_call(
        body, name="norm_qkv", grid=(t // tm, n // D),
        in_specs=[pl.BlockSpec((tm, D), lambda i, j: (i, 0)), _whole((1, D)), pl.BlockSpec((D, D), lambda i, j: (j, 0)),
                  pl.BlockSpec((1, D), lambda i, j: (0, j))],
        out_specs=[pl.BlockSpec((tm, D), lambda i, j: (i, 0)), pl.BlockSpec((tm, D), lambda i, j: (i, j))],
        out_shape=[jax.ShapeDtypeStruct((t, D), bf16), jax.ShapeDtypeStruct((t, n), bf16)],
        compiler_params=_params(("parallel", "arbitrary"), VMEM_LIMIT),
    )(x, w_pre, w_all, b_qkv)


def _mm(name, a, w, bias, out_dtype, tm, tn, w_rows=None):
    t, k = a.shape
    tm = min(tm, t)
    w_is_nk = w_rows is not None
    row0, n = w_rows if w_is_nk else (0, w.shape[1])
    assert row0 % tn == 0

    def body(a_ref, w_ref, *refs):
        acc = _dot_nt(a_ref[...], w_ref[...]) if w_is_nk else _dot(a_ref[...], w_ref[...])
        if bias is not None:
            acc = acc + refs[0][...]
        refs[-1][...] = acc.astype(out_dtype)

    in_specs = [pl.BlockSpec((tm, k), lambda i, j: (i, 0)),
                pl.BlockSpec((tn, k), lambda i, j: (row0 // tn + j, 0)) if w_is_nk
                else pl.BlockSpec((k, tn), lambda i, j: (0, j))]
    args = [a, w]
    if bias is not None:
        in_specs.append(pl.BlockSpec((1, tn), lambda i, j: (0, j)))
        args.append(bias)
    return pl.pallas_call(
        body, name=name, grid=(t // tm, n // tn), in_specs=in_specs,
        out_specs=pl.BlockSpec((tm, tn), lambda i, j: (i, j)), out_shape=jax.ShapeDtypeStruct((t, n), out_dtype),
        compiler_params=_params(("parallel", "parallel"), VMEM_LIMIT),
    )(*args)


def _forget_prep(f128, seq):
    t = f128.shape[0]
    nb = seq // LANES

    def body(f_ref, c_ref):
        r = lax.broadcasted_iota(jnp.int32, (LANES, LANES), 0)
        cidx = lax.broadcasted_iota(jnp.int32, (LANES, LANES), 1)
        tri = (r >= cidx).astype(f32)
        carry = jnp.zeros((1, LANES), f32)
        for blk in range(nb):
            fv = f_ref[pl.ds(blk * LANES, LANES), :]
            lf = -_softplus(-fv)
            c_ref[pl.ds(blk * LANES, LANES), :] = (
                jnp.dot(tri, lf, preferred_element_type=f32, precision=lax.Precision.HIGHEST) + carry)
            carry = carry + jnp.sum(lf, axis=0, keepdims=True)

    return pl.pallas_call(
        body, name="forget_prep", grid=(t // seq,),
        in_specs=[pl.BlockSpec((seq, LANES), lambda b: (b, 0))],
        out_specs=pl.BlockSpec((seq, LANES), lambda b: (b, 0)),
        out_shape=jax.ShapeDtypeStruct((t, LANES), f32),
        compiler_params=_params(("parallel",)),
    )(f128)


def _split3(cv):
    hi = cv.astype(bf16)
    r1 = cv - hi.astype(f32)
    mid = r1.astype(bf16)
    lo = (r1 - mid.astype(f32)).astype(bf16)
    return hi, mid, lo


def _attn_prep(qkv, c):
    t = qkv.shape[0]

    def body(q_ref, k_ref, c_ref, qa_ref, ka_ref):
        lane = lax.broadcasted_iota(jnp.int32, (1, LANES), 1)
        cv = c_ref[...]
        one = jnp.ones((), bf16)
        zero = jnp.zeros((), bf16)
        q_ones = jnp.where((lane >= 67) & (lane < 70), one, zero)
        k_ones = jnp.where((lane >= 64) & (lane < 67), one, zero)
        for head in range(HEADS):
            pair = pl.ds((head // 2) * LANES, LANES)
            ch = jnp.sum(jnp.where(lane == head, cv, 0.0), axis=1, keepdims=True)
            hi, mid, lo = _split3(ch)
            q2, k2 = q_ref[:, pair], k_ref[:, pair]
            if head % 2 == 1:
                q2, k2 = pltpu.roll(q2, 64, 1), pltpu.roll(k2, 64, 1)
            qa = jnp.where(lane < 64, q2 * jnp.asarray(QK_SCALE, bf16),
                           jnp.where(lane == 64, hi, jnp.where(lane == 65, mid, jnp.where(lane == 66, lo, q_ones))))
            ka = jnp.where(lane < 64, k2,
                           jnp.where(lane == 67, -hi, jnp.where(lane == 68, -mid, jnp.where(lane == 69, -lo, k_ones))))
            qa_ref[:, pl.ds(head * LANES, LANES)] = qa
            ka_ref[:, pl.ds(head * LANES, LANES)] = ka

    tm = min(TM, t)
    out = pl.BlockSpec((tm, 2 * D), lambda i: (i, 0))
    return pl.pallas_call(
        body, name="attn_prep", grid=(t // tm,),
        in_specs=[_tile(tm, D, 0), _tile(tm, D, 1), _tile(tm, LANES)],
        out_specs=[out, out],
        out_shape=[jax.ShapeDtypeStruct((t, 2 * D), bf16)] * 2,
        compiler_params=_params(("parallel",)),
    )(qkv, qkv, c)


def _attn_fwd(qa, ka, qkv, rest, seq):
    t = qkv.shape[0]
    nb, nq = t // seq, seq // TQ

    hg = ATT_GROUP_FWD
    ng = HEADS // hg

    def body(q_ref, k_ref, v_ref, ga_ref, o_ref, pa_ref, lse_ref, acc_scr):
        qi, gi = pl.program_id(1), pl.program_id(2)
        krow = lax.broadcasted_iota(jnp.int32, (TQ, TQ), 0)
        qcol = lax.broadcasted_iota(jnp.int32, (TQ, TQ), 1)
        acc_scr[...] = jnp.zeros_like(acc_scr)

        def kv_step(kt, carry, masked):
            ks = pl.multiple_of(kt * TQ, TQ)
            sts = [_dot_nt(k_ref[pl.ds(ks, TQ), pl.ds(g * LANES, LANES)], q_ref[:, pl.ds(g * LANES, LANES)])
                   for g in range(hg)]
            if masked:
                sts = [jnp.where(krow <= qcol, st, MASK_VALUE) for st in sts]
            m_new = [jnp.maximum(carry[g][0], jnp.max(sts[g], axis=0, keepdims=True)) for g in range(hg)]
            ps = [jnp.exp(sts[g] - m_new[g]) for g in range(hg)]
            alphas = [jnp.exp(carry[g][0] - m_new[g]) for g in range(hg)]
            phi = [ps[g].astype(bf16) for g in range(hg)]
            plo = [(ps[g] - phi[g].astype(f32)).astype(bf16) for g in range(hg)]
            vs = [v_ref[pl.ds(ks, TQ), pl.ds(j * LANES, LANES)] for j in range(hg // 2)]
            pvs = [_dot_tn(vs[g // 2], phi[g]) + _dot_tn(vs[g // 2], plo[g]) for g in range(hg)]
            olds = [acc_scr[g] for g in range(hg)]
            for g in range(hg):
                acc_scr[g] = alphas[g] * olds[g] + pvs[g]
            return tuple((m_new[g], alphas[g] * carry[g][1] + jnp.sum(ps[g], axis=0, keepdims=True))
                         for g in range(hg))

        init = tuple((jnp.full((1, TQ), MASK_VALUE, f32), jnp.zeros((1, TQ), f32)) for _ in range(hg))
        carry = lax.fori_loop(0, qi, lambda kt, cr: kv_step(kt, cr, False), init)
        stats = kv_step(qi, carry, True)
        drow = lax.broadcasted_iota(jnp.int32, (LANES, TQ), 0)
        for g in range(hg):
            m, l = stats[g]
            lse_ref[0, pl.ds(hg * gi + g, 1), :] = m + jnp.log(l)
        for j in range(hg // 2):
            o2 = jnp.where(drow < 64, acc_scr[2 * j] / stats[2 * j][1], acc_scr[2 * j + 1] / stats[2 * j + 1][1]).T
            o_ref[:, pl.ds(j * LANES, LANES)] = o2
            ga = ga_ref[:, pl.ds(j * LANES, LANES)].astype(f32)
            pa_ref[:, pl.ds(j * LANES, LANES)] = (o2 * (ga * _sig(ga))).astype(bf16)

    vw = hg * 64
    tile = pl.BlockSpec((TQ, vw), lambda b, qi, gi: (b * nq + qi, gi))
    return pl.pallas_call(
        body, name="attn_fwd", grid=(nb, nq, ng),
        in_specs=[pl.BlockSpec((TQ, hg * LANES), lambda b, qi, gi: (b * nq + qi, gi)),
                  pl.BlockSpec((seq, hg * LANES), lambda b, qi, gi: (b, gi)),
                  pl.BlockSpec((seq, vw), lambda b, qi, gi: (b, 2 * ng + gi)), tile],
        out_specs=[tile, tile, pl.BlockSpec((1, HEADS, TQ), lambda b, qi, gi: (b * nq + qi, 0, 0))],
        out_shape=[jax.ShapeDtypeStruct((t, D), f32), jax.ShapeDtypeStruct((t, D), bf16),
                   jax.ShapeDtypeStruct((t // TQ, HEADS, TQ), f32)],
        scratch_shapes=[pltpu.VMEM((hg, LANES, TQ), f32)],
        compiler_params=_params(("parallel", "parallel", "arbitrary"), VMEM_LIMIT),
    )(qa, ka, qkv, rest)


def _shifted_rows(x, top8, prev8, shift, row, row8):
    body = pltpu.roll(x, shift, 0)
    head = jnp.where(row8 < shift, pltpu.roll(prev8, shift, 0), pltpu.roll(top8, shift, 0))
    return body, head


def _rnn_gates(xc, wa_ref, wx_ref, ba_ref, bx_ref, lam_ref):
    xcb = xc.astype(bf16)
    r = _sig(_dot(xcb, wa_ref[...]) + ba_ref[...])
    i = _sig(_dot(xcb, wx_ref[...]) + bx_ref[...])
    sp = _softplus(-lam_ref[...])
    log_a = (-RG_C) * r * sp
    th = jnp.tanh(log_a)
    w1 = (-2.0) * th / (1.0 - th)
    sq = jnp.sqrt(jnp.maximum(w1, 0.0))
    return r, i, sp, log_a, w1, sq


def _conv_tile(x_ref, xprev_ref, has_prev, cw_ref, cb_ref, xc_ref):
    row = lax.broadcasted_iota(jnp.int32, (TL, D), 0)
    row8 = lax.broadcasted_iota(jnp.int32, (8, D), 0)
    x = x_ref[...].astype(f32)
    top8 = x[:8]
    prev8 = jnp.where(has_prev, xprev_ref[...].astype(f32)[PREV_ROWS - 8:], 0.0)
    xc = cb_ref[...] + cw_ref[pl.ds(3, 1), :] * x
    xc8 = cb_ref[...] + cw_ref[pl.ds(3, 1), :] * top8
    for sh in range(1, 4):
        w = cw_ref[pl.ds(3 - sh, 1), :]
        xs, xs8 = _shifted_rows(x, top8, prev8, sh, row, row8)
        xc = xc + w * xs
        xc8 = xc8 + w * xs8
    xc_ref[...] = xc
    xc_ref[pl.ds(0, 8), :] = xc8


def _rnn_fwd(rest, conv_w, conv_b, wa_d, wx_d, ba, bx, lam, seq):
    t = rest.shape[0]
    nb, nt = t // seq, seq // TL

    def body(x_ref, xprev_ref, gr_ref, cw_ref, cb_ref, wa_ref, wx_ref, ba_ref, bx_ref, lam_ref,
             xc_ref, a_ref, h_ref, pr_ref, xc_scr, u_scr, h_scr, carry):
        tt = pl.program_id(1)
        _conv_tile(x_ref, xprev_ref, tt > 0, cw_ref, cb_ref, xc_scr)
        xc = xc_scr[...]
        xc_ref[...] = xc.astype(bf16)
        r, i, sp, log_a, w1, sq = _rnn_gates(xc, wa_ref, wx_ref, ba_ref, bx_ref, lam_ref)
        a_ref[...] = jnp.exp(log_a)
        u_scr[...] = sq * (i * xc)

        @pl.when(tt == 0)
        def _():
            carry[...] = jnp.zeros_like(carry)

        def step(s, h):
            h = a_ref[pl.ds(s, 1), :] * h + u_scr[pl.ds(s, 1), :]
            h_scr[pl.ds(s, 1), :] = h
            return h

        carry[...] = lax.fori_loop(0, TL, step, carry[...], unroll=8)
        gr = gr_ref[...].astype(f32)
        h = h_scr[...]
        h_ref[...] = h.astype(bf16)
        pr_ref[...] = (h * (gr * _sig(gr))).astype(bf16)

    tile = lambda cb: pl.BlockSpec((TL, D), lambda b, tt, cb=cb: (b * nt + tt, cb))
    prev = lambda cb: pl.BlockSpec(
        (PREV_ROWS, D), lambda b, tt, cb=cb: (jnp.maximum((b * nt + tt) * (TL // PREV_ROWS) - 1, 0), cb))
    vec = _whole((1, D))
    return pl.pallas_call(
        body, name="rnn_fwd", grid=(nb, nt),
        in_specs=[tile(1), prev(1), tile(2), _whole((4, D)), vec, _whole((D, D)), _whole((D, D)), vec, vec, vec],
        out_specs=[tile(0)] * 4,
        out_shape=[jax.ShapeDtypeStruct((t, D), dt) for dt in (bf16, f32, bf16, bf16)],
        scratch_shapes=[pltpu.VMEM((TL, D), f32)] * 3 + [pltpu.VMEM((1, D), f32)],
        compiler_params=_params(("parallel", "arbitrary"), VMEM_LIMIT),
    )(rest, rest, rest, conv_w, conv_b, wa_d, wx_d, ba, bx, lam)


def _merge(mga, mgr, ya, yr):
    return (_sig(mga.astype(f32)) * ya.astype(f32) + _sig(mgr.astype(f32)) * yr.astype(f32)).astype(bf16)


def _out_proj_loss(rest, ya, yr, w_out, x, tgt, w_post):
    t = x.shape[0]

    def body(mga_ref, mgr_ref, ya_ref, yr_ref, wo_ref, x_ref, t_ref, w_ref, do_ref, dy_ref, mrg_ref, loss_ref, dwp_ref):
        @pl.when(pl.program_id(0) == 0)
        def _():
            loss_ref[...] = jnp.zeros_like(loss_ref)
            dwp_ref[...] = jnp.zeros_like(dwp_ref)

        mrg = _merge(mga_ref[...], mgr_ref[...], ya_ref[...], yr_ref[...])
        mrg_ref[...] = mrg
        ov = _dot(mrg, wo_ref[...])
        w = w_ref[...]
        r2 = lax.rsqrt(jnp.mean(ov * ov, axis=-1, keepdims=True) + NORM_EPS)
        oh = ov * r2
        e = x_ref[...] + oh * w - t_ref[...]
        loss_ref[...] += 0.5 * jnp.sum(jnp.mean(e * e, axis=-1, keepdims=True))
        dy = e * (1.0 / D)
        dy_ref[...] = dy
        dwp_ref[...] += jnp.sum(dy * oh, axis=0, keepdims=True)
        doh = dy * w
        do_ref[...] = (r2 * (doh - oh * jnp.mean(doh * oh, axis=-1, keepdims=True))).astype(bf16)

    return pl.pallas_call(
        body, name="out_proj_loss", grid=(t // TM,),
        in_specs=[_tile(TM, D, 3), _tile(TM, D, 4), _tile(TM, D), _tile(TM, D), _whole((D, D)), _tile(TM, D),
                  _tile(TM, D), _whole((1, D))],
        out_specs=[_tile(TM, D), _tile(TM, D), _tile(TM, D), _whole((8, LANES)), _whole((1, D))],
        out_shape=[jax.ShapeDtypeStruct((t, D), bf16), jax.ShapeDtypeStruct((t, D), f32),
                   jax.ShapeDtypeStruct((t, D), bf16), jax.ShapeDtypeStruct((8, LANES), f32),
                   jax.ShapeDtypeStruct((1, D), f32)],
        compiler_params=_params(("arbitrary",), VMEM_LIMIT),
    )(rest, rest, ya, yr, w_out, x, tgt, w_post)


def _out_bwd(do, rest, ya, yr, w_out):
    t = do.shape[0]

    def body(do_ref, mga_ref, mgr_ref, ya_ref, yr_ref, w_ref, dya_ref, dyr_ref, dmga_ref, dmgr_ref):
        sa, sr = _sig(mga_ref[...].astype(f32)), _sig(mgr_ref[...].astype(f32))
        ya, yr = ya_ref[...].astype(f32), yr_ref[...].astype(f32)
        dm = _dot_nt(do_ref[...], w_ref[...])
        dya_ref[...] = (dm * sa).astype(bf16)
        dyr_ref[...] = (dm * sr).astype(bf16)
        dmga_ref[...] = (dm * ya * sa * (1.0 - sa)).astype(bf16)
        dmgr_ref[...] = (dm * yr * sr * (1.0 - sr)).astype(bf16)

    return pl.pallas_call(
        body, name="out_bwd", grid=(t // TM,),
        in_specs=[_tile(TM, D), _tile(TM, D, 3), _tile(TM, D, 4), _tile(TM, D), _tile(TM, D), _whole((D, D))],
        out_specs=[_tile(TM, D)] * 4,
        out_shape=[jax.ShapeDtypeStruct((t, D), bf16)] * 4,
        compiler_params=_params(("parallel",), VMEM_LIMIT),
    )(do, rest, rest, ya, yr, w_out)


def _branch_bwd(name, dyb, rest, gate_cb, act, w, act_grad_dtype, head_sums=False):
    t = dyb.shape[0]

    def body(dy_ref, g_ref, act_ref, w_ref, dact_ref, dg_ref, *delta_ref):
        dp = _dot_nt(dy_ref[...], w_ref[...])
        g = g_ref[...].astype(f32)
        sg = _sig(g)
        act = act_ref[...].astype(f32)
        dact = (dp * (g * sg)).astype(act_grad_dtype)
        dact_ref[...] = dact
        dg_ref[...] = (dp * act * (sg * (1.0 + g * (1.0 - sg)))).astype(bf16)
        if head_sums:
            ch = lax.broadcasted_iota(jnp.int32, (D, LANES), 0)
            hd = lax.broadcasted_iota(jnp.int32, (D, LANES), 1)
            pick = (ch // 64 == hd).astype(bf16)
            per_head = sum(_dot(piece, pick) for piece in _split3(dact.astype(f32) * act))
            for s in range(TM // TQ):
                delta_ref[0][s] = per_head[s * TQ:(s + 1) * TQ].T[:HEADS, :]

    out_specs = [_tile(TM, D), _tile(TM, D)]
    out_shape = [jax.ShapeDtypeStruct((t, D), act_grad_dtype), jax.ShapeDtypeStruct((t, D), bf16)]
    if head_sums:
        out_specs.append(pl.BlockSpec((TM // TQ, HEADS, TQ), lambda i: (i, 0, 0)))
        out_shape.append(jax.ShapeDtypeStruct((t // TQ, HEADS, TQ), f32))
    return pl.pallas_call(
        body, name=name, grid=(t // TM,),
        in_specs=[_tile(TM, D), _tile(TM, D, gate_cb), _tile(TM, D), _whole((D, D))],
        out_specs=out_specs, out_shape=out_shape,
        compiler_params=_params(("parallel",), VMEM_LIMIT),
    )(dyb, rest, act, w)


def _rnn_bwd(dh, a, h, xc, rest, conv_w, conv_b, wa_d, wx_d, ba, bx, lam, seq):
    t = dh.shape[0]
    nb, nt = t // seq, seq // TL
    diag = (D // LANES, LANES, LANES)

    def body(dh_ref, a_ref, h_ref, hprev_ref, xc_ref, x_ref, xprev_ref, cw_ref, cb_ref, wa_ref, wx_ref,
             ba_ref, bx_ref, lam_ref, dxr_ref, dwa_ref, dwx_ref, vec_ref, g_scr, dxc_scr, dxr_scr, qcarry, dxc_next):
        b, tt = pl.program_id(0), pl.program_id(1)
        rt = nt - 1 - tt

        @pl.when((b == 0) & (tt == 0))
        def _():
            dwa_ref[...] = jnp.zeros_like(dwa_ref)
            dwx_ref[...] = jnp.zeros_like(dwx_ref)
            vec_ref[...] = jnp.zeros_like(vec_ref)

        @pl.when(tt == 0)
        def _():
            qcarry[...] = jnp.zeros_like(qcarry)
            dxc_next[...] = jnp.zeros_like(dxc_next)

        g_scr[...] = dh_ref[...].astype(f32)

        def step(k, q):
            s = TL - 1 - k
            g = g_scr[pl.ds(s, 1), :] + q
            g_scr[pl.ds(s, 1), :] = g
            return a_ref[pl.ds(s, 1), :] * g

        qcarry[...] = lax.fori_loop(0, TL, step, qcarry[...], unroll=8)

        row = lax.broadcasted_iota(jnp.int32, (TL, D), 0)
        row8 = lax.broadcasted_iota(jnp.int32, (8, D), 0)
        g = g_scr[...]
        av = a_ref[...]
        xc = xc_ref[...].astype(f32)
        hlast = jnp.where(rt > 0, hprev_ref[...].astype(f32)[PREV_ROWS - 1:], 0.0)
        hp = jnp.where(row == 0, hlast, pltpu.roll(h_ref[...].astype(f32), 1, 0))
        r, i, sp, log_a, w1, sq = _rnn_gates(xc, wa_ref, wx_ref, ba_ref, bx_ref, lam_ref)
        dix = g * sq
        di = dix * xc
        dxc = dix * i
        dsq = g * (i * xc)
        dlog_a = g * hp * av - dsq * jnp.where(sq > 0.0, (1.0 - w1) / sq, 0.0)
        dpr = (dlog_a * ((-RG_C) * sp)) * r * (1.0 - r)
        dpi = di * i * (1.0 - i)
        dprb, dpib, xcb = dpr.astype(bf16), dpi.astype(bf16), xc.astype(bf16)
        dxc = dxc + _dot_nt(dprb, wa_ref[...]) + _dot_nt(dpib, wx_ref[...])
        for j in range(D // LANES):
            cols = slice(j * LANES, (j + 1) * LANES)
            dwa_ref[j] += _dot_tn(xcb[:, cols], dprb[:, cols])
            dwx_ref[j] += _dot_tn(xcb[:, cols], dpib[:, cols])
        vec_ref[pl.ds(0, 1), :] += jnp.sum(dpr, axis=0, keepdims=True)
        vec_ref[pl.ds(1, 1), :] += jnp.sum(dpi, axis=0, keepdims=True)
        dsp = jnp.sum(dlog_a * ((-RG_C) * r), axis=0, keepdims=True)
        vec_ref[pl.ds(2, 1), :] += dsp * (-_sig(-lam_ref[...]))
        vec_ref[pl.ds(3, 1), :] += jnp.sum(dxc, axis=0, keepdims=True)

        dxc_scr[...] = dxc
        bot8 = dxc_scr[pl.ds(TL - 8, 8), :]
        nxt8 = dxc_next[...]
        dxr = cw_ref[pl.ds(3, 1), :] * dxc
        dxr8 = cw_ref[pl.ds(3, 1), :] * bot8
        for sh in range(1, 4):
            w = cw_ref[pl.ds(3 - sh, 1), :]
            dxr = dxr + w * pltpu.roll(dxc, TL - sh, 0)
            dxr8 = dxr8 + w * jnp.where(row8 < 8 - sh, pltpu.roll(bot8, 8 - sh, 0), pltpu.roll(nxt8, 8 - sh, 0))
        dxr_scr[...] = dxr
        dxr_scr[pl.ds(TL - 8, 8), :] = dxr8
        dxr_ref[...] = dxr_scr[...].astype(bf16)
        dxc_next[...] = dxc_scr[pl.ds(0, 8), :]

        x = x_ref[...].astype(f32)
        prev8 = jnp.where(rt > 0, xprev_ref[...].astype(f32)[PREV_ROWS - 8:], 0.0)
        dxc_top8 = dxc_scr[pl.ds(0, 8), :]
        vec_ref[pl.ds(7, 1), :] += jnp.sum(dxc * x, axis=0, keepdims=True)
        for sh in range(1, 4):
            inside = jnp.sum(dxc * jnp.where(row >= sh, pltpu.roll(x, sh, 0), 0.0), axis=0, keepdims=True)
            above = jnp.sum(dxc_top8 * jnp.where(row8 < sh, pltpu.roll(prev8, sh, 0), 0.0), axis=0, keepdims=True)
            vec_ref[pl.ds(7 - sh, 1), :] += inside + above

    tile = lambda cb: pl.BlockSpec((TL, D), lambda b, tt, cb=cb: (b * nt + nt - 1 - tt, cb))
    prev = lambda cb: pl.BlockSpec(
        (PREV_ROWS, D), lambda b, tt, cb=cb: (jnp.maximum((b * nt + nt - 1 - tt) * (TL // PREV_ROWS) - 1, 0), cb))
    vec = _whole((1, D))
    return pl.pallas_call(
        body, name="rnn_bwd", grid=(nb, nt),
        in_specs=[tile(0), tile(0), tile(0), prev(0), tile(0), tile(1), prev(1),
                  _whole((4, D)), vec, _whole((D, D)), _whole((D, D)), vec, vec, vec],
        out_specs=[tile(0), _whole(diag), _whole(diag), _whole((8, D))],
        out_shape=[jax.ShapeDtypeStruct((t, D), bf16), jax.ShapeDtypeStruct(diag, f32),
                   jax.ShapeDtypeStruct(diag, f32), jax.ShapeDtypeStruct((8, D), f32)],
        scratch_shapes=[pltpu.VMEM((TL, D), f32), pltpu.VMEM((TL, D), f32), pltpu.VMEM((TL, D), f32),
                        pltpu.VMEM((1, D), f32), pltpu.VMEM((8, D), f32)],
        compiler_params=_params(("arbitrary", "arbitrary"), VMEM_LIMIT),
    )(dh, a, h, h, xc, rest, rest, conv_w, conv_b, wa_d, wx_d, ba, bx, lam)


def _attn_bwd(qa, ka, qkv, doa, lse, delta, seq):
    t = qkv.shape[0]
    nb, nq = t // seq, seq // TQ
    hg = ATT_GROUP
    ng, npair = HEADS // hg, hg // 2

    def body(qa_ref, ka_ref, q_ref, k_ref, v_ref, do_ref, lse_ref, dl_ref, dq_ref, dk_ref, dv_ref, dc_ref,
             dqt_scr, dk_scr, dv_scr, ds_scr, kht_scr):
        gi, kt = pl.program_id(1), pl.program_id(2)
        lane = lax.broadcasted_iota(jnp.int32, (1, LANES), 1)
        krow = lax.broadcasted_iota(jnp.int32, (TQ, TQ), 0)
        qcol = lax.broadcasted_iota(jnp.int32, (TQ, TQ), 1)
        lmask = [(lane // 64) == hh for hh in range(2)]
        scale = jnp.asarray(QK_SCALE, bf16)

        @pl.when(kt == 0)
        def _():
            dqt_scr[...] = jnp.zeros_like(dqt_scr)

        dk_scr[...] = jnp.zeros_like(dk_scr)
        dv_scr[...] = jnp.zeros_like(dv_scr)
        ds_scr[...] = jnp.zeros_like(ds_scr)
        for g in range(hg):
            k2 = k_ref[:, pl.ds((g // 2) * LANES, LANES)]
            kht_scr[g] = jnp.where(lmask[g % 2], k2, jnp.zeros_like(k2)).T

        def q_step(qt, masked):
            qs = pl.multiple_of(qt * TQ, TQ)
            heads = range(hg)
            do2 = [do_ref[pl.ds(qs, TQ), pl.ds(j * LANES, LANES)] for j in range(npair)]
            q2 = [q_ref[pl.ds(qs, TQ), pl.ds(j * LANES, LANES)] for j in range(npair)]
            doh = [jnp.where(lmask[g % 2], do2[g // 2], jnp.zeros_like(do2[0])) for g in heads]
            qh = [jnp.where(lmask[g % 2], q2[g // 2], jnp.zeros_like(q2[0])) * scale for g in heads]
            st = [_dot_nt(ka_ref[:, pl.ds(g * LANES, LANES)], qa_ref[pl.ds(qs, TQ), pl.ds(g * LANES, LANES)])
                  for g in heads]
            if masked:
                st = [jnp.where(krow <= qcol, s, MASK_VALUE) for s in st]
            dp = [_dot_nt(v_ref[:, pl.ds((g // 2) * LANES, LANES)], doh[g]) for g in heads]
            p = [jnp.exp(st[g] - lse_ref[qt, pl.ds(hg * gi + g, 1), :]) for g in heads]
            ds = [p[g] * (dp[g] - dl_ref[qt, pl.ds(hg * gi + g, 1), :]) for g in heads]
            pb = [x.astype(bf16) for x in p]
            dsb = [x.astype(bf16) for x in ds]
            for j in range(npair):
                a, b = 2 * j, 2 * j + 1
                dv_scr[j] += _dot(pb[a], doh[a]) + _dot(pb[b], doh[b])
                dk_scr[j] += _dot(dsb[a], qh[a]) + _dot(dsb[b], qh[b])
                dqt_scr[qt, j] += (_dot(kht_scr[a], dsb[a]) + _dot(kht_scr[b], dsb[b])) * QK_SCALE
            for g in heads:
                ds_scr[g] += ds[g][:, :LANES] + ds[g][:, LANES:]

        q_step(kt, True)

        def loop_body(qt, carry):
            q_step(qt, False)
            return carry

        lax.fori_loop(kt + 1, nq, loop_body, 0)

        dc = jnp.zeros((TQ, LANES), f32)
        for g in range(hg):
            dc = jnp.where(lane == g, -jnp.sum(ds_scr[g], axis=1, keepdims=True), dc)
        dc_ref[...] = dc
        for j in range(npair):
            dk_ref[:, pl.ds(j * LANES, LANES)] = dk_scr[j].astype(bf16)
            dv_ref[:, pl.ds(j * LANES, LANES)] = dv_scr[j].astype(bf16)

        @pl.when(kt == nq - 1)
        def _():
            for qt in range(nq):
                for j in range(npair):
                    dq_ref[pl.ds(qt * TQ, TQ), pl.ds(j * LANES, LANES)] = dqt_scr[qt, j].T.astype(bf16)

    vw = hg * 64
    seqspec = pl.BlockSpec((seq, vw), lambda b, gi, kt: (b, gi))
    kspec = lambda off: pl.BlockSpec((TQ, vw), lambda b, gi, kt: (b * nq + kt, off + gi))
    rowspec = pl.BlockSpec((nq, HEADS, TQ), lambda b, gi, kt: (b, 0, 0))
    return pl.pallas_call(
        body, name="attn_bwd", grid=(nb, ng, nq),
        in_specs=[pl.BlockSpec((seq, hg * LANES), lambda b, gi, kt: (b, gi)),
                  pl.BlockSpec((TQ, hg * LANES), lambda b, gi, kt: (b * nq + kt, gi)),
                  seqspec, kspec(ng), kspec(2 * ng), seqspec, rowspec, rowspec],
        out_specs=[seqspec, kspec(0), kspec(0), pl.BlockSpec((TQ, LANES), lambda b, gi, kt: (b * nq + kt, gi))],
        out_shape=[jax.ShapeDtypeStruct((t, D), bf16)] * 3 + [jax.ShapeDtypeStruct((t, ng * LANES), f32)],
        scratch_shapes=[pltpu.VMEM((nq, npair, LANES, TQ), f32), pltpu.VMEM((npair, TQ, LANES), f32),
                        pltpu.VMEM((npair, TQ, LANES), f32), pltpu.VMEM((hg, TQ, LANES), f32),
                        pltpu.VMEM((hg, LANES, TQ), bf16)],
        compiler_params=_params(("parallel", "parallel", "arbitrary"), VMEM_LIMIT),
    )(qa, ka, qkv, qkv, qkv, doa, lse, delta)


def _forget_bwd(dc, f128, seq):
    t = f128.shape[0]
    nb = seq // LANES

    def body(dc_ref, f_ref, df_ref, dbf_ref):
        @pl.when(pl.program_id(0) == 0)
        def _():
            dbf_ref[...] = jnp.zeros_like(dbf_ref)

        r = lax.broadcasted_iota(jnp.int32, (LANES, LANES), 0)
        cidx = lax.broadcasted_iota(jnp.int32, (LANES, LANES), 1)
        tri = (r <= cidx).astype(f32)
        carry = jnp.zeros((1, LANES), f32)
        total = jnp.zeros((1, LANES), f32)
        for blk in reversed(range(nb)):
            dcb = dc_ref[pl.ds(blk * LANES, LANES), :]
            dlf = jnp.dot(tri, dcb, preferred_element_type=f32, precision=lax.Precision.HIGHEST) + carry
            df = dlf * _sig(-f_ref[pl.ds(blk * LANES, LANES), :])
            df_ref[pl.ds(blk * LANES, LANES), :] = df.astype(bf16)
            total = total + jnp.sum(df, axis=0, keepdims=True)
            carry = carry + jnp.sum(dcb, axis=0, keepdims=True)
        dbf_ref[...] += total

    return pl.pallas_call(
        body, name="forget_bwd", grid=(t // seq,),
        in_specs=[pl.BlockSpec((seq, LANES), lambda b: (b, 0)), pl.BlockSpec((seq, LANES), lambda b: (b, 0))],
        out_specs=[pl.BlockSpec((seq, LANES), lambda b: (b, 0)), _whole((1, LANES))],
        out_shape=[jax.ShapeDtypeStruct((t, LANES), bf16), jax.ShapeDtypeStruct((1, LANES), f32)],
        compiler_params=_params(("arbitrary",)),
    )(dc, f128)


def _in_bwd(dz, df, x, dy, w_all, w_pre):
    t = x.shape[0]
    n_dz = len(dz)

    def body(*refs):
        dz_refs = refs[:n_dz]
        df_ref, x_ref, dy_ref, w_ref, wp_ref, gx_ref, dwp_ref = refs[n_dz:]

        @pl.when(pl.program_id(0) == 0)
        def _():
            dwp_ref[...] = jnp.zeros_like(dwp_ref)

        dh = _dot(df_ref[...], w_ref[pl.ds(n_dz * D, LANES), :])
        for p in range(n_dz):
            dh = dh + _dot(dz_refs[p][...], w_ref[pl.ds(p * D, D), :])
        xv = x_ref[...]
        r1 = lax.rsqrt(jnp.mean(xv * xv, axis=-1, keepdims=True) + NORM_EPS)
        xh = xv * r1
        dwp_ref[...] += jnp.sum(dh * xh, axis=0, keepdims=True)
        dxh = dh * wp_ref[...]
        gx_ref[...] = dy_ref[...] + r1 * (dxh - xh * jnp.mean(dxh * xh, axis=-1, keepdims=True))

    once = lambda shape: pl.BlockSpec(shape, lambda i: (0, 0), pipeline_mode=pl.Buffered(1))
    return pl.pallas_call(
        body, name="in_bwd", grid=(t // TM,),
        in_specs=[_tile(TM, D)] * n_dz + [_tile(TM, LANES), _tile(TM, D), _tile(TM, D), once(w_all.shape),
                  _whole((1, D))],
        out_specs=[_tile(TM, D), _whole((1, D))],
        out_shape=[jax.ShapeDtypeStruct((t, D), f32), jax.ShapeDtypeStruct((1, D), f32)],
        compiler_params=_params(("arbitrary",), VMEM_LIMIT),
    )(*dz, df, x, dy, w_all, w_pre)


def _tn_mm(name, a, b, tn, tk=2048):
    t, k = a.shape
    tk = min(tk, t)
    n = b.shape[1]

    def body(a_ref, b_ref, o_ref, s_ref):
        j, kk = pl.program_id(0), pl.program_id(1)

        @pl.when(kk == 0)
        def _():
            o_ref[...] = jnp.zeros_like(o_ref)

        @pl.when((j == 0) & (kk == 0))
        def _():
            s_ref[...] = jnp.zeros_like(s_ref)

        av = a_ref[...]
        o_ref[...] += _dot_tn(av, b_ref[...])

        @pl.when(j == 0)
        def _():
            s_ref[...] += jnp.sum(av.astype(f32), axis=0, keepdims=True)

    return pl.pallas_call(
        body, name=name, grid=(n // tn, t // tk),
        in_specs=[pl.BlockSpec((tk, k), lambda j, kk: (kk, 0)), pl.BlockSpec((tk, tn), lambda j, kk: (kk, j))],
        out_specs=[pl.BlockSpec((k, tn), lambda j, kk: (0, j)), _whole((1, k))],
        out_shape=[jax.ShapeDtypeStruct((k, n), f32), jax.ShapeDtypeStruct((1, k), f32)],
        compiler_params=_params(("arbitrary", "arbitrary"), VMEM_LIMIT),
    )(a, b)


def _position():
    return lax.axis_index("x"), lax.axis_index("y"), lax.axis_index("c")


ROW_BLOCK = 128


def _assemble_rows(shards_ref, shard_rows, segments, out_ref):
    for out0, log0, count in segments:
        for b0 in range(0, count, ROW_BLOCK):
            n_rows = min(ROW_BLOCK, count - b0)
            shard, off = divmod(log0 + b0, shard_rows)
            acc, done = None, 0
            while done < n_rows:
                take = min(n_rows - done, shard_rows - off)
                start = off // 16 * 16
                win = -(-(off - start + take) // 16) * 16
                r = lax.broadcasted_iota(jnp.int32, (ROW_BLOCK, win), 0)
                col = lax.broadcasted_iota(jnp.int32, (ROW_BLOCK, win), 1)
                pick = ((col - r == off - start - done) & (r >= done) & (r < done + take)).astype(bf16)
                part = _dot(pick, shards_ref[shard, pl.ds(start, win), :])
                acc = part if acc is None else acc + part
                done, shard, off = done + take, shard + 1, 0
            out_ref[pl.ds(out0 + b0, ROW_BLOCK), :] = acc.astype(bf16)


def _gather_shards(parts, small, shard_rows, segments, out_rows):
    n = len(parts)
    halves = [p.shape[0] // 2 for p in parts]
    cuts = [-(-h // 32) * 16 for h in halves]
    n_direct, n_relay, n_sib = 4 * n, 2 * n, 6 * n

    def body(*refs):
        srcs, small_src = refs[:n], refs[n]
        dsts, small_dst, whole_ref = refs[n + 1:2 * n + 1], refs[2 * n + 1], refs[2 * n + 2]
        send, recv, local = refs[2 * n + 3:]
        x, y, c = _position()
        me = 2 * x + y
        chips = [(1 - x, y), (x, 1 - y), (1 - x, 1 - y)]
        ids = [2 * px + py for px, py in chips]

        def rows(a, half, quarter):
            start = half * halves[a] + (cuts[a] if quarter else 0)
            return pl.ds(start, halves[a] - cuts[a] if quarter else cuts[a])

        def landing(a, shard, half, quarter):
            return dsts[a].at[shard, rows(a, half, quarter), :]

        def direct(a, nb, quarter, shard):
            k = (a * 2 + nb) * 2 + quarter
            px, py = chips[nb]
            return pltpu.make_async_remote_copy(
                src_ref=srcs[a].at[rows(a, c, quarter), :], dst_ref=landing(a, shard, c, quarter),
                send_sem=send.at[k], recv_sem=recv.at[k], device_id=(px, py, c), device_id_type=MESH)

        def relay(a, quarter, shard):
            k = n_direct + a * 2 + quarter
            px, py = chips[1 - quarter]
            return pltpu.make_async_remote_copy(
                src_ref=landing(a, shard, c, quarter), dst_ref=landing(a, shard, c, quarter),
                send_sem=send.at[k], recv_sem=recv.at[k], device_id=(px, py, c), device_id_type=MESH)

        def to_sibling(a, origin, quarter, half):
            k = n_direct + n_relay + (a * 3 + origin) * 2 + quarter
            return pltpu.make_async_remote_copy(
                src_ref=landing(a, ids[origin], half, quarter), dst_ref=landing(a, ids[origin], half, quarter),
                send_sem=send.at[k], recv_sem=recv.at[k], device_id=(x, y, 1 - c), device_id_type=MESH)

        def small_copy(j, shard):
            k = n_direct + n_relay + n_sib + j
            px, py = chips[j]
            return pltpu.make_async_remote_copy(
                src_ref=small_src, dst_ref=small_dst.at[shard], send_sem=send.at[k], recv_sem=recv.at[k],
                device_id=(px, py, c), device_id_type=MESH)

        own = [pltpu.make_async_copy(srcs[a], dsts[a].at[me], local.at[a]) for a in range(n)]
        own.append(pltpu.make_async_copy(small_src, small_dst.at[me], local.at[n]))
        for cp in own:
            cp.start()
        sent = [direct(a, nb, q, me) for q in range(2) for a in range(n) for nb in range(2)]
        sent += [small_copy(j, me) for j in range(3)]
        for cp in sent:
            cp.start()

        def passed_on(cp):
            cp.start()
            sent.append(cp)

        for q in range(2):
            for a in range(n):
                for nb in range(2):
                    direct(a, nb, q, ids[nb]).wait_recv()
                    passed_on(to_sibling(a, nb, q, c))
                    if nb == q:
                        passed_on(relay(a, q, ids[nb]))
        for a in range(n):
            for q in range(2):
                relay(a, q, ids[2]).wait_recv()
                passed_on(to_sibling(a, 2, q, c))
        for j in range(3):
            small_copy(j, ids[j]).wait_recv()
            for a in range(n):
                for q in range(2):
                    to_sibling(a, j, q, 1 - c).wait_recv()
        for cp in sent:
            cp.wait_send()
        for cp in own:
            cp.wait()
        _assemble_rows(dsts[0], shard_rows, segments, whole_ref)

    vm = pl.BlockSpec(memory_space=pltpu.VMEM)
    n_sems = n_direct + n_relay + n_sib + 3
    out = pl.pallas_call(
        body, name="gather_shards",
        in_specs=[vm] * (n + 1), out_specs=[vm] * (n + 2),
        out_shape=[jax.ShapeDtypeStruct((N_CHIPS,) + p.shape, p.dtype) for p in parts + [small]]
        + [jax.ShapeDtypeStruct((out_rows, parts[0].shape[1]), parts[0].dtype)],
        scratch_shapes=[pltpu.SemaphoreType.DMA((n_sems,)), pltpu.SemaphoreType.DMA((n_sems,)),
                        pltpu.SemaphoreType.DMA((n + 1,))],
        compiler_params=pltpu.CompilerParams(vmem_limit_bytes=VMEM_LIMIT),
    )(*parts, small)
    return out[1:]


def _allsum_rows(part):
    rows_n = part.shape[0]

    def body(x_ref, gath_ref, sum_ref, send_sems, recv_sems, local_sem):
        x, y, c = _position()
        me, sibling = (x, y, c), (x, y, 1 - c)
        chips = [(1 - x, y), (x, 1 - y), (1 - x, 1 - y)]

        def rows(px, py, pc):
            return gath_ref.at[pl.ds((4 * px + 2 * py + pc) * rows_n, rows_n), :]

        def copy(k, block, to, src=None):
            return pltpu.make_async_remote_copy(
                src_ref=rows(*block) if src is None else src, dst_ref=rows(*block),
                send_sem=send_sems.at[k], recv_sem=recv_sems.at[k], device_id=to, device_id_type=MESH)

        mine = pltpu.make_async_copy(x_ref, rows(*me), local_sem)
        mine.start()
        first = [copy(0, me, sibling, src=x_ref)]
        first += [copy(1 + j, me, (*chip, c), src=x_ref) for j, chip in enumerate(chips)]
        for cp in first:
            cp.start()
        passed = [copy(4 + j, (*chip, c), sibling) for j, chip in enumerate(chips)]
        for j, chip in enumerate(chips):
            copy(1 + j, (*chip, c), me).wait_recv()
            passed[j].start()
        copy(0, sibling, me).wait_recv()
        for j, chip in enumerate(chips):
            copy(4 + j, (*chip, 1 - c), me).wait_recv()
        for cp in first + passed:
            cp.wait_send()
        mine.wait()
        total = gath_ref[pl.ds(0, rows_n), :]
        for d in range(1, N_DEV):
            total = total + gath_ref[pl.ds(d * rows_n, rows_n), :]
        sum_ref[...] = total

    vm = pl.BlockSpec(memory_space=pltpu.VMEM)
    return pl.pallas_call(
        body, name="allsum_rows", in_specs=[vm], out_specs=[vm, vm],
        out_shape=[jax.ShapeDtypeStruct((N_DEV * rows_n, D), f32), jax.ShapeDtypeStruct((rows_n, D), f32)],
        scratch_shapes=[pltpu.SemaphoreType.DMA((7,)), pltpu.SemaphoreType.DMA((7,)), pltpu.SemaphoreType.DMA],
    )(part)[1]


PAIR_ROWS = 16


def _pair_reduce(name, pieces):
    _, r, n = pieces.shape

    def body(p_ref, o_ref, land, send, recv):
        x, y, c = _position()

        def remote(j, half):
            return pltpu.make_async_remote_copy(
                src_ref=p_ref.at[2 * j + half], dst_ref=land.at[j], send_sem=send.at[j], recv_sem=recv.at[j],
                device_id=(x, y, 1 - c), device_id_type=MESH)

        sends = [remote(j, 1 - c) for j in range(N_CHIPS)]
        for cp in sends:
            cp.start()
        for j in range(N_CHIPS):
            remote(j, c).wait_recv()

            def add_rows(i, carry, j=j):
                rows = pl.ds(pl.multiple_of(i * PAIR_ROWS, PAIR_ROWS), PAIR_ROWS)
                o_ref[j, rows, :] = (p_ref[2 * j + c, rows, :].astype(f32) + land[j, rows, :].astype(f32)).astype(bf16)
                return carry

            lax.fori_loop(0, r // PAIR_ROWS, add_rows, 0)
        for cp in sends:
            cp.wait_send()

    vm = pl.BlockSpec(memory_space=pltpu.VMEM)
    return pl.pallas_call(
        body, name=name, in_specs=[vm], out_specs=vm,
        out_shape=jax.ShapeDtypeStruct((N_CHIPS, r, n), bf16),
        scratch_shapes=[pltpu.VMEM((N_CHIPS, r, n), bf16), pltpu.SemaphoreType.DMA((N_CHIPS,)),
                        pltpu.SemaphoreType.DMA((N_CHIPS,))],
        compiler_params=pltpu.CompilerParams(vmem_limit_bytes=VMEM_LIMIT),
    )(pieces)


def _chip_exchange(arrs):
    n = len(arrs)
    heights = [a.shape[1] for a in arrs]
    cuts = [-(-r // 32) * 16 for r in heights]

    def body(*refs):
        srcs, dsts, relays = refs[:n], refs[n:2 * n], refs[2 * n:3 * n]
        send, recv, local = refs[3 * n:]
        x, y, c = _position()
        me = 2 * x + y
        chips = [(1 - x, y), (x, 1 - y), (1 - x, 1 - y)]
        ids = [2 * px + py for px, py in chips]

        def rows(a, quarter):
            return pl.ds(cuts[a], heights[a] - cuts[a]) if quarter else pl.ds(0, cuts[a])

        def held(a, quarter):
            size = heights[a] - cuts[a] if quarter else cuts[a]
            return relays[a].at[quarter, pl.ds(0, size), :]

        def direct(a, nb, piece, landing):
            px, py = chips[nb]
            return pltpu.make_async_remote_copy(
                src_ref=srcs[a].at[piece], dst_ref=dsts[a].at[landing], send_sem=send.at[a * 2 + nb],
                recv_sem=recv.at[a * 2 + nb], device_id=(px, py, c), device_id_type=MESH)

        def first_hop(a, quarter):
            k = 2 * n + a * 2 + quarter
            px, py = chips[quarter]
            return pltpu.make_async_remote_copy(
                src_ref=srcs[a].at[ids[2], rows(a, quarter), :], dst_ref=held(a, quarter), send_sem=send.at[k],
                recv_sem=recv.at[k], device_id=(px, py, c), device_id_type=MESH)

        def second_hop(a, quarter, origin):
            k = 4 * n + a * 2 + quarter
            px, py = chips[1 - quarter]
            return pltpu.make_async_remote_copy(
                src_ref=held(a, quarter), dst_ref=dsts[a].at[origin, rows(a, quarter), :], send_sem=send.at[k],
                recv_sem=recv.at[k], device_id=(px, py, c), device_id_type=MESH)

        own = [pltpu.make_async_copy(srcs[a].at[me], dsts[a].at[me], local.at[a]) for a in range(n)]
        sent = [first_hop(a, q) for a in range(n) for q in range(2)]
        sent += [direct(a, nb, ids[nb], me) for a in range(n) for nb in range(2)]
        for cp in sent + own:
            cp.start()
        for a in range(n):
            for q in range(2):
                first_hop(a, q).wait_recv()
                sent.append(second_hop(a, q, ids[q]))
                sent[-1].start()
        for a in range(n):
            for nb in range(2):
                direct(a, nb, me, ids[nb]).wait_recv()
            for q in range(2):
                second_hop(a, q, ids[2]).wait_recv()
        for cp in sent:
            cp.wait_send()
        for cp in own:
            cp.wait()

    anyspec = pl.BlockSpec(memory_space=pl.ANY)
    out = pl.pallas_call(
        body, name="chip_exchange", in_specs=[anyspec] * n, out_specs=[anyspec] * (2 * n),
        out_shape=[jax.ShapeDtypeStruct(a.shape, a.dtype) for a in arrs]
        + [jax.ShapeDtypeStruct((2, cut, a.shape[2]), a.dtype) for a, cut in zip(arrs, cuts)],
        scratch_shapes=[pltpu.SemaphoreType.DMA((6 * n,)), pltpu.SemaphoreType.DMA((6 * n,)),
                        pltpu.SemaphoreType.DMA((n,))],
    )(*arrs)
    return out[:n]


def _swap_halves(arrs):
    n = len(arrs)

    def body(*refs):
        srcs, dsts = refs[:n], refs[n:2 * n]
        send, recv, local = refs[2 * n:]
        x, y, c = _position()

        def remote(a, landing):
            return pltpu.make_async_remote_copy(
                src_ref=srcs[a], dst_ref=dsts[a].at[landing], send_sem=send.at[a], recv_sem=recv.at[a],
                device_id=(x, y, 1 - c), device_id_type=MESH)

        own = [pltpu.make_async_copy(srcs[a], dsts[a].at[c], local.at[a]) for a in range(n)]
        sends = [remote(a, c) for a in range(n)]
        for cp in sends + own:
            cp.start()
        for a in range(n):
            remote(a, 1 - c).wait_recv()
        for cp in sends:
            cp.wait_send()
        for cp in own:
            cp.wait()

    vm = pl.BlockSpec(memory_space=pltpu.VMEM)
    return pl.pallas_call(
        body, name="swap_halves", in_specs=[vm] * n, out_specs=[vm] * n,
        out_shape=[jax.ShapeDtypeStruct((2,) + a.shape, a.dtype) for a in arrs],
        scratch_shapes=[pltpu.SemaphoreType.DMA((n,)), pltpu.SemaphoreType.DMA((n,)), pltpu.SemaphoreType.DMA((n,))],
        compiler_params=pltpu.CompilerParams(vmem_limit_bytes=VMEM_LIMIT),
    )(*arrs)


def _row_block(r):
    return 128 if r % 128 == 0 else r


def _sum_slots(name, slots):
    s, r, n = slots.shape
    rb = _row_block(r)

    def body(s_ref, o_ref):
        total = s_ref[0].astype(f32)
        for d in range(1, s):
            total = total + s_ref[d].astype(f32)
        o_ref[...] = total

    return pl.pallas_call(
        body, name=name, grid=(r // rb,),
        in_specs=[pl.BlockSpec((s, rb, n), lambda i: (0, i, 0))],
        out_specs=pl.BlockSpec((rb, n), lambda i: (i, 0)),
        out_shape=jax.ShapeDtypeStruct((r, n), f32),
        compiler_params=_params(("parallel",), VMEM_LIMIT),
    )(slots)


def _adamw(name, w, g, m, v):
    r, n = w.shape
    if r % 128 == 0 or r * n <= 128 * 1024:
        rb, nb = _row_block(r), n
    else:
        rb, nb = r, LANES

    def body(w_ref, g_ref, m_ref, v_ref, d_ref, nm_ref, nv_ref):
        gv = g_ref[...]
        m2 = ADAM_B1 * m_ref[...] + (1.0 - ADAM_B1) * gv
        v2 = ADAM_B2 * v_ref[...] + (1.0 - ADAM_B2) * (gv * gv)
        m_hat = m2 / (1.0 - ADAM_B1 ** ADAM_STEP)
        v_hat = v2 / (1.0 - ADAM_B2 ** ADAM_STEP)
        d_ref[...] = (-ADAM_LR) * (m_hat / (jnp.sqrt(v_hat) + ADAM_EPS) + ADAM_WD * w_ref[...])
        nm_ref[...] = m2
        nv_ref[...] = v2

    spec = pl.BlockSpec((rb, nb), lambda i, j: (i, j))
    return pl.pallas_call(
        body, name=name, grid=(r // rb, n // nb), in_specs=[spec] * 4, out_specs=[spec] * 3,
        out_shape=[jax.ShapeDtypeStruct((r, n), f32)] * 3,
        compiler_params=_params(("parallel", "parallel"), VMEM_LIMIT),
    )(w, g, m, v)


def _local_step(x2, tgt2, seq, wt):
    nb = x2.shape[0] // seq
    h, qkv = _norm_qkv(x2, wt["pre_w"], wt["w_all"], wt["b_qkv"])
    rest = _mm("in_rest", h, wt["w_all"], wt["b_rest"], bf16, 1024, 1024, w_rows=(3 * D, 5 * D))
    f128 = _mm("in_f", h, wt["w_all"], wt["b_f"], f32, 1024, LANES, w_rows=(8 * D, LANES))
    c = _forget_prep(f128, seq)
    qa, ka = _attn_prep(qkv, c)
    o_att, pa, lse = _attn_fwd(qa, ka, qkv, rest, seq)
    ya = _mm("proj_a", pa, wt["w_a"], None, bf16, 1024, D)
    rnn_w = (wt["conv_w"], wt["conv_b"], wt["wa_d"], wt["wx_d"], wt["ba"], wt["bx"], wt["lam"])
    xc, a, hrec, pr = _rnn_fwd(rest, *rnn_w, seq)
    yr = _mm("proj_r", pr, wt["w_r"], None, bf16, 1024, D)
    do, dy, mrg, loss8, d_post = _out_proj_loss(rest, ya, yr, wt["w_o"], x2, tgt2, wt["post_w"])
    dya, dyr, dmga, dmgr = _out_bwd(do, rest, ya, yr, wt["w_o"])
    doa, dga, delta = _branch_bwd("branch_a_bwd", dya, rest, 0, o_att, wt["w_a"], bf16, head_sums=True)
    dhrec, dgr = _branch_bwd("branch_r_bwd", dyr, rest, 2, hrec, wt["w_r"], bf16)
    d_wo, _ = _tn_mm("dw_out", mrg, do, D)
    d_wa, _ = _tn_mm("dw_branch_a", pa, dya, D)
    d_wr, _ = _tn_mm("dw_branch_r", pr, dyr, D)
    dxr, d_wad, d_wxd, vec = _rnn_bwd(dhrec, a, hrec, xc, rest, *rnn_w, seq)
    dq, dk, dv, dc_pairs = _attn_bwd(qa, ka, qkv, doa, lse, delta, seq)
    dc = dc_pairs.reshape(-1, HEADS // ATT_GROUP, LANES)[:, :, :ATT_GROUP].reshape(-1, HEADS)
    df, db_f = _forget_bwd(_pad_cols(dc, LANES), f128, seq)
    pieces = [dq, dk, dv, dga, dxr, dgr, dmga, dmgr]
    gx, d_pre = _in_bwd(pieces, df, x2, dy, wt["w_all"], wt["pre_w"])
    names = ["q", "k", "v", "ga", "xr", "gr", "mga", "mgr"]
    dws, dbs = [], []
    for nm, piece in zip(names, pieces):
        dw_p, db_p = _tn_mm("dw_in_" + nm, piece, h, D)
        dws.append(dw_p)
        dbs.append(db_p)
    dw_f, _ = _tn_mm("dw_in_f", df, h, D)
    zeros_w = jnp.zeros((IN_TOTAL - IN_USED, D), f32)
    d_w_in = jnp.concatenate(dws[:3] + [dw_f[:HEADS]] + dws[3:] + [zeros_w], axis=0)
    d_b_in = jnp.concatenate(dbs[:3] + [db_f[:, :HEADS]] + dbs[3:] + [zeros_w[:, :1].T], axis=1)
    return dict(loss=loss8[0, 0], grad_x=gx, pre_w=d_pre, w_in=d_w_in, b_in=d_b_in, conv_w=vec[4:8], conv_b=vec[3:4],
                wa_d=d_wad, ba=vec[0:1], wx_d=d_wxd, bx=vec[1:2], lam=vec[2:3], w_a=d_wa, w_r=d_wr, w_o=d_wo,
                post_w=d_post)


def _block_diag(w):
    g, bw, _ = w.shape
    eye = jnp.eye(g, dtype=w.dtype)
    return (w[:, :, None, :] * eye[:, None, :, None]).reshape(g * bw, g * bw)


def _gate_blocks(diag):
    half = diag.shape[1] // 2
    return jnp.stack([diag[:, :half, :half], diag[:, half:, half:]], axis=1).reshape(-1, half, half)


def _pad_cols(a, n):
    return jnp.pad(a, ((0, 0), (0, n - a.shape[1])))


def _pad_rows(a, n):
    return jnp.pad(a, ((0, n - a.shape[0]), (0, 0)))


def kernel(x, pre_norm_w, w_in, b_in, conv_w, conv_b, rg_wa, rg_ba, rg_wx, rg_bx, rg_lambda, w_branch_a, w_branch_r, w_out, post_norm_w, loss_target, m_pre_norm_w, m_w_in, m_b_in, m_conv_w, m_conv_b, m_rg_wa, m_rg_ba, m_rg_wx, m_rg_bx, m_rg_lambda, m_w_branch_a, m_w_branch_r, m_w_out, m_post_norm_w, v_pre_norm_w, v_w_in, v_b_in, v_conv_w, v_conv_b, v_rg_wa, v_rg_ba, v_rg_wx, v_rg_bx, v_rg_lambda, v_w_branch_a, v_w_branch_r, v_w_out, v_post_norm_w):
    nb, seq, _ = x.shape
    chip = 2 * lax.axis_index("x") + lax.axis_index("y")
    n_groups = rg_wa.shape[1]

    w_in_t = jnp.transpose(w_in[0])
    shard_cols = w_in_t.shape[0]
    padded = -(-shard_cols // 32) * 32
    q_end, f_end = 3 * D, 3 * D + HEADS
    segments = [(0, 0, q_end), (q_end, f_end, IN_USED - f_end), (IN_USED - HEADS, q_end, HEADS)]
    g_a, g_r, g_o, g_cw, w_all = _gather_shards(
        [_pad_rows(w_in_t.astype(bf16), padded), w_branch_a[0].astype(bf16), w_branch_r[0].astype(bf16),
         w_out[0].astype(bf16)], conv_w[0], shard_cols, segments, IN_USED - HEADS + LANES)
    wt = dict(
        pre_w=pre_norm_w, post_w=post_norm_w,
        w_all=w_all, b_qkv=b_in[:, :q_end], b_f=_pad_cols(b_in[:, q_end:f_end], LANES), b_rest=b_in[:, f_end:IN_USED],
        w_a=g_a.reshape(D, D), w_r=g_r.reshape(D, D), w_o=g_o.reshape(D, D),
        conv_w=jnp.transpose(g_cw, (1, 0, 2)).reshape(4, D), conv_b=conv_b,
        wa_d=_block_diag(rg_wa[0]).astype(bf16), wx_d=_block_diag(rg_wx[0]).astype(bf16),
        ba=rg_ba, bx=rg_bx, lam=rg_lambda)

    part = _local_step(x.reshape(nb * seq, D), loss_target.reshape(nb * seq, D), seq, wt)
    loss = lax.psum(part["loss"], ("x", "y", "c"))
    grad_x = part["grad_x"].reshape(nb, seq, D)

    small = jnp.concatenate([
        part["pre_w"], _pad_cols(part["b_in"], 10 * D).reshape(10, D), part["conv_b"],
        _gate_blocks(part["wa_d"]).reshape(-1, D), part["ba"],
        _gate_blocks(part["wx_d"]).reshape(-1, D), part["bx"], part["lam"], part["post_w"],
        part["conv_w"]], axis=0)
    n_small = small.shape[0]
    n_rep = n_small - 4
    tot = _allsum_rows(_pad_rows(small, -(-n_small // 8) * 8))
    g_rep = tot[:n_rep]
    g_conv_w = lax.dynamic_slice_in_dim(tot[n_rep:n_small], chip * (D // N_CHIPS), D // N_CHIPS, axis=1)

    def unpack(p):
        o = [0]

        def take(k):
            o[0] += k
            return p[o[0] - k:o[0]]

        pre = take(1)
        b = take(10).reshape(1, 10 * D)[:, :IN_TOTAL]
        cb = take(1)
        wa = take(64).reshape(rg_wa.shape)
        ba = take(1)
        wx = take(64).reshape(rg_wx.shape)
        bx = take(1)
        lam = take(1)
        post = take(1)
        return dict(pre_norm_w=pre, b_in=b, conv_b=cb, rg_wa=wa, rg_ba=ba, rg_wx=wx, rg_bx=bx, rg_lambda=lam,
                    post_norm_w=post)

    grads = unpack(g_rep)
    replicated = dict(
        pre_norm_w=(pre_norm_w, m_pre_norm_w, v_pre_norm_w), b_in=(b_in, m_b_in, v_b_in),
        conv_b=(conv_b, m_conv_b, v_conv_b), rg_wa=(rg_wa, m_rg_wa, v_rg_wa), rg_ba=(rg_ba, m_rg_ba, v_rg_ba),
        rg_wx=(rg_wx, m_rg_wx, v_rg_wx), rg_bx=(rg_bx, m_rg_bx, v_rg_bx),
        rg_lambda=(rg_lambda, m_rg_lambda, v_rg_lambda), post_norm_w=(post_norm_w, m_post_norm_w, v_post_norm_w))
    deltas, new_m, new_v = {}, {}, {}
    for name, (w, m, v) in replicated.items():
        as2d = lambda a: a.reshape(-1, D) if a.ndim > 2 else a
        upd = _adamw("adamw_" + name, as2d(w), as2d(grads[name]), as2d(m), as2d(v))
        deltas[name], new_m[name], new_v[name] = [a.reshape(w.shape) for a in upd]

    p_in = jnp.pad(part["w_in"].reshape(N_CHIPS, shard_cols, D), ((0, 0), (0, padded - shard_cols), (0, 0)))
    p_in = p_in.reshape(N_DEV, padded // 2, D)
    p_aro = jnp.concatenate([part[k].reshape(N_DEV, D // N_DEV, D) for k in ("w_a", "w_r", "w_o")], axis=1)
    s_in, s_aro = _chip_exchange([_pair_reduce("pair_w_in", p_in.astype(bf16)),
                                  _pair_reduce("pair_w_aro", p_aro.astype(bf16))])
    f_in, f_aro = _swap_halves([_sum_slots("sum_w_in", s_in), _sum_slots("sum_w_aro", s_aro)])
    g_w_in_t = f_in.reshape(padded, D)[:shard_cols]
    rows = D // N_DEV
    g_aro = [f_aro[:, i * rows:(i + 1) * rows, :].reshape(2 * rows, D) for i in range(3)]

    w_in_upd = _adamw("adamw_w_in", w_in_t, g_w_in_t, jnp.transpose(m_w_in[0]), jnp.transpose(v_w_in[0]))
    g_w_in, d_w_in, nm_w_in, nv_w_in = [jnp.transpose(a) for a in (g_w_in_t, *w_in_upd)]
    upd_a = _adamw("adamw_w_branch_a", w_branch_a[0], g_aro[0], m_w_branch_a[0], v_w_branch_a[0])
    upd_r = _adamw("adamw_w_branch_r", w_branch_r[0], g_aro[1], m_w_branch_r[0], v_w_branch_r[0])
    upd_o = _adamw("adamw_w_out", w_out[0], g_aro[2], m_w_out[0], v_w_out[0])
    d_aro, nm_aro, nv_aro = zip(upd_a, upd_r, upd_o)
    d_cw, nm_cw, nv_cw = _adamw("adamw_conv_w", conv_w[0], g_conv_w, m_conv_w[0], v_conv_w[0])

    def sharded(t_in, t_aro, t_cw):
        return dict(w_in=t_in[None], conv_w=t_cw[None], w_branch_a=t_aro[0][None], w_branch_r=t_aro[1][None],
                    w_out=t_aro[2][None])

    order = ["pre_norm_w", "w_in", "b_in", "conv_w", "conv_b", "rg_wa", "rg_ba", "rg_wx", "rg_bx", "rg_lambda",
             "w_branch_a", "w_branch_r", "w_out", "post_norm_w"]
    outs = [loss, grad_x]
    for rep, shd in ((grads, sharded(g_w_in, g_aro, g_conv_w)), (deltas, sharded(d_w_in, d_aro, d_cw)),
                     (new_m, sharded(nm_w_in, nm_aro, nm_cw)), (new_v, sharded(nv_w_in, nv_aro, nv_cw))):
        both = {**rep, **shd}
        outs.extend(both[k] for k in order)
    return tuple(outs)
```

```python
import jax
import jax.numpy as jnp
from jax import lax
from jax.experimental import pallas as pl
from jax.experimental.pallas import tpu as pltpu

f32 = jnp.float32
bf16 = jnp.bfloat16

D = 1024
HEADS = 16
HEAD_PAIRS = 8
LANES = 128
NORM_EPS = 1e-6
MASK_VALUE = -1e30
RG_C = 8.0
QK_SCALE = 0.125
TQ = 256
ATT_GROUP = 8
ATT_GROUP_FWD = 16
TL = 256
TM = 512
PREV_ROWS = 16
IN_USED = 8 * D + HEADS
IN_TOTAL = 9 * D + HEADS
N_CHIPS = 4
N_DEV = 8
ADAM_LR, ADAM_B1, ADAM_B2, ADAM_EPS, ADAM_WD, ADAM_STEP = 0.001, 0.9, 0.999, 1e-08, 0.01, 10
VMEM_LIMIT = 56 * 1024 * 1024
MESH = pl.DeviceIdType.MESH


def _dot(a, b):
    return jnp.dot(a, b, preferred_element_type=f32)


def _dot_nt(a, b):
    return lax.dot_general(a, b, (((1,), (1,)), ((), ())), preferred_element_type=f32)


def _dot_tn(a, b):
    return lax.dot_general(a, b, (((0,), (0,)), ((), ())), preferred_element_type=f32)


def _sig(x):
    return 0.5 * jnp.tanh(0.5 * x) + 0.5


def _softplus(x):
    return jnp.maximum(x, 0.0) + jnp.log(1.0 + jnp.exp(-jnp.abs(x)))


def _params(sem, vmem=None):
    return pltpu.CompilerParams(dimension_semantics=sem, vmem_limit_bytes=vmem)


def _tile(tm, width, cb=0):
    return pl.BlockSpec((tm, width), lambda i, cb=cb: (i, cb))


def _whole(shape):
    nd = len(shape)
    return pl.BlockSpec(shape, lambda *_: (0,) * nd)


def _norm_qkv(x, w_pre, w_all, b_qkv, tm=1024):
    t = x.shape[0]
    tm = min(tm, t)
    n = b_qkv.shape[1]

    def body(x_ref, wp_ref, w_ref, b_ref, h_ref, o_ref):
        @pl.when(pl.program_id(1) == 0)
        def _():
            xv = x_ref[...]
            r = lax.rsqrt(jnp.mean(xv * xv, axis=-1, keepdims=True) + NORM_EPS)
            h_ref[...] = (xv * r * wp_ref[...]).astype(bf16)

        o_ref[...] = (_dot_nt(h_ref[...], w_ref[...]) + b_ref[...]).astype(bf16)

    return pl.pallas_call(
        body, name="norm_qkv", grid=(t // tm, n // D),
        in_specs=[pl.BlockSpec((tm, D), lambda i, j: (i, 0)), _whole((1, D)), pl.BlockSpec((D, D), lambda i, j: (j, 0)),
                  pl.BlockSpec((1, D), lambda i, j: (0, j))],
        out_specs=[pl.BlockSpec((tm, D), lambda i, j: (i, 0)), pl.BlockSpec((tm, D), lambda i, j: (i, j))],
        out_shape=[jax.ShapeDtypeStruct((t, D), bf16), jax.ShapeDtypeStruct((t, n), bf16)],
        compiler_params=_params(("parallel", "arbitrary"), VMEM_LIMIT),
    )(x, w_pre, w_all, b_qkv)


def _mm(name, a, w, bias, out_dtype, tm, tn, w_rows=None):
    t, k = a.shape
    tm = min(tm, t)
    w_is_nk = w_rows is not None
    row0, n = w_rows if w_is_nk else (0, w.shape[1])
    assert row0 % tn == 0

    def body(a_ref, w_ref, *refs):
        acc = _dot_nt(a_ref[...], w_ref[...]) if w_is_nk else _dot(a_ref[...], w_ref[...])
        if bias is not None:
            acc = acc + refs[0][...]
        refs[-1][...] = acc.astype(out_dtype)

    in_specs = [pl.BlockSpec((tm, k), lambda i, j: (i, 0)),
                pl.BlockSpec((tn, k), lambda i, j: (row0 // tn + j, 0)) if w_is_nk
                else pl.BlockSpec((k, tn), lambda i, j: (0, j))]
    args = [a, w]
    if bias is not None:
        in_specs.append(pl.BlockSpec((1, tn), lambda i, j: (0, j)))
        args.append(bias)
    return pl.pallas_call(
        body, name=name, grid=(t // tm, n // tn), in_specs=in_specs,
        out_specs=pl.BlockSpec((tm, tn), lambda i, j: (i, j)), out_shape=jax.ShapeDtypeStruct((t, n), out_dtype),
        compiler_params=_params(("parallel", "parallel"), VMEM_LIMIT),
    )(*args)


def _forget_prep(f128, seq):
    t = f128.shape[0]
    nb = seq // LANES

    def body(f_ref, c_ref):
        r = lax.broadcasted_iota(jnp.int32, (LANES, LANES), 0)
        cidx = lax.broadcasted_iota(jnp.int32, (LANES, LANES), 1)
        tri = (r >= cidx).astype(f32)
        carry = jnp.zeros((1, LANES), f32)
        for blk in range(nb):
            fv = f_ref[pl.ds(blk * LANES, LANES), :]
            lf = -_softplus(-fv)
            c_ref[pl.ds(blk * LANES, LANES), :] = (
                jnp.dot(tri, lf, preferred_element_type=f32, precision=lax.Precision.HIGHEST) + carry)
            carry = carry + jnp.sum(lf, axis=0, keepdims=True)

    return pl.pallas_call(
        body, name="forget_prep", grid=(t // seq,),
        in_specs=[pl.BlockSpec((seq, LANES), lambda b: (b, 0))],
        out_specs=pl.BlockSpec((seq, LANES), lambda b: (b, 0)),
        out_shape=jax.ShapeDtypeStruct((t, LANES), f32),
        compiler_params=_params(("parallel",)),
    )(f128)


def _split3(cv):
    hi = cv.astype(bf16)
    r1 = cv - hi.astype(f32)
    mid = r1.astype(bf16)
    lo = (r1 - mid.astype(f32)).astype(bf16)
    return hi, mid, lo


def _attn_prep(qkv, c):
    t = qkv.shape[0]

    def body(q_ref, k_ref, c_ref, qa_ref, ka_ref):
        lane = lax.broadcasted_iota(jnp.int32, (1, LANES), 1)
        cv = c_ref[...]
        one = jnp.ones((), bf16)
        zero = jnp.zeros((), bf16)
        q_ones = jnp.where((lane >= 67) & (lane < 70), one, zero)
        k_ones = jnp.where((lane >= 64) & (lane < 67), one, zero)
        for head in range(HEADS):
            pair = pl.ds((head // 2) * LANES, LANES)
            ch = jnp.sum(jnp.where(lane == head, cv, 0.0), axis=1, keepdims=True)
            hi, mid, lo = _split3(ch)
            q2, k2 = q_ref[:, pair], k_ref[:, pair]
            if head % 2 == 1:
                q2, k2 = pltpu.roll(q2, 64, 1), pltpu.roll(k2, 64, 1)
            qa = jnp.where(lane < 64, q2 * jnp.asarray(QK_SCALE, bf16),
                           jnp.where(lane == 64, hi, jnp.where(lane == 65, mid, jnp.where(lane == 66, lo, q_ones))))
            ka = jnp.where(lane < 64, k2,
                           jnp.where(lane == 67, -hi, jnp.where(lane == 68, -mid, jnp.where(lane == 69, -lo, k_ones))))
            qa_ref[:, pl.ds(head * LANES, LANES)] = qa
            ka_ref[:, pl.ds(head * LANES, LANES)] = ka

    tm = min(TM, t)
    out = pl.BlockSpec((tm, 2 * D), lambda i: (i, 0))
    return pl.pallas_call(
        body, name="attn_prep", grid=(t // tm,),
        in_specs=[_tile(tm, D, 0), _tile(tm, D, 1), _tile(tm, LANES)],
        out_specs=[out, out],
        out_shape=[jax.ShapeDtypeStruct((t, 2 * D), bf16)] * 2,
        compiler_params=_params(("parallel",)),
    )(qkv, qkv, c)


def _attn_fwd(qa, ka, qkv, rest, seq):
    t = qkv.shape[0]
    nb, nq = t // seq, seq // TQ

    hg = ATT_GROUP_FWD
    ng = HEADS // hg

    def body(q_ref, k_ref, v_ref, ga_ref, o_ref, pa_ref, lse_ref, acc_scr):
        qi, gi = pl.program_id(1), pl.program_id(2)
        krow = lax.broadcasted_iota(jnp.int32, (TQ, TQ), 0)
        qcol = lax.broadcasted_iota(jnp.int32, (TQ, TQ), 1)
        acc_scr[...] = jnp.zeros_like(acc_scr)

        def kv_step(kt, carry, masked):
            ks = pl.multiple_of(kt * TQ, TQ)
            sts = [_dot_nt(k_ref[pl.ds(ks, TQ), pl.ds(g * LANES, LANES)], q_ref[:, pl.ds(g * LANES, LANES)])
                   for g in range(hg)]
            if masked:
                sts = [jnp.where(krow <= qcol, st, MASK_VALUE) for st in sts]
            m_new = [jnp.maximum(carry[g][0], jnp.max(sts[g], axis=0, keepdims=True)) for g in range(hg)]
            ps = [jnp.exp(sts[g] - m_new[g]) for g in range(hg)]
            alphas = [jnp.exp(carry[g][0] - m_new[g]) for g in range(hg)]
            phi = [ps[g].astype(bf16) for g in range(hg)]
            plo = [(ps[g] - phi[g].astype(f32)).astype(bf16) for g in range(hg)]
            vs = [v_ref[pl.ds(ks, TQ), pl.ds(j * LANES, LANES)] for j in range(hg // 2)]
            pvs = [_dot_tn(vs[g // 2], phi[g]) + _dot_tn(vs[g // 2], plo[g]) for g in range(hg)]
            olds = [acc_scr[g] for g in range(hg)]
            for g in range(hg):
                acc_scr[g] = alphas[g] * olds[g] + pvs[g]
            return tuple((m_new[g], alphas[g] * carry[g][1] + jnp.sum(ps[g], axis=0, keepdims=True))
                         for g in range(hg))

        init = tuple((jnp.full((1, TQ), MASK_VALUE, f32), jnp.zeros((1, TQ), f32)) for _ in range(hg))
        carry = lax.fori_loop(0, qi, lambda kt, cr: kv_step(kt, cr, False), init)
        stats = kv_step(qi, carry, True)
        drow = lax.broadcasted_iota(jnp.int32, (LANES, TQ), 0)
        for g in range(hg):
            m, l = stats[g]
            lse_ref[0, pl.ds(hg * gi + g, 1), :] = m + jnp.log(l)
        for j in range(hg // 2):
            o2 = jnp.where(drow < 64, acc_scr[2 * j] / stats[2 * j][1], acc_scr[2 * j + 1] / stats[2 * j + 1][1]).T
            o_ref[:, pl.ds(j * LANES, LANES)] = o2
            ga = ga_ref[:, pl.ds(j * LANES, LANES)].astype(f32)
            pa_ref[:, pl.ds(j * LANES, LANES)] = (o2 * (ga * _sig(ga))).astype(bf16)

    vw = hg * 64
    tile = pl.BlockSpec((TQ, vw), lambda b, qi, gi: (b * nq + qi, gi))
    return pl.pallas_call(
        body, name="attn_fwd", grid=(nb, nq, ng),
        in_specs=[pl.BlockSpec((TQ, hg * LANES), lambda b, qi, gi: (b * nq + qi, gi)),
                  pl.BlockSpec((seq, hg * LANES), lambda b, qi, gi: (b, gi)),
                  pl.BlockSpec((seq, vw), lambda b, qi, gi: (b, 2 * ng + gi)), tile],
        out_specs=[tile, tile, pl.BlockSpec((1, HEADS, TQ), lambda b, qi, gi: (b * nq + qi, 0, 0))],
        out_shape=[jax.ShapeDtypeStruct((t, D), f32), jax.ShapeDtypeStruct((t, D), bf16),
                   jax.ShapeDtypeStruct((t // TQ, HEADS, TQ), f32)],
        scratch_shapes=[pltpu.VMEM((hg, LANES, TQ), f32)],
        compiler_params=_params(("parallel", "parallel", "arbitrary"), VMEM_LIMIT),
    )(qa, ka, qkv, rest)


def _shifted_rows(x, top8, prev8, shift, row, row8):
    body = pltpu.roll(x, shift, 0)
    head = jnp.where(row8 < shift, pltpu.roll(prev8, shift, 0), pltpu.roll(top8, shift, 0))
    return body, head


def _rnn_gates(xc, wa_ref, wx_ref, ba_ref, bx_ref, lam_ref):
    xcb = xc.astype(bf16)
    r = _sig(_dot(xcb, wa_ref[...]) + ba_ref[...])
    i = _sig(_dot(xcb, wx_ref[...]) + bx_ref[...])
    sp = _softplus(-lam_ref[...])
    log_a = (-RG_C) * r * sp
    th = jnp.tanh(log_a)
    w1 = (-2.0) * th / (1.0 - th)
    sq = jnp.sqrt(jnp.maximum(w1, 0.0))
    return r, i, sp, log_a, w1, sq


def _conv_tile(x_ref, xprev_ref, has_prev, cw_ref, cb_ref, xc_ref):
    row = lax.broadcasted_iota(jnp.int32, (TL, D), 0)
    row8 = lax.broadcasted_iota(jnp.int32, (8, D), 0)
    x = x_ref[...].astype(f32)
    top8 = x[:8]
    prev8 = jnp.where(has_prev, xprev_ref[...].astype(f32)[PREV_ROWS - 8:], 0.0)
    xc = cb_ref[...] + cw_ref[pl.ds(3, 1), :] * x
    xc8 = cb_ref[...] + cw_ref[pl.ds(3, 1), :] * top8
    for sh in range(1, 4):
        w = cw_ref[pl.ds(3 - sh, 1), :]
        xs, xs8 = _shifted_rows(x, top8, prev8, sh, row, row8)
        xc = xc + w * xs
        xc8 = xc8 + w * xs8
    xc_ref[...] = xc
    xc_ref[pl.ds(0, 8), :] = xc8


def _rnn_fwd(rest, conv_w, conv_b, wa_d, wx_d, ba, bx, lam, seq):
    t = rest.shape[0]
    nb, nt = t // seq, seq // TL

    def body(x_ref, xprev_ref, gr_ref, cw_ref, cb_ref, wa_ref, wx_ref, ba_ref, bx_ref, lam_ref,
             xc_ref, a_ref, h_ref, pr_ref, xc_scr, u_scr, h_scr, carry):
        tt = pl.program_id(1)
        _conv_tile(x_ref, xprev_ref, tt > 0, cw_ref, cb_ref, xc_scr)
        xc = xc_scr[...]
        xc_ref[...] = xc.astype(bf16)
        r, i, sp, log_a, w1, sq = _rnn_gates(xc, wa_ref, wx_ref, ba_ref, bx_ref, lam_ref)
        a_ref[...] = jnp.exp(log_a)
        u_scr[...] = sq * (i * xc)

        @pl.when(tt == 0)
        def _():
            carry[...] = jnp.zeros_like(carry)

        def step(s, h):
            h = a_ref[pl.ds(s, 1), :] * h + u_scr[pl.ds(s, 1), :]
            h_scr[pl.ds(s, 1), :] = h
            return h

        carry[...] = lax.fori_loop(0, TL, step, carry[...], unroll=8)
        gr = gr_ref[...].astype(f32)
        h = h_scr[...]
        h_ref[...] = h.astype(bf16)
        pr_ref[...] = (h * (gr * _sig(gr))).astype(bf16)

    tile = lambda cb: pl.BlockSpec((TL, D), lambda b, tt, cb=cb: (b * nt + tt, cb))
    prev = lambda cb: pl.BlockSpec(
        (PREV_ROWS, D), lambda b, tt, cb=cb: (jnp.maximum((b * nt + tt) * (TL // PREV_ROWS) - 1, 0), cb))
    vec = _whole((1, D))
    return pl.pallas_call(
        body, name="rnn_fwd", grid=(nb, nt),
        in_specs=[tile(1), prev(1), tile(2), _whole((4, D)), vec, _whole((D, D)), _whole((D, D)), vec, vec, vec],
        out_specs=[tile(0)] * 4,
        out_shape=[jax.ShapeDtypeStruct((t, D), dt) for dt in (bf16, f32, bf16, bf16)],
        scratch_shapes=[pltpu.VMEM((TL, D), f32)] * 3 + [pltpu.VMEM((1, D), f32)],
        compiler_params=_params(("parallel", "arbitrary"), VMEM_LIMIT),
    )(rest, rest, rest, conv_w, conv_b, wa_d, wx_d, ba, bx, lam)


def _merge(mga, mgr, ya, yr):
    return (_sig(mga.astype(f32)) * ya.astype(f32) + _sig(mgr.astype(f32)) * yr.astype(f32)).astype(bf16)


def _out_proj_loss(rest, ya, yr, w_out, x, tgt, w_post):
    t = x.shape[0]

    def body(mga_ref, mgr_ref, ya_ref, yr_ref, wo_ref, x_ref, t_ref, w_ref, do_ref, dy_ref, mrg_ref, loss_ref, dwp_ref):
        @pl.when(pl.program_id(0) == 0)
        def _():
            loss_ref[...] = jnp.zeros_like(loss_ref)
            dwp_ref[...] = jnp.zeros_like(dwp_ref)

        mrg = _merge(mga_ref[...], mgr_ref[...], ya_ref[...], yr_ref[...])
        mrg_ref[...] = mrg
        ov = _dot(mrg, wo_ref[...])
        w = w_ref[...]
        r2 = lax.rsqrt(jnp.mean(ov * ov, axis=-1, keepdims=True) + NORM_EPS)
        oh = ov * r2
        e = x_ref[...] + oh * w - t_ref[...]
        loss_ref[...] += 0.5 * jnp.sum(jnp.mean(e * e, axis=-1, keepdims=True))
        dy = e * (1.0 / D)
        dy_ref[...] = dy
        dwp_ref[...] += jnp.sum(dy * oh, axis=0, keepdims=True)
        doh = dy * w
        do_ref[...] = (r2 * (doh - oh * jnp.mean(doh * oh, axis=-1, keepdims=True))).astype(bf16)

    return pl.pallas_call(
        body, name="out_proj_loss", grid=(t // TM,),
        in_specs=[_tile(TM, D, 3), _tile(TM, D, 4), _tile(TM, D), _tile(TM, D), _whole((D, D)), _tile(TM, D),
                  _tile(TM, D), _whole((1, D))],
        out_specs=[_tile(TM, D), _tile(TM, D), _tile(TM, D), _whole((8, LANES)), _whole((1, D))],
        out_shape=[jax.ShapeDtypeStruct((t, D), bf16), jax.ShapeDtypeStruct((t, D), f32),
                   jax.ShapeDtypeStruct((t, D), bf16), jax.ShapeDtypeStruct((8, LANES), f32),
                   jax.ShapeDtypeStruct((1, D), f32)],
        compiler_params=_params(("arbitrary",), VMEM_LIMIT),
    )(rest, rest, ya, yr, w_out, x, tgt, w_post)


def _out_bwd(do, rest, ya, yr, w_out):
    t = do.shape[0]

    def body(do_ref, mga_ref, mgr_ref, ya_ref, yr_ref, w_ref, dya_ref, dyr_ref, dmga_ref, dmgr_ref):
        sa, sr = _sig(mga_ref[...].astype(f32)), _sig(mgr_ref[...].astype(f32))
        ya, yr = ya_ref[...].astype(f32), yr_ref[...].astype(f32)
        dm = _dot_nt(do_ref[...], w_ref[...])
        dya_ref[...] = (dm * sa).astype(bf16)
        dyr_ref[...] = (dm * sr).astype(bf16)
        dmga_ref[...] = (dm * ya * sa * (1.0 - sa)).astype(bf16)
        dmgr_ref[...] = (dm * yr * sr * (1.0 - sr)).astype(bf16)

    return pl.pallas_call(
        body, name="out_bwd", grid=(t // TM,),
        in_specs=[_tile(TM, D), _tile(TM, D, 3), _tile(TM, D, 4), _tile(TM, D), _tile(TM, D), _whole((D, D))],
        out_specs=[_tile(TM, D)] * 4,
        out_shape=[jax.ShapeDtypeStruct((t, D), bf16)] * 4,
        compiler_params=_params(("parallel",), VMEM_LIMIT),
    )(do, rest, rest, ya, yr, w_out)


def _branch_bwd(name, dyb, rest, gate_cb, act, w, act_grad_dtype, head_sums=False):
    t = dyb.shape[0]

    def body(dy_ref, g_ref, act_ref, w_ref, dact_ref, dg_ref, *delta_ref):
        dp = _dot_nt(dy_ref[...], w_ref[...])
        g = g_ref[...].astype(f32)
        sg = _sig(g)
        act = act_ref[...].astype(f32)
        dact = (dp * (g * sg)).astype(act_grad_dtype)
        dact_ref[...] = dact
        dg_ref[...] = (dp * act * (sg * (1.0 + g * (1.0 - sg)))).astype(bf16)
        if head_sums:
            ch = lax.broadcasted_iota(jnp.int32, (D, LANES), 0)
            hd = lax.broadcasted_iota(jnp.int32, (D, LANES), 1)
            pick = (ch // 64 == hd).astype(bf16)
            per_head = sum(_dot(piece, pick) for piece in _split3(dact.astype(f32) * act))
            for s in range(TM // TQ):
                delta_ref[0][s] = per_head[s * TQ:(s + 1) * TQ].T[:HEADS, :]

    out_specs = [_tile(TM, D), _tile(TM, D)]
    out_shape = [jax.ShapeDtypeStruct((t, D), act_grad_dtype), jax.ShapeDtypeStruct((t, D), bf16)]
    if head_sums:
        out_specs.append(pl.BlockSpec((TM // TQ, HEADS, TQ), lambda i: (i, 0, 0)))
        out_shape.append(jax.ShapeDtypeStruct((t // TQ, HEADS, TQ), f32))
    return pl.pallas_call(
        body, name=name, grid=(t // TM,),
        in_specs=[_tile(TM, D), _tile(TM, D, gate_cb), _tile(TM, D), _whole((D, D))],
        out_specs=out_specs, out_shape=out_shape,
        compiler_params=_params(("parallel",), VMEM_LIMIT),
    )(dyb, rest, act, w)


def _rnn_bwd(dh, a, h, xc, rest, conv_w, conv_b, wa_d, wx_d, ba, bx, lam, seq):
    t = dh.shape[0]
    nb, nt = t // seq, seq // TL
    diag = (D // LANES, LANES, LANES)

    def body(dh_ref, a_ref, h_ref, hprev_ref, xc_ref, x_ref, xprev_ref, cw_ref, cb_ref, wa_ref, wx_ref,
             ba_ref, bx_ref, lam_ref, dxr_ref, dwa_ref, dwx_ref, vec_ref, g_scr, dxc_scr, dxr_scr, qcarry, dxc_next):
        b, tt = pl.program_id(0), pl.program_id(1)
        rt = nt - 1 - tt

        @pl.when((b == 0) & (tt == 0))
        def _():
            dwa_ref[...] = jnp.zeros_like(dwa_ref)
            dwx_ref[...] = jnp.zeros_like(dwx_ref)
            vec_ref[...] = jnp.zeros_like(vec_ref)

        @pl.when(tt == 0)
        def _():
            qcarry[...] = jnp.zeros_like(qcarry)
            dxc_next[...] = jnp.zeros_like(dxc_next)

        g_scr[...] = dh_ref[...].astype(f32)

        def step(k, q):
            s = TL - 1 - k
            g = g_scr[pl.ds(s, 1), :] + q
            g_scr[pl.ds(s, 1), :] = g
            return a_ref[pl.ds(s, 1), :] * g

        qcarry[...] = lax.fori_loop(0, TL, step, qcarry[...], unroll=8)

        row = lax.broadcasted_iota(jnp.int32, (TL, D), 0)
        row8 = lax.broadcasted_iota(jnp.int32, (8, D), 0)
        g = g_scr[...]
        av = a_ref[...]
        xc = xc_ref[...].astype(f32)
        hlast = jnp.where(rt > 0, hprev_ref[...].astype(f32)[PREV_ROWS - 1:], 0.0)
        hp = jnp.where(row == 0, hlast, pltpu.roll(h_ref[...].astype(f32), 1, 0))
        r, i, sp, log_a, w1, sq = _rnn_gates(xc, wa_ref, wx_ref, ba_ref, bx_ref, lam_ref)
        dix = g * sq
        di = dix * xc
        dxc = dix * i
        dsq = g * (i * xc)
        dlog_a = g * hp * av - dsq * jnp.where(sq > 0.0, (1.0 - w1) / sq, 0.0)
        dpr = (dlog_a * ((-RG_C) * sp)) * r * (1.0 - r)
        dpi = di * i * (1.0 - i)
        dprb, dpib, xcb = dpr.astype(bf16), dpi.astype(bf16), xc.astype(bf16)
        dxc = dxc + _dot_nt(dprb, wa_ref[...]) + _dot_nt(dpib, wx_ref[...])
        for j in range(D // LANES):
            cols = slice(j * LANES, (j + 1) * LANES)
            dwa_ref[j] += _dot_tn(xcb[:, cols], dprb[:, cols])
            dwx_ref[j] += _dot_tn(xcb[:, cols], dpib[:, cols])
        vec_ref[pl.ds(0, 1), :] += jnp.sum(dpr, axis=0, keepdims=True)
        vec_ref[pl.ds(1, 1), :] += jnp.sum(dpi, axis=0, keepdims=True)
        dsp = jnp.sum(dlog_a * ((-RG_C) * r), axis=0, keepdims=True)
        vec_ref[pl.ds(2, 1), :] += dsp * (-_sig(-lam_ref[...]))
        vec_ref[pl.ds(3, 1), :] += jnp.sum(dxc, axis=0, keepdims=True)

        dxc_scr[...] = dxc
        bot8 = dxc_scr[pl.ds(TL - 8, 8), :]
        nxt8 = dxc_next[...]
        dxr = cw_ref[pl.ds(3, 1), :] * dxc
        dxr8 = cw_ref[pl.ds(3, 1), :] * bot8
        for sh in range(1, 4):
            w = cw_ref[pl.ds(3 - sh, 1), :]
            dxr = dxr + w * pltpu.roll(dxc, TL - sh, 0)
            dxr8 = dxr8 + w * jnp.where(row8 < 8 - sh, pltpu.roll(bot8, 8 - sh, 0), pltpu.roll(nxt8, 8 - sh, 0))
        dxr_scr[...] = dxr
        dxr_scr[pl.ds(TL - 8, 8), :] = dxr8
        dxr_ref[...] = dxr_scr[...].astype(bf16)
        dxc_next[...] = dxc_scr[pl.ds(0, 8), :]

        x = x_ref[...].astype(f32)
        prev8 = jnp.where(rt > 0, xprev_ref[...].astype(f32)[PREV_ROWS - 8:], 0.0)
        dxc_top8 = dxc_scr[pl.ds(0, 8), :]
        vec_ref[pl.ds(7, 1), :] += jnp.sum(dxc * x, axis=0, keepdims=True)
        for sh in range(1, 4):
            inside = jnp.sum(dxc * jnp.where(row >= sh, pltpu.roll(x, sh, 0), 0.0), axis=0, keepdims=True)
            above = jnp.sum(dxc_top8 * jnp.where(row8 < sh, pltpu.roll(prev8, sh, 0), 0.0), axis=0, keepdims=True)
            vec_ref[pl.ds(7 - sh, 1), :] += inside + above

    tile = lambda cb: pl.BlockSpec((TL, D), lambda b, tt, cb=cb: (b * nt + nt - 1 - tt, cb))
    prev = lambda cb: pl.BlockSpec(
        (PREV_ROWS, D), lambda b, tt, cb=cb: (jnp.maximum((b * nt + nt - 1 - tt) * (TL // PREV_ROWS) - 1, 0), cb))
    vec = _whole((1, D))
    return pl.pallas_call(
        body, name="rnn_bwd", grid=(nb, nt),
        in_specs=[tile(0), tile(0), tile(0), prev(0), tile(0), tile(1), prev(1),
                  _whole((4, D)), vec, _whole((D, D)), _whole((D, D)), vec, vec, vec],
        out_specs=[tile(0), _whole(diag), _whole(diag), _whole((8, D))],
        out_shape=[jax.ShapeDtypeStruct((t, D), bf16), jax.ShapeDtypeStruct(diag, f32),
                   jax.ShapeDtypeStruct(diag, f32), jax.ShapeDtypeStruct((8, D), f32)],
        scratch_shapes=[pltpu.VMEM((TL, D), f32), pltpu.VMEM((TL, D), f32), pltpu.VMEM((TL, D), f32),
                        pltpu.VMEM((1, D), f32), pltpu.VMEM((8, D), f32)],
        compiler_params=_params(("arbitrary", "arbitrary"), VMEM_LIMIT),
    )(dh, a, h, h, xc, rest, rest, conv_w, conv_b, wa_d, wx_d, ba, bx, lam)


def _attn_bwd(qa, ka, qkv, doa, lse, delta, seq):
    t = qkv.shape[0]
    nb, nq = t // seq, seq // TQ
    hg = ATT_GROUP
    ng, npair = HEADS // hg, hg // 2

    def body(qa_ref, ka_ref, q_ref, k_ref, v_ref, do_ref, lse_ref, dl_ref, dq_ref, dk_ref, dv_ref, dc_ref,
             dqt_scr, dk_scr, dv_scr, ds_scr, kht_scr):
        gi, kt = pl.program_id(1), pl.program_id(2)
        lane = lax.broadcasted_iota(jnp.int32, (1, LANES), 1)
        krow = lax.broadcasted_iota(jnp.int32, (TQ, TQ), 0)
        qcol = lax.broadcasted_iota(jnp.int32, (TQ, TQ), 1)
        lmask = [(lane // 64) == hh for hh in range(2)]
        scale = jnp.asarray(QK_SCALE, bf16)

        @pl.when(kt == 0)
        def _():
            dqt_scr[...] = jnp.zeros_like(dqt_scr)

        dk_scr[...] = jnp.zeros_like(dk_scr)
        dv_scr[...] = jnp.zeros_like(dv_scr)
        ds_scr[...] = jnp.zeros_like(ds_scr)
        for g in range(hg):
            k2 = k_ref[:, pl.ds((g // 2) * LANES, LANES)]
            kht_scr[g] = jnp.where(lmask[g % 2], k2, jnp.zeros_like(k2)).T

        def q_step(qt, masked):
            qs = pl.multiple_of(qt * TQ, TQ)
            heads = range(hg)
            do2 = [do_ref[pl.ds(qs, TQ), pl.ds(j * LANES, LANES)] for j in range(npair)]
            q2 = [q_ref[pl.ds(qs, TQ), pl.ds(j * LANES, LANES)] for j in range(npair)]
            doh = [jnp.where(lmask[g % 2], do2[g // 2], jnp.zeros_like(do2[0])) for g in heads]
            qh = [jnp.where(lmask[g % 2], q2[g // 2], jnp.zeros_like(q2[0])) * scale for g in heads]
            st = [_dot_nt(ka_ref[:, pl.ds(g * LANES, LANES)], qa_ref[pl.ds(qs, TQ), pl.ds(g * LANES, LANES)])
                  for g in heads]
            if masked:
                st = [jnp.where(krow <= qcol, s, MASK_VALUE) for s in st]
            dp = [_dot_nt(v_ref[:, pl.ds((g // 2) * LANES, LANES)], doh[g]) for g in heads]
            p = [jnp.exp(st[g] - lse_ref[qt, pl.ds(hg * gi + g, 1), :]) for g in heads]
            ds = [p[g] * (dp[g] - dl_ref[qt, pl.ds(hg * gi + g, 1), :]) for g in heads]
            pb = [x.astype(bf16) for x in p]
            dsb = [x.astype(bf16) for x in ds]
            for j in range(npair):
                a, b = 2 * j, 2 * j + 1
                dv_scr[j] += _dot(pb[a], doh[a]) + _dot(pb[b], doh[b])
                dk_scr[j] += _dot(dsb[a], qh[a]) + _dot(dsb[b], qh[b])
                dqt_scr[qt, j] += (_dot(kht_scr[a], dsb[a]) + _dot(kht_scr[b], dsb[b])) * QK_SCALE
            for g in heads:
                ds_scr[g] += ds[g][:, :LANES] + ds[g][:, LANES:]

        q_step(kt, True)

        def loop_body(qt, carry):
            q_step(qt, False)
            return carry

        lax.fori_loop(kt + 1, nq, loop_body, 0)

        dc = jnp.zeros((TQ, LANES), f32)
        for g in range(hg):
            dc = jnp.where(lane == g, -jnp.sum(ds_scr[g], axis=1, keepdims=True), dc)
        dc_ref[...] = dc
        for j in range(npair):
            dk_ref[:, pl.ds(j * LANES, LANES)] = dk_scr[j].astype(bf16)
            dv_ref[:, pl.ds(j * LANES, LANES)] = dv_scr[j].astype(bf16)

        @pl.when(kt == nq - 1)
        def _():
            for qt in range(nq):
                for j in range(npair):
                    dq_ref[pl.ds(qt * TQ, TQ), pl.ds(j * LANES, LANES)] = dqt_scr[qt, j].T.astype(bf16)

    vw = hg * 64
    seqspec = pl.BlockSpec((seq, vw), lambda b, gi, kt: (b, gi))
    kspec = lambda off: pl.BlockSpec((TQ, vw), lambda b, gi, kt: (b * nq + kt, off + gi))
    rowspec = pl.BlockSpec((nq, HEADS, TQ), lambda b, gi, kt: (b, 0, 0))
    return pl.pallas_call(
        body, name="attn_bwd", grid=(nb, ng, nq),
        in_specs=[pl.BlockSpec((seq, hg * LANES), lambda b, gi, kt: (b, gi)),
                  pl.BlockSpec((TQ, hg * LANES), lambda b, gi, kt: (b * nq + kt, gi)),
                  seqspec, kspec(ng), kspec(2 * ng), seqspec, rowspec, rowspec],
        out_specs=[seqspec, kspec(0), kspec(0), pl.BlockSpec((TQ, LANES), lambda b, gi, kt: (b * nq + kt, gi))],
        out_shape=[jax.ShapeDtypeStruct((t, D), bf16)] * 3 + [jax.ShapeDtypeStruct((t, ng * LANES), f32)],
        scratch_shapes=[pltpu.VMEM((nq, npair, LANES, TQ), f32), pltpu.VMEM((npair, TQ, LANES), f32),
                        pltpu.VMEM((npair, TQ, LANES), f32), pltpu.VMEM((hg, TQ, LANES), f32),
                        pltpu.VMEM((hg, LANES, TQ), bf16)],
        compiler_params=_params(("parallel", "parallel", "arbitrary"), VMEM_LIMIT),
    )(qa, ka, qkv, qkv, qkv, doa, lse, delta)


def _forget_bwd(dc, f128, seq):
    t = f128.shape[0]
    nb = seq // LANES

    def body(dc_ref, f_ref, df_ref, dbf_ref):
        @pl.when(pl.program_id(0) == 0)
        def _():
            dbf_ref[...] = jnp.zeros_like(dbf_ref)

        r = lax.broadcasted_iota(jnp.int32, (LANES, LANES), 0)
        cidx = lax.broadcasted_iota(jnp.int32, (LANES, LANES), 1)
        tri = (r <= cidx).astype(f32)
        carry = jnp.zeros((1, LANES), f32)
        total = jnp.zeros((1, LANES), f32)
        for blk in reversed(range(nb)):
            dcb = dc_ref[pl.ds(blk * LANES, LANES), :]
            dlf = jnp.dot(tri, dcb, preferred_element_type=f32, precision=lax.Precision.HIGHEST) + carry
            df = dlf * _sig(-f_ref[pl.ds(blk * LANES, LANES), :])
            df_ref[pl.ds(blk * LANES, LANES), :] = df.astype(bf16)
            total = total + jnp.sum(df, axis=0, keepdims=True)
            carry = carry + jnp.sum(dcb, axis=0, keepdims=True)
        dbf_ref[...] += total

    return pl.pallas_call(
        body, name="forget_bwd", grid=(t // seq,),
        in_specs=[pl.BlockSpec((seq, LANES), lambda b: (b, 0)), pl.BlockSpec((seq, LANES), lambda b: (b, 0))],
        out_specs=[pl.BlockSpec((seq, LANES), lambda b: (b, 0)), _whole((1, LANES))],
        out_shape=[jax.ShapeDtypeStruct((t, LANES), bf16), jax.ShapeDtypeStruct((1, LANES), f32)],
        compiler_params=_params(("arbitrary",)),
    )(dc, f128)


def _in_bwd(dz, df, x, dy, w_all, w_pre):
    t = x.shape[0]
    n_dz = len(dz)

    def body(*refs):
        dz_refs = refs[:n_dz]
        df_ref, x_ref, dy_ref, w_ref, wp_ref, gx_ref, dwp_ref = refs[n_dz:]

        @pl.when(pl.program_id(0) == 0)
        def _():
            dwp_ref[...] = jnp.zeros_like(dwp_ref)

        dh = _dot(df_ref[...], w_ref[pl.ds(n_dz * D, LANES), :])
        for p in range(n_dz):
            dh = dh + _dot(dz_refs[p][...], w_ref[pl.ds(p * D, D), :])
        xv = x_ref[...]
        r1 = lax.rsqrt(jnp.mean(xv * xv, axis=-1, keepdims=True) + NORM_EPS)
        xh = xv * r1
        dwp_ref[...] += jnp.sum(dh * xh, axis=0, keepdims=True)
        dxh = dh * wp_ref[...]
        gx_ref[...] = dy_ref[...] + r1 * (dxh - xh * jnp.mean(dxh * xh, axis=-1, keepdims=True))

    once = lambda shape: pl.BlockSpec(shape, lambda i: (0, 0), pipeline_mode=pl.Buffered(1))
    return pl.pallas_call(
        body, name="in_bwd", grid=(t // TM,),
        in_specs=[_tile(TM, D)] * n_dz + [_tile(TM, LANES), _tile(TM, D), _tile(TM, D), once(w_all.shape),
                  _whole((1, D))],
        out_specs=[_tile(TM, D), _whole((1, D))],
        out_shape=[jax.ShapeDtypeStruct((t, D), f32), jax.ShapeDtypeStruct((1, D), f32)],
        compiler_params=_params(("arbitrary",), VMEM_LIMIT),
    )(*dz, df, x, dy, w_all, w_pre)


def _tn_mm(name, a, b, tn, out_dtype=f32, tk=2048):
    t, k = a.shape
    tk = min(tk, t)
    n = b.shape[1]
    nk = t // tk

    def body(a_ref, b_ref, o_ref, s_ref, acc_ref):
        j, kk = pl.program_id(0), pl.program_id(1)

        @pl.when(kk == 0)
        def _():
            acc_ref[...] = jnp.zeros_like(acc_ref)

        @pl.when((j == 0) & (kk == 0))
        def _():
            s_ref[...] = jnp.zeros_like(s_ref)

        av = a_ref[...]
        acc_ref[...] += _dot_tn(av, b_ref[...])

        @pl.when(j == 0)
        def _():
            s_ref[...] += jnp.sum(av.astype(f32), axis=0, keepdims=True)

        @pl.when(kk == nk - 1)
        def _():
            o_ref[...] = acc_ref[...].astype(out_dtype)

    return pl.pallas_call(
        body, name=name, grid=(n // tn, nk),
        in_specs=[pl.BlockSpec((tk, k), lambda j, kk: (kk, 0)), pl.BlockSpec((tk, tn), lambda j, kk: (kk, j))],
        out_specs=[pl.BlockSpec((k, tn), lambda j, kk: (0, j)), _whole((1, k))],
        out_shape=[jax.ShapeDtypeStruct((k, n), out_dtype), jax.ShapeDtypeStruct((1, k), f32)],
        scratch_shapes=[pltpu.VMEM((k, tn), f32)],
        compiler_params=_params(("arbitrary", "arbitrary"), VMEM_LIMIT),
    )(a, b)


def _position():
    return lax.axis_index("x"), lax.axis_index("y"), lax.axis_index("c")


ROW_BLOCK = 128


def _pick_rows(layout, first, count):
    acc = jnp.zeros((ROW_BLOCK, D), f32)
    seg_start = 0
    for ref, ref_row, rows in layout:
        lo, hi = max(first, seg_start), min(first + count, seg_start + rows)
        if lo < hi and ref is not None:
            off, take, done = ref_row + lo - seg_start, hi - lo, lo - first
            start = off // 16 * 16
            win = -(-(off - start + take) // 16) * 16
            r = lax.broadcasted_iota(jnp.int32, (ROW_BLOCK, win), 0)
            col = lax.broadcasted_iota(jnp.int32, (ROW_BLOCK, win), 1)
            pick = ((col - r == off - start - done) & (r >= done) & (r < done + take)).astype(bf16)
            acc = acc + _dot(pick, ref[pl.ds(start, win), :])
        seg_start += rows
    return acc


def _assemble_rows(shards_ref, shard_rows, segments, out_ref):
    layout = [(shards_ref.at[j], 0, shard_rows) for j in range(shards_ref.shape[0])]
    for out0, log0, count in segments:
        for b0 in range(0, count, ROW_BLOCK):
            block = _pick_rows(layout, log0 + b0, min(ROW_BLOCK, count - b0))
            out_ref[pl.ds(out0 + b0, ROW_BLOCK), :] = block.astype(bf16)


def _pack_pieces(blocks, shard_rows, padded):
    arrays = [a for a, _ in blocks if a is not None]
    piece_rows = padded // 2

    def body(*refs):
        out_ref = refs[-1]
        it = iter(refs[:-1])
        layout = [(None if a is None else next(it), 0, rows) for a, rows in blocks]
        for k in range(N_DEV):
            chip, half = divmod(k, 2)
            for b0 in range(0, piece_rows, ROW_BLOCK):
                n = min(ROW_BLOCK, piece_rows - b0)
                in_shard = half * piece_rows + b0
                count = max(0, min(n, shard_rows - in_shard))
                block = _pick_rows(layout, chip * shard_rows + in_shard, count)
                out_ref[k, pl.ds(b0, n), :] = block[:n].astype(bf16)

    vm = pl.BlockSpec(memory_space=pltpu.VMEM)
    return pl.pallas_call(
        body, name="pack_pieces", in_specs=[vm] * len(arrays), out_specs=vm,
        out_shape=jax.ShapeDtypeStruct((N_DEV, piece_rows, D), bf16),
        compiler_params=pltpu.CompilerParams(vmem_limit_bytes=VMEM_LIMIT),
    )(*arrays)


def _gather_shards(parts, small, shard_rows, segments, out_rows):
    n = len(parts)
    halves = [p.shape[0] // 2 for p in parts]
    cuts = [-(-h // 32) * 16 for h in halves]
    n_direct, n_relay, n_sib = 4 * n, 2 * n, 6 * n

    def body(*refs):
        srcs, small_src = refs[:n], refs[n]
        dsts, small_dst, whole_ref = refs[n + 1:2 * n + 1], refs[2 * n + 1], refs[2 * n + 2]
        send, recv, local = refs[2 * n + 3:]
        x, y, c = _position()
        me = 2 * x + y
        chips = [(1 - x, y), (x, 1 - y), (1 - x, 1 - y)]
        ids = [2 * px + py for px, py in chips]

        def rows(a, half, quarter):
            start = half * halves[a] + (cuts[a] if quarter else 0)
            return pl.ds(start, halves[a] - cuts[a] if quarter else cuts[a])

        def landing(a, shard, half, quarter):
            return dsts[a].at[shard, rows(a, half, quarter), :]

        def direct(a, nb, quarter, shard):
            k = (a * 2 + nb) * 2 + quarter
            px, py = chips[nb]
            return pltpu.make_async_remote_copy(
                src_ref=srcs[a].at[rows(a, c, quarter), :], dst_ref=landing(a, shard, c, quarter),
                send_sem=send.at[k], recv_sem=recv.at[k], device_id=(px, py, c), device_id_type=MESH)

        def relay(a, quarter, shard):
            k = n_direct + a * 2 + quarter
            px, py = chips[1 - quarter]
            return pltpu.make_async_remote_copy(
                src_ref=landing(a, shard, c, quarter), dst_ref=landing(a, shard, c, quarter),
                send_sem=send.at[k], recv_sem=recv.at[k], device_id=(px, py, c), device_id_type=MESH)

        def to_sibling(a, origin, quarter, half):
            k = n_direct + n_relay + (a * 3 + origin) * 2 + quarter
            return pltpu.make_async_remote_copy(
                src_ref=landing(a, ids[origin], half, quarter), dst_ref=landing(a, ids[origin], half, quarter),
                send_sem=send.at[k], recv_sem=recv.at[k], device_id=(x, y, 1 - c), device_id_type=MESH)

        def small_copy(j, shard):
            k = n_direct + n_relay + n_sib + j
            px, py = chips[j]
            return pltpu.make_async_remote_copy(
                src_ref=small_src, dst_ref=small_dst.at[shard], send_sem=send.at[k], recv_sem=recv.at[k],
                device_id=(px, py, c), device_id_type=MESH)

        own = [pltpu.make_async_copy(srcs[a], dsts[a].at[me], local.at[a]) for a in range(n)]
        own.append(pltpu.make_async_copy(small_src, small_dst.at[me], local.at[n]))
        for cp in own:
            cp.start()
        sent = [direct(a, nb, q, me) for q in range(2) for a in range(n) for nb in range(2)]
        sent += [small_copy(j, me) for j in range(3)]
        for cp in sent:
            cp.start()

        def passed_on(cp):
            cp.start()
            sent.append(cp)

        for q in range(2):
            for a in range(n):
                for nb in range(2):
                    direct(a, nb, q, ids[nb]).wait_recv()
                    passed_on(to_sibling(a, nb, q, c))
                    if nb == q:
                        passed_on(relay(a, q, ids[nb]))
        for a in range(n):
            for q in range(2):
                relay(a, q, ids[2]).wait_recv()
                passed_on(to_sibling(a, 2, q, c))
        for j in range(3):
            small_copy(j, ids[j]).wait_recv()
            for a in range(n):
                for q in range(2):
                    to_sibling(a, j, q, 1 - c).wait_recv()
        for cp in sent:
            cp.wait_send()
        for cp in own:
            cp.wait()
        _assemble_rows(dsts[0], shard_rows, segments, whole_ref)

    vm = pl.BlockSpec(memory_space=pltpu.VMEM)
    n_sems = n_direct + n_relay + n_sib + 3
    out = pl.pallas_call(
        body, name="gather_shards",
        in_specs=[vm] * (n + 1), out_specs=[vm] * (n + 2),
        out_shape=[jax.ShapeDtypeStruct((N_CHIPS,) + p.shape, p.dtype) for p in parts + [small]]
        + [jax.ShapeDtypeStruct((out_rows, parts[0].shape[1]), parts[0].dtype)],
        scratch_shapes=[pltpu.SemaphoreType.DMA((n_sems,)), pltpu.SemaphoreType.DMA((n_sems,)),
                        pltpu.SemaphoreType.DMA((n + 1,))],
        compiler_params=pltpu.CompilerParams(vmem_limit_bytes=VMEM_LIMIT),
    )(*parts, small)
    return out[1:]


def _allsum_rows(part):
    rows_n = part.shape[0]

    def body(x_ref, gath_ref, sum_ref, send_sems, recv_sems, local_sem):
        x, y, c = _position()
        me, sibling = (x, y, c), (x, y, 1 - c)
        chips = [(1 - x, y), (x, 1 - y), (1 - x, 1 - y)]

        def rows(px, py, pc):
            return gath_ref.at[pl.ds((4 * px + 2 * py + pc) * rows_n, rows_n), :]

        def copy(k, block, to, src=None):
            return pltpu.make_async_remote_copy(
                src_ref=rows(*block) if src is None else src, dst_ref=rows(*block),
                send_sem=send_sems.at[k], recv_sem=recv_sems.at[k], device_id=to, device_id_type=MESH)

        mine = pltpu.make_async_copy(x_ref, rows(*me), local_sem)
        mine.start()
        first = [copy(0, me, sibling, src=x_ref)]
        first += [copy(1 + j, me, (*chip, c), src=x_ref) for j, chip in enumerate(chips)]
        for cp in first:
            cp.start()
        passed = [copy(4 + j, (*chip, c), sibling) for j, chip in enumerate(chips)]
        for j, chip in enumerate(chips):
            copy(1 + j, (*chip, c), me).wait_recv()
            passed[j].start()
        copy(0, sibling, me).wait_recv()
        for j, chip in enumerate(chips):
            copy(4 + j, (*chip, 1 - c), me).wait_recv()
        for cp in first + passed:
            cp.wait_send()
        mine.wait()
        total = gath_ref[pl.ds(0, rows_n), :]
        for d in range(1, N_DEV):
            total = total + gath_ref[pl.ds(d * rows_n, rows_n), :]
        sum_ref[...] = total

    vm = pl.BlockSpec(memory_space=pltpu.VMEM)
    return pl.pallas_call(
        body, name="allsum_rows", in_specs=[vm], out_specs=[vm, vm],
        out_shape=[jax.ShapeDtypeStruct((N_DEV * rows_n, D), f32), jax.ShapeDtypeStruct((rows_n, D), f32)],
        scratch_shapes=[pltpu.SemaphoreType.DMA((7,)), pltpu.SemaphoreType.DMA((7,)), pltpu.SemaphoreType.DMA],
    )(part)[1]


PAIR_ROWS = 16


def _pair_reduce(name, pieces):
    _, r, n = pieces.shape

    def body(p_ref, o_ref, land, send, recv):
        x, y, c = _position()

        def remote(j, half):
            return pltpu.make_async_remote_copy(
                src_ref=p_ref.at[2 * j + half], dst_ref=land.at[j], send_sem=send.at[j], recv_sem=recv.at[j],
                device_id=(x, y, 1 - c), device_id_type=MESH)

        sends = [remote(j, 1 - c) for j in range(N_CHIPS)]
        for cp in sends:
            cp.start()
        for j in range(N_CHIPS):
            remote(j, c).wait_recv()

            def add_rows(i, carry, j=j):
                rows = pl.ds(pl.multiple_of(i * PAIR_ROWS, PAIR_ROWS), PAIR_ROWS)
                o_ref[j, rows, :] = (p_ref[2 * j + c, rows, :].astype(f32) + land[j, rows, :].astype(f32)).astype(bf16)
                return carry

            lax.fori_loop(0, r // PAIR_ROWS, add_rows, 0)
        for cp in sends:
            cp.wait_send()

    vm = pl.BlockSpec(memory_space=pltpu.VMEM)
    return pl.pallas_call(
        body, name=name, in_specs=[vm], out_specs=vm,
        out_shape=jax.ShapeDtypeStruct((N_CHIPS, r, n), bf16),
        scratch_shapes=[pltpu.VMEM((N_CHIPS, r, n), bf16), pltpu.SemaphoreType.DMA((N_CHIPS,)),
                        pltpu.SemaphoreType.DMA((N_CHIPS,))],
        compiler_params=pltpu.CompilerParams(vmem_limit_bytes=VMEM_LIMIT),
    )(pieces)


def _chip_exchange(arrs):
    n = len(arrs)
    heights = [a.shape[1] for a in arrs]
    cuts = [-(-r // 32) * 16 for r in heights]

    def body(*refs):
        srcs, dsts, relays = refs[:n], refs[n:2 * n], refs[2 * n:3 * n]
        send, recv, local = refs[3 * n:]
        x, y, c = _position()
        me = 2 * x + y
        chips = [(1 - x, y), (x, 1 - y), (1 - x, 1 - y)]
        ids = [2 * px + py for px, py in chips]

        def rows(a, quarter):
            return pl.ds(cuts[a], heights[a] - cuts[a]) if quarter else pl.ds(0, cuts[a])

        def held(a, quarter):
            size = heights[a] - cuts[a] if quarter else cuts[a]
            return relays[a].at[quarter, pl.ds(0, size), :]

        def direct(a, nb, piece, landing):
            px, py = chips[nb]
            return pltpu.make_async_remote_copy(
                src_ref=srcs[a].at[piece], dst_ref=dsts[a].at[landing], send_sem=send.at[a * 2 + nb],
                recv_sem=recv.at[a * 2 + nb], device_id=(px, py, c), device_id_type=MESH)

        def first_hop(a, quarter):
            k = 2 * n + a * 2 + quarter
            px, py = chips[quarter]
            return pltpu.make_async_remote_copy(
                src_ref=srcs[a].at[ids[2], rows(a, quarter), :], dst_ref=held(a, quarter), send_sem=send.at[k],
                recv_sem=recv.at[k], device_id=(px, py, c), device_id_type=MESH)

        def second_hop(a, quarter, origin):
            k = 4 * n + a * 2 + quarter
            px, py = chips[1 - quarter]
            return pltpu.make_async_remote_copy(
                src_ref=held(a, quarter), dst_ref=dsts[a].at[origin, rows(a, quarter), :], send_sem=send.at[k],
                recv_sem=recv.at[k], device_id=(px, py, c), device_id_type=MESH)

        own = [pltpu.make_async_copy(srcs[a].at[me], dsts[a].at[me], local.at[a]) for a in range(n)]
        sent = [first_hop(a, q) for a in range(n) for q in range(2)]
        sent += [direct(a, nb, ids[nb], me) for a in range(n) for nb in range(2)]
        for cp in sent + own:
            cp.start()
        for a in range(n):
            for q in range(2):
                first_hop(a, q).wait_recv()
                sent.append(second_hop(a, q, ids[q]))
                sent[-1].start()
        for a in range(n):
            for nb in range(2):
                direct(a, nb, me, ids[nb]).wait_recv()
            for q in range(2):
                second_hop(a, q, ids[2]).wait_recv()
        for cp in sent:
            cp.wait_send()
        for cp in own:
            cp.wait()

    anyspec = pl.BlockSpec(memory_space=pl.ANY)
    out = pl.pallas_call(
        body, name="chip_exchange", in_specs=[anyspec] * n, out_specs=[anyspec] * (2 * n),
        out_shape=[jax.ShapeDtypeStruct(a.shape, a.dtype) for a in arrs]
        + [jax.ShapeDtypeStruct((2, cut, a.shape[2]), a.dtype) for a, cut in zip(arrs, cuts)],
        scratch_shapes=[pltpu.SemaphoreType.DMA((6 * n,)), pltpu.SemaphoreType.DMA((6 * n,)),
                        pltpu.SemaphoreType.DMA((n,))],
    )(*arrs)
    return out[:n]


def _swap_halves(arrs):
    n = len(arrs)

    def body(*refs):
        srcs, dsts = refs[:n], refs[n:2 * n]
        send, recv, local = refs[2 * n:]
        x, y, c = _position()

        def remote(a, landing):
            return pltpu.make_async_remote_copy(
                src_ref=srcs[a], dst_ref=dsts[a].at[landing], send_sem=send.at[a], recv_sem=recv.at[a],
                device_id=(x, y, 1 - c), device_id_type=MESH)

        own = [pltpu.make_async_copy(srcs[a], dsts[a].at[c], local.at[a]) for a in range(n)]
        sends = [remote(a, c) for a in range(n)]
        for cp in sends + own:
            cp.start()
        for a in range(n):
            remote(a, 1 - c).wait_recv()
        for cp in sends:
            cp.wait_send()
        for cp in own:
            cp.wait()

    vm = pl.BlockSpec(memory_space=pltpu.VMEM)
    return pl.pallas_call(
        body, name="swap_halves", in_specs=[vm] * n, out_specs=[vm] * n,
        out_shape=[jax.ShapeDtypeStruct((2,) + a.shape, a.dtype) for a in arrs],
        scratch_shapes=[pltpu.SemaphoreType.DMA((n,)), pltpu.SemaphoreType.DMA((n,)), pltpu.SemaphoreType.DMA((n,))],
        compiler_params=pltpu.CompilerParams(vmem_limit_bytes=VMEM_LIMIT),
    )(*arrs)


def _row_block(r):
    return 128 if r % 128 == 0 else r


def _sum_slots(name, slots):
    s, r, n = slots.shape
    rb = _row_block(r)

    def body(s_ref, o_ref):
        total = s_ref[0].astype(f32)
        for d in range(1, s):
            total = total + s_ref[d].astype(f32)
        o_ref[...] = total

    return pl.pallas_call(
        body, name=name, grid=(r // rb,),
        in_specs=[pl.BlockSpec((s, rb, n), lambda i: (0, i, 0))],
        out_specs=pl.BlockSpec((rb, n), lambda i: (i, 0)),
        out_shape=jax.ShapeDtypeStruct((r, n), f32),
        compiler_params=_params(("parallel",), VMEM_LIMIT),
    )(slots)


def _adamw(name, w, g, m, v):
    r, n = w.shape
    if r % 128 == 0 or r * n <= 128 * 1024:
        rb, nb = _row_block(r), n
    else:
        rb, nb = r, LANES

    def body(w_ref, g_ref, m_ref, v_ref, d_ref, nm_ref, nv_ref):
        gv = g_ref[...]
        m2 = ADAM_B1 * m_ref[...] + (1.0 - ADAM_B1) * gv
        v2 = ADAM_B2 * v_ref[...] + (1.0 - ADAM_B2) * (gv * gv)
        m_hat = m2 / (1.0 - ADAM_B1 ** ADAM_STEP)
        v_hat = v2 / (1.0 - ADAM_B2 ** ADAM_STEP)
        d_ref[...] = (-ADAM_LR) * (m_hat / (jnp.sqrt(v_hat) + ADAM_EPS) + ADAM_WD * w_ref[...])
        nm_ref[...] = m2
        nv_ref[...] = v2

    spec = pl.BlockSpec((rb, nb), lambda i, j: (i, j))
    return pl.pallas_call(
        body, name=name, grid=(r // rb, n // nb), in_specs=[spec] * 4, out_specs=[spec] * 3,
        out_shape=[jax.ShapeDtypeStruct((r, n), f32)] * 3,
        compiler_params=_params(("parallel", "parallel"), VMEM_LIMIT),
    )(w, g, m, v)


def _local_step(x2, tgt2, seq, wt):
    nb = x2.shape[0] // seq
    h, qkv = _norm_qkv(x2, wt["pre_w"], wt["w_all"], wt["b_qkv"])
    rest = _mm("in_rest", h, wt["w_all"], wt["b_rest"], bf16, 1024, 1024, w_rows=(3 * D, 5 * D))
    f128 = _mm("in_f", h, wt["w_all"], wt["b_f"], f32, 1024, LANES, w_rows=(8 * D, LANES))
    c = _forget_prep(f128, seq)
    qa, ka = _attn_prep(qkv, c)
    o_att, pa, lse = _attn_fwd(qa, ka, qkv, rest, seq)
    ya = _mm("proj_a", pa, wt["w_a"], None, bf16, 1024, D)
    rnn_w = (wt["conv_w"], wt["conv_b"], wt["wa_d"], wt["wx_d"], wt["ba"], wt["bx"], wt["lam"])
    xc, a, hrec, pr = _rnn_fwd(rest, *rnn_w, seq)
    yr = _mm("proj_r", pr, wt["w_r"], None, bf16, 1024, D)
    do, dy, mrg, loss8, d_post = _out_proj_loss(rest, ya, yr, wt["w_o"], x2, tgt2, wt["post_w"])
    dya, dyr, dmga, dmgr = _out_bwd(do, rest, ya, yr, wt["w_o"])
    doa, dga, delta = _branch_bwd("branch_a_bwd", dya, rest, 0, o_att, wt["w_a"], bf16, head_sums=True)
    dhrec, dgr = _branch_bwd("branch_r_bwd", dyr, rest, 2, hrec, wt["w_r"], bf16)
    d_wo, _ = _tn_mm("dw_out", mrg, do, D)
    d_wa, _ = _tn_mm("dw_branch_a", pa, dya, D)
    d_wr, _ = _tn_mm("dw_branch_r", pr, dyr, D)
    dxr, d_wad, d_wxd, vec = _rnn_bwd(dhrec, a, hrec, xc, rest, *rnn_w, seq)
    dq, dk, dv, dc_pairs = _attn_bwd(qa, ka, qkv, doa, lse, delta, seq)
    dc = dc_pairs.reshape(-1, HEADS // ATT_GROUP, LANES)[:, :, :ATT_GROUP].reshape(-1, HEADS)
    df, db_f = _forget_bwd(_pad_cols(dc, LANES), f128, seq)
    pieces = [dq, dk, dv, dga, dxr, dgr, dmga, dmgr]
    gx, d_pre = _in_bwd(pieces, df, x2, dy, wt["w_all"], wt["pre_w"])
    names = ["q", "k", "v", "ga", "xr", "gr", "mga", "mgr"]
    dws, dbs = [], []
    for nm, piece in zip(names, pieces):
        dw_p, db_p = _tn_mm("dw_in_" + nm, piece, h, D, bf16)
        dws.append((dw_p, D))
        dbs.append(db_p)
    dw_f, _ = _tn_mm("dw_in_f", df, h, D, bf16)
    shard_rows = IN_TOTAL // N_CHIPS
    w_in_pieces = _pack_pieces(dws[:3] + [(dw_f, HEADS)] + dws[3:] + [(None, IN_TOTAL - IN_USED)], shard_rows,
                               _padded_rows(shard_rows))
    d_b_in = jnp.concatenate(dbs[:3] + [db_f[:, :HEADS]] + dbs[3:] + [jnp.zeros((1, IN_TOTAL - IN_USED), f32)], axis=1)
    return dict(loss=loss8[0, 0], grad_x=gx, pre_w=d_pre, w_in_pieces=w_in_pieces, b_in=d_b_in, conv_w=vec[4:8],
                conv_b=vec[3:4],
                wa_d=d_wad, ba=vec[0:1], wx_d=d_wxd, bx=vec[1:2], lam=vec[2:3], w_a=d_wa, w_r=d_wr, w_o=d_wo,
                post_w=d_post)


def _block_diag(w):
    g, bw, _ = w.shape
    eye = jnp.eye(g, dtype=w.dtype)
    return (w[:, :, None, :] * eye[:, None, :, None]).reshape(g * bw, g * bw)


def _gate_blocks(diag):
    half = diag.shape[1] // 2
    return jnp.stack([diag[:, :half, :half], diag[:, half:, half:]], axis=1).reshape(-1, half, half)


def _padded_rows(rows):
    return -(-rows // 32) * 32


def _pad_cols(a, n):
    return jnp.pad(a, ((0, 0), (0, n - a.shape[1])))


def _pad_rows(a, n):
    return jnp.pad(a, ((0, n - a.shape[0]), (0, 0)))


def kernel(x, pre_norm_w, w_in, b_in, conv_w, conv_b, rg_wa, rg_ba, rg_wx, rg_bx, rg_lambda, w_branch_a, w_branch_r, w_out, post_norm_w, loss_target, m_pre_norm_w, m_w_in, m_b_in, m_conv_w, m_conv_b, m_rg_wa, m_rg_ba, m_rg_wx, m_rg_bx, m_rg_lambda, m_w_branch_a, m_w_branch_r, m_w_out, m_post_norm_w, v_pre_norm_w, v_w_in, v_b_in, v_conv_w, v_conv_b, v_rg_wa, v_rg_ba, v_rg_wx, v_rg_bx, v_rg_lambda, v_w_branch_a, v_w_branch_r, v_w_out, v_post_norm_w):
    nb, seq, _ = x.shape
    chip = 2 * lax.axis_index("x") + lax.axis_index("y")
    n_groups = rg_wa.shape[1]

    w_in_t = jnp.transpose(w_in[0])
    shard_cols = w_in_t.shape[0]
    padded = _padded_rows(shard_cols)
    q_end, f_end = 3 * D, 3 * D + HEADS
    segments = [(0, 0, q_end), (q_end, f_end, IN_USED - f_end), (IN_USED - HEADS, q_end, HEADS)]
    g_a, g_r, g_o, g_cw, w_all = _gather_shards(
        [_pad_rows(w_in_t.astype(bf16), padded), w_branch_a[0].astype(bf16), w_branch_r[0].astype(bf16),
         w_out[0].astype(bf16)], conv_w[0], shard_cols, segments, IN_USED - HEADS + LANES)
    wt = dict(
        pre_w=pre_norm_w, post_w=post_norm_w,
        w_all=w_all, b_qkv=b_in[:, :q_end], b_f=_pad_cols(b_in[:, q_end:f_end], LANES), b_rest=b_in[:, f_end:IN_USED],
        w_a=g_a.reshape(D, D), w_r=g_r.reshape(D, D), w_o=g_o.reshape(D, D),
        conv_w=jnp.transpose(g_cw, (1, 0, 2)).reshape(4, D), conv_b=conv_b,
        wa_d=_block_diag(rg_wa[0]).astype(bf16), wx_d=_block_diag(rg_wx[0]).astype(bf16),
        ba=rg_ba, bx=rg_bx, lam=rg_lambda)

    part = _local_step(x.reshape(nb * seq, D), loss_target.reshape(nb * seq, D), seq, wt)
    loss = lax.psum(part["loss"], ("x", "y", "c"))
    grad_x = part["grad_x"].reshape(nb, seq, D)

    small = jnp.concatenate([
        part["pre_w"], _pad_cols(part["b_in"], 10 * D).reshape(10, D), part["conv_b"],
        _gate_blocks(part["wa_d"]).reshape(-1, D), part["ba"],
        _gate_blocks(part["wx_d"]).reshape(-1, D), part["bx"], part["lam"], part["post_w"],
        part["conv_w"]], axis=0)
    n_small = small.shape[0]
    n_rep = n_small - 4
    tot = _allsum_rows(_pad_rows(small, -(-n_small // 8) * 8))
    g_rep = tot[:n_rep]
    g_conv_w = lax.dynamic_slice_in_dim(tot[n_rep:n_small], chip * (D // N_CHIPS), D // N_CHIPS, axis=1)

    def unpack(p):
        o = [0]

        def take(k):
            o[0] += k
            return p[o[0] - k:o[0]]

        pre = take(1)
        b = take(10).reshape(1, 10 * D)[:, :IN_TOTAL]
        cb = take(1)
        wa = take(64).reshape(rg_wa.shape)
        ba = take(1)
        wx = take(64).reshape(rg_wx.shape)
        bx = take(1)
        lam = take(1)
        post = take(1)
        return dict(pre_norm_w=pre, b_in=b, conv_b=cb, rg_wa=wa, rg_ba=ba, rg_wx=wx, rg_bx=bx, rg_lambda=lam,
                    post_norm_w=post)

    grads = unpack(g_rep)
    replicated = dict(
        pre_norm_w=(pre_norm_w, m_pre_norm_w, v_pre_norm_w), b_in=(b_in, m_b_in, v_b_in),
        conv_b=(conv_b, m_conv_b, v_conv_b), rg_wa=(rg_wa, m_rg_wa, v_rg_wa), rg_ba=(rg_ba, m_rg_ba, v_rg_ba),
        rg_wx=(rg_wx, m_rg_wx, v_rg_wx), rg_bx=(rg_bx, m_rg_bx, v_rg_bx),
        rg_lambda=(rg_lambda, m_rg_lambda, v_rg_lambda), post_norm_w=(post_norm_w, m_post_norm_w, v_post_norm_w))
    deltas, new_m, new_v = {}, {}, {}
    for name, (w, m, v) in replicated.items():
        as2d = lambda a: a.reshape(-1, D) if a.ndim > 2 else a
        upd = _adamw("adamw_" + name, as2d(w), as2d(grads[name]), as2d(m), as2d(v))
        deltas[name], new_m[name], new_v[name] = [a.reshape(w.shape) for a in upd]

    p_aro = jnp.concatenate([part[k].reshape(N_DEV, D // N_DEV, D) for k in ("w_a", "w_r", "w_o")], axis=1)
    s_in, s_aro = _chip_exchange([_pair_reduce("pair_w_in", part["w_in_pieces"]),
                                  _pair_reduce("pair_w_aro", p_aro.astype(bf16))])
    f_in, f_aro = _swap_halves([_sum_slots("sum_w_in", s_in), _sum_slots("sum_w_aro", s_aro)])
    g_w_in_t = f_in.reshape(padded, D)[:shard_cols]
    rows = D // N_DEV
    g_aro = [f_aro[:, i * rows:(i + 1) * rows, :].reshape(2 * rows, D) for i in range(3)]

    w_in_upd = _adamw("adamw_w_in", w_in_t, g_w_in_t, jnp.transpose(m_w_in[0]), jnp.transpose(v_w_in[0]))
    g_w_in, d_w_in, nm_w_in, nv_w_in = [jnp.transpose(a) for a in (g_w_in_t, *w_in_upd)]
    upd_a = _adamw("adamw_w_branch_a", w_branch_a[0], g_aro[0], m_w_branch_a[0], v_w_branch_a[0])
    upd_r = _adamw("adamw_w_branch_r", w_branch_r[0], g_aro[1], m_w_branch_r[0], v_w_branch_r[0])
    upd_o = _adamw("adamw_w_out", w_out[0], g_aro[2], m_w_out[0], v_w_out[0])
    d_aro, nm_aro, nv_aro = zip(upd_a, upd_r, upd_o)
    d_cw, nm_cw, nv_cw = _adamw("adamw_conv_w", conv_w[0], g_conv_w, m_conv_w[0], v_conv_w[0])

    def sharded(t_in, t_aro, t_cw):
        return dict(w_in=t_in[None], conv_w=t_cw[None], w_branch_a=t_aro[0][None], w_branch_r=t_aro[1][None],
                    w_out=t_aro[2][None])

    order = ["pre_norm_w", "w_in", "b_in", "conv_w", "conv_b", "rg_wa", "rg_ba", "rg_wx", "rg_bx", "rg_lambda",
             "w_branch_a", "w_branch_r", "w_out", "post_norm_w"]
    outs = [loss, grad_x]
    for rep, shd in ((grads, sharded(g_w_in, g_aro, g_conv_w)), (deltas, sharded(d_w_in, d_aro, d_cw)),
                     (new_m, sharded(nm_w_in, nm_aro, nm_cw)), (new_v, sharded(nv_w_in, nv_aro, nv_cw))):
        both = {**rep, **shd}
        outs.extend(both[k] for k in order)
    return tuple(outs)
```

```python
import jax
import jax.numpy as jnp
from jax import lax
from jax.experimental import pallas as pl
from jax.experimental.pallas import tpu as pltpu

f32 = jnp.float32
bf16 = jnp.bfloat16

D = 1024
HEADS = 16
HEAD_PAIRS = 8
LANES = 128
NORM_EPS = 1e-6
MASK_VALUE = -1e30
RG_C = 8.0
QK_SCALE = 0.125
TQ = 256
ATT_GROUP = 8
ATT_GROUP_FWD = 16
TL = 256
TM = 512
PREV_ROWS = 16
IN_USED = 8 * D + HEADS
IN_TOTAL = 9 * D + HEADS
N_CHIPS = 4
N_DEV = 8
ADAM_LR, ADAM_B1, ADAM_B2, ADAM_EPS, ADAM_WD, ADAM_STEP = 0.001, 0.9, 0.999, 1e-08, 0.01, 10
VMEM_LIMIT = 56 * 1024 * 1024
MESH = pl.DeviceIdType.MESH


def _dot(a, b):
    return jnp.dot(a, b, preferred_element_type=f32)


def _dot_nt(a, b):
    return lax.dot_general(a, b, (((1,), (1,)), ((), ())), preferred_element_type=f32)


def _dot_tn(a, b):
    return lax.dot_general(a, b, (((0,), (0,)), ((), ())), preferred_element_type=f32)


def _sig(x):
    return 0.5 * jnp.tanh(0.5 * x) + 0.5


def _softplus(x):
    return jnp.maximum(x, 0.0) + jnp.log(1.0 + jnp.exp(-jnp.abs(x)))


def _params(sem, vmem=None):
    return pltpu.CompilerParams(dimension_semantics=sem, vmem_limit_bytes=vmem)


def _tile(tm, width, cb=0):
    return pl.BlockSpec((tm, width), lambda i, cb=cb: (i, cb))


def _whole(shape):
    nd = len(shape)
    return pl.BlockSpec(shape, lambda *_: (0,) * nd)


def _norm_qkv(x, w_pre, w_all, b_qkv, tm=1024):
    t = x.shape[0]
    tm = min(tm, t)
    n = b_qkv.shape[1]

    def body(x_ref, wp_ref, w_ref, b_ref, h_ref, o_ref):
        @pl.when(pl.program_id(1) == 0)
        def _():
            xv = x_ref[...]
            r = lax.rsqrt(jnp.mean(xv * xv, axis=-1, keepdims=True) + NORM_EPS)
            h_ref[...] = (xv * r * wp_ref[...]).astype(bf16)

        o_ref[...] = (_dot_nt(h_ref[...], w_ref[...]) + b_ref[...]).astype(bf16)

    return pl.pallas_call(
        body, name="norm_qkv", grid=(t // tm, n // D),
        in_specs=[pl.BlockSpec((tm, D), lambda i, j: (i, 0)), _whole((1, D)), pl.BlockSpec((D, D), lambda i, j: (j, 0)),
                  pl.BlockSpec((1, D), lambda i, j: (0, j))],
        out_specs=[pl.BlockSpec((tm, D), lambda i, j: (i, 0)), pl.BlockSpec((tm, D), lambda i, j: (i, j))],
        out_shape=[jax.ShapeDtypeStruct((t, D), bf16), jax.ShapeDtypeStruct((t, n), bf16)],
        compiler_params=_params(("parallel", "arbitrary"), VMEM_LIMIT),
    )(x, w_pre, w_all, b_qkv)


def _mm(name, a, w, bias, out_dtype, tm, tn, w_rows=None):
    t, k = a.shape
    tm = min(tm, t)
    w_is_nk = w_rows is not None
    row0, n = w_rows if w_is_nk else (0, w.shape[1])
    assert row0 % tn == 0

    def body(a_ref, w_ref, *refs):
        acc = _dot_nt(a_ref[...], w_ref[...]) if w_is_nk else _dot(a_ref[...], w_ref[...])
        if bias is not None:
            acc = acc + refs[0][...]
        refs[-1][...] = acc.astype(out_dtype)

    in_specs = [pl.BlockSpec((tm, k), lambda i, j: (i, 0)),
                pl.BlockSpec((tn, k), lambda i, j: (row0 // tn + j, 0)) if w_is_nk
                else pl.BlockSpec((k, tn), lambda i, j: (0, j))]
    args = [a, w]
    if bias is not None:
        in_specs.append(pl.BlockSpec((1, tn), lambda i, j: (0, j)))
        args.append(bias)
    return pl.pallas_call(
        body, name=name, grid=(t // tm, n // tn), in_specs=in_specs,
        out_specs=pl.BlockSpec((tm, tn), lambda i, j: (i, j)), out_shape=jax.ShapeDtypeStruct((t, n), out_dtype),
        compiler_params=_params(("parallel", "parallel"), VMEM_LIMIT),
    )(*args)


def _forget_prep(f128, seq):
    t = f128.shape[0]
    nb = seq // LANES

    def body(f_ref, c_ref):
        r = lax.broadcasted_iota(jnp.int32, (LANES, LANES), 0)
        cidx = lax.broadcasted_iota(jnp.int32, (LANES, LANES), 1)
        tri = (r >= cidx).astype(f32)
        carry = jnp.zeros((1, LANES), f32)
        for blk in range(nb):
            fv = f_ref[pl.ds(blk * LANES, LANES), :]
            lf = -_softplus(-fv)
            c_ref[pl.ds(blk * LANES, LANES), :] = (
                jnp.dot(tri, lf, preferred_element_type=f32, precision=lax.Precision.HIGHEST) + carry)
            carry = carry + jnp.sum(lf, axis=0, keepdims=True)

    return pl.pallas_call(
        body, name="forget_prep", grid=(t // seq,),
        in_specs=[pl.BlockSpec((seq, LANES), lambda b: (b, 0))],
        out_specs=pl.BlockSpec((seq, LANES), lambda b: (b, 0)),
        out_shape=jax.ShapeDtypeStruct((t, LANES), f32),
        compiler_params=_params(("parallel",)),
    )(f128)


def _split3(cv):
    hi = cv.astype(bf16)
    r1 = cv - hi.astype(f32)
    mid = r1.astype(bf16)
    lo = (r1 - mid.astype(f32)).astype(bf16)
    return hi, mid, lo


def _attn_prep(qkv, c):
    t = qkv.shape[0]

    def body(q_ref, k_ref, c_ref, qa_ref, ka_ref):
        lane = lax.broadcasted_iota(jnp.int32, (1, LANES), 1)
        cv = c_ref[...]
        one = jnp.ones((), bf16)
        zero = jnp.zeros((), bf16)
        q_ones = jnp.where((lane >= 67) & (lane < 70), one, zero)
        k_ones = jnp.where((lane >= 64) & (lane < 67), one, zero)
        for head in range(HEADS):
            pair = pl.ds((head // 2) * LANES, LANES)
            ch = jnp.sum(jnp.where(lane == head, cv, 0.0), axis=1, keepdims=True)
            hi, mid, lo = _split3(ch)
            q2, k2 = q_ref[:, pair], k_ref[:, pair]
            if head % 2 == 1:
                q2, k2 = pltpu.roll(q2, 64, 1), pltpu.roll(k2, 64, 1)
            qa = jnp.where(lane < 64, q2 * jnp.asarray(QK_SCALE, bf16),
                           jnp.where(lane == 64, hi, jnp.where(lane == 65, mid, jnp.where(lane == 66, lo, q_ones))))
            ka = jnp.where(lane < 64, k2,
                           jnp.where(lane == 67, -hi, jnp.where(lane == 68, -mid, jnp.where(lane == 69, -lo, k_ones))))
            qa_ref[:, pl.ds(head * LANES, LANES)] = qa
            ka_ref[:, pl.ds(head * LANES, LANES)] = ka

    tm = min(TM, t)
    out = pl.BlockSpec((tm, 2 * D), lambda i: (i, 0))
    return pl.pallas_call(
        body, name="attn_prep", grid=(t // tm,),
        in_specs=[_tile(tm, D, 0), _tile(tm, D, 1), _tile(tm, LANES)],
        out_specs=[out, out],
        out_shape=[jax.ShapeDtypeStruct((t, 2 * D), bf16)] * 2,
        compiler_params=_params(("parallel",)),
    )(qkv, qkv, c)


def _attn_fwd(qa, ka, qkv, rest, seq):
    t = qkv.shape[0]
    nb, nq = t // seq, seq // TQ

    hg = ATT_GROUP_FWD
    ng = HEADS // hg

    def body(q_ref, k_ref, v_ref, ga_ref, o_ref, pa_ref, lse_ref, acc_scr):
        qi, gi = pl.program_id(1), pl.program_id(2)
        krow = lax.broadcasted_iota(jnp.int32, (TQ, TQ), 0)
        qcol = lax.broadcasted_iota(jnp.int32, (TQ, TQ), 1)
        acc_scr[...] = jnp.zeros_like(acc_scr)

        def kv_step(kt, carry, masked):
            ks = pl.multiple_of(kt * TQ, TQ)
            sts = [_dot_nt(k_ref[pl.ds(ks, TQ), pl.ds(g * LANES, LANES)], q_ref[:, pl.ds(g * LANES, LANES)])
                   for g in range(hg)]
            if masked:
                sts = [jnp.where(krow <= qcol, st, MASK_VALUE) for st in sts]
            m_new = [jnp.maximum(carry[g][0], jnp.max(sts[g], axis=0, keepdims=True)) for g in range(hg)]
            ps = [jnp.exp(sts[g] - m_new[g]) for g in range(hg)]
            alphas = [jnp.exp(carry[g][0] - m_new[g]) for g in range(hg)]
            phi = [ps[g].astype(bf16) for g in range(hg)]
            plo = [(ps[g] - phi[g].astype(f32)).astype(bf16) for g in range(hg)]
            vs = [v_ref[pl.ds(ks, TQ), pl.ds(j * LANES, LANES)] for j in range(hg // 2)]
            pvs = [_dot_tn(vs[g // 2], phi[g]) + _dot_tn(vs[g // 2], plo[g]) for g in range(hg)]
            olds = [acc_scr[g] for g in range(hg)]
            for g in range(hg):
                acc_scr[g] = alphas[g] * olds[g] + pvs[g]
            return tuple((m_new[g], alphas[g] * carry[g][1] + jnp.sum(ps[g], axis=0, keepdims=True))
                         for g in range(hg))

        init = tuple((jnp.full((1, TQ), MASK_VALUE, f32), jnp.zeros((1, TQ), f32)) for _ in range(hg))
        carry = lax.fori_loop(0, qi, lambda kt, cr: kv_step(kt, cr, False), init)
        stats = kv_step(qi, carry, True)
        drow = lax.broadcasted_iota(jnp.int32, (LANES, TQ), 0)
        for g in range(hg):
            m, l = stats[g]
            lse_ref[0, pl.ds(hg * gi + g, 1), :] = m + jnp.log(l)
        for j in range(hg // 2):
            o2 = jnp.where(drow < 64, acc_scr[2 * j] / stats[2 * j][1], acc_scr[2 * j + 1] / stats[2 * j + 1][1]).T
            o_ref[:, pl.ds(j * LANES, LANES)] = o2
            ga = ga_ref[:, pl.ds(j * LANES, LANES)].astype(f32)
            pa_ref[:, pl.ds(j * LANES, LANES)] = (o2 * (ga * _sig(ga))).astype(bf16)

    vw = hg * 64
    tile = pl.BlockSpec((TQ, vw), lambda b, qi, gi: (b * nq + qi, gi))
    return pl.pallas_call(
        body, name="attn_fwd", grid=(nb, nq, ng),
        in_specs=[pl.BlockSpec((TQ, hg * LANES), lambda b, qi, gi: (b * nq + qi, gi)),
                  pl.BlockSpec((seq, hg * LANES), lambda b, qi, gi: (b, gi)),
                  pl.BlockSpec((seq, vw), lambda b, qi, gi: (b, 2 * ng + gi)), tile],
        out_specs=[tile, tile, pl.BlockSpec((1, HEADS, TQ), lambda b, qi, gi: (b * nq + qi, 0, 0))],
        out_shape=[jax.ShapeDtypeStruct((t, D), f32), jax.ShapeDtypeStruct((t, D), bf16),
                   jax.ShapeDtypeStruct((t // TQ, HEADS, TQ), f32)],
        scratch_shapes=[pltpu.VMEM((hg, LANES, TQ), f32)],
        compiler_params=_params(("parallel", "parallel", "arbitrary"), VMEM_LIMIT),
    )(qa, ka, qkv, rest)


def _shifted_rows(x, top8, prev8, shift, row, row8):
    body = pltpu.roll(x, shift, 0)
    head = jnp.where(row8 < shift, pltpu.roll(prev8, shift, 0), pltpu.roll(top8, shift, 0))
    return body, head


def _rnn_gates(xc, wa_ref, wx_ref, ba_ref, bx_ref, lam_ref):
    xcb = xc.astype(bf16)
    r = _sig(_dot(xcb, wa_ref[...]) + ba_ref[...])
    i = _sig(_dot(xcb, wx_ref[...]) + bx_ref[...])
    sp = _softplus(-lam_ref[...])
    log_a = (-RG_C) * r * sp
    th = jnp.tanh(log_a)
    w1 = (-2.0) * th / (1.0 - th)
    sq = jnp.sqrt(jnp.maximum(w1, 0.0))
    return r, i, sp, log_a, w1, sq


def _conv_tile(x_ref, xprev_ref, has_prev, cw_ref, cb_ref, xc_ref):
    row = lax.broadcasted_iota(jnp.int32, (TL, D), 0)
    row8 = lax.broadcasted_iota(jnp.int32, (8, D), 0)
    x = x_ref[...].astype(f32)
    top8 = x[:8]
    prev8 = jnp.where(has_prev, xprev_ref[...].astype(f32)[PREV_ROWS - 8:], 0.0)
    xc = cb_ref[...] + cw_ref[pl.ds(3, 1), :] * x
    xc8 = cb_ref[...] + cw_ref[pl.ds(3, 1), :] * top8
    for sh in range(1, 4):
        w = cw_ref[pl.ds(3 - sh, 1), :]
        xs, xs8 = _shifted_rows(x, top8, prev8, sh, row, row8)
        xc = xc + w * xs
        xc8 = xc8 + w * xs8
    xc_ref[...] = xc
    xc_ref[pl.ds(0, 8), :] = xc8


def _rnn_fwd(rest, conv_w, conv_b, wa_d, wx_d, ba, bx, lam, seq):
    t = rest.shape[0]
    nb, nt = t // seq, seq // TL

    def body(x_ref, xprev_ref, gr_ref, cw_ref, cb_ref, wa_ref, wx_ref, ba_ref, bx_ref, lam_ref,
             xc_ref, a_ref, h_ref, pr_ref, xc_scr, u_scr, h_scr, carry):
        tt = pl.program_id(1)
        _conv_tile(x_ref, xprev_ref, tt > 0, cw_ref, cb_ref, xc_scr)
        xc = xc_scr[...]
        xc_ref[...] = xc.astype(bf16)
        r, i, sp, log_a, w1, sq = _rnn_gates(xc, wa_ref, wx_ref, ba_ref, bx_ref, lam_ref)
        a_ref[...] = jnp.exp(log_a)
        u_scr[...] = sq * (i * xc)

        @pl.when(tt == 0)
        def _():
            carry[...] = jnp.zeros_like(carry)

        def step(s, h):
            h = a_ref[pl.ds(s, 1), :] * h + u_scr[pl.ds(s, 1), :]
            h_scr[pl.ds(s, 1), :] = h
            return h

        carry[...] = lax.fori_loop(0, TL, step, carry[...], unroll=8)
        gr = gr_ref[...].astype(f32)
        h = h_scr[...]
        h_ref[...] = h.astype(bf16)
        pr_ref[...] = (h * (gr * _sig(gr))).astype(bf16)

    tile = lambda cb: pl.BlockSpec((TL, D), lambda b, tt, cb=cb: (b * nt + tt, cb))
    prev = lambda cb: pl.BlockSpec(
        (PREV_ROWS, D), lambda b, tt, cb=cb: (jnp.maximum((b * nt + tt) * (TL // PREV_ROWS) - 1, 0), cb))
    vec = _whole((1, D))
    return pl.pallas_call(
        body, name="rnn_fwd", grid=(nb, nt),
        in_specs=[tile(1), prev(1), tile(2), _whole((4, D)), vec, _whole((D, D)), _whole((D, D)), vec, vec, vec],
        out_specs=[tile(0)] * 4,
        out_shape=[jax.ShapeDtypeStruct((t, D), dt) for dt in (bf16, f32, bf16, bf16)],
        scratch_shapes=[pltpu.VMEM((TL, D), f32)] * 3 + [pltpu.VMEM((1, D), f32)],
        compiler_params=_params(("parallel", "arbitrary"), VMEM_LIMIT),
    )(rest, rest, rest, conv_w, conv_b, wa_d, wx_d, ba, bx, lam)


def _merge(mga, mgr, ya, yr):
    return (_sig(mga.astype(f32)) * ya.astype(f32) + _sig(mgr.astype(f32)) * yr.astype(f32)).astype(bf16)


def _out_proj_loss(rest, ya, yr, w_out, x, tgt, w_post):
    t = x.shape[0]

    def body(mga_ref, mgr_ref, ya_ref, yr_ref, wo_ref, x_ref, t_ref, w_ref, do_ref, dy_ref, mrg_ref, loss_ref, dwp_ref):
        @pl.when(pl.program_id(0) == 0)
        def _():
            loss_ref[...] = jnp.zeros_like(loss_ref)
            dwp_ref[...] = jnp.zeros_like(dwp_ref)

        mrg = _merge(mga_ref[...], mgr_ref[...], ya_ref[...], yr_ref[...])
        mrg_ref[...] = mrg
        ov = _dot(mrg, wo_ref[...])
        w = w_ref[...]
        r2 = lax.rsqrt(jnp.mean(ov * ov, axis=-1, keepdims=True) + NORM_EPS)
        oh = ov * r2
        e = x_ref[...] + oh * w - t_ref[...]
        loss_ref[...] += 0.5 * jnp.sum(jnp.mean(e * e, axis=-1, keepdims=True))
        dy = e * (1.0 / D)
        dy_ref[...] = dy
        dwp_ref[...] += jnp.sum(dy * oh, axis=0, keepdims=True)
        doh = dy * w
        do_ref[...] = (r2 * (doh - oh * jnp.mean(doh * oh, axis=-1, keepdims=True))).astype(bf16)

    return pl.pallas_call(
        body, name="out_proj_loss", grid=(t // TM,),
        in_specs=[_tile(TM, D, 3), _tile(TM, D, 4), _tile(TM, D), _tile(TM, D), _whole((D, D)), _tile(TM, D),
                  _tile(TM, D), _whole((1, D))],
        out_specs=[_tile(TM, D), _tile(TM, D), _tile(TM, D), _whole((8, LANES)), _whole((1, D))],
        out_shape=[jax.ShapeDtypeStruct((t, D), bf16), jax.ShapeDtypeStruct((t, D), f32),
                   jax.ShapeDtypeStruct((t, D), bf16), jax.ShapeDtypeStruct((8, LANES), f32),
                   jax.ShapeDtypeStruct((1, D), f32)],
        compiler_params=_params(("arbitrary",), VMEM_LIMIT),
    )(rest, rest, ya, yr, w_out, x, tgt, w_post)


def _out_bwd(do, rest, ya, yr, w_out):
    t = do.shape[0]

    def body(do_ref, mga_ref, mgr_ref, ya_ref, yr_ref, w_ref, dya_ref, dyr_ref, dmga_ref, dmgr_ref):
        sa, sr = _sig(mga_ref[...].astype(f32)), _sig(mgr_ref[...].astype(f32))
        ya, yr = ya_ref[...].astype(f32), yr_ref[...].astype(f32)
        dm = _dot_nt(do_ref[...], w_ref[...])
        dya_ref[...] = (dm * sa).astype(bf16)
        dyr_ref[...] = (dm * sr).astype(bf16)
        dmga_ref[...] = (dm * ya * sa * (1.0 - sa)).astype(bf16)
        dmgr_ref[...] = (dm * yr * sr * (1.0 - sr)).astype(bf16)

    return pl.pallas_call(
        body, name="out_bwd", grid=(t // TM,),
        in_specs=[_tile(TM, D), _tile(TM, D, 3), _tile(TM, D, 4), _tile(TM, D), _tile(TM, D), _whole((D, D))],
        out_specs=[_tile(TM, D)] * 4,
        out_shape=[jax.ShapeDtypeStruct((t, D), bf16)] * 4,
        compiler_params=_params(("parallel",), VMEM_LIMIT),
    )(do, rest, rest, ya, yr, w_out)


def _branch_bwd(name, dyb, rest, gate_cb, act, w, act_grad_dtype, head_sums=False):
    t = dyb.shape[0]

    def body(dy_ref, g_ref, act_ref, w_ref, dact_ref, dg_ref, *delta_ref):
        dp = _dot_nt(dy_ref[...], w_ref[...])
        g = g_ref[...].astype(f32)
        sg = _sig(g)
        act = act_ref[...].astype(f32)
        dact = (dp * (g * sg)).astype(act_grad_dtype)
        dact_ref[...] = dact
        dg_ref[...] = (dp * act * (sg * (1.0 + g * (1.0 - sg)))).astype(bf16)
        if head_sums:
            ch = lax.broadcasted_iota(jnp.int32, (D, LANES), 0)
            hd = lax.broadcasted_iota(jnp.int32, (D, LANES), 1)
            pick = (ch // 64 == hd).astype(bf16)
            per_head = sum(_dot(piece, pick) for piece in _split3(dact.astype(f32) * act))
            for s in range(TM // TQ):
                delta_ref[0][s] = per_head[s * TQ:(s + 1) * TQ].T[:HEADS, :]

    out_specs = [_tile(TM, D), _tile(TM, D)]
    out_shape = [jax.ShapeDtypeStruct((t, D), act_grad_dtype), jax.ShapeDtypeStruct((t, D), bf16)]
    if head_sums:
        out_specs.append(pl.BlockSpec((TM // TQ, HEADS, TQ), lambda i: (i, 0, 0)))
        out_shape.append(jax.ShapeDtypeStruct((t // TQ, HEADS, TQ), f32))
    return pl.pallas_call(
        body, name=name, grid=(t // TM,),
        in_specs=[_tile(TM, D), _tile(TM, D, gate_cb), _tile(TM, D), _whole((D, D))],
        out_specs=out_specs, out_shape=out_shape,
        compiler_params=_params(("parallel",), VMEM_LIMIT),
    )(dyb, rest, act, w)


def _rnn_bwd(dh, a, h, xc, rest, conv_w, conv_b, wa_d, wx_d, ba, bx, lam, seq):
    t = dh.shape[0]
    nb, nt = t // seq, seq // TL
    diag = (D // LANES, LANES, LANES)

    def body(dh_ref, a_ref, h_ref, hprev_ref, xc_ref, x_ref, xprev_ref, cw_ref, cb_ref, wa_ref, wx_ref,
             ba_ref, bx_ref, lam_ref, dxr_ref, dwa_ref, dwx_ref, vec_ref, g_scr, dxc_scr, dxr_scr, qcarry, dxc_next):
        b, tt = pl.program_id(0), pl.program_id(1)
        rt = nt - 1 - tt

        @pl.when((b == 0) & (tt == 0))
        def _():
            dwa_ref[...] = jnp.zeros_like(dwa_ref)
            dwx_ref[...] = jnp.zeros_like(dwx_ref)
            vec_ref[...] = jnp.zeros_like(vec_ref)

        @pl.when(tt == 0)
        def _():
            qcarry[...] = jnp.zeros_like(qcarry)
            dxc_next[...] = jnp.zeros_like(dxc_next)

        g_scr[...] = dh_ref[...].astype(f32)

        def step(k, q):
            s = TL - 1 - k
            g = g_scr[pl.ds(s, 1), :] + q
            g_scr[pl.ds(s, 1), :] = g
            return a_ref[pl.ds(s, 1), :] * g

        qcarry[...] = lax.fori_loop(0, TL, step, qcarry[...], unroll=8)

        row = lax.broadcasted_iota(jnp.int32, (TL, D), 0)
        row8 = lax.broadcasted_iota(jnp.int32, (8, D), 0)
        g = g_scr[...]
        av = a_ref[...]
        xc = xc_ref[...].astype(f32)
        hlast = jnp.where(rt > 0, hprev_ref[...].astype(f32)[PREV_ROWS - 1:], 0.0)
        hp = jnp.where(row == 0, hlast, pltpu.roll(h_ref[...].astype(f32), 1, 0))
        r, i, sp, log_a, w1, sq = _rnn_gates(xc, wa_ref, wx_ref, ba_ref, bx_ref, lam_ref)
        dix = g * sq
        di = dix * xc
        dxc = dix * i
        dsq = g * (i * xc)
        dlog_a = g * hp * av - dsq * jnp.where(sq > 0.0, (1.0 - w1) / sq, 0.0)
        dpr = (dlog_a * ((-RG_C) * sp)) * r * (1.0 - r)
        dpi = di * i * (1.0 - i)
        dprb, dpib, xcb = dpr.astype(bf16), dpi.astype(bf16), xc.astype(bf16)
        dxc = dxc + _dot_nt(dprb, wa_ref[...]) + _dot_nt(dpib, wx_ref[...])
        for j in range(D // LANES):
            cols = slice(j * LANES, (j + 1) * LANES)
            dwa_ref[j] += _dot_tn(xcb[:, cols], dprb[:, cols])
            dwx_ref[j] += _dot_tn(xcb[:, cols], dpib[:, cols])
        vec_ref[pl.ds(0, 1), :] += jnp.sum(dpr, axis=0, keepdims=True)
        vec_ref[pl.ds(1, 1), :] += jnp.sum(dpi, axis=0, keepdims=True)
        dsp = jnp.sum(dlog_a * ((-RG_C) * r), axis=0, keepdims=True)
        vec_ref[pl.ds(2, 1), :] += dsp * (-_sig(-lam_ref[...]))
        vec_ref[pl.ds(3, 1), :] += jnp.sum(dxc, axis=0, keepdims=True)

        dxc_scr[...] = dxc
        bot8 = dxc_scr[pl.ds(TL - 8, 8), :]
        nxt8 = dxc_next[...]
        dxr = cw_ref[pl.ds(3, 1), :] * dxc
        dxr8 = cw_ref[pl.ds(3, 1), :] * bot8
        for sh in range(1, 4):
            w = cw_ref[pl.ds(3 - sh, 1), :]
            dxr = dxr + w * pltpu.roll(dxc, TL - sh, 0)
            dxr8 = dxr8 + w * jnp.where(row8 < 8 - sh, pltpu.roll(bot8, 8 - sh, 0), pltpu.roll(nxt8, 8 - sh, 0))
        dxr_scr[...] = dxr
        dxr_scr[pl.ds(TL - 8, 8), :] = dxr8
        dxr_ref[...] = dxr_scr[...].astype(bf16)
        dxc_next[...] = dxc_scr[pl.ds(0, 8), :]

        x = x_ref[...].astype(f32)
        prev8 = jnp.where(rt > 0, xprev_ref[...].astype(f32)[PREV_ROWS - 8:], 0.0)
        dxc_top8 = dxc_scr[pl.ds(0, 8), :]
        vec_ref[pl.ds(7, 1), :] += jnp.sum(dxc * x, axis=0, keepdims=True)
        for sh in range(1, 4):
            inside = jnp.sum(dxc * jnp.where(row >= sh, pltpu.roll(x, sh, 0), 0.0), axis=0, keepdims=True)
            above = jnp.sum(dxc_top8 * jnp.where(row8 < sh, pltpu.roll(prev8, sh, 0), 0.0), axis=0, keepdims=True)
            vec_ref[pl.ds(7 - sh, 1), :] += inside + above

    tile = lambda cb: pl.BlockSpec((TL, D), lambda b, tt, cb=cb: (b * nt + nt - 1 - tt, cb))
    prev = lambda cb: pl.BlockSpec(
        (PREV_ROWS, D), lambda b, tt, cb=cb: (jnp.maximum((b * nt + nt - 1 - tt) * (TL // PREV_ROWS) - 1, 0), cb))
    vec = _whole((1, D))
    return pl.pallas_call(
        body, name="rnn_bwd", grid=(nb, nt),
        in_specs=[tile(0), tile(0), tile(0), prev(0), tile(0), tile(1), prev(1),
                  _whole((4, D)), vec, _whole((D, D)), _whole((D, D)), vec, vec, vec],
        out_specs=[tile(0), _whole(diag), _whole(diag), _whole((8, D))],
        out_shape=[jax.ShapeDtypeStruct((t, D), bf16), jax.ShapeDtypeStruct(diag, f32),
                   jax.ShapeDtypeStruct(diag, f32), jax.ShapeDtypeStruct((8, D), f32)],
        scratch_shapes=[pltpu.VMEM((TL, D), f32), pltpu.VMEM((TL, D), f32), pltpu.VMEM((TL, D), f32),
                        pltpu.VMEM((1, D), f32), pltpu.VMEM((8, D), f32)],
        compiler_params=_params(("arbitrary", "arbitrary"), VMEM_LIMIT),
    )(dh, a, h, h, xc, rest, rest, conv_w, conv_b, wa_d, wx_d, ba, bx, lam)


def _attn_bwd(qa, ka, qkv, doa, lse, delta, seq):
    t = qkv.shape[0]
    nb, nq = t // seq, seq // TQ
    hg = ATT_GROUP
    ng, npair = HEADS // hg, hg // 2

    def body(qa_ref, ka_ref, q_ref, k_ref, v_ref, do_ref, lse_ref, dl_ref, dq_ref, dk_ref, dv_ref, dc_ref,
             dqt_scr, dk_scr, dv_scr, ds_scr, kht_scr):
        gi, kt = pl.program_id(1), pl.program_id(2)
        lane = lax.broadcasted_iota(jnp.int32, (1, LANES), 1)
        krow = lax.broadcasted_iota(jnp.int32, (TQ, TQ), 0)
        qcol = lax.broadcasted_iota(jnp.int32, (TQ, TQ), 1)
        lmask = [(lane // 64) == hh for hh in range(2)]
        scale = jnp.asarray(QK_SCALE, bf16)

        @pl.when(kt == 0)
        def _():
            dqt_scr[...] = jnp.zeros_like(dqt_scr)

        dk_scr[...] = jnp.zeros_like(dk_scr)
        dv_scr[...] = jnp.zeros_like(dv_scr)
        ds_scr[...] = jnp.zeros_like(ds_scr)
        for g in range(hg):
            k2 = k_ref[:, pl.ds((g // 2) * LANES, LANES)]
            kht_scr[g] = jnp.where(lmask[g % 2], k2, jnp.zeros_like(k2)).T

        def q_step(qt, masked):
            qs = pl.multiple_of(qt * TQ, TQ)
            heads = range(hg)
            do2 = [do_ref[pl.ds(qs, TQ), pl.ds(j * LANES, LANES)] for j in range(npair)]
            q2 = [q_ref[pl.ds(qs, TQ), pl.ds(j * LANES, LANES)] for j in range(npair)]
            doh = [jnp.where(lmask[g % 2], do2[g // 2], jnp.zeros_like(do2[0])) for g in heads]
            qh = [jnp.where(lmask[g % 2], q2[g // 2], jnp.zeros_like(q2[0])) * scale for g in heads]
            st = [_dot_nt(ka_ref[:, pl.ds(g * LANES, LANES)], qa_ref[pl.ds(qs, TQ), pl.ds(g * LANES, LANES)])
                  for g in heads]
            if masked:
                st = [jnp.where(krow <= qcol, s, MASK_VALUE) for s in st]
            dp = [_dot_nt(v_ref[:, pl.ds((g // 2) * LANES, LANES)], doh[g]) for g in heads]
            p = [jnp.exp(st[g] - lse_ref[qt, pl.ds(hg * gi + g, 1), :]) for g in heads]
            ds = [p[g] * (dp[g] - dl_ref[qt, pl.ds(hg * gi + g, 1), :]) for g in heads]
            pb = [x.astype(bf16) for x in p]
            dsb = [x.astype(bf16) for x in ds]
            for j in range(npair):
                a, b = 2 * j, 2 * j + 1
                dv_scr[j] += _dot(pb[a], doh[a]) + _dot(pb[b], doh[b])
                dk_scr[j] += _dot(dsb[a], qh[a]) + _dot(dsb[b], qh[b])
                dqt_scr[qt, j] += (_dot(kht_scr[a], dsb[a]) + _dot(kht_scr[b], dsb[b])) * QK_SCALE
            for g in heads:
                ds_scr[g] += ds[g][:, :LANES] + ds[g][:, LANES:]

        q_step(kt, True)

        def loop_body(qt, carry):
            q_step(qt, False)
            return carry

        lax.fori_loop(kt + 1, nq, loop_body, 0)

        dc = jnp.zeros((TQ, LANES), f32)
        for g in range(hg):
            dc = jnp.where(lane == g, -jnp.sum(ds_scr[g], axis=1, keepdims=True), dc)
        dc_ref[...] = dc
        for j in range(npair):
            dk_ref[:, pl.ds(j * LANES, LANES)] = dk_scr[j].astype(bf16)
            dv_ref[:, pl.ds(j * LANES, LANES)] = dv_scr[j].astype(bf16)

        @pl.when(kt == nq - 1)
        def _():
            for qt in range(nq):
                for j in range(npair):
                    dq_ref[pl.ds(qt * TQ, TQ), pl.ds(j * LANES, LANES)] = dqt_scr[qt, j].T.astype(bf16)

    vw = hg * 64
    seqspec = pl.BlockSpec((seq, vw), lambda b, gi, kt: (b, gi))
    kspec = lambda off: pl.BlockSpec((TQ, vw), lambda b, gi, kt: (b * nq + kt, off + gi))
    rowspec = pl.BlockSpec((nq, HEADS, TQ), lambda b, gi, kt: (b, 0, 0))
    return pl.pallas_call(
        body, name="attn_bwd", grid=(nb, ng, nq),
        in_specs=[pl.BlockSpec((seq, hg * LANES), lambda b, gi, kt: (b, gi)),
                  pl.BlockSpec((TQ, hg * LANES), lambda b, gi, kt: (b * nq + kt, gi)),
                  seqspec, kspec(ng), kspec(2 * ng), seqspec, rowspec, rowspec],
        out_specs=[seqspec, kspec(0), kspec(0), pl.BlockSpec((TQ, LANES), lambda b, gi, kt: (b * nq + kt, gi))],
        out_shape=[jax.ShapeDtypeStruct((t, D), bf16)] * 3 + [jax.ShapeDtypeStruct((t, ng * LANES), f32)],
        scratch_shapes=[pltpu.VMEM((nq, npair, LANES, TQ), f32), pltpu.VMEM((npair, TQ, LANES), f32),
                        pltpu.VMEM((npair, TQ, LANES), f32), pltpu.VMEM((hg, TQ, LANES), f32),
                        pltpu.VMEM((hg, LANES, TQ), bf16)],
        compiler_params=_params(("parallel", "parallel", "arbitrary"), VMEM_LIMIT),
    )(qa, ka, qkv, qkv, qkv, doa, lse, delta)


def _forget_bwd(dc, f128, seq):
    t = f128.shape[0]
    nb = seq // LANES
    groups = dc.shape[1] // LANES

    def body(dc_ref, f_ref, df_ref, dbf_ref):
        @pl.when(pl.program_id(0) == 0)
        def _():
            dbf_ref[...] = jnp.zeros_like(dbf_ref)

        r = lax.broadcasted_iota(jnp.int32, (LANES, LANES), 0)
        cidx = lax.broadcasted_iota(jnp.int32, (LANES, LANES), 1)
        tri = (r <= cidx).astype(f32)
        carry = jnp.zeros((1, LANES), f32)
        total = jnp.zeros((1, LANES), f32)
        for blk in reversed(range(nb)):
            dcb = dc_ref[pl.ds(blk * LANES, LANES), pl.ds(0, LANES)]
            for gi in range(1, groups):
                dcb = dcb + pltpu.roll(dc_ref[pl.ds(blk * LANES, LANES), pl.ds(gi * LANES, LANES)], gi * ATT_GROUP, 1)
            dlf = jnp.dot(tri, dcb, preferred_element_type=f32, precision=lax.Precision.HIGHEST) + carry
            df = dlf * _sig(-f_ref[pl.ds(blk * LANES, LANES), :])
            df_ref[pl.ds(blk * LANES, LANES), :] = df.astype(bf16)
            total = total + jnp.sum(df, axis=0, keepdims=True)
            carry = carry + jnp.sum(dcb, axis=0, keepdims=True)
        dbf_ref[...] += total

    return pl.pallas_call(
        body, name="forget_bwd", grid=(t // seq,),
        in_specs=[pl.BlockSpec((seq, groups * LANES), lambda b: (b, 0)), pl.BlockSpec((seq, LANES), lambda b: (b, 0))],
        out_specs=[pl.BlockSpec((seq, LANES), lambda b: (b, 0)), _whole((1, LANES))],
        out_shape=[jax.ShapeDtypeStruct((t, LANES), bf16), jax.ShapeDtypeStruct((1, LANES), f32)],
        compiler_params=_params(("arbitrary",)),
    )(dc, f128)


def _in_bwd(dz, df, x, dy, w_all, w_pre):
    t = x.shape[0]
    n_dz = len(dz)

    def body(*refs):
        dz_refs = refs[:n_dz]
        df_ref, x_ref, dy_ref, w_ref, wp_ref, gx_ref, dwp_ref = refs[n_dz:]

        @pl.when(pl.program_id(0) == 0)
        def _():
            dwp_ref[...] = jnp.zeros_like(dwp_ref)

        dh = _dot(df_ref[...], w_ref[pl.ds(n_dz * D, LANES), :])
        for p in range(n_dz):
            dh = dh + _dot(dz_refs[p][...], w_ref[pl.ds(p * D, D), :])
        xv = x_ref[...]
        r1 = lax.rsqrt(jnp.mean(xv * xv, axis=-1, keepdims=True) + NORM_EPS)
        xh = xv * r1
        dwp_ref[...] += jnp.sum(dh * xh, axis=0, keepdims=True)
        dxh = dh * wp_ref[...]
        gx_ref[...] = dy_ref[...] + r1 * (dxh - xh * jnp.mean(dxh * xh, axis=-1, keepdims=True))

    once = lambda shape: pl.BlockSpec(shape, lambda i: (0, 0), pipeline_mode=pl.Buffered(1))
    return pl.pallas_call(
        body, name="in_bwd", grid=(t // TM,),
        in_specs=[_tile(TM, D)] * n_dz + [_tile(TM, LANES), _tile(TM, D), _tile(TM, D), once(w_all.shape),
                  _whole((1, D))],
        out_specs=[_tile(TM, D), _whole((1, D))],
        out_shape=[jax.ShapeDtypeStruct((t, D), f32), jax.ShapeDtypeStruct((1, D), f32)],
        compiler_params=_params(("arbitrary",), VMEM_LIMIT),
    )(*dz, df, x, dy, w_all, w_pre)


def _tn_mm(name, a, b, tn, out_dtype=f32, tk=2048):
    t, k = a.shape
    tk = min(tk, t)
    n = b.shape[1]
    nk = t // tk

    def body(a_ref, b_ref, o_ref, s_ref, acc_ref):
        j, kk = pl.program_id(0), pl.program_id(1)

        @pl.when(kk == 0)
        def _():
            acc_ref[...] = jnp.zeros_like(acc_ref)

        @pl.when((j == 0) & (kk == 0))
        def _():
            s_ref[...] = jnp.zeros_like(s_ref)

        av = a_ref[...]
        acc_ref[...] += _dot_tn(av, b_ref[...])

        @pl.when(j == 0)
        def _():
            s_ref[...] += jnp.sum(av.astype(f32), axis=0, keepdims=True)

        @pl.when(kk == nk - 1)
        def _():
            o_ref[...] = acc_ref[...].astype(out_dtype)

    return pl.pallas_call(
        body, name=name, grid=(n // tn, nk),
        in_specs=[pl.BlockSpec((tk, k), lambda j, kk: (kk, 0)), pl.BlockSpec((tk, tn), lambda j, kk: (kk, j))],
        out_specs=[pl.BlockSpec((k, tn), lambda j, kk: (0, j)), _whole((1, k))],
        out_shape=[jax.ShapeDtypeStruct((k, n), out_dtype), jax.ShapeDtypeStruct((1, k), f32)],
        scratch_shapes=[pltpu.VMEM((k, tn), f32)],
        compiler_params=_params(("arbitrary", "arbitrary"), VMEM_LIMIT),
    )(a, b)


def _position():
    return lax.axis_index("x"), lax.axis_index("y"), lax.axis_index("c")


ROW_BLOCK = 128


def _pick_rows(layout, first, count):
    acc = jnp.zeros((ROW_BLOCK, D), f32)
    seg_start = 0
    for ref, ref_row, rows in layout:
        lo, hi = max(first, seg_start), min(first + count, seg_start + rows)
        if lo < hi and ref is not None:
            off, take, done = ref_row + lo - seg_start, hi - lo, lo - first
            start = off // 16 * 16
            win = -(-(off - start + take) // 16) * 16
            r = lax.broadcasted_iota(jnp.int32, (ROW_BLOCK, win), 0)
            col = lax.broadcasted_iota(jnp.int32, (ROW_BLOCK, win), 1)
            pick = ((col - r == off - start - done) & (r >= done) & (r < done + take)).astype(bf16)
            acc = acc + _dot(pick, ref[pl.ds(start, win), :])
        seg_start += rows
    return acc


def _assemble_rows(shards_ref, shard_rows, segments, out_ref):
    layout = [(shards_ref.at[j], 0, shard_rows) for j in range(shards_ref.shape[0])]
    for out0, log0, count in segments:
        for b0 in range(0, count, ROW_BLOCK):
            block = _pick_rows(layout, log0 + b0, min(ROW_BLOCK, count - b0))
            out_ref[pl.ds(out0 + b0, ROW_BLOCK), :] = block.astype(bf16)


def _pack_pieces(blocks, shard_rows, padded):
    arrays = [a for a, _ in blocks if a is not None]
    piece_rows = padded // 2

    def body(*refs):
        out_ref = refs[-1]
        it = iter(refs[:-1])
        layout = [(None if a is None else next(it), 0, rows) for a, rows in blocks]
        for k in range(N_DEV):
            chip, half = divmod(k, 2)
            for b0 in range(0, piece_rows, ROW_BLOCK):
                n = min(ROW_BLOCK, piece_rows - b0)
                in_shard = half * piece_rows + b0
                count = max(0, min(n, shard_rows - in_shard))
                block = _pick_rows(layout, chip * shard_rows + in_shard, count)
                out_ref[k, pl.ds(b0, n), :] = block[:n].astype(bf16)

    vm = pl.BlockSpec(memory_space=pltpu.VMEM)
    return pl.pallas_call(
        body, name="pack_pieces", in_specs=[vm] * len(arrays), out_specs=vm,
        out_shape=jax.ShapeDtypeStruct((N_DEV, piece_rows, D), bf16),
        compiler_params=pltpu.CompilerParams(vmem_limit_bytes=VMEM_LIMIT),
    )(*arrays)


def _gather_shards(parts, small, shard_rows, segments, out_rows):
    n = len(parts)
    halves = [p.shape[0] // 2 for p in parts]
    cuts = [-(-h // 32) * 16 for h in halves]
    n_direct, n_relay, n_sib = 4 * n, 2 * n, 6 * n

    def body(*refs):
        srcs, small_src = refs[:n], refs[n]
        dsts, small_dst, whole_ref = refs[n + 1:2 * n + 1], refs[2 * n + 1], refs[2 * n + 2]
        send, recv, local = refs[2 * n + 3:]
        x, y, c = _position()
        me = 2 * x + y
        chips = [(1 - x, y), (x, 1 - y), (1 - x, 1 - y)]
        ids = [2 * px + py for px, py in chips]

        def rows(a, half, quarter):
            start = half * halves[a] + (cuts[a] if quarter else 0)
            return pl.ds(start, halves[a] - cuts[a] if quarter else cuts[a])

        def landing(a, shard, half, quarter):
            return dsts[a].at[shard, rows(a, half, quarter), :]

        def direct(a, nb, quarter, shard):
            k = (a * 2 + nb) * 2 + quarter
            px, py = chips[nb]
            return pltpu.make_async_remote_copy(
                src_ref=srcs[a].at[rows(a, c, quarter), :], dst_ref=landing(a, shard, c, quarter),
                send_sem=send.at[k], recv_sem=recv.at[k], device_id=(px, py, c), device_id_type=MESH)

        def relay(a, quarter, shard):
            k = n_direct + a * 2 + quarter
            px, py = chips[1 - quarter]
            return pltpu.make_async_remote_copy(
                src_ref=landing(a, shard, c, quarter), dst_ref=landing(a, shard, c, quarter),
                send_sem=send.at[k], recv_sem=recv.at[k], device_id=(px, py, c), device_id_type=MESH)

        def to_sibling(a, origin, quarter, half):
            k = n_direct + n_relay + (a * 3 + origin) * 2 + quarter
            return pltpu.make_async_remote_copy(
                src_ref=landing(a, ids[origin], half, quarter), dst_ref=landing(a, ids[origin], half, quarter),
                send_sem=send.at[k], recv_sem=recv.at[k], device_id=(x, y, 1 - c), device_id_type=MESH)

        def small_copy(j, shard):
            k = n_direct + n_relay + n_sib + j
            px, py = chips[j]
            return pltpu.make_async_remote_copy(
                src_ref=small_src, dst_ref=small_dst.at[shard], send_sem=send.at[k], recv_sem=recv.at[k],
                device_id=(px, py, c), device_id_type=MESH)

        own = [pltpu.make_async_copy(srcs[a], dsts[a].at[me], local.at[a]) for a in range(n)]
        own.append(pltpu.make_async_copy(small_src, small_dst.at[me], local.at[n]))
        for cp in own:
            cp.start()
        sent = [direct(a, nb, q, me) for q in range(2) for a in range(n) for nb in range(2)]
        sent += [small_copy(j, me) for j in range(3)]
        for cp in sent:
            cp.start()

        def passed_on(cp):
            cp.start()
            sent.append(cp)

        for q in range(2):
            for a in range(n):
                for nb in range(2):
                    direct(a, nb, q, ids[nb]).wait_recv()
                    passed_on(to_sibling(a, nb, q, c))
                    if nb == q:
                        passed_on(relay(a, q, ids[nb]))
        for a in range(n):
            for q in range(2):
                relay(a, q, ids[2]).wait_recv()
                passed_on(to_sibling(a, 2, q, c))
        for j in range(3):
            small_copy(j, ids[j]).wait_recv()
            for a in range(n):
                for q in range(2):
                    to_sibling(a, j, q, 1 - c).wait_recv()
        for cp in sent:
            cp.wait_send()
        for cp in own:
            cp.wait()
        _assemble_rows(dsts[0], shard_rows, segments, whole_ref)

    vm = pl.BlockSpec(memory_space=pltpu.VMEM)
    n_sems = n_direct + n_relay + n_sib + 3
    out = pl.pallas_call(
        body, name="gather_shards",
        in_specs=[vm] * (n + 1), out_specs=[vm] * (n + 2),
        out_shape=[jax.ShapeDtypeStruct((N_CHIPS,) + p.shape, p.dtype) for p in parts + [small]]
        + [jax.ShapeDtypeStruct((out_rows, parts[0].shape[1]), parts[0].dtype)],
        scratch_shapes=[pltpu.SemaphoreType.DMA((n_sems,)), pltpu.SemaphoreType.DMA((n_sems,)),
                        pltpu.SemaphoreType.DMA((n + 1,))],
        compiler_params=pltpu.CompilerParams(vmem_limit_bytes=VMEM_LIMIT),
    )(*parts, small)
    return out[1:]


def _allsum_rows(part):
    rows_n = part.shape[0]

    def body(x_ref, gath_ref, sum_ref, send_sems, recv_sems, local_sem):
        x, y, c = _position()
        me, sibling = (x, y, c), (x, y, 1 - c)
        chips = [(1 - x, y), (x, 1 - y), (1 - x, 1 - y)]

        def rows(px, py, pc):
            return gath_ref.at[pl.ds((4 * px + 2 * py + pc) * rows_n, rows_n), :]

        def copy(k, block, to, src=None):
            return pltpu.make_async_remote_copy(
                src_ref=rows(*block) if src is None else src, dst_ref=rows(*block),
                send_sem=send_sems.at[k], recv_sem=recv_sems.at[k], device_id=to, device_id_type=MESH)

        mine = pltpu.make_async_copy(x_ref, rows(*me), local_sem)
        mine.start()
        first = [copy(0, me, sibling, src=x_ref)]
        first += [copy(1 + j, me, (*chip, c), src=x_ref) for j, chip in enumerate(chips)]
        for cp in first:
            cp.start()
        passed = [copy(4 + j, (*chip, c), sibling) for j, chip in enumerate(chips)]
        for j, chip in enumerate(chips):
            copy(1 + j, (*chip, c), me).wait_recv()
            passed[j].start()
        copy(0, sibling, me).wait_recv()
        for j, chip in enumerate(chips):
            copy(4 + j, (*chip, 1 - c), me).wait_recv()
        for cp in first + passed:
            cp.wait_send()
        mine.wait()
        total = gath_ref[pl.ds(0, rows_n), :]
        for d in range(1, N_DEV):
            total = total + gath_ref[pl.ds(d * rows_n, rows_n), :]
        sum_ref[...] = total

    vm = pl.BlockSpec(memory_space=pltpu.VMEM)
    return pl.pallas_call(
        body, name="allsum_rows", in_specs=[vm], out_specs=[vm, vm],
        out_shape=[jax.ShapeDtypeStruct((N_DEV * rows_n, D), f32), jax.ShapeDtypeStruct((rows_n, D), f32)],
        scratch_shapes=[pltpu.SemaphoreType.DMA((7,)), pltpu.SemaphoreType.DMA((7,)), pltpu.SemaphoreType.DMA],
    )(part)[1]


PAIR_ROWS = 16


def _pair_reduce(name, pieces):
    _, r, n = pieces.shape

    def body(p_ref, o_ref, land, send, recv):
        x, y, c = _position()

        def remote(j, half):
            return pltpu.make_async_remote_copy(
                src_ref=p_ref.at[2 * j + half], dst_ref=land.at[j], send_sem=send.at[j], recv_sem=recv.at[j],
                device_id=(x, y, 1 - c), device_id_type=MESH)

        sends = [remote(j, 1 - c) for j in range(N_CHIPS)]
        for cp in sends:
            cp.start()
        for j in range(N_CHIPS):
            remote(j, c).wait_recv()

            def add_rows(i, carry, j=j):
                rows = pl.ds(pl.multiple_of(i * PAIR_ROWS, PAIR_ROWS), PAIR_ROWS)
                o_ref[j, rows, :] = (p_ref[2 * j + c, rows, :].astype(f32) + land[j, rows, :].astype(f32)).astype(bf16)
                return carry

            lax.fori_loop(0, r // PAIR_ROWS, add_rows, 0)
        for cp in sends:
            cp.wait_send()

    vm = pl.BlockSpec(memory_space=pltpu.VMEM)
    return pl.pallas_call(
        body, name=name, in_specs=[vm], out_specs=vm,
        out_shape=jax.ShapeDtypeStruct((N_CHIPS, r, n), bf16),
        scratch_shapes=[pltpu.VMEM((N_CHIPS, r, n), bf16), pltpu.SemaphoreType.DMA((N_CHIPS,)),
                        pltpu.SemaphoreType.DMA((N_CHIPS,))],
        compiler_params=pltpu.CompilerParams(vmem_limit_bytes=VMEM_LIMIT),
    )(pieces)


def _chip_exchange(arrs):
    n = len(arrs)
    heights = [a.shape[1] for a in arrs]
    cuts = [-(-r // 32) * 16 for r in heights]

    def body(*refs):
        srcs, dsts, relays = refs[:n], refs[n:2 * n], refs[2 * n:3 * n]
        send, recv, local = refs[3 * n:]
        x, y, c = _position()
        me = 2 * x + y
        chips = [(1 - x, y), (x, 1 - y), (1 - x, 1 - y)]
        ids = [2 * px + py for px, py in chips]

        def rows(a, quarter):
            return pl.ds(cuts[a], heights[a] - cuts[a]) if quarter else pl.ds(0, cuts[a])

        def held(a, quarter):
            size = heights[a] - cuts[a] if quarter else cuts[a]
            return relays[a].at[quarter, pl.ds(0, size), :]

        def direct(a, nb, piece, landing):
            px, py = chips[nb]
            return pltpu.make_async_remote_copy(
                src_ref=srcs[a].at[piece], dst_ref=dsts[a].at[landing], send_sem=send.at[a * 2 + nb],
                recv_sem=recv.at[a * 2 + nb], device_id=(px, py, c), device_id_type=MESH)

        def first_hop(a, quarter):
            k = 2 * n + a * 2 + quarter
            px, py = chips[quarter]
            return pltpu.make_async_remote_copy(
                src_ref=srcs[a].at[ids[2], rows(a, quarter), :], dst_ref=held(a, quarter), send_sem=send.at[k],
                recv_sem=recv.at[k], device_id=(px, py, c), device_id_type=MESH)

        def second_hop(a, quarter, origin):
            k = 4 * n + a * 2 + quarter
            px, py = chips[1 - quarter]
            return pltpu.make_async_remote_copy(
                src_ref=held(a, quarter), dst_ref=dsts[a].at[origin, rows(a, quarter), :], send_sem=send.at[k],
                recv_sem=recv.at[k], device_id=(px, py, c), device_id_type=MESH)

        own = [pltpu.make_async_copy(srcs[a].at[me], dsts[a].at[me], local.at[a]) for a in range(n)]
        sent = [first_hop(a, q) for a in range(n) for q in range(2)]
        sent += [direct(a, nb, ids[nb], me) for a in range(n) for nb in range(2)]
        for cp in sent + own:
            cp.start()
        for a in range(n):
            for q in range(2):
                first_hop(a, q).wait_recv()
                sent.append(second_hop(a, q, ids[q]))
                sent[-1].start()
        for a in range(n):
            for nb in range(2):
                direct(a, nb, me, ids[nb]).wait_recv()
            for q in range(2):
                second_hop(a, q, ids[2]).wait_recv()
        for cp in sent:
            cp.wait_send()
        for cp in own:
            cp.wait()

    anyspec = pl.BlockSpec(memory_space=pl.ANY)
    out = pl.pallas_call(
        body, name="chip_exchange", in_specs=[anyspec] * n, out_specs=[anyspec] * (2 * n),
        out_shape=[jax.ShapeDtypeStruct(a.shape, a.dtype) for a in arrs]
        + [jax.ShapeDtypeStruct((2, cut, a.shape[2]), a.dtype) for a, cut in zip(arrs, cuts)],
        scratch_shapes=[pltpu.SemaphoreType.DMA((6 * n,)), pltpu.SemaphoreType.DMA((6 * n,)),
                        pltpu.SemaphoreType.DMA((n,))],
    )(*arrs)
    return out[:n]


def _swap_halves(arrs):
    n = len(arrs)

    def body(*refs):
        srcs, dsts = refs[:n], refs[n:2 * n]
        send, recv, local = refs[2 * n:]
        x, y, c = _position()

        def remote(a, landing):
            return pltpu.make_async_remote_copy(
                src_ref=srcs[a], dst_ref=dsts[a].at[landing], send_sem=send.at[a], recv_sem=recv.at[a],
                device_id=(x, y, 1 - c), device_id_type=MESH)

        own = [pltpu.make_async_copy(srcs[a], dsts[a].at[c], local.at[a]) for a in range(n)]
        sends = [remote(a, c) for a in range(n)]
        for cp in sends + own:
            cp.start()
        for a in range(n):
            remote(a, 1 - c).wait_recv()
        for cp in sends:
            cp.wait_send()
        for cp in own:
            cp.wait()

    vm = pl.BlockSpec(memory_space=pltpu.VMEM)
    return pl.pallas_call(
        body, name="swap_halves", in_specs=[vm] * n, out_specs=[vm] * n,
        out_shape=[jax.ShapeDtypeStruct((2,) + a.shape, a.dtype) for a in arrs],
        scratch_shapes=[pltpu.SemaphoreType.DMA((n,)), pltpu.SemaphoreType.DMA((n,)), pltpu.SemaphoreType.DMA((n,))],
        compiler_params=pltpu.CompilerParams(vmem_limit_bytes=VMEM_LIMIT),
    )(*arrs)


def _row_block(r):
    return 128 if r % 128 == 0 else r


def _sum_slots(name, slots):
    s, r, n = slots.shape
    rb = _row_block(r)

    def body(s_ref, o_ref):
        total = s_ref[0].astype(f32)
        for d in range(1, s):
            total = total + s_ref[d].astype(f32)
        o_ref[...] = total

    return pl.pallas_call(
        body, name=name, grid=(r // rb,),
        in_specs=[pl.BlockSpec((s, rb, n), lambda i: (0, i, 0))],
        out_specs=pl.BlockSpec((rb, n), lambda i: (i, 0)),
        out_shape=jax.ShapeDtypeStruct((r, n), f32),
        compiler_params=_params(("parallel",), VMEM_LIMIT),
    )(slots)


def _adamw(name, w, g, m, v):
    r, n = w.shape
    if r % 128 == 0 or r * n <= 128 * 1024:
        rb, nb = _row_block(r), n
    else:
        rb, nb = r, LANES

    def body(w_ref, g_ref, m_ref, v_ref, d_ref, nm_ref, nv_ref):
        gv = g_ref[...]
        m2 = ADAM_B1 * m_ref[...] + (1.0 - ADAM_B1) * gv
        v2 = ADAM_B2 * v_ref[...] + (1.0 - ADAM_B2) * (gv * gv)
        m_hat = m2 / (1.0 - ADAM_B1 ** ADAM_STEP)
        v_hat = v2 / (1.0 - ADAM_B2 ** ADAM_STEP)
        d_ref[...] = (-ADAM_LR) * (m_hat / (jnp.sqrt(v_hat) + ADAM_EPS) + ADAM_WD * w_ref[...])
        nm_ref[...] = m2
        nv_ref[...] = v2

    spec = pl.BlockSpec((rb, nb), lambda i, j: (i, j))
    return pl.pallas_call(
        body, name=name, grid=(r // rb, n // nb), in_specs=[spec] * 4, out_specs=[spec] * 3,
        out_shape=[jax.ShapeDtypeStruct((r, n), f32)] * 3,
        compiler_params=_params(("parallel", "parallel"), VMEM_LIMIT),
    )(w, g, m, v)


def _local_step(x2, tgt2, seq, wt):
    nb = x2.shape[0] // seq
    h, qkv = _norm_qkv(x2, wt["pre_w"], wt["w_all"], wt["b_qkv"])
    rest = _mm("in_rest", h, wt["w_all"], wt["b_rest"], bf16, 1024, 1024, w_rows=(3 * D, 5 * D))
    f128 = _mm("in_f", h, wt["w_all"], wt["b_f"], f32, 1024, LANES, w_rows=(8 * D, LANES))
    c = _forget_prep(f128, seq)
    qa, ka = _attn_prep(qkv, c)
    o_att, pa, lse = _attn_fwd(qa, ka, qkv, rest, seq)
    ya = _mm("proj_a", pa, wt["w_a"], None, bf16, 1024, D)
    rnn_w = (wt["conv_w"], wt["conv_b"], wt["wa_d"], wt["wx_d"], wt["ba"], wt["bx"], wt["lam"])
    xc, a, hrec, pr = _rnn_fwd(rest, *rnn_w, seq)
    yr = _mm("proj_r", pr, wt["w_r"], None, bf16, 1024, D)
    do, dy, mrg, loss8, d_post = _out_proj_loss(rest, ya, yr, wt["w_o"], x2, tgt2, wt["post_w"])
    dya, dyr, dmga, dmgr = _out_bwd(do, rest, ya, yr, wt["w_o"])
    doa, dga, delta = _branch_bwd("branch_a_bwd", dya, rest, 0, o_att, wt["w_a"], bf16, head_sums=True)
    dhrec, dgr = _branch_bwd("branch_r_bwd", dyr, rest, 2, hrec, wt["w_r"], bf16)
    d_wo, _ = _tn_mm("dw_out", mrg, do, D)
    d_wa, _ = _tn_mm("dw_branch_a", pa, dya, D)
    d_wr, _ = _tn_mm("dw_branch_r", pr, dyr, D)
    dxr, d_wad, d_wxd, vec = _rnn_bwd(dhrec, a, hrec, xc, rest, *rnn_w, seq)
    dq, dk, dv, dc = _attn_bwd(qa, ka, qkv, doa, lse, delta, seq)
    df, db_f = _forget_bwd(dc, f128, seq)
    pieces = [dq, dk, dv, dga, dxr, dgr, dmga, dmgr]
    gx, d_pre = _in_bwd(pieces, df, x2, dy, wt["w_all"], wt["pre_w"])
    names = ["q", "k", "v", "ga", "xr", "gr", "mga", "mgr"]
    dws, dbs = [], []
    for nm, piece in zip(names, pieces):
        dw_p, db_p = _tn_mm("dw_in_" + nm, piece, h, D, bf16)
        dws.append((dw_p, D))
        dbs.append(db_p)
    dw_f, _ = _tn_mm("dw_in_f", df, h, D, bf16)
    shard_rows = IN_TOTAL // N_CHIPS
    w_in_pieces = _pack_pieces(dws[:3] + [(dw_f, HEADS)] + dws[3:] + [(None, IN_TOTAL - IN_USED)], shard_rows,
                               _padded_rows(shard_rows))
    d_b_in = jnp.concatenate(dbs[:3] + [db_f[:, :HEADS]] + dbs[3:] + [jnp.zeros((1, IN_TOTAL - IN_USED), f32)], axis=1)
    return dict(loss=loss8[0, 0], grad_x=gx, pre_w=d_pre, w_in_pieces=w_in_pieces, b_in=d_b_in, conv_w=vec[4:8],
                conv_b=vec[3:4],
                wa_d=d_wad, ba=vec[0:1], wx_d=d_wxd, bx=vec[1:2], lam=vec[2:3], w_a=d_wa, w_r=d_wr, w_o=d_wo,
                post_w=d_post)


def _block_diag(w):
    g, bw, _ = w.shape
    eye = jnp.eye(g, dtype=w.dtype)
    return (w[:, :, None, :] * eye[:, None, :, None]).reshape(g * bw, g * bw)


def _gate_blocks(diag):
    half = diag.shape[1] // 2
    return jnp.stack([diag[:, :half, :half], diag[:, half:, half:]], axis=1).reshape(-1, half, half)


def _padded_rows(rows):
    return -(-rows // 32) * 32


def _pad_cols(a, n):
    return jnp.pad(a, ((0, 0), (0, n - a.shape[1])))


def _pad_rows(a, n):
    return jnp.pad(a, ((0, n - a.shape[0]), (0, 0)))


def kernel(x, pre_norm_w, w_in, b_in, conv_w, conv_b, rg_wa, rg_ba, rg_wx, rg_bx, rg_lambda, w_branch_a, w_branch_r, w_out, post_norm_w, loss_target, m_pre_norm_w, m_w_in, m_b_in, m_conv_w, m_conv_b, m_rg_wa, m_rg_ba, m_rg_wx, m_rg_bx, m_rg_lambda, m_w_branch_a, m_w_branch_r, m_w_out, m_post_norm_w, v_pre_norm_w, v_w_in, v_b_in, v_conv_w, v_conv_b, v_rg_wa, v_rg_ba, v_rg_wx, v_rg_bx, v_rg_lambda, v_w_branch_a, v_w_branch_r, v_w_out, v_post_norm_w):
    nb, seq, _ = x.shape
    chip = 2 * lax.axis_index("x") + lax.axis_index("y")
    n_groups = rg_wa.shape[1]

    w_in_t = jnp.transpose(w_in[0])
    shard_cols = w_in_t.shape[0]
    padded = _padded_rows(shard_cols)
    q_end, f_end = 3 * D, 3 * D + HEADS
    segments = [(0, 0, q_end), (q_end, f_end, IN_USED - f_end), (IN_USED - HEADS, q_end, HEADS)]
    g_a, g_r, g_o, g_cw, w_all = _gather_shards(
        [_pad_rows(w_in_t.astype(bf16), padded), w_branch_a[0].astype(bf16), w_branch_r[0].astype(bf16),
         w_out[0].astype(bf16)], conv_w[0], shard_cols, segments, IN_USED - HEADS + LANES)
    wt = dict(
        pre_w=pre_norm_w, post_w=post_norm_w,
        w_all=w_all, b_qkv=b_in[:, :q_end], b_f=_pad_cols(b_in[:, q_end:f_end], LANES), b_rest=b_in[:, f_end:IN_USED],
        w_a=g_a.reshape(D, D), w_r=g_r.reshape(D, D), w_o=g_o.reshape(D, D),
        conv_w=jnp.transpose(g_cw, (1, 0, 2)).reshape(4, D), conv_b=conv_b,
        wa_d=_block_diag(rg_wa[0]).astype(bf16), wx_d=_block_diag(rg_wx[0]).astype(bf16),
        ba=rg_ba, bx=rg_bx, lam=rg_lambda)

    part = _local_step(x.reshape(nb * seq, D), loss_target.reshape(nb * seq, D), seq, wt)
    loss = lax.psum(part["loss"], ("x", "y", "c"))
    grad_x = part["grad_x"].reshape(nb, seq, D)

    small = jnp.concatenate([
        part["pre_w"], _pad_cols(part["b_in"], 10 * D).reshape(10, D), part["conv_b"],
        _gate_blocks(part["wa_d"]).reshape(-1, D), part["ba"],
        _gate_blocks(part["wx_d"]).reshape(-1, D), part["bx"], part["lam"], part["post_w"],
        part["conv_w"]], axis=0)
    n_small = small.shape[0]
    n_rep = n_small - 4
    tot = _allsum_rows(_pad_rows(small, -(-n_small // 8) * 8))
    g_rep = tot[:n_rep]
    g_conv_w = lax.dynamic_slice_in_dim(tot[n_rep:n_small], chip * (D // N_CHIPS), D // N_CHIPS, axis=1)

    def unpack(p):
        o = [0]

        def take(k):
            o[0] += k
            return p[o[0] - k:o[0]]

        pre = take(1)
        b = take(10).reshape(1, 10 * D)[:, :IN_TOTAL]
        cb = take(1)
        wa = take(64).reshape(rg_wa.shape)
        ba = take(1)
        wx = take(64).reshape(rg_wx.shape)
        bx = take(1)
        lam = take(1)
        post = take(1)
        return dict(pre_norm_w=pre, b_in=b, conv_b=cb, rg_wa=wa, rg_ba=ba, rg_wx=wx, rg_bx=bx, rg_lambda=lam,
                    post_norm_w=post)

    grads = unpack(g_rep)
    replicated = dict(
        pre_norm_w=(pre_norm_w, m_pre_norm_w, v_pre_norm_w), b_in=(b_in, m_b_in, v_b_in),
        conv_b=(conv_b, m_conv_b, v_conv_b), rg_wa=(rg_wa, m_rg_wa, v_rg_wa), rg_ba=(rg_ba, m_rg_ba, v_rg_ba),
        rg_wx=(rg_wx, m_rg_wx, v_rg_wx), rg_bx=(rg_bx, m_rg_bx, v_rg_bx),
        rg_lambda=(rg_lambda, m_rg_lambda, v_rg_lambda), post_norm_w=(post_norm_w, m_post_norm_w, v_post_norm_w))
    deltas, new_m, new_v = {}, {}, {}
    for name, (w, m, v) in replicated.items():
        as2d = lambda a: a.reshape(-1, D) if a.ndim > 2 else a
        upd = _adamw("adamw_" + name, as2d(w), as2d(grads[name]), as2d(m), as2d(v))
        deltas[name], new_m[name], new_v[name] = [a.reshape(w.shape) for a in upd]

    p_aro = jnp.concatenate([part[k].reshape(N_DEV, D // N_DEV, D) for k in ("w_a", "w_r", "w_o")], axis=1)
    s_in, s_aro = _chip_exchange([_pair_reduce("pair_w_in", part["w_in_pieces"]),
                                  _pair_reduce("pair_w_aro", p_aro.astype(bf16))])
    f_in, f_aro = _swap_halves([_sum_slots("sum_w_in", s_in), _sum_slots("sum_w_aro", s_aro)])
    g_w_in_t = f_in.reshape(padded, D)[:shard_cols]
    rows = D // N_DEV
    g_aro = [f_aro[:, i * rows:(i + 1) * rows, :].reshape(2 * rows, D) for i in range(3)]

    w_in_upd = _adamw("adamw_w_in", w_in_t, g_w_in_t, jnp.transpose(m_w_in[0]), jnp.transpose(v_w_in[0]))
    g_w_in, d_w_in, nm_w_in, nv_w_in = [jnp.transpose(a) for a in (g_w_in_t, *w_in_upd)]
    upd_a = _adamw("adamw_w_branch_a", w_branch_a[0], g_aro[0], m_w_branch_a[0], v_w_branch_a[0])
    upd_r = _adamw("adamw_w_branch_r", w_branch_r[0], g_aro[1], m_w_branch_r[0], v_w_branch_r[0])
    upd_o = _adamw("adamw_w_out", w_out[0], g_aro[2], m_w_out[0], v_w_out[0])
    d_aro, nm_aro, nv_aro = zip(upd_a, upd_r, upd_o)
    d_cw, nm_cw, nv_cw = _adamw("adamw_conv_w", conv_w[0], g_conv_w, m_conv_w[0], v_conv_w[0])

    def sharded(t_in, t_aro, t_cw):
        return dict(w_in=t_in[None], conv_w=t_cw[None], w_branch_a=t_aro[0][None], w_branch_r=t_aro[1][None],
                    w_out=t_aro[2][None])

    order = ["pre_norm_w", "w_in", "b_in", "conv_w", "conv_b", "rg_wa", "rg_ba", "rg_wx", "rg_bx", "rg_lambda",
             "w_branch_a", "w_branch_r", "w_out", "post_norm_w"]
    outs = [loss, grad_x]
    for rep, shd in ((grads, sharded(g_w_in, g_aro, g_conv_w)), (deltas, sharded(d_w_in, d_aro, d_cw)),
                     (new_m, sharded(nm_w_in, nm_aro, nm_cw)), (new_v, sharded(nv_w_in, nv_aro, nv_cw))):
        both = {**rep, **shd}
        outs.extend(both[k] for k in order)
    return tuple(outs)
```

```python
import jax
import jax.numpy as jnp
from jax import lax
from jax.experimental import pallas as pl
from jax.experimental.pallas import tpu as pltpu

f32 = jnp.float32
bf16 = jnp.bfloat16

D = 1024
HEADS = 16
HEAD_PAIRS = 8
LANES = 128
NORM_EPS = 1e-6
MASK_VALUE = -1e30
RG_C = 8.0
QK_SCALE = 0.125
TQ = 256
ATT_GROUP = 8
ATT_GROUP_FWD = 16
TL = 256
TM = 512
PREV_ROWS = 16
IN_USED = 8 * D + HEADS
IN_TOTAL = 9 * D + HEADS
N_CHIPS = 4
N_DEV = 8
ADAM_LR, ADAM_B1, ADAM_B2, ADAM_EPS, ADAM_WD, ADAM_STEP = 0.001, 0.9, 0.999, 1e-08, 0.01, 10
VMEM_LIMIT = 56 * 1024 * 1024
MESH = pl.DeviceIdType.MESH


def _dot(a, b):
    return jnp.dot(a, b, preferred_element_type=f32)


def _dot_nt(a, b):
    return lax.dot_general(a, b, (((1,), (1,)), ((), ())), preferred_element_type=f32)


def _dot_tn(a, b):
    return lax.dot_general(a, b, (((0,), (0,)), ((), ())), preferred_element_type=f32)


def _sig(x):
    return 0.5 * jnp.tanh(0.5 * x) + 0.5


def _softplus(x):
    return jnp.maximum(x, 0.0) + jnp.log(1.0 + jnp.exp(-jnp.abs(x)))


def _params(sem, vmem=None):
    return pltpu.CompilerParams(dimension_semantics=sem, vmem_limit_bytes=vmem)


def _tile(tm, width, cb=0):
    return pl.BlockSpec((tm, width), lambda i, cb=cb: (i, cb))


def _whole(shape):
    nd = len(shape)
    return pl.BlockSpec(shape, lambda *_: (0,) * nd)


def _norm_qkv(x, w_pre, w_all, b_qkv, tm=1024):
    t = x.shape[0]
    tm = min(tm, t)
    n = b_qkv.shape[1]

    def body(x_ref, wp_ref, w_ref, b_ref, h_ref, o_ref):
        @pl.when(pl.program_id(1) == 0)
        def _():
            xv = x_ref[...]
            r = lax.rsqrt(jnp.mean(xv * xv, axis=-1, keepdims=True) + NORM_EPS)
            h_ref[...] = (xv * r * wp_ref[...]).astype(bf16)

        o_ref[...] = (_dot_nt(h_ref[...], w_ref[...]) + b_ref[...]).astype(bf16)

    return pl.pallas_call(
        body, name="norm_qkv", grid=(t // tm, n // D),
        in_specs=[pl.BlockSpec((tm, D), lambda i, j: (i, 0)), _whole((1, D)), pl.BlockSpec((D, D), lambda i, j: (j, 0)),
                  pl.BlockSpec((1, D), lambda i, j: (0, j))],
        out_specs=[pl.BlockSpec((tm, D), lambda i, j: (i, 0)), pl.BlockSpec((tm, D), lambda i, j: (i, j))],
        out_shape=[jax.ShapeDtypeStruct((t, D), bf16), jax.ShapeDtypeStruct((t, n), bf16)],
        compiler_params=_params(("parallel", "arbitrary"), VMEM_LIMIT),
    )(x, w_pre, w_all, b_qkv)


def _mm(name, a, w, w_rows, bias, out_dtype, tm, tn):
    t, k = a.shape
    tm = min(tm, t)
    row0, n = w_rows
    assert row0 % tn == 0

    def body(a_ref, w_ref, b_ref, o_ref):
        o_ref[...] = (_dot_nt(a_ref[...], w_ref[...]) + b_ref[...]).astype(out_dtype)

    return pl.pallas_call(
        body, name=name, grid=(t // tm, n // tn),
        in_specs=[pl.BlockSpec((tm, k), lambda i, j: (i, 0)), pl.BlockSpec((tn, k), lambda i, j: (row0 // tn + j, 0)),
                  pl.BlockSpec((1, tn), lambda i, j: (0, j))],
        out_specs=pl.BlockSpec((tm, tn), lambda i, j: (i, j)), out_shape=jax.ShapeDtypeStruct((t, n), out_dtype),
        compiler_params=_params(("parallel", "parallel"), VMEM_LIMIT),
    )(a, w, bias)


def _forget_prep(f128, seq):
    t = f128.shape[0]
    nb = seq // LANES

    def body(f_ref, c_ref):
        r = lax.broadcasted_iota(jnp.int32, (LANES, LANES), 0)
        cidx = lax.broadcasted_iota(jnp.int32, (LANES, LANES), 1)
        tri = (r >= cidx).astype(f32)
        carry = jnp.zeros((1, LANES), f32)
        for blk in range(nb):
            fv = f_ref[pl.ds(blk * LANES, LANES), :]
            lf = -_softplus(-fv)
            c_ref[pl.ds(blk * LANES, LANES), :] = (
                jnp.dot(tri, lf, preferred_element_type=f32, precision=lax.Precision.HIGHEST) + carry)
            carry = carry + jnp.sum(lf, axis=0, keepdims=True)

    return pl.pallas_call(
        body, name="forget_prep", grid=(t // seq,),
        in_specs=[pl.BlockSpec((seq, LANES), lambda b: (b, 0))],
        out_specs=pl.BlockSpec((seq, LANES), lambda b: (b, 0)),
        out_shape=jax.ShapeDtypeStruct((t, LANES), f32),
        compiler_params=_params(("parallel",)),
    )(f128)


def _split3(cv):
    hi = cv.astype(bf16)
    r1 = cv - hi.astype(f32)
    mid = r1.astype(bf16)
    lo = (r1 - mid.astype(f32)).astype(bf16)
    return hi, mid, lo


def _attn_prep(qkv, c):
    t = qkv.shape[0]

    def body(q_ref, k_ref, c_ref, qa_ref, ka_ref):
        lane = lax.broadcasted_iota(jnp.int32, (1, LANES), 1)
        cv = c_ref[...]
        one = jnp.ones((), bf16)
        zero = jnp.zeros((), bf16)
        q_ones = jnp.where((lane >= 67) & (lane < 70), one, zero)
        k_ones = jnp.where((lane >= 64) & (lane < 67), one, zero)
        for head in range(HEADS):
            pair = pl.ds((head // 2) * LANES, LANES)
            ch = jnp.sum(jnp.where(lane == head, cv, 0.0), axis=1, keepdims=True)
            hi, mid, lo = _split3(ch)
            q2, k2 = q_ref[:, pair], k_ref[:, pair]
            if head % 2 == 1:
                q2, k2 = pltpu.roll(q2, 64, 1), pltpu.roll(k2, 64, 1)
            qa = jnp.where(lane < 64, q2 * jnp.asarray(QK_SCALE, bf16),
                           jnp.where(lane == 64, hi, jnp.where(lane == 65, mid, jnp.where(lane == 66, lo, q_ones))))
            ka = jnp.where(lane < 64, k2,
                           jnp.where(lane == 67, -hi, jnp.where(lane == 68, -mid, jnp.where(lane == 69, -lo, k_ones))))
            qa_ref[:, pl.ds(head * LANES, LANES)] = qa
            ka_ref[:, pl.ds(head * LANES, LANES)] = ka

    tm = min(TM, t)
    out = pl.BlockSpec((tm, 2 * D), lambda i: (i, 0))
    return pl.pallas_call(
        body, name="attn_prep", grid=(t // tm,),
        in_specs=[_tile(tm, D, 0), _tile(tm, D, 1), _tile(tm, LANES)],
        out_specs=[out, out],
        out_shape=[jax.ShapeDtypeStruct((t, 2 * D), bf16)] * 2,
        compiler_params=_params(("parallel",)),
    )(qkv, qkv, c)


def _attn_fwd(qa, ka, qkv, rest, seq):
    t = qkv.shape[0]
    nb, nq = t // seq, seq // TQ

    hg = ATT_GROUP_FWD
    ng = HEADS // hg

    def body(q_ref, k_ref, v_ref, ga_ref, o_ref, pa_ref, lse_ref, acc_scr):
        qi, gi = pl.program_id(1), pl.program_id(2)
        krow = lax.broadcasted_iota(jnp.int32, (TQ, TQ), 0)
        qcol = lax.broadcasted_iota(jnp.int32, (TQ, TQ), 1)
        acc_scr[...] = jnp.zeros_like(acc_scr)

        def kv_step(kt, carry, masked):
            ks = pl.multiple_of(kt * TQ, TQ)
            sts = [_dot_nt(k_ref[pl.ds(ks, TQ), pl.ds(g * LANES, LANES)], q_ref[:, pl.ds(g * LANES, LANES)])
                   for g in range(hg)]
            if masked:
                sts = [jnp.where(krow <= qcol, st, MASK_VALUE) for st in sts]
            m_new = [jnp.maximum(carry[g][0], jnp.max(sts[g], axis=0, keepdims=True)) for g in range(hg)]
            ps = [jnp.exp(sts[g] - m_new[g]) for g in range(hg)]
            alphas = [jnp.exp(carry[g][0] - m_new[g]) for g in range(hg)]
            phi = [ps[g].astype(bf16) for g in range(hg)]
            plo = [(ps[g] - phi[g].astype(f32)).astype(bf16) for g in range(hg)]
            vs = [v_ref[pl.ds(ks, TQ), pl.ds(j * LANES, LANES)] for j in range(hg // 2)]
            pvs = [_dot_tn(vs[g // 2], phi[g]) + _dot_tn(vs[g // 2], plo[g]) for g in range(hg)]
            olds = [acc_scr[g] for g in range(hg)]
            for g in range(hg):
                acc_scr[g] = alphas[g] * olds[g] + pvs[g]
            return tuple((m_new[g], alphas[g] * carry[g][1] + jnp.sum(ps[g], axis=0, keepdims=True))
                         for g in range(hg))

        init = tuple((jnp.full((1, TQ), MASK_VALUE, f32), jnp.zeros((1, TQ), f32)) for _ in range(hg))
        carry = lax.fori_loop(0, qi, lambda kt, cr: kv_step(kt, cr, False), init)
        stats = kv_step(qi, carry, True)
        drow = lax.broadcasted_iota(jnp.int32, (LANES, TQ), 0)
        for g in range(hg):
            m, l = stats[g]
            lse_ref[0, pl.ds(hg * gi + g, 1), :] = m + jnp.log(l)
        for j in range(hg // 2):
            o2 = jnp.where(drow < 64, acc_scr[2 * j] / stats[2 * j][1], acc_scr[2 * j + 1] / stats[2 * j + 1][1]).T
            o_ref[:, pl.ds(j * LANES, LANES)] = o2
            ga = ga_ref[:, pl.ds(j * LANES, LANES)].astype(f32)
            pa_ref[:, pl.ds(j * LANES, LANES)] = (o2 * (ga * _sig(ga))).astype(bf16)

    vw = hg * 64
    tile = pl.BlockSpec((TQ, vw), lambda b, qi, gi: (b * nq + qi, gi))
    return pl.pallas_call(
        body, name="attn_fwd", grid=(nb, nq, ng),
        in_specs=[pl.BlockSpec((TQ, hg * LANES), lambda b, qi, gi: (b * nq + qi, gi)),
                  pl.BlockSpec((seq, hg * LANES), lambda b, qi, gi: (b, gi)),
                  pl.BlockSpec((seq, vw), lambda b, qi, gi: (b, 2 * ng + gi)), tile],
        out_specs=[tile, tile, pl.BlockSpec((1, HEADS, TQ), lambda b, qi, gi: (b * nq + qi, 0, 0))],
        out_shape=[jax.ShapeDtypeStruct((t, D), f32), jax.ShapeDtypeStruct((t, D), bf16),
                   jax.ShapeDtypeStruct((t // TQ, HEADS, TQ), f32)],
        scratch_shapes=[pltpu.VMEM((hg, LANES, TQ), f32)],
        compiler_params=_params(("parallel", "parallel", "arbitrary"), VMEM_LIMIT),
    )(qa, ka, qkv, rest)


def _shifted_rows(x, top8, prev8, shift, row, row8):
    body = pltpu.roll(x, shift, 0)
    head = jnp.where(row8 < shift, pltpu.roll(prev8, shift, 0), pltpu.roll(top8, shift, 0))
    return body, head


def _rnn_gates(xc, wa_ref, wx_ref, ba_ref, bx_ref, lam_ref):
    xcb = xc.astype(bf16)
    r = _sig(_dot(xcb, wa_ref[...]) + ba_ref[...])
    i = _sig(_dot(xcb, wx_ref[...]) + bx_ref[...])
    sp = _softplus(-lam_ref[...])
    log_a = (-RG_C) * r * sp
    th = jnp.tanh(log_a)
    w1 = (-2.0) * th / (1.0 - th)
    sq = jnp.sqrt(jnp.maximum(w1, 0.0))
    return r, i, sp, log_a, w1, sq


def _conv_tile(x_ref, xprev_ref, has_prev, cw_ref, cb_ref, xc_ref):
    row = lax.broadcasted_iota(jnp.int32, (TL, D), 0)
    row8 = lax.broadcasted_iota(jnp.int32, (8, D), 0)
    x = x_ref[...].astype(f32)
    top8 = x[:8]
    prev8 = jnp.where(has_prev, xprev_ref[...].astype(f32)[PREV_ROWS - 8:], 0.0)
    xc = cb_ref[...] + cw_ref[pl.ds(3, 1), :] * x
    xc8 = cb_ref[...] + cw_ref[pl.ds(3, 1), :] * top8
    for sh in range(1, 4):
        w = cw_ref[pl.ds(3 - sh, 1), :]
        xs, xs8 = _shifted_rows(x, top8, prev8, sh, row, row8)
        xc = xc + w * xs
        xc8 = xc8 + w * xs8
    xc_ref[...] = xc
    xc_ref[pl.ds(0, 8), :] = xc8


def _rnn_fwd(rest, conv_w, conv_b, wa_d, wx_d, ba, bx, lam, seq):
    t = rest.shape[0]
    nb, nt = t // seq, seq // TL

    def body(x_ref, xprev_ref, gr_ref, cw_ref, cb_ref, wa_ref, wx_ref, ba_ref, bx_ref, lam_ref,
             xc_ref, a_ref, h_ref, pr_ref, xc_scr, u_scr, h_scr, carry):
        tt = pl.program_id(1)
        _conv_tile(x_ref, xprev_ref, tt > 0, cw_ref, cb_ref, xc_scr)
        xc = xc_scr[...]
        xc_ref[...] = xc.astype(bf16)
        r, i, sp, log_a, w1, sq = _rnn_gates(xc, wa_ref, wx_ref, ba_ref, bx_ref, lam_ref)
        a_ref[...] = jnp.exp(log_a)
        u_scr[...] = sq * (i * xc)

        @pl.when(tt == 0)
        def _():
            carry[...] = jnp.zeros_like(carry)

        def step(s, h):
            h = a_ref[pl.ds(s, 1), :] * h + u_scr[pl.ds(s, 1), :]
            h_scr[pl.ds(s, 1), :] = h
            return h

        carry[...] = lax.fori_loop(0, TL, step, carry[...], unroll=8)
        gr = gr_ref[...].astype(f32)
        h = h_scr[...]
        h_ref[...] = h.astype(bf16)
        pr_ref[...] = (h * (gr * _sig(gr))).astype(bf16)

    tile = lambda cb: pl.BlockSpec((TL, D), lambda b, tt, cb=cb: (b * nt + tt, cb))
    prev = lambda cb: pl.BlockSpec(
        (PREV_ROWS, D), lambda b, tt, cb=cb: (jnp.maximum((b * nt + tt) * (TL // PREV_ROWS) - 1, 0), cb))
    vec = _whole((1, D))
    return pl.pallas_call(
        body, name="rnn_fwd", grid=(nb, nt),
        in_specs=[tile(1), prev(1), tile(2), _whole((4, D)), vec, _whole((D, D)), _whole((D, D)), vec, vec, vec],
        out_specs=[tile(0)] * 4,
        out_shape=[jax.ShapeDtypeStruct((t, D), dt) for dt in (bf16, f32, bf16, bf16)],
        scratch_shapes=[pltpu.VMEM((TL, D), f32)] * 3 + [pltpu.VMEM((1, D), f32)],
        compiler_params=_params(("parallel", "arbitrary"), VMEM_LIMIT),
    )(rest, rest, rest, conv_w, conv_b, wa_d, wx_d, ba, bx, lam)


def _merge(mga, mgr, ya, yr):
    return (_sig(mga.astype(f32)) * ya.astype(f32) + _sig(mgr.astype(f32)) * yr.astype(f32)).astype(bf16)


def _out_proj_loss(rest, pa, pr, w_a, w_r, w_out, x, tgt, w_post):
    t = x.shape[0]

    def body(mga_ref, mgr_ref, pa_ref, pr_ref, wa_ref, wr_ref, wo_ref, x_ref, t_ref, w_ref,
             do_ref, dy_ref, mrg_ref, loss_ref, dwp_ref):
        @pl.when(pl.program_id(0) == 0)
        def _():
            loss_ref[...] = jnp.zeros_like(loss_ref)
            dwp_ref[...] = jnp.zeros_like(dwp_ref)

        mrg = _merge(mga_ref[...], mgr_ref[...], _dot(pa_ref[...], wa_ref[...]), _dot(pr_ref[...], wr_ref[...]))
        mrg_ref[...] = mrg
        ov = _dot(mrg, wo_ref[...])
        w = w_ref[...]
        r2 = lax.rsqrt(jnp.mean(ov * ov, axis=-1, keepdims=True) + NORM_EPS)
        oh = ov * r2
        e = x_ref[...] + oh * w - t_ref[...]
        loss_ref[...] += 0.5 * jnp.sum(jnp.mean(e * e, axis=-1, keepdims=True))
        dy = e * (1.0 / D)
        dy_ref[...] = dy
        dwp_ref[...] += jnp.sum(dy * oh, axis=0, keepdims=True)
        doh = dy * w
        do_ref[...] = (r2 * (doh - oh * jnp.mean(doh * oh, axis=-1, keepdims=True))).astype(bf16)

    return pl.pallas_call(
        body, name="out_proj_loss", grid=(t // TM,),
        in_specs=[_tile(TM, D, 3), _tile(TM, D, 4), _tile(TM, D), _tile(TM, D), _whole((D, D)), _whole((D, D)),
                  _whole((D, D)), _tile(TM, D), _tile(TM, D), _whole((1, D))],
        out_specs=[_tile(TM, D), _tile(TM, D), _tile(TM, D), _whole((8, LANES)), _whole((1, D))],
        out_shape=[jax.ShapeDtypeStruct((t, D), bf16), jax.ShapeDtypeStruct((t, D), f32),
                   jax.ShapeDtypeStruct((t, D), bf16), jax.ShapeDtypeStruct((8, LANES), f32),
                   jax.ShapeDtypeStruct((1, D), f32)],
        compiler_params=_params(("arbitrary",), VMEM_LIMIT),
    )(rest, rest, pa, pr, w_a, w_r, w_out, x, tgt, w_post)


def _out_bwd(do, rest, pa, pr, w_a, w_r, w_out):
    t = do.shape[0]

    def body(do_ref, mga_ref, mgr_ref, pa_ref, pr_ref, wa_ref, wr_ref, w_ref, dya_ref, dyr_ref, dmga_ref, dmgr_ref):
        sa, sr = _sig(mga_ref[...].astype(f32)), _sig(mgr_ref[...].astype(f32))
        ya, yr = _dot(pa_ref[...], wa_ref[...]), _dot(pr_ref[...], wr_ref[...])
        dm = _dot_nt(do_ref[...], w_ref[...])
        dya_ref[...] = (dm * sa).astype(bf16)
        dyr_ref[...] = (dm * sr).astype(bf16)
        dmga_ref[...] = (dm * ya * sa * (1.0 - sa)).astype(bf16)
        dmgr_ref[...] = (dm * yr * sr * (1.0 - sr)).astype(bf16)

    return pl.pallas_call(
        body, name="out_bwd", grid=(t // TM,),
        in_specs=[_tile(TM, D), _tile(TM, D, 3), _tile(TM, D, 4), _tile(TM, D), _tile(TM, D), _whole((D, D)),
                  _whole((D, D)), _whole((D, D))],
        out_specs=[_tile(TM, D)] * 4,
        out_shape=[jax.ShapeDtypeStruct((t, D), bf16)] * 4,
        compiler_params=_params(("parallel",), VMEM_LIMIT),
    )(do, rest, rest, pa, pr, w_a, w_r, w_out)


def _branch_bwd(name, dyb, rest, gate_cb, act, w, act_grad_dtype, head_sums=False):
    t = dyb.shape[0]

    def body(dy_ref, g_ref, act_ref, w_ref, dact_ref, dg_ref, *delta_ref):
        dp = _dot_nt(dy_ref[...], w_ref[...])
        g = g_ref[...].astype(f32)
        sg = _sig(g)
        act = act_ref[...].astype(f32)
        dact = (dp * (g * sg)).astype(act_grad_dtype)
        dact_ref[...] = dact
        dg_ref[...] = (dp * act * (sg * (1.0 + g * (1.0 - sg)))).astype(bf16)
        if head_sums:
            ch = lax.broadcasted_iota(jnp.int32, (D, LANES), 0)
            hd = lax.broadcasted_iota(jnp.int32, (D, LANES), 1)
            pick = (ch // 64 == hd).astype(bf16)
            per_head = sum(_dot(piece, pick) for piece in _split3(dact.astype(f32) * act))
            for s in range(TM // TQ):
                delta_ref[0][s] = per_head[s * TQ:(s + 1) * TQ].T[:HEADS, :]

    out_specs = [_tile(TM, D), _tile(TM, D)]
    out_shape = [jax.ShapeDtypeStruct((t, D), act_grad_dtype), jax.ShapeDtypeStruct((t, D), bf16)]
    if head_sums:
        out_specs.append(pl.BlockSpec((TM // TQ, HEADS, TQ), lambda i: (i, 0, 0)))
        out_shape.append(jax.ShapeDtypeStruct((t // TQ, HEADS, TQ), f32))
    return pl.pallas_call(
        body, name=name, grid=(t // TM,),
        in_specs=[_tile(TM, D), _tile(TM, D, gate_cb), _tile(TM, D), _whole((D, D))],
        out_specs=out_specs, out_shape=out_shape,
        compiler_params=_params(("parallel",), VMEM_LIMIT),
    )(dyb, rest, act, w)


def _rnn_bwd(dh, a, h, xc, rest, conv_w, conv_b, wa_d, wx_d, ba, bx, lam, seq):
    t = dh.shape[0]
    nb, nt = t // seq, seq // TL
    diag = (D // LANES, LANES, LANES)

    def body(dh_ref, a_ref, h_ref, hprev_ref, xc_ref, x_ref, xprev_ref, cw_ref, cb_ref, wa_ref, wx_ref,
             ba_ref, bx_ref, lam_ref, dxr_ref, dwa_ref, dwx_ref, vec_ref, g_scr, dxc_scr, dxr_scr, qcarry, dxc_next):
        b, tt = pl.program_id(0), pl.program_id(1)
        rt = nt - 1 - tt

        @pl.when((b == 0) & (tt == 0))
        def _():
            dwa_ref[...] = jnp.zeros_like(dwa_ref)
            dwx_ref[...] = jnp.zeros_like(dwx_ref)
            vec_ref[...] = jnp.zeros_like(vec_ref)

        @pl.when(tt == 0)
        def _():
            qcarry[...] = jnp.zeros_like(qcarry)
            dxc_next[...] = jnp.zeros_like(dxc_next)

        g_scr[...] = dh_ref[...].astype(f32)

        def step(k, q):
            s = TL - 1 - k
            g = g_scr[pl.ds(s, 1), :] + q
            g_scr[pl.ds(s, 1), :] = g
            return a_ref[pl.ds(s, 1), :] * g

        qcarry[...] = lax.fori_loop(0, TL, step, qcarry[...], unroll=8)

        row = lax.broadcasted_iota(jnp.int32, (TL, D), 0)
        row8 = lax.broadcasted_iota(jnp.int32, (8, D), 0)
        g = g_scr[...]
        av = a_ref[...]
        xc = xc_ref[...].astype(f32)
        hlast = jnp.where(rt > 0, hprev_ref[...].astype(f32)[PREV_ROWS - 1:], 0.0)
        hp = jnp.where(row == 0, hlast, pltpu.roll(h_ref[...].astype(f32), 1, 0))
        r, i, sp, log_a, w1, sq = _rnn_gates(xc, wa_ref, wx_ref, ba_ref, bx_ref, lam_ref)
        dix = g * sq
        di = dix * xc
        dxc = dix * i
        dsq = g * (i * xc)
        dlog_a = g * hp * av - dsq * jnp.where(sq > 0.0, (1.0 - w1) / sq, 0.0)
        dpr = (dlog_a * ((-RG_C) * sp)) * r * (1.0 - r)
        dpi = di * i * (1.0 - i)
        dprb, dpib, xcb = dpr.astype(bf16), dpi.astype(bf16), xc.astype(bf16)
        dxc = dxc + _dot_nt(dprb, wa_ref[...]) + _dot_nt(dpib, wx_ref[...])
        for j in range(D // LANES):
            cols = slice(j * LANES, (j + 1) * LANES)
            dwa_ref[j] += _dot_tn(xcb[:, cols], dprb[:, cols])
            dwx_ref[j] += _dot_tn(xcb[:, cols], dpib[:, cols])
        vec_ref[pl.ds(0, 1), :] += jnp.sum(dpr, axis=0, keepdims=True)
        vec_ref[pl.ds(1, 1), :] += jnp.sum(dpi, axis=0, keepdims=True)
        dsp = jnp.sum(dlog_a * ((-RG_C) * r), axis=0, keepdims=True)
        vec_ref[pl.ds(2, 1), :] += dsp * (-_sig(-lam_ref[...]))
        vec_ref[pl.ds(3, 1), :] += jnp.sum(dxc, axis=0, keepdims=True)

        dxc_scr[...] = dxc
        bot8 = dxc_scr[pl.ds(TL - 8, 8), :]
        nxt8 = dxc_next[...]
        dxr = cw_ref[pl.ds(3, 1), :] * dxc
        dxr8 = cw_ref[pl.ds(3, 1), :] * bot8
        for sh in range(1, 4):
            w = cw_ref[pl.ds(3 - sh, 1), :]
            dxr = dxr + w * pltpu.roll(dxc, TL - sh, 0)
            dxr8 = dxr8 + w * jnp.where(row8 < 8 - sh, pltpu.roll(bot8, 8 - sh, 0), pltpu.roll(nxt8, 8 - sh, 0))
        dxr_scr[...] = dxr
        dxr_scr[pl.ds(TL - 8, 8), :] = dxr8
        dxr_ref[...] = dxr_scr[...].astype(bf16)
        dxc_next[...] = dxc_scr[pl.ds(0, 8), :]

        x = x_ref[...].astype(f32)
        prev8 = jnp.where(rt > 0, xprev_ref[...].astype(f32)[PREV_ROWS - 8:], 0.0)
        dxc_top8 = dxc_scr[pl.ds(0, 8), :]
        vec_ref[pl.ds(7, 1), :] += jnp.sum(dxc * x, axis=0, keepdims=True)
        for sh in range(1, 4):
            inside = jnp.sum(dxc * jnp.where(row >= sh, pltpu.roll(x, sh, 0), 0.0), axis=0, keepdims=True)
            above = jnp.sum(dxc_top8 * jnp.where(row8 < sh, pltpu.roll(prev8, sh, 0), 0.0), axis=0, keepdims=True)
            vec_ref[pl.ds(7 - sh, 1), :] += inside + above

    tile = lambda cb: pl.BlockSpec((TL, D), lambda b, tt, cb=cb: (b * nt + nt - 1 - tt, cb))
    prev = lambda cb: pl.BlockSpec(
        (PREV_ROWS, D), lambda b, tt, cb=cb: (jnp.maximum((b * nt + nt - 1 - tt) * (TL // PREV_ROWS) - 1, 0), cb))
    vec = _whole((1, D))
    return pl.pallas_call(
        body, name="rnn_bwd", grid=(nb, nt),
        in_specs=[tile(0), tile(0), tile(0), prev(0), tile(0), tile(1), prev(1),
                  _whole((4, D)), vec, _whole((D, D)), _whole((D, D)), vec, vec, vec],
        out_specs=[tile(0), _whole(diag), _whole(diag), _whole((8, D))],
        out_shape=[jax.ShapeDtypeStruct((t, D), bf16), jax.ShapeDtypeStruct(diag, f32),
                   jax.ShapeDtypeStruct(diag, f32), jax.ShapeDtypeStruct((8, D), f32)],
        scratch_shapes=[pltpu.VMEM((TL, D), f32), pltpu.VMEM((TL, D), f32), pltpu.VMEM((TL, D), f32),
                        pltpu.VMEM((1, D), f32), pltpu.VMEM((8, D), f32)],
        compiler_params=_params(("arbitrary", "arbitrary"), VMEM_LIMIT),
    )(dh, a, h, h, xc, rest, rest, conv_w, conv_b, wa_d, wx_d, ba, bx, lam)


def _attn_bwd(qa, ka, qkv, doa, lse, delta, seq):
    t = qkv.shape[0]
    nb, nq = t // seq, seq // TQ
    hg = ATT_GROUP
    ng, npair = HEADS // hg, hg // 2

    def body(qa_ref, ka_ref, q_ref, k_ref, v_ref, do_ref, lse_ref, dl_ref, dq_ref, dk_ref, dv_ref, dc_ref,
             dqt_scr, dk_scr, dv_scr, ds_scr, kht_scr):
        gi, kt = pl.program_id(1), pl.program_id(2)
        lane = lax.broadcasted_iota(jnp.int32, (1, LANES), 1)
        krow = lax.broadcasted_iota(jnp.int32, (TQ, TQ), 0)
        qcol = lax.broadcasted_iota(jnp.int32, (TQ, TQ), 1)
        lmask = [(lane // 64) == hh for hh in range(2)]
        scale = jnp.asarray(QK_SCALE, bf16)

        @pl.when(kt == 0)
        def _():
            dqt_scr[...] = jnp.zeros_like(dqt_scr)

        dk_scr[...] = jnp.zeros_like(dk_scr)
        dv_scr[...] = jnp.zeros_like(dv_scr)
        ds_scr[...] = jnp.zeros_like(ds_scr)
        for g in range(hg):
            k2 = k_ref[:, pl.ds((g // 2) * LANES, LANES)]
            kht_scr[g] = jnp.where(lmask[g % 2], k2, jnp.zeros_like(k2)).T

        def q_step(qt, masked):
            qs = pl.multiple_of(qt * TQ, TQ)
            heads = range(hg)
            do2 = [do_ref[pl.ds(qs, TQ), pl.ds(j * LANES, LANES)] for j in range(npair)]
            q2 = [q_ref[pl.ds(qs, TQ), pl.ds(j * LANES, LANES)] for j in range(npair)]
            doh = [jnp.where(lmask[g % 2], do2[g // 2], jnp.zeros_like(do2[0])) for g in heads]
            qh = [jnp.where(lmask[g % 2], q2[g // 2], jnp.zeros_like(q2[0])) * scale for g in heads]
            st = [_dot_nt(ka_ref[:, pl.ds(g * LANES, LANES)], qa_ref[pl.ds(qs, TQ), pl.ds(g * LANES, LANES)])
                  for g in heads]
            if masked:
                st = [jnp.where(krow <= qcol, s, MASK_VALUE) for s in st]
            dp = [_dot_nt(v_ref[:, pl.ds((g // 2) * LANES, LANES)], doh[g]) for g in heads]
            p = [jnp.exp(st[g] - lse_ref[qt, pl.ds(hg * gi + g, 1), :]) for g in heads]
            ds = [p[g] * (dp[g] - dl_ref[qt, pl.ds(hg * gi + g, 1), :]) for g in heads]
            pb = [x.astype(bf16) for x in p]
            dsb = [x.astype(bf16) for x in ds]
            for j in range(npair):
                a, b = 2 * j, 2 * j + 1
                dv_scr[j] += _dot(pb[a], doh[a]) + _dot(pb[b], doh[b])
                dk_scr[j] += _dot(dsb[a], qh[a]) + _dot(dsb[b], qh[b])
                dqt_scr[qt, j] += (_dot(kht_scr[a], dsb[a]) + _dot(kht_scr[b], dsb[b])) * QK_SCALE
            for g in heads:
                ds_scr[g] += ds[g][:, :LANES] + ds[g][:, LANES:]

        q_step(kt, True)

        def loop_body(qt, carry):
            q_step(qt, False)
            return carry

        lax.fori_loop(kt + 1, nq, loop_body, 0)

        dc = jnp.zeros((TQ, LANES), f32)
        for g in range(hg):
            dc = jnp.where(lane == g, -jnp.sum(ds_scr[g], axis=1, keepdims=True), dc)
        dc_ref[...] = dc
        for j in range(npair):
            dk_ref[:, pl.ds(j * LANES, LANES)] = dk_scr[j].astype(bf16)
            dv_ref[:, pl.ds(j * LANES, LANES)] = dv_scr[j].astype(bf16)

        @pl.when(kt == nq - 1)
        def _():
            for qt in range(nq):
                for j in range(npair):
                    dq_ref[pl.ds(qt * TQ, TQ), pl.ds(j * LANES, LANES)] = dqt_scr[qt, j].T.astype(bf16)

    vw = hg * 64
    seqspec = pl.BlockSpec((seq, vw), lambda b, gi, kt: (b, gi))
    kspec = lambda off: pl.BlockSpec((TQ, vw), lambda b, gi, kt: (b * nq + kt, off + gi))
    rowspec = pl.BlockSpec((nq, HEADS, TQ), lambda b, gi, kt: (b, 0, 0))
    return pl.pallas_call(
        body, name="attn_bwd", grid=(nb, ng, nq),
        in_specs=[pl.BlockSpec((seq, hg * LANES), lambda b, gi, kt: (b, gi)),
                  pl.BlockSpec((TQ, hg * LANES), lambda b, gi, kt: (b * nq + kt, gi)),
                  seqspec, kspec(ng), kspec(2 * ng), seqspec, rowspec, rowspec],
        out_specs=[seqspec, kspec(0), kspec(0), pl.BlockSpec((TQ, LANES), lambda b, gi, kt: (b * nq + kt, gi))],
        out_shape=[jax.ShapeDtypeStruct((t, D), bf16)] * 3 + [jax.ShapeDtypeStruct((t, ng * LANES), f32)],
        scratch_shapes=[pltpu.VMEM((nq, npair, LANES, TQ), f32), pltpu.VMEM((npair, TQ, LANES), f32),
                        pltpu.VMEM((npair, TQ, LANES), f32), pltpu.VMEM((hg, TQ, LANES), f32),
                        pltpu.VMEM((hg, LANES, TQ), bf16)],
        compiler_params=_params(("parallel", "parallel", "arbitrary"), VMEM_LIMIT),
    )(qa, ka, qkv, qkv, qkv, doa, lse, delta)


def _forget_bwd(dc, f128, seq):
    t = f128.shape[0]
    nb = seq // LANES
    groups = dc.shape[1] // LANES

    def body(dc_ref, f_ref, df_ref, dbf_ref):
        @pl.when(pl.program_id(0) == 0)
        def _():
            dbf_ref[...] = jnp.zeros_like(dbf_ref)

        r = lax.broadcasted_iota(jnp.int32, (LANES, LANES), 0)
        cidx = lax.broadcasted_iota(jnp.int32, (LANES, LANES), 1)
        tri = (r <= cidx).astype(f32)
        carry = jnp.zeros((1, LANES), f32)
        total = jnp.zeros((1, LANES), f32)
        for blk in reversed(range(nb)):
            dcb = dc_ref[pl.ds(blk * LANES, LANES), pl.ds(0, LANES)]
            for gi in range(1, groups):
                dcb = dcb + pltpu.roll(dc_ref[pl.ds(blk * LANES, LANES), pl.ds(gi * LANES, LANES)], gi * ATT_GROUP, 1)
            dlf = jnp.dot(tri, dcb, preferred_element_type=f32, precision=lax.Precision.HIGHEST) + carry
            df = dlf * _sig(-f_ref[pl.ds(blk * LANES, LANES), :])
            df_ref[pl.ds(blk * LANES, LANES), :] = df.astype(bf16)
            total = total + jnp.sum(df, axis=0, keepdims=True)
            carry = carry + jnp.sum(dcb, axis=0, keepdims=True)
        dbf_ref[...] += total

    return pl.pallas_call(
        body, name="forget_bwd", grid=(t // seq,),
        in_specs=[pl.BlockSpec((seq, groups * LANES), lambda b: (b, 0)), pl.BlockSpec((seq, LANES), lambda b: (b, 0))],
        out_specs=[pl.BlockSpec((seq, LANES), lambda b: (b, 0)), _whole((1, LANES))],
        out_shape=[jax.ShapeDtypeStruct((t, LANES), bf16), jax.ShapeDtypeStruct((1, LANES), f32)],
        compiler_params=_params(("arbitrary",)),
    )(dc, f128)


def _in_bwd(dz, df, x, dy, w_all, w_pre):
    t = x.shape[0]
    n_dz = len(dz)

    def body(*refs):
        dz_refs = refs[:n_dz]
        df_ref, x_ref, dy_ref, w_ref, wp_ref, gx_ref, dwp_ref = refs[n_dz:]

        @pl.when(pl.program_id(0) == 0)
        def _():
            dwp_ref[...] = jnp.zeros_like(dwp_ref)

        dh = _dot(df_ref[...], w_ref[pl.ds(n_dz * D, LANES), :])
        for p in range(n_dz):
            dh = dh + _dot(dz_refs[p][...], w_ref[pl.ds(p * D, D), :])
        xv = x_ref[...]
        r1 = lax.rsqrt(jnp.mean(xv * xv, axis=-1, keepdims=True) + NORM_EPS)
        xh = xv * r1
        dwp_ref[...] += jnp.sum(dh * xh, axis=0, keepdims=True)
        dxh = dh * wp_ref[...]
        gx_ref[...] = dy_ref[...] + r1 * (dxh - xh * jnp.mean(dxh * xh, axis=-1, keepdims=True))

    once = lambda shape: pl.BlockSpec(shape, lambda i: (0, 0), pipeline_mode=pl.Buffered(1))
    return pl.pallas_call(
        body, name="in_bwd", grid=(t // TM,),
        in_specs=[_tile(TM, D)] * n_dz + [_tile(TM, LANES), _tile(TM, D), _tile(TM, D), once(w_all.shape),
                  _whole((1, D))],
        out_specs=[_tile(TM, D), _whole((1, D))],
        out_shape=[jax.ShapeDtypeStruct((t, D), f32), jax.ShapeDtypeStruct((1, D), f32)],
        compiler_params=_params(("arbitrary",), VMEM_LIMIT),
    )(*dz, df, x, dy, w_all, w_pre)


def _tn_mm(name, a, b, tn, out_dtype=f32, tk=2048):
    t, k = a.shape
    tk = min(tk, t)
    n = b.shape[1]
    nk = t // tk

    def body(a_ref, b_ref, o_ref, s_ref, acc_ref):
        j, kk = pl.program_id(0), pl.program_id(1)

        @pl.when(kk == 0)
        def _():
            acc_ref[...] = jnp.zeros_like(acc_ref)

        @pl.when((j == 0) & (kk == 0))
        def _():
            s_ref[...] = jnp.zeros_like(s_ref)

        av = a_ref[...]
        acc_ref[...] += _dot_tn(av, b_ref[...])

        @pl.when(j == 0)
        def _():
            s_ref[...] += jnp.sum(av.astype(f32), axis=0, keepdims=True)

        @pl.when(kk == nk - 1)
        def _():
            o_ref[...] = acc_ref[...].astype(out_dtype)

    return pl.pallas_call(
        body, name=name, grid=(n // tn, nk),
        in_specs=[pl.BlockSpec((tk, k), lambda j, kk: (kk, 0)), pl.BlockSpec((tk, tn), lambda j, kk: (kk, j))],
        out_specs=[pl.BlockSpec((k, tn), lambda j, kk: (0, j)), _whole((1, k))],
        out_shape=[jax.ShapeDtypeStruct((k, n), out_dtype), jax.ShapeDtypeStruct((1, k), f32)],
        scratch_shapes=[pltpu.VMEM((k, tn), f32)],
        compiler_params=_params(("arbitrary", "arbitrary"), VMEM_LIMIT),
    )(a, b)


def _position():
    return lax.axis_index("x"), lax.axis_index("y"), lax.axis_index("c")


ROW_BLOCK = 128


def _pick_rows(layout, first, count):
    acc = jnp.zeros((ROW_BLOCK, D), f32)
    seg_start = 0
    for ref, ref_row, rows in layout:
        lo, hi = max(first, seg_start), min(first + count, seg_start + rows)
        if lo < hi and ref is not None:
            off, take, done = ref_row + lo - seg_start, hi - lo, lo - first
            start = off // 16 * 16
            win = -(-(off - start + take) // 16) * 16
            r = lax.broadcasted_iota(jnp.int32, (ROW_BLOCK, win), 0)
            col = lax.broadcasted_iota(jnp.int32, (ROW_BLOCK, win), 1)
            pick = ((col - r == off - start - done) & (r >= done) & (r < done + take)).astype(bf16)
            acc = acc + _dot(pick, ref[pl.ds(start, win), :])
        seg_start += rows
    return acc


def _assemble_rows(shards_ref, shard_rows, segments, out_ref):
    layout = [(shards_ref.at[j], 0, shard_rows) for j in range(shards_ref.shape[0])]
    for out0, log0, count in segments:
        for b0 in range(0, count, ROW_BLOCK):
            block = _pick_rows(layout, log0 + b0, min(ROW_BLOCK, count - b0))
            out_ref[pl.ds(out0 + b0, ROW_BLOCK), :] = block.astype(bf16)


def _pack_pieces(blocks, shard_rows, padded):
    arrays = [a for a, _ in blocks if a is not None]
    piece_rows = padded // 2

    def body(*refs):
        out_ref = refs[-1]
        it = iter(refs[:-1])
        layout = [(None if a is None else next(it), 0, rows) for a, rows in blocks]
        for k in range(N_DEV):
            chip, half = divmod(k, 2)
            for b0 in range(0, piece_rows, ROW_BLOCK):
                n = min(ROW_BLOCK, piece_rows - b0)
                in_shard = half * piece_rows + b0
                count = max(0, min(n, shard_rows - in_shard))
                block = _pick_rows(layout, chip * shard_rows + in_shard, count)
                out_ref[k, pl.ds(b0, n), :] = block[:n].astype(bf16)

    vm = pl.BlockSpec(memory_space=pltpu.VMEM)
    return pl.pallas_call(
        body, name="pack_pieces", in_specs=[vm] * len(arrays), out_specs=vm,
        out_shape=jax.ShapeDtypeStruct((N_DEV, piece_rows, D), bf16),
        compiler_params=pltpu.CompilerParams(vmem_limit_bytes=VMEM_LIMIT),
    )(*arrays)


def _gather_shards(parts, small, shard_rows, segments, out_rows):
    n = len(parts)
    halves = [p.shape[0] // 2 for p in parts]
    cuts = [-(-h // 32) * 16 for h in halves]
    n_direct, n_relay, n_sib = 4 * n, 2 * n, 6 * n

    def body(*refs):
        srcs, small_src = refs[:n], refs[n]
        dsts, small_dst, whole_ref = refs[n + 1:2 * n + 1], refs[2 * n + 1], refs[2 * n + 2]
        send, recv, local = refs[2 * n + 3:]
        x, y, c = _position()
        me = 2 * x + y
        chips = [(1 - x, y), (x, 1 - y), (1 - x, 1 - y)]
        ids = [2 * px + py for px, py in chips]

        def rows(a, half, quarter):
            start = half * halves[a] + (cuts[a] if quarter else 0)
            return pl.ds(start, halves[a] - cuts[a] if quarter else cuts[a])

        def landing(a, shard, half, quarter):
            return dsts[a].at[shard, rows(a, half, quarter), :]

        def direct(a, nb, quarter, shard):
            k = (a * 2 + nb) * 2 + quarter
            px, py = chips[nb]
            return pltpu.make_async_remote_copy(
                src_ref=srcs[a].at[rows(a, c, quarter), :], dst_ref=landing(a, shard, c, quarter),
                send_sem=send.at[k], recv_sem=recv.at[k], device_id=(px, py, c), device_id_type=MESH)

        def relay(a, quarter, shard):
            k = n_direct + a * 2 + quarter
            px, py = chips[1 - quarter]
            return pltpu.make_async_remote_copy(
                src_ref=landing(a, shard, c, quarter), dst_ref=landing(a, shard, c, quarter),
                send_sem=send.at[k], recv_sem=recv.at[k], device_id=(px, py, c), device_id_type=MESH)

        def to_sibling(a, origin, quarter, half):
            k = n_direct + n_relay + (a * 3 + origin) * 2 + quarter
            return pltpu.make_async_remote_copy(
                src_ref=landing(a, ids[origin], half, quarter), dst_ref=landing(a, ids[origin], half, quarter),
                send_sem=send.at[k], recv_sem=recv.at[k], device_id=(x, y, 1 - c), device_id_type=MESH)

        def small_copy(j, shard):
            k = n_direct + n_relay + n_sib + j
            px, py = chips[j]
            return pltpu.make_async_remote_copy(
                src_ref=small_src, dst_ref=small_dst.at[shard], send_sem=send.at[k], recv_sem=recv.at[k],
                device_id=(px, py, c), device_id_type=MESH)

        own = [pltpu.make_async_copy(srcs[a], dsts[a].at[me], local.at[a]) for a in range(n)]
        own.append(pltpu.make_async_copy(small_src, small_dst.at[me], local.at[n]))
        for cp in own:
            cp.start()
        sent = [direct(a, nb, q, me) for q in range(2) for a in range(n) for nb in range(2)]
        sent += [small_copy(j, me) for j in range(3)]
        for cp in sent:
            cp.start()

        def passed_on(cp):
            cp.start()
            sent.append(cp)

        for q in range(2):
            for a in range(n):
                for nb in range(2):
                    direct(a, nb, q, ids[nb]).wait_recv()
                    passed_on(to_sibling(a, nb, q, c))
                    if nb == q:
                        passed_on(relay(a, q, ids[nb]))
        for a in range(n):
            for q in range(2):
                relay(a, q, ids[2]).wait_recv()
                passed_on(to_sibling(a, 2, q, c))
        for j in range(3):
            small_copy(j, ids[j]).wait_recv()
            for a in range(n):
                for q in range(2):
                    to_sibling(a, j, q, 1 - c).wait_recv()
        for cp in sent:
            cp.wait_send()
        for cp in own:
            cp.wait()
        _assemble_rows(dsts[0], shard_rows, segments, whole_ref)

    vm = pl.BlockSpec(memory_space=pltpu.VMEM)
    n_sems = n_direct + n_relay + n_sib + 3
    out = pl.pallas_call(
        body, name="gather_shards",
        in_specs=[vm] * (n + 1), out_specs=[vm] * (n + 2),
        out_shape=[jax.ShapeDtypeStruct((N_CHIPS,) + p.shape, p.dtype) for p in parts + [small]]
        + [jax.ShapeDtypeStruct((out_rows, parts[0].shape[1]), parts[0].dtype)],
        scratch_shapes=[pltpu.SemaphoreType.DMA((n_sems,)), pltpu.SemaphoreType.DMA((n_sems,)),
                        pltpu.SemaphoreType.DMA((n + 1,))],
        compiler_params=pltpu.CompilerParams(vmem_limit_bytes=VMEM_LIMIT),
    )(*parts, small)
    return out[1:]


def _allsum_rows(part):
    rows_n = part.shape[0]

    def body(x_ref, gath_ref, sum_ref, send_sems, recv_sems, local_sem):
        x, y, c = _position()
        me, sibling = (x, y, c), (x, y, 1 - c)
        chips = [(1 - x, y), (x, 1 - y), (1 - x, 1 - y)]

        def rows(px, py, pc):
            return gath_ref.at[pl.ds((4 * px + 2 * py + pc) * rows_n, rows_n), :]

        def copy(k, block, to, src=None):
            return pltpu.make_async_remote_copy(
                src_ref=rows(*block) if src is None else src, dst_ref=rows(*block),
                send_sem=send_sems.at[k], recv_sem=recv_sems.at[k], device_id=to, device_id_type=MESH)

        mine = pltpu.make_async_copy(x_ref, rows(*me), local_sem)
        mine.start()
        first = [copy(0, me, sibling, src=x_ref)]
        first += [copy(1 + j, me, (*chip, c), src=x_ref) for j, chip in enumerate(chips)]
        for cp in first:
            cp.start()
        passed = [copy(4 + j, (*chip, c), sibling) for j, chip in enumerate(chips)]
        for j, chip in enumerate(chips):
            copy(1 + j, (*chip, c), me).wait_recv()
            passed[j].start()
        copy(0, sibling, me).wait_recv()
        for j, chip in enumerate(chips):
            copy(4 + j, (*chip, 1 - c), me).wait_recv()
        for cp in first + passed:
            cp.wait_send()
        mine.wait()
        total = gath_ref[pl.ds(0, rows_n), :]
        for d in range(1, N_DEV):
            total = total + gath_ref[pl.ds(d * rows_n, rows_n), :]
        sum_ref[...] = total

    vm = pl.BlockSpec(memory_space=pltpu.VMEM)
    return pl.pallas_call(
        body, name="allsum_rows", in_specs=[vm], out_specs=[vm, vm],
        out_shape=[jax.ShapeDtypeStruct((N_DEV * rows_n, D), f32), jax.ShapeDtypeStruct((rows_n, D), f32)],
        scratch_shapes=[pltpu.SemaphoreType.DMA((7,)), pltpu.SemaphoreType.DMA((7,)), pltpu.SemaphoreType.DMA],
    )(part)[1]


PAIR_ROWS = 16


def _pair_reduce(name, pieces):
    _, r, n = pieces.shape

    def body(p_ref, o_ref, land, send, recv):
        x, y, c = _position()

        def remote(j, half):
            return pltpu.make_async_remote_copy(
                src_ref=p_ref.at[2 * j + half], dst_ref=land.at[j], send_sem=send.at[j], recv_sem=recv.at[j],
                device_id=(x, y, 1 - c), device_id_type=MESH)

        sends = [remote(j, 1 - c) for j in range(N_CHIPS)]
        for cp in sends:
            cp.start()
        for j in range(N_CHIPS):
            remote(j, c).wait_recv()

            def add_rows(i, carry, j=j):
                rows = pl.ds(pl.multiple_of(i * PAIR_ROWS, PAIR_ROWS), PAIR_ROWS)
                o_ref[j, rows, :] = (p_ref[2 * j + c, rows, :].astype(f32) + land[j, rows, :].astype(f32)).astype(bf16)
                return carry

            lax.fori_loop(0, r // PAIR_ROWS, add_rows, 0)
        for cp in sends:
            cp.wait_send()

    vm = pl.BlockSpec(memory_space=pltpu.VMEM)
    return pl.pallas_call(
        body, name=name, in_specs=[vm], out_specs=vm,
        out_shape=jax.ShapeDtypeStruct((N_CHIPS, r, n), bf16),
        scratch_shapes=[pltpu.VMEM((N_CHIPS, r, n), bf16), pltpu.SemaphoreType.DMA((N_CHIPS,)),
                        pltpu.SemaphoreType.DMA((N_CHIPS,))],
        compiler_params=pltpu.CompilerParams(vmem_limit_bytes=VMEM_LIMIT),
    )(pieces)


def _chip_exchange(arrs):
    n = len(arrs)
    heights = [a.shape[1] for a in arrs]
    cuts = [-(-r // 32) * 16 for r in heights]

    def body(*refs):
        srcs, dsts, relays = refs[:n], refs[n:2 * n], refs[2 * n:3 * n]
        send, recv, local = refs[3 * n:]
        x, y, c = _position()
        me = 2 * x + y
        chips = [(1 - x, y), (x, 1 - y), (1 - x, 1 - y)]
        ids = [2 * px + py for px, py in chips]

        def rows(a, quarter):
            return pl.ds(cuts[a], heights[a] - cuts[a]) if quarter else pl.ds(0, cuts[a])

        def held(a, quarter):
            size = heights[a] - cuts[a] if quarter else cuts[a]
            return relays[a].at[quarter, pl.ds(0, size), :]

        def direct(a, nb, piece, landing):
            px, py = chips[nb]
            return pltpu.make_async_remote_copy(
                src_ref=srcs[a].at[piece], dst_ref=dsts[a].at[landing], send_sem=send.at[a * 2 + nb],
                recv_sem=recv.at[a * 2 + nb], device_id=(px, py, c), device_id_type=MESH)

        def first_hop(a, quarter):
            k = 2 * n + a * 2 + quarter
            px, py = chips[quarter]
            return pltpu.make_async_remote_copy(
                src_ref=srcs[a].at[ids[2], rows(a, quarter), :], dst_ref=held(a, quarter), send_sem=send.at[k],
                recv_sem=recv.at[k], device_id=(px, py, c), device_id_type=MESH)

        def second_hop(a, quarter, origin):
            k = 4 * n + a * 2 + quarter
            px, py = chips[1 - quarter]
            return pltpu.make_async_remote_copy(
                src_ref=held(a, quarter), dst_ref=dsts[a].at[origin, rows(a, quarter), :], send_sem=send.at[k],
                recv_sem=recv.at[k], device_id=(px, py, c), device_id_type=MESH)

        own = [pltpu.make_async_copy(srcs[a].at[me], dsts[a].at[me], local.at[a]) for a in range(n)]
        sent = [first_hop(a, q) for a in range(n) for q in range(2)]
        sent += [direct(a, nb, ids[nb], me) for a in range(n) for nb in range(2)]
        for cp in sent + own:
            cp.start()
        for a in range(n):
            for q in range(2):
                first_hop(a, q).wait_recv()
                sent.append(second_hop(a, q, ids[q]))
                sent[-1].start()
        for a in range(n):
            for nb in range(2):
                direct(a, nb, me, ids[nb]).wait_recv()
            for q in range(2):
                second_hop(a, q, ids[2]).wait_recv()
        for cp in sent:
            cp.wait_send()
        for cp in own:
            cp.wait()

    anyspec = pl.BlockSpec(memory_space=pl.ANY)
    out = pl.pallas_call(
        body, name="chip_exchange", in_specs=[anyspec] * n, out_specs=[anyspec] * (2 * n),
        out_shape=[jax.ShapeDtypeStruct(a.shape, a.dtype) for a in arrs]
        + [jax.ShapeDtypeStruct((2, cut, a.shape[2]), a.dtype) for a, cut in zip(arrs, cuts)],
        scratch_shapes=[pltpu.SemaphoreType.DMA((6 * n,)), pltpu.SemaphoreType.DMA((6 * n,)),
                        pltpu.SemaphoreType.DMA((n,))],
    )(*arrs)
    return out[:n]


def _swap_halves(arrs):
    n = len(arrs)

    def body(*refs):
        srcs, dsts = refs[:n], refs[n:2 * n]
        send, recv, local = refs[2 * n:]
        x, y, c = _position()

        def remote(a, landing):
            return pltpu.make_async_remote_copy(
                src_ref=srcs[a], dst_ref=dsts[a].at[landing], send_sem=send.at[a], recv_sem=recv.at[a],
                device_id=(x, y, 1 - c), device_id_type=MESH)

        own = [pltpu.make_async_copy(srcs[a], dsts[a].at[c], local.at[a]) for a in range(n)]
        sends = [remote(a, c) for a in range(n)]
        for cp in sends + own:
            cp.start()
        for a in range(n):
            remote(a, 1 - c).wait_recv()
        for cp in sends:
            cp.wait_send()
        for cp in own:
            cp.wait()

    vm = pl.BlockSpec(memory_space=pltpu.VMEM)
    return pl.pallas_call(
        body, name="swap_halves", in_specs=[vm] * n, out_specs=[vm] * n,
        out_shape=[jax.ShapeDtypeStruct((2,) + a.shape, a.dtype) for a in arrs],
        scratch_shapes=[pltpu.SemaphoreType.DMA((n,)), pltpu.SemaphoreType.DMA((n,)), pltpu.SemaphoreType.DMA((n,))],
        compiler_params=pltpu.CompilerParams(vmem_limit_bytes=VMEM_LIMIT),
    )(*arrs)


def _row_block(r):
    return 128 if r % 128 == 0 else r


def _sum_slots(name, slots):
    s, r, n = slots.shape
    rb = _row_block(r)

    def body(s_ref, o_ref):
        total = s_ref[0].astype(f32)
        for d in range(1, s):
            total = total + s_ref[d].astype(f32)
        o_ref[...] = total

    return pl.pallas_call(
        body, name=name, grid=(r // rb,),
        in_specs=[pl.BlockSpec((s, rb, n), lambda i: (0, i, 0))],
        out_specs=pl.BlockSpec((rb, n), lambda i: (i, 0)),
        out_shape=jax.ShapeDtypeStruct((r, n), f32),
        compiler_params=_params(("parallel",), VMEM_LIMIT),
    )(slots)


def _adamw(name, w, g, m, v):
    r, n = w.shape
    if r % 128 == 0 or r * n <= 128 * 1024:
        rb, nb = _row_block(r), n
    else:
        rb, nb = r, LANES

    def body(w_ref, g_ref, m_ref, v_ref, d_ref, nm_ref, nv_ref):
        gv = g_ref[...]
        m2 = ADAM_B1 * m_ref[...] + (1.0 - ADAM_B1) * gv
        v2 = ADAM_B2 * v_ref[...] + (1.0 - ADAM_B2) * (gv * gv)
        m_hat = m2 / (1.0 - ADAM_B1 ** ADAM_STEP)
        v_hat = v2 / (1.0 - ADAM_B2 ** ADAM_STEP)
        d_ref[...] = (-ADAM_LR) * (m_hat / (jnp.sqrt(v_hat) + ADAM_EPS) + ADAM_WD * w_ref[...])
        nm_ref[...] = m2
        nv_ref[...] = v2

    spec = pl.BlockSpec((rb, nb), lambda i, j: (i, j))
    return pl.pallas_call(
        body, name=name, grid=(r // rb, n // nb), in_specs=[spec] * 4, out_specs=[spec] * 3,
        out_shape=[jax.ShapeDtypeStruct((r, n), f32)] * 3,
        compiler_params=_params(("parallel", "parallel"), VMEM_LIMIT),
    )(w, g, m, v)


def _local_step(x2, tgt2, seq, wt):
    nb = x2.shape[0] // seq
    h, qkv = _norm_qkv(x2, wt["pre_w"], wt["w_all"], wt["b_qkv"])
    rest = _mm("in_rest", h, wt["w_all"], (3 * D, 5 * D), wt["b_rest"], bf16, 1024, 1024)
    f128 = _mm("in_f", h, wt["w_all"], (8 * D, LANES), wt["b_f"], f32, 1024, LANES)
    c = _forget_prep(f128, seq)
    qa, ka = _attn_prep(qkv, c)
    o_att, pa, lse = _attn_fwd(qa, ka, qkv, rest, seq)
    rnn_w = (wt["conv_w"], wt["conv_b"], wt["wa_d"], wt["wx_d"], wt["ba"], wt["bx"], wt["lam"])
    xc, a, hrec, pr = _rnn_fwd(rest, *rnn_w, seq)
    do, dy, mrg, loss8, d_post = _out_proj_loss(rest, pa, pr, wt["w_a"], wt["w_r"], wt["w_o"], x2, tgt2,
                                                wt["post_w"])
    dya, dyr, dmga, dmgr = _out_bwd(do, rest, pa, pr, wt["w_a"], wt["w_r"], wt["w_o"])
    doa, dga, delta = _branch_bwd("branch_a_bwd", dya, rest, 0, o_att, wt["w_a"], bf16, head_sums=True)
    dhrec, dgr = _branch_bwd("branch_r_bwd", dyr, rest, 2, hrec, wt["w_r"], bf16)
    d_wo, _ = _tn_mm("dw_out", mrg, do, D)
    d_wa, _ = _tn_mm("dw_branch_a", pa, dya, D)
    d_wr, _ = _tn_mm("dw_branch_r", pr, dyr, D)
    dxr, d_wad, d_wxd, vec = _rnn_bwd(dhrec, a, hrec, xc, rest, *rnn_w, seq)
    dq, dk, dv, dc = _attn_bwd(qa, ka, qkv, doa, lse, delta, seq)
    df, db_f = _forget_bwd(dc, f128, seq)
    pieces = [dq, dk, dv, dga, dxr, dgr, dmga, dmgr]
    gx, d_pre = _in_bwd(pieces, df, x2, dy, wt["w_all"], wt["pre_w"])
    names = ["q", "k", "v", "ga", "xr", "gr", "mga", "mgr"]
    dws, dbs = [], []
    for nm, piece in zip(names, pieces):
        dw_p, db_p = _tn_mm("dw_in_" + nm, piece, h, D, bf16)
        dws.append((dw_p, D))
        dbs.append(db_p)
    dw_f, _ = _tn_mm("dw_in_f", df, h, D, bf16)
    shard_rows = IN_TOTAL // N_CHIPS
    w_in_pieces = _pack_pieces(dws[:3] + [(dw_f, HEADS)] + dws[3:] + [(None, IN_TOTAL - IN_USED)], shard_rows,
                               _padded_rows(shard_rows))
    d_b_in = jnp.concatenate(dbs[:3] + [db_f[:, :HEADS]] + dbs[3:] + [jnp.zeros((1, IN_TOTAL - IN_USED), f32)], axis=1)
    return dict(loss=loss8[0, 0], grad_x=gx, pre_w=d_pre, w_in_pieces=w_in_pieces, b_in=d_b_in, conv_w=vec[4:8],
                conv_b=vec[3:4],
                wa_d=d_wad, ba=vec[0:1], wx_d=d_wxd, bx=vec[1:2], lam=vec[2:3], w_a=d_wa, w_r=d_wr, w_o=d_wo,
                post_w=d_post)


def _block_diag(w):
    g, bw, _ = w.shape
    eye = jnp.eye(g, dtype=w.dtype)
    return (w[:, :, None, :] * eye[:, None, :, None]).reshape(g * bw, g * bw)


def _gate_blocks(diag):
    half = diag.shape[1] // 2
    return jnp.stack([diag[:, :half, :half], diag[:, half:, half:]], axis=1).reshape(-1, half, half)


def _padded_rows(rows):
    return -(-rows // 32) * 32


def _pad_cols(a, n):
    return jnp.pad(a, ((0, 0), (0, n - a.shape[1])))


def _pad_rows(a, n):
    return jnp.pad(a, ((0, n - a.shape[0]), (0, 0)))


def kernel(x, pre_norm_w, w_in, b_in, conv_w, conv_b, rg_wa, rg_ba, rg_wx, rg_bx, rg_lambda, w_branch_a, w_branch_r, w_out, post_norm_w, loss_target, m_pre_norm_w, m_w_in, m_b_in, m_conv_w, m_conv_b, m_rg_wa, m_rg_ba, m_rg_wx, m_rg_bx, m_rg_lambda, m_w_branch_a, m_w_branch_r, m_w_out, m_post_norm_w, v_pre_norm_w, v_w_in, v_b_in, v_conv_w, v_conv_b, v_rg_wa, v_rg_ba, v_rg_wx, v_rg_bx, v_rg_lambda, v_w_branch_a, v_w_branch_r, v_w_out, v_post_norm_w):
    nb, seq, _ = x.shape
    chip = 2 * lax.axis_index("x") + lax.axis_index("y")
    n_groups = rg_wa.shape[1]

    w_in_t = jnp.transpose(w_in[0])
    shard_cols = w_in_t.shape[0]
    padded = _padded_rows(shard_cols)
    q_end, f_end = 3 * D, 3 * D + HEADS
    segments = [(0, 0, q_end), (q_end, f_end, IN_USED - f_end), (IN_USED - HEADS, q_end, HEADS)]
    g_a, g_r, g_o, g_cw, w_all = _gather_shards(
        [_pad_rows(w_in_t.astype(bf16), padded), w_branch_a[0].astype(bf16), w_branch_r[0].astype(bf16),
         w_out[0].astype(bf16)], conv_w[0], shard_cols, segments, IN_USED - HEADS + LANES)
    wt = dict(
        pre_w=pre_norm_w, post_w=post_norm_w,
        w_all=w_all, b_qkv=b_in[:, :q_end], b_f=_pad_cols(b_in[:, q_end:f_end], LANES), b_rest=b_in[:, f_end:IN_USED],
        w_a=g_a.reshape(D, D), w_r=g_r.reshape(D, D), w_o=g_o.reshape(D, D),
        conv_w=jnp.transpose(g_cw, (1, 0, 2)).reshape(4, D), conv_b=conv_b,
        wa_d=_block_diag(rg_wa[0]).astype(bf16), wx_d=_block_diag(rg_wx[0]).astype(bf16),
        ba=rg_ba, bx=rg_bx, lam=rg_lambda)

    part = _local_step(x.reshape(nb * seq, D), loss_target.reshape(nb * seq, D), seq, wt)
    loss = lax.psum(part["loss"], ("x", "y", "c"))
    grad_x = part["grad_x"].reshape(nb, seq, D)

    small = jnp.concatenate([
        part["pre_w"], _pad_cols(part["b_in"], 10 * D).reshape(10, D), part["conv_b"],
        _gate_blocks(part["wa_d"]).reshape(-1, D), part["ba"],
        _gate_blocks(part["wx_d"]).reshape(-1, D), part["bx"], part["lam"], part["post_w"],
        part["conv_w"]], axis=0)
    n_small = small.shape[0]
    n_rep = n_small - 4
    tot = _allsum_rows(_pad_rows(small, -(-n_small // 8) * 8))
    g_rep = tot[:n_rep]
    g_conv_w = lax.dynamic_slice_in_dim(tot[n_rep:n_small], chip * (D // N_CHIPS), D // N_CHIPS, axis=1)

    def unpack(p):
        o = [0]

        def take(k):
            o[0] += k
            return p[o[0] - k:o[0]]

        pre = take(1)
        b = take(10).reshape(1, 10 * D)[:, :IN_TOTAL]
        cb = take(1)
        wa = take(64).reshape(rg_wa.shape)
        ba = take(1)
        wx = take(64).reshape(rg_wx.shape)
        bx = take(1)
        lam = take(1)
        post = take(1)
        return dict(pre_norm_w=pre, b_in=b, conv_b=cb, rg_wa=wa, rg_ba=ba, rg_wx=wx, rg_bx=bx, rg_lambda=lam,
                    post_norm_w=post)

    grads = unpack(g_rep)
    replicated = dict(
        pre_norm_w=(pre_norm_w, m_pre_norm_w, v_pre_norm_w), b_in=(b_in, m_b_in, v_b_in),
        conv_b=(conv_b, m_conv_b, v_conv_b), rg_wa=(rg_wa, m_rg_wa, v_rg_wa), rg_ba=(rg_ba, m_rg_ba, v_rg_ba),
        rg_wx=(rg_wx, m_rg_wx, v_rg_wx), rg_bx=(rg_bx, m_rg_bx, v_rg_bx),
        rg_lambda=(rg_lambda, m_rg_lambda, v_rg_lambda), post_norm_w=(post_norm_w, m_post_norm_w, v_post_norm_w))
    deltas, new_m, new_v = {}, {}, {}
    for name, (w, m, v) in replicated.items():
        as2d = lambda a: a.reshape(-1, D) if a.ndim > 2 else a
        upd = _adamw("adamw_" + name, as2d(w), as2d(grads[name]), as2d(m), as2d(v))
        deltas[name], new_m[name], new_v[name] = [a.reshape(w.shape) for a in upd]

    p_aro = jnp.concatenate([part[k].reshape(N_DEV, D // N_DEV, D) for k in ("w_a", "w_r", "w_o")], axis=1)
    s_in, s_aro = _chip_exchange([_pair_reduce("pair_w_in", part["w_in_pieces"]),
                                  _pair_reduce("pair_w_aro", p_aro.astype(bf16))])
    f_in, f_aro = _swap_halves([_sum_slots("sum_w_in", s_in), _sum_slots("sum_w_aro", s_aro)])
    g_w_in_t = f_in.reshape(padded, D)[:shard_cols]
    rows = D // N_DEV
    g_aro = [f_aro[:, i * rows:(i + 1) * rows, :].reshape(2 * rows, D) for i in range(3)]

    w_in_upd = _adamw("adamw_w_in", w_in_t, g_w_in_t, jnp.transpose(m_w_in[0]), jnp.transpose(v_w_in[0]))
    g_w_in, d_w_in, nm_w_in, nv_w_in = [jnp.transpose(a) for a in (g_w_in_t, *w_in_upd)]
    upd_a = _adamw("adamw_w_branch_a", w_branch_a[0], g_aro[0], m_w_branch_a[0], v_w_branch_a[0])
    upd_r = _adamw("adamw_w_branch_r", w_branch_r[0], g_aro[1], m_w_branch_r[0], v_w_branch_r[0])
    upd_o = _adamw("adamw_w_out", w_out[0], g_aro[2], m_w_out[0], v_w_out[0])
    d_aro, nm_aro, nv_aro = zip(upd_a, upd_r, upd_o)
    d_cw, nm_cw, nv_cw = _adamw("adamw_conv_w", conv_w[0], g_conv_w, m_conv_w[0], v_conv_w[0])

    def sharded(t_in, t_aro, t_cw):
        return dict(w_in=t_in[None], conv_w=t_cw[None], w_branch_a=t_aro[0][None], w_branch_r=t_aro[1][None],
                    w_out=t_aro[2][None])

    order = ["pre_norm_w", "w_in", "b_in", "conv_w", "conv_b", "rg_wa", "rg_ba", "rg_wx", "rg_bx", "rg_lambda",
             "w_branch_a", "w_branch_r", "w_out", "post_norm_w"]
    outs = [loss, grad_x]
    for rep, shd in ((grads, sharded(g_w_in, g_aro, g_conv_w)), (deltas, sharded(d_w_in, d_aro, d_cw)),
                     (new_m, sharded(nm_w_in, nm_aro, nm_cw)), (new_v, sharded(nv_w_in, nv_aro, nv_cw))):
        both = {**rep, **shd}
        outs.extend(both[k] for k in order)
    return tuple(outs)
```

```python
import jax
import jax.numpy as jnp
from jax import lax
from jax.experimental import pallas as pl
from jax.experimental.pallas import tpu as pltpu

f32 = jnp.float32
bf16 = jnp.bfloat16

D = 1024
HEADS = 16
HEAD_PAIRS = 8
LANES = 128
NORM_EPS = 1e-6
MASK_VALUE = -1e30
RG_C = 8.0
QK_SCALE = 0.125
TQ = 256
ATT_GROUP = 8
ATT_GROUP_FWD = 16
TL = 512
TM = 512
PREV_ROWS = 16
IN_USED = 8 * D + HEADS
IN_TOTAL = 9 * D + HEADS
N_CHIPS = 4
N_DEV = 8
ADAM_LR, ADAM_B1, ADAM_B2, ADAM_EPS, ADAM_WD, ADAM_STEP = 0.001, 0.9, 0.999, 1e-08, 0.01, 10
VMEM_LIMIT = 56 * 1024 * 1024
MESH = pl.DeviceIdType.MESH


def _dot(a, b):
    return jnp.dot(a, b, preferred_element_type=f32)


def _dot_nt(a, b):
    return lax.dot_general(a, b, (((1,), (1,)), ((), ())), preferred_element_type=f32)


def _dot_tn(a, b):
    return lax.dot_general(a, b, (((0,), (0,)), ((), ())), preferred_element_type=f32)


def _sig(x):
    return 0.5 * jnp.tanh(0.5 * x) + 0.5


def _softplus(x):
    return jnp.maximum(x, 0.0) + jnp.log(1.0 + jnp.exp(-jnp.abs(x)))


def _params(sem, vmem=None):
    return pltpu.CompilerParams(dimension_semantics=sem, vmem_limit_bytes=vmem)


def _tile(tm, width, cb=0):
    return pl.BlockSpec((tm, width), lambda i, cb=cb: (i, cb))


def _whole(shape):
    nd = len(shape)
    return pl.BlockSpec(shape, lambda *_: (0,) * nd)


def _norm_qkv(x, w_pre, w_all, b_qkv, tm=1024):
    t = x.shape[0]
    tm = min(tm, t)
    n = b_qkv.shape[1]

    def body(x_ref, wp_ref, w_ref, b_ref, h_ref, o_ref):
        @pl.when(pl.program_id(1) == 0)
        def _():
            xv = x_ref[...]
            r = lax.rsqrt(jnp.mean(xv * xv, axis=-1, keepdims=True) + NORM_EPS)
            h_ref[...] = (xv * r * wp_ref[...]).astype(bf16)

        o_ref[...] = (_dot_nt(h_ref[...], w_ref[...]) + b_ref[...]).astype(bf16)

    return pl.pallas_call(
        body, name="norm_qkv", grid=(t // tm, n // D),
        in_specs=[pl.BlockSpec((tm, D), lambda i, j: (i, 0)), _whole((1, D)), pl.BlockSpec((D, D), lambda i, j: (j, 0)),
                  pl.BlockSpec((1, D), lambda i, j: (0, j))],
        out_specs=[pl.BlockSpec((tm, D), lambda i, j: (i, 0)), pl.BlockSpec((tm, D), lambda i, j: (i, j))],
        out_shape=[jax.ShapeDtypeStruct((t, D), bf16), jax.ShapeDtypeStruct((t, n), bf16)],
        compiler_params=_params(("parallel", "arbitrary"), VMEM_LIMIT),
    )(x, w_pre, w_all, b_qkv)


def _mm(name, a, w, w_rows, bias, out_dtype, tm, tn):
    t, k = a.shape
    tm = min(tm, t)
    row0, n = w_rows
    assert row0 % tn == 0

    def body(a_ref, w_ref, b_ref, o_ref):
        o_ref[...] = (_dot_nt(a_ref[...], w_ref[...]) + b_ref[...]).astype(out_dtype)

    return pl.pallas_call(
        body, name=name, grid=(t // tm, n // tn),
        in_specs=[pl.BlockSpec((tm, k), lambda i, j: (i, 0)), pl.BlockSpec((tn, k), lambda i, j: (row0 // tn + j, 0)),
                  pl.BlockSpec((1, tn), lambda i, j: (0, j))],
        out_specs=pl.BlockSpec((tm, tn), lambda i, j: (i, j)), out_shape=jax.ShapeDtypeStruct((t, n), out_dtype),
        compiler_params=_params(("parallel", "parallel"), VMEM_LIMIT),
    )(a, w, bias)


def _forget_prep(f128, seq):
    t = f128.shape[0]
    nb = seq // LANES

    def body(f_ref, c_ref):
        r = lax.broadcasted_iota(jnp.int32, (LANES, LANES), 0)
        cidx = lax.broadcasted_iota(jnp.int32, (LANES, LANES), 1)
        tri = (r >= cidx).astype(f32)
        carry = jnp.zeros((1, LANES), f32)
        for blk in range(nb):
            fv = f_ref[pl.ds(blk * LANES, LANES), :]
            lf = -_softplus(-fv)
            c_ref[pl.ds(blk * LANES, LANES), :] = (
                jnp.dot(tri, lf, preferred_element_type=f32, precision=lax.Precision.HIGHEST) + carry)
            carry = carry + jnp.sum(lf, axis=0, keepdims=True)

    return pl.pallas_call(
        body, name="forget_prep", grid=(t // seq,),
        in_specs=[pl.BlockSpec((seq, LANES), lambda b: (b, 0))],
        out_specs=pl.BlockSpec((seq, LANES), lambda b: (b, 0)),
        out_shape=jax.ShapeDtypeStruct((t, LANES), f32),
        compiler_params=_params(("parallel",)),
    )(f128)


def _split3(cv):
    hi = cv.astype(bf16)
    r1 = cv - hi.astype(f32)
    mid = r1.astype(bf16)
    lo = (r1 - mid.astype(f32)).astype(bf16)
    return hi, mid, lo


def _attn_prep(qkv, c):
    t = qkv.shape[0]

    def body(q_ref, k_ref, c_ref, qa_ref, ka_ref):
        lane = lax.broadcasted_iota(jnp.int32, (1, LANES), 1)
        cv = c_ref[...]
        one = jnp.ones((), bf16)
        zero = jnp.zeros((), bf16)
        q_ones = jnp.where((lane >= 67) & (lane < 70), one, zero)
        k_ones = jnp.where((lane >= 64) & (lane < 67), one, zero)
        for head in range(HEADS):
            pair = pl.ds((head // 2) * LANES, LANES)
            ch = jnp.sum(jnp.where(lane == head, cv, 0.0), axis=1, keepdims=True)
            hi, mid, lo = _split3(ch)
            q2, k2 = q_ref[:, pair], k_ref[:, pair]
            if head % 2 == 1:
                q2, k2 = pltpu.roll(q2, 64, 1), pltpu.roll(k2, 64, 1)
            qa = jnp.where(lane < 64, q2 * jnp.asarray(QK_SCALE, bf16),
                           jnp.where(lane == 64, hi, jnp.where(lane == 65, mid, jnp.where(lane == 66, lo, q_ones))))
            ka = jnp.where(lane < 64, k2,
                           jnp.where(lane == 67, -hi, jnp.where(lane == 68, -mid, jnp.where(lane == 69, -lo, k_ones))))
            qa_ref[:, pl.ds(head * LANES, LANES)] = qa
            ka_ref[:, pl.ds(head * LANES, LANES)] = ka

    tm = min(TM, t)
    out = pl.BlockSpec((tm, 2 * D), lambda i: (i, 0))
    return pl.pallas_call(
        body, name="attn_prep", grid=(t // tm,),
        in_specs=[_tile(tm, D, 0), _tile(tm, D, 1), _tile(tm, LANES)],
        out_specs=[out, out],
        out_shape=[jax.ShapeDtypeStruct((t, 2 * D), bf16)] * 2,
        compiler_params=_params(("parallel",)),
    )(qkv, qkv, c)


def _attn_fwd(qa, ka, qkv, rest, seq):
    t = qkv.shape[0]
    nb, nq = t // seq, seq // TQ

    hg = ATT_GROUP_FWD
    ng = HEADS // hg

    def body(q_ref, k_ref, v_ref, ga_ref, o_ref, pa_ref, lse_ref, acc_scr):
        qi, gi = pl.program_id(1), pl.program_id(2)
        krow = lax.broadcasted_iota(jnp.int32, (TQ, TQ), 0)
        qcol = lax.broadcasted_iota(jnp.int32, (TQ, TQ), 1)
        acc_scr[...] = jnp.zeros_like(acc_scr)

        def kv_step(kt, carry, masked):
            ks = pl.multiple_of(kt * TQ, TQ)
            sts = [_dot_nt(k_ref[pl.ds(ks, TQ), pl.ds(g * LANES, LANES)], q_ref[:, pl.ds(g * LANES, LANES)])
                   for g in range(hg)]
            if masked:
                sts = [jnp.where(krow <= qcol, st, MASK_VALUE) for st in sts]
            m_new = [jnp.maximum(carry[g][0], jnp.max(sts[g], axis=0, keepdims=True)) for g in range(hg)]
            ps = [jnp.exp(sts[g] - m_new[g]) for g in range(hg)]
            alphas = [jnp.exp(carry[g][0] - m_new[g]) for g in range(hg)]
            phi = [ps[g].astype(bf16) for g in range(hg)]
            plo = [(ps[g] - phi[g].astype(f32)).astype(bf16) for g in range(hg)]
            vs = [v_ref[pl.ds(ks, TQ), pl.ds(j * LANES, LANES)] for j in range(hg // 2)]
            pvs = [_dot_tn(vs[g // 2], phi[g]) + _dot_tn(vs[g // 2], plo[g]) for g in range(hg)]
            olds = [acc_scr[g] for g in range(hg)]
            for g in range(hg):
                acc_scr[g] = alphas[g] * olds[g] + pvs[g]
            return tuple((m_new[g], alphas[g] * carry[g][1] + jnp.sum(ps[g], axis=0, keepdims=True))
                         for g in range(hg))

        init = tuple((jnp.full((1, TQ), MASK_VALUE, f32), jnp.zeros((1, TQ), f32)) for _ in range(hg))
        carry = lax.fori_loop(0, qi, lambda kt, cr: kv_step(kt, cr, False), init)
        stats = kv_step(qi, carry, True)
        drow = lax.broadcasted_iota(jnp.int32, (LANES, TQ), 0)
        for g in range(hg):
            m, l = stats[g]
            lse_ref[0, pl.ds(hg * gi + g, 1), :] = m + jnp.log(l)
        for j in range(hg // 2):
            o2 = jnp.where(drow < 64, acc_scr[2 * j] / stats[2 * j][1], acc_scr[2 * j + 1] / stats[2 * j + 1][1]).T
            o_ref[:, pl.ds(j * LANES, LANES)] = o2
            ga = ga_ref[:, pl.ds(j * LANES, LANES)].astype(f32)
            pa_ref[:, pl.ds(j * LANES, LANES)] = (o2 * (ga * _sig(ga))).astype(bf16)

    vw = hg * 64
    tile = pl.BlockSpec((TQ, vw), lambda b, qi, gi: (b * nq + qi, gi))
    return pl.pallas_call(
        body, name="attn_fwd", grid=(nb, nq, ng),
        in_specs=[pl.BlockSpec((TQ, hg * LANES), lambda b, qi, gi: (b * nq + qi, gi)),
                  pl.BlockSpec((seq, hg * LANES), lambda b, qi, gi: (b, gi)),
                  pl.BlockSpec((seq, vw), lambda b, qi, gi: (b, 2 * ng + gi)), tile],
        out_specs=[tile, tile, pl.BlockSpec((1, HEADS, TQ), lambda b, qi, gi: (b * nq + qi, 0, 0))],
        out_shape=[jax.ShapeDtypeStruct((t, D), f32), jax.ShapeDtypeStruct((t, D), bf16),
                   jax.ShapeDtypeStruct((t // TQ, HEADS, TQ), f32)],
        scratch_shapes=[pltpu.VMEM((hg, LANES, TQ), f32)],
        compiler_params=_params(("parallel", "parallel", "arbitrary"), VMEM_LIMIT),
    )(qa, ka, qkv, rest)


def _shifted_rows(x, top8, prev8, shift, row, row8):
    body = pltpu.roll(x, shift, 0)
    head = jnp.where(row8 < shift, pltpu.roll(prev8, shift, 0), pltpu.roll(top8, shift, 0))
    return body, head


def _rnn_gates(xc, wa_ref, wx_ref, ba_ref, bx_ref, lam_ref):
    xcb = xc.astype(bf16)
    r = _sig(_dot(xcb, wa_ref[...]) + ba_ref[...])
    i = _sig(_dot(xcb, wx_ref[...]) + bx_ref[...])
    sp = _softplus(-lam_ref[...])
    log_a = (-RG_C) * r * sp
    th = jnp.tanh(log_a)
    w1 = (-2.0) * th / (1.0 - th)
    sq = jnp.sqrt(jnp.maximum(w1, 0.0))
    return r, i, sp, log_a, w1, sq


def _conv_tile(x_ref, xprev_ref, has_prev, cw_ref, cb_ref, xc_ref):
    row = lax.broadcasted_iota(jnp.int32, (TL, D), 0)
    row8 = lax.broadcasted_iota(jnp.int32, (8, D), 0)
    x = x_ref[...].astype(f32)
    top8 = x[:8]
    prev8 = jnp.where(has_prev, xprev_ref[...].astype(f32)[PREV_ROWS - 8:], 0.0)
    xc = cb_ref[...] + cw_ref[pl.ds(3, 1), :] * x
    xc8 = cb_ref[...] + cw_ref[pl.ds(3, 1), :] * top8
    for sh in range(1, 4):
        w = cw_ref[pl.ds(3 - sh, 1), :]
        xs, xs8 = _shifted_rows(x, top8, prev8, sh, row, row8)
        xc = xc + w * xs
        xc8 = xc8 + w * xs8
    xc_ref[...] = xc
    xc_ref[pl.ds(0, 8), :] = xc8


def _rnn_fwd(rest, conv_w, conv_b, wa_d, wx_d, ba, bx, lam, seq):
    t = rest.shape[0]
    nb, nt = t // seq, seq // TL

    def body(x_ref, xprev_ref, gr_ref, cw_ref, cb_ref, wa_ref, wx_ref, ba_ref, bx_ref, lam_ref,
             xc_ref, a_ref, h_ref, pr_ref, xc_scr, u_scr, h_scr, carry):
        tt = pl.program_id(1)
        _conv_tile(x_ref, xprev_ref, tt > 0, cw_ref, cb_ref, xc_scr)
        xc = xc_scr[...]
        xc_ref[...] = xc.astype(bf16)
        r, i, sp, log_a, w1, sq = _rnn_gates(xc, wa_ref, wx_ref, ba_ref, bx_ref, lam_ref)
        a_ref[...] = jnp.exp(log_a)
        u_scr[...] = sq * (i * xc)

        @pl.when(tt == 0)
        def _():
            carry[...] = jnp.zeros_like(carry)

        def step(s, h):
            h = a_ref[pl.ds(s, 1), :] * h + u_scr[pl.ds(s, 1), :]
            h_scr[pl.ds(s, 1), :] = h
            return h

        carry[...] = lax.fori_loop(0, TL, step, carry[...], unroll=8)
        gr = gr_ref[...].astype(f32)
        h = h_scr[...]
        h_ref[...] = h.astype(bf16)
        pr_ref[...] = (h * (gr * _sig(gr))).astype(bf16)

    tile = lambda cb: pl.BlockSpec((TL, D), lambda b, tt, cb=cb: (b * nt + tt, cb))
    prev = lambda cb: pl.BlockSpec(
        (PREV_ROWS, D), lambda b, tt, cb=cb: (jnp.maximum((b * nt + tt) * (TL // PREV_ROWS) - 1, 0), cb))
    vec = _whole((1, D))
    return pl.pallas_call(
        body, name="rnn_fwd", grid=(nb, nt),
        in_specs=[tile(1), prev(1), tile(2), _whole((4, D)), vec, _whole((D, D)), _whole((D, D)), vec, vec, vec],
        out_specs=[tile(0)] * 4,
        out_shape=[jax.ShapeDtypeStruct((t, D), dt) for dt in (bf16, f32, bf16, bf16)],
        scratch_shapes=[pltpu.VMEM((TL, D), f32)] * 3 + [pltpu.VMEM((1, D), f32)],
        compiler_params=_params(("parallel", "arbitrary"), VMEM_LIMIT),
    )(rest, rest, rest, conv_w, conv_b, wa_d, wx_d, ba, bx, lam)


def _merge(mga, mgr, ya, yr):
    return (_sig(mga.astype(f32)) * ya.astype(f32) + _sig(mgr.astype(f32)) * yr.astype(f32)).astype(bf16)


def _out_proj_loss(rest, pa, pr, w_a, w_r, w_out, x, tgt, w_post):
    t = x.shape[0]

    def body(mga_ref, mgr_ref, pa_ref, pr_ref, wa_ref, wr_ref, wo_ref, x_ref, t_ref, w_ref,
             do_ref, dy_ref, mrg_ref, loss_ref, dwp_ref):
        @pl.when(pl.program_id(0) == 0)
        def _():
            loss_ref[...] = jnp.zeros_like(loss_ref)
            dwp_ref[...] = jnp.zeros_like(dwp_ref)

        mrg = _merge(mga_ref[...], mgr_ref[...], _dot(pa_ref[...], wa_ref[...]), _dot(pr_ref[...], wr_ref[...]))
        mrg_ref[...] = mrg
        ov = _dot(mrg, wo_ref[...])
        w = w_ref[...]
        r2 = lax.rsqrt(jnp.mean(ov * ov, axis=-1, keepdims=True) + NORM_EPS)
        oh = ov * r2
        e = x_ref[...] + oh * w - t_ref[...]
        loss_ref[...] += 0.5 * jnp.sum(jnp.mean(e * e, axis=-1, keepdims=True))
        dy = e * (1.0 / D)
        dy_ref[...] = dy
        dwp_ref[...] += jnp.sum(dy * oh, axis=0, keepdims=True)
        doh = dy * w
        do_ref[...] = (r2 * (doh - oh * jnp.mean(doh * oh, axis=-1, keepdims=True))).astype(bf16)

    return pl.pallas_call(
        body, name="out_proj_loss", grid=(t // TM,),
        in_specs=[_tile(TM, D, 3), _tile(TM, D, 4), _tile(TM, D), _tile(TM, D), _whole((D, D)), _whole((D, D)),
                  _whole((D, D)), _tile(TM, D), _tile(TM, D), _whole((1, D))],
        out_specs=[_tile(TM, D), _tile(TM, D), _tile(TM, D), _whole((8, LANES)), _whole((1, D))],
        out_shape=[jax.ShapeDtypeStruct((t, D), bf16), jax.ShapeDtypeStruct((t, D), f32),
                   jax.ShapeDtypeStruct((t, D), bf16), jax.ShapeDtypeStruct((8, LANES), f32),
                   jax.ShapeDtypeStruct((1, D), f32)],
        compiler_params=_params(("arbitrary",), VMEM_LIMIT),
    )(rest, rest, pa, pr, w_a, w_r, w_out, x, tgt, w_post)


def _out_bwd(do, rest, pa, pr, w_a, w_r, w_out):
    t = do.shape[0]

    def body(do_ref, mga_ref, mgr_ref, pa_ref, pr_ref, wa_ref, wr_ref, w_ref, dya_ref, dyr_ref, dmga_ref, dmgr_ref):
        sa, sr = _sig(mga_ref[...].astype(f32)), _sig(mgr_ref[...].astype(f32))
        ya, yr = _dot(pa_ref[...], wa_ref[...]), _dot(pr_ref[...], wr_ref[...])
        dm = _dot_nt(do_ref[...], w_ref[...])
        dya_ref[...] = (dm * sa).astype(bf16)
        dyr_ref[...] = (dm * sr).astype(bf16)
        dmga_ref[...] = (dm * ya * sa * (1.0 - sa)).astype(bf16)
        dmgr_ref[...] = (dm * yr * sr * (1.0 - sr)).astype(bf16)

    return pl.pallas_call(
        body, name="out_bwd", grid=(t // TM,),
        in_specs=[_tile(TM, D), _tile(TM, D, 3), _tile(TM, D, 4), _tile(TM, D), _tile(TM, D), _whole((D, D)),
                  _whole((D, D)), _whole((D, D))],
        out_specs=[_tile(TM, D)] * 4,
        out_shape=[jax.ShapeDtypeStruct((t, D), bf16)] * 4,
        compiler_params=_params(("parallel",), VMEM_LIMIT),
    )(do, rest, rest, pa, pr, w_a, w_r, w_out)


def _branch_bwd(name, dyb, rest, gate_cb, act, w, act_grad_dtype, head_sums=False):
    t = dyb.shape[0]

    def body(dy_ref, g_ref, act_ref, w_ref, dact_ref, dg_ref, *delta_ref):
        dp = _dot_nt(dy_ref[...], w_ref[...])
        g = g_ref[...].astype(f32)
        sg = _sig(g)
        act = act_ref[...].astype(f32)
        dact = (dp * (g * sg)).astype(act_grad_dtype)
        dact_ref[...] = dact
        dg_ref[...] = (dp * act * (sg * (1.0 + g * (1.0 - sg)))).astype(bf16)
        if head_sums:
            ch = lax.broadcasted_iota(jnp.int32, (D, LANES), 0)
            hd = lax.broadcasted_iota(jnp.int32, (D, LANES), 1)
            pick = (ch // 64 == hd).astype(bf16)
            per_head = sum(_dot(piece, pick) for piece in _split3(dact.astype(f32) * act))
            for s in range(TM // TQ):
                delta_ref[0][s] = per_head[s * TQ:(s + 1) * TQ].T[:HEADS, :]

    out_specs = [_tile(TM, D), _tile(TM, D)]
    out_shape = [jax.ShapeDtypeStruct((t, D), act_grad_dtype), jax.ShapeDtypeStruct((t, D), bf16)]
    if head_sums:
        out_specs.append(pl.BlockSpec((TM // TQ, HEADS, TQ), lambda i: (i, 0, 0)))
        out_shape.append(jax.ShapeDtypeStruct((t // TQ, HEADS, TQ), f32))
    return pl.pallas_call(
        body, name=name, grid=(t // TM,),
        in_specs=[_tile(TM, D), _tile(TM, D, gate_cb), _tile(TM, D), _whole((D, D))],
        out_specs=out_specs, out_shape=out_shape,
        compiler_params=_params(("parallel",), VMEM_LIMIT),
    )(dyb, rest, act, w)


def _rnn_bwd(dh, a, h, xc, rest, conv_w, conv_b, wa_d, wx_d, ba, bx, lam, seq):
    t = dh.shape[0]
    nb, nt = t // seq, seq // TL
    diag = (D // LANES, LANES, LANES)

    def body(dh_ref, a_ref, h_ref, hprev_ref, xc_ref, x_ref, xprev_ref, cw_ref, cb_ref, wa_ref, wx_ref,
             ba_ref, bx_ref, lam_ref, dxr_ref, dwa_ref, dwx_ref, vec_ref, g_scr, dxc_scr, dxr_scr, qcarry, dxc_next):
        b, tt = pl.program_id(0), pl.program_id(1)
        rt = nt - 1 - tt

        @pl.when((b == 0) & (tt == 0))
        def _():
            dwa_ref[...] = jnp.zeros_like(dwa_ref)
            dwx_ref[...] = jnp.zeros_like(dwx_ref)
            vec_ref[...] = jnp.zeros_like(vec_ref)

        @pl.when(tt == 0)
        def _():
            qcarry[...] = jnp.zeros_like(qcarry)
            dxc_next[...] = jnp.zeros_like(dxc_next)

        g_scr[...] = dh_ref[...].astype(f32)

        def step(k, q):
            s = TL - 1 - k
            g = g_scr[pl.ds(s, 1), :] + q
            g_scr[pl.ds(s, 1), :] = g
            return a_ref[pl.ds(s, 1), :] * g

        qcarry[...] = lax.fori_loop(0, TL, step, qcarry[...], unroll=8)

        row = lax.broadcasted_iota(jnp.int32, (TL, D), 0)
        row8 = lax.broadcasted_iota(jnp.int32, (8, D), 0)
        g = g_scr[...]
        av = a_ref[...]
        xc = xc_ref[...].astype(f32)
        hlast = jnp.where(rt > 0, hprev_ref[...].astype(f32)[PREV_ROWS - 1:], 0.0)
        hp = jnp.where(row == 0, hlast, pltpu.roll(h_ref[...].astype(f32), 1, 0))
        r, i, sp, log_a, w1, sq = _rnn_gates(xc, wa_ref, wx_ref, ba_ref, bx_ref, lam_ref)
        dix = g * sq
        di = dix * xc
        dxc = dix * i
        dsq = g * (i * xc)
        dlog_a = g * hp * av - dsq * jnp.where(sq > 0.0, (1.0 - w1) / sq, 0.0)
        dpr = (dlog_a * ((-RG_C) * sp)) * r * (1.0 - r)
        dpi = di * i * (1.0 - i)
        dprb, dpib, xcb = dpr.astype(bf16), dpi.astype(bf16), xc.astype(bf16)
        dxc = dxc + _dot_nt(dprb, wa_ref[...]) + _dot_nt(dpib, wx_ref[...])
        for j in range(D // LANES):
            cols = slice(j * LANES, (j + 1) * LANES)
            dwa_ref[j] += _dot_tn(xcb[:, cols], dprb[:, cols])
            dwx_ref[j] += _dot_tn(xcb[:, cols], dpib[:, cols])
        vec_ref[pl.ds(0, 1), :] += jnp.sum(dpr, axis=0, keepdims=True)
        vec_ref[pl.ds(1, 1), :] += jnp.sum(dpi, axis=0, keepdims=True)
        dsp = jnp.sum(dlog_a * ((-RG_C) * r), axis=0, keepdims=True)
        vec_ref[pl.ds(2, 1), :] += dsp * (-_sig(-lam_ref[...]))
        vec_ref[pl.ds(3, 1), :] += jnp.sum(dxc, axis=0, keepdims=True)

        dxc_scr[...] = dxc
        bot8 = dxc_scr[pl.ds(TL - 8, 8), :]
        nxt8 = dxc_next[...]
        dxr = cw_ref[pl.ds(3, 1), :] * dxc
        dxr8 = cw_ref[pl.ds(3, 1), :] * bot8
        for sh in range(1, 4):
            w = cw_ref[pl.ds(3 - sh, 1), :]
            dxr = dxr + w * pltpu.roll(dxc, TL - sh, 0)
            dxr8 = dxr8 + w * jnp.where(row8 < 8 - sh, pltpu.roll(bot8, 8 - sh, 0), pltpu.roll(nxt8, 8 - sh, 0))
        dxr_scr[...] = dxr
        dxr_scr[pl.ds(TL - 8, 8), :] = dxr8
        dxr_ref[...] = dxr_scr[...].astype(bf16)
        dxc_next[...] = dxc_scr[pl.ds(0, 8), :]

        x = x_ref[...].astype(f32)
        prev8 = jnp.where(rt > 0, xprev_ref[...].astype(f32)[PREV_ROWS - 8:], 0.0)
        dxc_top8 = dxc_scr[pl.ds(0, 8), :]
        vec_ref[pl.ds(7, 1), :] += jnp.sum(dxc * x, axis=0, keepdims=True)
        for sh in range(1, 4):
            inside = jnp.sum(dxc * jnp.where(row >= sh, pltpu.roll(x, sh, 0), 0.0), axis=0, keepdims=True)
            above = jnp.sum(dxc_top8 * jnp.where(row8 < sh, pltpu.roll(prev8, sh, 0), 0.0), axis=0, keepdims=True)
            vec_ref[pl.ds(7 - sh, 1), :] += inside + above

    tile = lambda cb: pl.BlockSpec((TL, D), lambda b, tt, cb=cb: (b * nt + nt - 1 - tt, cb))
    prev = lambda cb: pl.BlockSpec(
        (PREV_ROWS, D), lambda b, tt, cb=cb: (jnp.maximum((b * nt + nt - 1 - tt) * (TL // PREV_ROWS) - 1, 0), cb))
    vec = _whole((1, D))
    return pl.pallas_call(
        body, name="rnn_bwd", grid=(nb, nt),
        in_specs=[tile(0), tile(0), tile(0), prev(0), tile(0), tile(1), prev(1),
                  _whole((4, D)), vec, _whole((D, D)), _whole((D, D)), vec, vec, vec],
        out_specs=[tile(0), _whole(diag), _whole(diag), _whole((8, D))],
        out_shape=[jax.ShapeDtypeStruct((t, D), bf16), jax.ShapeDtypeStruct(diag, f32),
                   jax.ShapeDtypeStruct(diag, f32), jax.ShapeDtypeStruct((8, D), f32)],
        scratch_shapes=[pltpu.VMEM((TL, D), f32), pltpu.VMEM((TL, D), f32), pltpu.VMEM((TL, D), f32),
                        pltpu.VMEM((1, D), f32), pltpu.VMEM((8, D), f32)],
        compiler_params=_params(("arbitrary", "arbitrary"), VMEM_LIMIT),
    )(dh, a, h, h, xc, rest, rest, conv_w, conv_b, wa_d, wx_d, ba, bx, lam)


def _attn_bwd(qa, ka, qkv, doa, lse, delta, seq):
    t = qkv.shape[0]
    nb, nq = t // seq, seq // TQ
    hg = ATT_GROUP
    ng, npair = HEADS // hg, hg // 2

    def body(qa_ref, ka_ref, q_ref, k_ref, v_ref, do_ref, lse_ref, dl_ref, dq_ref, dk_ref, dv_ref, dc_ref,
             dqt_scr, dk_scr, dv_scr, ds_scr, kht_scr):
        gi, kt = pl.program_id(1), pl.program_id(2)
        lane = lax.broadcasted_iota(jnp.int32, (1, LANES), 1)
        krow = lax.broadcasted_iota(jnp.int32, (TQ, TQ), 0)
        qcol = lax.broadcasted_iota(jnp.int32, (TQ, TQ), 1)
        lmask = [(lane // 64) == hh for hh in range(2)]
        scale = jnp.asarray(QK_SCALE, bf16)

        @pl.when(kt == 0)
        def _():
            dqt_scr[...] = jnp.zeros_like(dqt_scr)

        dk_scr[...] = jnp.zeros_like(dk_scr)
        dv_scr[...] = jnp.zeros_like(dv_scr)
        ds_scr[...] = jnp.zeros_like(ds_scr)
        for g in range(hg):
            k2 = k_ref[:, pl.ds((g // 2) * LANES, LANES)]
            kht_scr[g] = jnp.where(lmask[g % 2], k2, jnp.zeros_like(k2)).T

        def q_step(qt, masked):
            qs = pl.multiple_of(qt * TQ, TQ)
            heads = range(hg)
            do2 = [do_ref[pl.ds(qs, TQ), pl.ds(j * LANES, LANES)] for j in range(npair)]
            q2 = [q_ref[pl.ds(qs, TQ), pl.ds(j * LANES, LANES)] for j in range(npair)]
            doh = [jnp.where(lmask[g % 2], do2[g // 2], jnp.zeros_like(do2[0])) for g in heads]
            qh = [jnp.where(lmask[g % 2], q2[g // 2], jnp.zeros_like(q2[0])) * scale for g in heads]
            st = [_dot_nt(ka_ref[:, pl.ds(g * LANES, LANES)], qa_ref[pl.ds(qs, TQ), pl.ds(g * LANES, LANES)])
                  for g in heads]
            if masked:
                st = [jnp.where(krow <= qcol, s, MASK_VALUE) for s in st]
            dp = [_dot_nt(v_ref[:, pl.ds((g // 2) * LANES, LANES)], doh[g]) for g in heads]
            p = [jnp.exp(st[g] - lse_ref[qt, pl.ds(hg * gi + g, 1), :]) for g in heads]
            ds = [p[g] * (dp[g] - dl_ref[qt, pl.ds(hg * gi + g, 1), :]) for g in heads]
            pb = [x.astype(bf16) for x in p]
            dsb = [x.astype(bf16) for x in ds]
            for j in range(npair):
                a, b = 2 * j, 2 * j + 1
                dv_scr[j] += _dot(pb[a], doh[a]) + _dot(pb[b], doh[b])
                dk_scr[j] += _dot(dsb[a], qh[a]) + _dot(dsb[b], qh[b])
                dqt_scr[qt, j] += (_dot(kht_scr[a], dsb[a]) + _dot(kht_scr[b], dsb[b])) * QK_SCALE
            for g in heads:
                ds_scr[g] += ds[g][:, :LANES] + ds[g][:, LANES:]

        q_step(kt, True)

        def loop_body(qt, carry):
            q_step(qt, False)
            return carry

        lax.fori_loop(kt + 1, nq, loop_body, 0)

        dc = jnp.zeros((TQ, LANES), f32)
        for g in range(hg):
            dc = jnp.where(lane == g, -jnp.sum(ds_scr[g], axis=1, keepdims=True), dc)
        dc_ref[...] = dc
        for j in range(npair):
            dk_ref[:, pl.ds(j * LANES, LANES)] = dk_scr[j].astype(bf16)
            dv_ref[:, pl.ds(j * LANES, LANES)] = dv_scr[j].astype(bf16)

        @pl.when(kt == nq - 1)
        def _():
            for qt in range(nq):
                for j in range(npair):
                    dq_ref[pl.ds(qt * TQ, TQ), pl.ds(j * LANES, LANES)] = dqt_scr[qt, j].T.astype(bf16)

    vw = hg * 64
    seqspec = pl.BlockSpec((seq, vw), lambda b, gi, kt: (b, gi))
    kspec = lambda off: pl.BlockSpec((TQ, vw), lambda b, gi, kt: (b * nq + kt, off + gi))
    rowspec = pl.BlockSpec((nq, HEADS, TQ), lambda b, gi, kt: (b, 0, 0))
    return pl.pallas_call(
        body, name="attn_bwd", grid=(nb, ng, nq),
        in_specs=[pl.BlockSpec((seq, hg * LANES), lambda b, gi, kt: (b, gi)),
                  pl.BlockSpec((TQ, hg * LANES), lambda b, gi, kt: (b * nq + kt, gi)),
                  seqspec, kspec(ng), kspec(2 * ng), seqspec, rowspec, rowspec],
        out_specs=[seqspec, kspec(0), kspec(0), pl.BlockSpec((TQ, LANES), lambda b, gi, kt: (b * nq + kt, gi))],
        out_shape=[jax.ShapeDtypeStruct((t, D), bf16)] * 3 + [jax.ShapeDtypeStruct((t, ng * LANES), f32)],
        scratch_shapes=[pltpu.VMEM((nq, npair, LANES, TQ), f32), pltpu.VMEM((npair, TQ, LANES), f32),
                        pltpu.VMEM((npair, TQ, LANES), f32), pltpu.VMEM((hg, TQ, LANES), f32),
                        pltpu.VMEM((hg, LANES, TQ), bf16)],
        compiler_params=_params(("parallel", "parallel", "arbitrary"), VMEM_LIMIT),
    )(qa, ka, qkv, qkv, qkv, doa, lse, delta)


def _forget_bwd(dc, f128, seq):
    t = f128.shape[0]
    nb = seq // LANES
    groups = dc.shape[1] // LANES

    def body(dc_ref, f_ref, df_ref, dbf_ref):
        @pl.when(pl.program_id(0) == 0)
        def _():
            dbf_ref[...] = jnp.zeros_like(dbf_ref)

        r = lax.broadcasted_iota(jnp.int32, (LANES, LANES), 0)
        cidx = lax.broadcasted_iota(jnp.int32, (LANES, LANES), 1)
        tri = (r <= cidx).astype(f32)
        carry = jnp.zeros((1, LANES), f32)
        total = jnp.zeros((1, LANES), f32)
        for blk in reversed(range(nb)):
            dcb = dc_ref[pl.ds(blk * LANES, LANES), pl.ds(0, LANES)]
            for gi in range(1, groups):
                dcb = dcb + pltpu.roll(dc_ref[pl.ds(blk * LANES, LANES), pl.ds(gi * LANES, LANES)], gi * ATT_GROUP, 1)
            dlf = jnp.dot(tri, dcb, preferred_element_type=f32, precision=lax.Precision.HIGHEST) + carry
            df = dlf * _sig(-f_ref[pl.ds(blk * LANES, LANES), :])
            df_ref[pl.ds(blk * LANES, LANES), :] = df.astype(bf16)
            total = total + jnp.sum(df, axis=0, keepdims=True)
            carry = carry + jnp.sum(dcb, axis=0, keepdims=True)
        dbf_ref[...] += total

    return pl.pallas_call(
        body, name="forget_bwd", grid=(t // seq,),
        in_specs=[pl.BlockSpec((seq, groups * LANES), lambda b: (b, 0)), pl.BlockSpec((seq, LANES), lambda b: (b, 0))],
        out_specs=[pl.BlockSpec((seq, LANES), lambda b: (b, 0)), _whole((1, LANES))],
        out_shape=[jax.ShapeDtypeStruct((t, LANES), bf16), jax.ShapeDtypeStruct((1, LANES), f32)],
        compiler_params=_params(("arbitrary",)),
    )(dc, f128)


def _in_bwd(dz, df, x, dy, w_all, w_pre):
    t = x.shape[0]
    n_dz = len(dz)

    def body(*refs):
        dz_refs = refs[:n_dz]
        df_ref, x_ref, dy_ref, w_ref, wp_ref, gx_ref, dwp_ref = refs[n_dz:]

        @pl.when(pl.program_id(0) == 0)
        def _():
            dwp_ref[...] = jnp.zeros_like(dwp_ref)

        dh = _dot(df_ref[...], w_ref[pl.ds(n_dz * D, LANES), :])
        for p in range(n_dz):
            dh = dh + _dot(dz_refs[p][...], w_ref[pl.ds(p * D, D), :])
        xv = x_ref[...]
        r1 = lax.rsqrt(jnp.mean(xv * xv, axis=-1, keepdims=True) + NORM_EPS)
        xh = xv * r1
        dwp_ref[...] += jnp.sum(dh * xh, axis=0, keepdims=True)
        dxh = dh * wp_ref[...]
        gx_ref[...] = dy_ref[...] + r1 * (dxh - xh * jnp.mean(dxh * xh, axis=-1, keepdims=True))

    once = lambda shape: pl.BlockSpec(shape, lambda i: (0, 0), pipeline_mode=pl.Buffered(1))
    return pl.pallas_call(
        body, name="in_bwd", grid=(t // TM,),
        in_specs=[_tile(TM, D)] * n_dz + [_tile(TM, LANES), _tile(TM, D), _tile(TM, D), once(w_all.shape),
                  _whole((1, D))],
        out_specs=[_tile(TM, D), _whole((1, D))],
        out_shape=[jax.ShapeDtypeStruct((t, D), f32), jax.ShapeDtypeStruct((1, D), f32)],
        compiler_params=_params(("arbitrary",), VMEM_LIMIT),
    )(*dz, df, x, dy, w_all, w_pre)


def _tn_mm(name, a, b, tn, out_dtype=f32, tk=2048):
    t, k = a.shape
    tk = min(tk, t)
    n = b.shape[1]
    nk = t // tk

    def body(a_ref, b_ref, o_ref, s_ref, acc_ref):
        j, kk = pl.program_id(0), pl.program_id(1)

        @pl.when(kk == 0)
        def _():
            acc_ref[...] = jnp.zeros_like(acc_ref)

        @pl.when((j == 0) & (kk == 0))
        def _():
            s_ref[...] = jnp.zeros_like(s_ref)

        av = a_ref[...]
        acc_ref[...] += _dot_tn(av, b_ref[...])

        @pl.when(j == 0)
        def _():
            s_ref[...] += jnp.sum(av.astype(f32), axis=0, keepdims=True)

        @pl.when(kk == nk - 1)
        def _():
            o_ref[...] = acc_ref[...].astype(out_dtype)

    return pl.pallas_call(
        body, name=name, grid=(n // tn, nk),
        in_specs=[pl.BlockSpec((tk, k), lambda j, kk: (kk, 0)), pl.BlockSpec((tk, tn), lambda j, kk: (kk, j))],
        out_specs=[pl.BlockSpec((k, tn), lambda j, kk: (0, j)), _whole((1, k))],
        out_shape=[jax.ShapeDtypeStruct((k, n), out_dtype), jax.ShapeDtypeStruct((1, k), f32)],
        scratch_shapes=[pltpu.VMEM((k, tn), f32)],
        compiler_params=_params(("arbitrary", "arbitrary"), VMEM_LIMIT),
    )(a, b)


def _position():
    return lax.axis_index("x"), lax.axis_index("y"), lax.axis_index("c")


ROW_BLOCK = 128


def _pick_rows(layout, first, count):
    acc = jnp.zeros((ROW_BLOCK, D), f32)
    seg_start = 0
    for ref, ref_row, rows in layout:
        lo, hi = max(first, seg_start), min(first + count, seg_start + rows)
        if lo < hi and ref is not None:
            off, take, done = ref_row + lo - seg_start, hi - lo, lo - first
            start = off // 16 * 16
            win = -(-(off - start + take) // 16) * 16
            r = lax.broadcasted_iota(jnp.int32, (ROW_BLOCK, win), 0)
            col = lax.broadcasted_iota(jnp.int32, (ROW_BLOCK, win), 1)
            pick = ((col - r == off - start - done) & (r >= done) & (r < done + take)).astype(bf16)
            acc = acc + _dot(pick, ref[pl.ds(start, win), :])
        seg_start += rows
    return acc


def _assemble_rows(shards_ref, shard_rows, segments, out_ref):
    layout = [(shards_ref.at[j], 0, shard_rows) for j in range(shards_ref.shape[0])]
    for out0, log0, count in segments:
        for b0 in range(0, count, ROW_BLOCK):
            block = _pick_rows(layout, log0 + b0, min(ROW_BLOCK, count - b0))
            out_ref[pl.ds(out0 + b0, ROW_BLOCK), :] = block.astype(bf16)


def _pack_pieces(blocks, shard_rows, padded):
    arrays = [a for a, _ in blocks if a is not None]
    piece_rows = padded // 2

    def body(*refs):
        out_ref = refs[-1]
        it = iter(refs[:-1])
        layout = [(None if a is None else next(it), 0, rows) for a, rows in blocks]
        for k in range(N_DEV):
            chip, half = divmod(k, 2)
            for b0 in range(0, piece_rows, ROW_BLOCK):
                n = min(ROW_BLOCK, piece_rows - b0)
                in_shard = half * piece_rows + b0
                count = max(0, min(n, shard_rows - in_shard))
                block = _pick_rows(layout, chip * shard_rows + in_shard, count)
                out_ref[k, pl.ds(b0, n), :] = block[:n].astype(bf16)

    vm = pl.BlockSpec(memory_space=pltpu.VMEM)
    return pl.pallas_call(
        body, name="pack_pieces", in_specs=[vm] * len(arrays), out_specs=vm,
        out_shape=jax.ShapeDtypeStruct((N_DEV, piece_rows, D), bf16),
        compiler_params=pltpu.CompilerParams(vmem_limit_bytes=VMEM_LIMIT),
    )(*arrays)


def _gather_shards(parts, small, shard_rows, segments, out_rows):
    n = len(parts)
    halves = [p.shape[0] // 2 for p in parts]
    cuts = [-(-h // 32) * 16 for h in halves]
    n_direct, n_relay, n_sib = 4 * n, 2 * n, 6 * n

    def body(*refs):
        srcs, small_src = refs[:n], refs[n]
        dsts, small_dst, whole_ref = refs[n + 1:2 * n + 1], refs[2 * n + 1], refs[2 * n + 2]
        send, recv, local = refs[2 * n + 3:]
        x, y, c = _position()
        me = 2 * x + y
        chips = [(1 - x, y), (x, 1 - y), (1 - x, 1 - y)]
        ids = [2 * px + py for px, py in chips]

        def rows(a, half, quarter):
            start = half * halves[a] + (cuts[a] if quarter else 0)
            return pl.ds(start, halves[a] - cuts[a] if quarter else cuts[a])

        def landing(a, shard, half, quarter):
            return dsts[a].at[shard, rows(a, half, quarter), :]

        def direct(a, nb, quarter, shard):
            k = (a * 2 + nb) * 2 + quarter
            px, py = chips[nb]
            return pltpu.make_async_remote_copy(
                src_ref=srcs[a].at[rows(a, c, quarter), :], dst_ref=landing(a, shard, c, quarter),
                send_sem=send.at[k], recv_sem=recv.at[k], device_id=(px, py, c), device_id_type=MESH)

        def relay(a, quarter, shard):
            k = n_direct + a * 2 + quarter
            px, py = chips[1 - quarter]
            return pltpu.make_async_remote_copy(
                src_ref=landing(a, shard, c, quarter), dst_ref=landing(a, shard, c, quarter),
                send_sem=send.at[k], recv_sem=recv.at[k], device_id=(px, py, c), device_id_type=MESH)

        def to_sibling(a, origin, quarter, half):
            k = n_direct + n_relay + (a * 3 + origin) * 2 + quarter
            return pltpu.make_async_remote_copy(
                src_ref=landing(a, ids[origin], half, quarter), dst_ref=landing(a, ids[origin], half, quarter),
                send_sem=send.at[k], recv_sem=recv.at[k], device_id=(x, y, 1 - c), device_id_type=MESH)

        def small_copy(j, shard):
            k = n_direct + n_relay + n_sib + j
            px, py = chips[j]
            return pltpu.make_async_remote_copy(
                src_ref=small_src, dst_ref=small_dst.at[shard], send_sem=send.at[k], recv_sem=recv.at[k],
                device_id=(px, py, c), device_id_type=MESH)

        own = [pltpu.make_async_copy(srcs[a], dsts[a].at[me], local.at[a]) for a in range(n)]
        own.append(pltpu.make_async_copy(small_src, small_dst.at[me], local.at[n]))
        for cp in own:
            cp.start()
        sent = [direct(a, nb, q, me) for q in range(2) for a in range(n) for nb in range(2)]
        sent += [small_copy(j, me) for j in range(3)]
        for cp in sent:
            cp.start()

        def passed_on(cp):
            cp.start()
            sent.append(cp)

        for q in range(2):
            for a in range(n):
                for nb in range(2):
                    direct(a, nb, q, ids[nb]).wait_recv()
                    passed_on(to_sibling(a, nb, q, c))
                    if nb == q:
                        passed_on(relay(a, q, ids[nb]))
        for a in range(n):
            for q in range(2):
                relay(a, q, ids[2]).wait_recv()
                passed_on(to_sibling(a, 2, q, c))
        for j in range(3):
            small_copy(j, ids[j]).wait_recv()
            for a in range(n):
                for q in range(2):
                    to_sibling(a, j, q, 1 - c).wait_recv()
        for cp in sent:
            cp.wait_send()
        for cp in own:
            cp.wait()
        _assemble_rows(dsts[0], shard_rows, segments, whole_ref)

    vm = pl.BlockSpec(memory_space=pltpu.VMEM)
    n_sems = n_direct + n_relay + n_sib + 3
    out = pl.pallas_call(
        body, name="gather_shards",
        in_specs=[vm] * (n + 1), out_specs=[vm] * (n + 2),
        out_shape=[jax.ShapeDtypeStruct((N_CHIPS,) + p.shape, p.dtype) for p in parts + [small]]
        + [jax.ShapeDtypeStruct((out_rows, parts[0].shape[1]), parts[0].dtype)],
        scratch_shapes=[pltpu.SemaphoreType.DMA((n_sems,)), pltpu.SemaphoreType.DMA((n_sems,)),
                        pltpu.SemaphoreType.DMA((n + 1,))],
        compiler_params=pltpu.CompilerParams(vmem_limit_bytes=VMEM_LIMIT),
    )(*parts, small)
    return out[1:]


def _allsum_rows(part):
    rows_n = part.shape[0]

    def body(x_ref, gath_ref, sum_ref, send_sems, recv_sems, local_sem):
        x, y, c = _position()
        me, sibling = (x, y, c), (x, y, 1 - c)
        chips = [(1 - x, y), (x, 1 - y), (1 - x, 1 - y)]

        def rows(px, py, pc):
            return gath_ref.at[pl.ds((4 * px + 2 * py + pc) * rows_n, rows_n), :]

        def copy(k, block, to, src=None):
            return pltpu.make_async_remote_copy(
                src_ref=rows(*block) if src is None else src, dst_ref=rows(*block),
                send_sem=send_sems.at[k], recv_sem=recv_sems.at[k], device_id=to, device_id_type=MESH)

        mine = pltpu.make_async_copy(x_ref, rows(*me), local_sem)
        mine.start()
        first = [copy(0, me, sibling, src=x_ref)]
        first += [copy(1 + j, me, (*chip, c), src=x_ref) for j, chip in enumerate(chips)]
        for cp in first:
            cp.start()
        passed = [copy(4 + j, (*chip, c), sibling) for j, chip in enumerate(chips)]
        for j, chip in enumerate(chips):
            copy(1 + j, (*chip, c), me).wait_recv()
            passed[j].start()
        copy(0, sibling, me).wait_recv()
        for j, chip in enumerate(chips):
            copy(4 + j, (*chip, 1 - c), me).wait_recv()
        for cp in first + passed:
            cp.wait_send()
        mine.wait()
        total = gath_ref[pl.ds(0, rows_n), :]
        for d in range(1, N_DEV):
            total = total + gath_ref[pl.ds(d * rows_n, rows_n), :]
        sum_ref[...] = total

    vm = pl.BlockSpec(memory_space=pltpu.VMEM)
    return pl.pallas_call(
        body, name="allsum_rows", in_specs=[vm], out_specs=[vm, vm],
        out_shape=[jax.ShapeDtypeStruct((N_DEV * rows_n, D), f32), jax.ShapeDtypeStruct((rows_n, D), f32)],
        scratch_shapes=[pltpu.SemaphoreType.DMA((7,)), pltpu.SemaphoreType.DMA((7,)), pltpu.SemaphoreType.DMA],
    )(part)[1]


PAIR_ROWS = 16


def _pair_reduce(name, pieces):
    _, r, n = pieces.shape

    def body(p_ref, o_ref, land, send, recv):
        x, y, c = _position()

        def remote(j, half):
            return pltpu.make_async_remote_copy(
                src_ref=p_ref.at[2 * j + half], dst_ref=land.at[j], send_sem=send.at[j], recv_sem=recv.at[j],
                device_id=(x, y, 1 - c), device_id_type=MESH)

        sends = [remote(j, 1 - c) for j in range(N_CHIPS)]
        for cp in sends:
            cp.start()
        for j in range(N_CHIPS):
            remote(j, c).wait_recv()

            def add_rows(i, carry, j=j):
                rows = pl.ds(pl.multiple_of(i * PAIR_ROWS, PAIR_ROWS), PAIR_ROWS)
                o_ref[j, rows, :] = (p_ref[2 * j + c, rows, :].astype(f32) + land[j, rows, :].astype(f32)).astype(bf16)
                return carry

            lax.fori_loop(0, r // PAIR_ROWS, add_rows, 0)
        for cp in sends:
            cp.wait_send()

    vm = pl.BlockSpec(memory_space=pltpu.VMEM)
    return pl.pallas_call(
        body, name=name, in_specs=[vm], out_specs=vm,
        out_shape=jax.ShapeDtypeStruct((N_CHIPS, r, n), bf16),
        scratch_shapes=[pltpu.VMEM((N_CHIPS, r, n), bf16), pltpu.SemaphoreType.DMA((N_CHIPS,)),
                        pltpu.SemaphoreType.DMA((N_CHIPS,))],
        compiler_params=pltpu.CompilerParams(vmem_limit_bytes=VMEM_LIMIT),
    )(pieces)


def _chip_exchange(arrs):
    n = len(arrs)
    heights = [a.shape[1] for a in arrs]
    cuts = [-(-r // 32) * 16 for r in heights]

    def body(*refs):
        srcs, dsts, relays = refs[:n], refs[n:2 * n], refs[2 * n:3 * n]
        send, recv, local = refs[3 * n:]
        x, y, c = _position()
        me = 2 * x + y
        chips = [(1 - x, y), (x, 1 - y), (1 - x, 1 - y)]
        ids = [2 * px + py for px, py in chips]

        def rows(a, quarter):
            return pl.ds(cuts[a], heights[a] - cuts[a]) if quarter else pl.ds(0, cuts[a])

        def held(a, quarter):
            size = heights[a] - cuts[a] if quarter else cuts[a]
            return relays[a].at[quarter, pl.ds(0, size), :]

        def direct(a, nb, piece, landing):
            px, py = chips[nb]
            return pltpu.make_async_remote_copy(
                src_ref=srcs[a].at[piece], dst_ref=dsts[a].at[landing], send_sem=send.at[a * 2 + nb],
                recv_sem=recv.at[a * 2 + nb], device_id=(px, py, c), device_id_type=MESH)

        def first_hop(a, quarter):
            k = 2 * n + a * 2 + quarter
            px, py = chips[quarter]
            return pltpu.make_async_remote_copy(
                src_ref=srcs[a].at[ids[2], rows(a, quarter), :], dst_ref=held(a, quarter), send_sem=send.at[k],
                recv_sem=recv.at[k], device_id=(px, py, c), device_id_type=MESH)

        def second_hop(a, quarter, origin):
            k = 4 * n + a * 2 + quarter
            px, py = chips[1 - quarter]
            return pltpu.make_async_remote_copy(
                src_ref=held(a, quarter), dst_ref=dsts[a].at[origin, rows(a, quarter), :], send_sem=send.at[k],
                recv_sem=recv.at[k], device_id=(px, py, c), device_id_type=MESH)

        own = [pltpu.make_async_copy(srcs[a].at[me], dsts[a].at[me], local.at[a]) for a in range(n)]
        sent = [first_hop(a, q) for a in range(n) for q in range(2)]
        sent += [direct(a, nb, ids[nb], me) for a in range(n) for nb in range(2)]
        for cp in sent + own:
            cp.start()
        for a in range(n):
            for q in range(2):
                first_hop(a, q).wait_recv()
                sent.append(second_hop(a, q, ids[q]))
                sent[-1].start()
        for a in range(n):
            for nb in range(2):
                direct(a, nb, me, ids[nb]).wait_recv()
            for q in range(2):
                second_hop(a, q, ids[2]).wait_recv()
        for cp in sent:
            cp.wait_send()
        for cp in own:
            cp.wait()

    anyspec = pl.BlockSpec(memory_space=pl.ANY)
    out = pl.pallas_call(
        body, name="chip_exchange", in_specs=[anyspec] * n, out_specs=[anyspec] * (2 * n),
        out_shape=[jax.ShapeDtypeStruct(a.shape, a.dtype) for a in arrs]
        + [jax.ShapeDtypeStruct((2, cut, a.shape[2]), a.dtype) for a, cut in zip(arrs, cuts)],
        scratch_shapes=[pltpu.SemaphoreType.DMA((6 * n,)), pltpu.SemaphoreType.DMA((6 * n,)),
                        pltpu.SemaphoreType.DMA((n,))],
    )(*arrs)
    return out[:n]


def _swap_halves(arrs):
    n = len(arrs)

    def body(*refs):
        srcs, dsts = refs[:n], refs[n:2 * n]
        send, recv, local = refs[2 * n:]
        x, y, c = _position()

        def remote(a, landing):
            return pltpu.make_async_remote_copy(
                src_ref=srcs[a], dst_ref=dsts[a].at[landing], send_sem=send.at[a], recv_sem=recv.at[a],
                device_id=(x, y, 1 - c), device_id_type=MESH)

        own = [pltpu.make_async_copy(srcs[a], dsts[a].at[c], local.at[a]) for a in range(n)]
        sends = [remote(a, c) for a in range(n)]
        for cp in sends + own:
            cp.start()
        for a in range(n):
            remote(a, 1 - c).wait_recv()
        for cp in sends:
            cp.wait_send()
        for cp in own:
            cp.wait()

    vm = pl.BlockSpec(memory_space=pltpu.VMEM)
    return pl.pallas_call(
        body, name="swap_halves", in_specs=[vm] * n, out_specs=[vm] * n,
        out_shape=[jax.ShapeDtypeStruct((2,) + a.shape, a.dtype) for a in arrs],
        scratch_shapes=[pltpu.SemaphoreType.DMA((n,)), pltpu.SemaphoreType.DMA((n,)), pltpu.SemaphoreType.DMA((n,))],
        compiler_params=pltpu.CompilerParams(vmem_limit_bytes=VMEM_LIMIT),
    )(*arrs)


def _row_block(r):
    return 128 if r % 128 == 0 else r


def _sum_slots(name, slots):
    s, r, n = slots.shape
    rb = _row_block(r)

    def body(s_ref, o_ref):
        total = s_ref[0].astype(f32)
        for d in range(1, s):
            total = total + s_ref[d].astype(f32)
        o_ref[...] = total

    return pl.pallas_call(
        body, name=name, grid=(r // rb,),
        in_specs=[pl.BlockSpec((s, rb, n), lambda i: (0, i, 0))],
        out_specs=pl.BlockSpec((rb, n), lambda i: (i, 0)),
        out_shape=jax.ShapeDtypeStruct((r, n), f32),
        compiler_params=_params(("parallel",), VMEM_LIMIT),
    )(slots)


def _adamw(name, w, g, m, v):
    r, n = w.shape
    if r % 128 == 0 or r * n <= 128 * 1024:
        rb, nb = _row_block(r), n
    else:
        rb, nb = r, LANES

    def body(w_ref, g_ref, m_ref, v_ref, d_ref, nm_ref, nv_ref):
        gv = g_ref[...]
        m2 = ADAM_B1 * m_ref[...] + (1.0 - ADAM_B1) * gv
        v2 = ADAM_B2 * v_ref[...] + (1.0 - ADAM_B2) * (gv * gv)
        m_hat = m2 / (1.0 - ADAM_B1 ** ADAM_STEP)
        v_hat = v2 / (1.0 - ADAM_B2 ** ADAM_STEP)
        d_ref[...] = (-ADAM_LR) * (m_hat / (jnp.sqrt(v_hat) + ADAM_EPS) + ADAM_WD * w_ref[...])
        nm_ref[...] = m2
        nv_ref[...] = v2

    spec = pl.BlockSpec((rb, nb), lambda i, j: (i, j))
    return pl.pallas_call(
        body, name=name, grid=(r // rb, n // nb), in_specs=[spec] * 4, out_specs=[spec] * 3,
        out_shape=[jax.ShapeDtypeStruct((r, n), f32)] * 3,
        compiler_params=_params(("parallel", "parallel"), VMEM_LIMIT),
    )(w, g, m, v)


def _local_step(x2, tgt2, seq, wt):
    nb = x2.shape[0] // seq
    h, qkv = _norm_qkv(x2, wt["pre_w"], wt["w_all"], wt["b_qkv"])
    rest = _mm("in_rest", h, wt["w_all"], (3 * D, 5 * D), wt["b_rest"], bf16, 1024, 1024)
    f128 = _mm("in_f", h, wt["w_all"], (8 * D, LANES), wt["b_f"], f32, 1024, LANES)
    c = _forget_prep(f128, seq)
    qa, ka = _attn_prep(qkv, c)
    o_att, pa, lse = _attn_fwd(qa, ka, qkv, rest, seq)
    rnn_w = (wt["conv_w"], wt["conv_b"], wt["wa_d"], wt["wx_d"], wt["ba"], wt["bx"], wt["lam"])
    xc, a, hrec, pr = _rnn_fwd(rest, *rnn_w, seq)
    do, dy, mrg, loss8, d_post = _out_proj_loss(rest, pa, pr, wt["w_a"], wt["w_r"], wt["w_o"], x2, tgt2,
                                                wt["post_w"])
    dya, dyr, dmga, dmgr = _out_bwd(do, rest, pa, pr, wt["w_a"], wt["w_r"], wt["w_o"])
    doa, dga, delta = _branch_bwd("branch_a_bwd", dya, rest, 0, o_att, wt["w_a"], bf16, head_sums=True)
    dhrec, dgr = _branch_bwd("branch_r_bwd", dyr, rest, 2, hrec, wt["w_r"], bf16)
    d_wo, _ = _tn_mm("dw_out", mrg, do, D)
    d_wa, _ = _tn_mm("dw_branch_a", pa, dya, D)
    d_wr, _ = _tn_mm("dw_branch_r", pr, dyr, D)
    dxr, d_wad, d_wxd, vec = _rnn_bwd(dhrec, a, hrec, xc, rest, *rnn_w, seq)
    dq, dk, dv, dc = _attn_bwd(qa, ka, qkv, doa, lse, delta, seq)
    df, db_f = _forget_bwd(dc, f128, seq)
    pieces = [dq, dk, dv, dga, dxr, dgr, dmga, dmgr]
    gx, d_pre = _in_bwd(pieces, df, x2, dy, wt["w_all"], wt["pre_w"])
    names = ["q", "k", "v", "ga", "xr", "gr", "mga", "mgr"]
    dws, dbs = [], []
    for nm, piece in zip(names, pieces):
        dw_p, db_p = _tn_mm("dw_in_" + nm, piece, h, D, bf16)
        dws.append((dw_p, D))
        dbs.append(db_p)
    dw_f, _ = _tn_mm("dw_in_f", df, h, D, bf16)
    shard_rows = IN_TOTAL // N_CHIPS
    w_in_pieces = _pack_pieces(dws[:3] + [(dw_f, HEADS)] + dws[3:] + [(None, IN_TOTAL - IN_USED)], shard_rows,
                               _padded_rows(shard_rows))
    d_b_in = jnp.concatenate(dbs[:3] + [db_f[:, :HEADS]] + dbs[3:] + [jnp.zeros((1, IN_TOTAL - IN_USED), f32)], axis=1)
    return dict(loss=loss8[0, 0], grad_x=gx, pre_w=d_pre, w_in_pieces=w_in_pieces, b_in=d_b_in, conv_w=vec[4:8],
                conv_b=vec[3:4],
                wa_d=d_wad, ba=vec[0:1], wx_d=d_wxd, bx=vec[1:2], lam=vec[2:3], w_a=d_wa, w_r=d_wr, w_o=d_wo,
                post_w=d_post)


def _block_diag(w):
    g, bw, _ = w.shape
    eye = jnp.eye(g, dtype=w.dtype)
    return (w[:, :, None, :] * eye[:, None, :, None]).reshape(g * bw, g * bw)


def _gate_blocks(diag):
    half = diag.shape[1] // 2
    return jnp.stack([diag[:, :half, :half], diag[:, half:, half:]], axis=1).reshape(-1, half, half)


def _padded_rows(rows):
    return -(-rows // 32) * 32


def _pad_cols(a, n):
    return jnp.pad(a, ((0, 0), (0, n - a.shape[1])))


def _pad_rows(a, n):
    return jnp.pad(a, ((0, n - a.shape[0]), (0, 0)))


def kernel(x, pre_norm_w, w_in, b_in, conv_w, conv_b, rg_wa, rg_ba, rg_wx, rg_bx, rg_lambda, w_branch_a, w_branch_r, w_out, post_norm_w, loss_target, m_pre_norm_w, m_w_in, m_b_in, m_conv_w, m_conv_b, m_rg_wa, m_rg_ba, m_rg_wx, m_rg_bx, m_rg_lambda, m_w_branch_a, m_w_branch_r, m_w_out, m_post_norm_w, v_pre_norm_w, v_w_in, v_b_in, v_conv_w, v_conv_b, v_rg_wa, v_rg_ba, v_rg_wx, v_rg_bx, v_rg_lambda, v_w_branch_a, v_w_branch_r, v_w_out, v_post_norm_w):
    nb, seq, _ = x.shape
    chip = 2 * lax.axis_index("x") + lax.axis_index("y")
    n_groups = rg_wa.shape[1]

    w_in_t = jnp.transpose(w_in[0])
    shard_cols = w_in_t.shape[0]
    padded = _padded_rows(shard_cols)
    q_end, f_end = 3 * D, 3 * D + HEADS
    segments = [(0, 0, q_end), (q_end, f_end, IN_USED - f_end), (IN_USED - HEADS, q_end, HEADS)]
    g_a, g_r, g_o, g_cw, w_all = _gather_shards(
        [_pad_rows(w_in_t.astype(bf16), padded), w_branch_a[0].astype(bf16), w_branch_r[0].astype(bf16),
         w_out[0].astype(bf16)], conv_w[0], shard_cols, segments, IN_USED - HEADS + LANES)
    wt = dict(
        pre_w=pre_norm_w, post_w=post_norm_w,
        w_all=w_all, b_qkv=b_in[:, :q_end], b_f=_pad_cols(b_in[:, q_end:f_end], LANES), b_rest=b_in[:, f_end:IN_USED],
        w_a=g_a.reshape(D, D), w_r=g_r.reshape(D, D), w_o=g_o.reshape(D, D),
        conv_w=jnp.transpose(g_cw, (1, 0, 2)).reshape(4, D), conv_b=conv_b,
        wa_d=_block_diag(rg_wa[0]).astype(bf16), wx_d=_block_diag(rg_wx[0]).astype(bf16),
        ba=rg_ba, bx=rg_bx, lam=rg_lambda)

    part = _local_step(x.reshape(nb * seq, D), loss_target.reshape(nb * seq, D), seq, wt)
    loss = lax.psum(part["loss"], ("x", "y", "c"))
    grad_x = part["grad_x"].reshape(nb, seq, D)

    small = jnp.concatenate([
        part["pre_w"], _pad_cols(part["b_in"], 10 * D).reshape(10, D), part["conv_b"],
        _gate_blocks(part["wa_d"]).reshape(-1, D), part["ba"],
        _gate_blocks(part["wx_d"]).reshape(-1, D), part["bx"], part["lam"], part["post_w"],
        part["conv_w"]], axis=0)
    n_small = small.shape[0]
    n_rep = n_small - 4
    tot = _allsum_rows(_pad_rows(small, -(-n_small // 8) * 8))
    g_rep = tot[:n_rep]
    g_conv_w = lax.dynamic_slice_in_dim(tot[n_rep:n_small], chip * (D // N_CHIPS), D // N_CHIPS, axis=1)

    def unpack(p):
        o = [0]

        def take(k):
            o[0] += k
            return p[o[0] - k:o[0]]

        pre = take(1)
        b = take(10).reshape(1, 10 * D)[:, :IN_TOTAL]
        cb = take(1)
        wa = take(64).reshape(rg_wa.shape)
        ba = take(1)
        wx = take(64).reshape(rg_wx.shape)
        bx = take(1)
        lam = take(1)
        post = take(1)
        return dict(pre_norm_w=pre, b_in=b, conv_b=cb, rg_wa=wa, rg_ba=ba, rg_wx=wx, rg_bx=bx, rg_lambda=lam,
                    post_norm_w=post)

    grads = unpack(g_rep)
    replicated = dict(
        pre_norm_w=(pre_norm_w, m_pre_norm_w, v_pre_norm_w), b_in=(b_in, m_b_in, v_b_in),
        conv_b=(conv_b, m_conv_b, v_conv_b), rg_wa=(rg_wa, m_rg_wa, v_rg_wa), rg_ba=(rg_ba, m_rg_ba, v_rg_ba),
        rg_wx=(rg_wx, m_rg_wx, v_rg_wx), rg_bx=(rg_bx, m_rg_bx, v_rg_bx),
        rg_lambda=(rg_lambda, m_rg_lambda, v_rg_lambda), post_norm_w=(post_norm_w, m_post_norm_w, v_post_norm_w))
    deltas, new_m, new_v = {}, {}, {}
    for name, (w, m, v) in replicated.items():
        as2d = lambda a: a.reshape(-1, D) if a.ndim > 2 else a
        upd = _adamw("adamw_" + name, as2d(w), as2d(grads[name]), as2d(m), as2d(v))
        deltas[name], new_m[name], new_v[name] = [a.reshape(w.shape) for a in upd]

    p_aro = jnp.concatenate([part[k].reshape(N_DEV, D // N_DEV, D) for k in ("w_a", "w_r", "w_o")], axis=1)
    s_in, s_aro = _chip_exchange([_pair_reduce("pair_w_in", part["w_in_pieces"]),
                                  _pair_reduce("pair_w_aro", p_aro.astype(bf16))])
    f_in, f_aro = _swap_halves([_sum_slots("sum_w_in", s_in), _sum_slots("sum_w_aro", s_aro)])
    g_w_in_t = f_in.reshape(padded, D)[:shard_cols]
    rows = D // N_DEV
    g_aro = [f_aro[:, i * rows:(i + 1) * rows, :].reshape(2 * rows, D) for i in range(3)]

    w_in_upd = _adamw("adamw_w_in", w_in_t, g_w_in_t, jnp.transpose(m_w_in[0]), jnp.transpose(v_w_in[0]))
    g_w_in, d_w_in, nm_w_in, nv_w_in = [jnp.transpose(a) for a in (g_w_in_t, *w_in_upd)]
    upd_a = _adamw("adamw_w_branch_a", w_branch_a[0], g_aro[0], m_w_branch_a[0], v_w_branch_a[0])
    upd_r = _adamw("adamw_w_branch_r", w_branch_r[0], g_aro[1], m_w_branch_r[0], v_w_branch_r[0])
    upd_o = _adamw("adamw_w_out", w_out[0], g_aro[2], m_w_out[0], v_w_out[0])
    d_aro, nm_aro, nv_aro = zip(upd_a, upd_r, upd_o)
    d_cw, nm_cw, nv_cw = _adamw("adamw_conv_w", conv_w[0], g_conv_w, m_conv_w[0], v_conv_w[0])

    def sharded(t_in, t_aro, t_cw):
        return dict(w_in=t_in[None], conv_w=t_cw[None], w_branch_a=t_aro[0][None], w_branch_r=t_aro[1][None],
                    w_out=t_aro[2][None])

    order = ["pre_norm_w", "w_in", "b_in", "conv_w", "conv_b", "rg_wa", "rg_ba", "rg_wx", "rg_bx", "rg_lambda",
             "w_branch_a", "w_branch_r", "w_out", "post_norm_w"]
    outs = [loss, grad_x]
    for rep, shd in ((grads, sharded(g_w_in, g_aro, g_conv_w)), (deltas, sharded(d_w_in, d_aro, d_cw)),
                     (new_m, sharded(nm_w_in, nm_aro, nm_cw)), (new_v, sharded(nv_w_in, nv_aro, nv_cw))):
        both = {**rep, **shd}
        outs.extend(both[k] for k in order)
    return tuple(outs)
```

```python
import jax
import jax.numpy as jnp
from jax import lax
from jax.experimental import pallas as pl
from jax.experimental.pallas import tpu as pltpu

f32 = jnp.float32
bf16 = jnp.bfloat16

D = 1024
HEADS = 16
HEAD_PAIRS = 8
LANES = 128
NORM_EPS = 1e-6
MASK_VALUE = -1e30
RG_C = 8.0
QK_SCALE = 0.125
TQ = 256
ATT_GROUP = 8
ATT_GROUP_FWD = 16
TL = 512
TM = 512
PREV_ROWS = 16
IN_USED = 8 * D + HEADS
IN_TOTAL = 9 * D + HEADS
N_CHIPS = 4
N_DEV = 8
ADAM_LR, ADAM_B1, ADAM_B2, ADAM_EPS, ADAM_WD, ADAM_STEP = 0.001, 0.9, 0.999, 1e-08, 0.01, 10
VMEM_LIMIT = 56 * 1024 * 1024
MESH = pl.DeviceIdType.MESH


def _dot(a, b):
    return jnp.dot(a, b, preferred_element_type=f32)


def _dot_nt(a, b):
    return lax.dot_general(a, b, (((1,), (1,)), ((), ())), preferred_element_type=f32)


def _dot_tn(a, b):
    return lax.dot_general(a, b, (((0,), (0,)), ((), ())), preferred_element_type=f32)


def _sig(x):
    return 0.5 * jnp.tanh(0.5 * x) + 0.5


def _softplus(x):
    return jnp.maximum(x, 0.0) + jnp.log(1.0 + jnp.exp(-jnp.abs(x)))


def _params(sem, vmem=None):
    return pltpu.CompilerParams(dimension_semantics=sem, vmem_limit_bytes=vmem)


def _tile(tm, width, cb=0):
    return pl.BlockSpec((tm, width), lambda i, cb=cb: (i, cb))


def _whole(shape):
    nd = len(shape)
    return pl.BlockSpec(shape, lambda *_: (0,) * nd)


def _norm_qkv(x, w_pre, w_all, b_qkv, tm=1024):
    t = x.shape[0]
    tm = min(tm, t)
    n = b_qkv.shape[1]

    def body(x_ref, wp_ref, w_ref, b_ref, h_ref, o_ref):
        @pl.when(pl.program_id(1) == 0)
        def _():
            xv = x_ref[...]
            r = lax.rsqrt(jnp.mean(xv * xv, axis=-1, keepdims=True) + NORM_EPS)
            h_ref[...] = (xv * r * wp_ref[...]).astype(bf16)

        o_ref[...] = (_dot_nt(h_ref[...], w_ref[...]) + b_ref[...]).astype(bf16)

    return pl.pallas_call(
        body, name="norm_qkv", grid=(t // tm, n // D),
        in_specs=[pl.BlockSpec((tm, D), lambda i, j: (i, 0)), _whole((1, D)), pl.BlockSpec((D, D), lambda i, j: (j, 0)),
                  pl.BlockSpec((1, D), lambda i, j: (0, j))],
        out_specs=[pl.BlockSpec((tm, D), lambda i, j: (i, 0)), pl.BlockSpec((tm, D), lambda i, j: (i, j))],
        out_shape=[jax.ShapeDtypeStruct((t, D), bf16), jax.ShapeDtypeStruct((t, n), bf16)],
        compiler_params=_params(("parallel", "arbitrary"), VMEM_LIMIT),
    )(x, w_pre, w_all, b_qkv)


def _mm(name, a, w, w_rows, bias, out_dtype, tm, tn):
    t, k = a.shape
    tm = min(tm, t)
    row0, n = w_rows
    assert row0 % tn == 0

    def body(a_ref, w_ref, b_ref, o_ref):
        o_ref[...] = (_dot_nt(a_ref[...], w_ref[...]) + b_ref[...]).astype(out_dtype)

    return pl.pallas_call(
        body, name=name, grid=(t // tm, n // tn),
        in_specs=[pl.BlockSpec((tm, k), lambda i, j: (i, 0)), pl.BlockSpec((tn, k), lambda i, j: (row0 // tn + j, 0)),
                  pl.BlockSpec((1, tn), lambda i, j: (0, j))],
        out_specs=pl.BlockSpec((tm, tn), lambda i, j: (i, j)), out_shape=jax.ShapeDtypeStruct((t, n), out_dtype),
        compiler_params=_params(("parallel", "parallel"), VMEM_LIMIT),
    )(a, w, bias)


def _forget_prep(f128, seq):
    t = f128.shape[0]
    nb = seq // LANES

    def body(f_ref, c_ref):
        r = lax.broadcasted_iota(jnp.int32, (LANES, LANES), 0)
        cidx = lax.broadcasted_iota(jnp.int32, (LANES, LANES), 1)
        tri = (r >= cidx).astype(f32)
        carry = jnp.zeros((1, LANES), f32)
        for blk in range(nb):
            fv = f_ref[pl.ds(blk * LANES, LANES), :]
            lf = -_softplus(-fv)
            c_ref[pl.ds(blk * LANES, LANES), :] = (
                jnp.dot(tri, lf, preferred_element_type=f32, precision=lax.Precision.HIGHEST) + carry)
            carry = carry + jnp.sum(lf, axis=0, keepdims=True)

    return pl.pallas_call(
        body, name="forget_prep", grid=(t // seq,),
        in_specs=[pl.BlockSpec((seq, LANES), lambda b: (b, 0))],
        out_specs=pl.BlockSpec((seq, LANES), lambda b: (b, 0)),
        out_shape=jax.ShapeDtypeStruct((t, LANES), f32),
        compiler_params=_params(("parallel",)),
    )(f128)


def _split3(cv):
    hi = cv.astype(bf16)
    r1 = cv - hi.astype(f32)
    mid = r1.astype(bf16)
    lo = (r1 - mid.astype(f32)).astype(bf16)
    return hi, mid, lo


def _attn_prep(qkv, c):
    t = qkv.shape[0]

    def body(q_ref, k_ref, c_ref, qa_ref, ka_ref):
        lane = lax.broadcasted_iota(jnp.int32, (1, LANES), 1)
        cv = c_ref[...]
        one = jnp.ones((), bf16)
        zero = jnp.zeros((), bf16)
        q_ones = jnp.where((lane >= 67) & (lane < 70), one, zero)
        k_ones = jnp.where((lane >= 64) & (lane < 67), one, zero)
        for head in range(HEADS):
            pair = pl.ds((head // 2) * LANES, LANES)
            ch = jnp.sum(jnp.where(lane == head, cv, 0.0), axis=1, keepdims=True)
            hi, mid, lo = _split3(ch)
            q2, k2 = q_ref[:, pair], k_ref[:, pair]
            if head % 2 == 1:
                q2, k2 = pltpu.roll(q2, 64, 1), pltpu.roll(k2, 64, 1)
            qa = jnp.where(lane < 64, q2 * jnp.asarray(QK_SCALE, bf16),
                           jnp.where(lane == 64, hi, jnp.where(lane == 65, mid, jnp.where(lane == 66, lo, q_ones))))
            ka = jnp.where(lane < 64, k2,
                           jnp.where(lane == 67, -hi, jnp.where(lane == 68, -mid, jnp.where(lane == 69, -lo, k_ones))))
            qa_ref[:, pl.ds(head * LANES, LANES)] = qa
            ka_ref[:, pl.ds(head * LANES, LANES)] = ka

    tm = min(TM, t)
    out = pl.BlockSpec((tm, 2 * D), lambda i: (i, 0))
    return pl.pallas_call(
        body, name="attn_prep", grid=(t // tm,),
        in_specs=[_tile(tm, D, 0), _tile(tm, D, 1), _tile(tm, LANES)],
        out_specs=[out, out],
        out_shape=[jax.ShapeDtypeStruct((t, 2 * D), bf16)] * 2,
        compiler_params=_params(("parallel",)),
    )(qkv, qkv, c)


def _attn_fwd(qa, ka, qkv, rest, seq):
    t = qkv.shape[0]
    nb, nq = t // seq, seq // TQ

    hg = ATT_GROUP_FWD
    ng = HEADS // hg

    def body(q_ref, k_ref, v_ref, ga_ref, o_ref, pa_ref, lse_ref, acc_scr):
        qi, gi = pl.program_id(1), pl.program_id(2)
        krow = lax.broadcasted_iota(jnp.int32, (TQ, TQ), 0)
        qcol = lax.broadcasted_iota(jnp.int32, (TQ, TQ), 1)
        acc_scr[...] = jnp.zeros_like(acc_scr)

        def kv_step(kt, carry, masked):
            ks = pl.multiple_of(kt * TQ, TQ)
            sts = [_dot_nt(k_ref[pl.ds(ks, TQ), pl.ds(g * LANES, LANES)], q_ref[:, pl.ds(g * LANES, LANES)])
                   for g in range(hg)]
            if masked:
                sts = [jnp.where(krow <= qcol, st, MASK_VALUE) for st in sts]
            m_new = [jnp.maximum(carry[g][0], jnp.max(sts[g], axis=0, keepdims=True)) for g in range(hg)]
            ps = [jnp.exp(sts[g] - m_new[g]) for g in range(hg)]
            alphas = [jnp.exp(carry[g][0] - m_new[g]) for g in range(hg)]
            phi = [ps[g].astype(bf16) for g in range(hg)]
            plo = [(ps[g] - phi[g].astype(f32)).astype(bf16) for g in range(hg)]
            vs = [v_ref[pl.ds(ks, TQ), pl.ds(j * LANES, LANES)] for j in range(hg // 2)]
            pvs = [_dot_tn(vs[g // 2], phi[g]) + _dot_tn(vs[g // 2], plo[g]) for g in range(hg)]
            olds = [acc_scr[g] for g in range(hg)]
            for g in range(hg):
                acc_scr[g] = alphas[g] * olds[g] + pvs[g]
            return tuple((m_new[g], alphas[g] * carry[g][1] + jnp.sum(ps[g], axis=0, keepdims=True))
                         for g in range(hg))

        init = tuple((jnp.full((1, TQ), MASK_VALUE, f32), jnp.zeros((1, TQ), f32)) for _ in range(hg))
        carry = lax.fori_loop(0, qi, lambda kt, cr: kv_step(kt, cr, False), init)
        stats = kv_step(qi, carry, True)
        drow = lax.broadcasted_iota(jnp.int32, (LANES, TQ), 0)
        for g in range(hg):
            m, l = stats[g]
            lse_ref[0, pl.ds(hg * gi + g, 1), :] = m + jnp.log(l)
        for j in range(hg // 2):
            o2 = jnp.where(drow < 64, acc_scr[2 * j] / stats[2 * j][1], acc_scr[2 * j + 1] / stats[2 * j + 1][1]).T
            o_ref[:, pl.ds(j * LANES, LANES)] = o2
            ga = ga_ref[:, pl.ds(j * LANES, LANES)].astype(f32)
            pa_ref[:, pl.ds(j * LANES, LANES)] = (o2 * (ga * _sig(ga))).astype(bf16)

    vw = hg * 64
    tile = pl.BlockSpec((TQ, vw), lambda b, qi, gi: (b * nq + qi, gi))
    return pl.pallas_call(
        body, name="attn_fwd", grid=(nb, nq, ng),
        in_specs=[pl.BlockSpec((TQ, hg * LANES), lambda b, qi, gi: (b * nq + qi, gi)),
                  pl.BlockSpec((seq, hg * LANES), lambda b, qi, gi: (b, gi)),
                  pl.BlockSpec((seq, vw), lambda b, qi, gi: (b, 2 * ng + gi)), tile],
        out_specs=[tile, tile, pl.BlockSpec((1, HEADS, TQ), lambda b, qi, gi: (b * nq + qi, 0, 0))],
        out_shape=[jax.ShapeDtypeStruct((t, D), f32), jax.ShapeDtypeStruct((t, D), bf16),
                   jax.ShapeDtypeStruct((t // TQ, HEADS, TQ), f32)],
        scratch_shapes=[pltpu.VMEM((hg, LANES, TQ), f32)],
        compiler_params=_params(("parallel", "parallel", "arbitrary"), VMEM_LIMIT),
    )(qa, ka, qkv, rest)


def _shifted_rows(x, top8, prev8, shift, row, row8):
    body = pltpu.roll(x, shift, 0)
    head = jnp.where(row8 < shift, pltpu.roll(prev8, shift, 0), pltpu.roll(top8, shift, 0))
    return body, head


def _rnn_gates(xc, wa_ref, wx_ref, ba_ref, bx_ref, lam_ref):
    xcb = xc.astype(bf16)
    r = _sig(_dot(xcb, wa_ref[...]) + ba_ref[...])
    i = _sig(_dot(xcb, wx_ref[...]) + bx_ref[...])
    sp = _softplus(-lam_ref[...])
    log_a = (-RG_C) * r * sp
    th = jnp.tanh(log_a)
    w1 = (-2.0) * th / (1.0 - th)
    sq = jnp.sqrt(jnp.maximum(w1, 0.0))
    return r, i, sp, log_a, w1, sq


def _conv_tile(x_ref, xprev_ref, has_prev, cw_ref, cb_ref, xc_ref):
    row = lax.broadcasted_iota(jnp.int32, (TL, D), 0)
    row8 = lax.broadcasted_iota(jnp.int32, (8, D), 0)
    x = x_ref[...].astype(f32)
    top8 = x[:8]
    prev8 = jnp.where(has_prev, xprev_ref[...].astype(f32)[PREV_ROWS - 8:], 0.0)
    xc = cb_ref[...] + cw_ref[pl.ds(3, 1), :] * x
    xc8 = cb_ref[...] + cw_ref[pl.ds(3, 1), :] * top8
    for sh in range(1, 4):
        w = cw_ref[pl.ds(3 - sh, 1), :]
        xs, xs8 = _shifted_rows(x, top8, prev8, sh, row, row8)
        xc = xc + w * xs
        xc8 = xc8 + w * xs8
    xc_ref[...] = xc
    xc_ref[pl.ds(0, 8), :] = xc8


def _rnn_fwd(rest, conv_w, conv_b, wa_d, wx_d, ba, bx, lam, seq):
    t = rest.shape[0]
    nb, nt = t // seq, seq // TL

    def body(x_ref, xprev_ref, gr_ref, cw_ref, cb_ref, wa_ref, wx_ref, ba_ref, bx_ref, lam_ref,
             xc_ref, a_ref, h_ref, pr_ref, xc_scr, u_scr, h_scr, carry):
        tt = pl.program_id(1)
        _conv_tile(x_ref, xprev_ref, tt > 0, cw_ref, cb_ref, xc_scr)
        xc = xc_scr[...]
        xc_ref[...] = xc.astype(bf16)
        r, i, sp, log_a, w1, sq = _rnn_gates(xc, wa_ref, wx_ref, ba_ref, bx_ref, lam_ref)
        a_ref[...] = jnp.exp(log_a)
        u_scr[...] = sq * (i * xc)

        @pl.when(tt == 0)
        def _():
            carry[...] = jnp.zeros_like(carry)

        def step(s, h):
            h = a_ref[pl.ds(s, 1), :] * h + u_scr[pl.ds(s, 1), :]
            h_scr[pl.ds(s, 1), :] = h
            return h

        carry[...] = lax.fori_loop(0, TL, step, carry[...], unroll=8)
        gr = gr_ref[...].astype(f32)
        h = h_scr[...]
        h_ref[...] = h.astype(bf16)
        pr_ref[...] = (h * (gr * _sig(gr))).astype(bf16)

    tile = lambda cb: pl.BlockSpec((TL, D), lambda b, tt, cb=cb: (b * nt + tt, cb))
    prev = lambda cb: pl.BlockSpec(
        (PREV_ROWS, D), lambda b, tt, cb=cb: (jnp.maximum((b * nt + tt) * (TL // PREV_ROWS) - 1, 0), cb))
    vec = _whole((1, D))
    return pl.pallas_call(
        body, name="rnn_fwd", grid=(nb, nt),
        in_specs=[tile(1), prev(1), tile(2), _whole((4, D)), vec, _whole((D, D)), _whole((D, D)), vec, vec, vec],
        out_specs=[tile(0)] * 4,
        out_shape=[jax.ShapeDtypeStruct((t, D), dt) for dt in (bf16, f32, bf16, bf16)],
        scratch_shapes=[pltpu.VMEM((TL, D), f32)] * 3 + [pltpu.VMEM((1, D), f32)],
        compiler_params=_params(("parallel", "arbitrary"), VMEM_LIMIT),
    )(rest, rest, rest, conv_w, conv_b, wa_d, wx_d, ba, bx, lam)


def _merge(mga, mgr, ya, yr):
    return (_sig(mga.astype(f32)) * ya.astype(f32) + _sig(mgr.astype(f32)) * yr.astype(f32)).astype(bf16)


def _out_proj_loss(rest, pa, pr, w_a, w_r, w_out, x, tgt, w_post):
    t = x.shape[0]

    def body(mga_ref, mgr_ref, pa_ref, pr_ref, wa_ref, wr_ref, wo_ref, x_ref, t_ref, w_ref,
             do_ref, dy_ref, mrg_ref, loss_ref, dwp_ref):
        @pl.when(pl.program_id(0) == 0)
        def _():
            loss_ref[...] = jnp.zeros_like(loss_ref)
            dwp_ref[...] = jnp.zeros_like(dwp_ref)

        mrg = _merge(mga_ref[...], mgr_ref[...], _dot(pa_ref[...], wa_ref[...]), _dot(pr_ref[...], wr_ref[...]))
        mrg_ref[...] = mrg
        ov = _dot(mrg, wo_ref[...])
        w = w_ref[...]
        r2 = lax.rsqrt(jnp.mean(ov * ov, axis=-1, keepdims=True) + NORM_EPS)
        oh = ov * r2
        e = x_ref[...] + oh * w - t_ref[...]
        loss_ref[...] += 0.5 * jnp.sum(jnp.mean(e * e, axis=-1, keepdims=True))
        dy = e * (1.0 / D)
        dy_ref[...] = dy
        dwp_ref[...] += jnp.sum(dy * oh, axis=0, keepdims=True)
        doh = dy * w
        do_ref[...] = (r2 * (doh - oh * jnp.mean(doh * oh, axis=-1, keepdims=True))).astype(bf16)

    return pl.pallas_call(
        body, name="out_proj_loss", grid=(t // TM,),
        in_specs=[_tile(TM, D, 3), _tile(TM, D, 4), _tile(TM, D), _tile(TM, D), _whole((D, D)), _whole((D, D)),
                  _whole((D, D)), _tile(TM, D), _tile(TM, D), _whole((1, D))],
        out_specs=[_tile(TM, D), _tile(TM, D), _tile(TM, D), _whole((8, LANES)), _whole((1, D))],
        out_shape=[jax.ShapeDtypeStruct((t, D), bf16), jax.ShapeDtypeStruct((t, D), f32),
                   jax.ShapeDtypeStruct((t, D), bf16), jax.ShapeDtypeStruct((8, LANES), f32),
                   jax.ShapeDtypeStruct((1, D), f32)],
        compiler_params=_params(("arbitrary",), VMEM_LIMIT),
    )(rest, rest, pa, pr, w_a, w_r, w_out, x, tgt, w_post)


def _merge_bwd(do, rest, pa, pr, o_att, hrec, w_a, w_r, w_out):
    t = do.shape[0]

    def branch(dy, w_ref, g_ref, act):
        dp = _dot_nt(dy, w_ref[...])
        g = g_ref[...].astype(f32)
        sg = _sig(g)
        return (dp * (g * sg)).astype(bf16), (dp * act * (sg * (1.0 + g * (1.0 - sg)))).astype(bf16)

    def body(do_ref, mga_ref, mgr_ref, pa_ref, pr_ref, ga_ref, gr_ref, oa_ref, h_ref, wa_ref, wr_ref, wo_ref,
             dya_ref, dyr_ref, dmga_ref, dmgr_ref, doa_ref, dga_ref, dh_ref, dgr_ref, delta_ref):
        sa, sr = _sig(mga_ref[...].astype(f32)), _sig(mgr_ref[...].astype(f32))
        ya, yr = _dot(pa_ref[...], wa_ref[...]), _dot(pr_ref[...], wr_ref[...])
        dm = _dot_nt(do_ref[...], wo_ref[...])
        dya, dyr = (dm * sa).astype(bf16), (dm * sr).astype(bf16)
        dya_ref[...] = dya
        dyr_ref[...] = dyr
        dmga_ref[...] = (dm * ya * sa * (1.0 - sa)).astype(bf16)
        dmgr_ref[...] = (dm * yr * sr * (1.0 - sr)).astype(bf16)
        o_att = oa_ref[...]
        doa, dga_ref[...] = branch(dya, wa_ref, ga_ref, o_att)
        doa_ref[...] = doa
        dh_ref[...], dgr_ref[...] = branch(dyr, wr_ref, gr_ref, h_ref[...].astype(f32))
        ch = lax.broadcasted_iota(jnp.int32, (D, LANES), 0)
        hd = lax.broadcasted_iota(jnp.int32, (D, LANES), 1)
        pick = (ch // 64 == hd).astype(bf16)
        per_head = sum(_dot(piece, pick) for piece in _split3(doa.astype(f32) * o_att))
        delta_ref[0] = per_head.T[:HEADS, :]

    once = pl.BlockSpec((D, D), lambda i: (0, 0), pipeline_mode=pl.Buffered(1))
    return pl.pallas_call(
        body, name="merge_bwd", grid=(t // TQ,),
        in_specs=[_tile(TQ, D), _tile(TQ, D, 3), _tile(TQ, D, 4), _tile(TQ, D), _tile(TQ, D), _tile(TQ, D, 0),
                  _tile(TQ, D, 2), _tile(TQ, D), _tile(TQ, D), once, once, once],
        out_specs=[_tile(TQ, D)] * 8 + [pl.BlockSpec((1, HEADS, TQ), lambda i: (i, 0, 0))],
        out_shape=[jax.ShapeDtypeStruct((t, D), bf16)] * 8 + [jax.ShapeDtypeStruct((t // TQ, HEADS, TQ), f32)],
        compiler_params=_params(("parallel",), VMEM_LIMIT),
    )(do, rest, rest, pa, pr, rest, rest, o_att, hrec, w_a, w_r, w_out)


def _rnn_bwd(dh, a, h, xc, rest, conv_w, conv_b, wa_d, wx_d, ba, bx, lam, seq):
    t = dh.shape[0]
    nb, nt = t // seq, seq // TL
    diag = (D // LANES, LANES, LANES)

    def body(dh_ref, a_ref, h_ref, hprev_ref, xc_ref, x_ref, xprev_ref, cw_ref, cb_ref, wa_ref, wx_ref,
             ba_ref, bx_ref, lam_ref, dxr_ref, dwa_ref, dwx_ref, vec_ref, g_scr, dxc_scr, dxr_scr, qcarry, dxc_next):
        b, tt = pl.program_id(0), pl.program_id(1)
        rt = nt - 1 - tt

        @pl.when((b == 0) & (tt == 0))
        def _():
            dwa_ref[...] = jnp.zeros_like(dwa_ref)
            dwx_ref[...] = jnp.zeros_like(dwx_ref)
            vec_ref[...] = jnp.zeros_like(vec_ref)

        @pl.when(tt == 0)
        def _():
            qcarry[...] = jnp.zeros_like(qcarry)
            dxc_next[...] = jnp.zeros_like(dxc_next)

        g_scr[...] = dh_ref[...].astype(f32)

        def step(k, q):
            s = TL - 1 - k
            g = g_scr[pl.ds(s, 1), :] + q
            g_scr[pl.ds(s, 1), :] = g
            return a_ref[pl.ds(s, 1), :] * g

        qcarry[...] = lax.fori_loop(0, TL, step, qcarry[...], unroll=8)

        row = lax.broadcasted_iota(jnp.int32, (TL, D), 0)
        row8 = lax.broadcasted_iota(jnp.int32, (8, D), 0)
        g = g_scr[...]
        av = a_ref[...]
        xc = xc_ref[...].astype(f32)
        hlast = jnp.where(rt > 0, hprev_ref[...].astype(f32)[PREV_ROWS - 1:], 0.0)
        hp = jnp.where(row == 0, hlast, pltpu.roll(h_ref[...].astype(f32), 1, 0))
        r, i, sp, log_a, w1, sq = _rnn_gates(xc, wa_ref, wx_ref, ba_ref, bx_ref, lam_ref)
        dix = g * sq
        di = dix * xc
        dxc = dix * i
        dsq = g * (i * xc)
        dlog_a = g * hp * av - dsq * jnp.where(sq > 0.0, (1.0 - w1) / sq, 0.0)
        dpr = (dlog_a * ((-RG_C) * sp)) * r * (1.0 - r)
        dpi = di * i * (1.0 - i)
        dprb, dpib, xcb = dpr.astype(bf16), dpi.astype(bf16), xc.astype(bf16)
        dxc = dxc + _dot_nt(dprb, wa_ref[...]) + _dot_nt(dpib, wx_ref[...])
        for j in range(D // LANES):
            cols = slice(j * LANES, (j + 1) * LANES)
            dwa_ref[j] += _dot_tn(xcb[:, cols], dprb[:, cols])
            dwx_ref[j] += _dot_tn(xcb[:, cols], dpib[:, cols])
        vec_ref[pl.ds(0, 1), :] += jnp.sum(dpr, axis=0, keepdims=True)
        vec_ref[pl.ds(1, 1), :] += jnp.sum(dpi, axis=0, keepdims=True)
        dsp = jnp.sum(dlog_a * ((-RG_C) * r), axis=0, keepdims=True)
        vec_ref[pl.ds(2, 1), :] += dsp * (-_sig(-lam_ref[...]))
        vec_ref[pl.ds(3, 1), :] += jnp.sum(dxc, axis=0, keepdims=True)

        dxc_scr[...] = dxc
        bot8 = dxc_scr[pl.ds(TL - 8, 8), :]
        nxt8 = dxc_next[...]
        dxr = cw_ref[pl.ds(3, 1), :] * dxc
        dxr8 = cw_ref[pl.ds(3, 1), :] * bot8
        for sh in range(1, 4):
            w = cw_ref[pl.ds(3 - sh, 1), :]
            dxr = dxr + w * pltpu.roll(dxc, TL - sh, 0)
            dxr8 = dxr8 + w * jnp.where(row8 < 8 - sh, pltpu.roll(bot8, 8 - sh, 0), pltpu.roll(nxt8, 8 - sh, 0))
        dxr_scr[...] = dxr
        dxr_scr[pl.ds(TL - 8, 8), :] = dxr8
        dxr_ref[...] = dxr_scr[...].astype(bf16)
        dxc_next[...] = dxc_scr[pl.ds(0, 8), :]

        x = x_ref[...].astype(f32)
        prev8 = jnp.where(rt > 0, xprev_ref[...].astype(f32)[PREV_ROWS - 8:], 0.0)
        dxc_top8 = dxc_scr[pl.ds(0, 8), :]
        vec_ref[pl.ds(7, 1), :] += jnp.sum(dxc * x, axis=0, keepdims=True)
        for sh in range(1, 4):
            inside = jnp.sum(dxc * jnp.where(row >= sh, pltpu.roll(x, sh, 0), 0.0), axis=0, keepdims=True)
            above = jnp.sum(dxc_top8 * jnp.where(row8 < sh, pltpu.roll(prev8, sh, 0), 0.0), axis=0, keepdims=True)
            vec_ref[pl.ds(7 - sh, 1), :] += inside + above

    tile = lambda cb: pl.BlockSpec((TL, D), lambda b, tt, cb=cb: (b * nt + nt - 1 - tt, cb))
    prev = lambda cb: pl.BlockSpec(
        (PREV_ROWS, D), lambda b, tt, cb=cb: (jnp.maximum((b * nt + nt - 1 - tt) * (TL // PREV_ROWS) - 1, 0), cb))
    vec = _whole((1, D))
    return pl.pallas_call(
        body, name="rnn_bwd", grid=(nb, nt),
        in_specs=[tile(0), tile(0), tile(0), prev(0), tile(0), tile(1), prev(1),
                  _whole((4, D)), vec, _whole((D, D)), _whole((D, D)), vec, vec, vec],
        out_specs=[tile(0), _whole(diag), _whole(diag), _whole((8, D))],
        out_shape=[jax.ShapeDtypeStruct((t, D), bf16), jax.ShapeDtypeStruct(diag, f32),
                   jax.ShapeDtypeStruct(diag, f32), jax.ShapeDtypeStruct((8, D), f32)],
        scratch_shapes=[pltpu.VMEM((TL, D), f32), pltpu.VMEM((TL, D), f32), pltpu.VMEM((TL, D), f32),
                        pltpu.VMEM((1, D), f32), pltpu.VMEM((8, D), f32)],
        compiler_params=_params(("arbitrary", "arbitrary"), VMEM_LIMIT),
    )(dh, a, h, h, xc, rest, rest, conv_w, conv_b, wa_d, wx_d, ba, bx, lam)


def _attn_bwd(qa, ka, qkv, doa, lse, delta, seq):
    t = qkv.shape[0]
    nb, nq = t // seq, seq // TQ
    hg = ATT_GROUP
    ng, npair = HEADS // hg, hg // 2

    def body(qa_ref, ka_ref, q_ref, k_ref, v_ref, do_ref, lse_ref, dl_ref, dq_ref, dk_ref, dv_ref, dc_ref,
             dqt_scr, dk_scr, dv_scr, ds_scr, kht_scr):
        gi, kt = pl.program_id(1), pl.program_id(2)
        lane = lax.broadcasted_iota(jnp.int32, (1, LANES), 1)
        krow = lax.broadcasted_iota(jnp.int32, (TQ, TQ), 0)
        qcol = lax.broadcasted_iota(jnp.int32, (TQ, TQ), 1)
        lmask = [(lane // 64) == hh for hh in range(2)]
        scale = jnp.asarray(QK_SCALE, bf16)

        @pl.when(kt == 0)
        def _():
            dqt_scr[...] = jnp.zeros_like(dqt_scr)

        dk_scr[...] = jnp.zeros_like(dk_scr)
        dv_scr[...] = jnp.zeros_like(dv_scr)
        ds_scr[...] = jnp.zeros_like(ds_scr)
        for g in range(hg):
            k2 = k_ref[:, pl.ds((g // 2) * LANES, LANES)]
            kht_scr[g] = jnp.where(lmask[g % 2], k2, jnp.zeros_like(k2)).T

        def q_step(qt, masked):
            qs = pl.multiple_of(qt * TQ, TQ)
            heads = range(hg)
            do2 = [do_ref[pl.ds(qs, TQ), pl.ds(j * LANES, LANES)] for j in range(npair)]
            q2 = [q_ref[pl.ds(qs, TQ), pl.ds(j * LANES, LANES)] for j in range(npair)]
            doh = [jnp.where(lmask[g % 2], do2[g // 2], jnp.zeros_like(do2[0])) for g in heads]
            qh = [jnp.where(lmask[g % 2], q2[g // 2], jnp.zeros_like(q2[0])) * scale for g in heads]
            st = [_dot_nt(ka_ref[:, pl.ds(g * LANES, LANES)], qa_ref[pl.ds(qs, TQ), pl.ds(g * LANES, LANES)])
                  for g in heads]
            if masked:
                st = [jnp.where(krow <= qcol, s, MASK_VALUE) for s in st]
            dp = [_dot_nt(v_ref[:, pl.ds((g // 2) * LANES, LANES)], doh[g]) for g in heads]
            p = [jnp.exp(st[g] - lse_ref[qt, pl.ds(hg * gi + g, 1), :]) for g in heads]
            ds = [p[g] * (dp[g] - dl_ref[qt, pl.ds(hg * gi + g, 1), :]) for g in heads]
            pb = [x.astype(bf16) for x in p]
            dsb = [x.astype(bf16) for x in ds]
            for j in range(npair):
                a, b = 2 * j, 2 * j + 1
                dv_scr[j] += _dot(pb[a], doh[a]) + _dot(pb[b], doh[b])
                dk_scr[j] += _dot(dsb[a], qh[a]) + _dot(dsb[b], qh[b])
                dqt_scr[qt, j] += (_dot(kht_scr[a], dsb[a]) + _dot(kht_scr[b], dsb[b])) * QK_SCALE
            for g in heads:
                ds_scr[g] += ds[g][:, :LANES] + ds[g][:, LANES:]

        q_step(kt, True)

        def loop_body(qt, carry):
            q_step(qt, False)
            return carry

        lax.fori_loop(kt + 1, nq, loop_body, 0)

        dc = jnp.zeros((TQ, LANES), f32)
        for g in range(hg):
            dc = jnp.where(lane == g, -jnp.sum(ds_scr[g], axis=1, keepdims=True), dc)
        dc_ref[...] = dc
        for j in range(npair):
            dk_ref[:, pl.ds(j * LANES, LANES)] = dk_scr[j].astype(bf16)
            dv_ref[:, pl.ds(j * LANES, LANES)] = dv_scr[j].astype(bf16)

        @pl.when(kt == nq - 1)
        def _():
            for qt in range(nq):
                for j in range(npair):
                    dq_ref[pl.ds(qt * TQ, TQ), pl.ds(j * LANES, LANES)] = dqt_scr[qt, j].T.astype(bf16)

    vw = hg * 64
    seqspec = pl.BlockSpec((seq, vw), lambda b, gi, kt: (b, gi))
    kspec = lambda off: pl.BlockSpec((TQ, vw), lambda b, gi, kt: (b * nq + kt, off + gi))
    rowspec = pl.BlockSpec((nq, HEADS, TQ), lambda b, gi, kt: (b, 0, 0))
    return pl.pallas_call(
        body, name="attn_bwd", grid=(nb, ng, nq),
        in_specs=[pl.BlockSpec((seq, hg * LANES), lambda b, gi, kt: (b, gi)),
                  pl.BlockSpec((TQ, hg * LANES), lambda b, gi, kt: (b * nq + kt, gi)),
                  seqspec, kspec(ng), kspec(2 * ng), seqspec, rowspec, rowspec],
        out_specs=[seqspec, kspec(0), kspec(0), pl.BlockSpec((TQ, LANES), lambda b, gi, kt: (b * nq + kt, gi))],
        out_shape=[jax.ShapeDtypeStruct((t, D), bf16)] * 3 + [jax.ShapeDtypeStruct((t, ng * LANES), f32)],
        scratch_shapes=[pltpu.VMEM((nq, npair, LANES, TQ), f32), pltpu.VMEM((npair, TQ, LANES), f32),
                        pltpu.VMEM((npair, TQ, LANES), f32), pltpu.VMEM((hg, TQ, LANES), f32),
                        pltpu.VMEM((hg, LANES, TQ), bf16)],
        compiler_params=_params(("parallel", "parallel", "arbitrary"), VMEM_LIMIT),
    )(qa, ka, qkv, qkv, qkv, doa, lse, delta)


def _forget_bwd(dc, f128, seq):
    t = f128.shape[0]
    nb = seq // LANES
    groups = dc.shape[1] // LANES

    def body(dc_ref, f_ref, df_ref, dbf_ref):
        @pl.when(pl.program_id(0) == 0)
        def _():
            dbf_ref[...] = jnp.zeros_like(dbf_ref)

        r = lax.broadcasted_iota(jnp.int32, (LANES, LANES), 0)
        cidx = lax.broadcasted_iota(jnp.int32, (LANES, LANES), 1)
        tri = (r <= cidx).astype(f32)
        carry = jnp.zeros((1, LANES), f32)
        total = jnp.zeros((1, LANES), f32)
        for blk in reversed(range(nb)):
            dcb = dc_ref[pl.ds(blk * LANES, LANES), pl.ds(0, LANES)]
            for gi in range(1, groups):
                dcb = dcb + pltpu.roll(dc_ref[pl.ds(blk * LANES, LANES), pl.ds(gi * LANES, LANES)], gi * ATT_GROUP, 1)
            dlf = jnp.dot(tri, dcb, preferred_element_type=f32, precision=lax.Precision.HIGHEST) + carry
            df = dlf * _sig(-f_ref[pl.ds(blk * LANES, LANES), :])
            df_ref[pl.ds(blk * LANES, LANES), :] = df.astype(bf16)
            total = total + jnp.sum(df, axis=0, keepdims=True)
            carry = carry + jnp.sum(dcb, axis=0, keepdims=True)
        dbf_ref[...] += total

    return pl.pallas_call(
        body, name="forget_bwd", grid=(t // seq,),
        in_specs=[pl.BlockSpec((seq, groups * LANES), lambda b: (b, 0)), pl.BlockSpec((seq, LANES), lambda b: (b, 0))],
        out_specs=[pl.BlockSpec((seq, LANES), lambda b: (b, 0)), _whole((1, LANES))],
        out_shape=[jax.ShapeDtypeStruct((t, LANES), bf16), jax.ShapeDtypeStruct((1, LANES), f32)],
        compiler_params=_params(("arbitrary",)),
    )(dc, f128)


def _in_bwd(dz, df, x, dy, w_all, w_pre):
    t = x.shape[0]
    n_dz = len(dz)

    def body(*refs):
        dz_refs = refs[:n_dz]
        df_ref, x_ref, dy_ref, w_ref, wp_ref, gx_ref, dwp_ref = refs[n_dz:]

        @pl.when(pl.program_id(0) == 0)
        def _():
            dwp_ref[...] = jnp.zeros_like(dwp_ref)

        dh = _dot(df_ref[...], w_ref[pl.ds(n_dz * D, LANES), :])
        for p in range(n_dz):
            dh = dh + _dot(dz_refs[p][...], w_ref[pl.ds(p * D, D), :])
        xv = x_ref[...]
        r1 = lax.rsqrt(jnp.mean(xv * xv, axis=-1, keepdims=True) + NORM_EPS)
        xh = xv * r1
        dwp_ref[...] += jnp.sum(dh * xh, axis=0, keepdims=True)
        dxh = dh * wp_ref[...]
        gx_ref[...] = dy_ref[...] + r1 * (dxh - xh * jnp.mean(dxh * xh, axis=-1, keepdims=True))

    once = lambda shape: pl.BlockSpec(shape, lambda i: (0, 0), pipeline_mode=pl.Buffered(1))
    return pl.pallas_call(
        body, name="in_bwd", grid=(t // TM,),
        in_specs=[_tile(TM, D)] * n_dz + [_tile(TM, LANES), _tile(TM, D), _tile(TM, D), once(w_all.shape),
                  _whole((1, D))],
        out_specs=[_tile(TM, D), _whole((1, D))],
        out_shape=[jax.ShapeDtypeStruct((t, D), f32), jax.ShapeDtypeStruct((1, D), f32)],
        compiler_params=_params(("arbitrary",), VMEM_LIMIT),
    )(*dz, df, x, dy, w_all, w_pre)


def _tn_mm(name, a, b, tn, out_dtype=f32, tk=2048):
    t, k = a.shape
    tk = min(tk, t)
    n = b.shape[1]
    nk = t // tk

    def body(a_ref, b_ref, o_ref, s_ref, acc_ref):
        j, kk = pl.program_id(0), pl.program_id(1)

        @pl.when(kk == 0)
        def _():
            acc_ref[...] = jnp.zeros_like(acc_ref)

        @pl.when((j == 0) & (kk == 0))
        def _():
            s_ref[...] = jnp.zeros_like(s_ref)

        av = a_ref[...]
        acc_ref[...] += _dot_tn(av, b_ref[...])

        @pl.when(j == 0)
        def _():
            s_ref[...] += jnp.sum(av.astype(f32), axis=0, keepdims=True)

        @pl.when(kk == nk - 1)
        def _():
            o_ref[...] = acc_ref[...].astype(out_dtype)

    return pl.pallas_call(
        body, name=name, grid=(n // tn, nk),
        in_specs=[pl.BlockSpec((tk, k), lambda j, kk: (kk, 0)), pl.BlockSpec((tk, tn), lambda j, kk: (kk, j))],
        out_specs=[pl.BlockSpec((k, tn), lambda j, kk: (0, j)), _whole((1, k))],
        out_shape=[jax.ShapeDtypeStruct((k, n), out_dtype), jax.ShapeDtypeStruct((1, k), f32)],
        scratch_shapes=[pltpu.VMEM((k, tn), f32)],
        compiler_params=_params(("arbitrary", "arbitrary"), VMEM_LIMIT),
    )(a, b)


def _position():
    return lax.axis_index("x"), lax.axis_index("y"), lax.axis_index("c")


ROW_BLOCK = 128


def _pick_rows(layout, first, count):
    acc = jnp.zeros((ROW_BLOCK, D), f32)
    seg_start = 0
    for ref, ref_row, rows in layout:
        lo, hi = max(first, seg_start), min(first + count, seg_start + rows)
        if lo < hi and ref is not None:
            off, take, done = ref_row + lo - seg_start, hi - lo, lo - first
            start = off // 16 * 16
            win = -(-(off - start + take) // 16) * 16
            r = lax.broadcasted_iota(jnp.int32, (ROW_BLOCK, win), 0)
            col = lax.broadcasted_iota(jnp.int32, (ROW_BLOCK, win), 1)
            pick = ((col - r == off - start - done) & (r >= done) & (r < done + take)).astype(bf16)
            acc = acc + _dot(pick, ref[pl.ds(start, win), :])
        seg_start += rows
    return acc


def _assemble_rows(shards_ref, shard_rows, segments, out_ref):
    layout = [(shards_ref.at[j], 0, shard_rows) for j in range(shards_ref.shape[0])]
    for out0, log0, count in segments:
        for b0 in range(0, count, ROW_BLOCK):
            block = _pick_rows(layout, log0 + b0, min(ROW_BLOCK, count - b0))
            out_ref[pl.ds(out0 + b0, ROW_BLOCK), :] = block.astype(bf16)


def _pack_pieces(blocks, shard_rows, padded):
    arrays = [a for a, _ in blocks if a is not None]
    piece_rows = padded // 2

    def body(*refs):
        out_ref = refs[-1]
        it = iter(refs[:-1])
        layout = [(None if a is None else next(it), 0, rows) for a, rows in blocks]
        for k in range(N_DEV):
            chip, half = divmod(k, 2)
            for b0 in range(0, piece_rows, ROW_BLOCK):
                n = min(ROW_BLOCK, piece_rows - b0)
                in_shard = half * piece_rows + b0
                count = max(0, min(n, shard_rows - in_shard))
                block = _pick_rows(layout, chip * shard_rows + in_shard, count)
                out_ref[k, pl.ds(b0, n), :] = block[:n].astype(bf16)

    vm = pl.BlockSpec(memory_space=pltpu.VMEM)
    return pl.pallas_call(
        body, name="pack_pieces", in_specs=[vm] * len(arrays), out_specs=vm,
        out_shape=jax.ShapeDtypeStruct((N_DEV, piece_rows, D), bf16),
        compiler_params=pltpu.CompilerParams(vmem_limit_bytes=VMEM_LIMIT),
    )(*arrays)


def _gather_shards(parts, small, shard_rows, segments, out_rows):
    n = len(parts)
    halves = [p.shape[0] // 2 for p in parts]
    cuts = [-(-h // 32) * 16 for h in halves]
    n_direct, n_relay, n_sib = 4 * n, 2 * n, 6 * n

    def body(*refs):
        srcs, small_src = refs[:n], refs[n]
        dsts, small_dst, whole_ref = refs[n + 1:2 * n + 1], refs[2 * n + 1], refs[2 * n + 2]
        send, recv, local = refs[2 * n + 3:]
        x, y, c = _position()
        me = 2 * x + y
        chips = [(1 - x, y), (x, 1 - y), (1 - x, 1 - y)]
        ids = [2 * px + py for px, py in chips]

        def rows(a, half, quarter):
            start = half * halves[a] + (cuts[a] if quarter else 0)
            return pl.ds(start, halves[a] - cuts[a] if quarter else cuts[a])

        def landing(a, shard, half, quarter):
            return dsts[a].at[shard, rows(a, half, quarter), :]

        def direct(a, nb, quarter, shard):
            k = (a * 2 + nb) * 2 + quarter
            px, py = chips[nb]
            return pltpu.make_async_remote_copy(
                src_ref=srcs[a].at[rows(a, c, quarter), :], dst_ref=landing(a, shard, c, quarter),
                send_sem=send.at[k], recv_sem=recv.at[k], device_id=(px, py, c), device_id_type=MESH)

        def relay(a, quarter, shard):
            k = n_direct + a * 2 + quarter
            px, py = chips[1 - quarter]
            return pltpu.make_async_remote_copy(
                src_ref=landing(a, shard, c, quarter), dst_ref=landing(a, shard, c, quarter),
                send_sem=send.at[k], recv_sem=recv.at[k], device_id=(px, py, c), device_id_type=MESH)

        def to_sibling(a, origin, quarter, half):
            k = n_direct + n_relay + (a * 3 + origin) * 2 + quarter
            return pltpu.make_async_remote_copy(
                src_ref=landing(a, ids[origin], half, quarter), dst_ref=landing(a, ids[origin], half, quarter),
                send_sem=send.at[k], recv_sem=recv.at[k], device_id=(x, y, 1 - c), device_id_type=MESH)

        def small_copy(j, shard):
            k = n_direct + n_relay + n_sib + j
            px, py = chips[j]
            return pltpu.make_async_remote_copy(
                src_ref=small_src, dst_ref=small_dst.at[shard], send_sem=send.at[k], recv_sem=recv.at[k],
                device_id=(px, py, c), device_id_type=MESH)

        own = [pltpu.make_async_copy(srcs[a], dsts[a].at[me], local.at[a]) for a in range(n)]
        own.append(pltpu.make_async_copy(small_src, small_dst.at[me], local.at[n]))
        for cp in own:
            cp.start()
        sent = [direct(a, nb, q, me) for q in range(2) for a in range(n) for nb in range(2)]
        sent += [small_copy(j, me) for j in range(3)]
        for cp in sent:
            cp.start()

        def passed_on(cp):
            cp.start()
            sent.append(cp)

        for q in range(2):
            for a in range(n):
                for nb in range(2):
                    direct(a, nb, q, ids[nb]).wait_recv()
                    passed_on(to_sibling(a, nb, q, c))
                    if nb == q:
                        passed_on(relay(a, q, ids[nb]))
        for a in range(n):
            for q in range(2):
                relay(a, q, ids[2]).wait_recv()
                passed_on(to_sibling(a, 2, q, c))
        for j in range(3):
            small_copy(j, ids[j]).wait_recv()
            for a in range(n):
                for q in range(2):
                    to_sibling(a, j, q, 1 - c).wait_recv()
        for cp in sent:
            cp.wait_send()
        for cp in own:
            cp.wait()
        _assemble_rows(dsts[0], shard_rows, segments, whole_ref)

    vm = pl.BlockSpec(memory_space=pltpu.VMEM)
    n_sems = n_direct + n_relay + n_sib + 3
    out = pl.pallas_call(
        body, name="gather_shards",
        in_specs=[vm] * (n + 1), out_specs=[vm] * (n + 2),
        out_shape=[jax.ShapeDtypeStruct((N_CHIPS,) + p.shape, p.dtype) for p in parts + [small]]
        + [jax.ShapeDtypeStruct((out_rows, parts[0].shape[1]), parts[0].dtype)],
        scratch_shapes=[pltpu.SemaphoreType.DMA((n_sems,)), pltpu.SemaphoreType.DMA((n_sems,)),
                        pltpu.SemaphoreType.DMA((n + 1,))],
        compiler_params=pltpu.CompilerParams(vmem_limit_bytes=VMEM_LIMIT),
    )(*parts, small)
    return out[1:]


def _allsum_rows(part):
    rows_n = part.shape[0]

    def body(x_ref, gath_ref, sum_ref, send_sems, recv_sems, local_sem):
        x, y, c = _position()
        me, sibling = (x, y, c), (x, y, 1 - c)
        chips = [(1 - x, y), (x, 1 - y), (1 - x, 1 - y)]

        def rows(px, py, pc):
            return gath_ref.at[pl.ds((4 * px + 2 * py + pc) * rows_n, rows_n), :]

        def copy(k, block, to, src=None):
            return pltpu.make_async_remote_copy(
                src_ref=rows(*block) if src is None else src, dst_ref=rows(*block),
                send_sem=send_sems.at[k], recv_sem=recv_sems.at[k], device_id=to, device_id_type=MESH)

        mine = pltpu.make_async_copy(x_ref, rows(*me), local_sem)
        mine.start()
        first = [copy(0, me, sibling, src=x_ref)]
        first += [copy(1 + j, me, (*chip, c), src=x_ref) for j, chip in enumerate(chips)]
        for cp in first:
            cp.start()
        passed = [copy(4 + j, (*chip, c), sibling) for j, chip in enumerate(chips)]
        for j, chip in enumerate(chips):
            copy(1 + j, (*chip, c), me).wait_recv()
            passed[j].start()
        copy(0, sibling, me).wait_recv()
        for j, chip in enumerate(chips):
            copy(4 + j, (*chip, 1 - c), me).wait_recv()
        for cp in first + passed:
            cp.wait_send()
        mine.wait()
        total = gath_ref[pl.ds(0, rows_n), :]
        for d in range(1, N_DEV):
            total = total + gath_ref[pl.ds(d * rows_n, rows_n), :]
        sum_ref[...] = total

    vm = pl.BlockSpec(memory_space=pltpu.VMEM)
    return pl.pallas_call(
        body, name="allsum_rows", in_specs=[vm], out_specs=[vm, vm],
        out_shape=[jax.ShapeDtypeStruct((N_DEV * rows_n, D), f32), jax.ShapeDtypeStruct((rows_n, D), f32)],
        scratch_shapes=[pltpu.SemaphoreType.DMA((7,)), pltpu.SemaphoreType.DMA((7,)), pltpu.SemaphoreType.DMA],
    )(part)[1]


PAIR_ROWS = 16


def _pair_reduce(name, pieces):
    _, r, n = pieces.shape

    def body(p_ref, o_ref, land, send, recv):
        x, y, c = _position()

        def remote(j, half):
            return pltpu.make_async_remote_copy(
                src_ref=p_ref.at[2 * j + half], dst_ref=land.at[j], send_sem=send.at[j], recv_sem=recv.at[j],
                device_id=(x, y, 1 - c), device_id_type=MESH)

        sends = [remote(j, 1 - c) for j in range(N_CHIPS)]
        for cp in sends:
            cp.start()
        for j in range(N_CHIPS):
            remote(j, c).wait_recv()

            def add_rows(i, carry, j=j):
                rows = pl.ds(pl.multiple_of(i * PAIR_ROWS, PAIR_ROWS), PAIR_ROWS)
                o_ref[j, rows, :] = (p_ref[2 * j + c, rows, :].astype(f32) + land[j, rows, :].astype(f32)).astype(bf16)
                return carry

            lax.fori_loop(0, r // PAIR_ROWS, add_rows, 0)
        for cp in sends:
            cp.wait_send()

    vm = pl.BlockSpec(memory_space=pltpu.VMEM)
    return pl.pallas_call(
        body, name=name, in_specs=[vm], out_specs=vm,
        out_shape=jax.ShapeDtypeStruct((N_CHIPS, r, n), bf16),
        scratch_shapes=[pltpu.VMEM((N_CHIPS, r, n), bf16), pltpu.SemaphoreType.DMA((N_CHIPS,)),
                        pltpu.SemaphoreType.DMA((N_CHIPS,))],
        compiler_params=pltpu.CompilerParams(vmem_limit_bytes=VMEM_LIMIT),
    )(pieces)


def _chip_exchange(arrs):
    n = len(arrs)
    heights = [a.shape[1] for a in arrs]
    cuts = [-(-r // 32) * 16 for r in heights]

    def body(*refs):
        srcs, dsts, relays = refs[:n], refs[n:2 * n], refs[2 * n:3 * n]
        send, recv, local = refs[3 * n:]
        x, y, c = _position()
        me = 2 * x + y
        chips = [(1 - x, y), (x, 1 - y), (1 - x, 1 - y)]
        ids = [2 * px + py for px, py in chips]

        def rows(a, quarter):
            return pl.ds(cuts[a], heights[a] - cuts[a]) if quarter else pl.ds(0, cuts[a])

        def held(a, quarter):
            size = heights[a] - cuts[a] if quarter else cuts[a]
            return relays[a].at[quarter, pl.ds(0, size), :]

        def direct(a, nb, piece, landing):
            px, py = chips[nb]
            return pltpu.make_async_remote_copy(
                src_ref=srcs[a].at[piece], dst_ref=dsts[a].at[landing], send_sem=send.at[a * 2 + nb],
                recv_sem=recv.at[a * 2 + nb], device_id=(px, py, c), device_id_type=MESH)

        def first_hop(a, quarter):
            k = 2 * n + a * 2 + quarter
            px, py = chips[quarter]
            return pltpu.make_async_remote_copy(
                src_ref=srcs[a].at[ids[2], rows(a, quarter), :], dst_ref=held(a, quarter), send_sem=send.at[k],
                recv_sem=recv.at[k], device_id=(px, py, c), device_id_type=MESH)

        def second_hop(a, quarter, origin):
            k = 4 * n + a * 2 + quarter
            px, py = chips[1 - quarter]
            return pltpu.make_async_remote_copy(
                src_ref=held(a, quarter), dst_ref=dsts[a].at[origin, rows(a, quarter), :], send_sem=send.at[k],
                recv_sem=recv.at[k], device_id=(px, py, c), device_id_type=MESH)

        own = [pltpu.make_async_copy(srcs[a].at[me], dsts[a].at[me], local.at[a]) for a in range(n)]
        sent = [first_hop(a, q) for a in range(n) for q in range(2)]
        sent += [direct(a, nb, ids[nb], me) for a in range(n) for nb in range(2)]
        for cp in sent + own:
            cp.start()
        for a in range(n):
            for q in range(2):
                first_hop(a, q).wait_recv()
                sent.append(second_hop(a, q, ids[q]))
                sent[-1].start()
        for a in range(n):
            for nb in range(2):
                direct(a, nb, me, ids[nb]).wait_recv()
            for q in range(2):
                second_hop(a, q, ids[2]).wait_recv()
        for cp in sent:
            cp.wait_send()
        for cp in own:
            cp.wait()

    anyspec = pl.BlockSpec(memory_space=pl.ANY)
    out = pl.pallas_call(
        body, name="chip_exchange", in_specs=[anyspec] * n, out_specs=[anyspec] * (2 * n),
        out_shape=[jax.ShapeDtypeStruct(a.shape, a.dtype) for a in arrs]
        + [jax.ShapeDtypeStruct((2, cut, a.shape[2]), a.dtype) for a, cut in zip(arrs, cuts)],
        scratch_shapes=[pltpu.SemaphoreType.DMA((6 * n,)), pltpu.SemaphoreType.DMA((6 * n,)),
                        pltpu.SemaphoreType.DMA((n,))],
    )(*arrs)
    return out[:n]


def _swap_halves(arrs):
    n = len(arrs)

    def body(*refs):
        srcs, dsts = refs[:n], refs[n:2 * n]
        send, recv, local = refs[2 * n:]
        x, y, c = _position()

        def remote(a, landing):
            return pltpu.make_async_remote_copy(
                src_ref=srcs[a], dst_ref=dsts[a].at[landing], send_sem=send.at[a], recv_sem=recv.at[a],
                device_id=(x, y, 1 - c), device_id_type=MESH)

        own = [pltpu.make_async_copy(srcs[a], dsts[a].at[c], local.at[a]) for a in range(n)]
        sends = [remote(a, c) for a in range(n)]
        for cp in sends + own:
            cp.start()
        for a in range(n):
            remote(a, 1 - c).wait_recv()
        for cp in sends:
            cp.wait_send()
        for cp in own:
            cp.wait()

    vm = pl.BlockSpec(memory_space=pltpu.VMEM)
    return pl.pallas_call(
        body, name="swap_halves", in_specs=[vm] * n, out_specs=[vm] * n,
        out_shape=[jax.ShapeDtypeStruct((2,) + a.shape, a.dtype) for a in arrs],
        scratch_shapes=[pltpu.SemaphoreType.DMA((n,)), pltpu.SemaphoreType.DMA((n,)), pltpu.SemaphoreType.DMA((n,))],
        compiler_params=pltpu.CompilerParams(vmem_limit_bytes=VMEM_LIMIT),
    )(*arrs)


def _row_block(r):
    return 128 if r % 128 == 0 else r


def _sum_slots(name, slots):
    s, r, n = slots.shape
    rb = _row_block(r)

    def body(s_ref, o_ref):
        total = s_ref[0].astype(f32)
        for d in range(1, s):
            total = total + s_ref[d].astype(f32)
        o_ref[...] = total

    return pl.pallas_call(
        body, name=name, grid=(r // rb,),
        in_specs=[pl.BlockSpec((s, rb, n), lambda i: (0, i, 0))],
        out_specs=pl.BlockSpec((rb, n), lambda i: (i, 0)),
        out_shape=jax.ShapeDtypeStruct((r, n), f32),
        compiler_params=_params(("parallel",), VMEM_LIMIT),
    )(slots)


def _adamw(name, w, g, m, v):
    r, n = w.shape
    if r % 128 == 0 or r * n <= 128 * 1024:
        rb, nb = _row_block(r), n
    else:
        rb, nb = r, LANES

    def body(w_ref, g_ref, m_ref, v_ref, d_ref, nm_ref, nv_ref):
        gv = g_ref[...]
        m2 = ADAM_B1 * m_ref[...] + (1.0 - ADAM_B1) * gv
        v2 = ADAM_B2 * v_ref[...] + (1.0 - ADAM_B2) * (gv * gv)
        m_hat = m2 / (1.0 - ADAM_B1 ** ADAM_STEP)
        v_hat = v2 / (1.0 - ADAM_B2 ** ADAM_STEP)
        d_ref[...] = (-ADAM_LR) * (m_hat / (jnp.sqrt(v_hat) + ADAM_EPS) + ADAM_WD * w_ref[...])
        nm_ref[...] = m2
        nv_ref[...] = v2

    spec = pl.BlockSpec((rb, nb), lambda i, j: (i, j))
    return pl.pallas_call(
        body, name=name, grid=(r // rb, n // nb), in_specs=[spec] * 4, out_specs=[spec] * 3,
        out_shape=[jax.ShapeDtypeStruct((r, n), f32)] * 3,
        compiler_params=_params(("parallel", "parallel"), VMEM_LIMIT),
    )(w, g, m, v)


def _local_step(x2, tgt2, seq, wt):
    nb = x2.shape[0] // seq
    h, qkv = _norm_qkv(x2, wt["pre_w"], wt["w_all"], wt["b_qkv"])
    rest = _mm("in_rest", h, wt["w_all"], (3 * D, 5 * D), wt["b_rest"], bf16, 1024, 1024)
    f128 = _mm("in_f", h, wt["w_all"], (8 * D, LANES), wt["b_f"], f32, 1024, LANES)
    c = _forget_prep(f128, seq)
    qa, ka = _attn_prep(qkv, c)
    o_att, pa, lse = _attn_fwd(qa, ka, qkv, rest, seq)
    rnn_w = (wt["conv_w"], wt["conv_b"], wt["wa_d"], wt["wx_d"], wt["ba"], wt["bx"], wt["lam"])
    xc, a, hrec, pr = _rnn_fwd(rest, *rnn_w, seq)
    do, dy, mrg, loss8, d_post = _out_proj_loss(rest, pa, pr, wt["w_a"], wt["w_r"], wt["w_o"], x2, tgt2,
                                                wt["post_w"])
    dya, dyr, dmga, dmgr, doa, dga, dhrec, dgr, delta = _merge_bwd(do, rest, pa, pr, o_att, hrec, wt["w_a"], wt["w_r"],
                                                                   wt["w_o"])
    d_wo, _ = _tn_mm("dw_out", mrg, do, D)
    d_wa, _ = _tn_mm("dw_branch_a", pa, dya, D)
    d_wr, _ = _tn_mm("dw_branch_r", pr, dyr, D)
    dxr, d_wad, d_wxd, vec = _rnn_bwd(dhrec, a, hrec, xc, rest, *rnn_w, seq)
    dq, dk, dv, dc = _attn_bwd(qa, ka, qkv, doa, lse, delta, seq)
    df, db_f = _forget_bwd(dc, f128, seq)
    pieces = [dq, dk, dv, dga, dxr, dgr, dmga, dmgr]
    gx, d_pre = _in_bwd(pieces, df, x2, dy, wt["w_all"], wt["pre_w"])
    names = ["q", "k", "v", "ga", "xr", "gr", "mga", "mgr"]
    dws, dbs = [], []
    for nm, piece in zip(names, pieces):
        dw_p, db_p = _tn_mm("dw_in_" + nm, piece, h, D, bf16)
        dws.append((dw_p, D))
        dbs.append(db_p)
    dw_f, _ = _tn_mm("dw_in_f", df, h, D, bf16)
    shard_rows = IN_TOTAL // N_CHIPS
    w_in_pieces = _pack_pieces(dws[:3] + [(dw_f, HEADS)] + dws[3:] + [(None, IN_TOTAL - IN_USED)], shard_rows,
                               _padded_rows(shard_rows))
    d_b_in = jnp.concatenate(dbs[:3] + [db_f[:, :HEADS]] + dbs[3:] + [jnp.zeros((1, IN_TOTAL - IN_USED), f32)], axis=1)
    return dict(loss=loss8[0, 0], grad_x=gx, pre_w=d_pre, w_in_pieces=w_in_pieces, b_in=d_b_in, conv_w=vec[4:8],
                conv_b=vec[3:4],
                wa_d=d_wad, ba=vec[0:1], wx_d=d_wxd, bx=vec[1:2], lam=vec[2:3], w_a=d_wa, w_r=d_wr, w_o=d_wo,
                post_w=d_post)


def _block_diag(w):
    g, bw, _ = w.shape
    eye = jnp.eye(g, dtype=w.dtype)
    return (w[:, :, None, :] * eye[:, None, :, None]).reshape(g * bw, g * bw)


def _gate_blocks(diag):
    half = diag.shape[1] // 2
    return jnp.stack([diag[:, :half, :half], diag[:, half:, half:]], axis=1).reshape(-1, half, half)


def _padded_rows(rows):
    return -(-rows // 32) * 32


def _pad_cols(a, n):
    return jnp.pad(a, ((0, 0), (0, n - a.shape[1])))


def _pad_rows(a, n):
    return jnp.pad(a, ((0, n - a.shape[0]), (0, 0)))


def kernel(x, pre_norm_w, w_in, b_in, conv_w, conv_b, rg_wa, rg_ba, rg_wx, rg_bx, rg_lambda, w_branch_a, w_branch_r, w_out, post_norm_w, loss_target, m_pre_norm_w, m_w_in, m_b_in, m_conv_w, m_conv_b, m_rg_wa, m_rg_ba, m_rg_wx, m_rg_bx, m_rg_lambda, m_w_branch_a, m_w_branch_r, m_w_out, m_post_norm_w, v_pre_norm_w, v_w_in, v_b_in, v_conv_w, v_conv_b, v_rg_wa, v_rg_ba, v_rg_wx, v_rg_bx, v_rg_lambda, v_w_branch_a, v_w_branch_r, v_w_out, v_post_norm_w):
    nb, seq, _ = x.shape
    chip = 2 * lax.axis_index("x") + lax.axis_index("y")
    n_groups = rg_wa.shape[1]

    w_in_t = jnp.transpose(w_in[0])
    shard_cols = w_in_t.shape[0]
    padded = _padded_rows(shard_cols)
    q_end, f_end = 3 * D, 3 * D + HEADS
    segments = [(0, 0, q_end), (q_end, f_end, IN_USED - f_end), (IN_USED - HEADS, q_end, HEADS)]
    g_a, g_r, g_o, g_cw, w_all = _gather_shards(
        [_pad_rows(w_in_t.astype(bf16), padded), w_branch_a[0].astype(bf16), w_branch_r[0].astype(bf16),
         w_out[0].astype(bf16)], conv_w[0], shard_cols, segments, IN_USED - HEADS + LANES)
    wt = dict(
        pre_w=pre_norm_w, post_w=post_norm_w,
        w_all=w_all, b_qkv=b_in[:, :q_end], b_f=_pad_cols(b_in[:, q_end:f_end], LANES), b_rest=b_in[:, f_end:IN_USED],
        w_a=g_a.reshape(D, D), w_r=g_r.reshape(D, D), w_o=g_o.reshape(D, D),
        conv_w=jnp.transpose(g_cw, (1, 0, 2)).reshape(4, D), conv_b=conv_b,
        wa_d=_block_diag(rg_wa[0]).astype(bf16), wx_d=_block_diag(rg_wx[0]).astype(bf16),
        ba=rg_ba, bx=rg_bx, lam=rg_lambda)

    part = _local_step(x.reshape(nb * seq, D), loss_target.reshape(nb * seq, D), seq, wt)
    loss = lax.psum(part["loss"], ("x", "y", "c"))
    grad_x = part["grad_x"].reshape(nb, seq, D)

    small = jnp.concatenate([
        part["pre_w"], _pad_cols(part["b_in"], 10 * D).reshape(10, D), part["conv_b"],
        _gate_blocks(part["wa_d"]).reshape(-1, D), part["ba"],
        _gate_blocks(part["wx_d"]).reshape(-1, D), part["bx"], part["lam"], part["post_w"],
        part["conv_w"]], axis=0)
    n_small = small.shape[0]
    n_rep = n_small - 4
    tot = _allsum_rows(_pad_rows(small, -(-n_small // 8) * 8))
    g_rep = tot[:n_rep]
    g_conv_w = lax.dynamic_slice_in_dim(tot[n_rep:n_small], chip * (D // N_CHIPS), D // N_CHIPS, axis=1)

    def unpack(p):
        o = [0]

        def take(k):
            o[0] += k
            return p[o[0] - k:o[0]]

        pre = take(1)
        b = take(10).reshape(1, 10 * D)[:, :IN_TOTAL]
        cb = take(1)
        wa = take(64).reshape(rg_wa.shape)
        ba = take(1)
        wx = take(64).reshape(rg_wx.shape)
        bx = take(1)
        lam = take(1)
        post = take(1)
        return dict(pre_norm_w=pre, b_in=b, conv_b=cb, rg_wa=wa, rg_ba=ba, rg_wx=wx, rg_bx=bx, rg_lambda=lam,
                    post_norm_w=post)

    grads = unpack(g_rep)
    replicated = dict(
        pre_norm_w=(pre_norm_w, m_pre_norm_w, v_pre_norm_w), b_in=(b_in, m_b_in, v_b_in),
        conv_b=(conv_b, m_conv_b, v_conv_b), rg_wa=(rg_wa, m_rg_wa, v_rg_wa), rg_ba=(rg_ba, m_rg_ba, v_rg_ba),
        rg_wx=(rg_wx, m_rg_wx, v_rg_wx), rg_bx=(rg_bx, m_rg_bx, v_rg_bx),
        rg_lambda=(rg_lambda, m_rg_lambda, v_rg_lambda), post_norm_w=(post_norm_w, m_post_norm_w, v_post_norm_w))
    deltas, new_m, new_v = {}, {}, {}
    for name, (w, m, v) in replicated.items():
        as2d = lambda a: a.reshape(-1, D) if a.ndim > 2 else a
        upd = _adamw("adamw_" + name, as2d(w), as2d(grads[name]), as2d(m), as2d(v))
        deltas[name], new_m[name], new_v[name] = [a.reshape(w.shape) for a in upd]

    p_aro = jnp.concatenate([part[k].reshape(N_DEV, D // N_DEV, D) for k in ("w_a", "w_r", "w_o")], axis=1)
    s_in, s_aro = _chip_exchange([_pair_reduce("pair_w_in", part["w_in_pieces"]),
                                  _pair_reduce("pair_w_aro", p_aro.astype(bf16))])
    f_in, f_aro = _swap_halves([_sum_slots("sum_w_in", s_in), _sum_slots("sum_w_aro", s_aro)])
    g_w_in_t = f_in.reshape(padded, D)[:shard_cols]
    rows = D // N_DEV
    g_aro = [f_aro[:, i * rows:(i + 1) * rows, :].reshape(2 * rows, D) for i in range(3)]

    w_in_upd = _adamw("adamw_w_in", w_in_t, g_w_in_t, jnp.transpose(m_w_in[0]), jnp.transpose(v_w_in[0]))
    g_w_in, d_w_in, nm_w_in, nv_w_in = [jnp.transpose(a) for a in (g_w_in_t, *w_in_upd)]
    upd_a = _adamw("adamw_w_branch_a", w_branch_a[0], g_aro[0], m_w_branch_a[0], v_w_branch_a[0])
    upd_r = _adamw("adamw_w_branch_r", w_branch_r[0], g_aro[1], m_w_branch_r[0], v_w_branch_r[0])
    upd_o = _adamw("adamw_w_out", w_out[0], g_aro[2], m_w_out[0], v_w_out[0])
    d_aro, nm_aro, nv_aro = zip(upd_a, upd_r, upd_o)
    d_cw, nm_cw, nv_cw = _adamw("adamw_conv_w", conv_w[0], g_conv_w, m_conv_w[0], v_conv_w[0])

    def sharded(t_in, t_aro, t_cw):
        return dict(w_in=t_in[None], conv_w=t_cw[None], w_branch_a=t_aro[0][None], w_branch_r=t_aro[1][None],
                    w_out=t_aro[2][None])

    order = ["pre_norm_w", "w_in", "b_in", "conv_w", "conv_b", "rg_wa", "rg_ba", "rg_wx", "rg_bx", "rg_lambda",
             "w_branch_a", "w_branch_r", "w_out", "post_norm_w"]
    outs = [loss, grad_x]
    for rep, shd in ((grads, sharded(g_w_in, g_aro, g_conv_w)), (deltas, sharded(d_w_in, d_aro, d_cw)),
                     (new_m, sharded(nm_w_in, nm_aro, nm_cw)), (new_v, sharded(nv_w_in, nv_aro, nv_cw))):
        both = {**rep, **shd}
        outs.extend(both[k] for k in order)
    return tuple(outs)
```

```python
import jax
import jax.numpy as jnp
from jax import lax
from jax.experimental import pallas as pl
from jax.experimental.pallas import tpu as pltpu

f32 = jnp.float32
bf16 = jnp.bfloat16

D = 1024
HEADS = 16
HEAD_PAIRS = 8
LANES = 128
NORM_EPS = 1e-6
MASK_VALUE = -1e30
RG_C = 8.0
QK_SCALE = 0.125
TQ = 256
ATT_GROUP = 8
ATT_GROUP_FWD = 16
TL = 512
TM = 512
PREV_ROWS = 16
IN_USED = 8 * D + HEADS
IN_TOTAL = 9 * D + HEADS
N_CHIPS = 4
N_DEV = 8
ADAM_LR, ADAM_B1, ADAM_B2, ADAM_EPS, ADAM_WD, ADAM_STEP = 0.001, 0.9, 0.999, 1e-08, 0.01, 10
VMEM_LIMIT = 56 * 1024 * 1024
MESH = pl.DeviceIdType.MESH


def _dot(a, b):
    return jnp.dot(a, b, preferred_element_type=f32)


def _dot_nt(a, b):
    return lax.dot_general(a, b, (((1,), (1,)), ((), ())), preferred_element_type=f32)


def _dot_tn(a, b):
    return lax.dot_general(a, b, (((0,), (0,)), ((), ())), preferred_element_type=f32)


def _sig(x):
    return 0.5 * jnp.tanh(0.5 * x) + 0.5


def _softplus(x):
    return jnp.maximum(x, 0.0) + jnp.log(1.0 + jnp.exp(-jnp.abs(x)))


def _params(sem, vmem=None):
    return pltpu.CompilerParams(dimension_semantics=sem, vmem_limit_bytes=vmem)


def _tile(tm, width, cb=0):
    return pl.BlockSpec((tm, width), lambda i, cb=cb: (i, cb))


def _whole(shape):
    nd = len(shape)
    return pl.BlockSpec(shape, lambda *_: (0,) * nd)


def _norm_qkv(x, w_pre, w_all, b_qkv, tm=1024):
    t = x.shape[0]
    tm = min(tm, t)
    n = b_qkv.shape[1]

    def body(x_ref, wp_ref, w_ref, b_ref, h_ref, o_ref):
        @pl.when(pl.program_id(1) == 0)
        def _():
            xv = x_ref[...]
            r = lax.rsqrt(jnp.mean(xv * xv, axis=-1, keepdims=True) + NORM_EPS)
            h_ref[...] = (xv * r * wp_ref[...]).astype(bf16)

        o_ref[...] = (_dot_nt(h_ref[...], w_ref[...]) + b_ref[...]).astype(bf16)

    return pl.pallas_call(
        body, name="norm_qkv", grid=(t // tm, n // D),
        in_specs=[pl.BlockSpec((tm, D), lambda i, j: (i, 0)), _whole((1, D)), pl.BlockSpec((D, D), lambda i, j: (j, 0)),
                  pl.BlockSpec((1, D), lambda i, j: (0, j))],
        out_specs=[pl.BlockSpec((tm, D), lambda i, j: (i, 0)), pl.BlockSpec((tm, D), lambda i, j: (i, j))],
        out_shape=[jax.ShapeDtypeStruct((t, D), bf16), jax.ShapeDtypeStruct((t, n), bf16)],
        compiler_params=_params(("parallel", "arbitrary"), VMEM_LIMIT),
    )(x, w_pre, w_all, b_qkv)


def _mm(name, a, w, w_rows, bias, out_dtype, tm, tn):
    t, k = a.shape
    tm = min(tm, t)
    row0, n = w_rows
    assert row0 % tn == 0

    def body(a_ref, w_ref, b_ref, o_ref):
        o_ref[...] = (_dot_nt(a_ref[...], w_ref[...]) + b_ref[...]).astype(out_dtype)

    return pl.pallas_call(
        body, name=name, grid=(t // tm, n // tn),
        in_specs=[pl.BlockSpec((tm, k), lambda i, j: (i, 0)), pl.BlockSpec((tn, k), lambda i, j: (row0 // tn + j, 0)),
                  pl.BlockSpec((1, tn), lambda i, j: (0, j))],
        out_specs=pl.BlockSpec((tm, tn), lambda i, j: (i, j)), out_shape=jax.ShapeDtypeStruct((t, n), out_dtype),
        compiler_params=_params(("parallel", "parallel"), VMEM_LIMIT),
    )(a, w, bias)


def _forget_prep(f128, seq):
    t = f128.shape[0]
    nb = seq // LANES

    def body(f_ref, c_ref):
        r = lax.broadcasted_iota(jnp.int32, (LANES, LANES), 0)
        cidx = lax.broadcasted_iota(jnp.int32, (LANES, LANES), 1)
        tri = (r >= cidx).astype(f32)
        carry = jnp.zeros((1, LANES), f32)
        for blk in range(nb):
            fv = f_ref[pl.ds(blk * LANES, LANES), :]
            lf = -_softplus(-fv)
            c_ref[pl.ds(blk * LANES, LANES), :] = (
                jnp.dot(tri, lf, preferred_element_type=f32, precision=lax.Precision.HIGHEST) + carry)
            carry = carry + jnp.sum(lf, axis=0, keepdims=True)

    return pl.pallas_call(
        body, name="forget_prep", grid=(t // seq,),
        in_specs=[pl.BlockSpec((seq, LANES), lambda b: (b, 0))],
        out_specs=pl.BlockSpec((seq, LANES), lambda b: (b, 0)),
        out_shape=jax.ShapeDtypeStruct((t, LANES), f32),
        compiler_params=_params(("parallel",)),
    )(f128)


def _split3(cv):
    hi = cv.astype(bf16)
    r1 = cv - hi.astype(f32)
    mid = r1.astype(bf16)
    lo = (r1 - mid.astype(f32)).astype(bf16)
    return hi, mid, lo


def _attn_prep(qkv, c):
    t = qkv.shape[0]

    def body(q_ref, k_ref, c_ref, qa_ref, ka_ref):
        lane = lax.broadcasted_iota(jnp.int32, (1, LANES), 1)
        cv = c_ref[...]
        one = jnp.ones((), bf16)
        zero = jnp.zeros((), bf16)
        q_ones = jnp.where((lane >= 67) & (lane < 70), one, zero)
        k_ones = jnp.where((lane >= 64) & (lane < 67), one, zero)
        for head in range(HEADS):
            pair = pl.ds((head // 2) * LANES, LANES)
            ch = jnp.sum(jnp.where(lane == head, cv, 0.0), axis=1, keepdims=True)
            hi, mid, lo = _split3(ch)
            q2, k2 = q_ref[:, pair], k_ref[:, pair]
            if head % 2 == 1:
                q2, k2 = pltpu.roll(q2, 64, 1), pltpu.roll(k2, 64, 1)
            qa = jnp.where(lane < 64, q2 * jnp.asarray(QK_SCALE, bf16),
                           jnp.where(lane == 64, hi, jnp.where(lane == 65, mid, jnp.where(lane == 66, lo, q_ones))))
            ka = jnp.where(lane < 64, k2,
                           jnp.where(lane == 67, -hi, jnp.where(lane == 68, -mid, jnp.where(lane == 69, -lo, k_ones))))
            qa_ref[:, pl.ds(head * LANES, LANES)] = qa
            ka_ref[:, pl.ds(head * LANES, LANES)] = ka

    tm = min(TM, t)
    out = pl.BlockSpec((tm, 2 * D), lambda i: (i, 0))
    return pl.pallas_call(
        body, name="attn_prep", grid=(t // tm,),
        in_specs=[_tile(tm, D, 0), _tile(tm, D, 1), _tile(tm, LANES)],
        out_specs=[out, out],
        out_shape=[jax.ShapeDtypeStruct((t, 2 * D), bf16)] * 2,
        compiler_params=_params(("parallel",)),
    )(qkv, qkv, c)


def _attn_fwd(qa, ka, qkv, rest, seq):
    t = qkv.shape[0]
    nb, nq = t // seq, seq // TQ

    hg = ATT_GROUP_FWD
    ng = HEADS // hg

    def body(q_ref, k_ref, v_ref, ga_ref, o_ref, pa_ref, lse_ref, acc_scr):
        qi, gi = pl.program_id(1), pl.program_id(2)
        krow = lax.broadcasted_iota(jnp.int32, (TQ, TQ), 0)
        qcol = lax.broadcasted_iota(jnp.int32, (TQ, TQ), 1)
        acc_scr[...] = jnp.zeros_like(acc_scr)

        def kv_step(kt, carry, masked):
            ks = pl.multiple_of(kt * TQ, TQ)
            sts = [_dot_nt(k_ref[pl.ds(ks, TQ), pl.ds(g * LANES, LANES)], q_ref[:, pl.ds(g * LANES, LANES)])
                   for g in range(hg)]
            if masked:
                sts = [jnp.where(krow <= qcol, st, MASK_VALUE) for st in sts]
            m_new = [jnp.maximum(carry[g][0], jnp.max(sts[g], axis=0, keepdims=True)) for g in range(hg)]
            ps = [jnp.exp(sts[g] - m_new[g]) for g in range(hg)]
            alphas = [jnp.exp(carry[g][0] - m_new[g]) for g in range(hg)]
            phi = [ps[g].astype(bf16) for g in range(hg)]
            plo = [(ps[g] - phi[g].astype(f32)).astype(bf16) for g in range(hg)]
            vs = [v_ref[pl.ds(ks, TQ), pl.ds(j * LANES, LANES)] for j in range(hg // 2)]
            pvs = [_dot_tn(vs[g // 2], phi[g]) + _dot_tn(vs[g // 2], plo[g]) for g in range(hg)]
            olds = [acc_scr[g] for g in range(hg)]
            for g in range(hg):
                acc_scr[g] = alphas[g] * olds[g] + pvs[g]
            return tuple((m_new[g], alphas[g] * carry[g][1] + jnp.sum(ps[g], axis=0, keepdims=True))
                         for g in range(hg))

        init = tuple((jnp.full((1, TQ), MASK_VALUE, f32), jnp.zeros((1, TQ), f32)) for _ in range(hg))
        carry = lax.fori_loop(0, qi, lambda kt, cr: kv_step(kt, cr, False), init)
        stats = kv_step(qi, carry, True)
        drow = lax.broadcasted_iota(jnp.int32, (LANES, TQ), 0)
        for g in range(hg):
            m, l = stats[g]
            lse_ref[0, pl.ds(hg * gi + g, 1), :] = m + jnp.log(l)
        for j in range(hg // 2):
            o2 = jnp.where(drow < 64, acc_scr[2 * j] / stats[2 * j][1], acc_scr[2 * j + 1] / stats[2 * j + 1][1]).T
            o_ref[:, pl.ds(j * LANES, LANES)] = o2
            ga = ga_ref[:, pl.ds(j * LANES, LANES)].astype(f32)
            pa_ref[:, pl.ds(j * LANES, LANES)] = (o2 * (ga * _sig(ga))).astype(bf16)

    vw = hg * 64
    tile = pl.BlockSpec((TQ, vw), lambda b, qi, gi: (b * nq + qi, gi))
    return pl.pallas_call(
        body, name="attn_fwd", grid=(nb, nq, ng),
        in_specs=[pl.BlockSpec((TQ, hg * LANES), lambda b, qi, gi: (b * nq + qi, gi)),
                  pl.BlockSpec((seq, hg * LANES), lambda b, qi, gi: (b, gi)),
                  pl.BlockSpec((seq, vw), lambda b, qi, gi: (b, 2 * ng + gi)), tile],
        out_specs=[tile, tile, pl.BlockSpec((1, HEADS, TQ), lambda b, qi, gi: (b * nq + qi, 0, 0))],
        out_shape=[jax.ShapeDtypeStruct((t, D), f32), jax.ShapeDtypeStruct((t, D), bf16),
                   jax.ShapeDtypeStruct((t // TQ, HEADS, TQ), f32)],
        scratch_shapes=[pltpu.VMEM((hg, LANES, TQ), f32)],
        compiler_params=_params(("parallel", "parallel", "arbitrary"), VMEM_LIMIT),
    )(qa, ka, qkv, rest)


def _shifted_rows(x, top8, prev8, shift, row, row8):
    body = pltpu.roll(x, shift, 0)
    head = jnp.where(row8 < shift, pltpu.roll(prev8, shift, 0), pltpu.roll(top8, shift, 0))
    return body, head


def _rnn_gates(xc, wa_ref, wx_ref, ba_ref, bx_ref, lam_ref):
    xcb = xc.astype(bf16)
    r = _sig(_dot(xcb, wa_ref[...]) + ba_ref[...])
    i = _sig(_dot(xcb, wx_ref[...]) + bx_ref[...])
    sp = _softplus(-lam_ref[...])
    log_a = (-RG_C) * r * sp
    th = jnp.tanh(log_a)
    w1 = (-2.0) * th / (1.0 - th)
    sq = jnp.sqrt(jnp.maximum(w1, 0.0))
    return r, i, sp, log_a, w1, sq


def _conv_tile(x_ref, xprev_ref, has_prev, cw_ref, cb_ref, xc_ref):
    row = lax.broadcasted_iota(jnp.int32, (TL, D), 0)
    row8 = lax.broadcasted_iota(jnp.int32, (8, D), 0)
    x = x_ref[...].astype(f32)
    top8 = x[:8]
    prev8 = jnp.where(has_prev, xprev_ref[...].astype(f32)[PREV_ROWS - 8:], 0.0)
    xc = cb_ref[...] + cw_ref[pl.ds(3, 1), :] * x
    xc8 = cb_ref[...] + cw_ref[pl.ds(3, 1), :] * top8
    for sh in range(1, 4):
        w = cw_ref[pl.ds(3 - sh, 1), :]
        xs, xs8 = _shifted_rows(x, top8, prev8, sh, row, row8)
        xc = xc + w * xs
        xc8 = xc8 + w * xs8
    xc_ref[...] = xc
    xc_ref[pl.ds(0, 8), :] = xc8


def _rnn_fwd(rest, conv_w, conv_b, wa_d, wx_d, ba, bx, lam, seq):
    t = rest.shape[0]
    nb, nt = t // seq, seq // TL

    def body(x_ref, xprev_ref, gr_ref, cw_ref, cb_ref, wa_ref, wx_ref, ba_ref, bx_ref, lam_ref,
             xc_ref, a_ref, h_ref, pr_ref, xc_scr, u_scr, h_scr, carry):
        tt = pl.program_id(1)
        _conv_tile(x_ref, xprev_ref, tt > 0, cw_ref, cb_ref, xc_scr)
        xc = xc_scr[...]
        xc_ref[...] = xc.astype(bf16)
        r, i, sp, log_a, w1, sq = _rnn_gates(xc, wa_ref, wx_ref, ba_ref, bx_ref, lam_ref)
        a_ref[...] = jnp.exp(log_a)
        u_scr[...] = sq * (i * xc)

        @pl.when(tt == 0)
        def _():
            carry[...] = jnp.zeros_like(carry)

        def step(s, h):
            h = a_ref[pl.ds(s, 1), :] * h + u_scr[pl.ds(s, 1), :]
            h_scr[pl.ds(s, 1), :] = h
            return h

        carry[...] = lax.fori_loop(0, TL, step, carry[...], unroll=8)
        gr = gr_ref[...].astype(f32)
        h = h_scr[...]
        h_ref[...] = h.astype(bf16)
        pr_ref[...] = (h * (gr * _sig(gr))).astype(bf16)

    tile = lambda cb: pl.BlockSpec((TL, D), lambda b, tt, cb=cb: (b * nt + tt, cb))
    prev = lambda cb: pl.BlockSpec(
        (PREV_ROWS, D), lambda b, tt, cb=cb: (jnp.maximum((b * nt + tt) * (TL // PREV_ROWS) - 1, 0), cb))
    vec = _whole((1, D))
    return pl.pallas_call(
        body, name="rnn_fwd", grid=(nb, nt),
        in_specs=[tile(1), prev(1), tile(2), _whole((4, D)), vec, _whole((D, D)), _whole((D, D)), vec, vec, vec],
        out_specs=[tile(0)] * 4,
        out_shape=[jax.ShapeDtypeStruct((t, D), dt) for dt in (bf16, f32, bf16, bf16)],
        scratch_shapes=[pltpu.VMEM((TL, D), f32)] * 3 + [pltpu.VMEM((1, D), f32)],
        compiler_params=_params(("parallel", "arbitrary"), VMEM_LIMIT),
    )(rest, rest, rest, conv_w, conv_b, wa_d, wx_d, ba, bx, lam)


def _merge_loss(rest, pa, pr, o_att, hrec, w_a, w_r, w_out, x, tgt, w_post):
    t = x.shape[0]

    def branch(dy, w_ref, g_ref, act):
        dp = _dot_nt(dy, w_ref[...])
        g = g_ref[...].astype(f32)
        sg = _sig(g)
        return (dp * (g * sg)).astype(bf16), (dp * act * (sg * (1.0 + g * (1.0 - sg)))).astype(bf16)

    def body(mga_ref, mgr_ref, pa_ref, pr_ref, ga_ref, gr_ref, oa_ref, h_ref, x_ref, t_ref, wa_ref, wr_ref, wo_ref,
             w_ref, do_ref, dya_ref, dyr_ref, dmga_ref, dmgr_ref, doa_ref, dga_ref, dh_ref, dgr_ref, mrg_ref, dy_ref,
             delta_ref, loss_ref, dwp_ref):
        @pl.when(pl.program_id(0) == 0)
        def _():
            loss_ref[...] = jnp.zeros_like(loss_ref)
            dwp_ref[...] = jnp.zeros_like(dwp_ref)

        sa, sr = _sig(mga_ref[...].astype(f32)), _sig(mgr_ref[...].astype(f32))
        ya, yr = _dot(pa_ref[...], wa_ref[...]), _dot(pr_ref[...], wr_ref[...])
        mrg = (sa * ya + sr * yr).astype(bf16)
        mrg_ref[...] = mrg
        ov = _dot(mrg, wo_ref[...])
        w = w_ref[...]
        r2 = lax.rsqrt(jnp.mean(ov * ov, axis=-1, keepdims=True) + NORM_EPS)
        oh = ov * r2
        e = x_ref[...] + oh * w - t_ref[...]
        loss_ref[...] += 0.5 * jnp.sum(jnp.mean(e * e, axis=-1, keepdims=True))
        dy = e * (1.0 / D)
        dy_ref[...] = dy
        dwp_ref[...] += jnp.sum(dy * oh, axis=0, keepdims=True)
        doh = dy * w
        do = (r2 * (doh - oh * jnp.mean(doh * oh, axis=-1, keepdims=True))).astype(bf16)
        do_ref[...] = do

        dm = _dot_nt(do, wo_ref[...])
        dya, dyr = (dm * sa).astype(bf16), (dm * sr).astype(bf16)
        dya_ref[...] = dya
        dyr_ref[...] = dyr
        dmga_ref[...] = (dm * ya * sa * (1.0 - sa)).astype(bf16)
        dmgr_ref[...] = (dm * yr * sr * (1.0 - sr)).astype(bf16)
        o_att = oa_ref[...]
        doa, dga_ref[...] = branch(dya, wa_ref, ga_ref, o_att)
        doa_ref[...] = doa
        dh_ref[...], dgr_ref[...] = branch(dyr, wr_ref, gr_ref, h_ref[...].astype(f32))
        ch = lax.broadcasted_iota(jnp.int32, (D, LANES), 0)
        hd = lax.broadcasted_iota(jnp.int32, (D, LANES), 1)
        pick = (ch // 64 == hd).astype(bf16)
        per_head = sum(_dot(piece, pick) for piece in _split3(doa.astype(f32) * o_att))
        delta_ref[0] = per_head.T[:HEADS, :]

    once = pl.BlockSpec((D, D), lambda i: (0, 0), pipeline_mode=pl.Buffered(1))
    rows = _tile(TQ, D)
    return pl.pallas_call(
        body, name="merge_loss", grid=(t // TQ,),
        in_specs=[_tile(TQ, D, 3), _tile(TQ, D, 4), rows, rows, _tile(TQ, D, 0), _tile(TQ, D, 2), rows, rows, rows, rows,
                  once, once, once, _whole((1, D))],
        out_specs=[rows] * 11 + [pl.BlockSpec((1, HEADS, TQ), lambda i: (i, 0, 0)), _whole((8, LANES)), _whole((1, D))],
        out_shape=[jax.ShapeDtypeStruct((t, D), bf16)] * 10 + [jax.ShapeDtypeStruct((t, D), f32),
                   jax.ShapeDtypeStruct((t // TQ, HEADS, TQ), f32), jax.ShapeDtypeStruct((8, LANES), f32),
                   jax.ShapeDtypeStruct((1, D), f32)],
        compiler_params=_params(("arbitrary",), VMEM_LIMIT),
    )(rest, rest, pa, pr, rest, rest, o_att, hrec, x, tgt, w_a, w_r, w_out, w_post)


def _rnn_bwd(dh, a, h, xc, rest, conv_w, conv_b, wa_d, wx_d, ba, bx, lam, seq):
    t = dh.shape[0]
    nb, nt = t // seq, seq // TL
    diag = (D // LANES, LANES, LANES)

    def body(dh_ref, a_ref, h_ref, hprev_ref, xc_ref, x_ref, xprev_ref, cw_ref, cb_ref, wa_ref, wx_ref,
             ba_ref, bx_ref, lam_ref, dxr_ref, dwa_ref, dwx_ref, vec_ref, g_scr, dxc_scr, dxr_scr, qcarry, dxc_next):
        b, tt = pl.program_id(0), pl.program_id(1)
        rt = nt - 1 - tt

        @pl.when((b == 0) & (tt == 0))
        def _():
            dwa_ref[...] = jnp.zeros_like(dwa_ref)
            dwx_ref[...] = jnp.zeros_like(dwx_ref)
            vec_ref[...] = jnp.zeros_like(vec_ref)

        @pl.when(tt == 0)
        def _():
            qcarry[...] = jnp.zeros_like(qcarry)
            dxc_next[...] = jnp.zeros_like(dxc_next)

        g_scr[...] = dh_ref[...].astype(f32)

        def step(k, q):
            s = TL - 1 - k
            g = g_scr[pl.ds(s, 1), :] + q
            g_scr[pl.ds(s, 1), :] = g
            return a_ref[pl.ds(s, 1), :] * g

        qcarry[...] = lax.fori_loop(0, TL, step, qcarry[...], unroll=8)

        row = lax.broadcasted_iota(jnp.int32, (TL, D), 0)
        row8 = lax.broadcasted_iota(jnp.int32, (8, D), 0)
        g = g_scr[...]
        av = a_ref[...]
        xc = xc_ref[...].astype(f32)
        hlast = jnp.where(rt > 0, hprev_ref[...].astype(f32)[PREV_ROWS - 1:], 0.0)
        hp = jnp.where(row == 0, hlast, pltpu.roll(h_ref[...].astype(f32), 1, 0))
        r, i, sp, log_a, w1, sq = _rnn_gates(xc, wa_ref, wx_ref, ba_ref, bx_ref, lam_ref)
        dix = g * sq
        di = dix * xc
        dxc = dix * i
        dsq = g * (i * xc)
        dlog_a = g * hp * av - dsq * jnp.where(sq > 0.0, (1.0 - w1) / sq, 0.0)
        dpr = (dlog_a * ((-RG_C) * sp)) * r * (1.0 - r)
        dpi = di * i * (1.0 - i)
        dprb, dpib, xcb = dpr.astype(bf16), dpi.astype(bf16), xc.astype(bf16)
        dxc = dxc + _dot_nt(dprb, wa_ref[...]) + _dot_nt(dpib, wx_ref[...])
        for j in range(D // LANES):
            cols = slice(j * LANES, (j + 1) * LANES)
            dwa_ref[j] += _dot_tn(xcb[:, cols], dprb[:, cols])
            dwx_ref[j] += _dot_tn(xcb[:, cols], dpib[:, cols])
        vec_ref[pl.ds(0, 1), :] += jnp.sum(dpr, axis=0, keepdims=True)
        vec_ref[pl.ds(1, 1), :] += jnp.sum(dpi, axis=0, keepdims=True)
        dsp = jnp.sum(dlog_a * ((-RG_C) * r), axis=0, keepdims=True)
        vec_ref[pl.ds(2, 1), :] += dsp * (-_sig(-lam_ref[...]))
        vec_ref[pl.ds(3, 1), :] += jnp.sum(dxc, axis=0, keepdims=True)

        dxc_scr[...] = dxc
        bot8 = dxc_scr[pl.ds(TL - 8, 8), :]
        nxt8 = dxc_next[...]
        dxr = cw_ref[pl.ds(3, 1), :] * dxc
        dxr8 = cw_ref[pl.ds(3, 1), :] * bot8
        for sh in range(1, 4):
            w = cw_ref[pl.ds(3 - sh, 1), :]
            dxr = dxr + w * pltpu.roll(dxc, TL - sh, 0)
            dxr8 = dxr8 + w * jnp.where(row8 < 8 - sh, pltpu.roll(bot8, 8 - sh, 0), pltpu.roll(nxt8, 8 - sh, 0))
        dxr_scr[...] = dxr
        dxr_scr[pl.ds(TL - 8, 8), :] = dxr8
        dxr_ref[...] = dxr_scr[...].astype(bf16)
        dxc_next[...] = dxc_scr[pl.ds(0, 8), :]

        x = x_ref[...].astype(f32)
        prev8 = jnp.where(rt > 0, xprev_ref[...].astype(f32)[PREV_ROWS - 8:], 0.0)
        dxc_top8 = dxc_scr[pl.ds(0, 8), :]
        vec_ref[pl.ds(7, 1), :] += jnp.sum(dxc * x, axis=0, keepdims=True)
        for sh in range(1, 4):
            inside = jnp.sum(dxc * jnp.where(row >= sh, pltpu.roll(x, sh, 0), 0.0), axis=0, keepdims=True)
            above = jnp.sum(dxc_top8 * jnp.where(row8 < sh, pltpu.roll(prev8, sh, 0), 0.0), axis=0, keepdims=True)
            vec_ref[pl.ds(7 - sh, 1), :] += inside + above

    tile = lambda cb: pl.BlockSpec((TL, D), lambda b, tt, cb=cb: (b * nt + nt - 1 - tt, cb))
    prev = lambda cb: pl.BlockSpec(
        (PREV_ROWS, D), lambda b, tt, cb=cb: (jnp.maximum((b * nt + nt - 1 - tt) * (TL // PREV_ROWS) - 1, 0), cb))
    vec = _whole((1, D))
    return pl.pallas_call(
        body, name="rnn_bwd", grid=(nb, nt),
        in_specs=[tile(0), tile(0), tile(0), prev(0), tile(0), tile(1), prev(1),
                  _whole((4, D)), vec, _whole((D, D)), _whole((D, D)), vec, vec, vec],
        out_specs=[tile(0), _whole(diag), _whole(diag), _whole((8, D))],
        out_shape=[jax.ShapeDtypeStruct((t, D), bf16), jax.ShapeDtypeStruct(diag, f32),
                   jax.ShapeDtypeStruct(diag, f32), jax.ShapeDtypeStruct((8, D), f32)],
        scratch_shapes=[pltpu.VMEM((TL, D), f32), pltpu.VMEM((TL, D), f32), pltpu.VMEM((TL, D), f32),
                        pltpu.VMEM((1, D), f32), pltpu.VMEM((8, D), f32)],
        compiler_params=_params(("arbitrary", "arbitrary"), VMEM_LIMIT),
    )(dh, a, h, h, xc, rest, rest, conv_w, conv_b, wa_d, wx_d, ba, bx, lam)


def _attn_bwd(qa, ka, qkv, doa, lse, delta, seq):
    t = qkv.shape[0]
    nb, nq = t // seq, seq // TQ
    hg = ATT_GROUP
    ng, npair = HEADS // hg, hg // 2

    def body(qa_ref, ka_ref, q_ref, k_ref, v_ref, do_ref, lse_ref, dl_ref, dq_ref, dk_ref, dv_ref, dc_ref,
             dqt_scr, dk_scr, dv_scr, ds_scr, kht_scr):
        gi, kt = pl.program_id(1), pl.program_id(2)
        lane = lax.broadcasted_iota(jnp.int32, (1, LANES), 1)
        krow = lax.broadcasted_iota(jnp.int32, (TQ, TQ), 0)
        qcol = lax.broadcasted_iota(jnp.int32, (TQ, TQ), 1)
        lmask = [(lane // 64) == hh for hh in range(2)]
        scale = jnp.asarray(QK_SCALE, bf16)

        @pl.when(kt == 0)
        def _():
            dqt_scr[...] = jnp.zeros_like(dqt_scr)

        dk_scr[...] = jnp.zeros_like(dk_scr)
        dv_scr[...] = jnp.zeros_like(dv_scr)
        ds_scr[...] = jnp.zeros_like(ds_scr)
        for g in range(hg):
            k2 = k_ref[:, pl.ds((g // 2) * LANES, LANES)]
            kht_scr[g] = jnp.where(lmask[g % 2], k2, jnp.zeros_like(k2)).T

        def q_step(qt, masked):
            qs = pl.multiple_of(qt * TQ, TQ)
            heads = range(hg)
            do2 = [do_ref[pl.ds(qs, TQ), pl.ds(j * LANES, LANES)] for j in range(npair)]
            q2 = [q_ref[pl.ds(qs, TQ), pl.ds(j * LANES, LANES)] for j in range(npair)]
            doh = [jnp.where(lmask[g % 2], do2[g // 2], jnp.zeros_like(do2[0])) for g in heads]
            qh = [jnp.where(lmask[g % 2], q2[g // 2], jnp.zeros_like(q2[0])) * scale for g in heads]
            st = [_dot_nt(ka_ref[:, pl.ds(g * LANES, LANES)], qa_ref[pl.ds(qs, TQ), pl.ds(g * LANES, LANES)])
                  for g in heads]
            if masked:
                st = [jnp.where(krow <= qcol, s, MASK_VALUE) for s in st]
            dp = [_dot_nt(v_ref[:, pl.ds((g // 2) * LANES, LANES)], doh[g]) for g in heads]
            p = [jnp.exp(st[g] - lse_ref[qt, pl.ds(hg * gi + g, 1), :]) for g in heads]
            ds = [p[g] * (dp[g] - dl_ref[qt, pl.ds(hg * gi + g, 1), :]) for g in heads]
            pb = [x.astype(bf16) for x in p]
            dsb = [x.astype(bf16) for x in ds]
            for j in range(npair):
                a, b = 2 * j, 2 * j + 1
                dv_scr[j] += _dot(pb[a], doh[a]) + _dot(pb[b], doh[b])
                dk_scr[j] += _dot(dsb[a], qh[a]) + _dot(dsb[b], qh[b])
                dqt_scr[qt, j] += (_dot(kht_scr[a], dsb[a]) + _dot(kht_scr[b], dsb[b])) * QK_SCALE
            for g in heads:
                ds_scr[g] += ds[g][:, :LANES] + ds[g][:, LANES:]

        q_step(kt, True)

        def loop_body(qt, carry):
            q_step(qt, False)
            return carry

        lax.fori_loop(kt + 1, nq, loop_body, 0)

        dc = jnp.zeros((TQ, LANES), f32)
        for g in range(hg):
            dc = jnp.where(lane == g, -jnp.sum(ds_scr[g], axis=1, keepdims=True), dc)
        dc_ref[...] = dc
        for j in range(npair):
            dk_ref[:, pl.ds(j * LANES, LANES)] = dk_scr[j].astype(bf16)
            dv_ref[:, pl.ds(j * LANES, LANES)] = dv_scr[j].astype(bf16)

        @pl.when(kt == nq - 1)
        def _():
            for qt in range(nq):
                for j in range(npair):
                    dq_ref[pl.ds(qt * TQ, TQ), pl.ds(j * LANES, LANES)] = dqt_scr[qt, j].T.astype(bf16)

    vw = hg * 64
    seqspec = pl.BlockSpec((seq, vw), lambda b, gi, kt: (b, gi))
    kspec = lambda off: pl.BlockSpec((TQ, vw), lambda b, gi, kt: (b * nq + kt, off + gi))
    rowspec = pl.BlockSpec((nq, HEADS, TQ), lambda b, gi, kt: (b, 0, 0))
    return pl.pallas_call(
        body, name="attn_bwd", grid=(nb, ng, nq),
        in_specs=[pl.BlockSpec((seq, hg * LANES), lambda b, gi, kt: (b, gi)),
                  pl.BlockSpec((TQ, hg * LANES), lambda b, gi, kt: (b * nq + kt, gi)),
                  seqspec, kspec(ng), kspec(2 * ng), seqspec, rowspec, rowspec],
        out_specs=[seqspec, kspec(0), kspec(0), pl.BlockSpec((TQ, LANES), lambda b, gi, kt: (b * nq + kt, gi))],
        out_shape=[jax.ShapeDtypeStruct((t, D), bf16)] * 3 + [jax.ShapeDtypeStruct((t, ng * LANES), f32)],
        scratch_shapes=[pltpu.VMEM((nq, npair, LANES, TQ), f32), pltpu.VMEM((npair, TQ, LANES), f32),
                        pltpu.VMEM((npair, TQ, LANES), f32), pltpu.VMEM((hg, TQ, LANES), f32),
                        pltpu.VMEM((hg, LANES, TQ), bf16)],
        compiler_params=_params(("parallel", "parallel", "arbitrary"), VMEM_LIMIT),
    )(qa, ka, qkv, qkv, qkv, doa, lse, delta)


def _forget_bwd(dc, f128, seq):
    t = f128.shape[0]
    nb = seq // LANES
    groups = dc.shape[1] // LANES

    def body(dc_ref, f_ref, df_ref, dbf_ref):
        @pl.when(pl.program_id(0) == 0)
        def _():
            dbf_ref[...] = jnp.zeros_like(dbf_ref)

        r = lax.broadcasted_iota(jnp.int32, (LANES, LANES), 0)
        cidx = lax.broadcasted_iota(jnp.int32, (LANES, LANES), 1)
        tri = (r <= cidx).astype(f32)
        carry = jnp.zeros((1, LANES), f32)
        total = jnp.zeros((1, LANES), f32)
        for blk in reversed(range(nb)):
            dcb = dc_ref[pl.ds(blk * LANES, LANES), pl.ds(0, LANES)]
            for gi in range(1, groups):
                dcb = dcb + pltpu.roll(dc_ref[pl.ds(blk * LANES, LANES), pl.ds(gi * LANES, LANES)], gi * ATT_GROUP, 1)
            dlf = jnp.dot(tri, dcb, preferred_element_type=f32, precision=lax.Precision.HIGHEST) + carry
            df = dlf * _sig(-f_ref[pl.ds(blk * LANES, LANES), :])
            df_ref[pl.ds(blk * LANES, LANES), :] = df.astype(bf16)
            total = total + jnp.sum(df, axis=0, keepdims=True)
            carry = carry + jnp.sum(dcb, axis=0, keepdims=True)
        dbf_ref[...] += total

    return pl.pallas_call(
        body, name="forget_bwd", grid=(t // seq,),
        in_specs=[pl.BlockSpec((seq, groups * LANES), lambda b: (b, 0)), pl.BlockSpec((seq, LANES), lambda b: (b, 0))],
        out_specs=[pl.BlockSpec((seq, LANES), lambda b: (b, 0)), _whole((1, LANES))],
        out_shape=[jax.ShapeDtypeStruct((t, LANES), bf16), jax.ShapeDtypeStruct((1, LANES), f32)],
        compiler_params=_params(("arbitrary",)),
    )(dc, f128)


def _in_bwd(dz, df, x, dy, w_all, w_pre):
    t = x.shape[0]
    n_dz = len(dz)

    def body(*refs):
        dz_refs = refs[:n_dz]
        df_ref, x_ref, dy_ref, w_ref, wp_ref, gx_ref, dwp_ref = refs[n_dz:]

        @pl.when(pl.program_id(0) == 0)
        def _():
            dwp_ref[...] = jnp.zeros_like(dwp_ref)

        dh = _dot(df_ref[...], w_ref[pl.ds(n_dz * D, LANES), :])
        for p in range(n_dz):
            dh = dh + _dot(dz_refs[p][...], w_ref[pl.ds(p * D, D), :])
        xv = x_ref[...]
        r1 = lax.rsqrt(jnp.mean(xv * xv, axis=-1, keepdims=True) + NORM_EPS)
        xh = xv * r1
        dwp_ref[...] += jnp.sum(dh * xh, axis=0, keepdims=True)
        dxh = dh * wp_ref[...]
        gx_ref[...] = dy_ref[...] + r1 * (dxh - xh * jnp.mean(dxh * xh, axis=-1, keepdims=True))

    once = lambda shape: pl.BlockSpec(shape, lambda i: (0, 0), pipeline_mode=pl.Buffered(1))
    return pl.pallas_call(
        body, name="in_bwd", grid=(t // TM,),
        in_specs=[_tile(TM, D)] * n_dz + [_tile(TM, LANES), _tile(TM, D), _tile(TM, D), once(w_all.shape),
                  _whole((1, D))],
        out_specs=[_tile(TM, D), _whole((1, D))],
        out_shape=[jax.ShapeDtypeStruct((t, D), f32), jax.ShapeDtypeStruct((1, D), f32)],
        compiler_params=_params(("arbitrary",), VMEM_LIMIT),
    )(*dz, df, x, dy, w_all, w_pre)


def _tn_mm(name, a, b, tn, out_dtype=f32, tk=2048):
    t, k = a.shape
    tk = min(tk, t)
    n = b.shape[1]
    nk = t // tk

    def body(a_ref, b_ref, o_ref, s_ref, acc_ref):
        j, kk = pl.program_id(0), pl.program_id(1)

        @pl.when(kk == 0)
        def _():
            acc_ref[...] = jnp.zeros_like(acc_ref)

        @pl.when((j == 0) & (kk == 0))
        def _():
            s_ref[...] = jnp.zeros_like(s_ref)

        av = a_ref[...]
        acc_ref[...] += _dot_tn(av, b_ref[...])

        @pl.when(j == 0)
        def _():
            s_ref[...] += jnp.sum(av.astype(f32), axis=0, keepdims=True)

        @pl.when(kk == nk - 1)
        def _():
            o_ref[...] = acc_ref[...].astype(out_dtype)

    return pl.pallas_call(
        body, name=name, grid=(n // tn, nk),
        in_specs=[pl.BlockSpec((tk, k), lambda j, kk: (kk, 0)), pl.BlockSpec((tk, tn), lambda j, kk: (kk, j))],
        out_specs=[pl.BlockSpec((k, tn), lambda j, kk: (0, j)), _whole((1, k))],
        out_shape=[jax.ShapeDtypeStruct((k, n), out_dtype), jax.ShapeDtypeStruct((1, k), f32)],
        scratch_shapes=[pltpu.VMEM((k, tn), f32)],
        compiler_params=_params(("arbitrary", "arbitrary"), VMEM_LIMIT),
    )(a, b)


def _position():
    return lax.axis_index("x"), lax.axis_index("y"), lax.axis_index("c")


ROW_BLOCK = 128


def _pick_rows(layout, first, count):
    acc = jnp.zeros((ROW_BLOCK, D), f32)
    seg_start = 0
    for ref, ref_row, rows in layout:
        lo, hi = max(first, seg_start), min(first + count, seg_start + rows)
        if lo < hi and ref is not None:
            off, take, done = ref_row + lo - seg_start, hi - lo, lo - first
            start = off // 16 * 16
            win = -(-(off - start + take) // 16) * 16
            r = lax.broadcasted_iota(jnp.int32, (ROW_BLOCK, win), 0)
            col = lax.broadcasted_iota(jnp.int32, (ROW_BLOCK, win), 1)
            pick = ((col - r == off - start - done) & (r >= done) & (r < done + take)).astype(bf16)
            acc = acc + _dot(pick, ref[pl.ds(start, win), :])
        seg_start += rows
    return acc


def _assemble_rows(shards_ref, shard_rows, segments, out_ref):
    layout = [(shards_ref.at[j], 0, shard_rows) for j in range(shards_ref.shape[0])]
    for out0, log0, count in segments:
        for b0 in range(0, count, ROW_BLOCK):
            block = _pick_rows(layout, log0 + b0, min(ROW_BLOCK, count - b0))
            out_ref[pl.ds(out0 + b0, ROW_BLOCK), :] = block.astype(bf16)


def _pack_pieces(blocks, shard_rows, padded):
    arrays = [a for a, _ in blocks if a is not None]
    piece_rows = padded // 2

    def body(*refs):
        out_ref = refs[-1]
        it = iter(refs[:-1])
        layout = [(None if a is None else next(it), 0, rows) for a, rows in blocks]
        for k in range(N_DEV):
            chip, half = divmod(k, 2)
            for b0 in range(0, piece_rows, ROW_BLOCK):
                n = min(ROW_BLOCK, piece_rows - b0)
                in_shard = half * piece_rows + b0
                count = max(0, min(n, shard_rows - in_shard))
                block = _pick_rows(layout, chip * shard_rows + in_shard, count)
                out_ref[k, pl.ds(b0, n), :] = block[:n].astype(bf16)

    vm = pl.BlockSpec(memory_space=pltpu.VMEM)
    return pl.pallas_call(
        body, name="pack_pieces", in_specs=[vm] * len(arrays), out_specs=vm,
        out_shape=jax.ShapeDtypeStruct((N_DEV, piece_rows, D), bf16),
        compiler_params=pltpu.CompilerParams(vmem_limit_bytes=VMEM_LIMIT),
    )(*arrays)


def _gather_shards(parts, small, shard_rows, segments, out_rows):
    n = len(parts)
    halves = [p.shape[0] // 2 for p in parts]
    cuts = [-(-h // 32) * 16 for h in halves]
    n_direct, n_relay, n_sib = 4 * n, 2 * n, 6 * n

    def body(*refs):
        srcs, small_src = refs[:n], refs[n]
        dsts, small_dst, whole_ref = refs[n + 1:2 * n + 1], refs[2 * n + 1], refs[2 * n + 2]
        send, recv, local = refs[2 * n + 3:]
        x, y, c = _position()
        me = 2 * x + y
        chips = [(1 - x, y), (x, 1 - y), (1 - x, 1 - y)]
        ids = [2 * px + py for px, py in chips]

        def rows(a, half, quarter):
            start = half * halves[a] + (cuts[a] if quarter else 0)
            return pl.ds(start, halves[a] - cuts[a] if quarter else cuts[a])

        def landing(a, shard, half, quarter):
            return dsts[a].at[shard, rows(a, half, quarter), :]

        def direct(a, nb, quarter, shard):
            k = (a * 2 + nb) * 2 + quarter
            px, py = chips[nb]
            return pltpu.make_async_remote_copy(
                src_ref=srcs[a].at[rows(a, c, quarter), :], dst_ref=landing(a, shard, c, quarter),
                send_sem=send.at[k], recv_sem=recv.at[k], device_id=(px, py, c), device_id_type=MESH)

        def relay(a, quarter, shard):
            k = n_direct + a * 2 + quarter
            px, py = chips[1 - quarter]
            return pltpu.make_async_remote_copy(
                src_ref=landing(a, shard, c, quarter), dst_ref=landing(a, shard, c, quarter),
                send_sem=send.at[k], recv_sem=recv.at[k], device_id=(px, py, c), device_id_type=MESH)

        def to_sibling(a, origin, quarter, half):
            k = n_direct + n_relay + (a * 3 + origin) * 2 + quarter
            return pltpu.make_async_remote_copy(
                src_ref=landing(a, ids[origin], half, quarter), dst_ref=landing(a, ids[origin], half, quarter),
                send_sem=send.at[k], recv_sem=recv.at[k], device_id=(x, y, 1 - c), device_id_type=MESH)

        def small_copy(j, shard):
            k = n_direct + n_relay + n_sib + j
            px, py = chips[j]
            return pltpu.make_async_remote_copy(
                src_ref=small_src, dst_ref=small_dst.at[shard], send_sem=send.at[k], recv_sem=recv.at[k],
                device_id=(px, py, c), device_id_type=MESH)

        own = [pltpu.make_async_copy(srcs[a], dsts[a].at[me], local.at[a]) for a in range(n)]
        own.append(pltpu.make_async_copy(small_src, small_dst.at[me], local.at[n]))
        for cp in own:
            cp.start()
        sent = [direct(a, nb, q, me) for q in range(2) for a in range(n) for nb in range(2)]
        sent += [small_copy(j, me) for j in range(3)]
        for cp in sent:
            cp.start()

        def passed_on(cp):
            cp.start()
            sent.append(cp)

        for q in range(2):
            for a in range(n):
                for nb in range(2):
                    direct(a, nb, q, ids[nb]).wait_recv()
                    passed_on(to_sibling(a, nb, q, c))
                    if nb == q:
                        passed_on(relay(a, q, ids[nb]))
        for a in range(n):
            for q in range(2):
                relay(a, q, ids[2]).wait_recv()
                passed_on(to_sibling(a, 2, q, c))
        for j in range(3):
            small_copy(j, ids[j]).wait_recv()
            for a in range(n):
                for q in range(2):
                    to_sibling(a, j, q, 1 - c).wait_recv()
        for cp in sent:
            cp.wait_send()
        for cp in own:
            cp.wait()
        _assemble_rows(dsts[0], shard_rows, segments, whole_ref)

    vm = pl.BlockSpec(memory_space=pltpu.VMEM)
    n_sems = n_direct + n_relay + n_sib + 3
    out = pl.pallas_call(
        body, name="gather_shards",
        in_specs=[vm] * (n + 1), out_specs=[vm] * (n + 2),
        out_shape=[jax.ShapeDtypeStruct((N_CHIPS,) + p.shape, p.dtype) for p in parts + [small]]
        + [jax.ShapeDtypeStruct((out_rows, parts[0].shape[1]), parts[0].dtype)],
        scratch_shapes=[pltpu.SemaphoreType.DMA((n_sems,)), pltpu.SemaphoreType.DMA((n_sems,)),
                        pltpu.SemaphoreType.DMA((n + 1,))],
        compiler_params=pltpu.CompilerParams(vmem_limit_bytes=VMEM_LIMIT),
    )(*parts, small)
    return out[1:]


def _allsum_rows(part):
    rows_n = part.shape[0]

    def body(x_ref, gath_ref, sum_ref, send_sems, recv_sems, local_sem):
        x, y, c = _position()
        me, sibling = (x, y, c), (x, y, 1 - c)
        chips = [(1 - x, y), (x, 1 - y), (1 - x, 1 - y)]

        def rows(px, py, pc):
            return gath_ref.at[pl.ds((4 * px + 2 * py + pc) * rows_n, rows_n), :]

        def copy(k, block, to, src=None):
            return pltpu.make_async_remote_copy(
                src_ref=rows(*block) if src is None else src, dst_ref=rows(*block),
                send_sem=send_sems.at[k], recv_sem=recv_sems.at[k], device_id=to, device_id_type=MESH)

        mine = pltpu.make_async_copy(x_ref, rows(*me), local_sem)
        mine.start()
        first = [copy(0, me, sibling, src=x_ref)]
        first += [copy(1 + j, me, (*chip, c), src=x_ref) for j, chip in enumerate(chips)]
        for cp in first:
            cp.start()
        passed = [copy(4 + j, (*chip, c), sibling) for j, chip in enumerate(chips)]
        for j, chip in enumerate(chips):
            copy(1 + j, (*chip, c), me).wait_recv()
            passed[j].start()
        copy(0, sibling, me).wait_recv()
        for j, chip in enumerate(chips):
            copy(4 + j, (*chip, 1 - c), me).wait_recv()
        for cp in first + passed:
            cp.wait_send()
        mine.wait()
        total = gath_ref[pl.ds(0, rows_n), :]
        for d in range(1, N_DEV):
            total = total + gath_ref[pl.ds(d * rows_n, rows_n), :]
        sum_ref[...] = total

    vm = pl.BlockSpec(memory_space=pltpu.VMEM)
    return pl.pallas_call(
        body, name="allsum_rows", in_specs=[vm], out_specs=[vm, vm],
        out_shape=[jax.ShapeDtypeStruct((N_DEV * rows_n, D), f32), jax.ShapeDtypeStruct((rows_n, D), f32)],
        scratch_shapes=[pltpu.SemaphoreType.DMA((7,)), pltpu.SemaphoreType.DMA((7,)), pltpu.SemaphoreType.DMA],
    )(part)[1]


PAIR_ROWS = 16


def _pair_reduce(name, pieces):
    _, r, n = pieces.shape

    def body(p_ref, o_ref, land, send, recv):
        x, y, c = _position()

        def remote(j, half):
            return pltpu.make_async_remote_copy(
                src_ref=p_ref.at[2 * j + half], dst_ref=land.at[j], send_sem=send.at[j], recv_sem=recv.at[j],
                device_id=(x, y, 1 - c), device_id_type=MESH)

        sends = [remote(j, 1 - c) for j in range(N_CHIPS)]
        for cp in sends:
            cp.start()
        for j in range(N_CHIPS):
            remote(j, c).wait_recv()

            def add_rows(i, carry, j=j):
                rows = pl.ds(pl.multiple_of(i * PAIR_ROWS, PAIR_ROWS), PAIR_ROWS)
                o_ref[j, rows, :] = (p_ref[2 * j + c, rows, :].astype(f32) + land[j, rows, :].astype(f32)).astype(bf16)
                return carry

            lax.fori_loop(0, r // PAIR_ROWS, add_rows, 0)
        for cp in sends:
            cp.wait_send()

    vm = pl.BlockSpec(memory_space=pltpu.VMEM)
    return pl.pallas_call(
        body, name=name, in_specs=[vm], out_specs=vm,
        out_shape=jax.ShapeDtypeStruct((N_CHIPS, r, n), bf16),
        scratch_shapes=[pltpu.VMEM((N_CHIPS, r, n), bf16), pltpu.SemaphoreType.DMA((N_CHIPS,)),
                        pltpu.SemaphoreType.DMA((N_CHIPS,))],
        compiler_params=pltpu.CompilerParams(vmem_limit_bytes=VMEM_LIMIT),
    )(pieces)


def _chip_exchange(arrs):
    n = len(arrs)
    heights = [a.shape[1] for a in arrs]
    cuts = [-(-r // 32) * 16 for r in heights]

    def body(*refs):
        srcs, dsts, relays = refs[:n], refs[n:2 * n], refs[2 * n:3 * n]
        send, recv, local = refs[3 * n:]
        x, y, c = _position()
        me = 2 * x + y
        chips = [(1 - x, y), (x, 1 - y), (1 - x, 1 - y)]
        ids = [2 * px + py for px, py in chips]

        def rows(a, quarter):
            return pl.ds(cuts[a], heights[a] - cuts[a]) if quarter else pl.ds(0, cuts[a])

        def held(a, quarter):
            size = heights[a] - cuts[a] if quarter else cuts[a]
            return relays[a].at[quarter, pl.ds(0, size), :]

        def direct(a, nb, piece, landing):
            px, py = chips[nb]
            return pltpu.make_async_remote_copy(
                src_ref=srcs[a].at[piece], dst_ref=dsts[a].at[landing], send_sem=send.at[a * 2 + nb],
                recv_sem=recv.at[a * 2 + nb], device_id=(px, py, c), device_id_type=MESH)

        def first_hop(a, quarter):
            k = 2 * n + a * 2 + quarter
            px, py = chips[quarter]
            return pltpu.make_async_remote_copy(
                src_ref=srcs[a].at[ids[2], rows(a, quarter), :], dst_ref=held(a, quarter), send_sem=send.at[k],
                recv_sem=recv.at[k], device_id=(px, py, c), device_id_type=MESH)

        def second_hop(a, quarter, origin):
            k = 4 * n + a * 2 + quarter
            px, py = chips[1 - quarter]
            return pltpu.make_async_remote_copy(
                src_ref=held(a, quarter), dst_ref=dsts[a].at[origin, rows(a, quarter), :], send_sem=send.at[k],
                recv_sem=recv.at[k], device_id=(px, py, c), device_id_type=MESH)

        own = [pltpu.make_async_copy(srcs[a].at[me], dsts[a].at[me], local.at[a]) for a in range(n)]
        sent = [first_hop(a, q) for a in range(n) for q in range(2)]
        sent += [direct(a, nb, ids[nb], me) for a in range(n) for nb in range(2)]
        for cp in sent + own:
            cp.start()
        for a in range(n):
            for q in range(2):
                first_hop(a, q).wait_recv()
                sent.append(second_hop(a, q, ids[q]))
                sent[-1].start()
        for a in range(n):
            for nb in range(2):
                direct(a, nb, me, ids[nb]).wait_recv()
            for q in range(2):
                second_hop(a, q, ids[2]).wait_recv()
        for cp in sent:
            cp.wait_send()
        for cp in own:
            cp.wait()

    anyspec = pl.BlockSpec(memory_space=pl.ANY)
    out = pl.pallas_call(
        body, name="chip_exchange", in_specs=[anyspec] * n, out_specs=[anyspec] * (2 * n),
        out_shape=[jax.ShapeDtypeStruct(a.shape, a.dtype) for a in arrs]
        + [jax.ShapeDtypeStruct((2, cut, a.shape[2]), a.dtype) for a, cut in zip(arrs, cuts)],
        scratch_shapes=[pltpu.SemaphoreType.DMA((6 * n,)), pltpu.SemaphoreType.DMA((6 * n,)),
                        pltpu.SemaphoreType.DMA((n,))],
    )(*arrs)
    return out[:n]


def _swap_halves(arrs):
    n = len(arrs)

    def body(*refs):
        srcs, dsts = refs[:n], refs[n:2 * n]
        send, recv, local = refs[2 * n:]
        x, y, c = _position()

        def remote(a, landing):
            return pltpu.make_async_remote_copy(
                src_ref=srcs[a], dst_ref=dsts[a].at[landing], send_sem=send.at[a], recv_sem=recv.at[a],
                device_id=(x, y, 1 - c), device_id_type=MESH)

        own = [pltpu.make_async_copy(srcs[a], dsts[a].at[c], local.at[a]) for a in range(n)]
        sends = [remote(a, c) for a in range(n)]
        for cp in sends + own:
            cp.start()
        for a in range(n):
            remote(a, 1 - c).wait_recv()
        for cp in sends:
            cp.wait_send()
        for cp in own:
            cp.wait()

    vm = pl.BlockSpec(memory_space=pltpu.VMEM)
    return pl.pallas_call(
        body, name="swap_halves", in_specs=[vm] * n, out_specs=[vm] * n,
        out_shape=[jax.ShapeDtypeStruct((2,) + a.shape, a.dtype) for a in arrs],
        scratch_shapes=[pltpu.SemaphoreType.DMA((n,)), pltpu.SemaphoreType.DMA((n,)), pltpu.SemaphoreType.DMA((n,))],
        compiler_params=pltpu.CompilerParams(vmem_limit_bytes=VMEM_LIMIT),
    )(*arrs)


def _row_block(r):
    return 128 if r % 128 == 0 else r


def _sum_slots(name, slots):
    s, r, n = slots.shape
    rb = _row_block(r)

    def body(s_ref, o_ref):
        total = s_ref[0].astype(f32)
        for d in range(1, s):
            total = total + s_ref[d].astype(f32)
        o_ref[...] = total

    return pl.pallas_call(
        body, name=name, grid=(r // rb,),
        in_specs=[pl.BlockSpec((s, rb, n), lambda i: (0, i, 0))],
        out_specs=pl.BlockSpec((rb, n), lambda i: (i, 0)),
        out_shape=jax.ShapeDtypeStruct((r, n), f32),
        compiler_params=_params(("parallel",), VMEM_LIMIT),
    )(slots)


def _adamw(name, w, g, m, v):
    r, n = w.shape
    if r % 128 == 0 or r * n <= 128 * 1024:
        rb, nb = _row_block(r), n
    else:
        rb, nb = r, LANES

    def body(w_ref, g_ref, m_ref, v_ref, d_ref, nm_ref, nv_ref):
        gv = g_ref[...]
        m2 = ADAM_B1 * m_ref[...] + (1.0 - ADAM_B1) * gv
        v2 = ADAM_B2 * v_ref[...] + (1.0 - ADAM_B2) * (gv * gv)
        m_hat = m2 / (1.0 - ADAM_B1 ** ADAM_STEP)
        v_hat = v2 / (1.0 - ADAM_B2 ** ADAM_STEP)
        d_ref[...] = (-ADAM_LR) * (m_hat / (jnp.sqrt(v_hat) + ADAM_EPS) + ADAM_WD * w_ref[...])
        nm_ref[...] = m2
        nv_ref[...] = v2

    spec = pl.BlockSpec((rb, nb), lambda i, j: (i, j))
    return pl.pallas_call(
        body, name=name, grid=(r // rb, n // nb), in_specs=[spec] * 4, out_specs=[spec] * 3,
        out_shape=[jax.ShapeDtypeStruct((r, n), f32)] * 3,
        compiler_params=_params(("parallel", "parallel"), VMEM_LIMIT),
    )(w, g, m, v)


def _local_step(x2, tgt2, seq, wt):
    nb = x2.shape[0] // seq
    h, qkv = _norm_qkv(x2, wt["pre_w"], wt["w_all"], wt["b_qkv"])
    rest = _mm("in_rest", h, wt["w_all"], (3 * D, 5 * D), wt["b_rest"], bf16, 1024, 1024)
    f128 = _mm("in_f", h, wt["w_all"], (8 * D, LANES), wt["b_f"], f32, 1024, LANES)
    c = _forget_prep(f128, seq)
    qa, ka = _attn_prep(qkv, c)
    o_att, pa, lse = _attn_fwd(qa, ka, qkv, rest, seq)
    rnn_w = (wt["conv_w"], wt["conv_b"], wt["wa_d"], wt["wx_d"], wt["ba"], wt["bx"], wt["lam"])
    xc, a, hrec, pr = _rnn_fwd(rest, *rnn_w, seq)
    (do, dya, dyr, dmga, dmgr, doa, dga, dhrec, dgr, mrg, dy, delta, loss8, d_post) = _merge_loss(
        rest, pa, pr, o_att, hrec, wt["w_a"], wt["w_r"], wt["w_o"], x2, tgt2, wt["post_w"])
    d_wo, _ = _tn_mm("dw_out", mrg, do, D)
    d_wa, _ = _tn_mm("dw_branch_a", pa, dya, D)
    d_wr, _ = _tn_mm("dw_branch_r", pr, dyr, D)
    dxr, d_wad, d_wxd, vec = _rnn_bwd(dhrec, a, hrec, xc, rest, *rnn_w, seq)
    dq, dk, dv, dc = _attn_bwd(qa, ka, qkv, doa, lse, delta, seq)
    df, db_f = _forget_bwd(dc, f128, seq)
    pieces = [dq, dk, dv, dga, dxr, dgr, dmga, dmgr]
    gx, d_pre = _in_bwd(pieces, df, x2, dy, wt["w_all"], wt["pre_w"])
    names = ["q", "k", "v", "ga", "xr", "gr", "mga", "mgr"]
    dws, dbs = [], []
    for nm, piece in zip(names, pieces):
        dw_p, db_p = _tn_mm("dw_in_" + nm, piece, h, D, bf16)
        dws.append((dw_p, D))
        dbs.append(db_p)
    dw_f, _ = _tn_mm("dw_in_f", df, h, D, bf16)
    shard_rows = IN_TOTAL // N_CHIPS
    w_in_pieces = _pack_pieces(dws[:3] + [(dw_f, HEADS)] + dws[3:] + [(None, IN_TOTAL - IN_USED)], shard_rows,
                               _padded_rows(shard_rows))
    d_b_in = jnp.concatenate(dbs[:3] + [db_f[:, :HEADS]] + dbs[3:] + [jnp.zeros((1, IN_TOTAL - IN_USED), f32)], axis=1)
    return dict(loss=loss8[0, 0], grad_x=gx, pre_w=d_pre, w_in_pieces=w_in_pieces, b_in=d_b_in, conv_w=vec[4:8],
                conv_b=vec[3:4],
                wa_d=d_wad, ba=vec[0:1], wx_d=d_wxd, bx=vec[1:2], lam=vec[2:3], w_a=d_wa, w_r=d_wr, w_o=d_wo,
                post_w=d_post)


def _block_diag(w):
    g, bw, _ = w.shape
    eye = jnp.eye(g, dtype=w.dtype)
    return (w[:, :, None, :] * eye[:, None, :, None]).reshape(g * bw, g * bw)


def _gate_blocks(diag):
    half = diag.shape[1] // 2
    return jnp.stack([diag[:, :half, :half], diag[:, half:, half:]], axis=1).reshape(-1, half, half)


def _padded_rows(rows):
    return -(-rows // 32) * 32


def _pad_cols(a, n):
    return jnp.pad(a, ((0, 0), (0, n - a.shape[1])))


def _pad_rows(a, n):
    return jnp.pad(a, ((0, n - a.shape[0]), (0, 0)))


def kernel(x, pre_norm_w, w_in, b_in, conv_w, conv_b, rg_wa, rg_ba, rg_wx, rg_bx, rg_lambda, w_branch_a, w_branch_r, w_out, post_norm_w, loss_target, m_pre_norm_w, m_w_in, m_b_in, m_conv_w, m_conv_b, m_rg_wa, m_rg_ba, m_rg_wx, m_rg_bx, m_rg_lambda, m_w_branch_a, m_w_branch_r, m_w_out, m_post_norm_w, v_pre_norm_w, v_w_in, v_b_in, v_conv_w, v_conv_b, v_rg_wa, v_rg_ba, v_rg_wx, v_rg_bx, v_rg_lambda, v_w_branch_a, v_w_branch_r, v_w_out, v_post_norm_w):
    nb, seq, _ = x.shape
    chip = 2 * lax.axis_index("x") + lax.axis_index("y")
    n_groups = rg_wa.shape[1]

    w_in_t = jnp.transpose(w_in[0])
    shard_cols = w_in_t.shape[0]
    padded = _padded_rows(shard_cols)
    q_end, f_end = 3 * D, 3 * D + HEADS
    segments = [(0, 0, q_end), (q_end, f_end, IN_USED - f_end), (IN_USED - HEADS, q_end, HEADS)]
    g_a, g_r, g_o, g_cw, w_all = _gather_shards(
        [_pad_rows(w_in_t.astype(bf16), padded), w_branch_a[0].astype(bf16), w_branch_r[0].astype(bf16),
         w_out[0].astype(bf16)], conv_w[0], shard_cols, segments, IN_USED - HEADS + LANES)
    wt = dict(
        pre_w=pre_norm_w, post_w=post_norm_w,
        w_all=w_all, b_qkv=b_in[:, :q_end], b_f=_pad_cols(b_in[:, q_end:f_end], LANES), b_rest=b_in[:, f_end:IN_USED],
        w_a=g_a.reshape(D, D), w_r=g_r.reshape(D, D), w_o=g_o.reshape(D, D),
        conv_w=jnp.transpose(g_cw, (1, 0, 2)).reshape(4, D), conv_b=conv_b,
        wa_d=_block_diag(rg_wa[0]).astype(bf16), wx_d=_block_diag(rg_wx[0]).astype(bf16),
        ba=rg_ba, bx=rg_bx, lam=rg_lambda)

    part = _local_step(x.reshape(nb * seq, D), loss_target.reshape(nb * seq, D), seq, wt)
    loss = lax.psum(part["loss"], ("x", "y", "c"))
    grad_x = part["grad_x"].reshape(nb, seq, D)

    small = jnp.concatenate([
        part["pre_w"], _pad_cols(part["b_in"], 10 * D).reshape(10, D), part["conv_b"],
        _gate_blocks(part["wa_d"]).reshape(-1, D), part["ba"],
        _gate_blocks(part["wx_d"]).reshape(-1, D), part["bx"], part["lam"], part["post_w"],
        part["conv_w"]], axis=0)
    n_small = small.shape[0]
    n_rep = n_small - 4
    tot = _allsum_rows(_pad_rows(small, -(-n_small // 8) * 8))
    g_rep = tot[:n_rep]
    g_conv_w = lax.dynamic_slice_in_dim(tot[n_rep:n_small], chip * (D // N_CHIPS), D // N_CHIPS, axis=1)

    def unpack(p):
        o = [0]

        def take(k):
            o[0] += k
            return p[o[0] - k:o[0]]

        pre = take(1)
        b = take(10).reshape(1, 10 * D)[:, :IN_TOTAL]
        cb = take(1)
        wa = take(64).reshape(rg_wa.shape)
        ba = take(1)
        wx = take(64).reshape(rg_wx.shape)
        bx = take(1)
        lam = take(1)
        post = take(1)
        return dict(pre_norm_w=pre, b_in=b, conv_b=cb, rg_wa=wa, rg_ba=ba, rg_wx=wx, rg_bx=bx, rg_lambda=lam,
                    post_norm_w=post)

    grads = unpack(g_rep)
    replicated = dict(
        pre_norm_w=(pre_norm_w, m_pre_norm_w, v_pre_norm_w), b_in=(b_in, m_b_in, v_b_in),
        conv_b=(conv_b, m_conv_b, v_conv_b), rg_wa=(rg_wa, m_rg_wa, v_rg_wa), rg_ba=(rg_ba, m_rg_ba, v_rg_ba),
        rg_wx=(rg_wx, m_rg_wx, v_rg_wx), rg_bx=(rg_bx, m_rg_bx, v_rg_bx),
        rg_lambda=(rg_lambda, m_rg_lambda, v_rg_lambda), post_norm_w=(post_norm_w, m_post_norm_w, v_post_norm_w))
    deltas, new_m, new_v = {}, {}, {}
    for name, (w, m, v) in replicated.items():
        as2d = lambda a: a.reshape(-1, D) if a.ndim > 2 else a
        upd = _adamw("adamw_" + name, as2d(w), as2d(grads[name]), as2d(m), as2d(v))
        deltas[name], new_m[name], new_v[name] = [a.reshape(w.shape) for a in upd]

    p_aro = jnp.concatenate([part[k].reshape(N_DEV, D // N_DEV, D) for k in ("w_a", "w_r", "w_o")], axis=1)
    s_in, s_aro = _chip_exchange([_pair_reduce("pair_w_in", part["w_in_pieces"]),
                                  _pair_reduce("pair_w_aro", p_aro.astype(bf16))])
    f_in, f_aro = _swap_halves([_sum_slots("sum_w_in", s_in), _sum_slots("sum_w_aro", s_aro)])
    g_w_in_t = f_in.reshape(padded, D)[:shard_cols]
    rows = D // N_DEV
    g_aro = [f_aro[:, i * rows:(i + 1) * rows, :].reshape(2 * rows, D) for i in range(3)]

    w_in_upd = _adamw("adamw_w_in", w_in_t, g_w_in_t, jnp.transpose(m_w_in[0]), jnp.transpose(v_w_in[0]))
    g_w_in, d_w_in, nm_w_in, nv_w_in = [jnp.transpose(a) for a in (g_w_in_t, *w_in_upd)]
    upd_a = _adamw("adamw_w_branch_a", w_branch_a[0], g_aro[0], m_w_branch_a[0], v_w_branch_a[0])
    upd_r = _adamw("adamw_w_branch_r", w_branch_r[0], g_aro[1], m_w_branch_r[0], v_w_branch_r[0])
    upd_o = _adamw("adamw_w_out", w_out[0], g_aro[2], m_w_out[0], v_w_out[0])
    d_aro, nm_aro, nv_aro = zip(upd_a, upd_r, upd_o)
    d_cw, nm_cw, nv_cw = _adamw("adamw_conv_w", conv_w[0], g_conv_w, m_conv_w[0], v_conv_w[0])

    def sharded(t_in, t_aro, t_cw):
        return dict(w_in=t_in[None], conv_w=t_cw[None], w_branch_a=t_aro[0][None], w_branch_r=t_aro[1][None],
                    w_out=t_aro[2][None])

    order = ["pre_norm_w", "w_in", "b_in", "conv_w", "conv_b", "rg_wa", "rg_ba", "rg_wx", "rg_bx", "rg_lambda",
             "w_branch_a", "w_branch_r", "w_out", "post_norm_w"]
    outs = [loss, grad_x]
    for rep, shd in ((grads, sharded(g_w_in, g_aro, g_conv_w)), (deltas, sharded(d_w_in, d_aro, d_cw)),
                     (new_m, sharded(nm_w_in, nm_aro, nm_cw)), (new_v, sharded(nv_w_in, nv_aro, nv_cw))):
        both = {**rep, **shd}
        outs.extend(both[k] for k in order)
    return tuple(outs)
```

```python
import jax
import jax.numpy as jnp
from jax import lax
from jax.experimental import pallas as pl
from jax.experimental.pallas import tpu as pltpu

f32 = jnp.float32
bf16 = jnp.bfloat16

D = 1024
HEADS = 16
HEAD_PAIRS = 8
LANES = 128
NORM_EPS = 1e-6
MASK_VALUE = -1e30
RG_C = 8.0
QK_SCALE = 0.125
TQ = 256
ATT_GROUP = 8
ATT_GROUP_FWD = 16
TL = 512
TM = 512
PREV_ROWS = 16
IN_USED = 8 * D + HEADS
IN_TOTAL = 9 * D + HEADS
N_CHIPS = 4
N_DEV = 8
ADAM_LR, ADAM_B1, ADAM_B2, ADAM_EPS, ADAM_WD, ADAM_STEP = 0.001, 0.9, 0.999, 1e-08, 0.01, 10
VMEM_LIMIT = 56 * 1024 * 1024
MESH = pl.DeviceIdType.MESH


def _dot(a, b):
    return jnp.dot(a, b, preferred_element_type=f32)


def _dot_nt(a, b):
    return lax.dot_general(a, b, (((1,), (1,)), ((), ())), preferred_element_type=f32)


def _dot_tn(a, b):
    return lax.dot_general(a, b, (((0,), (0,)), ((), ())), preferred_element_type=f32)


def _sig(x):
    return 0.5 * jnp.tanh(0.5 * x) + 0.5


def _softplus(x):
    return jnp.maximum(x, 0.0) + jnp.log(1.0 + jnp.exp(-jnp.abs(x)))


def _params(sem, vmem=None):
    return pltpu.CompilerParams(dimension_semantics=sem, vmem_limit_bytes=vmem)


def _tile(tm, width, cb=0):
    return pl.BlockSpec((tm, width), lambda i, cb=cb: (i, cb))


def _whole(shape):
    nd = len(shape)
    return pl.BlockSpec(shape, lambda *_: (0,) * nd)


def _norm_qkv(x, w_pre, w_all, b_qkv, f_row0, b_f, tm=1024):
    t = x.shape[0]
    tm = min(tm, t)
    n = b_qkv.shape[1]

    def body(x_ref, wp_ref, w_ref, b_ref, wf_ref, bf_ref, h_ref, o_ref, f_ref):
        @pl.when(pl.program_id(1) == 0)
        def _():
            xv = x_ref[...]
            r = lax.rsqrt(jnp.mean(xv * xv, axis=-1, keepdims=True) + NORM_EPS)
            h = (xv * r * wp_ref[...]).astype(bf16)
            h_ref[...] = h
            f_ref[...] = _dot_nt(h, wf_ref[...]) + bf_ref[...]

        o_ref[...] = (_dot_nt(h_ref[...], w_ref[...]) + b_ref[...]).astype(bf16)

    return pl.pallas_call(
        body, name="norm_qkv", grid=(t // tm, n // D),
        in_specs=[pl.BlockSpec((tm, D), lambda i, j: (i, 0)), _whole((1, D)), pl.BlockSpec((D, D), lambda i, j: (j, 0)),
                  pl.BlockSpec((1, D), lambda i, j: (0, j)),
                  pl.BlockSpec((LANES, D), lambda i, j: (f_row0 // LANES, 0)), _whole((1, LANES))],
        out_specs=[pl.BlockSpec((tm, D), lambda i, j: (i, 0)), pl.BlockSpec((tm, D), lambda i, j: (i, j)),
                   pl.BlockSpec((tm, LANES), lambda i, j: (i, 0))],
        out_shape=[jax.ShapeDtypeStruct((t, D), bf16), jax.ShapeDtypeStruct((t, n), bf16),
                   jax.ShapeDtypeStruct((t, LANES), f32)],
        compiler_params=_params(("parallel", "arbitrary"), VMEM_LIMIT),
    )(x, w_pre, w_all, b_qkv, w_all, b_f)


def _mm(name, a, w, w_rows, bias, out_dtype, tm, tn):
    t, k = a.shape
    tm = min(tm, t)
    row0, n = w_rows
    assert row0 % tn == 0

    def body(a_ref, w_ref, b_ref, o_ref):
        o_ref[...] = (_dot_nt(a_ref[...], w_ref[...]) + b_ref[...]).astype(out_dtype)

    return pl.pallas_call(
        body, name=name, grid=(t // tm, n // tn),
        in_specs=[pl.BlockSpec((tm, k), lambda i, j: (i, 0)), pl.BlockSpec((tn, k), lambda i, j: (row0 // tn + j, 0)),
                  pl.BlockSpec((1, tn), lambda i, j: (0, j))],
        out_specs=pl.BlockSpec((tm, tn), lambda i, j: (i, j)), out_shape=jax.ShapeDtypeStruct((t, n), out_dtype),
        compiler_params=_params(("parallel", "parallel"), VMEM_LIMIT),
    )(a, w, bias)


def _forget_prep(f128, seq):
    t = f128.shape[0]
    nb = seq // LANES

    def body(f_ref, c_ref):
        r = lax.broadcasted_iota(jnp.int32, (LANES, LANES), 0)
        cidx = lax.broadcasted_iota(jnp.int32, (LANES, LANES), 1)
        tri = (r >= cidx).astype(f32)
        carry = jnp.zeros((1, LANES), f32)
        for blk in range(nb):
            fv = f_ref[pl.ds(blk * LANES, LANES), :]
            lf = -_softplus(-fv)
            c_ref[pl.ds(blk * LANES, LANES), :] = (
                jnp.dot(tri, lf, preferred_element_type=f32, precision=lax.Precision.HIGHEST) + carry)
            carry = carry + jnp.sum(lf, axis=0, keepdims=True)

    return pl.pallas_call(
        body, name="forget_prep", grid=(t // seq,),
        in_specs=[pl.BlockSpec((seq, LANES), lambda b: (b, 0))],
        out_specs=pl.BlockSpec((seq, LANES), lambda b: (b, 0)),
        out_shape=jax.ShapeDtypeStruct((t, LANES), f32),
        compiler_params=_params(("parallel",)),
    )(f128)


def _split3(cv):
    hi = cv.astype(bf16)
    r1 = cv - hi.astype(f32)
    mid = r1.astype(bf16)
    lo = (r1 - mid.astype(f32)).astype(bf16)
    return hi, mid, lo


def _attn_prep(qkv, c):
    t = qkv.shape[0]

    def body(q_ref, k_ref, c_ref, qa_ref, ka_ref):
        lane = lax.broadcasted_iota(jnp.int32, (1, LANES), 1)
        cv = c_ref[...]
        one = jnp.ones((), bf16)
        zero = jnp.zeros((), bf16)
        q_ones = jnp.where((lane >= 67) & (lane < 70), one, zero)
        k_ones = jnp.where((lane >= 64) & (lane < 67), one, zero)
        for head in range(HEADS):
            pair = pl.ds((head // 2) * LANES, LANES)
            ch = jnp.sum(jnp.where(lane == head, cv, 0.0), axis=1, keepdims=True)
            hi, mid, lo = _split3(ch)
            q2, k2 = q_ref[:, pair], k_ref[:, pair]
            if head % 2 == 1:
                q2, k2 = pltpu.roll(q2, 64, 1), pltpu.roll(k2, 64, 1)
            qa = jnp.where(lane < 64, q2 * jnp.asarray(QK_SCALE, bf16),
                           jnp.where(lane == 64, hi, jnp.where(lane == 65, mid, jnp.where(lane == 66, lo, q_ones))))
            ka = jnp.where(lane < 64, k2,
                           jnp.where(lane == 67, -hi, jnp.where(lane == 68, -mid, jnp.where(lane == 69, -lo, k_ones))))
            qa_ref[:, pl.ds(head * LANES, LANES)] = qa
            ka_ref[:, pl.ds(head * LANES, LANES)] = ka

    tm = min(TM, t)
    out = pl.BlockSpec((tm, 2 * D), lambda i: (i, 0))
    return pl.pallas_call(
        body, name="attn_prep", grid=(t // tm,),
        in_specs=[_tile(tm, D, 0), _tile(tm, D, 1), _tile(tm, LANES)],
        out_specs=[out, out],
        out_shape=[jax.ShapeDtypeStruct((t, 2 * D), bf16)] * 2,
        compiler_params=_params(("parallel",)),
    )(qkv, qkv, c)


def _attn_fwd(qa, ka, qkv, rest, seq):
    t = qkv.shape[0]
    nb, nq = t // seq, seq // TQ

    hg = ATT_GROUP_FWD
    ng = HEADS // hg

    def body(q_ref, k_ref, v_ref, ga_ref, o_ref, pa_ref, lse_ref, acc_scr):
        qi, gi = pl.program_id(1), pl.program_id(2)
        krow = lax.broadcasted_iota(jnp.int32, (TQ, TQ), 0)
        qcol = lax.broadcasted_iota(jnp.int32, (TQ, TQ), 1)
        acc_scr[...] = jnp.zeros_like(acc_scr)

        def kv_step(kt, carry, masked):
            ks = pl.multiple_of(kt * TQ, TQ)
            sts = [_dot_nt(k_ref[pl.ds(ks, TQ), pl.ds(g * LANES, LANES)], q_ref[:, pl.ds(g * LANES, LANES)])
                   for g in range(hg)]
            if masked:
                sts = [jnp.where(krow <= qcol, st, MASK_VALUE) for st in sts]
            m_new = [jnp.maximum(carry[g][0], jnp.max(sts[g], axis=0, keepdims=True)) for g in range(hg)]
            ps = [jnp.exp(sts[g] - m_new[g]) for g in range(hg)]
            alphas = [jnp.exp(carry[g][0] - m_new[g]) for g in range(hg)]
            phi = [ps[g].astype(bf16) for g in range(hg)]
            plo = [(ps[g] - phi[g].astype(f32)).astype(bf16) for g in range(hg)]
            vs = [v_ref[pl.ds(ks, TQ), pl.ds(j * LANES, LANES)] for j in range(hg // 2)]
            pvs = [_dot_tn(vs[g // 2], phi[g]) + _dot_tn(vs[g // 2], plo[g]) for g in range(hg)]
            olds = [acc_scr[g] for g in range(hg)]
            for g in range(hg):
                acc_scr[g] = alphas[g] * olds[g] + pvs[g]
            return tuple((m_new[g], alphas[g] * carry[g][1] + jnp.sum(ps[g], axis=0, keepdims=True))
                         for g in range(hg))

        init = tuple((jnp.full((1, TQ), MASK_VALUE, f32), jnp.zeros((1, TQ), f32)) for _ in range(hg))
        carry = lax.fori_loop(0, qi, lambda kt, cr: kv_step(kt, cr, False), init)
        stats = kv_step(qi, carry, True)
        drow = lax.broadcasted_iota(jnp.int32, (LANES, TQ), 0)
        for g in range(hg):
            m, l = stats[g]
            lse_ref[0, pl.ds(hg * gi + g, 1), :] = m + jnp.log(l)
        for j in range(hg // 2):
            o2 = jnp.where(drow < 64, acc_scr[2 * j] / stats[2 * j][1], acc_scr[2 * j + 1] / stats[2 * j + 1][1]).T
            o_ref[:, pl.ds(j * LANES, LANES)] = o2
            ga = ga_ref[:, pl.ds(j * LANES, LANES)].astype(f32)
            pa_ref[:, pl.ds(j * LANES, LANES)] = (o2 * (ga * _sig(ga))).astype(bf16)

    vw = hg * 64
    tile = pl.BlockSpec((TQ, vw), lambda b, qi, gi: (b * nq + qi, gi))
    return pl.pallas_call(
        body, name="attn_fwd", grid=(nb, nq, ng),
        in_specs=[pl.BlockSpec((TQ, hg * LANES), lambda b, qi, gi: (b * nq + qi, gi)),
                  pl.BlockSpec((seq, hg * LANES), lambda b, qi, gi: (b, gi)),
                  pl.BlockSpec((seq, vw), lambda b, qi, gi: (b, 2 * ng + gi)), tile],
        out_specs=[tile, tile, pl.BlockSpec((1, HEADS, TQ), lambda b, qi, gi: (b * nq + qi, 0, 0))],
        out_shape=[jax.ShapeDtypeStruct((t, D), f32), jax.ShapeDtypeStruct((t, D), bf16),
                   jax.ShapeDtypeStruct((t // TQ, HEADS, TQ), f32)],
        scratch_shapes=[pltpu.VMEM((hg, LANES, TQ), f32)],
        compiler_params=_params(("parallel", "parallel", "arbitrary"), VMEM_LIMIT),
    )(qa, ka, qkv, rest)


def _shifted_rows(x, top8, prev8, shift, row, row8):
    body = pltpu.roll(x, shift, 0)
    head = jnp.where(row8 < shift, pltpu.roll(prev8, shift, 0), pltpu.roll(top8, shift, 0))
    return body, head


def _rnn_gates(xc, wa_ref, wx_ref, ba_ref, bx_ref, lam_ref):
    xcb = xc.astype(bf16)
    r = _sig(_dot(xcb, wa_ref[...]) + ba_ref[...])
    i = _sig(_dot(xcb, wx_ref[...]) + bx_ref[...])
    sp = _softplus(-lam_ref[...])
    log_a = (-RG_C) * r * sp
    th = jnp.tanh(log_a)
    w1 = (-2.0) * th / (1.0 - th)
    sq = jnp.sqrt(jnp.maximum(w1, 0.0))
    return r, i, sp, log_a, w1, sq


def _conv_tile(x_ref, xprev_ref, has_prev, cw_ref, cb_ref, xc_ref):
    row = lax.broadcasted_iota(jnp.int32, (TL, D), 0)
    row8 = lax.broadcasted_iota(jnp.int32, (8, D), 0)
    x = x_ref[...].astype(f32)
    top8 = x[:8]
    prev8 = jnp.where(has_prev, xprev_ref[...].astype(f32)[PREV_ROWS - 8:], 0.0)
    xc = cb_ref[...] + cw_ref[pl.ds(3, 1), :] * x
    xc8 = cb_ref[...] + cw_ref[pl.ds(3, 1), :] * top8
    for sh in range(1, 4):
        w = cw_ref[pl.ds(3 - sh, 1), :]
        xs, xs8 = _shifted_rows(x, top8, prev8, sh, row, row8)
        xc = xc + w * xs
        xc8 = xc8 + w * xs8
    xc_ref[...] = xc
    xc_ref[pl.ds(0, 8), :] = xc8


def _rnn_fwd(rest, conv_w, conv_b, wa_d, wx_d, ba, bx, lam, seq):
    t = rest.shape[0]
    nb, nt = t // seq, seq // TL

    def body(x_ref, xprev_ref, gr_ref, cw_ref, cb_ref, wa_ref, wx_ref, ba_ref, bx_ref, lam_ref,
             xc_ref, a_ref, h_ref, pr_ref, xc_scr, u_scr, h_scr, carry):
        tt = pl.program_id(1)
        _conv_tile(x_ref, xprev_ref, tt > 0, cw_ref, cb_ref, xc_scr)
        xc = xc_scr[...]
        xc_ref[...] = xc.astype(bf16)
        r, i, sp, log_a, w1, sq = _rnn_gates(xc, wa_ref, wx_ref, ba_ref, bx_ref, lam_ref)
        a_ref[...] = jnp.exp(log_a)
        u_scr[...] = sq * (i * xc)

        @pl.when(tt == 0)
        def _():
            carry[...] = jnp.zeros_like(carry)

        def step(s, h):
            h = a_ref[pl.ds(s, 1), :] * h + u_scr[pl.ds(s, 1), :]
            h_scr[pl.ds(s, 1), :] = h
            return h

        carry[...] = lax.fori_loop(0, TL, step, carry[...], unroll=8)
        gr = gr_ref[...].astype(f32)
        h = h_scr[...]
        h_ref[...] = h.astype(bf16)
        pr_ref[...] = (h * (gr * _sig(gr))).astype(bf16)

    tile = lambda cb: pl.BlockSpec((TL, D), lambda b, tt, cb=cb: (b * nt + tt, cb))
    prev = lambda cb: pl.BlockSpec(
        (PREV_ROWS, D), lambda b, tt, cb=cb: (jnp.maximum((b * nt + tt) * (TL // PREV_ROWS) - 1, 0), cb))
    vec = _whole((1, D))
    return pl.pallas_call(
        body, name="rnn_fwd", grid=(nb, nt),
        in_specs=[tile(1), prev(1), tile(2), _whole((4, D)), vec, _whole((D, D)), _whole((D, D)), vec, vec, vec],
        out_specs=[tile(0)] * 4,
        out_shape=[jax.ShapeDtypeStruct((t, D), dt) for dt in (bf16, f32, bf16, bf16)],
        scratch_shapes=[pltpu.VMEM((TL, D), f32)] * 3 + [pltpu.VMEM((1, D), f32)],
        compiler_params=_params(("parallel", "arbitrary"), VMEM_LIMIT),
    )(rest, rest, rest, conv_w, conv_b, wa_d, wx_d, ba, bx, lam)


def _merge_loss(rest, pa, pr, o_att, hrec, w_a, w_r, w_out, x, tgt, w_post):
    t = x.shape[0]

    def branch(dy, w_ref, g_ref, act):
        dp = _dot_nt(dy, w_ref[...])
        g = g_ref[...].astype(f32)
        sg = _sig(g)
        return (dp * (g * sg)).astype(bf16), (dp * act * (sg * (1.0 + g * (1.0 - sg)))).astype(bf16)

    def body(mga_ref, mgr_ref, pa_ref, pr_ref, ga_ref, gr_ref, oa_ref, h_ref, x_ref, t_ref, wa_ref, wr_ref, wo_ref,
             w_ref, do_ref, dya_ref, dyr_ref, dmga_ref, dmgr_ref, doa_ref, dga_ref, dh_ref, dgr_ref, mrg_ref, dy_ref,
             delta_ref, loss_ref, dwp_ref):
        @pl.when(pl.program_id(0) == 0)
        def _():
            loss_ref[...] = jnp.zeros_like(loss_ref)
            dwp_ref[...] = jnp.zeros_like(dwp_ref)

        sa, sr = _sig(mga_ref[...].astype(f32)), _sig(mgr_ref[...].astype(f32))
        ya, yr = _dot(pa_ref[...], wa_ref[...]), _dot(pr_ref[...], wr_ref[...])
        mrg = (sa * ya + sr * yr).astype(bf16)
        mrg_ref[...] = mrg
        ov = _dot(mrg, wo_ref[...])
        w = w_ref[...]
        r2 = lax.rsqrt(jnp.mean(ov * ov, axis=-1, keepdims=True) + NORM_EPS)
        oh = ov * r2
        e = x_ref[...] + oh * w - t_ref[...]
        loss_ref[...] += 0.5 * jnp.sum(jnp.mean(e * e, axis=-1, keepdims=True))
        dy = e * (1.0 / D)
        dy_ref[...] = dy
        dwp_ref[...] += jnp.sum(dy * oh, axis=0, keepdims=True)
        doh = dy * w
        do = (r2 * (doh - oh * jnp.mean(doh * oh, axis=-1, keepdims=True))).astype(bf16)
        do_ref[...] = do

        dm = _dot_nt(do, wo_ref[...])
        dya, dyr = (dm * sa).astype(bf16), (dm * sr).astype(bf16)
        dya_ref[...] = dya
        dyr_ref[...] = dyr
        dmga_ref[...] = (dm * ya * sa * (1.0 - sa)).astype(bf16)
        dmgr_ref[...] = (dm * yr * sr * (1.0 - sr)).astype(bf16)
        o_att = oa_ref[...]
        doa, dga_ref[...] = branch(dya, wa_ref, ga_ref, o_att)
        doa_ref[...] = doa
        dh_ref[...], dgr_ref[...] = branch(dyr, wr_ref, gr_ref, h_ref[...].astype(f32))
        ch = lax.broadcasted_iota(jnp.int32, (D, LANES), 0)
        hd = lax.broadcasted_iota(jnp.int32, (D, LANES), 1)
        pick = (ch // 64 == hd).astype(bf16)
        per_head = sum(_dot(piece, pick) for piece in _split3(doa.astype(f32) * o_att))
        delta_ref[0] = per_head.T[:HEADS, :]

    once = pl.BlockSpec((D, D), lambda i: (0, 0), pipeline_mode=pl.Buffered(1))
    rows = _tile(TQ, D)
    return pl.pallas_call(
        body, name="merge_loss", grid=(t // TQ,),
        in_specs=[_tile(TQ, D, 3), _tile(TQ, D, 4), rows, rows, _tile(TQ, D, 0), _tile(TQ, D, 2), rows, rows, rows, rows,
                  once, once, once, _whole((1, D))],
        out_specs=[rows] * 11 + [pl.BlockSpec((1, HEADS, TQ), lambda i: (i, 0, 0)), _whole((8, LANES)), _whole((1, D))],
        out_shape=[jax.ShapeDtypeStruct((t, D), bf16)] * 10 + [jax.ShapeDtypeStruct((t, D), f32),
                   jax.ShapeDtypeStruct((t // TQ, HEADS, TQ), f32), jax.ShapeDtypeStruct((8, LANES), f32),
                   jax.ShapeDtypeStruct((1, D), f32)],
        compiler_params=_params(("arbitrary",), VMEM_LIMIT),
    )(rest, rest, pa, pr, rest, rest, o_att, hrec, x, tgt, w_a, w_r, w_out, w_post)


def _rnn_bwd(dh, a, h, xc, rest, conv_w, conv_b, wa_d, wx_d, ba, bx, lam, seq):
    t = dh.shape[0]
    nb, nt = t // seq, seq // TL
    diag = (D // LANES, LANES, LANES)

    def body(dh_ref, a_ref, h_ref, hprev_ref, xc_ref, x_ref, xprev_ref, cw_ref, cb_ref, wa_ref, wx_ref,
             ba_ref, bx_ref, lam_ref, dxr_ref, dwa_ref, dwx_ref, vec_ref, g_scr, dxc_scr, dxr_scr, qcarry, dxc_next):
        b, tt = pl.program_id(0), pl.program_id(1)
        rt = nt - 1 - tt

        @pl.when((b == 0) & (tt == 0))
        def _():
            dwa_ref[...] = jnp.zeros_like(dwa_ref)
            dwx_ref[...] = jnp.zeros_like(dwx_ref)
            vec_ref[...] = jnp.zeros_like(vec_ref)

        @pl.when(tt == 0)
        def _():
            qcarry[...] = jnp.zeros_like(qcarry)
            dxc_next[...] = jnp.zeros_like(dxc_next)

        g_scr[...] = dh_ref[...].astype(f32)

        def step(k, q):
            s = TL - 1 - k
            g = g_scr[pl.ds(s, 1), :] + q
            g_scr[pl.ds(s, 1), :] = g
            return a_ref[pl.ds(s, 1), :] * g

        qcarry[...] = lax.fori_loop(0, TL, step, qcarry[...], unroll=8)

        row = lax.broadcasted_iota(jnp.int32, (TL, D), 0)
        row8 = lax.broadcasted_iota(jnp.int32, (8, D), 0)
        g = g_scr[...]
        av = a_ref[...]
        xc = xc_ref[...].astype(f32)
        hlast = jnp.where(rt > 0, hprev_ref[...].astype(f32)[PREV_ROWS - 1:], 0.0)
        hp = jnp.where(row == 0, hlast, pltpu.roll(h_ref[...].astype(f32), 1, 0))
        r, i, sp, log_a, w1, sq = _rnn_gates(xc, wa_ref, wx_ref, ba_ref, bx_ref, lam_ref)
        dix = g * sq
        di = dix * xc
        dxc = dix * i
        dsq = g * (i * xc)
        dlog_a = g * hp * av - dsq * jnp.where(sq > 0.0, (1.0 - w1) / sq, 0.0)
        dpr = (dlog_a * ((-RG_C) * sp)) * r * (1.0 - r)
        dpi = di * i * (1.0 - i)
        dprb, dpib, xcb = dpr.astype(bf16), dpi.astype(bf16), xc.astype(bf16)
        dxc = dxc + _dot_nt(dprb, wa_ref[...]) + _dot_nt(dpib, wx_ref[...])
        for j in range(D // LANES):
            cols = slice(j * LANES, (j + 1) * LANES)
            dwa_ref[j] += _dot_tn(xcb[:, cols], dprb[:, cols])
            dwx_ref[j] += _dot_tn(xcb[:, cols], dpib[:, cols])
        vec_ref[pl.ds(0, 1), :] += jnp.sum(dpr, axis=0, keepdims=True)
        vec_ref[pl.ds(1, 1), :] += jnp.sum(dpi, axis=0, keepdims=True)
        dsp = jnp.sum(dlog_a * ((-RG_C) * r), axis=0, keepdims=True)
        vec_ref[pl.ds(2, 1), :] += dsp * (-_sig(-lam_ref[...]))
        vec_ref[pl.ds(3, 1), :] += jnp.sum(dxc, axis=0, keepdims=True)

        dxc_scr[...] = dxc
        bot8 = dxc_scr[pl.ds(TL - 8, 8), :]
        nxt8 = dxc_next[...]
        dxr = cw_ref[pl.ds(3, 1), :] * dxc
        dxr8 = cw_ref[pl.ds(3, 1), :] * bot8
        for sh in range(1, 4):
            w = cw_ref[pl.ds(3 - sh, 1), :]
            dxr = dxr + w * pltpu.roll(dxc, TL - sh, 0)
            dxr8 = dxr8 + w * jnp.where(row8 < 8 - sh, pltpu.roll(bot8, 8 - sh, 0), pltpu.roll(nxt8, 8 - sh, 0))
        dxr_scr[...] = dxr
        dxr_scr[pl.ds(TL - 8, 8), :] = dxr8
        dxr_ref[...] = dxr_scr[...].astype(bf16)
        dxc_next[...] = dxc_scr[pl.ds(0, 8), :]

        x = x_ref[...].astype(f32)
        prev8 = jnp.where(rt > 0, xprev_ref[...].astype(f32)[PREV_ROWS - 8:], 0.0)
        dxc_top8 = dxc_scr[pl.ds(0, 8), :]
        vec_ref[pl.ds(7, 1), :] += jnp.sum(dxc * x, axis=0, keepdims=True)
        for sh in range(1, 4):
            inside = jnp.sum(dxc * jnp.where(row >= sh, pltpu.roll(x, sh, 0), 0.0), axis=0, keepdims=True)
            above = jnp.sum(dxc_top8 * jnp.where(row8 < sh, pltpu.roll(prev8, sh, 0), 0.0), axis=0, keepdims=True)
            vec_ref[pl.ds(7 - sh, 1), :] += inside + above

    tile = lambda cb: pl.BlockSpec((TL, D), lambda b, tt, cb=cb: (b * nt + nt - 1 - tt, cb))
    prev = lambda cb: pl.BlockSpec(
        (PREV_ROWS, D), lambda b, tt, cb=cb: (jnp.maximum((b * nt + nt - 1 - tt) * (TL // PREV_ROWS) - 1, 0), cb))
    vec = _whole((1, D))
    return pl.pallas_call(
        body, name="rnn_bwd", grid=(nb, nt),
        in_specs=[tile(0), tile(0), tile(0), prev(0), tile(0), tile(1), prev(1),
                  _whole((4, D)), vec, _whole((D, D)), _whole((D, D)), vec, vec, vec],
        out_specs=[tile(0), _whole(diag), _whole(diag), _whole((8, D))],
        out_shape=[jax.ShapeDtypeStruct((t, D), bf16), jax.ShapeDtypeStruct(diag, f32),
                   jax.ShapeDtypeStruct(diag, f32), jax.ShapeDtypeStruct((8, D), f32)],
        scratch_shapes=[pltpu.VMEM((TL, D), f32), pltpu.VMEM((TL, D), f32), pltpu.VMEM((TL, D), f32),
                        pltpu.VMEM((1, D), f32), pltpu.VMEM((8, D), f32)],
        compiler_params=_params(("arbitrary", "arbitrary"), VMEM_LIMIT),
    )(dh, a, h, h, xc, rest, rest, conv_w, conv_b, wa_d, wx_d, ba, bx, lam)


def _attn_bwd(qa, ka, qkv, doa, lse, delta, seq):
    t = qkv.shape[0]
    nb, nq = t // seq, seq // TQ
    hg = ATT_GROUP
    ng, npair = HEADS // hg, hg // 2

    def body(qa_ref, ka_ref, q_ref, k_ref, v_ref, do_ref, lse_ref, dl_ref, dq_ref, dk_ref, dv_ref, dc_ref,
             dqt_scr, dk_scr, dv_scr, ds_scr, kht_scr):
        gi, kt = pl.program_id(1), pl.program_id(2)
        lane = lax.broadcasted_iota(jnp.int32, (1, LANES), 1)
        krow = lax.broadcasted_iota(jnp.int32, (TQ, TQ), 0)
        qcol = lax.broadcasted_iota(jnp.int32, (TQ, TQ), 1)
        lmask = [(lane // 64) == hh for hh in range(2)]
        scale = jnp.asarray(QK_SCALE, bf16)

        @pl.when(kt == 0)
        def _():
            dqt_scr[...] = jnp.zeros_like(dqt_scr)

        dk_scr[...] = jnp.zeros_like(dk_scr)
        dv_scr[...] = jnp.zeros_like(dv_scr)
        ds_scr[...] = jnp.zeros_like(ds_scr)
        for g in range(hg):
            k2 = k_ref[:, pl.ds((g // 2) * LANES, LANES)]
            kht_scr[g] = jnp.where(lmask[g % 2], k2, jnp.zeros_like(k2)).T

        def q_step(qt, masked):
            qs = pl.multiple_of(qt * TQ, TQ)
            heads = range(hg)
            do2 = [do_ref[pl.ds(qs, TQ), pl.ds(j * LANES, LANES)] for j in range(npair)]
            q2 = [q_ref[pl.ds(qs, TQ), pl.ds(j * LANES, LANES)] for j in range(npair)]
            doh = [jnp.where(lmask[g % 2], do2[g // 2], jnp.zeros_like(do2[0])) for g in heads]
            qh = [jnp.where(lmask[g % 2], q2[g // 2], jnp.zeros_like(q2[0])) * scale for g in heads]
            st = [_dot_nt(ka_ref[:, pl.ds(g * LANES, LANES)], qa_ref[pl.ds(qs, TQ), pl.ds(g * LANES, LANES)])
                  for g in heads]
            if masked:
                st = [jnp.where(krow <= qcol, s, MASK_VALUE) for s in st]
            dp = [_dot_nt(v_ref[:, pl.ds((g // 2) * LANES, LANES)], doh[g]) for g in heads]
            p = [jnp.exp(st[g] - lse_ref[qt, pl.ds(hg * gi + g, 1), :]) for g in heads]
            ds = [p[g] * (dp[g] - dl_ref[qt, pl.ds(hg * gi + g, 1), :]) for g in heads]
            pb = [x.astype(bf16) for x in p]
            dsb = [x.astype(bf16) for x in ds]
            for j in range(npair):
                a, b = 2 * j, 2 * j + 1
                dv_scr[j] += _dot(pb[a], doh[a]) + _dot(pb[b], doh[b])
                dk_scr[j] += _dot(dsb[a], qh[a]) + _dot(dsb[b], qh[b])
                dqt_scr[qt, j] += (_dot(kht_scr[a], dsb[a]) + _dot(kht_scr[b], dsb[b])) * QK_SCALE
            for g in heads:
                ds_scr[g] += ds[g][:, :LANES] + ds[g][:, LANES:]

        q_step(kt, True)

        def loop_body(qt, carry):
            q_step(qt, False)
            return carry

        lax.fori_loop(kt + 1, nq, loop_body, 0)

        dc = jnp.zeros((TQ, LANES), f32)
        for g in range(hg):
            dc = jnp.where(lane == g, -jnp.sum(ds_scr[g], axis=1, keepdims=True), dc)
        dc_ref[...] = dc
        for j in range(npair):
            dk_ref[:, pl.ds(j * LANES, LANES)] = dk_scr[j].astype(bf16)
            dv_ref[:, pl.ds(j * LANES, LANES)] = dv_scr[j].astype(bf16)

        @pl.when(kt == nq - 1)
        def _():
            for qt in range(nq):
                for j in range(npair):
                    dq_ref[pl.ds(qt * TQ, TQ), pl.ds(j * LANES, LANES)] = dqt_scr[qt, j].T.astype(bf16)

    vw = hg * 64
    seqspec = pl.BlockSpec((seq, vw), lambda b, gi, kt: (b, gi))
    kspec = lambda off: pl.BlockSpec((TQ, vw), lambda b, gi, kt: (b * nq + kt, off + gi))
    rowspec = pl.BlockSpec((nq, HEADS, TQ), lambda b, gi, kt: (b, 0, 0))
    return pl.pallas_call(
        body, name="attn_bwd", grid=(nb, ng, nq),
        in_specs=[pl.BlockSpec((seq, hg * LANES), lambda b, gi, kt: (b, gi)),
                  pl.BlockSpec((TQ, hg * LANES), lambda b, gi, kt: (b * nq + kt, gi)),
                  seqspec, kspec(ng), kspec(2 * ng), seqspec, rowspec, rowspec],
        out_specs=[seqspec, kspec(0), kspec(0), pl.BlockSpec((TQ, LANES), lambda b, gi, kt: (b * nq + kt, gi))],
        out_shape=[jax.ShapeDtypeStruct((t, D), bf16)] * 3 + [jax.ShapeDtypeStruct((t, ng * LANES), f32)],
        scratch_shapes=[pltpu.VMEM((nq, npair, LANES, TQ), f32), pltpu.VMEM((npair, TQ, LANES), f32),
                        pltpu.VMEM((npair, TQ, LANES), f32), pltpu.VMEM((hg, TQ, LANES), f32),
                        pltpu.VMEM((hg, LANES, TQ), bf16)],
        compiler_params=_params(("parallel", "parallel", "arbitrary"), VMEM_LIMIT),
    )(qa, ka, qkv, qkv, qkv, doa, lse, delta)


def _forget_bwd(dc, f128, seq):
    t = f128.shape[0]
    nb = seq // LANES
    groups = dc.shape[1] // LANES

    def body(dc_ref, f_ref, df_ref, dbf_ref):
        @pl.when(pl.program_id(0) == 0)
        def _():
            dbf_ref[...] = jnp.zeros_like(dbf_ref)

        r = lax.broadcasted_iota(jnp.int32, (LANES, LANES), 0)
        cidx = lax.broadcasted_iota(jnp.int32, (LANES, LANES), 1)
        tri = (r <= cidx).astype(f32)
        carry = jnp.zeros((1, LANES), f32)
        total = jnp.zeros((1, LANES), f32)
        for blk in reversed(range(nb)):
            dcb = dc_ref[pl.ds(blk * LANES, LANES), pl.ds(0, LANES)]
            for gi in range(1, groups):
                dcb = dcb + pltpu.roll(dc_ref[pl.ds(blk * LANES, LANES), pl.ds(gi * LANES, LANES)], gi * ATT_GROUP, 1)
            dlf = jnp.dot(tri, dcb, preferred_element_type=f32, precision=lax.Precision.HIGHEST) + carry
            df = dlf * _sig(-f_ref[pl.ds(blk * LANES, LANES), :])
            df_ref[pl.ds(blk * LANES, LANES), :] = df.astype(bf16)
            total = total + jnp.sum(df, axis=0, keepdims=True)
            carry = carry + jnp.sum(dcb, axis=0, keepdims=True)
        dbf_ref[...] += total

    return pl.pallas_call(
        body, name="forget_bwd", grid=(t // seq,),
        in_specs=[pl.BlockSpec((seq, groups * LANES), lambda b: (b, 0)), pl.BlockSpec((seq, LANES), lambda b: (b, 0))],
        out_specs=[pl.BlockSpec((seq, LANES), lambda b: (b, 0)), _whole((1, LANES))],
        out_shape=[jax.ShapeDtypeStruct((t, LANES), bf16), jax.ShapeDtypeStruct((1, LANES), f32)],
        compiler_params=_params(("arbitrary",)),
    )(dc, f128)


def _in_bwd(dz, df, x, dy, w_all, w_pre):
    t = x.shape[0]
    n_dz = len(dz)

    def body(*refs):
        dz_refs = refs[:n_dz]
        df_ref, x_ref, dy_ref, w_ref, wp_ref, gx_ref, dwp_ref = refs[n_dz:]

        @pl.when(pl.program_id(0) == 0)
        def _():
            dwp_ref[...] = jnp.zeros_like(dwp_ref)

        dh = _dot(df_ref[...], w_ref[pl.ds(n_dz * D, LANES), :])
        for p in range(n_dz):
            dh = dh + _dot(dz_refs[p][...], w_ref[pl.ds(p * D, D), :])
        xv = x_ref[...]
        r1 = lax.rsqrt(jnp.mean(xv * xv, axis=-1, keepdims=True) + NORM_EPS)
        xh = xv * r1
        dwp_ref[...] += jnp.sum(dh * xh, axis=0, keepdims=True)
        dxh = dh * wp_ref[...]
        gx_ref[...] = dy_ref[...] + r1 * (dxh - xh * jnp.mean(dxh * xh, axis=-1, keepdims=True))

    once = lambda shape: pl.BlockSpec(shape, lambda i: (0, 0), pipeline_mode=pl.Buffered(1))
    return pl.pallas_call(
        body, name="in_bwd", grid=(t // TM,),
        in_specs=[_tile(TM, D)] * n_dz + [_tile(TM, LANES), _tile(TM, D), _tile(TM, D), once(w_all.shape),
                  _whole((1, D))],
        out_specs=[_tile(TM, D), _whole((1, D))],
        out_shape=[jax.ShapeDtypeStruct((t, D), f32), jax.ShapeDtypeStruct((1, D), f32)],
        compiler_params=_params(("arbitrary",), VMEM_LIMIT),
    )(*dz, df, x, dy, w_all, w_pre)


def _tn_mm(name, a, b, tn, out_dtype=f32, tk=2048):
    t, k = a.shape
    tk = min(tk, t)
    n = b.shape[1]
    nk = t // tk

    def body(a_ref, b_ref, o_ref, s_ref, acc_ref):
        j, kk = pl.program_id(0), pl.program_id(1)

        @pl.when(kk == 0)
        def _():
            acc_ref[...] = jnp.zeros_like(acc_ref)

        @pl.when((j == 0) & (kk == 0))
        def _():
            s_ref[...] = jnp.zeros_like(s_ref)

        av = a_ref[...]
        acc_ref[...] += _dot_tn(av, b_ref[...])

        @pl.when(j == 0)
        def _():
            s_ref[...] += jnp.sum(av.astype(f32), axis=0, keepdims=True)

        @pl.when(kk == nk - 1)
        def _():
            o_ref[...] = acc_ref[...].astype(out_dtype)

    return pl.pallas_call(
        body, name=name, grid=(n // tn, nk),
        in_specs=[pl.BlockSpec((tk, k), lambda j, kk: (kk, 0)), pl.BlockSpec((tk, tn), lambda j, kk: (kk, j))],
        out_specs=[pl.BlockSpec((k, tn), lambda j, kk: (0, j)), _whole((1, k))],
        out_shape=[jax.ShapeDtypeStruct((k, n), out_dtype), jax.ShapeDtypeStruct((1, k), f32)],
        scratch_shapes=[pltpu.VMEM((k, tn), f32)],
        compiler_params=_params(("arbitrary", "arbitrary"), VMEM_LIMIT),
    )(a, b)


def _position():
    return lax.axis_index("x"), lax.axis_index("y"), lax.axis_index("c")


ROW_BLOCK = 128


def _pick_rows(layout, first, count):
    acc = jnp.zeros((ROW_BLOCK, D), f32)
    seg_start = 0
    for ref, ref_row, rows in layout:
        lo, hi = max(first, seg_start), min(first + count, seg_start + rows)
        if lo < hi and ref is not None:
            off, take, done = ref_row + lo - seg_start, hi - lo, lo - first
            start = off // 16 * 16
            win = -(-(off - start + take) // 16) * 16
            r = lax.broadcasted_iota(jnp.int32, (ROW_BLOCK, win), 0)
            col = lax.broadcasted_iota(jnp.int32, (ROW_BLOCK, win), 1)
            pick = ((col - r == off - start - done) & (r >= done) & (r < done + take)).astype(bf16)
            acc = acc + _dot(pick, ref[pl.ds(start, win), :])
        seg_start += rows
    return acc


def _assemble_rows(shards_ref, shard_rows, segments, out_ref):
    layout = [(shards_ref.at[j], 0, shard_rows) for j in range(shards_ref.shape[0])]
    for out0, log0, count in segments:
        for b0 in range(0, count, ROW_BLOCK):
            block = _pick_rows(layout, log0 + b0, min(ROW_BLOCK, count - b0))
            out_ref[pl.ds(out0 + b0, ROW_BLOCK), :] = block.astype(bf16)


def _pack_pieces(blocks, shard_rows, padded):
    arrays = [a for a, _ in blocks if a is not None]
    piece_rows = padded // 2

    def body(*refs):
        out_ref = refs[-1]
        it = iter(refs[:-1])
        layout = [(None if a is None else next(it), 0, rows) for a, rows in blocks]
        for k in range(N_DEV):
            chip, half = divmod(k, 2)
            for b0 in range(0, piece_rows, ROW_BLOCK):
                n = min(ROW_BLOCK, piece_rows - b0)
                in_shard = half * piece_rows + b0
                count = max(0, min(n, shard_rows - in_shard))
                block = _pick_rows(layout, chip * shard_rows + in_shard, count)
                out_ref[k, pl.ds(b0, n), :] = block[:n].astype(bf16)

    vm = pl.BlockSpec(memory_space=pltpu.VMEM)
    return pl.pallas_call(
        body, name="pack_pieces", in_specs=[vm] * len(arrays), out_specs=vm,
        out_shape=jax.ShapeDtypeStruct((N_DEV, piece_rows, D), bf16),
        compiler_params=pltpu.CompilerParams(vmem_limit_bytes=VMEM_LIMIT),
    )(*arrays)


def _gather_shards(parts, small, shard_rows, segments, out_rows):
    n = len(parts)
    halves = [p.shape[0] // 2 for p in parts]
    cuts = [-(-h // 32) * 16 for h in halves]
    n_direct, n_relay, n_sib = 4 * n, 2 * n, 6 * n

    def body(*refs):
        srcs, small_src = refs[:n], refs[n]
        dsts, small_dst, whole_ref = refs[n + 1:2 * n + 1], refs[2 * n + 1], refs[2 * n + 2]
        send, recv, local = refs[2 * n + 3:]
        x, y, c = _position()
        me = 2 * x + y
        chips = [(1 - x, y), (x, 1 - y), (1 - x, 1 - y)]
        ids = [2 * px + py for px, py in chips]

        def rows(a, half, quarter):
            start = half * halves[a] + (cuts[a] if quarter else 0)
            return pl.ds(start, halves[a] - cuts[a] if quarter else cuts[a])

        def landing(a, shard, half, quarter):
            return dsts[a].at[shard, rows(a, half, quarter), :]

        def direct(a, nb, quarter, shard):
            k = (a * 2 + nb) * 2 + quarter
            px, py = chips[nb]
            return pltpu.make_async_remote_copy(
                src_ref=srcs[a].at[rows(a, c, quarter), :], dst_ref=landing(a, shard, c, quarter),
                send_sem=send.at[k], recv_sem=recv.at[k], device_id=(px, py, c), device_id_type=MESH)

        def relay(a, quarter, shard):
            k = n_direct + a * 2 + quarter
            px, py = chips[1 - quarter]
            return pltpu.make_async_remote_copy(
                src_ref=landing(a, shard, c, quarter), dst_ref=landing(a, shard, c, quarter),
                send_sem=send.at[k], recv_sem=recv.at[k], device_id=(px, py, c), device_id_type=MESH)

        def to_sibling(a, origin, quarter, half):
            k = n_direct + n_relay + (a * 3 + origin) * 2 + quarter
            return pltpu.make_async_remote_copy(
                src_ref=landing(a, ids[origin], half, quarter), dst_ref=landing(a, ids[origin], half, quarter),
                send_sem=send.at[k], recv_sem=recv.at[k], device_id=(x, y, 1 - c), device_id_type=MESH)

        def small_copy(j, shard):
            k = n_direct + n_relay + n_sib + j
            px, py = chips[j]
            return pltpu.make_async_remote_copy(
                src_ref=small_src, dst_ref=small_dst.at[shard], send_sem=send.at[k], recv_sem=recv.at[k],
                device_id=(px, py, c), device_id_type=MESH)

        own = [pltpu.make_async_copy(srcs[a], dsts[a].at[me], local.at[a]) for a in range(n)]
        own.append(pltpu.make_async_copy(small_src, small_dst.at[me], local.at[n]))
        for cp in own:
            cp.start()
        sent = [direct(a, nb, q, me) for q in range(2) for a in range(n) for nb in range(2)]
        sent += [small_copy(j, me) for j in range(3)]
        for cp in sent:
            cp.start()

        def passed_on(cp):
            cp.start()
            sent.append(cp)

        for q in range(2):
            for a in range(n):
                for nb in range(2):
                    direct(a, nb, q, ids[nb]).wait_recv()
                    passed_on(to_sibling(a, nb, q, c))
                    if nb == q:
                        passed_on(relay(a, q, ids[nb]))
        for a in range(n):
            for q in range(2):
                relay(a, q, ids[2]).wait_recv()
                passed_on(to_sibling(a, 2, q, c))
        for j in range(3):
            small_copy(j, ids[j]).wait_recv()
            for a in range(n):
                for q in range(2):
                    to_sibling(a, j, q, 1 - c).wait_recv()
        for cp in sent:
            cp.wait_send()
        for cp in own:
            cp.wait()
        _assemble_rows(dsts[0], shard_rows, segments, whole_ref)

    vm = pl.BlockSpec(memory_space=pltpu.VMEM)
    n_sems = n_direct + n_relay + n_sib + 3
    out = pl.pallas_call(
        body, name="gather_shards",
        in_specs=[vm] * (n + 1), out_specs=[vm] * (n + 2),
        out_shape=[jax.ShapeDtypeStruct((N_CHIPS,) + p.shape, p.dtype) for p in parts + [small]]
        + [jax.ShapeDtypeStruct((out_rows, parts[0].shape[1]), parts[0].dtype)],
        scratch_shapes=[pltpu.SemaphoreType.DMA((n_sems,)), pltpu.SemaphoreType.DMA((n_sems,)),
                        pltpu.SemaphoreType.DMA((n + 1,))],
        compiler_params=pltpu.CompilerParams(vmem_limit_bytes=VMEM_LIMIT),
    )(*parts, small)
    return out[1:]


def _allsum_rows(part):
    rows_n = part.shape[0]

    def body(x_ref, gath_ref, sum_ref, send_sems, recv_sems, local_sem):
        x, y, c = _position()
        me, sibling = (x, y, c), (x, y, 1 - c)
        chips = [(1 - x, y), (x, 1 - y), (1 - x, 1 - y)]

        def rows(px, py, pc):
            return gath_ref.at[pl.ds((4 * px + 2 * py + pc) * rows_n, rows_n), :]

        def copy(k, block, to, src=None):
            return pltpu.make_async_remote_copy(
                src_ref=rows(*block) if src is None else src, dst_ref=rows(*block),
                send_sem=send_sems.at[k], recv_sem=recv_sems.at[k], device_id=to, device_id_type=MESH)

        mine = pltpu.make_async_copy(x_ref, rows(*me), local_sem)
        mine.start()
        first = [copy(0, me, sibling, src=x_ref)]
        first += [copy(1 + j, me, (*chip, c), src=x_ref) for j, chip in enumerate(chips)]
        for cp in first:
            cp.start()
        passed = [copy(4 + j, (*chip, c), sibling) for j, chip in enumerate(chips)]
        for j, chip in enumerate(chips):
            copy(1 + j, (*chip, c), me).wait_recv()
            passed[j].start()
        copy(0, sibling, me).wait_recv()
        for j, chip in enumerate(chips):
            copy(4 + j, (*chip, 1 - c), me).wait_recv()
        for cp in first + passed:
            cp.wait_send()
        mine.wait()
        total = gath_ref[pl.ds(0, rows_n), :]
        for d in range(1, N_DEV):
            total = total + gath_ref[pl.ds(d * rows_n, rows_n), :]
        sum_ref[...] = total

    vm = pl.BlockSpec(memory_space=pltpu.VMEM)
    return pl.pallas_call(
        body, name="allsum_rows", in_specs=[vm], out_specs=[vm, vm],
        out_shape=[jax.ShapeDtypeStruct((N_DEV * rows_n, D), f32), jax.ShapeDtypeStruct((rows_n, D), f32)],
        scratch_shapes=[pltpu.SemaphoreType.DMA((7,)), pltpu.SemaphoreType.DMA((7,)), pltpu.SemaphoreType.DMA],
    )(part)[1]


PAIR_ROWS = 16


def _pair_reduce(name, pieces):
    _, r, n = pieces.shape

    def body(p_ref, o_ref, land, send, recv):
        x, y, c = _position()

        def remote(j, half):
            return pltpu.make_async_remote_copy(
                src_ref=p_ref.at[2 * j + half], dst_ref=land.at[j], send_sem=send.at[j], recv_sem=recv.at[j],
                device_id=(x, y, 1 - c), device_id_type=MESH)

        sends = [remote(j, 1 - c) for j in range(N_CHIPS)]
        for cp in sends:
            cp.start()
        for j in range(N_CHIPS):
            remote(j, c).wait_recv()

            def add_rows(i, carry, j=j):
                rows = pl.ds(pl.multiple_of(i * PAIR_ROWS, PAIR_ROWS), PAIR_ROWS)
                o_ref[j, rows, :] = (p_ref[2 * j + c, rows, :].astype(f32) + land[j, rows, :].astype(f32)).astype(bf16)
                return carry

            lax.fori_loop(0, r // PAIR_ROWS, add_rows, 0)
        for cp in sends:
            cp.wait_send()

    vm = pl.BlockSpec(memory_space=pltpu.VMEM)
    return pl.pallas_call(
        body, name=name, in_specs=[vm], out_specs=vm,
        out_shape=jax.ShapeDtypeStruct((N_CHIPS, r, n), bf16),
        scratch_shapes=[pltpu.VMEM((N_CHIPS, r, n), bf16), pltpu.SemaphoreType.DMA((N_CHIPS,)),
                        pltpu.SemaphoreType.DMA((N_CHIPS,))],
        compiler_params=pltpu.CompilerParams(vmem_limit_bytes=VMEM_LIMIT),
    )(pieces)


def _chip_exchange(arrs):
    n = len(arrs)
    heights = [a.shape[1] for a in arrs]
    cuts = [-(-r // 32) * 16 for r in heights]

    def body(*refs):
        srcs, dsts, relays = refs[:n], refs[n:2 * n], refs[2 * n:3 * n]
        send, recv, local = refs[3 * n:]
        x, y, c = _position()
        me = 2 * x + y
        chips = [(1 - x, y), (x, 1 - y), (1 - x, 1 - y)]
        ids = [2 * px + py for px, py in chips]

        def rows(a, quarter):
            return pl.ds(cuts[a], heights[a] - cuts[a]) if quarter else pl.ds(0, cuts[a])

        def held(a, quarter):
            size = heights[a] - cuts[a] if quarter else cuts[a]
            return relays[a].at[quarter, pl.ds(0, size), :]

        def direct(a, nb, piece, landing):
            px, py = chips[nb]
            return pltpu.make_async_remote_copy(
                src_ref=srcs[a].at[piece], dst_ref=dsts[a].at[landing], send_sem=send.at[a * 2 + nb],
                recv_sem=recv.at[a * 2 + nb], device_id=(px, py, c), device_id_type=MESH)

        def first_hop(a, quarter):
            k = 2 * n + a * 2 + quarter
            px, py = chips[quarter]
            return pltpu.make_async_remote_copy(
                src_ref=srcs[a].at[ids[2], rows(a, quarter), :], dst_ref=held(a, quarter), send_sem=send.at[k],
                recv_sem=recv.at[k], device_id=(px, py, c), device_id_type=MESH)

        def second_hop(a, quarter, origin):
            k = 4 * n + a * 2 + quarter
            px, py = chips[1 - quarter]
            return pltpu.make_async_remote_copy(
                src_ref=held(a, quarter), dst_ref=dsts[a].at[origin, rows(a, quarter), :], send_sem=send.at[k],
                recv_sem=recv.at[k], device_id=(px, py, c), device_id_type=MESH)

        own = [pltpu.make_async_copy(srcs[a].at[me], dsts[a].at[me], local.at[a]) for a in range(n)]
        sent = [first_hop(a, q) for a in range(n) for q in range(2)]
        sent += [direct(a, nb, ids[nb], me) for a in range(n) for nb in range(2)]
        for cp in sent + own:
            cp.start()
        for a in range(n):
            for q in range(2):
                first_hop(a, q).wait_recv()
                sent.append(second_hop(a, q, ids[q]))
                sent[-1].start()
        for a in range(n):
            for nb in range(2):
                direct(a, nb, me, ids[nb]).wait_recv()
            for q in range(2):
                second_hop(a, q, ids[2]).wait_recv()
        for cp in sent:
            cp.wait_send()
        for cp in own:
            cp.wait()

    anyspec = pl.BlockSpec(memory_space=pl.ANY)
    out = pl.pallas_call(
        body, name="chip_exchange", in_specs=[anyspec] * n, out_specs=[anyspec] * (2 * n),
        out_shape=[jax.ShapeDtypeStruct(a.shape, a.dtype) for a in arrs]
        + [jax.ShapeDtypeStruct((2, cut, a.shape[2]), a.dtype) for a, cut in zip(arrs, cuts)],
        scratch_shapes=[pltpu.SemaphoreType.DMA((6 * n,)), pltpu.SemaphoreType.DMA((6 * n,)),
                        pltpu.SemaphoreType.DMA((n,))],
    )(*arrs)
    return out[:n]


def _sum_swap_halves(slots):
    n = len(slots)

    def body(*refs):
        srcs, dsts, halves = refs[:n], refs[n:2 * n], refs[2 * n:3 * n]
        send, recv, local = refs[3 * n:]
        x, y, c = _position()

        def remote(a, landing):
            return pltpu.make_async_remote_copy(
                src_ref=halves[a], dst_ref=dsts[a].at[landing], send_sem=send.at[a], recv_sem=recv.at[a],
                device_id=(x, y, 1 - c), device_id_type=MESH)

        for a in range(n):
            def add_rows(i, carry, a=a):
                rows = pl.ds(pl.multiple_of(i * PAIR_ROWS, PAIR_ROWS), PAIR_ROWS)
                total = srcs[a][0, rows, :].astype(f32)
                for s in range(1, N_CHIPS):
                    total = total + srcs[a][s, rows, :].astype(f32)
                halves[a][rows, :] = total
                return carry

            lax.fori_loop(0, srcs[a].shape[1] // PAIR_ROWS, add_rows, 0)
        own = [pltpu.make_async_copy(halves[a], dsts[a].at[c], local.at[a]) for a in range(n)]
        sends = [remote(a, c) for a in range(n)]
        for cp in sends + own:
            cp.start()
        for a in range(n):
            remote(a, 1 - c).wait_recv()
        for cp in sends:
            cp.wait_send()
        for cp in own:
            cp.wait()

    vm = pl.BlockSpec(memory_space=pltpu.VMEM)
    return pl.pallas_call(
        body, name="sum_swap_halves", in_specs=[vm] * n, out_specs=[vm] * n,
        out_shape=[jax.ShapeDtypeStruct((2,) + a.shape[1:], f32) for a in slots],
        scratch_shapes=[pltpu.VMEM(a.shape[1:], f32) for a in slots]
        + [pltpu.SemaphoreType.DMA((n,)), pltpu.SemaphoreType.DMA((n,)), pltpu.SemaphoreType.DMA((n,))],
        compiler_params=pltpu.CompilerParams(vmem_limit_bytes=VMEM_LIMIT),
    )(*slots)


def _row_block(r):
    return 128 if r % 128 == 0 else r


def _adamw(name, w, g, m, v):
    r, n = w.shape
    if r % 128 == 0 or r * n <= 128 * 1024:
        rb, nb = _row_block(r), n
    else:
        rb, nb = r, LANES

    def body(w_ref, g_ref, m_ref, v_ref, d_ref, nm_ref, nv_ref):
        gv = g_ref[...]
        m2 = ADAM_B1 * m_ref[...] + (1.0 - ADAM_B1) * gv
        v2 = ADAM_B2 * v_ref[...] + (1.0 - ADAM_B2) * (gv * gv)
        m_hat = m2 / (1.0 - ADAM_B1 ** ADAM_STEP)
        v_hat = v2 / (1.0 - ADAM_B2 ** ADAM_STEP)
        d_ref[...] = (-ADAM_LR) * (m_hat / (jnp.sqrt(v_hat) + ADAM_EPS) + ADAM_WD * w_ref[...])
        nm_ref[...] = m2
        nv_ref[...] = v2

    spec = pl.BlockSpec((rb, nb), lambda i, j: (i, j))
    return pl.pallas_call(
        body, name=name, grid=(r // rb, n // nb), in_specs=[spec] * 4, out_specs=[spec] * 3,
        out_shape=[jax.ShapeDtypeStruct((r, n), f32)] * 3,
        compiler_params=_params(("parallel", "parallel"), VMEM_LIMIT),
    )(w, g, m, v)


def _local_step(x2, tgt2, seq, wt):
    nb = x2.shape[0] // seq
    h, qkv, f128 = _norm_qkv(x2, wt["pre_w"], wt["w_all"], wt["b_qkv"], 8 * D, wt["b_f"])
    rest = _mm("in_rest", h, wt["w_all"], (3 * D, 5 * D), wt["b_rest"], bf16, 1024, 1024)
    c = _forget_prep(f128, seq)
    qa, ka = _attn_prep(qkv, c)
    o_att, pa, lse = _attn_fwd(qa, ka, qkv, rest, seq)
    rnn_w = (wt["conv_w"], wt["conv_b"], wt["wa_d"], wt["wx_d"], wt["ba"], wt["bx"], wt["lam"])
    xc, a, hrec, pr = _rnn_fwd(rest, *rnn_w, seq)
    (do, dya, dyr, dmga, dmgr, doa, dga, dhrec, dgr, mrg, dy, delta, loss8, d_post) = _merge_loss(
        rest, pa, pr, o_att, hrec, wt["w_a"], wt["w_r"], wt["w_o"], x2, tgt2, wt["post_w"])
    d_wo, _ = _tn_mm("dw_out", mrg, do, D)
    d_wa, _ = _tn_mm("dw_branch_a", pa, dya, D)
    d_wr, _ = _tn_mm("dw_branch_r", pr, dyr, D)
    dxr, d_wad, d_wxd, vec = _rnn_bwd(dhrec, a, hrec, xc, rest, *rnn_w, seq)
    dq, dk, dv, dc = _attn_bwd(qa, ka, qkv, doa, lse, delta, seq)
    df, db_f = _forget_bwd(dc, f128, seq)
    pieces = [dq, dk, dv, dga, dxr, dgr, dmga, dmgr]
    gx, d_pre = _in_bwd(pieces, df, x2, dy, wt["w_all"], wt["pre_w"])
    names = ["q", "k", "v", "ga", "xr", "gr", "mga", "mgr"]
    dws, dbs = [], []
    for nm, piece in zip(names, pieces):
        dw_p, db_p = _tn_mm("dw_in_" + nm, piece, h, D, bf16)
        dws.append((dw_p, D))
        dbs.append(db_p)
    dw_f, _ = _tn_mm("dw_in_f", df, h, D, bf16)
    shard_rows = IN_TOTAL // N_CHIPS
    w_in_pieces = _pack_pieces(dws[:3] + [(dw_f, HEADS)] + dws[3:] + [(None, IN_TOTAL - IN_USED)], shard_rows,
                               _padded_rows(shard_rows))
    d_b_in = jnp.concatenate(dbs[:3] + [db_f[:, :HEADS]] + dbs[3:] + [jnp.zeros((1, IN_TOTAL - IN_USED), f32)], axis=1)
    return dict(loss=loss8[0, 0], grad_x=gx, pre_w=d_pre, w_in_pieces=w_in_pieces, b_in=d_b_in, conv_w=vec[4:8],
                conv_b=vec[3:4],
                wa_d=d_wad, ba=vec[0:1], wx_d=d_wxd, bx=vec[1:2], lam=vec[2:3], w_a=d_wa, w_r=d_wr, w_o=d_wo,
                post_w=d_post)


def _block_diag(w):
    g, bw, _ = w.shape
    eye = jnp.eye(g, dtype=w.dtype)
    return (w[:, :, None, :] * eye[:, None, :, None]).reshape(g * bw, g * bw)


def _gate_blocks(diag):
    half = diag.shape[1] // 2
    return jnp.stack([diag[:, :half, :half], diag[:, half:, half:]], axis=1).reshape(-1, half, half)


def _padded_rows(rows):
    return -(-rows // 32) * 32


def _pad_cols(a, n):
    return jnp.pad(a, ((0, 0), (0, n - a.shape[1])))


def _pad_rows(a, n):
    return jnp.pad(a, ((0, n - a.shape[0]), (0, 0)))


def kernel(x, pre_norm_w, w_in, b_in, conv_w, conv_b, rg_wa, rg_ba, rg_wx, rg_bx, rg_lambda, w_branch_a, w_branch_r, w_out, post_norm_w, loss_target, m_pre_norm_w, m_w_in, m_b_in, m_conv_w, m_conv_b, m_rg_wa, m_rg_ba, m_rg_wx, m_rg_bx, m_rg_lambda, m_w_branch_a, m_w_branch_r, m_w_out, m_post_norm_w, v_pre_norm_w, v_w_in, v_b_in, v_conv_w, v_conv_b, v_rg_wa, v_rg_ba, v_rg_wx, v_rg_bx, v_rg_lambda, v_w_branch_a, v_w_branch_r, v_w_out, v_post_norm_w):
    nb, seq, _ = x.shape
    chip = 2 * lax.axis_index("x") + lax.axis_index("y")
    n_groups = rg_wa.shape[1]

    w_in_t = jnp.transpose(w_in[0])
    shard_cols = w_in_t.shape[0]
    padded = _padded_rows(shard_cols)
    q_end, f_end = 3 * D, 3 * D + HEADS
    segments = [(0, 0, q_end), (q_end, f_end, IN_USED - f_end), (IN_USED - HEADS, q_end, HEADS)]
    g_a, g_r, g_o, g_cw, w_all = _gather_shards(
        [_pad_rows(w_in_t.astype(bf16), padded), w_branch_a[0].astype(bf16), w_branch_r[0].astype(bf16),
         w_out[0].astype(bf16)], conv_w[0], shard_cols, segments, IN_USED - HEADS + LANES)
    wt = dict(
        pre_w=pre_norm_w, post_w=post_norm_w,
        w_all=w_all, b_qkv=b_in[:, :q_end], b_f=_pad_cols(b_in[:, q_end:f_end], LANES), b_rest=b_in[:, f_end:IN_USED],
        w_a=g_a.reshape(D, D), w_r=g_r.reshape(D, D), w_o=g_o.reshape(D, D),
        conv_w=jnp.transpose(g_cw, (1, 0, 2)).reshape(4, D), conv_b=conv_b,
        wa_d=_block_diag(rg_wa[0]).astype(bf16), wx_d=_block_diag(rg_wx[0]).astype(bf16),
        ba=rg_ba, bx=rg_bx, lam=rg_lambda)

    part = _local_step(x.reshape(nb * seq, D), loss_target.reshape(nb * seq, D), seq, wt)
    loss = lax.psum(part["loss"], ("x", "y", "c"))
    grad_x = part["grad_x"].reshape(nb, seq, D)

    small = jnp.concatenate([
        part["pre_w"], _pad_cols(part["b_in"], 10 * D).reshape(10, D), part["conv_b"],
        _gate_blocks(part["wa_d"]).reshape(-1, D), part["ba"],
        _gate_blocks(part["wx_d"]).reshape(-1, D), part["bx"], part["lam"], part["post_w"],
        part["conv_w"]], axis=0)
    n_small = small.shape[0]
    n_rep = n_small - 4
    tot = _allsum_rows(_pad_rows(small, -(-n_small // 8) * 8))
    g_rep = tot[:n_rep]
    g_conv_w = lax.dynamic_slice_in_dim(tot[n_rep:n_small], chip * (D // N_CHIPS), D // N_CHIPS, axis=1)

    def unpack(p):
        o = [0]

        def take(k):
            o[0] += k
            return p[o[0] - k:o[0]]

        pre = take(1)
        b = take(10).reshape(1, 10 * D)[:, :IN_TOTAL]
        cb = take(1)
        wa = take(64).reshape(rg_wa.shape)
        ba = take(1)
        wx = take(64).reshape(rg_wx.shape)
        bx = take(1)
        lam = take(1)
        post = take(1)
        return dict(pre_norm_w=pre, b_in=b, conv_b=cb, rg_wa=wa, rg_ba=ba, rg_wx=wx, rg_bx=bx, rg_lambda=lam,
                    post_norm_w=post)

    grads = unpack(g_rep)
    replicated = dict(
        pre_norm_w=(pre_norm_w, m_pre_norm_w, v_pre_norm_w), b_in=(b_in, m_b_in, v_b_in),
        conv_b=(conv_b, m_conv_b, v_conv_b), rg_wa=(rg_wa, m_rg_wa, v_rg_wa), rg_ba=(rg_ba, m_rg_ba, v_rg_ba),
        rg_wx=(rg_wx, m_rg_wx, v_rg_wx), rg_bx=(rg_bx, m_rg_bx, v_rg_bx),
        rg_lambda=(rg_lambda, m_rg_lambda, v_rg_lambda), post_norm_w=(post_norm_w, m_post_norm_w, v_post_norm_w))
    deltas, new_m, new_v = {}, {}, {}
    for name, (w, m, v) in replicated.items():
        as2d = lambda a: a.reshape(-1, D) if a.ndim > 2 else a
        upd = _adamw("adamw_" + name, as2d(w), as2d(grads[name]), as2d(m), as2d(v))
        deltas[name], new_m[name], new_v[name] = [a.reshape(w.shape) for a in upd]

    p_aro = jnp.concatenate([part[k].reshape(N_DEV, D // N_DEV, D) for k in ("w_a", "w_r", "w_o")], axis=1)
    s_in, s_aro = _chip_exchange([_pair_reduce("pair_w_in", part["w_in_pieces"]),
                                  _pair_reduce("pair_w_aro", p_aro.astype(bf16))])
    f_in, f_aro = _sum_swap_halves([s_in, s_aro])
    g_w_in_t = f_in.reshape(padded, D)[:shard_cols]
    rows = D // N_DEV
    g_aro = [f_aro[:, i * rows:(i + 1) * rows, :].reshape(2 * rows, D) for i in range(3)]

    w_in_upd = _adamw("adamw_w_in", w_in_t, g_w_in_t, jnp.transpose(m_w_in[0]), jnp.transpose(v_w_in[0]))
    g_w_in, d_w_in, nm_w_in, nv_w_in = [jnp.transpose(a) for a in (g_w_in_t, *w_in_upd)]
    upd_a = _adamw("adamw_w_branch_a", w_branch_a[0], g_aro[0], m_w_branch_a[0], v_w_branch_a[0])
    upd_r = _adamw("adamw_w_branch_r", w_branch_r[0], g_aro[1], m_w_branch_r[0], v_w_branch_r[0])
    upd_o = _adamw("adamw_w_out", w_out[0], g_aro[2], m_w_out[0], v_w_out[0])
    d_aro, nm_aro, nv_aro = zip(upd_a, upd_r, upd_o)
    d_cw, nm_cw, nv_cw = _adamw("adamw_conv_w", conv_w[0], g_conv_w, m_conv_w[0], v_conv_w[0])

    def sharded(t_in, t_aro, t_cw):
        return dict(w_in=t_in[None], conv_w=t_cw[None], w_branch_a=t_aro[0][None], w_branch_r=t_aro[1][None],
                    w_out=t_aro[2][None])

    order = ["pre_norm_w", "w_in", "b_in", "conv_w", "conv_b", "rg_wa", "rg_ba", "rg_wx", "rg_bx", "rg_lambda",
             "w_branch_a", "w_branch_r", "w_out", "post_norm_w"]
    outs = [loss, grad_x]
    for rep, shd in ((grads, sharded(g_w_in, g_aro, g_conv_w)), (deltas, sharded(d_w_in, d_aro, d_cw)),
                     (new_m, sharded(nm_w_in, nm_aro, nm_cw)), (new_v, sharded(nv_w_in, nv_aro, nv_cw))):
        both = {**rep, **shd}
        outs.extend(both[k] for k in order)
    return tuple(outs)
```

```python
import jax
import jax.numpy as jnp
from jax import lax
from jax.experimental import pallas as pl
from jax.experimental.pallas import tpu as pltpu

f32 = jnp.float32
bf16 = jnp.bfloat16

D = 1024
HEADS = 16
HEAD_PAIRS = 8
LANES = 128
NORM_EPS = 1e-6
MASK_VALUE = -1e30
RG_C = 8.0
QK_SCALE = 0.125
TQ = 256
ATT_GROUP = 8
ATT_GROUP_FWD = 16
TL = 512
TM = 512
PREV_ROWS = 16
IN_USED = 8 * D + HEADS
IN_TOTAL = 9 * D + HEADS
N_CHIPS = 4
N_DEV = 8
ADAM_LR, ADAM_B1, ADAM_B2, ADAM_EPS, ADAM_WD, ADAM_STEP = 0.001, 0.9, 0.999, 1e-08, 0.01, 10
VMEM_LIMIT = 56 * 1024 * 1024
MESH = pl.DeviceIdType.MESH


def _dot(a, b):
    return jnp.dot(a, b, preferred_element_type=f32)


def _dot_nt(a, b):
    return lax.dot_general(a, b, (((1,), (1,)), ((), ())), preferred_element_type=f32)


def _dot_tn(a, b):
    return lax.dot_general(a, b, (((0,), (0,)), ((), ())), preferred_element_type=f32)


def _sig(x):
    return 0.5 * jnp.tanh(0.5 * x) + 0.5


def _softplus(x):
    return jnp.maximum(x, 0.0) + jnp.log(1.0 + jnp.exp(-jnp.abs(x)))


def _params(sem, vmem=None):
    return pltpu.CompilerParams(dimension_semantics=sem, vmem_limit_bytes=vmem)


def _tile(tm, width, cb=0):
    return pl.BlockSpec((tm, width), lambda i, cb=cb: (i, cb))


def _whole(shape):
    nd = len(shape)
    return pl.BlockSpec(shape, lambda *_: (0,) * nd)


def _norm_qkv(x, w_pre, w_all, b_qkv, f_row0, b_f, tm=1024):
    t = x.shape[0]
    tm = min(tm, t)
    n = b_qkv.shape[1]

    def body(x_ref, wp_ref, w_ref, b_ref, wf_ref, bf_ref, h_ref, o_ref, f_ref):
        @pl.when(pl.program_id(1) == 0)
        def _():
            xv = x_ref[...]
            r = lax.rsqrt(jnp.mean(xv * xv, axis=-1, keepdims=True) + NORM_EPS)
            h = (xv * r * wp_ref[...]).astype(bf16)
            h_ref[...] = h
            f_ref[...] = _dot_nt(h, wf_ref[...]) + bf_ref[...]

        o_ref[...] = (_dot_nt(h_ref[...], w_ref[...]) + b_ref[...]).astype(bf16)

    return pl.pallas_call(
        body, name="norm_qkv", grid=(t // tm, n // D),
        in_specs=[pl.BlockSpec((tm, D), lambda i, j: (i, 0)), _whole((1, D)), pl.BlockSpec((D, D), lambda i, j: (j, 0)),
                  pl.BlockSpec((1, D), lambda i, j: (0, j)),
                  pl.BlockSpec((LANES, D), lambda i, j: (f_row0 // LANES, 0)), _whole((1, LANES))],
        out_specs=[pl.BlockSpec((tm, D), lambda i, j: (i, 0)), pl.BlockSpec((tm, D), lambda i, j: (i, j)),
                   pl.BlockSpec((tm, LANES), lambda i, j: (i, 0))],
        out_shape=[jax.ShapeDtypeStruct((t, D), bf16), jax.ShapeDtypeStruct((t, n), bf16),
                   jax.ShapeDtypeStruct((t, LANES), f32)],
        compiler_params=_params(("parallel", "arbitrary"), VMEM_LIMIT),
    )(x, w_pre, w_all, b_qkv, w_all, b_f)


def _mm(name, a, w, w_rows, bias, out_dtype, tm, tn):
    t, k = a.shape
    tm = min(tm, t)
    row0, n = w_rows
    assert row0 % tn == 0

    def body(a_ref, w_ref, b_ref, o_ref):
        o_ref[...] = (_dot_nt(a_ref[...], w_ref[...]) + b_ref[...]).astype(out_dtype)

    return pl.pallas_call(
        body, name=name, grid=(t // tm, n // tn),
        in_specs=[pl.BlockSpec((tm, k), lambda i, j: (i, 0)), pl.BlockSpec((tn, k), lambda i, j: (row0 // tn + j, 0)),
                  pl.BlockSpec((1, tn), lambda i, j: (0, j))],
        out_specs=pl.BlockSpec((tm, tn), lambda i, j: (i, j)), out_shape=jax.ShapeDtypeStruct((t, n), out_dtype),
        compiler_params=_params(("parallel", "parallel"), VMEM_LIMIT),
    )(a, w, bias)


def _forget_prep(f128, seq):
    t = f128.shape[0]
    nb = seq // LANES

    def body(f_ref, c_ref):
        r = lax.broadcasted_iota(jnp.int32, (LANES, LANES), 0)
        cidx = lax.broadcasted_iota(jnp.int32, (LANES, LANES), 1)
        tri = (r >= cidx).astype(f32)
        carry = jnp.zeros((1, LANES), f32)
        for blk in range(nb):
            fv = f_ref[pl.ds(blk * LANES, LANES), :]
            lf = -_softplus(-fv)
            c_ref[pl.ds(blk * LANES, LANES), :] = (
                jnp.dot(tri, lf, preferred_element_type=f32, precision=lax.Precision.HIGHEST) + carry)
            carry = carry + jnp.sum(lf, axis=0, keepdims=True)

    return pl.pallas_call(
        body, name="forget_prep", grid=(t // seq,),
        in_specs=[pl.BlockSpec((seq, LANES), lambda b: (b, 0))],
        out_specs=pl.BlockSpec((seq, LANES), lambda b: (b, 0)),
        out_shape=jax.ShapeDtypeStruct((t, LANES), f32),
        compiler_params=_params(("parallel",)),
    )(f128)


def _split3(cv):
    hi = cv.astype(bf16)
    r1 = cv - hi.astype(f32)
    mid = r1.astype(bf16)
    lo = (r1 - mid.astype(f32)).astype(bf16)
    return hi, mid, lo


def _attn_prep(qkv, c):
    t = qkv.shape[0]

    def body(q_ref, k_ref, c_ref, qa_ref, ka_ref):
        lane = lax.broadcasted_iota(jnp.int32, (1, LANES), 1)
        cv = c_ref[...]
        one = jnp.ones((), bf16)
        zero = jnp.zeros((), bf16)
        q_ones = jnp.where((lane >= 67) & (lane < 70), one, zero)
        k_ones = jnp.where((lane >= 64) & (lane < 67), one, zero)
        for head in range(HEADS):
            pair = pl.ds((head // 2) * LANES, LANES)
            ch = jnp.sum(jnp.where(lane == head, cv, 0.0), axis=1, keepdims=True)
            hi, mid, lo = _split3(ch)
            q2, k2 = q_ref[:, pair], k_ref[:, pair]
            if head % 2 == 1:
                q2, k2 = pltpu.roll(q2, 64, 1), pltpu.roll(k2, 64, 1)
            qa = jnp.where(lane < 64, q2 * jnp.asarray(QK_SCALE, bf16),
                           jnp.where(lane == 64, hi, jnp.where(lane == 65, mid, jnp.where(lane == 66, lo, q_ones))))
            ka = jnp.where(lane < 64, k2,
                           jnp.where(lane == 67, -hi, jnp.where(lane == 68, -mid, jnp.where(lane == 69, -lo, k_ones))))
            qa_ref[:, pl.ds(head * LANES, LANES)] = qa
            ka_ref[:, pl.ds(head * LANES, LANES)] = ka

    tm = min(TM, t)
    out = pl.BlockSpec((tm, 2 * D), lambda i: (i, 0))
    return pl.pallas_call(
        body, name="attn_prep", grid=(t // tm,),
        in_specs=[_tile(tm, D, 0), _tile(tm, D, 1), _tile(tm, LANES)],
        out_specs=[out, out],
        out_shape=[jax.ShapeDtypeStruct((t, 2 * D), bf16)] * 2,
        compiler_params=_params(("parallel",)),
    )(qkv, qkv, c)


def _attn_fwd(qa, ka, qkv, rest, seq):
    t = qkv.shape[0]
    nb, nq = t // seq, seq // TQ

    hg = ATT_GROUP_FWD
    ng = HEADS // hg

    def body(q_ref, k_ref, v_ref, ga_ref, o_ref, pa_ref, lse_ref, acc_scr):
        qi, gi = pl.program_id(1), pl.program_id(2)
        krow = lax.broadcasted_iota(jnp.int32, (TQ, TQ), 0)
        qcol = lax.broadcasted_iota(jnp.int32, (TQ, TQ), 1)
        acc_scr[...] = jnp.zeros_like(acc_scr)

        def kv_step(kt, carry, masked):
            ks = pl.multiple_of(kt * TQ, TQ)
            sts = [_dot_nt(k_ref[pl.ds(ks, TQ), pl.ds(g * LANES, LANES)], q_ref[:, pl.ds(g * LANES, LANES)])
                   for g in range(hg)]
            if masked:
                sts = [jnp.where(krow <= qcol, st, MASK_VALUE) for st in sts]
            m_new = [jnp.maximum(carry[g][0], jnp.max(sts[g], axis=0, keepdims=True)) for g in range(hg)]
            ps = [jnp.exp(sts[g] - m_new[g]) for g in range(hg)]
            alphas = [jnp.exp(carry[g][0] - m_new[g]) for g in range(hg)]
            phi = [ps[g].astype(bf16) for g in range(hg)]
            plo = [(ps[g] - phi[g].astype(f32)).astype(bf16) for g in range(hg)]
            vs = [v_ref[pl.ds(ks, TQ), pl.ds(j * LANES, LANES)] for j in range(hg // 2)]
            pvs = [_dot_tn(vs[g // 2], phi[g]) + _dot_tn(vs[g // 2], plo[g]) for g in range(hg)]
            olds = [acc_scr[g] for g in range(hg)]
            for g in range(hg):
                acc_scr[g] = alphas[g] * olds[g] + pvs[g]
            return tuple((m_new[g], alphas[g] * carry[g][1] + jnp.sum(ps[g], axis=0, keepdims=True))
                         for g in range(hg))

        init = tuple((jnp.full((1, TQ), MASK_VALUE, f32), jnp.zeros((1, TQ), f32)) for _ in range(hg))
        carry = lax.fori_loop(0, qi, lambda kt, cr: kv_step(kt, cr, False), init)
        stats = kv_step(qi, carry, True)
        drow = lax.broadcasted_iota(jnp.int32, (LANES, TQ), 0)
        for g in range(hg):
            m, l = stats[g]
            lse_ref[0, pl.ds(hg * gi + g, 1), :] = m + jnp.log(l)
        for j in range(hg // 2):
            o2 = jnp.where(drow < 64, acc_scr[2 * j] / stats[2 * j][1], acc_scr[2 * j + 1] / stats[2 * j + 1][1]).T
            o_ref[:, pl.ds(j * LANES, LANES)] = o2
            ga = ga_ref[:, pl.ds(j * LANES, LANES)].astype(f32)
            pa_ref[:, pl.ds(j * LANES, LANES)] = (o2 * (ga * _sig(ga))).astype(bf16)

    vw = hg * 64
    tile = pl.BlockSpec((TQ, vw), lambda b, qi, gi: (b * nq + qi, gi))
    return pl.pallas_call(
        body, name="attn_fwd", grid=(nb, nq, ng),
        in_specs=[pl.BlockSpec((TQ, hg * LANES), lambda b, qi, gi: (b * nq + qi, gi)),
                  pl.BlockSpec((seq, hg * LANES), lambda b, qi, gi: (b, gi)),
                  pl.BlockSpec((seq, vw), lambda b, qi, gi: (b, 2 * ng + gi)), tile],
        out_specs=[tile, tile, pl.BlockSpec((1, HEADS, TQ), lambda b, qi, gi: (b * nq + qi, 0, 0))],
        out_shape=[jax.ShapeDtypeStruct((t, D), f32), jax.ShapeDtypeStruct((t, D), bf16),
                   jax.ShapeDtypeStruct((t // TQ, HEADS, TQ), f32)],
        scratch_shapes=[pltpu.VMEM((hg, LANES, TQ), f32)],
        compiler_params=_params(("parallel", "parallel", "arbitrary"), VMEM_LIMIT),
    )(qa, ka, qkv, rest)


def _shifted_rows(x, top8, prev8, shift, row, row8):
    body = pltpu.roll(x, shift, 0)
    head = jnp.where(row8 < shift, pltpu.roll(prev8, shift, 0), pltpu.roll(top8, shift, 0))
    return body, head


def _rnn_gates(xc, wa_ref, wx_ref, ba_ref, bx_ref, lam_ref):
    xcb = xc.astype(bf16)
    r = _sig(_dot(xcb, wa_ref[...]) + ba_ref[...])
    i = _sig(_dot(xcb, wx_ref[...]) + bx_ref[...])
    sp = _softplus(-lam_ref[...])
    log_a = (-RG_C) * r * sp
    th = jnp.tanh(log_a)
    w1 = (-2.0) * th / (1.0 - th)
    sq = jnp.sqrt(jnp.maximum(w1, 0.0))
    return r, i, sp, log_a, w1, sq


def _conv_tile(x_ref, xprev_ref, has_prev, cw_ref, cb_ref, xc_ref):
    row = lax.broadcasted_iota(jnp.int32, (TL, D), 0)
    row8 = lax.broadcasted_iota(jnp.int32, (8, D), 0)
    x = x_ref[...].astype(f32)
    top8 = x[:8]
    prev8 = jnp.where(has_prev, xprev_ref[...].astype(f32)[PREV_ROWS - 8:], 0.0)
    xc = cb_ref[...] + cw_ref[pl.ds(3, 1), :] * x
    xc8 = cb_ref[...] + cw_ref[pl.ds(3, 1), :] * top8
    for sh in range(1, 4):
        w = cw_ref[pl.ds(3 - sh, 1), :]
        xs, xs8 = _shifted_rows(x, top8, prev8, sh, row, row8)
        xc = xc + w * xs
        xc8 = xc8 + w * xs8
    xc_ref[...] = xc
    xc_ref[pl.ds(0, 8), :] = xc8


def _rnn_fwd(rest, conv_w, conv_b, wa_d, wx_d, ba, bx, lam, seq):
    t = rest.shape[0]
    nb, nt = t // seq, seq // TL

    def body(x_ref, xprev_ref, gr_ref, cw_ref, cb_ref, wa_ref, wx_ref, ba_ref, bx_ref, lam_ref,
             xc_ref, a_ref, h_ref, pr_ref, xc_scr, u_scr, h_scr, carry):
        tt = pl.program_id(1)
        _conv_tile(x_ref, xprev_ref, tt > 0, cw_ref, cb_ref, xc_scr)
        xc = xc_scr[...]
        xc_ref[...] = xc.astype(bf16)
        r, i, sp, log_a, w1, sq = _rnn_gates(xc, wa_ref, wx_ref, ba_ref, bx_ref, lam_ref)
        a_ref[...] = jnp.exp(log_a)
        u_scr[...] = sq * (i * xc)

        @pl.when(tt == 0)
        def _():
            carry[...] = jnp.zeros_like(carry)

        def step(s, h):
            h = a_ref[pl.ds(s, 1), :] * h + u_scr[pl.ds(s, 1), :]
            h_scr[pl.ds(s, 1), :] = h
            return h

        carry[...] = lax.fori_loop(0, TL, step, carry[...], unroll=8)
        gr = gr_ref[...].astype(f32)
        h = h_scr[...]
        h_ref[...] = h.astype(bf16)
        pr_ref[...] = (h * (gr * _sig(gr))).astype(bf16)

    tile = lambda cb: pl.BlockSpec((TL, D), lambda b, tt, cb=cb: (b * nt + tt, cb))
    prev = lambda cb: pl.BlockSpec(
        (PREV_ROWS, D), lambda b, tt, cb=cb: (jnp.maximum((b * nt + tt) * (TL // PREV_ROWS) - 1, 0), cb))
    vec = _whole((1, D))
    return pl.pallas_call(
        body, name="rnn_fwd", grid=(nb, nt),
        in_specs=[tile(1), prev(1), tile(2), _whole((4, D)), vec, _whole((D, D)), _whole((D, D)), vec, vec, vec],
        out_specs=[tile(0)] * 4,
        out_shape=[jax.ShapeDtypeStruct((t, D), dt) for dt in (bf16, f32, bf16, bf16)],
        scratch_shapes=[pltpu.VMEM((TL, D), f32)] * 3 + [pltpu.VMEM((1, D), f32)],
        compiler_params=_params(("parallel", "arbitrary"), VMEM_LIMIT),
    )(rest, rest, rest, conv_w, conv_b, wa_d, wx_d, ba, bx, lam)


def _merge_loss(rest, pa, pr, o_att, hrec, w_a, w_r, w_out, x, tgt, w_post):
    t = x.shape[0]

    def branch(dy, w_ref, g_ref, act):
        dp = _dot_nt(dy, w_ref[...])
        g = g_ref[...].astype(f32)
        sg = _sig(g)
        return (dp * (g * sg)).astype(bf16), (dp * act * (sg * (1.0 + g * (1.0 - sg)))).astype(bf16)

    def body(mga_ref, mgr_ref, pa_ref, pr_ref, ga_ref, gr_ref, oa_ref, h_ref, x_ref, t_ref, wa_ref, wr_ref, wo_ref,
             w_ref, do_ref, dya_ref, dyr_ref, dmga_ref, dmgr_ref, doa_ref, dga_ref, dh_ref, dgr_ref, mrg_ref, dy_ref,
             delta_ref, loss_ref, dwp_ref):
        @pl.when(pl.program_id(0) == 0)
        def _():
            loss_ref[...] = jnp.zeros_like(loss_ref)
            dwp_ref[...] = jnp.zeros_like(dwp_ref)

        sa, sr = _sig(mga_ref[...].astype(f32)), _sig(mgr_ref[...].astype(f32))
        ya, yr = _dot(pa_ref[...], wa_ref[...]), _dot(pr_ref[...], wr_ref[...])
        mrg = (sa * ya + sr * yr).astype(bf16)
        mrg_ref[...] = mrg
        ov = _dot(mrg, wo_ref[...])
        w = w_ref[...]
        r2 = lax.rsqrt(jnp.mean(ov * ov, axis=-1, keepdims=True) + NORM_EPS)
        oh = ov * r2
        e = x_ref[...] + oh * w - t_ref[...]
        loss_ref[...] += 0.5 * jnp.sum(jnp.mean(e * e, axis=-1, keepdims=True))
        dy = e * (1.0 / D)
        dy_ref[...] = dy
        dwp_ref[...] += jnp.sum(dy * oh, axis=0, keepdims=True)
        doh = dy * w
        do = (r2 * (doh - oh * jnp.mean(doh * oh, axis=-1, keepdims=True))).astype(bf16)
        do_ref[...] = do

        dm = _dot_nt(do, wo_ref[...])
        dya, dyr = (dm * sa).astype(bf16), (dm * sr).astype(bf16)
        dya_ref[...] = dya
        dyr_ref[...] = dyr
        dmga_ref[...] = (dm * ya * sa * (1.0 - sa)).astype(bf16)
        dmgr_ref[...] = (dm * yr * sr * (1.0 - sr)).astype(bf16)
        o_att = oa_ref[...]
        doa, dga_ref[...] = branch(dya, wa_ref, ga_ref, o_att)
        doa_ref[...] = doa
        dh_ref[...], dgr_ref[...] = branch(dyr, wr_ref, gr_ref, h_ref[...].astype(f32))
        ch = lax.broadcasted_iota(jnp.int32, (D, LANES), 0)
        hd = lax.broadcasted_iota(jnp.int32, (D, LANES), 1)
        pick = (ch // 64 == hd).astype(bf16)
        per_head = sum(_dot(piece, pick) for piece in _split3(doa.astype(f32) * o_att))
        delta_ref[0] = per_head.T[:HEADS, :]

    once = pl.BlockSpec((D, D), lambda i: (0, 0), pipeline_mode=pl.Buffered(1))
    rows = _tile(TQ, D)
    return pl.pallas_call(
        body, name="merge_loss", grid=(t // TQ,),
        in_specs=[_tile(TQ, D, 3), _tile(TQ, D, 4), rows, rows, _tile(TQ, D, 0), _tile(TQ, D, 2), rows, rows, rows, rows,
                  once, once, once, _whole((1, D))],
        out_specs=[rows] * 11 + [pl.BlockSpec((1, HEADS, TQ), lambda i: (i, 0, 0)), _whole((8, LANES)), _whole((1, D))],
        out_shape=[jax.ShapeDtypeStruct((t, D), bf16)] * 10 + [jax.ShapeDtypeStruct((t, D), f32),
                   jax.ShapeDtypeStruct((t // TQ, HEADS, TQ), f32), jax.ShapeDtypeStruct((8, LANES), f32),
                   jax.ShapeDtypeStruct((1, D), f32)],
        compiler_params=_params(("arbitrary",), VMEM_LIMIT),
    )(rest, rest, pa, pr, rest, rest, o_att, hrec, x, tgt, w_a, w_r, w_out, w_post)


def _rnn_bwd(dh, a, h, xc, rest, conv_w, conv_b, wa_d, wx_d, ba, bx, lam, seq):
    t = dh.shape[0]
    nb, nt = t // seq, seq // TL
    diag = (D // LANES, LANES, LANES)

    def body(dh_ref, a_ref, h_ref, hprev_ref, xc_ref, x_ref, cw_ref, cb_ref, wa_ref, wx_ref,
             ba_ref, bx_ref, lam_ref, dxr_ref, dwa_ref, dwx_ref, vec_ref, g_scr, dxc_scr, dxr_scr, qcarry, dxc_next):
        b, tt = pl.program_id(0), pl.program_id(1)
        rt = nt - 1 - tt

        @pl.when((b == 0) & (tt == 0))
        def _():
            dwa_ref[...] = jnp.zeros_like(dwa_ref)
            dwx_ref[...] = jnp.zeros_like(dwx_ref)
            vec_ref[...] = jnp.zeros_like(vec_ref)

        @pl.when(tt == 0)
        def _():
            qcarry[...] = jnp.zeros_like(qcarry)
            dxc_next[...] = jnp.zeros_like(dxc_next)

        g_scr[...] = dh_ref[...].astype(f32)

        def step(k, q):
            s = TL - 1 - k
            g = g_scr[pl.ds(s, 1), :] + q
            g_scr[pl.ds(s, 1), :] = g
            return a_ref[pl.ds(s, 1), :] * g

        qcarry[...] = lax.fori_loop(0, TL, step, qcarry[...], unroll=8)

        row = lax.broadcasted_iota(jnp.int32, (TL, D), 0)
        row8 = lax.broadcasted_iota(jnp.int32, (8, D), 0)
        g = g_scr[...]
        av = a_ref[...]
        xc = xc_ref[...].astype(f32)
        hlast = jnp.where(rt > 0, hprev_ref[...].astype(f32)[PREV_ROWS - 1:], 0.0)
        hp = jnp.where(row == 0, hlast, pltpu.roll(h_ref[...].astype(f32), 1, 0))
        r, i, sp, log_a, w1, sq = _rnn_gates(xc, wa_ref, wx_ref, ba_ref, bx_ref, lam_ref)
        dix = g * sq
        di = dix * xc
        dxc = dix * i
        dsq = g * (i * xc)
        dlog_a = g * hp * av - dsq * jnp.where(sq > 0.0, (1.0 - w1) / sq, 0.0)
        dpr = (dlog_a * ((-RG_C) * sp)) * r * (1.0 - r)
        dpi = di * i * (1.0 - i)
        dprb, dpib, xcb = dpr.astype(bf16), dpi.astype(bf16), xc.astype(bf16)
        dxc = dxc + _dot_nt(dprb, wa_ref[...]) + _dot_nt(dpib, wx_ref[...])
        for j in range(D // LANES):
            cols = slice(j * LANES, (j + 1) * LANES)
            dwa_ref[j] += _dot_tn(xcb[:, cols], dprb[:, cols])
            dwx_ref[j] += _dot_tn(xcb[:, cols], dpib[:, cols])
        vec_ref[pl.ds(0, 1), :] += jnp.sum(dpr, axis=0, keepdims=True)
        vec_ref[pl.ds(1, 1), :] += jnp.sum(dpi, axis=0, keepdims=True)
        dsp = jnp.sum(dlog_a * ((-RG_C) * r), axis=0, keepdims=True)
        vec_ref[pl.ds(2, 1), :] += dsp * (-_sig(-lam_ref[...]))
        vec_ref[pl.ds(3, 1), :] += jnp.sum(dxc, axis=0, keepdims=True)

        dxc_scr[...] = dxc
        bot8 = dxc_scr[pl.ds(TL - 8, 8), :]
        nxt8 = dxc_next[...]
        x = x_ref[...].astype(f32)
        x_bot8 = x[TL - 8:]
        dxr = cw_ref[pl.ds(3, 1), :] * dxc
        dxr8 = cw_ref[pl.ds(3, 1), :] * bot8
        vec_ref[pl.ds(7, 1), :] += jnp.sum(dxc * x, axis=0, keepdims=True)
        for sh in range(1, 4):
            w = cw_ref[pl.ds(3 - sh, 1), :]
            up = pltpu.roll(dxc, TL - sh, 0)
            from_next = pltpu.roll(nxt8, 8 - sh, 0)
            dxr = dxr + w * up
            dxr8 = dxr8 + w * jnp.where(row8 < 8 - sh, pltpu.roll(bot8, 8 - sh, 0), from_next)
            inside = jnp.sum(jnp.where(row < TL - sh, up, 0.0) * x, axis=0, keepdims=True)
            across = jnp.sum(jnp.where(row8 >= 8 - sh, from_next, 0.0) * x_bot8, axis=0, keepdims=True)
            vec_ref[pl.ds(7 - sh, 1), :] += inside + across
        dxr_scr[...] = dxr
        dxr_scr[pl.ds(TL - 8, 8), :] = dxr8
        dxr_ref[...] = dxr_scr[...].astype(bf16)
        dxc_next[...] = dxc_scr[pl.ds(0, 8), :]

    tile = lambda cb: pl.BlockSpec((TL, D), lambda b, tt, cb=cb: (b * nt + nt - 1 - tt, cb))
    prev = lambda cb: pl.BlockSpec(
        (PREV_ROWS, D), lambda b, tt, cb=cb: (jnp.maximum((b * nt + nt - 1 - tt) * (TL // PREV_ROWS) - 1, 0), cb))
    vec = _whole((1, D))
    return pl.pallas_call(
        body, name="rnn_bwd", grid=(nb, nt),
        in_specs=[tile(0), tile(0), tile(0), prev(0), tile(0), tile(1),
                  _whole((4, D)), vec, _whole((D, D)), _whole((D, D)), vec, vec, vec],
        out_specs=[tile(0), _whole(diag), _whole(diag), _whole((8, D))],
        out_shape=[jax.ShapeDtypeStruct((t, D), bf16), jax.ShapeDtypeStruct(diag, f32),
                   jax.ShapeDtypeStruct(diag, f32), jax.ShapeDtypeStruct((8, D), f32)],
        scratch_shapes=[pltpu.VMEM((TL, D), f32), pltpu.VMEM((TL, D), f32), pltpu.VMEM((TL, D), f32),
                        pltpu.VMEM((1, D), f32), pltpu.VMEM((8, D), f32)],
        compiler_params=_params(("arbitrary", "arbitrary"), VMEM_LIMIT),
    )(dh, a, h, h, xc, rest, conv_w, conv_b, wa_d, wx_d, ba, bx, lam)


def _attn_bwd(qa, ka, qkv, doa, lse, delta, seq):
    t = qkv.shape[0]
    nb, nq = t // seq, seq // TQ
    hg = ATT_GROUP
    ng, npair = HEADS // hg, hg // 2

    def body(qa_ref, ka_ref, q_ref, k_ref, v_ref, do_ref, lse_ref, dl_ref, dq_ref, dk_ref, dv_ref, dc_ref,
             dqt_scr, dk_scr, dv_scr, ds_scr, kht_scr):
        gi, kt = pl.program_id(1), pl.program_id(2)
        lane = lax.broadcasted_iota(jnp.int32, (1, LANES), 1)
        krow = lax.broadcasted_iota(jnp.int32, (TQ, TQ), 0)
        qcol = lax.broadcasted_iota(jnp.int32, (TQ, TQ), 1)
        lmask = [(lane // 64) == hh for hh in range(2)]
        scale = jnp.asarray(QK_SCALE, bf16)

        @pl.when(kt == 0)
        def _():
            dqt_scr[...] = jnp.zeros_like(dqt_scr)

        dk_scr[...] = jnp.zeros_like(dk_scr)
        dv_scr[...] = jnp.zeros_like(dv_scr)
        ds_scr[...] = jnp.zeros_like(ds_scr)
        for g in range(hg):
            k2 = k_ref[:, pl.ds((g // 2) * LANES, LANES)]
            kht_scr[g] = jnp.where(lmask[g % 2], k2, jnp.zeros_like(k2)).T

        def q_step(qt, masked):
            qs = pl.multiple_of(qt * TQ, TQ)
            heads = range(hg)
            do2 = [do_ref[pl.ds(qs, TQ), pl.ds(j * LANES, LANES)] for j in range(npair)]
            q2 = [q_ref[pl.ds(qs, TQ), pl.ds(j * LANES, LANES)] for j in range(npair)]
            doh = [jnp.where(lmask[g % 2], do2[g // 2], jnp.zeros_like(do2[0])) for g in heads]
            qh = [jnp.where(lmask[g % 2], q2[g // 2], jnp.zeros_like(q2[0])) * scale for g in heads]
            st = [_dot_nt(ka_ref[:, pl.ds(g * LANES, LANES)], qa_ref[pl.ds(qs, TQ), pl.ds(g * LANES, LANES)])
                  for g in heads]
            if masked:
                st = [jnp.where(krow <= qcol, s, MASK_VALUE) for s in st]
            dp = [_dot_nt(v_ref[:, pl.ds((g // 2) * LANES, LANES)], doh[g]) for g in heads]
            p = [jnp.exp(st[g] - lse_ref[qt, pl.ds(hg * gi + g, 1), :]) for g in heads]
            ds = [p[g] * (dp[g] - dl_ref[qt, pl.ds(hg * gi + g, 1), :]) for g in heads]
            pb = [x.astype(bf16) for x in p]
            dsb = [x.astype(bf16) for x in ds]
            for j in range(npair):
                a, b = 2 * j, 2 * j + 1
                dv_scr[j] += _dot(pb[a], doh[a]) + _dot(pb[b], doh[b])
                dk_scr[j] += _dot(dsb[a], qh[a]) + _dot(dsb[b], qh[b])
                dqt_scr[qt, j] += (_dot(kht_scr[a], dsb[a]) + _dot(kht_scr[b], dsb[b])) * QK_SCALE
            for g in heads:
                ds_scr[g] += ds[g][:, :LANES] + ds[g][:, LANES:]

        q_step(kt, True)

        def loop_body(qt, carry):
            q_step(qt, False)
            return carry

        lax.fori_loop(kt + 1, nq, loop_body, 0)

        dc = jnp.zeros((TQ, LANES), f32)
        for g in range(hg):
            dc = jnp.where(lane == g, -jnp.sum(ds_scr[g], axis=1, keepdims=True), dc)
        dc_ref[...] = dc
        for j in range(npair):
            dk_ref[:, pl.ds(j * LANES, LANES)] = dk_scr[j].astype(bf16)
            dv_ref[:, pl.ds(j * LANES, LANES)] = dv_scr[j].astype(bf16)

        @pl.when(kt == nq - 1)
        def _():
            for qt in range(nq):
                for j in range(npair):
                    dq_ref[pl.ds(qt * TQ, TQ), pl.ds(j * LANES, LANES)] = dqt_scr[qt, j].T.astype(bf16)

    vw = hg * 64
    seqspec = pl.BlockSpec((seq, vw), lambda b, gi, kt: (b, gi))
    kspec = lambda off: pl.BlockSpec((TQ, vw), lambda b, gi, kt: (b * nq + kt, off + gi))
    rowspec = pl.BlockSpec((nq, HEADS, TQ), lambda b, gi, kt: (b, 0, 0))
    return pl.pallas_call(
        body, name="attn_bwd", grid=(nb, ng, nq),
        in_specs=[pl.BlockSpec((seq, hg * LANES), lambda b, gi, kt: (b, gi)),
                  pl.BlockSpec((TQ, hg * LANES), lambda b, gi, kt: (b * nq + kt, gi)),
                  seqspec, kspec(ng), kspec(2 * ng), seqspec, rowspec, rowspec],
        out_specs=[seqspec, kspec(0), kspec(0), pl.BlockSpec((TQ, LANES), lambda b, gi, kt: (b * nq + kt, gi))],
        out_shape=[jax.ShapeDtypeStruct((t, D), bf16)] * 3 + [jax.ShapeDtypeStruct((t, ng * LANES), f32)],
        scratch_shapes=[pltpu.VMEM((nq, npair, LANES, TQ), f32), pltpu.VMEM((npair, TQ, LANES), f32),
                        pltpu.VMEM((npair, TQ, LANES), f32), pltpu.VMEM((hg, TQ, LANES), f32),
                        pltpu.VMEM((hg, LANES, TQ), bf16)],
        compiler_params=_params(("parallel", "parallel", "arbitrary"), VMEM_LIMIT),
    )(qa, ka, qkv, qkv, qkv, doa, lse, delta)


def _forget_bwd(dc, f128, seq):
    t = f128.shape[0]
    nb = seq // LANES
    groups = dc.shape[1] // LANES

    def body(dc_ref, f_ref, df_ref, dbf_ref):
        @pl.when(pl.program_id(0) == 0)
        def _():
            dbf_ref[...] = jnp.zeros_like(dbf_ref)

        r = lax.broadcasted_iota(jnp.int32, (LANES, LANES), 0)
        cidx = lax.broadcasted_iota(jnp.int32, (LANES, LANES), 1)
        tri = (r <= cidx).astype(f32)
        carry = jnp.zeros((1, LANES), f32)
        total = jnp.zeros((1, LANES), f32)
        for blk in reversed(range(nb)):
            dcb = dc_ref[pl.ds(blk * LANES, LANES), pl.ds(0, LANES)]
            for gi in range(1, groups):
                dcb = dcb + pltpu.roll(dc_ref[pl.ds(blk * LANES, LANES), pl.ds(gi * LANES, LANES)], gi * ATT_GROUP, 1)
            dlf = jnp.dot(tri, dcb, preferred_element_type=f32, precision=lax.Precision.HIGHEST) + carry
            df = dlf * _sig(-f_ref[pl.ds(blk * LANES, LANES), :])
            df_ref[pl.ds(blk * LANES, LANES), :] = df.astype(bf16)
            total = total + jnp.sum(df, axis=0, keepdims=True)
            carry = carry + jnp.sum(dcb, axis=0, keepdims=True)
        dbf_ref[...] += total

    return pl.pallas_call(
        body, name="forget_bwd", grid=(t // seq,),
        in_specs=[pl.BlockSpec((seq, groups * LANES), lambda b: (b, 0)), pl.BlockSpec((seq, LANES), lambda b: (b, 0))],
        out_specs=[pl.BlockSpec((seq, LANES), lambda b: (b, 0)), _whole((1, LANES))],
        out_shape=[jax.ShapeDtypeStruct((t, LANES), bf16), jax.ShapeDtypeStruct((1, LANES), f32)],
        compiler_params=_params(("arbitrary",)),
    )(dc, f128)


def _in_bwd(dz, df, x, dy, w_all, w_pre):
    t = x.shape[0]
    n_dz = len(dz)

    def body(*refs):
        dz_refs = refs[:n_dz]
        df_ref, x_ref, dy_ref, w_ref, wp_ref, gx_ref, dwp_ref = refs[n_dz:]

        @pl.when(pl.program_id(0) == 0)
        def _():
            dwp_ref[...] = jnp.zeros_like(dwp_ref)

        dh = _dot(df_ref[...], w_ref[pl.ds(n_dz * D, LANES), :])
        for p in range(n_dz):
            dh = dh + _dot(dz_refs[p][...], w_ref[pl.ds(p * D, D), :])
        xv = x_ref[...]
        r1 = lax.rsqrt(jnp.mean(xv * xv, axis=-1, keepdims=True) + NORM_EPS)
        xh = xv * r1
        dwp_ref[...] += jnp.sum(dh * xh, axis=0, keepdims=True)
        dxh = dh * wp_ref[...]
        gx_ref[...] = dy_ref[...] + r1 * (dxh - xh * jnp.mean(dxh * xh, axis=-1, keepdims=True))

    once = lambda shape: pl.BlockSpec(shape, lambda i: (0, 0), pipeline_mode=pl.Buffered(1))
    return pl.pallas_call(
        body, name="in_bwd", grid=(t // TM,),
        in_specs=[_tile(TM, D)] * n_dz + [_tile(TM, LANES), _tile(TM, D), _tile(TM, D), once(w_all.shape),
                  _whole((1, D))],
        out_specs=[_tile(TM, D), _whole((1, D))],
        out_shape=[jax.ShapeDtypeStruct((t, D), f32), jax.ShapeDtypeStruct((1, D), f32)],
        compiler_params=_params(("arbitrary",), VMEM_LIMIT),
    )(*dz, df, x, dy, w_all, w_pre)


def _tn_mm(name, a, b, tn, out_dtype=f32, tk=2048):
    t, k = a.shape
    tk = min(tk, t)
    n = b.shape[1]
    nk = t // tk

    def body(a_ref, b_ref, o_ref, s_ref, acc_ref):
        j, kk = pl.program_id(0), pl.program_id(1)

        @pl.when(kk == 0)
        def _():
            acc_ref[...] = jnp.zeros_like(acc_ref)

        @pl.when((j == 0) & (kk == 0))
        def _():
            s_ref[...] = jnp.zeros_like(s_ref)

        av = a_ref[...]
        acc_ref[...] += _dot_tn(av, b_ref[...])

        @pl.when(j == 0)
        def _():
            s_ref[...] += jnp.sum(av.astype(f32), axis=0, keepdims=True)

        @pl.when(kk == nk - 1)
        def _():
            o_ref[...] = acc_ref[...].astype(out_dtype)

    return pl.pallas_call(
        body, name=name, grid=(n // tn, nk),
        in_specs=[pl.BlockSpec((tk, k), lambda j, kk: (kk, 0)), pl.BlockSpec((tk, tn), lambda j, kk: (kk, j))],
        out_specs=[pl.BlockSpec((k, tn), lambda j, kk: (0, j)), _whole((1, k))],
        out_shape=[jax.ShapeDtypeStruct((k, n), out_dtype), jax.ShapeDtypeStruct((1, k), f32)],
        scratch_shapes=[pltpu.VMEM((k, tn), f32)],
        compiler_params=_params(("arbitrary", "arbitrary"), VMEM_LIMIT),
    )(a, b)


def _position():
    return lax.axis_index("x"), lax.axis_index("y"), lax.axis_index("c")


ROW_BLOCK = 128


def _pick_rows(layout, first, count):
    acc = jnp.zeros((ROW_BLOCK, D), f32)
    seg_start = 0
    for ref, ref_row, rows in layout:
        lo, hi = max(first, seg_start), min(first + count, seg_start + rows)
        if lo < hi and ref is not None:
            off, take, done = ref_row + lo - seg_start, hi - lo, lo - first
            start = off // 16 * 16
            win = -(-(off - start + take) // 16) * 16
            r = lax.broadcasted_iota(jnp.int32, (ROW_BLOCK, win), 0)
            col = lax.broadcasted_iota(jnp.int32, (ROW_BLOCK, win), 1)
            pick = ((col - r == off - start - done) & (r >= done) & (r < done + take)).astype(bf16)
            acc = acc + _dot(pick, ref[pl.ds(start, win), :])
        seg_start += rows
    return acc


def _assemble_rows(shards_ref, shard_rows, segments, out_ref):
    layout = [(shards_ref.at[j], 0, shard_rows) for j in range(shards_ref.shape[0])]
    for out0, log0, count in segments:
        for b0 in range(0, count, ROW_BLOCK):
            block = _pick_rows(layout, log0 + b0, min(ROW_BLOCK, count - b0))
            out_ref[pl.ds(out0 + b0, ROW_BLOCK), :] = block.astype(bf16)


def _pack_pieces(blocks, shard_rows, padded):
    arrays = [a for a, _ in blocks if a is not None]
    piece_rows = padded // 2

    def body(*refs):
        out_ref = refs[-1]
        it = iter(refs[:-1])
        layout = [(None if a is None else next(it), 0, rows) for a, rows in blocks]
        for k in range(N_DEV):
            chip, half = divmod(k, 2)
            for b0 in range(0, piece_rows, ROW_BLOCK):
                n = min(ROW_BLOCK, piece_rows - b0)
                in_shard = half * piece_rows + b0
                count = max(0, min(n, shard_rows - in_shard))
                block = _pick_rows(layout, chip * shard_rows + in_shard, count)
                out_ref[k, pl.ds(b0, n), :] = block[:n].astype(bf16)

    vm = pl.BlockSpec(memory_space=pltpu.VMEM)
    return pl.pallas_call(
        body, name="pack_pieces", in_specs=[vm] * len(arrays), out_specs=vm,
        out_shape=jax.ShapeDtypeStruct((N_DEV, piece_rows, D), bf16),
        compiler_params=pltpu.CompilerParams(vmem_limit_bytes=VMEM_LIMIT),
    )(*arrays)


def _gather_shards(parts, small, shard_rows, segments, out_rows):
    n = len(parts)
    halves = [p.shape[0] // 2 for p in parts]
    cuts = [-(-h // 32) * 16 for h in halves]
    n_direct, n_relay, n_sib = 4 * n, 2 * n, 6 * n

    def body(*refs):
        srcs, small_src = refs[:n], refs[n]
        dsts, small_dst, whole_ref = refs[n + 1:2 * n + 1], refs[2 * n + 1], refs[2 * n + 2]
        send, recv, local = refs[2 * n + 3:]
        x, y, c = _position()
        me = 2 * x + y
        chips = [(1 - x, y), (x, 1 - y), (1 - x, 1 - y)]
        ids = [2 * px + py for px, py in chips]

        def rows(a, half, quarter):
            start = half * halves[a] + (cuts[a] if quarter else 0)
            return pl.ds(start, halves[a] - cuts[a] if quarter else cuts[a])

        def landing(a, shard, half, quarter):
            return dsts[a].at[shard, rows(a, half, quarter), :]

        def direct(a, nb, quarter, shard):
            k = (a * 2 + nb) * 2 + quarter
            px, py = chips[nb]
            return pltpu.make_async_remote_copy(
                src_ref=srcs[a].at[rows(a, c, quarter), :], dst_ref=landing(a, shard, c, quarter),
                send_sem=send.at[k], recv_sem=recv.at[k], device_id=(px, py, c), device_id_type=MESH)

        def relay(a, quarter, shard):
            k = n_direct + a * 2 + quarter
            px, py = chips[1 - quarter]
            return pltpu.make_async_remote_copy(
                src_ref=landing(a, shard, c, quarter), dst_ref=landing(a, shard, c, quarter),
                send_sem=send.at[k], recv_sem=recv.at[k], device_id=(px, py, c), device_id_type=MESH)

        def to_sibling(a, origin, quarter, half):
            k = n_direct + n_relay + (a * 3 + origin) * 2 + quarter
            return pltpu.make_async_remote_copy(
                src_ref=landing(a, ids[origin], half, quarter), dst_ref=landing(a, ids[origin], half, quarter),
                send_sem=send.at[k], recv_sem=recv.at[k], device_id=(x, y, 1 - c), device_id_type=MESH)

        def small_copy(j, shard):
            k = n_direct + n_relay + n_sib + j
            px, py = chips[j]
            return pltpu.make_async_remote_copy(
                src_ref=small_src, dst_ref=small_dst.at[shard], send_sem=send.at[k], recv_sem=recv.at[k],
                device_id=(px, py, c), device_id_type=MESH)

        own = [pltpu.make_async_copy(srcs[a], dsts[a].at[me], local.at[a]) for a in range(n)]
        own.append(pltpu.make_async_copy(small_src, small_dst.at[me], local.at[n]))
        for cp in own:
            cp.start()
        sent = [direct(a, nb, q, me) for q in range(2) for a in range(n) for nb in range(2)]
        sent += [small_copy(j, me) for j in range(3)]
        for cp in sent:
            cp.start()

        def passed_on(cp):
            cp.start()
            sent.append(cp)

        for q in range(2):
            for a in range(n):
                for nb in range(2):
                    direct(a, nb, q, ids[nb]).wait_recv()
                    passed_on(to_sibling(a, nb, q, c))
                    if nb == q:
                        passed_on(relay(a, q, ids[nb]))
        for a in range(n):
            for q in range(2):
                relay(a, q, ids[2]).wait_recv()
                passed_on(to_sibling(a, 2, q, c))
        for j in range(3):
            small_copy(j, ids[j]).wait_recv()
            for a in range(n):
                for q in range(2):
                    to_sibling(a, j, q, 1 - c).wait_recv()
        for cp in sent:
            cp.wait_send()
        for cp in own:
            cp.wait()
        _assemble_rows(dsts[0], shard_rows, segments, whole_ref)

    vm = pl.BlockSpec(memory_space=pltpu.VMEM)
    n_sems = n_direct + n_relay + n_sib + 3
    out = pl.pallas_call(
        body, name="gather_shards",
        in_specs=[vm] * (n + 1), out_specs=[vm] * (n + 2),
        out_shape=[jax.ShapeDtypeStruct((N_CHIPS,) + p.shape, p.dtype) for p in parts + [small]]
        + [jax.ShapeDtypeStruct((out_rows, parts[0].shape[1]), parts[0].dtype)],
        scratch_shapes=[pltpu.SemaphoreType.DMA((n_sems,)), pltpu.SemaphoreType.DMA((n_sems,)),
                        pltpu.SemaphoreType.DMA((n + 1,))],
        compiler_params=pltpu.CompilerParams(vmem_limit_bytes=VMEM_LIMIT),
    )(*parts, small)
    return out[1:]


def _allsum_rows(part):
    rows_n = part.shape[0]

    def body(x_ref, gath_ref, sum_ref, send_sems, recv_sems, local_sem):
        x, y, c = _position()
        me, sibling = (x, y, c), (x, y, 1 - c)
        chips = [(1 - x, y), (x, 1 - y), (1 - x, 1 - y)]

        def rows(px, py, pc):
            return gath_ref.at[pl.ds((4 * px + 2 * py + pc) * rows_n, rows_n), :]

        def copy(k, block, to, src=None):
            return pltpu.make_async_remote_copy(
                src_ref=rows(*block) if src is None else src, dst_ref=rows(*block),
                send_sem=send_sems.at[k], recv_sem=recv_sems.at[k], device_id=to, device_id_type=MESH)

        mine = pltpu.make_async_copy(x_ref, rows(*me), local_sem)
        mine.start()
        first = [copy(0, me, sibling, src=x_ref)]
        first += [copy(1 + j, me, (*chip, c), src=x_ref) for j, chip in enumerate(chips)]
        for cp in first:
            cp.start()
        passed = [copy(4 + j, (*chip, c), sibling) for j, chip in enumerate(chips)]
        for j, chip in enumerate(chips):
            copy(1 + j, (*chip, c), me).wait_recv()
            passed[j].start()
        copy(0, sibling, me).wait_recv()
        for j, chip in enumerate(chips):
            copy(4 + j, (*chip, 1 - c), me).wait_recv()
        for cp in first + passed:
            cp.wait_send()
        mine.wait()
        total = gath_ref[pl.ds(0, rows_n), :]
        for d in range(1, N_DEV):
            total = total + gath_ref[pl.ds(d * rows_n, rows_n), :]
        sum_ref[...] = total

    vm = pl.BlockSpec(memory_space=pltpu.VMEM)
    return pl.pallas_call(
        body, name="allsum_rows", in_specs=[vm], out_specs=[vm, vm],
        out_shape=[jax.ShapeDtypeStruct((N_DEV * rows_n, D), f32), jax.ShapeDtypeStruct((rows_n, D), f32)],
        scratch_shapes=[pltpu.SemaphoreType.DMA((7,)), pltpu.SemaphoreType.DMA((7,)), pltpu.SemaphoreType.DMA],
    )(part)[1]


PAIR_ROWS = 16


def _pair_reduce(name, pieces):
    _, r, n = pieces.shape

    def body(p_ref, o_ref, land, send, recv):
        x, y, c = _position()

        def remote(j, half):
            return pltpu.make_async_remote_copy(
                src_ref=p_ref.at[2 * j + half], dst_ref=land.at[j], send_sem=send.at[j], recv_sem=recv.at[j],
                device_id=(x, y, 1 - c), device_id_type=MESH)

        sends = [remote(j, 1 - c) for j in range(N_CHIPS)]
        for cp in sends:
            cp.start()
        for j in range(N_CHIPS):
            remote(j, c).wait_recv()

            def add_rows(i, carry, j=j):
                rows = pl.ds(pl.multiple_of(i * PAIR_ROWS, PAIR_ROWS), PAIR_ROWS)
                o_ref[j, rows, :] = (p_ref[2 * j + c, rows, :].astype(f32) + land[j, rows, :].astype(f32)).astype(bf16)
                return carry

            lax.fori_loop(0, r // PAIR_ROWS, add_rows, 0)
        for cp in sends:
            cp.wait_send()

    vm = pl.BlockSpec(memory_space=pltpu.VMEM)
    return pl.pallas_call(
        body, name=name, in_specs=[vm], out_specs=vm,
        out_shape=jax.ShapeDtypeStruct((N_CHIPS, r, n), bf16),
        scratch_shapes=[pltpu.VMEM((N_CHIPS, r, n), bf16), pltpu.SemaphoreType.DMA((N_CHIPS,)),
                        pltpu.SemaphoreType.DMA((N_CHIPS,))],
        compiler_params=pltpu.CompilerParams(vmem_limit_bytes=VMEM_LIMIT),
    )(pieces)


def _chip_exchange(arrs):
    n = len(arrs)
    heights = [a.shape[1] for a in arrs]
    cuts = [-(-r // 32) * 16 for r in heights]

    def body(*refs):
        srcs, dsts, relays = refs[:n], refs[n:2 * n], refs[2 * n:3 * n]
        send, recv, local = refs[3 * n:]
        x, y, c = _position()
        me = 2 * x + y
        chips = [(1 - x, y), (x, 1 - y), (1 - x, 1 - y)]
        ids = [2 * px + py for px, py in chips]

        def rows(a, quarter):
            return pl.ds(cuts[a], heights[a] - cuts[a]) if quarter else pl.ds(0, cuts[a])

        def held(a, quarter):
            size = heights[a] - cuts[a] if quarter else cuts[a]
            return relays[a].at[quarter, pl.ds(0, size), :]

        def direct(a, nb, piece, landing):
            px, py = chips[nb]
            return pltpu.make_async_remote_copy(
                src_ref=srcs[a].at[piece], dst_ref=dsts[a].at[landing], send_sem=send.at[a * 2 + nb],
                recv_sem=recv.at[a * 2 + nb], device_id=(px, py, c), device_id_type=MESH)

        def first_hop(a, quarter):
            k = 2 * n + a * 2 + quarter
            px, py = chips[quarter]
            return pltpu.make_async_remote_copy(
                src_ref=srcs[a].at[ids[2], rows(a, quarter), :], dst_ref=held(a, quarter), send_sem=send.at[k],
                recv_sem=recv.at[k], device_id=(px, py, c), device_id_type=MESH)

        def second_hop(a, quarter, origin):
            k = 4 * n + a * 2 + quarter
            px, py = chips[1 - quarter]
            return pltpu.make_async_remote_copy(
                src_ref=held(a, quarter), dst_ref=dsts[a].at[origin, rows(a, quarter), :], send_sem=send.at[k],
                recv_sem=recv.at[k], device_id=(px, py, c), device_id_type=MESH)

        own = [pltpu.make_async_copy(srcs[a].at[me], dsts[a].at[me], local.at[a]) for a in range(n)]
        sent = [first_hop(a, q) for a in range(n) for q in range(2)]
        sent += [direct(a, nb, ids[nb], me) for a in range(n) for nb in range(2)]
        for cp in sent + own:
            cp.start()
        for a in range(n):
            for q in range(2):
                first_hop(a, q).wait_recv()
                sent.append(second_hop(a, q, ids[q]))
                sent[-1].start()
        for a in range(n):
            for nb in range(2):
                direct(a, nb, me, ids[nb]).wait_recv()
            for q in range(2):
                second_hop(a, q, ids[2]).wait_recv()
        for cp in sent:
            cp.wait_send()
        for cp in own:
            cp.wait()

    anyspec = pl.BlockSpec(memory_space=pl.ANY)
    out = pl.pallas_call(
        body, name="chip_exchange", in_specs=[anyspec] * n, out_specs=[anyspec] * (2 * n),
        out_shape=[jax.ShapeDtypeStruct(a.shape, a.dtype) for a in arrs]
        + [jax.ShapeDtypeStruct((2, cut, a.shape[2]), a.dtype) for a, cut in zip(arrs, cuts)],
        scratch_shapes=[pltpu.SemaphoreType.DMA((6 * n,)), pltpu.SemaphoreType.DMA((6 * n,)),
                        pltpu.SemaphoreType.DMA((n,))],
    )(*arrs)
    return out[:n]


def _sum_swap_halves(slots):
    n = len(slots)

    def body(*refs):
        srcs, dsts, halves = refs[:n], refs[n:2 * n], refs[2 * n:3 * n]
        send, recv, local = refs[3 * n:]
        x, y, c = _position()

        def remote(a, landing):
            return pltpu.make_async_remote_copy(
                src_ref=halves[a], dst_ref=dsts[a].at[landing], send_sem=send.at[a], recv_sem=recv.at[a],
                device_id=(x, y, 1 - c), device_id_type=MESH)

        for a in range(n):
            def add_rows(i, carry, a=a):
                rows = pl.ds(pl.multiple_of(i * PAIR_ROWS, PAIR_ROWS), PAIR_ROWS)
                total = srcs[a][0, rows, :].astype(f32)
                for s in range(1, N_CHIPS):
                    total = total + srcs[a][s, rows, :].astype(f32)
                halves[a][rows, :] = total
                return carry

            lax.fori_loop(0, srcs[a].shape[1] // PAIR_ROWS, add_rows, 0)
        own = [pltpu.make_async_copy(halves[a], dsts[a].at[c], local.at[a]) for a in range(n)]
        sends = [remote(a, c) for a in range(n)]
        for cp in sends + own:
            cp.start()
        for a in range(n):
            remote(a, 1 - c).wait_recv()
        for cp in sends:
            cp.wait_send()
        for cp in own:
            cp.wait()

    vm = pl.BlockSpec(memory_space=pltpu.VMEM)
    return pl.pallas_call(
        body, name="sum_swap_halves", in_specs=[vm] * n, out_specs=[vm] * n,
        out_shape=[jax.ShapeDtypeStruct((2,) + a.shape[1:], f32) for a in slots],
        scratch_shapes=[pltpu.VMEM(a.shape[1:], f32) for a in slots]
        + [pltpu.SemaphoreType.DMA((n,)), pltpu.SemaphoreType.DMA((n,)), pltpu.SemaphoreType.DMA((n,))],
        compiler_params=pltpu.CompilerParams(vmem_limit_bytes=VMEM_LIMIT),
    )(*slots)


def _row_block(r):
    return 128 if r % 128 == 0 else r


def _adamw(name, w, g, m, v):
    r, n = w.shape
    if r % 128 == 0 or r * n <= 128 * 1024:
        rb, nb = _row_block(r), n
    else:
        rb, nb = r, LANES

    def body(w_ref, g_ref, m_ref, v_ref, d_ref, nm_ref, nv_ref):
        gv = g_ref[...]
        m2 = ADAM_B1 * m_ref[...] + (1.0 - ADAM_B1) * gv
        v2 = ADAM_B2 * v_ref[...] + (1.0 - ADAM_B2) * (gv * gv)
        m_hat = m2 / (1.0 - ADAM_B1 ** ADAM_STEP)
        v_hat = v2 / (1.0 - ADAM_B2 ** ADAM_STEP)
        d_ref[...] = (-ADAM_LR) * (m_hat / (jnp.sqrt(v_hat) + ADAM_EPS) + ADAM_WD * w_ref[...])
        nm_ref[...] = m2
        nv_ref[...] = v2

    spec = pl.BlockSpec((rb, nb), lambda i, j: (i, j))
    return pl.pallas_call(
        body, name=name, grid=(r // rb, n // nb), in_specs=[spec] * 4, out_specs=[spec] * 3,
        out_shape=[jax.ShapeDtypeStruct((r, n), f32)] * 3,
        compiler_params=_params(("parallel", "parallel"), VMEM_LIMIT),
    )(w, g, m, v)


def _local_step(x2, tgt2, seq, wt):
    nb = x2.shape[0] // seq
    h, qkv, f128 = _norm_qkv(x2, wt["pre_w"], wt["w_all"], wt["b_qkv"], 8 * D, wt["b_f"])
    rest = _mm("in_rest", h, wt["w_all"], (3 * D, 5 * D), wt["b_rest"], bf16, 1024, 1024)
    c = _forget_prep(f128, seq)
    qa, ka = _attn_prep(qkv, c)
    o_att, pa, lse = _attn_fwd(qa, ka, qkv, rest, seq)
    rnn_w = (wt["conv_w"], wt["conv_b"], wt["wa_d"], wt["wx_d"], wt["ba"], wt["bx"], wt["lam"])
    xc, a, hrec, pr = _rnn_fwd(rest, *rnn_w, seq)
    (do, dya, dyr, dmga, dmgr, doa, dga, dhrec, dgr, mrg, dy, delta, loss8, d_post) = _merge_loss(
        rest, pa, pr, o_att, hrec, wt["w_a"], wt["w_r"], wt["w_o"], x2, tgt2, wt["post_w"])
    d_wo, _ = _tn_mm("dw_out", mrg, do, D)
    d_wa, _ = _tn_mm("dw_branch_a", pa, dya, D)
    d_wr, _ = _tn_mm("dw_branch_r", pr, dyr, D)
    dxr, d_wad, d_wxd, vec = _rnn_bwd(dhrec, a, hrec, xc, rest, *rnn_w, seq)
    dq, dk, dv, dc = _attn_bwd(qa, ka, qkv, doa, lse, delta, seq)
    df, db_f = _forget_bwd(dc, f128, seq)
    pieces = [dq, dk, dv, dga, dxr, dgr, dmga, dmgr]
    gx, d_pre = _in_bwd(pieces, df, x2, dy, wt["w_all"], wt["pre_w"])
    names = ["q", "k", "v", "ga", "xr", "gr", "mga", "mgr"]
    dws, dbs = [], []
    for nm, piece in zip(names, pieces):
        dw_p, db_p = _tn_mm("dw_in_" + nm, piece, h, D, bf16)
        dws.append((dw_p, D))
        dbs.append(db_p)
    dw_f, _ = _tn_mm("dw_in_f", df, h, D, bf16)
    shard_rows = IN_TOTAL // N_CHIPS
    w_in_pieces = _pack_pieces(dws[:3] + [(dw_f, HEADS)] + dws[3:] + [(None, IN_TOTAL - IN_USED)], shard_rows,
                               _padded_rows(shard_rows))
    d_b_in = jnp.concatenate(dbs[:3] + [db_f[:, :HEADS]] + dbs[3:] + [jnp.zeros((1, IN_TOTAL - IN_USED), f32)], axis=1)
    return dict(loss=loss8[0, 0], grad_x=gx, pre_w=d_pre, w_in_pieces=w_in_pieces, b_in=d_b_in, conv_w=vec[4:8],
                conv_b=vec[3:4],
                wa_d=d_wad, ba=vec[0:1], wx_d=d_wxd, bx=vec[1:2], lam=vec[2:3], w_a=d_wa, w_r=d_wr, w_o=d_wo,
                post_w=d_post)


def _block_diag(w):
    g, bw, _ = w.shape
    eye = jnp.eye(g, dtype=w.dtype)
    return (w[:, :, None, :] * eye[:, None, :, None]).reshape(g * bw, g * bw)


def _gate_blocks(diag):
    half = diag.shape[1] // 2
    return jnp.stack([diag[:, :half, :half], diag[:, half:, half:]], axis=1).reshape(-1, half, half)


def _padded_rows(rows):
    return -(-rows // 32) * 32


def _pad_cols(a, n):
    return jnp.pad(a, ((0, 0), (0, n - a.shape[1])))


def _pad_rows(a, n):
    return jnp.pad(a, ((0, n - a.shape[0]), (0, 0)))


def kernel(x, pre_norm_w, w_in, b_in, conv_w, conv_b, rg_wa, rg_ba, rg_wx, rg_bx, rg_lambda, w_branch_a, w_branch_r, w_out, post_norm_w, loss_target, m_pre_norm_w, m_w_in, m_b_in, m_conv_w, m_conv_b, m_rg_wa, m_rg_ba, m_rg_wx, m_rg_bx, m_rg_lambda, m_w_branch_a, m_w_branch_r, m_w_out, m_post_norm_w, v_pre_norm_w, v_w_in, v_b_in, v_conv_w, v_conv_b, v_rg_wa, v_rg_ba, v_rg_wx, v_rg_bx, v_rg_lambda, v_w_branch_a, v_w_branch_r, v_w_out, v_post_norm_w):
    nb, seq, _ = x.shape
    chip = 2 * lax.axis_index("x") + lax.axis_index("y")
    n_groups = rg_wa.shape[1]

    w_in_t = jnp.transpose(w_in[0])
    shard_cols = w_in_t.shape[0]
    padded = _padded_rows(shard_cols)
    q_end, f_end = 3 * D, 3 * D + HEADS
    segments = [(0, 0, q_end), (q_end, f_end, IN_USED - f_end), (IN_USED - HEADS, q_end, HEADS)]
    g_a, g_r, g_o, g_cw, w_all = _gather_shards(
        [_pad_rows(w_in_t.astype(bf16), padded), w_branch_a[0].astype(bf16), w_branch_r[0].astype(bf16),
         w_out[0].astype(bf16)], conv_w[0], shard_cols, segments, IN_USED - HEADS + LANES)
    wt = dict(
        pre_w=pre_norm_w, post_w=post_norm_w,
        w_all=w_all, b_qkv=b_in[:, :q_end], b_f=_pad_cols(b_in[:, q_end:f_end], LANES), b_rest=b_in[:, f_end:IN_USED],
        w_a=g_a.reshape(D, D), w_r=g_r.reshape(D, D), w_o=g_o.reshape(D, D),
        conv_w=jnp.transpose(g_cw, (1, 0, 2)).reshape(4, D), conv_b=conv_b,
        wa_d=_block_diag(rg_wa[0]).astype(bf16), wx_d=_block_diag(rg_wx[0]).astype(bf16),
        ba=rg_ba, bx=rg_bx, lam=rg_lambda)

    part = _local_step(x.reshape(nb * seq, D), loss_target.reshape(nb * seq, D), seq, wt)
    loss = lax.psum(part["loss"], ("x", "y", "c"))
    grad_x = part["grad_x"].reshape(nb, seq, D)

    small = jnp.concatenate([
        part["pre_w"], _pad_cols(part["b_in"], 10 * D).reshape(10, D), part["conv_b"],
        _gate_blocks(part["wa_d"]).reshape(-1, D), part["ba"],
        _gate_blocks(part["wx_d"]).reshape(-1, D), part["bx"], part["lam"], part["post_w"],
        part["conv_w"]], axis=0)
    n_small = small.shape[0]
    n_rep = n_small - 4
    tot = _allsum_rows(_pad_rows(small, -(-n_small // 8) * 8))
    g_rep = tot[:n_rep]
    g_conv_w = lax.dynamic_slice_in_dim(tot[n_rep:n_small], chip * (D // N_CHIPS), D // N_CHIPS, axis=1)

    def unpack(p):
        o = [0]

        def take(k):
            o[0] += k
            return p[o[0] - k:o[0]]

        pre = take(1)
        b = take(10).reshape(1, 10 * D)[:, :IN_TOTAL]
        cb = take(1)
        wa = take(64).reshape(rg_wa.shape)
        ba = take(1)
        wx = take(64).reshape(rg_wx.shape)
        bx = take(1)
        lam = take(1)
        post = take(1)
        return dict(pre_norm_w=pre, b_in=b, conv_b=cb, rg_wa=wa, rg_ba=ba, rg_wx=wx, rg_bx=bx, rg_lambda=lam,
                    post_norm_w=post)

    grads = unpack(g_rep)
    replicated = dict(
        pre_norm_w=(pre_norm_w, m_pre_norm_w, v_pre_norm_w), b_in=(b_in, m_b_in, v_b_in),
        conv_b=(conv_b, m_conv_b, v_conv_b), rg_wa=(rg_wa, m_rg_wa, v_rg_wa), rg_ba=(rg_ba, m_rg_ba, v_rg_ba),
        rg_wx=(rg_wx, m_rg_wx, v_rg_wx), rg_bx=(rg_bx, m_rg_bx, v_rg_bx),
        rg_lambda=(rg_lambda, m_rg_lambda, v_rg_lambda), post_norm_w=(post_norm_w, m_post_norm_w, v_post_norm_w))
    deltas, new_m, new_v = {}, {}, {}
    for name, (w, m, v) in replicated.items():
        as2d = lambda a: a.reshape(-1, D) if a.ndim > 2 else a
        upd = _adamw("adamw_" + name, as2d(w), as2d(grads[name]), as2d(m), as2d(v))
        deltas[name], new_m[name], new_v[name] = [a.reshape(w.shape) for a in upd]

    p_aro = jnp.concatenate([part[k].reshape(N_DEV, D // N_DEV, D) for k in ("w_a", "w_r", "w_o")], axis=1)
    s_in, s_aro = _chip_exchange([_pair_reduce("pair_w_in", part["w_in_pieces"]),
                                  _pair_reduce("pair_w_aro", p_aro.astype(bf16))])
    f_in, f_aro = _sum_swap_halves([s_in, s_aro])
    g_w_in_t = f_in.reshape(padded, D)[:shard_cols]
    rows = D // N_DEV
    g_aro = [f_aro[:, i * rows:(i + 1) * rows, :].reshape(2 * rows, D) for i in range(3)]

    w_in_upd = _adamw("adamw_w_in", w_in_t, g_w_in_t, jnp.transpose(m_w_in[0]), jnp.transpose(v_w_in[0]))
    g_w_in, d_w_in, nm_w_in, nv_w_in = [jnp.transpose(a) for a in (g_w_in_t, *w_in_upd)]
    upd_a = _adamw("adamw_w_branch_a", w_branch_a[0], g_aro[0], m_w_branch_a[0], v_w_branch_a[0])
    upd_r = _adamw("adamw_w_branch_r", w_branch_r[0], g_aro[1], m_w_branch_r[0], v_w_branch_r[0])
    upd_o = _adamw("adamw_w_out", w_out[0], g_aro[2], m_w_out[0], v_w_out[0])
    d_aro, nm_aro, nv_aro = zip(upd_a, upd_r, upd_o)
    d_cw, nm_cw, nv_cw = _adamw("adamw_conv_w", conv_w[0], g_conv_w, m_conv_w[0], v_conv_w[0])

    def sharded(t_in, t_aro, t_cw):
        return dict(w_in=t_in[None], conv_w=t_cw[None], w_branch_a=t_aro[0][None], w_branch_r=t_aro[1][None],
                    w_out=t_aro[2][None])

    order = ["pre_norm_w", "w_in", "b_in", "conv_w", "conv_b", "rg_wa", "rg_ba", "rg_wx", "rg_bx", "rg_lambda",
             "w_branch_a", "w_branch_r", "w_out", "post_norm_w"]
    outs = [loss, grad_x]
    for rep, shd in ((grads, sharded(g_w_in, g_aro, g_conv_w)), (deltas, sharded(d_w_in, d_aro, d_cw)),
                     (new_m, sharded(nm_w_in, nm_aro, nm_cw)), (new_v, sharded(nv_w_in, nv_aro, nv_cw))):
        both = {**rep, **shd}
        outs.extend(both[k] for k in order)
    return tuple(outs)
```

```python
import jax
import jax.numpy as jnp
from jax import lax
from jax.experimental import pallas as pl
from jax.experimental.pallas import tpu as pltpu

f32 = jnp.float32
bf16 = jnp.bfloat16

D = 1024
HEADS = 16
LANES = 128
NORM_EPS = 1e-6
MASK_VALUE = -1e30
RG_C = 8.0
QK_SCALE = 0.125
TQ = 256
ATT_GROUP = 8
ATT_GROUP_FWD = 16
TL = 512
TM = 512
PREV_ROWS = 16
IN_USED = 8 * D + HEADS
IN_TOTAL = 9 * D + HEADS
N_CHIPS = 4
N_DEV = 8
ADAM_LR, ADAM_B1, ADAM_B2, ADAM_EPS, ADAM_WD, ADAM_STEP = 0.001, 0.9, 0.999, 1e-08, 0.01, 10
VMEM_LIMIT = 56 * 1024 * 1024
MESH = pl.DeviceIdType.MESH


def _dot(a, b):
    return jnp.dot(a, b, preferred_element_type=f32)


def _dot_nt(a, b):
    return lax.dot_general(a, b, (((1,), (1,)), ((), ())), preferred_element_type=f32)


def _dot_tn(a, b):
    return lax.dot_general(a, b, (((0,), (0,)), ((), ())), preferred_element_type=f32)


def _sig(x):
    return 0.5 * jnp.tanh(0.5 * x) + 0.5


def _softplus(x):
    return jnp.maximum(x, 0.0) + jnp.log(1.0 + jnp.exp(-jnp.abs(x)))


def _params(sem, vmem=None):
    return pltpu.CompilerParams(dimension_semantics=sem, vmem_limit_bytes=vmem)


def _tile(tm, width, cb=0):
    return pl.BlockSpec((tm, width), lambda i, cb=cb: (i, cb))


def _whole(shape):
    nd = len(shape)
    return pl.BlockSpec(shape, lambda *_: (0,) * nd)


def _norm_qkv(x, w_pre, w_all, b_qkv, f_row0, b_f, tm=1024):
    t = x.shape[0]
    tm = min(tm, t)
    n = b_qkv.shape[1]

    def body(x_ref, wp_ref, w_ref, b_ref, wf_ref, bf_ref, h_ref, o_ref, f_ref):
        @pl.when(pl.program_id(1) == 0)
        def _():
            xv = x_ref[...]
            r = lax.rsqrt(jnp.mean(xv * xv, axis=-1, keepdims=True) + NORM_EPS)
            h = (xv * r * wp_ref[...]).astype(bf16)
            h_ref[...] = h
            f_ref[...] = _dot_nt(h, wf_ref[...]) + bf_ref[...]

        o_ref[...] = (_dot_nt(h_ref[...], w_ref[...]) + b_ref[...]).astype(bf16)

    return pl.pallas_call(
        body, name="norm_qkv", grid=(t // tm, n // D),
        in_specs=[pl.BlockSpec((tm, D), lambda i, j: (i, 0)), _whole((1, D)), pl.BlockSpec((D, D), lambda i, j: (j, 0)),
                  pl.BlockSpec((1, D), lambda i, j: (0, j)),
                  pl.BlockSpec((LANES, D), lambda i, j: (f_row0 // LANES, 0)), _whole((1, LANES))],
        out_specs=[pl.BlockSpec((tm, D), lambda i, j: (i, 0)), pl.BlockSpec((tm, D), lambda i, j: (i, j)),
                   pl.BlockSpec((tm, LANES), lambda i, j: (i, 0))],
        out_shape=[jax.ShapeDtypeStruct((t, D), bf16), jax.ShapeDtypeStruct((t, n), bf16),
                   jax.ShapeDtypeStruct((t, LANES), f32)],
        compiler_params=_params(("parallel", "arbitrary"), VMEM_LIMIT),
    )(x, w_pre, w_all, b_qkv, w_all, b_f)


def _mm(name, a, w, w_rows, bias, out_dtype, tm, tn):
    t, k = a.shape
    tm = min(tm, t)
    row0, n = w_rows
    assert row0 % tn == 0

    def body(a_ref, w_ref, b_ref, o_ref):
        o_ref[...] = (_dot_nt(a_ref[...], w_ref[...]) + b_ref[...]).astype(out_dtype)

    return pl.pallas_call(
        body, name=name, grid=(t // tm, n // tn),
        in_specs=[pl.BlockSpec((tm, k), lambda i, j: (i, 0)), pl.BlockSpec((tn, k), lambda i, j: (row0 // tn + j, 0)),
                  pl.BlockSpec((1, tn), lambda i, j: (0, j))],
        out_specs=pl.BlockSpec((tm, tn), lambda i, j: (i, j)), out_shape=jax.ShapeDtypeStruct((t, n), out_dtype),
        compiler_params=_params(("parallel", "parallel"), VMEM_LIMIT),
    )(a, w, bias)


def _forget_prep(f128, seq):
    t = f128.shape[0]
    nb = seq // LANES

    def body(f_ref, c_ref):
        r = lax.broadcasted_iota(jnp.int32, (LANES, LANES), 0)
        cidx = lax.broadcasted_iota(jnp.int32, (LANES, LANES), 1)
        tri = (r >= cidx).astype(f32)
        carry = jnp.zeros((1, LANES), f32)
        for blk in range(nb):
            fv = f_ref[pl.ds(blk * LANES, LANES), :]
            lf = -_softplus(-fv)
            c_ref[pl.ds(blk * LANES, LANES), :] = (
                jnp.dot(tri, lf, preferred_element_type=f32, precision=lax.Precision.HIGHEST) + carry)
            carry = carry + jnp.sum(lf, axis=0, keepdims=True)

    return pl.pallas_call(
        body, name="forget_prep", grid=(t // seq,),
        in_specs=[pl.BlockSpec((seq, LANES), lambda b: (b, 0))],
        out_specs=pl.BlockSpec((seq, LANES), lambda b: (b, 0)),
        out_shape=jax.ShapeDtypeStruct((t, LANES), f32),
        compiler_params=_params(("parallel",)),
    )(f128)


def _split3(cv):
    hi = cv.astype(bf16)
    r1 = cv - hi.astype(f32)
    mid = r1.astype(bf16)
    lo = (r1 - mid.astype(f32)).astype(bf16)
    return hi, mid, lo


def _attn_prep(qkv, c):
    t = qkv.shape[0]

    def body(q_ref, k_ref, c_ref, qa_ref, ka_ref):
        lane = lax.broadcasted_iota(jnp.int32, (1, LANES), 1)
        cv = c_ref[...]
        q_tail = jnp.where((lane >= 64) & (lane < 67), jnp.ones((), bf16), jnp.zeros((), bf16))
        for head in range(HEADS):
            pair = pl.ds((head // 2) * LANES, LANES)
            ch = jnp.sum(jnp.where(lane == head, cv, 0.0), axis=1, keepdims=True)
            hi, mid, lo = _split3(-ch)
            q2, k2 = q_ref[:, pair], k_ref[:, pair]
            if head % 2 == 1:
                q2, k2 = pltpu.roll(q2, 64, 1), pltpu.roll(k2, 64, 1)
            qa = jnp.where(lane < 64, q2 * jnp.asarray(QK_SCALE, bf16), q_tail)
            ka = jnp.where(lane < 64, k2, jnp.where(lane == 64, hi, jnp.where(lane == 65, mid, jnp.where(
                lane == 66, lo, jnp.zeros((), bf16)))))
            qa_ref[:, pl.ds(head * LANES, LANES)] = qa
            ka_ref[:, pl.ds(head * LANES, LANES)] = ka

    tm = min(TM, t)
    out = pl.BlockSpec((tm, 2 * D), lambda i: (i, 0))
    return pl.pallas_call(
        body, name="attn_prep", grid=(t // tm,),
        in_specs=[_tile(tm, D, 0), _tile(tm, D, 1), _tile(tm, LANES)],
        out_specs=[out, out],
        out_shape=[jax.ShapeDtypeStruct((t, 2 * D), bf16)] * 2,
        compiler_params=_params(("parallel",)),
    )(qkv, qkv, c)


def _attn_fwd(qa, ka, qkv, rest, seq):
    t = qkv.shape[0]
    nb, nq = t // seq, seq // TQ

    hg = ATT_GROUP_FWD
    ng = HEADS // hg

    def body(q_ref, k_ref, v_ref, ga_ref, o_ref, pa_ref, lse_ref, acc_scr):
        qi, gi = pl.program_id(1), pl.program_id(2)
        krow = lax.broadcasted_iota(jnp.int32, (TQ, TQ), 0)
        qcol = lax.broadcasted_iota(jnp.int32, (TQ, TQ), 1)
        acc_scr[...] = jnp.zeros_like(acc_scr)

        def kv_step(kt, carry, masked):
            ks = pl.multiple_of(kt * TQ, TQ)
            sts = [_dot_nt(k_ref[pl.ds(ks, TQ), pl.ds(g * LANES, LANES)], q_ref[:, pl.ds(g * LANES, LANES)])
                   for g in range(hg)]
            if masked:
                sts = [jnp.where(krow <= qcol, st, MASK_VALUE) for st in sts]
            m_new = [jnp.maximum(carry[g][0], jnp.max(sts[g], axis=0, keepdims=True)) for g in range(hg)]
            ps = [jnp.exp(sts[g] - m_new[g]) for g in range(hg)]
            alphas = [jnp.exp(carry[g][0] - m_new[g]) for g in range(hg)]
            phi = [ps[g].astype(bf16) for g in range(hg)]
            plo = [(ps[g] - phi[g].astype(f32)).astype(bf16) for g in range(hg)]
            vs = [v_ref[pl.ds(ks, TQ), pl.ds(j * LANES, LANES)] for j in range(hg // 2)]
            pvs = [_dot_tn(vs[g // 2], phi[g]) + _dot_tn(vs[g // 2], plo[g]) for g in range(hg)]
            olds = [acc_scr[g] for g in range(hg)]
            for g in range(hg):
                acc_scr[g] = alphas[g] * olds[g] + pvs[g]
            return tuple((m_new[g], alphas[g] * carry[g][1] + jnp.sum(ps[g], axis=0, keepdims=True))
                         for g in range(hg))

        init = tuple((jnp.full((1, TQ), MASK_VALUE, f32), jnp.zeros((1, TQ), f32)) for _ in range(hg))
        carry = lax.fori_loop(0, qi, lambda kt, cr: kv_step(kt, cr, False), init)
        stats = kv_step(qi, carry, True)
        drow = lax.broadcasted_iota(jnp.int32, (LANES, TQ), 0)
        for g in range(hg):
            m, l = stats[g]
            lse_ref[0, pl.ds(hg * gi + g, 1), :] = m + jnp.log(l)
        for j in range(hg // 2):
            o2 = jnp.where(drow < 64, acc_scr[2 * j] / stats[2 * j][1], acc_scr[2 * j + 1] / stats[2 * j + 1][1]).T
            o_ref[:, pl.ds(j * LANES, LANES)] = o2
            ga = ga_ref[:, pl.ds(j * LANES, LANES)].astype(f32)
            pa_ref[:, pl.ds(j * LANES, LANES)] = (o2 * (ga * _sig(ga))).astype(bf16)

    vw = hg * 64
    tile = pl.BlockSpec((TQ, vw), lambda b, qi, gi: (b * nq + qi, gi))
    return pl.pallas_call(
        body, name="attn_fwd", grid=(nb, nq, ng),
        in_specs=[pl.BlockSpec((TQ, hg * LANES), lambda b, qi, gi: (b * nq + qi, gi)),
                  pl.BlockSpec((seq, hg * LANES), lambda b, qi, gi: (b, gi)),
                  pl.BlockSpec((seq, vw), lambda b, qi, gi: (b, 2 * ng + gi)), tile],
        out_specs=[tile, tile, pl.BlockSpec((1, HEADS, TQ), lambda b, qi, gi: (b * nq + qi, 0, 0))],
        out_shape=[jax.ShapeDtypeStruct((t, D), f32), jax.ShapeDtypeStruct((t, D), bf16),
                   jax.ShapeDtypeStruct((t // TQ, HEADS, TQ), f32)],
        scratch_shapes=[pltpu.VMEM((hg, LANES, TQ), f32)],
        compiler_params=_params(("parallel", "parallel", "arbitrary"), VMEM_LIMIT),
    )(qa, ka, qkv, rest)


def _shifted_rows(x, top8, prev8, shift, row, row8):
    body = pltpu.roll(x, shift, 0)
    head = jnp.where(row8 < shift, pltpu.roll(prev8, shift, 0), pltpu.roll(top8, shift, 0))
    return body, head


def _rnn_gates(xc, wa_ref, wx_ref, ba_ref, bx_ref, lam_ref):
    xcb = xc.astype(bf16)
    r = _sig(_dot(xcb, wa_ref[...]) + ba_ref[...])
    i = _sig(_dot(xcb, wx_ref[...]) + bx_ref[...])
    sp = _softplus(-lam_ref[...])
    log_a = (-RG_C) * r * sp
    th = jnp.tanh(log_a)
    w1 = (-2.0) * th / (1.0 - th)
    sq = jnp.sqrt(jnp.maximum(w1, 0.0))
    return r, i, sp, log_a, w1, sq


def _conv_tile(x_ref, xprev_ref, has_prev, cw_ref, cb_ref, xc_ref):
    row = lax.broadcasted_iota(jnp.int32, (TL, D), 0)
    row8 = lax.broadcasted_iota(jnp.int32, (8, D), 0)
    x = x_ref[...].astype(f32)
    top8 = x[:8]
    prev8 = jnp.where(has_prev, xprev_ref[...].astype(f32)[PREV_ROWS - 8:], 0.0)
    xc = cb_ref[...] + cw_ref[pl.ds(3, 1), :] * x
    xc8 = cb_ref[...] + cw_ref[pl.ds(3, 1), :] * top8
    for sh in range(1, 4):
        w = cw_ref[pl.ds(3 - sh, 1), :]
        xs, xs8 = _shifted_rows(x, top8, prev8, sh, row, row8)
        xc = xc + w * xs
        xc8 = xc8 + w * xs8
    xc_ref[...] = xc
    xc_ref[pl.ds(0, 8), :] = xc8


def _rnn_fwd(rest, conv_w, conv_b, wa_d, wx_d, ba, bx, lam, seq):
    t = rest.shape[0]
    nb, nt = t // seq, seq // TL

    def body(x_ref, xprev_ref, gr_ref, cw_ref, cb_ref, wa_ref, wx_ref, ba_ref, bx_ref, lam_ref,
             xc_ref, a_ref, h_ref, pr_ref, xc_scr, u_scr, h_scr, carry):
        tt = pl.program_id(1)
        _conv_tile(x_ref, xprev_ref, tt > 0, cw_ref, cb_ref, xc_scr)
        xc = xc_scr[...]
        xc_ref[...] = xc.astype(bf16)
        r, i, sp, log_a, w1, sq = _rnn_gates(xc, wa_ref, wx_ref, ba_ref, bx_ref, lam_ref)
        a_ref[...] = jnp.exp(log_a)
        u_scr[...] = sq * (i * xc)

        @pl.when(tt == 0)
        def _():
            carry[...] = jnp.zeros_like(carry)

        def step(s, h):
            h = a_ref[pl.ds(s, 1), :] * h + u_scr[pl.ds(s, 1), :]
            h_scr[pl.ds(s, 1), :] = h
            return h

        carry[...] = lax.fori_loop(0, TL, step, carry[...], unroll=8)
        gr = gr_ref[...].astype(f32)
        h = h_scr[...]
        h_ref[...] = h.astype(bf16)
        pr_ref[...] = (h * (gr * _sig(gr))).astype(bf16)

    tile = lambda cb: pl.BlockSpec((TL, D), lambda b, tt, cb=cb: (b * nt + tt, cb))
    prev = lambda cb: pl.BlockSpec(
        (PREV_ROWS, D), lambda b, tt, cb=cb: (jnp.maximum((b * nt + tt) * (TL // PREV_ROWS) - 1, 0), cb))
    vec = _whole((1, D))
    return pl.pallas_call(
        body, name="rnn_fwd", grid=(nb, nt),
        in_specs=[tile(1), prev(1), tile(2), _whole((4, D)), vec, _whole((D, D)), _whole((D, D)), vec, vec, vec],
        out_specs=[tile(0)] * 4,
        out_shape=[jax.ShapeDtypeStruct((t, D), dt) for dt in (bf16, f32, bf16, bf16)],
        scratch_shapes=[pltpu.VMEM((TL, D), f32)] * 3 + [pltpu.VMEM((1, D), f32)],
        compiler_params=_params(("parallel", "arbitrary"), VMEM_LIMIT),
    )(rest, rest, rest, conv_w, conv_b, wa_d, wx_d, ba, bx, lam)


def _merge_loss(rest, pa, pr, o_att, hrec, w_a, w_r, w_out, x, tgt, w_post):
    t = x.shape[0]

    def branch(dy, w_ref, g_ref, act):
        dp = _dot_nt(dy, w_ref[...])
        g = g_ref[...].astype(f32)
        sg = _sig(g)
        return (dp * (g * sg)).astype(bf16), (dp * act * (sg * (1.0 + g * (1.0 - sg)))).astype(bf16)

    def body(mga_ref, mgr_ref, pa_ref, pr_ref, ga_ref, gr_ref, oa_ref, h_ref, x_ref, t_ref, wa_ref, wr_ref, wo_ref,
             w_ref, do_ref, dya_ref, dyr_ref, dmga_ref, dmgr_ref, doa_ref, dga_ref, dh_ref, dgr_ref, mrg_ref, dy_ref,
             delta_ref, loss_ref, dwp_ref):
        @pl.when(pl.program_id(0) == 0)
        def _():
            loss_ref[...] = jnp.zeros_like(loss_ref)
            dwp_ref[...] = jnp.zeros_like(dwp_ref)

        sa, sr = _sig(mga_ref[...].astype(f32)), _sig(mgr_ref[...].astype(f32))
        ya, yr = _dot(pa_ref[...], wa_ref[...]), _dot(pr_ref[...], wr_ref[...])
        mrg = (sa * ya + sr * yr).astype(bf16)
        mrg_ref[...] = mrg
        ov = _dot(mrg, wo_ref[...])
        w = w_ref[...]
        r2 = lax.rsqrt(jnp.mean(ov * ov, axis=-1, keepdims=True) + NORM_EPS)
        oh = ov * r2
        e = x_ref[...] + oh * w - t_ref[...]
        loss_ref[...] += 0.5 * jnp.sum(jnp.mean(e * e, axis=-1, keepdims=True))
        dy = e * (1.0 / D)
        dy_ref[...] = dy
        dwp_ref[...] += jnp.sum(dy * oh, axis=0, keepdims=True)
        doh = dy * w
        do = (r2 * (doh - oh * jnp.mean(doh * oh, axis=-1, keepdims=True))).astype(bf16)
        do_ref[...] = do

        dm = _dot_nt(do, wo_ref[...])
        dya, dyr = (dm * sa).astype(bf16), (dm * sr).astype(bf16)
        dya_ref[...] = dya
        dyr_ref[...] = dyr
        dmga_ref[...] = (dm * ya * sa * (1.0 - sa)).astype(bf16)
        dmgr_ref[...] = (dm * yr * sr * (1.0 - sr)).astype(bf16)
        o_att = oa_ref[...]
        doa, dga_ref[...] = branch(dya, wa_ref, ga_ref, o_att)
        doa_ref[...] = doa
        dh_ref[...], dgr_ref[...] = branch(dyr, wr_ref, gr_ref, h_ref[...].astype(f32))
        ch = lax.broadcasted_iota(jnp.int32, (D, LANES), 0)
        hd = lax.broadcasted_iota(jnp.int32, (D, LANES), 1)
        pick = (ch // 64 == hd).astype(bf16)
        per_head = sum(_dot(piece, pick) for piece in _split3(doa.astype(f32) * o_att))
        delta_ref[0] = per_head.T[:HEADS, :]

    once = pl.BlockSpec((D, D), lambda i: (0, 0), pipeline_mode=pl.Buffered(1))
    rows = _tile(TQ, D)
    return pl.pallas_call(
        body, name="merge_loss", grid=(t // TQ,),
        in_specs=[_tile(TQ, D, 3), _tile(TQ, D, 4), rows, rows, _tile(TQ, D, 0), _tile(TQ, D, 2), rows, rows, rows, rows,
                  once, once, once, _whole((1, D))],
        out_specs=[rows] * 11 + [pl.BlockSpec((1, HEADS, TQ), lambda i: (i, 0, 0)), _whole((8, LANES)), _whole((1, D))],
        out_shape=[jax.ShapeDtypeStruct((t, D), bf16)] * 10 + [jax.ShapeDtypeStruct((t, D), f32),
                   jax.ShapeDtypeStruct((t // TQ, HEADS, TQ), f32), jax.ShapeDtypeStruct((8, LANES), f32),
                   jax.ShapeDtypeStruct((1, D), f32)],
        compiler_params=_params(("arbitrary",), VMEM_LIMIT),
    )(rest, rest, pa, pr, rest, rest, o_att, hrec, x, tgt, w_a, w_r, w_out, w_post)


def _rnn_bwd(dh, a, h, xc, rest, conv_w, conv_b, wa_d, wx_d, ba, bx, lam, seq):
    t = dh.shape[0]
    nb, nt = t // seq, seq // TL
    diag = (D // LANES, LANES, LANES)

    def body(dh_ref, a_ref, h_ref, hprev_ref, xc_ref, x_ref, cw_ref, cb_ref, wa_ref, wx_ref,
             ba_ref, bx_ref, lam_ref, dxr_ref, dwa_ref, dwx_ref, vec_ref, g_scr, dxc_scr, dxr_scr, qcarry, dxc_next):
        b, tt = pl.program_id(0), pl.program_id(1)
        rt = nt - 1 - tt

        @pl.when((b == 0) & (tt == 0))
        def _():
            dwa_ref[...] = jnp.zeros_like(dwa_ref)
            dwx_ref[...] = jnp.zeros_like(dwx_ref)
            vec_ref[...] = jnp.zeros_like(vec_ref)

        @pl.when(tt == 0)
        def _():
            qcarry[...] = jnp.zeros_like(qcarry)
            dxc_next[...] = jnp.zeros_like(dxc_next)

        g_scr[...] = dh_ref[...].astype(f32)

        def step(k, q):
            s = TL - 1 - k
            g = g_scr[pl.ds(s, 1), :] + q
            g_scr[pl.ds(s, 1), :] = g
            return a_ref[pl.ds(s, 1), :] * g

        qcarry[...] = lax.fori_loop(0, TL, step, qcarry[...], unroll=8)

        row = lax.broadcasted_iota(jnp.int32, (TL, D), 0)
        row8 = lax.broadcasted_iota(jnp.int32, (8, D), 0)
        g = g_scr[...]
        av = a_ref[...]
        xc = xc_ref[...].astype(f32)
        hlast = jnp.where(rt > 0, hprev_ref[...].astype(f32)[PREV_ROWS - 1:], 0.0)
        hp = jnp.where(row == 0, hlast, pltpu.roll(h_ref[...].astype(f32), 1, 0))
        r, i, sp, log_a, w1, sq = _rnn_gates(xc, wa_ref, wx_ref, ba_ref, bx_ref, lam_ref)
        dix = g * sq
        di = dix * xc
        dxc = dix * i
        dsq = g * (i * xc)
        dlog_a = g * hp * av - dsq * jnp.where(sq > 0.0, (1.0 - w1) / sq, 0.0)
        dpr = (dlog_a * ((-RG_C) * sp)) * r * (1.0 - r)
        dpi = di * i * (1.0 - i)
        dprb, dpib, xcb = dpr.astype(bf16), dpi.astype(bf16), xc.astype(bf16)
        dxc = dxc + _dot_nt(dprb, wa_ref[...]) + _dot_nt(dpib, wx_ref[...])
        for j in range(D // LANES):
            cols = slice(j * LANES, (j + 1) * LANES)
            dwa_ref[j] += _dot_tn(xcb[:, cols], dprb[:, cols])
            dwx_ref[j] += _dot_tn(xcb[:, cols], dpib[:, cols])
        vec_ref[pl.ds(0, 1), :] += jnp.sum(dpr, axis=0, keepdims=True)
        vec_ref[pl.ds(1, 1), :] += jnp.sum(dpi, axis=0, keepdims=True)
        dsp = jnp.sum(dlog_a * ((-RG_C) * r), axis=0, keepdims=True)
        vec_ref[pl.ds(2, 1), :] += dsp * (-_sig(-lam_ref[...]))
        vec_ref[pl.ds(3, 1), :] += jnp.sum(dxc, axis=0, keepdims=True)

        dxc_scr[...] = dxc
        bot8 = dxc_scr[pl.ds(TL - 8, 8), :]
        nxt8 = dxc_next[...]
        x = x_ref[...].astype(f32)
        x_bot8 = x[TL - 8:]
        dxr = cw_ref[pl.ds(3, 1), :] * dxc
        dxr8 = cw_ref[pl.ds(3, 1), :] * bot8
        vec_ref[pl.ds(7, 1), :] += jnp.sum(dxc * x, axis=0, keepdims=True)
        for sh in range(1, 4):
            w = cw_ref[pl.ds(3 - sh, 1), :]
            up = pltpu.roll(dxc, TL - sh, 0)
            from_next = pltpu.roll(nxt8, 8 - sh, 0)
            dxr = dxr + w * up
            dxr8 = dxr8 + w * jnp.where(row8 < 8 - sh, pltpu.roll(bot8, 8 - sh, 0), from_next)
            inside = jnp.sum(jnp.where(row < TL - sh, up, 0.0) * x, axis=0, keepdims=True)
            across = jnp.sum(jnp.where(row8 >= 8 - sh, from_next, 0.0) * x_bot8, axis=0, keepdims=True)
            vec_ref[pl.ds(7 - sh, 1), :] += inside + across
        dxr_scr[...] = dxr
        dxr_scr[pl.ds(TL - 8, 8), :] = dxr8
        dxr_ref[...] = dxr_scr[...].astype(bf16)
        dxc_next[...] = dxc_scr[pl.ds(0, 8), :]

    tile = lambda cb: pl.BlockSpec((TL, D), lambda b, tt, cb=cb: (b * nt + nt - 1 - tt, cb))
    prev = lambda cb: pl.BlockSpec(
        (PREV_ROWS, D), lambda b, tt, cb=cb: (jnp.maximum((b * nt + nt - 1 - tt) * (TL // PREV_ROWS) - 1, 0), cb))
    vec = _whole((1, D))
    return pl.pallas_call(
        body, name="rnn_bwd", grid=(nb, nt),
        in_specs=[tile(0), tile(0), tile(0), prev(0), tile(0), tile(1),
                  _whole((4, D)), vec, _whole((D, D)), _whole((D, D)), vec, vec, vec],
        out_specs=[tile(0), _whole(diag), _whole(diag), _whole((8, D))],
        out_shape=[jax.ShapeDtypeStruct((t, D), bf16), jax.ShapeDtypeStruct(diag, f32),
                   jax.ShapeDtypeStruct(diag, f32), jax.ShapeDtypeStruct((8, D), f32)],
        scratch_shapes=[pltpu.VMEM((TL, D), f32), pltpu.VMEM((TL, D), f32), pltpu.VMEM((TL, D), f32),
                        pltpu.VMEM((1, D), f32), pltpu.VMEM((8, D), f32)],
        compiler_params=_params(("arbitrary", "arbitrary"), VMEM_LIMIT),
    )(dh, a, h, h, xc, rest, conv_w, conv_b, wa_d, wx_d, ba, bx, lam)


def _attn_bwd(qa, ka, qkv, doa, lse, delta, seq):
    t = qkv.shape[0]
    nb, nq = t // seq, seq // TQ
    hg = ATT_GROUP
    ng, npair = HEADS // hg, hg // 2

    def body(qa_ref, ka_ref, q_ref, k_ref, v_ref, do_ref, lse_ref, dl_ref, dq_ref, dk_ref, dv_ref, dc_ref,
             dqt_scr, dk_scr, dv_scr, ds_scr, kht_scr):
        gi, kt = pl.program_id(1), pl.program_id(2)
        lane = lax.broadcasted_iota(jnp.int32, (1, LANES), 1)
        krow = lax.broadcasted_iota(jnp.int32, (TQ, TQ), 0)
        qcol = lax.broadcasted_iota(jnp.int32, (TQ, TQ), 1)
        lmask = [(lane // 64) == hh for hh in range(2)]
        scale = jnp.asarray(QK_SCALE, bf16)

        @pl.when(kt == 0)
        def _():
            dqt_scr[...] = jnp.zeros_like(dqt_scr)

        dk_scr[...] = jnp.zeros_like(dk_scr)
        dv_scr[...] = jnp.zeros_like(dv_scr)
        ds_scr[...] = jnp.zeros_like(ds_scr)
        for g in range(hg):
            k2 = k_ref[:, pl.ds((g // 2) * LANES, LANES)]
            kht_scr[g] = jnp.where(lmask[g % 2], k2, jnp.zeros_like(k2)).T

        def q_step(qt, masked):
            qs = pl.multiple_of(qt * TQ, TQ)
            heads = range(hg)
            do2 = [do_ref[pl.ds(qs, TQ), pl.ds(j * LANES, LANES)] for j in range(npair)]
            q2 = [q_ref[pl.ds(qs, TQ), pl.ds(j * LANES, LANES)] for j in range(npair)]
            doh = [jnp.where(lmask[g % 2], do2[g // 2], jnp.zeros_like(do2[0])) for g in heads]
            qh = [jnp.where(lmask[g % 2], q2[g // 2], jnp.zeros_like(q2[0])) * scale for g in heads]
            st = [_dot_nt(ka_ref[:, pl.ds(g * LANES, LANES)], qa_ref[pl.ds(qs, TQ), pl.ds(g * LANES, LANES)])
                  for g in heads]
            if masked:
                st = [jnp.where(krow <= qcol, s, MASK_VALUE) for s in st]
            dp = [_dot_nt(v_ref[:, pl.ds((g // 2) * LANES, LANES)], doh[g]) for g in heads]
            p = [jnp.exp(st[g] - lse_ref[qt, pl.ds(hg * gi + g, 1), :]) for g in heads]
            ds = [p[g] * (dp[g] - dl_ref[qt, pl.ds(hg * gi + g, 1), :]) for g in heads]
            pb = [x.astype(bf16) for x in p]
            dsb = [x.astype(bf16) for x in ds]
            for j in range(npair):
                a, b = 2 * j, 2 * j + 1
                dv_scr[j] += _dot(pb[a], doh[a]) + _dot(pb[b], doh[b])
                dk_scr[j] += _dot(dsb[a], qh[a]) + _dot(dsb[b], qh[b])
                dqt_scr[qt, j] += (_dot(kht_scr[a], dsb[a]) + _dot(kht_scr[b], dsb[b])) * QK_SCALE
            for g in heads:
                ds_scr[g] += ds[g][:, :LANES] + ds[g][:, LANES:]

        q_step(kt, True)

        def loop_body(qt, carry):
            q_step(qt, False)
            return carry

        lax.fori_loop(kt + 1, nq, loop_body, 0)

        dc = jnp.zeros((TQ, LANES), f32)
        for g in range(hg):
            dc = jnp.where(lane == g, -jnp.sum(ds_scr[g], axis=1, keepdims=True), dc)
        dc_ref[...] = dc
        for j in range(npair):
            dk_ref[:, pl.ds(j * LANES, LANES)] = dk_scr[j].astype(bf16)
            dv_ref[:, pl.ds(j * LANES, LANES)] = dv_scr[j].astype(bf16)

        @pl.when(kt == nq - 1)
        def _():
            for qt in range(nq):
                for j in range(npair):
                    dq_ref[pl.ds(qt * TQ, TQ), pl.ds(j * LANES, LANES)] = dqt_scr[qt, j].T.astype(bf16)

    vw = hg * 64
    seqspec = pl.BlockSpec((seq, vw), lambda b, gi, kt: (b, gi))
    kspec = lambda off: pl.BlockSpec((TQ, vw), lambda b, gi, kt: (b * nq + kt, off + gi))
    rowspec = pl.BlockSpec((nq, HEADS, TQ), lambda b, gi, kt: (b, 0, 0))
    return pl.pallas_call(
        body, name="attn_bwd", grid=(nb, ng, nq),
        in_specs=[pl.BlockSpec((seq, hg * LANES), lambda b, gi, kt: (b, gi)),
                  pl.BlockSpec((TQ, hg * LANES), lambda b, gi, kt: (b * nq + kt, gi)),
                  seqspec, kspec(ng), kspec(2 * ng), seqspec, rowspec, rowspec],
        out_specs=[seqspec, kspec(0), kspec(0), pl.BlockSpec((TQ, LANES), lambda b, gi, kt: (b * nq + kt, gi))],
        out_shape=[jax.ShapeDtypeStruct((t, D), bf16)] * 3 + [jax.ShapeDtypeStruct((t, ng * LANES), f32)],
        scratch_shapes=[pltpu.VMEM((nq, npair, LANES, TQ), f32), pltpu.VMEM((npair, TQ, LANES), f32),
                        pltpu.VMEM((npair, TQ, LANES), f32), pltpu.VMEM((hg, TQ, LANES), f32),
                        pltpu.VMEM((hg, LANES, TQ), bf16)],
        compiler_params=_params(("parallel", "parallel", "arbitrary"), VMEM_LIMIT),
    )(qa, ka, qkv, qkv, qkv, doa, lse, delta)


def _forget_bwd(dc, f128, seq):
    t = f128.shape[0]
    nb = seq // LANES
    groups = dc.shape[1] // LANES

    def body(dc_ref, f_ref, df_ref, dbf_ref):
        @pl.when(pl.program_id(0) == 0)
        def _():
            dbf_ref[...] = jnp.zeros_like(dbf_ref)

        r = lax.broadcasted_iota(jnp.int32, (LANES, LANES), 0)
        cidx = lax.broadcasted_iota(jnp.int32, (LANES, LANES), 1)
        tri = (r <= cidx).astype(f32)
        carry = jnp.zeros((1, LANES), f32)
        total = jnp.zeros((1, LANES), f32)
        for blk in reversed(range(nb)):
            dcb = dc_ref[pl.ds(blk * LANES, LANES), pl.ds(0, LANES)]
            for gi in range(1, groups):
                dcb = dcb + pltpu.roll(dc_ref[pl.ds(blk * LANES, LANES), pl.ds(gi * LANES, LANES)], gi * ATT_GROUP, 1)
            dlf = jnp.dot(tri, dcb, preferred_element_type=f32, precision=lax.Precision.HIGHEST) + carry
            df = dlf * _sig(-f_ref[pl.ds(blk * LANES, LANES), :])
            df_ref[pl.ds(blk * LANES, LANES), :] = df.astype(bf16)
            total = total + jnp.sum(df, axis=0, keepdims=True)
            carry = carry + jnp.sum(dcb, axis=0, keepdims=True)
        dbf_ref[...] += total

    return pl.pallas_call(
        body, name="forget_bwd", grid=(t // seq,),
        in_specs=[pl.BlockSpec((seq, groups * LANES), lambda b: (b, 0)), pl.BlockSpec((seq, LANES), lambda b: (b, 0))],
        out_specs=[pl.BlockSpec((seq, LANES), lambda b: (b, 0)), _whole((1, LANES))],
        out_shape=[jax.ShapeDtypeStruct((t, LANES), bf16), jax.ShapeDtypeStruct((1, LANES), f32)],
        compiler_params=_params(("arbitrary",)),
    )(dc, f128)


def _in_bwd(dz, df, x, dy, w_all, w_pre):
    t = x.shape[0]
    n_dz = len(dz)

    def body(*refs):
        dz_refs = refs[:n_dz]
        df_ref, x_ref, dy_ref, w_ref, wp_ref, gx_ref, dwp_ref = refs[n_dz:]

        @pl.when(pl.program_id(0) == 0)
        def _():
            dwp_ref[...] = jnp.zeros_like(dwp_ref)

        dh = _dot(df_ref[...], w_ref[pl.ds(n_dz * D, LANES), :])
        for p in range(n_dz):
            dh = dh + _dot(dz_refs[p][...], w_ref[pl.ds(p * D, D), :])
        xv = x_ref[...]
        r1 = lax.rsqrt(jnp.mean(xv * xv, axis=-1, keepdims=True) + NORM_EPS)
        xh = xv * r1
        dwp_ref[...] += jnp.sum(dh * xh, axis=0, keepdims=True)
        dxh = dh * wp_ref[...]
        gx_ref[...] = dy_ref[...] + r1 * (dxh - xh * jnp.mean(dxh * xh, axis=-1, keepdims=True))

    once = lambda shape: pl.BlockSpec(shape, lambda i: (0, 0), pipeline_mode=pl.Buffered(1))
    return pl.pallas_call(
        body, name="in_bwd", grid=(t // TM,),
        in_specs=[_tile(TM, D)] * n_dz + [_tile(TM, LANES), _tile(TM, D), _tile(TM, D), once(w_all.shape),
                  _whole((1, D))],
        out_specs=[_tile(TM, D), _whole((1, D))],
        out_shape=[jax.ShapeDtypeStruct((t, D), f32), jax.ShapeDtypeStruct((1, D), f32)],
        compiler_params=_params(("arbitrary",), VMEM_LIMIT),
    )(*dz, df, x, dy, w_all, w_pre)


def _tn_mm(name, a, b, tn, out_dtype=f32, tk=2048):
    t, k = a.shape
    tk = min(tk, t)
    n = b.shape[1]
    nk = t // tk

    def body(a_ref, b_ref, o_ref, s_ref, acc_ref):
        j, kk = pl.program_id(0), pl.program_id(1)

        @pl.when(kk == 0)
        def _():
            acc_ref[...] = jnp.zeros_like(acc_ref)

        @pl.when((j == 0) & (kk == 0))
        def _():
            s_ref[...] = jnp.zeros_like(s_ref)

        av = a_ref[...]
        acc_ref[...] += _dot_tn(av, b_ref[...])

        @pl.when(j == 0)
        def _():
            s_ref[...] += jnp.sum(av.astype(f32), axis=0, keepdims=True)

        @pl.when(kk == nk - 1)
        def _():
            o_ref[...] = acc_ref[...].astype(out_dtype)

    return pl.pallas_call(
        body, name=name, grid=(n // tn, nk),
        in_specs=[pl.BlockSpec((tk, k), lambda j, kk: (kk, 0)), pl.BlockSpec((tk, tn), lambda j, kk: (kk, j))],
        out_specs=[pl.BlockSpec((k, tn), lambda j, kk: (0, j)), _whole((1, k))],
        out_shape=[jax.ShapeDtypeStruct((k, n), out_dtype), jax.ShapeDtypeStruct((1, k), f32)],
        scratch_shapes=[pltpu.VMEM((k, tn), f32)],
        compiler_params=_params(("arbitrary", "arbitrary"), VMEM_LIMIT),
    )(a, b)


def _position():
    return lax.axis_index("x"), lax.axis_index("y"), lax.axis_index("c")


ROW_BLOCK = 128


def _pick_rows(layout, first, count):
    acc = jnp.zeros((ROW_BLOCK, D), f32)
    seg_start = 0
    for ref, ref_row, rows in layout:
        lo, hi = max(first, seg_start), min(first + count, seg_start + rows)
        if lo < hi and ref is not None:
            off, take, done = ref_row + lo - seg_start, hi - lo, lo - first
            start = off // 16 * 16
            win = -(-(off - start + take) // 16) * 16
            r = lax.broadcasted_iota(jnp.int32, (ROW_BLOCK, win), 0)
            col = lax.broadcasted_iota(jnp.int32, (ROW_BLOCK, win), 1)
            pick = ((col - r == off - start - done) & (r >= done) & (r < done + take)).astype(bf16)
            acc = acc + _dot(pick, ref[pl.ds(start, win), :])
        seg_start += rows
    return acc


def _assemble_rows(shards_ref, shard_rows, segments, out_ref):
    layout = [(shards_ref.at[j], 0, shard_rows) for j in range(shards_ref.shape[0])]
    for out0, log0, count in segments:
        for b0 in range(0, count, ROW_BLOCK):
            block = _pick_rows(layout, log0 + b0, min(ROW_BLOCK, count - b0))
            out_ref[pl.ds(out0 + b0, ROW_BLOCK), :] = block.astype(bf16)


def _pack_pieces(blocks, shard_rows, padded):
    arrays = [a for a, _ in blocks if a is not None]
    piece_rows = padded // 2

    def body(*refs):
        out_ref = refs[-1]
        it = iter(refs[:-1])
        layout = [(None if a is None else next(it), 0, rows) for a, rows in blocks]
        for k in range(N_DEV):
            chip, half = divmod(k, 2)
            for b0 in range(0, piece_rows, ROW_BLOCK):
                n = min(ROW_BLOCK, piece_rows - b0)
                in_shard = half * piece_rows + b0
                count = max(0, min(n, shard_rows - in_shard))
                block = _pick_rows(layout, chip * shard_rows + in_shard, count)
                out_ref[k, pl.ds(b0, n), :] = block[:n].astype(bf16)

    vm = pl.BlockSpec(memory_space=pltpu.VMEM)
    return pl.pallas_call(
        body, name="pack_pieces", in_specs=[vm] * len(arrays), out_specs=vm,
        out_shape=jax.ShapeDtypeStruct((N_DEV, piece_rows, D), bf16),
        compiler_params=pltpu.CompilerParams(vmem_limit_bytes=VMEM_LIMIT),
    )(*arrays)


def _gather_shards(parts, small, shard_rows, segments, out_rows):
    n = len(parts)
    halves = [p.shape[0] // 2 for p in parts]
    cuts = [-(-h // 32) * 16 for h in halves]
    n_direct, n_relay, n_sib = 4 * n, 2 * n, 6 * n

    def body(*refs):
        srcs, small_src = refs[:n], refs[n]
        dsts, small_dst, whole_ref = refs[n + 1:2 * n + 1], refs[2 * n + 1], refs[2 * n + 2]
        send, recv, local = refs[2 * n + 3:]
        x, y, c = _position()
        me = 2 * x + y
        chips = [(1 - x, y), (x, 1 - y), (1 - x, 1 - y)]
        ids = [2 * px + py for px, py in chips]

        def rows(a, half, quarter):
            start = half * halves[a] + (cuts[a] if quarter else 0)
            return pl.ds(start, halves[a] - cuts[a] if quarter else cuts[a])

        def landing(a, shard, half, quarter):
            return dsts[a].at[shard, rows(a, half, quarter), :]

        def direct(a, nb, quarter, shard):
            k = (a * 2 + nb) * 2 + quarter
            px, py = chips[nb]
            return pltpu.make_async_remote_copy(
                src_ref=srcs[a].at[rows(a, c, quarter), :], dst_ref=landing(a, shard, c, quarter),
                send_sem=send.at[k], recv_sem=recv.at[k], device_id=(px, py, c), device_id_type=MESH)

        def relay(a, quarter, shard):
            k = n_direct + a * 2 + quarter
            px, py = chips[1 - quarter]
            return pltpu.make_async_remote_copy(
                src_ref=landing(a, shard, c, quarter), dst_ref=landing(a, shard, c, quarter),
                send_sem=send.at[k], recv_sem=recv.at[k], device_id=(px, py, c), device_id_type=MESH)

        def to_sibling(a, origin, quarter, half):
            k = n_direct + n_relay + (a * 3 + origin) * 2 + quarter
            return pltpu.make_async_remote_copy(
                src_ref=landing(a, ids[origin], half, quarter), dst_ref=landing(a, ids[origin], half, quarter),
                send_sem=send.at[k], recv_sem=recv.at[k], device_id=(x, y, 1 - c), device_id_type=MESH)

        def small_copy(j, shard):
            k = n_direct + n_relay + n_sib + j
            px, py = chips[j]
            return pltpu.make_async_remote_copy(
                src_ref=small_src, dst_ref=small_dst.at[shard], send_sem=send.at[k], recv_sem=recv.at[k],
                device_id=(px, py, c), device_id_type=MESH)

        own = [pltpu.make_async_copy(srcs[a], dsts[a].at[me], local.at[a]) for a in range(n)]
        own.append(pltpu.make_async_copy(small_src, small_dst.at[me], local.at[n]))
        for cp in own:
            cp.start()
        sent = [direct(a, nb, q, me) for q in range(2) for a in range(n) for nb in range(2)]
        sent += [small_copy(j, me) for j in range(3)]
        for cp in sent:
            cp.start()

        def passed_on(cp):
            cp.start()
            sent.append(cp)

        for q in range(2):
            for a in range(n):
                for nb in range(2):
                    direct(a, nb, q, ids[nb]).wait_recv()
                    passed_on(to_sibling(a, nb, q, c))
                    if nb == q:
                        passed_on(relay(a, q, ids[nb]))
        for a in range(n):
            for q in range(2):
                relay(a, q, ids[2]).wait_recv()
                passed_on(to_sibling(a, 2, q, c))
        for j in range(3):
            small_copy(j, ids[j]).wait_recv()
            for a in range(n):
                for q in range(2):
                    to_sibling(a, j, q, 1 - c).wait_recv()
        for cp in sent:
            cp.wait_send()
        for cp in own:
            cp.wait()
        _assemble_rows(dsts[0], shard_rows, segments, whole_ref)

    vm = pl.BlockSpec(memory_space=pltpu.VMEM)
    n_sems = n_direct + n_relay + n_sib + 3
    out = pl.pallas_call(
        body, name="gather_shards",
        in_specs=[vm] * (n + 1), out_specs=[vm] * (n + 2),
        out_shape=[jax.ShapeDtypeStruct((N_CHIPS,) + p.shape, p.dtype) for p in parts + [small]]
        + [jax.ShapeDtypeStruct((out_rows, parts[0].shape[1]), parts[0].dtype)],
        scratch_shapes=[pltpu.SemaphoreType.DMA((n_sems,)), pltpu.SemaphoreType.DMA((n_sems,)),
                        pltpu.SemaphoreType.DMA((n + 1,))],
        compiler_params=pltpu.CompilerParams(vmem_limit_bytes=VMEM_LIMIT),
    )(*parts, small)
    return out[1:]


def _allsum_rows(part):
    rows_n = part.shape[0]

    def body(x_ref, gath_ref, sum_ref, send_sems, recv_sems, local_sem):
        x, y, c = _position()
        me, sibling = (x, y, c), (x, y, 1 - c)
        chips = [(1 - x, y), (x, 1 - y), (1 - x, 1 - y)]

        def rows(px, py, pc):
            return gath_ref.at[pl.ds((4 * px + 2 * py + pc) * rows_n, rows_n), :]

        def copy(k, block, to, src=None):
            return pltpu.make_async_remote_copy(
                src_ref=rows(*block) if src is None else src, dst_ref=rows(*block),
                send_sem=send_sems.at[k], recv_sem=recv_sems.at[k], device_id=to, device_id_type=MESH)

        mine = pltpu.make_async_copy(x_ref, rows(*me), local_sem)
        mine.start()
        first = [copy(0, me, sibling, src=x_ref)]
        first += [copy(1 + j, me, (*chip, c), src=x_ref) for j, chip in enumerate(chips)]
        for cp in first:
            cp.start()
        passed = [copy(4 + j, (*chip, c), sibling) for j, chip in enumerate(chips)]
        for j, chip in enumerate(chips):
            copy(1 + j, (*chip, c), me).wait_recv()
            passed[j].start()
        copy(0, sibling, me).wait_recv()
        for j, chip in enumerate(chips):
            copy(4 + j, (*chip, 1 - c), me).wait_recv()
        for cp in first + passed:
            cp.wait_send()
        mine.wait()
        total = gath_ref[pl.ds(0, rows_n), :]
        for d in range(1, N_DEV):
            total = total + gath_ref[pl.ds(d * rows_n, rows_n), :]
        sum_ref[...] = total

    vm = pl.BlockSpec(memory_space=pltpu.VMEM)
    return pl.pallas_call(
        body, name="allsum_rows", in_specs=[vm], out_specs=[vm, vm],
        out_shape=[jax.ShapeDtypeStruct((N_DEV * rows_n, D), f32), jax.ShapeDtypeStruct((rows_n, D), f32)],
        scratch_shapes=[pltpu.SemaphoreType.DMA((7,)), pltpu.SemaphoreType.DMA((7,)), pltpu.SemaphoreType.DMA],
    )(part)[1]


PAIR_ROWS = 16


def _pair_reduce(name, pieces):
    _, r, n = pieces.shape

    def body(p_ref, o_ref, land, send, recv):
        x, y, c = _position()

        def remote(j, half):
            return pltpu.make_async_remote_copy(
                src_ref=p_ref.at[2 * j + half], dst_ref=land.at[j], send_sem=send.at[j], recv_sem=recv.at[j],
                device_id=(x, y, 1 - c), device_id_type=MESH)

        sends = [remote(j, 1 - c) for j in range(N_CHIPS)]
        for cp in sends:
            cp.start()
        for j in range(N_CHIPS):
            remote(j, c).wait_recv()

            def add_rows(i, carry, j=j):
                rows = pl.ds(pl.multiple_of(i * PAIR_ROWS, PAIR_ROWS), PAIR_ROWS)
                o_ref[j, rows, :] = (p_ref[2 * j + c, rows, :].astype(f32) + land[j, rows, :].astype(f32)).astype(bf16)
                return carry

            lax.fori_loop(0, r // PAIR_ROWS, add_rows, 0)
        for cp in sends:
            cp.wait_send()

    vm = pl.BlockSpec(memory_space=pltpu.VMEM)
    return pl.pallas_call(
        body, name=name, in_specs=[vm], out_specs=vm,
        out_shape=jax.ShapeDtypeStruct((N_CHIPS, r, n), bf16),
        scratch_shapes=[pltpu.VMEM((N_CHIPS, r, n), bf16), pltpu.SemaphoreType.DMA((N_CHIPS,)),
                        pltpu.SemaphoreType.DMA((N_CHIPS,))],
        compiler_params=pltpu.CompilerParams(vmem_limit_bytes=VMEM_LIMIT),
    )(pieces)


def _chip_exchange(arrs):
    n = len(arrs)
    heights = [a.shape[1] for a in arrs]
    cuts = [-(-r // 32) * 16 for r in heights]

    def body(*refs):
        srcs, dsts, relays = refs[:n], refs[n:2 * n], refs[2 * n:3 * n]
        send, recv, local = refs[3 * n:]
        x, y, c = _position()
        me = 2 * x + y
        chips = [(1 - x, y), (x, 1 - y), (1 - x, 1 - y)]
        ids = [2 * px + py for px, py in chips]

        def rows(a, quarter):
            return pl.ds(cuts[a], heights[a] - cuts[a]) if quarter else pl.ds(0, cuts[a])

        def held(a, quarter):
            size = heights[a] - cuts[a] if quarter else cuts[a]
            return relays[a].at[quarter, pl.ds(0, size), :]

        def direct(a, nb, piece, landing):
            px, py = chips[nb]
            return pltpu.make_async_remote_copy(
                src_ref=srcs[a].at[piece], dst_ref=dsts[a].at[landing], send_sem=send.at[a * 2 + nb],
                recv_sem=recv.at[a * 2 + nb], device_id=(px, py, c), device_id_type=MESH)

        def first_hop(a, quarter):
            k = 2 * n + a * 2 + quarter
            px, py = chips[quarter]
            return pltpu.make_async_remote_copy(
                src_ref=srcs[a].at[ids[2], rows(a, quarter), :], dst_ref=held(a, quarter), send_sem=send.at[k],
                recv_sem=recv.at[k], device_id=(px, py, c), device_id_type=MESH)

        def second_hop(a, quarter, origin):
            k = 4 * n + a * 2 + quarter
            px, py = chips[1 - quarter]
            return pltpu.make_async_remote_copy(
                src_ref=held(a, quarter), dst_ref=dsts[a].at[origin, rows(a, quarter), :], send_sem=send.at[k],
                recv_sem=recv.at[k], device_id=(px, py, c), device_id_type=MESH)

        own = [pltpu.make_async_copy(srcs[a].at[me], dsts[a].at[me], local.at[a]) for a in range(n)]
        sent = [first_hop(a, q) for a in range(n) for q in range(2)]
        sent += [direct(a, nb, ids[nb], me) for a in range(n) for nb in range(2)]
        for cp in sent + own:
            cp.start()
        for a in range(n):
            for q in range(2):
                first_hop(a, q).wait_recv()
                sent.append(second_hop(a, q, ids[q]))
                sent[-1].start()
        for a in range(n):
            for nb in range(2):
                direct(a, nb, me, ids[nb]).wait_recv()
            for q in range(2):
                second_hop(a, q, ids[2]).wait_recv()
        for cp in sent:
            cp.wait_send()
        for cp in own:
            cp.wait()

    anyspec = pl.BlockSpec(memory_space=pl.ANY)
    out = pl.pallas_call(
        body, name="chip_exchange", in_specs=[anyspec] * n, out_specs=[anyspec] * (2 * n),
        out_shape=[jax.ShapeDtypeStruct(a.shape, a.dtype) for a in arrs]
        + [jax.ShapeDtypeStruct((2, cut, a.shape[2]), a.dtype) for a, cut in zip(arrs, cuts)],
        scratch_shapes=[pltpu.SemaphoreType.DMA((6 * n,)), pltpu.SemaphoreType.DMA((6 * n,)),
                        pltpu.SemaphoreType.DMA((n,))],
    )(*arrs)
    return out[:n]


def _sum_swap_halves(slots):
    n = len(slots)

    def body(*refs):
        srcs, dsts, halves = refs[:n], refs[n:2 * n], refs[2 * n:3 * n]
        send, recv, local = refs[3 * n:]
        x, y, c = _position()

        def remote(a, landing):
            return pltpu.make_async_remote_copy(
                src_ref=halves[a], dst_ref=dsts[a].at[landing], send_sem=send.at[a], recv_sem=recv.at[a],
                device_id=(x, y, 1 - c), device_id_type=MESH)

        for a in range(n):
            def add_rows(i, carry, a=a):
                rows = pl.ds(pl.multiple_of(i * PAIR_ROWS, PAIR_ROWS), PAIR_ROWS)
                total = srcs[a][0, rows, :].astype(f32)
                for s in range(1, N_CHIPS):
                    total = total + srcs[a][s, rows, :].astype(f32)
                halves[a][rows, :] = total
                return carry

            lax.fori_loop(0, srcs[a].shape[1] // PAIR_ROWS, add_rows, 0)
        own = [pltpu.make_async_copy(halves[a], dsts[a].at[c], local.at[a]) for a in range(n)]
        sends = [remote(a, c) for a in range(n)]
        for cp in sends + own:
            cp.start()
        for a in range(n):
            remote(a, 1 - c).wait_recv()
        for cp in sends:
            cp.wait_send()
        for cp in own:
            cp.wait()

    vm = pl.BlockSpec(memory_space=pltpu.VMEM)
    return pl.pallas_call(
        body, name="sum_swap_halves", in_specs=[vm] * n, out_specs=[vm] * n,
        out_shape=[jax.ShapeDtypeStruct((2,) + a.shape[1:], f32) for a in slots],
        scratch_shapes=[pltpu.VMEM(a.shape[1:], f32) for a in slots]
        + [pltpu.SemaphoreType.DMA((n,)), pltpu.SemaphoreType.DMA((n,)), pltpu.SemaphoreType.DMA((n,))],
        compiler_params=pltpu.CompilerParams(vmem_limit_bytes=VMEM_LIMIT),
    )(*slots)


def _row_block(r):
    return 128 if r % 128 == 0 else r


def _adamw(name, w, g, m, v):
    r, n = w.shape
    if r % 128 == 0 or r * n <= 128 * 1024:
        rb, nb = _row_block(r), n
    else:
        rb, nb = r, LANES

    def body(w_ref, g_ref, m_ref, v_ref, d_ref, nm_ref, nv_ref):
        gv = g_ref[...]
        m2 = ADAM_B1 * m_ref[...] + (1.0 - ADAM_B1) * gv
        v2 = ADAM_B2 * v_ref[...] + (1.0 - ADAM_B2) * (gv * gv)
        m_hat = m2 / (1.0 - ADAM_B1 ** ADAM_STEP)
        v_hat = v2 / (1.0 - ADAM_B2 ** ADAM_STEP)
        d_ref[...] = (-ADAM_LR) * (m_hat / (jnp.sqrt(v_hat) + ADAM_EPS) + ADAM_WD * w_ref[...])
        nm_ref[...] = m2
        nv_ref[...] = v2

    spec = pl.BlockSpec((rb, nb), lambda i, j: (i, j))
    return pl.pallas_call(
        body, name=name, grid=(r // rb, n // nb), in_specs=[spec] * 4, out_specs=[spec] * 3,
        out_shape=[jax.ShapeDtypeStruct((r, n), f32)] * 3,
        compiler_params=_params(("parallel", "parallel"), VMEM_LIMIT),
    )(w, g, m, v)


def _local_step(x2, tgt2, seq, wt):
    nb = x2.shape[0] // seq
    h, qkv, f128 = _norm_qkv(x2, wt["pre_w"], wt["w_all"], wt["b_qkv"], 8 * D, wt["b_f"])
    rest = _mm("in_rest", h, wt["w_all"], (3 * D, 5 * D), wt["b_rest"], bf16, 1024, 1024)
    c = _forget_prep(f128, seq)
    qa, ka = _attn_prep(qkv, c)
    o_att, pa, lse = _attn_fwd(qa, ka, qkv, rest, seq)
    rnn_w = (wt["conv_w"], wt["conv_b"], wt["wa_d"], wt["wx_d"], wt["ba"], wt["bx"], wt["lam"])
    xc, a, hrec, pr = _rnn_fwd(rest, *rnn_w, seq)
    (do, dya, dyr, dmga, dmgr, doa, dga, dhrec, dgr, mrg, dy, delta, loss8, d_post) = _merge_loss(
        rest, pa, pr, o_att, hrec, wt["w_a"], wt["w_r"], wt["w_o"], x2, tgt2, wt["post_w"])
    d_wo, _ = _tn_mm("dw_out", mrg, do, D)
    d_wa, _ = _tn_mm("dw_branch_a", pa, dya, D)
    d_wr, _ = _tn_mm("dw_branch_r", pr, dyr, D)
    dxr, d_wad, d_wxd, vec = _rnn_bwd(dhrec, a, hrec, xc, rest, *rnn_w, seq)
    dq, dk, dv, dc = _attn_bwd(qa, ka, qkv, doa, lse, delta, seq)
    df, db_f = _forget_bwd(dc, f128, seq)
    pieces = [dq, dk, dv, dga, dxr, dgr, dmga, dmgr]
    gx, d_pre = _in_bwd(pieces, df, x2, dy, wt["w_all"], wt["pre_w"])
    names = ["q", "k", "v", "ga", "xr", "gr", "mga", "mgr"]
    dws, dbs = [], []
    for nm, piece in zip(names, pieces):
        dw_p, db_p = _tn_mm("dw_in_" + nm, piece, h, D, bf16)
        dws.append((dw_p, D))
        dbs.append(db_p)
    dw_f, _ = _tn_mm("dw_in_f", df, h, D, bf16)
    shard_rows = IN_TOTAL // N_CHIPS
    w_in_pieces = _pack_pieces(dws[:3] + [(dw_f, HEADS)] + dws[3:] + [(None, IN_TOTAL - IN_USED)], shard_rows,
                               _padded_rows(shard_rows))
    d_b_in = jnp.concatenate(dbs[:3] + [db_f[:, :HEADS]] + dbs[3:] + [jnp.zeros((1, IN_TOTAL - IN_USED), f32)], axis=1)
    return dict(loss=loss8[0, 0], grad_x=gx, pre_w=d_pre, w_in_pieces=w_in_pieces, b_in=d_b_in, conv_w=vec[4:8],
                conv_b=vec[3:4],
                wa_d=d_wad, ba=vec[0:1], wx_d=d_wxd, bx=vec[1:2], lam=vec[2:3], w_a=d_wa, w_r=d_wr, w_o=d_wo,
                post_w=d_post)


def _block_diag(w):
    g, bw, _ = w.shape
    eye = jnp.eye(g, dtype=w.dtype)
    return (w[:, :, None, :] * eye[:, None, :, None]).reshape(g * bw, g * bw)


def _gate_blocks(diag):
    half = diag.shape[1] // 2
    return jnp.stack([diag[:, :half, :half], diag[:, half:, half:]], axis=1).reshape(-1, half, half)


def _padded_rows(rows):
    return -(-rows // 32) * 32


def _pad_cols(a, n):
    return jnp.pad(a, ((0, 0), (0, n - a.shape[1])))


def _pad_rows(a, n):
    return jnp.pad(a, ((0, n - a.shape[0]), (0, 0)))


def kernel(x, pre_norm_w, w_in, b_in, conv_w, conv_b, rg_wa, rg_ba, rg_wx, rg_bx, rg_lambda, w_branch_a, w_branch_r, w_out, post_norm_w, loss_target, m_pre_norm_w, m_w_in, m_b_in, m_conv_w, m_conv_b, m_rg_wa, m_rg_ba, m_rg_wx, m_rg_bx, m_rg_lambda, m_w_branch_a, m_w_branch_r, m_w_out, m_post_norm_w, v_pre_norm_w, v_w_in, v_b_in, v_conv_w, v_conv_b, v_rg_wa, v_rg_ba, v_rg_wx, v_rg_bx, v_rg_lambda, v_w_branch_a, v_w_branch_r, v_w_out, v_post_norm_w):
    nb, seq, _ = x.shape
    chip = 2 * lax.axis_index("x") + lax.axis_index("y")
    n_groups = rg_wa.shape[1]

    w_in_t = jnp.transpose(w_in[0])
    shard_cols = w_in_t.shape[0]
    padded = _padded_rows(shard_cols)
    q_end, f_end = 3 * D, 3 * D + HEADS
    segments = [(0, 0, q_end), (q_end, f_end, IN_USED - f_end), (IN_USED - HEADS, q_end, HEADS)]
    g_a, g_r, g_o, g_cw, w_all = _gather_shards(
        [_pad_rows(w_in_t.astype(bf16), padded), w_branch_a[0].astype(bf16), w_branch_r[0].astype(bf16),
         w_out[0].astype(bf16)], conv_w[0], shard_cols, segments, IN_USED - HEADS + LANES)
    wt = dict(
        pre_w=pre_norm_w, post_w=post_norm_w,
        w_all=w_all, b_qkv=b_in[:, :q_end], b_f=_pad_cols(b_in[:, q_end:f_end], LANES), b_rest=b_in[:, f_end:IN_USED],
        w_a=g_a.reshape(D, D), w_r=g_r.reshape(D, D), w_o=g_o.reshape(D, D),
        conv_w=jnp.transpose(g_cw, (1, 0, 2)).reshape(4, D), conv_b=conv_b,
        wa_d=_block_diag(rg_wa[0]).astype(bf16), wx_d=_block_diag(rg_wx[0]).astype(bf16),
        ba=rg_ba, bx=rg_bx, lam=rg_lambda)

    part = _local_step(x.reshape(nb * seq, D), loss_target.reshape(nb * seq, D), seq, wt)
    loss = lax.psum(part["loss"], ("x", "y", "c"))
    grad_x = part["grad_x"].reshape(nb, seq, D)

    small = jnp.concatenate([
        part["pre_w"], _pad_cols(part["b_in"], 10 * D).reshape(10, D), part["conv_b"],
        _gate_blocks(part["wa_d"]).reshape(-1, D), part["ba"],
        _gate_blocks(part["wx_d"]).reshape(-1, D), part["bx"], part["lam"], part["post_w"],
        part["conv_w"]], axis=0)
    n_small = small.shape[0]
    n_rep = n_small - 4
    tot = _allsum_rows(_pad_rows(small, -(-n_small // 8) * 8))
    g_rep = tot[:n_rep]
    g_conv_w = lax.dynamic_slice_in_dim(tot[n_rep:n_small], chip * (D // N_CHIPS), D // N_CHIPS, axis=1)

    def unpack(p):
        o = [0]

        def take(k):
            o[0] += k
            return p[o[0] - k:o[0]]

        pre = take(1)
        b = take(10).reshape(1, 10 * D)[:, :IN_TOTAL]
        cb = take(1)
        wa = take(64).reshape(rg_wa.shape)
        ba = take(1)
        wx = take(64).reshape(rg_wx.shape)
        bx = take(1)
        lam = take(1)
        post = take(1)
        return dict(pre_norm_w=pre, b_in=b, conv_b=cb, rg_wa=wa, rg_ba=ba, rg_wx=wx, rg_bx=bx, rg_lambda=lam,
                    post_norm_w=post)

    grads = unpack(g_rep)
    replicated = dict(
        pre_norm_w=(pre_norm_w, m_pre_norm_w, v_pre_norm_w), b_in=(b_in, m_b_in, v_b_in),
        conv_b=(conv_b, m_conv_b, v_conv_b), rg_wa=(rg_wa, m_rg_wa, v_rg_wa), rg_ba=(rg_ba, m_rg_ba, v_rg_ba),
        rg_wx=(rg_wx, m_rg_wx, v_rg_wx), rg_bx=(rg_bx, m_rg_bx, v_rg_bx),
        rg_lambda=(rg_lambda, m_rg_lambda, v_rg_lambda), post_norm_w=(post_norm_w, m_post_norm_w, v_post_norm_w))
    deltas, new_m, new_v = {}, {}, {}
    for name, (w, m, v) in replicated.items():
        as2d = lambda a: a.reshape(-1, D) if a.ndim > 2 else a
        upd = _adamw("adamw_" + name, as2d(w), as2d(grads[name]), as2d(m), as2d(v))
        deltas[name], new_m[name], new_v[name] = [a.reshape(w.shape) for a in upd]

    p_aro = jnp.concatenate([part[k].reshape(N_DEV, D // N_DEV, D) for k in ("w_a", "w_r", "w_o")], axis=1)
    s_in, s_aro = _chip_exchange([_pair_reduce("pair_w_in", part["w_in_pieces"]),
                                  _pair_reduce("pair_w_aro", p_aro.astype(bf16))])
    f_in, f_aro = _sum_swap_halves([s_in, s_aro])
    g_w_in_t = f_in.reshape(padded, D)[:shard_cols]
    rows = D // N_DEV
    g_aro = [f_aro[:, i * rows:(i + 1) * rows, :].reshape(2 * rows, D) for i in range(3)]

    w_in_upd = _adamw("adamw_w_in", w_in_t, g_w_in_t, jnp.transpose(m_w_in[0]), jnp.transpose(v_w_in[0]))
    g_w_in, d_w_in, nm_w_in, nv_w_in = [jnp.transpose(a) for a in (g_w_in_t, *w_in_upd)]
    upd_a = _adamw("adamw_w_branch_a", w_branch_a[0], g_aro[0], m_w_branch_a[0], v_w_branch_a[0])
    upd_r = _adamw("adamw_w_branch_r", w_branch_r[0], g_aro[1], m_w_branch_r[0], v_w_branch_r[0])
    upd_o = _adamw("adamw_w_out", w_out[0], g_aro[2], m_w_out[0], v_w_out[0])
    d_aro, nm_aro, nv_aro = zip(upd_a, upd_r, upd_o)
    d_cw, nm_cw, nv_cw = _adamw("adamw_conv_w", conv_w[0], g_conv_w, m_conv_w[0], v_conv_w[0])

    def sharded(t_in, t_aro, t_cw):
        return dict(w_in=t_in[None], conv_w=t_cw[None], w_branch_a=t_aro[0][None], w_branch_r=t_aro[1][None],
                    w_out=t_aro[2][None])

    order = ["pre_norm_w", "w_in", "b_in", "conv_w", "conv_b", "rg_wa", "rg_ba", "rg_wx", "rg_bx", "rg_lambda",
             "w_branch_a", "w_branch_r", "w_out", "post_norm_w"]
    outs = [loss, grad_x]
    for rep, shd in ((grads, sharded(g_w_in, g_aro, g_conv_w)), (deltas, sharded(d_w_in, d_aro, d_cw)),
                     (new_m, sharded(nm_w_in, nm_aro, nm_cw)), (new_v, sharded(nv_w_in, nv_aro, nv_cw))):
        both = {**rep, **shd}
        outs.extend(both[k] for k in order)
    return tuple(outs)
```

```python
import jax
import jax.numpy as jnp
from jax import lax
from jax.experimental import pallas as pl
from jax.experimental.pallas import tpu as pltpu

f32 = jnp.float32
bf16 = jnp.bfloat16

D = 1024
HEADS = 16
LANES = 128
NORM_EPS = 1e-6
MASK_VALUE = -1e30
RG_C = 8.0
QK_SCALE = 0.125
TQ = 256
ATT_GROUP = 8
ATT_GROUP_FWD = 16
TL = 512
TM = 512
PREV_ROWS = 16
IN_USED = 8 * D + HEADS
IN_TOTAL = 9 * D + HEADS
N_CHIPS = 4
N_DEV = 8
ADAM_LR, ADAM_B1, ADAM_B2, ADAM_EPS, ADAM_WD, ADAM_STEP = 0.001, 0.9, 0.999, 1e-08, 0.01, 10
VMEM_LIMIT = 56 * 1024 * 1024
MESH = pl.DeviceIdType.MESH


def _dot(a, b):
    return jnp.dot(a, b, preferred_element_type=f32)


def _dot_nt(a, b):
    return lax.dot_general(a, b, (((1,), (1,)), ((), ())), preferred_element_type=f32)


def _dot_tn(a, b):
    return lax.dot_general(a, b, (((0,), (0,)), ((), ())), preferred_element_type=f32)


def _sig(x):
    return 0.5 * jnp.tanh(0.5 * x) + 0.5


def _softplus(x):
    return jnp.maximum(x, 0.0) + jnp.log(1.0 + jnp.exp(-jnp.abs(x)))


def _params(sem, vmem=None):
    return pltpu.CompilerParams(dimension_semantics=sem, vmem_limit_bytes=vmem)


def _tile(tm, width, cb=0):
    return pl.BlockSpec((tm, width), lambda i, cb=cb: (i, cb))


def _whole(shape):
    nd = len(shape)
    return pl.BlockSpec(shape, lambda *_: (0,) * nd)


def _norm_qkv(x, w_pre, w_all, b_qkv, f_row0, b_f, tm=1024):
    t = x.shape[0]
    tm = min(tm, t)
    n = b_qkv.shape[1]

    def body(x_ref, wp_ref, w_ref, b_ref, wf_ref, bf_ref, h_ref, o_ref, f_ref):
        @pl.when(pl.program_id(1) == 0)
        def _():
            xv = x_ref[...]
            r = lax.rsqrt(jnp.mean(xv * xv, axis=-1, keepdims=True) + NORM_EPS)
            h = (xv * r * wp_ref[...]).astype(bf16)
            h_ref[...] = h
            f_ref[...] = _dot_nt(h, wf_ref[...]) + bf_ref[...]

        o_ref[...] = (_dot_nt(h_ref[...], w_ref[...]) + b_ref[...]).astype(bf16)

    return pl.pallas_call(
        body, name="norm_qkv", grid=(t // tm, n // D),
        in_specs=[pl.BlockSpec((tm, D), lambda i, j: (i, 0)), _whole((1, D)), pl.BlockSpec((D, D), lambda i, j: (j, 0)),
                  pl.BlockSpec((1, D), lambda i, j: (0, j)),
                  pl.BlockSpec((LANES, D), lambda i, j: (f_row0 // LANES, 0)), _whole((1, LANES))],
        out_specs=[pl.BlockSpec((tm, D), lambda i, j: (i, 0)), pl.BlockSpec((tm, D), lambda i, j: (i, j)),
                   pl.BlockSpec((tm, LANES), lambda i, j: (i, 0))],
        out_shape=[jax.ShapeDtypeStruct((t, D), bf16), jax.ShapeDtypeStruct((t, n), bf16),
                   jax.ShapeDtypeStruct((t, LANES), f32)],
        compiler_params=_params(("parallel", "arbitrary"), VMEM_LIMIT),
    )(x, w_pre, w_all, b_qkv, w_all, b_f)


def _mm(name, a, w, w_rows, bias, out_dtype, tm, tn):
    t, k = a.shape
    tm = min(tm, t)
    row0, n = w_rows
    assert row0 % tn == 0

    def body(a_ref, w_ref, b_ref, o_ref):
        o_ref[...] = (_dot_nt(a_ref[...], w_ref[...]) + b_ref[...]).astype(out_dtype)

    return pl.pallas_call(
        body, name=name, grid=(t // tm, n // tn),
        in_specs=[pl.BlockSpec((tm, k), lambda i, j: (i, 0)), pl.BlockSpec((tn, k), lambda i, j: (row0 // tn + j, 0)),
                  pl.BlockSpec((1, tn), lambda i, j: (0, j))],
        out_specs=pl.BlockSpec((tm, tn), lambda i, j: (i, j)), out_shape=jax.ShapeDtypeStruct((t, n), out_dtype),
        compiler_params=_params(("parallel", "parallel"), VMEM_LIMIT),
    )(a, w, bias)


def _forget_prep(f128, seq):
    t = f128.shape[0]
    nb = seq // LANES

    def body(f_ref, c_ref):
        r = lax.broadcasted_iota(jnp.int32, (LANES, LANES), 0)
        cidx = lax.broadcasted_iota(jnp.int32, (LANES, LANES), 1)
        tri = (r >= cidx).astype(f32)
        carry = jnp.zeros((1, LANES), f32)
        for blk in range(nb):
            fv = f_ref[pl.ds(blk * LANES, LANES), :]
            lf = -_softplus(-fv)
            c_ref[pl.ds(blk * LANES, LANES), :] = (
                jnp.dot(tri, lf, preferred_element_type=f32, precision=lax.Precision.HIGHEST) + carry)
            carry = carry + jnp.sum(lf, axis=0, keepdims=True)

    return pl.pallas_call(
        body, name="forget_prep", grid=(t // seq,),
        in_specs=[pl.BlockSpec((seq, LANES), lambda b: (b, 0))],
        out_specs=pl.BlockSpec((seq, LANES), lambda b: (b, 0)),
        out_shape=jax.ShapeDtypeStruct((t, LANES), f32),
        compiler_params=_params(("parallel",)),
    )(f128)


def _split3(cv):
    hi = cv.astype(bf16)
    r1 = cv - hi.astype(f32)
    mid = r1.astype(bf16)
    lo = (r1 - mid.astype(f32)).astype(bf16)
    return hi, mid, lo


def _q_operand(q2, half):
    lane = lax.broadcasted_iota(jnp.int32, (1, LANES), 1)
    first = 64 * (1 - half)
    ones = jnp.where((lane >= first) & (lane < first + 3), jnp.ones((), bf16), jnp.zeros((), bf16))
    return jnp.where((lane // 64) == half, q2 * jnp.asarray(QK_SCALE, bf16), ones)


def _attn_prep(qkv, c):
    t = qkv.shape[0]

    def body(k_ref, c_ref, ka_ref):
        lane = lax.broadcasted_iota(jnp.int32, (1, LANES), 1)
        cv = c_ref[...]
        zero = jnp.zeros((), bf16)
        for head in range(HEADS):
            half, first = head % 2, 64 * (1 - head % 2)
            ch = jnp.sum(jnp.where(lane == head, cv, 0.0), axis=1, keepdims=True)
            hi, mid, lo = _split3(-ch)
            pieces = jnp.where(lane == first, hi, jnp.where(lane == first + 1, mid, jnp.where(lane == first + 2, lo, zero)))
            ka_ref[:, pl.ds(head * LANES, LANES)] = jnp.where(
                (lane // 64) == half, k_ref[:, pl.ds((head // 2) * LANES, LANES)], pieces)

    tm = min(TM, t)
    return pl.pallas_call(
        body, name="attn_prep", grid=(t // tm,),
        in_specs=[_tile(tm, D, 1), _tile(tm, LANES)],
        out_specs=pl.BlockSpec((tm, 2 * D), lambda i: (i, 0)),
        out_shape=jax.ShapeDtypeStruct((t, 2 * D), bf16),
        compiler_params=_params(("parallel",)),
    )(qkv, c)


def _attn_fwd(ka, qkv, rest, seq):
    t = qkv.shape[0]
    nb, nq = t // seq, seq // TQ

    hg = ATT_GROUP_FWD
    ng = HEADS // hg

    def body(q_ref, k_ref, v_ref, ga_ref, o_ref, pa_ref, lse_ref, acc_scr, qop_scr):
        qi, gi = pl.program_id(1), pl.program_id(2)
        krow = lax.broadcasted_iota(jnp.int32, (TQ, TQ), 0)
        qcol = lax.broadcasted_iota(jnp.int32, (TQ, TQ), 1)
        acc_scr[...] = jnp.zeros_like(acc_scr)
        for g in range(hg):
            qop_scr[g] = _q_operand(q_ref[:, pl.ds((g // 2) * LANES, LANES)], g % 2)

        def kv_step(kt, carry, masked):
            ks = pl.multiple_of(kt * TQ, TQ)
            sts = [_dot_nt(k_ref[pl.ds(ks, TQ), pl.ds(g * LANES, LANES)], qop_scr[g]) for g in range(hg)]
            if masked:
                sts = [jnp.where(krow <= qcol, st, MASK_VALUE) for st in sts]
            m_new = [jnp.maximum(carry[g][0], jnp.max(sts[g], axis=0, keepdims=True)) for g in range(hg)]
            ps = [jnp.exp(sts[g] - m_new[g]) for g in range(hg)]
            alphas = [jnp.exp(carry[g][0] - m_new[g]) for g in range(hg)]
            phi = [ps[g].astype(bf16) for g in range(hg)]
            plo = [(ps[g] - phi[g].astype(f32)).astype(bf16) for g in range(hg)]
            vs = [v_ref[pl.ds(ks, TQ), pl.ds(j * LANES, LANES)] for j in range(hg // 2)]
            pvs = [_dot_tn(vs[g // 2], phi[g]) + _dot_tn(vs[g // 2], plo[g]) for g in range(hg)]
            olds = [acc_scr[g] for g in range(hg)]
            for g in range(hg):
                acc_scr[g] = alphas[g] * olds[g] + pvs[g]
            return tuple((m_new[g], alphas[g] * carry[g][1] + jnp.sum(ps[g], axis=0, keepdims=True))
                         for g in range(hg))

        init = tuple((jnp.full((1, TQ), MASK_VALUE, f32), jnp.zeros((1, TQ), f32)) for _ in range(hg))
        carry = lax.fori_loop(0, qi, lambda kt, cr: kv_step(kt, cr, False), init)
        stats = kv_step(qi, carry, True)
        drow = lax.broadcasted_iota(jnp.int32, (LANES, TQ), 0)
        for g in range(hg):
            m, l = stats[g]
            lse_ref[0, pl.ds(hg * gi + g, 1), :] = m + jnp.log(l)
        for j in range(hg // 2):
            o2 = jnp.where(drow < 64, acc_scr[2 * j] / stats[2 * j][1], acc_scr[2 * j + 1] / stats[2 * j + 1][1]).T
            o_ref[:, pl.ds(j * LANES, LANES)] = o2
            ga = ga_ref[:, pl.ds(j * LANES, LANES)].astype(f32)
            pa_ref[:, pl.ds(j * LANES, LANES)] = (o2 * (ga * _sig(ga))).astype(bf16)

    vw = hg * 64
    tile = pl.BlockSpec((TQ, vw), lambda b, qi, gi: (b * nq + qi, gi))
    return pl.pallas_call(
        body, name="attn_fwd", grid=(nb, nq, ng),
        in_specs=[tile, pl.BlockSpec((seq, hg * LANES), lambda b, qi, gi: (b, gi)),
                  pl.BlockSpec((seq, vw), lambda b, qi, gi: (b, 2 * ng + gi)), tile],
        out_specs=[tile, tile, pl.BlockSpec((1, HEADS, TQ), lambda b, qi, gi: (b * nq + qi, 0, 0))],
        out_shape=[jax.ShapeDtypeStruct((t, D), f32), jax.ShapeDtypeStruct((t, D), bf16),
                   jax.ShapeDtypeStruct((t // TQ, HEADS, TQ), f32)],
        scratch_shapes=[pltpu.VMEM((hg, LANES, TQ), f32), pltpu.VMEM((hg, TQ, LANES), bf16)],
        compiler_params=_params(("parallel", "parallel", "arbitrary"), VMEM_LIMIT),
    )(qkv, ka, qkv, rest)


def _shifted_rows(x, top8, prev8, shift, row, row8):
    body = pltpu.roll(x, shift, 0)
    head = jnp.where(row8 < shift, pltpu.roll(prev8, shift, 0), pltpu.roll(top8, shift, 0))
    return body, head


def _rnn_gates(xc, wa_ref, wx_ref, ba_ref, bx_ref, lam_ref):
    xcb = xc.astype(bf16)
    r = _sig(_dot(xcb, wa_ref[...]) + ba_ref[...])
    i = _sig(_dot(xcb, wx_ref[...]) + bx_ref[...])
    sp = _softplus(-lam_ref[...])
    log_a = (-RG_C) * r * sp
    th = jnp.tanh(log_a)
    w1 = (-2.0) * th / (1.0 - th)
    sq = jnp.sqrt(jnp.maximum(w1, 0.0))
    return r, i, sp, log_a, w1, sq


def _conv_tile(x_ref, xprev_ref, has_prev, cw_ref, cb_ref, xc_ref):
    row = lax.broadcasted_iota(jnp.int32, (TL, D), 0)
    row8 = lax.broadcasted_iota(jnp.int32, (8, D), 0)
    x = x_ref[...].astype(f32)
    top8 = x[:8]
    prev8 = jnp.where(has_prev, xprev_ref[...].astype(f32)[PREV_ROWS - 8:], 0.0)
    xc = cb_ref[...] + cw_ref[pl.ds(3, 1), :] * x
    xc8 = cb_ref[...] + cw_ref[pl.ds(3, 1), :] * top8
    for sh in range(1, 4):
        w = cw_ref[pl.ds(3 - sh, 1), :]
        xs, xs8 = _shifted_rows(x, top8, prev8, sh, row, row8)
        xc = xc + w * xs
        xc8 = xc8 + w * xs8
    xc_ref[...] = xc
    xc_ref[pl.ds(0, 8), :] = xc8


def _rnn_fwd(rest, conv_w, conv_b, wa_d, wx_d, ba, bx, lam, seq):
    t = rest.shape[0]
    nb, nt = t // seq, seq // TL

    def body(x_ref, xprev_ref, gr_ref, cw_ref, cb_ref, wa_ref, wx_ref, ba_ref, bx_ref, lam_ref,
             xc_ref, a_ref, h_ref, pr_ref, xc_scr, u_scr, h_scr, carry):
        tt = pl.program_id(1)
        _conv_tile(x_ref, xprev_ref, tt > 0, cw_ref, cb_ref, xc_scr)
        xc = xc_scr[...]
        xc_ref[...] = xc.astype(bf16)
        r, i, sp, log_a, w1, sq = _rnn_gates(xc, wa_ref, wx_ref, ba_ref, bx_ref, lam_ref)
        a_ref[...] = jnp.exp(log_a)
        u_scr[...] = sq * (i * xc)

        @pl.when(tt == 0)
        def _():
            carry[...] = jnp.zeros_like(carry)

        def step(s, h):
            h = a_ref[pl.ds(s, 1), :] * h + u_scr[pl.ds(s, 1), :]
            h_scr[pl.ds(s, 1), :] = h
            return h

        carry[...] = lax.fori_loop(0, TL, step, carry[...], unroll=8)
        gr = gr_ref[...].astype(f32)
        h = h_scr[...]
        h_ref[...] = h.astype(bf16)
        pr_ref[...] = (h * (gr * _sig(gr))).astype(bf16)

    tile = lambda cb: pl.BlockSpec((TL, D), lambda b, tt, cb=cb: (b * nt + tt, cb))
    prev = lambda cb: pl.BlockSpec(
        (PREV_ROWS, D), lambda b, tt, cb=cb: (jnp.maximum((b * nt + tt) * (TL // PREV_ROWS) - 1, 0), cb))
    vec = _whole((1, D))
    return pl.pallas_call(
        body, name="rnn_fwd", grid=(nb, nt),
        in_specs=[tile(1), prev(1), tile(2), _whole((4, D)), vec, _whole((D, D)), _whole((D, D)), vec, vec, vec],
        out_specs=[tile(0)] * 4,
        out_shape=[jax.ShapeDtypeStruct((t, D), dt) for dt in (bf16, f32, bf16, bf16)],
        scratch_shapes=[pltpu.VMEM((TL, D), f32)] * 3 + [pltpu.VMEM((1, D), f32)],
        compiler_params=_params(("parallel", "arbitrary"), VMEM_LIMIT),
    )(rest, rest, rest, conv_w, conv_b, wa_d, wx_d, ba, bx, lam)


def _merge_loss(rest, pa, pr, o_att, hrec, w_a, w_r, w_out, x, tgt, w_post):
    t = x.shape[0]

    def branch(dy, w_ref, g_ref, act):
        dp = _dot_nt(dy, w_ref[...])
        g = g_ref[...].astype(f32)
        sg = _sig(g)
        return (dp * (g * sg)).astype(bf16), (dp * act * (sg * (1.0 + g * (1.0 - sg)))).astype(bf16)

    def body(mga_ref, mgr_ref, pa_ref, pr_ref, ga_ref, gr_ref, oa_ref, h_ref, x_ref, t_ref, wa_ref, wr_ref, wo_ref,
             w_ref, do_ref, dya_ref, dyr_ref, dmga_ref, dmgr_ref, doa_ref, dga_ref, dh_ref, dgr_ref, mrg_ref, dy_ref,
             delta_ref, loss_ref, dwp_ref):
        @pl.when(pl.program_id(0) == 0)
        def _():
            loss_ref[...] = jnp.zeros_like(loss_ref)
            dwp_ref[...] = jnp.zeros_like(dwp_ref)

        sa, sr = _sig(mga_ref[...].astype(f32)), _sig(mgr_ref[...].astype(f32))
        ya, yr = _dot(pa_ref[...], wa_ref[...]), _dot(pr_ref[...], wr_ref[...])
        mrg = (sa * ya + sr * yr).astype(bf16)
        mrg_ref[...] = mrg
        ov = _dot(mrg, wo_ref[...])
        w = w_ref[...]
        r2 = lax.rsqrt(jnp.mean(ov * ov, axis=-1, keepdims=True) + NORM_EPS)
        oh = ov * r2
        e = x_ref[...] + oh * w - t_ref[...]
        loss_ref[...] += 0.5 * jnp.sum(jnp.mean(e * e, axis=-1, keepdims=True))
        dy = e * (1.0 / D)
        dy_ref[...] = dy
        dwp_ref[...] += jnp.sum(dy * oh, axis=0, keepdims=True)
        doh = dy * w
        do = (r2 * (doh - oh * jnp.mean(doh * oh, axis=-1, keepdims=True))).astype(bf16)
        do_ref[...] = do

        dm = _dot_nt(do, wo_ref[...])
        dya, dyr = (dm * sa).astype(bf16), (dm * sr).astype(bf16)
        dya_ref[...] = dya
        dyr_ref[...] = dyr
        dmga_ref[...] = (dm * ya * sa * (1.0 - sa)).astype(bf16)
        dmgr_ref[...] = (dm * yr * sr * (1.0 - sr)).astype(bf16)
        o_att = oa_ref[...]
        doa, dga_ref[...] = branch(dya, wa_ref, ga_ref, o_att)
        doa_ref[...] = doa
        dh_ref[...], dgr_ref[...] = branch(dyr, wr_ref, gr_ref, h_ref[...].astype(f32))
        ch = lax.broadcasted_iota(jnp.int32, (D, LANES), 0)
        hd = lax.broadcasted_iota(jnp.int32, (D, LANES), 1)
        pick = (ch // 64 == hd).astype(bf16)
        per_head = sum(_dot(piece, pick) for piece in _split3(doa.astype(f32) * o_att))
        delta_ref[0] = per_head.T[:HEADS, :]

    once = pl.BlockSpec((D, D), lambda i: (0, 0), pipeline_mode=pl.Buffered(1))
    rows = _tile(TQ, D)
    return pl.pallas_call(
        body, name="merge_loss", grid=(t // TQ,),
        in_specs=[_tile(TQ, D, 3), _tile(TQ, D, 4), rows, rows, _tile(TQ, D, 0), _tile(TQ, D, 2), rows, rows, rows, rows,
                  once, once, once, _whole((1, D))],
        out_specs=[rows] * 11 + [pl.BlockSpec((1, HEADS, TQ), lambda i: (i, 0, 0)), _whole((8, LANES)), _whole((1, D))],
        out_shape=[jax.ShapeDtypeStruct((t, D), bf16)] * 10 + [jax.ShapeDtypeStruct((t, D), f32),
                   jax.ShapeDtypeStruct((t // TQ, HEADS, TQ), f32), jax.ShapeDtypeStruct((8, LANES), f32),
                   jax.ShapeDtypeStruct((1, D), f32)],
        compiler_params=_params(("arbitrary",), VMEM_LIMIT),
    )(rest, rest, pa, pr, rest, rest, o_att, hrec, x, tgt, w_a, w_r, w_out, w_post)


def _rnn_bwd(dh, a, h, xc, rest, conv_w, conv_b, wa_d, wx_d, ba, bx, lam, seq):
    t = dh.shape[0]
    nb, nt = t // seq, seq // TL
    diag = (D // LANES, LANES, LANES)

    def body(dh_ref, a_ref, h_ref, hprev_ref, xc_ref, x_ref, cw_ref, cb_ref, wa_ref, wx_ref,
             ba_ref, bx_ref, lam_ref, dxr_ref, dwa_ref, dwx_ref, vec_ref, g_scr, dxc_scr, dxr_scr, qcarry, dxc_next):
        b, tt = pl.program_id(0), pl.program_id(1)
        rt = nt - 1 - tt

        @pl.when((b == 0) & (tt == 0))
        def _():
            dwa_ref[...] = jnp.zeros_like(dwa_ref)
            dwx_ref[...] = jnp.zeros_like(dwx_ref)
            vec_ref[...] = jnp.zeros_like(vec_ref)

        @pl.when(tt == 0)
        def _():
            qcarry[...] = jnp.zeros_like(qcarry)
            dxc_next[...] = jnp.zeros_like(dxc_next)

        g_scr[...] = dh_ref[...].astype(f32)

        def step(k, q):
            s = TL - 1 - k
            g = g_scr[pl.ds(s, 1), :] + q
            g_scr[pl.ds(s, 1), :] = g
            return a_ref[pl.ds(s, 1), :] * g

        qcarry[...] = lax.fori_loop(0, TL, step, qcarry[...], unroll=8)

        row = lax.broadcasted_iota(jnp.int32, (TL, D), 0)
        row8 = lax.broadcasted_iota(jnp.int32, (8, D), 0)
        g = g_scr[...]
        av = a_ref[...]
        xc = xc_ref[...].astype(f32)
        hlast = jnp.where(rt > 0, hprev_ref[...].astype(f32)[PREV_ROWS - 1:], 0.0)
        hp = jnp.where(row == 0, hlast, pltpu.roll(h_ref[...].astype(f32), 1, 0))
        r, i, sp, log_a, w1, sq = _rnn_gates(xc, wa_ref, wx_ref, ba_ref, bx_ref, lam_ref)
        dix = g * sq
        di = dix * xc
        dxc = dix * i
        dsq = g * (i * xc)
        dlog_a = g * hp * av - dsq * jnp.where(sq > 0.0, (1.0 - w1) / sq, 0.0)
        dpr = (dlog_a * ((-RG_C) * sp)) * r * (1.0 - r)
        dpi = di * i * (1.0 - i)
        dprb, dpib, xcb = dpr.astype(bf16), dpi.astype(bf16), xc.astype(bf16)
        dxc = dxc + _dot_nt(dprb, wa_ref[...]) + _dot_nt(dpib, wx_ref[...])
        for j in range(D // LANES):
            cols = slice(j * LANES, (j + 1) * LANES)
            dwa_ref[j] += _dot_tn(xcb[:, cols], dprb[:, cols])
            dwx_ref[j] += _dot_tn(xcb[:, cols], dpib[:, cols])
        vec_ref[pl.ds(0, 1), :] += jnp.sum(dpr, axis=0, keepdims=True)
        vec_ref[pl.ds(1, 1), :] += jnp.sum(dpi, axis=0, keepdims=True)
        dsp = jnp.sum(dlog_a * ((-RG_C) * r), axis=0, keepdims=True)
        vec_ref[pl.ds(2, 1), :] += dsp * (-_sig(-lam_ref[...]))
        vec_ref[pl.ds(3, 1), :] += jnp.sum(dxc, axis=0, keepdims=True)

        dxc_scr[...] = dxc
        bot8 = dxc_scr[pl.ds(TL - 8, 8), :]
        nxt8 = dxc_next[...]
        x = x_ref[...].astype(f32)
        x_bot8 = x[TL - 8:]
        dxr = cw_ref[pl.ds(3, 1), :] * dxc
        dxr8 = cw_ref[pl.ds(3, 1), :] * bot8
        vec_ref[pl.ds(7, 1), :] += jnp.sum(dxc * x, axis=0, keepdims=True)
        for sh in range(1, 4):
            w = cw_ref[pl.ds(3 - sh, 1), :]
            up = pltpu.roll(dxc, TL - sh, 0)
            from_next = pltpu.roll(nxt8, 8 - sh, 0)
            dxr = dxr + w * up
            dxr8 = dxr8 + w * jnp.where(row8 < 8 - sh, pltpu.roll(bot8, 8 - sh, 0), from_next)
            inside = jnp.sum(jnp.where(row < TL - sh, up, 0.0) * x, axis=0, keepdims=True)
            across = jnp.sum(jnp.where(row8 >= 8 - sh, from_next, 0.0) * x_bot8, axis=0, keepdims=True)
            vec_ref[pl.ds(7 - sh, 1), :] += inside + across
        dxr_scr[...] = dxr
        dxr_scr[pl.ds(TL - 8, 8), :] = dxr8
        dxr_ref[...] = dxr_scr[...].astype(bf16)
        dxc_next[...] = dxc_scr[pl.ds(0, 8), :]

    tile = lambda cb: pl.BlockSpec((TL, D), lambda b, tt, cb=cb: (b * nt + nt - 1 - tt, cb))
    prev = lambda cb: pl.BlockSpec(
        (PREV_ROWS, D), lambda b, tt, cb=cb: (jnp.maximum((b * nt + nt - 1 - tt) * (TL // PREV_ROWS) - 1, 0), cb))
    vec = _whole((1, D))
    return pl.pallas_call(
        body, name="rnn_bwd", grid=(nb, nt),
        in_specs=[tile(0), tile(0), tile(0), prev(0), tile(0), tile(1),
                  _whole((4, D)), vec, _whole((D, D)), _whole((D, D)), vec, vec, vec],
        out_specs=[tile(0), _whole(diag), _whole(diag), _whole((8, D))],
        out_shape=[jax.ShapeDtypeStruct((t, D), bf16), jax.ShapeDtypeStruct(diag, f32),
                   jax.ShapeDtypeStruct(diag, f32), jax.ShapeDtypeStruct((8, D), f32)],
        scratch_shapes=[pltpu.VMEM((TL, D), f32), pltpu.VMEM((TL, D), f32), pltpu.VMEM((TL, D), f32),
                        pltpu.VMEM((1, D), f32), pltpu.VMEM((8, D), f32)],
        compiler_params=_params(("arbitrary", "arbitrary"), VMEM_LIMIT),
    )(dh, a, h, h, xc, rest, conv_w, conv_b, wa_d, wx_d, ba, bx, lam)


def _attn_bwd(ka, qkv, doa, lse, delta, seq):
    t = qkv.shape[0]
    nb, nq = t // seq, seq // TQ
    hg = ATT_GROUP
    ng, npair = HEADS // hg, hg // 2

    def body(ka_ref, q_ref, k_ref, v_ref, do_ref, lse_ref, dl_ref, dq_ref, dk_ref, dv_ref, dc_ref,
             dqt_scr, dk_scr, dv_scr, ds_scr, kht_scr):
        gi, kt = pl.program_id(1), pl.program_id(2)
        lane = lax.broadcasted_iota(jnp.int32, (1, LANES), 1)
        krow = lax.broadcasted_iota(jnp.int32, (TQ, TQ), 0)
        qcol = lax.broadcasted_iota(jnp.int32, (TQ, TQ), 1)
        lmask = [(lane // 64) == hh for hh in range(2)]
        scale = jnp.asarray(QK_SCALE, bf16)

        @pl.when(kt == 0)
        def _():
            dqt_scr[...] = jnp.zeros_like(dqt_scr)

        dk_scr[...] = jnp.zeros_like(dk_scr)
        dv_scr[...] = jnp.zeros_like(dv_scr)
        ds_scr[...] = jnp.zeros_like(ds_scr)
        for g in range(hg):
            k2 = k_ref[:, pl.ds((g // 2) * LANES, LANES)]
            kht_scr[g] = jnp.where(lmask[g % 2], k2, jnp.zeros_like(k2)).T

        def q_step(qt, masked):
            qs = pl.multiple_of(qt * TQ, TQ)
            heads = range(hg)
            do2 = [do_ref[pl.ds(qs, TQ), pl.ds(j * LANES, LANES)] for j in range(npair)]
            q2 = [q_ref[pl.ds(qs, TQ), pl.ds(j * LANES, LANES)] for j in range(npair)]
            doh = [jnp.where(lmask[g % 2], do2[g // 2], jnp.zeros_like(do2[0])) for g in heads]
            qh = [jnp.where(lmask[g % 2], q2[g // 2], jnp.zeros_like(q2[0])) * scale for g in heads]
            st = [_dot_nt(ka_ref[:, pl.ds(g * LANES, LANES)], _q_operand(q2[g // 2], g % 2)) for g in heads]
            if masked:
                st = [jnp.where(krow <= qcol, s, MASK_VALUE) for s in st]
            dp = [_dot_nt(v_ref[:, pl.ds((g // 2) * LANES, LANES)], doh[g]) for g in heads]
            p = [jnp.exp(st[g] - lse_ref[qt, pl.ds(hg * gi + g, 1), :]) for g in heads]
            ds = [p[g] * (dp[g] - dl_ref[qt, pl.ds(hg * gi + g, 1), :]) for g in heads]
            pb = [x.astype(bf16) for x in p]
            dsb = [x.astype(bf16) for x in ds]
            for j in range(npair):
                a, b = 2 * j, 2 * j + 1
                dv_scr[j] += _dot(pb[a], doh[a]) + _dot(pb[b], doh[b])
                dk_scr[j] += _dot(dsb[a], qh[a]) + _dot(dsb[b], qh[b])
                dqt_scr[qt, j] += (_dot(kht_scr[a], dsb[a]) + _dot(kht_scr[b], dsb[b])) * QK_SCALE
            for g in heads:
                ds_scr[g] += ds[g][:, :LANES] + ds[g][:, LANES:]

        q_step(kt, True)

        def loop_body(qt, carry):
            q_step(qt, False)
            return carry

        lax.fori_loop(kt + 1, nq, loop_body, 0)

        dc = jnp.zeros((TQ, LANES), f32)
        for g in range(hg):
            dc = jnp.where(lane == g, -jnp.sum(ds_scr[g], axis=1, keepdims=True), dc)
        dc_ref[...] = dc
        for j in range(npair):
            dk_ref[:, pl.ds(j * LANES, LANES)] = dk_scr[j].astype(bf16)
            dv_ref[:, pl.ds(j * LANES, LANES)] = dv_scr[j].astype(bf16)

        @pl.when(kt == nq - 1)
        def _():
            for qt in range(nq):
                for j in range(npair):
                    dq_ref[pl.ds(qt * TQ, TQ), pl.ds(j * LANES, LANES)] = dqt_scr[qt, j].T.astype(bf16)

    vw = hg * 64
    seqspec = pl.BlockSpec((seq, vw), lambda b, gi, kt: (b, gi))
    kspec = lambda off: pl.BlockSpec((TQ, vw), lambda b, gi, kt: (b * nq + kt, off + gi))
    rowspec = pl.BlockSpec((nq, HEADS, TQ), lambda b, gi, kt: (b, 0, 0))
    return pl.pallas_call(
        body, name="attn_bwd", grid=(nb, ng, nq),
        in_specs=[pl.BlockSpec((TQ, hg * LANES), lambda b, gi, kt: (b * nq + kt, gi)), seqspec, kspec(ng), kspec(2 * ng), seqspec, rowspec, rowspec],
        out_specs=[seqspec, kspec(0), kspec(0), pl.BlockSpec((TQ, LANES), lambda b, gi, kt: (b * nq + kt, gi))],
        out_shape=[jax.ShapeDtypeStruct((t, D), bf16)] * 3 + [jax.ShapeDtypeStruct((t, ng * LANES), f32)],
        scratch_shapes=[pltpu.VMEM((nq, npair, LANES, TQ), f32), pltpu.VMEM((npair, TQ, LANES), f32),
                        pltpu.VMEM((npair, TQ, LANES), f32), pltpu.VMEM((hg, TQ, LANES), f32),
                        pltpu.VMEM((hg, LANES, TQ), bf16)],
        compiler_params=_params(("parallel", "parallel", "arbitrary"), VMEM_LIMIT),
    )(ka, qkv, qkv, qkv, doa, lse, delta)


def _forget_bwd(dc, f128, seq):
    t = f128.shape[0]
    nb = seq // LANES
    groups = dc.shape[1] // LANES

    def body(dc_ref, f_ref, df_ref, dbf_ref):
        @pl.when(pl.program_id(0) == 0)
        def _():
            dbf_ref[...] = jnp.zeros_like(dbf_ref)

        r = lax.broadcasted_iota(jnp.int32, (LANES, LANES), 0)
        cidx = lax.broadcasted_iota(jnp.int32, (LANES, LANES), 1)
        tri = (r <= cidx).astype(f32)
        carry = jnp.zeros((1, LANES), f32)
        total = jnp.zeros((1, LANES), f32)
        for blk in reversed(range(nb)):
            dcb = dc_ref[pl.ds(blk * LANES, LANES), pl.ds(0, LANES)]
            for gi in range(1, groups):
                dcb = dcb + pltpu.roll(dc_ref[pl.ds(blk * LANES, LANES), pl.ds(gi * LANES, LANES)], gi * ATT_GROUP, 1)
            dlf = jnp.dot(tri, dcb, preferred_element_type=f32, precision=lax.Precision.HIGHEST) + carry
            df = dlf * _sig(-f_ref[pl.ds(blk * LANES, LANES), :])
            df_ref[pl.ds(blk * LANES, LANES), :] = df.astype(bf16)
            total = total + jnp.sum(df, axis=0, keepdims=True)
            carry = carry + jnp.sum(dcb, axis=0, keepdims=True)
        dbf_ref[...] += total

    return pl.pallas_call(
        body, name="forget_bwd", grid=(t // seq,),
        in_specs=[pl.BlockSpec((seq, groups * LANES), lambda b: (b, 0)), pl.BlockSpec((seq, LANES), lambda b: (b, 0))],
        out_specs=[pl.BlockSpec((seq, LANES), lambda b: (b, 0)), _whole((1, LANES))],
        out_shape=[jax.ShapeDtypeStruct((t, LANES), bf16), jax.ShapeDtypeStruct((1, LANES), f32)],
        compiler_params=_params(("arbitrary",)),
    )(dc, f128)


def _in_bwd(dz, df, x, dy, w_all, w_pre):
    t = x.shape[0]
    n_dz = len(dz)

    def body(*refs):
        dz_refs = refs[:n_dz]
        df_ref, x_ref, dy_ref, w_ref, wp_ref, gx_ref, dwp_ref = refs[n_dz:]

        @pl.when(pl.program_id(0) == 0)
        def _():
            dwp_ref[...] = jnp.zeros_like(dwp_ref)

        dh = _dot(df_ref[...], w_ref[pl.ds(n_dz * D, LANES), :])
        for p in range(n_dz):
            dh = dh + _dot(dz_refs[p][...], w_ref[pl.ds(p * D, D), :])
        xv = x_ref[...]
        r1 = lax.rsqrt(jnp.mean(xv * xv, axis=-1, keepdims=True) + NORM_EPS)
        xh = xv * r1
        dwp_ref[...] += jnp.sum(dh * xh, axis=0, keepdims=True)
        dxh = dh * wp_ref[...]
        gx_ref[...] = dy_ref[...] + r1 * (dxh - xh * jnp.mean(dxh * xh, axis=-1, keepdims=True))

    once = lambda shape: pl.BlockSpec(shape, lambda i: (0, 0), pipeline_mode=pl.Buffered(1))
    return pl.pallas_call(
        body, name="in_bwd", grid=(t // TM,),
        in_specs=[_tile(TM, D)] * n_dz + [_tile(TM, LANES), _tile(TM, D), _tile(TM, D), once(w_all.shape),
                  _whole((1, D))],
        out_specs=[_tile(TM, D), _whole((1, D))],
        out_shape=[jax.ShapeDtypeStruct((t, D), f32), jax.ShapeDtypeStruct((1, D), f32)],
        compiler_params=_params(("arbitrary",), VMEM_LIMIT),
    )(*dz, df, x, dy, w_all, w_pre)


def _tn_mm(name, a, b, tn, out_dtype=f32, tk=2048):
    t, k = a.shape
    tk = min(tk, t)
    n = b.shape[1]
    nk = t // tk

    def body(a_ref, b_ref, o_ref, s_ref, acc_ref):
        j, kk = pl.program_id(0), pl.program_id(1)

        @pl.when(kk == 0)
        def _():
            acc_ref[...] = jnp.zeros_like(acc_ref)

        @pl.when((j == 0) & (kk == 0))
        def _():
            s_ref[...] = jnp.zeros_like(s_ref)

        av = a_ref[...]
        acc_ref[...] += _dot_tn(av, b_ref[...])

        @pl.when(j == 0)
        def _():
            s_ref[...] += jnp.sum(av.astype(f32), axis=0, keepdims=True)

        @pl.when(kk == nk - 1)
        def _():
            o_ref[...] = acc_ref[...].astype(out_dtype)

    return pl.pallas_call(
        body, name=name, grid=(n // tn, nk),
        in_specs=[pl.BlockSpec((tk, k), lambda j, kk: (kk, 0)), pl.BlockSpec((tk, tn), lambda j, kk: (kk, j))],
        out_specs=[pl.BlockSpec((k, tn), lambda j, kk: (0, j)), _whole((1, k))],
        out_shape=[jax.ShapeDtypeStruct((k, n), out_dtype), jax.ShapeDtypeStruct((1, k), f32)],
        scratch_shapes=[pltpu.VMEM((k, tn), f32)],
        compiler_params=_params(("arbitrary", "arbitrary"), VMEM_LIMIT),
    )(a, b)


def _position():
    return lax.axis_index("x"), lax.axis_index("y"), lax.axis_index("c")


ROW_BLOCK = 128


def _pick_rows(layout, first, count):
    acc = jnp.zeros((ROW_BLOCK, D), f32)
    seg_start = 0
    for ref, ref_row, rows in layout:
        lo, hi = max(first, seg_start), min(first + count, seg_start + rows)
        if lo < hi and ref is not None:
            off, take, done = ref_row + lo - seg_start, hi - lo, lo - first
            start = off // 16 * 16
            win = -(-(off - start + take) // 16) * 16
            r = lax.broadcasted_iota(jnp.int32, (ROW_BLOCK, win), 0)
            col = lax.broadcasted_iota(jnp.int32, (ROW_BLOCK, win), 1)
            pick = ((col - r == off - start - done) & (r >= done) & (r < done + take)).astype(bf16)
            acc = acc + _dot(pick, ref[pl.ds(start, win), :])
        seg_start += rows
    return acc


def _assemble_rows(shards_ref, shard_rows, segments, out_ref):
    layout = [(shards_ref.at[j], 0, shard_rows) for j in range(shards_ref.shape[0])]
    for out0, log0, count in segments:
        for b0 in range(0, count, ROW_BLOCK):
            block = _pick_rows(layout, log0 + b0, min(ROW_BLOCK, count - b0))
            out_ref[pl.ds(out0 + b0, ROW_BLOCK), :] = block.astype(bf16)


def _pack_pieces(blocks, shard_rows, padded):
    arrays = [a for a, _ in blocks if a is not None]
    piece_rows = padded // 2

    def body(*refs):
        out_ref = refs[-1]
        it = iter(refs[:-1])
        layout = [(None if a is None else next(it), 0, rows) for a, rows in blocks]
        for k in range(N_DEV):
            chip, half = divmod(k, 2)
            for b0 in range(0, piece_rows, ROW_BLOCK):
                n = min(ROW_BLOCK, piece_rows - b0)
                in_shard = half * piece_rows + b0
                count = max(0, min(n, shard_rows - in_shard))
                block = _pick_rows(layout, chip * shard_rows + in_shard, count)
                out_ref[k, pl.ds(b0, n), :] = block[:n].astype(bf16)

    vm = pl.BlockSpec(memory_space=pltpu.VMEM)
    return pl.pallas_call(
        body, name="pack_pieces", in_specs=[vm] * len(arrays), out_specs=vm,
        out_shape=jax.ShapeDtypeStruct((N_DEV, piece_rows, D), bf16),
        compiler_params=pltpu.CompilerParams(vmem_limit_bytes=VMEM_LIMIT),
    )(*arrays)


def _gather_shards(parts, small, shard_rows, segments, out_rows):
    n = len(parts)
    halves = [p.shape[0] // 2 for p in parts]
    cuts = [-(-h // 32) * 16 for h in halves]
    n_direct, n_relay, n_sib = 4 * n, 2 * n, 6 * n

    def body(*refs):
        srcs, small_src = refs[:n], refs[n]
        dsts, small_dst, whole_ref = refs[n + 1:2 * n + 1], refs[2 * n + 1], refs[2 * n + 2]
        send, recv, local = refs[2 * n + 3:]
        x, y, c = _position()
        me = 2 * x + y
        chips = [(1 - x, y), (x, 1 - y), (1 - x, 1 - y)]
        ids = [2 * px + py for px, py in chips]

        def rows(a, half, quarter):
            start = half * halves[a] + (cuts[a] if quarter else 0)
            return pl.ds(start, halves[a] - cuts[a] if quarter else cuts[a])

        def landing(a, shard, half, quarter):
            return dsts[a].at[shard, rows(a, half, quarter), :]

        def direct(a, nb, quarter, shard):
            k = (a * 2 + nb) * 2 + quarter
            px, py = chips[nb]
            return pltpu.make_async_remote_copy(
                src_ref=srcs[a].at[rows(a, c, quarter), :], dst_ref=landing(a, shard, c, quarter),
                send_sem=send.at[k], recv_sem=recv.at[k], device_id=(px, py, c), device_id_type=MESH)

        def relay(a, quarter, shard):
            k = n_direct + a * 2 + quarter
            px, py = chips[1 - quarter]
            return pltpu.make_async_remote_copy(
                src_ref=landing(a, shard, c, quarter), dst_ref=landing(a, shard, c, quarter),
                send_sem=send.at[k], recv_sem=recv.at[k], device_id=(px, py, c), device_id_type=MESH)

        def to_sibling(a, origin, quarter, half):
            k = n_direct + n_relay + (a * 3 + origin) * 2 + quarter
            return pltpu.make_async_remote_copy(
                src_ref=landing(a, ids[origin], half, quarter), dst_ref=landing(a, ids[origin], half, quarter),
                send_sem=send.at[k], recv_sem=recv.at[k], device_id=(x, y, 1 - c), device_id_type=MESH)

        def small_copy(j, shard):
            k = n_direct + n_relay + n_sib + j
            px, py = chips[j]
            return pltpu.make_async_remote_copy(
                src_ref=small_src, dst_ref=small_dst.at[shard], send_sem=send.at[k], recv_sem=recv.at[k],
                device_id=(px, py, c), device_id_type=MESH)

        own = [pltpu.make_async_copy(srcs[a], dsts[a].at[me], local.at[a]) for a in range(n)]
        own.append(pltpu.make_async_copy(small_src, small_dst.at[me], local.at[n]))
        for cp in own:
            cp.start()
        sent = [direct(a, nb, q, me) for q in range(2) for a in range(n) for nb in range(2)]
        sent += [small_copy(j, me) for j in range(3)]
        for cp in sent:
            cp.start()

        def passed_on(cp):
            cp.start()
            sent.append(cp)

        for q in range(2):
            for a in range(n):
                for nb in range(2):
                    direct(a, nb, q, ids[nb]).wait_recv()
                    passed_on(to_sibling(a, nb, q, c))
                    if nb == q:
                        passed_on(relay(a, q, ids[nb]))
        for a in range(n):
            for q in range(2):
                relay(a, q, ids[2]).wait_recv()
                passed_on(to_sibling(a, 2, q, c))
        for j in range(3):
            small_copy(j, ids[j]).wait_recv()
            for a in range(n):
                for q in range(2):
                    to_sibling(a, j, q, 1 - c).wait_recv()
        for cp in sent:
            cp.wait_send()
        for cp in own:
            cp.wait()
        _assemble_rows(dsts[0], shard_rows, segments, whole_ref)

    vm = pl.BlockSpec(memory_space=pltpu.VMEM)
    n_sems = n_direct + n_relay + n_sib + 3
    out = pl.pallas_call(
        body, name="gather_shards",
        in_specs=[vm] * (n + 1), out_specs=[vm] * (n + 2),
        out_shape=[jax.ShapeDtypeStruct((N_CHIPS,) + p.shape, p.dtype) for p in parts + [small]]
        + [jax.ShapeDtypeStruct((out_rows, parts[0].shape[1]), parts[0].dtype)],
        scratch_shapes=[pltpu.SemaphoreType.DMA((n_sems,)), pltpu.SemaphoreType.DMA((n_sems,)),
                        pltpu.SemaphoreType.DMA((n + 1,))],
        compiler_params=pltpu.CompilerParams(vmem_limit_bytes=VMEM_LIMIT),
    )(*parts, small)
    return out[1:]


def _allsum_rows(part):
    rows_n = part.shape[0]

    def body(x_ref, gath_ref, sum_ref, send_sems, recv_sems, local_sem):
        x, y, c = _position()
        me, sibling = (x, y, c), (x, y, 1 - c)
        chips = [(1 - x, y), (x, 1 - y), (1 - x, 1 - y)]

        def rows(px, py, pc):
            return gath_ref.at[pl.ds((4 * px + 2 * py + pc) * rows_n, rows_n), :]

        def copy(k, block, to, src=None):
            return pltpu.make_async_remote_copy(
                src_ref=rows(*block) if src is None else src, dst_ref=rows(*block),
                send_sem=send_sems.at[k], recv_sem=recv_sems.at[k], device_id=to, device_id_type=MESH)

        mine = pltpu.make_async_copy(x_ref, rows(*me), local_sem)
        mine.start()
        first = [copy(0, me, sibling, src=x_ref)]
        first += [copy(1 + j, me, (*chip, c), src=x_ref) for j, chip in enumerate(chips)]
        for cp in first:
            cp.start()
        passed = [copy(4 + j, (*chip, c), sibling) for j, chip in enumerate(chips)]
        for j, chip in enumerate(chips):
            copy(1 + j, (*chip, c), me).wait_recv()
            passed[j].start()
        copy(0, sibling, me).wait_recv()
        for j, chip in enumerate(chips):
            copy(4 + j, (*chip, 1 - c), me).wait_recv()
        for cp in first + passed:
            cp.wait_send()
        mine.wait()
        total = gath_ref[pl.ds(0, rows_n), :]
        for d in range(1, N_DEV):
            total = total + gath_ref[pl.ds(d * rows_n, rows_n), :]
        sum_ref[...] = total

    vm = pl.BlockSpec(memory_space=pltpu.VMEM)
    return pl.pallas_call(
        body, name="allsum_rows", in_specs=[vm], out_specs=[vm, vm],
        out_shape=[jax.ShapeDtypeStruct((N_DEV * rows_n, D), f32), jax.ShapeDtypeStruct((rows_n, D), f32)],
        scratch_shapes=[pltpu.SemaphoreType.DMA((7,)), pltpu.SemaphoreType.DMA((7,)), pltpu.SemaphoreType.DMA],
    )(part)[1]


PAIR_ROWS = 16


def _pair_reduce(name, pieces):
    _, r, n = pieces.shape

    def body(p_ref, o_ref, land, send, recv):
        x, y, c = _position()

        def remote(j, half):
            return pltpu.make_async_remote_copy(
                src_ref=p_ref.at[2 * j + half], dst_ref=land.at[j], send_sem=send.at[j], recv_sem=recv.at[j],
                device_id=(x, y, 1 - c), device_id_type=MESH)

        sends = [remote(j, 1 - c) for j in range(N_CHIPS)]
        for cp in sends:
            cp.start()
        for j in range(N_CHIPS):
            remote(j, c).wait_recv()

            def add_rows(i, carry, j=j):
                rows = pl.ds(pl.multiple_of(i * PAIR_ROWS, PAIR_ROWS), PAIR_ROWS)
                o_ref[j, rows, :] = (p_ref[2 * j + c, rows, :].astype(f32) + land[j, rows, :].astype(f32)).astype(bf16)
                return carry

            lax.fori_loop(0, r // PAIR_ROWS, add_rows, 0)
        for cp in sends:
            cp.wait_send()

    vm = pl.BlockSpec(memory_space=pltpu.VMEM)
    return pl.pallas_call(
        body, name=name, in_specs=[vm], out_specs=vm,
        out_shape=jax.ShapeDtypeStruct((N_CHIPS, r, n), bf16),
        scratch_shapes=[pltpu.VMEM((N_CHIPS, r, n), bf16), pltpu.SemaphoreType.DMA((N_CHIPS,)),
                        pltpu.SemaphoreType.DMA((N_CHIPS,))],
        compiler_params=pltpu.CompilerParams(vmem_limit_bytes=VMEM_LIMIT),
    )(pieces)


def _chip_exchange(arrs):
    n = len(arrs)
    heights = [a.shape[1] for a in arrs]
    cuts = [-(-r // 32) * 16 for r in heights]

    def body(*refs):
        srcs, dsts, relays = refs[:n], refs[n:2 * n], refs[2 * n:3 * n]
        send, recv, local = refs[3 * n:]
        x, y, c = _position()
        me = 2 * x + y
        chips = [(1 - x, y), (x, 1 - y), (1 - x, 1 - y)]
        ids = [2 * px + py for px, py in chips]

        def rows(a, quarter):
            return pl.ds(cuts[a], heights[a] - cuts[a]) if quarter else pl.ds(0, cuts[a])

        def held(a, quarter):
            size = heights[a] - cuts[a] if quarter else cuts[a]
            return relays[a].at[quarter, pl.ds(0, size), :]

        def direct(a, nb, piece, landing):
            px, py = chips[nb]
            return pltpu.make_async_remote_copy(
                src_ref=srcs[a].at[piece], dst_ref=dsts[a].at[landing], send_sem=send.at[a * 2 + nb],
                recv_sem=recv.at[a * 2 + nb], device_id=(px, py, c), device_id_type=MESH)

        def first_hop(a, quarter):
            k = 2 * n + a * 2 + quarter
            px, py = chips[quarter]
            return pltpu.make_async_remote_copy(
                src_ref=srcs[a].at[ids[2], rows(a, quarter), :], dst_ref=held(a, quarter), send_sem=send.at[k],
                recv_sem=recv.at[k], device_id=(px, py, c), device_id_type=MESH)

        def second_hop(a, quarter, origin):
            k = 4 * n + a * 2 + quarter
            px, py = chips[1 - quarter]
            return pltpu.make_async_remote_copy(
                src_ref=held(a, quarter), dst_ref=dsts[a].at[origin, rows(a, quarter), :], send_sem=send.at[k],
                recv_sem=recv.at[k], device_id=(px, py, c), device_id_type=MESH)

        own = [pltpu.make_async_copy(srcs[a].at[me], dsts[a].at[me], local.at[a]) for a in range(n)]
        sent = [first_hop(a, q) for a in range(n) for q in range(2)]
        sent += [direct(a, nb, ids[nb], me) for a in range(n) for nb in range(2)]
        for cp in sent + own:
            cp.start()
        for a in range(n):
            for q in range(2):
                first_hop(a, q).wait_recv()
                sent.append(second_hop(a, q, ids[q]))
                sent[-1].start()
        for a in range(n):
            for nb in range(2):
                direct(a, nb, me, ids[nb]).wait_recv()
            for q in range(2):
                second_hop(a, q, ids[2]).wait_recv()
        for cp in sent:
            cp.wait_send()
        for cp in own:
            cp.wait()

    anyspec = pl.BlockSpec(memory_space=pl.ANY)
    out = pl.pallas_call(
        body, name="chip_exchange", in_specs=[anyspec] * n, out_specs=[anyspec] * (2 * n),
        out_shape=[jax.ShapeDtypeStruct(a.shape, a.dtype) for a in arrs]
        + [jax.ShapeDtypeStruct((2, cut, a.shape[2]), a.dtype) for a, cut in zip(arrs, cuts)],
        scratch_shapes=[pltpu.SemaphoreType.DMA((6 * n,)), pltpu.SemaphoreType.DMA((6 * n,)),
                        pltpu.SemaphoreType.DMA((n,))],
    )(*arrs)
    return out[:n]


def _sum_swap_halves(slots):
    n = len(slots)

    def body(*refs):
        srcs, dsts, halves = refs[:n], refs[n:2 * n], refs[2 * n:3 * n]
        send, recv, local = refs[3 * n:]
        x, y, c = _position()

        def remote(a, landing):
            return pltpu.make_async_remote_copy(
                src_ref=halves[a], dst_ref=dsts[a].at[landing], send_sem=send.at[a], recv_sem=recv.at[a],
                device_id=(x, y, 1 - c), device_id_type=MESH)

        for a in range(n):
            def add_rows(i, carry, a=a):
                rows = pl.ds(pl.multiple_of(i * PAIR_ROWS, PAIR_ROWS), PAIR_ROWS)
                total = srcs[a][0, rows, :].astype(f32)
                for s in range(1, N_CHIPS):
                    total = total + srcs[a][s, rows, :].astype(f32)
                halves[a][rows, :] = total
                return carry

            lax.fori_loop(0, srcs[a].shape[1] // PAIR_ROWS, add_rows, 0)
        own = [pltpu.make_async_copy(halves[a], dsts[a].at[c], local.at[a]) for a in range(n)]
        sends = [remote(a, c) for a in range(n)]
        for cp in sends + own:
            cp.start()
        for a in range(n):
            remote(a, 1 - c).wait_recv()
        for cp in sends:
            cp.wait_send()
        for cp in own:
            cp.wait()

    vm = pl.BlockSpec(memory_space=pltpu.VMEM)
    return pl.pallas_call(
        body, name="sum_swap_halves", in_specs=[vm] * n, out_specs=[vm] * n,
        out_shape=[jax.ShapeDtypeStruct((2,) + a.shape[1:], f32) for a in slots],
        scratch_shapes=[pltpu.VMEM(a.shape[1:], f32) for a in slots]
        + [pltpu.SemaphoreType.DMA((n,)), pltpu.SemaphoreType.DMA((n,)), pltpu.SemaphoreType.DMA((n,))],
        compiler_params=pltpu.CompilerParams(vmem_limit_bytes=VMEM_LIMIT),
    )(*slots)


def _row_block(r):
    return 128 if r % 128 == 0 else r


def _adamw(name, w, g, m, v):
    r, n = w.shape
    if r % 128 == 0 or r * n <= 128 * 1024:
        rb, nb = _row_block(r), n
    else:
        rb, nb = r, LANES

    def body(w_ref, g_ref, m_ref, v_ref, d_ref, nm_ref, nv_ref):
        gv = g_ref[...]
        m2 = ADAM_B1 * m_ref[...] + (1.0 - ADAM_B1) * gv
        v2 = ADAM_B2 * v_ref[...] + (1.0 - ADAM_B2) * (gv * gv)
        m_hat = m2 / (1.0 - ADAM_B1 ** ADAM_STEP)
        v_hat = v2 / (1.0 - ADAM_B2 ** ADAM_STEP)
        d_ref[...] = (-ADAM_LR) * (m_hat / (jnp.sqrt(v_hat) + ADAM_EPS) + ADAM_WD * w_ref[...])
        nm_ref[...] = m2
        nv_ref[...] = v2

    spec = pl.BlockSpec((rb, nb), lambda i, j: (i, j))
    return pl.pallas_call(
        body, name=name, grid=(r // rb, n // nb), in_specs=[spec] * 4, out_specs=[spec] * 3,
        out_shape=[jax.ShapeDtypeStruct((r, n), f32)] * 3,
        compiler_params=_params(("parallel", "parallel"), VMEM_LIMIT),
    )(w, g, m, v)


def _local_step(x2, tgt2, seq, wt):
    nb = x2.shape[0] // seq
    h, qkv, f128 = _norm_qkv(x2, wt["pre_w"], wt["w_all"], wt["b_qkv"], 8 * D, wt["b_f"])
    rest = _mm("in_rest", h, wt["w_all"], (3 * D, 5 * D), wt["b_rest"], bf16, 1024, 1024)
    c = _forget_prep(f128, seq)
    ka = _attn_prep(qkv, c)
    o_att, pa, lse = _attn_fwd(ka, qkv, rest, seq)
    rnn_w = (wt["conv_w"], wt["conv_b"], wt["wa_d"], wt["wx_d"], wt["ba"], wt["bx"], wt["lam"])
    xc, a, hrec, pr = _rnn_fwd(rest, *rnn_w, seq)
    (do, dya, dyr, dmga, dmgr, doa, dga, dhrec, dgr, mrg, dy, delta, loss8, d_post) = _merge_loss(
        rest, pa, pr, o_att, hrec, wt["w_a"], wt["w_r"], wt["w_o"], x2, tgt2, wt["post_w"])
    d_wo, _ = _tn_mm("dw_out", mrg, do, D)
    d_wa, _ = _tn_mm("dw_branch_a", pa, dya, D)
    d_wr, _ = _tn_mm("dw_branch_r", pr, dyr, D)
    dxr, d_wad, d_wxd, vec = _rnn_bwd(dhrec, a, hrec, xc, rest, *rnn_w, seq)
    dq, dk, dv, dc = _attn_bwd(ka, qkv, doa, lse, delta, seq)
    df, db_f = _forget_bwd(dc, f128, seq)
    pieces = [dq, dk, dv, dga, dxr, dgr, dmga, dmgr]
    gx, d_pre = _in_bwd(pieces, df, x2, dy, wt["w_all"], wt["pre_w"])
    names = ["q", "k", "v", "ga", "xr", "gr", "mga", "mgr"]
    dws, dbs = [], []
    for nm, piece in zip(names, pieces):
        dw_p, db_p = _tn_mm("dw_in_" + nm, piece, h, D, bf16)
        dws.append((dw_p, D))
        dbs.append(db_p)
    dw_f, _ = _tn_mm("dw_in_f", df, h, D, bf16)
    shard_rows = IN_TOTAL // N_CHIPS
    w_in_pieces = _pack_pieces(dws[:3] + [(dw_f, HEADS)] + dws[3:] + [(None, IN_TOTAL - IN_USED)], shard_rows,
                               _padded_rows(shard_rows))
    d_b_in = jnp.concatenate(dbs[:3] + [db_f[:, :HEADS]] + dbs[3:] + [jnp.zeros((1, IN_TOTAL - IN_USED), f32)], axis=1)
    return dict(loss=loss8[0, 0], grad_x=gx, pre_w=d_pre, w_in_pieces=w_in_pieces, b_in=d_b_in, conv_w=vec[4:8],
                conv_b=vec[3:4],
                wa_d=d_wad, ba=vec[0:1], wx_d=d_wxd, bx=vec[1:2], lam=vec[2:3], w_a=d_wa, w_r=d_wr, w_o=d_wo,
                post_w=d_post)


def _block_diag(w):
    g, bw, _ = w.shape
    eye = jnp.eye(g, dtype=w.dtype)
    return (w[:, :, None, :] * eye[:, None, :, None]).reshape(g * bw, g * bw)


def _gate_blocks(diag):
    half = diag.shape[1] // 2
    return jnp.stack([diag[:, :half, :half], diag[:, half:, half:]], axis=1).reshape(-1, half, half)


def _padded_rows(rows):
    return -(-rows // 32) * 32


def _pad_cols(a, n):
    return jnp.pad(a, ((0, 0), (0, n - a.shape[1])))


def _pad_rows(a, n):
    return jnp.pad(a, ((0, n - a.shape[0]), (0, 0)))


def kernel(x, pre_norm_w, w_in, b_in, conv_w, conv_b, rg_wa, rg_ba, rg_wx, rg_bx, rg_lambda, w_branch_a, w_branch_r, w_out, post_norm_w, loss_target, m_pre_norm_w, m_w_in, m_b_in, m_conv_w, m_conv_b, m_rg_wa, m_rg_ba, m_rg_wx, m_rg_bx, m_rg_lambda, m_w_branch_a, m_w_branch_r, m_w_out, m_post_norm_w, v_pre_norm_w, v_w_in, v_b_in, v_conv_w, v_conv_b, v_rg_wa, v_rg_ba, v_rg_wx, v_rg_bx, v_rg_lambda, v_w_branch_a, v_w_branch_r, v_w_out, v_post_norm_w):
    nb, seq, _ = x.shape
    chip = 2 * lax.axis_index("x") + lax.axis_index("y")
    n_groups = rg_wa.shape[1]

    w_in_t = jnp.transpose(w_in[0])
    shard_cols = w_in_t.shape[0]
    padded = _padded_rows(shard_cols)
    q_end, f_end = 3 * D, 3 * D + HEADS
    segments = [(0, 0, q_end), (q_end, f_end, IN_USED - f_end), (IN_USED - HEADS, q_end, HEADS)]
    g_a, g_r, g_o, g_cw, w_all = _gather_shards(
        [_pad_rows(w_in_t.astype(bf16), padded), w_branch_a[0].astype(bf16), w_branch_r[0].astype(bf16),
         w_out[0].astype(bf16)], conv_w[0], shard_cols, segments, IN_USED - HEADS + LANES)
    wt = dict(
        pre_w=pre_norm_w, post_w=post_norm_w,
        w_all=w_all, b_qkv=b_in[:, :q_end], b_f=_pad_cols(b_in[:, q_end:f_end], LANES), b_rest=b_in[:, f_end:IN_USED],
        w_a=g_a.reshape(D, D), w_r=g_r.reshape(D, D), w_o=g_o.reshape(D, D),
        conv_w=jnp.transpose(g_cw, (1, 0, 2)).reshape(4, D), conv_b=conv_b,
        wa_d=_block_diag(rg_wa[0]).astype(bf16), wx_d=_block_diag(rg_wx[0]).astype(bf16),
        ba=rg_ba, bx=rg_bx, lam=rg_lambda)

    part = _local_step(x.reshape(nb * seq, D), loss_target.reshape(nb * seq, D), seq, wt)
    loss = lax.psum(part["loss"], ("x", "y", "c"))
    grad_x = part["grad_x"].reshape(nb, seq, D)

    small = jnp.concatenate([
        part["pre_w"], _pad_cols(part["b_in"], 10 * D).reshape(10, D), part["conv_b"],
        _gate_blocks(part["wa_d"]).reshape(-1, D), part["ba"],
        _gate_blocks(part["wx_d"]).reshape(-1, D), part["bx"], part["lam"], part["post_w"],
        part["conv_w"]], axis=0)
    n_small = small.shape[0]
    n_rep = n_small - 4
    tot = _allsum_rows(_pad_rows(small, -(-n_small // 8) * 8))
    g_rep = tot[:n_rep]
    g_conv_w = lax.dynamic_slice_in_dim(tot[n_rep:n_small], chip * (D // N_CHIPS), D // N_CHIPS, axis=1)

    def unpack(p):
        o = [0]

        def take(k):
            o[0] += k
            return p[o[0] - k:o[0]]

        pre = take(1)
        b = take(10).reshape(1, 10 * D)[:, :IN_TOTAL]
        cb = take(1)
        wa = take(64).reshape(rg_wa.shape)
        ba = take(1)
        wx = take(64).reshape(rg_wx.shape)
        bx = take(1)
        lam = take(1)
        post = take(1)
        return dict(pre_norm_w=pre, b_in=b, conv_b=cb, rg_wa=wa, rg_ba=ba, rg_wx=wx, rg_bx=bx, rg_lambda=lam,
                    post_norm_w=post)

    grads = unpack(g_rep)
    replicated = dict(
        pre_norm_w=(pre_norm_w, m_pre_norm_w, v_pre_norm_w), b_in=(b_in, m_b_in, v_b_in),
        conv_b=(conv_b, m_conv_b, v_conv_b), rg_wa=(rg_wa, m_rg_wa, v_rg_wa), rg_ba=(rg_ba, m_rg_ba, v_rg_ba),
        rg_wx=(rg_wx, m_rg_wx, v_rg_wx), rg_bx=(rg_bx, m_rg_bx, v_rg_bx),
        rg_lambda=(rg_lambda, m_rg_lambda, v_rg_lambda), post_norm_w=(post_norm_w, m_post_norm_w, v_post_norm_w))
    deltas, new_m, new_v = {}, {}, {}
    for name, (w, m, v) in replicated.items():
        as2d = lambda a: a.reshape(-1, D) if a.ndim > 2 else a
        upd = _adamw("adamw_" + name, as2d(w), as2d(grads[name]), as2d(m), as2d(v))
        deltas[name], new_m[name], new_v[name] = [a.reshape(w.shape) for a in upd]

    p_aro = jnp.concatenate([part[k].reshape(N_DEV, D // N_DEV, D) for k in ("w_a", "w_r", "w_o")], axis=1)
    s_in, s_aro = _chip_exchange([_pair_reduce("pair_w_in", part["w_in_pieces"]),
                                  _pair_reduce("pair_w_aro", p_aro.astype(bf16))])
    f_in, f_aro = _sum_swap_halves([s_in, s_aro])
    g_w_in_t = f_in.reshape(padded, D)[:shard_cols]
    rows = D // N_DEV
    g_aro = [f_aro[:, i * rows:(i + 1) * rows, :].reshape(2 * rows, D) for i in range(3)]

    w_in_upd = _adamw("adamw_w_in", w_in_t, g_w_in_t, jnp.transpose(m_w_in[0]), jnp.transpose(v_w_in[0]))
    g_w_in, d_w_in, nm_w_in, nv_w_in = [jnp.transpose(a) for a in (g_w_in_t, *w_in_upd)]
    upd_a = _adamw("adamw_w_branch_a", w_branch_a[0], g_aro[0], m_w_branch_a[0], v_w_branch_a[0])
    upd_r = _adamw("adamw_w_branch_r", w_branch_r[0], g_aro[1], m_w_branch_r[0], v_w_branch_r[0])
    upd_o = _adamw("adamw_w_out", w_out[0], g_aro[2], m_w_out[0], v_w_out[0])
    d_aro, nm_aro, nv_aro = zip(upd_a, upd_r, upd_o)
    d_cw, nm_cw, nv_cw = _adamw("adamw_conv_w", conv_w[0], g_conv_w, m_conv_w[0], v_conv_w[0])

    def sharded(t_in, t_aro, t_cw):
        return dict(w_in=t_in[None], conv_w=t_cw[None], w_branch_a=t_aro[0][None], w_branch_r=t_aro[1][None],
                    w_out=t_aro[2][None])

    order = ["pre_norm_w", "w_in", "b_in", "conv_w", "conv_b", "rg_wa", "rg_ba", "rg_wx", "rg_bx", "rg_lambda",
             "w_branch_a", "w_branch_r", "w_out", "post_norm_w"]
    outs = [loss, grad_x]
    for rep, shd in ((grads, sharded(g_w_in, g_aro, g_conv_w)), (deltas, sharded(d_w_in, d_aro, d_cw)),
                     (new_m, sharded(nm_w_in, nm_aro, nm_cw)), (new_v, sharded(nv_w_in, nv_aro, nv_cw))):
        both = {**rep, **shd}
        outs.extend(both[k] for k in order)
    return tuple(outs)
```

```python
import jax
import jax.numpy as jnp
from jax import lax
from jax.experimental import pallas as pl
from jax.experimental.pallas import tpu as pltpu

f32 = jnp.float32
bf16 = jnp.bfloat16

D = 1024
HEADS = 16
LANES = 128
NORM_EPS = 1e-6
MASK_VALUE = -1e30
RG_C = 8.0
QK_SCALE = 0.125
TQ = 256
ATT_GROUP = 8
ATT_GROUP_FWD = 16
TL = 512
TM = 512
PREV_ROWS = 16
IN_USED = 8 * D + HEADS
IN_TOTAL = 9 * D + HEADS
N_CHIPS = 4
N_DEV = 8
ADAM_LR, ADAM_B1, ADAM_B2, ADAM_EPS, ADAM_WD, ADAM_STEP = 0.001, 0.9, 0.999, 1e-08, 0.01, 10
VMEM_LIMIT = 56 * 1024 * 1024
MESH = pl.DeviceIdType.MESH


def _dot(a, b):
    return jnp.dot(a, b, preferred_element_type=f32)


def _dot_nt(a, b):
    return lax.dot_general(a, b, (((1,), (1,)), ((), ())), preferred_element_type=f32)


def _dot_tn(a, b):
    return lax.dot_general(a, b, (((0,), (0,)), ((), ())), preferred_element_type=f32)


def _sig(x):
    return 0.5 * jnp.tanh(0.5 * x) + 0.5


def _softplus(x):
    return jnp.maximum(x, 0.0) + jnp.log(1.0 + jnp.exp(-jnp.abs(x)))


def _params(sem, vmem=None):
    return pltpu.CompilerParams(dimension_semantics=sem, vmem_limit_bytes=vmem)


def _tile(tm, width, cb=0):
    return pl.BlockSpec((tm, width), lambda i, cb=cb: (i, cb))


def _whole(shape):
    nd = len(shape)
    return pl.BlockSpec(shape, lambda *_: (0,) * nd)


def _norm_qkv(x, w_pre, w_all, b_qkv, f_row0, b_f, tm=1024):
    t = x.shape[0]
    tm = min(tm, t)
    n = b_qkv.shape[1]

    def body(x_ref, wp_ref, w_ref, b_ref, wf_ref, bf_ref, h_ref, o_ref, f_ref):
        @pl.when(pl.program_id(1) == 0)
        def _():
            xv = x_ref[...]
            r = lax.rsqrt(jnp.mean(xv * xv, axis=-1, keepdims=True) + NORM_EPS)
            h = (xv * r * wp_ref[...]).astype(bf16)
            h_ref[...] = h
            f_ref[...] = _dot_nt(h, wf_ref[...]) + bf_ref[...]

        o_ref[...] = (_dot_nt(h_ref[...], w_ref[...]) + b_ref[...]).astype(bf16)

    return pl.pallas_call(
        body, name="norm_qkv", grid=(t // tm, n // D),
        in_specs=[pl.BlockSpec((tm, D), lambda i, j: (i, 0)), _whole((1, D)), pl.BlockSpec((D, D), lambda i, j: (j, 0)),
                  pl.BlockSpec((1, D), lambda i, j: (0, j)),
                  pl.BlockSpec((LANES, D), lambda i, j: (f_row0 // LANES, 0)), _whole((1, LANES))],
        out_specs=[pl.BlockSpec((tm, D), lambda i, j: (i, 0)), pl.BlockSpec((tm, D), lambda i, j: (i, j)),
                   pl.BlockSpec((tm, LANES), lambda i, j: (i, 0))],
        out_shape=[jax.ShapeDtypeStruct((t, D), bf16), jax.ShapeDtypeStruct((t, n), bf16),
                   jax.ShapeDtypeStruct((t, LANES), f32)],
        compiler_params=_params(("parallel", "arbitrary"), VMEM_LIMIT),
    )(x, w_pre, w_all, b_qkv, w_all, b_f)


def _mm(name, a, w, w_rows, bias, out_dtype, tm, tn):
    t, k = a.shape
    tm = min(tm, t)
    row0, n = w_rows
    assert row0 % tn == 0

    def body(a_ref, w_ref, b_ref, o_ref):
        o_ref[...] = (_dot_nt(a_ref[...], w_ref[...]) + b_ref[...]).astype(out_dtype)

    return pl.pallas_call(
        body, name=name, grid=(t // tm, n // tn),
        in_specs=[pl.BlockSpec((tm, k), lambda i, j: (i, 0)), pl.BlockSpec((tn, k), lambda i, j: (row0 // tn + j, 0)),
                  pl.BlockSpec((1, tn), lambda i, j: (0, j))],
        out_specs=pl.BlockSpec((tm, tn), lambda i, j: (i, j)), out_shape=jax.ShapeDtypeStruct((t, n), out_dtype),
        compiler_params=_params(("parallel", "parallel"), VMEM_LIMIT),
    )(a, w, bias)


def _forget_prep(f128, seq):
    t = f128.shape[0]
    nb = seq // LANES

    def body(f_ref, c_ref):
        r = lax.broadcasted_iota(jnp.int32, (LANES, LANES), 0)
        cidx = lax.broadcasted_iota(jnp.int32, (LANES, LANES), 1)
        tri = (r >= cidx).astype(f32)
        carry = jnp.zeros((1, LANES), f32)
        for blk in range(nb):
            fv = f_ref[pl.ds(blk * LANES, LANES), :]
            lf = -_softplus(-fv)
            c_ref[pl.ds(blk * LANES, LANES), :] = (
                jnp.dot(tri, lf, preferred_element_type=f32, precision=lax.Precision.HIGHEST) + carry)
            carry = carry + jnp.sum(lf, axis=0, keepdims=True)

    return pl.pallas_call(
        body, name="forget_prep", grid=(t // seq,),
        in_specs=[pl.BlockSpec((seq, LANES), lambda b: (b, 0))],
        out_specs=pl.BlockSpec((seq, LANES), lambda b: (b, 0)),
        out_shape=jax.ShapeDtypeStruct((t, LANES), f32),
        compiler_params=_params(("parallel",)),
    )(f128)


def _split3(cv):
    hi = cv.astype(bf16)
    r1 = cv - hi.astype(f32)
    mid = r1.astype(bf16)
    lo = (r1 - mid.astype(f32)).astype(bf16)
    return hi, mid, lo


def _q_operand(q2, half):
    lane = lax.broadcasted_iota(jnp.int32, (1, LANES), 1)
    first = 64 * (1 - half)
    ones = jnp.where((lane >= first) & (lane < first + 3), jnp.ones((), bf16), jnp.zeros((), bf16))
    return jnp.where((lane // 64) == half, q2 * jnp.asarray(QK_SCALE, bf16), ones)


def _attn_prep(qkv, c):
    t = qkv.shape[0]

    def body(k_ref, c_ref, ka_ref):
        lane = lax.broadcasted_iota(jnp.int32, (1, LANES), 1)
        cv = c_ref[...]
        zero = jnp.zeros((), bf16)
        for head in range(HEADS):
            half, first = head % 2, 64 * (1 - head % 2)
            ch = jnp.sum(jnp.where(lane == head, cv, 0.0), axis=1, keepdims=True)
            hi, mid, lo = _split3(-ch)
            pieces = jnp.where(lane == first, hi, jnp.where(lane == first + 1, mid, jnp.where(lane == first + 2, lo, zero)))
            ka_ref[:, pl.ds(head * LANES, LANES)] = jnp.where(
                (lane // 64) == half, k_ref[:, pl.ds((head // 2) * LANES, LANES)], pieces)

    tm = min(TM, t)
    return pl.pallas_call(
        body, name="attn_prep", grid=(t // tm,),
        in_specs=[_tile(tm, D, 1), _tile(tm, LANES)],
        out_specs=pl.BlockSpec((tm, 2 * D), lambda i: (i, 0)),
        out_shape=jax.ShapeDtypeStruct((t, 2 * D), bf16),
        compiler_params=_params(("parallel",)),
    )(qkv, c)


def _attn_fwd(ka, qkv, rest, seq):
    t = qkv.shape[0]
    nb, nq = t // seq, seq // TQ

    hg = ATT_GROUP_FWD
    ng = HEADS // hg

    def body(q_ref, k_ref, v_ref, ga_ref, o_ref, pa_ref, lse_ref, acc_scr, qop_scr):
        qi, gi = pl.program_id(1), pl.program_id(2)
        krow = lax.broadcasted_iota(jnp.int32, (TQ, TQ), 0)
        qcol = lax.broadcasted_iota(jnp.int32, (TQ, TQ), 1)
        acc_scr[...] = jnp.zeros_like(acc_scr)
        for g in range(hg):
            qop_scr[g] = _q_operand(q_ref[:, pl.ds((g // 2) * LANES, LANES)], g % 2)

        def kv_step(kt, carry, masked):
            ks = pl.multiple_of(kt * TQ, TQ)
            sts = [_dot_nt(k_ref[pl.ds(ks, TQ), pl.ds(g * LANES, LANES)], qop_scr[g]) for g in range(hg)]
            if masked:
                sts = [jnp.where(krow <= qcol, st, MASK_VALUE) for st in sts]
            m_new = [jnp.maximum(carry[g][0], jnp.max(sts[g], axis=0, keepdims=True)) for g in range(hg)]
            ps = [jnp.exp(sts[g] - m_new[g]) for g in range(hg)]
            alphas = [jnp.exp(carry[g][0] - m_new[g]) for g in range(hg)]
            phi = [ps[g].astype(bf16) for g in range(hg)]
            plo = [(ps[g] - phi[g].astype(f32)).astype(bf16) for g in range(hg)]
            vs = [v_ref[pl.ds(ks, TQ), pl.ds(j * LANES, LANES)] for j in range(hg // 2)]
            pvs = [_dot_tn(vs[g // 2], phi[g]) + _dot_tn(vs[g // 2], plo[g]) for g in range(hg)]
            olds = [acc_scr[g] for g in range(hg)]
            for g in range(hg):
                acc_scr[g] = alphas[g] * olds[g] + pvs[g]
            return tuple((m_new[g], alphas[g] * carry[g][1] + jnp.sum(ps[g], axis=0, keepdims=True))
                         for g in range(hg))

        init = tuple((jnp.full((1, TQ), MASK_VALUE, f32), jnp.zeros((1, TQ), f32)) for _ in range(hg))
        carry = lax.fori_loop(0, qi, lambda kt, cr: kv_step(kt, cr, False), init)
        stats = kv_step(qi, carry, True)
        drow = lax.broadcasted_iota(jnp.int32, (LANES, TQ), 0)
        for g in range(hg):
            m, l = stats[g]
            lse_ref[0, pl.ds(hg * gi + g, 1), :] = m + jnp.log(l)
        for j in range(hg // 2):
            o2 = jnp.where(drow < 64, acc_scr[2 * j] / stats[2 * j][1], acc_scr[2 * j + 1] / stats[2 * j + 1][1]).T
            o_ref[:, pl.ds(j * LANES, LANES)] = o2
            ga = ga_ref[:, pl.ds(j * LANES, LANES)].astype(f32)
            pa_ref[:, pl.ds(j * LANES, LANES)] = (o2 * (ga * _sig(ga))).astype(bf16)

    vw = hg * 64
    tile = pl.BlockSpec((TQ, vw), lambda b, qi, gi: (b * nq + qi, gi))
    return pl.pallas_call(
        body, name="attn_fwd", grid=(nb, nq, ng),
        in_specs=[tile, pl.BlockSpec((seq, hg * LANES), lambda b, qi, gi: (b, gi)),
                  pl.BlockSpec((seq, vw), lambda b, qi, gi: (b, 2 * ng + gi)), tile],
        out_specs=[tile, tile, pl.BlockSpec((1, HEADS, TQ), lambda b, qi, gi: (b * nq + qi, 0, 0))],
        out_shape=[jax.ShapeDtypeStruct((t, D), f32), jax.ShapeDtypeStruct((t, D), bf16),
                   jax.ShapeDtypeStruct((t // TQ, HEADS, TQ), f32)],
        scratch_shapes=[pltpu.VMEM((hg, LANES, TQ), f32), pltpu.VMEM((hg, TQ, LANES), bf16)],
        compiler_params=_params(("parallel", "parallel", "arbitrary"), VMEM_LIMIT),
    )(qkv, ka, qkv, rest)


def _shifted_rows(x, top8, prev8, shift, row, row8):
    body = pltpu.roll(x, shift, 0)
    head = jnp.where(row8 < shift, pltpu.roll(prev8, shift, 0), pltpu.roll(top8, shift, 0))
    return body, head


def _rnn_gates(xc, wa_ref, wx_ref, ba_ref, bx_ref, lam_ref):
    xcb = xc.astype(bf16)
    r = _sig(_dot(xcb, wa_ref[...]) + ba_ref[...])
    i = _sig(_dot(xcb, wx_ref[...]) + bx_ref[...])
    sp = _softplus(-lam_ref[...])
    log_a = (-RG_C) * r * sp
    th = jnp.tanh(log_a)
    w1 = (-2.0) * th / (1.0 - th)
    sq = jnp.sqrt(jnp.maximum(w1, 0.0))
    return r, i, sp, log_a, w1, sq


def _conv_tile(x_ref, xprev_ref, has_prev, cw_ref, cb_ref, xc_ref):
    row = lax.broadcasted_iota(jnp.int32, (TL, D), 0)
    row8 = lax.broadcasted_iota(jnp.int32, (8, D), 0)
    x = x_ref[...].astype(f32)
    top8 = x[:8]
    prev8 = jnp.where(has_prev, xprev_ref[...].astype(f32)[PREV_ROWS - 8:], 0.0)
    xc = cb_ref[...] + cw_ref[pl.ds(3, 1), :] * x
    xc8 = cb_ref[...] + cw_ref[pl.ds(3, 1), :] * top8
    for sh in range(1, 4):
        w = cw_ref[pl.ds(3 - sh, 1), :]
        xs, xs8 = _shifted_rows(x, top8, prev8, sh, row, row8)
        xc = xc + w * xs
        xc8 = xc8 + w * xs8
    xc_ref[...] = xc
    xc_ref[pl.ds(0, 8), :] = xc8


def _rnn_fwd(rest, conv_w, conv_b, wa_d, wx_d, ba, bx, lam, seq):
    t = rest.shape[0]
    nb, nt = t // seq, seq // TL

    def body(x_ref, xprev_ref, gr_ref, cw_ref, cb_ref, wa_ref, wx_ref, ba_ref, bx_ref, lam_ref,
             xc_ref, a_ref, h_ref, pr_ref, xc_scr, u_scr, h_scr, carry):
        tt = pl.program_id(1)
        _conv_tile(x_ref, xprev_ref, tt > 0, cw_ref, cb_ref, xc_scr)
        xc = xc_scr[...]
        xc_ref[...] = xc.astype(bf16)
        r, i, sp, log_a, w1, sq = _rnn_gates(xc, wa_ref, wx_ref, ba_ref, bx_ref, lam_ref)
        a_ref[...] = jnp.exp(log_a)
        u_scr[...] = sq * (i * xc)

        @pl.when(tt == 0)
        def _():
            carry[...] = jnp.zeros_like(carry)

        def step(s, h):
            h = a_ref[pl.ds(s, 1), :] * h + u_scr[pl.ds(s, 1), :]
            h_scr[pl.ds(s, 1), :] = h
            return h

        carry[...] = lax.fori_loop(0, TL, step, carry[...], unroll=8)
        gr = gr_ref[...].astype(f32)
        h = h_scr[...]
        h_ref[...] = h.astype(bf16)
        pr_ref[...] = (h * (gr * _sig(gr))).astype(bf16)

    tile = lambda cb: pl.BlockSpec((TL, D), lambda b, tt, cb=cb: (b * nt + tt, cb))
    prev = lambda cb: pl.BlockSpec(
        (PREV_ROWS, D), lambda b, tt, cb=cb: (jnp.maximum((b * nt + tt) * (TL // PREV_ROWS) - 1, 0), cb))
    vec = _whole((1, D))
    return pl.pallas_call(
        body, name="rnn_fwd", grid=(nb, nt),
        in_specs=[tile(1), prev(1), tile(2), _whole((4, D)), vec, _whole((D, D)), _whole((D, D)), vec, vec, vec],
        out_specs=[tile(0)] * 4,
        out_shape=[jax.ShapeDtypeStruct((t, D), dt) for dt in (bf16, f32, bf16, bf16)],
        scratch_shapes=[pltpu.VMEM((TL, D), f32)] * 3 + [pltpu.VMEM((1, D), f32)],
        compiler_params=_params(("parallel", "arbitrary"), VMEM_LIMIT),
    )(rest, rest, rest, conv_w, conv_b, wa_d, wx_d, ba, bx, lam)


def _merge_loss(rest, pa, pr, o_att, hrec, w_a, w_r, w_out, x, tgt, w_post):
    t = x.shape[0]

    def branch(dy, w_ref, g_ref, act):
        dp = _dot_nt(dy, w_ref[...])
        g = g_ref[...].astype(f32)
        sg = _sig(g)
        return (dp * (g * sg)).astype(bf16), (dp * act * (sg * (1.0 + g * (1.0 - sg)))).astype(bf16)

    def body(mga_ref, mgr_ref, pa_ref, pr_ref, ga_ref, gr_ref, oa_ref, h_ref, x_ref, t_ref, wa_ref, wr_ref, wo_ref,
             w_ref, do_ref, dya_ref, dyr_ref, dmga_ref, dmgr_ref, doa_ref, dga_ref, dh_ref, dgr_ref, mrg_ref, dy_ref,
             delta_ref, loss_ref, dwp_ref):
        @pl.when(pl.program_id(0) == 0)
        def _():
            loss_ref[...] = jnp.zeros_like(loss_ref)
            dwp_ref[...] = jnp.zeros_like(dwp_ref)

        sa, sr = _sig(mga_ref[...].astype(f32)), _sig(mgr_ref[...].astype(f32))
        ya, yr = _dot(pa_ref[...], wa_ref[...]), _dot(pr_ref[...], wr_ref[...])
        mrg = (sa * ya + sr * yr).astype(bf16)
        mrg_ref[...] = mrg
        ov = _dot(mrg, wo_ref[...])
        w = w_ref[...]
        r2 = lax.rsqrt(jnp.mean(ov * ov, axis=-1, keepdims=True) + NORM_EPS)
        oh = ov * r2
        e = x_ref[...] + oh * w - t_ref[...]
        loss_ref[...] += 0.5 * jnp.sum(jnp.mean(e * e, axis=-1, keepdims=True))
        dy = e * (1.0 / D)
        dy_ref[...] = dy
        dwp_ref[...] += jnp.sum(dy * oh, axis=0, keepdims=True)
        doh = dy * w
        do = (r2 * (doh - oh * jnp.mean(doh * oh, axis=-1, keepdims=True))).astype(bf16)
        do_ref[...] = do

        dm = _dot_nt(do, wo_ref[...])
        dya, dyr = (dm * sa).astype(bf16), (dm * sr).astype(bf16)
        dya_ref[...] = dya
        dyr_ref[...] = dyr
        dmga_ref[...] = (dm * ya * sa * (1.0 - sa)).astype(bf16)
        dmgr_ref[...] = (dm * yr * sr * (1.0 - sr)).astype(bf16)
        o_att = oa_ref[...]
        doa, dga_ref[...] = branch(dya, wa_ref, ga_ref, o_att)
        doa_ref[...] = doa
        dh_ref[...], dgr_ref[...] = branch(dyr, wr_ref, gr_ref, h_ref[...].astype(f32))
        ch = lax.broadcasted_iota(jnp.int32, (D, LANES), 0)
        hd = lax.broadcasted_iota(jnp.int32, (D, LANES), 1)
        pick = (ch // 64 == hd).astype(bf16)
        per_head = sum(_dot(piece, pick) for piece in _split3(doa.astype(f32) * o_att))
        delta_ref[0] = per_head.T[:HEADS, :]

    once = pl.BlockSpec((D, D), lambda i: (0, 0), pipeline_mode=pl.Buffered(1))
    rows = _tile(TQ, D)
    return pl.pallas_call(
        body, name="merge_loss", grid=(t // TQ,),
        in_specs=[_tile(TQ, D, 3), _tile(TQ, D, 4), rows, rows, _tile(TQ, D, 0), _tile(TQ, D, 2), rows, rows, rows, rows,
                  once, once, once, _whole((1, D))],
        out_specs=[rows] * 11 + [pl.BlockSpec((1, HEADS, TQ), lambda i: (i, 0, 0)), _whole((8, LANES)), _whole((1, D))],
        out_shape=[jax.ShapeDtypeStruct((t, D), bf16)] * 10 + [jax.ShapeDtypeStruct((t, D), f32),
                   jax.ShapeDtypeStruct((t // TQ, HEADS, TQ), f32), jax.ShapeDtypeStruct((8, LANES), f32),
                   jax.ShapeDtypeStruct((1, D), f32)],
        compiler_params=_params(("arbitrary",), VMEM_LIMIT),
    )(rest, rest, pa, pr, rest, rest, o_att, hrec, x, tgt, w_a, w_r, w_out, w_post)


def _rnn_bwd(dh, a, h, xc, rest, conv_w, conv_b, wa_d, wx_d, ba, bx, lam, seq):
    t = dh.shape[0]
    nb, nt = t // seq, seq // TL
    diag = (D // LANES, LANES, LANES)

    def body(dh_ref, a_ref, h_ref, hprev_ref, xc_ref, x_ref, cw_ref, cb_ref, wa_ref, wx_ref,
             ba_ref, bx_ref, lam_ref, dxr_ref, dwa_ref, dwx_ref, vec_ref, g_scr, dxc_scr, dxr_scr, qcarry, dxc_next):
        b, tt = pl.program_id(0), pl.program_id(1)
        rt = nt - 1 - tt

        @pl.when((b == 0) & (tt == 0))
        def _():
            dwa_ref[...] = jnp.zeros_like(dwa_ref)
            dwx_ref[...] = jnp.zeros_like(dwx_ref)
            vec_ref[...] = jnp.zeros_like(vec_ref)

        @pl.when(tt == 0)
        def _():
            qcarry[...] = jnp.zeros_like(qcarry)
            dxc_next[...] = jnp.zeros_like(dxc_next)

        g_scr[...] = dh_ref[...].astype(f32)

        def step(k, q):
            s = TL - 1 - k
            g = g_scr[pl.ds(s, 1), :] + q
            g_scr[pl.ds(s, 1), :] = g
            return a_ref[pl.ds(s, 1), :] * g

        qcarry[...] = lax.fori_loop(0, TL, step, qcarry[...], unroll=8)

        row = lax.broadcasted_iota(jnp.int32, (TL, D), 0)
        row8 = lax.broadcasted_iota(jnp.int32, (8, D), 0)
        g = g_scr[...]
        av = a_ref[...]
        xc = xc_ref[...].astype(f32)
        hlast = jnp.where(rt > 0, hprev_ref[...].astype(f32)[PREV_ROWS - 1:], 0.0)
        hp = jnp.where(row == 0, hlast, pltpu.roll(h_ref[...].astype(f32), 1, 0))
        r, i, sp, log_a, w1, sq = _rnn_gates(xc, wa_ref, wx_ref, ba_ref, bx_ref, lam_ref)
        dix = g * sq
        di = dix * xc
        dxc = dix * i
        dsq = g * (i * xc)
        dlog_a = g * hp * av - dsq * jnp.where(sq > 0.0, (1.0 - w1) / sq, 0.0)
        dpr = (dlog_a * ((-RG_C) * sp)) * r * (1.0 - r)
        dpi = di * i * (1.0 - i)
        dprb, dpib, xcb = dpr.astype(bf16), dpi.astype(bf16), xc.astype(bf16)
        dxc = dxc + _dot_nt(dprb, wa_ref[...]) + _dot_nt(dpib, wx_ref[...])
        for j in range(D // LANES):
            cols = slice(j * LANES, (j + 1) * LANES)
            dwa_ref[j] += _dot_tn(xcb[:, cols], dprb[:, cols])
            dwx_ref[j] += _dot_tn(xcb[:, cols], dpib[:, cols])
        vec_ref[pl.ds(0, 1), :] += jnp.sum(dpr, axis=0, keepdims=True)
        vec_ref[pl.ds(1, 1), :] += jnp.sum(dpi, axis=0, keepdims=True)
        dsp = jnp.sum(dlog_a * ((-RG_C) * r), axis=0, keepdims=True)
        vec_ref[pl.ds(2, 1), :] += dsp * (-_sig(-lam_ref[...]))
        vec_ref[pl.ds(3, 1), :] += jnp.sum(dxc, axis=0, keepdims=True)

        dxc_scr[...] = dxc
        bot8 = dxc_scr[pl.ds(TL - 8, 8), :]
        nxt8 = dxc_next[...]
        x = x_ref[...].astype(f32)
        x_bot8 = x[TL - 8:]
        dxr = cw_ref[pl.ds(3, 1), :] * dxc
        dxr8 = cw_ref[pl.ds(3, 1), :] * bot8
        vec_ref[pl.ds(7, 1), :] += jnp.sum(dxc * x, axis=0, keepdims=True)
        for sh in range(1, 4):
            w = cw_ref[pl.ds(3 - sh, 1), :]
            up = pltpu.roll(dxc, TL - sh, 0)
            from_next = pltpu.roll(nxt8, 8 - sh, 0)
            dxr = dxr + w * up
            dxr8 = dxr8 + w * jnp.where(row8 < 8 - sh, pltpu.roll(bot8, 8 - sh, 0), from_next)
            inside = jnp.sum(jnp.where(row < TL - sh, up, 0.0) * x, axis=0, keepdims=True)
            across = jnp.sum(jnp.where(row8 >= 8 - sh, from_next, 0.0) * x_bot8, axis=0, keepdims=True)
            vec_ref[pl.ds(7 - sh, 1), :] += inside + across
        dxr_scr[...] = dxr
        dxr_scr[pl.ds(TL - 8, 8), :] = dxr8
        dxr_ref[...] = dxr_scr[...].astype(bf16)
        dxc_next[...] = dxc_scr[pl.ds(0, 8), :]

    tile = lambda cb: pl.BlockSpec((TL, D), lambda b, tt, cb=cb: (b * nt + nt - 1 - tt, cb))
    prev = lambda cb: pl.BlockSpec(
        (PREV_ROWS, D), lambda b, tt, cb=cb: (jnp.maximum((b * nt + nt - 1 - tt) * (TL // PREV_ROWS) - 1, 0), cb))
    vec = _whole((1, D))
    return pl.pallas_call(
        body, name="rnn_bwd", grid=(nb, nt),
        in_specs=[tile(0), tile(0), tile(0), prev(0), tile(0), tile(1),
                  _whole((4, D)), vec, _whole((D, D)), _whole((D, D)), vec, vec, vec],
        out_specs=[tile(0), _whole(diag), _whole(diag), _whole((8, D))],
        out_shape=[jax.ShapeDtypeStruct((t, D), bf16), jax.ShapeDtypeStruct(diag, f32),
                   jax.ShapeDtypeStruct(diag, f32), jax.ShapeDtypeStruct((8, D), f32)],
        scratch_shapes=[pltpu.VMEM((TL, D), f32), pltpu.VMEM((TL, D), f32), pltpu.VMEM((TL, D), f32),
                        pltpu.VMEM((1, D), f32), pltpu.VMEM((8, D), f32)],
        compiler_params=_params(("arbitrary", "arbitrary"), VMEM_LIMIT),
    )(dh, a, h, h, xc, rest, conv_w, conv_b, wa_d, wx_d, ba, bx, lam)


def _attn_bwd(ka, qkv, doa, lse, delta, seq):
    t = qkv.shape[0]
    nb, nq = t // seq, seq // TQ
    hg = ATT_GROUP
    ng, npair = HEADS // hg, hg // 2

    def body(ka_ref, q_ref, k_ref, v_ref, do_ref, lse_ref, dl_ref, dq_ref, dk_ref, dv_ref, dc_ref,
             dqt_scr, dk_scr, dv_scr, ds_scr, kht_scr):
        gi, kt = pl.program_id(1), pl.program_id(2)
        lane = lax.broadcasted_iota(jnp.int32, (1, LANES), 1)
        krow = lax.broadcasted_iota(jnp.int32, (TQ, TQ), 0)
        qcol = lax.broadcasted_iota(jnp.int32, (TQ, TQ), 1)
        lmask = [(lane // 64) == hh for hh in range(2)]
        scale = jnp.asarray(QK_SCALE, bf16)

        @pl.when(kt == 0)
        def _():
            dqt_scr[...] = jnp.zeros_like(dqt_scr)

        dk_scr[...] = jnp.zeros_like(dk_scr)
        dv_scr[...] = jnp.zeros_like(dv_scr)
        ds_scr[...] = jnp.zeros_like(ds_scr)
        for g in range(hg):
            k2 = k_ref[:, pl.ds((g // 2) * LANES, LANES)]
            kht_scr[g] = jnp.where(lmask[g % 2], k2, jnp.zeros_like(k2)).T

        def q_step(qt, masked):
            qs = pl.multiple_of(qt * TQ, TQ)
            heads = range(hg)
            do2 = [do_ref[pl.ds(qs, TQ), pl.ds(j * LANES, LANES)] for j in range(npair)]
            q2 = [q_ref[pl.ds(qs, TQ), pl.ds(j * LANES, LANES)] for j in range(npair)]
            doh = [jnp.where(lmask[g % 2], do2[g // 2], jnp.zeros_like(do2[0])) for g in heads]
            qh = [jnp.where(lmask[g % 2], q2[g // 2], jnp.zeros_like(q2[0])) * scale for g in heads]
            st = [_dot_nt(ka_ref[:, pl.ds(g * LANES, LANES)], _q_operand(q2[g // 2], g % 2)) for g in heads]
            if masked:
                st = [jnp.where(krow <= qcol, s, MASK_VALUE) for s in st]
            dp = [_dot_nt(v_ref[:, pl.ds((g // 2) * LANES, LANES)], doh[g]) for g in heads]
            p = [jnp.exp(st[g] - lse_ref[qt, pl.ds(hg * gi + g, 1), :]) for g in heads]
            ds = [p[g] * (dp[g] - dl_ref[qt, pl.ds(hg * gi + g, 1), :]) for g in heads]
            pb = [x.astype(bf16) for x in p]
            dsb = [x.astype(bf16) for x in ds]
            for j in range(npair):
                a, b = 2 * j, 2 * j + 1
                dv_scr[j] += _dot(pb[a], doh[a]) + _dot(pb[b], doh[b])
                dk_scr[j] += _dot(dsb[a], qh[a]) + _dot(dsb[b], qh[b])
                dqt_scr[qt, j] += (_dot(kht_scr[a], dsb[a]) + _dot(kht_scr[b], dsb[b])) * QK_SCALE
            for g in heads:
                ds_scr[g] += ds[g][:, :LANES] + ds[g][:, LANES:]

        q_step(kt, True)

        def loop_body(qt, carry):
            q_step(qt, False)
            return carry

        lax.fori_loop(kt + 1, nq, loop_body, 0)

        dc = jnp.zeros((TQ, LANES), f32)
        for g in range(hg):
            dc = jnp.where(lane == g, -jnp.sum(ds_scr[g], axis=1, keepdims=True), dc)
        dc_ref[...] = dc
        for j in range(npair):
            dk_ref[:, pl.ds(j * LANES, LANES)] = dk_scr[j].astype(bf16)
            dv_ref[:, pl.ds(j * LANES, LANES)] = dv_scr[j].astype(bf16)

        @pl.when(kt == nq - 1)
        def _():
            for qt in range(nq):
                for j in range(npair):
                    dq_ref[pl.ds(qt * TQ, TQ), pl.ds(j * LANES, LANES)] = dqt_scr[qt, j].T.astype(bf16)

    vw = hg * 64
    seqspec = pl.BlockSpec((seq, vw), lambda b, gi, kt: (b, gi))
    kspec = lambda off: pl.BlockSpec((TQ, vw), lambda b, gi, kt: (b * nq + kt, off + gi))
    rowspec = pl.BlockSpec((nq, HEADS, TQ), lambda b, gi, kt: (b, 0, 0))
    return pl.pallas_call(
        body, name="attn_bwd", grid=(nb, ng, nq),
        in_specs=[pl.BlockSpec((TQ, hg * LANES), lambda b, gi, kt: (b * nq + kt, gi)), seqspec, kspec(ng), kspec(2 * ng), seqspec, rowspec, rowspec],
        out_specs=[seqspec, kspec(0), kspec(0), pl.BlockSpec((TQ, LANES), lambda b, gi, kt: (b * nq + kt, gi))],
        out_shape=[jax.ShapeDtypeStruct((t, D), bf16)] * 3 + [jax.ShapeDtypeStruct((t, ng * LANES), f32)],
        scratch_shapes=[pltpu.VMEM((nq, npair, LANES, TQ), f32), pltpu.VMEM((npair, TQ, LANES), f32),
                        pltpu.VMEM((npair, TQ, LANES), f32), pltpu.VMEM((hg, TQ, LANES), f32),
                        pltpu.VMEM((hg, LANES, TQ), bf16)],
        compiler_params=_params(("parallel", "parallel", "arbitrary"), VMEM_LIMIT),
    )(ka, qkv, qkv, qkv, doa, lse, delta)


def _forget_bwd(dc, f128, seq):
    t = f128.shape[0]
    nb = seq // LANES
    groups = dc.shape[1] // LANES

    def body(dc_ref, f_ref, df_ref, dbf_ref):
        @pl.when(pl.program_id(0) == 0)
        def _():
            dbf_ref[...] = jnp.zeros_like(dbf_ref)

        r = lax.broadcasted_iota(jnp.int32, (LANES, LANES), 0)
        cidx = lax.broadcasted_iota(jnp.int32, (LANES, LANES), 1)
        tri = (r <= cidx).astype(f32)
        carry = jnp.zeros((1, LANES), f32)
        total = jnp.zeros((1, LANES), f32)
        for blk in reversed(range(nb)):
            dcb = dc_ref[pl.ds(blk * LANES, LANES), pl.ds(0, LANES)]
            for gi in range(1, groups):
                dcb = dcb + pltpu.roll(dc_ref[pl.ds(blk * LANES, LANES), pl.ds(gi * LANES, LANES)], gi * ATT_GROUP, 1)
            dlf = jnp.dot(tri, dcb, preferred_element_type=f32, precision=lax.Precision.HIGHEST) + carry
            df = dlf * _sig(-f_ref[pl.ds(blk * LANES, LANES), :])
            df_ref[pl.ds(blk * LANES, LANES), :] = df.astype(bf16)
            total = total + jnp.sum(df, axis=0, keepdims=True)
            carry = carry + jnp.sum(dcb, axis=0, keepdims=True)
        dbf_ref[...] += total

    return pl.pallas_call(
        body, name="forget_bwd", grid=(t // seq,),
        in_specs=[pl.BlockSpec((seq, groups * LANES), lambda b: (b, 0)), pl.BlockSpec((seq, LANES), lambda b: (b, 0))],
        out_specs=[pl.BlockSpec((seq, LANES), lambda b: (b, 0)), _whole((1, LANES))],
        out_shape=[jax.ShapeDtypeStruct((t, LANES), bf16), jax.ShapeDtypeStruct((1, LANES), f32)],
        compiler_params=_params(("arbitrary",)),
    )(dc, f128)


def _in_bwd(dz, df, x, dy, w_all, w_pre):
    t = x.shape[0]
    n_dz = len(dz)

    def body(*refs):
        dz_refs = refs[:n_dz]
        df_ref, x_ref, dy_ref, w_ref, wp_ref, gx_ref, dwp_ref = refs[n_dz:]

        @pl.when(pl.program_id(0) == 0)
        def _():
            dwp_ref[...] = jnp.zeros_like(dwp_ref)

        dh = _dot(df_ref[...], w_ref[pl.ds(n_dz * D, LANES), :])
        for p in range(n_dz):
            dh = dh + _dot(dz_refs[p][...], w_ref[pl.ds(p * D, D), :])
        xv = x_ref[...]
        r1 = lax.rsqrt(jnp.mean(xv * xv, axis=-1, keepdims=True) + NORM_EPS)
        xh = xv * r1
        dwp_ref[...] += jnp.sum(dh * xh, axis=0, keepdims=True)
        dxh = dh * wp_ref[...]
        gx_ref[...] = dy_ref[...] + r1 * (dxh - xh * jnp.mean(dxh * xh, axis=-1, keepdims=True))

    once = lambda shape: pl.BlockSpec(shape, lambda i: (0, 0), pipeline_mode=pl.Buffered(1))
    return pl.pallas_call(
        body, name="in_bwd", grid=(t // TM,),
        in_specs=[_tile(TM, D)] * n_dz + [_tile(TM, LANES), _tile(TM, D), _tile(TM, D), once(w_all.shape),
                  _whole((1, D))],
        out_specs=[_tile(TM, D), _whole((1, D))],
        out_shape=[jax.ShapeDtypeStruct((t, D), f32), jax.ShapeDtypeStruct((1, D), f32)],
        compiler_params=_params(("arbitrary",), VMEM_LIMIT),
    )(*dz, df, x, dy, w_all, w_pre)


def _tn_mm(name, a, b, tn, out_dtype=f32, tk=2048):
    t, k = a.shape
    tk = min(tk, t)
    n = b.shape[1]
    nk = t // tk

    def body(a_ref, b_ref, o_ref, s_ref, acc_ref):
        j, kk = pl.program_id(0), pl.program_id(1)

        @pl.when(kk == 0)
        def _():
            acc_ref[...] = jnp.zeros_like(acc_ref)

        @pl.when((j == 0) & (kk == 0))
        def _():
            s_ref[...] = jnp.zeros_like(s_ref)

        av = a_ref[...]
        acc_ref[...] += _dot_tn(av, b_ref[...])

        @pl.when(j == 0)
        def _():
            s_ref[...] += jnp.sum(av.astype(f32), axis=0, keepdims=True)

        @pl.when(kk == nk - 1)
        def _():
            o_ref[...] = acc_ref[...].astype(out_dtype)

    return pl.pallas_call(
        body, name=name, grid=(n // tn, nk),
        in_specs=[pl.BlockSpec((tk, k), lambda j, kk: (kk, 0)), pl.BlockSpec((tk, tn), lambda j, kk: (kk, j))],
        out_specs=[pl.BlockSpec((k, tn), lambda j, kk: (0, j)), _whole((1, k))],
        out_shape=[jax.ShapeDtypeStruct((k, n), out_dtype), jax.ShapeDtypeStruct((1, k), f32)],
        scratch_shapes=[pltpu.VMEM((k, tn), f32)],
        compiler_params=_params(("arbitrary", "arbitrary"), VMEM_LIMIT),
    )(a, b)


def _tn_mm_stack(name, parts, b, out_dtype, tk=1024):
    t, k = parts[0].shape
    tk = min(tk, t)
    n = b.shape[1]
    nk, count = t // tk, len(parts)

    def body(*refs):
        a_refs, b_ref, o_ref, s_ref, acc_ref = refs[:count], refs[count], refs[count + 1], refs[count + 2], refs[-1]
        p, kk = pl.program_id(0), pl.program_id(1)

        @pl.when(kk == 0)
        def _():
            acc_ref[...] = jnp.zeros_like(acc_ref)
            s_ref[...] = jnp.zeros_like(s_ref)

        for i in range(count):
            @pl.when(p == i)
            def _(i=i):
                av = a_refs[i][...]
                acc_ref[...] += _dot_tn(av, b_ref[...])
                s_ref[0] += jnp.sum(av.astype(f32), axis=0, keepdims=True)

        @pl.when(kk == nk - 1)
        def _():
            o_ref[...] = acc_ref[...].astype(out_dtype)

    def part_spec(i):
        return pl.BlockSpec((tk, k), lambda p, kk: (jnp.where(p == i, kk, jnp.where(p < i, 0, nk - 1)), 0))

    return pl.pallas_call(
        body, name=name, grid=(count, nk),
        in_specs=[part_spec(i) for i in range(count)] + [pl.BlockSpec((tk, n), lambda p, kk: (kk, 0))],
        out_specs=[pl.BlockSpec((k, n), lambda p, kk: (p, 0)), pl.BlockSpec((1, 1, k), lambda p, kk: (p, 0, 0))],
        out_shape=[jax.ShapeDtypeStruct((count * k, n), out_dtype), jax.ShapeDtypeStruct((count, 1, k), f32)],
        scratch_shapes=[pltpu.VMEM((k, n), f32)],
        compiler_params=_params(("arbitrary", "arbitrary"), VMEM_LIMIT),
    )(*parts, b)


def _position():
    return lax.axis_index("x"), lax.axis_index("y"), lax.axis_index("c")


ROW_BLOCK = 128


def _pick_rows(layout, first, count):
    acc = jnp.zeros((ROW_BLOCK, D), f32)
    seg_start = 0
    for ref, ref_row, rows in layout:
        lo, hi = max(first, seg_start), min(first + count, seg_start + rows)
        if lo < hi and ref is not None:
            off, take, done = ref_row + lo - seg_start, hi - lo, lo - first
            start = off // 16 * 16
            win = -(-(off - start + take) // 16) * 16
            r = lax.broadcasted_iota(jnp.int32, (ROW_BLOCK, win), 0)
            col = lax.broadcasted_iota(jnp.int32, (ROW_BLOCK, win), 1)
            pick = ((col - r == off - start - done) & (r >= done) & (r < done + take)).astype(bf16)
            acc = acc + _dot(pick, ref[pl.ds(start, win), :])
        seg_start += rows
    return acc


def _assemble_rows(shards_ref, shard_rows, segments, out_ref):
    layout = [(shards_ref.at[j], 0, shard_rows) for j in range(shards_ref.shape[0])]
    for out0, log0, count in segments:
        for b0 in range(0, count, ROW_BLOCK):
            block = _pick_rows(layout, log0 + b0, min(ROW_BLOCK, count - b0))
            out_ref[pl.ds(out0 + b0, ROW_BLOCK), :] = block.astype(bf16)


def _pack_pieces(blocks, shard_rows, padded):
    arrays = []
    for a, _, _ in blocks:
        if a is not None and all(a is not seen for seen in arrays):
            arrays.append(a)
    piece_rows = padded // 2

    def body(*refs):
        out_ref = refs[-1]
        ref_of = lambda a: next(r for r, seen in zip(refs, arrays) if seen is a)
        layout = [(None if a is None else ref_of(a), row, rows) for a, row, rows in blocks]
        for k in range(N_DEV):
            chip, half = divmod(k, 2)
            for b0 in range(0, piece_rows, ROW_BLOCK):
                n = min(ROW_BLOCK, piece_rows - b0)
                in_shard = half * piece_rows + b0
                count = max(0, min(n, shard_rows - in_shard))
                block = _pick_rows(layout, chip * shard_rows + in_shard, count)
                out_ref[k, pl.ds(b0, n), :] = block[:n].astype(bf16)

    vm = pl.BlockSpec(memory_space=pltpu.VMEM)
    return pl.pallas_call(
        body, name="pack_pieces", in_specs=[vm] * len(arrays), out_specs=vm,
        out_shape=jax.ShapeDtypeStruct((N_DEV, piece_rows, D), bf16),
        compiler_params=pltpu.CompilerParams(vmem_limit_bytes=VMEM_LIMIT),
    )(*arrays)


def _gather_shards(parts, small, shard_rows, segments, out_rows):
    n = len(parts)
    halves = [p.shape[0] // 2 for p in parts]
    cuts = [-(-h // 32) * 16 for h in halves]
    n_direct, n_relay, n_sib = 4 * n, 2 * n, 6 * n

    def body(*refs):
        srcs, small_src = refs[:n], refs[n]
        dsts, small_dst, whole_ref = refs[n + 1:2 * n + 1], refs[2 * n + 1], refs[2 * n + 2]
        send, recv, local = refs[2 * n + 3:]
        x, y, c = _position()
        me = 2 * x + y
        chips = [(1 - x, y), (x, 1 - y), (1 - x, 1 - y)]
        ids = [2 * px + py for px, py in chips]

        def rows(a, half, quarter):
            start = half * halves[a] + (cuts[a] if quarter else 0)
            return pl.ds(start, halves[a] - cuts[a] if quarter else cuts[a])

        def landing(a, shard, half, quarter):
            return dsts[a].at[shard, rows(a, half, quarter), :]

        def direct(a, nb, quarter, shard):
            k = (a * 2 + nb) * 2 + quarter
            px, py = chips[nb]
            return pltpu.make_async_remote_copy(
                src_ref=srcs[a].at[rows(a, c, quarter), :], dst_ref=landing(a, shard, c, quarter),
                send_sem=send.at[k], recv_sem=recv.at[k], device_id=(px, py, c), device_id_type=MESH)

        def relay(a, quarter, shard):
            k = n_direct + a * 2 + quarter
            px, py = chips[1 - quarter]
            return pltpu.make_async_remote_copy(
                src_ref=landing(a, shard, c, quarter), dst_ref=landing(a, shard, c, quarter),
                send_sem=send.at[k], recv_sem=recv.at[k], device_id=(px, py, c), device_id_type=MESH)

        def to_sibling(a, origin, quarter, half):
            k = n_direct + n_relay + (a * 3 + origin) * 2 + quarter
            return pltpu.make_async_remote_copy(
                src_ref=landing(a, ids[origin], half, quarter), dst_ref=landing(a, ids[origin], half, quarter),
                send_sem=send.at[k], recv_sem=recv.at[k], device_id=(x, y, 1 - c), device_id_type=MESH)

        def small_copy(j, shard):
            k = n_direct + n_relay + n_sib + j
            px, py = chips[j]
            return pltpu.make_async_remote_copy(
                src_ref=small_src, dst_ref=small_dst.at[shard], send_sem=send.at[k], recv_sem=recv.at[k],
                device_id=(px, py, c), device_id_type=MESH)

        own = [pltpu.make_async_copy(srcs[a], dsts[a].at[me], local.at[a]) for a in range(n)]
        own.append(pltpu.make_async_copy(small_src, small_dst.at[me], local.at[n]))
        for cp in own:
            cp.start()
        sent = [direct(a, nb, q, me) for q in range(2) for a in range(n) for nb in range(2)]
        sent += [small_copy(j, me) for j in range(3)]
        for cp in sent:
            cp.start()

        def passed_on(cp):
            cp.start()
            sent.append(cp)

        for q in range(2):
            for a in range(n):
                for nb in range(2):
                    direct(a, nb, q, ids[nb]).wait_recv()
                    passed_on(to_sibling(a, nb, q, c))
                    if nb == q:
                        passed_on(relay(a, q, ids[nb]))
        for a in range(n):
            for q in range(2):
                relay(a, q, ids[2]).wait_recv()
                passed_on(to_sibling(a, 2, q, c))
        for j in range(3):
            small_copy(j, ids[j]).wait_recv()
            for a in range(n):
                for q in range(2):
                    to_sibling(a, j, q, 1 - c).wait_recv()
        for cp in sent:
            cp.wait_send()
        for cp in own:
            cp.wait()
        _assemble_rows(dsts[0], shard_rows, segments, whole_ref)

    vm = pl.BlockSpec(memory_space=pltpu.VMEM)
    n_sems = n_direct + n_relay + n_sib + 3
    out = pl.pallas_call(
        body, name="gather_shards",
        in_specs=[vm] * (n + 1), out_specs=[vm] * (n + 2),
        out_shape=[jax.ShapeDtypeStruct((N_CHIPS,) + p.shape, p.dtype) for p in parts + [small]]
        + [jax.ShapeDtypeStruct((out_rows, parts[0].shape[1]), parts[0].dtype)],
        scratch_shapes=[pltpu.SemaphoreType.DMA((n_sems,)), pltpu.SemaphoreType.DMA((n_sems,)),
                        pltpu.SemaphoreType.DMA((n + 1,))],
        compiler_params=pltpu.CompilerParams(vmem_limit_bytes=VMEM_LIMIT),
    )(*parts, small)
    return out[1:]


def _allsum_rows(part):
    rows_n = part.shape[0]

    def body(x_ref, gath_ref, sum_ref, send_sems, recv_sems, local_sem):
        x, y, c = _position()
        me, sibling = (x, y, c), (x, y, 1 - c)
        chips = [(1 - x, y), (x, 1 - y), (1 - x, 1 - y)]

        def rows(px, py, pc):
            return gath_ref.at[pl.ds((4 * px + 2 * py + pc) * rows_n, rows_n), :]

        def copy(k, block, to, src=None):
            return pltpu.make_async_remote_copy(
                src_ref=rows(*block) if src is None else src, dst_ref=rows(*block),
                send_sem=send_sems.at[k], recv_sem=recv_sems.at[k], device_id=to, device_id_type=MESH)

        mine = pltpu.make_async_copy(x_ref, rows(*me), local_sem)
        mine.start()
        first = [copy(0, me, sibling, src=x_ref)]
        first += [copy(1 + j, me, (*chip, c), src=x_ref) for j, chip in enumerate(chips)]
        for cp in first:
            cp.start()
        passed = [copy(4 + j, (*chip, c), sibling) for j, chip in enumerate(chips)]
        for j, chip in enumerate(chips):
            copy(1 + j, (*chip, c), me).wait_recv()
            passed[j].start()
        copy(0, sibling, me).wait_recv()
        for j, chip in enumerate(chips):
            copy(4 + j, (*chip, 1 - c), me).wait_recv()
        for cp in first + passed:
            cp.wait_send()
        mine.wait()
        total = gath_ref[pl.ds(0, rows_n), :]
        for d in range(1, N_DEV):
            total = total + gath_ref[pl.ds(d * rows_n, rows_n), :]
        sum_ref[...] = total

    vm = pl.BlockSpec(memory_space=pltpu.VMEM)
    return pl.pallas_call(
        body, name="allsum_rows", in_specs=[vm], out_specs=[vm, vm],
        out_shape=[jax.ShapeDtypeStruct((N_DEV * rows_n, D), f32), jax.ShapeDtypeStruct((rows_n, D), f32)],
        scratch_shapes=[pltpu.SemaphoreType.DMA((7,)), pltpu.SemaphoreType.DMA((7,)), pltpu.SemaphoreType.DMA],
    )(part)[1]


PAIR_ROWS = 16


def _pair_reduce(name, pieces):
    _, r, n = pieces.shape

    def body(p_ref, o_ref, land, send, recv):
        x, y, c = _position()

        def remote(j, half):
            return pltpu.make_async_remote_copy(
                src_ref=p_ref.at[2 * j + half], dst_ref=land.at[j], send_sem=send.at[j], recv_sem=recv.at[j],
                device_id=(x, y, 1 - c), device_id_type=MESH)

        sends = [remote(j, 1 - c) for j in range(N_CHIPS)]
        for cp in sends:
            cp.start()
        for j in range(N_CHIPS):
            remote(j, c).wait_recv()

            def add_rows(i, carry, j=j):
                rows = pl.ds(pl.multiple_of(i * PAIR_ROWS, PAIR_ROWS), PAIR_ROWS)
                o_ref[j, rows, :] = (p_ref[2 * j + c, rows, :].astype(f32) + land[j, rows, :].astype(f32)).astype(bf16)
                return carry

            lax.fori_loop(0, r // PAIR_ROWS, add_rows, 0)
        for cp in sends:
            cp.wait_send()

    vm = pl.BlockSpec(memory_space=pltpu.VMEM)
    return pl.pallas_call(
        body, name=name, in_specs=[vm], out_specs=vm,
        out_shape=jax.ShapeDtypeStruct((N_CHIPS, r, n), bf16),
        scratch_shapes=[pltpu.VMEM((N_CHIPS, r, n), bf16), pltpu.SemaphoreType.DMA((N_CHIPS,)),
                        pltpu.SemaphoreType.DMA((N_CHIPS,))],
        compiler_params=pltpu.CompilerParams(vmem_limit_bytes=VMEM_LIMIT),
    )(pieces)


def _chip_exchange(arrs):
    n = len(arrs)
    heights = [a.shape[1] for a in arrs]
    cuts = [-(-r // 32) * 16 for r in heights]

    def body(*refs):
        srcs, dsts, relays = refs[:n], refs[n:2 * n], refs[2 * n:3 * n]
        send, recv, local = refs[3 * n:]
        x, y, c = _position()
        me = 2 * x + y
        chips = [(1 - x, y), (x, 1 - y), (1 - x, 1 - y)]
        ids = [2 * px + py for px, py in chips]

        def rows(a, quarter):
            return pl.ds(cuts[a], heights[a] - cuts[a]) if quarter else pl.ds(0, cuts[a])

        def held(a, quarter):
            size = heights[a] - cuts[a] if quarter else cuts[a]
            return relays[a].at[quarter, pl.ds(0, size), :]

        def direct(a, nb, piece, landing):
            px, py = chips[nb]
            return pltpu.make_async_remote_copy(
                src_ref=srcs[a].at[piece], dst_ref=dsts[a].at[landing], send_sem=send.at[a * 2 + nb],
                recv_sem=recv.at[a * 2 + nb], device_id=(px, py, c), device_id_type=MESH)

        def first_hop(a, quarter):
            k = 2 * n + a * 2 + quarter
            px, py = chips[quarter]
            return pltpu.make_async_remote_copy(
                src_ref=srcs[a].at[ids[2], rows(a, quarter), :], dst_ref=held(a, quarter), send_sem=send.at[k],
                recv_sem=recv.at[k], device_id=(px, py, c), device_id_type=MESH)

        def second_hop(a, quarter, origin):
            k = 4 * n + a * 2 + quarter
            px, py = chips[1 - quarter]
            return pltpu.make_async_remote_copy(
                src_ref=held(a, quarter), dst_ref=dsts[a].at[origin, rows(a, quarter), :], send_sem=send.at[k],
                recv_sem=recv.at[k], device_id=(px, py, c), device_id_type=MESH)

        own = [pltpu.make_async_copy(srcs[a].at[me], dsts[a].at[me], local.at[a]) for a in range(n)]
        sent = [first_hop(a, q) for a in range(n) for q in range(2)]
        sent += [direct(a, nb, ids[nb], me) for a in range(n) for nb in range(2)]
        for cp in sent + own:
            cp.start()
        for a in range(n):
            for q in range(2):
                first_hop(a, q).wait_recv()
                sent.append(second_hop(a, q, ids[q]))
                sent[-1].start()
        for a in range(n):
            for nb in range(2):
                direct(a, nb, me, ids[nb]).wait_recv()
            for q in range(2):
                second_hop(a, q, ids[2]).wait_recv()
        for cp in sent:
            cp.wait_send()
        for cp in own:
            cp.wait()

    anyspec = pl.BlockSpec(memory_space=pl.ANY)
    out = pl.pallas_call(
        body, name="chip_exchange", in_specs=[anyspec] * n, out_specs=[anyspec] * (2 * n),
        out_shape=[jax.ShapeDtypeStruct(a.shape, a.dtype) for a in arrs]
        + [jax.ShapeDtypeStruct((2, cut, a.shape[2]), a.dtype) for a, cut in zip(arrs, cuts)],
        scratch_shapes=[pltpu.SemaphoreType.DMA((6 * n,)), pltpu.SemaphoreType.DMA((6 * n,)),
                        pltpu.SemaphoreType.DMA((n,))],
    )(*arrs)
    return out[:n]


def _sum_swap_halves(slots):
    n = len(slots)

    def body(*refs):
        srcs, dsts, halves = refs[:n], refs[n:2 * n], refs[2 * n:3 * n]
        send, recv, local = refs[3 * n:]
        x, y, c = _position()

        def remote(a, landing):
            return pltpu.make_async_remote_copy(
                src_ref=halves[a], dst_ref=dsts[a].at[landing], send_sem=send.at[a], recv_sem=recv.at[a],
                device_id=(x, y, 1 - c), device_id_type=MESH)

        for a in range(n):
            def add_rows(i, carry, a=a):
                rows = pl.ds(pl.multiple_of(i * PAIR_ROWS, PAIR_ROWS), PAIR_ROWS)
                total = srcs[a][0, rows, :].astype(f32)
                for s in range(1, N_CHIPS):
                    total = total + srcs[a][s, rows, :].astype(f32)
                halves[a][rows, :] = total
                return carry

            lax.fori_loop(0, srcs[a].shape[1] // PAIR_ROWS, add_rows, 0)
        own = [pltpu.make_async_copy(halves[a], dsts[a].at[c], local.at[a]) for a in range(n)]
        sends = [remote(a, c) for a in range(n)]
        for cp in sends + own:
            cp.start()
        for a in range(n):
            remote(a, 1 - c).wait_recv()
        for cp in sends:
            cp.wait_send()
        for cp in own:
            cp.wait()

    vm = pl.BlockSpec(memory_space=pltpu.VMEM)
    return pl.pallas_call(
        body, name="sum_swap_halves", in_specs=[vm] * n, out_specs=[vm] * n,
        out_shape=[jax.ShapeDtypeStruct((2,) + a.shape[1:], f32) for a in slots],
        scratch_shapes=[pltpu.VMEM(a.shape[1:], f32) for a in slots]
        + [pltpu.SemaphoreType.DMA((n,)), pltpu.SemaphoreType.DMA((n,)), pltpu.SemaphoreType.DMA((n,))],
        compiler_params=pltpu.CompilerParams(vmem_limit_bytes=VMEM_LIMIT),
    )(*slots)


def _row_block(r):
    return 128 if r % 128 == 0 else r


def _adamw(name, w, g, m, v):
    r, n = w.shape
    if r % 128 == 0 or r * n <= 128 * 1024:
        rb, nb = _row_block(r), n
    else:
        rb, nb = r, LANES

    def body(w_ref, g_ref, m_ref, v_ref, d_ref, nm_ref, nv_ref):
        gv = g_ref[...]
        m2 = ADAM_B1 * m_ref[...] + (1.0 - ADAM_B1) * gv
        v2 = ADAM_B2 * v_ref[...] + (1.0 - ADAM_B2) * (gv * gv)
        m_hat = m2 / (1.0 - ADAM_B1 ** ADAM_STEP)
        v_hat = v2 / (1.0 - ADAM_B2 ** ADAM_STEP)
        d_ref[...] = (-ADAM_LR) * (m_hat / (jnp.sqrt(v_hat) + ADAM_EPS) + ADAM_WD * w_ref[...])
        nm_ref[...] = m2
        nv_ref[...] = v2

    spec = pl.BlockSpec((rb, nb), lambda i, j: (i, j))
    return pl.pallas_call(
        body, name=name, grid=(r // rb, n // nb), in_specs=[spec] * 4, out_specs=[spec] * 3,
        out_shape=[jax.ShapeDtypeStruct((r, n), f32)] * 3,
        compiler_params=_params(("parallel", "parallel"), VMEM_LIMIT),
    )(w, g, m, v)


def _local_step(x2, tgt2, seq, wt):
    nb = x2.shape[0] // seq
    h, qkv, f128 = _norm_qkv(x2, wt["pre_w"], wt["w_all"], wt["b_qkv"], 8 * D, wt["b_f"])
    rest = _mm("in_rest", h, wt["w_all"], (3 * D, 5 * D), wt["b_rest"], bf16, 1024, 1024)
    c = _forget_prep(f128, seq)
    ka = _attn_prep(qkv, c)
    o_att, pa, lse = _attn_fwd(ka, qkv, rest, seq)
    rnn_w = (wt["conv_w"], wt["conv_b"], wt["wa_d"], wt["wx_d"], wt["ba"], wt["bx"], wt["lam"])
    xc, a, hrec, pr = _rnn_fwd(rest, *rnn_w, seq)
    (do, dya, dyr, dmga, dmgr, doa, dga, dhrec, dgr, mrg, dy, delta, loss8, d_post) = _merge_loss(
        rest, pa, pr, o_att, hrec, wt["w_a"], wt["w_r"], wt["w_o"], x2, tgt2, wt["post_w"])
    d_wo, _ = _tn_mm("dw_out", mrg, do, D)
    d_wa, _ = _tn_mm("dw_branch_a", pa, dya, D)
    d_wr, _ = _tn_mm("dw_branch_r", pr, dyr, D)
    dxr, d_wad, d_wxd, vec = _rnn_bwd(dhrec, a, hrec, xc, rest, *rnn_w, seq)
    dq, dk, dv, dc = _attn_bwd(ka, qkv, doa, lse, delta, seq)
    df, db_f = _forget_bwd(dc, f128, seq)
    pieces = [dq, dk, dv, dga, dxr, dgr, dmga, dmgr]
    gx, d_pre = _in_bwd(pieces, df, x2, dy, wt["w_all"], wt["pre_w"])
    dw_stack, db_stack = _tn_mm_stack("dw_in", pieces, h, bf16)
    dws = [(dw_stack, p * D, D) for p in range(len(pieces))]
    dbs = [db_stack[p] for p in range(len(pieces))]
    dw_f, _ = _tn_mm("dw_in_f", df, h, D, bf16)
    shard_rows = IN_TOTAL // N_CHIPS
    w_in_pieces = _pack_pieces(dws[:3] + [(dw_f, 0, HEADS)] + dws[3:] + [(None, 0, IN_TOTAL - IN_USED)], shard_rows,
                               _padded_rows(shard_rows))
    d_b_in = jnp.concatenate(dbs[:3] + [db_f[:, :HEADS]] + dbs[3:] + [jnp.zeros((1, IN_TOTAL - IN_USED), f32)], axis=1)
    return dict(loss=loss8[0, 0], grad_x=gx, pre_w=d_pre, w_in_pieces=w_in_pieces, b_in=d_b_in, conv_w=vec[4:8],
                conv_b=vec[3:4],
                wa_d=d_wad, ba=vec[0:1], wx_d=d_wxd, bx=vec[1:2], lam=vec[2:3], w_a=d_wa, w_r=d_wr, w_o=d_wo,
                post_w=d_post)


def _block_diag(w):
    g, bw, _ = w.shape
    eye = jnp.eye(g, dtype=w.dtype)
    return (w[:, :, None, :] * eye[:, None, :, None]).reshape(g * bw, g * bw)


def _gate_blocks(diag):
    half = diag.shape[1] // 2
    return jnp.stack([diag[:, :half, :half], diag[:, half:, half:]], axis=1).reshape(-1, half, half)


def _padded_rows(rows):
    return -(-rows // 32) * 32


def _pad_cols(a, n):
    return jnp.pad(a, ((0, 0), (0, n - a.shape[1])))


def _pad_rows(a, n):
    return jnp.pad(a, ((0, n - a.shape[0]), (0, 0)))


def kernel(x, pre_norm_w, w_in, b_in, conv_w, conv_b, rg_wa, rg_ba, rg_wx, rg_bx, rg_lambda, w_branch_a, w_branch_r, w_out, post_norm_w, loss_target, m_pre_norm_w, m_w_in, m_b_in, m_conv_w, m_conv_b, m_rg_wa, m_rg_ba, m_rg_wx, m_rg_bx, m_rg_lambda, m_w_branch_a, m_w_branch_r, m_w_out, m_post_norm_w, v_pre_norm_w, v_w_in, v_b_in, v_conv_w, v_conv_b, v_rg_wa, v_rg_ba, v_rg_wx, v_rg_bx, v_rg_lambda, v_w_branch_a, v_w_branch_r, v_w_out, v_post_norm_w):
    nb, seq, _ = x.shape
    chip = 2 * lax.axis_index("x") + lax.axis_index("y")
    n_groups = rg_wa.shape[1]

    w_in_t = jnp.transpose(w_in[0])
    shard_cols = w_in_t.shape[0]
    padded = _padded_rows(shard_cols)
    q_end, f_end = 3 * D, 3 * D + HEADS
    segments = [(0, 0, q_end), (q_end, f_end, IN_USED - f_end), (IN_USED - HEADS, q_end, HEADS)]
    g_a, g_r, g_o, g_cw, w_all = _gather_shards(
        [_pad_rows(w_in_t.astype(bf16), padded), w_branch_a[0].astype(bf16), w_branch_r[0].astype(bf16),
         w_out[0].astype(bf16)], conv_w[0], shard_cols, segments, IN_USED - HEADS + LANES)
    wt = dict(
        pre_w=pre_norm_w, post_w=post_norm_w,
        w_all=w_all, b_qkv=b_in[:, :q_end], b_f=_pad_cols(b_in[:, q_end:f_end], LANES), b_rest=b_in[:, f_end:IN_USED],
        w_a=g_a.reshape(D, D), w_r=g_r.reshape(D, D), w_o=g_o.reshape(D, D),
        conv_w=jnp.transpose(g_cw, (1, 0, 2)).reshape(4, D), conv_b=conv_b,
        wa_d=_block_diag(rg_wa[0]).astype(bf16), wx_d=_block_diag(rg_wx[0]).astype(bf16),
        ba=rg_ba, bx=rg_bx, lam=rg_lambda)

    part = _local_step(x.reshape(nb * seq, D), loss_target.reshape(nb * seq, D), seq, wt)
    loss = lax.psum(part["loss"], ("x", "y", "c"))
    grad_x = part["grad_x"].reshape(nb, seq, D)

    small = jnp.concatenate([
        part["pre_w"], _pad_cols(part["b_in"], 10 * D).reshape(10, D), part["conv_b"],
        _gate_blocks(part["wa_d"]).reshape(-1, D), part["ba"],
        _gate_blocks(part["wx_d"]).reshape(-1, D), part["bx"], part["lam"], part["post_w"],
        part["conv_w"]], axis=0)
    n_small = small.shape[0]
    n_rep = n_small - 4
    tot = _allsum_rows(_pad_rows(small, -(-n_small // 8) * 8))
    g_rep = tot[:n_rep]
    g_conv_w = lax.dynamic_slice_in_dim(tot[n_rep:n_small], chip * (D // N_CHIPS), D // N_CHIPS, axis=1)

    def unpack(p):
        o = [0]

        def take(k):
            o[0] += k
            return p[o[0] - k:o[0]]

        pre = take(1)
        b = take(10).reshape(1, 10 * D)[:, :IN_TOTAL]
        cb = take(1)
        wa = take(64).reshape(rg_wa.shape)
        ba = take(1)
        wx = take(64).reshape(rg_wx.shape)
        bx = take(1)
        lam = take(1)
        post = take(1)
        return dict(pre_norm_w=pre, b_in=b, conv_b=cb, rg_wa=wa, rg_ba=ba, rg_wx=wx, rg_bx=bx, rg_lambda=lam,
                    post_norm_w=post)

    grads = unpack(g_rep)
    replicated = dict(
        pre_norm_w=(pre_norm_w, m_pre_norm_w, v_pre_norm_w), b_in=(b_in, m_b_in, v_b_in),
        conv_b=(conv_b, m_conv_b, v_conv_b), rg_wa=(rg_wa, m_rg_wa, v_rg_wa), rg_ba=(rg_ba, m_rg_ba, v_rg_ba),
        rg_wx=(rg_wx, m_rg_wx, v_rg_wx), rg_bx=(rg_bx, m_rg_bx, v_rg_bx),
        rg_lambda=(rg_lambda, m_rg_lambda, v_rg_lambda), post_norm_w=(post_norm_w, m_post_norm_w, v_post_norm_w))
    deltas, new_m, new_v = {}, {}, {}
    for name, (w, m, v) in replicated.items():
        as2d = lambda a: a.reshape(-1, D) if a.ndim > 2 else a
        upd = _adamw("adamw_" + name, as2d(w), as2d(grads[name]), as2d(m), as2d(v))
        deltas[name], new_m[name], new_v[name] = [a.reshape(w.shape) for a in upd]

    p_aro = jnp.concatenate([part[k].reshape(N_DEV, D // N_DEV, D) for k in ("w_a", "w_r", "w_o")], axis=1)
    s_in, s_aro = _chip_exchange([_pair_reduce("pair_w_in", part["w_in_pieces"]),
                                  _pair_reduce("pair_w_aro", p_aro.astype(bf16))])
    f_in, f_aro = _sum_swap_halves([s_in, s_aro])
    g_w_in_t = f_in.reshape(padded, D)[:shard_cols]
    rows = D // N_DEV
    g_aro = [f_aro[:, i * rows:(i + 1) * rows, :].reshape(2 * rows, D) for i in range(3)]

    w_in_upd = _adamw("adamw_w_in", w_in_t, g_w_in_t, jnp.transpose(m_w_in[0]), jnp.transpose(v_w_in[0]))
    g_w_in, d_w_in, nm_w_in, nv_w_in = [jnp.transpose(a) for a in (g_w_in_t, *w_in_upd)]
    upd_a = _adamw("adamw_w_branch_a", w_branch_a[0], g_aro[0], m_w_branch_a[0], v_w_branch_a[0])
    upd_r = _adamw("adamw_w_branch_r", w_branch_r[0], g_aro[1], m_w_branch_r[0], v_w_branch_r[0])
    upd_o = _adamw("adamw_w_out", w_out[0], g_aro[2], m_w_out[0], v_w_out[0])
    d_aro, nm_aro, nv_aro = zip(upd_a, upd_r, upd_o)
    d_cw, nm_cw, nv_cw = _adamw("adamw_conv_w", conv_w[0], g_conv_w, m_conv_w[0], v_conv_w[0])

    def sharded(t_in, t_aro, t_cw):
        return dict(w_in=t_in[None], conv_w=t_cw[None], w_branch_a=t_aro[0][None], w_branch_r=t_aro[1][None],
                    w_out=t_aro[2][None])

    order = ["pre_norm_w", "w_in", "b_in", "conv_w", "conv_b", "rg_wa", "rg_ba", "rg_wx", "rg_bx", "rg_lambda",
             "w_branch_a", "w_branch_r", "w_out", "post_norm_w"]
    outs = [loss, grad_x]
    for rep, shd in ((grads, sharded(g_w_in, g_aro, g_conv_w)), (deltas, sharded(d_w_in, d_aro, d_cw)),
                     (new_m, sharded(nm_w_in, nm_aro, nm_cw)), (new_v, sharded(nv_w_in, nv_aro, nv_cw))):
        both = {**rep, **shd}
        outs.extend(both[k] for k in order)
    return tuple(outs)
```

```python
import jax
import jax.numpy as jnp
from jax import lax
from jax.experimental import pallas as pl
from jax.experimental.pallas import tpu as pltpu

f32 = jnp.float32
bf16 = jnp.bfloat16

D = 1024
HEADS = 16
LANES = 128
NORM_EPS = 1e-6
MASK_VALUE = -1e30
RG_C = 8.0
QK_SCALE = 0.125
TQ = 256
ATT_GROUP = 8
ATT_GROUP_FWD = 16
TL = 512
TM = 512
PREV_ROWS = 16
IN_USED = 8 * D + HEADS
IN_TOTAL = 9 * D + HEADS
N_CHIPS = 4
N_DEV = 8
ADAM_LR, ADAM_B1, ADAM_B2, ADAM_EPS, ADAM_WD, ADAM_STEP = 0.001, 0.9, 0.999, 1e-08, 0.01, 10
VMEM_LIMIT = 56 * 1024 * 1024
MESH = pl.DeviceIdType.MESH


def _dot(a, b):
    return jnp.dot(a, b, preferred_element_type=f32)


def _dot_nt(a, b):
    return lax.dot_general(a, b, (((1,), (1,)), ((), ())), preferred_element_type=f32)


def _dot_tn(a, b):
    return lax.dot_general(a, b, (((0,), (0,)), ((), ())), preferred_element_type=f32)


def _sig(x):
    return 0.5 * jnp.tanh(0.5 * x) + 0.5


def _softplus(x):
    return jnp.maximum(x, 0.0) + jnp.log(1.0 + jnp.exp(-jnp.abs(x)))


def _params(sem, vmem=None):
    return pltpu.CompilerParams(dimension_semantics=sem, vmem_limit_bytes=vmem)


def _tile(tm, width, cb=0):
    return pl.BlockSpec((tm, width), lambda i, cb=cb: (i, cb))


def _whole(shape):
    nd = len(shape)
    return pl.BlockSpec(shape, lambda *_: (0,) * nd)


def _norm_qkv(x, w_pre, w_all, b_qkv, f_row0, b_f, tm=1024):
    t = x.shape[0]
    tm = min(tm, t)
    n = b_qkv.shape[1]

    def body(x_ref, wp_ref, w_ref, b_ref, wf_ref, bf_ref, h_ref, o_ref, f_ref):
        @pl.when(pl.program_id(1) == 0)
        def _():
            xv = x_ref[...]
            r = lax.rsqrt(jnp.mean(xv * xv, axis=-1, keepdims=True) + NORM_EPS)
            h = (xv * r * wp_ref[...]).astype(bf16)
            h_ref[...] = h
            f_ref[...] = _dot_nt(h, wf_ref[...]) + bf_ref[...]

        o_ref[...] = (_dot_nt(h_ref[...], w_ref[...]) + b_ref[...]).astype(bf16)

    return pl.pallas_call(
        body, name="norm_qkv", grid=(t // tm, n // D),
        in_specs=[pl.BlockSpec((tm, D), lambda i, j: (i, 0)), _whole((1, D)), pl.BlockSpec((D, D), lambda i, j: (j, 0)),
                  pl.BlockSpec((1, D), lambda i, j: (0, j)),
                  pl.BlockSpec((LANES, D), lambda i, j: (f_row0 // LANES, 0)), _whole((1, LANES))],
        out_specs=[pl.BlockSpec((tm, D), lambda i, j: (i, 0)), pl.BlockSpec((tm, D), lambda i, j: (i, j)),
                   pl.BlockSpec((tm, LANES), lambda i, j: (i, 0))],
        out_shape=[jax.ShapeDtypeStruct((t, D), bf16), jax.ShapeDtypeStruct((t, n), bf16),
                   jax.ShapeDtypeStruct((t, LANES), f32)],
        compiler_params=_params(("parallel", "arbitrary"), VMEM_LIMIT),
    )(x, w_pre, w_all, b_qkv, w_all, b_f)


def _mm(name, a, w, w_rows, bias, out_dtype, tm, tn):
    t, k = a.shape
    tm = min(tm, t)
    row0, n = w_rows
    assert row0 % tn == 0

    def body(a_ref, w_ref, b_ref, o_ref):
        o_ref[...] = (_dot_nt(a_ref[...], w_ref[...]) + b_ref[...]).astype(out_dtype)

    return pl.pallas_call(
        body, name=name, grid=(t // tm, n // tn),
        in_specs=[pl.BlockSpec((tm, k), lambda i, j: (i, 0)), pl.BlockSpec((tn, k), lambda i, j: (row0 // tn + j, 0)),
                  pl.BlockSpec((1, tn), lambda i, j: (0, j))],
        out_specs=pl.BlockSpec((tm, tn), lambda i, j: (i, j)), out_shape=jax.ShapeDtypeStruct((t, n), out_dtype),
        compiler_params=_params(("parallel", "parallel"), VMEM_LIMIT),
    )(a, w, bias)


def _forget_prep(f128, seq):
    t = f128.shape[0]
    nb = seq // LANES

    def body(f_ref, c_ref):
        r = lax.broadcasted_iota(jnp.int32, (LANES, LANES), 0)
        cidx = lax.broadcasted_iota(jnp.int32, (LANES, LANES), 1)
        tri = (r >= cidx).astype(f32)
        carry = jnp.zeros((1, LANES), f32)
        for blk in range(nb):
            fv = f_ref[pl.ds(blk * LANES, LANES), :]
            lf = -_softplus(-fv)
            c_ref[pl.ds(blk * LANES, LANES), :] = (
                jnp.dot(tri, lf, preferred_element_type=f32, precision=lax.Precision.HIGHEST) + carry)
            carry = carry + jnp.sum(lf, axis=0, keepdims=True)

    return pl.pallas_call(
        body, name="forget_prep", grid=(t // seq,),
        in_specs=[pl.BlockSpec((seq, LANES), lambda b: (b, 0))],
        out_specs=pl.BlockSpec((seq, LANES), lambda b: (b, 0)),
        out_shape=jax.ShapeDtypeStruct((t, LANES), f32),
        compiler_params=_params(("parallel",)),
    )(f128)


def _split3(cv):
    hi = cv.astype(bf16)
    r1 = cv - hi.astype(f32)
    mid = r1.astype(bf16)
    lo = (r1 - mid.astype(f32)).astype(bf16)
    return hi, mid, lo


def _q_operand(q2, half):
    lane = lax.broadcasted_iota(jnp.int32, (1, LANES), 1)
    first = 64 * (1 - half)
    ones = jnp.where((lane >= first) & (lane < first + 3), jnp.ones((), bf16), jnp.zeros((), bf16))
    return jnp.where((lane // 64) == half, q2 * jnp.asarray(QK_SCALE, bf16), ones)


def _attn_prep(qkv, c):
    t = qkv.shape[0]

    def body(k_ref, c_ref, ka_ref):
        lane = lax.broadcasted_iota(jnp.int32, (1, LANES), 1)
        cv = c_ref[...]
        zero = jnp.zeros((), bf16)
        for head in range(HEADS):
            half, first = head % 2, 64 * (1 - head % 2)
            ch = jnp.sum(jnp.where(lane == head, cv, 0.0), axis=1, keepdims=True)
            hi, mid, lo = _split3(-ch)
            pieces = jnp.where(lane == first, hi, jnp.where(lane == first + 1, mid, jnp.where(lane == first + 2, lo, zero)))
            ka_ref[:, pl.ds(head * LANES, LANES)] = jnp.where(
                (lane // 64) == half, k_ref[:, pl.ds((head // 2) * LANES, LANES)], pieces)

    tm = min(TM, t)
    return pl.pallas_call(
        body, name="attn_prep", grid=(t // tm,),
        in_specs=[_tile(tm, D, 1), _tile(tm, LANES)],
        out_specs=pl.BlockSpec((tm, 2 * D), lambda i: (i, 0)),
        out_shape=jax.ShapeDtypeStruct((t, 2 * D), bf16),
        compiler_params=_params(("parallel",)),
    )(qkv, c)


def _attn_fwd(ka, qkv, rest, seq):
    t = qkv.shape[0]
    nb, nq = t // seq, seq // TQ

    hg = ATT_GROUP_FWD
    ng = HEADS // hg

    def body(q_ref, k_ref, v_ref, ga_ref, o_ref, pa_ref, lse_ref, acc_scr, qop_scr):
        qi, gi = pl.program_id(1), pl.program_id(2)
        krow = lax.broadcasted_iota(jnp.int32, (TQ, TQ), 0)
        qcol = lax.broadcasted_iota(jnp.int32, (TQ, TQ), 1)
        acc_scr[...] = jnp.zeros_like(acc_scr)
        for g in range(hg):
            qop_scr[g] = _q_operand(q_ref[:, pl.ds((g // 2) * LANES, LANES)], g % 2)

        def kv_step(kt, carry, masked):
            ks = pl.multiple_of(kt * TQ, TQ)
            sts = [_dot_nt(k_ref[pl.ds(ks, TQ), pl.ds(g * LANES, LANES)], qop_scr[g]) for g in range(hg)]
            if masked:
                sts = [jnp.where(krow <= qcol, st, MASK_VALUE) for st in sts]
            m_new = [jnp.maximum(carry[g][0], jnp.max(sts[g], axis=0, keepdims=True)) for g in range(hg)]
            ps = [jnp.exp(sts[g] - m_new[g]) for g in range(hg)]
            alphas = [jnp.exp(carry[g][0] - m_new[g]) for g in range(hg)]
            phi = [ps[g].astype(bf16) for g in range(hg)]
            plo = [(ps[g] - phi[g].astype(f32)).astype(bf16) for g in range(hg)]
            vs = [v_ref[pl.ds(ks, TQ), pl.ds(j * LANES, LANES)] for j in range(hg // 2)]
            pvs = [_dot_tn(vs[g // 2], phi[g]) + _dot_tn(vs[g // 2], plo[g]) for g in range(hg)]
            olds = [acc_scr[g] for g in range(hg)]
            for g in range(hg):
                acc_scr[g] = alphas[g] * olds[g] + pvs[g]
            return tuple((m_new[g], alphas[g] * carry[g][1] + jnp.sum(ps[g], axis=0, keepdims=True))
                         for g in range(hg))

        init = tuple((jnp.full((1, TQ), MASK_VALUE, f32), jnp.zeros((1, TQ), f32)) for _ in range(hg))
        carry = lax.fori_loop(0, qi, lambda kt, cr: kv_step(kt, cr, False), init)
        stats = kv_step(qi, carry, True)
        drow = lax.broadcasted_iota(jnp.int32, (LANES, TQ), 0)
        for g in range(hg):
            m, l = stats[g]
            lse_ref[0, pl.ds(hg * gi + g, 1), :] = m + jnp.log(l)
        for j in range(hg // 2):
            o2 = jnp.where(drow < 64, acc_scr[2 * j] / stats[2 * j][1], acc_scr[2 * j + 1] / stats[2 * j + 1][1]).T
            o_ref[:, pl.ds(j * LANES, LANES)] = o2
            ga = ga_ref[:, pl.ds(j * LANES, LANES)].astype(f32)
            pa_ref[:, pl.ds(j * LANES, LANES)] = (o2 * (ga * _sig(ga))).astype(bf16)

    vw = hg * 64
    tile = pl.BlockSpec((TQ, vw), lambda b, qi, gi: (b * nq + qi, gi))
    return pl.pallas_call(
        body, name="attn_fwd", grid=(nb, nq, ng),
        in_specs=[tile, pl.BlockSpec((seq, hg * LANES), lambda b, qi, gi: (b, gi)),
                  pl.BlockSpec((seq, vw), lambda b, qi, gi: (b, 2 * ng + gi)), tile],
        out_specs=[tile, tile, pl.BlockSpec((1, HEADS, TQ), lambda b, qi, gi: (b * nq + qi, 0, 0))],
        out_shape=[jax.ShapeDtypeStruct((t, D), f32), jax.ShapeDtypeStruct((t, D), bf16),
                   jax.ShapeDtypeStruct((t // TQ, HEADS, TQ), f32)],
        scratch_shapes=[pltpu.VMEM((hg, LANES, TQ), f32), pltpu.VMEM((hg, TQ, LANES), bf16)],
        compiler_params=_params(("parallel", "parallel", "arbitrary"), VMEM_LIMIT),
    )(qkv, ka, qkv, rest)


def _shifted_rows(x, top8, prev8, shift, row, row8):
    body = pltpu.roll(x, shift, 0)
    head = jnp.where(row8 < shift, pltpu.roll(prev8, shift, 0), pltpu.roll(top8, shift, 0))
    return body, head


def _rnn_gates(xc, wa_ref, wx_ref, ba_ref, bx_ref, lam_ref):
    xcb = xc.astype(bf16)
    r = _sig(_dot(xcb, wa_ref[...]) + ba_ref[...])
    i = _sig(_dot(xcb, wx_ref[...]) + bx_ref[...])
    sp = _softplus(-lam_ref[...])
    log_a = (-RG_C) * r * sp
    th = jnp.tanh(log_a)
    w1 = (-2.0) * th / (1.0 - th)
    sq = jnp.sqrt(jnp.maximum(w1, 0.0))
    return r, i, sp, log_a, w1, sq


def _conv_tile(x_ref, xprev_ref, has_prev, cw_ref, cb_ref, xc_ref):
    row = lax.broadcasted_iota(jnp.int32, (TL, D), 0)
    row8 = lax.broadcasted_iota(jnp.int32, (8, D), 0)
    x = x_ref[...].astype(f32)
    top8 = x[:8]
    prev8 = jnp.where(has_prev, xprev_ref[...].astype(f32)[PREV_ROWS - 8:], 0.0)
    xc = cb_ref[...] + cw_ref[pl.ds(3, 1), :] * x
    xc8 = cb_ref[...] + cw_ref[pl.ds(3, 1), :] * top8
    for sh in range(1, 4):
        w = cw_ref[pl.ds(3 - sh, 1), :]
        xs, xs8 = _shifted_rows(x, top8, prev8, sh, row, row8)
        xc = xc + w * xs
        xc8 = xc8 + w * xs8
    xc_ref[...] = xc
    xc_ref[pl.ds(0, 8), :] = xc8


def _rnn_fwd(rest, conv_w, conv_b, wa_d, wx_d, ba, bx, lam, seq):
    t = rest.shape[0]
    nb, nt = t // seq, seq // TL

    def body(x_ref, xprev_ref, gr_ref, cw_ref, cb_ref, wa_ref, wx_ref, ba_ref, bx_ref, lam_ref,
             xc_ref, a_ref, h_ref, pr_ref, xc_scr, u_scr, h_scr, carry):
        tt = pl.program_id(1)
        _conv_tile(x_ref, xprev_ref, tt > 0, cw_ref, cb_ref, xc_scr)
        xc = xc_scr[...]
        xc_ref[...] = xc.astype(bf16)
        r, i, sp, log_a, w1, sq = _rnn_gates(xc, wa_ref, wx_ref, ba_ref, bx_ref, lam_ref)
        a_ref[...] = jnp.exp(log_a)
        u_scr[...] = sq * (i * xc)

        @pl.when(tt == 0)
        def _():
            carry[...] = jnp.zeros_like(carry)

        def step(s, h):
            h = a_ref[pl.ds(s, 1), :] * h + u_scr[pl.ds(s, 1), :]
            h_scr[pl.ds(s, 1), :] = h
            return h

        carry[...] = lax.fori_loop(0, TL, step, carry[...], unroll=8)
        gr = gr_ref[...].astype(f32)
        h = h_scr[...]
        h_ref[...] = h.astype(bf16)
        pr_ref[...] = (h * (gr * _sig(gr))).astype(bf16)

    tile = lambda cb: pl.BlockSpec((TL, D), lambda b, tt, cb=cb: (b * nt + tt, cb))
    prev = lambda cb: pl.BlockSpec(
        (PREV_ROWS, D), lambda b, tt, cb=cb: (jnp.maximum((b * nt + tt) * (TL // PREV_ROWS) - 1, 0), cb))
    vec = _whole((1, D))
    return pl.pallas_call(
        body, name="rnn_fwd", grid=(nb, nt),
        in_specs=[tile(1), prev(1), tile(2), _whole((4, D)), vec, _whole((D, D)), _whole((D, D)), vec, vec, vec],
        out_specs=[tile(0)] * 4,
        out_shape=[jax.ShapeDtypeStruct((t, D), dt) for dt in (bf16, f32, bf16, bf16)],
        scratch_shapes=[pltpu.VMEM((TL, D), f32)] * 3 + [pltpu.VMEM((1, D), f32)],
        compiler_params=_params(("parallel", "arbitrary"), VMEM_LIMIT),
    )(rest, rest, rest, conv_w, conv_b, wa_d, wx_d, ba, bx, lam)


def _merge_loss(rest, pa, pr, o_att, hrec, w_a, w_r, w_out, x, tgt, w_post):
    t = x.shape[0]

    def branch(dy, w_ref, g_ref, act):
        dp = _dot_nt(dy, w_ref[...])
        g = g_ref[...].astype(f32)
        sg = _sig(g)
        return (dp * (g * sg)).astype(bf16), (dp * act * (sg * (1.0 + g * (1.0 - sg)))).astype(bf16)

    def body(mga_ref, mgr_ref, pa_ref, pr_ref, ga_ref, gr_ref, oa_ref, h_ref, x_ref, t_ref, wa_ref, wr_ref, wo_ref,
             w_ref, do_ref, dya_ref, dyr_ref, dmga_ref, dmgr_ref, doa_ref, dga_ref, dh_ref, dgr_ref, mrg_ref, dy_ref,
             delta_ref, loss_ref, dwp_ref):
        @pl.when(pl.program_id(0) == 0)
        def _():
            loss_ref[...] = jnp.zeros_like(loss_ref)
            dwp_ref[...] = jnp.zeros_like(dwp_ref)

        sa, sr = _sig(mga_ref[...].astype(f32)), _sig(mgr_ref[...].astype(f32))
        ya, yr = _dot(pa_ref[...], wa_ref[...]), _dot(pr_ref[...], wr_ref[...])
        mrg = (sa * ya + sr * yr).astype(bf16)
        mrg_ref[...] = mrg
        ov = _dot(mrg, wo_ref[...])
        w = w_ref[...]
        r2 = lax.rsqrt(jnp.mean(ov * ov, axis=-1, keepdims=True) + NORM_EPS)
        oh = ov * r2
        e = x_ref[...] + oh * w - t_ref[...]
        loss_ref[...] += 0.5 * jnp.sum(jnp.mean(e * e, axis=-1, keepdims=True))
        dy = e * (1.0 / D)
        dy_ref[...] = dy
        dwp_ref[...] += jnp.sum(dy * oh, axis=0, keepdims=True)
        doh = dy * w
        do = (r2 * (doh - oh * jnp.mean(doh * oh, axis=-1, keepdims=True))).astype(bf16)
        do_ref[...] = do

        dm = _dot_nt(do, wo_ref[...])
        dya, dyr = (dm * sa).astype(bf16), (dm * sr).astype(bf16)
        dya_ref[...] = dya
        dyr_ref[...] = dyr
        dmga_ref[...] = (dm * ya * sa * (1.0 - sa)).astype(bf16)
        dmgr_ref[...] = (dm * yr * sr * (1.0 - sr)).astype(bf16)
        o_att = oa_ref[...]
        doa, dga_ref[...] = branch(dya, wa_ref, ga_ref, o_att)
        doa_ref[...] = doa
        dh_ref[...], dgr_ref[...] = branch(dyr, wr_ref, gr_ref, h_ref[...].astype(f32))
        ch = lax.broadcasted_iota(jnp.int32, (D, LANES), 0)
        hd = lax.broadcasted_iota(jnp.int32, (D, LANES), 1)
        pick = (ch // 64 == hd).astype(bf16)
        per_head = sum(_dot(piece, pick) for piece in _split3(doa.astype(f32) * o_att))
        delta_ref[0] = per_head.T[:HEADS, :]

    once = pl.BlockSpec((D, D), lambda i: (0, 0), pipeline_mode=pl.Buffered(1))
    rows = _tile(TQ, D)
    return pl.pallas_call(
        body, name="merge_loss", grid=(t // TQ,),
        in_specs=[_tile(TQ, D, 3), _tile(TQ, D, 4), rows, rows, _tile(TQ, D, 0), _tile(TQ, D, 2), rows, rows, rows, rows,
                  once, once, once, _whole((1, D))],
        out_specs=[rows] * 11 + [pl.BlockSpec((1, HEADS, TQ), lambda i: (i, 0, 0)), _whole((8, LANES)), _whole((1, D))],
        out_shape=[jax.ShapeDtypeStruct((t, D), bf16)] * 10 + [jax.ShapeDtypeStruct((t, D), f32),
                   jax.ShapeDtypeStruct((t // TQ, HEADS, TQ), f32), jax.ShapeDtypeStruct((8, LANES), f32),
                   jax.ShapeDtypeStruct((1, D), f32)],
        compiler_params=_params(("arbitrary",), VMEM_LIMIT),
    )(rest, rest, pa, pr, rest, rest, o_att, hrec, x, tgt, w_a, w_r, w_out, w_post)


def _rnn_bwd(dh, a, h, xc, rest, conv_w, conv_b, wa_d, wx_d, ba, bx, lam, seq):
    t = dh.shape[0]
    nb, nt = t // seq, seq // TL
    diag = (D // LANES, LANES, LANES)

    def body(dh_ref, a_ref, h_ref, hprev_ref, xc_ref, x_ref, cw_ref, cb_ref, wa_ref, wx_ref,
             ba_ref, bx_ref, lam_ref, dxr_ref, dwa_ref, dwx_ref, vec_ref, g_scr, dxc_scr, dxr_scr, qcarry, dxc_next):
        b, tt = pl.program_id(0), pl.program_id(1)
        rt = nt - 1 - tt

        @pl.when((b == 0) & (tt == 0))
        def _():
            dwa_ref[...] = jnp.zeros_like(dwa_ref)
            dwx_ref[...] = jnp.zeros_like(dwx_ref)
            vec_ref[...] = jnp.zeros_like(vec_ref)

        @pl.when(tt == 0)
        def _():
            qcarry[...] = jnp.zeros_like(qcarry)
            dxc_next[...] = jnp.zeros_like(dxc_next)

        g_scr[...] = dh_ref[...].astype(f32)

        def step(k, q):
            s = TL - 1 - k
            g = g_scr[pl.ds(s, 1), :] + q
            g_scr[pl.ds(s, 1), :] = g
            return a_ref[pl.ds(s, 1), :] * g

        qcarry[...] = lax.fori_loop(0, TL, step, qcarry[...], unroll=8)

        row = lax.broadcasted_iota(jnp.int32, (TL, D), 0)
        row8 = lax.broadcasted_iota(jnp.int32, (8, D), 0)
        g = g_scr[...]
        av = a_ref[...]
        xc = xc_ref[...].astype(f32)
        hlast = jnp.where(rt > 0, hprev_ref[...].astype(f32)[PREV_ROWS - 1:], 0.0)
        hp = jnp.where(row == 0, hlast, pltpu.roll(h_ref[...].astype(f32), 1, 0))
        r, i, sp, log_a, w1, sq = _rnn_gates(xc, wa_ref, wx_ref, ba_ref, bx_ref, lam_ref)
        dix = g * sq
        di = dix * xc
        dxc = dix * i
        dsq = g * (i * xc)
        dlog_a = g * hp * av - dsq * jnp.where(sq > 0.0, (1.0 - w1) / sq, 0.0)
        dpr = (dlog_a * ((-RG_C) * sp)) * r * (1.0 - r)
        dpi = di * i * (1.0 - i)
        dprb, dpib, xcb = dpr.astype(bf16), dpi.astype(bf16), xc.astype(bf16)
        dxc = dxc + _dot_nt(dprb, wa_ref[...]) + _dot_nt(dpib, wx_ref[...])
        for j in range(D // LANES):
            cols = slice(j * LANES, (j + 1) * LANES)
            dwa_ref[j] += _dot_tn(xcb[:, cols], dprb[:, cols])
            dwx_ref[j] += _dot_tn(xcb[:, cols], dpib[:, cols])
        vec_ref[pl.ds(0, 1), :] += jnp.sum(dpr, axis=0, keepdims=True)
        vec_ref[pl.ds(1, 1), :] += jnp.sum(dpi, axis=0, keepdims=True)
        dsp = jnp.sum(dlog_a * ((-RG_C) * r), axis=0, keepdims=True)
        vec_ref[pl.ds(2, 1), :] += dsp * (-_sig(-lam_ref[...]))
        vec_ref[pl.ds(3, 1), :] += jnp.sum(dxc, axis=0, keepdims=True)

        dxc_scr[...] = dxc
        bot8 = dxc_scr[pl.ds(TL - 8, 8), :]
        nxt8 = dxc_next[...]
        x = x_ref[...].astype(f32)
        x_bot8 = x[TL - 8:]
        dxr = cw_ref[pl.ds(3, 1), :] * dxc
        dxr8 = cw_ref[pl.ds(3, 1), :] * bot8
        vec_ref[pl.ds(7, 1), :] += jnp.sum(dxc * x, axis=0, keepdims=True)
        for sh in range(1, 4):
            w = cw_ref[pl.ds(3 - sh, 1), :]
            up = pltpu.roll(dxc, TL - sh, 0)
            from_next = pltpu.roll(nxt8, 8 - sh, 0)
            dxr = dxr + w * up
            dxr8 = dxr8 + w * jnp.where(row8 < 8 - sh, pltpu.roll(bot8, 8 - sh, 0), from_next)
            inside = jnp.sum(jnp.where(row < TL - sh, up, 0.0) * x, axis=0, keepdims=True)
            across = jnp.sum(jnp.where(row8 >= 8 - sh, from_next, 0.0) * x_bot8, axis=0, keepdims=True)
            vec_ref[pl.ds(7 - sh, 1), :] += inside + across
        dxr_scr[...] = dxr
        dxr_scr[pl.ds(TL - 8, 8), :] = dxr8
        dxr_ref[...] = dxr_scr[...].astype(bf16)
        dxc_next[...] = dxc_scr[pl.ds(0, 8), :]

    tile = lambda cb: pl.BlockSpec((TL, D), lambda b, tt, cb=cb: (b * nt + nt - 1 - tt, cb))
    prev = lambda cb: pl.BlockSpec(
        (PREV_ROWS, D), lambda b, tt, cb=cb: (jnp.maximum((b * nt + nt - 1 - tt) * (TL // PREV_ROWS) - 1, 0), cb))
    vec = _whole((1, D))
    return pl.pallas_call(
        body, name="rnn_bwd", grid=(nb, nt),
        in_specs=[tile(0), tile(0), tile(0), prev(0), tile(0), tile(1),
                  _whole((4, D)), vec, _whole((D, D)), _whole((D, D)), vec, vec, vec],
        out_specs=[tile(0), _whole(diag), _whole(diag), _whole((8, D))],
        out_shape=[jax.ShapeDtypeStruct((t, D), bf16), jax.ShapeDtypeStruct(diag, f32),
                   jax.ShapeDtypeStruct(diag, f32), jax.ShapeDtypeStruct((8, D), f32)],
        scratch_shapes=[pltpu.VMEM((TL, D), f32), pltpu.VMEM((TL, D), f32), pltpu.VMEM((TL, D), f32),
                        pltpu.VMEM((1, D), f32), pltpu.VMEM((8, D), f32)],
        compiler_params=_params(("arbitrary", "arbitrary"), VMEM_LIMIT),
    )(dh, a, h, h, xc, rest, conv_w, conv_b, wa_d, wx_d, ba, bx, lam)


def _attn_bwd(ka, qkv, doa, lse, delta, seq):
    t = qkv.shape[0]
    nb, nq = t // seq, seq // TQ
    hg = ATT_GROUP
    ng, npair = HEADS // hg, hg // 2

    def body(ka_ref, q_ref, k_ref, v_ref, do_ref, lse_ref, dl_ref, dq_ref, dk_ref, dv_ref, dc_ref,
             dqt_scr, dk_scr, dv_scr, ds_scr, kht_scr):
        gi, kt = pl.program_id(1), pl.program_id(2)
        lane = lax.broadcasted_iota(jnp.int32, (1, LANES), 1)
        krow = lax.broadcasted_iota(jnp.int32, (TQ, TQ), 0)
        qcol = lax.broadcasted_iota(jnp.int32, (TQ, TQ), 1)
        lmask = [(lane // 64) == hh for hh in range(2)]
        scale = jnp.asarray(QK_SCALE, bf16)

        @pl.when(kt == 0)
        def _():
            dqt_scr[...] = jnp.zeros_like(dqt_scr)

        dk_scr[...] = jnp.zeros_like(dk_scr)
        dv_scr[...] = jnp.zeros_like(dv_scr)
        ds_scr[...] = jnp.zeros_like(ds_scr)
        for g in range(hg):
            k2 = k_ref[:, pl.ds((g // 2) * LANES, LANES)]
            kht_scr[g] = jnp.where(lmask[g % 2], k2, jnp.zeros_like(k2)).T

        def q_step(qt, masked):
            qs = pl.multiple_of(qt * TQ, TQ)
            heads = range(hg)
            do2 = [do_ref[pl.ds(qs, TQ), pl.ds(j * LANES, LANES)] for j in range(npair)]
            q2 = [q_ref[pl.ds(qs, TQ), pl.ds(j * LANES, LANES)] for j in range(npair)]
            doh = [jnp.where(lmask[g % 2], do2[g // 2], jnp.zeros_like(do2[0])) for g in heads]
            qh = [jnp.where(lmask[g % 2], q2[g // 2], jnp.zeros_like(q2[0])) * scale for g in heads]
            st = [_dot_nt(ka_ref[:, pl.ds(g * LANES, LANES)], _q_operand(q2[g // 2], g % 2)) for g in heads]
            if masked:
                st = [jnp.where(krow <= qcol, s, MASK_VALUE) for s in st]
            dp = [_dot_nt(v_ref[:, pl.ds((g // 2) * LANES, LANES)], doh[g]) for g in heads]
            p = [jnp.exp(st[g] - lse_ref[qt, pl.ds(hg * gi + g, 1), :]) for g in heads]
            ds = [p[g] * (dp[g] - dl_ref[qt, pl.ds(hg * gi + g, 1), :]) for g in heads]
            pb = [x.astype(bf16) for x in p]
            dsb = [x.astype(bf16) for x in ds]
            for j in range(npair):
                a, b = 2 * j, 2 * j + 1
                dv_scr[j] += _dot(pb[a], doh[a]) + _dot(pb[b], doh[b])
                dk_scr[j] += _dot(dsb[a], qh[a]) + _dot(dsb[b], qh[b])
                dqt_scr[qt, j] += (_dot(kht_scr[a], dsb[a]) + _dot(kht_scr[b], dsb[b])) * QK_SCALE
            for g in heads:
                ds_scr[g] += ds[g][:, :LANES] + ds[g][:, LANES:]

        q_step(kt, True)

        def loop_body(qt, carry):
            q_step(qt, False)
            return carry

        lax.fori_loop(kt + 1, nq, loop_body, 0)

        dc = jnp.zeros((TQ, LANES), f32)
        for g in range(hg):
            dc = jnp.where(lane == g, -jnp.sum(ds_scr[g], axis=1, keepdims=True), dc)
        dc_ref[...] = dc
        for j in range(npair):
            dk_ref[:, pl.ds(j * LANES, LANES)] = dk_scr[j].astype(bf16)
            dv_ref[:, pl.ds(j * LANES, LANES)] = dv_scr[j].astype(bf16)

        @pl.when(kt == nq - 1)
        def _():
            for qt in range(nq):
                for j in range(npair):
                    dq_ref[pl.ds(qt * TQ, TQ), pl.ds(j * LANES, LANES)] = dqt_scr[qt, j].T.astype(bf16)

    vw = hg * 64
    seqspec = pl.BlockSpec((seq, vw), lambda b, gi, kt: (b, gi))
    kspec = lambda off: pl.BlockSpec((TQ, vw), lambda b, gi, kt: (b * nq + kt, off + gi))
    rowspec = pl.BlockSpec((nq, HEADS, TQ), lambda b, gi, kt: (b, 0, 0))
    return pl.pallas_call(
        body, name="attn_bwd", grid=(nb, ng, nq),
        in_specs=[pl.BlockSpec((TQ, hg * LANES), lambda b, gi, kt: (b * nq + kt, gi)), seqspec, kspec(ng), kspec(2 * ng), seqspec, rowspec, rowspec],
        out_specs=[seqspec, kspec(0), kspec(0), pl.BlockSpec((TQ, LANES), lambda b, gi, kt: (b * nq + kt, gi))],
        out_shape=[jax.ShapeDtypeStruct((t, D), bf16)] * 3 + [jax.ShapeDtypeStruct((t, ng * LANES), f32)],
        scratch_shapes=[pltpu.VMEM((nq, npair, LANES, TQ), f32), pltpu.VMEM((npair, TQ, LANES), f32),
                        pltpu.VMEM((npair, TQ, LANES), f32), pltpu.VMEM((hg, TQ, LANES), f32),
                        pltpu.VMEM((hg, LANES, TQ), bf16)],
        compiler_params=_params(("parallel", "parallel", "arbitrary"), VMEM_LIMIT),
    )(ka, qkv, qkv, qkv, doa, lse, delta)


def _forget_bwd(dc, f128, seq):
    t = f128.shape[0]
    nb = seq // LANES
    groups = dc.shape[1] // LANES

    def body(dc_ref, f_ref, df_ref, dbf_ref):
        @pl.when(pl.program_id(0) == 0)
        def _():
            dbf_ref[...] = jnp.zeros_like(dbf_ref)

        r = lax.broadcasted_iota(jnp.int32, (LANES, LANES), 0)
        cidx = lax.broadcasted_iota(jnp.int32, (LANES, LANES), 1)
        tri = (r <= cidx).astype(f32)
        carry = jnp.zeros((1, LANES), f32)
        total = jnp.zeros((1, LANES), f32)
        for blk in reversed(range(nb)):
            dcb = dc_ref[pl.ds(blk * LANES, LANES), pl.ds(0, LANES)]
            for gi in range(1, groups):
                dcb = dcb + pltpu.roll(dc_ref[pl.ds(blk * LANES, LANES), pl.ds(gi * LANES, LANES)], gi * ATT_GROUP, 1)
            dlf = jnp.dot(tri, dcb, preferred_element_type=f32, precision=lax.Precision.HIGHEST) + carry
            df = dlf * _sig(-f_ref[pl.ds(blk * LANES, LANES), :])
            df_ref[pl.ds(blk * LANES, LANES), :] = df.astype(bf16)
            total = total + jnp.sum(df, axis=0, keepdims=True)
            carry = carry + jnp.sum(dcb, axis=0, keepdims=True)
        dbf_ref[...] += total

    return pl.pallas_call(
        body, name="forget_bwd", grid=(t // seq,),
        in_specs=[pl.BlockSpec((seq, groups * LANES), lambda b: (b, 0)), pl.BlockSpec((seq, LANES), lambda b: (b, 0))],
        out_specs=[pl.BlockSpec((seq, LANES), lambda b: (b, 0)), _whole((1, LANES))],
        out_shape=[jax.ShapeDtypeStruct((t, LANES), bf16), jax.ShapeDtypeStruct((1, LANES), f32)],
        compiler_params=_params(("arbitrary",)),
    )(dc, f128)


def _in_bwd(dz, df, x, dy, w_all, w_pre):
    t = x.shape[0]
    n_dz = len(dz)

    def body(*refs):
        dz_refs = refs[:n_dz]
        df_ref, x_ref, dy_ref, w_ref, wp_ref, gx_ref, dwp_ref = refs[n_dz:]

        @pl.when(pl.program_id(0) == 0)
        def _():
            dwp_ref[...] = jnp.zeros_like(dwp_ref)

        dh = _dot(df_ref[...], w_ref[pl.ds(n_dz * D, LANES), :])
        for p in range(n_dz):
            dh = dh + _dot(dz_refs[p][...], w_ref[pl.ds(p * D, D), :])
        xv = x_ref[...]
        r1 = lax.rsqrt(jnp.mean(xv * xv, axis=-1, keepdims=True) + NORM_EPS)
        xh = xv * r1
        dwp_ref[...] += jnp.sum(dh * xh, axis=0, keepdims=True)
        dxh = dh * wp_ref[...]
        gx_ref[...] = dy_ref[...] + r1 * (dxh - xh * jnp.mean(dxh * xh, axis=-1, keepdims=True))

    once = lambda shape: pl.BlockSpec(shape, lambda i: (0, 0), pipeline_mode=pl.Buffered(1))
    return pl.pallas_call(
        body, name="in_bwd", grid=(t // TM,),
        in_specs=[_tile(TM, D)] * n_dz + [_tile(TM, LANES), _tile(TM, D), _tile(TM, D), once(w_all.shape),
                  _whole((1, D))],
        out_specs=[_tile(TM, D), _whole((1, D))],
        out_shape=[jax.ShapeDtypeStruct((t, D), f32), jax.ShapeDtypeStruct((1, D), f32)],
        compiler_params=_params(("arbitrary",), VMEM_LIMIT),
    )(*dz, df, x, dy, w_all, w_pre)


def _tn_mm(name, a, b, tn, out_dtype=f32, tk=2048):
    t, k = a.shape
    tk = min(tk, t)
    n = b.shape[1]
    nk = t // tk

    def body(a_ref, b_ref, o_ref, s_ref, acc_ref):
        j, kk = pl.program_id(0), pl.program_id(1)

        @pl.when(kk == 0)
        def _():
            acc_ref[...] = jnp.zeros_like(acc_ref)

        @pl.when((j == 0) & (kk == 0))
        def _():
            s_ref[...] = jnp.zeros_like(s_ref)

        av = a_ref[...]
        acc_ref[...] += _dot_tn(av, b_ref[...])

        @pl.when(j == 0)
        def _():
            s_ref[...] += jnp.sum(av.astype(f32), axis=0, keepdims=True)

        @pl.when(kk == nk - 1)
        def _():
            o_ref[...] = acc_ref[...].astype(out_dtype)

    return pl.pallas_call(
        body, name=name, grid=(n // tn, nk),
        in_specs=[pl.BlockSpec((tk, k), lambda j, kk: (kk, 0)), pl.BlockSpec((tk, tn), lambda j, kk: (kk, j))],
        out_specs=[pl.BlockSpec((k, tn), lambda j, kk: (0, j)), _whole((1, k))],
        out_shape=[jax.ShapeDtypeStruct((k, n), out_dtype), jax.ShapeDtypeStruct((1, k), f32)],
        scratch_shapes=[pltpu.VMEM((k, tn), f32)],
        compiler_params=_params(("arbitrary", "arbitrary"), VMEM_LIMIT),
    )(a, b)


def _tn_mm_stack(name, parts, b, out_dtype, tk=1024):
    t, k = parts[0].shape
    tk = min(tk, t)
    n = b.shape[1]
    nk, count = t // tk, len(parts)

    def body(*refs):
        a_refs, b_ref, o_ref, s_ref, acc_ref = refs[:count], refs[count], refs[count + 1], refs[count + 2], refs[-1]
        p, kk = pl.program_id(0), pl.program_id(1)

        @pl.when(kk == 0)
        def _():
            acc_ref[...] = jnp.zeros_like(acc_ref)
            s_ref[...] = jnp.zeros_like(s_ref)

        for i in range(count):
            @pl.when(p == i)
            def _(i=i):
                av = a_refs[i][...]
                acc_ref[...] += _dot_tn(av, b_ref[...])
                s_ref[0] += jnp.sum(av.astype(f32), axis=0, keepdims=True)

        @pl.when(kk == nk - 1)
        def _():
            o_ref[...] = acc_ref[...].astype(out_dtype)

    def part_spec(i):
        return pl.BlockSpec((tk, k), lambda p, kk: (jnp.where(p == i, kk, jnp.where(p < i, 0, nk - 1)), 0))

    return pl.pallas_call(
        body, name=name, grid=(count, nk),
        in_specs=[part_spec(i) for i in range(count)] + [pl.BlockSpec((tk, n), lambda p, kk: (kk, 0))],
        out_specs=[pl.BlockSpec((k, n), lambda p, kk: (p, 0)), pl.BlockSpec((1, 1, k), lambda p, kk: (p, 0, 0))],
        out_shape=[jax.ShapeDtypeStruct((count * k, n), out_dtype), jax.ShapeDtypeStruct((count, 1, k), f32)],
        scratch_shapes=[pltpu.VMEM((k, n), f32)],
        compiler_params=_params(("arbitrary", "arbitrary"), VMEM_LIMIT),
    )(*parts, b)


def _tn_mm_exchange(name, pairs, tk=1024):
    t, k = pairs[0][0].shape
    tk = min(tk, t)
    n = pairs[0][1].shape[1]
    nk, count, rows = t // tk, len(pairs), k // N_DEV

    def body(*refs):
        a_refs, b_refs, o_ref, acc_ref = refs[:count], refs[count:2 * count], refs[2 * count], refs[-1]
        j, kk = pl.program_id(0), pl.program_id(1)

        @pl.when(kk == 0)
        def _():
            acc_ref[...] = jnp.zeros_like(acc_ref)

        for i in range(count):
            @pl.when(j == i)
            def _(i=i):
                acc_ref[...] += _dot_tn(a_refs[i][...], b_refs[i][...])

        @pl.when(kk == nk - 1)
        def _():
            for dev in range(N_DEV):
                o_ref[dev] = acc_ref[pl.ds(dev * rows, rows), :].astype(bf16)

    def pair_spec(i, width):
        return pl.BlockSpec((tk, width), lambda j, kk: (jnp.where(j == i, kk, jnp.where(j < i, 0, nk - 1)), 0))

    return pl.pallas_call(
        body, name=name, grid=(count, nk),
        in_specs=[pair_spec(i, k) for i in range(count)] + [pair_spec(i, n) for i in range(count)],
        out_specs=pl.BlockSpec((N_DEV, rows, n), lambda j, kk: (0, j, 0)),
        out_shape=jax.ShapeDtypeStruct((N_DEV, count * rows, n), bf16),
        scratch_shapes=[pltpu.VMEM((k, n), f32)],
        compiler_params=_params(("arbitrary", "arbitrary"), VMEM_LIMIT),
    )(*[a for a, _ in pairs], *[b for _, b in pairs])


def _position():
    return lax.axis_index("x"), lax.axis_index("y"), lax.axis_index("c")


ROW_BLOCK = 128


def _pick_rows(layout, first, count):
    acc = jnp.zeros((ROW_BLOCK, D), f32)
    seg_start = 0
    for ref, ref_row, rows in layout:
        lo, hi = max(first, seg_start), min(first + count, seg_start + rows)
        if lo < hi and ref is not None:
            off, take, done = ref_row + lo - seg_start, hi - lo, lo - first
            start = off // 16 * 16
            win = -(-(off - start + take) // 16) * 16
            r = lax.broadcasted_iota(jnp.int32, (ROW_BLOCK, win), 0)
            col = lax.broadcasted_iota(jnp.int32, (ROW_BLOCK, win), 1)
            pick = ((col - r == off - start - done) & (r >= done) & (r < done + take)).astype(bf16)
            acc = acc + _dot(pick, ref[pl.ds(start, win), :])
        seg_start += rows
    return acc


def _assemble_rows(shards_ref, shard_rows, segments, out_ref):
    layout = [(shards_ref.at[j], 0, shard_rows) for j in range(shards_ref.shape[0])]
    for out0, log0, count in segments:
        for b0 in range(0, count, ROW_BLOCK):
            block = _pick_rows(layout, log0 + b0, min(ROW_BLOCK, count - b0))
            out_ref[pl.ds(out0 + b0, ROW_BLOCK), :] = block.astype(bf16)


def _pack_pieces(blocks, shard_rows, padded):
    arrays = []
    for a, _, _ in blocks:
        if a is not None and all(a is not seen for seen in arrays):
            arrays.append(a)
    piece_rows = padded // 2

    def body(*refs):
        out_ref = refs[-1]
        ref_of = lambda a: next(r for r, seen in zip(refs, arrays) if seen is a)
        layout = [(None if a is None else ref_of(a), row, rows) for a, row, rows in blocks]
        for k in range(N_DEV):
            chip, half = divmod(k, 2)
            for b0 in range(0, piece_rows, ROW_BLOCK):
                n = min(ROW_BLOCK, piece_rows - b0)
                in_shard = half * piece_rows + b0
                count = max(0, min(n, shard_rows - in_shard))
                block = _pick_rows(layout, chip * shard_rows + in_shard, count)
                out_ref[k, pl.ds(b0, n), :] = block[:n].astype(bf16)

    vm = pl.BlockSpec(memory_space=pltpu.VMEM)
    return pl.pallas_call(
        body, name="pack_pieces", in_specs=[vm] * len(arrays), out_specs=vm,
        out_shape=jax.ShapeDtypeStruct((N_DEV, piece_rows, D), bf16),
        compiler_params=pltpu.CompilerParams(vmem_limit_bytes=VMEM_LIMIT),
    )(*arrays)


def _gather_shards(parts, small, shard_rows, segments, out_rows):
    n = len(parts)
    halves = [p.shape[0] // 2 for p in parts]
    cuts = [-(-h // 32) * 16 for h in halves]
    n_direct, n_relay, n_sib = 4 * n, 2 * n, 6 * n

    def body(*refs):
        srcs, small_src = refs[:n], refs[n]
        dsts, small_dst, whole_ref = refs[n + 1:2 * n + 1], refs[2 * n + 1], refs[2 * n + 2]
        send, recv, local = refs[2 * n + 3:]
        x, y, c = _position()
        me = 2 * x + y
        chips = [(1 - x, y), (x, 1 - y), (1 - x, 1 - y)]
        ids = [2 * px + py for px, py in chips]

        def rows(a, half, quarter):
            start = half * halves[a] + (cuts[a] if quarter else 0)
            return pl.ds(start, halves[a] - cuts[a] if quarter else cuts[a])

        def landing(a, shard, half, quarter):
            return dsts[a].at[shard, rows(a, half, quarter), :]

        def direct(a, nb, quarter, shard):
            k = (a * 2 + nb) * 2 + quarter
            px, py = chips[nb]
            return pltpu.make_async_remote_copy(
                src_ref=srcs[a].at[rows(a, c, quarter), :], dst_ref=landing(a, shard, c, quarter),
                send_sem=send.at[k], recv_sem=recv.at[k], device_id=(px, py, c), device_id_type=MESH)

        def relay(a, quarter, shard):
            k = n_direct + a * 2 + quarter
            px, py = chips[1 - quarter]
            return pltpu.make_async_remote_copy(
                src_ref=landing(a, shard, c, quarter), dst_ref=landing(a, shard, c, quarter),
                send_sem=send.at[k], recv_sem=recv.at[k], device_id=(px, py, c), device_id_type=MESH)

        def to_sibling(a, origin, quarter, half):
            k = n_direct + n_relay + (a * 3 + origin) * 2 + quarter
            return pltpu.make_async_remote_copy(
                src_ref=landing(a, ids[origin], half, quarter), dst_ref=landing(a, ids[origin], half, quarter),
                send_sem=send.at[k], recv_sem=recv.at[k], device_id=(x, y, 1 - c), device_id_type=MESH)

        def small_copy(j, shard):
            k = n_direct + n_relay + n_sib + j
            px, py = chips[j]
            return pltpu.make_async_remote_copy(
                src_ref=small_src, dst_ref=small_dst.at[shard], send_sem=send.at[k], recv_sem=recv.at[k],
                device_id=(px, py, c), device_id_type=MESH)

        own = [pltpu.make_async_copy(srcs[a], dsts[a].at[me], local.at[a]) for a in range(n)]
        own.append(pltpu.make_async_copy(small_src, small_dst.at[me], local.at[n]))
        for cp in own:
            cp.start()
        sent = [direct(a, nb, q, me) for q in range(2) for a in range(n) for nb in range(2)]
        sent += [small_copy(j, me) for j in range(3)]
        for cp in sent:
            cp.start()

        def passed_on(cp):
            cp.start()
            sent.append(cp)

        for q in range(2):
            for a in range(n):
                for nb in range(2):
                    direct(a, nb, q, ids[nb]).wait_recv()
                    passed_on(to_sibling(a, nb, q, c))
                    if nb == q:
                        passed_on(relay(a, q, ids[nb]))
        for a in range(n):
            for q in range(2):
                relay(a, q, ids[2]).wait_recv()
                passed_on(to_sibling(a, 2, q, c))
        for j in range(3):
            small_copy(j, ids[j]).wait_recv()
            for a in range(n):
                for q in range(2):
                    to_sibling(a, j, q, 1 - c).wait_recv()
        for cp in sent:
            cp.wait_send()
        for cp in own:
            cp.wait()
        _assemble_rows(dsts[0], shard_rows, segments, whole_ref)

    vm = pl.BlockSpec(memory_space=pltpu.VMEM)
    n_sems = n_direct + n_relay + n_sib + 3
    out = pl.pallas_call(
        body, name="gather_shards",
        in_specs=[vm] * (n + 1), out_specs=[vm] * (n + 2),
        out_shape=[jax.ShapeDtypeStruct((N_CHIPS,) + p.shape, p.dtype) for p in parts + [small]]
        + [jax.ShapeDtypeStruct((out_rows, parts[0].shape[1]), parts[0].dtype)],
        scratch_shapes=[pltpu.SemaphoreType.DMA((n_sems,)), pltpu.SemaphoreType.DMA((n_sems,)),
                        pltpu.SemaphoreType.DMA((n + 1,))],
        compiler_params=pltpu.CompilerParams(vmem_limit_bytes=VMEM_LIMIT),
    )(*parts, small)
    return out[1:]


def _allsum_rows(part):
    rows_n = part.shape[0]

    def body(x_ref, gath_ref, sum_ref, send_sems, recv_sems, local_sem):
        x, y, c = _position()
        me, sibling = (x, y, c), (x, y, 1 - c)
        chips = [(1 - x, y), (x, 1 - y), (1 - x, 1 - y)]

        def rows(px, py, pc):
            return gath_ref.at[pl.ds((4 * px + 2 * py + pc) * rows_n, rows_n), :]

        def copy(k, block, to, src=None):
            return pltpu.make_async_remote_copy(
                src_ref=rows(*block) if src is None else src, dst_ref=rows(*block),
                send_sem=send_sems.at[k], recv_sem=recv_sems.at[k], device_id=to, device_id_type=MESH)

        mine = pltpu.make_async_copy(x_ref, rows(*me), local_sem)
        mine.start()
        first = [copy(0, me, sibling, src=x_ref)]
        first += [copy(1 + j, me, (*chip, c), src=x_ref) for j, chip in enumerate(chips)]
        for cp in first:
            cp.start()
        passed = [copy(4 + j, (*chip, c), sibling) for j, chip in enumerate(chips)]
        for j, chip in enumerate(chips):
            copy(1 + j, (*chip, c), me).wait_recv()
            passed[j].start()
        copy(0, sibling, me).wait_recv()
        for j, chip in enumerate(chips):
            copy(4 + j, (*chip, 1 - c), me).wait_recv()
        for cp in first + passed:
            cp.wait_send()
        mine.wait()
        total = gath_ref[pl.ds(0, rows_n), :]
        for d in range(1, N_DEV):
            total = total + gath_ref[pl.ds(d * rows_n, rows_n), :]
        sum_ref[...] = total

    vm = pl.BlockSpec(memory_space=pltpu.VMEM)
    return pl.pallas_call(
        body, name="allsum_rows", in_specs=[vm], out_specs=[vm, vm],
        out_shape=[jax.ShapeDtypeStruct((N_DEV * rows_n, D), f32), jax.ShapeDtypeStruct((rows_n, D), f32)],
        scratch_shapes=[pltpu.SemaphoreType.DMA((7,)), pltpu.SemaphoreType.DMA((7,)), pltpu.SemaphoreType.DMA],
    )(part)[1]


PAIR_ROWS = 16


def _pair_reduce(name, pieces):
    _, r, n = pieces.shape

    def body(p_ref, o_ref, land, send, recv):
        x, y, c = _position()

        def remote(j, half):
            return pltpu.make_async_remote_copy(
                src_ref=p_ref.at[2 * j + half], dst_ref=land.at[j], send_sem=send.at[j], recv_sem=recv.at[j],
                device_id=(x, y, 1 - c), device_id_type=MESH)

        sends = [remote(j, 1 - c) for j in range(N_CHIPS)]
        for cp in sends:
            cp.start()
        for j in range(N_CHIPS):
            remote(j, c).wait_recv()

            def add_rows(i, carry, j=j):
                rows = pl.ds(pl.multiple_of(i * PAIR_ROWS, PAIR_ROWS), PAIR_ROWS)
                o_ref[j, rows, :] = (p_ref[2 * j + c, rows, :].astype(f32) + land[j, rows, :].astype(f32)).astype(bf16)
                return carry

            lax.fori_loop(0, r // PAIR_ROWS, add_rows, 0)
        for cp in sends:
            cp.wait_send()

    vm = pl.BlockSpec(memory_space=pltpu.VMEM)
    return pl.pallas_call(
        body, name=name, in_specs=[vm], out_specs=vm,
        out_shape=jax.ShapeDtypeStruct((N_CHIPS, r, n), bf16),
        scratch_shapes=[pltpu.VMEM((N_CHIPS, r, n), bf16), pltpu.SemaphoreType.DMA((N_CHIPS,)),
                        pltpu.SemaphoreType.DMA((N_CHIPS,))],
        compiler_params=pltpu.CompilerParams(vmem_limit_bytes=VMEM_LIMIT),
    )(pieces)


def _chip_exchange(arrs):
    n = len(arrs)
    heights = [a.shape[1] for a in arrs]
    cuts = [-(-r // 32) * 16 for r in heights]

    def body(*refs):
        srcs, dsts, relays = refs[:n], refs[n:2 * n], refs[2 * n:3 * n]
        send, recv, local = refs[3 * n:]
        x, y, c = _position()
        me = 2 * x + y
        chips = [(1 - x, y), (x, 1 - y), (1 - x, 1 - y)]
        ids = [2 * px + py for px, py in chips]

        def rows(a, quarter):
            return pl.ds(cuts[a], heights[a] - cuts[a]) if quarter else pl.ds(0, cuts[a])

        def held(a, quarter):
            size = heights[a] - cuts[a] if quarter else cuts[a]
            return relays[a].at[quarter, pl.ds(0, size), :]

        def direct(a, nb, piece, landing):
            px, py = chips[nb]
            return pltpu.make_async_remote_copy(
                src_ref=srcs[a].at[piece], dst_ref=dsts[a].at[landing], send_sem=send.at[a * 2 + nb],
                recv_sem=recv.at[a * 2 + nb], device_id=(px, py, c), device_id_type=MESH)

        def first_hop(a, quarter):
            k = 2 * n + a * 2 + quarter
            px, py = chips[quarter]
            return pltpu.make_async_remote_copy(
                src_ref=srcs[a].at[ids[2], rows(a, quarter), :], dst_ref=held(a, quarter), send_sem=send.at[k],
                recv_sem=recv.at[k], device_id=(px, py, c), device_id_type=MESH)

        def second_hop(a, quarter, origin):
            k = 4 * n + a * 2 + quarter
            px, py = chips[1 - quarter]
            return pltpu.make_async_remote_copy(
                src_ref=held(a, quarter), dst_ref=dsts[a].at[origin, rows(a, quarter), :], send_sem=send.at[k],
                recv_sem=recv.at[k], device_id=(px, py, c), device_id_type=MESH)

        own = [pltpu.make_async_copy(srcs[a].at[me], dsts[a].at[me], local.at[a]) for a in range(n)]
        sent = [first_hop(a, q) for a in range(n) for q in range(2)]
        sent += [direct(a, nb, ids[nb], me) for a in range(n) for nb in range(2)]
        for cp in sent + own:
            cp.start()
        for a in range(n):
            for q in range(2):
                first_hop(a, q).wait_recv()
                sent.append(second_hop(a, q, ids[q]))
                sent[-1].start()
        for a in range(n):
            for nb in range(2):
                direct(a, nb, me, ids[nb]).wait_recv()
            for q in range(2):
                second_hop(a, q, ids[2]).wait_recv()
        for cp in sent:
            cp.wait_send()
        for cp in own:
            cp.wait()

    anyspec = pl.BlockSpec(memory_space=pl.ANY)
    out = pl.pallas_call(
        body, name="chip_exchange", in_specs=[anyspec] * n, out_specs=[anyspec] * (2 * n),
        out_shape=[jax.ShapeDtypeStruct(a.shape, a.dtype) for a in arrs]
        + [jax.ShapeDtypeStruct((2, cut, a.shape[2]), a.dtype) for a, cut in zip(arrs, cuts)],
        scratch_shapes=[pltpu.SemaphoreType.DMA((6 * n,)), pltpu.SemaphoreType.DMA((6 * n,)),
                        pltpu.SemaphoreType.DMA((n,))],
    )(*arrs)
    return out[:n]


def _sum_swap_halves(slots):
    n = len(slots)

    def body(*refs):
        srcs, dsts, halves = refs[:n], refs[n:2 * n], refs[2 * n:3 * n]
        send, recv, local = refs[3 * n:]
        x, y, c = _position()

        def remote(a, landing):
            return pltpu.make_async_remote_copy(
                src_ref=halves[a], dst_ref=dsts[a].at[landing], send_sem=send.at[a], recv_sem=recv.at[a],
                device_id=(x, y, 1 - c), device_id_type=MESH)

        for a in range(n):
            def add_rows(i, carry, a=a):
                rows = pl.ds(pl.multiple_of(i * PAIR_ROWS, PAIR_ROWS), PAIR_ROWS)
                total = srcs[a][0, rows, :].astype(f32)
                for s in range(1, N_CHIPS):
                    total = total + srcs[a][s, rows, :].astype(f32)
                halves[a][rows, :] = total
                return carry

            lax.fori_loop(0, srcs[a].shape[1] // PAIR_ROWS, add_rows, 0)
        own = [pltpu.make_async_copy(halves[a], dsts[a].at[c], local.at[a]) for a in range(n)]
        sends = [remote(a, c) for a in range(n)]
        for cp in sends + own:
            cp.start()
        for a in range(n):
            remote(a, 1 - c).wait_recv()
        for cp in sends:
            cp.wait_send()
        for cp in own:
            cp.wait()

    vm = pl.BlockSpec(memory_space=pltpu.VMEM)
    return pl.pallas_call(
        body, name="sum_swap_halves", in_specs=[vm] * n, out_specs=[vm] * n,
        out_shape=[jax.ShapeDtypeStruct((2,) + a.shape[1:], f32) for a in slots],
        scratch_shapes=[pltpu.VMEM(a.shape[1:], f32) for a in slots]
        + [pltpu.SemaphoreType.DMA((n,)), pltpu.SemaphoreType.DMA((n,)), pltpu.SemaphoreType.DMA((n,))],
        compiler_params=pltpu.CompilerParams(vmem_limit_bytes=VMEM_LIMIT),
    )(*slots)


def _row_block(r):
    return 128 if r % 128 == 0 else r


def _adamw(name, w, g, m, v):
    r, n = w.shape
    if r % 128 == 0 or r * n <= 128 * 1024:
        rb, nb = _row_block(r), n
    else:
        rb, nb = r, LANES

    def body(w_ref, g_ref, m_ref, v_ref, d_ref, nm_ref, nv_ref):
        gv = g_ref[...]
        m2 = ADAM_B1 * m_ref[...] + (1.0 - ADAM_B1) * gv
        v2 = ADAM_B2 * v_ref[...] + (1.0 - ADAM_B2) * (gv * gv)
        m_hat = m2 / (1.0 - ADAM_B1 ** ADAM_STEP)
        v_hat = v2 / (1.0 - ADAM_B2 ** ADAM_STEP)
        d_ref[...] = (-ADAM_LR) * (m_hat / (jnp.sqrt(v_hat) + ADAM_EPS) + ADAM_WD * w_ref[...])
        nm_ref[...] = m2
        nv_ref[...] = v2

    spec = pl.BlockSpec((rb, nb), lambda i, j: (i, j))
    return pl.pallas_call(
        body, name=name, grid=(r // rb, n // nb), in_specs=[spec] * 4, out_specs=[spec] * 3,
        out_shape=[jax.ShapeDtypeStruct((r, n), f32)] * 3,
        compiler_params=_params(("parallel", "parallel"), VMEM_LIMIT),
    )(w, g, m, v)


def _local_step(x2, tgt2, seq, wt):
    nb = x2.shape[0] // seq
    h, qkv, f128 = _norm_qkv(x2, wt["pre_w"], wt["w_all"], wt["b_qkv"], 8 * D, wt["b_f"])
    rest = _mm("in_rest", h, wt["w_all"], (3 * D, 5 * D), wt["b_rest"], bf16, 1024, 1024)
    c = _forget_prep(f128, seq)
    ka = _attn_prep(qkv, c)
    o_att, pa, lse = _attn_fwd(ka, qkv, rest, seq)
    rnn_w = (wt["conv_w"], wt["conv_b"], wt["wa_d"], wt["wx_d"], wt["ba"], wt["bx"], wt["lam"])
    xc, a, hrec, pr = _rnn_fwd(rest, *rnn_w, seq)
    (do, dya, dyr, dmga, dmgr, doa, dga, dhrec, dgr, mrg, dy, delta, loss8, d_post) = _merge_loss(
        rest, pa, pr, o_att, hrec, wt["w_a"], wt["w_r"], wt["w_o"], x2, tgt2, wt["post_w"])
    d_aro = _tn_mm_exchange("dw_branch_out", [(pa, dya), (pr, dyr), (mrg, do)])
    dxr, d_wad, d_wxd, vec = _rnn_bwd(dhrec, a, hrec, xc, rest, *rnn_w, seq)
    dq, dk, dv, dc = _attn_bwd(ka, qkv, doa, lse, delta, seq)
    df, db_f = _forget_bwd(dc, f128, seq)
    pieces = [dq, dk, dv, dga, dxr, dgr, dmga, dmgr]
    gx, d_pre = _in_bwd(pieces, df, x2, dy, wt["w_all"], wt["pre_w"])
    dw_stack, db_stack = _tn_mm_stack("dw_in", pieces, h, bf16)
    dws = [(dw_stack, p * D, D) for p in range(len(pieces))]
    dbs = [db_stack[p] for p in range(len(pieces))]
    dw_f, _ = _tn_mm("dw_in_f", df, h, D, bf16)
    shard_rows = IN_TOTAL // N_CHIPS
    w_in_pieces = _pack_pieces(dws[:3] + [(dw_f, 0, HEADS)] + dws[3:] + [(None, 0, IN_TOTAL - IN_USED)], shard_rows,
                               _padded_rows(shard_rows))
    d_b_in = jnp.concatenate(dbs[:3] + [db_f[:, :HEADS]] + dbs[3:] + [jnp.zeros((1, IN_TOTAL - IN_USED), f32)], axis=1)
    return dict(loss=loss8[0, 0], grad_x=gx, pre_w=d_pre, w_in_pieces=w_in_pieces, b_in=d_b_in, conv_w=vec[4:8],
                conv_b=vec[3:4],
                wa_d=d_wad, ba=vec[0:1], wx_d=d_wxd, bx=vec[1:2], lam=vec[2:3], w_aro_pieces=d_aro,
                post_w=d_post)


def _block_diag(w):
    g, bw, _ = w.shape
    eye = jnp.eye(g, dtype=w.dtype)
    return (w[:, :, None, :] * eye[:, None, :, None]).reshape(g * bw, g * bw)


def _gate_blocks(diag):
    half = diag.shape[1] // 2
    return jnp.stack([diag[:, :half, :half], diag[:, half:, half:]], axis=1).reshape(-1, half, half)


def _padded_rows(rows):
    return -(-rows // 32) * 32


def _pad_cols(a, n):
    return jnp.pad(a, ((0, 0), (0, n - a.shape[1])))


def _pad_rows(a, n):
    return jnp.pad(a, ((0, n - a.shape[0]), (0, 0)))


def kernel(x, pre_norm_w, w_in, b_in, conv_w, conv_b, rg_wa, rg_ba, rg_wx, rg_bx, rg_lambda, w_branch_a, w_branch_r, w_out, post_norm_w, loss_target, m_pre_norm_w, m_w_in, m_b_in, m_conv_w, m_conv_b, m_rg_wa, m_rg_ba, m_rg_wx, m_rg_bx, m_rg_lambda, m_w_branch_a, m_w_branch_r, m_w_out, m_post_norm_w, v_pre_norm_w, v_w_in, v_b_in, v_conv_w, v_conv_b, v_rg_wa, v_rg_ba, v_rg_wx, v_rg_bx, v_rg_lambda, v_w_branch_a, v_w_branch_r, v_w_out, v_post_norm_w):
    nb, seq, _ = x.shape
    chip = 2 * lax.axis_index("x") + lax.axis_index("y")
    n_groups = rg_wa.shape[1]

    w_in_t = jnp.transpose(w_in[0])
    shard_cols = w_in_t.shape[0]
    padded = _padded_rows(shard_cols)
    q_end, f_end = 3 * D, 3 * D + HEADS
    segments = [(0, 0, q_end), (q_end, f_end, IN_USED - f_end), (IN_USED - HEADS, q_end, HEADS)]
    g_a, g_r, g_o, g_cw, w_all = _gather_shards(
        [_pad_rows(w_in_t.astype(bf16), padded), w_branch_a[0].astype(bf16), w_branch_r[0].astype(bf16),
         w_out[0].astype(bf16)], conv_w[0], shard_cols, segments, IN_USED - HEADS + LANES)
    wt = dict(
        pre_w=pre_norm_w, post_w=post_norm_w,
        w_all=w_all, b_qkv=b_in[:, :q_end], b_f=_pad_cols(b_in[:, q_end:f_end], LANES), b_rest=b_in[:, f_end:IN_USED],
        w_a=g_a.reshape(D, D), w_r=g_r.reshape(D, D), w_o=g_o.reshape(D, D),
        conv_w=jnp.transpose(g_cw, (1, 0, 2)).reshape(4, D), conv_b=conv_b,
        wa_d=_block_diag(rg_wa[0]).astype(bf16), wx_d=_block_diag(rg_wx[0]).astype(bf16),
        ba=rg_ba, bx=rg_bx, lam=rg_lambda)

    part = _local_step(x.reshape(nb * seq, D), loss_target.reshape(nb * seq, D), seq, wt)
    loss = lax.psum(part["loss"], ("x", "y", "c"))
    grad_x = part["grad_x"].reshape(nb, seq, D)

    small = jnp.concatenate([
        part["pre_w"], _pad_cols(part["b_in"], 10 * D).reshape(10, D), part["conv_b"],
        _gate_blocks(part["wa_d"]).reshape(-1, D), part["ba"],
        _gate_blocks(part["wx_d"]).reshape(-1, D), part["bx"], part["lam"], part["post_w"],
        part["conv_w"]], axis=0)
    n_small = small.shape[0]
    n_rep = n_small - 4
    tot = _allsum_rows(_pad_rows(small, -(-n_small // 8) * 8))
    g_rep = tot[:n_rep]
    g_conv_w = lax.dynamic_slice_in_dim(tot[n_rep:n_small], chip * (D // N_CHIPS), D // N_CHIPS, axis=1)

    def unpack(p):
        o = [0]

        def take(k):
            o[0] += k
            return p[o[0] - k:o[0]]

        pre = take(1)
        b = take(10).reshape(1, 10 * D)[:, :IN_TOTAL]
        cb = take(1)
        wa = take(64).reshape(rg_wa.shape)
        ba = take(1)
        wx = take(64).reshape(rg_wx.shape)
        bx = take(1)
        lam = take(1)
        post = take(1)
        return dict(pre_norm_w=pre, b_in=b, conv_b=cb, rg_wa=wa, rg_ba=ba, rg_wx=wx, rg_bx=bx, rg_lambda=lam,
                    post_norm_w=post)

    grads = unpack(g_rep)
    replicated = dict(
        pre_norm_w=(pre_norm_w, m_pre_norm_w, v_pre_norm_w), b_in=(b_in, m_b_in, v_b_in),
        conv_b=(conv_b, m_conv_b, v_conv_b), rg_wa=(rg_wa, m_rg_wa, v_rg_wa), rg_ba=(rg_ba, m_rg_ba, v_rg_ba),
        rg_wx=(rg_wx, m_rg_wx, v_rg_wx), rg_bx=(rg_bx, m_rg_bx, v_rg_bx),
        rg_lambda=(rg_lambda, m_rg_lambda, v_rg_lambda), post_norm_w=(post_norm_w, m_post_norm_w, v_post_norm_w))
    deltas, new_m, new_v = {}, {}, {}
    for name, (w, m, v) in replicated.items():
        as2d = lambda a: a.reshape(-1, D) if a.ndim > 2 else a
        upd = _adamw("adamw_" + name, as2d(w), as2d(grads[name]), as2d(m), as2d(v))
        deltas[name], new_m[name], new_v[name] = [a.reshape(w.shape) for a in upd]

    s_in, s_aro = _chip_exchange([_pair_reduce("pair_w_in", part["w_in_pieces"]),
                                  _pair_reduce("pair_w_aro", part["w_aro_pieces"])])
    f_in, f_aro = _sum_swap_halves([s_in, s_aro])
    g_w_in_t = f_in.reshape(padded, D)[:shard_cols]
    rows = D // N_DEV
    g_aro = [f_aro[:, i * rows:(i + 1) * rows, :].reshape(2 * rows, D) for i in range(3)]

    w_in_upd = _adamw("adamw_w_in", w_in_t, g_w_in_t, jnp.transpose(m_w_in[0]), jnp.transpose(v_w_in[0]))
    g_w_in, d_w_in, nm_w_in, nv_w_in = [jnp.transpose(a) for a in (g_w_in_t, *w_in_upd)]
    upd_a = _adamw("adamw_w_branch_a", w_branch_a[0], g_aro[0], m_w_branch_a[0], v_w_branch_a[0])
    upd_r = _adamw("adamw_w_branch_r", w_branch_r[0], g_aro[1], m_w_branch_r[0], v_w_branch_r[0])
    upd_o = _adamw("adamw_w_out", w_out[0], g_aro[2], m_w_out[0], v_w_out[0])
    d_aro, nm_aro, nv_aro = zip(upd_a, upd_r, upd_o)
    d_cw, nm_cw, nv_cw = _adamw("adamw_conv_w", conv_w[0], g_conv_w, m_conv_w[0], v_conv_w[0])

    def sharded(t_in, t_aro, t_cw):
        return dict(w_in=t_in[None], conv_w=t_cw[None], w_branch_a=t_aro[0][None], w_branch_r=t_aro[1][None],
                    w_out=t_aro[2][None])

    order = ["pre_norm_w", "w_in", "b_in", "conv_w", "conv_b", "rg_wa", "rg_ba", "rg_wx", "rg_bx", "rg_lambda",
             "w_branch_a", "w_branch_r", "w_out", "post_norm_w"]
    outs = [loss, grad_x]
    for rep, shd in ((grads, sharded(g_w_in, g_aro, g_conv_w)), (deltas, sharded(d_w_in, d_aro, d_cw)),
                     (new_m, sharded(nm_w_in, nm_aro, nm_cw)), (new_v, sharded(nv_w_in, nv_aro, nv_cw))):
        both = {**rep, **shd}
        outs.extend(both[k] for k in order)
    return tuple(outs)
```
